```python
import jax, jax.numpy as jnp
from jax import lax
import numpy as np

D_MODEL = 1024
BATCH = 16
SEQ = 2048
DEPTH = 1

D_FF = 2816
W_A = D_MODEL // 2
H_A = 8
HD_A = W_A // H_A
W_B = D_MODEL - W_A
G_B = 8
CHUNK = 128
CONV_K = 31
N_MOD = 9
EPS = 1e-6
HALF = 0.5

kernel_name = "hybrid_gmlp_conformer_macaron_adaln"


def _rms_norm(x, g):
    xf = x.astype(jnp.float32)
    xf = xf * lax.rsqrt(jnp.mean(xf * xf, axis=-1, keepdims=True) + EPS)
    return (xf * g.astype(jnp.float32)).astype(x.dtype)


def _layer_norm(x, g, b):
    xf = x.astype(jnp.float32)
    mu = jnp.mean(xf, axis=-1, keepdims=True)
    xc = xf - mu
    var = jnp.mean(xc * xc, axis=-1, keepdims=True)
    y = xc * lax.rsqrt(var + EPS) * g.astype(jnp.float32) + b.astype(jnp.float32)
    return y.astype(x.dtype)


def _modulate(h, shift, scale):
    return h * (1 + scale[:, None, :]) + shift[:, None, :]


def _swiglu(h, w_in, w_out):
    gate, up = jnp.split(h @ w_in, 2, axis=-1)
    return (jax.nn.silu(gate) * up) @ w_out


def _hybrid_mixer(h, w_mix_in, gmlp_norm_g, gmlp_norm_b, w_spatial, b_spatial,
                  conv_w, conv_b, conv_norm_g, conv_norm_b, g_out_a, g_out_b, w_mix_out):
    bsz, seq, _ = h.shape
    proj = h @ w_mix_in
    u, v, a, g = jnp.split(proj, [W_A, 2 * W_A, 2 * W_A + W_B], axis=-1)

    v = _layer_norm(v, gmlp_norm_g, gmlp_norm_b)
    n_chunks = seq // CHUNK
    v = v.reshape(bsz, n_chunks, CHUNK, H_A, HD_A)
    causal = jnp.tril(jnp.ones((CHUNK, CHUNK), dtype=bool))
    w_s = jnp.where(causal[None], w_spatial, jnp.zeros_like(w_spatial))
    z = jnp.einsum('hts,bnshd->bnthd', w_s, v) + b_spatial.T[None, None, :, :, None]
    y_a = u * z.reshape(bsz, seq, W_A)

    glu = a * jax.nn.sigmoid(g)
    conv = lax.conv_general_dilated(
        glu, conv_w[:, None, :], window_strides=(1,), padding=[(CONV_K - 1, 0)],
        dimension_numbers=('NWC', 'WIO', 'NWC'), feature_group_count=W_B) + conv_b
    y_b = jax.nn.silu(_layer_norm(conv, conv_norm_g, conv_norm_b))

    y = jnp.concatenate([_rms_norm(y_a, g_out_a), _rms_norm(y_b, g_out_b)], axis=-1)
    return y @ w_mix_out


def _fwd_setup_inputs(seed: int = 0) -> dict:
    key = jax.random.key(seed)
    ks = jax.random.split(key, 32)
    L, D = DEPTH, D_MODEL

    def nrm(k, shape, std):
        return std * jax.random.normal(k, shape, jnp.float32)

    def gain(k, shape):
        return 1.0 + 0.05 * jax.random.normal(k, shape, jnp.float32)

    return {
        "x": nrm(ks[0], (BATCH, SEQ, D), 1.0),
        "c": nrm(ks[1], (BATCH, D), 1.0),
        "w_ada": nrm(ks[2], (L, D, N_MOD * D), 0.5 * D ** -0.5),
        "b_ada": nrm(ks[3], (L, N_MOD * D), 0.02),
        "g_pre_f1": gain(ks[4], (L, D)),
        "g_post_f1": gain(ks[5], (L, D)),
        "w_f1_in": nrm(ks[6], (L, D, 2 * D_FF), D ** -0.5),
        "w_f1_out": nrm(ks[7], (L, D_FF, D), D_FF ** -0.5),
        "g_pre_m": gain(ks[8], (L, D)),
        "g_post_m": gain(ks[9], (L, D)),
        "w_mix_in": nrm(ks[10], (L, D, 2 * W_A + 2 * W_B), D ** -0.5),
        "gmlp_norm_g": gain(ks[11], (L, W_A)),
        "gmlp_norm_b": nrm(ks[12], (L, W_A), 0.02),
        "w_spatial": nrm(ks[13], (L, H_A, CHUNK, CHUNK), CHUNK ** -0.5),
        "b_spatial": gain(ks[14], (L, H_A, CHUNK)),
        "conv_w": nrm(ks[15], (L, CONV_K, W_B), CONV_K ** -0.5),
        "conv_b": nrm(ks[16], (L, W_B), 0.02),
        "conv_norm_g": gain(ks[17], (L, W_B)),
        "conv_norm_b": nrm(ks[18], (L, W_B), 0.02),
        "g_out_a": gain(ks[19], (L, W_A)),
        "g_out_b": gain(ks[20], (L, W_B)),
        "w_mix_out": nrm(ks[21], (L, W_A + W_B, D), (W_A + W_B) ** -0.5),
        "g_pre_f2": gain(ks[22], (L, D)),
        "g_post_f2": gain(ks[23], (L, D)),
        "w_f2_in": nrm(ks[24], (L, D, 2 * D_FF), D ** -0.5),
        "w_f2_out": nrm(ks[25], (L, D_FF, D), D_FF ** -0.5),
    }


def _fwd_reference(x, c, w_ada, b_ada, g_pre_f1, g_post_f1, w_f1_in, w_f1_out,
              g_pre_m, g_post_m, w_mix_in, gmlp_norm_g, gmlp_norm_b, w_spatial, b_spatial,
              conv_w, conv_b, conv_norm_g, conv_norm_b, g_out_a, g_out_b, w_mix_out,
              g_pre_f2, g_post_f2, w_f2_in, w_f2_out):
    for l in range(DEPTH):
        ada = jax.nn.silu(c) @ w_ada[l] + b_ada[l]
        sh1, sc1, gt1, sh2, sc2, gt2, sh3, sc3, gt3 = jnp.split(ada, N_MOD, axis=-1)

        h = _modulate(_rms_norm(x, g_pre_f1[l]), sh1, sc1)
        x = x + HALF * gt1[:, None, :] * _rms_norm(_swiglu(h, w_f1_in[l], w_f1_out[l]), g_post_f1[l])

        h = _modulate(_rms_norm(x, g_pre_m[l]), sh2, sc2)
        y = _hybrid_mixer(h, w_mix_in[l], gmlp_norm_g[l], gmlp_norm_b[l], w_spatial[l], b_spatial[l],
                          conv_w[l], conv_b[l], conv_norm_g[l], conv_norm_b[l],
                          g_out_a[l], g_out_b[l], w_mix_out[l])
        x = x + gt2[:, None, :] * _rms_norm(y, g_post_m[l])

        h = _modulate(_rms_norm(x, g_pre_f2[l]), sh3, sc3)
        x = x + HALF * gt3[:, None, :] * _rms_norm(_swiglu(h, w_f2_in[l], w_f2_out[l]), g_post_f2[l])
    return x


import jax as _jax
import jax.numpy as _jnp

TWIN_FORMAT = 'train_step'
FWD_PARAMS = ['x', 'c', 'w_ada', 'b_ada', 'g_pre_f1', 'g_post_f1', 'w_f1_in', 'w_f1_out', 'g_pre_m', 'g_post_m', 'w_mix_in', 'gmlp_norm_g', 'gmlp_norm_b', 'w_spatial', 'b_spatial', 'conv_w', 'conv_b', 'conv_norm_g', 'conv_norm_b', 'g_out_a', 'g_out_b', 'w_mix_out', 'g_pre_f2', 'g_post_f2', 'w_f2_in', 'w_f2_out']
TWIN_WEIGHTS = ['w_ada', 'b_ada', 'g_pre_f1', 'g_post_f1', 'w_f1_in', 'w_f1_out', 'g_pre_m', 'g_post_m', 'w_mix_in', 'gmlp_norm_g', 'gmlp_norm_b', 'w_spatial', 'b_spatial', 'conv_w', 'conv_b', 'conv_norm_g', 'conv_norm_b', 'g_out_a', 'g_out_b', 'w_mix_out', 'g_pre_f2', 'g_post_f2', 'w_f2_in', 'w_f2_out']
TWIN_DIFF_INPUT = 'x'
TWIN_INPUTS = ['x', 'c', 'w_ada', 'b_ada', 'g_pre_f1', 'g_post_f1', 'w_f1_in', 'w_f1_out', 'g_pre_m', 'g_post_m', 'w_mix_in', 'gmlp_norm_g', 'gmlp_norm_b', 'w_spatial', 'b_spatial', 'conv_w', 'conv_b', 'conv_norm_g', 'conv_norm_b', 'g_out_a', 'g_out_b', 'w_mix_out', 'g_pre_f2', 'g_post_f2', 'w_f2_in', 'w_f2_out', 'loss_target', 'm_w_ada', 'm_b_ada', 'm_g_pre_f1', 'm_g_post_f1', 'm_w_f1_in', 'm_w_f1_out', 'm_g_pre_m', 'm_g_post_m', 'm_w_mix_in', 'm_gmlp_norm_g', 'm_gmlp_norm_b', 'm_w_spatial', 'm_b_spatial', 'm_conv_w', 'm_conv_b', 'm_conv_norm_g', 'm_conv_norm_b', 'm_g_out_a', 'm_g_out_b', 'm_w_mix_out', 'm_g_pre_f2', 'm_g_post_f2', 'm_w_f2_in', 'm_w_f2_out', 'v_w_ada', 'v_b_ada', 'v_g_pre_f1', 'v_g_post_f1', 'v_w_f1_in', 'v_w_f1_out', 'v_g_pre_m', 'v_g_post_m', 'v_w_mix_in', 'v_gmlp_norm_g', 'v_gmlp_norm_b', 'v_w_spatial', 'v_b_spatial', 'v_conv_w', 'v_conv_b', 'v_conv_norm_g', 'v_conv_norm_b', 'v_g_out_a', 'v_g_out_b', 'v_w_mix_out', 'v_g_pre_f2', 'v_g_post_f2', 'v_w_f2_in', 'v_w_f2_out']
TWIN_OUTPUTS = ['loss', 'grad_x', 'grad_w_ada', 'grad_b_ada', 'grad_g_pre_f1', 'grad_g_post_f1', 'grad_w_f1_in', 'grad_w_f1_out', 'grad_g_pre_m', 'grad_g_post_m', 'grad_w_mix_in', 'grad_gmlp_norm_g', 'grad_gmlp_norm_b', 'grad_w_spatial', 'grad_b_spatial', 'grad_conv_w', 'grad_conv_b', 'grad_conv_norm_g', 'grad_conv_norm_b', 'grad_g_out_a', 'grad_g_out_b', 'grad_w_mix_out', 'grad_g_pre_f2', 'grad_g_post_f2', 'grad_w_f2_in', 'grad_w_f2_out', 'delta_w_ada', 'delta_b_ada', 'delta_g_pre_f1', 'delta_g_post_f1', 'delta_w_f1_in', 'delta_w_f1_out', 'delta_g_pre_m', 'delta_g_post_m', 'delta_w_mix_in', 'delta_gmlp_norm_g', 'delta_gmlp_norm_b', 'delta_w_spatial', 'delta_b_spatial', 'delta_conv_w', 'delta_conv_b', 'delta_conv_norm_g', 'delta_conv_norm_b', 'delta_g_out_a', 'delta_g_out_b', 'delta_w_mix_out', 'delta_g_pre_f2', 'delta_g_post_f2', 'delta_w_f2_in', 'delta_w_f2_out', 'new_m_w_ada', 'new_m_b_ada', 'new_m_g_pre_f1', 'new_m_g_post_f1', 'new_m_w_f1_in', 'new_m_w_f1_out', 'new_m_g_pre_m', 'new_m_g_post_m', 'new_m_w_mix_in', 'new_m_gmlp_norm_g', 'new_m_gmlp_norm_b', 'new_m_w_spatial', 'new_m_b_spatial', 'new_m_conv_w', 'new_m_conv_b', 'new_m_conv_norm_g', 'new_m_conv_norm_b', 'new_m_g_out_a', 'new_m_g_out_b', 'new_m_w_mix_out', 'new_m_g_pre_f2', 'new_m_g_post_f2', 'new_m_w_f2_in', 'new_m_w_f2_out', 'new_v_w_ada', 'new_v_b_ada', 'new_v_g_pre_f1', 'new_v_g_post_f1', 'new_v_w_f1_in', 'new_v_w_f1_out', 'new_v_g_pre_m', 'new_v_g_post_m', 'new_v_w_mix_in', 'new_v_gmlp_norm_g', 'new_v_gmlp_norm_b', 'new_v_w_spatial', 'new_v_b_spatial', 'new_v_conv_w', 'new_v_conv_b', 'new_v_conv_norm_g', 'new_v_conv_norm_b', 'new_v_g_out_a', 'new_v_g_out_b', 'new_v_w_mix_out', 'new_v_g_pre_f2', 'new_v_g_post_f2', 'new_v_w_f2_in', 'new_v_w_f2_out']
TWIN_LEAF_KINDS = {'loss': 'loss', 'grad_x': 'grad_x', 'grad_w_ada': 'grad_w', 'grad_b_ada': 'grad_w', 'grad_g_pre_f1': 'grad_w', 'grad_g_post_f1': 'grad_w', 'grad_w_f1_in': 'grad_w', 'grad_w_f1_out': 'grad_w', 'grad_g_pre_m': 'grad_w', 'grad_g_post_m': 'grad_w', 'grad_w_mix_in': 'grad_w', 'grad_gmlp_norm_g': 'grad_w', 'grad_gmlp_norm_b': 'grad_w', 'grad_w_spatial': 'grad_w', 'grad_b_spatial': 'grad_w', 'grad_conv_w': 'grad_w', 'grad_conv_b': 'grad_w', 'grad_conv_norm_g': 'grad_w', 'grad_conv_norm_b': 'grad_w', 'grad_g_out_a': 'grad_w', 'grad_g_out_b': 'grad_w', 'grad_w_mix_out': 'grad_w', 'grad_g_pre_f2': 'grad_w', 'grad_g_post_f2': 'grad_w', 'grad_w_f2_in': 'grad_w', 'grad_w_f2_out': 'grad_w', 'delta_w_ada': 'delta_w', 'delta_b_ada': 'delta_w', 'delta_g_pre_f1': 'delta_w', 'delta_g_post_f1': 'delta_w', 'delta_w_f1_in': 'delta_w', 'delta_w_f1_out': 'delta_w', 'delta_g_pre_m': 'delta_w', 'delta_g_post_m': 'delta_w', 'delta_w_mix_in': 'delta_w', 'delta_gmlp_norm_g': 'delta_w', 'delta_gmlp_norm_b': 'delta_w', 'delta_w_spatial': 'delta_w', 'delta_b_spatial': 'delta_w', 'delta_conv_w': 'delta_w', 'delta_conv_b': 'delta_w', 'delta_conv_norm_g': 'delta_w', 'delta_conv_norm_b': 'delta_w', 'delta_g_out_a': 'delta_w', 'delta_g_out_b': 'delta_w', 'delta_w_mix_out': 'delta_w', 'delta_g_pre_f2': 'delta_w', 'delta_g_post_f2': 'delta_w', 'delta_w_f2_in': 'delta_w', 'delta_w_f2_out': 'delta_w', 'new_m_w_ada': 'new_m', 'new_m_b_ada': 'new_m', 'new_m_g_pre_f1': 'new_m', 'new_m_g_post_f1': 'new_m', 'new_m_w_f1_in': 'new_m', 'new_m_w_f1_out': 'new_m', 'new_m_g_pre_m': 'new_m', 'new_m_g_post_m': 'new_m', 'new_m_w_mix_in': 'new_m', 'new_m_gmlp_norm_g': 'new_m', 'new_m_gmlp_norm_b': 'new_m', 'new_m_w_spatial': 'new_m', 'new_m_b_spatial': 'new_m', 'new_m_conv_w': 'new_m', 'new_m_conv_b': 'new_m', 'new_m_conv_norm_g': 'new_m', 'new_m_conv_norm_b': 'new_m', 'new_m_g_out_a': 'new_m', 'new_m_g_out_b': 'new_m', 'new_m_w_mix_out': 'new_m', 'new_m_g_pre_f2': 'new_m', 'new_m_g_post_f2': 'new_m', 'new_m_w_f2_in': 'new_m', 'new_m_w_f2_out': 'new_m', 'new_v_w_ada': 'new_v', 'new_v_b_ada': 'new_v', 'new_v_g_pre_f1': 'new_v', 'new_v_g_post_f1': 'new_v', 'new_v_w_f1_in': 'new_v', 'new_v_w_f1_out': 'new_v', 'new_v_g_pre_m': 'new_v', 'new_v_g_post_m': 'new_v', 'new_v_w_mix_in': 'new_v', 'new_v_gmlp_norm_g': 'new_v', 'new_v_gmlp_norm_b': 'new_v', 'new_v_w_spatial': 'new_v', 'new_v_b_spatial': 'new_v', 'new_v_conv_w': 'new_v', 'new_v_conv_b': 'new_v', 'new_v_conv_norm_g': 'new_v', 'new_v_conv_norm_b': 'new_v', 'new_v_g_out_a': 'new_v', 'new_v_g_out_b': 'new_v', 'new_v_w_mix_out': 'new_v', 'new_v_g_pre_f2': 'new_v', 'new_v_g_post_f2': 'new_v', 'new_v_w_f2_in': 'new_v', 'new_v_w_f2_out': 'new_v'}


def _forward(args):
    return _fwd_reference(*[args[k] for k in FWD_PARAMS])


def _output_shape():
    out = _jax.eval_shape(lambda: _forward(_fwd_setup_inputs(0)))
    return out.shape, out.dtype

N_MICROBATCH = 1
ADAM_LR = 0.001
ADAM_B1 = 0.9
ADAM_B2 = 0.999
ADAM_EPS = 1e-08
ADAM_WD = 0.01
ADAM_STEP = 10
PER_EXAMPLE_BATCH_AXIS = {'x': 0, 'c': 0, 'loss_target': 0}
SHARED_INPUTS = []
_WEIGHT_DTYPES = {'w_ada': _jnp.float32, 'b_ada': _jnp.float32, 'g_pre_f1': _jnp.float32, 'g_post_f1': _jnp.float32, 'w_f1_in': _jnp.float32, 'w_f1_out': _jnp.float32, 'g_pre_m': _jnp.float32, 'g_post_m': _jnp.float32, 'w_mix_in': _jnp.float32, 'gmlp_norm_g': _jnp.float32, 'gmlp_norm_b': _jnp.float32, 'w_spatial': _jnp.float32, 'b_spatial': _jnp.float32, 'conv_w': _jnp.float32, 'conv_b': _jnp.float32, 'conv_norm_g': _jnp.float32, 'conv_norm_b': _jnp.float32, 'g_out_a': _jnp.float32, 'g_out_b': _jnp.float32, 'w_mix_out': _jnp.float32, 'g_pre_f2': _jnp.float32, 'g_post_f2': _jnp.float32, 'w_f2_in': _jnp.float32, 'w_f2_out': _jnp.float32}
MOMENT_SCALE = {'w_ada': 1.085030e+00, 'b_ada': 2.037594e+00, 'g_pre_f1': 6.344382e-02, 'g_post_f1': 9.076577e-01, 'w_f1_in': 3.060727e-02, 'w_f1_out': 5.332201e-02, 'g_pre_m': 1.072641e-01, 'g_post_m': 3.741169e+00, 'w_mix_in': 1.049636e-01, 'gmlp_norm_g': 4.621143e-02, 'gmlp_norm_b': 4.598082e-02, 'w_spatial': 3.264994e-02, 'b_spatial': 4.754478e-02, 'conv_w': 1.357000e-01, 'conv_b': 6.824044e-01, 'conv_norm_g': 2.803333e-01, 'conv_norm_b': 4.471771e-01, 'g_out_a': 1.601953e-01, 'g_out_b': 1.902983e-01, 'w_mix_out': 1.935451e-01, 'g_pre_f2': 6.531422e-02, 'g_post_f2': 9.207680e-01, 'w_f2_in': 2.937242e-02, 'w_f2_out': 5.223750e-02}


def _to_microbatches(a, axis):
    t = _jnp.moveaxis(a, axis, 0)
    t = t.reshape((N_MICROBATCH, t.shape[0] // N_MICROBATCH) + t.shape[1:])
    return _jnp.moveaxis(t, 1, axis + 1)


def setup_inputs(seed: int = 0) -> dict:
    inp = _fwd_setup_inputs(seed)
    key = _jax.random.fold_in(_jax.random.key(seed), 7919)
    shape, _ = _output_shape()
    out = dict(inp)
    out["loss_target"] = _jax.random.normal(_jax.random.fold_in(key, 0), shape, _jnp.float32)
    for i, name in enumerate(TWIN_WEIGHTS):
        w = inp[name].astype(_jnp.float32)
        if MOMENT_SCALE is None:
            s = _jnp.sqrt(_jnp.mean(_jnp.square(w)) + 1e-30)
        else:
            s = MOMENT_SCALE[name]
        km, kv = _jax.random.split(_jax.random.fold_in(key, i + 1))
        out[name] = w
        out["m_" + name] = s * _jax.random.normal(km, w.shape, _jnp.float32)
        out["v_" + name] = (s * s) * _jax.random.uniform(kv, w.shape, _jnp.float32, 0.5, 1.5)
    if N_MICROBATCH > 1:
        for name, axis in PER_EXAMPLE_BATCH_AXIS.items():
            out[name] = _to_microbatches(out[name], axis)
    return {'x': out['x'], 'c': out['c'], 'w_ada': out['w_ada'], 'b_ada': out['b_ada'], 'g_pre_f1': out['g_pre_f1'], 'g_post_f1': out['g_post_f1'], 'w_f1_in': out['w_f1_in'], 'w_f1_out': out['w_f1_out'], 'g_pre_m': out['g_pre_m'], 'g_post_m': out['g_post_m'], 'w_mix_in': out['w_mix_in'], 'gmlp_norm_g': out['gmlp_norm_g'], 'gmlp_norm_b': out['gmlp_norm_b'], 'w_spatial': out['w_spatial'], 'b_spatial': out['b_spatial'], 'conv_w': out['conv_w'], 'conv_b': out['conv_b'], 'conv_norm_g': out['conv_norm_g'], 'conv_norm_b': out['conv_norm_b'], 'g_out_a': out['g_out_a'], 'g_out_b': out['g_out_b'], 'w_mix_out': out['w_mix_out'], 'g_pre_f2': out['g_pre_f2'], 'g_post_f2': out['g_post_f2'], 'w_f2_in': out['w_f2_in'], 'w_f2_out': out['w_f2_out'], 'loss_target': out['loss_target'], 'm_w_ada': out['m_w_ada'], 'm_b_ada': out['m_b_ada'], 'm_g_pre_f1': out['m_g_pre_f1'], 'm_g_post_f1': out['m_g_post_f1'], 'm_w_f1_in': out['m_w_f1_in'], 'm_w_f1_out': out['m_w_f1_out'], 'm_g_pre_m': out['m_g_pre_m'], 'm_g_post_m': out['m_g_post_m'], 'm_w_mix_in': out['m_w_mix_in'], 'm_gmlp_norm_g': out['m_gmlp_norm_g'], 'm_gmlp_norm_b': out['m_gmlp_norm_b'], 'm_w_spatial': out['m_w_spatial'], 'm_b_spatial': out['m_b_spatial'], 'm_conv_w': out['m_conv_w'], 'm_conv_b': out['m_conv_b'], 'm_conv_norm_g': out['m_conv_norm_g'], 'm_conv_norm_b': out['m_conv_norm_b'], 'm_g_out_a': out['m_g_out_a'], 'm_g_out_b': out['m_g_out_b'], 'm_w_mix_out': out['m_w_mix_out'], 'm_g_pre_f2': out['m_g_pre_f2'], 'm_g_post_f2': out['m_g_post_f2'], 'm_w_f2_in': out['m_w_f2_in'], 'm_w_f2_out': out['m_w_f2_out'], 'v_w_ada': out['v_w_ada'], 'v_b_ada': out['v_b_ada'], 'v_g_pre_f1': out['v_g_pre_f1'], 'v_g_post_f1': out['v_g_post_f1'], 'v_w_f1_in': out['v_w_f1_in'], 'v_w_f1_out': out['v_w_f1_out'], 'v_g_pre_m': out['v_g_pre_m'], 'v_g_post_m': out['v_g_post_m'], 'v_w_mix_in': out['v_w_mix_in'], 'v_gmlp_norm_g': out['v_gmlp_norm_g'], 'v_gmlp_norm_b': out['v_gmlp_norm_b'], 'v_w_spatial': out['v_w_spatial'], 'v_b_spatial': out['v_b_spatial'], 'v_conv_w': out['v_conv_w'], 'v_conv_b': out['v_conv_b'], 'v_conv_norm_g': out['v_conv_norm_g'], 'v_conv_norm_b': out['v_conv_norm_b'], 'v_g_out_a': out['v_g_out_a'], 'v_g_out_b': out['v_g_out_b'], 'v_w_mix_out': out['v_w_mix_out'], 'v_g_pre_f2': out['v_g_pre_f2'], 'v_g_post_f2': out['v_g_post_f2'], 'v_w_f2_in': out['v_w_f2_in'], 'v_w_f2_out': out['v_w_f2_out']}


def _loss(weights, diff, rest, loss_target):
    with _jax.named_scope("forward"):
        args = {**rest, TWIN_DIFF_INPUT: diff, **{k: w.astype(_WEIGHT_DTYPES[k]) for k, w in weights.items()}}
        y = _forward(args)
    with _jax.named_scope("loss_head"):
        err = _jnp.square(y.astype(_jnp.float32) - loss_target)
        return 0.5 * _jnp.sum(_jnp.mean(err, axis=-1)) if err.ndim else 0.5 * err


def _adamw(w, g, m, v):
    m = ADAM_B1 * m + (1.0 - ADAM_B1) * g
    v = ADAM_B2 * v + (1.0 - ADAM_B2) * _jnp.square(g)
    m_hat = m / (1.0 - ADAM_B1 ** ADAM_STEP)
    v_hat = v / (1.0 - ADAM_B2 ** ADAM_STEP)
    delta = -ADAM_LR * (m_hat / (_jnp.sqrt(v_hat) + ADAM_EPS) + ADAM_WD * w)
    return delta, m, v


def reference(x, c, w_ada, b_ada, g_pre_f1, g_post_f1, w_f1_in, w_f1_out, g_pre_m, g_post_m, w_mix_in, gmlp_norm_g, gmlp_norm_b, w_spatial, b_spatial, conv_w, conv_b, conv_norm_g, conv_norm_b, g_out_a, g_out_b, w_mix_out, g_pre_f2, g_post_f2, w_f2_in, w_f2_out, loss_target, m_w_ada, m_b_ada, m_g_pre_f1, m_g_post_f1, m_w_f1_in, m_w_f1_out, m_g_pre_m, m_g_post_m, m_w_mix_in, m_gmlp_norm_g, m_gmlp_norm_b, m_w_spatial, m_b_spatial, m_conv_w, m_conv_b, m_conv_norm_g, m_conv_norm_b, m_g_out_a, m_g_out_b, m_w_mix_out, m_g_pre_f2, m_g_post_f2, m_w_f2_in, m_w_f2_out, v_w_ada, v_b_ada, v_g_pre_f1, v_g_post_f1, v_w_f1_in, v_w_f1_out, v_g_pre_m, v_g_post_m, v_w_mix_in, v_gmlp_norm_g, v_gmlp_norm_b, v_w_spatial, v_b_spatial, v_conv_w, v_conv_b, v_conv_norm_g, v_conv_norm_b, v_g_out_a, v_g_out_b, v_w_mix_out, v_g_pre_f2, v_g_post_f2, v_w_f2_in, v_w_f2_out):
    given = dict(x=x, c=c, w_ada=w_ada, b_ada=b_ada, g_pre_f1=g_pre_f1, g_post_f1=g_post_f1, w_f1_in=w_f1_in, w_f1_out=w_f1_out, g_pre_m=g_pre_m, g_post_m=g_post_m, w_mix_in=w_mix_in, gmlp_norm_g=gmlp_norm_g, gmlp_norm_b=gmlp_norm_b, w_spatial=w_spatial, b_spatial=b_spatial, conv_w=conv_w, conv_b=conv_b, conv_norm_g=conv_norm_g, conv_norm_b=conv_norm_b, g_out_a=g_out_a, g_out_b=g_out_b, w_mix_out=w_mix_out, g_pre_f2=g_pre_f2, g_post_f2=g_post_f2, w_f2_in=w_f2_in, w_f2_out=w_f2_out, loss_target=loss_target, m_w_ada=m_w_ada, m_b_ada=m_b_ada, m_g_pre_f1=m_g_pre_f1, m_g_post_f1=m_g_post_f1, m_w_f1_in=m_w_f1_in, m_w_f1_out=m_w_f1_out, m_g_pre_m=m_g_pre_m, m_g_post_m=m_g_post_m, m_w_mix_in=m_w_mix_in, m_gmlp_norm_g=m_gmlp_norm_g, m_gmlp_norm_b=m_gmlp_norm_b, m_w_spatial=m_w_spatial, m_b_spatial=m_b_spatial, m_conv_w=m_conv_w, m_conv_b=m_conv_b, m_conv_norm_g=m_conv_norm_g, m_conv_norm_b=m_conv_norm_b, m_g_out_a=m_g_out_a, m_g_out_b=m_g_out_b, m_w_mix_out=m_w_mix_out, m_g_pre_f2=m_g_pre_f2, m_g_post_f2=m_g_post_f2, m_w_f2_in=m_w_f2_in, m_w_f2_out=m_w_f2_out, v_w_ada=v_w_ada, v_b_ada=v_b_ada, v_g_pre_f1=v_g_pre_f1, v_g_post_f1=v_g_post_f1, v_w_f1_in=v_w_f1_in, v_w_f1_out=v_w_f1_out, v_g_pre_m=v_g_pre_m, v_g_post_m=v_g_post_m, v_w_mix_in=v_w_mix_in, v_gmlp_norm_g=v_gmlp_norm_g, v_gmlp_norm_b=v_gmlp_norm_b, v_w_spatial=v_w_spatial, v_b_spatial=v_b_spatial, v_conv_w=v_conv_w, v_conv_b=v_conv_b, v_conv_norm_g=v_conv_norm_g, v_conv_norm_b=v_conv_norm_b, v_g_out_a=v_g_out_a, v_g_out_b=v_g_out_b, v_w_mix_out=v_w_mix_out, v_g_pre_f2=v_g_pre_f2, v_g_post_f2=v_g_post_f2, v_w_f2_in=v_w_f2_in, v_w_f2_out=v_w_f2_out)
    weights = {n: given[n] for n in TWIN_WEIGHTS}
    shared = {n: given[n] for n in SHARED_INPUTS}
    per_example = {n: given[n] for n in ['x', 'c']}
    grad_fn = _jax.value_and_grad(_loss, argnums=(0, 1))

    def one_microbatch(ex, loss_target):
        ex = dict(ex)
        diff = ex.pop(TWIN_DIFF_INPUT)
        return grad_fn(weights, diff, {**shared, **ex}, loss_target)

    if N_MICROBATCH == 1:
        loss, (grad_w, grad_x) = one_microbatch(per_example, given["loss_target"])
    else:
        def body(carry, xs):
            loss_sum, grad_sum = carry
            l_k, (gw_k, gx_k) = one_microbatch(xs[0], xs[1])
            with _jax.named_scope("update"):
                return (loss_sum + l_k, _jax.tree.map(_jnp.add, grad_sum, gw_k)), gx_k

        init = (_jnp.zeros((), _jnp.float32), _jax.tree.map(_jnp.zeros_like, weights))
        (loss, grad_w), grad_x = _jax.lax.scan(body, init, (per_example, given["loss_target"]))
    with _jax.named_scope("update"):
        delta_w, new_m, new_v = {}, {}, {}
        for n in TWIN_WEIGHTS:
            delta_w[n], new_m[n], new_v[n] = _adamw(weights[n], grad_w[n], given["m_" + n], given["v_" + n])
    return (loss, grad_x, *[grad_w[n] for n in TWIN_WEIGHTS], *[delta_w[n] for n in TWIN_WEIGHTS],
            *[new_m[n] for n in TWIN_WEIGHTS], *[new_v[n] for n in TWIN_WEIGHTS])
```

```python
import functools

import jax
import jax.numpy as jnp
from jax import lax
from jax.experimental import pallas as pl
from jax.experimental.pallas import tpu as pltpu

D = 1024
DFF = 2816
WA = 512
WB = 512
NH = 8
HD = 64
CH = 128
CK = 31
HALO = 32
NMOD = 9
EPS = 1e-6
NCHIP = 4
NDEV = 8
FBLK = DFF // 2
ADA_SH = NMOD * D // NCHIP

LR, B1, B2, EPS_A, WD, STEP = 0.001, 0.9, 0.999, 1e-08, 0.01, 10

F32 = jnp.float32
BF16 = jnp.bfloat16
MESH = pl.DeviceIdType.MESH
ANY = pl.BlockSpec(memory_space=pl.ANY)
VMEM_FULL = pl.BlockSpec(memory_space=pltpu.VMEM)
VMEM_LIMIT = 56 * 1024 * 1024

NT = (((1,), (1,)), ((), ()))
TN = (((0,), (0,)), ((), ()))


def _dot(a, b):
    return jnp.dot(a, b, preferred_element_type=F32)


def _dot_nt(a, b):
    return lax.dot_general(a, b, NT, preferred_element_type=F32)


def _dot_tn(a, b):
    return lax.dot_general(a, b, TN, preferred_element_type=F32)


def _cparams():
    return pltpu.CompilerParams(vmem_limit_bytes=VMEM_LIMIT)


def _allgather8(name, arrs):
    n = len(arrs)

    def body(*refs):
        ins, outs = refs[:n], refs[n:2 * n]
        send_sems, recv_sems, local_sems = refs[2 * n:]
        x, y, c = lax.axis_index("x"), lax.axis_index("y"), lax.axis_index("c")
        me, sibling = (x, y, c), (x, y, 1 - c)
        chips = [(1 - x, y), (x, 1 - y), (1 - x, 1 - y)]

        def copy(a, k, block, to, src=None):
            rows = outs[a].at[4 * block[0] + 2 * block[1] + block[2]]
            return pltpu.make_async_remote_copy(
                src_ref=rows if src is None else src, dst_ref=rows,
                send_sem=send_sems.at[a, k], recv_sem=recv_sems.at[a, k],
                device_id=to, device_id_type=MESH)

        started, mine = [], []
        for a in range(n):
            loc = pltpu.make_async_copy(ins[a], outs[a].at[4 * x + 2 * y + c], local_sems.at[a])
            loc.start()
            mine.append(loc)
            first = [copy(a, 0, me, sibling, src=ins[a])]
            first += [copy(a, 1 + j, me, (*chip, c), src=ins[a]) for j, chip in enumerate(chips)]
            for cp in first:
                cp.start()
            started += first
        for a in range(n):
            for j, chip in enumerate(chips):
                copy(a, 1 + j, (*chip, c), me).wait_recv()
                fwd = copy(a, 4 + j, (*chip, c), sibling)
                fwd.start()
                started.append(fwd)
        for a in range(n):
            copy(a, 0, sibling, me).wait_recv()
            for j, chip in enumerate(chips):
                copy(a, 4 + j, (*chip, 1 - c), me).wait_recv()
        for cp in started:
            cp.wait_send()
        for loc in mine:
            loc.wait()

    return pl.pallas_call(
        body, name=name,
        out_shape=[jax.ShapeDtypeStruct((NDEV,) + a.shape, a.dtype) for a in arrs],
        in_specs=[ANY] * n, out_specs=[ANY] * n,
        scratch_shapes=[pltpu.SemaphoreType.DMA((n, 7)), pltpu.SemaphoreType.DMA((n, 7)),
                        pltpu.SemaphoreType.DMA((n,))],
    )(*arrs)


def _chip_relations(x, y):
    return [(1 - x, y), (x, 1 - y), (1 - x, 1 - y)]


def _exchange(name, arrs, out_shapes, plan):
    n = len(arrs)
    n_out = len(out_shapes)

    def body(*refs):
        ins, outs = refs[:n], refs[n:n + n_out]
        send_sems, recv_sems, local_sems = refs[n + n_out:]
        x, y, c = lax.axis_index("x"), lax.axis_index("y"), lax.axis_index("c")
        local, sends = plan(x, y, c, ins, outs)
        locs = [pltpu.make_async_copy(s, d, local_sems.at[i]) for i, (s, d) in enumerate(local)]
        for loc in locs:
            loc.start()
        cps = [pltpu.make_async_remote_copy(src_ref=s, dst_ref=d, send_sem=send_sems.at[i], recv_sem=recv_sems.at[i],
                                            device_id=peer, device_id_type=MESH)
               for i, (s, d, peer, _) in enumerate(sends)]
        for cp in cps:
            cp.start()
        for i, (s, _, peer, landing) in enumerate(sends):
            pltpu.make_async_remote_copy(src_ref=s, dst_ref=landing, send_sem=send_sems.at[i], recv_sem=recv_sems.at[i],
                                         device_id=peer, device_id_type=MESH).wait_recv()
        for cp in cps:
            cp.wait_send()
        for loc in locs:
            loc.wait()

    return n, n_out, body


def _run_exchange(name, arrs, out_shapes, plan, n_local, n_send):
    n, n_out, body = _exchange(name, arrs, out_shapes, plan)
    return pl.pallas_call(
        body, name=name, out_shape=out_shapes,
        in_specs=[ANY] * n, out_specs=[ANY] * n_out,
        scratch_shapes=[pltpu.SemaphoreType.DMA((n_send,)), pltpu.SemaphoreType.DMA((n_send,)),
                        pltpu.SemaphoreType.DMA((max(n_local, 1),))],
    )(*arrs)


def _chip_allgather(name, arrs):
    n = len(arrs)

    def plan(x, y, c, ins, outs):
        j_me = 2 * x + y
        local = [(ins[a], outs[a].at[j_me]) for a in range(n)]
        sends = []
        for a in range(n):
            for (px, py) in _chip_relations(x, y):
                sends.append((ins[a], outs[a].at[j_me], (px, py, c), outs[a].at[2 * px + py]))
        return local, sends

    shapes = [jax.ShapeDtypeStruct((NCHIP,) + a.shape, a.dtype) for a in arrs]
    return _run_exchange(name, arrs, shapes, plan, n, 3 * n)


def _chip_scatter(name, arrs):
    n = len(arrs)

    def plan(x, y, c, ins, outs):
        sends = []
        for a in range(n):
            for k, (px, py) in enumerate(_chip_relations(x, y)):
                sends.append((ins[a].at[2 * px + py], outs[a].at[k], (px, py, c), outs[a].at[k]))
        return [], sends

    shapes = [jax.ShapeDtypeStruct((3,) + a.shape[1:], a.dtype) for a in arrs]
    return _run_exchange(name, arrs, shapes, plan, 0, 3 * n)


def _sibling_swap(name, arrs):
    n = len(arrs)

    def plan(x, y, c, ins, outs):
        return [], [(ins[a], outs[a], (x, y, 1 - c), outs[a]) for a in range(n)]

    shapes = [jax.ShapeDtypeStruct(a.shape, a.dtype) for a in arrs]
    return _run_exchange(name, arrs, shapes, plan, 0, n)


def _rms(x):
    r = lax.rsqrt(jnp.mean(x * x, axis=-1, keepdims=True) + EPS)
    return x * r, r


def _rms_bwd(dy, n, r, g):
    dg = jnp.sum(dy * n, axis=0, keepdims=True)
    dn = dy * g
    dx = r * (dn - n * jnp.mean(dn * n, axis=-1, keepdims=True))
    return dx, dg


def _ln(x):
    mu = jnp.mean(x, axis=-1, keepdims=True)
    xc = x - mu
    rstd = lax.rsqrt(jnp.mean(xc * xc, axis=-1, keepdims=True) + EPS)
    return xc * rstd, rstd


def _ln_bwd(dy, xhat, rstd, g):
    dg = jnp.sum(dy * xhat, axis=0, keepdims=True)
    db = jnp.sum(dy, axis=0, keepdims=True)
    dxh = dy * g
    dx = rstd * (dxh - jnp.mean(dxh, axis=-1, keepdims=True) - xhat * jnp.mean(dxh * xhat, axis=-1, keepdims=True))
    return dx, dg, db


def _sigmoid(x):
    return jax.nn.sigmoid(x)


def _dsilu(x, s):
    return s * (1.0 + x * (1.0 - s))


def _adam(w, g, m, v):
    m = B1 * m + (1.0 - B1) * g
    v = B2 * v + (1.0 - B2) * (g * g)
    m_hat = m / (1.0 - B1 ** STEP)
    v_hat = v / (1.0 - B2 ** STEP)
    delta = -LR * (m_hat / (jnp.sqrt(v_hat) + EPS_A) + WD * w)
    return delta, m, v


def _head_mask(shape):
    lane = lax.broadcasted_iota(jnp.int32, shape, len(shape) - 1)
    return [(lane >= h * HD) & (lane < (h + 1) * HD) for h in range(NH)]


def _first(b, i):
    return jnp.logical_and(b == 0, i == 0)


def _acc(ref, val, first):
    @pl.when(first)
    def _():
        ref[...] = val

    @pl.when(jnp.logical_not(first))
    def _():
        ref[...] += val


def _ada_fwd(c_all, w_sh, b_sh):
    nb = c_all.shape[0]
    tn = 768

    def body(c_ref, w_ref, b_ref, o_ref):
        cv = c_ref[...]
        cs = (cv * _sigmoid(cv)).astype(BF16)
        o_ref[...] = _dot(cs, w_ref[...].astype(BF16)) + b_ref[...]

    return pl.pallas_call(
        body, name="ada_fwd", grid=(ADA_SH // tn,),
        out_shape=jax.ShapeDtypeStruct((nb, ADA_SH), F32),
        in_specs=[pl.BlockSpec((nb, D), lambda j: (0, 0)), pl.BlockSpec((D, tn), lambda j: (0, j)),
                  pl.BlockSpec((1, tn), lambda j: (0, j))],
        out_specs=pl.BlockSpec((nb, tn), lambda j: (0, j)),
        compiler_params=_cparams(),
    )(c_all, w_sh, b_sh)


def _ada_bwd_adam(c_all, dada_sh, w, m, v):
    nb = c_all.shape[0]
    tn = 768

    def body(c_ref, d_ref, w_ref, m_ref, v_ref, g_out, d_out, m_out, v_out):
        cv = c_ref[...]
        cs = (cv * _sigmoid(cv)).astype(BF16)
        g = _dot_tn(cs, d_ref[...].astype(BF16))
        delta, m2, v2 = _adam(w_ref[...], g, m_ref[...], v_ref[...])
        g_out[...] = g
        d_out[...] = delta
        m_out[...] = m2
        v_out[...] = v2

    big = pl.BlockSpec((D, tn), lambda j: (0, j))
    shape = jax.ShapeDtypeStruct((D, ADA_SH), F32)
    return pl.pallas_call(
        body, name="ada_bwd_adam", grid=(ADA_SH // tn,),
        out_shape=[shape] * 4,
        in_specs=[pl.BlockSpec((nb, D), lambda j: (0, 0)), pl.BlockSpec((nb, tn), lambda j: (0, j)), big, big, big],
        out_specs=[big] * 4,
        compiler_params=_cparams(),
    )(c_all, dada_sh, w, m, v)


def _tok_specs(tm, width):
    return pl.BlockSpec((1, tm, width), lambda b, i: (b, i, 0))


def _mod_spec():
    return pl.BlockSpec((1, 1, D), lambda b, i: (b, 0, 0))


def _row_spec(width=D):
    return pl.BlockSpec((1, width), lambda b, i: (0, 0))


def _ffn_fwd(x, sh, sc, gt, g_pre, g_post, w_in4, w_out, target=None):
    nb, s, _ = x.shape
    tm = min(256, s)
    with_loss = target is not None

    def body(*refs):
        if with_loss:
            (x_ref, sh_ref, sc_ref, gt_ref, gpre_ref, gpost_ref, win_ref, wout_ref, tgt_ref,
             xo_ref, f_ref, p_ref, ls_ref) = refs
        else:
            (x_ref, sh_ref, sc_ref, gt_ref, gpre_ref, gpost_ref, win_ref, wout_ref,
             xo_ref, f_ref, p_ref) = refs
        xv = x_ref[0]
        n, _ = _rms(xv)
        h = (n * gpre_ref[...]) * (1.0 + sc_ref[0]) + sh_ref[0]
        hb = h.astype(BF16)
        acc = jnp.zeros((tm, D), F32)
        for j in range(2):
            gate = _dot(hb, win_ref[j])
            up = _dot(hb, win_ref[2 + j])
            p_ref[0, :, j * FBLK:(j + 1) * FBLK] = gate.astype(BF16)
            p_ref[0, :, DFF + j * FBLK:DFF + (j + 1) * FBLK] = up.astype(BF16)
            a = (gate * _sigmoid(gate)) * up
            acc = acc + _dot(a.astype(BF16), wout_ref[j * FBLK:(j + 1) * FBLK, :])
        f_ref[0] = acc
        nf, _ = _rms(acc)
        out = xv + (0.5 * gt_ref[0]) * (nf * gpost_ref[...])
        if with_loss:
            err = out - tgt_ref[0]
            xo_ref[0] = err * (1.0 / D)
            row = jnp.sum(err * err, axis=0, keepdims=True)
            part = row[:, 0:128]
            for k in range(1, D // 128):
                part = part + row[:, k * 128:(k + 1) * 128]
            _acc(ls_ref, part, _first(pl.program_id(0), pl.program_id(1)))
        else:
            xo_ref[0] = out

    in_specs = [_tok_specs(tm, D), _mod_spec(), _mod_spec(), _mod_spec(), _row_spec(), _row_spec(), VMEM_FULL, VMEM_FULL]
    args = [x, sh, sc, gt, g_pre, g_post, w_in4, w_out]
    out_shape = [jax.ShapeDtypeStruct((nb, s, D), F32), jax.ShapeDtypeStruct((nb, s, D), F32),
                 jax.ShapeDtypeStruct((nb, s, 2 * DFF), BF16)]
    out_specs = [_tok_specs(tm, D), _tok_specs(tm, D), _tok_specs(tm, 2 * DFF)]
    if with_loss:
        in_specs.append(_tok_specs(tm, D))
        args.append(target)
        out_shape.append(jax.ShapeDtypeStruct((1, 128), F32))
        out_specs.append(pl.BlockSpec((1, 128), lambda b, i: (0, 0)))
    return pl.pallas_call(
        body, name="ffn_loss_fwd" if with_loss else "ffn_fwd", grid=(nb, s // tm),
        out_shape=out_shape, in_specs=in_specs, out_specs=out_specs,
        compiler_params=_cparams(),
    )(*args)


def _ffn_bwd(dxo, x, f, p, sh, sc, gt, g_pre, g_post, w_in4, w_out):
    nb, s, _ = x.shape
    tm = min(256, s)

    def body(dxo_ref, x_ref, f_ref, p_ref, sh_ref, sc_ref, gt_ref, gpre_ref, gpost_ref, win_ref, wout_ref,
             dx_ref, dp_ref, h_ref, a_ref, df_ref, dgpre_ref, dgpost_ref, dsh_ref, dsc_ref, dgt_ref):
        b, i = pl.program_id(0), pl.program_id(1)
        dxo_v = dxo_ref[0]
        nf, q = _rms(f_ref[0])
        gpost = gpost_ref[...]
        dgt = jnp.sum(dxo_v * (0.5 * (nf * gpost)), axis=0, keepdims=True)
        do = dxo_v * (0.5 * gt_ref[0])
        df, dgpost = _rms_bwd(do, nf, q, gpost)
        dfb = df.astype(BF16)
        df_ref[0] = dfb
        xv = x_ref[0]
        n, r = _rms(xv)
        gpre = gpre_ref[...]
        ng = n * gpre
        scale1 = 1.0 + sc_ref[0]
        h = ng * scale1 + sh_ref[0]
        h_ref[0] = h.astype(BF16)
        dh = jnp.zeros((tm, D), F32)
        for j in range(2):
            gate = p_ref[0, :, j * FBLK:(j + 1) * FBLK].astype(F32)
            up = p_ref[0, :, DFF + j * FBLK:DFF + (j + 1) * FBLK].astype(F32)
            sg = _sigmoid(gate)
            act = gate * sg
            a_ref[0, :, j * FBLK:(j + 1) * FBLK] = (act * up).astype(BF16)
            da = _dot_nt(dfb, wout_ref[j * FBLK:(j + 1) * FBLK, :])
            dgate = (da * up * _dsilu(gate, sg)).astype(BF16)
            dup = (da * act).astype(BF16)
            dp_ref[0, :, j * FBLK:(j + 1) * FBLK] = dgate
            dp_ref[0, :, DFF + j * FBLK:DFF + (j + 1) * FBLK] = dup
            dh = dh + _dot_nt(dgate, win_ref[j]) + _dot_nt(dup, win_ref[2 + j])
        dsh = jnp.sum(dh, axis=0, keepdims=True)
        dsc = jnp.sum(dh * ng, axis=0, keepdims=True)
        dxn, dgpre = _rms_bwd(dh * scale1, n, r, gpre)
        dx_ref[0] = dxo_v + dxn
        _acc(dgpre_ref, dgpre, _first(b, i))
        _acc(dgpost_ref, dgpost, _first(b, i))
        _acc(dsh_ref, dsh[None], i == 0)
        _acc(dsc_ref, dsc[None], i == 0)
        _acc(dgt_ref, dgt[None], i == 0)

    tok = _tok_specs(tm, D)
    mod_shape = jax.ShapeDtypeStruct((nb, 1, D), F32)
    row_shape = jax.ShapeDtypeStruct((1, D), F32)
    return pl.pallas_call(
        body, name="ffn_bwd", grid=(nb, s // tm),
        out_shape=[jax.ShapeDtypeStruct((nb, s, D), F32), jax.ShapeDtypeStruct((nb, s, 2 * DFF), BF16),
                   jax.ShapeDtypeStruct((nb, s, D), BF16), jax.ShapeDtypeStruct((nb, s, DFF), BF16),
                   jax.ShapeDtypeStruct((nb, s, D), BF16), row_shape, row_shape, mod_shape, mod_shape, mod_shape],
        in_specs=[tok, tok, tok, _tok_specs(tm, 2 * DFF), _mod_spec(), _mod_spec(), _mod_spec(), _row_spec(), _row_spec(),
                  VMEM_FULL, VMEM_FULL],
        out_specs=[tok, _tok_specs(tm, 2 * DFF), tok, _tok_specs(tm, DFF), tok, _row_spec(), _row_spec(),
                   _mod_spec(), _mod_spec(), _mod_spec()],
        compiler_params=_cparams(),
    )(dxo, x, f, p, sh, sc, gt, g_pre, g_post, w_in4, w_out)


def _wgrad(name, a, b, col_block, chip_major):
    t, ka = a.shape
    n = b.shape[1]
    tk = min(512, t)
    nk = t // tk
    nblk = n // col_block

    def body(a_ref, b_ref, o_ref, obf_ref, acc_ref):
        k = pl.program_id(1)

        @pl.when(k == 0)
        def _():
            acc_ref[...] = jnp.zeros_like(acc_ref)

        acc_ref[...] += _dot_tn(a_ref[...], b_ref[...])

        @pl.when(k == nk - 1)
        def _():
            val = acc_ref[...]
            if chip_major:
                o_ref[0] = val
                obf_ref[0] = val.astype(BF16)
            else:
                o_ref[...] = val
                obf_ref[...] = val.astype(BF16)

    if chip_major:
        shape = (nblk, ka, col_block)
        ospec = pl.BlockSpec((1, ka, col_block), lambda j, k: (j, 0, 0))
    else:
        shape = (ka, n)
        ospec = pl.BlockSpec((ka, col_block), lambda j, k: (0, j))
    return pl.pallas_call(
        body, name=name, grid=(nblk, nk),
        out_shape=[jax.ShapeDtypeStruct(shape, F32), jax.ShapeDtypeStruct(shape, BF16)],
        in_specs=[pl.BlockSpec((tk, ka), lambda j, k: (k, 0)), pl.BlockSpec((tk, col_block), lambda j, k: (k, j))],
        out_specs=[ospec, ospec],
        scratch_shapes=[pltpu.VMEM((ka, col_block), F32)],
        compiler_params=_cparams(),
    )(a, b)


def _mix_in_fwd(x, sh, sc, g_pre, w_mi4):
    nb, s, _ = x.shape
    tm = min(512, s)

    def body(x_ref, sh_ref, sc_ref, gpre_ref, w_ref, u_ref, v_ref, a_ref, g_ref):
        n, _ = _rms(x_ref[0])
        hb = ((n * gpre_ref[...]) * (1.0 + sc_ref[0]) + sh_ref[0]).astype(BF16)
        for k, o_ref in enumerate((u_ref, v_ref, a_ref, g_ref)):
            o_ref[0] = _dot(hb, w_ref[k])

    shape = jax.ShapeDtypeStruct((nb, s, WA), F32)
    return pl.pallas_call(
        body, name="mix_in_fwd", grid=(nb, s // tm),
        out_shape=[shape] * 4,
        in_specs=[_tok_specs(tm, D), _mod_spec(), _mod_spec(), _row_spec(), VMEM_FULL],
        out_specs=[_tok_specs(tm, WA)] * 4,
        compiler_params=_cparams(),
    )(x, sh, sc, g_pre, w_mi4)


def _spatial_weights(wcat_ref, transposed):
    w = wcat_ref[...]
    row = lax.broadcasted_iota(jnp.int32, w.shape, 0)
    col = lax.broadcasted_iota(jnp.int32, w.shape, 1)
    keep = ((row & (CH - 1)) <= col) if transposed else ((col & (CH - 1)) <= row)
    return jnp.where(keep, w, 0.0).astype(BF16)


def _expand_heads(vc, masks):
    return jnp.concatenate([jnp.where(mk, vc, jnp.zeros_like(vc)) for mk in masks], axis=0)


def _spatial_bias(bspt_ref):
    return bspt_ref[...]


def _conv_taps(ext_ref, w_ref, tm, offset):
    acc = jnp.zeros((tm, WB), F32)
    for k in range(CK):
        acc = acc + w_ref[k:k + 1, :] * ext_ref[offset(k):offset(k) + tm, :]
    return acc


def _halo_prev_spec(tm):
    return pl.BlockSpec((1, HALO, WB), lambda b, i: (b, jnp.maximum(i * (tm // HALO) - 1, 0), 0))


def _halo_next_spec(tm, s):
    return pl.BlockSpec((1, HALO, WB), lambda b, i: (b, jnp.minimum((i + 1) * (tm // HALO), s // HALO - 1), 0))


def _mix_mid_fwd(x, u, v, a, g, gt, gn_g, gn_b, wcat, bspt, conv_w, conv_b, cn_g, cn_b, go_a, go_b, w_mo, g_post):
    nb, s, _ = x.shape
    tm = min(512, s)

    def body(x_ref, u_ref, v_ref, a_ref, g_ref, ah_ref, gh_ref, gt_ref, gng_ref, gnb_ref, wcat_ref, bspt_ref,
             cw_ref, cb_ref, cng_ref, cnb_ref, goa_ref, gob_ref, wmo_ref, gpost_ref,
             xo_ref, conv_ref, y_ref, m_ref, ext_ref):
        i = pl.program_id(1)
        xhat, _ = _ln(v_ref[0])
        vb = (xhat * gng_ref[...] + gnb_ref[...]).astype(BF16)
        wsb = _spatial_weights(wcat_ref, False)
        bias = _spatial_bias(bspt_ref)
        masks = _head_mask((CH, WA))
        zs = []
        for cidx in range(tm // CH):
            vexp = _expand_heads(vb[cidx * CH:(cidx + 1) * CH, :], masks)
            zs.append(_dot(wsb, vexp) + bias)
        z = jnp.concatenate(zs, axis=0)
        na, _ = _rms(u_ref[0] * z)
        keep = jnp.where(i == 0, 0.0, 1.0).astype(F32)
        ext_ref[0:HALO, :] = (ah_ref[0] * _sigmoid(gh_ref[0])) * keep
        ext_ref[HALO:HALO + tm, :] = a_ref[0] * _sigmoid(g_ref[0])
        conv = _conv_taps(ext_ref, cw_ref, tm, lambda k: k + HALO - (CK - 1)) + cb_ref[...]
        conv_ref[0] = conv
        chat, _ = _ln(conv)
        cln = chat * cng_ref[...] + cnb_ref[...]
        nbb, _ = _rms(cln * _sigmoid(cln))
        yb = jnp.concatenate([na * goa_ref[...], nbb * gob_ref[...]], axis=1).astype(BF16)
        y_ref[0] = yb
        m = _dot(yb, wmo_ref[...])
        m_ref[0] = m
        nm, _ = _rms(m)
        xo_ref[0] = x_ref[0] + gt_ref[0] * (nm * gpost_ref[...])

    t5 = _tok_specs(tm, WA)
    tok = _tok_specs(tm, D)
    r5 = _row_spec(WA)
    full = lambda shape: pl.BlockSpec(shape, lambda b, i: (0,) * len(shape))
    return pl.pallas_call(
        body, name="mix_mid_fwd", grid=(nb, s // tm),
        out_shape=[jax.ShapeDtypeStruct((nb, s, D), F32), jax.ShapeDtypeStruct((nb, s, WB), F32),
                   jax.ShapeDtypeStruct((nb, s, D), BF16), jax.ShapeDtypeStruct((nb, s, D), F32)],
        in_specs=[tok, t5, t5, t5, t5, _halo_prev_spec(tm), _halo_prev_spec(tm), _mod_spec(), r5, r5,
                  full((CH, NH * CH)), full((CH, WA)), full((HALO, WB)), r5, r5, r5, r5, r5, VMEM_FULL, _row_spec()],
        out_specs=[tok, t5, tok, tok],
        scratch_shapes=[pltpu.VMEM((HALO + tm, WB), F32)],
        compiler_params=_cparams(),
    )(x, u, v, a, g, a, g, gt, gn_g, gn_b, wcat, bspt, conv_w, conv_b, cn_g, cn_b, go_a, go_b, w_mo, g_post)


def _mix_out_bwd(dxo, m, gt, g_post, w_mo):
    nb, s, _ = m.shape
    tm = min(512, s)

    def body(dxo_ref, m_ref, gt_ref, gpost_ref, wmo_ref, dy_ref, dm_ref, dgpost_ref, dgt_ref):
        b, i = pl.program_id(0), pl.program_id(1)
        dxo_v = dxo_ref[0]
        nm, q = _rms(m_ref[0])
        gpost = gpost_ref[...]
        dgt = jnp.sum(dxo_v * (nm * gpost), axis=0, keepdims=True)
        dm, dgpost = _rms_bwd(dxo_v * gt_ref[0], nm, q, gpost)
        dmb = dm.astype(BF16)
        dm_ref[0] = dmb
        dy_ref[0] = _dot_nt(dmb, wmo_ref[...])
        _acc(dgpost_ref, dgpost, _first(b, i))
        _acc(dgt_ref, dgt[None], i == 0)

    tok = _tok_specs(tm, D)
    return pl.pallas_call(
        body, name="mix_out_bwd", grid=(nb, s // tm),
        out_shape=[jax.ShapeDtypeStruct((nb, s, D), F32), jax.ShapeDtypeStruct((nb, s, D), BF16),
                   jax.ShapeDtypeStruct((1, D), F32), jax.ShapeDtypeStruct((nb, 1, D), F32)],
        in_specs=[tok, tok, _mod_spec(), _row_spec(), VMEM_FULL],
        out_specs=[tok, tok, _row_spec(), _mod_spec()],
        compiler_params=_cparams(),
    )(dxo, m, gt, g_post, w_mo)


def _mix_mid_bwd(dy, u, v, conv, gn_g, gn_b, wcat, wcat_t, bspt, cn_g, cn_b, go_a, go_b):
    nb, s, _ = dy.shape
    tm = min(512, s)
    nchunk = tm // CH

    def body(dy_ref, u_ref, v_ref, conv_ref, gng_ref, gnb_ref, wcat_ref, wcatt_ref, bspt_ref, cng_ref, cnb_ref,
             goa_ref, gob_ref,
             du_ref, dv_ref, dconv_ref, dwcat_ref, dbsp_ref, dgng_ref, dgnb_ref, dgoa_ref, dgob_ref,
             dcng_ref, dcnb_ref, dcb_ref):
        first = _first(pl.program_id(0), pl.program_id(1))
        dyv = dy_ref[0]
        xhat, rstd = _ln(v_ref[0])
        gng = gng_ref[...]
        vb = (xhat * gng + gnb_ref[...]).astype(BF16)
        wsb = _spatial_weights(wcat_ref, False)
        wsb_t = _spatial_weights(wcatt_ref, True)
        bias = _spatial_bias(bspt_ref)
        masks = _head_mask((CH, WA))
        vexps, zs = [], []
        for cidx in range(nchunk):
            vexp = _expand_heads(vb[cidx * CH:(cidx + 1) * CH, :], masks)
            vexps.append(vexp)
            zs.append(_dot(wsb, vexp) + bias)
        z = jnp.concatenate(zs, axis=0)
        uv = u_ref[0]
        na, ra = _rms(uv * z)
        dya, dgoa = _rms_bwd(dyv[:, 0:WA], na, ra, goa_ref[...])
        du_ref[0] = dya * z
        dz = dya * uv
        dwcat = jnp.zeros((CH, NH * CH), F32)
        dzsum = jnp.zeros((CH, WA), F32)
        dvlns = []
        for cidx in range(nchunk):
            dzc = dz[cidx * CH:(cidx + 1) * CH, :]
            dzsum = dzsum + dzc
            dzb = dzc.astype(BF16)
            dwcat = dwcat + _dot_nt(dzb, vexps[cidx])
            dvexp = _dot(wsb_t, dzb)
            dvl = jnp.zeros((CH, WA), F32)
            for h in range(NH):
                dvl = dvl + jnp.where(masks[h], dvexp[h * CH:(h + 1) * CH, :], 0.0)
            dvlns.append(dvl)
        dvln = jnp.concatenate(dvlns, axis=0)
        dv, dgng, dgnb = _ln_bwd(dvln, xhat, rstd, gng)
        dv_ref[0] = dv
        lane = lax.broadcasted_iota(jnp.int32, (NH, WA), 1)
        head = lax.broadcasted_iota(jnp.int32, (NH, WA), 0)
        sel = jnp.where((lane >= head * HD) & (lane < (head + 1) * HD), 1.0, 0.0).astype(F32)
        dbsp = lax.dot_general(sel, dzsum, NT, preferred_element_type=F32, precision=lax.Precision.HIGHEST)
        chat, crstd = _ln(conv_ref[0])
        cng = cng_ref[...]
        cln = chat * cng + cnb_ref[...]
        sg = _sigmoid(cln)
        nbb, rb = _rms(cln * sg)
        dyb, dgob = _rms_bwd(dyv[:, WA:D], nbb, rb, gob_ref[...])
        dconv, dcng, dcnb = _ln_bwd(dyb * _dsilu(cln, sg), chat, crstd, cng)
        dconv_ref[0] = dconv
        dcb = jnp.sum(dconv, axis=0, keepdims=True)
        for ref, val in ((dwcat_ref, dwcat), (dbsp_ref, dbsp), (dgng_ref, dgng), (dgnb_ref, dgnb), (dgoa_ref, dgoa),
                         (dgob_ref, dgob), (dcng_ref, dcng), (dcnb_ref, dcnb), (dcb_ref, dcb)):
            _acc(ref, val, first)

    t5 = _tok_specs(tm, WA)
    r5 = _row_spec(WA)
    full = lambda shape: pl.BlockSpec(shape, lambda b, i: (0,) * len(shape))
    big = jax.ShapeDtypeStruct((nb, s, WA), F32)
    row = jax.ShapeDtypeStruct((1, WA), F32)
    return pl.pallas_call(
        body, name="mix_mid_bwd", grid=(nb, s // tm),
        out_shape=[big, big, big, jax.ShapeDtypeStruct((CH, NH * CH), F32), jax.ShapeDtypeStruct((NH, CH), F32),
                   row, row, row, row, row, row, row],
        in_specs=[_tok_specs(tm, D), t5, t5, t5, r5, r5, full((CH, NH * CH)), full((NH * CH, CH)), full((CH, WA)),
                  r5, r5, r5, r5],
        out_specs=[t5, t5, t5, full((CH, NH * CH)), full((NH, CH)), r5, r5, r5, r5, r5, r5, r5],
        compiler_params=_cparams(),
    )(dy, u, v, conv, gn_g, gn_b, wcat, wcat_t, bspt, cn_g, cn_b, go_a, go_b)


def _mix_in_bwd(dxo, x, du, dv, dconv, a, g, sh, sc, g_pre, w_mi4, conv_w):
    nb, s, _ = x.shape
    tm = min(512, s)
    n_i = s // tm

    def body(dxo_ref, x_ref, du_ref, dv_ref, dc_ref, dch_ref, a_ref, g_ref, ah_ref, gh_ref, sh_ref, sc_ref,
             gpre_ref, w_ref, cw_ref,
             dx_ref, dproj_ref, h_ref, dgpre_ref, dsh_ref, dsc_ref, dcw_ref, dext_ref, gext_ref):
        b, i = pl.program_id(0), pl.program_id(1)
        first = _first(b, i)
        av, gv = a_ref[0], g_ref[0]
        sg = _sigmoid(gv)
        dconv = dc_ref[0]
        dext_ref[0:tm, :] = dconv
        dext_ref[tm:tm + HALO, :] = dch_ref[0] * jnp.where(i == n_i - 1, 0.0, 1.0).astype(F32)
        gext_ref[0:HALO, :] = (ah_ref[0] * _sigmoid(gh_ref[0])) * jnp.where(i == 0, 0.0, 1.0).astype(F32)
        gext_ref[HALO:HALO + tm, :] = av * sg
        dglu = _conv_taps(dext_ref, cw_ref, tm, lambda k: CK - 1 - k)

        @pl.when(first)
        def _():
            dcw_ref[...] = jnp.zeros((HALO, WB), F32)

        for k in range(CK):
            lo = k + HALO - (CK - 1)
            dcw_ref[k:k + 1, :] += jnp.sum(dconv * gext_ref[lo:lo + tm, :], axis=0, keepdims=True)
        da = dglu * sg
        dg = dglu * av * (sg * (1.0 - sg))
        parts = [du_ref[0].astype(BF16), dv_ref[0].astype(BF16), da.astype(BF16), dg.astype(BF16)]
        dh = jnp.zeros((tm, D), F32)
        for k in range(4):
            dproj_ref[0, :, k * WA:(k + 1) * WA] = parts[k]
            dh = dh + _dot_nt(parts[k], w_ref[k])
        n, r = _rms(x_ref[0])
        gpre = gpre_ref[...]
        ng = n * gpre
        scale1 = 1.0 + sc_ref[0]
        h_ref[0] = (ng * scale1 + sh_ref[0]).astype(BF16)
        dsh = jnp.sum(dh, axis=0, keepdims=True)
        dsc = jnp.sum(dh * ng, axis=0, keepdims=True)
        dxn, dgpre = _rms_bwd(dh * scale1, n, r, gpre)
        dx_ref[0] = dxo_ref[0] + dxn
        _acc(dgpre_ref, dgpre, first)
        _acc(dsh_ref, dsh[None], i == 0)
        _acc(dsc_ref, dsc[None], i == 0)

    tok = _tok_specs(tm, D)
    t5 = _tok_specs(tm, WA)
    full = lambda shape: pl.BlockSpec(shape, lambda b, i: (0,) * len(shape))
    mod_shape = jax.ShapeDtypeStruct((nb, 1, D), F32)
    return pl.pallas_call(
        body, name="mix_in_bwd", grid=(nb, n_i),
        out_shape=[jax.ShapeDtypeStruct((nb, s, D), F32), jax.ShapeDtypeStruct((nb, s, 4 * WA), BF16),
                   jax.ShapeDtypeStruct((nb, s, D), BF16), jax.ShapeDtypeStruct((1, D), F32), mod_shape, mod_shape,
                   jax.ShapeDtypeStruct((HALO, WB), F32)],
        in_specs=[tok, tok, t5, t5, t5, _halo_next_spec(tm, s), t5, t5, _halo_prev_spec(tm), _halo_prev_spec(tm),
                  _mod_spec(), _mod_spec(), _row_spec(), VMEM_FULL, full((HALO, WB))],
        out_specs=[tok, _tok_specs(tm, 4 * WA), tok, _row_spec(), _mod_spec(), _mod_spec(), full((HALO, WB))],
        scratch_shapes=[pltpu.VMEM((tm + HALO, WB), F32), pltpu.VMEM((HALO + tm, WB), F32)],
        compiler_params=_cparams(),
    )(dxo, x, du, dv, dconv, dconv, a, g, a, g, sh, sc, g_pre, w_mi4, conv_w)


def _row_tile(rows, cols):
    best = 8
    for t in range(8, rows + 1, 8):
        if rows % t == 0 and t * cols * 4 <= 1536 * 1024:
            best = t
    return best


def _sum4(name, own, recv):
    rows, cols = own.shape
    tr = _row_tile(rows, cols)

    def body(own_ref, recv_ref, o_ref):
        acc = own_ref[...]
        for k in range(3):
            acc = acc + recv_ref[k].astype(F32)
        o_ref[...] = acc

    return pl.pallas_call(
        body, name=name, grid=(rows // tr,),
        in_specs=[pl.BlockSpec((tr, cols), lambda i: (i, 0)), pl.BlockSpec((3, tr, cols), lambda i: (0, i, 0))],
        out_specs=pl.BlockSpec((tr, cols), lambda i: (i, 0)),
        out_shape=jax.ShapeDtypeStruct((rows, cols), F32),
        compiler_params=_cparams(),
    )(own, recv)


def _adam_big(name, w, m, v, ga, gb):
    rows, cols = w.shape
    tr = _row_tile(rows, cols)

    def body(w_ref, m_ref, v_ref, ga_ref, gb_ref, g_out, d_out, m_out, v_out):
        gsum = ga_ref[...] + gb_ref[...]
        delta, m2, v2 = _adam(w_ref[...], gsum, m_ref[...], v_ref[...])
        g_out[...] = gsum
        d_out[...] = delta
        m_out[...] = m2
        v_out[...] = v2

    spec = pl.BlockSpec((tr, cols), lambda i: (i, 0))
    shape = jax.ShapeDtypeStruct((rows, cols), F32)
    return pl.pallas_call(
        body, name=name, grid=(rows // tr,), out_shape=[shape] * 4,
        in_specs=[spec] * 5, out_specs=[spec] * 4, compiler_params=_cparams(),
    )(w, m, v, ga, gb)


PK_VEC = 0
PK_PAIR = 8
PK_BSP = 16
PK_WCAT = 24
PK_ROWS = PK_WCAT + CH
PAIR_ORDER = ("gmlp_norm_g", "gmlp_norm_b", "conv_b", "conv_norm_g", "conv_norm_b", "g_out_a", "g_out_b")
VEC_ORDER = ("g_pre_f1", "g_post_f1", "g_pre_m", "g_post_m", "g_pre_f2", "g_post_f2")


def _pack_small(vecs, pairs, dbsp, dwcat):
    def body(*refs):
        vec_refs = refs[:6]
        pair_refs = refs[6:13]
        dbsp_ref, dwcat_ref, o_ref = refs[13:]
        o_ref[0:PK_WCAT, :] = jnp.zeros((PK_WCAT, D), F32)
        for k, r in enumerate(vec_refs):
            o_ref[PK_VEC + k:PK_VEC + k + 1, :] = r[...]
        for k, r in enumerate(pair_refs):
            row, half = PK_PAIR + k // 2, k % 2
            o_ref[row:row + 1, half * WA:(half + 1) * WA] = r[...]
        o_ref[PK_BSP:PK_BSP + NH, 0:CH] = dbsp_ref[...]
        o_ref[PK_WCAT:PK_ROWS, :] = dwcat_ref[...]

    args = list(vecs) + list(pairs) + [dbsp, dwcat]
    return pl.pallas_call(
        body, name="pack_small", out_shape=jax.ShapeDtypeStruct((PK_ROWS, D), F32),
        in_specs=[VMEM_FULL] * len(args), out_specs=VMEM_FULL, compiler_params=_cparams(),
    )(*args)


def _small_adam(pack_all, dcw_all, dada_all, params):
    names = list(VEC_ORDER) + list(PAIR_ORDER) + ["b_spatial", "w_spatial", "conv_w", "b_ada"]
    flat = []
    for nm in names:
        flat += list(params[nm])
    n_in = 3 + len(flat)

    def body(*refs):
        pack_ref, dcw_ref, dada_ref = refs[:3]
        prm = refs[3:n_in]
        outs = refs[n_in:]

        def total(r0, nr, c0, nc):
            acc = pack_ref[0, r0:r0 + nr, c0:c0 + nc]
            for d in range(1, NDEV):
                acc = acc + pack_ref[d, r0:r0 + nr, c0:c0 + nc]
            return acc

        def emit(idx, g, getw, put):
            w_ref, m_ref, v_ref = prm[3 * idx:3 * idx + 3]
            delta, m2, v2 = _adam(getw(w_ref), g, getw(m_ref), getw(v_ref))
            for o_ref, val in zip(outs[4 * idx:4 * idx + 4], (g, delta, m2, v2)):
                put(o_ref, val)

        def whole(ref):
            return ref[...]

        def put_whole(ref, val):
            ref[...] = val

        idx = 0
        for k in range(6):
            emit(idx, total(PK_VEC + k, 1, 0, D), whole, put_whole)
            idx += 1
        for k in range(7):
            emit(idx, total(PK_PAIR + k // 2, 1, (k % 2) * WA, WA), whole, put_whole)
            idx += 1
        emit(idx, total(PK_BSP, NH, 0, CH), lambda r: r[0], lambda r, val: r.__setitem__(0, val))
        idx += 1
        row = lax.broadcasted_iota(jnp.int32, (CH, CH), 0)
        col = lax.broadcasted_iota(jnp.int32, (CH, CH), 1)
        for h in range(NH):
            gh = jnp.where(col <= row, total(PK_WCAT, CH, h * CH, CH), 0.0)
            w_ref, m_ref, v_ref = prm[3 * idx:3 * idx + 3]
            delta, m2, v2 = _adam(w_ref[0, h], gh, m_ref[0, h], v_ref[0, h])
            for o_ref, val in zip(outs[4 * idx:4 * idx + 4], (gh, delta, m2, v2)):
                o_ref[0, h] = val
        idx += 1
        gcw = dcw_ref[0, 0:CK, :]
        for d in range(1, NDEV):
            gcw = gcw + dcw_ref[d, 0:CK, :]
        emit(idx, gcw, lambda r: r[0], lambda r, val: r.__setitem__(0, val))
        idx += 1
        emit(idx, jnp.sum(dada_ref[...], axis=0, keepdims=True), whole, put_whole)

    out_shape = []
    for nm in names:
        w = params[nm][0]
        out_shape += [jax.ShapeDtypeStruct(w.shape, F32)] * 4
    res = pl.pallas_call(
        body, name="small_adam", out_shape=out_shape,
        in_specs=[VMEM_FULL] * n_in, out_specs=[VMEM_FULL] * len(out_shape), compiler_params=_cparams(),
    )(pack_all, dcw_all, dada_all, *flat)
    return {nm: tuple(res[4 * k:4 * k + 4]) for k, nm in enumerate(names)}


WEIGHTS = ['w_ada', 'b_ada', 'g_pre_f1', 'g_post_f1', 'w_f1_in', 'w_f1_out', 'g_pre_m', 'g_post_m', 'w_mix_in',
           'gmlp_norm_g', 'gmlp_norm_b', 'w_spatial', 'b_spatial', 'conv_w', 'conv_b', 'conv_norm_g', 'conv_norm_b',
           'g_out_a', 'g_out_b', 'w_mix_out', 'g_pre_f2', 'g_post_f2', 'w_f2_in', 'w_f2_out']
BIG = ('w_f1_in', 'w_f1_out', 'w_mix_in', 'w_mix_out', 'w_f2_in', 'w_f2_out')


def kernel(x, c, w_ada, b_ada, g_pre_f1, g_post_f1, w_f1_in, w_f1_out, g_pre_m, g_post_m, w_mix_in, gmlp_norm_g, gmlp_norm_b, w_spatial, b_spatial, conv_w, conv_b, conv_norm_g, conv_norm_b, g_out_a, g_out_b, w_mix_out, g_pre_f2, g_post_f2, w_f2_in, w_f2_out, loss_target, m_w_ada, m_b_ada, m_g_pre_f1, m_g_post_f1, m_w_f1_in, m_w_f1_out, m_g_pre_m, m_g_post_m, m_w_mix_in, m_gmlp_norm_g, m_gmlp_norm_b, m_w_spatial, m_b_spatial, m_conv_w, m_conv_b, m_conv_norm_g, m_conv_norm_b, m_g_out_a, m_g_out_b, m_w_mix_out, m_g_pre_f2, m_g_post_f2, m_w_f2_in, m_w_f2_out, v_w_ada, v_b_ada, v_g_pre_f1, v_g_post_f1, v_w_f1_in, v_w_f1_out, v_g_pre_m, v_g_post_m, v_w_mix_in, v_gmlp_norm_g, v_gmlp_norm_b, v_w_spatial, v_b_spatial, v_conv_w, v_conv_b, v_conv_norm_g, v_conv_norm_b, v_g_out_a, v_g_out_b, v_w_mix_out, v_g_pre_f2, v_g_post_f2, v_w_f2_in, v_w_f2_out):
    env = dict(locals())
    wts = {n: env[n] for n in WEIGHTS}
    mom = {n: env["m_" + n] for n in WEIGHTS}
    var = {n: env["v_" + n] for n in WEIGHTS}
    nb, s, _ = x.shape
    t = nb * s
    ax, ay, ac = lax.axis_index("x"), lax.axis_index("y"), lax.axis_index("c")
    j_chip = 2 * ax + ay
    dev = 4 * ax + 2 * ay + ac

    (c_all8,) = _allgather8("gather_c", [c.reshape(8, (nb * D) // 8)])
    c_all = c_all8.reshape(NDEV * nb, D)
    b_sh = lax.dynamic_slice(b_ada, (0, j_chip * ADA_SH), (1, ADA_SH))
    ada_sh = _ada_fwd(c_all, w_ada[0], b_sh)
    (ada4,) = _chip_allgather("gather_ada", [ada_sh])
    ada_me = lax.dynamic_slice(ada4, (0, dev * nb, 0), (NCHIP, nb, ADA_SH))
    ada_me = jnp.transpose(ada_me, (1, 0, 2)).reshape(nb, NMOD * D)
    sh1, sc1, gt1, sh2, sc2, gt2, sh3, sc3, gt3 = [ada_me[:, k * D:(k + 1) * D].reshape(nb, 1, D) for k in range(NMOD)]

    gathered = _chip_allgather("gather_w", [wts[n][0].astype(BF16) for n in BIG] + [conv_w[0]])
    w1i, w1o, wmi, wmo, w2i, w2o, cw4 = gathered
    w1o = w1o.reshape(DFF, D)
    w2o = w2o.reshape(DFF, D)
    wmo = wmo.reshape(D, D)
    cw_full = jnp.transpose(cw4, (1, 0, 2)).reshape(CK, WB)
    cw_pad = jnp.pad(cw_full, ((0, HALO - CK), (0, 0)))
    wcat = jnp.transpose(w_spatial[0], (1, 0, 2)).reshape(CH, NH * CH)
    wcat_t = jnp.transpose(w_spatial[0], (0, 2, 1)).reshape(NH * CH, CH)
    bspt = jnp.repeat(b_spatial[0].T, HD, axis=1)

    x1, f1, p1 = _ffn_fwd(x, sh1, sc1, gt1, g_pre_f1, g_post_f1, w1i, w1o)
    u, v, a, g = _mix_in_fwd(x1, sh2, sc2, g_pre_m, wmi)
    x2, conv, yb, m = _mix_mid_fwd(x1, u, v, a, g, gt2, gmlp_norm_g, gmlp_norm_b, wcat, bspt, cw_pad, conv_b,
                                   conv_norm_g, conv_norm_b, g_out_a, g_out_b, wmo, g_post_m)
    dx3, f2, p2, lsum = _ffn_fwd(x2, sh3, sc3, gt3, g_pre_f2, g_post_f2, w2i, w2o, target=loss_target)
    loss = lax.psum(0.5 * jnp.sum(lsum) / D, ("x", "y", "c"))

    dx2, dp2, h3, a2, df2, dg_pre_f2, dg_post_f2, dsh3, dsc3, dgt3 = _ffn_bwd(
        dx3, x2, f2, p2, sh3, sc3, gt3, g_pre_f2, g_post_f2, w2i, w2o)
    gw2i = _wgrad("wgrad_f2_in", h3.reshape(t, D), dp2.reshape(t, 2 * DFF), 2 * DFF // NCHIP, True)
    gw2o = _wgrad("wgrad_f2_out", a2.reshape(t, DFF), df2.reshape(t, D), D // 2, False)
    dy, dm, dg_post_m, dgt2 = _mix_out_bwd(dx2, m, gt2, g_post_m, wmo)
    gwmo = _wgrad("wgrad_mix_out", yb.reshape(t, D), dm.reshape(t, D), D // 2, False)
    (du, dv, dconv, dwcat, dbsp, dgn_g, dgn_b, dgo_a, dgo_b, dcn_g, dcn_b, dcb) = _mix_mid_bwd(
        dy, u, v, conv, gmlp_norm_g, gmlp_norm_b, wcat, wcat_t, bspt, conv_norm_g, conv_norm_b, g_out_a, g_out_b)
    dx1, dproj, h2, dg_pre_m, dsh2, dsc2, dcw = _mix_in_bwd(dx2, x1, du, dv, dconv, a, g, sh2, sc2, g_pre_m, wmi, cw_pad)
    gwmi = _wgrad("wgrad_mix_in", h2.reshape(t, D), dproj.reshape(t, 4 * WA), WA, True)
    grad_x, dp1, h1, a1, df1, dg_pre_f1, dg_post_f1, dsh1, dsc1, dgt1 = _ffn_bwd(
        dx1, x, f1, p1, sh1, sc1, gt1, g_pre_f1, g_post_f1, w1i, w1o)
    gw1i = _wgrad("wgrad_f1_in", h1.reshape(t, D), dp1.reshape(t, 2 * DFF), 2 * DFF // NCHIP, True)
    gw1o = _wgrad("wgrad_f1_out", a1.reshape(t, DFF), df1.reshape(t, D), D // 2, False)

    def chip4(pair, rows):
        return [arr.reshape(NCHIP, rows, arr.shape[-1]) for arr in pair]

    full = {"w_f1_in": gw1i, "w_f1_out": chip4(gw1o, DFF // NCHIP), "w_mix_in": gwmi, "w_mix_out": chip4(gwmo, D // NCHIP),
            "w_f2_in": gw2i, "w_f2_out": chip4(gw2o, DFF // NCHIP)}
    recv = _chip_scatter("scatter_grads", [full[n][1] for n in BIG])
    part = [_sum4("sum4_" + n, lax.dynamic_index_in_dim(full[n][0], j_chip, 0, keepdims=False), recv[k])
            for k, n in enumerate(BIG)]
    other = _sibling_swap("swap_grads", part)

    out = {}
    for k, n in enumerate(BIG):
        out[n] = tuple(r[None] for r in _adam_big("adam_" + n, wts[n][0], mom[n][0], var[n][0], part[k], other[k]))

    dada = jnp.concatenate([q.reshape(nb, D) for q in (dsh1, dsc1, dgt1, dsh2, dsc2, dgt2, dsh3, dsc3, dgt3)], axis=1)
    vec_grads = dict(g_pre_f1=dg_pre_f1, g_post_f1=dg_post_f1, g_pre_m=dg_pre_m, g_post_m=dg_post_m,
                     g_pre_f2=dg_pre_f2, g_post_f2=dg_post_f2)
    pair_grads = dict(gmlp_norm_g=dgn_g, gmlp_norm_b=dgn_b, conv_b=dcb, conv_norm_g=dcn_g, conv_norm_b=dcn_b,
                      g_out_a=dgo_a, g_out_b=dgo_b)
    pack = _pack_small([vec_grads[n] for n in VEC_ORDER], [pair_grads[n] for n in PAIR_ORDER], dbsp, dwcat)
    pack_all, dcw_all, dada_all8 = _allgather8("gather_small", [pack, dcw, dada.reshape(8, (nb * NMOD * D) // 8)])
    dada_all = dada_all8.reshape(NDEV * nb, NMOD * D)
    dcw_mine = lax.dynamic_slice(dcw_all, (0, 0, j_chip * (WB // NCHIP)), (NDEV, HALO, WB // NCHIP))
    small = {n: (wts[n], mom[n], var[n]) for n in list(VEC_ORDER) + list(PAIR_ORDER) + ["b_spatial", "w_spatial", "conv_w", "b_ada"]}
    out.update(_small_adam(pack_all, dcw_mine, dada_all, small))
    dada_sh = lax.dynamic_slice(dada_all, (0, j_chip * ADA_SH), (NDEV * nb, ADA_SH))
    out["w_ada"] = tuple(r[None] for r in _ada_bwd_adam(c_all, dada_sh, w_ada[0], m_w_ada[0], v_w_ada[0]))

    res = [loss, grad_x]
    for k in range(4):
        res += [out[n][k] for n in WEIGHTS]
    return tuple(res)
```

```python
import functools

import jax
import jax.numpy as jnp
from jax import lax
from jax.experimental import pallas as pl
from jax.experimental.pallas import tpu as pltpu

D = 1024
DFF = 2816
WA = 512
WB = 512
NH = 8
HD = 64
CH = 128
CK = 31
HALO = 32
NMOD = 9
EPS = 1e-6
NCHIP = 4
NDEV = 8
FBLK = DFF // 2
ADA_SH = NMOD * D // NCHIP

LR, B1, B2, EPS_A, WD, STEP = 0.001, 0.9, 0.999, 1e-08, 0.01, 10

F32 = jnp.float32
BF16 = jnp.bfloat16
MESH = pl.DeviceIdType.MESH
ANY = pl.BlockSpec(memory_space=pl.ANY)
VMEM_FULL = pl.BlockSpec(memory_space=pltpu.VMEM)
VMEM_LIMIT = 56 * 1024 * 1024

NT = (((1,), (1,)), ((), ()))
TN = (((0,), (0,)), ((), ()))


def _dot(a, b):
    return jnp.dot(a, b, preferred_element_type=F32)


def _dot_nt(a, b):
    return lax.dot_general(a, b, NT, preferred_element_type=F32)


def _dot_tn(a, b):
    return lax.dot_general(a, b, TN, preferred_element_type=F32)


def _cparams():
    return pltpu.CompilerParams(vmem_limit_bytes=VMEM_LIMIT)


def _allgather8(name, arrs):
    n = len(arrs)

    def body(*refs):
        ins, outs = refs[:n], refs[n:2 * n]
        send_sems, recv_sems, local_sems = refs[2 * n:]
        x, y, c = lax.axis_index("x"), lax.axis_index("y"), lax.axis_index("c")
        me, sibling = (x, y, c), (x, y, 1 - c)
        chips = [(1 - x, y), (x, 1 - y), (1 - x, 1 - y)]

        def copy(a, k, block, to, src=None):
            rows = outs[a].at[4 * block[0] + 2 * block[1] + block[2]]
            return pltpu.make_async_remote_copy(
                src_ref=rows if src is None else src, dst_ref=rows,
                send_sem=send_sems.at[a, k], recv_sem=recv_sems.at[a, k],
                device_id=to, device_id_type=MESH)

        started, mine = [], []
        for a in range(n):
            loc = pltpu.make_async_copy(ins[a], outs[a].at[4 * x + 2 * y + c], local_sems.at[a])
            loc.start()
            mine.append(loc)
            first = [copy(a, 0, me, sibling, src=ins[a])]
            first += [copy(a, 1 + j, me, (*chip, c), src=ins[a]) for j, chip in enumerate(chips)]
            for cp in first:
                cp.start()
            started += first
        for a in range(n):
            for j, chip in enumerate(chips):
                copy(a, 1 + j, (*chip, c), me).wait_recv()
                fwd = copy(a, 4 + j, (*chip, c), sibling)
                fwd.start()
                started.append(fwd)
        for a in range(n):
            copy(a, 0, sibling, me).wait_recv()
            for j, chip in enumerate(chips):
                copy(a, 4 + j, (*chip, 1 - c), me).wait_recv()
        for cp in started:
            cp.wait_send()
        for loc in mine:
            loc.wait()

    return pl.pallas_call(
        body, name=name,
        out_shape=[jax.ShapeDtypeStruct((NDEV,) + a.shape, a.dtype) for a in arrs],
        in_specs=[ANY] * n, out_specs=[ANY] * n,
        scratch_shapes=[pltpu.SemaphoreType.DMA((n, 7)), pltpu.SemaphoreType.DMA((n, 7)),
                        pltpu.SemaphoreType.DMA((n,))],
    )(*arrs)


def _chip_relations(x, y):
    return [(1 - x, y), (x, 1 - y), (1 - x, 1 - y)]


def _exchange(name, arrs, out_shapes, plan):
    n = len(arrs)
    n_out = len(out_shapes)

    def body(*refs):
        ins, outs = refs[:n], refs[n:n + n_out]
        send_sems, recv_sems, local_sems = refs[n + n_out:]
        x, y, c = lax.axis_index("x"), lax.axis_index("y"), lax.axis_index("c")
        local, sends = plan(x, y, c, ins, outs)
        locs = [pltpu.make_async_copy(s, d, local_sems.at[i]) for i, (s, d) in enumerate(local)]
        for loc in locs:
            loc.start()
        cps = [pltpu.make_async_remote_copy(src_ref=s, dst_ref=d, send_sem=send_sems.at[i], recv_sem=recv_sems.at[i],
                                            device_id=peer, device_id_type=MESH)
               for i, (s, d, peer, _) in enumerate(sends)]
        for cp in cps:
            cp.start()
        for i, (s, _, peer, landing) in enumerate(sends):
            pltpu.make_async_remote_copy(src_ref=s, dst_ref=landing, send_sem=send_sems.at[i], recv_sem=recv_sems.at[i],
                                         device_id=peer, device_id_type=MESH).wait_recv()
        for cp in cps:
            cp.wait_send()
        for loc in locs:
            loc.wait()

    return n, n_out, body


def _run_exchange(name, arrs, out_shapes, plan, n_local, n_send):
    n, n_out, body = _exchange(name, arrs, out_shapes, plan)
    return pl.pallas_call(
        body, name=name, out_shape=out_shapes,
        in_specs=[ANY] * n, out_specs=[ANY] * n_out,
        scratch_shapes=[pltpu.SemaphoreType.DMA((n_send,)), pltpu.SemaphoreType.DMA((n_send,)),
                        pltpu.SemaphoreType.DMA((max(n_local, 1),))],
    )(*arrs)


def _chip_allgather(name, arrs):
    n = len(arrs)

    def plan(x, y, c, ins, outs):
        j_me = 2 * x + y
        local = [(ins[a], outs[a].at[j_me]) for a in range(n)]
        sends = []
        for a in range(n):
            for (px, py) in _chip_relations(x, y):
                sends.append((ins[a], outs[a].at[j_me], (px, py, c), outs[a].at[2 * px + py]))
        return local, sends

    shapes = [jax.ShapeDtypeStruct((NCHIP,) + a.shape, a.dtype) for a in arrs]
    return _run_exchange(name, arrs, shapes, plan, n, 3 * n)


def _chip_scatter(name, arrs):
    n = len(arrs)

    def plan(x, y, c, ins, outs):
        sends = []
        for a in range(n):
            for k, (px, py) in enumerate(_chip_relations(x, y)):
                sends.append((ins[a].at[2 * px + py], outs[a].at[k], (px, py, c), outs[a].at[k]))
        return [], sends

    shapes = [jax.ShapeDtypeStruct((3,) + a.shape[1:], a.dtype) for a in arrs]
    return _run_exchange(name, arrs, shapes, plan, 0, 3 * n)


def _sibling_swap(name, arrs):
    n = len(arrs)

    def plan(x, y, c, ins, outs):
        return [], [(ins[a], outs[a], (x, y, 1 - c), outs[a]) for a in range(n)]

    shapes = [jax.ShapeDtypeStruct(a.shape, a.dtype) for a in arrs]
    return _run_exchange(name, arrs, shapes, plan, 0, n)


HBM = pl.BlockSpec(memory_space=pltpu.HBM)
SEM = pl.BlockSpec(memory_space=pltpu.SEMAPHORE)
EFFECT = pltpu.SideEffectType.DATAFLOW_SIDE_EFFECTING


def _split_start(name, srcs, lands, plan, n_send, after):
    n, nl = len(srcs), len(lands)

    def body(*refs):
        src, land = refs[:n], refs[n:n + nl]
        send_sems, recv_sems = refs[n + nl + 1], refs[n + nl + 2]
        token = refs[-2]
        local_sems = refs[-1]
        x, y, c = lax.axis_index("x"), lax.axis_index("y"), lax.axis_index("c")
        local, sends = plan(x, y, c, src, land)
        locs = [pltpu.make_async_copy(s, d, local_sems.at[i]) for i, (s, d) in enumerate(local)]
        for loc in locs:
            loc.start()
        for i, (s, d, peer, _) in enumerate(sends):
            pltpu.make_async_remote_copy(src_ref=s, dst_ref=d, send_sem=send_sems.at[i], recv_sem=recv_sems.at[i],
                                         device_id=peer, device_id_type=MESH).start()
        for loc in locs:
            loc.wait()
        token[...] = jnp.zeros_like(token)

    thru = [pltpu.HBM(a.shape, a.dtype) for a in list(srcs) + list(lands)]
    res = pl.pallas_call(
        body, name=name,
        out_shape=(pltpu.SemaphoreType.DMA((n_send,)), pltpu.SemaphoreType.DMA((n_send,)), *thru,
                   jax.ShapeDtypeStruct((8, 128), F32)),
        in_specs=[HBM] * (n + nl) + [ANY],
        out_specs=(SEM, SEM, *([HBM] * (n + nl)), pl.BlockSpec(memory_space=pltpu.VMEM)),
        input_output_aliases={i: 2 + i for i in range(n + nl)},
        scratch_shapes=[pltpu.SemaphoreType.DMA((max(len(srcs), 1),))],
        compiler_params=pltpu.CompilerParams(has_side_effects=EFFECT),
    )(*[pltpu.with_memory_space_constraint(a, pltpu.HBM) for a in list(srcs) + list(lands)], after)
    return res[0], res[1], list(res[2:2 + n]), list(res[2 + n:2 + n + nl]), res[-1]


def _split_wait(name, srcs, lands, send_sems, recv_sems, plan, after):
    n, nl = len(srcs), len(lands)

    def body(*refs):
        src, land = refs[:n], refs[n:n + nl]
        send_sems, recv_sems = refs[n + nl], refs[n + nl + 1]
        x, y, c = lax.axis_index("x"), lax.axis_index("y"), lax.axis_index("c")
        _, sends = plan(x, y, c, src, land)
        for i, (s, _, peer, landing) in enumerate(sends):
            cp = pltpu.make_async_remote_copy(src_ref=s, dst_ref=landing, send_sem=send_sems.at[i],
                                              recv_sem=recv_sems.at[i], device_id=peer, device_id_type=MESH)
            cp.wait_send()
            cp.wait_recv()

    thru = [pltpu.HBM(a.shape, a.dtype) for a in list(srcs) + list(lands)]
    res = pl.pallas_call(
        body, name=name, out_shape=tuple(thru),
        in_specs=[HBM] * (n + nl) + [SEM, SEM, ANY], out_specs=tuple([HBM] * (n + nl)),
        input_output_aliases={i: i for i in range(n + nl)},
        compiler_params=pltpu.CompilerParams(has_side_effects=EFFECT),
    )(*srcs, *lands, send_sems, recv_sems, after)
    return list(res[n:])


def _gather_plan(n):
    def plan(x, y, c, src, land):
        j_me = 2 * x + y
        local = [(src[a], land[a].at[j_me]) for a in range(n)]
        sends = []
        for a in range(n):
            for (px, py) in _chip_relations(x, y):
                sends.append((src[a], land[a].at[j_me], (px, py, c), land[a].at[2 * px + py]))
        return local, sends

    return plan


def _scatter_plan(n):
    def plan(x, y, c, src, land):
        sends = []
        for a in range(n):
            for k, (px, py) in enumerate(_chip_relations(x, y)):
                sends.append((src[a].at[2 * px + py], land[a].at[k], (px, py, c), land[a].at[k]))
        return [], sends

    return plan


def _rms(x):
    r = lax.rsqrt(jnp.mean(x * x, axis=-1, keepdims=True) + EPS)
    return x * r, r


def _rms_bwd(dy, n, r, g):
    dg = jnp.sum(dy * n, axis=0, keepdims=True)
    dn = dy * g
    dx = r * (dn - n * jnp.mean(dn * n, axis=-1, keepdims=True))
    return dx, dg


def _ln(x):
    mu = jnp.mean(x, axis=-1, keepdims=True)
    xc = x - mu
    rstd = lax.rsqrt(jnp.mean(xc * xc, axis=-1, keepdims=True) + EPS)
    return xc * rstd, rstd


def _ln_bwd(dy, xhat, rstd, g):
    dg = jnp.sum(dy * xhat, axis=0, keepdims=True)
    db = jnp.sum(dy, axis=0, keepdims=True)
    dxh = dy * g
    dx = rstd * (dxh - jnp.mean(dxh, axis=-1, keepdims=True) - xhat * jnp.mean(dxh * xhat, axis=-1, keepdims=True))
    return dx, dg, db


def _sigmoid(x):
    return jax.nn.sigmoid(x)


def _dsilu(x, s):
    return s * (1.0 + x * (1.0 - s))


def _adam(w, g, m, v):
    m = B1 * m + (1.0 - B1) * g
    v = B2 * v + (1.0 - B2) * (g * g)
    m_hat = m / (1.0 - B1 ** STEP)
    v_hat = v / (1.0 - B2 ** STEP)
    delta = -LR * (m_hat / (jnp.sqrt(v_hat) + EPS_A) + WD * w)
    return delta, m, v


def _head_mask(shape):
    lane = lax.broadcasted_iota(jnp.int32, shape, len(shape) - 1)
    return [(lane >= h * HD) & (lane < (h + 1) * HD) for h in range(NH)]


def _first(b, i):
    return jnp.logical_and(b == 0, i == 0)


def _acc(ref, val, first):
    @pl.when(first)
    def _():
        ref[...] = val

    @pl.when(jnp.logical_not(first))
    def _():
        ref[...] += val


def _ada_fwd(c_all, w_sh, b_sh):
    nb = c_all.shape[0]
    tn = 768

    def body(c_ref, w_ref, b_ref, o_ref):
        cv = c_ref[...]
        cs = (cv * _sigmoid(cv)).astype(BF16)
        o_ref[...] = _dot(cs, w_ref[...].astype(BF16)) + b_ref[...]

    return pl.pallas_call(
        body, name="ada_fwd", grid=(ADA_SH // tn,),
        out_shape=jax.ShapeDtypeStruct((nb, ADA_SH), F32),
        in_specs=[pl.BlockSpec((nb, D), lambda j: (0, 0)), pl.BlockSpec((D, tn), lambda j: (0, j)),
                  pl.BlockSpec((1, tn), lambda j: (0, j))],
        out_specs=pl.BlockSpec((nb, tn), lambda j: (0, j)),
        compiler_params=_cparams(),
    )(c_all, w_sh, b_sh)


def _ada_bwd_adam(c_all, dada_sh, w, m, v):
    nb = c_all.shape[0]
    tn = 768

    def body(c_ref, d_ref, w_ref, m_ref, v_ref, g_out, d_out, m_out, v_out):
        cv = c_ref[...]
        cs = (cv * _sigmoid(cv)).astype(BF16)
        g = _dot_tn(cs, d_ref[...].astype(BF16))
        delta, m2, v2 = _adam(w_ref[...], g, m_ref[...], v_ref[...])
        g_out[...] = g
        d_out[...] = delta
        m_out[...] = m2
        v_out[...] = v2

    big = pl.BlockSpec((D, tn), lambda j: (0, j))
    shape = jax.ShapeDtypeStruct((D, ADA_SH), F32)
    return pl.pallas_call(
        body, name="ada_bwd_adam", grid=(ADA_SH // tn,),
        out_shape=[shape] * 4,
        in_specs=[pl.BlockSpec((nb, D), lambda j: (0, 0)), pl.BlockSpec((nb, tn), lambda j: (0, j)), big, big, big],
        out_specs=[big] * 4,
        compiler_params=_cparams(),
    )(c_all, dada_sh, w, m, v)


def _tok_specs(tm, width):
    return pl.BlockSpec((1, tm, width), lambda b, i: (b, i, 0))


def _mod_spec():
    return pl.BlockSpec((1, 1, D), lambda b, i: (b, 0, 0))


def _row_spec(width=D):
    return pl.BlockSpec((1, width), lambda b, i: (0, 0))


def _ffn_fwd(x, sh, sc, gt, g_pre, g_post, w_in4, w_out, target=None):
    nb, s, _ = x.shape
    tm = min(256, s)
    with_loss = target is not None

    def body(*refs):
        if with_loss:
            (x_ref, sh_ref, sc_ref, gt_ref, gpre_ref, gpost_ref, win_ref, wout_ref, tgt_ref,
             xo_ref, f_ref, p_ref, ls_ref) = refs
        else:
            (x_ref, sh_ref, sc_ref, gt_ref, gpre_ref, gpost_ref, win_ref, wout_ref,
             xo_ref, f_ref, p_ref) = refs
        xv = x_ref[0]
        n, _ = _rms(xv)
        h = (n * gpre_ref[...]) * (1.0 + sc_ref[0]) + sh_ref[0]
        hb = h.astype(BF16)
        acc = jnp.zeros((tm, D), F32)
        for j in range(2):
            gate = _dot(hb, win_ref[j])
            up = _dot(hb, win_ref[2 + j])
            p_ref[0, :, j * FBLK:(j + 1) * FBLK] = gate.astype(BF16)
            p_ref[0, :, DFF + j * FBLK:DFF + (j + 1) * FBLK] = up.astype(BF16)
            a = (gate * _sigmoid(gate)) * up
            acc = acc + _dot(a.astype(BF16), wout_ref[j * FBLK:(j + 1) * FBLK, :])
        f_ref[0] = acc
        nf, _ = _rms(acc)
        out = xv + (0.5 * gt_ref[0]) * (nf * gpost_ref[...])
        if with_loss:
            err = out - tgt_ref[0]
            xo_ref[0] = err * (1.0 / D)
            row = jnp.sum(err * err, axis=0, keepdims=True)
            part = row[:, 0:128]
            for k in range(1, D // 128):
                part = part + row[:, k * 128:(k + 1) * 128]
            _acc(ls_ref, part, _first(pl.program_id(0), pl.program_id(1)))
        else:
            xo_ref[0] = out

    in_specs = [_tok_specs(tm, D), _mod_spec(), _mod_spec(), _mod_spec(), _row_spec(), _row_spec(), VMEM_FULL, VMEM_FULL]
    args = [x, sh, sc, gt, g_pre, g_post, w_in4, w_out]
    out_shape = [jax.ShapeDtypeStruct((nb, s, D), F32), jax.ShapeDtypeStruct((nb, s, D), F32),
                 jax.ShapeDtypeStruct((nb, s, 2 * DFF), BF16)]
    out_specs = [_tok_specs(tm, D), _tok_specs(tm, D), _tok_specs(tm, 2 * DFF)]
    if with_loss:
        in_specs.append(_tok_specs(tm, D))
        args.append(target)
        out_shape.append(jax.ShapeDtypeStruct((1, 128), F32))
        out_specs.append(pl.BlockSpec((1, 128), lambda b, i: (0, 0)))
    return pl.pallas_call(
        body, name="ffn_loss_fwd" if with_loss else "ffn_fwd", grid=(nb, s // tm),
        out_shape=out_shape, in_specs=in_specs, out_specs=out_specs,
        compiler_params=_cparams(),
    )(*args)


def _ffn_bwd(dxo, x, f, p, sh, sc, gt, g_pre, g_post, w_in4, w_out):
    nb, s, _ = x.shape
    tm = min(256, s)

    def body(dxo_ref, x_ref, f_ref, p_ref, sh_ref, sc_ref, gt_ref, gpre_ref, gpost_ref, win_ref, wout_ref,
             dx_ref, dp_ref, h_ref, a_ref, df_ref, dgpre_ref, dgpost_ref, dsh_ref, dsc_ref, dgt_ref):
        b, i = pl.program_id(0), pl.program_id(1)
        dxo_v = dxo_ref[0]
        nf, q = _rms(f_ref[0])
        gpost = gpost_ref[...]
        dgt = jnp.sum(dxo_v * (0.5 * (nf * gpost)), axis=0, keepdims=True)
        do = dxo_v * (0.5 * gt_ref[0])
        df, dgpost = _rms_bwd(do, nf, q, gpost)
        dfb = df.astype(BF16)
        df_ref[0] = dfb
        xv = x_ref[0]
        n, r = _rms(xv)
        gpre = gpre_ref[...]
        ng = n * gpre
        scale1 = 1.0 + sc_ref[0]
        h = ng * scale1 + sh_ref[0]
        h_ref[0] = h.astype(BF16)
        dh = jnp.zeros((tm, D), F32)
        for j in range(2):
            gate = p_ref[0, :, j * FBLK:(j + 1) * FBLK].astype(F32)
            up = p_ref[0, :, DFF + j * FBLK:DFF + (j + 1) * FBLK].astype(F32)
            sg = _sigmoid(gate)
            act = gate * sg
            a_ref[0, :, j * FBLK:(j + 1) * FBLK] = (act * up).astype(BF16)
            da = _dot_nt(dfb, wout_ref[j * FBLK:(j + 1) * FBLK, :])
            dgate = (da * up * _dsilu(gate, sg)).astype(BF16)
            dup = (da * act).astype(BF16)
            dp_ref[0, :, j * FBLK:(j + 1) * FBLK] = dgate
            dp_ref[0, :, DFF + j * FBLK:DFF + (j + 1) * FBLK] = dup
            dh = dh + _dot_nt(dgate, win_ref[j]) + _dot_nt(dup, win_ref[2 + j])
        dsh = jnp.sum(dh, axis=0, keepdims=True)
        dsc = jnp.sum(dh * ng, axis=0, keepdims=True)
        dxn, dgpre = _rms_bwd(dh * scale1, n, r, gpre)
        dx_ref[0] = dxo_v + dxn
        _acc(dgpre_ref, dgpre, _first(b, i))
        _acc(dgpost_ref, dgpost, _first(b, i))
        _acc(dsh_ref, dsh[None], i == 0)
        _acc(dsc_ref, dsc[None], i == 0)
        _acc(dgt_ref, dgt[None], i == 0)

    tok = _tok_specs(tm, D)
    mod_shape = jax.ShapeDtypeStruct((nb, 1, D), F32)
    row_shape = jax.ShapeDtypeStruct((1, D), F32)
    return pl.pallas_call(
        body, name="ffn_bwd", grid=(nb, s // tm),
        out_shape=[jax.ShapeDtypeStruct((nb, s, D), F32), jax.ShapeDtypeStruct((nb, s, 2 * DFF), BF16),
                   jax.ShapeDtypeStruct((nb, s, D), BF16), jax.ShapeDtypeStruct((nb, s, DFF), BF16),
                   jax.ShapeDtypeStruct((nb, s, D), BF16), row_shape, row_shape, mod_shape, mod_shape, mod_shape],
        in_specs=[tok, tok, tok, _tok_specs(tm, 2 * DFF), _mod_spec(), _mod_spec(), _mod_spec(), _row_spec(), _row_spec(),
                  VMEM_FULL, VMEM_FULL],
        out_specs=[tok, _tok_specs(tm, 2 * DFF), tok, _tok_specs(tm, DFF), tok, _row_spec(), _row_spec(),
                   _mod_spec(), _mod_spec(), _mod_spec()],
        compiler_params=_cparams(),
    )(dxo, x, f, p, sh, sc, gt, g_pre, g_post, w_in4, w_out)


def _wgrad(name, a, b, col_block, chip_major):
    t, ka = a.shape
    n = b.shape[1]
    tk = min(512, t)
    nk = t // tk
    nblk = n // col_block

    def body(a_ref, b_ref, o_ref, obf_ref, acc_ref):
        k = pl.program_id(1)

        @pl.when(k == 0)
        def _():
            acc_ref[...] = jnp.zeros_like(acc_ref)

        acc_ref[...] += _dot_tn(a_ref[...], b_ref[...])

        @pl.when(k == nk - 1)
        def _():
            val = acc_ref[...]
            if chip_major:
                o_ref[0] = val
                obf_ref[0] = val.astype(BF16)
            else:
                o_ref[...] = val
                obf_ref[...] = val.astype(BF16)

    if chip_major:
        shape = (nblk, ka, col_block)
        ospec = pl.BlockSpec((1, ka, col_block), lambda j, k: (j, 0, 0))
    else:
        shape = (ka, n)
        ospec = pl.BlockSpec((ka, col_block), lambda j, k: (0, j))
    return pl.pallas_call(
        body, name=name, grid=(nblk, nk),
        out_shape=[jax.ShapeDtypeStruct(shape, F32), jax.ShapeDtypeStruct(shape, BF16)],
        in_specs=[pl.BlockSpec((tk, ka), lambda j, k: (k, 0)), pl.BlockSpec((tk, col_block), lambda j, k: (k, j))],
        out_specs=[ospec, ospec],
        scratch_shapes=[pltpu.VMEM((ka, col_block), F32)],
        compiler_params=_cparams(),
    )(a, b)


def _mix_in_fwd(x, sh, sc, g_pre, w_mi4):
    nb, s, _ = x.shape
    tm = min(512, s)

    def body(x_ref, sh_ref, sc_ref, gpre_ref, w_ref, u_ref, v_ref, a_ref, g_ref):
        n, _ = _rms(x_ref[0])
        hb = ((n * gpre_ref[...]) * (1.0 + sc_ref[0]) + sh_ref[0]).astype(BF16)
        for k, o_ref in enumerate((u_ref, v_ref, a_ref, g_ref)):
            o_ref[0] = _dot(hb, w_ref[k])

    shape = jax.ShapeDtypeStruct((nb, s, WA), F32)
    return pl.pallas_call(
        body, name="mix_in_fwd", grid=(nb, s // tm),
        out_shape=[shape] * 4,
        in_specs=[_tok_specs(tm, D), _mod_spec(), _mod_spec(), _row_spec(), VMEM_FULL],
        out_specs=[_tok_specs(tm, WA)] * 4,
        compiler_params=_cparams(),
    )(x, sh, sc, g_pre, w_mi4)


def _spatial_weights(wcat_ref, transposed):
    w = wcat_ref[...]
    row = lax.broadcasted_iota(jnp.int32, w.shape, 0)
    col = lax.broadcasted_iota(jnp.int32, w.shape, 1)
    keep = ((row & (CH - 1)) <= col) if transposed else ((col & (CH - 1)) <= row)
    return jnp.where(keep, w, 0.0).astype(BF16)


def _expand_heads(vc, masks):
    return jnp.concatenate([jnp.where(mk, vc, jnp.zeros_like(vc)) for mk in masks], axis=0)


def _spatial_bias(bspt_ref):
    return bspt_ref[...]


def _conv_taps(ext_ref, w_ref, tm, offset):
    acc = jnp.zeros((tm, WB), F32)
    for k in range(CK):
        acc = acc + w_ref[k:k + 1, :] * ext_ref[offset(k):offset(k) + tm, :]
    return acc


def _halo_prev_spec(tm):
    return pl.BlockSpec((1, HALO, WB), lambda b, i: (b, jnp.maximum(i * (tm // HALO) - 1, 0), 0))


def _halo_next_spec(tm, s):
    return pl.BlockSpec((1, HALO, WB), lambda b, i: (b, jnp.minimum((i + 1) * (tm // HALO), s // HALO - 1), 0))


def _mix_mid_fwd(x, u, v, a, g, gt, gn_g, gn_b, wcat, bspt, conv_w, conv_b, cn_g, cn_b, go_a, go_b, w_mo, g_post):
    nb, s, _ = x.shape
    tm = min(512, s)

    def body(x_ref, u_ref, v_ref, a_ref, g_ref, ah_ref, gh_ref, gt_ref, gng_ref, gnb_ref, wcat_ref, bspt_ref,
             cw_ref, cb_ref, cng_ref, cnb_ref, goa_ref, gob_ref, wmo_ref, gpost_ref,
             xo_ref, conv_ref, y_ref, m_ref, ext_ref):
        i = pl.program_id(1)
        xhat, _ = _ln(v_ref[0])
        vb = (xhat * gng_ref[...] + gnb_ref[...]).astype(BF16)
        wsb = _spatial_weights(wcat_ref, False)
        bias = _spatial_bias(bspt_ref)
        masks = _head_mask((CH, WA))
        zs = []
        for cidx in range(tm // CH):
            vexp = _expand_heads(vb[cidx * CH:(cidx + 1) * CH, :], masks)
            zs.append(_dot(wsb, vexp) + bias)
        z = jnp.concatenate(zs, axis=0)
        na, _ = _rms(u_ref[0] * z)
        keep = jnp.where(i == 0, 0.0, 1.0).astype(F32)
        ext_ref[0:HALO, :] = (ah_ref[0] * _sigmoid(gh_ref[0])) * keep
        ext_ref[HALO:HALO + tm, :] = a_ref[0] * _sigmoid(g_ref[0])
        conv = _conv_taps(ext_ref, cw_ref, tm, lambda k: k + HALO - (CK - 1)) + cb_ref[...]
        conv_ref[0] = conv
        chat, _ = _ln(conv)
        cln = chat * cng_ref[...] + cnb_ref[...]
        nbb, _ = _rms(cln * _sigmoid(cln))
        yb = jnp.concatenate([na * goa_ref[...], nbb * gob_ref[...]], axis=1).astype(BF16)
        y_ref[0] = yb
        m = _dot(yb, wmo_ref[...])
        m_ref[0] = m
        nm, _ = _rms(m)
        xo_ref[0] = x_ref[0] + gt_ref[0] * (nm * gpost_ref[...])

    t5 = _tok_specs(tm, WA)
    tok = _tok_specs(tm, D)
    r5 = _row_spec(WA)
    full = lambda shape: pl.BlockSpec(shape, lambda b, i: (0,) * len(shape))
    return pl.pallas_call(
        body, name="mix_mid_fwd", grid=(nb, s // tm),
        out_shape=[jax.ShapeDtypeStruct((nb, s, D), F32), jax.ShapeDtypeStruct((nb, s, WB), F32),
                   jax.ShapeDtypeStruct((nb, s, D), BF16), jax.ShapeDtypeStruct((nb, s, D), F32)],
        in_specs=[tok, t5, t5, t5, t5, _halo_prev_spec(tm), _halo_prev_spec(tm), _mod_spec(), r5, r5,
                  full((CH, NH * CH)), full((CH, WA)), full((HALO, WB)), r5, r5, r5, r5, r5, VMEM_FULL, _row_spec()],
        out_specs=[tok, t5, tok, tok],
        scratch_shapes=[pltpu.VMEM((HALO + tm, WB), F32)],
        compiler_params=_cparams(),
    )(x, u, v, a, g, a, g, gt, gn_g, gn_b, wcat, bspt, conv_w, conv_b, cn_g, cn_b, go_a, go_b, w_mo, g_post)


def _mix_out_bwd(dxo, m, gt, g_post, w_mo):
    nb, s, _ = m.shape
    tm = min(512, s)

    def body(dxo_ref, m_ref, gt_ref, gpost_ref, wmo_ref, dy_ref, dm_ref, dgpost_ref, dgt_ref):
        b, i = pl.program_id(0), pl.program_id(1)
        dxo_v = dxo_ref[0]
        nm, q = _rms(m_ref[0])
        gpost = gpost_ref[...]
        dgt = jnp.sum(dxo_v * (nm * gpost), axis=0, keepdims=True)
        dm, dgpost = _rms_bwd(dxo_v * gt_ref[0], nm, q, gpost)
        dmb = dm.astype(BF16)
        dm_ref[0] = dmb
        dy_ref[0] = _dot_nt(dmb, wmo_ref[...])
        _acc(dgpost_ref, dgpost, _first(b, i))
        _acc(dgt_ref, dgt[None], i == 0)

    tok = _tok_specs(tm, D)
    return pl.pallas_call(
        body, name="mix_out_bwd", grid=(nb, s // tm),
        out_shape=[jax.ShapeDtypeStruct((nb, s, D), F32), jax.ShapeDtypeStruct((nb, s, D), BF16),
                   jax.ShapeDtypeStruct((1, D), F32), jax.ShapeDtypeStruct((nb, 1, D), F32)],
        in_specs=[tok, tok, _mod_spec(), _row_spec(), VMEM_FULL],
        out_specs=[tok, tok, _row_spec(), _mod_spec()],
        compiler_params=_cparams(),
    )(dxo, m, gt, g_post, w_mo)


def _mix_mid_bwd(dy, u, v, conv, gn_g, gn_b, wcat, wcat_t, bspt, cn_g, cn_b, go_a, go_b):
    nb, s, _ = dy.shape
    tm = min(512, s)
    nchunk = tm // CH

    def body(dy_ref, u_ref, v_ref, conv_ref, gng_ref, gnb_ref, wcat_ref, wcatt_ref, bspt_ref, cng_ref, cnb_ref,
             goa_ref, gob_ref,
             du_ref, dv_ref, dconv_ref, dwcat_ref, dbsp_ref, dgng_ref, dgnb_ref, dgoa_ref, dgob_ref,
             dcng_ref, dcnb_ref, dcb_ref):
        first = _first(pl.program_id(0), pl.program_id(1))
        dyv = dy_ref[0]
        xhat, rstd = _ln(v_ref[0])
        gng = gng_ref[...]
        vb = (xhat * gng + gnb_ref[...]).astype(BF16)
        wsb = _spatial_weights(wcat_ref, False)
        wsb_t = _spatial_weights(wcatt_ref, True)
        bias = _spatial_bias(bspt_ref)
        masks = _head_mask((CH, WA))
        vexps, zs = [], []
        for cidx in range(nchunk):
            vexp = _expand_heads(vb[cidx * CH:(cidx + 1) * CH, :], masks)
            vexps.append(vexp)
            zs.append(_dot(wsb, vexp) + bias)
        z = jnp.concatenate(zs, axis=0)
        uv = u_ref[0]
        na, ra = _rms(uv * z)
        dya, dgoa = _rms_bwd(dyv[:, 0:WA], na, ra, goa_ref[...])
        du_ref[0] = dya * z
        dz = dya * uv
        dwcat = jnp.zeros((CH, NH * CH), F32)
        dzsum = jnp.zeros((CH, WA), F32)
        dvlns = []
        for cidx in range(nchunk):
            dzc = dz[cidx * CH:(cidx + 1) * CH, :]
            dzsum = dzsum + dzc
            dzb = dzc.astype(BF16)
            dwcat = dwcat + _dot_nt(dzb, vexps[cidx])
            dvexp = _dot(wsb_t, dzb)
            dvl = jnp.zeros((CH, WA), F32)
            for h in range(NH):
                dvl = dvl + jnp.where(masks[h], dvexp[h * CH:(h + 1) * CH, :], 0.0)
            dvlns.append(dvl)
        dvln = jnp.concatenate(dvlns, axis=0)
        dv, dgng, dgnb = _ln_bwd(dvln, xhat, rstd, gng)
        dv_ref[0] = dv
        lane = lax.broadcasted_iota(jnp.int32, (NH, WA), 1)
        head = lax.broadcasted_iota(jnp.int32, (NH, WA), 0)
        sel = jnp.where((lane >= head * HD) & (lane < (head + 1) * HD), 1.0, 0.0).astype(F32)
        dbsp = lax.dot_general(sel, dzsum, NT, preferred_element_type=F32, precision=lax.Precision.HIGHEST)
        chat, crstd = _ln(conv_ref[0])
        cng = cng_ref[...]
        cln = chat * cng + cnb_ref[...]
        sg = _sigmoid(cln)
        nbb, rb = _rms(cln * sg)
        dyb, dgob = _rms_bwd(dyv[:, WA:D], nbb, rb, gob_ref[...])
        dconv, dcng, dcnb = _ln_bwd(dyb * _dsilu(cln, sg), chat, crstd, cng)
        dconv_ref[0] = dconv
        dcb = jnp.sum(dconv, axis=0, keepdims=True)
        for ref, val in ((dwcat_ref, dwcat), (dbsp_ref, dbsp), (dgng_ref, dgng), (dgnb_ref, dgnb), (dgoa_ref, dgoa),
                         (dgob_ref, dgob), (dcng_ref, dcng), (dcnb_ref, dcnb), (dcb_ref, dcb)):
            _acc(ref, val, first)

    t5 = _tok_specs(tm, WA)
    r5 = _row_spec(WA)
    full = lambda shape: pl.BlockSpec(shape, lambda b, i: (0,) * len(shape))
    big = jax.ShapeDtypeStruct((nb, s, WA), F32)
    row = jax.ShapeDtypeStruct((1, WA), F32)
    return pl.pallas_call(
        body, name="mix_mid_bwd", grid=(nb, s // tm),
        out_shape=[big, big, big, jax.ShapeDtypeStruct((CH, NH * CH), F32), jax.ShapeDtypeStruct((NH, CH), F32),
                   row, row, row, row, row, row, row],
        in_specs=[_tok_specs(tm, D), t5, t5, t5, r5, r5, full((CH, NH * CH)), full((NH * CH, CH)), full((CH, WA)),
                  r5, r5, r5, r5],
        out_specs=[t5, t5, t5, full((CH, NH * CH)), full((NH, CH)), r5, r5, r5, r5, r5, r5, r5],
        compiler_params=_cparams(),
    )(dy, u, v, conv, gn_g, gn_b, wcat, wcat_t, bspt, cn_g, cn_b, go_a, go_b)


def _mix_in_bwd(dxo, x, du, dv, dconv, a, g, sh, sc, g_pre, w_mi4, conv_w):
    nb, s, _ = x.shape
    tm = min(512, s)
    n_i = s // tm

    def body(dxo_ref, x_ref, du_ref, dv_ref, dc_ref, dch_ref, a_ref, g_ref, ah_ref, gh_ref, sh_ref, sc_ref,
             gpre_ref, w_ref, cw_ref,
             dx_ref, dproj_ref, h_ref, dgpre_ref, dsh_ref, dsc_ref, dcw_ref, dext_ref, gext_ref):
        b, i = pl.program_id(0), pl.program_id(1)
        first = _first(b, i)
        av, gv = a_ref[0], g_ref[0]
        sg = _sigmoid(gv)
        dconv = dc_ref[0]
        dext_ref[0:tm, :] = dconv
        dext_ref[tm:tm + HALO, :] = dch_ref[0] * jnp.where(i == n_i - 1, 0.0, 1.0).astype(F32)
        gext_ref[0:HALO, :] = (ah_ref[0] * _sigmoid(gh_ref[0])) * jnp.where(i == 0, 0.0, 1.0).astype(F32)
        gext_ref[HALO:HALO + tm, :] = av * sg
        dglu = _conv_taps(dext_ref, cw_ref, tm, lambda k: CK - 1 - k)

        @pl.when(first)
        def _():
            dcw_ref[...] = jnp.zeros((HALO, WB), F32)

        for k in range(CK):
            lo = k + HALO - (CK - 1)
            dcw_ref[k:k + 1, :] += jnp.sum(dconv * gext_ref[lo:lo + tm, :], axis=0, keepdims=True)
        da = dglu * sg
        dg = dglu * av * (sg * (1.0 - sg))
        parts = [du_ref[0].astype(BF16), dv_ref[0].astype(BF16), da.astype(BF16), dg.astype(BF16)]
        dh = jnp.zeros((tm, D), F32)
        for k in range(4):
            dproj_ref[0, :, k * WA:(k + 1) * WA] = parts[k]
            dh = dh + _dot_nt(parts[k], w_ref[k])
        n, r = _rms(x_ref[0])
        gpre = gpre_ref[...]
        ng = n * gpre
        scale1 = 1.0 + sc_ref[0]
        h_ref[0] = (ng * scale1 + sh_ref[0]).astype(BF16)
        dsh = jnp.sum(dh, axis=0, keepdims=True)
        dsc = jnp.sum(dh * ng, axis=0, keepdims=True)
        dxn, dgpre = _rms_bwd(dh * scale1, n, r, gpre)
        dx_ref[0] = dxo_ref[0] + dxn
        _acc(dgpre_ref, dgpre, first)
        _acc(dsh_ref, dsh[None], i == 0)
        _acc(dsc_ref, dsc[None], i == 0)

    tok = _tok_specs(tm, D)
    t5 = _tok_specs(tm, WA)
    full = lambda shape: pl.BlockSpec(shape, lambda b, i: (0,) * len(shape))
    mod_shape = jax.ShapeDtypeStruct((nb, 1, D), F32)
    return pl.pallas_call(
        body, name="mix_in_bwd", grid=(nb, n_i),
        out_shape=[jax.ShapeDtypeStruct((nb, s, D), F32), jax.ShapeDtypeStruct((nb, s, 4 * WA), BF16),
                   jax.ShapeDtypeStruct((nb, s, D), BF16), jax.ShapeDtypeStruct((1, D), F32), mod_shape, mod_shape,
                   jax.ShapeDtypeStruct((HALO, WB), F32)],
        in_specs=[tok, tok, t5, t5, t5, _halo_next_spec(tm, s), t5, t5, _halo_prev_spec(tm), _halo_prev_spec(tm),
                  _mod_spec(), _mod_spec(), _row_spec(), VMEM_FULL, full((HALO, WB))],
        out_specs=[tok, _tok_specs(tm, 4 * WA), tok, _row_spec(), _mod_spec(), _mod_spec(), full((HALO, WB))],
        scratch_shapes=[pltpu.VMEM((tm + HALO, WB), F32), pltpu.VMEM((HALO + tm, WB), F32)],
        compiler_params=_cparams(),
    )(dxo, x, du, dv, dconv, dconv, a, g, a, g, sh, sc, g_pre, w_mi4, conv_w)


def _row_tile(rows, cols):
    best = 8
    for t in range(8, rows + 1, 8):
        if rows % t == 0 and t * cols * 4 <= 1536 * 1024:
            best = t
    return best


def _sum4(name, own, recv):
    rows, cols = own.shape
    tr = _row_tile(rows, cols)

    def body(own_ref, recv_ref, o_ref):
        acc = own_ref[...]
        for k in range(3):
            acc = acc + recv_ref[k].astype(F32)
        o_ref[...] = acc

    return pl.pallas_call(
        body, name=name, grid=(rows // tr,),
        in_specs=[pl.BlockSpec((tr, cols), lambda i: (i, 0)), pl.BlockSpec((3, tr, cols), lambda i: (0, i, 0))],
        out_specs=pl.BlockSpec((tr, cols), lambda i: (i, 0)),
        out_shape=jax.ShapeDtypeStruct((rows, cols), F32),
        compiler_params=_cparams(),
    )(own, recv)


def _adam_big(name, w, m, v, ga, gb):
    rows, cols = w.shape
    tr = _row_tile(rows, cols)

    def body(w_ref, m_ref, v_ref, ga_ref, gb_ref, g_out, d_out, m_out, v_out):
        gsum = ga_ref[...] + gb_ref[...]
        delta, m2, v2 = _adam(w_ref[...], gsum, m_ref[...], v_ref[...])
        g_out[...] = gsum
        d_out[...] = delta
        m_out[...] = m2
        v_out[...] = v2

    spec = pl.BlockSpec((tr, cols), lambda i: (i, 0))
    shape = jax.ShapeDtypeStruct((rows, cols), F32)
    return pl.pallas_call(
        body, name=name, grid=(rows // tr,), out_shape=[shape] * 4,
        in_specs=[spec] * 5, out_specs=[spec] * 4, compiler_params=_cparams(),
    )(w, m, v, ga, gb)


PK_VEC = 0
PK_PAIR = 8
PK_BSP = 16
PK_WCAT = 24
PK_ROWS = PK_WCAT + CH
PAIR_ORDER = ("gmlp_norm_g", "gmlp_norm_b", "conv_b", "conv_norm_g", "conv_norm_b", "g_out_a", "g_out_b")
VEC_ORDER = ("g_pre_f1", "g_post_f1", "g_pre_m", "g_post_m", "g_pre_f2", "g_post_f2")


def _pack_small(vecs, pairs, dbsp, dwcat):
    def body(*refs):
        vec_refs = refs[:6]
        pair_refs = refs[6:13]
        dbsp_ref, dwcat_ref, o_ref = refs[13:]
        o_ref[0:PK_WCAT, :] = jnp.zeros((PK_WCAT, D), F32)
        for k, r in enumerate(vec_refs):
            o_ref[PK_VEC + k:PK_VEC + k + 1, :] = r[...]
        for k, r in enumerate(pair_refs):
            row, half = PK_PAIR + k // 2, k % 2
            o_ref[row:row + 1, half * WA:(half + 1) * WA] = r[...]
        o_ref[PK_BSP:PK_BSP + NH, 0:CH] = dbsp_ref[...]
        o_ref[PK_WCAT:PK_ROWS, :] = dwcat_ref[...]

    args = list(vecs) + list(pairs) + [dbsp, dwcat]
    return pl.pallas_call(
        body, name="pack_small", out_shape=jax.ShapeDtypeStruct((PK_ROWS, D), F32),
        in_specs=[VMEM_FULL] * len(args), out_specs=VMEM_FULL, compiler_params=_cparams(),
    )(*args)


def _small_adam(pack_all, dcw_all, dada_all, params):
    names = list(VEC_ORDER) + list(PAIR_ORDER) + ["b_spatial", "w_spatial", "conv_w", "b_ada"]
    flat = []
    for nm in names:
        flat += list(params[nm])
    n_in = 3 + len(flat)

    def body(*refs):
        pack_ref, dcw_ref, dada_ref = refs[:3]
        prm = refs[3:n_in]
        outs = refs[n_in:]

        def total(r0, nr, c0, nc):
            acc = pack_ref[0, r0:r0 + nr, c0:c0 + nc]
            for d in range(1, NDEV):
                acc = acc + pack_ref[d, r0:r0 + nr, c0:c0 + nc]
            return acc

        def emit(idx, g, getw, put):
            w_ref, m_ref, v_ref = prm[3 * idx:3 * idx + 3]
            delta, m2, v2 = _adam(getw(w_ref), g, getw(m_ref), getw(v_ref))
            for o_ref, val in zip(outs[4 * idx:4 * idx + 4], (g, delta, m2, v2)):
                put(o_ref, val)

        def whole(ref):
            return ref[...]

        def put_whole(ref, val):
            ref[...] = val

        idx = 0
        for k in range(6):
            emit(idx, total(PK_VEC + k, 1, 0, D), whole, put_whole)
            idx += 1
        for k in range(7):
            emit(idx, total(PK_PAIR + k // 2, 1, (k % 2) * WA, WA), whole, put_whole)
            idx += 1
        emit(idx, total(PK_BSP, NH, 0, CH), lambda r: r[0], lambda r, val: r.__setitem__(0, val))
        idx += 1
        row = lax.broadcasted_iota(jnp.int32, (CH, CH), 0)
        col = lax.broadcasted_iota(jnp.int32, (CH, CH), 1)
        for h in range(NH):
            gh = jnp.where(col <= row, total(PK_WCAT, CH, h * CH, CH), 0.0)
            w_ref, m_ref, v_ref = prm[3 * idx:3 * idx + 3]
            delta, m2, v2 = _adam(w_ref[0, h], gh, m_ref[0, h], v_ref[0, h])
            for o_ref, val in zip(outs[4 * idx:4 * idx + 4], (gh, delta, m2, v2)):
                o_ref[0, h] = val
        idx += 1
        gcw = dcw_ref[0, 0:CK, :]
        for d in range(1, NDEV):
            gcw = gcw + dcw_ref[d, 0:CK, :]
        emit(idx, gcw, lambda r: r[0], lambda r, val: r.__setitem__(0, val))
        idx += 1
        emit(idx, jnp.sum(dada_ref[...], axis=0, keepdims=True), whole, put_whole)

    out_shape = []
    for nm in names:
        w = params[nm][0]
        out_shape += [jax.ShapeDtypeStruct(w.shape, F32)] * 4
    res = pl.pallas_call(
        body, name="small_adam", out_shape=out_shape,
        in_specs=[VMEM_FULL] * n_in, out_specs=[VMEM_FULL] * len(out_shape), compiler_params=_cparams(),
    )(pack_all, dcw_all, dada_all, *flat)
    return {nm: tuple(res[4 * k:4 * k + 4]) for k, nm in enumerate(names)}


WEIGHTS = ['w_ada', 'b_ada', 'g_pre_f1', 'g_post_f1', 'w_f1_in', 'w_f1_out', 'g_pre_m', 'g_post_m', 'w_mix_in',
           'gmlp_norm_g', 'gmlp_norm_b', 'w_spatial', 'b_spatial', 'conv_w', 'conv_b', 'conv_norm_g', 'conv_norm_b',
           'g_out_a', 'g_out_b', 'w_mix_out', 'g_pre_f2', 'g_post_f2', 'w_f2_in', 'w_f2_out']
BIG = ('w_f1_in', 'w_f1_out', 'w_mix_in', 'w_mix_out', 'w_f2_in', 'w_f2_out')


def kernel(x, c, w_ada, b_ada, g_pre_f1, g_post_f1, w_f1_in, w_f1_out, g_pre_m, g_post_m, w_mix_in, gmlp_norm_g, gmlp_norm_b, w_spatial, b_spatial, conv_w, conv_b, conv_norm_g, conv_norm_b, g_out_a, g_out_b, w_mix_out, g_pre_f2, g_post_f2, w_f2_in, w_f2_out, loss_target, m_w_ada, m_b_ada, m_g_pre_f1, m_g_post_f1, m_w_f1_in, m_w_f1_out, m_g_pre_m, m_g_post_m, m_w_mix_in, m_gmlp_norm_g, m_gmlp_norm_b, m_w_spatial, m_b_spatial, m_conv_w, m_conv_b, m_conv_norm_g, m_conv_norm_b, m_g_out_a, m_g_out_b, m_w_mix_out, m_g_pre_f2, m_g_post_f2, m_w_f2_in, m_w_f2_out, v_w_ada, v_b_ada, v_g_pre_f1, v_g_post_f1, v_w_f1_in, v_w_f1_out, v_g_pre_m, v_g_post_m, v_w_mix_in, v_gmlp_norm_g, v_gmlp_norm_b, v_w_spatial, v_b_spatial, v_conv_w, v_conv_b, v_conv_norm_g, v_conv_norm_b, v_g_out_a, v_g_out_b, v_w_mix_out, v_g_pre_f2, v_g_post_f2, v_w_f2_in, v_w_f2_out):
    env = dict(locals())
    wts = {n: env[n] for n in WEIGHTS}
    mom = {n: env["m_" + n] for n in WEIGHTS}
    var = {n: env["v_" + n] for n in WEIGHTS}
    nb, s, _ = x.shape
    t = nb * s
    ax, ay, ac = lax.axis_index("x"), lax.axis_index("y"), lax.axis_index("c")
    j_chip = 2 * ax + ay
    dev = 4 * ax + 2 * ay + ac

    groups = (("w_f1_in", "w_f1_out"), ("w_mix_in", "w_mix_out"), ("w_f2_in", "w_f2_out"))
    gather, after = [], c
    for gi, grp in enumerate(groups):
        srcs = [wts[n][0].astype(BF16) for n in grp] + ([conv_w[0]] if gi == 1 else [])
        lands = [lax.empty((NCHIP,) + a.shape, a.dtype) for a in srcs]
        ssem, rsem, srcs, lands, after = _split_start("gw_start%d" % gi, srcs, lands, _gather_plan(len(srcs)),
                                                      3 * len(srcs), after)
        gather.append((srcs, lands, ssem, rsem))

    def gathered(gi, behind):
        srcs, lands, ssem, rsem = gather[gi]
        return _split_wait("gw_wait%d" % gi, srcs, lands, ssem, rsem, _gather_plan(len(srcs)), behind)

    c = c + after[0, 0]
    (c_all8,) = _allgather8("gather_c", [c.reshape(8, (nb * D) // 8)])
    c_all = c_all8.reshape(NDEV * nb, D)
    b_sh = lax.dynamic_slice(b_ada, (0, j_chip * ADA_SH), (1, ADA_SH))
    ada_sh = _ada_fwd(c_all, w_ada[0], b_sh)
    (ada4,) = _chip_allgather("gather_ada", [ada_sh])
    ada_me = lax.dynamic_slice(ada4, (0, dev * nb, 0), (NCHIP, nb, ADA_SH))
    ada_me = jnp.transpose(ada_me, (1, 0, 2)).reshape(nb, NMOD * D)
    sh1, sc1, gt1, sh2, sc2, gt2, sh3, sc3, gt3 = [ada_me[:, k * D:(k + 1) * D].reshape(nb, 1, D) for k in range(NMOD)]

    wcat = jnp.transpose(w_spatial[0], (1, 0, 2)).reshape(CH, NH * CH)
    wcat_t = jnp.transpose(w_spatial[0], (0, 2, 1)).reshape(NH * CH, CH)
    bspt = jnp.repeat(b_spatial[0].T, HD, axis=1)

    w1i, w1o = gathered(0, sh1)
    w1o = w1o.reshape(DFF, D)
    x1, f1, p1 = _ffn_fwd(x, sh1, sc1, gt1, g_pre_f1, g_post_f1, w1i, w1o)
    wmi, wmo, cw4 = gathered(1, x1)
    wmo = wmo.reshape(D, D)
    cw_full = jnp.transpose(cw4, (1, 0, 2)).reshape(CK, WB)
    cw_pad = jnp.pad(cw_full, ((0, HALO - CK), (0, 0)))
    u, v, a, g = _mix_in_fwd(x1, sh2, sc2, g_pre_m, wmi)
    x2, conv, yb, m = _mix_mid_fwd(x1, u, v, a, g, gt2, gmlp_norm_g, gmlp_norm_b, wcat, bspt, cw_pad, conv_b,
                                   conv_norm_g, conv_norm_b, g_out_a, g_out_b, wmo, g_post_m)
    w2i, w2o = gathered(2, x2)
    w2o = w2o.reshape(DFF, D)
    dx3, f2, p2, lsum = _ffn_fwd(x2, sh3, sc3, gt3, g_pre_f2, g_post_f2, w2i, w2o, target=loss_target)
    loss = lax.psum(0.5 * jnp.sum(lsum) / D, ("x", "y", "c"))

    def chip4(pair, rows):
        return [arr.reshape(NCHIP, rows, arr.shape[-1]) for arr in pair]

    scatter = {}

    def scatter_start(tag, pairs, behind):
        srcs = [p[1] for p in pairs]
        lands = [lax.empty((3,) + a.shape[1:], a.dtype) for a in srcs]
        ssem, rsem, srcs, lands, token = _split_start("gs_start_" + tag, srcs, lands, _scatter_plan(len(srcs)),
                                                      3 * len(srcs), behind)
        scatter[tag] = (srcs, lands, ssem, rsem)
        return token

    def reduce_and_update(tag, names, pairs, behind):
        srcs, lands, ssem, rsem = scatter[tag]
        recv = _split_wait("gs_wait_" + tag, srcs, lands, ssem, rsem, _scatter_plan(len(srcs)), behind)
        part = [_sum4("sum4_" + n, lax.dynamic_index_in_dim(pairs[k][0], j_chip, 0, keepdims=False), recv[k])
                for k, n in enumerate(names)]
        other = _sibling_swap("swap_" + tag, part)
        for k, n in enumerate(names):
            out[n] = tuple(r[None] for r in _adam_big("adam_" + n, wts[n][0], mom[n][0], var[n][0], part[k], other[k]))

    out = {}
    dx2, dp2, h3, a2, df2, dg_pre_f2, dg_post_f2, dsh3, dsc3, dgt3 = _ffn_bwd(
        dx3, x2, f2, p2, sh3, sc3, gt3, g_pre_f2, g_post_f2, w2i, w2o)
    gw2i = _wgrad("wgrad_f2_in", h3.reshape(t, D), dp2.reshape(t, 2 * DFF), 2 * DFF // NCHIP, True)
    gw2o = chip4(_wgrad("wgrad_f2_out", a2.reshape(t, DFF), df2.reshape(t, D), D // 2, False), DFF // NCHIP)
    tok = scatter_start("f2", [gw2i, gw2o], dg_post_f2)
    dy, dm, dg_post_m, dgt2 = _mix_out_bwd(dx2, m, gt2 + tok[0, 0], g_post_m, wmo)
    gwmo = chip4(_wgrad("wgrad_mix_out", yb.reshape(t, D), dm.reshape(t, D), D // 2, False), D // NCHIP)
    (du, dv, dconv, dwcat, dbsp, dgn_g, dgn_b, dgo_a, dgo_b, dcn_g, dcn_b, dcb) = _mix_mid_bwd(
        dy, u, v, conv, gmlp_norm_g, gmlp_norm_b, wcat, wcat_t, bspt, conv_norm_g, conv_norm_b, g_out_a, g_out_b)
    dx1, dproj, h2, dg_pre_m, dsh2, dsc2, dcw = _mix_in_bwd(dx2, x1, du, dv, dconv, a, g, sh2, sc2, g_pre_m, wmi, cw_pad)
    gwmi = _wgrad("wgrad_mix_in", h2.reshape(t, D), dproj.reshape(t, 4 * WA), WA, True)
    tok = scatter_start("mix", [gwmi, gwmo], dg_pre_m)
    grad_x, dp1, h1, a1, df1, dg_pre_f1, dg_post_f1, dsh1, dsc1, dgt1 = _ffn_bwd(
        dx1, x, f1, p1, sh1 + tok[0, 0], sc1, gt1, g_pre_f1, g_post_f1, w1i, w1o)
    gw1i = _wgrad("wgrad_f1_in", h1.reshape(t, D), dp1.reshape(t, 2 * DFF), 2 * DFF // NCHIP, True)
    gw1o = chip4(_wgrad("wgrad_f1_out", a1.reshape(t, DFF), df1.reshape(t, D), D // 2, False), DFF // NCHIP)
    tok = scatter_start("f1", [gw1i, gw1o], dg_post_f1)
    reduce_and_update("f2", ("w_f2_in", "w_f2_out"), [gw2i, gw2o], tok)
    reduce_and_update("mix", ("w_mix_in", "w_mix_out"), [gwmi, gwmo], out["w_f2_out"][3])

    dada = jnp.concatenate([q.reshape(nb, D) for q in (dsh1, dsc1, dgt1, dsh2, dsc2, dgt2, dsh3, dsc3, dgt3)], axis=1)
    vec_grads = dict(g_pre_f1=dg_pre_f1, g_post_f1=dg_post_f1, g_pre_m=dg_pre_m, g_post_m=dg_post_m,
                     g_pre_f2=dg_pre_f2, g_post_f2=dg_post_f2)
    pair_grads = dict(gmlp_norm_g=dgn_g, gmlp_norm_b=dgn_b, conv_b=dcb, conv_norm_g=dcn_g, conv_norm_b=dcn_b,
                      g_out_a=dgo_a, g_out_b=dgo_b)
    pack = _pack_small([vec_grads[n] for n in VEC_ORDER], [pair_grads[n] for n in PAIR_ORDER], dbsp, dwcat)
    pack_all, dcw_all, dada_all8 = _allgather8("gather_small", [pack, dcw, dada.reshape(8, (nb * NMOD * D) // 8)])
    dada_all = dada_all8.reshape(NDEV * nb, NMOD * D)
    dcw_mine = lax.dynamic_slice(dcw_all, (0, 0, j_chip * (WB // NCHIP)), (NDEV, HALO, WB // NCHIP))
    small = {n: (wts[n], mom[n], var[n]) for n in list(VEC_ORDER) + list(PAIR_ORDER) + ["b_spatial", "w_spatial", "conv_w", "b_ada"]}
    out.update(_small_adam(pack_all, dcw_mine, dada_all, small))
    dada_sh = lax.dynamic_slice(dada_all, (0, j_chip * ADA_SH), (NDEV * nb, ADA_SH))
    out["w_ada"] = tuple(r[None] for r in _ada_bwd_adam(c_all, dada_sh, w_ada[0], m_w_ada[0], v_w_ada[0]))
    reduce_and_update("f1", ("w_f1_in", "w_f1_out"), [gw1i, gw1o], out["w_ada"][3])

    res = [loss, grad_x]
    for k in range(4):
        res += [out[n][k] for n in WEIGHTS]
    return tuple(res)
```

```python
import functools

import jax
import jax.numpy as jnp
from jax import lax
from jax.experimental import pallas as pl
from jax.experimental.pallas import tpu as pltpu

D = 1024
DFF = 2816
WA = 512
WB = 512
NH = 8
HD = 64
CH = 128
CK = 31
HALO = 32
NMOD = 9
EPS = 1e-6
NCHIP = 4
NDEV = 8
FBLK = DFF // 2
ADA_SH = NMOD * D // NCHIP

LR, B1, B2, EPS_A, WD, STEP = 0.001, 0.9, 0.999, 1e-08, 0.01, 10

F32 = jnp.float32
BF16 = jnp.bfloat16
MESH = pl.DeviceIdType.MESH
ANY = pl.BlockSpec(memory_space=pl.ANY)
VMEM_FULL = pl.BlockSpec(memory_space=pltpu.VMEM)
VMEM_LIMIT = 56 * 1024 * 1024

NT = (((1,), (1,)), ((), ()))
TN = (((0,), (0,)), ((), ()))


def _dot(a, b):
    return jnp.dot(a, b, preferred_element_type=F32)


def _dot_nt(a, b):
    return lax.dot_general(a, b, NT, preferred_element_type=F32)


def _dot_tn(a, b):
    return lax.dot_general(a, b, TN, preferred_element_type=F32)


def _cparams():
    return pltpu.CompilerParams(vmem_limit_bytes=VMEM_LIMIT)


def _allgather8(name, arrs):
    n = len(arrs)

    def body(*refs):
        ins, outs = refs[:n], refs[n:2 * n]
        send_sems, recv_sems, local_sems = refs[2 * n:]
        x, y, c = lax.axis_index("x"), lax.axis_index("y"), lax.axis_index("c")
        me, sibling = (x, y, c), (x, y, 1 - c)
        chips = [(1 - x, y), (x, 1 - y), (1 - x, 1 - y)]

        def copy(a, k, block, to, src=None):
            rows = outs[a].at[4 * block[0] + 2 * block[1] + block[2]]
            return pltpu.make_async_remote_copy(
                src_ref=rows if src is None else src, dst_ref=rows,
                send_sem=send_sems.at[a, k], recv_sem=recv_sems.at[a, k],
                device_id=to, device_id_type=MESH)

        started, mine = [], []
        for a in range(n):
            loc = pltpu.make_async_copy(ins[a], outs[a].at[4 * x + 2 * y + c], local_sems.at[a])
            loc.start()
            mine.append(loc)
            first = [copy(a, 0, me, sibling, src=ins[a])]
            first += [copy(a, 1 + j, me, (*chip, c), src=ins[a]) for j, chip in enumerate(chips)]
            for cp in first:
                cp.start()
            started += first
        for a in range(n):
            for j, chip in enumerate(chips):
                copy(a, 1 + j, (*chip, c), me).wait_recv()
                fwd = copy(a, 4 + j, (*chip, c), sibling)
                fwd.start()
                started.append(fwd)
        for a in range(n):
            copy(a, 0, sibling, me).wait_recv()
            for j, chip in enumerate(chips):
                copy(a, 4 + j, (*chip, 1 - c), me).wait_recv()
        for cp in started:
            cp.wait_send()
        for loc in mine:
            loc.wait()

    return pl.pallas_call(
        body, name=name,
        out_shape=[jax.ShapeDtypeStruct((NDEV,) + a.shape, a.dtype) for a in arrs],
        in_specs=[ANY] * n, out_specs=[ANY] * n,
        scratch_shapes=[pltpu.SemaphoreType.DMA((n, 7)), pltpu.SemaphoreType.DMA((n, 7)),
                        pltpu.SemaphoreType.DMA((n,))],
    )(*arrs)


def _chip_relations(x, y):
    return [(1 - x, y), (x, 1 - y), (1 - x, 1 - y)]


def _exchange(name, arrs, out_shapes, plan):
    n = len(arrs)
    n_out = len(out_shapes)

    def body(*refs):
        ins, outs = refs[:n], refs[n:n + n_out]
        send_sems, recv_sems, local_sems = refs[n + n_out:]
        x, y, c = lax.axis_index("x"), lax.axis_index("y"), lax.axis_index("c")
        local, sends = plan(x, y, c, ins, outs)
        locs = [pltpu.make_async_copy(s, d, local_sems.at[i]) for i, (s, d) in enumerate(local)]
        for loc in locs:
            loc.start()
        cps = [pltpu.make_async_remote_copy(src_ref=s, dst_ref=d, send_sem=send_sems.at[i], recv_sem=recv_sems.at[i],
                                            device_id=peer, device_id_type=MESH)
               for i, (s, d, peer, _) in enumerate(sends)]
        for cp in cps:
            cp.start()
        for i, (s, _, peer, landing) in enumerate(sends):
            pltpu.make_async_remote_copy(src_ref=s, dst_ref=landing, send_sem=send_sems.at[i], recv_sem=recv_sems.at[i],
                                         device_id=peer, device_id_type=MESH).wait_recv()
        for cp in cps:
            cp.wait_send()
        for loc in locs:
            loc.wait()

    return n, n_out, body


def _run_exchange(name, arrs, out_shapes, plan, n_local, n_send):
    n, n_out, body = _exchange(name, arrs, out_shapes, plan)
    return pl.pallas_call(
        body, name=name, out_shape=out_shapes,
        in_specs=[ANY] * n, out_specs=[ANY] * n_out,
        scratch_shapes=[pltpu.SemaphoreType.DMA((n_send,)), pltpu.SemaphoreType.DMA((n_send,)),
                        pltpu.SemaphoreType.DMA((max(n_local, 1),))],
    )(*arrs)


def _chip_allgather(name, arrs):
    n = len(arrs)

    def plan(x, y, c, ins, outs):
        j_me = 2 * x + y
        local = [(ins[a], outs[a].at[j_me]) for a in range(n)]
        sends = []
        for a in range(n):
            for (px, py) in _chip_relations(x, y):
                sends.append((ins[a], outs[a].at[j_me], (px, py, c), outs[a].at[2 * px + py]))
        return local, sends

    shapes = [jax.ShapeDtypeStruct((NCHIP,) + a.shape, a.dtype) for a in arrs]
    return _run_exchange(name, arrs, shapes, plan, n, 3 * n)


def _chip_scatter(name, arrs):
    n = len(arrs)

    def plan(x, y, c, ins, outs):
        sends = []
        for a in range(n):
            for k, (px, py) in enumerate(_chip_relations(x, y)):
                sends.append((ins[a].at[2 * px + py], outs[a].at[k], (px, py, c), outs[a].at[k]))
        return [], sends

    shapes = [jax.ShapeDtypeStruct((3,) + a.shape[1:], a.dtype) for a in arrs]
    return _run_exchange(name, arrs, shapes, plan, 0, 3 * n)


def _sibling_swap(name, arrs):
    n = len(arrs)

    def plan(x, y, c, ins, outs):
        return [], [(ins[a], outs[a], (x, y, 1 - c), outs[a]) for a in range(n)]

    shapes = [jax.ShapeDtypeStruct(a.shape, a.dtype) for a in arrs]
    return _run_exchange(name, arrs, shapes, plan, 0, n)


HBM = pl.BlockSpec(memory_space=pltpu.HBM)
SEM = pl.BlockSpec(memory_space=pltpu.SEMAPHORE)
EFFECT = pltpu.SideEffectType.DATAFLOW_SIDE_EFFECTING


def _split_start(name, srcs, lands, plan, n_send, after):
    n, nl = len(srcs), len(lands)

    def body(*refs):
        src, land = refs[:n], refs[n:n + nl]
        send_sems, recv_sems = refs[n + nl + 1], refs[n + nl + 2]
        token = refs[-2]
        local_sems = refs[-1]
        x, y, c = lax.axis_index("x"), lax.axis_index("y"), lax.axis_index("c")
        local, sends = plan(x, y, c, src, land)
        locs = [pltpu.make_async_copy(s, d, local_sems.at[i]) for i, (s, d) in enumerate(local)]
        for loc in locs:
            loc.start()
        for loc in locs:
            loc.wait()
        for i, (s, d, peer, _) in enumerate(sends):
            pltpu.make_async_remote_copy(src_ref=s, dst_ref=d, send_sem=send_sems.at[i], recv_sem=recv_sems.at[i],
                                         device_id=peer, device_id_type=MESH).start()
        token[...] = jnp.zeros_like(token)

    thru = [pltpu.HBM(a.shape, a.dtype) for a in list(srcs) + list(lands)]
    res = pl.pallas_call(
        body, name=name,
        out_shape=(pltpu.SemaphoreType.DMA((n_send,)), pltpu.SemaphoreType.DMA((n_send,)), *thru,
                   jax.ShapeDtypeStruct((8, 128), F32)),
        in_specs=[HBM] * (n + nl) + [ANY],
        out_specs=(SEM, SEM, *([HBM] * (n + nl)), pl.BlockSpec(memory_space=pltpu.VMEM)),
        input_output_aliases={i: 2 + i for i in range(n + nl)},
        scratch_shapes=[pltpu.SemaphoreType.DMA((max(len(srcs), 1),))],
        compiler_params=pltpu.CompilerParams(has_side_effects=EFFECT),
    )(*[pltpu.with_memory_space_constraint(a, pltpu.HBM) for a in list(srcs) + list(lands)], after)
    return res[0], res[1], list(res[2:2 + n]), list(res[2 + n:2 + n + nl]), res[-1]


def _split_wait(name, srcs, lands, send_sems, recv_sems, plan, after):
    n, nl = len(srcs), len(lands)

    def body(*refs):
        src, land = refs[:n], refs[n:n + nl]
        send_sems, recv_sems = refs[n + nl], refs[n + nl + 1]
        x, y, c = lax.axis_index("x"), lax.axis_index("y"), lax.axis_index("c")
        _, sends = plan(x, y, c, src, land)
        for i, (s, _, peer, landing) in enumerate(sends):
            cp = pltpu.make_async_remote_copy(src_ref=s, dst_ref=landing, send_sem=send_sems.at[i],
                                              recv_sem=recv_sems.at[i], device_id=peer, device_id_type=MESH)
            cp.wait_send()
            cp.wait_recv()

    thru = [pltpu.HBM(a.shape, a.dtype) for a in list(srcs) + list(lands)]
    res = pl.pallas_call(
        body, name=name, out_shape=tuple(thru),
        in_specs=[HBM] * (n + nl) + [SEM, SEM, ANY], out_specs=tuple([HBM] * (n + nl)),
        input_output_aliases={i: i for i in range(n + nl)},
        compiler_params=pltpu.CompilerParams(has_side_effects=EFFECT),
    )(*srcs, *lands, send_sems, recv_sems, after)
    return list(res[n:])


def _gather_plan(n):
    def plan(x, y, c, src, land):
        j_me = 2 * x + y
        local = [(src[a], land[a].at[j_me]) for a in range(n)]
        sends = []
        for a in range(n):
            for (px, py) in _chip_relations(x, y):
                sends.append((src[a], land[a].at[j_me], (px, py, c), land[a].at[2 * px + py]))
        return local, sends

    return plan


def _scatter_plan(n):
    def plan(x, y, c, src, land):
        sends = []
        for a in range(n):
            for k, (px, py) in enumerate(_chip_relations(x, y)):
                sends.append((src[a].at[2 * px + py], land[a].at[k], (px, py, c), land[a].at[k]))
        return [], sends

    return plan


def _rms(x):
    r = lax.rsqrt(jnp.mean(x * x, axis=-1, keepdims=True) + EPS)
    return x * r, r


def _rms_bwd(dy, n, r, g):
    dg = jnp.sum(dy * n, axis=0, keepdims=True)
    dn = dy * g
    dx = r * (dn - n * jnp.mean(dn * n, axis=-1, keepdims=True))
    return dx, dg


def _ln(x):
    mu = jnp.mean(x, axis=-1, keepdims=True)
    xc = x - mu
    rstd = lax.rsqrt(jnp.mean(xc * xc, axis=-1, keepdims=True) + EPS)
    return xc * rstd, rstd


def _ln_bwd(dy, xhat, rstd, g):
    dg = jnp.sum(dy * xhat, axis=0, keepdims=True)
    db = jnp.sum(dy, axis=0, keepdims=True)
    dxh = dy * g
    dx = rstd * (dxh - jnp.mean(dxh, axis=-1, keepdims=True) - xhat * jnp.mean(dxh * xhat, axis=-1, keepdims=True))
    return dx, dg, db


def _sigmoid(x):
    return jax.nn.sigmoid(x)


def _dsilu(x, s):
    return s * (1.0 + x * (1.0 - s))


def _adam(w, g, m, v):
    m = B1 * m + (1.0 - B1) * g
    v = B2 * v + (1.0 - B2) * (g * g)
    m_hat = m / (1.0 - B1 ** STEP)
    v_hat = v / (1.0 - B2 ** STEP)
    delta = -LR * (m_hat / (jnp.sqrt(v_hat) + EPS_A) + WD * w)
    return delta, m, v


def _head_mask(shape):
    lane = lax.broadcasted_iota(jnp.int32, shape, len(shape) - 1)
    return [(lane >= h * HD) & (lane < (h + 1) * HD) for h in range(NH)]


def _first(b, i):
    return jnp.logical_and(b == 0, i == 0)


def _acc(ref, val, first):
    @pl.when(first)
    def _():
        ref[...] = val

    @pl.when(jnp.logical_not(first))
    def _():
        ref[...] += val


def _ada_fwd(c_all, w_sh, b_sh):
    nb = c_all.shape[0]
    tn = 768

    def body(c_ref, w_ref, b_ref, o_ref):
        cv = c_ref[...]
        cs = (cv * _sigmoid(cv)).astype(BF16)
        o_ref[...] = _dot(cs, w_ref[...].astype(BF16)) + b_ref[...]

    return pl.pallas_call(
        body, name="ada_fwd", grid=(ADA_SH // tn,),
        out_shape=jax.ShapeDtypeStruct((nb, ADA_SH), F32),
        in_specs=[pl.BlockSpec((nb, D), lambda j: (0, 0)), pl.BlockSpec((D, tn), lambda j: (0, j)),
                  pl.BlockSpec((1, tn), lambda j: (0, j))],
        out_specs=pl.BlockSpec((nb, tn), lambda j: (0, j)),
        compiler_params=_cparams(),
    )(c_all, w_sh, b_sh)


def _ada_bwd_adam(c_all, dada_sh, w, m, v):
    nb = c_all.shape[0]
    tn = 768

    def body(c_ref, d_ref, w_ref, m_ref, v_ref, g_out, d_out, m_out, v_out):
        cv = c_ref[...]
        cs = (cv * _sigmoid(cv)).astype(BF16)
        g = _dot_tn(cs, d_ref[...].astype(BF16))
        delta, m2, v2 = _adam(w_ref[...], g, m_ref[...], v_ref[...])
        g_out[...] = g
        d_out[...] = delta
        m_out[...] = m2
        v_out[...] = v2

    big = pl.BlockSpec((D, tn), lambda j: (0, j))
    shape = jax.ShapeDtypeStruct((D, ADA_SH), F32)
    return pl.pallas_call(
        body, name="ada_bwd_adam", grid=(ADA_SH // tn,),
        out_shape=[shape] * 4,
        in_specs=[pl.BlockSpec((nb, D), lambda j: (0, 0)), pl.BlockSpec((nb, tn), lambda j: (0, j)), big, big, big],
        out_specs=[big] * 4,
        compiler_params=_cparams(),
    )(c_all, dada_sh, w, m, v)


def _tok_specs(tm, width):
    return pl.BlockSpec((1, tm, width), lambda b, i: (b, i, 0))


def _mod_spec():
    return pl.BlockSpec((1, 1, D), lambda b, i: (b, 0, 0))


def _row_spec(width=D):
    return pl.BlockSpec((1, width), lambda b, i: (0, 0))


def _ffn_fwd(x, sh, sc, gt, g_pre, g_post, w_in4, w_out, target=None):
    nb, s, _ = x.shape
    tm = min(256, s)
    with_loss = target is not None

    def body(*refs):
        if with_loss:
            (x_ref, sh_ref, sc_ref, gt_ref, gpre_ref, gpost_ref, win_ref, wout_ref, tgt_ref,
             xo_ref, f_ref, p_ref, ls_ref) = refs
        else:
            (x_ref, sh_ref, sc_ref, gt_ref, gpre_ref, gpost_ref, win_ref, wout_ref,
             xo_ref, f_ref, p_ref) = refs
        xv = x_ref[0]
        n, _ = _rms(xv)
        h = (n * gpre_ref[...]) * (1.0 + sc_ref[0]) + sh_ref[0]
        hb = h.astype(BF16)
        acc = jnp.zeros((tm, D), F32)
        for j in range(2):
            gate = _dot(hb, win_ref[j])
            up = _dot(hb, win_ref[2 + j])
            p_ref[0, :, j * FBLK:(j + 1) * FBLK] = gate.astype(BF16)
            p_ref[0, :, DFF + j * FBLK:DFF + (j + 1) * FBLK] = up.astype(BF16)
            a = (gate * _sigmoid(gate)) * up
            acc = acc + _dot(a.astype(BF16), wout_ref[j * FBLK:(j + 1) * FBLK, :])
        f_ref[0] = acc
        nf, _ = _rms(acc)
        out = xv + (0.5 * gt_ref[0]) * (nf * gpost_ref[...])
        if with_loss:
            err = out - tgt_ref[0]
            xo_ref[0] = err * (1.0 / D)
            row = jnp.sum(err * err, axis=0, keepdims=True)
            part = row[:, 0:128]
            for k in range(1, D // 128):
                part = part + row[:, k * 128:(k + 1) * 128]
            _acc(ls_ref, part, _first(pl.program_id(0), pl.program_id(1)))
        else:
            xo_ref[0] = out

    in_specs = [_tok_specs(tm, D), _mod_spec(), _mod_spec(), _mod_spec(), _row_spec(), _row_spec(), VMEM_FULL, VMEM_FULL]
    args = [x, sh, sc, gt, g_pre, g_post, w_in4, w_out]
    out_shape = [jax.ShapeDtypeStruct((nb, s, D), F32), jax.ShapeDtypeStruct((nb, s, D), F32),
                 jax.ShapeDtypeStruct((nb, s, 2 * DFF), BF16)]
    out_specs = [_tok_specs(tm, D), _tok_specs(tm, D), _tok_specs(tm, 2 * DFF)]
    if with_loss:
        in_specs.append(_tok_specs(tm, D))
        args.append(target)
        out_shape.append(jax.ShapeDtypeStruct((1, 128), F32))
        out_specs.append(pl.BlockSpec((1, 128), lambda b, i: (0, 0)))
    return pl.pallas_call(
        body, name="ffn_loss_fwd" if with_loss else "ffn_fwd", grid=(nb, s // tm),
        out_shape=out_shape, in_specs=in_specs, out_specs=out_specs,
        compiler_params=_cparams(),
    )(*args)


def _ffn_bwd(dxo, x, f, p, sh, sc, gt, g_pre, g_post, w_in4, w_out):
    nb, s, _ = x.shape
    tm = min(256, s)

    def body(dxo_ref, x_ref, f_ref, p_ref, sh_ref, sc_ref, gt_ref, gpre_ref, gpost_ref, win_ref, wout_ref,
             dx_ref, dp_ref, h_ref, a_ref, df_ref, dgpre_ref, dgpost_ref, dsh_ref, dsc_ref, dgt_ref):
        b, i = pl.program_id(0), pl.program_id(1)
        dxo_v = dxo_ref[0]
        nf, q = _rms(f_ref[0])
        gpost = gpost_ref[...]
        dgt = jnp.sum(dxo_v * (0.5 * (nf * gpost)), axis=0, keepdims=True)
        do = dxo_v * (0.5 * gt_ref[0])
        df, dgpost = _rms_bwd(do, nf, q, gpost)
        dfb = df.astype(BF16)
        df_ref[0] = dfb
        xv = x_ref[0]
        n, r = _rms(xv)
        gpre = gpre_ref[...]
        ng = n * gpre
        scale1 = 1.0 + sc_ref[0]
        h = ng * scale1 + sh_ref[0]
        h_ref[0] = h.astype(BF16)
        dh = jnp.zeros((tm, D), F32)
        for j in range(2):
            gate = p_ref[0, :, j * FBLK:(j + 1) * FBLK].astype(F32)
            up = p_ref[0, :, DFF + j * FBLK:DFF + (j + 1) * FBLK].astype(F32)
            sg = _sigmoid(gate)
            act = gate * sg
            a_ref[0, :, j * FBLK:(j + 1) * FBLK] = (act * up).astype(BF16)
            da = _dot_nt(dfb, wout_ref[j * FBLK:(j + 1) * FBLK, :])
            dgate = (da * up * _dsilu(gate, sg)).astype(BF16)
            dup = (da * act).astype(BF16)
            dp_ref[0, :, j * FBLK:(j + 1) * FBLK] = dgate
            dp_ref[0, :, DFF + j * FBLK:DFF + (j + 1) * FBLK] = dup
            dh = dh + _dot_nt(dgate, win_ref[j]) + _dot_nt(dup, win_ref[2 + j])
        dsh = jnp.sum(dh, axis=0, keepdims=True)
        dsc = jnp.sum(dh * ng, axis=0, keepdims=True)
        dxn, dgpre = _rms_bwd(dh * scale1, n, r, gpre)
        dx_ref[0] = dxo_v + dxn
        _acc(dgpre_ref, dgpre, _first(b, i))
        _acc(dgpost_ref, dgpost, _first(b, i))
        _acc(dsh_ref, dsh[None], i == 0)
        _acc(dsc_ref, dsc[None], i == 0)
        _acc(dgt_ref, dgt[None], i == 0)

    tok = _tok_specs(tm, D)
    mod_shape = jax.ShapeDtypeStruct((nb, 1, D), F32)
    row_shape = jax.ShapeDtypeStruct((1, D), F32)
    return pl.pallas_call(
        body, name="ffn_bwd", grid=(nb, s // tm),
        out_shape=[jax.ShapeDtypeStruct((nb, s, D), F32), jax.ShapeDtypeStruct((nb, s, 2 * DFF), BF16),
                   jax.ShapeDtypeStruct((nb, s, D), BF16), jax.ShapeDtypeStruct((nb, s, DFF), BF16),
                   jax.ShapeDtypeStruct((nb, s, D), BF16), row_shape, row_shape, mod_shape, mod_shape, mod_shape],
        in_specs=[tok, tok, tok, _tok_specs(tm, 2 * DFF), _mod_spec(), _mod_spec(), _mod_spec(), _row_spec(), _row_spec(),
                  VMEM_FULL, VMEM_FULL],
        out_specs=[tok, _tok_specs(tm, 2 * DFF), tok, _tok_specs(tm, DFF), tok, _row_spec(), _row_spec(),
                   _mod_spec(), _mod_spec(), _mod_spec()],
        compiler_params=_cparams(),
    )(dxo, x, f, p, sh, sc, gt, g_pre, g_post, w_in4, w_out)


def _wgrad(name, a, b, col_block, chip_major):
    t, ka = a.shape
    n = b.shape[1]
    tk = min(512, t)
    nk = t // tk
    nblk = n // col_block

    def body(a_ref, b_ref, o_ref, obf_ref, acc_ref):
        k = pl.program_id(1)

        @pl.when(k == 0)
        def _():
            acc_ref[...] = jnp.zeros_like(acc_ref)

        acc_ref[...] += _dot_tn(a_ref[...], b_ref[...])

        @pl.when(k == nk - 1)
        def _():
            val = acc_ref[...]
            if chip_major:
                o_ref[0] = val
                obf_ref[0] = val.astype(BF16)
            else:
                o_ref[...] = val
                obf_ref[...] = val.astype(BF16)

    if chip_major:
        shape = (nblk, ka, col_block)
        ospec = pl.BlockSpec((1, ka, col_block), lambda j, k: (j, 0, 0))
    else:
        shape = (ka, n)
        ospec = pl.BlockSpec((ka, col_block), lambda j, k: (0, j))
    return pl.pallas_call(
        body, name=name, grid=(nblk, nk),
        out_shape=[jax.ShapeDtypeStruct(shape, F32), jax.ShapeDtypeStruct(shape, BF16)],
        in_specs=[pl.BlockSpec((tk, ka), lambda j, k: (k, 0)), pl.BlockSpec((tk, col_block), lambda j, k: (k, j))],
        out_specs=[ospec, ospec],
        scratch_shapes=[pltpu.VMEM((ka, col_block), F32)],
        compiler_params=_cparams(),
    )(a, b)


def _mix_in_fwd(x, sh, sc, g_pre, w_mi4):
    nb, s, _ = x.shape
    tm = min(512, s)

    def body(x_ref, sh_ref, sc_ref, gpre_ref, w_ref, u_ref, v_ref, a_ref, g_ref):
        n, _ = _rms(x_ref[0])
        hb = ((n * gpre_ref[...]) * (1.0 + sc_ref[0]) + sh_ref[0]).astype(BF16)
        for k, o_ref in enumerate((u_ref, v_ref, a_ref, g_ref)):
            o_ref[0] = _dot(hb, w_ref[k])

    shape = jax.ShapeDtypeStruct((nb, s, WA), F32)
    return pl.pallas_call(
        body, name="mix_in_fwd", grid=(nb, s // tm),
        out_shape=[shape] * 4,
        in_specs=[_tok_specs(tm, D), _mod_spec(), _mod_spec(), _row_spec(), VMEM_FULL],
        out_specs=[_tok_specs(tm, WA)] * 4,
        compiler_params=_cparams(),
    )(x, sh, sc, g_pre, w_mi4)


def _spatial_weights(wcat_ref, transposed):
    w = wcat_ref[...]
    row = lax.broadcasted_iota(jnp.int32, w.shape, 0)
    col = lax.broadcasted_iota(jnp.int32, w.shape, 1)
    keep = ((row & (CH - 1)) <= col) if transposed else ((col & (CH - 1)) <= row)
    return jnp.where(keep, w, 0.0).astype(BF16)


def _expand_heads(vc, masks):
    return jnp.concatenate([jnp.where(mk, vc, jnp.zeros_like(vc)) for mk in masks], axis=0)


def _spatial_bias(bspt_ref):
    return bspt_ref[...]


def _conv_taps(ext_ref, w_ref, tm, offset):
    acc = jnp.zeros((tm, WB), F32)
    for k in range(CK):
        acc = acc + w_ref[k:k + 1, :] * ext_ref[offset(k):offset(k) + tm, :]
    return acc


def _halo_prev_spec(tm):
    return pl.BlockSpec((1, HALO, WB), lambda b, i: (b, jnp.maximum(i * (tm // HALO) - 1, 0), 0))


def _halo_next_spec(tm, s):
    return pl.BlockSpec((1, HALO, WB), lambda b, i: (b, jnp.minimum((i + 1) * (tm // HALO), s // HALO - 1), 0))


def _mix_mid_fwd(x, u, v, a, g, gt, gn_g, gn_b, wcat, bspt, conv_w, conv_b, cn_g, cn_b, go_a, go_b, w_mo, g_post):
    nb, s, _ = x.shape
    tm = min(512, s)

    def body(x_ref, u_ref, v_ref, a_ref, g_ref, ah_ref, gh_ref, gt_ref, gng_ref, gnb_ref, wcat_ref, bspt_ref,
             cw_ref, cb_ref, cng_ref, cnb_ref, goa_ref, gob_ref, wmo_ref, gpost_ref,
             xo_ref, conv_ref, y_ref, m_ref, ext_ref):
        i = pl.program_id(1)
        xhat, _ = _ln(v_ref[0])
        vb = (xhat * gng_ref[...] + gnb_ref[...]).astype(BF16)
        wsb = _spatial_weights(wcat_ref, False)
        bias = _spatial_bias(bspt_ref)
        masks = _head_mask((CH, WA))
        zs = []
        for cidx in range(tm // CH):
            vexp = _expand_heads(vb[cidx * CH:(cidx + 1) * CH, :], masks)
            zs.append(_dot(wsb, vexp) + bias)
        z = jnp.concatenate(zs, axis=0)
        na, _ = _rms(u_ref[0] * z)
        keep = jnp.where(i == 0, 0.0, 1.0).astype(F32)
        ext_ref[0:HALO, :] = (ah_ref[0] * _sigmoid(gh_ref[0])) * keep
        ext_ref[HALO:HALO + tm, :] = a_ref[0] * _sigmoid(g_ref[0])
        conv = _conv_taps(ext_ref, cw_ref, tm, lambda k: k + HALO - (CK - 1)) + cb_ref[...]
        conv_ref[0] = conv
        chat, _ = _ln(conv)
        cln = chat * cng_ref[...] + cnb_ref[...]
        nbb, _ = _rms(cln * _sigmoid(cln))
        yb = jnp.concatenate([na * goa_ref[...], nbb * gob_ref[...]], axis=1).astype(BF16)
        y_ref[0] = yb
        m = _dot(yb, wmo_ref[...])
        m_ref[0] = m
        nm, _ = _rms(m)
        xo_ref[0] = x_ref[0] + gt_ref[0] * (nm * gpost_ref[...])

    t5 = _tok_specs(tm, WA)
    tok = _tok_specs(tm, D)
    r5 = _row_spec(WA)
    full = lambda shape: pl.BlockSpec(shape, lambda b, i: (0,) * len(shape))
    return pl.pallas_call(
        body, name="mix_mid_fwd", grid=(nb, s // tm),
        out_shape=[jax.ShapeDtypeStruct((nb, s, D), F32), jax.ShapeDtypeStruct((nb, s, WB), F32),
                   jax.ShapeDtypeStruct((nb, s, D), BF16), jax.ShapeDtypeStruct((nb, s, D), F32)],
        in_specs=[tok, t5, t5, t5, t5, _halo_prev_spec(tm), _halo_prev_spec(tm), _mod_spec(), r5, r5,
                  full((CH, NH * CH)), full((CH, WA)), full((HALO, WB)), r5, r5, r5, r5, r5, VMEM_FULL, _row_spec()],
        out_specs=[tok, t5, tok, tok],
        scratch_shapes=[pltpu.VMEM((HALO + tm, WB), F32)],
        compiler_params=_cparams(),
    )(x, u, v, a, g, a, g, gt, gn_g, gn_b, wcat, bspt, conv_w, conv_b, cn_g, cn_b, go_a, go_b, w_mo, g_post)


def _mix_out_bwd(dxo, m, gt, g_post, w_mo):
    nb, s, _ = m.shape
    tm = min(512, s)

    def body(dxo_ref, m_ref, gt_ref, gpost_ref, wmo_ref, dy_ref, dm_ref, dgpost_ref, dgt_ref):
        b, i = pl.program_id(0), pl.program_id(1)
        dxo_v = dxo_ref[0]
        nm, q = _rms(m_ref[0])
        gpost = gpost_ref[...]
        dgt = jnp.sum(dxo_v * (nm * gpost), axis=0, keepdims=True)
        dm, dgpost = _rms_bwd(dxo_v * gt_ref[0], nm, q, gpost)
        dmb = dm.astype(BF16)
        dm_ref[0] = dmb
        dy_ref[0] = _dot_nt(dmb, wmo_ref[...])
        _acc(dgpost_ref, dgpost, _first(b, i))
        _acc(dgt_ref, dgt[None], i == 0)

    tok = _tok_specs(tm, D)
    return pl.pallas_call(
        body, name="mix_out_bwd", grid=(nb, s // tm),
        out_shape=[jax.ShapeDtypeStruct((nb, s, D), F32), jax.ShapeDtypeStruct((nb, s, D), BF16),
                   jax.ShapeDtypeStruct((1, D), F32), jax.ShapeDtypeStruct((nb, 1, D), F32)],
        in_specs=[tok, tok, _mod_spec(), _row_spec(), VMEM_FULL],
        out_specs=[tok, tok, _row_spec(), _mod_spec()],
        compiler_params=_cparams(),
    )(dxo, m, gt, g_post, w_mo)


def _mix_mid_bwd(dy, u, v, conv, gn_g, gn_b, wcat, wcat_t, bspt, cn_g, cn_b, go_a, go_b):
    nb, s, _ = dy.shape
    tm = min(512, s)
    nchunk = tm // CH

    def body(dy_ref, u_ref, v_ref, conv_ref, gng_ref, gnb_ref, wcat_ref, wcatt_ref, bspt_ref, cng_ref, cnb_ref,
             goa_ref, gob_ref,
             du_ref, dv_ref, dconv_ref, dwcat_ref, dbsp_ref, dgng_ref, dgnb_ref, dgoa_ref, dgob_ref,
             dcng_ref, dcnb_ref, dcb_ref):
        first = _first(pl.program_id(0), pl.program_id(1))
        dyv = dy_ref[0]
        xhat, rstd = _ln(v_ref[0])
        gng = gng_ref[...]
        vb = (xhat * gng + gnb_ref[...]).astype(BF16)
        wsb = _spatial_weights(wcat_ref, False)
        wsb_t = _spatial_weights(wcatt_ref, True)
        bias = _spatial_bias(bspt_ref)
        masks = _head_mask((CH, WA))
        vexps, zs = [], []
        for cidx in range(nchunk):
            vexp = _expand_heads(vb[cidx * CH:(cidx + 1) * CH, :], masks)
            vexps.append(vexp)
            zs.append(_dot(wsb, vexp) + bias)
        z = jnp.concatenate(zs, axis=0)
        uv = u_ref[0]
        na, ra = _rms(uv * z)
        dya, dgoa = _rms_bwd(dyv[:, 0:WA], na, ra, goa_ref[...])
        du_ref[0] = dya * z
        dz = dya * uv
        dwcat = jnp.zeros((CH, NH * CH), F32)
        dzsum = jnp.zeros((CH, WA), F32)
        dvlns = []
        for cidx in range(nchunk):
            dzc = dz[cidx * CH:(cidx + 1) * CH, :]
            dzsum = dzsum + dzc
            dzb = dzc.astype(BF16)
            dwcat = dwcat + _dot_nt(dzb, vexps[cidx])
            dvexp = _dot(wsb_t, dzb)
            dvl = jnp.zeros((CH, WA), F32)
            for h in range(NH):
                dvl = dvl + jnp.where(masks[h], dvexp[h * CH:(h + 1) * CH, :], 0.0)
            dvlns.append(dvl)
        dvln = jnp.concatenate(dvlns, axis=0)
        dv, dgng, dgnb = _ln_bwd(dvln, xhat, rstd, gng)
        dv_ref[0] = dv
        lane = lax.broadcasted_iota(jnp.int32, (NH, WA), 1)
        head = lax.broadcasted_iota(jnp.int32, (NH, WA), 0)
        sel = jnp.where((lane >= head * HD) & (lane < (head + 1) * HD), 1.0, 0.0).astype(F32)
        dbsp = lax.dot_general(sel, dzsum, NT, preferred_element_type=F32, precision=lax.Precision.HIGHEST)
        chat, crstd = _ln(conv_ref[0])
        cng = cng_ref[...]
        cln = chat * cng + cnb_ref[...]
        sg = _sigmoid(cln)
        nbb, rb = _rms(cln * sg)
        dyb, dgob = _rms_bwd(dyv[:, WA:D], nbb, rb, gob_ref[...])
        dconv, dcng, dcnb = _ln_bwd(dyb * _dsilu(cln, sg), chat, crstd, cng)
        dconv_ref[0] = dconv
        dcb = jnp.sum(dconv, axis=0, keepdims=True)
        for ref, val in ((dwcat_ref, dwcat), (dbsp_ref, dbsp), (dgng_ref, dgng), (dgnb_ref, dgnb), (dgoa_ref, dgoa),
                         (dgob_ref, dgob), (dcng_ref, dcng), (dcnb_ref, dcnb), (dcb_ref, dcb)):
            _acc(ref, val, first)

    t5 = _tok_specs(tm, WA)
    r5 = _row_spec(WA)
    full = lambda shape: pl.BlockSpec(shape, lambda b, i: (0,) * len(shape))
    big = jax.ShapeDtypeStruct((nb, s, WA), F32)
    row = jax.ShapeDtypeStruct((1, WA), F32)
    return pl.pallas_call(
        body, name="mix_mid_bwd", grid=(nb, s // tm),
        out_shape=[big, big, big, jax.ShapeDtypeStruct((CH, NH * CH), F32), jax.ShapeDtypeStruct((NH, CH), F32),
                   row, row, row, row, row, row, row],
        in_specs=[_tok_specs(tm, D), t5, t5, t5, r5, r5, full((CH, NH * CH)), full((NH * CH, CH)), full((CH, WA)),
                  r5, r5, r5, r5],
        out_specs=[t5, t5, t5, full((CH, NH * CH)), full((NH, CH)), r5, r5, r5, r5, r5, r5, r5],
        compiler_params=_cparams(),
    )(dy, u, v, conv, gn_g, gn_b, wcat, wcat_t, bspt, cn_g, cn_b, go_a, go_b)


def _mix_in_bwd(dxo, x, du, dv, dconv, a, g, sh, sc, g_pre, w_mi4, conv_w):
    nb, s, _ = x.shape
    tm = min(512, s)
    n_i = s // tm

    def body(dxo_ref, x_ref, du_ref, dv_ref, dc_ref, dch_ref, a_ref, g_ref, ah_ref, gh_ref, sh_ref, sc_ref,
             gpre_ref, w_ref, cw_ref,
             dx_ref, dproj_ref, h_ref, dgpre_ref, dsh_ref, dsc_ref, dcw_ref, dext_ref, gext_ref):
        b, i = pl.program_id(0), pl.program_id(1)
        first = _first(b, i)
        av, gv = a_ref[0], g_ref[0]
        sg = _sigmoid(gv)
        dconv = dc_ref[0]
        dext_ref[0:tm, :] = dconv
        dext_ref[tm:tm + HALO, :] = dch_ref[0] * jnp.where(i == n_i - 1, 0.0, 1.0).astype(F32)
        gext_ref[0:HALO, :] = (ah_ref[0] * _sigmoid(gh_ref[0])) * jnp.where(i == 0, 0.0, 1.0).astype(F32)
        gext_ref[HALO:HALO + tm, :] = av * sg
        dglu = _conv_taps(dext_ref, cw_ref, tm, lambda k: CK - 1 - k)

        @pl.when(first)
        def _():
            dcw_ref[...] = jnp.zeros((HALO, WB), F32)

        for k in range(CK):
            lo = k + HALO - (CK - 1)
            dcw_ref[k:k + 1, :] += jnp.sum(dconv * gext_ref[lo:lo + tm, :], axis=0, keepdims=True)
        da = dglu * sg
        dg = dglu * av * (sg * (1.0 - sg))
        parts = [du_ref[0].astype(BF16), dv_ref[0].astype(BF16), da.astype(BF16), dg.astype(BF16)]
        dh = jnp.zeros((tm, D), F32)
        for k in range(4):
            dproj_ref[0, :, k * WA:(k + 1) * WA] = parts[k]
            dh = dh + _dot_nt(parts[k], w_ref[k])
        n, r = _rms(x_ref[0])
        gpre = gpre_ref[...]
        ng = n * gpre
        scale1 = 1.0 + sc_ref[0]
        h_ref[0] = (ng * scale1 + sh_ref[0]).astype(BF16)
        dsh = jnp.sum(dh, axis=0, keepdims=True)
        dsc = jnp.sum(dh * ng, axis=0, keepdims=True)
        dxn, dgpre = _rms_bwd(dh * scale1, n, r, gpre)
        dx_ref[0] = dxo_ref[0] + dxn
        _acc(dgpre_ref, dgpre, first)
        _acc(dsh_ref, dsh[None], i == 0)
        _acc(dsc_ref, dsc[None], i == 0)

    tok = _tok_specs(tm, D)
    t5 = _tok_specs(tm, WA)
    full = lambda shape: pl.BlockSpec(shape, lambda b, i: (0,) * len(shape))
    mod_shape = jax.ShapeDtypeStruct((nb, 1, D), F32)
    return pl.pallas_call(
        body, name="mix_in_bwd", grid=(nb, n_i),
        out_shape=[jax.ShapeDtypeStruct((nb, s, D), F32), jax.ShapeDtypeStruct((nb, s, 4 * WA), BF16),
                   jax.ShapeDtypeStruct((nb, s, D), BF16), jax.ShapeDtypeStruct((1, D), F32), mod_shape, mod_shape,
                   jax.ShapeDtypeStruct((HALO, WB), F32)],
        in_specs=[tok, tok, t5, t5, t5, _halo_next_spec(tm, s), t5, t5, _halo_prev_spec(tm), _halo_prev_spec(tm),
                  _mod_spec(), _mod_spec(), _row_spec(), VMEM_FULL, full((HALO, WB))],
        out_specs=[tok, _tok_specs(tm, 4 * WA), tok, _row_spec(), _mod_spec(), _mod_spec(), full((HALO, WB))],
        scratch_shapes=[pltpu.VMEM((tm + HALO, WB), F32), pltpu.VMEM((HALO + tm, WB), F32)],
        compiler_params=_cparams(),
    )(dxo, x, du, dv, dconv, dconv, a, g, a, g, sh, sc, g_pre, w_mi4, conv_w)


def _row_tile(rows, cols):
    best = 8
    for t in range(8, rows + 1, 8):
        if rows % t == 0 and t * cols * 4 <= 1536 * 1024:
            best = t
    return best


def _sum4(name, own, recv):
    rows, cols = own.shape
    tr = _row_tile(rows, cols)

    def body(own_ref, recv_ref, o_ref):
        acc = own_ref[...]
        for k in range(3):
            acc = acc + recv_ref[k].astype(F32)
        o_ref[...] = acc

    return pl.pallas_call(
        body, name=name, grid=(rows // tr,),
        in_specs=[pl.BlockSpec((tr, cols), lambda i: (i, 0)), pl.BlockSpec((3, tr, cols), lambda i: (0, i, 0))],
        out_specs=pl.BlockSpec((tr, cols), lambda i: (i, 0)),
        out_shape=jax.ShapeDtypeStruct((rows, cols), F32),
        compiler_params=_cparams(),
    )(own, recv)


def _adam_big(name, w, m, v, ga, gb):
    rows, cols = w.shape
    tr = _row_tile(rows, cols)

    def body(w_ref, m_ref, v_ref, ga_ref, gb_ref, g_out, d_out, m_out, v_out):
        gsum = ga_ref[...] + gb_ref[...]
        delta, m2, v2 = _adam(w_ref[...], gsum, m_ref[...], v_ref[...])
        g_out[...] = gsum
        d_out[...] = delta
        m_out[...] = m2
        v_out[...] = v2

    spec = pl.BlockSpec((tr, cols), lambda i: (i, 0))
    shape = jax.ShapeDtypeStruct((rows, cols), F32)
    return pl.pallas_call(
        body, name=name, grid=(rows // tr,), out_shape=[shape] * 4,
        in_specs=[spec] * 5, out_specs=[spec] * 4, compiler_params=_cparams(),
    )(w, m, v, ga, gb)


PK_VEC = 0
PK_LOSS = 6
PK_PAIR = 8
PK_BSP = 16
PK_WCAT = 24
PK_ROWS = PK_WCAT + CH
PAIR_ORDER = ("gmlp_norm_g", "gmlp_norm_b", "conv_b", "conv_norm_g", "conv_norm_b", "g_out_a", "g_out_b")
VEC_ORDER = ("g_pre_f1", "g_post_f1", "g_pre_m", "g_post_m", "g_pre_f2", "g_post_f2")


def _pack_small(vecs, pairs, dbsp, dwcat, lsum, behind):
    def body(*refs):
        vec_refs = refs[:6]
        pair_refs = refs[6:13]
        dbsp_ref, dwcat_ref, lsum_ref, _, o_ref = refs[13:]
        o_ref[0:PK_WCAT, :] = jnp.zeros((PK_WCAT, D), F32)
        o_ref[PK_LOSS:PK_LOSS + 1, 0:128] = lsum_ref[...]
        for k, r in enumerate(vec_refs):
            o_ref[PK_VEC + k:PK_VEC + k + 1, :] = r[...]
        for k, r in enumerate(pair_refs):
            row, half = PK_PAIR + k // 2, k % 2
            o_ref[row:row + 1, half * WA:(half + 1) * WA] = r[...]
        o_ref[PK_BSP:PK_BSP + NH, 0:CH] = dbsp_ref[...]
        o_ref[PK_WCAT:PK_ROWS, :] = dwcat_ref[...]

    args = list(vecs) + list(pairs) + [dbsp, dwcat, lsum]
    return pl.pallas_call(
        body, name="pack_small", out_shape=jax.ShapeDtypeStruct((PK_ROWS, D), F32),
        in_specs=[VMEM_FULL] * len(args) + [ANY], out_specs=VMEM_FULL, compiler_params=_cparams(),
    )(*args, behind)


def _small_adam(pack_all, dcw_all, dada_all, params):
    names = list(VEC_ORDER) + list(PAIR_ORDER) + ["b_spatial", "w_spatial", "conv_w", "b_ada"]
    flat = []
    for nm in names:
        flat += list(params[nm])
    n_in = 3 + len(flat)

    def body(*refs):
        pack_ref, dcw_ref, dada_ref = refs[:3]
        prm = refs[3:n_in]
        outs = refs[n_in:]

        def total(r0, nr, c0, nc):
            acc = pack_ref[0, r0:r0 + nr, c0:c0 + nc]
            for d in range(1, NDEV):
                acc = acc + pack_ref[d, r0:r0 + nr, c0:c0 + nc]
            return acc

        def emit(idx, g, getw, put):
            w_ref, m_ref, v_ref = prm[3 * idx:3 * idx + 3]
            delta, m2, v2 = _adam(getw(w_ref), g, getw(m_ref), getw(v_ref))
            for o_ref, val in zip(outs[4 * idx:4 * idx + 4], (g, delta, m2, v2)):
                put(o_ref, val)

        def whole(ref):
            return ref[...]

        def put_whole(ref, val):
            ref[...] = val

        idx = 0
        for k in range(6):
            emit(idx, total(PK_VEC + k, 1, 0, D), whole, put_whole)
            idx += 1
        for k in range(7):
            emit(idx, total(PK_PAIR + k // 2, 1, (k % 2) * WA, WA), whole, put_whole)
            idx += 1
        emit(idx, total(PK_BSP, NH, 0, CH), lambda r: r[0], lambda r, val: r.__setitem__(0, val))
        idx += 1
        row = lax.broadcasted_iota(jnp.int32, (CH, CH), 0)
        col = lax.broadcasted_iota(jnp.int32, (CH, CH), 1)
        for h in range(NH):
            gh = jnp.where(col <= row, total(PK_WCAT, CH, h * CH, CH), 0.0)
            w_ref, m_ref, v_ref = prm[3 * idx:3 * idx + 3]
            delta, m2, v2 = _adam(w_ref[0, h], gh, m_ref[0, h], v_ref[0, h])
            for o_ref, val in zip(outs[4 * idx:4 * idx + 4], (gh, delta, m2, v2)):
                o_ref[0, h] = val
        idx += 1
        gcw = dcw_ref[0, 0:CK, :]
        for d in range(1, NDEV):
            gcw = gcw + dcw_ref[d, 0:CK, :]
        emit(idx, gcw, lambda r: r[0], lambda r, val: r.__setitem__(0, val))
        idx += 1
        emit(idx, jnp.sum(dada_ref[...], axis=0, keepdims=True), whole, put_whole)
        outs[-1][...] = jnp.sum(total(PK_LOSS, 1, 0, 128), axis=1, keepdims=True) * (0.5 / D)

    out_shape = []
    for nm in names:
        w = params[nm][0]
        out_shape += [jax.ShapeDtypeStruct(w.shape, F32)] * 4
    out_shape.append(jax.ShapeDtypeStruct((1, 1), F32))
    res = pl.pallas_call(
        body, name="small_adam", out_shape=out_shape,
        in_specs=[VMEM_FULL] * n_in, out_specs=[VMEM_FULL] * len(out_shape), compiler_params=_cparams(),
    )(pack_all, dcw_all, dada_all, *flat)
    return {nm: tuple(res[4 * k:4 * k + 4]) for k, nm in enumerate(names)}, res[-1].reshape(())


WEIGHTS = ['w_ada', 'b_ada', 'g_pre_f1', 'g_post_f1', 'w_f1_in', 'w_f1_out', 'g_pre_m', 'g_post_m', 'w_mix_in',
           'gmlp_norm_g', 'gmlp_norm_b', 'w_spatial', 'b_spatial', 'conv_w', 'conv_b', 'conv_norm_g', 'conv_norm_b',
           'g_out_a', 'g_out_b', 'w_mix_out', 'g_pre_f2', 'g_post_f2', 'w_f2_in', 'w_f2_out']
BIG = ('w_f1_in', 'w_f1_out', 'w_mix_in', 'w_mix_out', 'w_f2_in', 'w_f2_out')


def kernel(x, c, w_ada, b_ada, g_pre_f1, g_post_f1, w_f1_in, w_f1_out, g_pre_m, g_post_m, w_mix_in, gmlp_norm_g, gmlp_norm_b, w_spatial, b_spatial, conv_w, conv_b, conv_norm_g, conv_norm_b, g_out_a, g_out_b, w_mix_out, g_pre_f2, g_post_f2, w_f2_in, w_f2_out, loss_target, m_w_ada, m_b_ada, m_g_pre_f1, m_g_post_f1, m_w_f1_in, m_w_f1_out, m_g_pre_m, m_g_post_m, m_w_mix_in, m_gmlp_norm_g, m_gmlp_norm_b, m_w_spatial, m_b_spatial, m_conv_w, m_conv_b, m_conv_norm_g, m_conv_norm_b, m_g_out_a, m_g_out_b, m_w_mix_out, m_g_pre_f2, m_g_post_f2, m_w_f2_in, m_w_f2_out, v_w_ada, v_b_ada, v_g_pre_f1, v_g_post_f1, v_w_f1_in, v_w_f1_out, v_g_pre_m, v_g_post_m, v_w_mix_in, v_gmlp_norm_g, v_gmlp_norm_b, v_w_spatial, v_b_spatial, v_conv_w, v_conv_b, v_conv_norm_g, v_conv_norm_b, v_g_out_a, v_g_out_b, v_w_mix_out, v_g_pre_f2, v_g_post_f2, v_w_f2_in, v_w_f2_out):
    env = dict(locals())
    wts = {n: env[n] for n in WEIGHTS}
    mom = {n: env["m_" + n] for n in WEIGHTS}
    var = {n: env["v_" + n] for n in WEIGHTS}
    nb, s, _ = x.shape
    t = nb * s
    ax, ay, ac = lax.axis_index("x"), lax.axis_index("y"), lax.axis_index("c")
    j_chip = 2 * ax + ay
    dev = 4 * ax + 2 * ay + ac

    groups = (("w_f1_in", "w_f1_out"), ("w_mix_in", "w_mix_out"), ("w_f2_in", "w_f2_out"))
    gather = []

    def gather_start(gi, behind):
        srcs = [wts[n][0].astype(BF16) for n in groups[gi]] + ([conv_w[0]] if gi == 1 else [])
        lands = [lax.empty((NCHIP,) + a.shape, a.dtype) for a in srcs]
        ssem, rsem, srcs, lands, token = _split_start("gw_start%d" % gi, srcs, lands, _gather_plan(len(srcs)),
                                                      3 * len(srcs), behind)
        gather.append((srcs, lands, ssem, rsem))
        return token

    def gathered(gi, behind):
        srcs, lands, ssem, rsem = gather[gi]
        return _split_wait("gw_wait%d" % gi, srcs, lands, ssem, rsem, _gather_plan(len(srcs)), behind)

    (c_all8,) = _allgather8("gather_c", [c.reshape(8, (nb * D) // 8)])
    token = gather_start(0, c_all8)
    c_all = c_all8.reshape(NDEV * nb, D) + token[0, 0]
    b_sh = lax.dynamic_slice(b_ada, (0, j_chip * ADA_SH), (1, ADA_SH))
    ada_sh = _ada_fwd(c_all, w_ada[0], b_sh)
    (ada4,) = _chip_allgather("gather_ada", [ada_sh])
    token = gather_start(1, ada4)
    token = gather_start(2, token)
    ada4 = ada4 + token[0:1, 0:1]
    ada_me = lax.dynamic_slice(ada4, (0, dev * nb, 0), (NCHIP, nb, ADA_SH))
    ada_me = jnp.transpose(ada_me, (1, 0, 2)).reshape(nb, NMOD * D)
    sh1, sc1, gt1, sh2, sc2, gt2, sh3, sc3, gt3 = [ada_me[:, k * D:(k + 1) * D].reshape(nb, 1, D) for k in range(NMOD)]

    wcat = jnp.transpose(w_spatial[0], (1, 0, 2)).reshape(CH, NH * CH)
    wcat_t = jnp.transpose(w_spatial[0], (0, 2, 1)).reshape(NH * CH, CH)
    bspt = jnp.repeat(b_spatial[0].T, HD, axis=1)

    w1i, w1o = gathered(0, sh1)
    w1o = w1o.reshape(DFF, D)
    x1, f1, p1 = _ffn_fwd(x, sh1, sc1, gt1, g_pre_f1, g_post_f1, w1i, w1o)
    wmi, wmo, cw4 = gathered(1, x1)
    wmo = wmo.reshape(D, D)
    cw_full = jnp.transpose(cw4, (1, 0, 2)).reshape(CK, WB)
    cw_pad = jnp.pad(cw_full, ((0, HALO - CK), (0, 0)))
    u, v, a, g = _mix_in_fwd(x1, sh2, sc2, g_pre_m, wmi)
    x2, conv, yb, m = _mix_mid_fwd(x1, u, v, a, g, gt2, gmlp_norm_g, gmlp_norm_b, wcat, bspt, cw_pad, conv_b,
                                   conv_norm_g, conv_norm_b, g_out_a, g_out_b, wmo, g_post_m)
    w2i, w2o = gathered(2, x2)
    w2o = w2o.reshape(DFF, D)
    dx3, f2, p2, lsum = _ffn_fwd(x2, sh3, sc3, gt3, g_pre_f2, g_post_f2, w2i, w2o, target=loss_target)

    def chip4(pair, rows):
        return [arr.reshape(NCHIP, rows, arr.shape[-1]) for arr in pair]

    scatter = {}

    def scatter_start(tag, pairs, behind):
        srcs = [p[1] for p in pairs]
        lands = [lax.empty((3,) + a.shape[1:], a.dtype) for a in srcs]
        ssem, rsem, srcs, lands, token = _split_start("gs_start_" + tag, srcs, lands, _scatter_plan(len(srcs)),
                                                      3 * len(srcs), behind)
        scatter[tag] = (srcs, lands, ssem, rsem)
        return token

    def reduce_and_update(tag, names, pairs, behind):
        srcs, lands, ssem, rsem = scatter[tag]
        recv = _split_wait("gs_wait_" + tag, srcs, lands, ssem, rsem, _scatter_plan(len(srcs)), behind)
        part = [_sum4("sum4_" + n, lax.dynamic_index_in_dim(pairs[k][0], j_chip, 0, keepdims=False), recv[k])
                for k, n in enumerate(names)]
        other = _sibling_swap("swap_" + tag, part)
        for k, n in enumerate(names):
            out[n] = tuple(r[None] for r in _adam_big("adam_" + n, wts[n][0], mom[n][0], var[n][0], part[k], other[k]))

    out = {}
    dx2, dp2, h3, a2, df2, dg_pre_f2, dg_post_f2, dsh3, dsc3, dgt3 = _ffn_bwd(
        dx3, x2, f2, p2, sh3, sc3, gt3, g_pre_f2, g_post_f2, w2i, w2o)
    gw2i = _wgrad("wgrad_f2_in", h3.reshape(t, D), dp2.reshape(t, 2 * DFF), 2 * DFF // NCHIP, True)
    gw2o = chip4(_wgrad("wgrad_f2_out", a2.reshape(t, DFF), df2.reshape(t, D), D // 2, False), DFF // NCHIP)
    tok = scatter_start("f2", [gw2i, gw2o], dg_post_f2)
    dy, dm, dg_post_m, dgt2 = _mix_out_bwd(dx2, m, gt2 + tok[0, 0], g_post_m, wmo)
    gwmo = chip4(_wgrad("wgrad_mix_out", yb.reshape(t, D), dm.reshape(t, D), D // 2, False), D // NCHIP)
    (du, dv, dconv, dwcat, dbsp, dgn_g, dgn_b, dgo_a, dgo_b, dcn_g, dcn_b, dcb) = _mix_mid_bwd(
        dy, u, v, conv, gmlp_norm_g, gmlp_norm_b, wcat, wcat_t, bspt, conv_norm_g, conv_norm_b, g_out_a, g_out_b)
    dx1, dproj, h2, dg_pre_m, dsh2, dsc2, dcw = _mix_in_bwd(dx2, x1, du, dv, dconv, a, g, sh2, sc2, g_pre_m, wmi, cw_pad)
    gwmi = _wgrad("wgrad_mix_in", h2.reshape(t, D), dproj.reshape(t, 4 * WA), WA, True)
    tok = scatter_start("mix", [gwmi, gwmo], dg_pre_m)
    grad_x, dp1, h1, a1, df1, dg_pre_f1, dg_post_f1, dsh1, dsc1, dgt1 = _ffn_bwd(
        dx1, x, f1, p1, sh1 + tok[0, 0], sc1, gt1, g_pre_f1, g_post_f1, w1i, w1o)
    gw1i = _wgrad("wgrad_f1_in", h1.reshape(t, D), dp1.reshape(t, 2 * DFF), 2 * DFF // NCHIP, True)
    gw1o = chip4(_wgrad("wgrad_f1_out", a1.reshape(t, DFF), df1.reshape(t, D), D // 2, False), DFF // NCHIP)
    tok = scatter_start("f1", [gw1i, gw1o], dg_post_f1)
    reduce_and_update("f2", ("w_f2_in", "w_f2_out"), [gw2i, gw2o], tok)
    reduce_and_update("mix", ("w_mix_in", "w_mix_out"), [gwmi, gwmo], out["w_f2_out"][3])

    dada = jnp.concatenate([q.reshape(nb, D) for q in (dsh1, dsc1, dgt1, dsh2, dsc2, dgt2, dsh3, dsc3, dgt3)], axis=1)
    vec_grads = dict(g_pre_f1=dg_pre_f1, g_post_f1=dg_post_f1, g_pre_m=dg_pre_m, g_post_m=dg_post_m,
                     g_pre_f2=dg_pre_f2, g_post_f2=dg_post_f2)
    pair_grads = dict(gmlp_norm_g=dgn_g, gmlp_norm_b=dgn_b, conv_b=dcb, conv_norm_g=dcn_g, conv_norm_b=dcn_b,
                      g_out_a=dgo_a, g_out_b=dgo_b)
    pack = _pack_small([vec_grads[n] for n in VEC_ORDER], [pair_grads[n] for n in PAIR_ORDER], dbsp, dwcat, lsum,
                       out["w_mix_out"][3])
    pack_all, dcw_all, dada_all8 = _allgather8("gather_small", [pack, dcw, dada.reshape(8, (nb * NMOD * D) // 8)])
    dada_all = dada_all8.reshape(NDEV * nb, NMOD * D)
    dcw_mine = lax.dynamic_slice(dcw_all, (0, 0, j_chip * (WB // NCHIP)), (NDEV, HALO, WB // NCHIP))
    small = {n: (wts[n], mom[n], var[n]) for n in list(VEC_ORDER) + list(PAIR_ORDER) + ["b_spatial", "w_spatial", "conv_w", "b_ada"]}
    small_out, loss = _small_adam(pack_all, dcw_mine, dada_all, small)
    out.update(small_out)
    dada_sh = lax.dynamic_slice(dada_all, (0, j_chip * ADA_SH), (NDEV * nb, ADA_SH))
    out["w_ada"] = tuple(r[None] for r in _ada_bwd_adam(c_all, dada_sh, w_ada[0], m_w_ada[0], v_w_ada[0]))
    reduce_and_update("f1", ("w_f1_in", "w_f1_out"), [gw1i, gw1o], out["w_ada"][3])

    res = [loss, grad_x]
    for k in range(4):
        res += [out[n][k] for n in WEIGHTS]
    return tuple(res)
```

```python
import functools

import jax
import jax.numpy as jnp
from jax import lax
from jax.experimental import pallas as pl
from jax.experimental.pallas import tpu as pltpu

D = 1024
DFF = 2816
WA = 512
WB = 512
NH = 8
HD = 64
CH = 128
CK = 31
HALO = 32
NMOD = 9
EPS = 1e-6
NCHIP = 4
NDEV = 8
FBLK = DFF // 2
ADA_SH = NMOD * D // NCHIP

LR, B1, B2, EPS_A, WD, STEP = 0.001, 0.9, 0.999, 1e-08, 0.01, 10

F32 = jnp.float32
BF16 = jnp.bfloat16
MESH = pl.DeviceIdType.MESH
ANY = pl.BlockSpec(memory_space=pl.ANY)
VMEM_FULL = pl.BlockSpec(memory_space=pltpu.VMEM)
VMEM_LIMIT = 56 * 1024 * 1024

NT = (((1,), (1,)), ((), ()))
TN = (((0,), (0,)), ((), ()))


def _dot(a, b):
    return jnp.dot(a, b, preferred_element_type=F32)


def _dot_nt(a, b):
    return lax.dot_general(a, b, NT, preferred_element_type=F32)


def _dot_tn(a, b):
    return lax.dot_general(a, b, TN, preferred_element_type=F32)


def _cparams():
    return pltpu.CompilerParams(vmem_limit_bytes=VMEM_LIMIT)


def _allgather8(name, arrs):
    n = len(arrs)

    def body(*refs):
        ins, outs = refs[:n], refs[n:2 * n]
        send_sems, recv_sems, local_sems = refs[2 * n:]
        x, y, c = lax.axis_index("x"), lax.axis_index("y"), lax.axis_index("c")
        me, sibling = (x, y, c), (x, y, 1 - c)
        chips = [(1 - x, y), (x, 1 - y), (1 - x, 1 - y)]

        def copy(a, k, block, to, src=None):
            rows = outs[a].at[4 * block[0] + 2 * block[1] + block[2]]
            return pltpu.make_async_remote_copy(
                src_ref=rows if src is None else src, dst_ref=rows,
                send_sem=send_sems.at[a, k], recv_sem=recv_sems.at[a, k],
                device_id=to, device_id_type=MESH)

        started, mine = [], []
        for a in range(n):
            loc = pltpu.make_async_copy(ins[a], outs[a].at[4 * x + 2 * y + c], local_sems.at[a])
            loc.start()
            mine.append(loc)
            first = [copy(a, 0, me, sibling, src=ins[a])]
            first += [copy(a, 1 + j, me, (*chip, c), src=ins[a]) for j, chip in enumerate(chips)]
            for cp in first:
                cp.start()
            started += first
        for a in range(n):
            for j, chip in enumerate(chips):
                copy(a, 1 + j, (*chip, c), me).wait_recv()
                fwd = copy(a, 4 + j, (*chip, c), sibling)
                fwd.start()
                started.append(fwd)
        for a in range(n):
            copy(a, 0, sibling, me).wait_recv()
            for j, chip in enumerate(chips):
                copy(a, 4 + j, (*chip, 1 - c), me).wait_recv()
        for cp in started:
            cp.wait_send()
        for loc in mine:
            loc.wait()

    return pl.pallas_call(
        body, name=name,
        out_shape=[jax.ShapeDtypeStruct((NDEV,) + a.shape, a.dtype) for a in arrs],
        in_specs=[ANY] * n, out_specs=[ANY] * n,
        scratch_shapes=[pltpu.SemaphoreType.DMA((n, 7)), pltpu.SemaphoreType.DMA((n, 7)),
                        pltpu.SemaphoreType.DMA((n,))],
    )(*arrs)


def _chip_relations(x, y):
    return [(1 - x, y), (x, 1 - y), (1 - x, 1 - y)]


def _exchange(name, arrs, out_shapes, plan):
    n = len(arrs)
    n_out = len(out_shapes)

    def body(*refs):
        ins, outs = refs[:n], refs[n:n + n_out]
        send_sems, recv_sems, local_sems = refs[n + n_out:]
        x, y, c = lax.axis_index("x"), lax.axis_index("y"), lax.axis_index("c")
        local, sends = plan(x, y, c, ins, outs)
        locs = [pltpu.make_async_copy(s, d, local_sems.at[i]) for i, (s, d) in enumerate(local)]
        for loc in locs:
            loc.start()
        cps = [pltpu.make_async_remote_copy(src_ref=s, dst_ref=d, send_sem=send_sems.at[i], recv_sem=recv_sems.at[i],
                                            device_id=peer, device_id_type=MESH)
               for i, (s, d, peer, _) in enumerate(sends)]
        for cp in cps:
            cp.start()
        for i, (s, _, peer, landing) in enumerate(sends):
            pltpu.make_async_remote_copy(src_ref=s, dst_ref=landing, send_sem=send_sems.at[i], recv_sem=recv_sems.at[i],
                                         device_id=peer, device_id_type=MESH).wait_recv()
        for cp in cps:
            cp.wait_send()
        for loc in locs:
            loc.wait()

    return n, n_out, body


def _run_exchange(name, arrs, out_shapes, plan, n_local, n_send):
    n, n_out, body = _exchange(name, arrs, out_shapes, plan)
    return pl.pallas_call(
        body, name=name, out_shape=out_shapes,
        in_specs=[ANY] * n, out_specs=[ANY] * n_out,
        scratch_shapes=[pltpu.SemaphoreType.DMA((n_send,)), pltpu.SemaphoreType.DMA((n_send,)),
                        pltpu.SemaphoreType.DMA((max(n_local, 1),))],
    )(*arrs)


def _chip_allgather(name, arrs):
    n = len(arrs)

    def plan(x, y, c, ins, outs):
        j_me = 2 * x + y
        local = [(ins[a], outs[a].at[j_me]) for a in range(n)]
        sends = []
        for a in range(n):
            for (px, py) in _chip_relations(x, y):
                sends.append((ins[a], outs[a].at[j_me], (px, py, c), outs[a].at[2 * px + py]))
        return local, sends

    shapes = [jax.ShapeDtypeStruct((NCHIP,) + a.shape, a.dtype) for a in arrs]
    return _run_exchange(name, arrs, shapes, plan, n, 3 * n)


def _chip_scatter(name, arrs):
    n = len(arrs)

    def plan(x, y, c, ins, outs):
        sends = []
        for a in range(n):
            for k, (px, py) in enumerate(_chip_relations(x, y)):
                sends.append((ins[a].at[2 * px + py], outs[a].at[k], (px, py, c), outs[a].at[k]))
        return [], sends

    shapes = [jax.ShapeDtypeStruct((3,) + a.shape[1:], a.dtype) for a in arrs]
    return _run_exchange(name, arrs, shapes, plan, 0, 3 * n)


def _sibling_swap(name, arrs):
    n = len(arrs)

    def plan(x, y, c, ins, outs):
        return [], [(ins[a], outs[a], (x, y, 1 - c), outs[a]) for a in range(n)]

    shapes = [jax.ShapeDtypeStruct(a.shape, a.dtype) for a in arrs]
    return _run_exchange(name, arrs, shapes, plan, 0, n)


HBM = pl.BlockSpec(memory_space=pltpu.HBM)
SEM = pl.BlockSpec(memory_space=pltpu.SEMAPHORE)
EFFECT = pltpu.SideEffectType.DATAFLOW_SIDE_EFFECTING


def _split_start(name, srcs, lands, plan, n_send, after):
    n, nl = len(srcs), len(lands)

    def body(*refs):
        src, land = refs[:n], refs[n:n + nl]
        send_sems, recv_sems = refs[n + nl + 1], refs[n + nl + 2]
        token = refs[-2]
        local_sems = refs[-1]
        x, y, c = lax.axis_index("x"), lax.axis_index("y"), lax.axis_index("c")
        local, sends = plan(x, y, c, src, land)
        locs = [pltpu.make_async_copy(s, d, local_sems.at[i]) for i, (s, d) in enumerate(local)]
        for loc in locs:
            loc.start()
        for loc in locs:
            loc.wait()
        for i, (s, d, peer, _) in enumerate(sends):
            pltpu.make_async_remote_copy(src_ref=s, dst_ref=d, send_sem=send_sems.at[i], recv_sem=recv_sems.at[i],
                                         device_id=peer, device_id_type=MESH).start()
        token[...] = jnp.zeros_like(token)

    thru = [pltpu.HBM(a.shape, a.dtype) for a in list(srcs) + list(lands)]
    res = pl.pallas_call(
        body, name=name,
        out_shape=(pltpu.SemaphoreType.DMA((n_send,)), pltpu.SemaphoreType.DMA((n_send,)), *thru,
                   jax.ShapeDtypeStruct((8, 128), F32)),
        in_specs=[HBM] * (n + nl) + [ANY],
        out_specs=(SEM, SEM, *([HBM] * (n + nl)), pl.BlockSpec(memory_space=pltpu.VMEM)),
        input_output_aliases={i: 2 + i for i in range(n + nl)},
        scratch_shapes=[pltpu.SemaphoreType.DMA((max(len(srcs), 1),))],
        compiler_params=pltpu.CompilerParams(has_side_effects=EFFECT),
    )(*[pltpu.with_memory_space_constraint(a, pltpu.HBM) for a in list(srcs) + list(lands)], after)
    return res[0], res[1], list(res[2:2 + n]), list(res[2 + n:2 + n + nl]), res[-1]


def _split_wait(name, srcs, lands, send_sems, recv_sems, plan, after):
    n, nl = len(srcs), len(lands)

    def body(*refs):
        src, land = refs[:n], refs[n:n + nl]
        send_sems, recv_sems = refs[n + nl], refs[n + nl + 1]
        x, y, c = lax.axis_index("x"), lax.axis_index("y"), lax.axis_index("c")
        _, sends = plan(x, y, c, src, land)
        for i, (s, _, peer, landing) in enumerate(sends):
            cp = pltpu.make_async_remote_copy(src_ref=s, dst_ref=landing, send_sem=send_sems.at[i],
                                              recv_sem=recv_sems.at[i], device_id=peer, device_id_type=MESH)
            cp.wait_send()
            cp.wait_recv()

    thru = [pltpu.HBM(a.shape, a.dtype) for a in list(srcs) + list(lands)]
    res = pl.pallas_call(
        body, name=name, out_shape=tuple(thru),
        in_specs=[HBM] * (n + nl) + [SEM, SEM, ANY], out_specs=tuple([HBM] * (n + nl)),
        input_output_aliases={i: i for i in range(n + nl)},
        compiler_params=pltpu.CompilerParams(has_side_effects=EFFECT),
    )(*srcs, *lands, send_sems, recv_sems, after)
    return list(res[n:])


def _gather_plan(n):
    def plan(x, y, c, src, land):
        j_me = 2 * x + y
        sends = []
        for a in range(n):
            for (px, py) in _chip_relations(x, y):
                sends.append((src[a], land[a].at[j_me], (px, py, c), land[a].at[2 * px + py]))
        return [], sends

    return plan


def _scatter_plan(n):
    def plan(x, y, c, src, land):
        sends = []
        for a in range(n):
            for k, (px, py) in enumerate(_chip_relations(x, y)):
                sends.append((src[a].at[2 * px + py], land[a].at[k], (px, py, c), land[a].at[k]))
        return [], sends

    return plan


def _rms(x):
    r = lax.rsqrt(jnp.mean(x * x, axis=-1, keepdims=True) + EPS)
    return x * r, r


def _rms_bwd(dy, n, r, g):
    dg = jnp.sum(dy * n, axis=0, keepdims=True)
    dn = dy * g
    dx = r * (dn - n * jnp.mean(dn * n, axis=-1, keepdims=True))
    return dx, dg


def _ln(x):
    mu = jnp.mean(x, axis=-1, keepdims=True)
    xc = x - mu
    rstd = lax.rsqrt(jnp.mean(xc * xc, axis=-1, keepdims=True) + EPS)
    return xc * rstd, rstd


def _ln_bwd(dy, xhat, rstd, g):
    dg = jnp.sum(dy * xhat, axis=0, keepdims=True)
    db = jnp.sum(dy, axis=0, keepdims=True)
    dxh = dy * g
    dx = rstd * (dxh - jnp.mean(dxh, axis=-1, keepdims=True) - xhat * jnp.mean(dxh * xhat, axis=-1, keepdims=True))
    return dx, dg, db


def _sigmoid(x):
    return jax.nn.sigmoid(x)


def _dsilu(x, s):
    return s * (1.0 + x * (1.0 - s))


def _adam(w, g, m, v):
    m = B1 * m + (1.0 - B1) * g
    v = B2 * v + (1.0 - B2) * (g * g)
    m_hat = m / (1.0 - B1 ** STEP)
    v_hat = v / (1.0 - B2 ** STEP)
    delta = -LR * (m_hat / (jnp.sqrt(v_hat) + EPS_A) + WD * w)
    return delta, m, v


def _head_mask(shape):
    lane = lax.broadcasted_iota(jnp.int32, shape, len(shape) - 1)
    return [(lane >= h * HD) & (lane < (h + 1) * HD) for h in range(NH)]


def _first(b, i):
    return jnp.logical_and(b == 0, i == 0)


def _acc(ref, val, first):
    @pl.when(first)
    def _():
        ref[...] = val

    @pl.when(jnp.logical_not(first))
    def _():
        ref[...] += val


def _ada_fwd(c_all, w_sh, b_sh):
    nb = c_all.shape[0]
    tn = 768

    def body(c_ref, w_ref, b_ref, o_ref):
        cv = c_ref[...]
        cs = (cv * _sigmoid(cv)).astype(BF16)
        o_ref[...] = _dot(cs, w_ref[...].astype(BF16)) + b_ref[...]

    return pl.pallas_call(
        body, name="ada_fwd", grid=(ADA_SH // tn,),
        out_shape=jax.ShapeDtypeStruct((nb, ADA_SH), F32),
        in_specs=[pl.BlockSpec((nb, D), lambda j: (0, 0)), pl.BlockSpec((D, tn), lambda j: (0, j)),
                  pl.BlockSpec((1, tn), lambda j: (0, j))],
        out_specs=pl.BlockSpec((nb, tn), lambda j: (0, j)),
        compiler_params=_cparams(),
    )(c_all, w_sh, b_sh)


def _ada_bwd_adam(c_all, dada_sh, w, m, v):
    nb = c_all.shape[0]
    tn = 768

    def body(c_ref, d_ref, w_ref, m_ref, v_ref, g_out, d_out, m_out, v_out):
        cv = c_ref[...]
        cs = (cv * _sigmoid(cv)).astype(BF16)
        g = _dot_tn(cs, d_ref[...].astype(BF16))
        delta, m2, v2 = _adam(w_ref[...], g, m_ref[...], v_ref[...])
        g_out[...] = g
        d_out[...] = delta
        m_out[...] = m2
        v_out[...] = v2

    big = pl.BlockSpec((D, tn), lambda j: (0, j))
    shape = jax.ShapeDtypeStruct((D, ADA_SH), F32)
    return pl.pallas_call(
        body, name="ada_bwd_adam", grid=(ADA_SH // tn,),
        out_shape=[shape] * 4,
        in_specs=[pl.BlockSpec((nb, D), lambda j: (0, 0)), pl.BlockSpec((nb, tn), lambda j: (0, j)), big, big, big],
        out_specs=[big] * 4,
        compiler_params=_cparams(),
    )(c_all, dada_sh, w, m, v)


def _tok_specs(tm, width):
    return pl.BlockSpec((1, tm, width), lambda b, i: (b, i, 0))


def _mod_spec():
    return pl.BlockSpec((1, 1, D), lambda b, i: (b, 0, 0))


def _row_spec(width=D):
    return pl.BlockSpec((1, width), lambda b, i: (0, 0))


def _ffn_fwd(x, sh, sc, gt, g_pre, g_post, w_in4, w_out, target=None):
    nb, s, _ = x.shape
    tm = min(256, s)
    with_loss = target is not None

    def body(*refs):
        if with_loss:
            (x_ref, sh_ref, sc_ref, gt_ref, gpre_ref, gpost_ref, win_ref, wout_ref, tgt_ref,
             xo_ref, f_ref, p_ref, ls_ref) = refs
        else:
            (x_ref, sh_ref, sc_ref, gt_ref, gpre_ref, gpost_ref, win_ref, wout_ref,
             xo_ref, f_ref, p_ref) = refs
        xv = x_ref[0]
        n, _ = _rms(xv)
        h = (n * gpre_ref[...]) * (1.0 + sc_ref[0]) + sh_ref[0]
        hb = h.astype(BF16)
        acc = jnp.zeros((tm, D), F32)
        for j in range(2):
            gate = _dot(hb, win_ref[j])
            up = _dot(hb, win_ref[2 + j])
            p_ref[0, :, j * FBLK:(j + 1) * FBLK] = gate.astype(BF16)
            p_ref[0, :, DFF + j * FBLK:DFF + (j + 1) * FBLK] = up.astype(BF16)
            a = (gate * _sigmoid(gate)) * up
            acc = acc + _dot(a.astype(BF16), wout_ref[j * FBLK:(j + 1) * FBLK, :])
        f_ref[0] = acc
        nf, _ = _rms(acc)
        out = xv + (0.5 * gt_ref[0]) * (nf * gpost_ref[...])
        if with_loss:
            err = out - tgt_ref[0]
            xo_ref[0] = err * (1.0 / D)
            row = jnp.sum(err * err, axis=0, keepdims=True)
            part = row[:, 0:128]
            for k in range(1, D // 128):
                part = part + row[:, k * 128:(k + 1) * 128]
            _acc(ls_ref, part, _first(pl.program_id(0), pl.program_id(1)))
        else:
            xo_ref[0] = out

    in_specs = [_tok_specs(tm, D), _mod_spec(), _mod_spec(), _mod_spec(), _row_spec(), _row_spec(), VMEM_FULL, VMEM_FULL]
    args = [x, sh, sc, gt, g_pre, g_post, w_in4, w_out]
    out_shape = [jax.ShapeDtypeStruct((nb, s, D), F32), jax.ShapeDtypeStruct((nb, s, D), F32),
                 jax.ShapeDtypeStruct((nb, s, 2 * DFF), BF16)]
    out_specs = [_tok_specs(tm, D), _tok_specs(tm, D), _tok_specs(tm, 2 * DFF)]
    if with_loss:
        in_specs.append(_tok_specs(tm, D))
        args.append(target)
        out_shape.append(jax.ShapeDtypeStruct((1, 128), F32))
        out_specs.append(pl.BlockSpec((1, 128), lambda b, i: (0, 0)))
    return pl.pallas_call(
        body, name="ffn_loss_fwd" if with_loss else "ffn_fwd", grid=(nb, s // tm),
        out_shape=out_shape, in_specs=in_specs, out_specs=out_specs,
        compiler_params=_cparams(),
    )(*args)


def _ffn_bwd(dxo, x, f, p, sh, sc, gt, g_pre, g_post, w_in4, w_out):
    nb, s, _ = x.shape
    tm = min(256, s)

    def body(dxo_ref, x_ref, f_ref, p_ref, sh_ref, sc_ref, gt_ref, gpre_ref, gpost_ref, win_ref, wout_ref,
             dx_ref, dp_ref, h_ref, a_ref, df_ref, dgpre_ref, dgpost_ref, dsh_ref, dsc_ref, dgt_ref):
        b, i = pl.program_id(0), pl.program_id(1)
        dxo_v = dxo_ref[0]
        nf, q = _rms(f_ref[0])
        gpost = gpost_ref[...]
        dgt = jnp.sum(dxo_v * (0.5 * (nf * gpost)), axis=0, keepdims=True)
        do = dxo_v * (0.5 * gt_ref[0])
        df, dgpost = _rms_bwd(do, nf, q, gpost)
        dfb = df.astype(BF16)
        df_ref[0] = dfb
        xv = x_ref[0]
        n, r = _rms(xv)
        gpre = gpre_ref[...]
        ng = n * gpre
        scale1 = 1.0 + sc_ref[0]
        h = ng * scale1 + sh_ref[0]
        h_ref[0] = h.astype(BF16)
        dh = jnp.zeros((tm, D), F32)
        for j in range(2):
            gate = p_ref[0, :, j * FBLK:(j + 1) * FBLK].astype(F32)
            up = p_ref[0, :, DFF + j * FBLK:DFF + (j + 1) * FBLK].astype(F32)
            sg = _sigmoid(gate)
            act = gate * sg
            a_ref[0, :, j * FBLK:(j + 1) * FBLK] = (act * up).astype(BF16)
            da = _dot_nt(dfb, wout_ref[j * FBLK:(j + 1) * FBLK, :])
            dgate = (da * up * _dsilu(gate, sg)).astype(BF16)
            dup = (da * act).astype(BF16)
            dp_ref[0, :, j * FBLK:(j + 1) * FBLK] = dgate
            dp_ref[0, :, DFF + j * FBLK:DFF + (j + 1) * FBLK] = dup
            dh = dh + _dot_nt(dgate, win_ref[j]) + _dot_nt(dup, win_ref[2 + j])
        dsh = jnp.sum(dh, axis=0, keepdims=True)
        dsc = jnp.sum(dh * ng, axis=0, keepdims=True)
        dxn, dgpre = _rms_bwd(dh * scale1, n, r, gpre)
        dx_ref[0] = dxo_v + dxn
        _acc(dgpre_ref, dgpre, _first(b, i))
        _acc(dgpost_ref, dgpost, _first(b, i))
        _acc(dsh_ref, dsh[None], i == 0)
        _acc(dsc_ref, dsc[None], i == 0)
        _acc(dgt_ref, dgt[None], i == 0)

    tok = _tok_specs(tm, D)
    mod_shape = jax.ShapeDtypeStruct((nb, 1, D), F32)
    row_shape = jax.ShapeDtypeStruct((1, D), F32)
    return pl.pallas_call(
        body, name="ffn_bwd", grid=(nb, s // tm),
        out_shape=[jax.ShapeDtypeStruct((nb, s, D), F32), jax.ShapeDtypeStruct((nb, s, 2 * DFF), BF16),
                   jax.ShapeDtypeStruct((nb, s, D), BF16), jax.ShapeDtypeStruct((nb, s, DFF), BF16),
                   jax.ShapeDtypeStruct((nb, s, D), BF16), row_shape, row_shape, mod_shape, mod_shape, mod_shape],
        in_specs=[tok, tok, tok, _tok_specs(tm, 2 * DFF), _mod_spec(), _mod_spec(), _mod_spec(), _row_spec(), _row_spec(),
                  VMEM_FULL, VMEM_FULL],
        out_specs=[tok, _tok_specs(tm, 2 * DFF), tok, _tok_specs(tm, DFF), tok, _row_spec(), _row_spec(),
                   _mod_spec(), _mod_spec(), _mod_spec()],
        compiler_params=_cparams(),
    )(dxo, x, f, p, sh, sc, gt, g_pre, g_post, w_in4, w_out)


def _wgrad(name, a, b, col_block, chip_major):
    t, ka = a.shape
    n = b.shape[1]
    tk = min(512, t)
    nk = t // tk
    nblk = n // col_block

    def body(a_ref, b_ref, o_ref, obf_ref, acc_ref):
        k = pl.program_id(1)

        @pl.when(k == 0)
        def _():
            acc_ref[...] = jnp.zeros_like(acc_ref)

        acc_ref[...] += _dot_tn(a_ref[...], b_ref[...])

        @pl.when(k == nk - 1)
        def _():
            val = acc_ref[...]
            if chip_major:
                o_ref[0] = val
                obf_ref[0] = val.astype(BF16)
            else:
                o_ref[...] = val
                obf_ref[...] = val.astype(BF16)

    if chip_major:
        shape = (nblk, ka, col_block)
        ospec = pl.BlockSpec((1, ka, col_block), lambda j, k: (j, 0, 0))
    else:
        shape = (ka, n)
        ospec = pl.BlockSpec((ka, col_block), lambda j, k: (0, j))
    return pl.pallas_call(
        body, name=name, grid=(nblk, nk),
        out_shape=[jax.ShapeDtypeStruct(shape, F32), jax.ShapeDtypeStruct(shape, BF16)],
        in_specs=[pl.BlockSpec((tk, ka), lambda j, k: (k, 0)), pl.BlockSpec((tk, col_block), lambda j, k: (k, j))],
        out_specs=[ospec, ospec],
        scratch_shapes=[pltpu.VMEM((ka, col_block), F32)],
        compiler_params=_cparams(),
    )(a, b)


def _mix_in_fwd(x, sh, sc, g_pre, w_mi4):
    nb, s, _ = x.shape
    tm = min(512, s)

    def body(x_ref, sh_ref, sc_ref, gpre_ref, w_ref, u_ref, v_ref, a_ref, g_ref):
        n, _ = _rms(x_ref[0])
        hb = ((n * gpre_ref[...]) * (1.0 + sc_ref[0]) + sh_ref[0]).astype(BF16)
        for k, o_ref in enumerate((u_ref, v_ref, a_ref, g_ref)):
            o_ref[0] = _dot(hb, w_ref[k])

    shape = jax.ShapeDtypeStruct((nb, s, WA), F32)
    return pl.pallas_call(
        body, name="mix_in_fwd", grid=(nb, s // tm),
        out_shape=[shape] * 4,
        in_specs=[_tok_specs(tm, D), _mod_spec(), _mod_spec(), _row_spec(), VMEM_FULL],
        out_specs=[_tok_specs(tm, WA)] * 4,
        compiler_params=_cparams(),
    )(x, sh, sc, g_pre, w_mi4)


def _spatial_weights(wcat_ref, transposed):
    w = wcat_ref[...]
    row = lax.broadcasted_iota(jnp.int32, w.shape, 0)
    col = lax.broadcasted_iota(jnp.int32, w.shape, 1)
    keep = ((row & (CH - 1)) <= col) if transposed else ((col & (CH - 1)) <= row)
    return jnp.where(keep, w, 0.0).astype(BF16)


def _expand_heads(vc, masks):
    return jnp.concatenate([jnp.where(mk, vc, jnp.zeros_like(vc)) for mk in masks], axis=0)


def _spatial_bias(bspt_ref):
    return bspt_ref[...]


def _conv_taps(ext_ref, w_ref, tm, offset):
    acc = jnp.zeros((tm, WB), F32)
    for k in range(CK):
        acc = acc + w_ref[k:k + 1, :] * ext_ref[offset(k):offset(k) + tm, :]
    return acc


def _halo_prev_spec(tm):
    return pl.BlockSpec((1, HALO, WB), lambda b, i: (b, jnp.maximum(i * (tm // HALO) - 1, 0), 0))


def _halo_next_spec(tm, s):
    return pl.BlockSpec((1, HALO, WB), lambda b, i: (b, jnp.minimum((i + 1) * (tm // HALO), s // HALO - 1), 0))


def _mix_mid_fwd(x, u, v, a, g, gt, gn_g, gn_b, wcat, bspt, conv_w, conv_b, cn_g, cn_b, go_a, go_b, w_mo, g_post):
    nb, s, _ = x.shape
    tm = min(512, s)

    def body(x_ref, u_ref, v_ref, a_ref, g_ref, ah_ref, gh_ref, gt_ref, gng_ref, gnb_ref, wcat_ref, bspt_ref,
             cw_ref, cb_ref, cng_ref, cnb_ref, goa_ref, gob_ref, wmo_ref, gpost_ref,
             xo_ref, conv_ref, y_ref, m_ref, ext_ref):
        i = pl.program_id(1)
        xhat, _ = _ln(v_ref[0])
        vb = (xhat * gng_ref[...] + gnb_ref[...]).astype(BF16)
        wsb = _spatial_weights(wcat_ref, False)
        bias = _spatial_bias(bspt_ref)
        masks = _head_mask((CH, WA))
        zs = []
        for cidx in range(tm // CH):
            vexp = _expand_heads(vb[cidx * CH:(cidx + 1) * CH, :], masks)
            zs.append(_dot(wsb, vexp) + bias)
        z = jnp.concatenate(zs, axis=0)
        na, _ = _rms(u_ref[0] * z)
        keep = jnp.where(i == 0, 0.0, 1.0).astype(F32)
        ext_ref[0:HALO, :] = (ah_ref[0] * _sigmoid(gh_ref[0])) * keep
        ext_ref[HALO:HALO + tm, :] = a_ref[0] * _sigmoid(g_ref[0])
        conv = _conv_taps(ext_ref, cw_ref, tm, lambda k: k + HALO - (CK - 1)) + cb_ref[...]
        conv_ref[0] = conv
        chat, _ = _ln(conv)
        cln = chat * cng_ref[...] + cnb_ref[...]
        nbb, _ = _rms(cln * _sigmoid(cln))
        yb = jnp.concatenate([na * goa_ref[...], nbb * gob_ref[...]], axis=1).astype(BF16)
        y_ref[0] = yb
        m = _dot(yb, wmo_ref[...])
        m_ref[0] = m
        nm, _ = _rms(m)
        xo_ref[0] = x_ref[0] + gt_ref[0] * (nm * gpost_ref[...])

    t5 = _tok_specs(tm, WA)
    tok = _tok_specs(tm, D)
    r5 = _row_spec(WA)
    full = lambda shape: pl.BlockSpec(shape, lambda b, i: (0,) * len(shape))
    return pl.pallas_call(
        body, name="mix_mid_fwd", grid=(nb, s // tm),
        out_shape=[jax.ShapeDtypeStruct((nb, s, D), F32), jax.ShapeDtypeStruct((nb, s, WB), F32),
                   jax.ShapeDtypeStruct((nb, s, D), BF16), jax.ShapeDtypeStruct((nb, s, D), F32)],
        in_specs=[tok, t5, t5, t5, t5, _halo_prev_spec(tm), _halo_prev_spec(tm), _mod_spec(), r5, r5,
                  full((CH, NH * CH)), full((CH, WA)), full((HALO, WB)), r5, r5, r5, r5, r5, VMEM_FULL, _row_spec()],
        out_specs=[tok, t5, tok, tok],
        scratch_shapes=[pltpu.VMEM((HALO + tm, WB), F32)],
        compiler_params=_cparams(),
    )(x, u, v, a, g, a, g, gt, gn_g, gn_b, wcat, bspt, conv_w, conv_b, cn_g, cn_b, go_a, go_b, w_mo, g_post)


def _mix_out_bwd(dxo, m, gt, g_post, w_mo):
    nb, s, _ = m.shape
    tm = min(512, s)

    def body(dxo_ref, m_ref, gt_ref, gpost_ref, wmo_ref, dy_ref, dm_ref, dgpost_ref, dgt_ref):
        b, i = pl.program_id(0), pl.program_id(1)
        dxo_v = dxo_ref[0]
        nm, q = _rms(m_ref[0])
        gpost = gpost_ref[...]
        dgt = jnp.sum(dxo_v * (nm * gpost), axis=0, keepdims=True)
        dm, dgpost = _rms_bwd(dxo_v * gt_ref[0], nm, q, gpost)
        dmb = dm.astype(BF16)
        dm_ref[0] = dmb
        dy_ref[0] = _dot_nt(dmb, wmo_ref[...])
        _acc(dgpost_ref, dgpost, _first(b, i))
        _acc(dgt_ref, dgt[None], i == 0)

    tok = _tok_specs(tm, D)
    return pl.pallas_call(
        body, name="mix_out_bwd", grid=(nb, s // tm),
        out_shape=[jax.ShapeDtypeStruct((nb, s, D), F32), jax.ShapeDtypeStruct((nb, s, D), BF16),
                   jax.ShapeDtypeStruct((1, D), F32), jax.ShapeDtypeStruct((nb, 1, D), F32)],
        in_specs=[tok, tok, _mod_spec(), _row_spec(), VMEM_FULL],
        out_specs=[tok, tok, _row_spec(), _mod_spec()],
        compiler_params=_cparams(),
    )(dxo, m, gt, g_post, w_mo)


def _mix_mid_bwd(dy, u, v, conv, gn_g, gn_b, wcat, wcat_t, bspt, cn_g, cn_b, go_a, go_b):
    nb, s, _ = dy.shape
    tm = min(512, s)
    nchunk = tm // CH

    def body(dy_ref, u_ref, v_ref, conv_ref, gng_ref, gnb_ref, wcat_ref, wcatt_ref, bspt_ref, cng_ref, cnb_ref,
             goa_ref, gob_ref,
             du_ref, dv_ref, dconv_ref, dwcat_ref, dbsp_ref, dgng_ref, dgnb_ref, dgoa_ref, dgob_ref,
             dcng_ref, dcnb_ref, dcb_ref):
        first = _first(pl.program_id(0), pl.program_id(1))
        dyv = dy_ref[0]
        xhat, rstd = _ln(v_ref[0])
        gng = gng_ref[...]
        vb = (xhat * gng + gnb_ref[...]).astype(BF16)
        wsb = _spatial_weights(wcat_ref, False)
        wsb_t = _spatial_weights(wcatt_ref, True)
        bias = _spatial_bias(bspt_ref)
        masks = _head_mask((CH, WA))
        vexps, zs = [], []
        for cidx in range(nchunk):
            vexp = _expand_heads(vb[cidx * CH:(cidx + 1) * CH, :], masks)
            vexps.append(vexp)
            zs.append(_dot(wsb, vexp) + bias)
        z = jnp.concatenate(zs, axis=0)
        uv = u_ref[0]
        na, ra = _rms(uv * z)
        dya, dgoa = _rms_bwd(dyv[:, 0:WA], na, ra, goa_ref[...])
        du_ref[0] = dya * z
        dz = dya * uv
        dwcat = jnp.zeros((CH, NH * CH), F32)
        dzsum = jnp.zeros((CH, WA), F32)
        dvlns = []
        for cidx in range(nchunk):
            dzc = dz[cidx * CH:(cidx + 1) * CH, :]
            dzsum = dzsum + dzc
            dzb = dzc.astype(BF16)
            dwcat = dwcat + _dot_nt(dzb, vexps[cidx])
            dvexp = _dot(wsb_t, dzb)
            dvl = jnp.zeros((CH, WA), F32)
            for h in range(NH):
                dvl = dvl + jnp.where(masks[h], dvexp[h * CH:(h + 1) * CH, :], 0.0)
            dvlns.append(dvl)
        dvln = jnp.concatenate(dvlns, axis=0)
        dv, dgng, dgnb = _ln_bwd(dvln, xhat, rstd, gng)
        dv_ref[0] = dv
        lane = lax.broadcasted_iota(jnp.int32, (NH, WA), 1)
        head = lax.broadcasted_iota(jnp.int32, (NH, WA), 0)
        sel = jnp.where((lane >= head * HD) & (lane < (head + 1) * HD), 1.0, 0.0).astype(F32)
        dbsp = lax.dot_general(sel, dzsum, NT, preferred_element_type=F32, precision=lax.Precision.HIGHEST)
        chat, crstd = _ln(conv_ref[0])
        cng = cng_ref[...]
        cln = chat * cng + cnb_ref[...]
        sg = _sigmoid(cln)
        nbb, rb = _rms(cln * sg)
        dyb, dgob = _rms_bwd(dyv[:, WA:D], nbb, rb, gob_ref[...])
        dconv, dcng, dcnb = _ln_bwd(dyb * _dsilu(cln, sg), chat, crstd, cng)
        dconv_ref[0] = dconv
        dcb = jnp.sum(dconv, axis=0, keepdims=True)
        for ref, val in ((dwcat_ref, dwcat), (dbsp_ref, dbsp), (dgng_ref, dgng), (dgnb_ref, dgnb), (dgoa_ref, dgoa),
                         (dgob_ref, dgob), (dcng_ref, dcng), (dcnb_ref, dcnb), (dcb_ref, dcb)):
            _acc(ref, val, first)

    t5 = _tok_specs(tm, WA)
    r5 = _row_spec(WA)
    full = lambda shape: pl.BlockSpec(shape, lambda b, i: (0,) * len(shape))
    big = jax.ShapeDtypeStruct((nb, s, WA), F32)
    row = jax.ShapeDtypeStruct((1, WA), F32)
    return pl.pallas_call(
        body, name="mix_mid_bwd", grid=(nb, s // tm),
        out_shape=[big, big, big, jax.ShapeDtypeStruct((CH, NH * CH), F32), jax.ShapeDtypeStruct((NH, CH), F32),
                   row, row, row, row, row, row, row],
        in_specs=[_tok_specs(tm, D), t5, t5, t5, r5, r5, full((CH, NH * CH)), full((NH * CH, CH)), full((CH, WA)),
                  r5, r5, r5, r5],
        out_specs=[t5, t5, t5, full((CH, NH * CH)), full((NH, CH)), r5, r5, r5, r5, r5, r5, r5],
        compiler_params=_cparams(),
    )(dy, u, v, conv, gn_g, gn_b, wcat, wcat_t, bspt, cn_g, cn_b, go_a, go_b)


def _mix_in_bwd(dxo, x, du, dv, dconv, a, g, sh, sc, g_pre, w_mi4, conv_w):
    nb, s, _ = x.shape
    tm = min(512, s)
    n_i = s // tm

    def body(dxo_ref, x_ref, du_ref, dv_ref, dc_ref, dch_ref, a_ref, g_ref, ah_ref, gh_ref, sh_ref, sc_ref,
             gpre_ref, w_ref, cw_ref,
             dx_ref, dproj_ref, h_ref, dgpre_ref, dsh_ref, dsc_ref, dcw_ref, dext_ref, gext_ref):
        b, i = pl.program_id(0), pl.program_id(1)
        first = _first(b, i)
        av, gv = a_ref[0], g_ref[0]
        sg = _sigmoid(gv)
        dconv = dc_ref[0]
        dext_ref[0:tm, :] = dconv
        dext_ref[tm:tm + HALO, :] = dch_ref[0] * jnp.where(i == n_i - 1, 0.0, 1.0).astype(F32)
        gext_ref[0:HALO, :] = (ah_ref[0] * _sigmoid(gh_ref[0])) * jnp.where(i == 0, 0.0, 1.0).astype(F32)
        gext_ref[HALO:HALO + tm, :] = av * sg
        dglu = _conv_taps(dext_ref, cw_ref, tm, lambda k: CK - 1 - k)

        @pl.when(first)
        def _():
            dcw_ref[...] = jnp.zeros((HALO, WB), F32)

        for k in range(CK):
            lo = k + HALO - (CK - 1)
            dcw_ref[k:k + 1, :] += jnp.sum(dconv * gext_ref[lo:lo + tm, :], axis=0, keepdims=True)
        da = dglu * sg
        dg = dglu * av * (sg * (1.0 - sg))
        parts = [du_ref[0].astype(BF16), dv_ref[0].astype(BF16), da.astype(BF16), dg.astype(BF16)]
        dh = jnp.zeros((tm, D), F32)
        for k in range(4):
            dproj_ref[0, :, k * WA:(k + 1) * WA] = parts[k]
            dh = dh + _dot_nt(parts[k], w_ref[k])
        n, r = _rms(x_ref[0])
        gpre = gpre_ref[...]
        ng = n * gpre
        scale1 = 1.0 + sc_ref[0]
        h_ref[0] = (ng * scale1 + sh_ref[0]).astype(BF16)
        dsh = jnp.sum(dh, axis=0, keepdims=True)
        dsc = jnp.sum(dh * ng, axis=0, keepdims=True)
        dxn, dgpre = _rms_bwd(dh * scale1, n, r, gpre)
        dx_ref[0] = dxo_ref[0] + dxn
        _acc(dgpre_ref, dgpre, first)
        _acc(dsh_ref, dsh[None], i == 0)
        _acc(dsc_ref, dsc[None], i == 0)

    tok = _tok_specs(tm, D)
    t5 = _tok_specs(tm, WA)
    full = lambda shape: pl.BlockSpec(shape, lambda b, i: (0,) * len(shape))
    mod_shape = jax.ShapeDtypeStruct((nb, 1, D), F32)
    return pl.pallas_call(
        body, name="mix_in_bwd", grid=(nb, n_i),
        out_shape=[jax.ShapeDtypeStruct((nb, s, D), F32), jax.ShapeDtypeStruct((nb, s, 4 * WA), BF16),
                   jax.ShapeDtypeStruct((nb, s, D), BF16), jax.ShapeDtypeStruct((1, D), F32), mod_shape, mod_shape,
                   jax.ShapeDtypeStruct((HALO, WB), F32)],
        in_specs=[tok, tok, t5, t5, t5, _halo_next_spec(tm, s), t5, t5, _halo_prev_spec(tm), _halo_prev_spec(tm),
                  _mod_spec(), _mod_spec(), _row_spec(), VMEM_FULL, full((HALO, WB))],
        out_specs=[tok, _tok_specs(tm, 4 * WA), tok, _row_spec(), _mod_spec(), _mod_spec(), full((HALO, WB))],
        scratch_shapes=[pltpu.VMEM((tm + HALO, WB), F32), pltpu.VMEM((HALO + tm, WB), F32)],
        compiler_params=_cparams(),
    )(dxo, x, du, dv, dconv, dconv, a, g, a, g, sh, sc, g_pre, w_mi4, conv_w)


def _row_tile(rows, cols):
    best = 8
    for t in range(8, rows + 1, 8):
        if rows % t == 0 and t * cols * 4 <= 1536 * 1024:
            best = t
    return best


def _sum4(name, own, recv):
    rows, cols = own.shape
    tr = _row_tile(rows, cols)

    def body(own_ref, recv_ref, o_ref):
        acc = own_ref[...]
        for k in range(3):
            acc = acc + recv_ref[k].astype(F32)
        o_ref[...] = acc

    return pl.pallas_call(
        body, name=name, grid=(rows // tr,),
        in_specs=[pl.BlockSpec((tr, cols), lambda i: (i, 0)), pl.BlockSpec((3, tr, cols), lambda i: (0, i, 0))],
        out_specs=pl.BlockSpec((tr, cols), lambda i: (i, 0)),
        out_shape=jax.ShapeDtypeStruct((rows, cols), F32),
        compiler_params=_cparams(),
    )(own, recv)


def _adam_big(name, w, m, v, ga, gb):
    rows, cols = w.shape
    tr = _row_tile(rows, cols)

    def body(w_ref, m_ref, v_ref, ga_ref, gb_ref, g_out, d_out, m_out, v_out):
        gsum = ga_ref[...] + gb_ref[...]
        delta, m2, v2 = _adam(w_ref[...], gsum, m_ref[...], v_ref[...])
        g_out[...] = gsum
        d_out[...] = delta
        m_out[...] = m2
        v_out[...] = v2

    spec = pl.BlockSpec((tr, cols), lambda i: (i, 0))
    shape = jax.ShapeDtypeStruct((rows, cols), F32)
    return pl.pallas_call(
        body, name=name, grid=(rows // tr,), out_shape=[shape] * 4,
        in_specs=[spec] * 5, out_specs=[spec] * 4, compiler_params=_cparams(),
    )(w, m, v, ga, gb)


PK_VEC = 0
PK_LOSS = 6
PK_PAIR = 8
PK_BSP = 16
PK_WCAT = 24
PK_ROWS = PK_WCAT + CH
PAIR_ORDER = ("gmlp_norm_g", "gmlp_norm_b", "conv_b", "conv_norm_g", "conv_norm_b", "g_out_a", "g_out_b")
VEC_ORDER = ("g_pre_f1", "g_post_f1", "g_pre_m", "g_post_m", "g_pre_f2", "g_post_f2")


def _pack_small(vecs, pairs, dbsp, dwcat, lsum, behind):
    def body(*refs):
        vec_refs = refs[:6]
        pair_refs = refs[6:13]
        dbsp_ref, dwcat_ref, lsum_ref, _, o_ref = refs[13:]
        o_ref[0:PK_WCAT, :] = jnp.zeros((PK_WCAT, D), F32)
        o_ref[PK_LOSS:PK_LOSS + 1, 0:128] = lsum_ref[...]
        for k, r in enumerate(vec_refs):
            o_ref[PK_VEC + k:PK_VEC + k + 1, :] = r[...]
        for k, r in enumerate(pair_refs):
            row, half = PK_PAIR + k // 2, k % 2
            o_ref[row:row + 1, half * WA:(half + 1) * WA] = r[...]
        o_ref[PK_BSP:PK_BSP + NH, 0:CH] = dbsp_ref[...]
        o_ref[PK_WCAT:PK_ROWS, :] = dwcat_ref[...]

    args = list(vecs) + list(pairs) + [dbsp, dwcat, lsum]
    return pl.pallas_call(
        body, name="pack_small", out_shape=jax.ShapeDtypeStruct((PK_ROWS, D), F32),
        in_specs=[VMEM_FULL] * len(args) + [ANY], out_specs=VMEM_FULL, compiler_params=_cparams(),
    )(*args, behind)


def _small_adam(pack_all, dcw_all, dada_all, params):
    names = list(VEC_ORDER) + list(PAIR_ORDER) + ["b_spatial", "w_spatial", "conv_w", "b_ada"]
    flat = []
    for nm in names:
        flat += list(params[nm])
    n_in = 3 + len(flat)

    def body(*refs):
        pack_ref, dcw_ref, dada_ref = refs[:3]
        prm = refs[3:n_in]
        outs = refs[n_in:]

        def total(r0, nr, c0, nc):
            acc = pack_ref[0, r0:r0 + nr, c0:c0 + nc]
            for d in range(1, NDEV):
                acc = acc + pack_ref[d, r0:r0 + nr, c0:c0 + nc]
            return acc

        def emit(idx, g, getw, put):
            w_ref, m_ref, v_ref = prm[3 * idx:3 * idx + 3]
            delta, m2, v2 = _adam(getw(w_ref), g, getw(m_ref), getw(v_ref))
            for o_ref, val in zip(outs[4 * idx:4 * idx + 4], (g, delta, m2, v2)):
                put(o_ref, val)

        def whole(ref):
            return ref[...]

        def put_whole(ref, val):
            ref[...] = val

        idx = 0
        for k in range(6):
            emit(idx, total(PK_VEC + k, 1, 0, D), whole, put_whole)
            idx += 1
        for k in range(7):
            emit(idx, total(PK_PAIR + k // 2, 1, (k % 2) * WA, WA), whole, put_whole)
            idx += 1
        emit(idx, total(PK_BSP, NH, 0, CH), lambda r: r[0], lambda r, val: r.__setitem__(0, val))
        idx += 1
        row = lax.broadcasted_iota(jnp.int32, (CH, CH), 0)
        col = lax.broadcasted_iota(jnp.int32, (CH, CH), 1)
        for h in range(NH):
            gh = jnp.where(col <= row, total(PK_WCAT, CH, h * CH, CH), 0.0)
            w_ref, m_ref, v_ref = prm[3 * idx:3 * idx + 3]
            delta, m2, v2 = _adam(w_ref[0, h], gh, m_ref[0, h], v_ref[0, h])
            for o_ref, val in zip(outs[4 * idx:4 * idx + 4], (gh, delta, m2, v2)):
                o_ref[0, h] = val
        idx += 1
        gcw = dcw_ref[0, 0:CK, :]
        for d in range(1, NDEV):
            gcw = gcw + dcw_ref[d, 0:CK, :]
        emit(idx, gcw, lambda r: r[0], lambda r, val: r.__setitem__(0, val))
        idx += 1
        emit(idx, jnp.sum(dada_ref[...], axis=0, keepdims=True), whole, put_whole)
        outs[-1][...] = jnp.sum(total(PK_LOSS, 1, 0, 128), axis=1, keepdims=True) * (0.5 / D)

    out_shape = []
    for nm in names:
        w = params[nm][0]
        out_shape += [jax.ShapeDtypeStruct(w.shape, F32)] * 4
    out_shape.append(jax.ShapeDtypeStruct((1, 1), F32))
    res = pl.pallas_call(
        body, name="small_adam", out_shape=out_shape,
        in_specs=[VMEM_FULL] * n_in, out_specs=[VMEM_FULL] * len(out_shape), compiler_params=_cparams(),
    )(pack_all, dcw_all, dada_all, *flat)
    return {nm: tuple(res[4 * k:4 * k + 4]) for k, nm in enumerate(names)}, res[-1].reshape(())


WEIGHTS = ['w_ada', 'b_ada', 'g_pre_f1', 'g_post_f1', 'w_f1_in', 'w_f1_out', 'g_pre_m', 'g_post_m', 'w_mix_in',
           'gmlp_norm_g', 'gmlp_norm_b', 'w_spatial', 'b_spatial', 'conv_w', 'conv_b', 'conv_norm_g', 'conv_norm_b',
           'g_out_a', 'g_out_b', 'w_mix_out', 'g_pre_f2', 'g_post_f2', 'w_f2_in', 'w_f2_out']
BIG = ('w_f1_in', 'w_f1_out', 'w_mix_in', 'w_mix_out', 'w_f2_in', 'w_f2_out')


def kernel(x, c, w_ada, b_ada, g_pre_f1, g_post_f1, w_f1_in, w_f1_out, g_pre_m, g_post_m, w_mix_in, gmlp_norm_g, gmlp_norm_b, w_spatial, b_spatial, conv_w, conv_b, conv_norm_g, conv_norm_b, g_out_a, g_out_b, w_mix_out, g_pre_f2, g_post_f2, w_f2_in, w_f2_out, loss_target, m_w_ada, m_b_ada, m_g_pre_f1, m_g_post_f1, m_w_f1_in, m_w_f1_out, m_g_pre_m, m_g_post_m, m_w_mix_in, m_gmlp_norm_g, m_gmlp_norm_b, m_w_spatial, m_b_spatial, m_conv_w, m_conv_b, m_conv_norm_g, m_conv_norm_b, m_g_out_a, m_g_out_b, m_w_mix_out, m_g_pre_f2, m_g_post_f2, m_w_f2_in, m_w_f2_out, v_w_ada, v_b_ada, v_g_pre_f1, v_g_post_f1, v_w_f1_in, v_w_f1_out, v_g_pre_m, v_g_post_m, v_w_mix_in, v_gmlp_norm_g, v_gmlp_norm_b, v_w_spatial, v_b_spatial, v_conv_w, v_conv_b, v_conv_norm_g, v_conv_norm_b, v_g_out_a, v_g_out_b, v_w_mix_out, v_g_pre_f2, v_g_post_f2, v_w_f2_in, v_w_f2_out):
    env = dict(locals())
    wts = {n: env[n] for n in WEIGHTS}
    mom = {n: env["m_" + n] for n in WEIGHTS}
    var = {n: env["v_" + n] for n in WEIGHTS}
    nb, s, _ = x.shape
    t = nb * s
    ax, ay, ac = lax.axis_index("x"), lax.axis_index("y"), lax.axis_index("c")
    j_chip = 2 * ax + ay
    dev = 4 * ax + 2 * ay + ac

    groups = (("w_f1_in", "w_f1_out"), ("w_mix_in", "w_mix_out"), ("w_f2_in", "w_f2_out"))
    gather = []

    def gather_start(gi, behind):
        srcs = [wts[n][0].astype(BF16) for n in groups[gi]] + ([conv_w[0]] if gi == 1 else [])
        lands = [lax.dynamic_update_index_in_dim(lax.empty((NCHIP,) + a.shape, a.dtype), a, j_chip, 0) for a in srcs]
        ssem, rsem, srcs, lands, token = _split_start("gw_start%d" % gi, srcs, lands, _gather_plan(len(srcs)),
                                                      3 * len(srcs), behind)
        gather.append((srcs, lands, ssem, rsem))
        return token

    def gathered(gi, behind):
        srcs, lands, ssem, rsem = gather[gi]
        return _split_wait("gw_wait%d" % gi, srcs, lands, ssem, rsem, _gather_plan(len(srcs)), behind)

    (c_all8,) = _allgather8("gather_c", [c.reshape(8, (nb * D) // 8)])
    token = gather_start(0, c_all8)
    c_all = c_all8.reshape(NDEV * nb, D) + token[0, 0]
    b_sh = lax.dynamic_slice(b_ada, (0, j_chip * ADA_SH), (1, ADA_SH))
    ada_sh = _ada_fwd(c_all, w_ada[0], b_sh)
    (ada4,) = _chip_allgather("gather_ada", [ada_sh])
    token = gather_start(1, ada4)
    token = gather_start(2, token)
    ada4 = ada4 + token[0:1, 0:1]
    ada_me = lax.dynamic_slice(ada4, (0, dev * nb, 0), (NCHIP, nb, ADA_SH))
    ada_me = jnp.transpose(ada_me, (1, 0, 2)).reshape(nb, NMOD * D)
    sh1, sc1, gt1, sh2, sc2, gt2, sh3, sc3, gt3 = [ada_me[:, k * D:(k + 1) * D].reshape(nb, 1, D) for k in range(NMOD)]

    wcat = jnp.transpose(w_spatial[0], (1, 0, 2)).reshape(CH, NH * CH)
    wcat_t = jnp.transpose(w_spatial[0], (0, 2, 1)).reshape(NH * CH, CH)
    bspt = jnp.repeat(b_spatial[0].T, HD, axis=1)

    w1i, w1o = gathered(0, sh1)
    w1o = w1o.reshape(DFF, D)
    x1, f1, p1 = _ffn_fwd(x, sh1, sc1, gt1, g_pre_f1, g_post_f1, w1i, w1o)
    wmi, wmo, cw4 = gathered(1, x1)
    wmo = wmo.reshape(D, D)
    cw_full = jnp.transpose(cw4, (1, 0, 2)).reshape(CK, WB)
    cw_pad = jnp.pad(cw_full, ((0, HALO - CK), (0, 0)))
    u, v, a, g = _mix_in_fwd(x1, sh2, sc2, g_pre_m, wmi)
    x2, conv, yb, m = _mix_mid_fwd(x1, u, v, a, g, gt2, gmlp_norm_g, gmlp_norm_b, wcat, bspt, cw_pad, conv_b,
                                   conv_norm_g, conv_norm_b, g_out_a, g_out_b, wmo, g_post_m)
    w2i, w2o = gathered(2, x2)
    w2o = w2o.reshape(DFF, D)
    dx3, f2, p2, lsum = _ffn_fwd(x2, sh3, sc3, gt3, g_pre_f2, g_post_f2, w2i, w2o, target=loss_target)

    def chip4(pair, rows):
        return [arr.reshape(NCHIP, rows, arr.shape[-1]) for arr in pair]

    scatter = {}

    def scatter_start(tag, pairs, behind):
        srcs = [p[1] for p in pairs]
        lands = [lax.empty((3,) + a.shape[1:], a.dtype) for a in srcs]
        ssem, rsem, srcs, lands, token = _split_start("gs_start_" + tag, srcs, lands, _scatter_plan(len(srcs)),
                                                      3 * len(srcs), behind)
        scatter[tag] = (srcs, lands, ssem, rsem)
        return token

    def reduce_and_update(tag, names, pairs, behind):
        srcs, lands, ssem, rsem = scatter[tag]
        recv = _split_wait("gs_wait_" + tag, srcs, lands, ssem, rsem, _scatter_plan(len(srcs)), behind)
        part = [_sum4("sum4_" + n, lax.dynamic_index_in_dim(pairs[k][0], j_chip, 0, keepdims=False), recv[k])
                for k, n in enumerate(names)]
        other = _sibling_swap("swap_" + tag, part)
        for k, n in enumerate(names):
            out[n] = tuple(r[None] for r in _adam_big("adam_" + n, wts[n][0], mom[n][0], var[n][0], part[k], other[k]))

    out = {}
    dx2, dp2, h3, a2, df2, dg_pre_f2, dg_post_f2, dsh3, dsc3, dgt3 = _ffn_bwd(
        dx3, x2, f2, p2, sh3, sc3, gt3, g_pre_f2, g_post_f2, w2i, w2o)
    gw2i = _wgrad("wgrad_f2_in", h3.reshape(t, D), dp2.reshape(t, 2 * DFF), 2 * DFF // NCHIP, True)
    gw2o = chip4(_wgrad("wgrad_f2_out", a2.reshape(t, DFF), df2.reshape(t, D), D // 2, False), DFF // NCHIP)
    tok = scatter_start("f2", [gw2i, gw2o], dg_post_f2)
    dy, dm, dg_post_m, dgt2 = _mix_out_bwd(dx2, m, gt2 + tok[0, 0], g_post_m, wmo)
    gwmo = chip4(_wgrad("wgrad_mix_out", yb.reshape(t, D), dm.reshape(t, D), D // 2, False), D // NCHIP)
    (du, dv, dconv, dwcat, dbsp, dgn_g, dgn_b, dgo_a, dgo_b, dcn_g, dcn_b, dcb) = _mix_mid_bwd(
        dy, u, v, conv, gmlp_norm_g, gmlp_norm_b, wcat, wcat_t, bspt, conv_norm_g, conv_norm_b, g_out_a, g_out_b)
    dx1, dproj, h2, dg_pre_m, dsh2, dsc2, dcw = _mix_in_bwd(dx2, x1, du, dv, dconv, a, g, sh2, sc2, g_pre_m, wmi, cw_pad)
    gwmi = _wgrad("wgrad_mix_in", h2.reshape(t, D), dproj.reshape(t, 4 * WA), WA, True)
    tok = scatter_start("mix", [gwmi, gwmo], dg_pre_m)
    grad_x, dp1, h1, a1, df1, dg_pre_f1, dg_post_f1, dsh1, dsc1, dgt1 = _ffn_bwd(
        dx1, x, f1, p1, sh1 + tok[0, 0], sc1, gt1, g_pre_f1, g_post_f1, w1i, w1o)
    gw1i = _wgrad("wgrad_f1_in", h1.reshape(t, D), dp1.reshape(t, 2 * DFF), 2 * DFF // NCHIP, True)
    gw1o = chip4(_wgrad("wgrad_f1_out", a1.reshape(t, DFF), df1.reshape(t, D), D // 2, False), DFF // NCHIP)
    tok = scatter_start("f1", [gw1i, gw1o], dg_post_f1)
    reduce_and_update("f2", ("w_f2_in", "w_f2_out"), [gw2i, gw2o], tok)
    reduce_and_update("mix", ("w_mix_in", "w_mix_out"), [gwmi, gwmo], out["w_f2_out"][3])

    dada = jnp.concatenate([q.reshape(nb, D) for q in (dsh1, dsc1, dgt1, dsh2, dsc2, dgt2, dsh3, dsc3, dgt3)], axis=1)
    vec_grads = dict(g_pre_f1=dg_pre_f1, g_post_f1=dg_post_f1, g_pre_m=dg_pre_m, g_post_m=dg_post_m,
                     g_pre_f2=dg_pre_f2, g_post_f2=dg_post_f2)
    pair_grads = dict(gmlp_norm_g=dgn_g, gmlp_norm_b=dgn_b, conv_b=dcb, conv_norm_g=dcn_g, conv_norm_b=dcn_b,
                      g_out_a=dgo_a, g_out_b=dgo_b)
    pack = _pack_small([vec_grads[n] for n in VEC_ORDER], [pair_grads[n] for n in PAIR_ORDER], dbsp, dwcat, lsum,
                       out["w_mix_out"][3])
    pack_all, dcw_all, dada_all8 = _allgather8("gather_small", [pack, dcw, dada.reshape(8, (nb * NMOD * D) // 8)])
    dada_all = dada_all8.reshape(NDEV * nb, NMOD * D)
    dcw_mine = lax.dynamic_slice(dcw_all, (0, 0, j_chip * (WB // NCHIP)), (NDEV, HALO, WB // NCHIP))
    small = {n: (wts[n], mom[n], var[n]) for n in list(VEC_ORDER) + list(PAIR_ORDER) + ["b_spatial", "w_spatial", "conv_w", "b_ada"]}
    small_out, loss = _small_adam(pack_all, dcw_mine, dada_all, small)
    out.update(small_out)
    dada_sh = lax.dynamic_slice(dada_all, (0, j_chip * ADA_SH), (NDEV * nb, ADA_SH))
    out["w_ada"] = tuple(r[None] for r in _ada_bwd_adam(c_all, dada_sh, w_ada[0], m_w_ada[0], v_w_ada[0]))
    reduce_and_update("f1", ("w_f1_in", "w_f1_out"), [gw1i, gw1o], out["w_ada"][3])

    res = [loss, grad_x]
    for k in range(4):
        res += [out[n][k] for n in WEIGHTS]
    return tuple(res)
```

```python
import functools

import jax
import jax.numpy as jnp
from jax import lax
from jax.experimental import pallas as pl
from jax.experimental.pallas import tpu as pltpu

D = 1024
DFF = 2816
WA = 512
WB = 512
NH = 8
HD = 64
CH = 128
CK = 31
HALO = 32
NMOD = 9
EPS = 1e-6
NCHIP = 4
NDEV = 8
FBLK = DFF // 2
ADA_SH = NMOD * D // NCHIP

LR, B1, B2, EPS_A, WD, STEP = 0.001, 0.9, 0.999, 1e-08, 0.01, 10

F32 = jnp.float32
BF16 = jnp.bfloat16
MESH = pl.DeviceIdType.MESH
ANY = pl.BlockSpec(memory_space=pl.ANY)
VMEM_FULL = pl.BlockSpec(memory_space=pltpu.VMEM)
VMEM_LIMIT = 56 * 1024 * 1024

NT = (((1,), (1,)), ((), ()))
TN = (((0,), (0,)), ((), ()))


def _dot(a, b):
    return jnp.dot(a, b, preferred_element_type=F32)


def _dot_nt(a, b):
    return lax.dot_general(a, b, NT, preferred_element_type=F32)


def _dot_tn(a, b):
    return lax.dot_general(a, b, TN, preferred_element_type=F32)


def _cparams():
    return pltpu.CompilerParams(vmem_limit_bytes=VMEM_LIMIT)


def _allgather8(name, arrs):
    n = len(arrs)

    def body(*refs):
        ins, outs = refs[:n], refs[n:2 * n]
        send_sems, recv_sems, local_sems = refs[2 * n:]
        x, y, c = lax.axis_index("x"), lax.axis_index("y"), lax.axis_index("c")
        me, sibling = (x, y, c), (x, y, 1 - c)
        chips = [(1 - x, y), (x, 1 - y), (1 - x, 1 - y)]

        def copy(a, k, block, to, src=None):
            rows = outs[a].at[4 * block[0] + 2 * block[1] + block[2]]
            return pltpu.make_async_remote_copy(
                src_ref=rows if src is None else src, dst_ref=rows,
                send_sem=send_sems.at[a, k], recv_sem=recv_sems.at[a, k],
                device_id=to, device_id_type=MESH)

        started, mine = [], []
        for a in range(n):
            loc = pltpu.make_async_copy(ins[a], outs[a].at[4 * x + 2 * y + c], local_sems.at[a])
            loc.start()
            mine.append(loc)
            first = [copy(a, 0, me, sibling, src=ins[a])]
            first += [copy(a, 1 + j, me, (*chip, c), src=ins[a]) for j, chip in enumerate(chips)]
            for cp in first:
                cp.start()
            started += first
        for a in range(n):
            for j, chip in enumerate(chips):
                copy(a, 1 + j, (*chip, c), me).wait_recv()
                fwd = copy(a, 4 + j, (*chip, c), sibling)
                fwd.start()
                started.append(fwd)
        for a in range(n):
            copy(a, 0, sibling, me).wait_recv()
            for j, chip in enumerate(chips):
                copy(a, 4 + j, (*chip, 1 - c), me).wait_recv()
        for cp in started:
            cp.wait_send()
        for loc in mine:
            loc.wait()

    return pl.pallas_call(
        body, name=name,
        out_shape=[jax.ShapeDtypeStruct((NDEV,) + a.shape, a.dtype) for a in arrs],
        in_specs=[ANY] * n, out_specs=[ANY] * n,
        scratch_shapes=[pltpu.SemaphoreType.DMA((n, 7)), pltpu.SemaphoreType.DMA((n, 7)),
                        pltpu.SemaphoreType.DMA((n,))],
    )(*arrs)


def _chip_relations(x, y):
    return [(1 - x, y), (x, 1 - y), (1 - x, 1 - y)]


def _exchange(name, arrs, out_shapes, plan):
    n = len(arrs)
    n_out = len(out_shapes)

    def body(*refs):
        ins, outs = refs[:n], refs[n:n + n_out]
        send_sems, recv_sems, local_sems = refs[n + n_out:]
        x, y, c = lax.axis_index("x"), lax.axis_index("y"), lax.axis_index("c")
        local, sends = plan(x, y, c, ins, outs)
        locs = [pltpu.make_async_copy(s, d, local_sems.at[i]) for i, (s, d) in enumerate(local)]
        for loc in locs:
            loc.start()
        cps = [pltpu.make_async_remote_copy(src_ref=s, dst_ref=d, send_sem=send_sems.at[i], recv_sem=recv_sems.at[i],
                                            device_id=peer, device_id_type=MESH)
               for i, (s, d, peer, _) in enumerate(sends)]
        for cp in cps:
            cp.start()
        for i, (s, _, peer, landing) in enumerate(sends):
            pltpu.make_async_remote_copy(src_ref=s, dst_ref=landing, send_sem=send_sems.at[i], recv_sem=recv_sems.at[i],
                                         device_id=peer, device_id_type=MESH).wait_recv()
        for cp in cps:
            cp.wait_send()
        for loc in locs:
            loc.wait()

    return n, n_out, body


def _run_exchange(name, arrs, out_shapes, plan, n_local, n_send):
    n, n_out, body = _exchange(name, arrs, out_shapes, plan)
    return pl.pallas_call(
        body, name=name, out_shape=out_shapes,
        in_specs=[ANY] * n, out_specs=[ANY] * n_out,
        scratch_shapes=[pltpu.SemaphoreType.DMA((n_send,)), pltpu.SemaphoreType.DMA((n_send,)),
                        pltpu.SemaphoreType.DMA((max(n_local, 1),))],
    )(*arrs)


def _chip_allgather(name, arrs):
    n = len(arrs)

    def plan(x, y, c, ins, outs):
        j_me = 2 * x + y
        local = [(ins[a], outs[a].at[j_me]) for a in range(n)]
        sends = []
        for a in range(n):
            for (px, py) in _chip_relations(x, y):
                sends.append((ins[a], outs[a].at[j_me], (px, py, c), outs[a].at[2 * px + py]))
        return local, sends

    shapes = [jax.ShapeDtypeStruct((NCHIP,) + a.shape, a.dtype) for a in arrs]
    return _run_exchange(name, arrs, shapes, plan, n, 3 * n)


def _chip_scatter(name, arrs):
    n = len(arrs)

    def plan(x, y, c, ins, outs):
        sends = []
        for a in range(n):
            for k, (px, py) in enumerate(_chip_relations(x, y)):
                sends.append((ins[a].at[2 * px + py], outs[a].at[k], (px, py, c), outs[a].at[k]))
        return [], sends

    shapes = [jax.ShapeDtypeStruct((3,) + a.shape[1:], a.dtype) for a in arrs]
    return _run_exchange(name, arrs, shapes, plan, 0, 3 * n)


def _sibling_swap(name, arrs):
    n = len(arrs)

    def plan(x, y, c, ins, outs):
        return [], [(ins[a], outs[a], (x, y, 1 - c), outs[a]) for a in range(n)]

    shapes = [jax.ShapeDtypeStruct(a.shape, a.dtype) for a in arrs]
    return _run_exchange(name, arrs, shapes, plan, 0, n)


HBM = pl.BlockSpec(memory_space=pltpu.HBM)
SEM = pl.BlockSpec(memory_space=pltpu.SEMAPHORE)
EFFECT = pltpu.SideEffectType.DATAFLOW_SIDE_EFFECTING


def _split_start(name, srcs, lands, plan, n_send, after):
    n, nl = len(srcs), len(lands)

    def body(*refs):
        src, land = refs[:n], refs[n:n + nl]
        send_sems, recv_sems = refs[n + nl + 1], refs[n + nl + 2]
        token = refs[-2]
        local_sems = refs[-1]
        x, y, c = lax.axis_index("x"), lax.axis_index("y"), lax.axis_index("c")
        local, sends = plan(x, y, c, src, land)
        locs = [pltpu.make_async_copy(s, d, local_sems.at[i]) for i, (s, d) in enumerate(local)]
        for loc in locs:
            loc.start()
        for loc in locs:
            loc.wait()
        for i, (s, d, peer, _) in enumerate(sends):
            pltpu.make_async_remote_copy(src_ref=s, dst_ref=d, send_sem=send_sems.at[i], recv_sem=recv_sems.at[i],
                                         device_id=peer, device_id_type=MESH).start()
        token[...] = jnp.zeros_like(token)

    thru = [pltpu.HBM(a.shape, a.dtype) for a in list(srcs) + list(lands)]
    res = pl.pallas_call(
        body, name=name,
        out_shape=(pltpu.SemaphoreType.DMA((n_send,)), pltpu.SemaphoreType.DMA((n_send,)), *thru,
                   jax.ShapeDtypeStruct((8, 128), F32)),
        in_specs=[HBM] * (n + nl) + [ANY],
        out_specs=(SEM, SEM, *([HBM] * (n + nl)), pl.BlockSpec(memory_space=pltpu.VMEM)),
        input_output_aliases={i: 2 + i for i in range(n + nl)},
        scratch_shapes=[pltpu.SemaphoreType.DMA((max(len(srcs), 1),))],
        compiler_params=pltpu.CompilerParams(has_side_effects=EFFECT),
    )(*[pltpu.with_memory_space_constraint(a, pltpu.HBM) for a in list(srcs) + list(lands)], after)
    return res[0], res[1], list(res[2:2 + n]), list(res[2 + n:2 + n + nl]), res[-1]


def _split_wait(name, srcs, lands, send_sems, recv_sems, plan, after):
    n, nl = len(srcs), len(lands)

    def body(*refs):
        src, land = refs[:n], refs[n:n + nl]
        send_sems, recv_sems = refs[n + nl], refs[n + nl + 1]
        x, y, c = lax.axis_index("x"), lax.axis_index("y"), lax.axis_index("c")
        _, sends = plan(x, y, c, src, land)
        for i, (s, _, peer, landing) in enumerate(sends):
            cp = pltpu.make_async_remote_copy(src_ref=s, dst_ref=landing, send_sem=send_sems.at[i],
                                              recv_sem=recv_sems.at[i], device_id=peer, device_id_type=MESH)
            cp.wait_send()
            cp.wait_recv()

    thru = [pltpu.HBM(a.shape, a.dtype) for a in list(srcs) + list(lands)]
    res = pl.pallas_call(
        body, name=name, out_shape=tuple(thru),
        in_specs=[HBM] * (n + nl) + [SEM, SEM, ANY], out_specs=tuple([HBM] * (n + nl)),
        input_output_aliases={i: i for i in range(n + nl)},
        compiler_params=pltpu.CompilerParams(has_side_effects=EFFECT),
    )(*srcs, *lands, send_sems, recv_sems, after)
    return list(res[n:])


def _split_forward(name, srcs, lands, send_a, recv_a, plan_a, plan_b, n_b, after):
    n, nl = len(srcs), len(lands)

    def body(*refs):
        src, land = refs[:n], refs[n:n + nl]
        send_a, recv_a = refs[n + nl], refs[n + nl + 1]
        send_b, recv_b = refs[n + nl + 3], refs[n + nl + 4]
        token = refs[-1]
        x, y, c = lax.axis_index("x"), lax.axis_index("y"), lax.axis_index("c")
        _, first = plan_a(x, y, c, src, land)
        for i, (s, _, peer, landing) in enumerate(first):
            cp = pltpu.make_async_remote_copy(src_ref=s, dst_ref=landing, send_sem=send_a.at[i],
                                              recv_sem=recv_a.at[i], device_id=peer, device_id_type=MESH)
            cp.wait_send()
            cp.wait_recv()
        _, second = plan_b(x, y, c, src, land)
        for i, (s, d, peer, _) in enumerate(second):
            pltpu.make_async_remote_copy(src_ref=s, dst_ref=d, send_sem=send_b.at[i], recv_sem=recv_b.at[i],
                                         device_id=peer, device_id_type=MESH).start()
        token[...] = jnp.zeros_like(token)

    thru = [pltpu.HBM(a.shape, a.dtype) for a in lands]
    res = pl.pallas_call(
        body, name=name,
        out_shape=(pltpu.SemaphoreType.DMA((n_b,)), pltpu.SemaphoreType.DMA((n_b,)), *thru,
                   jax.ShapeDtypeStruct((8, 128), F32)),
        in_specs=[HBM] * (n + nl) + [SEM, SEM, ANY],
        out_specs=(SEM, SEM, *([HBM] * nl), pl.BlockSpec(memory_space=pltpu.VMEM)),
        input_output_aliases={n + i: 2 + i for i in range(nl)},
        compiler_params=pltpu.CompilerParams(has_side_effects=EFFECT),
    )(*srcs, *lands, send_a, recv_a, after)
    return res[0], res[1], list(res[2:2 + nl]), res[-1]


def _gather_plans(shapes):
    n = len(shapes)

    def halves(a, c):
        rows = shapes[a][0] // 2
        return pl.ds(pl.multiple_of(c * rows, 16), rows), pl.ds(pl.multiple_of((1 - c) * rows, 16), rows)

    def split(a):
        return shapes[a][0] % 32 == 0

    def plan_a(x, y, c, src, land):
        j_me = 2 * x + y
        sends = []
        for a in range(n):
            for (px, py) in _chip_relations(x, y):
                if split(a):
                    mine, _ = halves(a, c)
                    sends.append((src[a].at[mine], land[a].at[j_me, mine], (px, py, c), land[a].at[2 * px + py, mine]))
                else:
                    sends.append((src[a], land[a].at[j_me], (px, py, c), land[a].at[2 * px + py]))
        return [], sends

    def plan_b(x, y, c, src, land):
        sends = []
        for a in range(n):
            if split(a):
                mine, other = halves(a, c)
                for (px, py) in _chip_relations(x, y):
                    j = 2 * px + py
                    sends.append((land[a].at[j, mine], land[a].at[j, mine], (x, y, 1 - c), land[a].at[j, other]))
        return [], sends

    n_b = 3 * sum(1 for a in range(n) if split(a))
    return plan_a, plan_b, n_b


def _scatter_plan(n):
    def plan(x, y, c, src, land):
        sends = []
        for a in range(n):
            for k, (px, py) in enumerate(_chip_relations(x, y)):
                sends.append((src[a].at[2 * px + py], land[a].at[k], (px, py, c), land[a].at[k]))
        return [], sends

    return plan


def _rms(x):
    r = lax.rsqrt(jnp.mean(x * x, axis=-1, keepdims=True) + EPS)
    return x * r, r


def _rms_bwd(dy, n, r, g):
    dg = jnp.sum(dy * n, axis=0, keepdims=True)
    dn = dy * g
    dx = r * (dn - n * jnp.mean(dn * n, axis=-1, keepdims=True))
    return dx, dg


def _ln(x):
    mu = jnp.mean(x, axis=-1, keepdims=True)
    xc = x - mu
    rstd = lax.rsqrt(jnp.mean(xc * xc, axis=-1, keepdims=True) + EPS)
    return xc * rstd, rstd


def _ln_bwd(dy, xhat, rstd, g):
    dg = jnp.sum(dy * xhat, axis=0, keepdims=True)
    db = jnp.sum(dy, axis=0, keepdims=True)
    dxh = dy * g
    dx = rstd * (dxh - jnp.mean(dxh, axis=-1, keepdims=True) - xhat * jnp.mean(dxh * xhat, axis=-1, keepdims=True))
    return dx, dg, db


def _sigmoid(x):
    return jax.nn.sigmoid(x)


def _dsilu(x, s):
    return s * (1.0 + x * (1.0 - s))


def _adam(w, g, m, v):
    m = B1 * m + (1.0 - B1) * g
    v = B2 * v + (1.0 - B2) * (g * g)
    m_hat = m / (1.0 - B1 ** STEP)
    v_hat = v / (1.0 - B2 ** STEP)
    delta = -LR * (m_hat / (jnp.sqrt(v_hat) + EPS_A) + WD * w)
    return delta, m, v


def _head_mask(shape):
    lane = lax.broadcasted_iota(jnp.int32, shape, len(shape) - 1)
    return [(lane >= h * HD) & (lane < (h + 1) * HD) for h in range(NH)]


def _first(b, i):
    return jnp.logical_and(b == 0, i == 0)


def _acc(ref, val, first):
    @pl.when(first)
    def _():
        ref[...] = val

    @pl.when(jnp.logical_not(first))
    def _():
        ref[...] += val


def _ada_fwd(c_all, w_sh, b_sh):
    nb = c_all.shape[0]
    tn = 768

    def body(c_ref, w_ref, b_ref, o_ref):
        cv = c_ref[...]
        cs = (cv * _sigmoid(cv)).astype(BF16)
        o_ref[...] = _dot(cs, w_ref[...].astype(BF16)) + b_ref[...]

    return pl.pallas_call(
        body, name="ada_fwd", grid=(ADA_SH // tn,),
        out_shape=jax.ShapeDtypeStruct((nb, ADA_SH), F32),
        in_specs=[pl.BlockSpec((nb, D), lambda j: (0, 0)), pl.BlockSpec((D, tn), lambda j: (0, j)),
                  pl.BlockSpec((1, tn), lambda j: (0, j))],
        out_specs=pl.BlockSpec((nb, tn), lambda j: (0, j)),
        compiler_params=_cparams(),
    )(c_all, w_sh, b_sh)


def _ada_bwd_adam(c_all, dada_sh, w, m, v):
    nb = c_all.shape[0]
    tn = 768

    def body(c_ref, d_ref, w_ref, m_ref, v_ref, g_out, d_out, m_out, v_out):
        cv = c_ref[...]
        cs = (cv * _sigmoid(cv)).astype(BF16)
        g = _dot_tn(cs, d_ref[...].astype(BF16))
        delta, m2, v2 = _adam(w_ref[...], g, m_ref[...], v_ref[...])
        g_out[...] = g
        d_out[...] = delta
        m_out[...] = m2
        v_out[...] = v2

    big = pl.BlockSpec((D, tn), lambda j: (0, j))
    shape = jax.ShapeDtypeStruct((D, ADA_SH), F32)
    return pl.pallas_call(
        body, name="ada_bwd_adam", grid=(ADA_SH // tn,),
        out_shape=[shape] * 4,
        in_specs=[pl.BlockSpec((nb, D), lambda j: (0, 0)), pl.BlockSpec((nb, tn), lambda j: (0, j)), big, big, big],
        out_specs=[big] * 4,
        compiler_params=_cparams(),
    )(c_all, dada_sh, w, m, v)


def _tok_specs(tm, width):
    return pl.BlockSpec((1, tm, width), lambda b, i: (b, i, 0))


def _mod_spec():
    return pl.BlockSpec((1, 1, D), lambda b, i: (b, 0, 0))


def _row_spec(width=D):
    return pl.BlockSpec((1, width), lambda b, i: (0, 0))


def _ffn_fwd(x, sh, sc, gt, g_pre, g_post, w_in4, w_out, target=None):
    nb, s, _ = x.shape
    tm = min(256, s)
    with_loss = target is not None

    def body(*refs):
        if with_loss:
            (x_ref, sh_ref, sc_ref, gt_ref, gpre_ref, gpost_ref, win_ref, wout_ref, tgt_ref,
             xo_ref, f_ref, p_ref, ls_ref) = refs
        else:
            (x_ref, sh_ref, sc_ref, gt_ref, gpre_ref, gpost_ref, win_ref, wout_ref,
             xo_ref, f_ref, p_ref) = refs
        xv = x_ref[0]
        n, _ = _rms(xv)
        h = (n * gpre_ref[...]) * (1.0 + sc_ref[0]) + sh_ref[0]
        hb = h.astype(BF16)
        acc = jnp.zeros((tm, D), F32)
        for j in range(2):
            gate = _dot(hb, win_ref[j])
            up = _dot(hb, win_ref[2 + j])
            p_ref[0, :, j * FBLK:(j + 1) * FBLK] = gate.astype(BF16)
            p_ref[0, :, DFF + j * FBLK:DFF + (j + 1) * FBLK] = up.astype(BF16)
            a = (gate * _sigmoid(gate)) * up
            acc = acc + _dot(a.astype(BF16), wout_ref[j * FBLK:(j + 1) * FBLK, :])
        f_ref[0] = acc
        nf, _ = _rms(acc)
        out = xv + (0.5 * gt_ref[0]) * (nf * gpost_ref[...])
        if with_loss:
            err = out - tgt_ref[0]
            xo_ref[0] = err * (1.0 / D)
            row = jnp.sum(err * err, axis=0, keepdims=True)
            part = row[:, 0:128]
            for k in range(1, D // 128):
                part = part + row[:, k * 128:(k + 1) * 128]
            _acc(ls_ref, part, _first(pl.program_id(0), pl.program_id(1)))
        else:
            xo_ref[0] = out

    in_specs = [_tok_specs(tm, D), _mod_spec(), _mod_spec(), _mod_spec(), _row_spec(), _row_spec(), VMEM_FULL, VMEM_FULL]
    args = [x, sh, sc, gt, g_pre, g_post, w_in4, w_out]
    out_shape = [jax.ShapeDtypeStruct((nb, s, D), F32), jax.ShapeDtypeStruct((nb, s, D), F32),
                 jax.ShapeDtypeStruct((nb, s, 2 * DFF), BF16)]
    out_specs = [_tok_specs(tm, D), _tok_specs(tm, D), _tok_specs(tm, 2 * DFF)]
    if with_loss:
        in_specs.append(_tok_specs(tm, D))
        args.append(target)
        out_shape.append(jax.ShapeDtypeStruct((1, 128), F32))
        out_specs.append(pl.BlockSpec((1, 128), lambda b, i: (0, 0)))
    return pl.pallas_call(
        body, name="ffn_loss_fwd" if with_loss else "ffn_fwd", grid=(nb, s // tm),
        out_shape=out_shape, in_specs=in_specs, out_specs=out_specs,
        compiler_params=_cparams(),
    )(*args)


def _ffn_bwd(dxo, x, f, p, sh, sc, gt, g_pre, g_post, w_in4, w_out):
    nb, s, _ = x.shape
    tm = min(256, s)

    def body(dxo_ref, x_ref, f_ref, p_ref, sh_ref, sc_ref, gt_ref, gpre_ref, gpost_ref, win_ref, wout_ref,
             dx_ref, dp_ref, h_ref, a_ref, df_ref, dgpre_ref, dgpost_ref, dsh_ref, dsc_ref, dgt_ref):
        b, i = pl.program_id(0), pl.program_id(1)
        dxo_v = dxo_ref[0]
        nf, q = _rms(f_ref[0])
        gpost = gpost_ref[...]
        dgt = jnp.sum(dxo_v * (0.5 * (nf * gpost)), axis=0, keepdims=True)
        do = dxo_v * (0.5 * gt_ref[0])
        df, dgpost = _rms_bwd(do, nf, q, gpost)
        dfb = df.astype(BF16)
        df_ref[0] = dfb
        xv = x_ref[0]
        n, r = _rms(xv)
        gpre = gpre_ref[...]
        ng = n * gpre
        scale1 = 1.0 + sc_ref[0]
        h = ng * scale1 + sh_ref[0]
        h_ref[0] = h.astype(BF16)
        dh = jnp.zeros((tm, D), F32)
        for j in range(2):
            gate = p_ref[0, :, j * FBLK:(j + 1) * FBLK].astype(F32)
            up = p_ref[0, :, DFF + j * FBLK:DFF + (j + 1) * FBLK].astype(F32)
            sg = _sigmoid(gate)
            act = gate * sg
            a_ref[0, :, j * FBLK:(j + 1) * FBLK] = (act * up).astype(BF16)
            da = _dot_nt(dfb, wout_ref[j * FBLK:(j + 1) * FBLK, :])
            dgate = (da * up * _dsilu(gate, sg)).astype(BF16)
            dup = (da * act).astype(BF16)
            dp_ref[0, :, j * FBLK:(j + 1) * FBLK] = dgate
            dp_ref[0, :, DFF + j * FBLK:DFF + (j + 1) * FBLK] = dup
            dh = dh + _dot_nt(dgate, win_ref[j]) + _dot_nt(dup, win_ref[2 + j])
        dsh = jnp.sum(dh, axis=0, keepdims=True)
        dsc = jnp.sum(dh * ng, axis=0, keepdims=True)
        dxn, dgpre = _rms_bwd(dh * scale1, n, r, gpre)
        dx_ref[0] = dxo_v + dxn
        _acc(dgpre_ref, dgpre, _first(b, i))
        _acc(dgpost_ref, dgpost, _first(b, i))
        _acc(dsh_ref, dsh[None], i == 0)
        _acc(dsc_ref, dsc[None], i == 0)
        _acc(dgt_ref, dgt[None], i == 0)

    tok = _tok_specs(tm, D)
    mod_shape = jax.ShapeDtypeStruct((nb, 1, D), F32)
    row_shape = jax.ShapeDtypeStruct((1, D), F32)
    return pl.pallas_call(
        body, name="ffn_bwd", grid=(nb, s // tm),
        out_shape=[jax.ShapeDtypeStruct((nb, s, D), F32), jax.ShapeDtypeStruct((nb, s, 2 * DFF), BF16),
                   jax.ShapeDtypeStruct((nb, s, D), BF16), jax.ShapeDtypeStruct((nb, s, DFF), BF16),
                   jax.ShapeDtypeStruct((nb, s, D), BF16), row_shape, row_shape, mod_shape, mod_shape, mod_shape],
        in_specs=[tok, tok, tok, _tok_specs(tm, 2 * DFF), _mod_spec(), _mod_spec(), _mod_spec(), _row_spec(), _row_spec(),
                  VMEM_FULL, VMEM_FULL],
        out_specs=[tok, _tok_specs(tm, 2 * DFF), tok, _tok_specs(tm, DFF), tok, _row_spec(), _row_spec(),
                   _mod_spec(), _mod_spec(), _mod_spec()],
        compiler_params=_cparams(),
    )(dxo, x, f, p, sh, sc, gt, g_pre, g_post, w_in4, w_out)


def _wgrad(name, a, b, col_block, chip_major):
    t, ka = a.shape
    n = b.shape[1]
    tk = min(512, t)
    nk = t // tk
    nblk = n // col_block

    def body(a_ref, b_ref, o_ref, obf_ref, acc_ref):
        k = pl.program_id(1)

        @pl.when(k == 0)
        def _():
            acc_ref[...] = jnp.zeros_like(acc_ref)

        acc_ref[...] += _dot_tn(a_ref[...], b_ref[...])

        @pl.when(k == nk - 1)
        def _():
            val = acc_ref[...]
            if chip_major:
                o_ref[0] = val
                obf_ref[0] = val.astype(BF16)
            else:
                o_ref[...] = val
                obf_ref[...] = val.astype(BF16)

    if chip_major:
        shape = (nblk, ka, col_block)
        ospec = pl.BlockSpec((1, ka, col_block), lambda j, k: (j, 0, 0))
    else:
        shape = (ka, n)
        ospec = pl.BlockSpec((ka, col_block), lambda j, k: (0, j))
    return pl.pallas_call(
        body, name=name, grid=(nblk, nk),
        out_shape=[jax.ShapeDtypeStruct(shape, F32), jax.ShapeDtypeStruct(shape, BF16)],
        in_specs=[pl.BlockSpec((tk, ka), lambda j, k: (k, 0)), pl.BlockSpec((tk, col_block), lambda j, k: (k, j))],
        out_specs=[ospec, ospec],
        scratch_shapes=[pltpu.VMEM((ka, col_block), F32)],
        compiler_params=_cparams(),
    )(a, b)


def _mix_in_fwd(x, sh, sc, g_pre, w_mi4):
    nb, s, _ = x.shape
    tm = min(512, s)

    def body(x_ref, sh_ref, sc_ref, gpre_ref, w_ref, u_ref, v_ref, a_ref, g_ref):
        n, _ = _rms(x_ref[0])
        hb = ((n * gpre_ref[...]) * (1.0 + sc_ref[0]) + sh_ref[0]).astype(BF16)
        for k, o_ref in enumerate((u_ref, v_ref, a_ref, g_ref)):
            o_ref[0] = _dot(hb, w_ref[k])

    shape = jax.ShapeDtypeStruct((nb, s, WA), F32)
    return pl.pallas_call(
        body, name="mix_in_fwd", grid=(nb, s // tm),
        out_shape=[shape] * 4,
        in_specs=[_tok_specs(tm, D), _mod_spec(), _mod_spec(), _row_spec(), VMEM_FULL],
        out_specs=[_tok_specs(tm, WA)] * 4,
        compiler_params=_cparams(),
    )(x, sh, sc, g_pre, w_mi4)


def _spatial_weights(wcat_ref, transposed):
    w = wcat_ref[...]
    row = lax.broadcasted_iota(jnp.int32, w.shape, 0)
    col = lax.broadcasted_iota(jnp.int32, w.shape, 1)
    keep = ((row & (CH - 1)) <= col) if transposed else ((col & (CH - 1)) <= row)
    return jnp.where(keep, w, 0.0).astype(BF16)


def _expand_heads(vc, masks):
    return jnp.concatenate([jnp.where(mk, vc, jnp.zeros_like(vc)) for mk in masks], axis=0)


def _spatial_bias(bspt_ref):
    return bspt_ref[...]


def _conv_taps(ext_ref, w_ref, tm, offset):
    acc = jnp.zeros((tm, WB), F32)
    for k in range(CK):
        acc = acc + w_ref[k:k + 1, :] * ext_ref[offset(k):offset(k) + tm, :]
    return acc


def _halo_prev_spec(tm):
    return pl.BlockSpec((1, HALO, WB), lambda b, i: (b, jnp.maximum(i * (tm // HALO) - 1, 0), 0))


def _halo_next_spec(tm, s):
    return pl.BlockSpec((1, HALO, WB), lambda b, i: (b, jnp.minimum((i + 1) * (tm // HALO), s // HALO - 1), 0))


def _mix_mid_fwd(x, u, v, a, g, gt, gn_g, gn_b, wcat, bspt, conv_w, conv_b, cn_g, cn_b, go_a, go_b, w_mo, g_post):
    nb, s, _ = x.shape
    tm = min(512, s)

    def body(x_ref, u_ref, v_ref, a_ref, g_ref, ah_ref, gh_ref, gt_ref, gng_ref, gnb_ref, wcat_ref, bspt_ref,
             cw_ref, cb_ref, cng_ref, cnb_ref, goa_ref, gob_ref, wmo_ref, gpost_ref,
             xo_ref, conv_ref, y_ref, m_ref, ext_ref):
        i = pl.program_id(1)
        xhat, _ = _ln(v_ref[0])
        vb = (xhat * gng_ref[...] + gnb_ref[...]).astype(BF16)
        wsb = _spatial_weights(wcat_ref, False)
        bias = _spatial_bias(bspt_ref)
        masks = _head_mask((CH, WA))
        zs = []
        for cidx in range(tm // CH):
            vexp = _expand_heads(vb[cidx * CH:(cidx + 1) * CH, :], masks)
            zs.append(_dot(wsb, vexp) + bias)
        z = jnp.concatenate(zs, axis=0)
        na, _ = _rms(u_ref[0] * z)
        keep = jnp.where(i == 0, 0.0, 1.0).astype(F32)
        ext_ref[0:HALO, :] = (ah_ref[0] * _sigmoid(gh_ref[0])) * keep
        ext_ref[HALO:HALO + tm, :] = a_ref[0] * _sigmoid(g_ref[0])
        conv = _conv_taps(ext_ref, cw_ref, tm, lambda k: k + HALO - (CK - 1)) + cb_ref[...]
        conv_ref[0] = conv
        chat, _ = _ln(conv)
        cln = chat * cng_ref[...] + cnb_ref[...]
        nbb, _ = _rms(cln * _sigmoid(cln))
        yb = jnp.concatenate([na * goa_ref[...], nbb * gob_ref[...]], axis=1).astype(BF16)
        y_ref[0] = yb
        m = _dot(yb, wmo_ref[...])
        m_ref[0] = m
        nm, _ = _rms(m)
        xo_ref[0] = x_ref[0] + gt_ref[0] * (nm * gpost_ref[...])

    t5 = _tok_specs(tm, WA)
    tok = _tok_specs(tm, D)
    r5 = _row_spec(WA)
    full = lambda shape: pl.BlockSpec(shape, lambda b, i: (0,) * len(shape))
    return pl.pallas_call(
        body, name="mix_mid_fwd", grid=(nb, s // tm),
        out_shape=[jax.ShapeDtypeStruct((nb, s, D), F32), jax.ShapeDtypeStruct((nb, s, WB), F32),
                   jax.ShapeDtypeStruct((nb, s, D), BF16), jax.ShapeDtypeStruct((nb, s, D), F32)],
        in_specs=[tok, t5, t5, t5, t5, _halo_prev_spec(tm), _halo_prev_spec(tm), _mod_spec(), r5, r5,
                  full((CH, NH * CH)), full((CH, WA)), full((HALO, WB)), r5, r5, r5, r5, r5, VMEM_FULL, _row_spec()],
        out_specs=[tok, t5, tok, tok],
        scratch_shapes=[pltpu.VMEM((HALO + tm, WB), F32)],
        compiler_params=_cparams(),
    )(x, u, v, a, g, a, g, gt, gn_g, gn_b, wcat, bspt, conv_w, conv_b, cn_g, cn_b, go_a, go_b, w_mo, g_post)


def _mix_out_bwd(dxo, m, gt, g_post, w_mo):
    nb, s, _ = m.shape
    tm = min(512, s)

    def body(dxo_ref, m_ref, gt_ref, gpost_ref, wmo_ref, dy_ref, dm_ref, dgpost_ref, dgt_ref):
        b, i = pl.program_id(0), pl.program_id(1)
        dxo_v = dxo_ref[0]
        nm, q = _rms(m_ref[0])
        gpost = gpost_ref[...]
        dgt = jnp.sum(dxo_v * (nm * gpost), axis=0, keepdims=True)
        dm, dgpost = _rms_bwd(dxo_v * gt_ref[0], nm, q, gpost)
        dmb = dm.astype(BF16)
        dm_ref[0] = dmb
        dy_ref[0] = _dot_nt(dmb, wmo_ref[...])
        _acc(dgpost_ref, dgpost, _first(b, i))
        _acc(dgt_ref, dgt[None], i == 0)

    tok = _tok_specs(tm, D)
    return pl.pallas_call(
        body, name="mix_out_bwd", grid=(nb, s // tm),
        out_shape=[jax.ShapeDtypeStruct((nb, s, D), F32), jax.ShapeDtypeStruct((nb, s, D), BF16),
                   jax.ShapeDtypeStruct((1, D), F32), jax.ShapeDtypeStruct((nb, 1, D), F32)],
        in_specs=[tok, tok, _mod_spec(), _row_spec(), VMEM_FULL],
        out_specs=[tok, tok, _row_spec(), _mod_spec()],
        compiler_params=_cparams(),
    )(dxo, m, gt, g_post, w_mo)


def _mix_mid_bwd(dy, u, v, conv, gn_g, gn_b, wcat, wcat_t, bspt, cn_g, cn_b, go_a, go_b):
    nb, s, _ = dy.shape
    tm = min(512, s)
    nchunk = tm // CH

    def body(dy_ref, u_ref, v_ref, conv_ref, gng_ref, gnb_ref, wcat_ref, wcatt_ref, bspt_ref, cng_ref, cnb_ref,
             goa_ref, gob_ref,
             du_ref, dv_ref, dconv_ref, dwcat_ref, dbsp_ref, dgng_ref, dgnb_ref, dgoa_ref, dgob_ref,
             dcng_ref, dcnb_ref, dcb_ref):
        first = _first(pl.program_id(0), pl.program_id(1))
        dyv = dy_ref[0]
        xhat, rstd = _ln(v_ref[0])
        gng = gng_ref[...]
        vb = (xhat * gng + gnb_ref[...]).astype(BF16)
        wsb = _spatial_weights(wcat_ref, False)
        wsb_t = _spatial_weights(wcatt_ref, True)
        bias = _spatial_bias(bspt_ref)
        masks = _head_mask((CH, WA))
        vexps, zs = [], []
        for cidx in range(nchunk):
            vexp = _expand_heads(vb[cidx * CH:(cidx + 1) * CH, :], masks)
            vexps.append(vexp)
            zs.append(_dot(wsb, vexp) + bias)
        z = jnp.concatenate(zs, axis=0)
        uv = u_ref[0]
        na, ra = _rms(uv * z)
        dya, dgoa = _rms_bwd(dyv[:, 0:WA], na, ra, goa_ref[...])
        du_ref[0] = dya * z
        dz = dya * uv
        dwcat = jnp.zeros((CH, NH * CH), F32)
        dzsum = jnp.zeros((CH, WA), F32)
        dvlns = []
        for cidx in range(nchunk):
            dzc = dz[cidx * CH:(cidx + 1) * CH, :]
            dzsum = dzsum + dzc
            dzb = dzc.astype(BF16)
            dwcat = dwcat + _dot_nt(dzb, vexps[cidx])
            dvexp = _dot(wsb_t, dzb)
            dvl = jnp.zeros((CH, WA), F32)
            for h in range(NH):
                dvl = dvl + jnp.where(masks[h], dvexp[h * CH:(h + 1) * CH, :], 0.0)
            dvlns.append(dvl)
        dvln = jnp.concatenate(dvlns, axis=0)
        dv, dgng, dgnb = _ln_bwd(dvln, xhat, rstd, gng)
        dv_ref[0] = dv
        lane = lax.broadcasted_iota(jnp.int32, (NH, WA), 1)
        head = lax.broadcasted_iota(jnp.int32, (NH, WA), 0)
        sel = jnp.where((lane >= head * HD) & (lane < (head + 1) * HD), 1.0, 0.0).astype(F32)
        dbsp = lax.dot_general(sel, dzsum, NT, preferred_element_type=F32, precision=lax.Precision.HIGHEST)
        chat, crstd = _ln(conv_ref[0])
        cng = cng_ref[...]
        cln = chat * cng + cnb_ref[...]
        sg = _sigmoid(cln)
        nbb, rb = _rms(cln * sg)
        dyb, dgob = _rms_bwd(dyv[:, WA:D], nbb, rb, gob_ref[...])
        dconv, dcng, dcnb = _ln_bwd(dyb * _dsilu(cln, sg), chat, crstd, cng)
        dconv_ref[0] = dconv
        dcb = jnp.sum(dconv, axis=0, keepdims=True)
        for ref, val in ((dwcat_ref, dwcat), (dbsp_ref, dbsp), (dgng_ref, dgng), (dgnb_ref, dgnb), (dgoa_ref, dgoa),
                         (dgob_ref, dgob), (dcng_ref, dcng), (dcnb_ref, dcnb), (dcb_ref, dcb)):
            _acc(ref, val, first)

    t5 = _tok_specs(tm, WA)
    r5 = _row_spec(WA)
    full = lambda shape: pl.BlockSpec(shape, lambda b, i: (0,) * len(shape))
    big = jax.ShapeDtypeStruct((nb, s, WA), F32)
    row = jax.ShapeDtypeStruct((1, WA), F32)
    return pl.pallas_call(
        body, name="mix_mid_bwd", grid=(nb, s // tm),
        out_shape=[big, big, big, jax.ShapeDtypeStruct((CH, NH * CH), F32), jax.ShapeDtypeStruct((NH, CH), F32),
                   row, row, row, row, row, row, row],
        in_specs=[_tok_specs(tm, D), t5, t5, t5, r5, r5, full((CH, NH * CH)), full((NH * CH, CH)), full((CH, WA)),
                  r5, r5, r5, r5],
        out_specs=[t5, t5, t5, full((CH, NH * CH)), full((NH, CH)), r5, r5, r5, r5, r5, r5, r5],
        compiler_params=_cparams(),
    )(dy, u, v, conv, gn_g, gn_b, wcat, wcat_t, bspt, cn_g, cn_b, go_a, go_b)


def _mix_in_bwd(dxo, x, du, dv, dconv, a, g, sh, sc, g_pre, w_mi4, conv_w):
    nb, s, _ = x.shape
    tm = min(512, s)
    n_i = s // tm

    def body(dxo_ref, x_ref, du_ref, dv_ref, dc_ref, dch_ref, a_ref, g_ref, ah_ref, gh_ref, sh_ref, sc_ref,
             gpre_ref, w_ref, cw_ref,
             dx_ref, dproj_ref, h_ref, dgpre_ref, dsh_ref, dsc_ref, dcw_ref, dext_ref, gext_ref):
        b, i = pl.program_id(0), pl.program_id(1)
        first = _first(b, i)
        av, gv = a_ref[0], g_ref[0]
        sg = _sigmoid(gv)
        dconv = dc_ref[0]
        dext_ref[0:tm, :] = dconv
        dext_ref[tm:tm + HALO, :] = dch_ref[0] * jnp.where(i == n_i - 1, 0.0, 1.0).astype(F32)
        gext_ref[0:HALO, :] = (ah_ref[0] * _sigmoid(gh_ref[0])) * jnp.where(i == 0, 0.0, 1.0).astype(F32)
        gext_ref[HALO:HALO + tm, :] = av * sg
        dglu = _conv_taps(dext_ref, cw_ref, tm, lambda k: CK - 1 - k)

        @pl.when(first)
        def _():
            dcw_ref[...] = jnp.zeros((HALO, WB), F32)

        for k in range(CK):
            lo = k + HALO - (CK - 1)
            dcw_ref[k:k + 1, :] += jnp.sum(dconv * gext_ref[lo:lo + tm, :], axis=0, keepdims=True)
        da = dglu * sg
        dg = dglu * av * (sg * (1.0 - sg))
        parts = [du_ref[0].astype(BF16), dv_ref[0].astype(BF16), da.astype(BF16), dg.astype(BF16)]
        dh = jnp.zeros((tm, D), F32)
        for k in range(4):
            dproj_ref[0, :, k * WA:(k + 1) * WA] = parts[k]
            dh = dh + _dot_nt(parts[k], w_ref[k])
        n, r = _rms(x_ref[0])
        gpre = gpre_ref[...]
        ng = n * gpre
        scale1 = 1.0 + sc_ref[0]
        h_ref[0] = (ng * scale1 + sh_ref[0]).astype(BF16)
        dsh = jnp.sum(dh, axis=0, keepdims=True)
        dsc = jnp.sum(dh * ng, axis=0, keepdims=True)
        dxn, dgpre = _rms_bwd(dh * scale1, n, r, gpre)
        dx_ref[0] = dxo_ref[0] + dxn
        _acc(dgpre_ref, dgpre, first)
        _acc(dsh_ref, dsh[None], i == 0)
        _acc(dsc_ref, dsc[None], i == 0)

    tok = _tok_specs(tm, D)
    t5 = _tok_specs(tm, WA)
    full = lambda shape: pl.BlockSpec(shape, lambda b, i: (0,) * len(shape))
    mod_shape = jax.ShapeDtypeStruct((nb, 1, D), F32)
    return pl.pallas_call(
        body, name="mix_in_bwd", grid=(nb, n_i),
        out_shape=[jax.ShapeDtypeStruct((nb, s, D), F32), jax.ShapeDtypeStruct((nb, s, 4 * WA), BF16),
                   jax.ShapeDtypeStruct((nb, s, D), BF16), jax.ShapeDtypeStruct((1, D), F32), mod_shape, mod_shape,
                   jax.ShapeDtypeStruct((HALO, WB), F32)],
        in_specs=[tok, tok, t5, t5, t5, _halo_next_spec(tm, s), t5, t5, _halo_prev_spec(tm), _halo_prev_spec(tm),
                  _mod_spec(), _mod_spec(), _row_spec(), VMEM_FULL, full((HALO, WB))],
        out_specs=[tok, _tok_specs(tm, 4 * WA), tok, _row_spec(), _mod_spec(), _mod_spec(), full((HALO, WB))],
        scratch_shapes=[pltpu.VMEM((tm + HALO, WB), F32), pltpu.VMEM((HALO + tm, WB), F32)],
        compiler_params=_cparams(),
    )(dxo, x, du, dv, dconv, dconv, a, g, a, g, sh, sc, g_pre, w_mi4, conv_w)


def _row_tile(rows, cols):
    best = 8
    for t in range(8, rows + 1, 8):
        if rows % t == 0 and t * cols * 4 <= 1536 * 1024:
            best = t
    return best


def _sum4(name, own, recv):
    rows, cols = own.shape
    tr = _row_tile(rows, cols)

    def body(own_ref, recv_ref, o_ref):
        acc = own_ref[...]
        for k in range(3):
            acc = acc + recv_ref[k].astype(F32)
        o_ref[...] = acc

    return pl.pallas_call(
        body, name=name, grid=(rows // tr,),
        in_specs=[pl.BlockSpec((tr, cols), lambda i: (i, 0)), pl.BlockSpec((3, tr, cols), lambda i: (0, i, 0))],
        out_specs=pl.BlockSpec((tr, cols), lambda i: (i, 0)),
        out_shape=jax.ShapeDtypeStruct((rows, cols), F32),
        compiler_params=_cparams(),
    )(own, recv)


def _pair_exchange(name, arrs):
    n = len(arrs)

    def plan(x, y, c, ins, outs):
        sends = []
        for a in range(n):
            rows = arrs[a].shape[1] // 2
            theirs = pl.ds(pl.multiple_of((1 - c) * rows, 16), rows)
            sends.append((ins[a].at[:, theirs], outs[a], (x, y, 1 - c), outs[a]))
        return [], sends

    shapes = [jax.ShapeDtypeStruct((a.shape[0], a.shape[1] // 2, a.shape[2]), a.dtype) for a in arrs]
    return _run_exchange(name, arrs, shapes, plan, 0, n)


def _pair_sum(name, g32, recv, c_arr):
    nblk, rows, cols = recv.shape
    tr = _row_tile(rows, cols)
    nh = rows // tr

    def body(c_ref, g_ref, r_ref, o32_ref, obf_ref):
        del c_ref
        val = g_ref[0] + r_ref[0].astype(F32)
        o32_ref[0] = val
        obf_ref[0] = val.astype(BF16)

    spec = pl.BlockSpec((1, tr, cols), lambda k, i, c: (k, i, 0))
    return pl.pallas_call(
        body, name=name,
        grid_spec=pltpu.PrefetchScalarGridSpec(
            num_scalar_prefetch=1, grid=(nblk, nh),
            in_specs=[pl.BlockSpec((1, tr, cols), lambda k, i, c: (k, c[0] * nh + i, 0)), spec],
            out_specs=[spec, spec]),
        out_shape=[jax.ShapeDtypeStruct(recv.shape, F32), jax.ShapeDtypeStruct(recv.shape, BF16)],
        compiler_params=_cparams(),
    )(c_arr, g32, recv)


def _adam_halves(name, w, m, v, mine, theirs, c_arr):
    rows, cols = w.shape
    tr = _row_tile(rows // 2, cols)
    nh = (rows // 2) // tr

    def body(c_ref, w_ref, m_ref, v_ref, mine_ref, theirs_ref, g_out, d_out, m_out, v_out):
        here = (pl.program_id(0) // nh) == c_ref[0]
        g = jnp.where(here, mine_ref[...], theirs_ref[...])
        delta, m2, v2 = _adam(w_ref[...], g, m_ref[...], v_ref[...])
        g_out[...] = g
        d_out[...] = delta
        m_out[...] = m2
        v_out[...] = v2

    spec = pl.BlockSpec((tr, cols), lambda i, c: (i, 0))
    shape = jax.ShapeDtypeStruct((rows, cols), F32)
    return pl.pallas_call(
        body, name=name,
        grid_spec=pltpu.PrefetchScalarGridSpec(
            num_scalar_prefetch=1, grid=(2 * nh,),
            in_specs=[spec, spec, spec,
                      pl.BlockSpec((tr, cols), lambda i, c: (jnp.clip(i - c[0] * nh, 0, nh - 1), 0)),
                      pl.BlockSpec((tr, cols), lambda i, c: (jnp.clip(i - (1 - c[0]) * nh, 0, nh - 1), 0))],
            out_specs=[spec] * 4),
        out_shape=[shape] * 4,
        compiler_params=_cparams(),
    )(c_arr, w, m, v, mine, theirs)


def _adam_big(name, w, m, v, ga, gb):
    rows, cols = w.shape
    tr = _row_tile(rows, cols)

    def body(w_ref, m_ref, v_ref, ga_ref, gb_ref, g_out, d_out, m_out, v_out):
        gsum = ga_ref[...] + gb_ref[...]
        delta, m2, v2 = _adam(w_ref[...], gsum, m_ref[...], v_ref[...])
        g_out[...] = gsum
        d_out[...] = delta
        m_out[...] = m2
        v_out[...] = v2

    spec = pl.BlockSpec((tr, cols), lambda i: (i, 0))
    shape = jax.ShapeDtypeStruct((rows, cols), F32)
    return pl.pallas_call(
        body, name=name, grid=(rows // tr,), out_shape=[shape] * 4,
        in_specs=[spec] * 5, out_specs=[spec] * 4, compiler_params=_cparams(),
    )(w, m, v, ga, gb)


PK_VEC = 0
PK_LOSS = 6
PK_PAIR = 8
PK_BSP = 16
PK_WCAT = 24
PK_ROWS = PK_WCAT + CH
PAIR_ORDER = ("gmlp_norm_g", "gmlp_norm_b", "conv_b", "conv_norm_g", "conv_norm_b", "g_out_a", "g_out_b")
VEC_ORDER = ("g_pre_f1", "g_post_f1", "g_pre_m", "g_post_m", "g_pre_f2", "g_post_f2")


def _pack_small(vecs, pairs, dbsp, dwcat, lsum, behind):
    def body(*refs):
        vec_refs = refs[:6]
        pair_refs = refs[6:13]
        dbsp_ref, dwcat_ref, lsum_ref, _, o_ref = refs[13:]
        o_ref[0:PK_WCAT, :] = jnp.zeros((PK_WCAT, D), F32)
        o_ref[PK_LOSS:PK_LOSS + 1, 0:128] = lsum_ref[...]
        for k, r in enumerate(vec_refs):
            o_ref[PK_VEC + k:PK_VEC + k + 1, :] = r[...]
        for k, r in enumerate(pair_refs):
            row, half = PK_PAIR + k // 2, k % 2
            o_ref[row:row + 1, half * WA:(half + 1) * WA] = r[...]
        o_ref[PK_BSP:PK_BSP + NH, 0:CH] = dbsp_ref[...]
        o_ref[PK_WCAT:PK_ROWS, :] = dwcat_ref[...]

    args = list(vecs) + list(pairs) + [dbsp, dwcat, lsum]
    return pl.pallas_call(
        body, name="pack_small", out_shape=jax.ShapeDtypeStruct((PK_ROWS, D), F32),
        in_specs=[VMEM_FULL] * len(args) + [ANY], out_specs=VMEM_FULL, compiler_params=_cparams(),
    )(*args, behind)


def _small_adam(pack_all, dcw_all, dada_all, params):
    names = list(VEC_ORDER) + list(PAIR_ORDER) + ["b_spatial", "w_spatial", "conv_w", "b_ada"]
    flat = []
    for nm in names:
        flat += list(params[nm])
    n_in = 3 + len(flat)

    def body(*refs):
        pack_ref, dcw_ref, dada_ref = refs[:3]
        prm = refs[3:n_in]
        outs = refs[n_in:]

        def total(r0, nr, c0, nc):
            acc = pack_ref[0, r0:r0 + nr, c0:c0 + nc]
            for d in range(1, NDEV):
                acc = acc + pack_ref[d, r0:r0 + nr, c0:c0 + nc]
            return acc

        def emit(idx, g, getw, put):
            w_ref, m_ref, v_ref = prm[3 * idx:3 * idx + 3]
            delta, m2, v2 = _adam(getw(w_ref), g, getw(m_ref), getw(v_ref))
            for o_ref, val in zip(outs[4 * idx:4 * idx + 4], (g, delta, m2, v2)):
                put(o_ref, val)

        def whole(ref):
            return ref[...]

        def put_whole(ref, val):
            ref[...] = val

        idx = 0
        for k in range(6):
            emit(idx, total(PK_VEC + k, 1, 0, D), whole, put_whole)
            idx += 1
        for k in range(7):
            emit(idx, total(PK_PAIR + k // 2, 1, (k % 2) * WA, WA), whole, put_whole)
            idx += 1
        emit(idx, total(PK_BSP, NH, 0, CH), lambda r: r[0], lambda r, val: r.__setitem__(0, val))
        idx += 1
        row = lax.broadcasted_iota(jnp.int32, (CH, CH), 0)
        col = lax.broadcasted_iota(jnp.int32, (CH, CH), 1)
        for h in range(NH):
            gh = jnp.where(col <= row, total(PK_WCAT, CH, h * CH, CH), 0.0)
            w_ref, m_ref, v_ref = prm[3 * idx:3 * idx + 3]
            delta, m2, v2 = _adam(w_ref[0, h], gh, m_ref[0, h], v_ref[0, h])
            for o_ref, val in zip(outs[4 * idx:4 * idx + 4], (gh, delta, m2, v2)):
                o_ref[0, h] = val
        idx += 1
        gcw = dcw_ref[0, 0:CK, :]
        for d in range(1, NDEV):
            gcw = gcw + dcw_ref[d, 0:CK, :]
        emit(idx, gcw, lambda r: r[0], lambda r, val: r.__setitem__(0, val))
        idx += 1
        emit(idx, jnp.sum(dada_ref[...], axis=0, keepdims=True), whole, put_whole)
        outs[-1][...] = jnp.sum(total(PK_LOSS, 1, 0, 128), axis=1, keepdims=True) * (0.5 / D)

    out_shape = []
    for nm in names:
        w = params[nm][0]
        out_shape += [jax.ShapeDtypeStruct(w.shape, F32)] * 4
    out_shape.append(jax.ShapeDtypeStruct((1, 1), F32))
    res = pl.pallas_call(
        body, name="small_adam", out_shape=out_shape,
        in_specs=[VMEM_FULL] * n_in, out_specs=[VMEM_FULL] * len(out_shape), compiler_params=_cparams(),
    )(pack_all, dcw_all, dada_all, *flat)
    return {nm: tuple(res[4 * k:4 * k + 4]) for k, nm in enumerate(names)}, res[-1].reshape(())


WEIGHTS = ['w_ada', 'b_ada', 'g_pre_f1', 'g_post_f1', 'w_f1_in', 'w_f1_out', 'g_pre_m', 'g_post_m', 'w_mix_in',
           'gmlp_norm_g', 'gmlp_norm_b', 'w_spatial', 'b_spatial', 'conv_w', 'conv_b', 'conv_norm_g', 'conv_norm_b',
           'g_out_a', 'g_out_b', 'w_mix_out', 'g_pre_f2', 'g_post_f2', 'w_f2_in', 'w_f2_out']
BIG = ('w_f1_in', 'w_f1_out', 'w_mix_in', 'w_mix_out', 'w_f2_in', 'w_f2_out')


def kernel(x, c, w_ada, b_ada, g_pre_f1, g_post_f1, w_f1_in, w_f1_out, g_pre_m, g_post_m, w_mix_in, gmlp_norm_g, gmlp_norm_b, w_spatial, b_spatial, conv_w, conv_b, conv_norm_g, conv_norm_b, g_out_a, g_out_b, w_mix_out, g_pre_f2, g_post_f2, w_f2_in, w_f2_out, loss_target, m_w_ada, m_b_ada, m_g_pre_f1, m_g_post_f1, m_w_f1_in, m_w_f1_out, m_g_pre_m, m_g_post_m, m_w_mix_in, m_gmlp_norm_g, m_gmlp_norm_b, m_w_spatial, m_b_spatial, m_conv_w, m_conv_b, m_conv_norm_g, m_conv_norm_b, m_g_out_a, m_g_out_b, m_w_mix_out, m_g_pre_f2, m_g_post_f2, m_w_f2_in, m_w_f2_out, v_w_ada, v_b_ada, v_g_pre_f1, v_g_post_f1, v_w_f1_in, v_w_f1_out, v_g_pre_m, v_g_post_m, v_w_mix_in, v_gmlp_norm_g, v_gmlp_norm_b, v_w_spatial, v_b_spatial, v_conv_w, v_conv_b, v_conv_norm_g, v_conv_norm_b, v_g_out_a, v_g_out_b, v_w_mix_out, v_g_pre_f2, v_g_post_f2, v_w_f2_in, v_w_f2_out):
    env = dict(locals())
    wts = {n: env[n] for n in WEIGHTS}
    mom = {n: env["m_" + n] for n in WEIGHTS}
    var = {n: env["v_" + n] for n in WEIGHTS}
    nb, s, _ = x.shape
    t = nb * s
    ax, ay, ac = lax.axis_index("x"), lax.axis_index("y"), lax.axis_index("c")
    j_chip = 2 * ax + ay
    dev = 4 * ax + 2 * ay + ac

    groups = (("w_f1_in", "w_f1_out"), ("w_mix_in", "w_mix_out"), ("w_f2_in", "w_f2_out"))
    def gather_start(gi, behind):
        srcs = [wts[n][0].astype(BF16) for n in groups[gi]] + ([conv_w[0]] if gi == 1 else [])
        plan_a, plan_b, n_b = _gather_plans([a.shape for a in srcs])
        lands = [lax.dynamic_update_index_in_dim(lax.empty((NCHIP,) + a.shape, a.dtype), a, j_chip, 0) for a in srcs]
        ssem, rsem, srcs, lands, token = _split_start("gw_start%d" % gi, srcs, lands, plan_a, 3 * len(srcs), behind)
        gather[gi] = (srcs, lands, ssem, rsem, plan_a, plan_b, n_b)
        return token

    def gather_forward(gi, behind):
        srcs, lands, ssem, rsem, plan_a, plan_b, n_b = gather[gi]
        ssem, rsem, lands, token = _split_forward("gw_fwd%d" % gi, srcs, lands, ssem, rsem, plan_a, plan_b, n_b, behind)
        gather[gi] = (lands, ssem, rsem, plan_b)
        return token

    def gathered(gi, behind):
        lands, ssem, rsem, plan_b = gather[gi]
        return _split_wait("gw_wait%d" % gi, [], lands, ssem, rsem, plan_b, behind)

    gather = {}
    (c_all8,) = _allgather8("gather_c", [c.reshape(8, (nb * D) // 8)])
    token = gather_start(0, c_all8)
    c_all = c_all8.reshape(NDEV * nb, D) + token[0, 0]
    b_sh = lax.dynamic_slice(b_ada, (0, j_chip * ADA_SH), (1, ADA_SH))
    ada_sh = _ada_fwd(c_all, w_ada[0], b_sh)
    token = gather_forward(0, ada_sh)
    (ada4,) = _chip_allgather("gather_ada", [ada_sh + token[0:1, 0:1]])
    token = gather_start(1, ada4)
    token = gather_start(2, token)
    ada4 = ada4 + token[0:1, 0:1]
    ada_me = lax.dynamic_slice(ada4, (0, dev * nb, 0), (NCHIP, nb, ADA_SH))
    ada_me = jnp.transpose(ada_me, (1, 0, 2)).reshape(nb, NMOD * D)
    sh1, sc1, gt1, sh2, sc2, gt2, sh3, sc3, gt3 = [ada_me[:, k * D:(k + 1) * D].reshape(nb, 1, D) for k in range(NMOD)]

    wcat = jnp.transpose(w_spatial[0], (1, 0, 2)).reshape(CH, NH * CH)
    wcat_t = jnp.transpose(w_spatial[0], (0, 2, 1)).reshape(NH * CH, CH)
    bspt = jnp.repeat(b_spatial[0].T, HD, axis=1)

    w1i, w1o = gathered(0, sh1)
    w1o = w1o.reshape(DFF, D)
    x1, f1, p1 = _ffn_fwd(x, sh1, sc1, gt1, g_pre_f1, g_post_f1, w1i, w1o)
    wmi, wmo, cw4 = gathered(1, gather_forward(1, x1))
    wmo = wmo.reshape(D, D)
    cw_full = jnp.transpose(cw4, (1, 0, 2)).reshape(CK, WB)
    cw_pad = jnp.pad(cw_full, ((0, HALO - CK), (0, 0)))
    u, v, a, g = _mix_in_fwd(x1, sh2, sc2, g_pre_m, wmi)
    x2, conv, yb, m = _mix_mid_fwd(x1, u, v, a, g, gt2, gmlp_norm_g, gmlp_norm_b, wcat, bspt, cw_pad, conv_b,
                                   conv_norm_g, conv_norm_b, g_out_a, g_out_b, wmo, g_post_m)
    w2i, w2o = gathered(2, gather_forward(2, x2))
    w2o = w2o.reshape(DFF, D)
    dx3, f2, p2, lsum = _ffn_fwd(x2, sh3, sc3, gt3, g_pre_f2, g_post_f2, w2i, w2o, target=loss_target)

    def chip4(pair, rows):
        return [arr.reshape(NCHIP, rows, arr.shape[-1]) for arr in pair]

    scatter = {}

    def scatter_start(tag, pairs, behind):
        srcs = [p[1] for p in pairs]
        lands = [lax.empty((3,) + a.shape[1:], a.dtype) for a in srcs]
        ssem, rsem, srcs, lands, token = _split_start("gs_start_" + tag, srcs, lands, _scatter_plan(len(srcs)),
                                                      3 * len(srcs), behind)
        scatter[tag] = (srcs, lands, ssem, rsem)
        return token

    def reduce_and_update(tag, names, pairs, behind):
        srcs, lands, ssem, rsem = scatter[tag]
        recv = _split_wait("gs_wait_" + tag, srcs, lands, ssem, rsem, _scatter_plan(len(srcs)), behind)
        part = [_sum4("sum4_" + n, lax.dynamic_index_in_dim(pairs[k][0], j_chip, 0, keepdims=False), recv[k])
                for k, n in enumerate(names)]
        other = _sibling_swap("swap_" + tag, part)
        for k, n in enumerate(names):
            out[n] = tuple(r[None] for r in _adam_big("adam_" + n, wts[n][0], mom[n][0], var[n][0], part[k], other[k]))

    out = {}
    dx2, dp2, h3, a2, df2, dg_pre_f2, dg_post_f2, dsh3, dsc3, dgt3 = _ffn_bwd(
        dx3, x2, f2, p2, sh3, sc3, gt3, g_pre_f2, g_post_f2, w2i, w2o)
    gw2i = _wgrad("wgrad_f2_in", h3.reshape(t, D), dp2.reshape(t, 2 * DFF), 2 * DFF // NCHIP, True)
    gw2o = chip4(_wgrad("wgrad_f2_out", a2.reshape(t, DFF), df2.reshape(t, D), D // 2, False), DFF // NCHIP)
    tok = scatter_start("f2", [gw2i, gw2o], dg_post_f2)
    dy, dm, dg_post_m, dgt2 = _mix_out_bwd(dx2, m, gt2 + tok[0, 0], g_post_m, wmo)
    gwmo = chip4(_wgrad("wgrad_mix_out", yb.reshape(t, D), dm.reshape(t, D), D // 2, False), D // NCHIP)
    (du, dv, dconv, dwcat, dbsp, dgn_g, dgn_b, dgo_a, dgo_b, dcn_g, dcn_b, dcb) = _mix_mid_bwd(
        dy, u, v, conv, gmlp_norm_g, gmlp_norm_b, wcat, wcat_t, bspt, conv_norm_g, conv_norm_b, g_out_a, g_out_b)
    dx1, dproj, h2, dg_pre_m, dsh2, dsc2, dcw = _mix_in_bwd(dx2, x1, du, dv, dconv, a, g, sh2, sc2, g_pre_m, wmi, cw_pad)
    gwmi = _wgrad("wgrad_mix_in", h2.reshape(t, D), dproj.reshape(t, 4 * WA), WA, True)
    tok = scatter_start("mix", [gwmi, gwmo], dg_pre_m)
    grad_x, dp1, h1, a1, df1, dg_pre_f1, dg_post_f1, dsh1, dsc1, dgt1 = _ffn_bwd(
        dx1, x, f1, p1, sh1 + tok[0, 0], sc1, gt1, g_pre_f1, g_post_f1, w1i, w1o)
    gw1i = _wgrad("wgrad_f1_in", h1.reshape(t, D), dp1.reshape(t, 2 * DFF), 2 * DFF // NCHIP, True)
    gw1o = chip4(_wgrad("wgrad_f1_out", a1.reshape(t, DFF), df1.reshape(t, D), D // 2, False), DFF // NCHIP)
    c_arr = ac.reshape(1).astype(jnp.int32)
    sib = _pair_exchange("pair_f1", [gw1i[1], gw1o[1]])
    pair_i = _pair_sum("pairsum_f1_in", gw1i[0], sib[0], c_arr)
    pair_o = _pair_sum("pairsum_f1_out", gw1o[0], sib[1], c_arr)
    tok = scatter_start("f1", [pair_i, pair_o], dg_post_f1)
    reduce_and_update("f2", ("w_f2_in", "w_f2_out"), [gw2i, gw2o], tok)
    reduce_and_update("mix", ("w_mix_in", "w_mix_out"), [gwmi, gwmo], out["w_f2_out"][3])

    dada = jnp.concatenate([q.reshape(nb, D) for q in (dsh1, dsc1, dgt1, dsh2, dsc2, dgt2, dsh3, dsc3, dgt3)], axis=1)
    vec_grads = dict(g_pre_f1=dg_pre_f1, g_post_f1=dg_post_f1, g_pre_m=dg_pre_m, g_post_m=dg_post_m,
                     g_pre_f2=dg_pre_f2, g_post_f2=dg_post_f2)
    pair_grads = dict(gmlp_norm_g=dgn_g, gmlp_norm_b=dgn_b, conv_b=dcb, conv_norm_g=dcn_g, conv_norm_b=dcn_b,
                      g_out_a=dgo_a, g_out_b=dgo_b)
    pack = _pack_small([vec_grads[n] for n in VEC_ORDER], [pair_grads[n] for n in PAIR_ORDER], dbsp, dwcat, lsum,
                       out["w_mix_out"][3])
    pack_all, dcw_all, dada_all8 = _allgather8("gather_small", [pack, dcw, dada.reshape(8, (nb * NMOD * D) // 8)])
    dada_all = dada_all8.reshape(NDEV * nb, NMOD * D)
    dcw_mine = lax.dynamic_slice(dcw_all, (0, 0, j_chip * (WB // NCHIP)), (NDEV, HALO, WB // NCHIP))
    small = {n: (wts[n], mom[n], var[n]) for n in list(VEC_ORDER) + list(PAIR_ORDER) + ["b_spatial", "w_spatial", "conv_w", "b_ada"]}
    small_out, loss = _small_adam(pack_all, dcw_mine, dada_all, small)
    out.update(small_out)
    dada_sh = lax.dynamic_slice(dada_all, (0, j_chip * ADA_SH), (NDEV * nb, ADA_SH))
    out["w_ada"] = tuple(r[None] for r in _ada_bwd_adam(c_all, dada_sh, w_ada[0], m_w_ada[0], v_w_ada[0]))
    srcs, lands, ssem, rsem = scatter["f1"]
    recv = _split_wait("gs_wait_f1", srcs, lands, ssem, rsem, _scatter_plan(len(srcs)), out["w_ada"][3])
    names = ("w_f1_in", "w_f1_out")
    mine = [_sum4("sum4_" + n, lax.dynamic_index_in_dim(p[0], j_chip, 0, keepdims=False), recv[k])
            for k, (n, p) in enumerate(zip(names, (pair_i, pair_o)))]
    theirs = _sibling_swap("swap_f1", mine)
    for k, n in enumerate(names):
        out[n] = tuple(r[None] for r in _adam_halves("adam_" + n, wts[n][0], mom[n][0], var[n][0], mine[k], theirs[k],
                                                     c_arr))

    res = [loss, grad_x]
    for k in range(4):
        res += [out[n][k] for n in WEIGHTS]
    return tuple(res)
```

```python
import functools

import jax
import jax.numpy as jnp
from jax import lax
from jax.experimental import pallas as pl
from jax.experimental.pallas import tpu as pltpu

D = 1024
DFF = 2816
WA = 512
WB = 512
NH = 8
HD = 64
CH = 128
CK = 31
HALO = 32
NMOD = 9
EPS = 1e-6
NCHIP = 4
NDEV = 8
FBLK = DFF // 2
ADA_SH = NMOD * D // NCHIP

LR, B1, B2, EPS_A, WD, STEP = 0.001, 0.9, 0.999, 1e-08, 0.01, 10

F32 = jnp.float32
BF16 = jnp.bfloat16
MESH = pl.DeviceIdType.MESH
ANY = pl.BlockSpec(memory_space=pl.ANY)
VMEM_FULL = pl.BlockSpec(memory_space=pltpu.VMEM)
VMEM_LIMIT = 56 * 1024 * 1024

NT = (((1,), (1,)), ((), ()))
TN = (((0,), (0,)), ((), ()))


def _dot(a, b):
    return jnp.dot(a, b, preferred_element_type=F32)


def _dot_nt(a, b):
    return lax.dot_general(a, b, NT, preferred_element_type=F32)


def _dot_tn(a, b):
    return lax.dot_general(a, b, TN, preferred_element_type=F32)


def _cparams():
    return pltpu.CompilerParams(vmem_limit_bytes=VMEM_LIMIT)


def _allgather8(name, arrs):
    n = len(arrs)

    def body(*refs):
        ins, outs = refs[:n], refs[n:2 * n]
        send_sems, recv_sems, local_sems = refs[2 * n:]
        x, y, c = lax.axis_index("x"), lax.axis_index("y"), lax.axis_index("c")
        me, sibling = (x, y, c), (x, y, 1 - c)
        chips = [(1 - x, y), (x, 1 - y), (1 - x, 1 - y)]

        def copy(a, k, block, to, src=None):
            rows = outs[a].at[4 * block[0] + 2 * block[1] + block[2]]
            return pltpu.make_async_remote_copy(
                src_ref=rows if src is None else src, dst_ref=rows,
                send_sem=send_sems.at[a, k], recv_sem=recv_sems.at[a, k],
                device_id=to, device_id_type=MESH)

        started, mine = [], []
        for a in range(n):
            loc = pltpu.make_async_copy(ins[a], outs[a].at[4 * x + 2 * y + c], local_sems.at[a])
            loc.start()
            mine.append(loc)
            first = [copy(a, 0, me, sibling, src=ins[a])]
            first += [copy(a, 1 + j, me, (*chip, c), src=ins[a]) for j, chip in enumerate(chips)]
            for cp in first:
                cp.start()
            started += first
        for a in range(n):
            for j, chip in enumerate(chips):
                copy(a, 1 + j, (*chip, c), me).wait_recv()
                fwd = copy(a, 4 + j, (*chip, c), sibling)
                fwd.start()
                started.append(fwd)
        for a in range(n):
            copy(a, 0, sibling, me).wait_recv()
            for j, chip in enumerate(chips):
                copy(a, 4 + j, (*chip, 1 - c), me).wait_recv()
        for cp in started:
            cp.wait_send()
        for loc in mine:
            loc.wait()

    return pl.pallas_call(
        body, name=name,
        out_shape=[jax.ShapeDtypeStruct((NDEV,) + a.shape, a.dtype) for a in arrs],
        in_specs=[ANY] * n, out_specs=[ANY] * n,
        scratch_shapes=[pltpu.SemaphoreType.DMA((n, 7)), pltpu.SemaphoreType.DMA((n, 7)),
                        pltpu.SemaphoreType.DMA((n,))],
    )(*arrs)


def _chip_relations(x, y):
    return [(1 - x, y), (x, 1 - y), (1 - x, 1 - y)]


def _exchange(name, arrs, out_shapes, plan):
    n = len(arrs)
    n_out = len(out_shapes)

    def body(*refs):
        ins, outs = refs[:n], refs[n:n + n_out]
        send_sems, recv_sems, local_sems = refs[n + n_out:]
        x, y, c = lax.axis_index("x"), lax.axis_index("y"), lax.axis_index("c")
        local, sends = plan(x, y, c, ins, outs)
        locs = [pltpu.make_async_copy(s, d, local_sems.at[i]) for i, (s, d) in enumerate(local)]
        for loc in locs:
            loc.start()
        cps = [pltpu.make_async_remote_copy(src_ref=s, dst_ref=d, send_sem=send_sems.at[i], recv_sem=recv_sems.at[i],
                                            device_id=peer, device_id_type=MESH)
               for i, (s, d, peer, _) in enumerate(sends)]
        for cp in cps:
            cp.start()
        for i, (s, _, peer, landing) in enumerate(sends):
            pltpu.make_async_remote_copy(src_ref=s, dst_ref=landing, send_sem=send_sems.at[i], recv_sem=recv_sems.at[i],
                                         device_id=peer, device_id_type=MESH).wait_recv()
        for cp in cps:
            cp.wait_send()
        for loc in locs:
            loc.wait()

    return n, n_out, body


def _run_exchange(name, arrs, out_shapes, plan, n_local, n_send):
    n, n_out, body = _exchange(name, arrs, out_shapes, plan)
    return pl.pallas_call(
        body, name=name, out_shape=out_shapes,
        in_specs=[ANY] * n, out_specs=[ANY] * n_out,
        scratch_shapes=[pltpu.SemaphoreType.DMA((n_send,)), pltpu.SemaphoreType.DMA((n_send,)),
                        pltpu.SemaphoreType.DMA((max(n_local, 1),))],
    )(*arrs)


def _chip_allgather(name, arrs):
    n = len(arrs)

    def plan(x, y, c, ins, outs):
        j_me = 2 * x + y
        local = [(ins[a], outs[a].at[j_me]) for a in range(n)]
        sends = []
        for a in range(n):
            for (px, py) in _chip_relations(x, y):
                sends.append((ins[a], outs[a].at[j_me], (px, py, c), outs[a].at[2 * px + py]))
        return local, sends

    shapes = [jax.ShapeDtypeStruct((NCHIP,) + a.shape, a.dtype) for a in arrs]
    return _run_exchange(name, arrs, shapes, plan, n, 3 * n)


def _chip_scatter(name, arrs):
    n = len(arrs)

    def plan(x, y, c, ins, outs):
        sends = []
        for a in range(n):
            for k, (px, py) in enumerate(_chip_relations(x, y)):
                sends.append((ins[a].at[2 * px + py], outs[a].at[k], (px, py, c), outs[a].at[k]))
        return [], sends

    shapes = [jax.ShapeDtypeStruct((3,) + a.shape[1:], a.dtype) for a in arrs]
    return _run_exchange(name, arrs, shapes, plan, 0, 3 * n)


def _sibling_swap(name, arrs):
    n = len(arrs)

    def plan(x, y, c, ins, outs):
        return [], [(ins[a], outs[a], (x, y, 1 - c), outs[a]) for a in range(n)]

    shapes = [jax.ShapeDtypeStruct(a.shape, a.dtype) for a in arrs]
    return _run_exchange(name, arrs, shapes, plan, 0, n)


HBM = pl.BlockSpec(memory_space=pltpu.HBM)
SEM = pl.BlockSpec(memory_space=pltpu.SEMAPHORE)
EFFECT = pltpu.SideEffectType.DATAFLOW_SIDE_EFFECTING


def _split_start(name, srcs, lands, plan, n_send, after):
    n, nl = len(srcs), len(lands)

    def body(*refs):
        src, land = refs[:n], refs[n:n + nl]
        send_sems, recv_sems = refs[n + nl + 1], refs[n + nl + 2]
        token = refs[-2]
        local_sems = refs[-1]
        x, y, c = lax.axis_index("x"), lax.axis_index("y"), lax.axis_index("c")
        local, sends = plan(x, y, c, src, land)
        locs = [pltpu.make_async_copy(s, d, local_sems.at[i]) for i, (s, d) in enumerate(local)]
        for loc in locs:
            loc.start()
        for loc in locs:
            loc.wait()
        for i, (s, d, peer, _) in enumerate(sends):
            pltpu.make_async_remote_copy(src_ref=s, dst_ref=d, send_sem=send_sems.at[i], recv_sem=recv_sems.at[i],
                                         device_id=peer, device_id_type=MESH).start()
        token[...] = jnp.zeros_like(token)

    thru = [pltpu.HBM(a.shape, a.dtype) for a in list(srcs) + list(lands)]
    res = pl.pallas_call(
        body, name=name,
        out_shape=(pltpu.SemaphoreType.DMA((n_send,)), pltpu.SemaphoreType.DMA((n_send,)), *thru,
                   jax.ShapeDtypeStruct((8, 128), F32)),
        in_specs=[HBM] * (n + nl) + [ANY],
        out_specs=(SEM, SEM, *([HBM] * (n + nl)), pl.BlockSpec(memory_space=pltpu.VMEM)),
        input_output_aliases={i: 2 + i for i in range(n + nl)},
        scratch_shapes=[pltpu.SemaphoreType.DMA((max(len(srcs), 1),))],
        compiler_params=pltpu.CompilerParams(has_side_effects=EFFECT),
    )(*[pltpu.with_memory_space_constraint(a, pltpu.HBM) for a in list(srcs) + list(lands)], after)
    return res[0], res[1], list(res[2:2 + n]), list(res[2 + n:2 + n + nl]), res[-1]


def _split_wait(name, srcs, lands, send_sems, recv_sems, plan, after):
    n, nl = len(srcs), len(lands)

    def body(*refs):
        src, land = refs[:n], refs[n:n + nl]
        send_sems, recv_sems = refs[n + nl], refs[n + nl + 1]
        x, y, c = lax.axis_index("x"), lax.axis_index("y"), lax.axis_index("c")
        _, sends = plan(x, y, c, src, land)
        for i, (s, _, peer, landing) in enumerate(sends):
            cp = pltpu.make_async_remote_copy(src_ref=s, dst_ref=landing, send_sem=send_sems.at[i],
                                              recv_sem=recv_sems.at[i], device_id=peer, device_id_type=MESH)
            cp.wait_send()
            cp.wait_recv()

    thru = [pltpu.HBM(a.shape, a.dtype) for a in list(srcs) + list(lands)]
    res = pl.pallas_call(
        body, name=name, out_shape=tuple(thru),
        in_specs=[HBM] * (n + nl) + [SEM, SEM, ANY], out_specs=tuple([HBM] * (n + nl)),
        input_output_aliases={i: i for i in range(n + nl)},
        compiler_params=pltpu.CompilerParams(has_side_effects=EFFECT),
    )(*srcs, *lands, send_sems, recv_sems, after)
    return list(res[n:])


def _split_forward(name, srcs, lands, send_a, recv_a, plan_a, plan_b, n_b, after):
    n, nl = len(srcs), len(lands)

    def body(*refs):
        src, land = refs[:n], refs[n:n + nl]
        send_a, recv_a = refs[n + nl], refs[n + nl + 1]
        send_b, recv_b = refs[n + nl + 3], refs[n + nl + 4]
        token = refs[-1]
        x, y, c = lax.axis_index("x"), lax.axis_index("y"), lax.axis_index("c")
        _, first = plan_a(x, y, c, src, land)
        for i, (s, _, peer, landing) in enumerate(first):
            cp = pltpu.make_async_remote_copy(src_ref=s, dst_ref=landing, send_sem=send_a.at[i],
                                              recv_sem=recv_a.at[i], device_id=peer, device_id_type=MESH)
            cp.wait_send()
            cp.wait_recv()
        _, second = plan_b(x, y, c, src, land)
        for i, (s, d, peer, _) in enumerate(second):
            pltpu.make_async_remote_copy(src_ref=s, dst_ref=d, send_sem=send_b.at[i], recv_sem=recv_b.at[i],
                                         device_id=peer, device_id_type=MESH).start()
        token[...] = jnp.zeros_like(token)

    thru = [pltpu.HBM(a.shape, a.dtype) for a in lands]
    res = pl.pallas_call(
        body, name=name,
        out_shape=(pltpu.SemaphoreType.DMA((n_b,)), pltpu.SemaphoreType.DMA((n_b,)), *thru,
                   jax.ShapeDtypeStruct((8, 128), F32)),
        in_specs=[HBM] * (n + nl) + [SEM, SEM, ANY],
        out_specs=(SEM, SEM, *([HBM] * nl), pl.BlockSpec(memory_space=pltpu.VMEM)),
        input_output_aliases={n + i: 2 + i for i in range(nl)},
        compiler_params=pltpu.CompilerParams(has_side_effects=EFFECT),
    )(*srcs, *lands, send_a, recv_a, after)
    return res[0], res[1], list(res[2:2 + nl]), res[-1]


def _gather_plans(shapes):
    n = len(shapes)

    def halves(a, c):
        rows = shapes[a][0] // 2
        return pl.ds(pl.multiple_of(c * rows, 16), rows), pl.ds(pl.multiple_of((1 - c) * rows, 16), rows)

    def split(a):
        return shapes[a][0] % 32 == 0

    def plan_a(x, y, c, src, land):
        j_me = 2 * x + y
        sends = []
        for a in range(n):
            for (px, py) in _chip_relations(x, y):
                if split(a):
                    mine, _ = halves(a, c)
                    sends.append((src[a].at[mine], land[a].at[j_me, mine], (px, py, c), land[a].at[2 * px + py, mine]))
                else:
                    sends.append((src[a], land[a].at[j_me], (px, py, c), land[a].at[2 * px + py]))
        return [], sends

    def plan_b(x, y, c, src, land):
        sends = []
        for a in range(n):
            if split(a):
                mine, other = halves(a, c)
                for (px, py) in _chip_relations(x, y):
                    j = 2 * px + py
                    sends.append((land[a].at[j, mine], land[a].at[j, mine], (x, y, 1 - c), land[a].at[j, other]))
        return [], sends

    n_b = 3 * sum(1 for a in range(n) if split(a))
    return plan_a, plan_b, n_b


def _scatter_plan(n):
    def plan(x, y, c, src, land):
        sends = []
        for a in range(n):
            for k, (px, py) in enumerate(_chip_relations(x, y)):
                sends.append((src[a].at[2 * px + py], land[a].at[k], (px, py, c), land[a].at[k]))
        return [], sends

    return plan


def _rms(x):
    r = lax.rsqrt(jnp.mean(x * x, axis=-1, keepdims=True) + EPS)
    return x * r, r


def _rms_bwd(dy, n, r, g):
    dg = jnp.sum(dy * n, axis=0, keepdims=True)
    dn = dy * g
    dx = r * (dn - n * jnp.mean(dn * n, axis=-1, keepdims=True))
    return dx, dg


def _ln(x):
    mu = jnp.mean(x, axis=-1, keepdims=True)
    xc = x - mu
    rstd = lax.rsqrt(jnp.mean(xc * xc, axis=-1, keepdims=True) + EPS)
    return xc * rstd, rstd


def _ln_bwd(dy, xhat, rstd, g):
    dg = jnp.sum(dy * xhat, axis=0, keepdims=True)
    db = jnp.sum(dy, axis=0, keepdims=True)
    dxh = dy * g
    dx = rstd * (dxh - jnp.mean(dxh, axis=-1, keepdims=True) - xhat * jnp.mean(dxh * xhat, axis=-1, keepdims=True))
    return dx, dg, db


def _sigmoid(x):
    return jax.nn.sigmoid(x)


def _dsilu(x, s):
    return s * (1.0 + x * (1.0 - s))


def _adam(w, g, m, v):
    m = B1 * m + (1.0 - B1) * g
    v = B2 * v + (1.0 - B2) * (g * g)
    m_hat = m / (1.0 - B1 ** STEP)
    v_hat = v / (1.0 - B2 ** STEP)
    delta = -LR * (m_hat / (jnp.sqrt(v_hat) + EPS_A) + WD * w)
    return delta, m, v


def _head_mask(shape):
    lane = lax.broadcasted_iota(jnp.int32, shape, len(shape) - 1)
    return [(lane >= h * HD) & (lane < (h + 1) * HD) for h in range(NH)]


def _first(b, i):
    return jnp.logical_and(b == 0, i == 0)


def _acc(ref, val, first):
    @pl.when(first)
    def _():
        ref[...] = val

    @pl.when(jnp.logical_not(first))
    def _():
        ref[...] += val


def _ada_fwd(c_all, w_sh, b_sh):
    nb = c_all.shape[0]
    tn = 768

    def body(c_ref, w_ref, b_ref, o_ref):
        cv = c_ref[...]
        cs = (cv * _sigmoid(cv)).astype(BF16)
        o_ref[...] = _dot(cs, w_ref[...].astype(BF16)) + b_ref[...]

    return pl.pallas_call(
        body, name="ada_fwd", grid=(ADA_SH // tn,),
        out_shape=jax.ShapeDtypeStruct((nb, ADA_SH), F32),
        in_specs=[pl.BlockSpec((nb, D), lambda j: (0, 0)), pl.BlockSpec((D, tn), lambda j: (0, j)),
                  pl.BlockSpec((1, tn), lambda j: (0, j))],
        out_specs=pl.BlockSpec((nb, tn), lambda j: (0, j)),
        compiler_params=_cparams(),
    )(c_all, w_sh, b_sh)


def _ada_bwd_adam(c_all, dada_sh, w, m, v):
    nb = c_all.shape[0]
    tn = 768

    def body(c_ref, d_ref, w_ref, m_ref, v_ref, g_out, d_out, m_out, v_out):
        cv = c_ref[...]
        cs = (cv * _sigmoid(cv)).astype(BF16)
        g = _dot_tn(cs, d_ref[...].astype(BF16))
        delta, m2, v2 = _adam(w_ref[...], g, m_ref[...], v_ref[...])
        g_out[...] = g
        d_out[...] = delta
        m_out[...] = m2
        v_out[...] = v2

    big = pl.BlockSpec((D, tn), lambda j: (0, j))
    shape = jax.ShapeDtypeStruct((D, ADA_SH), F32)
    return pl.pallas_call(
        body, name="ada_bwd_adam", grid=(ADA_SH // tn,),
        out_shape=[shape] * 4,
        in_specs=[pl.BlockSpec((nb, D), lambda j: (0, 0)), pl.BlockSpec((nb, tn), lambda j: (0, j)), big, big, big],
        out_specs=[big] * 4,
        compiler_params=_cparams(),
    )(c_all, dada_sh, w, m, v)


def _tok_specs(tm, width):
    return pl.BlockSpec((1, tm, width), lambda b, i: (b, i, 0))


def _mod_spec():
    return pl.BlockSpec((1, 1, D), lambda b, i: (b, 0, 0))


def _row_spec(width=D):
    return pl.BlockSpec((1, width), lambda b, i: (0, 0))


def _ffn_fwd(x, sh, sc, gt, g_pre, g_post, w_in4, w_out, target=None):
    nb, s, _ = x.shape
    tm = min(256, s)
    with_loss = target is not None

    def body(*refs):
        if with_loss:
            (x_ref, sh_ref, sc_ref, gt_ref, gpre_ref, gpost_ref, win_ref, wout_ref, tgt_ref,
             xo_ref, f_ref, p_ref, ls_ref) = refs
        else:
            (x_ref, sh_ref, sc_ref, gt_ref, gpre_ref, gpost_ref, win_ref, wout_ref,
             xo_ref, f_ref, p_ref) = refs
        xv = x_ref[0]
        n, _ = _rms(xv)
        h = (n * gpre_ref[...]) * (1.0 + sc_ref[0]) + sh_ref[0]
        hb = h.astype(BF16)
        acc = jnp.zeros((tm, D), F32)
        for j in range(2):
            gate = _dot(hb, win_ref[j])
            up = _dot(hb, win_ref[2 + j])
            p_ref[0, :, j * FBLK:(j + 1) * FBLK] = gate.astype(BF16)
            p_ref[0, :, DFF + j * FBLK:DFF + (j + 1) * FBLK] = up.astype(BF16)
            a = (gate * _sigmoid(gate)) * up
            acc = acc + _dot(a.astype(BF16), wout_ref[j * FBLK:(j + 1) * FBLK, :])
        f_ref[0] = acc
        nf, _ = _rms(acc)
        out = xv + (0.5 * gt_ref[0]) * (nf * gpost_ref[...])
        if with_loss:
            err = out - tgt_ref[0]
            xo_ref[0] = err * (1.0 / D)
            row = jnp.sum(err * err, axis=0, keepdims=True)
            part = row[:, 0:128]
            for k in range(1, D // 128):
                part = part + row[:, k * 128:(k + 1) * 128]
            _acc(ls_ref, part, _first(pl.program_id(0), pl.program_id(1)))
        else:
            xo_ref[0] = out

    in_specs = [_tok_specs(tm, D), _mod_spec(), _mod_spec(), _mod_spec(), _row_spec(), _row_spec(), VMEM_FULL, VMEM_FULL]
    args = [x, sh, sc, gt, g_pre, g_post, w_in4, w_out]
    out_shape = [jax.ShapeDtypeStruct((nb, s, D), F32), jax.ShapeDtypeStruct((nb, s, D), F32),
                 jax.ShapeDtypeStruct((nb, s, 2 * DFF), BF16)]
    out_specs = [_tok_specs(tm, D), _tok_specs(tm, D), _tok_specs(tm, 2 * DFF)]
    if with_loss:
        in_specs.append(_tok_specs(tm, D))
        args.append(target)
        out_shape.append(jax.ShapeDtypeStruct((1, 128), F32))
        out_specs.append(pl.BlockSpec((1, 128), lambda b, i: (0, 0)))
    return pl.pallas_call(
        body, name="ffn_loss_fwd" if with_loss else "ffn_fwd", grid=(nb, s // tm),
        out_shape=out_shape, in_specs=in_specs, out_specs=out_specs,
        compiler_params=_cparams(),
    )(*args)


def _ffn_bwd(dxo, x, f, p, sh, sc, gt, g_pre, g_post, w_in4, w_out):
    nb, s, _ = x.shape
    tm = min(256, s)

    def body(dxo_ref, x_ref, f_ref, p_ref, sh_ref, sc_ref, gt_ref, gpre_ref, gpost_ref, win_ref, wout_ref,
             dx_ref, dp_ref, h_ref, a_ref, df_ref, dgpre_ref, dgpost_ref, dsh_ref, dsc_ref, dgt_ref):
        b, i = pl.program_id(0), pl.program_id(1)
        dxo_v = dxo_ref[0]
        nf, q = _rms(f_ref[0])
        gpost = gpost_ref[...]
        dgt = jnp.sum(dxo_v * (0.5 * (nf * gpost)), axis=0, keepdims=True)
        do = dxo_v * (0.5 * gt_ref[0])
        df, dgpost = _rms_bwd(do, nf, q, gpost)
        dfb = df.astype(BF16)
        df_ref[0] = dfb
        xv = x_ref[0]
        n, r = _rms(xv)
        gpre = gpre_ref[...]
        ng = n * gpre
        scale1 = 1.0 + sc_ref[0]
        h = ng * scale1 + sh_ref[0]
        h_ref[0] = h.astype(BF16)
        dh = jnp.zeros((tm, D), F32)
        for j in range(2):
            gate = p_ref[0, :, j * FBLK:(j + 1) * FBLK].astype(F32)
            up = p_ref[0, :, DFF + j * FBLK:DFF + (j + 1) * FBLK].astype(F32)
            sg = _sigmoid(gate)
            act = gate * sg
            a_ref[0, :, j * FBLK:(j + 1) * FBLK] = (act * up).astype(BF16)
            da = _dot_nt(dfb, wout_ref[j * FBLK:(j + 1) * FBLK, :])
            dgate = (da * up * _dsilu(gate, sg)).astype(BF16)
            dup = (da * act).astype(BF16)
            dp_ref[0, :, j * FBLK:(j + 1) * FBLK] = dgate
            dp_ref[0, :, DFF + j * FBLK:DFF + (j + 1) * FBLK] = dup
            dh = dh + _dot_nt(dgate, win_ref[j]) + _dot_nt(dup, win_ref[2 + j])
        dsh = jnp.sum(dh, axis=0, keepdims=True)
        dsc = jnp.sum(dh * ng, axis=0, keepdims=True)
        dxn, dgpre = _rms_bwd(dh * scale1, n, r, gpre)
        dx_ref[0] = dxo_v + dxn
        _acc(dgpre_ref, dgpre, _first(b, i))
        _acc(dgpost_ref, dgpost, _first(b, i))
        _acc(dsh_ref, dsh[None], i == 0)
        _acc(dsc_ref, dsc[None], i == 0)
        _acc(dgt_ref, dgt[None], i == 0)

    tok = _tok_specs(tm, D)
    mod_shape = jax.ShapeDtypeStruct((nb, 1, D), F32)
    row_shape = jax.ShapeDtypeStruct((1, D), F32)
    return pl.pallas_call(
        body, name="ffn_bwd", grid=(nb, s // tm),
        out_shape=[jax.ShapeDtypeStruct((nb, s, D), F32), jax.ShapeDtypeStruct((nb, s, 2 * DFF), BF16),
                   jax.ShapeDtypeStruct((nb, s, D), BF16), jax.ShapeDtypeStruct((nb, s, DFF), BF16),
                   jax.ShapeDtypeStruct((nb, s, D), BF16), row_shape, row_shape, mod_shape, mod_shape, mod_shape],
        in_specs=[tok, tok, tok, _tok_specs(tm, 2 * DFF), _mod_spec(), _mod_spec(), _mod_spec(), _row_spec(), _row_spec(),
                  VMEM_FULL, VMEM_FULL],
        out_specs=[tok, _tok_specs(tm, 2 * DFF), tok, _tok_specs(tm, DFF), tok, _row_spec(), _row_spec(),
                   _mod_spec(), _mod_spec(), _mod_spec()],
        compiler_params=_cparams(),
    )(dxo, x, f, p, sh, sc, gt, g_pre, g_post, w_in4, w_out)


def _wgrad(name, a, b, col_block, chip_major):
    t, ka = a.shape
    n = b.shape[1]
    tk = min(t, 512)
    while tk * 2 <= t and t % (tk * 2) == 0 and 2 * (tk * 2) * max(ka, col_block) <= 6 * 1024 * 1024:
        tk *= 2
    nk = t // tk
    nblk = n // col_block

    def body(a_ref, b_ref, o_ref, obf_ref, acc_ref):
        k = pl.program_id(1)

        @pl.when(k == 0)
        def _():
            acc_ref[...] = jnp.zeros_like(acc_ref)

        acc_ref[...] += _dot_tn(a_ref[...], b_ref[...])

        @pl.when(k == nk - 1)
        def _():
            val = acc_ref[...]
            if chip_major:
                o_ref[0] = val
                obf_ref[0] = val.astype(BF16)
            else:
                o_ref[...] = val
                obf_ref[...] = val.astype(BF16)

    if chip_major:
        shape = (nblk, ka, col_block)
        ospec = pl.BlockSpec((1, ka, col_block), lambda j, k: (j, 0, 0))
    else:
        shape = (ka, n)
        ospec = pl.BlockSpec((ka, col_block), lambda j, k: (0, j))
    return pl.pallas_call(
        body, name=name, grid=(nblk, nk),
        out_shape=[jax.ShapeDtypeStruct(shape, F32), jax.ShapeDtypeStruct(shape, BF16)],
        in_specs=[pl.BlockSpec((tk, ka), lambda j, k: (k, 0)), pl.BlockSpec((tk, col_block), lambda j, k: (k, j))],
        out_specs=[ospec, ospec],
        scratch_shapes=[pltpu.VMEM((ka, col_block), F32)],
        compiler_params=_cparams(),
    )(a, b)


def _mix_in_fwd(x, sh, sc, g_pre, w_mi4):
    nb, s, _ = x.shape
    tm = min(512, s)

    def body(x_ref, sh_ref, sc_ref, gpre_ref, w_ref, u_ref, v_ref, a_ref, g_ref):
        n, _ = _rms(x_ref[0])
        hb = ((n * gpre_ref[...]) * (1.0 + sc_ref[0]) + sh_ref[0]).astype(BF16)
        for k, o_ref in enumerate((u_ref, v_ref, a_ref, g_ref)):
            o_ref[0] = _dot(hb, w_ref[k])

    shape = jax.ShapeDtypeStruct((nb, s, WA), F32)
    return pl.pallas_call(
        body, name="mix_in_fwd", grid=(nb, s // tm),
        out_shape=[shape] * 4,
        in_specs=[_tok_specs(tm, D), _mod_spec(), _mod_spec(), _row_spec(), VMEM_FULL],
        out_specs=[_tok_specs(tm, WA)] * 4,
        compiler_params=_cparams(),
    )(x, sh, sc, g_pre, w_mi4)


def _spatial_weights(wcat_ref, transposed):
    w = wcat_ref[...]
    row = lax.broadcasted_iota(jnp.int32, w.shape, 0)
    col = lax.broadcasted_iota(jnp.int32, w.shape, 1)
    keep = ((row & (CH - 1)) <= col) if transposed else ((col & (CH - 1)) <= row)
    return jnp.where(keep, w, 0.0).astype(BF16)


def _expand_heads(vc, masks):
    return jnp.concatenate([jnp.where(mk, vc, jnp.zeros_like(vc)) for mk in masks], axis=0)


def _spatial_bias(bspt_ref):
    return bspt_ref[...]


SHIFTS = 8
TAP_ROWS = 32


def _ext_rows(tm):
    return tm + HALO + SHIFTS


def _make_shifts(ext_ref, sh_ref, tm):
    ext_ref[tm + HALO:tm + HALO + SHIFTS, :] = jnp.zeros((SHIFTS, WB), F32)
    for r in range(SHIFTS):
        sh_ref[r] = ext_ref[r:r + tm + HALO, :]


def _conv_taps(sh_ref, w_ref, tm, taps, emit):
    def block(i, carry):
        r0 = pl.multiple_of(i * TAP_ROWS, TAP_ROWS)
        acc = jnp.zeros((TAP_ROWS, WB), F32)
        for o, k in taps:
            acc = acc + w_ref[k:k + 1, :] * sh_ref[o % SHIFTS, pl.ds(r0 + SHIFTS * (o // SHIFTS), TAP_ROWS), :]
        emit(r0, acc)
        return carry

    lax.fori_loop(0, tm // TAP_ROWS, block, 0)


def _halo_prev_spec(tm):
    return pl.BlockSpec((1, HALO, WB), lambda b, i: (b, jnp.maximum(i * (tm // HALO) - 1, 0), 0))


def _halo_next_spec(tm, s):
    return pl.BlockSpec((1, HALO, WB), lambda b, i: (b, jnp.minimum((i + 1) * (tm // HALO), s // HALO - 1), 0))


def _mix_mid_fwd(x, u, v, a, g, gt, gn_g, gn_b, wcat, bspt, conv_w, conv_b, cn_g, cn_b, go_a, go_b, w_mo, g_post):
    nb, s, _ = x.shape
    tm = min(512, s)

    def body(x_ref, u_ref, v_ref, a_ref, g_ref, ah_ref, gh_ref, gt_ref, gng_ref, gnb_ref, wcat_ref, bspt_ref,
             cw_ref, cb_ref, cng_ref, cnb_ref, goa_ref, gob_ref, wmo_ref, gpost_ref,
             xo_ref, conv_ref, y_ref, m_ref, ext_ref, sh_ref):
        i = pl.program_id(1)
        xhat, _ = _ln(v_ref[0])
        vb = (xhat * gng_ref[...] + gnb_ref[...]).astype(BF16)
        wsb = _spatial_weights(wcat_ref, False)
        bias = _spatial_bias(bspt_ref)
        masks = _head_mask((CH, WA))
        zs = []
        for cidx in range(tm // CH):
            vexp = _expand_heads(vb[cidx * CH:(cidx + 1) * CH, :], masks)
            zs.append(_dot(wsb, vexp) + bias)
        z = jnp.concatenate(zs, axis=0)
        na, _ = _rms(u_ref[0] * z)
        keep = jnp.where(i == 0, 0.0, 1.0).astype(F32)
        ext_ref[0:HALO, :] = (ah_ref[0] * _sigmoid(gh_ref[0])) * keep
        ext_ref[HALO:HALO + tm, :] = a_ref[0] * _sigmoid(g_ref[0])
        _make_shifts(ext_ref, sh_ref, tm)
        cb = cb_ref[...]

        def put_conv(r0, acc):
            conv_ref[0, pl.ds(r0, TAP_ROWS), :] = acc + cb

        _conv_taps(sh_ref, cw_ref, tm, [(k + HALO - (CK - 1), k) for k in range(CK)], put_conv)
        conv = conv_ref[0]
        chat, _ = _ln(conv)
        cln = chat * cng_ref[...] + cnb_ref[...]
        nbb, _ = _rms(cln * _sigmoid(cln))
        yb = jnp.concatenate([na * goa_ref[...], nbb * gob_ref[...]], axis=1).astype(BF16)
        y_ref[0] = yb
        m = _dot(yb, wmo_ref[...])
        m_ref[0] = m
        nm, _ = _rms(m)
        xo_ref[0] = x_ref[0] + gt_ref[0] * (nm * gpost_ref[...])

    t5 = _tok_specs(tm, WA)
    tok = _tok_specs(tm, D)
    r5 = _row_spec(WA)
    full = lambda shape: pl.BlockSpec(shape, lambda b, i: (0,) * len(shape))
    return pl.pallas_call(
        body, name="mix_mid_fwd", grid=(nb, s // tm),
        out_shape=[jax.ShapeDtypeStruct((nb, s, D), F32), jax.ShapeDtypeStruct((nb, s, WB), F32),
                   jax.ShapeDtypeStruct((nb, s, D), BF16), jax.ShapeDtypeStruct((nb, s, D), F32)],
        in_specs=[tok, t5, t5, t5, t5, _halo_prev_spec(tm), _halo_prev_spec(tm), _mod_spec(), r5, r5,
                  full((CH, NH * CH)), full((CH, WA)), full((HALO, WB)), r5, r5, r5, r5, r5, VMEM_FULL, _row_spec()],
        out_specs=[tok, t5, tok, tok],
        scratch_shapes=[pltpu.VMEM((_ext_rows(tm), WB), F32), pltpu.VMEM((SHIFTS, tm + HALO, WB), F32)],
        compiler_params=_cparams(),
    )(x, u, v, a, g, a, g, gt, gn_g, gn_b, wcat, bspt, conv_w, conv_b, cn_g, cn_b, go_a, go_b, w_mo, g_post)


def _mix_out_bwd(dxo, m, gt, g_post, w_mo):
    nb, s, _ = m.shape
    tm = min(512, s)

    def body(dxo_ref, m_ref, gt_ref, gpost_ref, wmo_ref, dy_ref, dm_ref, dgpost_ref, dgt_ref):
        b, i = pl.program_id(0), pl.program_id(1)
        dxo_v = dxo_ref[0]
        nm, q = _rms(m_ref[0])
        gpost = gpost_ref[...]
        dgt = jnp.sum(dxo_v * (nm * gpost), axis=0, keepdims=True)
        dm, dgpost = _rms_bwd(dxo_v * gt_ref[0], nm, q, gpost)
        dmb = dm.astype(BF16)
        dm_ref[0] = dmb
        dy_ref[0] = _dot_nt(dmb, wmo_ref[...])
        _acc(dgpost_ref, dgpost, _first(b, i))
        _acc(dgt_ref, dgt[None], i == 0)

    tok = _tok_specs(tm, D)
    return pl.pallas_call(
        body, name="mix_out_bwd", grid=(nb, s // tm),
        out_shape=[jax.ShapeDtypeStruct((nb, s, D), F32), jax.ShapeDtypeStruct((nb, s, D), BF16),
                   jax.ShapeDtypeStruct((1, D), F32), jax.ShapeDtypeStruct((nb, 1, D), F32)],
        in_specs=[tok, tok, _mod_spec(), _row_spec(), VMEM_FULL],
        out_specs=[tok, tok, _row_spec(), _mod_spec()],
        compiler_params=_cparams(),
    )(dxo, m, gt, g_post, w_mo)


def _mix_mid_bwd(dy, u, v, conv, gn_g, gn_b, wcat, wcat_t, bspt, cn_g, cn_b, go_a, go_b):
    nb, s, _ = dy.shape
    tm = min(512, s)
    nchunk = tm // CH

    def body(dy_ref, u_ref, v_ref, conv_ref, gng_ref, gnb_ref, wcat_ref, wcatt_ref, bspt_ref, cng_ref, cnb_ref,
             goa_ref, gob_ref,
             du_ref, dv_ref, dconv_ref, dwcat_ref, dbsp_ref, dgng_ref, dgnb_ref, dgoa_ref, dgob_ref,
             dcng_ref, dcnb_ref, dcb_ref):
        first = _first(pl.program_id(0), pl.program_id(1))
        dyv = dy_ref[0]
        xhat, rstd = _ln(v_ref[0])
        gng = gng_ref[...]
        vb = (xhat * gng + gnb_ref[...]).astype(BF16)
        wsb = _spatial_weights(wcat_ref, False)
        wsb_t = _spatial_weights(wcatt_ref, True)
        bias = _spatial_bias(bspt_ref)
        masks = _head_mask((CH, WA))
        vexps, zs = [], []
        for cidx in range(nchunk):
            vexp = _expand_heads(vb[cidx * CH:(cidx + 1) * CH, :], masks)
            vexps.append(vexp)
            zs.append(_dot(wsb, vexp) + bias)
        z = jnp.concatenate(zs, axis=0)
        uv = u_ref[0]
        na, ra = _rms(uv * z)
        dya, dgoa = _rms_bwd(dyv[:, 0:WA], na, ra, goa_ref[...])
        du_ref[0] = dya * z
        dz = dya * uv
        dwcat = jnp.zeros((CH, NH * CH), F32)
        dzsum = jnp.zeros((CH, WA), F32)
        dvlns = []
        for cidx in range(nchunk):
            dzc = dz[cidx * CH:(cidx + 1) * CH, :]
            dzsum = dzsum + dzc
            dzb = dzc.astype(BF16)
            dwcat = dwcat + _dot_nt(dzb, vexps[cidx])
            dvexp = _dot(wsb_t, dzb)
            dvl = jnp.zeros((CH, WA), F32)
            for h in range(NH):
                dvl = dvl + jnp.where(masks[h], dvexp[h * CH:(h + 1) * CH, :], 0.0)
            dvlns.append(dvl)
        dvln = jnp.concatenate(dvlns, axis=0)
        dv, dgng, dgnb = _ln_bwd(dvln, xhat, rstd, gng)
        dv_ref[0] = dv
        lane = lax.broadcasted_iota(jnp.int32, (NH, WA), 1)
        head = lax.broadcasted_iota(jnp.int32, (NH, WA), 0)
        sel = jnp.where((lane >= head * HD) & (lane < (head + 1) * HD), 1.0, 0.0).astype(F32)
        dbsp = lax.dot_general(sel, dzsum, NT, preferred_element_type=F32, precision=lax.Precision.HIGHEST)
        chat, crstd = _ln(conv_ref[0])
        cng = cng_ref[...]
        cln = chat * cng + cnb_ref[...]
        sg = _sigmoid(cln)
        nbb, rb = _rms(cln * sg)
        dyb, dgob = _rms_bwd(dyv[:, WA:D], nbb, rb, gob_ref[...])
        dconv, dcng, dcnb = _ln_bwd(dyb * _dsilu(cln, sg), chat, crstd, cng)
        dconv_ref[0] = dconv
        dcb = jnp.sum(dconv, axis=0, keepdims=True)
        for ref, val in ((dwcat_ref, dwcat), (dbsp_ref, dbsp), (dgng_ref, dgng), (dgnb_ref, dgnb), (dgoa_ref, dgoa),
                         (dgob_ref, dgob), (dcng_ref, dcng), (dcnb_ref, dcnb), (dcb_ref, dcb)):
            _acc(ref, val, first)

    t5 = _tok_specs(tm, WA)
    r5 = _row_spec(WA)
    full = lambda shape: pl.BlockSpec(shape, lambda b, i: (0,) * len(shape))
    big = jax.ShapeDtypeStruct((nb, s, WA), F32)
    row = jax.ShapeDtypeStruct((1, WA), F32)
    return pl.pallas_call(
        body, name="mix_mid_bwd", grid=(nb, s // tm),
        out_shape=[big, big, big, jax.ShapeDtypeStruct((CH, NH * CH), F32), jax.ShapeDtypeStruct((NH, CH), F32),
                   row, row, row, row, row, row, row],
        in_specs=[_tok_specs(tm, D), t5, t5, t5, r5, r5, full((CH, NH * CH)), full((NH * CH, CH)), full((CH, WA)),
                  r5, r5, r5, r5],
        out_specs=[t5, t5, t5, full((CH, NH * CH)), full((NH, CH)), r5, r5, r5, r5, r5, r5, r5],
        compiler_params=_cparams(),
    )(dy, u, v, conv, gn_g, gn_b, wcat, wcat_t, bspt, cn_g, cn_b, go_a, go_b)


def _mix_in_bwd(dxo, x, du, dv, dconv, a, g, sh, sc, g_pre, w_mi4, conv_w):
    nb, s, _ = x.shape
    tm = min(512, s)
    n_i = s // tm

    def body(dxo_ref, x_ref, du_ref, dv_ref, dc_ref, dch_ref, a_ref, g_ref, ah_ref, gh_ref, sh_ref, sc_ref,
             gpre_ref, w_ref, cw_ref,
             dx_ref, dproj_ref, h_ref, dgpre_ref, dsh_ref, dsc_ref, dcw_ref, ext_ref, shf_ref, dglu_ref):
        b, i = pl.program_id(0), pl.program_id(1)
        first = _first(b, i)
        av, gv = a_ref[0], g_ref[0]
        sg = _sigmoid(gv)
        dconv = dc_ref[0]
        ext_ref[0:tm, :] = dconv
        ext_ref[tm:tm + HALO, :] = dch_ref[0] * jnp.where(i == n_i - 1, 0.0, 1.0).astype(F32)
        _make_shifts(ext_ref, shf_ref, tm)

        def put_dglu(r0, acc):
            dglu_ref[pl.ds(r0, TAP_ROWS), :] = acc

        _conv_taps(shf_ref, cw_ref, tm, [(CK - 1 - k, k) for k in range(CK)], put_dglu)
        dglu = dglu_ref[...]
        ext_ref[0:HALO, :] = (ah_ref[0] * _sigmoid(gh_ref[0])) * jnp.where(i == 0, 0.0, 1.0).astype(F32)
        ext_ref[HALO:HALO + tm, :] = av * sg
        _make_shifts(ext_ref, shf_ref, tm)

        @pl.when(first)
        def _():
            dcw_ref[...] = jnp.zeros((HALO, WB), F32)

        for k in range(CK):
            o = k + HALO - (CK - 1)
            lo = SHIFTS * (o // SHIFTS)
            dcw_ref[k:k + 1, :] += jnp.sum(dconv * shf_ref[o % SHIFTS, lo:lo + tm, :], axis=0, keepdims=True)
        da = dglu * sg
        dg = dglu * av * (sg * (1.0 - sg))
        parts = [du_ref[0].astype(BF16), dv_ref[0].astype(BF16), da.astype(BF16), dg.astype(BF16)]
        dh = jnp.zeros((tm, D), F32)
        for k in range(4):
            dproj_ref[0, :, k * WA:(k + 1) * WA] = parts[k]
            dh = dh + _dot_nt(parts[k], w_ref[k])
        n, r = _rms(x_ref[0])
        gpre = gpre_ref[...]
        ng = n * gpre
        scale1 = 1.0 + sc_ref[0]
        h_ref[0] = (ng * scale1 + sh_ref[0]).astype(BF16)
        dsh = jnp.sum(dh, axis=0, keepdims=True)
        dsc = jnp.sum(dh * ng, axis=0, keepdims=True)
        dxn, dgpre = _rms_bwd(dh * scale1, n, r, gpre)
        dx_ref[0] = dxo_ref[0] + dxn
        _acc(dgpre_ref, dgpre, first)
        _acc(dsh_ref, dsh[None], i == 0)
        _acc(dsc_ref, dsc[None], i == 0)

    tok = _tok_specs(tm, D)
    t5 = _tok_specs(tm, WA)
    full = lambda shape: pl.BlockSpec(shape, lambda b, i: (0,) * len(shape))
    mod_shape = jax.ShapeDtypeStruct((nb, 1, D), F32)
    return pl.pallas_call(
        body, name="mix_in_bwd", grid=(nb, n_i),
        out_shape=[jax.ShapeDtypeStruct((nb, s, D), F32), jax.ShapeDtypeStruct((nb, s, 4 * WA), BF16),
                   jax.ShapeDtypeStruct((nb, s, D), BF16), jax.ShapeDtypeStruct((1, D), F32), mod_shape, mod_shape,
                   jax.ShapeDtypeStruct((HALO, WB), F32)],
        in_specs=[tok, tok, t5, t5, t5, _halo_next_spec(tm, s), t5, t5, _halo_prev_spec(tm), _halo_prev_spec(tm),
                  _mod_spec(), _mod_spec(), _row_spec(), VMEM_FULL, full((HALO, WB))],
        out_specs=[tok, _tok_specs(tm, 4 * WA), tok, _row_spec(), _mod_spec(), _mod_spec(), full((HALO, WB))],
        scratch_shapes=[pltpu.VMEM((_ext_rows(tm), WB), F32), pltpu.VMEM((SHIFTS, tm + HALO, WB), F32),
                        pltpu.VMEM((tm, WB), F32)],
        compiler_params=_cparams(),
    )(dxo, x, du, dv, dconv, dconv, a, g, a, g, sh, sc, g_pre, w_mi4, conv_w)


def _row_tile(rows, cols):
    best = 8
    for t in range(8, rows + 1, 8):
        if rows % t == 0 and t * cols * 4 <= 1536 * 1024:
            best = t
    return best


def _sum4(name, own4, recv, j_arr):
    _, rows, cols = own4.shape
    tr = _row_tile(rows, cols)

    def body(j_ref, own_ref, recv_ref, o_ref):
        del j_ref
        acc = own_ref[0]
        for k in range(3):
            acc = acc + recv_ref[k].astype(F32)
        o_ref[...] = acc

    return pl.pallas_call(
        body, name=name,
        grid_spec=pltpu.PrefetchScalarGridSpec(
            num_scalar_prefetch=1, grid=(rows // tr,),
            in_specs=[pl.BlockSpec((1, tr, cols), lambda i, j: (j[0], i, 0)),
                      pl.BlockSpec((3, tr, cols), lambda i, j: (0, i, 0))],
            out_specs=pl.BlockSpec((tr, cols), lambda i, j: (i, 0))),
        out_shape=jax.ShapeDtypeStruct((rows, cols), F32),
        compiler_params=_cparams(),
    )(j_arr, own4, recv)


def _pair_exchange(name, arrs):
    n = len(arrs)

    def plan(x, y, c, ins, outs):
        sends = []
        for a in range(n):
            rows = arrs[a].shape[1] // 2
            theirs = pl.ds(pl.multiple_of((1 - c) * rows, 16), rows)
            sends.append((ins[a].at[:, theirs], outs[a], (x, y, 1 - c), outs[a]))
        return [], sends

    shapes = [jax.ShapeDtypeStruct((a.shape[0], a.shape[1] // 2, a.shape[2]), a.dtype) for a in arrs]
    return _run_exchange(name, arrs, shapes, plan, 0, n)


def _pair_sum(name, g32, recv, c_arr):
    nblk, rows, cols = recv.shape
    tr = _row_tile(rows, cols)
    nh = rows // tr

    def body(c_ref, g_ref, r_ref, o32_ref, obf_ref):
        del c_ref
        val = g_ref[0] + r_ref[0].astype(F32)
        o32_ref[0] = val
        obf_ref[0] = val.astype(BF16)

    spec = pl.BlockSpec((1, tr, cols), lambda k, i, c: (k, i, 0))
    return pl.pallas_call(
        body, name=name,
        grid_spec=pltpu.PrefetchScalarGridSpec(
            num_scalar_prefetch=1, grid=(nblk, nh),
            in_specs=[pl.BlockSpec((1, tr, cols), lambda k, i, c: (k, c[0] * nh + i, 0)), spec],
            out_specs=[spec, spec]),
        out_shape=[jax.ShapeDtypeStruct(recv.shape, F32), jax.ShapeDtypeStruct(recv.shape, BF16)],
        compiler_params=_cparams(),
    )(c_arr, g32, recv)


def _adam_halves(name, w, m, v, mine, theirs, c_arr):
    rows, cols = w.shape
    tr = _row_tile(rows // 2, cols)
    nh = (rows // 2) // tr

    def body(c_ref, w_ref, m_ref, v_ref, mine_ref, theirs_ref, g_out, d_out, m_out, v_out):
        here = (pl.program_id(0) // nh) == c_ref[0]
        g = jnp.where(here, mine_ref[...], theirs_ref[...])
        delta, m2, v2 = _adam(w_ref[...], g, m_ref[...], v_ref[...])
        g_out[...] = g
        d_out[...] = delta
        m_out[...] = m2
        v_out[...] = v2

    spec = pl.BlockSpec((tr, cols), lambda i, c: (i, 0))
    shape = jax.ShapeDtypeStruct((rows, cols), F32)
    return pl.pallas_call(
        body, name=name,
        grid_spec=pltpu.PrefetchScalarGridSpec(
            num_scalar_prefetch=1, grid=(2 * nh,),
            in_specs=[spec, spec, spec,
                      pl.BlockSpec((tr, cols), lambda i, c: (jnp.clip(i - c[0] * nh, 0, nh - 1), 0)),
                      pl.BlockSpec((tr, cols), lambda i, c: (jnp.clip(i - (1 - c[0]) * nh, 0, nh - 1), 0))],
            out_specs=[spec] * 4),
        out_shape=[shape] * 4,
        compiler_params=_cparams(),
    )(c_arr, w, m, v, mine, theirs)


def _adam_big(name, w, m, v, ga, gb):
    rows, cols = w.shape
    tr = _row_tile(rows, cols)

    def body(w_ref, m_ref, v_ref, ga_ref, gb_ref, g_out, d_out, m_out, v_out):
        gsum = ga_ref[...] + gb_ref[...]
        delta, m2, v2 = _adam(w_ref[...], gsum, m_ref[...], v_ref[...])
        g_out[...] = gsum
        d_out[...] = delta
        m_out[...] = m2
        v_out[...] = v2

    spec = pl.BlockSpec((tr, cols), lambda i: (i, 0))
    shape = jax.ShapeDtypeStruct((rows, cols), F32)
    return pl.pallas_call(
        body, name=name, grid=(rows // tr,), out_shape=[shape] * 4,
        in_specs=[spec] * 5, out_specs=[spec] * 4, compiler_params=_cparams(),
    )(w, m, v, ga, gb)


PK_VEC = 0
PK_LOSS = 6
PK_PAIR = 8
PK_BSP = 16
PK_WCAT = 24
PK_ROWS = PK_WCAT + CH
PAIR_ORDER = ("gmlp_norm_g", "gmlp_norm_b", "conv_b", "conv_norm_g", "conv_norm_b", "g_out_a", "g_out_b")
VEC_ORDER = ("g_pre_f1", "g_post_f1", "g_pre_m", "g_post_m", "g_pre_f2", "g_post_f2")


def _pack_small(vecs, pairs, dbsp, dwcat, lsum, behind):
    def body(*refs):
        vec_refs = refs[:6]
        pair_refs = refs[6:13]
        dbsp_ref, dwcat_ref, lsum_ref, _, o_ref = refs[13:]
        o_ref[0:PK_WCAT, :] = jnp.zeros((PK_WCAT, D), F32)
        o_ref[PK_LOSS:PK_LOSS + 1, 0:128] = lsum_ref[...]
        for k, r in enumerate(vec_refs):
            o_ref[PK_VEC + k:PK_VEC + k + 1, :] = r[...]
        for k, r in enumerate(pair_refs):
            row, half = PK_PAIR + k // 2, k % 2
            o_ref[row:row + 1, half * WA:(half + 1) * WA] = r[...]
        o_ref[PK_BSP:PK_BSP + NH, 0:CH] = dbsp_ref[...]
        o_ref[PK_WCAT:PK_ROWS, :] = dwcat_ref[...]

    args = list(vecs) + list(pairs) + [dbsp, dwcat, lsum]
    return pl.pallas_call(
        body, name="pack_small", out_shape=jax.ShapeDtypeStruct((PK_ROWS, D), F32),
        in_specs=[VMEM_FULL] * len(args) + [ANY], out_specs=VMEM_FULL, compiler_params=_cparams(),
    )(*args, behind)


def _small_adam(pack_all, dcw_all, dada_all, params):
    names = list(VEC_ORDER) + list(PAIR_ORDER) + ["b_spatial", "w_spatial", "conv_w", "b_ada"]
    flat = []
    for nm in names:
        flat += list(params[nm])
    n_in = 3 + len(flat)

    def body(*refs):
        pack_ref, dcw_ref, dada_ref = refs[:3]
        prm = refs[3:n_in]
        outs = refs[n_in:]

        def total(r0, nr, c0, nc):
            acc = pack_ref[0, r0:r0 + nr, c0:c0 + nc]
            for d in range(1, NDEV):
                acc = acc + pack_ref[d, r0:r0 + nr, c0:c0 + nc]
            return acc

        def emit(idx, g, getw, put):
            w_ref, m_ref, v_ref = prm[3 * idx:3 * idx + 3]
            delta, m2, v2 = _adam(getw(w_ref), g, getw(m_ref), getw(v_ref))
            for o_ref, val in zip(outs[4 * idx:4 * idx + 4], (g, delta, m2, v2)):
                put(o_ref, val)

        def whole(ref):
            return ref[...]

        def put_whole(ref, val):
            ref[...] = val

        idx = 0
        for k in range(6):
            emit(idx, total(PK_VEC + k, 1, 0, D), whole, put_whole)
            idx += 1
        for k in range(7):
            emit(idx, total(PK_PAIR + k // 2, 1, (k % 2) * WA, WA), whole, put_whole)
            idx += 1
        emit(idx, total(PK_BSP, NH, 0, CH), lambda r: r[0], lambda r, val: r.__setitem__(0, val))
        idx += 1
        row = lax.broadcasted_iota(jnp.int32, (CH, CH), 0)
        col = lax.broadcasted_iota(jnp.int32, (CH, CH), 1)
        for h in range(NH):
            gh = jnp.where(col <= row, total(PK_WCAT, CH, h * CH, CH), 0.0)
            w_ref, m_ref, v_ref = prm[3 * idx:3 * idx + 3]
            delta, m2, v2 = _adam(w_ref[0, h], gh, m_ref[0, h], v_ref[0, h])
            for o_ref, val in zip(outs[4 * idx:4 * idx + 4], (gh, delta, m2, v2)):
                o_ref[0, h] = val
        idx += 1
        gcw = dcw_ref[0, 0:CK, :]
        for d in range(1, NDEV):
            gcw = gcw + dcw_ref[d, 0:CK, :]
        emit(idx, gcw, lambda r: r[0], lambda r, val: r.__setitem__(0, val))
        idx += 1
        emit(idx, jnp.sum(dada_ref[...], axis=0, keepdims=True), whole, put_whole)
        outs[-1][...] = jnp.sum(total(PK_LOSS, 1, 0, 128), axis=1, keepdims=True) * (0.5 / D)

    out_shape = []
    for nm in names:
        w = params[nm][0]
        out_shape += [jax.ShapeDtypeStruct(w.shape, F32)] * 4
    out_shape.append(jax.ShapeDtypeStruct((1, 1), F32))
    res = pl.pallas_call(
        body, name="small_adam", out_shape=out_shape,
        in_specs=[VMEM_FULL] * n_in, out_specs=[VMEM_FULL] * len(out_shape), compiler_params=_cparams(),
    )(pack_all, dcw_all, dada_all, *flat)
    return {nm: tuple(res[4 * k:4 * k + 4]) for k, nm in enumerate(names)}, res[-1].reshape(())


WEIGHTS = ['w_ada', 'b_ada', 'g_pre_f1', 'g_post_f1', 'w_f1_in', 'w_f1_out', 'g_pre_m', 'g_post_m', 'w_mix_in',
           'gmlp_norm_g', 'gmlp_norm_b', 'w_spatial', 'b_spatial', 'conv_w', 'conv_b', 'conv_norm_g', 'conv_norm_b',
           'g_out_a', 'g_out_b', 'w_mix_out', 'g_pre_f2', 'g_post_f2', 'w_f2_in', 'w_f2_out']
BIG = ('w_f1_in', 'w_f1_out', 'w_mix_in', 'w_mix_out', 'w_f2_in', 'w_f2_out')


def kernel(x, c, w_ada, b_ada, g_pre_f1, g_post_f1, w_f1_in, w_f1_out, g_pre_m, g_post_m, w_mix_in, gmlp_norm_g, gmlp_norm_b, w_spatial, b_spatial, conv_w, conv_b, conv_norm_g, conv_norm_b, g_out_a, g_out_b, w_mix_out, g_pre_f2, g_post_f2, w_f2_in, w_f2_out, loss_target, m_w_ada, m_b_ada, m_g_pre_f1, m_g_post_f1, m_w_f1_in, m_w_f1_out, m_g_pre_m, m_g_post_m, m_w_mix_in, m_gmlp_norm_g, m_gmlp_norm_b, m_w_spatial, m_b_spatial, m_conv_w, m_conv_b, m_conv_norm_g, m_conv_norm_b, m_g_out_a, m_g_out_b, m_w_mix_out, m_g_pre_f2, m_g_post_f2, m_w_f2_in, m_w_f2_out, v_w_ada, v_b_ada, v_g_pre_f1, v_g_post_f1, v_w_f1_in, v_w_f1_out, v_g_pre_m, v_g_post_m, v_w_mix_in, v_gmlp_norm_g, v_gmlp_norm_b, v_w_spatial, v_b_spatial, v_conv_w, v_conv_b, v_conv_norm_g, v_conv_norm_b, v_g_out_a, v_g_out_b, v_w_mix_out, v_g_pre_f2, v_g_post_f2, v_w_f2_in, v_w_f2_out):
    env = dict(locals())
    wts = {n: env[n] for n in WEIGHTS}
    mom = {n: env["m_" + n] for n in WEIGHTS}
    var = {n: env["v_" + n] for n in WEIGHTS}
    nb, s, _ = x.shape
    t = nb * s
    ax, ay, ac = lax.axis_index("x"), lax.axis_index("y"), lax.axis_index("c")
    j_chip = 2 * ax + ay
    dev = 4 * ax + 2 * ay + ac
    j_arr = j_chip.reshape(1).astype(jnp.int32)

    groups = (("w_f1_in", "w_f1_out"), ("w_mix_in", "w_mix_out"), ("w_f2_in", "w_f2_out"))
    def gather_start(gi, behind):
        srcs = [wts[n][0].astype(BF16) for n in groups[gi]] + ([conv_w[0]] if gi == 1 else [])
        plan_a, plan_b, n_b = _gather_plans([a.shape for a in srcs])
        lands = [lax.dynamic_update_index_in_dim(lax.empty((NCHIP,) + a.shape, a.dtype), a, j_chip, 0) for a in srcs]
        ssem, rsem, srcs, lands, token = _split_start("gw_start%d" % gi, srcs, lands, plan_a, 3 * len(srcs), behind)
        gather[gi] = (srcs, lands, ssem, rsem, plan_a, plan_b, n_b)
        return token

    def gather_forward(gi, behind):
        srcs, lands, ssem, rsem, plan_a, plan_b, n_b = gather[gi]
        ssem, rsem, lands, token = _split_forward("gw_fwd%d" % gi, srcs, lands, ssem, rsem, plan_a, plan_b, n_b, behind)
        gather[gi] = (lands, ssem, rsem, plan_b)
        return token

    def gathered(gi, behind):
        lands, ssem, rsem, plan_b = gather[gi]
        return _split_wait("gw_wait%d" % gi, [], lands, ssem, rsem, plan_b, behind)

    gather = {}
    (c_all8,) = _allgather8("gather_c", [c.reshape(8, (nb * D) // 8)])
    token = gather_start(0, c_all8)
    c_all = c_all8.reshape(NDEV * nb, D) + token[0, 0]
    b_sh = lax.dynamic_slice(b_ada, (0, j_chip * ADA_SH), (1, ADA_SH))
    ada_sh = _ada_fwd(c_all, w_ada[0], b_sh)
    token = gather_forward(0, ada_sh)
    (ada4,) = _chip_allgather("gather_ada", [ada_sh + token[0:1, 0:1]])
    token = gather_start(1, ada4)
    token = gather_start(2, token)
    ada4 = ada4 + token[0:1, 0:1]
    ada_me = lax.dynamic_slice(ada4, (0, dev * nb, 0), (NCHIP, nb, ADA_SH))
    ada_me = jnp.transpose(ada_me, (1, 0, 2)).reshape(nb, NMOD * D)
    sh1, sc1, gt1, sh2, sc2, gt2, sh3, sc3, gt3 = [ada_me[:, k * D:(k + 1) * D].reshape(nb, 1, D) for k in range(NMOD)]

    wcat = jnp.transpose(w_spatial[0], (1, 0, 2)).reshape(CH, NH * CH)
    wcat_t = jnp.transpose(w_spatial[0], (0, 2, 1)).reshape(NH * CH, CH)
    bspt = jnp.repeat(b_spatial[0].T, HD, axis=1)

    w1i, w1o = gathered(0, sh1)
    w1o = w1o.reshape(DFF, D)
    x1, f1, p1 = _ffn_fwd(x, sh1, sc1, gt1, g_pre_f1, g_post_f1, w1i, w1o)
    wmi, wmo, cw4 = gathered(1, gather_forward(1, x1))
    wmo = wmo.reshape(D, D)
    cw_full = jnp.transpose(cw4, (1, 0, 2)).reshape(CK, WB)
    cw_pad = jnp.pad(cw_full, ((0, HALO - CK), (0, 0)))
    u, v, a, g = _mix_in_fwd(x1, sh2, sc2, g_pre_m, wmi)
    x2, conv, yb, m = _mix_mid_fwd(x1, u, v, a, g, gt2, gmlp_norm_g, gmlp_norm_b, wcat, bspt, cw_pad, conv_b,
                                   conv_norm_g, conv_norm_b, g_out_a, g_out_b, wmo, g_post_m)
    w2i, w2o = gathered(2, gather_forward(2, x2))
    w2o = w2o.reshape(DFF, D)
    dx3, f2, p2, lsum = _ffn_fwd(x2, sh3, sc3, gt3, g_pre_f2, g_post_f2, w2i, w2o, target=loss_target)

    def chip4(pair, rows):
        return [arr.reshape(NCHIP, rows, arr.shape[-1]) for arr in pair]

    scatter = {}

    def scatter_start(tag, pairs, behind):
        srcs = [p[1] for p in pairs]
        lands = [lax.empty((3,) + a.shape[1:], a.dtype) for a in srcs]
        ssem, rsem, srcs, lands, token = _split_start("gs_start_" + tag, srcs, lands, _scatter_plan(len(srcs)),
                                                      3 * len(srcs), behind)
        scatter[tag] = (srcs, lands, ssem, rsem)
        return token

    def reduce_and_update(tag, names, pairs, behind):
        srcs, lands, ssem, rsem = scatter[tag]
        recv = _split_wait("gs_wait_" + tag, srcs, lands, ssem, rsem, _scatter_plan(len(srcs)), behind)
        part = [_sum4("sum4_" + n, pairs[k][0], recv[k], j_arr)
                for k, n in enumerate(names)]
        other = _sibling_swap("swap_" + tag, part)
        for k, n in enumerate(names):
            out[n] = tuple(r[None] for r in _adam_big("adam_" + n, wts[n][0], mom[n][0], var[n][0], part[k], other[k]))

    out = {}
    dx2, dp2, h3, a2, df2, dg_pre_f2, dg_post_f2, dsh3, dsc3, dgt3 = _ffn_bwd(
        dx3, x2, f2, p2, sh3, sc3, gt3, g_pre_f2, g_post_f2, w2i, w2o)
    gw2i = _wgrad("wgrad_f2_in", h3.reshape(t, D), dp2.reshape(t, 2 * DFF), 2 * DFF // NCHIP, True)
    gw2o = chip4(_wgrad("wgrad_f2_out", a2.reshape(t, DFF), df2.reshape(t, D), D // 2, False), DFF // NCHIP)
    tok = scatter_start("f2", [gw2i, gw2o], dg_post_f2)
    dy, dm, dg_post_m, dgt2 = _mix_out_bwd(dx2, m, gt2 + tok[0, 0], g_post_m, wmo)
    gwmo = chip4(_wgrad("wgrad_mix_out", yb.reshape(t, D), dm.reshape(t, D), D // 2, False), D // NCHIP)
    (du, dv, dconv, dwcat, dbsp, dgn_g, dgn_b, dgo_a, dgo_b, dcn_g, dcn_b, dcb) = _mix_mid_bwd(
        dy, u, v, conv, gmlp_norm_g, gmlp_norm_b, wcat, wcat_t, bspt, conv_norm_g, conv_norm_b, g_out_a, g_out_b)
    dx1, dproj, h2, dg_pre_m, dsh2, dsc2, dcw = _mix_in_bwd(dx2, x1, du, dv, dconv, a, g, sh2, sc2, g_pre_m, wmi, cw_pad)
    gwmi = _wgrad("wgrad_mix_in", h2.reshape(t, D), dproj.reshape(t, 4 * WA), WA, True)
    tok = scatter_start("mix", [gwmi, gwmo], dg_pre_m)
    grad_x, dp1, h1, a1, df1, dg_pre_f1, dg_post_f1, dsh1, dsc1, dgt1 = _ffn_bwd(
        dx1, x, f1, p1, sh1 + tok[0, 0], sc1, gt1, g_pre_f1, g_post_f1, w1i, w1o)
    gw1i = _wgrad("wgrad_f1_in", h1.reshape(t, D), dp1.reshape(t, 2 * DFF), 2 * DFF // NCHIP, True)
    gw1o = chip4(_wgrad("wgrad_f1_out", a1.reshape(t, DFF), df1.reshape(t, D), D // 2, False), DFF // NCHIP)
    c_arr = ac.reshape(1).astype(jnp.int32)
    sib = _pair_exchange("pair_f1", [gw1i[1], gw1o[1]])
    pair_i = _pair_sum("pairsum_f1_in", gw1i[0], sib[0], c_arr)
    pair_o = _pair_sum("pairsum_f1_out", gw1o[0], sib[1], c_arr)
    tok = scatter_start("f1", [pair_i, pair_o], dg_post_f1)
    reduce_and_update("f2", ("w_f2_in", "w_f2_out"), [gw2i, gw2o], tok)
    reduce_and_update("mix", ("w_mix_in", "w_mix_out"), [gwmi, gwmo], out["w_f2_out"][3])

    dada = jnp.concatenate([q.reshape(nb, D) for q in (dsh1, dsc1, dgt1, dsh2, dsc2, dgt2, dsh3, dsc3, dgt3)], axis=1)
    vec_grads = dict(g_pre_f1=dg_pre_f1, g_post_f1=dg_post_f1, g_pre_m=dg_pre_m, g_post_m=dg_post_m,
                     g_pre_f2=dg_pre_f2, g_post_f2=dg_post_f2)
    pair_grads = dict(gmlp_norm_g=dgn_g, gmlp_norm_b=dgn_b, conv_b=dcb, conv_norm_g=dcn_g, conv_norm_b=dcn_b,
                      g_out_a=dgo_a, g_out_b=dgo_b)
    pack = _pack_small([vec_grads[n] for n in VEC_ORDER], [pair_grads[n] for n in PAIR_ORDER], dbsp, dwcat, lsum,
                       out["w_mix_out"][3])
    pack_all, dcw_all, dada_all8 = _allgather8("gather_small", [pack, dcw, dada.reshape(8, (nb * NMOD * D) // 8)])
    dada_all = dada_all8.reshape(NDEV * nb, NMOD * D)
    dcw_mine = lax.dynamic_slice(dcw_all, (0, 0, j_chip * (WB // NCHIP)), (NDEV, HALO, WB // NCHIP))
    small = {n: (wts[n], mom[n], var[n]) for n in list(VEC_ORDER) + list(PAIR_ORDER) + ["b_spatial", "w_spatial", "conv_w", "b_ada"]}
    small_out, loss = _small_adam(pack_all, dcw_mine, dada_all, small)
    out.update(small_out)
    dada_sh = lax.dynamic_slice(dada_all, (0, j_chip * ADA_SH), (NDEV * nb, ADA_SH))
    out["w_ada"] = tuple(r[None] for r in _ada_bwd_adam(c_all, dada_sh, w_ada[0], m_w_ada[0], v_w_ada[0]))
    srcs, lands, ssem, rsem = scatter["f1"]
    recv = _split_wait("gs_wait_f1", srcs, lands, ssem, rsem, _scatter_plan(len(srcs)), out["w_ada"][3])
    names = ("w_f1_in", "w_f1_out")
    mine = [_sum4("sum4_" + n, p[0], recv[k], j_arr)
            for k, (n, p) in enumerate(zip(names, (pair_i, pair_o)))]
    theirs = _sibling_swap("swap_f1", mine)
    for k, n in enumerate(names):
        out[n] = tuple(r[None] for r in _adam_halves("adam_" + n, wts[n][0], mom[n][0], var[n][0], mine[k], theirs[k],
                                                     c_arr))

    res = [loss, grad_x]
    for k in range(4):
        res += [out[n][k] for n in WEIGHTS]
    return tuple(res)
```

```python
import functools

import jax
import jax.numpy as jnp
from jax import lax
from jax.experimental import pallas as pl
from jax.experimental.pallas import tpu as pltpu

D = 1024
DFF = 2816
WA = 512
WB = 512
NH = 8
HD = 64
CH = 128
CK = 31
HALO = 32
NMOD = 9
EPS = 1e-6
NCHIP = 4
NDEV = 8
FBLK = DFF // 2
ADA_SH = NMOD * D // NCHIP

LR, B1, B2, EPS_A, WD, STEP = 0.001, 0.9, 0.999, 1e-08, 0.01, 10

F32 = jnp.float32
BF16 = jnp.bfloat16
MESH = pl.DeviceIdType.MESH
ANY = pl.BlockSpec(memory_space=pl.ANY)
VMEM_FULL = pl.BlockSpec(memory_space=pltpu.VMEM)
VMEM_LIMIT = 56 * 1024 * 1024

NT = (((1,), (1,)), ((), ()))
TN = (((0,), (0,)), ((), ()))


def _dot(a, b):
    return jnp.dot(a, b, preferred_element_type=F32)


def _dot_nt(a, b):
    return lax.dot_general(a, b, NT, preferred_element_type=F32)


def _dot_tn(a, b):
    return lax.dot_general(a, b, TN, preferred_element_type=F32)


def _cparams():
    return pltpu.CompilerParams(vmem_limit_bytes=VMEM_LIMIT)


def _allgather8(name, arrs):
    n = len(arrs)

    def body(*refs):
        ins, outs = refs[:n], refs[n:2 * n]
        send_sems, recv_sems, local_sems = refs[2 * n:]
        x, y, c = lax.axis_index("x"), lax.axis_index("y"), lax.axis_index("c")
        me, sibling = (x, y, c), (x, y, 1 - c)
        chips = [(1 - x, y), (x, 1 - y), (1 - x, 1 - y)]

        def copy(a, k, block, to, src=None):
            rows = outs[a].at[4 * block[0] + 2 * block[1] + block[2]]
            return pltpu.make_async_remote_copy(
                src_ref=rows if src is None else src, dst_ref=rows,
                send_sem=send_sems.at[a, k], recv_sem=recv_sems.at[a, k],
                device_id=to, device_id_type=MESH)

        started, mine = [], []
        for a in range(n):
            loc = pltpu.make_async_copy(ins[a], outs[a].at[4 * x + 2 * y + c], local_sems.at[a])
            loc.start()
            mine.append(loc)
            first = [copy(a, 0, me, sibling, src=ins[a])]
            first += [copy(a, 1 + j, me, (*chip, c), src=ins[a]) for j, chip in enumerate(chips)]
            for cp in first:
                cp.start()
            started += first
        for a in range(n):
            for j, chip in enumerate(chips):
                copy(a, 1 + j, (*chip, c), me).wait_recv()
                fwd = copy(a, 4 + j, (*chip, c), sibling)
                fwd.start()
                started.append(fwd)
        for a in range(n):
            copy(a, 0, sibling, me).wait_recv()
            for j, chip in enumerate(chips):
                copy(a, 4 + j, (*chip, 1 - c), me).wait_recv()
        for cp in started:
            cp.wait_send()
        for loc in mine:
            loc.wait()

    return pl.pallas_call(
        body, name=name,
        out_shape=[jax.ShapeDtypeStruct((NDEV,) + a.shape, a.dtype) for a in arrs],
        in_specs=[ANY] * n, out_specs=[ANY] * n,
        scratch_shapes=[pltpu.SemaphoreType.DMA((n, 7)), pltpu.SemaphoreType.DMA((n, 7)),
                        pltpu.SemaphoreType.DMA((n,))],
    )(*arrs)


def _chip_relations(x, y):
    return [(1 - x, y), (x, 1 - y), (1 - x, 1 - y)]


def _exchange(name, arrs, out_shapes, plan):
    n = len(arrs)
    n_out = len(out_shapes)

    def body(*refs):
        ins, outs = refs[:n], refs[n:n + n_out]
        send_sems, recv_sems, local_sems = refs[n + n_out:]
        x, y, c = lax.axis_index("x"), lax.axis_index("y"), lax.axis_index("c")
        local, sends = plan(x, y, c, ins, outs)
        locs = [pltpu.make_async_copy(s, d, local_sems.at[i]) for i, (s, d) in enumerate(local)]
        for loc in locs:
            loc.start()
        cps = [pltpu.make_async_remote_copy(src_ref=s, dst_ref=d, send_sem=send_sems.at[i], recv_sem=recv_sems.at[i],
                                            device_id=peer, device_id_type=MESH)
               for i, (s, d, peer, _) in enumerate(sends)]
        for cp in cps:
            cp.start()
        for i, (s, _, peer, landing) in enumerate(sends):
            pltpu.make_async_remote_copy(src_ref=s, dst_ref=landing, send_sem=send_sems.at[i], recv_sem=recv_sems.at[i],
                                         device_id=peer, device_id_type=MESH).wait_recv()
        for cp in cps:
            cp.wait_send()
        for loc in locs:
            loc.wait()

    return n, n_out, body


def _run_exchange(name, arrs, out_shapes, plan, n_local, n_send):
    n, n_out, body = _exchange(name, arrs, out_shapes, plan)
    return pl.pallas_call(
        body, name=name, out_shape=out_shapes,
        in_specs=[ANY] * n, out_specs=[ANY] * n_out,
        scratch_shapes=[pltpu.SemaphoreType.DMA((n_send,)), pltpu.SemaphoreType.DMA((n_send,)),
                        pltpu.SemaphoreType.DMA((max(n_local, 1),))],
    )(*arrs)


def _chip_allgather(name, arrs):
    n = len(arrs)

    def plan(x, y, c, ins, outs):
        j_me = 2 * x + y
        local = [(ins[a], outs[a].at[j_me]) for a in range(n)]
        sends = []
        for a in range(n):
            for (px, py) in _chip_relations(x, y):
                sends.append((ins[a], outs[a].at[j_me], (px, py, c), outs[a].at[2 * px + py]))
        return local, sends

    shapes = [jax.ShapeDtypeStruct((NCHIP,) + a.shape, a.dtype) for a in arrs]
    return _run_exchange(name, arrs, shapes, plan, n, 3 * n)


def _chip_scatter(name, arrs):
    n = len(arrs)

    def plan(x, y, c, ins, outs):
        sends = []
        for a in range(n):
            for k, (px, py) in enumerate(_chip_relations(x, y)):
                sends.append((ins[a].at[2 * px + py], outs[a].at[k], (px, py, c), outs[a].at[k]))
        return [], sends

    shapes = [jax.ShapeDtypeStruct((3,) + a.shape[1:], a.dtype) for a in arrs]
    return _run_exchange(name, arrs, shapes, plan, 0, 3 * n)


def _sibling_swap(name, arrs):
    n = len(arrs)

    def plan(x, y, c, ins, outs):
        return [], [(ins[a], outs[a], (x, y, 1 - c), outs[a]) for a in range(n)]

    shapes = [jax.ShapeDtypeStruct(a.shape, a.dtype) for a in arrs]
    return _run_exchange(name, arrs, shapes, plan, 0, n)


HBM = pl.BlockSpec(memory_space=pltpu.HBM)
SEM = pl.BlockSpec(memory_space=pltpu.SEMAPHORE)
EFFECT = pltpu.SideEffectType.DATAFLOW_SIDE_EFFECTING


def _split_start(name, srcs, lands, plan, n_send, after):
    n, nl = len(srcs), len(lands)

    def body(*refs):
        src, land = refs[:n], refs[n:n + nl]
        send_sems, recv_sems = refs[n + nl + 1], refs[n + nl + 2]
        token = refs[-2]
        local_sems = refs[-1]
        x, y, c = lax.axis_index("x"), lax.axis_index("y"), lax.axis_index("c")
        local, sends = plan(x, y, c, src, land)
        locs = [pltpu.make_async_copy(s, d, local_sems.at[i]) for i, (s, d) in enumerate(local)]
        for loc in locs:
            loc.start()
        for loc in locs:
            loc.wait()
        for i, (s, d, peer, _) in enumerate(sends):
            pltpu.make_async_remote_copy(src_ref=s, dst_ref=d, send_sem=send_sems.at[i], recv_sem=recv_sems.at[i],
                                         device_id=peer, device_id_type=MESH).start()
        token[...] = jnp.zeros_like(token)

    thru = [pltpu.HBM(a.shape, a.dtype) for a in list(srcs) + list(lands)]
    res = pl.pallas_call(
        body, name=name,
        out_shape=(pltpu.SemaphoreType.DMA((n_send,)), pltpu.SemaphoreType.DMA((n_send,)), *thru,
                   jax.ShapeDtypeStruct((8, 128), F32)),
        in_specs=[HBM] * (n + nl) + [ANY],
        out_specs=(SEM, SEM, *([HBM] * (n + nl)), pl.BlockSpec(memory_space=pltpu.VMEM)),
        input_output_aliases={i: 2 + i for i in range(n + nl)},
        scratch_shapes=[pltpu.SemaphoreType.DMA((max(len(srcs), 1),))],
        compiler_params=pltpu.CompilerParams(has_side_effects=EFFECT),
    )(*[pltpu.with_memory_space_constraint(a, pltpu.HBM) for a in list(srcs) + list(lands)], after)
    return res[0], res[1], list(res[2:2 + n]), list(res[2 + n:2 + n + nl]), res[-1]


def _split_wait(name, srcs, lands, send_sems, recv_sems, plan, after):
    n, nl = len(srcs), len(lands)

    def body(*refs):
        src, land = refs[:n], refs[n:n + nl]
        send_sems, recv_sems = refs[n + nl], refs[n + nl + 1]
        x, y, c = lax.axis_index("x"), lax.axis_index("y"), lax.axis_index("c")
        _, sends = plan(x, y, c, src, land)
        for i, (s, _, peer, landing) in enumerate(sends):
            cp = pltpu.make_async_remote_copy(src_ref=s, dst_ref=landing, send_sem=send_sems.at[i],
                                              recv_sem=recv_sems.at[i], device_id=peer, device_id_type=MESH)
            cp.wait_send()
            cp.wait_recv()

    thru = [pltpu.HBM(a.shape, a.dtype) for a in list(srcs) + list(lands)]
    res = pl.pallas_call(
        body, name=name, out_shape=tuple(thru),
        in_specs=[HBM] * (n + nl) + [SEM, SEM, ANY], out_specs=tuple([HBM] * (n + nl)),
        input_output_aliases={i: i for i in range(n + nl)},
        compiler_params=pltpu.CompilerParams(has_side_effects=EFFECT),
    )(*srcs, *lands, send_sems, recv_sems, after)
    return list(res[n:])


def _split_forward(name, srcs, lands, send_a, recv_a, plan_a, plan_b, n_b, after):
    n, nl = len(srcs), len(lands)

    def body(*refs):
        src, land = refs[:n], refs[n:n + nl]
        send_a, recv_a = refs[n + nl], refs[n + nl + 1]
        send_b, recv_b = refs[n + nl + 3], refs[n + nl + 4]
        token = refs[-1]
        x, y, c = lax.axis_index("x"), lax.axis_index("y"), lax.axis_index("c")
        _, first = plan_a(x, y, c, src, land)
        for i, (s, _, peer, landing) in enumerate(first):
            cp = pltpu.make_async_remote_copy(src_ref=s, dst_ref=landing, send_sem=send_a.at[i],
                                              recv_sem=recv_a.at[i], device_id=peer, device_id_type=MESH)
            cp.wait_send()
            cp.wait_recv()
        _, second = plan_b(x, y, c, src, land)
        for i, (s, d, peer, _) in enumerate(second):
            pltpu.make_async_remote_copy(src_ref=s, dst_ref=d, send_sem=send_b.at[i], recv_sem=recv_b.at[i],
                                         device_id=peer, device_id_type=MESH).start()
        token[...] = jnp.zeros_like(token)

    thru = [pltpu.HBM(a.shape, a.dtype) for a in lands]
    res = pl.pallas_call(
        body, name=name,
        out_shape=(pltpu.SemaphoreType.DMA((n_b,)), pltpu.SemaphoreType.DMA((n_b,)), *thru,
                   jax.ShapeDtypeStruct((8, 128), F32)),
        in_specs=[HBM] * (n + nl) + [SEM, SEM, ANY],
        out_specs=(SEM, SEM, *([HBM] * nl), pl.BlockSpec(memory_space=pltpu.VMEM)),
        input_output_aliases={n + i: 2 + i for i in range(nl)},
        compiler_params=pltpu.CompilerParams(has_side_effects=EFFECT),
    )(*srcs, *lands, send_a, recv_a, after)
    return res[0], res[1], list(res[2:2 + nl]), res[-1]


def _gather_plans(shapes):
    n = len(shapes)

    def halves(a, c):
        rows = shapes[a][0] // 2
        return pl.ds(pl.multiple_of(c * rows, 16), rows), pl.ds(pl.multiple_of((1 - c) * rows, 16), rows)

    def split(a):
        return shapes[a][0] % 32 == 0

    def plan_a(x, y, c, src, land):
        j_me = 2 * x + y
        sends = []
        for a in range(n):
            for (px, py) in _chip_relations(x, y):
                if split(a):
                    mine, _ = halves(a, c)
                    sends.append((src[a].at[mine], land[a].at[j_me, mine], (px, py, c), land[a].at[2 * px + py, mine]))
                else:
                    sends.append((src[a], land[a].at[j_me], (px, py, c), land[a].at[2 * px + py]))
        return [], sends

    def plan_b(x, y, c, src, land):
        sends = []
        for a in range(n):
            if split(a):
                mine, other = halves(a, c)
                for (px, py) in _chip_relations(x, y):
                    j = 2 * px + py
                    sends.append((land[a].at[j, mine], land[a].at[j, mine], (x, y, 1 - c), land[a].at[j, other]))
        return [], sends

    n_b = 3 * sum(1 for a in range(n) if split(a))
    return plan_a, plan_b, n_b


def _allgather_plan(n):
    flips = [(dx, dy, dc) for dx in (0, 1) for dy in (0, 1) for dc in (0, 1) if dx or dy or dc]

    def plan(x, y, c, src, land):
        sends = []
        for a in range(n):
            for dx, dy, dc in flips:
                px, py, pc = x ^ dx, y ^ dy, c ^ dc
                sends.append((src[a], land[a].at[4 * x + 2 * y + c], (px, py, pc), land[a].at[4 * px + 2 * py + pc]))
        return [], sends

    return plan


def _scatter_plan(n):
    def plan(x, y, c, src, land):
        sends = []
        for a in range(n):
            for k, (px, py) in enumerate(_chip_relations(x, y)):
                sends.append((src[a].at[2 * px + py], land[a].at[k], (px, py, c), land[a].at[k]))
        return [], sends

    return plan


def _rms(x):
    r = lax.rsqrt(jnp.mean(x * x, axis=-1, keepdims=True) + EPS)
    return x * r, r


def _rms_bwd(dy, n, r, g):
    dg = jnp.sum(dy * n, axis=0, keepdims=True)
    dn = dy * g
    dx = r * (dn - n * jnp.mean(dn * n, axis=-1, keepdims=True))
    return dx, dg


def _ln(x):
    mu = jnp.mean(x, axis=-1, keepdims=True)
    xc = x - mu
    rstd = lax.rsqrt(jnp.mean(xc * xc, axis=-1, keepdims=True) + EPS)
    return xc * rstd, rstd


def _ln_bwd(dy, xhat, rstd, g):
    dg = jnp.sum(dy * xhat, axis=0, keepdims=True)
    db = jnp.sum(dy, axis=0, keepdims=True)
    dxh = dy * g
    dx = rstd * (dxh - jnp.mean(dxh, axis=-1, keepdims=True) - xhat * jnp.mean(dxh * xhat, axis=-1, keepdims=True))
    return dx, dg, db


def _sigmoid(x):
    return jax.nn.sigmoid(x)


def _dsilu(x, s):
    return s * (1.0 + x * (1.0 - s))


def _adam(w, g, m, v):
    m = B1 * m + (1.0 - B1) * g
    v = B2 * v + (1.0 - B2) * (g * g)
    m_hat = m / (1.0 - B1 ** STEP)
    v_hat = v / (1.0 - B2 ** STEP)
    delta = -LR * (m_hat / (jnp.sqrt(v_hat) + EPS_A) + WD * w)
    return delta, m, v


def _head_mask(shape):
    lane = lax.broadcasted_iota(jnp.int32, shape, len(shape) - 1)
    return [(lane >= h * HD) & (lane < (h + 1) * HD) for h in range(NH)]


def _first(b, i):
    return jnp.logical_and(b == 0, i == 0)


def _acc(ref, val, first):
    @pl.when(first)
    def _():
        ref[...] = val

    @pl.when(jnp.logical_not(first))
    def _():
        ref[...] += val


def _ada_fwd(c_all, w_sh, b_sh):
    nb = c_all.shape[0]
    tn = 768

    def body(c_ref, w_ref, b_ref, o_ref):
        cv = c_ref[...]
        cs = (cv * _sigmoid(cv)).astype(BF16)
        o_ref[...] = _dot(cs, w_ref[...].astype(BF16)) + b_ref[...]

    return pl.pallas_call(
        body, name="ada_fwd", grid=(ADA_SH // tn,),
        out_shape=jax.ShapeDtypeStruct((nb, ADA_SH), F32),
        in_specs=[pl.BlockSpec((nb, D), lambda j: (0, 0)), pl.BlockSpec((D, tn), lambda j: (0, j)),
                  pl.BlockSpec((1, tn), lambda j: (0, j))],
        out_specs=pl.BlockSpec((nb, tn), lambda j: (0, j)),
        compiler_params=_cparams(),
    )(c_all, w_sh, b_sh)


def _ada_bwd_adam(c_all, dada_sh, w, m, v):
    nb = c_all.shape[0]
    tn = 768

    def body(c_ref, d_ref, w_ref, m_ref, v_ref, g_out, d_out, m_out, v_out):
        cv = c_ref[...]
        cs = (cv * _sigmoid(cv)).astype(BF16)
        g = _dot_tn(cs, d_ref[...].astype(BF16))
        delta, m2, v2 = _adam(w_ref[...], g, m_ref[...], v_ref[...])
        g_out[...] = g
        d_out[...] = delta
        m_out[...] = m2
        v_out[...] = v2

    big = pl.BlockSpec((D, tn), lambda j: (0, j))
    shape = jax.ShapeDtypeStruct((D, ADA_SH), F32)
    return pl.pallas_call(
        body, name="ada_bwd_adam", grid=(ADA_SH // tn,),
        out_shape=[shape] * 4,
        in_specs=[pl.BlockSpec((nb, D), lambda j: (0, 0)), pl.BlockSpec((nb, tn), lambda j: (0, j)), big, big, big],
        out_specs=[big] * 4,
        compiler_params=_cparams(),
    )(c_all, dada_sh, w, m, v)


def _tok_specs(tm, width):
    return pl.BlockSpec((1, tm, width), lambda b, i: (b, i, 0))


def _mod_spec():
    return pl.BlockSpec((1, 1, D), lambda b, i: (b, 0, 0))


def _row_spec(width=D):
    return pl.BlockSpec((1, width), lambda b, i: (0, 0))


def _ffn_fwd(x, sh, sc, gt, g_pre, g_post, w_in4, w_out, target=None):
    nb, s, _ = x.shape
    tm = min(256, s)
    with_loss = target is not None

    def body(*refs):
        if with_loss:
            (x_ref, sh_ref, sc_ref, gt_ref, gpre_ref, gpost_ref, win_ref, wout_ref, tgt_ref,
             xo_ref, f_ref, p_ref, ls_ref) = refs
        else:
            (x_ref, sh_ref, sc_ref, gt_ref, gpre_ref, gpost_ref, win_ref, wout_ref,
             xo_ref, f_ref, p_ref) = refs
        xv = x_ref[0]
        n, _ = _rms(xv)
        h = (n * gpre_ref[...]) * (1.0 + sc_ref[0]) + sh_ref[0]
        hb = h.astype(BF16)
        acc = jnp.zeros((tm, D), F32)
        for j in range(2):
            gate = _dot(hb, win_ref[j])
            up = _dot(hb, win_ref[2 + j])
            p_ref[0, :, j * FBLK:(j + 1) * FBLK] = gate.astype(BF16)
            p_ref[0, :, DFF + j * FBLK:DFF + (j + 1) * FBLK] = up.astype(BF16)
            a = (gate * _sigmoid(gate)) * up
            acc = acc + _dot(a.astype(BF16), wout_ref[j * FBLK:(j + 1) * FBLK, :])
        f_ref[0] = acc
        nf, _ = _rms(acc)
        out = xv + (0.5 * gt_ref[0]) * (nf * gpost_ref[...])
        if with_loss:
            err = out - tgt_ref[0]
            xo_ref[0] = err * (1.0 / D)
            row = jnp.sum(err * err, axis=0, keepdims=True)
            part = row[:, 0:128]
            for k in range(1, D // 128):
                part = part + row[:, k * 128:(k + 1) * 128]
            _acc(ls_ref, part, _first(pl.program_id(0), pl.program_id(1)))
        else:
            xo_ref[0] = out

    in_specs = [_tok_specs(tm, D), _mod_spec(), _mod_spec(), _mod_spec(), _row_spec(), _row_spec(), VMEM_FULL, VMEM_FULL]
    args = [x, sh, sc, gt, g_pre, g_post, w_in4, w_out]
    out_shape = [jax.ShapeDtypeStruct((nb, s, D), F32), jax.ShapeDtypeStruct((nb, s, D), F32),
                 jax.ShapeDtypeStruct((nb, s, 2 * DFF), BF16)]
    out_specs = [_tok_specs(tm, D), _tok_specs(tm, D), _tok_specs(tm, 2 * DFF)]
    if with_loss:
        in_specs.append(_tok_specs(tm, D))
        args.append(target)
        out_shape.append(jax.ShapeDtypeStruct((1, 128), F32))
        out_specs.append(pl.BlockSpec((1, 128), lambda b, i: (0, 0)))
    return pl.pallas_call(
        body, name="ffn_loss_fwd" if with_loss else "ffn_fwd", grid=(nb, s // tm),
        out_shape=out_shape, in_specs=in_specs, out_specs=out_specs,
        compiler_params=_cparams(),
    )(*args)


def _ffn_bwd(dxo, x, f, p, sh, sc, gt, g_pre, g_post, w_in4, w_out):
    nb, s, _ = x.shape
    tm = min(256, s)

    def body(dxo_ref, x_ref, f_ref, p_ref, sh_ref, sc_ref, gt_ref, gpre_ref, gpost_ref, win_ref, wout_ref,
             dx_ref, dp_ref, h_ref, a_ref, df_ref, dgpre_ref, dgpost_ref, dsh_ref, dsc_ref, dgt_ref):
        b, i = pl.program_id(0), pl.program_id(1)
        dxo_v = dxo_ref[0]
        nf, q = _rms(f_ref[0])
        gpost = gpost_ref[...]
        dgt = jnp.sum(dxo_v * (0.5 * (nf * gpost)), axis=0, keepdims=True)
        do = dxo_v * (0.5 * gt_ref[0])
        df, dgpost = _rms_bwd(do, nf, q, gpost)
        dfb = df.astype(BF16)
        df_ref[0] = dfb
        xv = x_ref[0]
        n, r = _rms(xv)
        gpre = gpre_ref[...]
        ng = n * gpre
        scale1 = 1.0 + sc_ref[0]
        h = ng * scale1 + sh_ref[0]
        h_ref[0] = h.astype(BF16)
        dh = jnp.zeros((tm, D), F32)
        for j in range(2):
            gate = p_ref[0, :, j * FBLK:(j + 1) * FBLK].astype(F32)
            up = p_ref[0, :, DFF + j * FBLK:DFF + (j + 1) * FBLK].astype(F32)
            sg = _sigmoid(gate)
            act = gate * sg
            a_ref[0, :, j * FBLK:(j + 1) * FBLK] = (act * up).astype(BF16)
            da = _dot_nt(dfb, wout_ref[j * FBLK:(j + 1) * FBLK, :])
            dgate = (da * up * _dsilu(gate, sg)).astype(BF16)
            dup = (da * act).astype(BF16)
            dp_ref[0, :, j * FBLK:(j + 1) * FBLK] = dgate
            dp_ref[0, :, DFF + j * FBLK:DFF + (j + 1) * FBLK] = dup
            dh = dh + _dot_nt(dgate, win_ref[j]) + _dot_nt(dup, win_ref[2 + j])
        dsh = jnp.sum(dh, axis=0, keepdims=True)
        dsc = jnp.sum(dh * ng, axis=0, keepdims=True)
        dxn, dgpre = _rms_bwd(dh * scale1, n, r, gpre)
        dx_ref[0] = dxo_v + dxn
        _acc(dgpre_ref, dgpre, _first(b, i))
        _acc(dgpost_ref, dgpost, _first(b, i))
        _acc(dsh_ref, dsh[None], i == 0)
        _acc(dsc_ref, dsc[None], i == 0)
        _acc(dgt_ref, dgt[None], i == 0)

    tok = _tok_specs(tm, D)
    mod_shape = jax.ShapeDtypeStruct((nb, 1, D), F32)
    row_shape = jax.ShapeDtypeStruct((1, D), F32)
    return pl.pallas_call(
        body, name="ffn_bwd", grid=(nb, s // tm),
        out_shape=[jax.ShapeDtypeStruct((nb, s, D), F32), jax.ShapeDtypeStruct((nb, s, 2 * DFF), BF16),
                   jax.ShapeDtypeStruct((nb, s, D), BF16), jax.ShapeDtypeStruct((nb, s, DFF), BF16),
                   jax.ShapeDtypeStruct((nb, s, D), BF16), row_shape, row_shape, mod_shape, mod_shape, mod_shape],
        in_specs=[tok, tok, tok, _tok_specs(tm, 2 * DFF), _mod_spec(), _mod_spec(), _mod_spec(), _row_spec(), _row_spec(),
                  VMEM_FULL, VMEM_FULL],
        out_specs=[tok, _tok_specs(tm, 2 * DFF), tok, _tok_specs(tm, DFF), tok, _row_spec(), _row_spec(),
                   _mod_spec(), _mod_spec(), _mod_spec()],
        compiler_params=_cparams(),
    )(dxo, x, f, p, sh, sc, gt, g_pre, g_post, w_in4, w_out)


def _wgrad(name, a, b, col_block, chip_major):
    t, ka = a.shape
    n = b.shape[1]
    tk = min(t, 512)
    while tk * 2 <= t and t % (tk * 2) == 0 and 2 * (tk * 2) * max(ka, col_block) <= 6 * 1024 * 1024:
        tk *= 2
    nk = t // tk
    nblk = n // col_block

    def body(a_ref, b_ref, o_ref, obf_ref, acc_ref):
        k = pl.program_id(1)

        @pl.when(k == 0)
        def _():
            acc_ref[...] = jnp.zeros_like(acc_ref)

        acc_ref[...] += _dot_tn(a_ref[...], b_ref[...])

        @pl.when(k == nk - 1)
        def _():
            val = acc_ref[...]
            if chip_major:
                o_ref[0] = val
                obf_ref[0] = val.astype(BF16)
            else:
                o_ref[...] = val
                obf_ref[...] = val.astype(BF16)

    if chip_major:
        shape = (nblk, ka, col_block)
        ospec = pl.BlockSpec((1, ka, col_block), lambda j, k: (j, 0, 0))
    else:
        shape = (ka, n)
        ospec = pl.BlockSpec((ka, col_block), lambda j, k: (0, j))
    return pl.pallas_call(
        body, name=name, grid=(nblk, nk),
        out_shape=[jax.ShapeDtypeStruct(shape, F32), jax.ShapeDtypeStruct(shape, BF16)],
        in_specs=[pl.BlockSpec((tk, ka), lambda j, k: (k, 0)), pl.BlockSpec((tk, col_block), lambda j, k: (k, j))],
        out_specs=[ospec, ospec],
        scratch_shapes=[pltpu.VMEM((ka, col_block), F32)],
        compiler_params=_cparams(),
    )(a, b)


def _mix_in_fwd(x, sh, sc, g_pre, w_mi4):
    nb, s, _ = x.shape
    tm = min(512, s)

    def body(x_ref, sh_ref, sc_ref, gpre_ref, w_ref, u_ref, v_ref, a_ref, g_ref):
        n, _ = _rms(x_ref[0])
        hb = ((n * gpre_ref[...]) * (1.0 + sc_ref[0]) + sh_ref[0]).astype(BF16)
        for k, o_ref in enumerate((u_ref, v_ref, a_ref, g_ref)):
            o_ref[0] = _dot(hb, w_ref[k])

    shape = jax.ShapeDtypeStruct((nb, s, WA), F32)
    return pl.pallas_call(
        body, name="mix_in_fwd", grid=(nb, s // tm),
        out_shape=[shape] * 4,
        in_specs=[_tok_specs(tm, D), _mod_spec(), _mod_spec(), _row_spec(), VMEM_FULL],
        out_specs=[_tok_specs(tm, WA)] * 4,
        compiler_params=_cparams(),
    )(x, sh, sc, g_pre, w_mi4)


def _spatial_weights(wcat_ref, transposed):
    w = wcat_ref[...]
    row = lax.broadcasted_iota(jnp.int32, w.shape, 0)
    col = lax.broadcasted_iota(jnp.int32, w.shape, 1)
    keep = ((row & (CH - 1)) <= col) if transposed else ((col & (CH - 1)) <= row)
    return jnp.where(keep, w, 0.0).astype(BF16)


def _expand_heads(vc, masks):
    return jnp.concatenate([jnp.where(mk, vc, jnp.zeros_like(vc)) for mk in masks], axis=0)


def _spatial_bias(bspt_ref):
    return bspt_ref[...]


SHIFTS = 8
TAP_ROWS = 32


def _ext_rows(tm):
    return tm + HALO + SHIFTS


def _make_shifts(ext_ref, sh_ref, tm):
    ext_ref[tm + HALO:tm + HALO + SHIFTS, :] = jnp.zeros((SHIFTS, WB), F32)
    for r in range(SHIFTS):
        sh_ref[r] = ext_ref[r:r + tm + HALO, :]


def _conv_taps(sh_ref, w_ref, tm, taps, emit):
    def block(i, carry):
        r0 = pl.multiple_of(i * TAP_ROWS, TAP_ROWS)
        acc = jnp.zeros((TAP_ROWS, WB), F32)
        for o, k in taps:
            acc = acc + w_ref[k:k + 1, :] * sh_ref[o % SHIFTS, pl.ds(r0 + SHIFTS * (o // SHIFTS), TAP_ROWS), :]
        emit(r0, acc)
        return carry

    lax.fori_loop(0, tm // TAP_ROWS, block, 0)


def _halo_prev_spec(tm):
    return pl.BlockSpec((1, HALO, WB), lambda b, i: (b, jnp.maximum(i * (tm // HALO) - 1, 0), 0))


def _halo_next_spec(tm, s):
    return pl.BlockSpec((1, HALO, WB), lambda b, i: (b, jnp.minimum((i + 1) * (tm // HALO), s // HALO - 1), 0))


def _mix_mid_fwd(x, u, v, a, g, gt, gn_g, gn_b, wcat, bspt, conv_w, conv_b, cn_g, cn_b, go_a, go_b, w_mo, g_post):
    nb, s, _ = x.shape
    tm = min(512, s)

    def body(x_ref, u_ref, v_ref, a_ref, g_ref, ah_ref, gh_ref, gt_ref, gng_ref, gnb_ref, wcat_ref, bspt_ref,
             cw_ref, cb_ref, cng_ref, cnb_ref, goa_ref, gob_ref, wmo_ref, gpost_ref,
             xo_ref, conv_ref, y_ref, m_ref, ext_ref, sh_ref):
        i = pl.program_id(1)
        xhat, _ = _ln(v_ref[0])
        vb = (xhat * gng_ref[...] + gnb_ref[...]).astype(BF16)
        wsb = _spatial_weights(wcat_ref, False)
        bias = _spatial_bias(bspt_ref)
        masks = _head_mask((CH, WA))
        zs = []
        for cidx in range(tm // CH):
            vexp = _expand_heads(vb[cidx * CH:(cidx + 1) * CH, :], masks)
            zs.append(_dot(wsb, vexp) + bias)
        z = jnp.concatenate(zs, axis=0)
        na, _ = _rms(u_ref[0] * z)
        keep = jnp.where(i == 0, 0.0, 1.0).astype(F32)
        ext_ref[0:HALO, :] = (ah_ref[0] * _sigmoid(gh_ref[0])) * keep
        ext_ref[HALO:HALO + tm, :] = a_ref[0] * _sigmoid(g_ref[0])
        _make_shifts(ext_ref, sh_ref, tm)
        cb = cb_ref[...]

        def put_conv(r0, acc):
            conv_ref[0, pl.ds(r0, TAP_ROWS), :] = acc + cb

        _conv_taps(sh_ref, cw_ref, tm, [(k + HALO - (CK - 1), k) for k in range(CK)], put_conv)
        conv = conv_ref[0]
        chat, _ = _ln(conv)
        cln = chat * cng_ref[...] + cnb_ref[...]
        nbb, _ = _rms(cln * _sigmoid(cln))
        yb = jnp.concatenate([na * goa_ref[...], nbb * gob_ref[...]], axis=1).astype(BF16)
        y_ref[0] = yb
        m = _dot(yb, wmo_ref[...])
        m_ref[0] = m
        nm, _ = _rms(m)
        xo_ref[0] = x_ref[0] + gt_ref[0] * (nm * gpost_ref[...])

    t5 = _tok_specs(tm, WA)
    tok = _tok_specs(tm, D)
    r5 = _row_spec(WA)
    full = lambda shape: pl.BlockSpec(shape, lambda b, i: (0,) * len(shape))
    return pl.pallas_call(
        body, name="mix_mid_fwd", grid=(nb, s // tm),
        out_shape=[jax.ShapeDtypeStruct((nb, s, D), F32), jax.ShapeDtypeStruct((nb, s, WB), F32),
                   jax.ShapeDtypeStruct((nb, s, D), BF16), jax.ShapeDtypeStruct((nb, s, D), F32)],
        in_specs=[tok, t5, t5, t5, t5, _halo_prev_spec(tm), _halo_prev_spec(tm), _mod_spec(), r5, r5,
                  full((CH, NH * CH)), full((CH, WA)), full((HALO, WB)), r5, r5, r5, r5, r5, VMEM_FULL, _row_spec()],
        out_specs=[tok, t5, tok, tok],
        scratch_shapes=[pltpu.VMEM((_ext_rows(tm), WB), F32), pltpu.VMEM((SHIFTS, tm + HALO, WB), F32)],
        compiler_params=_cparams(),
    )(x, u, v, a, g, a, g, gt, gn_g, gn_b, wcat, bspt, conv_w, conv_b, cn_g, cn_b, go_a, go_b, w_mo, g_post)


def _mix_out_bwd(dxo, m, gt, g_post, w_mo):
    nb, s, _ = m.shape
    tm = min(512, s)

    def body(dxo_ref, m_ref, gt_ref, gpost_ref, wmo_ref, dy_ref, dm_ref, dgpost_ref, dgt_ref):
        b, i = pl.program_id(0), pl.program_id(1)
        dxo_v = dxo_ref[0]
        nm, q = _rms(m_ref[0])
        gpost = gpost_ref[...]
        dgt = jnp.sum(dxo_v * (nm * gpost), axis=0, keepdims=True)
        dm, dgpost = _rms_bwd(dxo_v * gt_ref[0], nm, q, gpost)
        dmb = dm.astype(BF16)
        dm_ref[0] = dmb
        dy_ref[0] = _dot_nt(dmb, wmo_ref[...])
        _acc(dgpost_ref, dgpost, _first(b, i))
        _acc(dgt_ref, dgt[None], i == 0)

    tok = _tok_specs(tm, D)
    return pl.pallas_call(
        body, name="mix_out_bwd", grid=(nb, s // tm),
        out_shape=[jax.ShapeDtypeStruct((nb, s, D), F32), jax.ShapeDtypeStruct((nb, s, D), BF16),
                   jax.ShapeDtypeStruct((1, D), F32), jax.ShapeDtypeStruct((nb, 1, D), F32)],
        in_specs=[tok, tok, _mod_spec(), _row_spec(), VMEM_FULL],
        out_specs=[tok, tok, _row_spec(), _mod_spec()],
        compiler_params=_cparams(),
    )(dxo, m, gt, g_post, w_mo)


def _mix_mid_bwd(dy, u, v, conv, gn_g, gn_b, wcat, wcat_t, bspt, cn_g, cn_b, go_a, go_b):
    nb, s, _ = dy.shape
    tm = min(512, s)
    nchunk = tm // CH

    def body(dy_ref, u_ref, v_ref, conv_ref, gng_ref, gnb_ref, wcat_ref, wcatt_ref, bspt_ref, cng_ref, cnb_ref,
             goa_ref, gob_ref,
             du_ref, dv_ref, dconv_ref, dwcat_ref, dbsp_ref, dgng_ref, dgnb_ref, dgoa_ref, dgob_ref,
             dcng_ref, dcnb_ref, dcb_ref):
        first = _first(pl.program_id(0), pl.program_id(1))
        dyv = dy_ref[0]
        xhat, rstd = _ln(v_ref[0])
        gng = gng_ref[...]
        vb = (xhat * gng + gnb_ref[...]).astype(BF16)
        wsb = _spatial_weights(wcat_ref, False)
        wsb_t = _spatial_weights(wcatt_ref, True)
        bias = _spatial_bias(bspt_ref)
        masks = _head_mask((CH, WA))
        vexps, zs = [], []
        for cidx in range(nchunk):
            vexp = _expand_heads(vb[cidx * CH:(cidx + 1) * CH, :], masks)
            vexps.append(vexp)
            zs.append(_dot(wsb, vexp) + bias)
        z = jnp.concatenate(zs, axis=0)
        uv = u_ref[0]
        na, ra = _rms(uv * z)
        dya, dgoa = _rms_bwd(dyv[:, 0:WA], na, ra, goa_ref[...])
        du_ref[0] = dya * z
        dz = dya * uv
        dwcat = jnp.zeros((CH, NH * CH), F32)
        dzsum = jnp.zeros((CH, WA), F32)
        dvlns = []
        for cidx in range(nchunk):
            dzc = dz[cidx * CH:(cidx + 1) * CH, :]
            dzsum = dzsum + dzc
            dzb = dzc.astype(BF16)
            dwcat = dwcat + _dot_nt(dzb, vexps[cidx])
            dvexp = _dot(wsb_t, dzb)
            dvl = jnp.zeros((CH, WA), F32)
            for h in range(NH):
                dvl = dvl + jnp.where(masks[h], dvexp[h * CH:(h + 1) * CH, :], 0.0)
            dvlns.append(dvl)
        dvln = jnp.concatenate(dvlns, axis=0)
        dv, dgng, dgnb = _ln_bwd(dvln, xhat, rstd, gng)
        dv_ref[0] = dv
        lane = lax.broadcasted_iota(jnp.int32, (NH, WA), 1)
        head = lax.broadcasted_iota(jnp.int32, (NH, WA), 0)
        sel = jnp.where((lane >= head * HD) & (lane < (head + 1) * HD), 1.0, 0.0).astype(F32)
        dbsp = lax.dot_general(sel, dzsum, NT, preferred_element_type=F32, precision=lax.Precision.HIGHEST)
        chat, crstd = _ln(conv_ref[0])
        cng = cng_ref[...]
        cln = chat * cng + cnb_ref[...]
        sg = _sigmoid(cln)
        nbb, rb = _rms(cln * sg)
        dyb, dgob = _rms_bwd(dyv[:, WA:D], nbb, rb, gob_ref[...])
        dconv, dcng, dcnb = _ln_bwd(dyb * _dsilu(cln, sg), chat, crstd, cng)
        dconv_ref[0] = dconv
        dcb = jnp.sum(dconv, axis=0, keepdims=True)
        for ref, val in ((dwcat_ref, dwcat), (dbsp_ref, dbsp), (dgng_ref, dgng), (dgnb_ref, dgnb), (dgoa_ref, dgoa),
                         (dgob_ref, dgob), (dcng_ref, dcng), (dcnb_ref, dcnb), (dcb_ref, dcb)):
            _acc(ref, val, first)

    t5 = _tok_specs(tm, WA)
    r5 = _row_spec(WA)
    full = lambda shape: pl.BlockSpec(shape, lambda b, i: (0,) * len(shape))
    big = jax.ShapeDtypeStruct((nb, s, WA), F32)
    row = jax.ShapeDtypeStruct((1, WA), F32)
    return pl.pallas_call(
        body, name="mix_mid_bwd", grid=(nb, s // tm),
        out_shape=[big, big, big, jax.ShapeDtypeStruct((CH, NH * CH), F32), jax.ShapeDtypeStruct((NH, CH), F32),
                   row, row, row, row, row, row, row],
        in_specs=[_tok_specs(tm, D), t5, t5, t5, r5, r5, full((CH, NH * CH)), full((NH * CH, CH)), full((CH, WA)),
                  r5, r5, r5, r5],
        out_specs=[t5, t5, t5, full((CH, NH * CH)), full((NH, CH)), r5, r5, r5, r5, r5, r5, r5],
        compiler_params=_cparams(),
    )(dy, u, v, conv, gn_g, gn_b, wcat, wcat_t, bspt, cn_g, cn_b, go_a, go_b)


def _mix_in_bwd(dxo, x, du, dv, dconv, a, g, sh, sc, g_pre, w_mi4, conv_w):
    nb, s, _ = x.shape
    tm = min(512, s)
    n_i = s // tm

    def body(dxo_ref, x_ref, du_ref, dv_ref, dc_ref, dch_ref, a_ref, g_ref, ah_ref, gh_ref, sh_ref, sc_ref,
             gpre_ref, w_ref, cw_ref,
             dx_ref, dproj_ref, h_ref, dgpre_ref, dsh_ref, dsc_ref, dcw_ref, ext_ref, shf_ref, dglu_ref):
        b, i = pl.program_id(0), pl.program_id(1)
        first = _first(b, i)
        av, gv = a_ref[0], g_ref[0]
        sg = _sigmoid(gv)
        dconv = dc_ref[0]
        ext_ref[0:tm, :] = dconv
        ext_ref[tm:tm + HALO, :] = dch_ref[0] * jnp.where(i == n_i - 1, 0.0, 1.0).astype(F32)
        _make_shifts(ext_ref, shf_ref, tm)

        def put_dglu(r0, acc):
            dglu_ref[pl.ds(r0, TAP_ROWS), :] = acc

        _conv_taps(shf_ref, cw_ref, tm, [(CK - 1 - k, k) for k in range(CK)], put_dglu)
        dglu = dglu_ref[...]
        ext_ref[0:HALO, :] = (ah_ref[0] * _sigmoid(gh_ref[0])) * jnp.where(i == 0, 0.0, 1.0).astype(F32)
        ext_ref[HALO:HALO + tm, :] = av * sg
        _make_shifts(ext_ref, shf_ref, tm)

        @pl.when(first)
        def _():
            dcw_ref[...] = jnp.zeros((HALO, WB), F32)

        for k in range(CK):
            o = k + HALO - (CK - 1)
            lo = SHIFTS * (o // SHIFTS)
            dcw_ref[k:k + 1, :] += jnp.sum(dconv * shf_ref[o % SHIFTS, lo:lo + tm, :], axis=0, keepdims=True)
        da = dglu * sg
        dg = dglu * av * (sg * (1.0 - sg))
        parts = [du_ref[0].astype(BF16), dv_ref[0].astype(BF16), da.astype(BF16), dg.astype(BF16)]
        dh = jnp.zeros((tm, D), F32)
        for k in range(4):
            dproj_ref[0, :, k * WA:(k + 1) * WA] = parts[k]
            dh = dh + _dot_nt(parts[k], w_ref[k])
        n, r = _rms(x_ref[0])
        gpre = gpre_ref[...]
        ng = n * gpre
        scale1 = 1.0 + sc_ref[0]
        h_ref[0] = (ng * scale1 + sh_ref[0]).astype(BF16)
        dsh = jnp.sum(dh, axis=0, keepdims=True)
        dsc = jnp.sum(dh * ng, axis=0, keepdims=True)
        dxn, dgpre = _rms_bwd(dh * scale1, n, r, gpre)
        dx_ref[0] = dxo_ref[0] + dxn
        _acc(dgpre_ref, dgpre, first)
        _acc(dsh_ref, dsh[None], i == 0)
        _acc(dsc_ref, dsc[None], i == 0)

    tok = _tok_specs(tm, D)
    t5 = _tok_specs(tm, WA)
    full = lambda shape: pl.BlockSpec(shape, lambda b, i: (0,) * len(shape))
    mod_shape = jax.ShapeDtypeStruct((nb, 1, D), F32)
    return pl.pallas_call(
        body, name="mix_in_bwd", grid=(nb, n_i),
        out_shape=[jax.ShapeDtypeStruct((nb, s, D), F32), jax.ShapeDtypeStruct((nb, s, 4 * WA), BF16),
                   jax.ShapeDtypeStruct((nb, s, D), BF16), jax.ShapeDtypeStruct((1, D), F32), mod_shape, mod_shape,
                   jax.ShapeDtypeStruct((HALO, WB), F32)],
        in_specs=[tok, tok, t5, t5, t5, _halo_next_spec(tm, s), t5, t5, _halo_prev_spec(tm), _halo_prev_spec(tm),
                  _mod_spec(), _mod_spec(), _row_spec(), VMEM_FULL, full((HALO, WB))],
        out_specs=[tok, _tok_specs(tm, 4 * WA), tok, _row_spec(), _mod_spec(), _mod_spec(), full((HALO, WB))],
        scratch_shapes=[pltpu.VMEM((_ext_rows(tm), WB), F32), pltpu.VMEM((SHIFTS, tm + HALO, WB), F32),
                        pltpu.VMEM((tm, WB), F32)],
        compiler_params=_cparams(),
    )(dxo, x, du, dv, dconv, dconv, a, g, a, g, sh, sc, g_pre, w_mi4, conv_w)


def _row_tile(rows, cols):
    best = 8
    for t in range(8, rows + 1, 8):
        if rows % t == 0 and t * cols * 4 <= 1536 * 1024:
            best = t
    return best


def _sum4(name, own4, recv, j_arr):
    _, rows, cols = own4.shape
    tr = _row_tile(rows, cols)

    def body(j_ref, own_ref, recv_ref, o_ref):
        del j_ref
        acc = own_ref[0]
        for k in range(3):
            acc = acc + recv_ref[k].astype(F32)
        o_ref[...] = acc

    return pl.pallas_call(
        body, name=name,
        grid_spec=pltpu.PrefetchScalarGridSpec(
            num_scalar_prefetch=1, grid=(rows // tr,),
            in_specs=[pl.BlockSpec((1, tr, cols), lambda i, j: (j[0], i, 0)),
                      pl.BlockSpec((3, tr, cols), lambda i, j: (0, i, 0))],
            out_specs=pl.BlockSpec((tr, cols), lambda i, j: (i, 0))),
        out_shape=jax.ShapeDtypeStruct((rows, cols), F32),
        compiler_params=_cparams(),
    )(j_arr, own4, recv)


def _pair_exchange(name, arrs):
    n = len(arrs)

    def plan(x, y, c, ins, outs):
        sends = []
        for a in range(n):
            rows = arrs[a].shape[1] // 2
            theirs = pl.ds(pl.multiple_of((1 - c) * rows, 16), rows)
            sends.append((ins[a].at[:, theirs], outs[a], (x, y, 1 - c), outs[a]))
        return [], sends

    shapes = [jax.ShapeDtypeStruct((a.shape[0], a.shape[1] // 2, a.shape[2]), a.dtype) for a in arrs]
    return _run_exchange(name, arrs, shapes, plan, 0, n)


def _pair_sum(name, g32, recv, c_arr):
    nblk, rows, cols = recv.shape
    tr = _row_tile(rows, cols)
    nh = rows // tr

    def body(c_ref, g_ref, r_ref, o32_ref, obf_ref):
        del c_ref
        val = g_ref[0] + r_ref[0].astype(F32)
        o32_ref[0] = val
        obf_ref[0] = val.astype(BF16)

    spec = pl.BlockSpec((1, tr, cols), lambda k, i, c: (k, i, 0))
    return pl.pallas_call(
        body, name=name,
        grid_spec=pltpu.PrefetchScalarGridSpec(
            num_scalar_prefetch=1, grid=(nblk, nh),
            in_specs=[pl.BlockSpec((1, tr, cols), lambda k, i, c: (k, c[0] * nh + i, 0)), spec],
            out_specs=[spec, spec]),
        out_shape=[jax.ShapeDtypeStruct(recv.shape, F32), jax.ShapeDtypeStruct(recv.shape, BF16)],
        compiler_params=_cparams(),
    )(c_arr, g32, recv)


def _adam_halves(name, w, m, v, mine, theirs, c_arr):
    rows, cols = w.shape
    tr = _row_tile(rows // 2, cols)
    nh = (rows // 2) // tr

    def body(c_ref, w_ref, m_ref, v_ref, mine_ref, theirs_ref, g_out, d_out, m_out, v_out):
        here = (pl.program_id(0) // nh) == c_ref[0]
        g = jnp.where(here, mine_ref[...], theirs_ref[...])
        delta, m2, v2 = _adam(w_ref[...], g, m_ref[...], v_ref[...])
        g_out[...] = g
        d_out[...] = delta
        m_out[...] = m2
        v_out[...] = v2

    spec = pl.BlockSpec((tr, cols), lambda i, c: (i, 0))
    shape = jax.ShapeDtypeStruct((rows, cols), F32)
    return pl.pallas_call(
        body, name=name,
        grid_spec=pltpu.PrefetchScalarGridSpec(
            num_scalar_prefetch=1, grid=(2 * nh,),
            in_specs=[spec, spec, spec,
                      pl.BlockSpec((tr, cols), lambda i, c: (jnp.clip(i - c[0] * nh, 0, nh - 1), 0)),
                      pl.BlockSpec((tr, cols), lambda i, c: (jnp.clip(i - (1 - c[0]) * nh, 0, nh - 1), 0))],
            out_specs=[spec] * 4),
        out_shape=[shape] * 4,
        compiler_params=_cparams(),
    )(c_arr, w, m, v, mine, theirs)


def _adam_big(name, w, m, v, ga, gb):
    rows, cols = w.shape
    tr = _row_tile(rows, cols)

    def body(w_ref, m_ref, v_ref, ga_ref, gb_ref, g_out, d_out, m_out, v_out):
        gsum = ga_ref[...] + gb_ref[...]
        delta, m2, v2 = _adam(w_ref[...], gsum, m_ref[...], v_ref[...])
        g_out[...] = gsum
        d_out[...] = delta
        m_out[...] = m2
        v_out[...] = v2

    spec = pl.BlockSpec((tr, cols), lambda i: (i, 0))
    shape = jax.ShapeDtypeStruct((rows, cols), F32)
    return pl.pallas_call(
        body, name=name, grid=(rows // tr,), out_shape=[shape] * 4,
        in_specs=[spec] * 5, out_specs=[spec] * 4, compiler_params=_cparams(),
    )(w, m, v, ga, gb)


PK_VEC = 0
PK_LOSS = 6
PK_PAIR = 8
PK_BSP = 16
PK_WCAT = 24
PK_ROWS = PK_WCAT + CH
PAIR_ORDER = ("gmlp_norm_g", "gmlp_norm_b", "conv_b", "conv_norm_g", "conv_norm_b", "g_out_a", "g_out_b")
VEC_ORDER = ("g_pre_f1", "g_post_f1", "g_pre_m", "g_post_m", "g_pre_f2", "g_post_f2")


def _pack_late(rows):
    counts = [r.shape[0] for r in rows]
    assert sum(counts) == 8

    def body(*refs):
        o_ref = refs[-1]
        at = 0
        for r, cnt in zip(refs[:-1], counts):
            o_ref[at:at + cnt, :] = r[...]
            at += cnt

    return pl.pallas_call(
        body, name="pack_late", out_shape=jax.ShapeDtypeStruct((8, D), F32),
        in_specs=[VMEM_FULL] * len(rows), out_specs=VMEM_FULL, compiler_params=_cparams(),
    )(*rows)


def _pack_small(vecs, pairs, dbsp, dwcat, lsum):
    def body(*refs):
        vec_refs = refs[:4]
        pair_refs = refs[4:11]
        dbsp_ref, dwcat_ref, lsum_ref, o_ref = refs[11:]
        o_ref[0:PK_WCAT, :] = jnp.zeros((PK_WCAT, D), F32)
        o_ref[PK_LOSS:PK_LOSS + 1, 0:128] = lsum_ref[...]
        for k, r in enumerate(vec_refs):
            o_ref[PK_VEC + 2 + k:PK_VEC + 3 + k, :] = r[...]
        for k, r in enumerate(pair_refs):
            row, half = PK_PAIR + k // 2, k % 2
            o_ref[row:row + 1, half * WA:(half + 1) * WA] = r[...]
        o_ref[PK_BSP:PK_BSP + NH, 0:CH] = dbsp_ref[...]
        o_ref[PK_WCAT:PK_ROWS, :] = dwcat_ref[...]

    args = list(vecs) + list(pairs) + [dbsp, dwcat, lsum]
    return pl.pallas_call(
        body, name="pack_small", out_shape=jax.ShapeDtypeStruct((PK_ROWS, D), F32),
        in_specs=[VMEM_FULL] * len(args), out_specs=VMEM_FULL, compiler_params=_cparams(),
    )(*args)


def _small_adam(pack_all, late_all, dcw_all, dada_all, params):
    names = list(VEC_ORDER) + list(PAIR_ORDER) + ["b_spatial", "w_spatial", "conv_w", "b_ada"]
    flat = []
    for nm in names:
        flat += list(params[nm])
    n_in = 4 + len(flat)

    def body(*refs):
        pack_ref, late_ref, dcw_ref, dada_ref = refs[:4]
        prm = refs[4:n_in]
        outs = refs[n_in:]

        def total(r0, nr, c0, nc):
            acc = pack_ref[0, r0:r0 + nr, c0:c0 + nc]
            for d in range(1, NDEV):
                acc = acc + pack_ref[d, r0:r0 + nr, c0:c0 + nc]
            return acc

        def emit(idx, g, getw, put):
            w_ref, m_ref, v_ref = prm[3 * idx:3 * idx + 3]
            delta, m2, v2 = _adam(getw(w_ref), g, getw(m_ref), getw(v_ref))
            for o_ref, val in zip(outs[4 * idx:4 * idx + 4], (g, delta, m2, v2)):
                put(o_ref, val)

        def whole(ref):
            return ref[...]

        def put_whole(ref, val):
            ref[...] = val

        idx = 0
        for k in range(6):
            if k < 2:
                g = late_ref[0, k:k + 1, :]
                for d in range(1, NDEV):
                    g = g + late_ref[d, k:k + 1, :]
            else:
                g = total(PK_VEC + k, 1, 0, D)
            emit(idx, g, whole, put_whole)
            idx += 1
        for k in range(7):
            emit(idx, total(PK_PAIR + k // 2, 1, (k % 2) * WA, WA), whole, put_whole)
            idx += 1
        emit(idx, total(PK_BSP, NH, 0, CH), lambda r: r[0], lambda r, val: r.__setitem__(0, val))
        idx += 1
        row = lax.broadcasted_iota(jnp.int32, (CH, CH), 0)
        col = lax.broadcasted_iota(jnp.int32, (CH, CH), 1)
        for h in range(NH):
            gh = jnp.where(col <= row, total(PK_WCAT, CH, h * CH, CH), 0.0)
            w_ref, m_ref, v_ref = prm[3 * idx:3 * idx + 3]
            delta, m2, v2 = _adam(w_ref[0, h], gh, m_ref[0, h], v_ref[0, h])
            for o_ref, val in zip(outs[4 * idx:4 * idx + 4], (gh, delta, m2, v2)):
                o_ref[0, h] = val
        idx += 1
        gcw = dcw_ref[0, 0:CK, :]
        for d in range(1, NDEV):
            gcw = gcw + dcw_ref[d, 0:CK, :]
        emit(idx, gcw, lambda r: r[0], lambda r, val: r.__setitem__(0, val))
        idx += 1
        emit(idx, jnp.sum(dada_ref[...], axis=0, keepdims=True), whole, put_whole)
        outs[-1][...] = jnp.sum(total(PK_LOSS, 1, 0, 128), axis=1, keepdims=True) * (0.5 / D)

    out_shape = []
    for nm in names:
        w = params[nm][0]
        out_shape += [jax.ShapeDtypeStruct(w.shape, F32)] * 4
    out_shape.append(jax.ShapeDtypeStruct((1, 1), F32))
    res = pl.pallas_call(
        body, name="small_adam", out_shape=out_shape,
        in_specs=[VMEM_FULL] * n_in, out_specs=[VMEM_FULL] * len(out_shape), compiler_params=_cparams(),
    )(pack_all, late_all, dcw_all, dada_all, *flat)
    return {nm: tuple(res[4 * k:4 * k + 4]) for k, nm in enumerate(names)}, res[-1].reshape(())


WEIGHTS = ['w_ada', 'b_ada', 'g_pre_f1', 'g_post_f1', 'w_f1_in', 'w_f1_out', 'g_pre_m', 'g_post_m', 'w_mix_in',
           'gmlp_norm_g', 'gmlp_norm_b', 'w_spatial', 'b_spatial', 'conv_w', 'conv_b', 'conv_norm_g', 'conv_norm_b',
           'g_out_a', 'g_out_b', 'w_mix_out', 'g_pre_f2', 'g_post_f2', 'w_f2_in', 'w_f2_out']
BIG = ('w_f1_in', 'w_f1_out', 'w_mix_in', 'w_mix_out', 'w_f2_in', 'w_f2_out')


def kernel(x, c, w_ada, b_ada, g_pre_f1, g_post_f1, w_f1_in, w_f1_out, g_pre_m, g_post_m, w_mix_in, gmlp_norm_g, gmlp_norm_b, w_spatial, b_spatial, conv_w, conv_b, conv_norm_g, conv_norm_b, g_out_a, g_out_b, w_mix_out, g_pre_f2, g_post_f2, w_f2_in, w_f2_out, loss_target, m_w_ada, m_b_ada, m_g_pre_f1, m_g_post_f1, m_w_f1_in, m_w_f1_out, m_g_pre_m, m_g_post_m, m_w_mix_in, m_gmlp_norm_g, m_gmlp_norm_b, m_w_spatial, m_b_spatial, m_conv_w, m_conv_b, m_conv_norm_g, m_conv_norm_b, m_g_out_a, m_g_out_b, m_w_mix_out, m_g_pre_f2, m_g_post_f2, m_w_f2_in, m_w_f2_out, v_w_ada, v_b_ada, v_g_pre_f1, v_g_post_f1, v_w_f1_in, v_w_f1_out, v_g_pre_m, v_g_post_m, v_w_mix_in, v_gmlp_norm_g, v_gmlp_norm_b, v_w_spatial, v_b_spatial, v_conv_w, v_conv_b, v_conv_norm_g, v_conv_norm_b, v_g_out_a, v_g_out_b, v_w_mix_out, v_g_pre_f2, v_g_post_f2, v_w_f2_in, v_w_f2_out):
    env = dict(locals())
    wts = {n: env[n] for n in WEIGHTS}
    mom = {n: env["m_" + n] for n in WEIGHTS}
    var = {n: env["v_" + n] for n in WEIGHTS}
    nb, s, _ = x.shape
    t = nb * s
    ax, ay, ac = lax.axis_index("x"), lax.axis_index("y"), lax.axis_index("c")
    j_chip = 2 * ax + ay
    dev = 4 * ax + 2 * ay + ac
    j_arr = j_chip.reshape(1).astype(jnp.int32)

    groups = (("w_f1_in", "w_f1_out"), ("w_mix_in", "w_mix_out"), ("w_f2_in", "w_f2_out"))
    def gather_start(gi, behind):
        srcs = [wts[n][0].astype(BF16) for n in groups[gi]] + ([conv_w[0]] if gi == 1 else [])
        plan_a, plan_b, n_b = _gather_plans([a.shape for a in srcs])
        lands = [lax.dynamic_update_index_in_dim(lax.empty((NCHIP,) + a.shape, a.dtype), a, j_chip, 0) for a in srcs]
        ssem, rsem, srcs, lands, token = _split_start("gw_start%d" % gi, srcs, lands, plan_a, 3 * len(srcs), behind)
        gather[gi] = (srcs, lands, ssem, rsem, plan_a, plan_b, n_b)
        return token

    def gather_forward(gi, behind):
        srcs, lands, ssem, rsem, plan_a, plan_b, n_b = gather[gi]
        ssem, rsem, lands, token = _split_forward("gw_fwd%d" % gi, srcs, lands, ssem, rsem, plan_a, plan_b, n_b, behind)
        gather[gi] = (lands, ssem, rsem, plan_b)
        return token

    def gathered(gi, behind):
        lands, ssem, rsem, plan_b = gather[gi]
        return _split_wait("gw_wait%d" % gi, [], lands, ssem, rsem, plan_b, behind)

    gather = {}
    (c_all8,) = _allgather8("gather_c", [c.reshape(8, (nb * D) // 8)])
    token = gather_start(0, c_all8)
    c_all = c_all8.reshape(NDEV * nb, D) + token[0, 0]
    b_sh = lax.dynamic_slice(b_ada, (0, j_chip * ADA_SH), (1, ADA_SH))
    ada_sh = _ada_fwd(c_all, w_ada[0], b_sh)
    token = gather_forward(0, ada_sh)
    (ada4,) = _chip_allgather("gather_ada", [ada_sh + token[0:1, 0:1]])
    token = gather_start(1, ada4)
    token = gather_start(2, token)
    ada4 = ada4 + token[0:1, 0:1]
    ada_me = lax.dynamic_slice(ada4, (0, dev * nb, 0), (NCHIP, nb, ADA_SH))
    ada_me = jnp.transpose(ada_me, (1, 0, 2)).reshape(nb, NMOD * D)
    sh1, sc1, gt1, sh2, sc2, gt2, sh3, sc3, gt3 = [ada_me[:, k * D:(k + 1) * D].reshape(nb, 1, D) for k in range(NMOD)]

    wcat = jnp.transpose(w_spatial[0], (1, 0, 2)).reshape(CH, NH * CH)
    wcat_t = jnp.transpose(w_spatial[0], (0, 2, 1)).reshape(NH * CH, CH)
    bspt = jnp.repeat(b_spatial[0].T, HD, axis=1)

    w1i, w1o = gathered(0, sh1)
    w1o = w1o.reshape(DFF, D)
    x1, f1, p1 = _ffn_fwd(x, sh1, sc1, gt1, g_pre_f1, g_post_f1, w1i, w1o)
    wmi, wmo, cw4 = gathered(1, gather_forward(1, x1))
    wmo = wmo.reshape(D, D)
    cw_full = jnp.transpose(cw4, (1, 0, 2)).reshape(CK, WB)
    cw_pad = jnp.pad(cw_full, ((0, HALO - CK), (0, 0)))
    u, v, a, g = _mix_in_fwd(x1, sh2, sc2, g_pre_m, wmi)
    x2, conv, yb, m = _mix_mid_fwd(x1, u, v, a, g, gt2, gmlp_norm_g, gmlp_norm_b, wcat, bspt, cw_pad, conv_b,
                                   conv_norm_g, conv_norm_b, g_out_a, g_out_b, wmo, g_post_m)
    w2i, w2o = gathered(2, gather_forward(2, x2))
    w2o = w2o.reshape(DFF, D)
    dx3, f2, p2, lsum = _ffn_fwd(x2, sh3, sc3, gt3, g_pre_f2, g_post_f2, w2i, w2o, target=loss_target)

    def chip4(pair, rows):
        return [arr.reshape(NCHIP, rows, arr.shape[-1]) for arr in pair]

    scatter = {}

    def scatter_start(tag, pairs, behind):
        srcs = [p[1] for p in pairs]
        lands = [lax.empty((3,) + a.shape[1:], a.dtype) for a in srcs]
        ssem, rsem, srcs, lands, token = _split_start("gs_start_" + tag, srcs, lands, _scatter_plan(len(srcs)),
                                                      3 * len(srcs), behind)
        scatter[tag] = (srcs, lands, ssem, rsem)
        return token

    def reduce_and_update(tag, names, pairs, behind):
        srcs, lands, ssem, rsem = scatter[tag]
        recv = _split_wait("gs_wait_" + tag, srcs, lands, ssem, rsem, _scatter_plan(len(srcs)), behind)
        part = [_sum4("sum4_" + n, pairs[k][0], recv[k], j_arr)
                for k, n in enumerate(names)]
        other = _sibling_swap("swap_" + tag, part)
        for k, n in enumerate(names):
            out[n] = tuple(r[None] for r in _adam_big("adam_" + n, wts[n][0], mom[n][0], var[n][0], part[k], other[k]))

    out = {}
    dx2, dp2, h3, a2, df2, dg_pre_f2, dg_post_f2, dsh3, dsc3, dgt3 = _ffn_bwd(
        dx3, x2, f2, p2, sh3, sc3, gt3, g_pre_f2, g_post_f2, w2i, w2o)
    gw2i = _wgrad("wgrad_f2_in", h3.reshape(t, D), dp2.reshape(t, 2 * DFF), 2 * DFF // NCHIP, True)
    gw2o = chip4(_wgrad("wgrad_f2_out", a2.reshape(t, DFF), df2.reshape(t, D), D // 2, False), DFF // NCHIP)
    tok = scatter_start("f2", [gw2i, gw2o], dg_post_f2)
    dy, dm, dg_post_m, dgt2 = _mix_out_bwd(dx2, m, gt2 + tok[0, 0], g_post_m, wmo)
    gwmo = chip4(_wgrad("wgrad_mix_out", yb.reshape(t, D), dm.reshape(t, D), D // 2, False), D // NCHIP)
    (du, dv, dconv, dwcat, dbsp, dgn_g, dgn_b, dgo_a, dgo_b, dcn_g, dcn_b, dcb) = _mix_mid_bwd(
        dy, u, v, conv, gmlp_norm_g, gmlp_norm_b, wcat, wcat_t, bspt, conv_norm_g, conv_norm_b, g_out_a, g_out_b)
    dx1, dproj, h2, dg_pre_m, dsh2, dsc2, dcw = _mix_in_bwd(dx2, x1, du, dv, dconv, a, g, sh2, sc2, g_pre_m, wmi, cw_pad)
    gwmi = _wgrad("wgrad_mix_in", h2.reshape(t, D), dproj.reshape(t, 4 * WA), WA, True)
    tok = scatter_start("mix", [gwmi, gwmo], dg_pre_m)

    def allgather_start(tag, arrs, behind):
        lands = [lax.dynamic_update_index_in_dim(lax.empty((NDEV,) + a.shape, a.dtype), a, dev, 0) for a in arrs]
        ssem, rsem, srcs, lands, token = _split_start("small_start_" + tag, arrs, lands, _allgather_plan(len(arrs)),
                                                      7 * len(arrs), behind)
        return (srcs, lands, ssem, rsem), token

    def allgather_wait(tag, state, behind):
        srcs, lands, ssem, rsem = state
        return _split_wait("small_wait_" + tag, srcs, lands, ssem, rsem, _allgather_plan(len(srcs)), behind)

    vec_grads = dict(g_pre_m=dg_pre_m, g_post_m=dg_post_m, g_pre_f2=dg_pre_f2, g_post_f2=dg_post_f2)
    pair_grads = dict(gmlp_norm_g=dgn_g, gmlp_norm_b=dgn_b, conv_b=dcb, conv_norm_g=dcn_g, conv_norm_b=dcn_b,
                      g_out_a=dgo_a, g_out_b=dgo_b)
    pack = _pack_small([vec_grads[n] for n in VEC_ORDER[2:]], [pair_grads[n] for n in PAIR_ORDER], dbsp, dwcat, lsum)
    dada_early = jnp.concatenate([q.reshape(nb, D) for q in (dsh2, dsc2, dgt2, dsh3, dsc3, dgt3)], axis=1)
    early, tok2 = allgather_start("early", [pack, dcw, dada_early.reshape(8, (nb * 6 * D) // 8)], tok)
    grad_x, dp1, h1, a1, df1, dg_pre_f1, dg_post_f1, dsh1, dsc1, dgt1 = _ffn_bwd(
        dx1, x, f1, p1, sh1 + tok2[0, 0], sc1, gt1, g_pre_f1, g_post_f1, w1i, w1o)
    late_pack = _pack_late([dg_pre_f1, dg_post_f1] + [q.reshape(nb, D) for q in (dsh1, dsc1, dgt1)])
    late, tok2 = allgather_start("late", [late_pack], dg_post_f1)
    gw1i = _wgrad("wgrad_f1_in", h1.reshape(t, D), dp1.reshape(t, 2 * DFF), 2 * DFF // NCHIP, True)
    gw1o = chip4(_wgrad("wgrad_f1_out", a1.reshape(t, DFF), df1.reshape(t, D), D // 2, False), DFF // NCHIP)
    c_arr = ac.reshape(1).astype(jnp.int32)
    sib = _pair_exchange("pair_f1", [gw1i[1], gw1o[1]])
    pair_i = _pair_sum("pairsum_f1_in", gw1i[0], sib[0], c_arr)
    pair_o = _pair_sum("pairsum_f1_out", gw1o[0], sib[1], c_arr)
    tok = scatter_start("f1", [pair_i, pair_o], tok2)
    reduce_and_update("f2", ("w_f2_in", "w_f2_out"), [gw2i, gw2o], tok)
    reduce_and_update("mix", ("w_mix_in", "w_mix_out"), [gwmi, gwmo], out["w_f2_out"][3])

    pack_all, dcw_all, dada_early8 = allgather_wait("early", early, out["w_mix_out"][3])
    (late_all,) = allgather_wait("late", late, pack_all)
    dada_late = jnp.transpose(late_all[:, 2:8, :].reshape(NDEV, 3, nb, D), (0, 2, 1, 3)).reshape(NDEV * nb, 3 * D)
    dada_all = jnp.concatenate([dada_late, dada_early8.reshape(NDEV * nb, 6 * D)], axis=1)
    dcw_mine = lax.dynamic_slice(dcw_all, (0, 0, j_chip * (WB // NCHIP)), (NDEV, HALO, WB // NCHIP))
    small = {n: (wts[n], mom[n], var[n]) for n in list(VEC_ORDER) + list(PAIR_ORDER) + ["b_spatial", "w_spatial", "conv_w", "b_ada"]}
    small_out, loss = _small_adam(pack_all, late_all, dcw_mine, dada_all, small)
    out.update(small_out)
    dada_sh = lax.dynamic_slice(dada_all, (0, j_chip * ADA_SH), (NDEV * nb, ADA_SH))
    out["w_ada"] = tuple(r[None] for r in _ada_bwd_adam(c_all, dada_sh, w_ada[0], m_w_ada[0], v_w_ada[0]))
    srcs, lands, ssem, rsem = scatter["f1"]
    recv = _split_wait("gs_wait_f1", srcs, lands, ssem, rsem, _scatter_plan(len(srcs)), out["w_ada"][3])
    names = ("w_f1_in", "w_f1_out")
    mine = [_sum4("sum4_" + n, p[0], recv[k], j_arr)
            for k, (n, p) in enumerate(zip(names, (pair_i, pair_o)))]
    theirs = _sibling_swap("swap_f1", mine)
    for k, n in enumerate(names):
        out[n] = tuple(r[None] for r in _adam_halves("adam_" + n, wts[n][0], mom[n][0], var[n][0], mine[k], theirs[k],
                                                     c_arr))

    res = [loss, grad_x]
    for k in range(4):
        res += [out[n][k] for n in WEIGHTS]
    return tuple(res)
```

```python
import functools

import jax
import jax.numpy as jnp
from jax import lax
from jax.experimental import pallas as pl
from jax.experimental.pallas import tpu as pltpu

D = 1024
DFF = 2816
WA = 512
WB = 512
NH = 8
HD = 64
CH = 128
CK = 31
HALO = 32
NMOD = 9
EPS = 1e-6
NCHIP = 4
NDEV = 8
FBLK = DFF // 2
ADA_SH = NMOD * D // NCHIP

LR, B1, B2, EPS_A, WD, STEP = 0.001, 0.9, 0.999, 1e-08, 0.01, 10

F32 = jnp.float32
BF16 = jnp.bfloat16
MESH = pl.DeviceIdType.MESH
ANY = pl.BlockSpec(memory_space=pl.ANY)
VMEM_FULL = pl.BlockSpec(memory_space=pltpu.VMEM)
VMEM_LIMIT = 56 * 1024 * 1024

NT = (((1,), (1,)), ((), ()))
TN = (((0,), (0,)), ((), ()))


def _dot(a, b):
    return jnp.dot(a, b, preferred_element_type=F32)


def _dot_nt(a, b):
    return lax.dot_general(a, b, NT, preferred_element_type=F32)


def _dot_tn(a, b):
    return lax.dot_general(a, b, TN, preferred_element_type=F32)


def _cparams():
    return pltpu.CompilerParams(vmem_limit_bytes=VMEM_LIMIT)


def _allgather8(name, arrs):
    n = len(arrs)

    def body(*refs):
        ins, outs = refs[:n], refs[n:2 * n]
        send_sems, recv_sems, local_sems = refs[2 * n:]
        x, y, c = lax.axis_index("x"), lax.axis_index("y"), lax.axis_index("c")
        me, sibling = (x, y, c), (x, y, 1 - c)
        chips = [(1 - x, y), (x, 1 - y), (1 - x, 1 - y)]

        def copy(a, k, block, to, src=None):
            rows = outs[a].at[4 * block[0] + 2 * block[1] + block[2]]
            return pltpu.make_async_remote_copy(
                src_ref=rows if src is None else src, dst_ref=rows,
                send_sem=send_sems.at[a, k], recv_sem=recv_sems.at[a, k],
                device_id=to, device_id_type=MESH)

        started, mine = [], []
        for a in range(n):
            loc = pltpu.make_async_copy(ins[a], outs[a].at[4 * x + 2 * y + c], local_sems.at[a])
            loc.start()
            mine.append(loc)
            first = [copy(a, 0, me, sibling, src=ins[a])]
            first += [copy(a, 1 + j, me, (*chip, c), src=ins[a]) for j, chip in enumerate(chips)]
            for cp in first:
                cp.start()
            started += first
        for a in range(n):
            for j, chip in enumerate(chips):
                copy(a, 1 + j, (*chip, c), me).wait_recv()
                fwd = copy(a, 4 + j, (*chip, c), sibling)
                fwd.start()
                started.append(fwd)
        for a in range(n):
            copy(a, 0, sibling, me).wait_recv()
            for j, chip in enumerate(chips):
                copy(a, 4 + j, (*chip, 1 - c), me).wait_recv()
        for cp in started:
            cp.wait_send()
        for loc in mine:
            loc.wait()

    return pl.pallas_call(
        body, name=name,
        out_shape=[jax.ShapeDtypeStruct((NDEV,) + a.shape, a.dtype) for a in arrs],
        in_specs=[ANY] * n, out_specs=[ANY] * n,
        scratch_shapes=[pltpu.SemaphoreType.DMA((n, 7)), pltpu.SemaphoreType.DMA((n, 7)),
                        pltpu.SemaphoreType.DMA((n,))],
    )(*arrs)


def _chip_relations(x, y):
    return [(1 - x, y), (x, 1 - y), (1 - x, 1 - y)]


def _exchange(name, arrs, out_shapes, plan):
    n = len(arrs)
    n_out = len(out_shapes)

    def body(*refs):
        ins, outs = refs[:n], refs[n:n + n_out]
        send_sems, recv_sems, local_sems = refs[n + n_out:]
        x, y, c = lax.axis_index("x"), lax.axis_index("y"), lax.axis_index("c")
        local, sends = plan(x, y, c, ins, outs)
        locs = [pltpu.make_async_copy(s, d, local_sems.at[i]) for i, (s, d) in enumerate(local)]
        for loc in locs:
            loc.start()
        cps = [pltpu.make_async_remote_copy(src_ref=s, dst_ref=d, send_sem=send_sems.at[i], recv_sem=recv_sems.at[i],
                                            device_id=peer, device_id_type=MESH)
               for i, (s, d, peer, _) in enumerate(sends)]
        for cp in cps:
            cp.start()
        for i, (s, _, peer, landing) in enumerate(sends):
            pltpu.make_async_remote_copy(src_ref=s, dst_ref=landing, send_sem=send_sems.at[i], recv_sem=recv_sems.at[i],
                                         device_id=peer, device_id_type=MESH).wait_recv()
        for cp in cps:
            cp.wait_send()
        for loc in locs:
            loc.wait()

    return n, n_out, body


def _run_exchange(name, arrs, out_shapes, plan, n_local, n_send):
    n, n_out, body = _exchange(name, arrs, out_shapes, plan)
    return pl.pallas_call(
        body, name=name, out_shape=out_shapes,
        in_specs=[ANY] * n, out_specs=[ANY] * n_out,
        scratch_shapes=[pltpu.SemaphoreType.DMA((n_send,)), pltpu.SemaphoreType.DMA((n_send,)),
                        pltpu.SemaphoreType.DMA((max(n_local, 1),))],
    )(*arrs)


def _chip_allgather(name, arrs):
    n = len(arrs)

    def plan(x, y, c, ins, outs):
        j_me = 2 * x + y
        local = [(ins[a], outs[a].at[j_me]) for a in range(n)]
        sends = []
        for a in range(n):
            for (px, py) in _chip_relations(x, y):
                sends.append((ins[a], outs[a].at[j_me], (px, py, c), outs[a].at[2 * px + py]))
        return local, sends

    shapes = [jax.ShapeDtypeStruct((NCHIP,) + a.shape, a.dtype) for a in arrs]
    return _run_exchange(name, arrs, shapes, plan, n, 3 * n)


def _chip_scatter(name, arrs):
    n = len(arrs)

    def plan(x, y, c, ins, outs):
        sends = []
        for a in range(n):
            for k, (px, py) in enumerate(_chip_relations(x, y)):
                sends.append((ins[a].at[2 * px + py], outs[a].at[k], (px, py, c), outs[a].at[k]))
        return [], sends

    shapes = [jax.ShapeDtypeStruct((3,) + a.shape[1:], a.dtype) for a in arrs]
    return _run_exchange(name, arrs, shapes, plan, 0, 3 * n)


def _sibling_swap(name, arrs, behind=None):
    n = len(arrs)

    def plan(x, y, c, ins, outs):
        return [], [(ins[a], outs[a], (x, y, 1 - c), outs[a]) for a in range(n)]

    shapes = [jax.ShapeDtypeStruct(a.shape, a.dtype) for a in arrs]
    return _run_exchange(name, list(arrs) + ([] if behind is None else [behind]), shapes, plan, 0, n)


HBM = pl.BlockSpec(memory_space=pltpu.HBM)
SEM = pl.BlockSpec(memory_space=pltpu.SEMAPHORE)
EFFECT = pltpu.SideEffectType.DATAFLOW_SIDE_EFFECTING


def _split_start(name, srcs, lands, plan, n_send, after):
    n, nl = len(srcs), len(lands)

    def body(*refs):
        src, land = refs[:n], refs[n:n + nl]
        send_sems, recv_sems = refs[n + nl + 1], refs[n + nl + 2]
        token = refs[-2]
        local_sems = refs[-1]
        x, y, c = lax.axis_index("x"), lax.axis_index("y"), lax.axis_index("c")
        local, sends = plan(x, y, c, src, land)
        locs = [pltpu.make_async_copy(s, d, local_sems.at[i]) for i, (s, d) in enumerate(local)]
        for loc in locs:
            loc.start()
        for loc in locs:
            loc.wait()
        for i, (s, d, peer, _) in enumerate(sends):
            pltpu.make_async_remote_copy(src_ref=s, dst_ref=d, send_sem=send_sems.at[i], recv_sem=recv_sems.at[i],
                                         device_id=peer, device_id_type=MESH).start()
        token[...] = jnp.zeros_like(token)

    thru = [pltpu.HBM(a.shape, a.dtype) for a in list(srcs) + list(lands)]
    res = pl.pallas_call(
        body, name=name,
        out_shape=(pltpu.SemaphoreType.DMA((n_send,)), pltpu.SemaphoreType.DMA((n_send,)), *thru,
                   jax.ShapeDtypeStruct((8, 128), F32)),
        in_specs=[HBM] * (n + nl) + [ANY],
        out_specs=(SEM, SEM, *([HBM] * (n + nl)), pl.BlockSpec(memory_space=pltpu.VMEM)),
        input_output_aliases={i: 2 + i for i in range(n + nl)},
        scratch_shapes=[pltpu.SemaphoreType.DMA((max(len(srcs), 1),))],
        compiler_params=pltpu.CompilerParams(has_side_effects=EFFECT),
    )(*[pltpu.with_memory_space_constraint(a, pltpu.HBM) for a in list(srcs) + list(lands)], after)
    return res[0], res[1], list(res[2:2 + n]), list(res[2 + n:2 + n + nl]), res[-1]


def _split_wait(name, srcs, lands, send_sems, recv_sems, plan, after):
    n, nl = len(srcs), len(lands)

    def body(*refs):
        src, land = refs[:n], refs[n:n + nl]
        send_sems, recv_sems = refs[n + nl], refs[n + nl + 1]
        x, y, c = lax.axis_index("x"), lax.axis_index("y"), lax.axis_index("c")
        _, sends = plan(x, y, c, src, land)
        for i, (s, _, peer, landing) in enumerate(sends):
            cp = pltpu.make_async_remote_copy(src_ref=s, dst_ref=landing, send_sem=send_sems.at[i],
                                              recv_sem=recv_sems.at[i], device_id=peer, device_id_type=MESH)
            cp.wait_send()
            cp.wait_recv()

    thru = [pltpu.HBM(a.shape, a.dtype) for a in list(srcs) + list(lands)]
    res = pl.pallas_call(
        body, name=name, out_shape=tuple(thru),
        in_specs=[HBM] * (n + nl) + [SEM, SEM, ANY], out_specs=tuple([HBM] * (n + nl)),
        input_output_aliases={i: i for i in range(n + nl)},
        compiler_params=pltpu.CompilerParams(has_side_effects=EFFECT),
    )(*srcs, *lands, send_sems, recv_sems, after)
    return list(res[n:])


def _split_forward(name, srcs, lands, send_a, recv_a, plan_a, plan_b, n_b, after):
    n, nl = len(srcs), len(lands)

    def body(*refs):
        src, land = refs[:n], refs[n:n + nl]
        send_a, recv_a = refs[n + nl], refs[n + nl + 1]
        send_b, recv_b = refs[n + nl + 3], refs[n + nl + 4]
        token = refs[-1]
        x, y, c = lax.axis_index("x"), lax.axis_index("y"), lax.axis_index("c")
        _, first = plan_a(x, y, c, src, land)
        for i, (s, _, peer, landing) in enumerate(first):
            cp = pltpu.make_async_remote_copy(src_ref=s, dst_ref=landing, send_sem=send_a.at[i],
                                              recv_sem=recv_a.at[i], device_id=peer, device_id_type=MESH)
            cp.wait_send()
            cp.wait_recv()
        _, second = plan_b(x, y, c, src, land)
        for i, (s, d, peer, _) in enumerate(second):
            pltpu.make_async_remote_copy(src_ref=s, dst_ref=d, send_sem=send_b.at[i], recv_sem=recv_b.at[i],
                                         device_id=peer, device_id_type=MESH).start()
        token[...] = jnp.zeros_like(token)

    thru = [pltpu.HBM(a.shape, a.dtype) for a in lands]
    res = pl.pallas_call(
        body, name=name,
        out_shape=(pltpu.SemaphoreType.DMA((n_b,)), pltpu.SemaphoreType.DMA((n_b,)), *thru,
                   jax.ShapeDtypeStruct((8, 128), F32)),
        in_specs=[HBM] * (n + nl) + [SEM, SEM, ANY],
        out_specs=(SEM, SEM, *([HBM] * nl), pl.BlockSpec(memory_space=pltpu.VMEM)),
        input_output_aliases={n + i: 2 + i for i in range(nl)},
        compiler_params=pltpu.CompilerParams(has_side_effects=EFFECT),
    )(*srcs, *lands, send_a, recv_a, after)
    return res[0], res[1], list(res[2:2 + nl]), res[-1]


def _gather_plans(shapes):
    n = len(shapes)

    def halves(a, c):
        rows = shapes[a][0] // 2
        return pl.ds(pl.multiple_of(c * rows, 16), rows), pl.ds(pl.multiple_of((1 - c) * rows, 16), rows)

    def split(a):
        return shapes[a][0] % 32 == 0

    def plan_a(x, y, c, src, land):
        j_me = 2 * x + y
        sends = []
        for a in range(n):
            for (px, py) in _chip_relations(x, y):
                if split(a):
                    mine, _ = halves(a, c)
                    sends.append((src[a].at[mine], land[a].at[j_me, mine], (px, py, c), land[a].at[2 * px + py, mine]))
                else:
                    sends.append((src[a], land[a].at[j_me], (px, py, c), land[a].at[2 * px + py]))
        return [], sends

    def plan_b(x, y, c, src, land):
        sends = []
        for a in range(n):
            if split(a):
                mine, other = halves(a, c)
                for (px, py) in _chip_relations(x, y):
                    j = 2 * px + py
                    sends.append((land[a].at[j, mine], land[a].at[j, mine], (x, y, 1 - c), land[a].at[j, other]))
        return [], sends

    n_b = 3 * sum(1 for a in range(n) if split(a))
    return plan_a, plan_b, n_b


def _allgather_plan(n):
    flips = [(dx, dy, dc) for dx in (0, 1) for dy in (0, 1) for dc in (0, 1) if dx or dy or dc]

    def plan(x, y, c, src, land):
        sends = []
        for a in range(n):
            for dx, dy, dc in flips:
                px, py, pc = x ^ dx, y ^ dy, c ^ dc
                sends.append((src[a], land[a].at[4 * x + 2 * y + c], (px, py, pc), land[a].at[4 * px + 2 * py + pc]))
        return [], sends

    return plan


def _scatter_plan(n):
    def plan(x, y, c, src, land):
        sends = []
        for a in range(n):
            for k, (px, py) in enumerate(_chip_relations(x, y)):
                sends.append((src[a].at[2 * px + py], land[a].at[k], (px, py, c), land[a].at[k]))
        return [], sends

    return plan


def _rms(x):
    r = lax.rsqrt(jnp.mean(x * x, axis=-1, keepdims=True) + EPS)
    return x * r, r


def _rms_bwd(dy, n, r, g):
    dg = jnp.sum(dy * n, axis=0, keepdims=True)
    dn = dy * g
    dx = r * (dn - n * jnp.mean(dn * n, axis=-1, keepdims=True))
    return dx, dg


def _ln(x):
    mu = jnp.mean(x, axis=-1, keepdims=True)
    xc = x - mu
    rstd = lax.rsqrt(jnp.mean(xc * xc, axis=-1, keepdims=True) + EPS)
    return xc * rstd, rstd


def _ln_bwd(dy, xhat, rstd, g):
    dg = jnp.sum(dy * xhat, axis=0, keepdims=True)
    db = jnp.sum(dy, axis=0, keepdims=True)
    dxh = dy * g
    dx = rstd * (dxh - jnp.mean(dxh, axis=-1, keepdims=True) - xhat * jnp.mean(dxh * xhat, axis=-1, keepdims=True))
    return dx, dg, db


def _sigmoid(x):
    return jax.nn.sigmoid(x)


def _dsilu(x, s):
    return s * (1.0 + x * (1.0 - s))


def _adam(w, g, m, v):
    m = B1 * m + (1.0 - B1) * g
    v = B2 * v + (1.0 - B2) * (g * g)
    m_hat = m / (1.0 - B1 ** STEP)
    v_hat = v / (1.0 - B2 ** STEP)
    delta = -LR * (m_hat / (jnp.sqrt(v_hat) + EPS_A) + WD * w)
    return delta, m, v


def _head_mask(shape):
    lane = lax.broadcasted_iota(jnp.int32, shape, len(shape) - 1)
    return [(lane >= h * HD) & (lane < (h + 1) * HD) for h in range(NH)]


def _first(b, i):
    return jnp.logical_and(b == 0, i == 0)


def _acc(ref, val, first):
    @pl.when(first)
    def _():
        ref[...] = val

    @pl.when(jnp.logical_not(first))
    def _():
        ref[...] += val


def _ada_fwd(c_all, w_sh, b_sh):
    nb = c_all.shape[0]
    tn = 768

    def body(c_ref, w_ref, b_ref, o_ref):
        cv = c_ref[...]
        cs = (cv * _sigmoid(cv)).astype(BF16)
        o_ref[...] = _dot(cs, w_ref[...].astype(BF16)) + b_ref[...]

    return pl.pallas_call(
        body, name="ada_fwd", grid=(ADA_SH // tn,),
        out_shape=jax.ShapeDtypeStruct((nb, ADA_SH), F32),
        in_specs=[pl.BlockSpec((nb, D), lambda j: (0, 0)), pl.BlockSpec((D, tn), lambda j: (0, j)),
                  pl.BlockSpec((1, tn), lambda j: (0, j))],
        out_specs=pl.BlockSpec((nb, tn), lambda j: (0, j)),
        compiler_params=_cparams(),
    )(c_all, w_sh, b_sh)


def _ada_bwd_adam(c_all, dada_sh, w, m, v):
    nb = c_all.shape[0]
    tn = 768

    def body(c_ref, d_ref, w_ref, m_ref, v_ref, g_out, d_out, m_out, v_out):
        cv = c_ref[...]
        cs = (cv * _sigmoid(cv)).astype(BF16)
        g = _dot_tn(cs, d_ref[...].astype(BF16))
        delta, m2, v2 = _adam(w_ref[...], g, m_ref[...], v_ref[...])
        g_out[...] = g
        d_out[...] = delta
        m_out[...] = m2
        v_out[...] = v2

    big = pl.BlockSpec((D, tn), lambda j: (0, j))
    shape = jax.ShapeDtypeStruct((D, ADA_SH), F32)
    return pl.pallas_call(
        body, name="ada_bwd_adam", grid=(ADA_SH // tn,),
        out_shape=[shape] * 4,
        in_specs=[pl.BlockSpec((nb, D), lambda j: (0, 0)), pl.BlockSpec((nb, tn), lambda j: (0, j)), big, big, big],
        out_specs=[big] * 4,
        compiler_params=_cparams(),
    )(c_all, dada_sh, w, m, v)


def _tok_specs(tm, width):
    return pl.BlockSpec((1, tm, width), lambda b, i: (b, i, 0))


def _mod_spec():
    return pl.BlockSpec((1, 1, D), lambda b, i: (b, 0, 0))


def _row_spec(width=D):
    return pl.BlockSpec((1, width), lambda b, i: (0, 0))


def _ffn_fwd(x, sh, sc, gt, g_pre, g_post, w_in4, w_out, target=None):
    nb, s, _ = x.shape
    tm = min(256, s)
    with_loss = target is not None

    def body(*refs):
        if with_loss:
            (x_ref, sh_ref, sc_ref, gt_ref, gpre_ref, gpost_ref, win_ref, wout_ref, tgt_ref,
             xo_ref, f_ref, p_ref, ls_ref) = refs
        else:
            (x_ref, sh_ref, sc_ref, gt_ref, gpre_ref, gpost_ref, win_ref, wout_ref,
             xo_ref, f_ref, p_ref) = refs
        xv = x_ref[0]
        n, _ = _rms(xv)
        h = (n * gpre_ref[...]) * (1.0 + sc_ref[0]) + sh_ref[0]
        hb = h.astype(BF16)
        acc = jnp.zeros((tm, D), F32)
        for j in range(2):
            gate = _dot(hb, win_ref[j])
            up = _dot(hb, win_ref[2 + j])
            p_ref[0, :, j * FBLK:(j + 1) * FBLK] = gate.astype(BF16)
            p_ref[0, :, DFF + j * FBLK:DFF + (j + 1) * FBLK] = up.astype(BF16)
            a = (gate * _sigmoid(gate)) * up
            acc = acc + _dot(a.astype(BF16), wout_ref[j * FBLK:(j + 1) * FBLK, :])
        f_ref[0] = acc
        nf, _ = _rms(acc)
        out = xv + (0.5 * gt_ref[0]) * (nf * gpost_ref[...])
        if with_loss:
            err = out - tgt_ref[0]
            xo_ref[0] = err * (1.0 / D)
            row = jnp.sum(err * err, axis=0, keepdims=True)
            part = row[:, 0:128]
            for k in range(1, D // 128):
                part = part + row[:, k * 128:(k + 1) * 128]
            _acc(ls_ref, part, _first(pl.program_id(0), pl.program_id(1)))
        else:
            xo_ref[0] = out

    in_specs = [_tok_specs(tm, D), _mod_spec(), _mod_spec(), _mod_spec(), _row_spec(), _row_spec(), VMEM_FULL, VMEM_FULL]
    args = [x, sh, sc, gt, g_pre, g_post, w_in4, w_out]
    out_shape = [jax.ShapeDtypeStruct((nb, s, D), F32), jax.ShapeDtypeStruct((nb, s, D), F32),
                 jax.ShapeDtypeStruct((nb, s, 2 * DFF), BF16)]
    out_specs = [_tok_specs(tm, D), _tok_specs(tm, D), _tok_specs(tm, 2 * DFF)]
    if with_loss:
        in_specs.append(_tok_specs(tm, D))
        args.append(target)
        out_shape.append(jax.ShapeDtypeStruct((1, 128), F32))
        out_specs.append(pl.BlockSpec((1, 128), lambda b, i: (0, 0)))
    return pl.pallas_call(
        body, name="ffn_loss_fwd" if with_loss else "ffn_fwd", grid=(nb, s // tm),
        out_shape=out_shape, in_specs=in_specs, out_specs=out_specs,
        compiler_params=_cparams(),
    )(*args)


def _ffn_bwd(dxo, x, f, p, sh, sc, gt, g_pre, g_post, w_in4, w_out):
    nb, s, _ = x.shape
    tm = min(256, s)

    def body(dxo_ref, x_ref, f_ref, p_ref, sh_ref, sc_ref, gt_ref, gpre_ref, gpost_ref, win_ref, wout_ref,
             dx_ref, dp_ref, h_ref, a_ref, df_ref, dgpre_ref, dgpost_ref, dsh_ref, dsc_ref, dgt_ref):
        b, i = pl.program_id(0), pl.program_id(1)
        dxo_v = dxo_ref[0]
        nf, q = _rms(f_ref[0])
        gpost = gpost_ref[...]
        dgt = jnp.sum(dxo_v * (0.5 * (nf * gpost)), axis=0, keepdims=True)
        do = dxo_v * (0.5 * gt_ref[0])
        df, dgpost = _rms_bwd(do, nf, q, gpost)
        dfb = df.astype(BF16)
        df_ref[0] = dfb
        xv = x_ref[0]
        n, r = _rms(xv)
        gpre = gpre_ref[...]
        ng = n * gpre
        scale1 = 1.0 + sc_ref[0]
        h = ng * scale1 + sh_ref[0]
        h_ref[0] = h.astype(BF16)
        dh = jnp.zeros((tm, D), F32)
        for j in range(2):
            gate = p_ref[0, :, j * FBLK:(j + 1) * FBLK].astype(F32)
            up = p_ref[0, :, DFF + j * FBLK:DFF + (j + 1) * FBLK].astype(F32)
            sg = _sigmoid(gate)
            act = gate * sg
            a_ref[0, :, j * FBLK:(j + 1) * FBLK] = (act * up).astype(BF16)
            da = _dot_nt(dfb, wout_ref[j * FBLK:(j + 1) * FBLK, :])
            dgate = (da * up * _dsilu(gate, sg)).astype(BF16)
            dup = (da * act).astype(BF16)
            dp_ref[0, :, j * FBLK:(j + 1) * FBLK] = dgate
            dp_ref[0, :, DFF + j * FBLK:DFF + (j + 1) * FBLK] = dup
            dh = dh + _dot_nt(dgate, win_ref[j]) + _dot_nt(dup, win_ref[2 + j])
        dsh = jnp.sum(dh, axis=0, keepdims=True)
        dsc = jnp.sum(dh * ng, axis=0, keepdims=True)
        dxn, dgpre = _rms_bwd(dh * scale1, n, r, gpre)
        dx_ref[0] = dxo_v + dxn
        _acc(dgpre_ref, dgpre, _first(b, i))
        _acc(dgpost_ref, dgpost, _first(b, i))
        _acc(dsh_ref, dsh[None], i == 0)
        _acc(dsc_ref, dsc[None], i == 0)
        _acc(dgt_ref, dgt[None], i == 0)

    tok = _tok_specs(tm, D)
    mod_shape = jax.ShapeDtypeStruct((nb, 1, D), F32)
    row_shape = jax.ShapeDtypeStruct((1, D), F32)
    return pl.pallas_call(
        body, name="ffn_bwd", grid=(nb, s // tm),
        out_shape=[jax.ShapeDtypeStruct((nb, s, D), F32), jax.ShapeDtypeStruct((nb, s, 2 * DFF), BF16),
                   jax.ShapeDtypeStruct((nb, s, D), BF16), jax.ShapeDtypeStruct((nb, s, DFF), BF16),
                   jax.ShapeDtypeStruct((nb, s, D), BF16), row_shape, row_shape, mod_shape, mod_shape, mod_shape],
        in_specs=[tok, tok, tok, _tok_specs(tm, 2 * DFF), _mod_spec(), _mod_spec(), _mod_spec(), _row_spec(), _row_spec(),
                  VMEM_FULL, VMEM_FULL],
        out_specs=[tok, _tok_specs(tm, 2 * DFF), tok, _tok_specs(tm, DFF), tok, _row_spec(), _row_spec(),
                   _mod_spec(), _mod_spec(), _mod_spec()],
        compiler_params=_cparams(),
    )(dxo, x, f, p, sh, sc, gt, g_pre, g_post, w_in4, w_out)


def _tile_rows(rows, n_steps, multiple):
    best = 1
    for d in range(1, n_steps + 1):
        if rows % d == 0 and (rows // d) % multiple == 0:
            best = d
    return best


def _row_side(ins, lead_blocks, outs, multiple, compute):
    rows, cols = ins[0].shape[-2:]

    def build(n_steps, step):
        d = _tile_rows(rows, n_steps, multiple)
        tr = rows // d

        def tile(j, k):
            return jnp.minimum(step(j, k), d - 1)

        in_specs = []
        for lead in lead_blocks:
            if lead is None:
                in_specs.append(pl.BlockSpec((tr, cols), lambda j, k, p: (tile(j, k), 0)))
            elif lead[1] is None:
                in_specs.append(pl.BlockSpec((lead[0], tr, cols), lambda j, k, p: (0, tile(j, k), 0)))
            else:
                in_specs.append(pl.BlockSpec((1, tr, cols), functools.partial(
                    lambda j, k, p, word: (p[word], tile(j, k), 0), word=lead[1])))
        out_specs = [pl.BlockSpec((tr, cols), lambda j, k, p: (tile(j, k), 0)) for _ in outs]
        return d, in_specs, out_specs

    return dict(ins=list(ins), out_shape=[jax.ShapeDtypeStruct((rows, cols), F32) for _ in outs], build=build,
                compute=compute)


def _side_sum4(own4, recv):
    def compute(ins, outs):
        acc = ins[0][0]
        for q in range(3):
            acc = acc + ins[1][q].astype(F32)
        outs[0][...] = acc

    return _row_side([own4, recv], [(1, 0), (3, None)], [None], 16, compute)


def _side_adam(w, m, v, ga, gb):
    def compute(ins, outs):
        g = ins[3][...] + ins[4][...]
        delta, m2, v2 = _adam(ins[0][...], g, ins[1][...], ins[2][...])
        for ref, val in zip(outs, (g, delta, m2, v2)):
            ref[...] = val

    return _row_side([w, m, v, ga, gb], [None] * 5, [None] * 4, 8, compute)


def _wgrad(name, a, b, col_block, chip_major, pref, sides=(), tk=None):
    t, ka = a.shape
    n = b.shape[1]
    if tk is None:
        tk = min(t, 512)
        while tk * 2 <= t and t % (tk * 2) == 0 and 2 * (tk * 2) * max(ka, col_block) <= 6 * 1024 * 1024:
            tk *= 2
    tk = min(tk, t)
    nk = t // tk
    nblk = n // col_block

    def step(j, k):
        return j * nk + k

    built = [sd["build"](nblk * nk, step) for sd in sides]
    n_side_in = [len(sd["ins"]) for sd in sides]
    n_side_out = [len(sd["out_shape"]) for sd in sides]

    def body(pref_ref, a_ref, b_ref, *rest):
        del pref_ref
        side_in = rest[:sum(n_side_in)]
        o_ref, obf_ref = rest[sum(n_side_in):sum(n_side_in) + 2]
        side_out = rest[sum(n_side_in) + 2:-1]
        acc_ref = rest[-1]
        k = pl.program_id(1)

        @pl.when(k == 0)
        def _():
            acc_ref[...] = jnp.zeros_like(acc_ref)

        acc_ref[...] += _dot_tn(a_ref[...], b_ref[...])

        @pl.when(k == nk - 1)
        def _():
            val = acc_ref[...]
            if chip_major:
                o_ref[0] = val
                obf_ref[0] = val.astype(BF16)
            else:
                o_ref[...] = val
                obf_ref[...] = val.astype(BF16)

        at_in = at_out = 0
        for sd, (d, _, _), ni, no in zip(sides, built, n_side_in, n_side_out):
            pl.when(step(pl.program_id(0), k) < d)(
                functools.partial(sd["compute"], side_in[at_in:at_in + ni], side_out[at_out:at_out + no]))
            at_in += ni
            at_out += no

    if chip_major:
        shape = (nblk, ka, col_block)
        ospec = pl.BlockSpec((1, ka, col_block), lambda j, k, p: (j, 0, 0))
    else:
        shape = (ka, n)
        ospec = pl.BlockSpec((ka, col_block), lambda j, k, p: (0, j))
    in_specs = [pl.BlockSpec((tk, ka), lambda j, k, p: (k, 0)), pl.BlockSpec((tk, col_block), lambda j, k, p: (k, j))]
    out_specs = [ospec, ospec]
    out_shape = [jax.ShapeDtypeStruct(shape, F32), jax.ShapeDtypeStruct(shape, BF16)]
    args = [a, b]
    for sd, (_, side_in_specs, side_out_specs) in zip(sides, built):
        in_specs += side_in_specs
        args += sd["ins"]
        out_specs += side_out_specs
        out_shape += sd["out_shape"]
    return pl.pallas_call(
        body, name=name,
        grid_spec=pltpu.PrefetchScalarGridSpec(
            num_scalar_prefetch=1, grid=(nblk, nk), in_specs=in_specs, out_specs=out_specs,
            scratch_shapes=[pltpu.VMEM((ka, col_block), F32)]),
        out_shape=out_shape,
        compiler_params=_cparams(),
    )(pref, *args)


def _mix_in_fwd(x, sh, sc, g_pre, w_mi4):
    nb, s, _ = x.shape
    tm = min(512, s)

    def body(x_ref, sh_ref, sc_ref, gpre_ref, w_ref, u_ref, v_ref, a_ref, g_ref):
        n, _ = _rms(x_ref[0])
        hb = ((n * gpre_ref[...]) * (1.0 + sc_ref[0]) + sh_ref[0]).astype(BF16)
        for k, o_ref in enumerate((u_ref, v_ref, a_ref, g_ref)):
            o_ref[0] = _dot(hb, w_ref[k])

    shape = jax.ShapeDtypeStruct((nb, s, WA), F32)
    return pl.pallas_call(
        body, name="mix_in_fwd", grid=(nb, s // tm),
        out_shape=[shape] * 4,
        in_specs=[_tok_specs(tm, D), _mod_spec(), _mod_spec(), _row_spec(), VMEM_FULL],
        out_specs=[_tok_specs(tm, WA)] * 4,
        compiler_params=_cparams(),
    )(x, sh, sc, g_pre, w_mi4)


def _spatial_weights(wcat_ref, transposed):
    w = wcat_ref[...]
    row = lax.broadcasted_iota(jnp.int32, w.shape, 0)
    col = lax.broadcasted_iota(jnp.int32, w.shape, 1)
    keep = ((row & (CH - 1)) <= col) if transposed else ((col & (CH - 1)) <= row)
    return jnp.where(keep, w, 0.0).astype(BF16)


def _expand_heads(vc, masks):
    return jnp.concatenate([jnp.where(mk, vc, jnp.zeros_like(vc)) for mk in masks], axis=0)


def _spatial_bias(bspt_ref):
    return bspt_ref[...]


SHIFTS = 8
TAP_ROWS = 32


def _ext_rows(tm):
    return tm + HALO + SHIFTS


def _make_shifts(ext_ref, sh_ref, tm):
    ext_ref[tm + HALO:tm + HALO + SHIFTS, :] = jnp.zeros((SHIFTS, WB), F32)
    for r in range(SHIFTS):
        sh_ref[r] = ext_ref[r:r + tm + HALO, :]


def _conv_taps(sh_ref, w_ref, tm, taps, emit):
    def block(i, carry):
        r0 = pl.multiple_of(i * TAP_ROWS, TAP_ROWS)
        acc = jnp.zeros((TAP_ROWS, WB), F32)
        for o, k in taps:
            acc = acc + w_ref[k:k + 1, :] * sh_ref[o % SHIFTS, pl.ds(r0 + SHIFTS * (o // SHIFTS), TAP_ROWS), :]
        emit(r0, acc)
        return carry

    lax.fori_loop(0, tm // TAP_ROWS, block, 0)


def _halo_prev_spec(tm):
    return pl.BlockSpec((1, HALO, WB), lambda b, i: (b, jnp.maximum(i * (tm // HALO) - 1, 0), 0))


def _halo_next_spec(tm, s):
    return pl.BlockSpec((1, HALO, WB), lambda b, i: (b, jnp.minimum((i + 1) * (tm // HALO), s // HALO - 1), 0))


def _mix_mid_fwd(x, u, v, a, g, gt, gn_g, gn_b, wcat, bspt, conv_w, conv_b, cn_g, cn_b, go_a, go_b, w_mo, g_post):
    nb, s, _ = x.shape
    tm = min(512, s)

    def body(x_ref, u_ref, v_ref, a_ref, g_ref, ah_ref, gh_ref, gt_ref, gng_ref, gnb_ref, wcat_ref, bspt_ref,
             cw_ref, cb_ref, cng_ref, cnb_ref, goa_ref, gob_ref, wmo_ref, gpost_ref,
             xo_ref, conv_ref, y_ref, m_ref, ext_ref, sh_ref):
        i = pl.program_id(1)
        xhat, _ = _ln(v_ref[0])
        vb = (xhat * gng_ref[...] + gnb_ref[...]).astype(BF16)
        wsb = _spatial_weights(wcat_ref, False)
        bias = _spatial_bias(bspt_ref)
        masks = _head_mask((CH, WA))
        zs = []
        for cidx in range(tm // CH):
            vexp = _expand_heads(vb[cidx * CH:(cidx + 1) * CH, :], masks)
            zs.append(_dot(wsb, vexp) + bias)
        z = jnp.concatenate(zs, axis=0)
        na, _ = _rms(u_ref[0] * z)
        keep = jnp.where(i == 0, 0.0, 1.0).astype(F32)
        ext_ref[0:HALO, :] = (ah_ref[0] * _sigmoid(gh_ref[0])) * keep
        ext_ref[HALO:HALO + tm, :] = a_ref[0] * _sigmoid(g_ref[0])
        _make_shifts(ext_ref, sh_ref, tm)
        cb = cb_ref[...]

        def put_conv(r0, acc):
            conv_ref[0, pl.ds(r0, TAP_ROWS), :] = acc + cb

        _conv_taps(sh_ref, cw_ref, tm, [(k + HALO - (CK - 1), k) for k in range(CK)], put_conv)
        conv = conv_ref[0]
        chat, _ = _ln(conv)
        cln = chat * cng_ref[...] + cnb_ref[...]
        nbb, _ = _rms(cln * _sigmoid(cln))
        yb = jnp.concatenate([na * goa_ref[...], nbb * gob_ref[...]], axis=1).astype(BF16)
        y_ref[0] = yb
        m = _dot(yb, wmo_ref[...])
        m_ref[0] = m
        nm, _ = _rms(m)
        xo_ref[0] = x_ref[0] + gt_ref[0] * (nm * gpost_ref[...])

    t5 = _tok_specs(tm, WA)
    tok = _tok_specs(tm, D)
    r5 = _row_spec(WA)
    full = lambda shape: pl.BlockSpec(shape, lambda b, i: (0,) * len(shape))
    return pl.pallas_call(
        body, name="mix_mid_fwd", grid=(nb, s // tm),
        out_shape=[jax.ShapeDtypeStruct((nb, s, D), F32), jax.ShapeDtypeStruct((nb, s, WB), F32),
                   jax.ShapeDtypeStruct((nb, s, D), BF16), jax.ShapeDtypeStruct((nb, s, D), F32)],
        in_specs=[tok, t5, t5, t5, t5, _halo_prev_spec(tm), _halo_prev_spec(tm), _mod_spec(), r5, r5,
                  full((CH, NH * CH)), full((CH, WA)), full((HALO, WB)), r5, r5, r5, r5, r5, VMEM_FULL, _row_spec()],
        out_specs=[tok, t5, tok, tok],
        scratch_shapes=[pltpu.VMEM((_ext_rows(tm), WB), F32), pltpu.VMEM((SHIFTS, tm + HALO, WB), F32)],
        compiler_params=_cparams(),
    )(x, u, v, a, g, a, g, gt, gn_g, gn_b, wcat, bspt, conv_w, conv_b, cn_g, cn_b, go_a, go_b, w_mo, g_post)


def _mix_out_bwd(dxo, m, gt, g_post, w_mo):
    nb, s, _ = m.shape
    tm = min(512, s)

    def body(dxo_ref, m_ref, gt_ref, gpost_ref, wmo_ref, dy_ref, dm_ref, dgpost_ref, dgt_ref):
        b, i = pl.program_id(0), pl.program_id(1)
        dxo_v = dxo_ref[0]
        nm, q = _rms(m_ref[0])
        gpost = gpost_ref[...]
        dgt = jnp.sum(dxo_v * (nm * gpost), axis=0, keepdims=True)
        dm, dgpost = _rms_bwd(dxo_v * gt_ref[0], nm, q, gpost)
        dmb = dm.astype(BF16)
        dm_ref[0] = dmb
        dy_ref[0] = _dot_nt(dmb, wmo_ref[...])
        _acc(dgpost_ref, dgpost, _first(b, i))
        _acc(dgt_ref, dgt[None], i == 0)

    tok = _tok_specs(tm, D)
    return pl.pallas_call(
        body, name="mix_out_bwd", grid=(nb, s // tm),
        out_shape=[jax.ShapeDtypeStruct((nb, s, D), F32), jax.ShapeDtypeStruct((nb, s, D), BF16),
                   jax.ShapeDtypeStruct((1, D), F32), jax.ShapeDtypeStruct((nb, 1, D), F32)],
        in_specs=[tok, tok, _mod_spec(), _row_spec(), VMEM_FULL],
        out_specs=[tok, tok, _row_spec(), _mod_spec()],
        compiler_params=_cparams(),
    )(dxo, m, gt, g_post, w_mo)


def _mix_mid_bwd(dy, u, v, conv, gn_g, gn_b, wcat, wcat_t, bspt, cn_g, cn_b, go_a, go_b):
    nb, s, _ = dy.shape
    tm = min(512, s)
    nchunk = tm // CH

    def body(dy_ref, u_ref, v_ref, conv_ref, gng_ref, gnb_ref, wcat_ref, wcatt_ref, bspt_ref, cng_ref, cnb_ref,
             goa_ref, gob_ref,
             du_ref, dv_ref, dconv_ref, dwcat_ref, dbsp_ref, dgng_ref, dgnb_ref, dgoa_ref, dgob_ref,
             dcng_ref, dcnb_ref, dcb_ref):
        first = _first(pl.program_id(0), pl.program_id(1))
        dyv = dy_ref[0]
        xhat, rstd = _ln(v_ref[0])
        gng = gng_ref[...]
        vb = (xhat * gng + gnb_ref[...]).astype(BF16)
        wsb = _spatial_weights(wcat_ref, False)
        wsb_t = _spatial_weights(wcatt_ref, True)
        bias = _spatial_bias(bspt_ref)
        masks = _head_mask((CH, WA))
        vexps, zs = [], []
        for cidx in range(nchunk):
            vexp = _expand_heads(vb[cidx * CH:(cidx + 1) * CH, :], masks)
            vexps.append(vexp)
            zs.append(_dot(wsb, vexp) + bias)
        z = jnp.concatenate(zs, axis=0)
        uv = u_ref[0]
        na, ra = _rms(uv * z)
        dya, dgoa = _rms_bwd(dyv[:, 0:WA], na, ra, goa_ref[...])
        du_ref[0] = dya * z
        dz = dya * uv
        dwcat = jnp.zeros((CH, NH * CH), F32)
        dzsum = jnp.zeros((CH, WA), F32)
        dvlns = []
        for cidx in range(nchunk):
            dzc = dz[cidx * CH:(cidx + 1) * CH, :]
            dzsum = dzsum + dzc
            dzb = dzc.astype(BF16)
            dwcat = dwcat + _dot_nt(dzb, vexps[cidx])
            dvexp = _dot(wsb_t, dzb)
            dvl = jnp.zeros((CH, WA), F32)
            for h in range(NH):
                dvl = dvl + jnp.where(masks[h], dvexp[h * CH:(h + 1) * CH, :], 0.0)
            dvlns.append(dvl)
        dvln = jnp.concatenate(dvlns, axis=0)
        dv, dgng, dgnb = _ln_bwd(dvln, xhat, rstd, gng)
        dv_ref[0] = dv
        lane = lax.broadcasted_iota(jnp.int32, (NH, WA), 1)
        head = lax.broadcasted_iota(jnp.int32, (NH, WA), 0)
        sel = jnp.where((lane >= head * HD) & (lane < (head + 1) * HD), 1.0, 0.0).astype(F32)
        dbsp = lax.dot_general(sel, dzsum, NT, preferred_element_type=F32, precision=lax.Precision.HIGHEST)
        chat, crstd = _ln(conv_ref[0])
        cng = cng_ref[...]
        cln = chat * cng + cnb_ref[...]
        sg = _sigmoid(cln)
        nbb, rb = _rms(cln * sg)
        dyb, dgob = _rms_bwd(dyv[:, WA:D], nbb, rb, gob_ref[...])
        dconv, dcng, dcnb = _ln_bwd(dyb * _dsilu(cln, sg), chat, crstd, cng)
        dconv_ref[0] = dconv
        dcb = jnp.sum(dconv, axis=0, keepdims=True)
        for ref, val in ((dwcat_ref, dwcat), (dbsp_ref, dbsp), (dgng_ref, dgng), (dgnb_ref, dgnb), (dgoa_ref, dgoa),
                         (dgob_ref, dgob), (dcng_ref, dcng), (dcnb_ref, dcnb), (dcb_ref, dcb)):
            _acc(ref, val, first)

    t5 = _tok_specs(tm, WA)
    r5 = _row_spec(WA)
    full = lambda shape: pl.BlockSpec(shape, lambda b, i: (0,) * len(shape))
    big = jax.ShapeDtypeStruct((nb, s, WA), F32)
    row = jax.ShapeDtypeStruct((1, WA), F32)
    return pl.pallas_call(
        body, name="mix_mid_bwd", grid=(nb, s // tm),
        out_shape=[big, big, big, jax.ShapeDtypeStruct((CH, NH * CH), F32), jax.ShapeDtypeStruct((NH, CH), F32),
                   row, row, row, row, row, row, row],
        in_specs=[_tok_specs(tm, D), t5, t5, t5, r5, r5, full((CH, NH * CH)), full((NH * CH, CH)), full((CH, WA)),
                  r5, r5, r5, r5],
        out_specs=[t5, t5, t5, full((CH, NH * CH)), full((NH, CH)), r5, r5, r5, r5, r5, r5, r5],
        compiler_params=_cparams(),
    )(dy, u, v, conv, gn_g, gn_b, wcat, wcat_t, bspt, cn_g, cn_b, go_a, go_b)


def _mix_in_bwd(dxo, x, du, dv, dconv, a, g, sh, sc, g_pre, w_mi4, conv_w):
    nb, s, _ = x.shape
    tm = min(512, s)
    n_i = s // tm

    def body(dxo_ref, x_ref, du_ref, dv_ref, dc_ref, dch_ref, a_ref, g_ref, ah_ref, gh_ref, sh_ref, sc_ref,
             gpre_ref, w_ref, cw_ref,
             dx_ref, dproj_ref, h_ref, dgpre_ref, dsh_ref, dsc_ref, dcw_ref, ext_ref, shf_ref, dglu_ref):
        b, i = pl.program_id(0), pl.program_id(1)
        first = _first(b, i)
        av, gv = a_ref[0], g_ref[0]
        sg = _sigmoid(gv)
        dconv = dc_ref[0]
        ext_ref[0:tm, :] = dconv
        ext_ref[tm:tm + HALO, :] = dch_ref[0] * jnp.where(i == n_i - 1, 0.0, 1.0).astype(F32)
        _make_shifts(ext_ref, shf_ref, tm)

        def put_dglu(r0, acc):
            dglu_ref[pl.ds(r0, TAP_ROWS), :] = acc

        _conv_taps(shf_ref, cw_ref, tm, [(CK - 1 - k, k) for k in range(CK)], put_dglu)
        dglu = dglu_ref[...]
        ext_ref[0:HALO, :] = (ah_ref[0] * _sigmoid(gh_ref[0])) * jnp.where(i == 0, 0.0, 1.0).astype(F32)
        ext_ref[HALO:HALO + tm, :] = av * sg
        _make_shifts(ext_ref, shf_ref, tm)

        @pl.when(first)
        def _():
            dcw_ref[...] = jnp.zeros((HALO, WB), F32)

        for k in range(CK):
            o = k + HALO - (CK - 1)
            lo = SHIFTS * (o // SHIFTS)
            dcw_ref[k:k + 1, :] += jnp.sum(dconv * shf_ref[o % SHIFTS, lo:lo + tm, :], axis=0, keepdims=True)
        da = dglu * sg
        dg = dglu * av * (sg * (1.0 - sg))
        parts = [du_ref[0].astype(BF16), dv_ref[0].astype(BF16), da.astype(BF16), dg.astype(BF16)]
        dh = jnp.zeros((tm, D), F32)
        for k in range(4):
            dproj_ref[0, :, k * WA:(k + 1) * WA] = parts[k]
            dh = dh + _dot_nt(parts[k], w_ref[k])
        n, r = _rms(x_ref[0])
        gpre = gpre_ref[...]
        ng = n * gpre
        scale1 = 1.0 + sc_ref[0]
        h_ref[0] = (ng * scale1 + sh_ref[0]).astype(BF16)
        dsh = jnp.sum(dh, axis=0, keepdims=True)
        dsc = jnp.sum(dh * ng, axis=0, keepdims=True)
        dxn, dgpre = _rms_bwd(dh * scale1, n, r, gpre)
        dx_ref[0] = dxo_ref[0] + dxn
        _acc(dgpre_ref, dgpre, first)
        _acc(dsh_ref, dsh[None], i == 0)
        _acc(dsc_ref, dsc[None], i == 0)

    tok = _tok_specs(tm, D)
    t5 = _tok_specs(tm, WA)
    full = lambda shape: pl.BlockSpec(shape, lambda b, i: (0,) * len(shape))
    mod_shape = jax.ShapeDtypeStruct((nb, 1, D), F32)
    return pl.pallas_call(
        body, name="mix_in_bwd", grid=(nb, n_i),
        out_shape=[jax.ShapeDtypeStruct((nb, s, D), F32), jax.ShapeDtypeStruct((nb, s, 4 * WA), BF16),
                   jax.ShapeDtypeStruct((nb, s, D), BF16), jax.ShapeDtypeStruct((1, D), F32), mod_shape, mod_shape,
                   jax.ShapeDtypeStruct((HALO, WB), F32)],
        in_specs=[tok, tok, t5, t5, t5, _halo_next_spec(tm, s), t5, t5, _halo_prev_spec(tm), _halo_prev_spec(tm),
                  _mod_spec(), _mod_spec(), _row_spec(), VMEM_FULL, full((HALO, WB))],
        out_specs=[tok, _tok_specs(tm, 4 * WA), tok, _row_spec(), _mod_spec(), _mod_spec(), full((HALO, WB))],
        scratch_shapes=[pltpu.VMEM((_ext_rows(tm), WB), F32), pltpu.VMEM((SHIFTS, tm + HALO, WB), F32),
                        pltpu.VMEM((tm, WB), F32)],
        compiler_params=_cparams(),
    )(dxo, x, du, dv, dconv, dconv, a, g, a, g, sh, sc, g_pre, w_mi4, conv_w)


def _row_tile(rows, cols):
    best = 8
    for t in range(8, rows + 1, 8):
        if rows % t == 0 and t * cols * 4 <= 1536 * 1024:
            best = t
    return best


def _sum4(name, own4, recv, j_arr):
    _, rows, cols = own4.shape
    tr = _row_tile(rows, cols)

    def body(j_ref, own_ref, recv_ref, o_ref):
        del j_ref
        acc = own_ref[0]
        for k in range(3):
            acc = acc + recv_ref[k].astype(F32)
        o_ref[...] = acc

    return pl.pallas_call(
        body, name=name,
        grid_spec=pltpu.PrefetchScalarGridSpec(
            num_scalar_prefetch=1, grid=(rows // tr,),
            in_specs=[pl.BlockSpec((1, tr, cols), lambda i, j: (j[0], i, 0)),
                      pl.BlockSpec((3, tr, cols), lambda i, j: (0, i, 0))],
            out_specs=pl.BlockSpec((tr, cols), lambda i, j: (i, 0))),
        out_shape=jax.ShapeDtypeStruct((rows, cols), F32),
        compiler_params=_cparams(),
    )(j_arr, own4, recv)


def _pair_exchange(name, arrs):
    n = len(arrs)

    def plan(x, y, c, ins, outs):
        sends = []
        for a in range(n):
            rows = arrs[a].shape[1] // 2
            theirs = pl.ds(pl.multiple_of((1 - c) * rows, 16), rows)
            sends.append((ins[a].at[:, theirs], outs[a], (x, y, 1 - c), outs[a]))
        return [], sends

    shapes = [jax.ShapeDtypeStruct((a.shape[0], a.shape[1] // 2, a.shape[2]), a.dtype) for a in arrs]
    return _run_exchange(name, arrs, shapes, plan, 0, n)


def _pair_sum(name, g32, recv, c_arr):
    nblk, rows, cols = recv.shape
    tr = _row_tile(rows, cols)
    nh = rows // tr

    def body(c_ref, g_ref, r_ref, o32_ref, obf_ref):
        del c_ref
        val = g_ref[0] + r_ref[0].astype(F32)
        o32_ref[0] = val
        obf_ref[0] = val.astype(BF16)

    spec = pl.BlockSpec((1, tr, cols), lambda k, i, c: (k, i, 0))
    return pl.pallas_call(
        body, name=name,
        grid_spec=pltpu.PrefetchScalarGridSpec(
            num_scalar_prefetch=1, grid=(nblk, nh),
            in_specs=[pl.BlockSpec((1, tr, cols), lambda k, i, c: (k, c[0] * nh + i, 0)), spec],
            out_specs=[spec, spec]),
        out_shape=[jax.ShapeDtypeStruct(recv.shape, F32), jax.ShapeDtypeStruct(recv.shape, BF16)],
        compiler_params=_cparams(),
    )(c_arr, g32, recv)


def _adam_halves(name, w, m, v, mine, theirs, c_arr):
    rows, cols = w.shape
    tr = _row_tile(rows // 2, cols)
    nh = (rows // 2) // tr

    def body(c_ref, w_ref, m_ref, v_ref, mine_ref, theirs_ref, g_out, d_out, m_out, v_out):
        here = (pl.program_id(0) // nh) == c_ref[0]
        g = jnp.where(here, mine_ref[...], theirs_ref[...])
        delta, m2, v2 = _adam(w_ref[...], g, m_ref[...], v_ref[...])
        g_out[...] = g
        d_out[...] = delta
        m_out[...] = m2
        v_out[...] = v2

    spec = pl.BlockSpec((tr, cols), lambda i, c: (i, 0))
    shape = jax.ShapeDtypeStruct((rows, cols), F32)
    return pl.pallas_call(
        body, name=name,
        grid_spec=pltpu.PrefetchScalarGridSpec(
            num_scalar_prefetch=1, grid=(2 * nh,),
            in_specs=[spec, spec, spec,
                      pl.BlockSpec((tr, cols), lambda i, c: (jnp.clip(i - c[0] * nh, 0, nh - 1), 0)),
                      pl.BlockSpec((tr, cols), lambda i, c: (jnp.clip(i - (1 - c[0]) * nh, 0, nh - 1), 0))],
            out_specs=[spec] * 4),
        out_shape=[shape] * 4,
        compiler_params=_cparams(),
    )(c_arr, w, m, v, mine, theirs)


def _adam_big(name, w, m, v, ga, gb):
    rows, cols = w.shape
    tr = _row_tile(rows, cols)

    def body(w_ref, m_ref, v_ref, ga_ref, gb_ref, g_out, d_out, m_out, v_out):
        gsum = ga_ref[...] + gb_ref[...]
        delta, m2, v2 = _adam(w_ref[...], gsum, m_ref[...], v_ref[...])
        g_out[...] = gsum
        d_out[...] = delta
        m_out[...] = m2
        v_out[...] = v2

    spec = pl.BlockSpec((tr, cols), lambda i: (i, 0))
    shape = jax.ShapeDtypeStruct((rows, cols), F32)
    return pl.pallas_call(
        body, name=name, grid=(rows // tr,), out_shape=[shape] * 4,
        in_specs=[spec] * 5, out_specs=[spec] * 4, compiler_params=_cparams(),
    )(w, m, v, ga, gb)


PK_VEC = 0
PK_LOSS = 6
PK_PAIR = 8
PK_BSP = 16
PK_WCAT = 24
PK_ROWS = PK_WCAT + CH
PAIR_ORDER = ("gmlp_norm_g", "gmlp_norm_b", "conv_b", "conv_norm_g", "conv_norm_b", "g_out_a", "g_out_b")
VEC_ORDER = ("g_pre_f1", "g_post_f1", "g_pre_m", "g_post_m", "g_pre_f2", "g_post_f2")


def _pack_late(rows):
    counts = [r.shape[0] for r in rows]
    assert sum(counts) == 8

    def body(*refs):
        o_ref = refs[-1]
        at = 0
        for r, cnt in zip(refs[:-1], counts):
            o_ref[at:at + cnt, :] = r[...]
            at += cnt

    return pl.pallas_call(
        body, name="pack_late", out_shape=jax.ShapeDtypeStruct((8, D), F32),
        in_specs=[VMEM_FULL] * len(rows), out_specs=VMEM_FULL, compiler_params=_cparams(),
    )(*rows)


def _pack_small(vecs, pairs, dbsp, dwcat, lsum):
    def body(*refs):
        vec_refs = refs[:4]
        pair_refs = refs[4:11]
        dbsp_ref, dwcat_ref, lsum_ref, o_ref = refs[11:]
        o_ref[0:PK_WCAT, :] = jnp.zeros((PK_WCAT, D), F32)
        o_ref[PK_LOSS:PK_LOSS + 1, 0:128] = lsum_ref[...]
        for k, r in enumerate(vec_refs):
            o_ref[PK_VEC + 2 + k:PK_VEC + 3 + k, :] = r[...]
        for k, r in enumerate(pair_refs):
            row, half = PK_PAIR + k // 2, k % 2
            o_ref[row:row + 1, half * WA:(half + 1) * WA] = r[...]
        o_ref[PK_BSP:PK_BSP + NH, 0:CH] = dbsp_ref[...]
        o_ref[PK_WCAT:PK_ROWS, :] = dwcat_ref[...]

    args = list(vecs) + list(pairs) + [dbsp, dwcat, lsum]
    return pl.pallas_call(
        body, name="pack_small", out_shape=jax.ShapeDtypeStruct((PK_ROWS, D), F32),
        in_specs=[VMEM_FULL] * len(args), out_specs=VMEM_FULL, compiler_params=_cparams(),
    )(*args)


def _small_adam(pack_all, late_all, dcw_all, dada_all, params):
    names = list(VEC_ORDER) + list(PAIR_ORDER) + ["b_spatial", "w_spatial", "conv_w", "b_ada"]
    flat = []
    for nm in names:
        flat += list(params[nm])
    n_in = 4 + len(flat)

    def body(*refs):
        pack_ref, late_ref, dcw_ref, dada_ref = refs[:4]
        prm = refs[4:n_in]
        outs = refs[n_in:]

        def total(r0, nr, c0, nc):
            acc = pack_ref[0, r0:r0 + nr, c0:c0 + nc]
            for d in range(1, NDEV):
                acc = acc + pack_ref[d, r0:r0 + nr, c0:c0 + nc]
            return acc

        def emit(idx, g, getw, put):
            w_ref, m_ref, v_ref = prm[3 * idx:3 * idx + 3]
            delta, m2, v2 = _adam(getw(w_ref), g, getw(m_ref), getw(v_ref))
            for o_ref, val in zip(outs[4 * idx:4 * idx + 4], (g, delta, m2, v2)):
                put(o_ref, val)

        def whole(ref):
            return ref[...]

        def put_whole(ref, val):
            ref[...] = val

        idx = 0
        for k in range(6):
            if k < 2:
                g = late_ref[0, k:k + 1, :]
                for d in range(1, NDEV):
                    g = g + late_ref[d, k:k + 1, :]
            else:
                g = total(PK_VEC + k, 1, 0, D)
            emit(idx, g, whole, put_whole)
            idx += 1
        for k in range(7):
            emit(idx, total(PK_PAIR + k // 2, 1, (k % 2) * WA, WA), whole, put_whole)
            idx += 1
        emit(idx, total(PK_BSP, NH, 0, CH), lambda r: r[0], lambda r, val: r.__setitem__(0, val))
        idx += 1
        row = lax.broadcasted_iota(jnp.int32, (CH, CH), 0)
        col = lax.broadcasted_iota(jnp.int32, (CH, CH), 1)
        for h in range(NH):
            gh = jnp.where(col <= row, total(PK_WCAT, CH, h * CH, CH), 0.0)
            w_ref, m_ref, v_ref = prm[3 * idx:3 * idx + 3]
            delta, m2, v2 = _adam(w_ref[0, h], gh, m_ref[0, h], v_ref[0, h])
            for o_ref, val in zip(outs[4 * idx:4 * idx + 4], (gh, delta, m2, v2)):
                o_ref[0, h] = val
        idx += 1
        gcw = dcw_ref[0, 0:CK, :]
        for d in range(1, NDEV):
            gcw = gcw + dcw_ref[d, 0:CK, :]
        emit(idx, gcw, lambda r: r[0], lambda r, val: r.__setitem__(0, val))
        idx += 1
        emit(idx, jnp.sum(dada_ref[...], axis=0, keepdims=True), whole, put_whole)
        outs[-1][...] = jnp.sum(total(PK_LOSS, 1, 0, 128), axis=1, keepdims=True) * (0.5 / D)

    out_shape = []
    for nm in names:
        w = params[nm][0]
        out_shape += [jax.ShapeDtypeStruct(w.shape, F32)] * 4
    out_shape.append(jax.ShapeDtypeStruct((1, 1), F32))
    res = pl.pallas_call(
        body, name="small_adam", out_shape=out_shape,
        in_specs=[VMEM_FULL] * n_in, out_specs=[VMEM_FULL] * len(out_shape), compiler_params=_cparams(),
    )(pack_all, late_all, dcw_all, dada_all, *flat)
    return {nm: tuple(res[4 * k:4 * k + 4]) for k, nm in enumerate(names)}, res[-1].reshape(())


WEIGHTS = ['w_ada', 'b_ada', 'g_pre_f1', 'g_post_f1', 'w_f1_in', 'w_f1_out', 'g_pre_m', 'g_post_m', 'w_mix_in',
           'gmlp_norm_g', 'gmlp_norm_b', 'w_spatial', 'b_spatial', 'conv_w', 'conv_b', 'conv_norm_g', 'conv_norm_b',
           'g_out_a', 'g_out_b', 'w_mix_out', 'g_pre_f2', 'g_post_f2', 'w_f2_in', 'w_f2_out']
BIG = ('w_f1_in', 'w_f1_out', 'w_mix_in', 'w_mix_out', 'w_f2_in', 'w_f2_out')


def kernel(x, c, w_ada, b_ada, g_pre_f1, g_post_f1, w_f1_in, w_f1_out, g_pre_m, g_post_m, w_mix_in, gmlp_norm_g, gmlp_norm_b, w_spatial, b_spatial, conv_w, conv_b, conv_norm_g, conv_norm_b, g_out_a, g_out_b, w_mix_out, g_pre_f2, g_post_f2, w_f2_in, w_f2_out, loss_target, m_w_ada, m_b_ada, m_g_pre_f1, m_g_post_f1, m_w_f1_in, m_w_f1_out, m_g_pre_m, m_g_post_m, m_w_mix_in, m_gmlp_norm_g, m_gmlp_norm_b, m_w_spatial, m_b_spatial, m_conv_w, m_conv_b, m_conv_norm_g, m_conv_norm_b, m_g_out_a, m_g_out_b, m_w_mix_out, m_g_pre_f2, m_g_post_f2, m_w_f2_in, m_w_f2_out, v_w_ada, v_b_ada, v_g_pre_f1, v_g_post_f1, v_w_f1_in, v_w_f1_out, v_g_pre_m, v_g_post_m, v_w_mix_in, v_gmlp_norm_g, v_gmlp_norm_b, v_w_spatial, v_b_spatial, v_conv_w, v_conv_b, v_conv_norm_g, v_conv_norm_b, v_g_out_a, v_g_out_b, v_w_mix_out, v_g_pre_f2, v_g_post_f2, v_w_f2_in, v_w_f2_out):
    env = dict(locals())
    wts = {n: env[n] for n in WEIGHTS}
    mom = {n: env["m_" + n] for n in WEIGHTS}
    var = {n: env["v_" + n] for n in WEIGHTS}
    nb, s, _ = x.shape
    t = nb * s
    ax, ay, ac = lax.axis_index("x"), lax.axis_index("y"), lax.axis_index("c")
    j_chip = 2 * ax + ay
    dev = 4 * ax + 2 * ay + ac
    j_arr = j_chip.reshape(1).astype(jnp.int32)

    groups = (("w_f1_in", "w_f1_out"), ("w_mix_in", "w_mix_out"), ("w_f2_in", "w_f2_out"))
    def gather_start(gi, behind):
        srcs = [wts[n][0].astype(BF16) for n in groups[gi]] + ([conv_w[0]] if gi == 1 else [])
        plan_a, plan_b, n_b = _gather_plans([a.shape for a in srcs])
        lands = [lax.dynamic_update_index_in_dim(lax.empty((NCHIP,) + a.shape, a.dtype), a, j_chip, 0) for a in srcs]
        ssem, rsem, srcs, lands, token = _split_start("gw_start%d" % gi, srcs, lands, plan_a, 3 * len(srcs), behind)
        gather[gi] = (srcs, lands, ssem, rsem, plan_a, plan_b, n_b)
        return token

    def gather_forward(gi, behind):
        srcs, lands, ssem, rsem, plan_a, plan_b, n_b = gather[gi]
        ssem, rsem, lands, token = _split_forward("gw_fwd%d" % gi, srcs, lands, ssem, rsem, plan_a, plan_b, n_b, behind)
        gather[gi] = (lands, ssem, rsem, plan_b)
        return token

    def gathered(gi, behind):
        lands, ssem, rsem, plan_b = gather[gi]
        return _split_wait("gw_wait%d" % gi, [], lands, ssem, rsem, plan_b, behind)

    gather = {}
    (c_all8,) = _allgather8("gather_c", [c.reshape(8, (nb * D) // 8)])
    token = gather_start(0, c_all8)
    c_all = c_all8.reshape(NDEV * nb, D) + token[0, 0]
    b_sh = lax.dynamic_slice(b_ada, (0, j_chip * ADA_SH), (1, ADA_SH))
    ada_sh = _ada_fwd(c_all, w_ada[0], b_sh)
    token = gather_forward(0, ada_sh)
    (ada4,) = _chip_allgather("gather_ada", [ada_sh + token[0:1, 0:1]])
    token = gather_start(1, ada4)
    token = gather_start(2, token)
    ada4 = ada4 + token[0:1, 0:1]
    ada_me = lax.dynamic_slice(ada4, (0, dev * nb, 0), (NCHIP, nb, ADA_SH))
    ada_me = jnp.transpose(ada_me, (1, 0, 2)).reshape(nb, NMOD * D)
    sh1, sc1, gt1, sh2, sc2, gt2, sh3, sc3, gt3 = [ada_me[:, k * D:(k + 1) * D].reshape(nb, 1, D) for k in range(NMOD)]

    wcat = jnp.transpose(w_spatial[0], (1, 0, 2)).reshape(CH, NH * CH)
    wcat_t = jnp.transpose(w_spatial[0], (0, 2, 1)).reshape(NH * CH, CH)
    bspt = jnp.repeat(b_spatial[0].T, HD, axis=1)

    w1i, w1o = gathered(0, sh1)
    w1o = w1o.reshape(DFF, D)
    x1, f1, p1 = _ffn_fwd(x, sh1, sc1, gt1, g_pre_f1, g_post_f1, w1i, w1o)
    wmi, wmo, cw4 = gathered(1, gather_forward(1, x1))
    wmo = wmo.reshape(D, D)
    cw_full = jnp.transpose(cw4, (1, 0, 2)).reshape(CK, WB)
    cw_pad = jnp.pad(cw_full, ((0, HALO - CK), (0, 0)))
    u, v, a, g = _mix_in_fwd(x1, sh2, sc2, g_pre_m, wmi)
    x2, conv, yb, m = _mix_mid_fwd(x1, u, v, a, g, gt2, gmlp_norm_g, gmlp_norm_b, wcat, bspt, cw_pad, conv_b,
                                   conv_norm_g, conv_norm_b, g_out_a, g_out_b, wmo, g_post_m)
    w2i, w2o = gathered(2, gather_forward(2, x2))
    w2o = w2o.reshape(DFF, D)
    dx3, f2, p2, lsum = _ffn_fwd(x2, sh3, sc3, gt3, g_pre_f2, g_post_f2, w2i, w2o, target=loss_target)

    def chip4(pair, rows):
        return [arr.reshape(NCHIP, rows, arr.shape[-1]) for arr in pair]

    def scatter_start(tag, pairs, behind):
        srcs = [p[1] for p in pairs]
        lands = [lax.empty((3,) + a.shape[1:], a.dtype) for a in srcs]
        ssem, rsem, srcs, lands, token = _split_start("gs_start_" + tag, srcs, lands, _scatter_plan(len(srcs)),
                                                      3 * len(srcs), behind)
        return (srcs, lands, ssem, rsem), token

    def scatter_wait(tag, state, behind):
        srcs, lands, ssem, rsem = state
        return _split_wait("gs_wait_" + tag, srcs, lands, ssem, rsem, _scatter_plan(len(srcs)), behind)

    def swap_plan(n):
        def plan(x, y, c, src, land):
            return [], [(src[a], land[a], (x, y, 1 - c), land[a]) for a in range(n)]
        return plan

    def swap_start(tag, parts, behind):
        lands = [lax.empty(a.shape, a.dtype) for a in parts]
        ssem, rsem, srcs, lands, token = _split_start("swap_start_" + tag, parts, lands, swap_plan(len(parts)),
                                                      len(parts), behind)
        return (srcs, lands, ssem, rsem), token

    def swap_wait(tag, state, behind):
        srcs, lands, ssem, rsem = state
        return _split_wait("swap_wait_" + tag, srcs, lands, ssem, rsem, swap_plan(len(srcs)), behind)

    def allgather_start(tag, arrs, behind):
        lands = [lax.dynamic_update_index_in_dim(lax.empty((NDEV,) + a.shape, a.dtype), a, dev, 0) for a in arrs]
        ssem, rsem, srcs, lands, token = _split_start("small_start_" + tag, arrs, lands, _allgather_plan(len(arrs)),
                                                      7 * len(arrs), behind)
        return (srcs, lands, ssem, rsem), token

    def allgather_wait(tag, state, behind):
        srcs, lands, ssem, rsem = state
        return _split_wait("small_wait_" + tag, srcs, lands, ssem, rsem, _allgather_plan(len(srcs)), behind)

    out = {}
    pref = jnp.stack([j_chip, ac]).astype(jnp.int32)
    dx2, dp2, h3, a2, df2, dg_pre_f2, dg_post_f2, dsh3, dsc3, dgt3 = _ffn_bwd(
        dx3, x2, f2, p2, sh3, sc3, gt3, g_pre_f2, g_post_f2, w2i, w2o)
    gw2i = _wgrad("wgrad_f2_in", h3.reshape(t, D), dp2.reshape(t, 2 * DFF), 2 * DFF // NCHIP, True, pref)
    gw2o = chip4(_wgrad("wgrad_f2_out", a2.reshape(t, DFF), df2.reshape(t, D), D // 2, False, pref), DFF // NCHIP)
    scat_f2, tok = scatter_start("f2", [gw2i, gw2o], dg_post_f2)
    dy, dm, dg_post_m, dgt2 = _mix_out_bwd(dx2, m, gt2 + tok[0, 0], g_post_m, wmo)
    gwmo = chip4(_wgrad("wgrad_mix_out", yb.reshape(t, D), dm.reshape(t, D), D // 2, False, pref), D // NCHIP)
    (du, dv, dconv, dwcat, dbsp, dgn_g, dgn_b, dgo_a, dgo_b, dcn_g, dcn_b, dcb) = _mix_mid_bwd(
        dy, u, v, conv, gmlp_norm_g, gmlp_norm_b, wcat, wcat_t, bspt, conv_norm_g, conv_norm_b, g_out_a, g_out_b)
    dx1, dproj, h2, dg_pre_m, dsh2, dsc2, dcw = _mix_in_bwd(dx2, x1, du, dv, dconv, a, g, sh2, sc2, g_pre_m, wmi, cw_pad)
    recv_f2 = scatter_wait("f2", scat_f2, dg_pre_m)
    res_mi = _wgrad("wgrad_mix_in", h2.reshape(t, D), dproj.reshape(t, 4 * WA), WA, True, pref,
                    sides=[_side_sum4(gw2i[0], recv_f2[0]), _side_sum4(gw2o[0], recv_f2[1])])
    gwmi, part_f2 = res_mi[:2], list(res_mi[2:])
    scat_mix, tok = scatter_start("mix", [gwmi, gwmo], dg_pre_m)
    swap_f2, tok = swap_start("f2", part_f2, tok)

    vec_grads = dict(g_pre_m=dg_pre_m, g_post_m=dg_post_m, g_pre_f2=dg_pre_f2, g_post_f2=dg_post_f2)
    pair_grads = dict(gmlp_norm_g=dgn_g, gmlp_norm_b=dgn_b, conv_b=dcb, conv_norm_g=dcn_g, conv_norm_b=dcn_b,
                      g_out_a=dgo_a, g_out_b=dgo_b)
    pack = _pack_small([vec_grads[n] for n in VEC_ORDER[2:]], [pair_grads[n] for n in PAIR_ORDER], dbsp, dwcat, lsum)
    dada_early = jnp.concatenate([q.reshape(nb, D) for q in (dsh2, dsc2, dgt2, dsh3, dsc3, dgt3)], axis=1)
    early, tok2 = allgather_start("early", [pack, dcw, dada_early.reshape(8, (nb * 6 * D) // 8)], tok)
    grad_x, dp1, h1, a1, df1, dg_pre_f1, dg_post_f1, dsh1, dsc1, dgt1 = _ffn_bwd(
        dx1, x, f1, p1, sh1 + tok2[0, 0], sc1, gt1, g_pre_f1, g_post_f1, w1i, w1o)
    late_pack = _pack_late([dg_pre_f1, dg_post_f1] + [q.reshape(nb, D) for q in (dsh1, dsc1, dgt1)])
    late, tok2 = allgather_start("late", [late_pack], dg_post_f1)
    recv_mix = scatter_wait("mix", scat_mix, tok2)
    res_1o = _wgrad("wgrad_f1_out", a1.reshape(t, DFF), df1.reshape(t, D), D // 2, False, pref, tk=1024,
                    sides=[_side_sum4(gwmi[0], recv_mix[0]), _side_sum4(gwmo[0], recv_mix[1])])
    gw1o, part_mix = chip4(res_1o[:2], DFF // NCHIP), list(res_1o[2:])
    other_f2 = swap_wait("f2", swap_f2, part_mix[0])
    adam_sides = [_side_adam(wts[n][0], mom[n][0], var[n][0], part_f2[k], other_f2[k])
                  for k, n in enumerate(("w_f2_in", "w_f2_out"))]
    res_1i = _wgrad("wgrad_f1_in", h1.reshape(t, D), dp1.reshape(t, 2 * DFF), 2 * DFF // NCHIP, True, pref, tk=1024,
                    sides=adam_sides)
    gw1i = res_1i[:2]
    out["w_f2_in"] = tuple(r[None] for r in res_1i[2:6])
    out["w_f2_out"] = tuple(r[None] for r in res_1i[6:10])
    c_arr = ac.reshape(1).astype(jnp.int32)
    sib = _pair_exchange("pair_f1", [gw1i[1], gw1o[1]])
    pair_i = _pair_sum("pairsum_f1_in", gw1i[0], sib[0], c_arr)
    pair_o = _pair_sum("pairsum_f1_out", gw1o[0], sib[1], c_arr)
    scat_f1, tok = scatter_start("f1", [pair_i, pair_o], tok2)
    other_mix = _sibling_swap("swap_mix", part_mix, tok)
    for k, n in enumerate(("w_mix_in", "w_mix_out")):
        out[n] = tuple(r[None] for r in _adam_big("adam_" + n, wts[n][0], mom[n][0], var[n][0], part_mix[k], other_mix[k]))

    pack_all, dcw_all, dada_early8 = allgather_wait("early", early, out["w_mix_out"][3])
    (late_all,) = allgather_wait("late", late, pack_all)
    dada_late = jnp.transpose(late_all[:, 2:8, :].reshape(NDEV, 3, nb, D), (0, 2, 1, 3)).reshape(NDEV * nb, 3 * D)
    dada_all = jnp.concatenate([dada_late, dada_early8.reshape(NDEV * nb, 6 * D)], axis=1)
    dcw_mine = lax.dynamic_slice(dcw_all, (0, 0, j_chip * (WB // NCHIP)), (NDEV, HALO, WB // NCHIP))
    small = {n: (wts[n], mom[n], var[n]) for n in list(VEC_ORDER) + list(PAIR_ORDER) + ["b_spatial", "w_spatial", "conv_w", "b_ada"]}
    small_out, loss = _small_adam(pack_all, late_all, dcw_mine, dada_all, small)
    out.update(small_out)
    dada_sh = lax.dynamic_slice(dada_all, (0, j_chip * ADA_SH), (NDEV * nb, ADA_SH))
    out["w_ada"] = tuple(r[None] for r in _ada_bwd_adam(c_all, dada_sh, w_ada[0], m_w_ada[0], v_w_ada[0]))
    recv = scatter_wait("f1", scat_f1, out["w_ada"][3])
    names = ("w_f1_in", "w_f1_out")
    mine = [_sum4("sum4_" + n, p[0], recv[k], j_arr)
            for k, (n, p) in enumerate(zip(names, (pair_i, pair_o)))]
    theirs = _sibling_swap("swap_f1", mine)
    for k, n in enumerate(names):
        out[n] = tuple(r[None] for r in _adam_halves("adam_" + n, wts[n][0], mom[n][0], var[n][0], mine[k], theirs[k],
                                                     c_arr))

    res = [loss, grad_x]
    for k in range(4):
        res += [out[n][k] for n in WEIGHTS]
    return tuple(res)
```

```python
import functools

import jax
import jax.numpy as jnp
from jax import lax
from jax.experimental import pallas as pl
from jax.experimental.pallas import tpu as pltpu

D = 1024
DFF = 2816
WA = 512
WB = 512
NH = 8
HD = 64
CH = 128
CK = 31
HALO = 32
NMOD = 9
EPS = 1e-6
NCHIP = 4
NDEV = 8
FBLK = DFF // 2
ADA_SH = NMOD * D // NCHIP

LR, B1, B2, EPS_A, WD, STEP = 0.001, 0.9, 0.999, 1e-08, 0.01, 10

F32 = jnp.float32
BF16 = jnp.bfloat16
MESH = pl.DeviceIdType.MESH
ANY = pl.BlockSpec(memory_space=pl.ANY)
VMEM_FULL = pl.BlockSpec(memory_space=pltpu.VMEM)
VMEM_LIMIT = 56 * 1024 * 1024

NT = (((1,), (1,)), ((), ()))
TN = (((0,), (0,)), ((), ()))


def _dot(a, b):
    return jnp.dot(a, b, preferred_element_type=F32)


def _dot_nt(a, b):
    return lax.dot_general(a, b, NT, preferred_element_type=F32)


def _dot_tn(a, b):
    return lax.dot_general(a, b, TN, preferred_element_type=F32)


def _cparams():
    return pltpu.CompilerParams(vmem_limit_bytes=VMEM_LIMIT)


def _allgather8(name, arrs):
    n = len(arrs)

    def body(*refs):
        ins, outs = refs[:n], refs[n:2 * n]
        send_sems, recv_sems, local_sems = refs[2 * n:]
        x, y, c = lax.axis_index("x"), lax.axis_index("y"), lax.axis_index("c")
        me, sibling = (x, y, c), (x, y, 1 - c)
        chips = [(1 - x, y), (x, 1 - y), (1 - x, 1 - y)]

        def copy(a, k, block, to, src=None):
            rows = outs[a].at[4 * block[0] + 2 * block[1] + block[2]]
            return pltpu.make_async_remote_copy(
                src_ref=rows if src is None else src, dst_ref=rows,
                send_sem=send_sems.at[a, k], recv_sem=recv_sems.at[a, k],
                device_id=to, device_id_type=MESH)

        started, mine = [], []
        for a in range(n):
            loc = pltpu.make_async_copy(ins[a], outs[a].at[4 * x + 2 * y + c], local_sems.at[a])
            loc.start()
            mine.append(loc)
            first = [copy(a, 0, me, sibling, src=ins[a])]
            first += [copy(a, 1 + j, me, (*chip, c), src=ins[a]) for j, chip in enumerate(chips)]
            for cp in first:
                cp.start()
            started += first
        for a in range(n):
            for j, chip in enumerate(chips):
                copy(a, 1 + j, (*chip, c), me).wait_recv()
                fwd = copy(a, 4 + j, (*chip, c), sibling)
                fwd.start()
                started.append(fwd)
        for a in range(n):
            copy(a, 0, sibling, me).wait_recv()
            for j, chip in enumerate(chips):
                copy(a, 4 + j, (*chip, 1 - c), me).wait_recv()
        for cp in started:
            cp.wait_send()
        for loc in mine:
            loc.wait()

    return pl.pallas_call(
        body, name=name,
        out_shape=[jax.ShapeDtypeStruct((NDEV,) + a.shape, a.dtype) for a in arrs],
        in_specs=[ANY] * n, out_specs=[ANY] * n,
        scratch_shapes=[pltpu.SemaphoreType.DMA((n, 7)), pltpu.SemaphoreType.DMA((n, 7)),
                        pltpu.SemaphoreType.DMA((n,))],
    )(*arrs)


def _chip_relations(x, y):
    return [(1 - x, y), (x, 1 - y), (1 - x, 1 - y)]


def _exchange(name, arrs, out_shapes, plan):
    n = len(arrs)
    n_out = len(out_shapes)

    def body(*refs):
        ins, outs = refs[:n], refs[n:n + n_out]
        send_sems, recv_sems, local_sems = refs[n + n_out:]
        x, y, c = lax.axis_index("x"), lax.axis_index("y"), lax.axis_index("c")
        local, sends = plan(x, y, c, ins, outs)
        locs = [pltpu.make_async_copy(s, d, local_sems.at[i]) for i, (s, d) in enumerate(local)]
        for loc in locs:
            loc.start()
        cps = [pltpu.make_async_remote_copy(src_ref=s, dst_ref=d, send_sem=send_sems.at[i], recv_sem=recv_sems.at[i],
                                            device_id=peer, device_id_type=MESH)
               for i, (s, d, peer, _) in enumerate(sends)]
        for cp in cps:
            cp.start()
        for i, (s, _, peer, landing) in enumerate(sends):
            pltpu.make_async_remote_copy(src_ref=s, dst_ref=landing, send_sem=send_sems.at[i], recv_sem=recv_sems.at[i],
                                         device_id=peer, device_id_type=MESH).wait_recv()
        for cp in cps:
            cp.wait_send()
        for loc in locs:
            loc.wait()

    return n, n_out, body


def _run_exchange(name, arrs, out_shapes, plan, n_local, n_send):
    n, n_out, body = _exchange(name, arrs, out_shapes, plan)
    return pl.pallas_call(
        body, name=name, out_shape=out_shapes,
        in_specs=[ANY] * n, out_specs=[ANY] * n_out,
        scratch_shapes=[pltpu.SemaphoreType.DMA((n_send,)), pltpu.SemaphoreType.DMA((n_send,)),
                        pltpu.SemaphoreType.DMA((max(n_local, 1),))],
    )(*arrs)


def _chip_allgather(name, arrs, behind=()):
    n = len(arrs)

    def plan(x, y, c, ins, outs):
        j_me = 2 * x + y
        local = [(ins[a], outs[a].at[j_me]) for a in range(n)]
        sends = []
        for a in range(n):
            for (px, py) in _chip_relations(x, y):
                sends.append((ins[a], outs[a].at[j_me], (px, py, c), outs[a].at[2 * px + py]))
        return local, sends

    shapes = [jax.ShapeDtypeStruct((NCHIP,) + a.shape, a.dtype) for a in arrs]
    return _run_exchange(name, list(arrs) + list(behind), shapes, plan, n, 3 * n)


def _chip_scatter(name, arrs):
    n = len(arrs)

    def plan(x, y, c, ins, outs):
        sends = []
        for a in range(n):
            for k, (px, py) in enumerate(_chip_relations(x, y)):
                sends.append((ins[a].at[2 * px + py], outs[a].at[k], (px, py, c), outs[a].at[k]))
        return [], sends

    shapes = [jax.ShapeDtypeStruct((3,) + a.shape[1:], a.dtype) for a in arrs]
    return _run_exchange(name, arrs, shapes, plan, 0, 3 * n)


def _sibling_swap(name, arrs, behind=None):
    n = len(arrs)

    def plan(x, y, c, ins, outs):
        return [], [(ins[a], outs[a], (x, y, 1 - c), outs[a]) for a in range(n)]

    shapes = [jax.ShapeDtypeStruct(a.shape, a.dtype) for a in arrs]
    return _run_exchange(name, list(arrs) + ([] if behind is None else [behind]), shapes, plan, 0, n)


HBM = pl.BlockSpec(memory_space=pltpu.HBM)
SEM = pl.BlockSpec(memory_space=pltpu.SEMAPHORE)
EFFECT = pltpu.SideEffectType.DATAFLOW_SIDE_EFFECTING


def _split_start(name, srcs, lands, plan, n_send, after):
    n, nl = len(srcs), len(lands)

    def body(*refs):
        src, land = refs[:n], refs[n:n + nl]
        send_sems, recv_sems = refs[n + nl + 1], refs[n + nl + 2]
        token = refs[-2]
        local_sems = refs[-1]
        x, y, c = lax.axis_index("x"), lax.axis_index("y"), lax.axis_index("c")
        local, sends = plan(x, y, c, src, land)
        locs = [pltpu.make_async_copy(s, d, local_sems.at[i]) for i, (s, d) in enumerate(local)]
        for loc in locs:
            loc.start()
        for loc in locs:
            loc.wait()
        for i, (s, d, peer, _) in enumerate(sends):
            pltpu.make_async_remote_copy(src_ref=s, dst_ref=d, send_sem=send_sems.at[i], recv_sem=recv_sems.at[i],
                                         device_id=peer, device_id_type=MESH).start()
        token[...] = jnp.zeros_like(token)

    thru = [pltpu.HBM(a.shape, a.dtype) for a in list(srcs) + list(lands)]
    res = pl.pallas_call(
        body, name=name,
        out_shape=(pltpu.SemaphoreType.DMA((n_send,)), pltpu.SemaphoreType.DMA((n_send,)), *thru,
                   jax.ShapeDtypeStruct((8, 128), F32)),
        in_specs=[HBM] * (n + nl) + [ANY],
        out_specs=(SEM, SEM, *([HBM] * (n + nl)), pl.BlockSpec(memory_space=pltpu.VMEM)),
        input_output_aliases={i: 2 + i for i in range(n + nl)},
        scratch_shapes=[pltpu.SemaphoreType.DMA((max(len(srcs), 1),))],
        compiler_params=pltpu.CompilerParams(has_side_effects=EFFECT),
    )(*[pltpu.with_memory_space_constraint(a, pltpu.HBM) for a in list(srcs) + list(lands)], after)
    return res[0], res[1], list(res[2:2 + n]), list(res[2 + n:2 + n + nl]), res[-1]


def _split_wait(name, srcs, lands, send_sems, recv_sems, plan, after):
    n, nl = len(srcs), len(lands)

    def body(*refs):
        src, land = refs[:n], refs[n:n + nl]
        send_sems, recv_sems = refs[n + nl], refs[n + nl + 1]
        x, y, c = lax.axis_index("x"), lax.axis_index("y"), lax.axis_index("c")
        _, sends = plan(x, y, c, src, land)
        for i, (s, _, peer, landing) in enumerate(sends):
            cp = pltpu.make_async_remote_copy(src_ref=s, dst_ref=landing, send_sem=send_sems.at[i],
                                              recv_sem=recv_sems.at[i], device_id=peer, device_id_type=MESH)
            cp.wait_send()
            cp.wait_recv()

    thru = [pltpu.HBM(a.shape, a.dtype) for a in list(srcs) + list(lands)]
    res = pl.pallas_call(
        body, name=name, out_shape=tuple(thru),
        in_specs=[HBM] * (n + nl) + [SEM, SEM, ANY], out_specs=tuple([HBM] * (n + nl)),
        input_output_aliases={i: i for i in range(n + nl)},
        compiler_params=pltpu.CompilerParams(has_side_effects=EFFECT),
    )(*srcs, *lands, send_sems, recv_sems, after)
    return list(res[n:])


def _split_forward(name, srcs, lands, send_a, recv_a, plan_a, plan_b, n_b, after):
    n, nl = len(srcs), len(lands)

    def body(*refs):
        src, land = refs[:n], refs[n:n + nl]
        send_a, recv_a = refs[n + nl], refs[n + nl + 1]
        send_b, recv_b = refs[n + nl + 3], refs[n + nl + 4]
        token = refs[-1]
        x, y, c = lax.axis_index("x"), lax.axis_index("y"), lax.axis_index("c")
        _, first = plan_a(x, y, c, src, land)
        for i, (s, _, peer, landing) in enumerate(first):
            cp = pltpu.make_async_remote_copy(src_ref=s, dst_ref=landing, send_sem=send_a.at[i],
                                              recv_sem=recv_a.at[i], device_id=peer, device_id_type=MESH)
            cp.wait_send()
            cp.wait_recv()
        _, second = plan_b(x, y, c, src, land)
        for i, (s, d, peer, _) in enumerate(second):
            pltpu.make_async_remote_copy(src_ref=s, dst_ref=d, send_sem=send_b.at[i], recv_sem=recv_b.at[i],
                                         device_id=peer, device_id_type=MESH).start()
        token[...] = jnp.zeros_like(token)

    thru = [pltpu.HBM(a.shape, a.dtype) for a in lands]
    res = pl.pallas_call(
        body, name=name,
        out_shape=(pltpu.SemaphoreType.DMA((n_b,)), pltpu.SemaphoreType.DMA((n_b,)), *thru,
                   jax.ShapeDtypeStruct((8, 128), F32)),
        in_specs=[HBM] * (n + nl) + [SEM, SEM, ANY],
        out_specs=(SEM, SEM, *([HBM] * nl), pl.BlockSpec(memory_space=pltpu.VMEM)),
        input_output_aliases={n + i: 2 + i for i in range(nl)},
        compiler_params=pltpu.CompilerParams(has_side_effects=EFFECT),
    )(*srcs, *lands, send_a, recv_a, after)
    return res[0], res[1], list(res[2:2 + nl]), res[-1]


def _gather_plans(shapes):
    n = len(shapes)

    def halves(a, c):
        rows = shapes[a][0] // 2
        return pl.ds(pl.multiple_of(c * rows, 16), rows), pl.ds(pl.multiple_of((1 - c) * rows, 16), rows)

    def split(a):
        return shapes[a][0] % 32 == 0

    def plan_a(x, y, c, src, land):
        j_me = 2 * x + y
        sends = []
        for a in range(n):
            for (px, py) in _chip_relations(x, y):
                if split(a):
                    mine, _ = halves(a, c)
                    sends.append((src[a].at[mine], land[a].at[j_me, mine], (px, py, c), land[a].at[2 * px + py, mine]))
                else:
                    sends.append((src[a], land[a].at[j_me], (px, py, c), land[a].at[2 * px + py]))
        return [], sends

    def plan_b(x, y, c, src, land):
        sends = []
        for a in range(n):
            if split(a):
                mine, other = halves(a, c)
                for (px, py) in _chip_relations(x, y):
                    j = 2 * px + py
                    sends.append((land[a].at[j, mine], land[a].at[j, mine], (x, y, 1 - c), land[a].at[j, other]))
        return [], sends

    n_b = 3 * sum(1 for a in range(n) if split(a))
    return plan_a, plan_b, n_b


def _allgather_plan(n):
    flips = [(dx, dy, dc) for dx in (0, 1) for dy in (0, 1) for dc in (0, 1) if dx or dy or dc]

    def plan(x, y, c, src, land):
        sends = []
        for a in range(n):
            for dx, dy, dc in flips:
                px, py, pc = x ^ dx, y ^ dy, c ^ dc
                sends.append((src[a], land[a].at[4 * x + 2 * y + c], (px, py, pc), land[a].at[4 * px + 2 * py + pc]))
        return [], sends

    return plan


def _scatter_plan(n):
    def plan(x, y, c, src, land):
        sends = []
        for a in range(n):
            for k, (px, py) in enumerate(_chip_relations(x, y)):
                sends.append((src[a].at[2 * px + py], land[a].at[k], (px, py, c), land[a].at[k]))
        return [], sends

    return plan


def _rms(x):
    r = lax.rsqrt(jnp.mean(x * x, axis=-1, keepdims=True) + EPS)
    return x * r, r


def _rms_bwd(dy, n, r, g):
    dg = jnp.sum(dy * n, axis=0, keepdims=True)
    dn = dy * g
    dx = r * (dn - n * jnp.mean(dn * n, axis=-1, keepdims=True))
    return dx, dg


def _ln(x):
    mu = jnp.mean(x, axis=-1, keepdims=True)
    xc = x - mu
    rstd = lax.rsqrt(jnp.mean(xc * xc, axis=-1, keepdims=True) + EPS)
    return xc * rstd, rstd


def _ln_bwd(dy, xhat, rstd, g):
    dg = jnp.sum(dy * xhat, axis=0, keepdims=True)
    db = jnp.sum(dy, axis=0, keepdims=True)
    dxh = dy * g
    dx = rstd * (dxh - jnp.mean(dxh, axis=-1, keepdims=True) - xhat * jnp.mean(dxh * xhat, axis=-1, keepdims=True))
    return dx, dg, db


def _sigmoid(x):
    return jax.nn.sigmoid(x)


def _dsilu(x, s):
    return s * (1.0 + x * (1.0 - s))


def _adam(w, g, m, v):
    m = B1 * m + (1.0 - B1) * g
    v = B2 * v + (1.0 - B2) * (g * g)
    m_hat = m / (1.0 - B1 ** STEP)
    v_hat = v / (1.0 - B2 ** STEP)
    delta = -LR * (m_hat / (jnp.sqrt(v_hat) + EPS_A) + WD * w)
    return delta, m, v


def _head_mask(shape):
    lane = lax.broadcasted_iota(jnp.int32, shape, len(shape) - 1)
    return [(lane >= h * HD) & (lane < (h + 1) * HD) for h in range(NH)]


def _first(b, i):
    return jnp.logical_and(b == 0, i == 0)


def _acc(ref, val, first):
    @pl.when(first)
    def _():
        ref[...] = val

    @pl.when(jnp.logical_not(first))
    def _():
        ref[...] += val


def _ada_fwd(c_all, w_sh, b_sh):
    nb = c_all.shape[0]
    tn = 768

    def body(c_ref, w_ref, b_ref, o_ref):
        cv = c_ref[...]
        cs = (cv * _sigmoid(cv)).astype(BF16)
        o_ref[...] = _dot(cs, w_ref[...].astype(BF16)) + b_ref[...]

    return pl.pallas_call(
        body, name="ada_fwd", grid=(ADA_SH // tn,),
        out_shape=jax.ShapeDtypeStruct((nb, ADA_SH), F32),
        in_specs=[pl.BlockSpec((nb, D), lambda j: (0, 0)), pl.BlockSpec((D, tn), lambda j: (0, j)),
                  pl.BlockSpec((1, tn), lambda j: (0, j))],
        out_specs=pl.BlockSpec((nb, tn), lambda j: (0, j)),
        compiler_params=_cparams(),
    )(c_all, w_sh, b_sh)


def _ada_bwd_adam(c_all, dada_sh, w, m, v):
    nb = c_all.shape[0]
    tn = 768

    def body(c_ref, d_ref, w_ref, m_ref, v_ref, g_out, d_out, m_out, v_out):
        cv = c_ref[...]
        cs = (cv * _sigmoid(cv)).astype(BF16)
        g = _dot_tn(cs, d_ref[...].astype(BF16))
        delta, m2, v2 = _adam(w_ref[...], g, m_ref[...], v_ref[...])
        g_out[...] = g
        d_out[...] = delta
        m_out[...] = m2
        v_out[...] = v2

    big = pl.BlockSpec((D, tn), lambda j: (0, j))
    shape = jax.ShapeDtypeStruct((D, ADA_SH), F32)
    return pl.pallas_call(
        body, name="ada_bwd_adam", grid=(ADA_SH // tn,),
        out_shape=[shape] * 4,
        in_specs=[pl.BlockSpec((nb, D), lambda j: (0, 0)), pl.BlockSpec((nb, tn), lambda j: (0, j)), big, big, big],
        out_specs=[big] * 4,
        compiler_params=_cparams(),
    )(c_all, dada_sh, w, m, v)


def _tok_specs(tm, width):
    return pl.BlockSpec((1, tm, width), lambda b, i: (b, i, 0))


def _mod_spec():
    return pl.BlockSpec((1, 1, D), lambda b, i: (b, 0, 0))


def _row_spec(width=D):
    return pl.BlockSpec((1, width), lambda b, i: (0, 0))


def _ffn_fwd(x, sh, sc, gt, g_pre, g_post, w_in4, w_out, target=None):
    nb, s, _ = x.shape
    tm = min(512, s)
    with_loss = target is not None

    def body(*refs):
        if with_loss:
            (x_ref, sh_ref, sc_ref, gt_ref, gpre_ref, gpost_ref, win_ref, wout_ref, tgt_ref,
             xo_ref, df_ref, p_ref, ls_ref, dgpost_ref, dgt_ref) = refs
        else:
            (x_ref, sh_ref, sc_ref, gt_ref, gpre_ref, gpost_ref, win_ref, wout_ref,
             xo_ref, f_ref, p_ref) = refs
        xv = x_ref[0]
        n, _ = _rms(xv)
        h = (n * gpre_ref[...]) * (1.0 + sc_ref[0]) + sh_ref[0]
        hb = h.astype(BF16)
        acc = jnp.zeros((tm, D), F32)
        for j in range(2):
            gate = _dot(hb, win_ref[j])
            up = _dot(hb, win_ref[2 + j])
            p_ref[0, :, j * FBLK:(j + 1) * FBLK] = gate.astype(BF16)
            p_ref[0, :, DFF + j * FBLK:DFF + (j + 1) * FBLK] = up.astype(BF16)
            a = (gate * _sigmoid(gate)) * up
            acc = acc + _dot(a.astype(BF16), wout_ref[j * FBLK:(j + 1) * FBLK, :])
        nf, q = _rms(acc)
        gpost = gpost_ref[...]
        half_gate = 0.5 * gt_ref[0]
        out = xv + half_gate * (nf * gpost)
        if with_loss:
            first = _first(pl.program_id(0), pl.program_id(1))
            err = out - tgt_ref[0]
            dout = err * (1.0 / D)
            xo_ref[0] = dout
            row = jnp.sum(err * err, axis=0, keepdims=True)
            part = row[:, 0:128]
            for k in range(1, D // 128):
                part = part + row[:, k * 128:(k + 1) * 128]
            _acc(ls_ref, part, first)
            df, dgpost = _rms_bwd(dout * half_gate, nf, q, gpost)
            df_ref[0] = df.astype(BF16)
            _acc(dgpost_ref, dgpost, first)
            _acc(dgt_ref, jnp.sum(dout * (0.5 * (nf * gpost)), axis=0, keepdims=True)[None], pl.program_id(1) == 0)
        else:
            f_ref[0] = acc
            xo_ref[0] = out

    in_specs = [_tok_specs(tm, D), _mod_spec(), _mod_spec(), _mod_spec(), _row_spec(), _row_spec(), VMEM_FULL, VMEM_FULL]
    args = [x, sh, sc, gt, g_pre, g_post, w_in4, w_out]
    out_shape = [jax.ShapeDtypeStruct((nb, s, D), F32), jax.ShapeDtypeStruct((nb, s, D), BF16 if with_loss else F32),
                 jax.ShapeDtypeStruct((nb, s, 2 * DFF), BF16)]
    out_specs = [_tok_specs(tm, D), _tok_specs(tm, D), _tok_specs(tm, 2 * DFF)]
    if with_loss:
        in_specs.append(_tok_specs(tm, D))
        args.append(target)
        out_shape += [jax.ShapeDtypeStruct((1, 128), F32), jax.ShapeDtypeStruct((1, D), F32),
                      jax.ShapeDtypeStruct((nb, 1, D), F32)]
        out_specs += [pl.BlockSpec((1, 128), lambda b, i: (0, 0)), _row_spec(), _mod_spec()]
    return pl.pallas_call(
        body, name="ffn_loss_fwd" if with_loss else "ffn_fwd", grid=(nb, s // tm),
        out_shape=out_shape, in_specs=in_specs, out_specs=out_specs,
        compiler_params=_cparams(),
    )(*args)


def _ffn_bwd(dxo, x, f, p, sh, sc, gt, g_pre, g_post, w_in4, w_out, df=None):
    nb, s, _ = x.shape
    tm = min(256, s)
    given = df is not None

    def body(*refs):
        if given:
            (dxo_ref, x_ref, dfin_ref, p_ref, sh_ref, sc_ref, gpre_ref, win_ref, wout_ref,
             dx_ref, dp_ref, h_ref, a_ref, dgpre_ref, dsh_ref, dsc_ref) = refs
        else:
            (dxo_ref, x_ref, f_ref, p_ref, sh_ref, sc_ref, gt_ref, gpre_ref, gpost_ref, win_ref, wout_ref,
             dx_ref, dp_ref, h_ref, a_ref, df_ref, dgpre_ref, dgpost_ref, dsh_ref, dsc_ref, dgt_ref) = refs
        b, i = pl.program_id(0), pl.program_id(1)
        dxo_v = dxo_ref[0]
        if given:
            dfb = dfin_ref[0]
        else:
            nf, q = _rms(f_ref[0])
            gpost = gpost_ref[...]
            dgt = jnp.sum(dxo_v * (0.5 * (nf * gpost)), axis=0, keepdims=True)
            do = dxo_v * (0.5 * gt_ref[0])
            dfv, dgpost = _rms_bwd(do, nf, q, gpost)
            dfb = dfv.astype(BF16)
            df_ref[0] = dfb
        xv = x_ref[0]
        n, r = _rms(xv)
        gpre = gpre_ref[...]
        ng = n * gpre
        scale1 = 1.0 + sc_ref[0]
        h = ng * scale1 + sh_ref[0]
        h_ref[0] = h.astype(BF16)
        dh = jnp.zeros((tm, D), F32)
        for j in range(2):
            gate = p_ref[0, :, j * FBLK:(j + 1) * FBLK].astype(F32)
            up = p_ref[0, :, DFF + j * FBLK:DFF + (j + 1) * FBLK].astype(F32)
            sg = _sigmoid(gate)
            act = gate * sg
            a_ref[0, :, j * FBLK:(j + 1) * FBLK] = (act * up).astype(BF16)
            da = _dot_nt(dfb, wout_ref[j * FBLK:(j + 1) * FBLK, :])
            dgate = (da * up * _dsilu(gate, sg)).astype(BF16)
            dup = (da * act).astype(BF16)
            dp_ref[0, :, j * FBLK:(j + 1) * FBLK] = dgate
            dp_ref[0, :, DFF + j * FBLK:DFF + (j + 1) * FBLK] = dup
            dh = dh + _dot_nt(dgate, win_ref[j]) + _dot_nt(dup, win_ref[2 + j])
        dsh = jnp.sum(dh, axis=0, keepdims=True)
        dsc = jnp.sum(dh * ng, axis=0, keepdims=True)
        dxn, dgpre = _rms_bwd(dh * scale1, n, r, gpre)
        dx_ref[0] = dxo_v + dxn
        _acc(dgpre_ref, dgpre, _first(b, i))
        _acc(dsh_ref, dsh[None], i == 0)
        _acc(dsc_ref, dsc[None], i == 0)
        if not given:
            _acc(dgpost_ref, dgpost, _first(b, i))
            _acc(dgt_ref, dgt[None], i == 0)

    tok = _tok_specs(tm, D)
    mod_shape = jax.ShapeDtypeStruct((nb, 1, D), F32)
    row_shape = jax.ShapeDtypeStruct((1, D), F32)
    big = [jax.ShapeDtypeStruct((nb, s, D), F32), jax.ShapeDtypeStruct((nb, s, 2 * DFF), BF16),
           jax.ShapeDtypeStruct((nb, s, D), BF16), jax.ShapeDtypeStruct((nb, s, DFF), BF16)]
    big_specs = [tok, _tok_specs(tm, 2 * DFF), tok, _tok_specs(tm, DFF)]
    if given:
        return pl.pallas_call(
            body, name="ffn_bwd_after_loss", grid=(nb, s // tm),
            out_shape=big + [row_shape, mod_shape, mod_shape],
            in_specs=[tok, tok, tok, _tok_specs(tm, 2 * DFF), _mod_spec(), _mod_spec(), _row_spec(), VMEM_FULL, VMEM_FULL],
            out_specs=big_specs + [_row_spec(), _mod_spec(), _mod_spec()],
            compiler_params=_cparams(),
        )(dxo, x, df, p, sh, sc, g_pre, w_in4, w_out)
    return pl.pallas_call(
        body, name="ffn_bwd", grid=(nb, s // tm),
        out_shape=big + [jax.ShapeDtypeStruct((nb, s, D), BF16), row_shape, row_shape, mod_shape, mod_shape, mod_shape],
        in_specs=[tok, tok, tok, _tok_specs(tm, 2 * DFF), _mod_spec(), _mod_spec(), _mod_spec(), _row_spec(), _row_spec(),
                  VMEM_FULL, VMEM_FULL],
        out_specs=big_specs + [tok, _row_spec(), _row_spec(), _mod_spec(), _mod_spec(), _mod_spec()],
        compiler_params=_cparams(),
    )(dxo, x, f, p, sh, sc, gt, g_pre, g_post, w_in4, w_out)


def _wgrad(name, a, b, col_block, chip_major):
    t, ka = a.shape
    n = b.shape[1]
    tk = min(t, 512)
    while tk * 2 <= t and t % (tk * 2) == 0 and 2 * (tk * 2) * max(ka, col_block) <= 6 * 1024 * 1024:
        tk *= 2
    nk = t // tk
    nblk = n // col_block

    def body(a_ref, b_ref, o_ref, obf_ref, acc_ref):
        k = pl.program_id(1)

        @pl.when(k == 0)
        def _():
            acc_ref[...] = jnp.zeros_like(acc_ref)

        acc_ref[...] += _dot_tn(a_ref[...], b_ref[...])

        @pl.when(k == nk - 1)
        def _():
            val = acc_ref[...]
            if chip_major:
                o_ref[0] = val
                obf_ref[0] = val.astype(BF16)
            else:
                o_ref[...] = val
                obf_ref[...] = val.astype(BF16)

    if chip_major:
        shape = (nblk, ka, col_block)
        ospec = pl.BlockSpec((1, ka, col_block), lambda j, k: (j, 0, 0))
    else:
        shape = (ka, n)
        ospec = pl.BlockSpec((ka, col_block), lambda j, k: (0, j))
    return pl.pallas_call(
        body, name=name, grid=(nblk, nk),
        out_shape=[jax.ShapeDtypeStruct(shape, F32), jax.ShapeDtypeStruct(shape, BF16)],
        in_specs=[pl.BlockSpec((tk, ka), lambda j, k: (k, 0)), pl.BlockSpec((tk, col_block), lambda j, k: (k, j))],
        out_specs=[ospec, ospec],
        scratch_shapes=[pltpu.VMEM((ka, col_block), F32)],
        compiler_params=_cparams(),
    )(a, b)


def _mix_in_fwd(x, sh, sc, g_pre, w_mi4):
    nb, s, _ = x.shape
    tm = min(512, s)

    def body(x_ref, sh_ref, sc_ref, gpre_ref, w_ref, u_ref, v_ref, a_ref, g_ref):
        n, _ = _rms(x_ref[0])
        hb = ((n * gpre_ref[...]) * (1.0 + sc_ref[0]) + sh_ref[0]).astype(BF16)
        for k, o_ref in enumerate((u_ref, v_ref, a_ref, g_ref)):
            o_ref[0] = _dot(hb, w_ref[k])

    shape = jax.ShapeDtypeStruct((nb, s, WA), F32)
    return pl.pallas_call(
        body, name="mix_in_fwd", grid=(nb, s // tm),
        out_shape=[shape] * 4,
        in_specs=[_tok_specs(tm, D), _mod_spec(), _mod_spec(), _row_spec(), VMEM_FULL],
        out_specs=[_tok_specs(tm, WA)] * 4,
        compiler_params=_cparams(),
    )(x, sh, sc, g_pre, w_mi4)


def _spatial_weights(wcat_ref, transposed):
    w = wcat_ref[...]
    row = lax.broadcasted_iota(jnp.int32, w.shape, 0)
    col = lax.broadcasted_iota(jnp.int32, w.shape, 1)
    keep = ((row & (CH - 1)) <= col) if transposed else ((col & (CH - 1)) <= row)
    return jnp.where(keep, w, 0.0).astype(BF16)


def _expand_heads(vc, masks):
    return jnp.concatenate([jnp.where(mk, vc, jnp.zeros_like(vc)) for mk in masks], axis=0)


def _spatial_bias(bspt_ref):
    return bspt_ref[...]


SHIFTS = 8
TAP_ROWS = 32


def _ext_rows(tm):
    return tm + HALO + SHIFTS


def _make_shifts(ext_ref, sh_ref, tm):
    ext_ref[tm + HALO:tm + HALO + SHIFTS, :] = jnp.zeros((SHIFTS, WB), F32)
    for r in range(SHIFTS):
        sh_ref[r] = ext_ref[r:r + tm + HALO, :]


def _conv_taps(sh_ref, w_ref, tm, taps, emit):
    def block(i, carry):
        r0 = pl.multiple_of(i * TAP_ROWS, TAP_ROWS)
        acc = jnp.zeros((TAP_ROWS, WB), F32)
        for o, k in taps:
            acc = acc + w_ref[k:k + 1, :] * sh_ref[o % SHIFTS, pl.ds(r0 + SHIFTS * (o // SHIFTS), TAP_ROWS), :]
        emit(r0, acc)
        return carry

    lax.fori_loop(0, tm // TAP_ROWS, block, 0)


def _halo_prev_spec(tm):
    return pl.BlockSpec((1, HALO, WB), lambda b, i: (b, jnp.maximum(i * (tm // HALO) - 1, 0), 0))


def _halo_next_spec(tm, s):
    return pl.BlockSpec((1, HALO, WB), lambda b, i: (b, jnp.minimum((i + 1) * (tm // HALO), s // HALO - 1), 0))


def _mix_mid_fwd(x, u, v, a, g, gt, gn_g, gn_b, wcat, bspt, conv_w, conv_b, cn_g, cn_b, go_a, go_b, w_mo, g_post):
    nb, s, _ = x.shape
    tm = min(512, s)

    def body(x_ref, u_ref, v_ref, a_ref, g_ref, ah_ref, gh_ref, gt_ref, gng_ref, gnb_ref, wcat_ref, bspt_ref,
             cw_ref, cb_ref, cng_ref, cnb_ref, goa_ref, gob_ref, wmo_ref, gpost_ref,
             xo_ref, conv_ref, y_ref, m_ref, ext_ref, sh_ref):
        i = pl.program_id(1)
        xhat, _ = _ln(v_ref[0])
        vb = (xhat * gng_ref[...] + gnb_ref[...]).astype(BF16)
        wsb = _spatial_weights(wcat_ref, False)
        bias = _spatial_bias(bspt_ref)
        masks = _head_mask((CH, WA))
        zs = []
        for cidx in range(tm // CH):
            vexp = _expand_heads(vb[cidx * CH:(cidx + 1) * CH, :], masks)
            zs.append(_dot(wsb, vexp) + bias)
        z = jnp.concatenate(zs, axis=0)
        na, _ = _rms(u_ref[0] * z)
        keep = jnp.where(i == 0, 0.0, 1.0).astype(F32)
        ext_ref[0:HALO, :] = (ah_ref[0] * _sigmoid(gh_ref[0])) * keep
        ext_ref[HALO:HALO + tm, :] = a_ref[0] * _sigmoid(g_ref[0])
        _make_shifts(ext_ref, sh_ref, tm)
        cb = cb_ref[...]

        def put_conv(r0, acc):
            conv_ref[0, pl.ds(r0, TAP_ROWS), :] = acc + cb

        _conv_taps(sh_ref, cw_ref, tm, [(k + HALO - (CK - 1), k) for k in range(CK)], put_conv)
        conv = conv_ref[0]
        chat, _ = _ln(conv)
        cln = chat * cng_ref[...] + cnb_ref[...]
        nbb, _ = _rms(cln * _sigmoid(cln))
        yb = jnp.concatenate([na * goa_ref[...], nbb * gob_ref[...]], axis=1).astype(BF16)
        y_ref[0] = yb
        m = _dot(yb, wmo_ref[...])
        m_ref[0] = m
        nm, _ = _rms(m)
        xo_ref[0] = x_ref[0] + gt_ref[0] * (nm * gpost_ref[...])

    t5 = _tok_specs(tm, WA)
    tok = _tok_specs(tm, D)
    r5 = _row_spec(WA)
    full = lambda shape: pl.BlockSpec(shape, lambda b, i: (0,) * len(shape))
    return pl.pallas_call(
        body, name="mix_mid_fwd", grid=(nb, s // tm),
        out_shape=[jax.ShapeDtypeStruct((nb, s, D), F32), jax.ShapeDtypeStruct((nb, s, WB), F32),
                   jax.ShapeDtypeStruct((nb, s, D), BF16), jax.ShapeDtypeStruct((nb, s, D), F32)],
        in_specs=[tok, t5, t5, t5, t5, _halo_prev_spec(tm), _halo_prev_spec(tm), _mod_spec(), r5, r5,
                  full((CH, NH * CH)), full((CH, WA)), full((HALO, WB)), r5, r5, r5, r5, r5, VMEM_FULL, _row_spec()],
        out_specs=[tok, t5, tok, tok],
        scratch_shapes=[pltpu.VMEM((_ext_rows(tm), WB), F32), pltpu.VMEM((SHIFTS, tm + HALO, WB), F32)],
        compiler_params=_cparams(),
    )(x, u, v, a, g, a, g, gt, gn_g, gn_b, wcat, bspt, conv_w, conv_b, cn_g, cn_b, go_a, go_b, w_mo, g_post)


def _mix_out_bwd(dxo, m, gt, g_post, w_mo):
    nb, s, _ = m.shape
    tm = min(512, s)

    def body(dxo_ref, m_ref, gt_ref, gpost_ref, wmo_ref, dy_ref, dm_ref, dgpost_ref, dgt_ref):
        b, i = pl.program_id(0), pl.program_id(1)
        dxo_v = dxo_ref[0]
        nm, q = _rms(m_ref[0])
        gpost = gpost_ref[...]
        dgt = jnp.sum(dxo_v * (nm * gpost), axis=0, keepdims=True)
        dm, dgpost = _rms_bwd(dxo_v * gt_ref[0], nm, q, gpost)
        dmb = dm.astype(BF16)
        dm_ref[0] = dmb
        dy_ref[0] = _dot_nt(dmb, wmo_ref[...])
        _acc(dgpost_ref, dgpost, _first(b, i))
        _acc(dgt_ref, dgt[None], i == 0)

    tok = _tok_specs(tm, D)
    return pl.pallas_call(
        body, name="mix_out_bwd", grid=(nb, s // tm),
        out_shape=[jax.ShapeDtypeStruct((nb, s, D), F32), jax.ShapeDtypeStruct((nb, s, D), BF16),
                   jax.ShapeDtypeStruct((1, D), F32), jax.ShapeDtypeStruct((nb, 1, D), F32)],
        in_specs=[tok, tok, _mod_spec(), _row_spec(), VMEM_FULL],
        out_specs=[tok, tok, _row_spec(), _mod_spec()],
        compiler_params=_cparams(),
    )(dxo, m, gt, g_post, w_mo)


def _mix_mid_bwd(dy, u, v, conv, gn_g, gn_b, wcat, wcat_t, bspt, cn_g, cn_b, go_a, go_b):
    nb, s, _ = dy.shape
    tm = min(512, s)
    nchunk = tm // CH

    def body(dy_ref, u_ref, v_ref, conv_ref, gng_ref, gnb_ref, wcat_ref, wcatt_ref, bspt_ref, cng_ref, cnb_ref,
             goa_ref, gob_ref,
             du_ref, dv_ref, dconv_ref, dwcat_ref, dbsp_ref, dgng_ref, dgnb_ref, dgoa_ref, dgob_ref,
             dcng_ref, dcnb_ref, dcb_ref):
        first = _first(pl.program_id(0), pl.program_id(1))
        dyv = dy_ref[0]
        xhat, rstd = _ln(v_ref[0])
        gng = gng_ref[...]
        vb = (xhat * gng + gnb_ref[...]).astype(BF16)
        wsb = _spatial_weights(wcat_ref, False)
        wsb_t = _spatial_weights(wcatt_ref, True)
        bias = _spatial_bias(bspt_ref)
        masks = _head_mask((CH, WA))
        vexps, zs = [], []
        for cidx in range(nchunk):
            vexp = _expand_heads(vb[cidx * CH:(cidx + 1) * CH, :], masks)
            vexps.append(vexp)
            zs.append(_dot(wsb, vexp) + bias)
        z = jnp.concatenate(zs, axis=0)
        uv = u_ref[0]
        na, ra = _rms(uv * z)
        dya, dgoa = _rms_bwd(dyv[:, 0:WA], na, ra, goa_ref[...])
        du_ref[0] = dya * z
        dz = dya * uv
        dwcat = jnp.zeros((CH, NH * CH), F32)
        dzsum = jnp.zeros((CH, WA), F32)
        dvlns = []
        for cidx in range(nchunk):
            dzc = dz[cidx * CH:(cidx + 1) * CH, :]
            dzsum = dzsum + dzc
            dzb = dzc.astype(BF16)
            dwcat = dwcat + _dot_nt(dzb, vexps[cidx])
            dvexp = _dot(wsb_t, dzb)
            dvl = jnp.zeros((CH, WA), F32)
            for h in range(NH):
                dvl = dvl + jnp.where(masks[h], dvexp[h * CH:(h + 1) * CH, :], 0.0)
            dvlns.append(dvl)
        dvln = jnp.concatenate(dvlns, axis=0)
        dv, dgng, dgnb = _ln_bwd(dvln, xhat, rstd, gng)
        dv_ref[0] = dv
        lane = lax.broadcasted_iota(jnp.int32, (NH, WA), 1)
        head = lax.broadcasted_iota(jnp.int32, (NH, WA), 0)
        sel = jnp.where((lane >= head * HD) & (lane < (head + 1) * HD), 1.0, 0.0).astype(F32)
        dbsp = lax.dot_general(sel, dzsum, NT, preferred_element_type=F32, precision=lax.Precision.HIGHEST)
        chat, crstd = _ln(conv_ref[0])
        cng = cng_ref[...]
        cln = chat * cng + cnb_ref[...]
        sg = _sigmoid(cln)
        nbb, rb = _rms(cln * sg)
        dyb, dgob = _rms_bwd(dyv[:, WA:D], nbb, rb, gob_ref[...])
        dconv, dcng, dcnb = _ln_bwd(dyb * _dsilu(cln, sg), chat, crstd, cng)
        dconv_ref[0] = dconv
        dcb = jnp.sum(dconv, axis=0, keepdims=True)
        for ref, val in ((dwcat_ref, dwcat), (dbsp_ref, dbsp), (dgng_ref, dgng), (dgnb_ref, dgnb), (dgoa_ref, dgoa),
                         (dgob_ref, dgob), (dcng_ref, dcng), (dcnb_ref, dcnb), (dcb_ref, dcb)):
            _acc(ref, val, first)

    t5 = _tok_specs(tm, WA)
    r5 = _row_spec(WA)
    full = lambda shape: pl.BlockSpec(shape, lambda b, i: (0,) * len(shape))
    big = jax.ShapeDtypeStruct((nb, s, WA), F32)
    row = jax.ShapeDtypeStruct((1, WA), F32)
    return pl.pallas_call(
        body, name="mix_mid_bwd", grid=(nb, s // tm),
        out_shape=[big, big, big, jax.ShapeDtypeStruct((CH, NH * CH), F32), jax.ShapeDtypeStruct((NH, CH), F32),
                   row, row, row, row, row, row, row],
        in_specs=[_tok_specs(tm, D), t5, t5, t5, r5, r5, full((CH, NH * CH)), full((NH * CH, CH)), full((CH, WA)),
                  r5, r5, r5, r5],
        out_specs=[t5, t5, t5, full((CH, NH * CH)), full((NH, CH)), r5, r5, r5, r5, r5, r5, r5],
        compiler_params=_cparams(),
    )(dy, u, v, conv, gn_g, gn_b, wcat, wcat_t, bspt, cn_g, cn_b, go_a, go_b)


def _mix_in_bwd(dxo, x, du, dv, dconv, a, g, sh, sc, g_pre, w_mi4, conv_w):
    nb, s, _ = x.shape
    tm = min(512, s)
    n_i = s // tm

    def body(dxo_ref, x_ref, du_ref, dv_ref, dc_ref, dch_ref, a_ref, g_ref, ah_ref, gh_ref, sh_ref, sc_ref,
             gpre_ref, w_ref, cw_ref,
             dx_ref, dproj_ref, h_ref, dgpre_ref, dsh_ref, dsc_ref, dcw_ref, ext_ref, shf_ref, dglu_ref):
        b, i = pl.program_id(0), pl.program_id(1)
        first = _first(b, i)
        av, gv = a_ref[0], g_ref[0]
        sg = _sigmoid(gv)
        dconv = dc_ref[0]
        ext_ref[0:tm, :] = dconv
        ext_ref[tm:tm + HALO, :] = dch_ref[0] * jnp.where(i == n_i - 1, 0.0, 1.0).astype(F32)
        _make_shifts(ext_ref, shf_ref, tm)

        def put_dglu(r0, acc):
            dglu_ref[pl.ds(r0, TAP_ROWS), :] = acc

        _conv_taps(shf_ref, cw_ref, tm, [(CK - 1 - k, k) for k in range(CK)], put_dglu)
        dglu = dglu_ref[...]
        ext_ref[0:HALO, :] = (ah_ref[0] * _sigmoid(gh_ref[0])) * jnp.where(i == 0, 0.0, 1.0).astype(F32)
        ext_ref[HALO:HALO + tm, :] = av * sg
        _make_shifts(ext_ref, shf_ref, tm)

        @pl.when(first)
        def _():
            dcw_ref[...] = jnp.zeros((HALO, WB), F32)

        for k in range(CK):
            o = k + HALO - (CK - 1)
            lo = SHIFTS * (o // SHIFTS)
            dcw_ref[k:k + 1, :] += jnp.sum(dconv * shf_ref[o % SHIFTS, lo:lo + tm, :], axis=0, keepdims=True)
        da = dglu * sg
        dg = dglu * av * (sg * (1.0 - sg))
        parts = [du_ref[0].astype(BF16), dv_ref[0].astype(BF16), da.astype(BF16), dg.astype(BF16)]
        dh = jnp.zeros((tm, D), F32)
        for k in range(4):
            dproj_ref[0, :, k * WA:(k + 1) * WA] = parts[k]
            dh = dh + _dot_nt(parts[k], w_ref[k])
        n, r = _rms(x_ref[0])
        gpre = gpre_ref[...]
        ng = n * gpre
        scale1 = 1.0 + sc_ref[0]
        h_ref[0] = (ng * scale1 + sh_ref[0]).astype(BF16)
        dsh = jnp.sum(dh, axis=0, keepdims=True)
        dsc = jnp.sum(dh * ng, axis=0, keepdims=True)
        dxn, dgpre = _rms_bwd(dh * scale1, n, r, gpre)
        dx_ref[0] = dxo_ref[0] + dxn
        _acc(dgpre_ref, dgpre, first)
        _acc(dsh_ref, dsh[None], i == 0)
        _acc(dsc_ref, dsc[None], i == 0)

    tok = _tok_specs(tm, D)
    t5 = _tok_specs(tm, WA)
    full = lambda shape: pl.BlockSpec(shape, lambda b, i: (0,) * len(shape))
    mod_shape = jax.ShapeDtypeStruct((nb, 1, D), F32)
    return pl.pallas_call(
        body, name="mix_in_bwd", grid=(nb, n_i),
        out_shape=[jax.ShapeDtypeStruct((nb, s, D), F32), jax.ShapeDtypeStruct((nb, s, 4 * WA), BF16),
                   jax.ShapeDtypeStruct((nb, s, D), BF16), jax.ShapeDtypeStruct((1, D), F32), mod_shape, mod_shape,
                   jax.ShapeDtypeStruct((HALO, WB), F32)],
        in_specs=[tok, tok, t5, t5, t5, _halo_next_spec(tm, s), t5, t5, _halo_prev_spec(tm), _halo_prev_spec(tm),
                  _mod_spec(), _mod_spec(), _row_spec(), VMEM_FULL, full((HALO, WB))],
        out_specs=[tok, _tok_specs(tm, 4 * WA), tok, _row_spec(), _mod_spec(), _mod_spec(), full((HALO, WB))],
        scratch_shapes=[pltpu.VMEM((_ext_rows(tm), WB), F32), pltpu.VMEM((SHIFTS, tm + HALO, WB), F32),
                        pltpu.VMEM((tm, WB), F32)],
        compiler_params=_cparams(),
    )(dxo, x, du, dv, dconv, dconv, a, g, a, g, sh, sc, g_pre, w_mi4, conv_w)


def _row_tile(rows, cols):
    best = 8
    for t in range(8, rows + 1, 8):
        if rows % t == 0 and t * cols * 4 <= 1536 * 1024:
            best = t
    return best


def _sum4(name, own4, recv, j_arr):
    _, rows, cols = own4.shape
    tr = _row_tile(rows, cols)

    def body(j_ref, own_ref, recv_ref, o_ref):
        del j_ref
        acc = own_ref[0]
        for k in range(3):
            acc = acc + recv_ref[k].astype(F32)
        o_ref[...] = acc

    return pl.pallas_call(
        body, name=name,
        grid_spec=pltpu.PrefetchScalarGridSpec(
            num_scalar_prefetch=1, grid=(rows // tr,),
            in_specs=[pl.BlockSpec((1, tr, cols), lambda i, j: (j[0], i, 0)),
                      pl.BlockSpec((3, tr, cols), lambda i, j: (0, i, 0))],
            out_specs=pl.BlockSpec((tr, cols), lambda i, j: (i, 0))),
        out_shape=jax.ShapeDtypeStruct((rows, cols), F32),
        compiler_params=_cparams(),
    )(j_arr, own4, recv)


def _pair_exchange(name, arrs):
    n = len(arrs)

    def plan(x, y, c, ins, outs):
        sends = []
        for a in range(n):
            rows = arrs[a].shape[1] // 2
            theirs = pl.ds(pl.multiple_of((1 - c) * rows, 16), rows)
            sends.append((ins[a].at[:, theirs], outs[a], (x, y, 1 - c), outs[a]))
        return [], sends

    shapes = [jax.ShapeDtypeStruct((a.shape[0], a.shape[1] // 2, a.shape[2]), a.dtype) for a in arrs]
    return _run_exchange(name, arrs, shapes, plan, 0, n)


def _pair_sum(name, g32, recv, c_arr):
    nblk, rows, cols = recv.shape
    tr = _row_tile(rows, cols)
    nh = rows // tr

    def body(c_ref, g_ref, r_ref, o32_ref, obf_ref):
        del c_ref
        val = g_ref[0] + r_ref[0].astype(F32)
        o32_ref[0] = val
        obf_ref[0] = val.astype(BF16)

    spec = pl.BlockSpec((1, tr, cols), lambda k, i, c: (k, i, 0))
    return pl.pallas_call(
        body, name=name,
        grid_spec=pltpu.PrefetchScalarGridSpec(
            num_scalar_prefetch=1, grid=(nblk, nh),
            in_specs=[pl.BlockSpec((1, tr, cols), lambda k, i, c: (k, c[0] * nh + i, 0)), spec],
            out_specs=[spec, spec]),
        out_shape=[jax.ShapeDtypeStruct(recv.shape, F32), jax.ShapeDtypeStruct(recv.shape, BF16)],
        compiler_params=_cparams(),
    )(c_arr, g32, recv)


def _adam_halves(name, w, m, v, mine, theirs, c_arr):
    rows, cols = w.shape
    tr = _row_tile(rows // 2, cols)
    nh = (rows // 2) // tr

    def body(c_ref, w_ref, m_ref, v_ref, mine_ref, theirs_ref, g_out, d_out, m_out, v_out):
        here = (pl.program_id(0) // nh) == c_ref[0]
        g = jnp.where(here, mine_ref[...], theirs_ref[...])
        delta, m2, v2 = _adam(w_ref[...], g, m_ref[...], v_ref[...])
        g_out[...] = g
        d_out[...] = delta
        m_out[...] = m2
        v_out[...] = v2

    spec = pl.BlockSpec((tr, cols), lambda i, c: (i, 0))
    shape = jax.ShapeDtypeStruct((rows, cols), F32)
    return pl.pallas_call(
        body, name=name,
        grid_spec=pltpu.PrefetchScalarGridSpec(
            num_scalar_prefetch=1, grid=(2 * nh,),
            in_specs=[spec, spec, spec,
                      pl.BlockSpec((tr, cols), lambda i, c: (jnp.clip(i - c[0] * nh, 0, nh - 1), 0)),
                      pl.BlockSpec((tr, cols), lambda i, c: (jnp.clip(i - (1 - c[0]) * nh, 0, nh - 1), 0))],
            out_specs=[spec] * 4),
        out_shape=[shape] * 4,
        compiler_params=_cparams(),
    )(c_arr, w, m, v, mine, theirs)


def _adam_big(name, w, m, v, ga, gb):
    rows, cols = w.shape
    tr = _row_tile(rows, cols)

    def body(w_ref, m_ref, v_ref, ga_ref, gb_ref, g_out, d_out, m_out, v_out):
        gsum = ga_ref[...] + gb_ref[...]
        delta, m2, v2 = _adam(w_ref[...], gsum, m_ref[...], v_ref[...])
        g_out[...] = gsum
        d_out[...] = delta
        m_out[...] = m2
        v_out[...] = v2

    spec = pl.BlockSpec((tr, cols), lambda i: (i, 0))
    shape = jax.ShapeDtypeStruct((rows, cols), F32)
    return pl.pallas_call(
        body, name=name, grid=(rows // tr,), out_shape=[shape] * 4,
        in_specs=[spec] * 5, out_specs=[spec] * 4, compiler_params=_cparams(),
    )(w, m, v, ga, gb)


PK_VEC = 0
PK_LOSS = 6
PK_PAIR = 8
PK_BSP = 16
PK_WCAT = 24
PK_ROWS = PK_WCAT + CH
PAIR_ORDER = ("gmlp_norm_g", "gmlp_norm_b", "conv_b", "conv_norm_g", "conv_norm_b", "g_out_a", "g_out_b")
VEC_ORDER = ("g_pre_f1", "g_post_f1", "g_pre_m", "g_post_m", "g_pre_f2", "g_post_f2")


def _pack_late(rows):
    counts = [r.shape[0] for r in rows]
    assert sum(counts) == 8

    def body(*refs):
        o_ref = refs[-1]
        at = 0
        for r, cnt in zip(refs[:-1], counts):
            o_ref[at:at + cnt, :] = r[...]
            at += cnt

    return pl.pallas_call(
        body, name="pack_late", out_shape=jax.ShapeDtypeStruct((8, D), F32),
        in_specs=[VMEM_FULL] * len(rows), out_specs=VMEM_FULL, compiler_params=_cparams(),
    )(*rows)


def _pack_small(vecs, pairs, dbsp, dwcat, lsum):
    def body(*refs):
        vec_refs = refs[:4]
        pair_refs = refs[4:11]
        dbsp_ref, dwcat_ref, lsum_ref, o_ref = refs[11:]
        o_ref[0:PK_WCAT, :] = jnp.zeros((PK_WCAT, D), F32)
        o_ref[PK_LOSS:PK_LOSS + 1, 0:128] = lsum_ref[...]
        for k, r in enumerate(vec_refs):
            o_ref[PK_VEC + 2 + k:PK_VEC + 3 + k, :] = r[...]
        for k, r in enumerate(pair_refs):
            row, half = PK_PAIR + k // 2, k % 2
            o_ref[row:row + 1, half * WA:(half + 1) * WA] = r[...]
        o_ref[PK_BSP:PK_BSP + NH, 0:CH] = dbsp_ref[...]
        o_ref[PK_WCAT:PK_ROWS, :] = dwcat_ref[...]

    args = list(vecs) + list(pairs) + [dbsp, dwcat, lsum]
    return pl.pallas_call(
        body, name="pack_small", out_shape=jax.ShapeDtypeStruct((PK_ROWS, D), F32),
        in_specs=[VMEM_FULL] * len(args), out_specs=VMEM_FULL, compiler_params=_cparams(),
    )(*args)


def _small_adam(pack_all, late_all, dcw_all, dada_all, params):
    names = list(VEC_ORDER) + list(PAIR_ORDER) + ["b_spatial", "w_spatial", "conv_w", "b_ada"]
    flat = []
    for nm in names:
        flat += list(params[nm])
    n_in = 4 + len(flat)

    def body(*refs):
        pack_ref, late_ref, dcw_ref, dada_ref = refs[:4]
        prm = refs[4:n_in]
        outs = refs[n_in:]

        def total(r0, nr, c0, nc):
            acc = pack_ref[0, r0:r0 + nr, c0:c0 + nc]
            for d in range(1, NDEV):
                acc = acc + pack_ref[d, r0:r0 + nr, c0:c0 + nc]
            return acc

        def emit(idx, g, getw, put):
            w_ref, m_ref, v_ref = prm[3 * idx:3 * idx + 3]
            delta, m2, v2 = _adam(getw(w_ref), g, getw(m_ref), getw(v_ref))
            for o_ref, val in zip(outs[4 * idx:4 * idx + 4], (g, delta, m2, v2)):
                put(o_ref, val)

        def whole(ref):
            return ref[...]

        def put_whole(ref, val):
            ref[...] = val

        idx = 0
        for k in range(6):
            if k < 2:
                g = late_ref[0, k:k + 1, :]
                for d in range(1, NDEV):
                    g = g + late_ref[d, k:k + 1, :]
            else:
                g = total(PK_VEC + k, 1, 0, D)
            emit(idx, g, whole, put_whole)
            idx += 1
        for k in range(7):
            emit(idx, total(PK_PAIR + k // 2, 1, (k % 2) * WA, WA), whole, put_whole)
            idx += 1
        emit(idx, total(PK_BSP, NH, 0, CH), lambda r: r[0], lambda r, val: r.__setitem__(0, val))
        idx += 1
        row = lax.broadcasted_iota(jnp.int32, (CH, CH), 0)
        col = lax.broadcasted_iota(jnp.int32, (CH, CH), 1)
        for h in range(NH):
            gh = jnp.where(col <= row, total(PK_WCAT, CH, h * CH, CH), 0.0)
            w_ref, m_ref, v_ref = prm[3 * idx:3 * idx + 3]
            delta, m2, v2 = _adam(w_ref[0, h], gh, m_ref[0, h], v_ref[0, h])
            for o_ref, val in zip(outs[4 * idx:4 * idx + 4], (gh, delta, m2, v2)):
                o_ref[0, h] = val
        idx += 1
        gcw = dcw_ref[0, 0:CK, :]
        for d in range(1, NDEV):
            gcw = gcw + dcw_ref[d, 0:CK, :]
        emit(idx, gcw, lambda r: r[0], lambda r, val: r.__setitem__(0, val))
        idx += 1
        emit(idx, jnp.sum(dada_ref[...], axis=0, keepdims=True), whole, put_whole)
        outs[-1][...] = jnp.sum(total(PK_LOSS, 1, 0, 128), axis=1, keepdims=True) * (0.5 / D)

    out_shape = []
    for nm in names:
        w = params[nm][0]
        out_shape += [jax.ShapeDtypeStruct(w.shape, F32)] * 4
    out_shape.append(jax.ShapeDtypeStruct((1, 1), F32))
    res = pl.pallas_call(
        body, name="small_adam", out_shape=out_shape,
        in_specs=[VMEM_FULL] * n_in, out_specs=[VMEM_FULL] * len(out_shape), compiler_params=_cparams(),
    )(pack_all, late_all, dcw_all, dada_all, *flat)
    return {nm: tuple(res[4 * k:4 * k + 4]) for k, nm in enumerate(names)}, res[-1].reshape(())


WEIGHTS = ['w_ada', 'b_ada', 'g_pre_f1', 'g_post_f1', 'w_f1_in', 'w_f1_out', 'g_pre_m', 'g_post_m', 'w_mix_in',
           'gmlp_norm_g', 'gmlp_norm_b', 'w_spatial', 'b_spatial', 'conv_w', 'conv_b', 'conv_norm_g', 'conv_norm_b',
           'g_out_a', 'g_out_b', 'w_mix_out', 'g_pre_f2', 'g_post_f2', 'w_f2_in', 'w_f2_out']
BIG = ('w_f1_in', 'w_f1_out', 'w_mix_in', 'w_mix_out', 'w_f2_in', 'w_f2_out')


def kernel(x, c, w_ada, b_ada, g_pre_f1, g_post_f1, w_f1_in, w_f1_out, g_pre_m, g_post_m, w_mix_in, gmlp_norm_g, gmlp_norm_b, w_spatial, b_spatial, conv_w, conv_b, conv_norm_g, conv_norm_b, g_out_a, g_out_b, w_mix_out, g_pre_f2, g_post_f2, w_f2_in, w_f2_out, loss_target, m_w_ada, m_b_ada, m_g_pre_f1, m_g_post_f1, m_w_f1_in, m_w_f1_out, m_g_pre_m, m_g_post_m, m_w_mix_in, m_gmlp_norm_g, m_gmlp_norm_b, m_w_spatial, m_b_spatial, m_conv_w, m_conv_b, m_conv_norm_g, m_conv_norm_b, m_g_out_a, m_g_out_b, m_w_mix_out, m_g_pre_f2, m_g_post_f2, m_w_f2_in, m_w_f2_out, v_w_ada, v_b_ada, v_g_pre_f1, v_g_post_f1, v_w_f1_in, v_w_f1_out, v_g_pre_m, v_g_post_m, v_w_mix_in, v_gmlp_norm_g, v_gmlp_norm_b, v_w_spatial, v_b_spatial, v_conv_w, v_conv_b, v_conv_norm_g, v_conv_norm_b, v_g_out_a, v_g_out_b, v_w_mix_out, v_g_pre_f2, v_g_post_f2, v_w_f2_in, v_w_f2_out):
    env = dict(locals())
    wts = {n: env[n] for n in WEIGHTS}
    mom = {n: env["m_" + n] for n in WEIGHTS}
    var = {n: env["v_" + n] for n in WEIGHTS}
    nb, s, _ = x.shape
    t = nb * s
    ax, ay, ac = lax.axis_index("x"), lax.axis_index("y"), lax.axis_index("c")
    j_chip = 2 * ax + ay
    dev = 4 * ax + 2 * ay + ac
    j_arr = j_chip.reshape(1).astype(jnp.int32)

    groups = (("w_f1_in", "w_f1_out"), ("w_mix_in", "w_mix_out"), ("w_f2_in", "w_f2_out"))
    def gather_operands(gi):
        srcs = [wts[n][0].astype(BF16) for n in groups[gi]] + ([conv_w[0]] if gi == 1 else [])
        lands = [lax.dynamic_update_index_in_dim(lax.empty((NCHIP,) + a.shape, a.dtype), a, j_chip, 0) for a in srcs]
        return srcs, lands

    def gather_start(gi, behind, operands=None):
        srcs, lands = operands or gather_operands(gi)
        plan_a, plan_b, n_b = _gather_plans([a.shape for a in srcs])
        ssem, rsem, srcs, lands, token = _split_start("gw_start%d" % gi, srcs, lands, plan_a, 3 * len(srcs), behind)
        gather[gi] = (srcs, lands, ssem, rsem, plan_a, plan_b, n_b)
        return token

    def gather_forward(gi, behind):
        srcs, lands, ssem, rsem, plan_a, plan_b, n_b = gather[gi]
        ssem, rsem, lands, token = _split_forward("gw_fwd%d" % gi, srcs, lands, ssem, rsem, plan_a, plan_b, n_b, behind)
        gather[gi] = (lands, ssem, rsem, plan_b)
        return token

    def gathered(gi, behind):
        lands, ssem, rsem, plan_b = gather[gi]
        return _split_wait("gw_wait%d" % gi, [], lands, ssem, rsem, plan_b, behind)

    gather = {}
    (c_all8,) = _allgather8("gather_c", [c.reshape(8, (nb * D) // 8)])
    token = gather_start(0, c_all8)
    c_all = c_all8.reshape(NDEV * nb, D) + token[0, 0]
    b_sh = lax.dynamic_slice(b_ada, (0, j_chip * ADA_SH), (1, ADA_SH))
    ada_sh = _ada_fwd(c_all, w_ada[0], b_sh)
    later = [gather_operands(1), gather_operands(2)]
    (ada4,) = _chip_allgather("gather_ada", [ada_sh], behind=[a for _, lands in later for a in lands])
    token = gather_forward(0, ada4)
    token = gather_start(1, token, later[0])
    token = gather_start(2, token, later[1])
    ada4 = ada4 + token[0:1, 0:1]
    ada_me = lax.dynamic_slice(ada4, (0, dev * nb, 0), (NCHIP, nb, ADA_SH))
    ada_me = jnp.transpose(ada_me, (1, 0, 2)).reshape(nb, NMOD * D)
    sh1, sc1, gt1, sh2, sc2, gt2, sh3, sc3, gt3 = [ada_me[:, k * D:(k + 1) * D].reshape(nb, 1, D) for k in range(NMOD)]

    wcat = jnp.transpose(w_spatial[0], (1, 0, 2)).reshape(CH, NH * CH)
    wcat_t = jnp.transpose(w_spatial[0], (0, 2, 1)).reshape(NH * CH, CH)
    bspt = jnp.repeat(b_spatial[0].T, HD, axis=1)

    w1i, w1o = gathered(0, sh1)
    w1o = w1o.reshape(DFF, D)
    x1, f1, p1 = _ffn_fwd(x, sh1, sc1, gt1, g_pre_f1, g_post_f1, w1i, w1o)
    wmi, wmo, cw4 = gathered(1, gather_forward(1, x1))
    wmo = wmo.reshape(D, D)
    cw_full = jnp.transpose(cw4, (1, 0, 2)).reshape(CK, WB)
    cw_pad = jnp.pad(cw_full, ((0, HALO - CK), (0, 0)))
    u, v, a, g = _mix_in_fwd(x1, sh2, sc2, g_pre_m, wmi)
    x2, conv, yb, m = _mix_mid_fwd(x1, u, v, a, g, gt2, gmlp_norm_g, gmlp_norm_b, wcat, bspt, cw_pad, conv_b,
                                   conv_norm_g, conv_norm_b, g_out_a, g_out_b, wmo, g_post_m)
    w2i, w2o = gathered(2, gather_forward(2, x2))
    w2o = w2o.reshape(DFF, D)
    dx3, df2, p2, lsum, dg_post_f2, dgt3 = _ffn_fwd(x2, sh3, sc3, gt3, g_pre_f2, g_post_f2, w2i, w2o, target=loss_target)

    def chip4(pair, rows):
        return [arr.reshape(NCHIP, rows, arr.shape[-1]) for arr in pair]

    def scatter_start(tag, pairs, behind):
        srcs = [p[1] for p in pairs]
        lands = [lax.empty((3,) + a.shape[1:], a.dtype) for a in srcs]
        ssem, rsem, srcs, lands, token = _split_start("gs_start_" + tag, srcs, lands, _scatter_plan(len(srcs)),
                                                      3 * len(srcs), behind)
        return (srcs, lands, ssem, rsem), token

    def scatter_wait(tag, state, behind):
        srcs, lands, ssem, rsem = state
        return _split_wait("gs_wait_" + tag, srcs, lands, ssem, rsem, _scatter_plan(len(srcs)), behind)

    def allgather_start(tag, arrs, behind):
        lands = [lax.dynamic_update_index_in_dim(lax.empty((NDEV,) + a.shape, a.dtype), a, dev, 0) for a in arrs]
        ssem, rsem, srcs, lands, token = _split_start("small_start_" + tag, arrs, lands, _allgather_plan(len(arrs)),
                                                      7 * len(arrs), behind)
        return (srcs, lands, ssem, rsem), token

    def allgather_wait(tag, state, behind):
        srcs, lands, ssem, rsem = state
        return _split_wait("small_wait_" + tag, srcs, lands, ssem, rsem, _allgather_plan(len(srcs)), behind)

    out = {}
    dx2, dp2, h3, a2, dg_pre_f2, dsh3, dsc3 = _ffn_bwd(
        dx3, x2, None, p2, sh3, sc3, gt3, g_pre_f2, g_post_f2, w2i, w2o, df=df2)
    gw2i = _wgrad("wgrad_f2_in", h3.reshape(t, D), dp2.reshape(t, 2 * DFF), 2 * DFF // NCHIP, True)
    gw2o = chip4(_wgrad("wgrad_f2_out", a2.reshape(t, DFF), df2.reshape(t, D), D // 2, False), DFF // NCHIP)
    scat_f2, tok = scatter_start("f2", [gw2i, gw2o], dg_post_f2)
    dy, dm, dg_post_m, dgt2 = _mix_out_bwd(dx2, m, gt2 + tok[0, 0], g_post_m, wmo)
    gwmo = chip4(_wgrad("wgrad_mix_out", yb.reshape(t, D), dm.reshape(t, D), D // 2, False), D // NCHIP)
    (du, dv, dconv, dwcat, dbsp, dgn_g, dgn_b, dgo_a, dgo_b, dcn_g, dcn_b, dcb) = _mix_mid_bwd(
        dy, u, v, conv, gmlp_norm_g, gmlp_norm_b, wcat, wcat_t, bspt, conv_norm_g, conv_norm_b, g_out_a, g_out_b)
    dx1, dproj, h2, dg_pre_m, dsh2, dsc2, dcw = _mix_in_bwd(dx2, x1, du, dv, dconv, a, g, sh2, sc2, g_pre_m, wmi, cw_pad)
    gwmi = _wgrad("wgrad_mix_in", h2.reshape(t, D), dproj.reshape(t, 4 * WA), WA, True)
    scat_mix, tok = scatter_start("mix", [gwmi, gwmo], dg_pre_m)

    vec_grads = dict(g_pre_m=dg_pre_m, g_post_m=dg_post_m, g_pre_f2=dg_pre_f2, g_post_f2=dg_post_f2)
    pair_grads = dict(gmlp_norm_g=dgn_g, gmlp_norm_b=dgn_b, conv_b=dcb, conv_norm_g=dcn_g, conv_norm_b=dcn_b,
                      g_out_a=dgo_a, g_out_b=dgo_b)
    pack = _pack_small([vec_grads[n] for n in VEC_ORDER[2:]], [pair_grads[n] for n in PAIR_ORDER], dbsp, dwcat, lsum)
    dada_early = jnp.concatenate([q.reshape(nb, D) for q in (dsh2, dsc2, dgt2, dsh3, dsc3, dgt3)], axis=1)
    early, tok2 = allgather_start("early", [pack, dcw, dada_early.reshape(8, (nb * 6 * D) // 8)], tok)
    grad_x, dp1, h1, a1, df1, dg_pre_f1, dg_post_f1, dsh1, dsc1, dgt1 = _ffn_bwd(
        dx1, x, f1, p1, sh1 + tok2[0, 0], sc1, gt1, g_pre_f1, g_post_f1, w1i, w1o)
    late_pack = _pack_late([dg_pre_f1, dg_post_f1] + [q.reshape(nb, D) for q in (dsh1, dsc1, dgt1)])
    late, tok2 = allgather_start("late", [late_pack], dg_post_f1)
    gw1i = _wgrad("wgrad_f1_in", h1.reshape(t, D), dp1.reshape(t, 2 * DFF), 2 * DFF // NCHIP, True)
    gw1o = chip4(_wgrad("wgrad_f1_out", a1.reshape(t, DFF), df1.reshape(t, D), D // 2, False), DFF // NCHIP)
    c_arr = ac.reshape(1).astype(jnp.int32)
    sib = _pair_exchange("pair_f1", [gw1i[1], gw1o[1]])
    pair_i = _pair_sum("pairsum_f1_in", gw1i[0], sib[0], c_arr)
    pair_o = _pair_sum("pairsum_f1_out", gw1o[0], sib[1], c_arr)
    scat_f1, tok = scatter_start("f1", [pair_i, pair_o], tok2)

    def reduce_and_update(tag, state, names, pairs, behind):
        recv = scatter_wait(tag, state, behind)
        part = [_sum4("sum4_" + n, pairs[k][0], recv[k], j_arr) for k, n in enumerate(names)]
        other = _sibling_swap("swap_" + tag, part)
        for k, n in enumerate(names):
            out[n] = tuple(r[None] for r in _adam_big("adam_" + n, wts[n][0], mom[n][0], var[n][0], part[k], other[k]))

    reduce_and_update("f2", scat_f2, ("w_f2_in", "w_f2_out"), [gw2i, gw2o], tok)
    reduce_and_update("mix", scat_mix, ("w_mix_in", "w_mix_out"), [gwmi, gwmo], out["w_f2_out"][3])

    pack_all, dcw_all, dada_early8 = allgather_wait("early", early, out["w_mix_out"][3])
    (late_all,) = allgather_wait("late", late, pack_all)
    dada_late = jnp.transpose(late_all[:, 2:8, :].reshape(NDEV, 3, nb, D), (0, 2, 1, 3)).reshape(NDEV * nb, 3 * D)
    dada_all = jnp.concatenate([dada_late, dada_early8.reshape(NDEV * nb, 6 * D)], axis=1)
    dcw_mine = lax.dynamic_slice(dcw_all, (0, 0, j_chip * (WB // NCHIP)), (NDEV, HALO, WB // NCHIP))
    small = {n: (wts[n], mom[n], var[n]) for n in list(VEC_ORDER) + list(PAIR_ORDER) + ["b_spatial", "w_spatial", "conv_w", "b_ada"]}
    small_out, loss = _small_adam(pack_all, late_all, dcw_mine, dada_all, small)
    out.update(small_out)
    dada_sh = lax.dynamic_slice(dada_all, (0, j_chip * ADA_SH), (NDEV * nb, ADA_SH))
    out["w_ada"] = tuple(r[None] for r in _ada_bwd_adam(c_all, dada_sh, w_ada[0], m_w_ada[0], v_w_ada[0]))
    recv = scatter_wait("f1", scat_f1, out["w_ada"][3])
    names = ("w_f1_in", "w_f1_out")
    mine = [_sum4("sum4_" + n, p[0], recv[k], j_arr)
            for k, (n, p) in enumerate(zip(names, (pair_i, pair_o)))]
    theirs = _sibling_swap("swap_f1", mine)
    for k, n in enumerate(names):
        out[n] = tuple(r[None] for r in _adam_halves("adam_" + n, wts[n][0], mom[n][0], var[n][0], mine[k], theirs[k],
                                                     c_arr))

    res = [loss, grad_x]
    for k in range(4):
        res += [out[n][k] for n in WEIGHTS]
    return tuple(res)
```

```python
import functools

import jax
import jax.numpy as jnp
from jax import lax
from jax.experimental import pallas as pl
from jax.experimental.pallas import tpu as pltpu

D = 1024
DFF = 2816
WA = 512
WB = 512
NH = 8
HD = 64
CH = 128
CK = 31
HALO = 32
NMOD = 9
EPS = 1e-6
NCHIP = 4
NDEV = 8
FBLK = DFF // 2
ADA_SH = NMOD * D // NCHIP

LR, B1, B2, EPS_A, WD, STEP = 0.001, 0.9, 0.999, 1e-08, 0.01, 10

F32 = jnp.float32
BF16 = jnp.bfloat16
MESH = pl.DeviceIdType.MESH
ANY = pl.BlockSpec(memory_space=pl.ANY)
VMEM_FULL = pl.BlockSpec(memory_space=pltpu.VMEM)
VMEM_LIMIT = 56 * 1024 * 1024

NT = (((1,), (1,)), ((), ()))
TN = (((0,), (0,)), ((), ()))


def _dot(a, b):
    return jnp.dot(a, b, preferred_element_type=F32)


def _dot_nt(a, b):
    return lax.dot_general(a, b, NT, preferred_element_type=F32)


def _dot_tn(a, b):
    return lax.dot_general(a, b, TN, preferred_element_type=F32)


def _cparams():
    return pltpu.CompilerParams(vmem_limit_bytes=VMEM_LIMIT)


def _allgather8(name, arrs):
    n = len(arrs)

    def body(*refs):
        ins, outs = refs[:n], refs[n:2 * n]
        send_sems, recv_sems, local_sems = refs[2 * n:]
        x, y, c = lax.axis_index("x"), lax.axis_index("y"), lax.axis_index("c")
        me, sibling = (x, y, c), (x, y, 1 - c)
        chips = [(1 - x, y), (x, 1 - y), (1 - x, 1 - y)]

        def copy(a, k, block, to, src=None):
            rows = outs[a].at[4 * block[0] + 2 * block[1] + block[2]]
            return pltpu.make_async_remote_copy(
                src_ref=rows if src is None else src, dst_ref=rows,
                send_sem=send_sems.at[a, k], recv_sem=recv_sems.at[a, k],
                device_id=to, device_id_type=MESH)

        started, mine = [], []
        for a in range(n):
            loc = pltpu.make_async_copy(ins[a], outs[a].at[4 * x + 2 * y + c], local_sems.at[a])
            loc.start()
            mine.append(loc)
            first = [copy(a, 0, me, sibling, src=ins[a])]
            first += [copy(a, 1 + j, me, (*chip, c), src=ins[a]) for j, chip in enumerate(chips)]
            for cp in first:
                cp.start()
            started += first
        for a in range(n):
            for j, chip in enumerate(chips):
                copy(a, 1 + j, (*chip, c), me).wait_recv()
                fwd = copy(a, 4 + j, (*chip, c), sibling)
                fwd.start()
                started.append(fwd)
        for a in range(n):
            copy(a, 0, sibling, me).wait_recv()
            for j, chip in enumerate(chips):
                copy(a, 4 + j, (*chip, 1 - c), me).wait_recv()
        for cp in started:
            cp.wait_send()
        for loc in mine:
            loc.wait()

    return pl.pallas_call(
        body, name=name,
        out_shape=[jax.ShapeDtypeStruct((NDEV,) + a.shape, a.dtype) for a in arrs],
        in_specs=[ANY] * n, out_specs=[ANY] * n,
        scratch_shapes=[pltpu.SemaphoreType.DMA((n, 7)), pltpu.SemaphoreType.DMA((n, 7)),
                        pltpu.SemaphoreType.DMA((n,))],
    )(*arrs)


def _chip_relations(x, y):
    return [(1 - x, y), (x, 1 - y), (1 - x, 1 - y)]


def _exchange(name, arrs, out_shapes, plan):
    n = len(arrs)
    n_out = len(out_shapes)

    def body(*refs):
        ins, outs = refs[:n], refs[n:n + n_out]
        send_sems, recv_sems, local_sems = refs[n + n_out:]
        x, y, c = lax.axis_index("x"), lax.axis_index("y"), lax.axis_index("c")
        local, sends = plan(x, y, c, ins, outs)
        locs = [pltpu.make_async_copy(s, d, local_sems.at[i]) for i, (s, d) in enumerate(local)]
        for loc in locs:
            loc.start()
        cps = [pltpu.make_async_remote_copy(src_ref=s, dst_ref=d, send_sem=send_sems.at[i], recv_sem=recv_sems.at[i],
                                            device_id=peer, device_id_type=MESH)
               for i, (s, d, peer, _) in enumerate(sends)]
        for cp in cps:
            cp.start()
        for i, (s, _, peer, landing) in enumerate(sends):
            pltpu.make_async_remote_copy(src_ref=s, dst_ref=landing, send_sem=send_sems.at[i], recv_sem=recv_sems.at[i],
                                         device_id=peer, device_id_type=MESH).wait_recv()
        for cp in cps:
            cp.wait_send()
        for loc in locs:
            loc.wait()

    return n, n_out, body


def _run_exchange(name, arrs, out_shapes, plan, n_local, n_send):
    n, n_out, body = _exchange(name, arrs, out_shapes, plan)
    return pl.pallas_call(
        body, name=name, out_shape=out_shapes,
        in_specs=[ANY] * n, out_specs=[ANY] * n_out,
        scratch_shapes=[pltpu.SemaphoreType.DMA((n_send,)), pltpu.SemaphoreType.DMA((n_send,)),
                        pltpu.SemaphoreType.DMA((max(n_local, 1),))],
    )(*arrs)


def _chip_allgather(name, arrs, behind=()):
    n = len(arrs)

    def plan(x, y, c, ins, outs):
        j_me = 2 * x + y
        local = [(ins[a], outs[a].at[j_me]) for a in range(n)]
        sends = []
        for a in range(n):
            for (px, py) in _chip_relations(x, y):
                sends.append((ins[a], outs[a].at[j_me], (px, py, c), outs[a].at[2 * px + py]))
        return local, sends

    shapes = [jax.ShapeDtypeStruct((NCHIP,) + a.shape, a.dtype) for a in arrs]
    return _run_exchange(name, list(arrs) + list(behind), shapes, plan, n, 3 * n)


def _chip_scatter(name, arrs):
    n = len(arrs)

    def plan(x, y, c, ins, outs):
        sends = []
        for a in range(n):
            for k, (px, py) in enumerate(_chip_relations(x, y)):
                sends.append((ins[a].at[2 * px + py], outs[a].at[k], (px, py, c), outs[a].at[k]))
        return [], sends

    shapes = [jax.ShapeDtypeStruct((3,) + a.shape[1:], a.dtype) for a in arrs]
    return _run_exchange(name, arrs, shapes, plan, 0, 3 * n)


def _sibling_swap(name, arrs, behind=None):
    n = len(arrs)

    def plan(x, y, c, ins, outs):
        return [], [(ins[a], outs[a], (x, y, 1 - c), outs[a]) for a in range(n)]

    shapes = [jax.ShapeDtypeStruct(a.shape, a.dtype) for a in arrs]
    return _run_exchange(name, list(arrs) + ([] if behind is None else [behind]), shapes, plan, 0, n)


HBM = pl.BlockSpec(memory_space=pltpu.HBM)
SEM = pl.BlockSpec(memory_space=pltpu.SEMAPHORE)
EFFECT = pltpu.SideEffectType.DATAFLOW_SIDE_EFFECTING


def _split_start(name, srcs, lands, plan, n_send, after):
    n, nl = len(srcs), len(lands)

    def body(*refs):
        src, land = refs[:n], refs[n:n + nl]
        send_sems, recv_sems = refs[n + nl + 1], refs[n + nl + 2]
        token = refs[-2]
        local_sems = refs[-1]
        x, y, c = lax.axis_index("x"), lax.axis_index("y"), lax.axis_index("c")
        local, sends = plan(x, y, c, src, land)
        locs = [pltpu.make_async_copy(s, d, local_sems.at[i]) for i, (s, d) in enumerate(local)]
        for loc in locs:
            loc.start()
        for loc in locs:
            loc.wait()
        for i, (s, d, peer, _) in enumerate(sends):
            pltpu.make_async_remote_copy(src_ref=s, dst_ref=d, send_sem=send_sems.at[i], recv_sem=recv_sems.at[i],
                                         device_id=peer, device_id_type=MESH).start()
        token[...] = jnp.zeros_like(token)

    thru = [pltpu.HBM(a.shape, a.dtype) for a in list(srcs) + list(lands)]
    res = pl.pallas_call(
        body, name=name,
        out_shape=(pltpu.SemaphoreType.DMA((n_send,)), pltpu.SemaphoreType.DMA((n_send,)), *thru,
                   jax.ShapeDtypeStruct((8, 128), F32)),
        in_specs=[HBM] * (n + nl) + [ANY],
        out_specs=(SEM, SEM, *([HBM] * (n + nl)), pl.BlockSpec(memory_space=pltpu.VMEM)),
        input_output_aliases={i: 2 + i for i in range(n + nl)},
        scratch_shapes=[pltpu.SemaphoreType.DMA((max(len(srcs), 1),))],
        compiler_params=pltpu.CompilerParams(has_side_effects=EFFECT),
    )(*[pltpu.with_memory_space_constraint(a, pltpu.HBM) for a in list(srcs) + list(lands)], after)
    return res[0], res[1], list(res[2:2 + n]), list(res[2 + n:2 + n + nl]), res[-1]


def _split_wait(name, srcs, lands, send_sems, recv_sems, plan, after):
    n, nl = len(srcs), len(lands)

    def body(*refs):
        src, land = refs[:n], refs[n:n + nl]
        send_sems, recv_sems = refs[n + nl], refs[n + nl + 1]
        x, y, c = lax.axis_index("x"), lax.axis_index("y"), lax.axis_index("c")
        _, sends = plan(x, y, c, src, land)
        for i, (s, _, peer, landing) in enumerate(sends):
            cp = pltpu.make_async_remote_copy(src_ref=s, dst_ref=landing, send_sem=send_sems.at[i],
                                              recv_sem=recv_sems.at[i], device_id=peer, device_id_type=MESH)
            cp.wait_send()
            cp.wait_recv()

    thru = [pltpu.HBM(a.shape, a.dtype) for a in list(srcs) + list(lands)]
    res = pl.pallas_call(
        body, name=name, out_shape=tuple(thru),
        in_specs=[HBM] * (n + nl) + [SEM, SEM, ANY], out_specs=tuple([HBM] * (n + nl)),
        input_output_aliases={i: i for i in range(n + nl)},
        compiler_params=pltpu.CompilerParams(has_side_effects=EFFECT),
    )(*srcs, *lands, send_sems, recv_sems, after)
    return list(res[n:])


def _split_forward(name, srcs, lands, send_a, recv_a, plan_a, plan_b, n_b, after):
    n, nl = len(srcs), len(lands)

    def body(*refs):
        src, land = refs[:n], refs[n:n + nl]
        send_a, recv_a = refs[n + nl], refs[n + nl + 1]
        send_b, recv_b = refs[n + nl + 3], refs[n + nl + 4]
        token = refs[-1]
        x, y, c = lax.axis_index("x"), lax.axis_index("y"), lax.axis_index("c")
        _, first = plan_a(x, y, c, src, land)
        for i, (s, _, peer, landing) in enumerate(first):
            cp = pltpu.make_async_remote_copy(src_ref=s, dst_ref=landing, send_sem=send_a.at[i],
                                              recv_sem=recv_a.at[i], device_id=peer, device_id_type=MESH)
            cp.wait_send()
            cp.wait_recv()
        _, second = plan_b(x, y, c, src, land)
        for i, (s, d, peer, _) in enumerate(second):
            pltpu.make_async_remote_copy(src_ref=s, dst_ref=d, send_sem=send_b.at[i], recv_sem=recv_b.at[i],
                                         device_id=peer, device_id_type=MESH).start()
        token[...] = jnp.zeros_like(token)

    thru = [pltpu.HBM(a.shape, a.dtype) for a in lands]
    res = pl.pallas_call(
        body, name=name,
        out_shape=(pltpu.SemaphoreType.DMA((n_b,)), pltpu.SemaphoreType.DMA((n_b,)), *thru,
                   jax.ShapeDtypeStruct((8, 128), F32)),
        in_specs=[HBM] * (n + nl) + [SEM, SEM, ANY],
        out_specs=(SEM, SEM, *([HBM] * nl), pl.BlockSpec(memory_space=pltpu.VMEM)),
        input_output_aliases={n + i: 2 + i for i in range(nl)},
        compiler_params=pltpu.CompilerParams(has_side_effects=EFFECT),
    )(*srcs, *lands, send_a, recv_a, after)
    return res[0], res[1], list(res[2:2 + nl]), res[-1]


def _gather_plans(shapes):
    n = len(shapes)

    def halves(a, c):
        rows = shapes[a][0] // 2
        return pl.ds(pl.multiple_of(c * rows, 16), rows), pl.ds(pl.multiple_of((1 - c) * rows, 16), rows)

    def split(a):
        return shapes[a][0] % 32 == 0

    def plan_a(x, y, c, src, land):
        j_me = 2 * x + y
        sends = []
        for a in range(n):
            for (px, py) in _chip_relations(x, y):
                if split(a):
                    mine, _ = halves(a, c)
                    sends.append((src[a].at[mine], land[a].at[j_me, mine], (px, py, c), land[a].at[2 * px + py, mine]))
                else:
                    sends.append((src[a], land[a].at[j_me], (px, py, c), land[a].at[2 * px + py]))
        return [], sends

    def plan_b(x, y, c, src, land):
        sends = []
        for a in range(n):
            if split(a):
                mine, other = halves(a, c)
                for (px, py) in _chip_relations(x, y):
                    j = 2 * px + py
                    sends.append((land[a].at[j, mine], land[a].at[j, mine], (x, y, 1 - c), land[a].at[j, other]))
        return [], sends

    n_b = 3 * sum(1 for a in range(n) if split(a))
    return plan_a, plan_b, n_b


def _allgather_plan(n):
    flips = [(dx, dy, dc) for dx in (0, 1) for dy in (0, 1) for dc in (0, 1) if dx or dy or dc]

    def plan(x, y, c, src, land):
        sends = []
        for a in range(n):
            for dx, dy, dc in flips:
                px, py, pc = x ^ dx, y ^ dy, c ^ dc
                sends.append((src[a], land[a].at[4 * x + 2 * y + c], (px, py, pc), land[a].at[4 * px + 2 * py + pc]))
        return [], sends

    return plan


def _scatter_plan(n):
    def plan(x, y, c, src, land):
        sends = []
        for a in range(n):
            for k, (px, py) in enumerate(_chip_relations(x, y)):
                sends.append((src[a].at[2 * px + py], land[a].at[k], (px, py, c), land[a].at[k]))
        return [], sends

    return plan


def _rms(x):
    r = lax.rsqrt(jnp.mean(x * x, axis=-1, keepdims=True) + EPS)
    return x * r, r


def _rms_bwd(dy, n, r, g):
    dg = jnp.sum(dy * n, axis=0, keepdims=True)
    dn = dy * g
    dx = r * (dn - n * jnp.mean(dn * n, axis=-1, keepdims=True))
    return dx, dg


def _ln(x):
    mu = jnp.mean(x, axis=-1, keepdims=True)
    xc = x - mu
    rstd = lax.rsqrt(jnp.mean(xc * xc, axis=-1, keepdims=True) + EPS)
    return xc * rstd, rstd


def _ln_bwd(dy, xhat, rstd, g):
    dg = jnp.sum(dy * xhat, axis=0, keepdims=True)
    db = jnp.sum(dy, axis=0, keepdims=True)
    dxh = dy * g
    dx = rstd * (dxh - jnp.mean(dxh, axis=-1, keepdims=True) - xhat * jnp.mean(dxh * xhat, axis=-1, keepdims=True))
    return dx, dg, db


def _sigmoid(x):
    return jax.nn.sigmoid(x)


def _dsilu(x, s):
    return s * (1.0 + x * (1.0 - s))


def _adam(w, g, m, v):
    m = B1 * m + (1.0 - B1) * g
    v = B2 * v + (1.0 - B2) * (g * g)
    m_hat = m / (1.0 - B1 ** STEP)
    v_hat = v / (1.0 - B2 ** STEP)
    delta = -LR * (m_hat / (jnp.sqrt(v_hat) + EPS_A) + WD * w)
    return delta, m, v


def _head_mask(shape):
    lane = lax.broadcasted_iota(jnp.int32, shape, len(shape) - 1)
    return [(lane >= h * HD) & (lane < (h + 1) * HD) for h in range(NH)]


def _first(b, i):
    return jnp.logical_and(b == 0, i == 0)


def _acc(ref, val, first):
    @pl.when(first)
    def _():
        ref[...] = val

    @pl.when(jnp.logical_not(first))
    def _():
        ref[...] += val


def _ada_fwd(c_all, w_sh, b_sh):
    nb = c_all.shape[0]
    tn = 768

    def body(c_ref, w_ref, b_ref, o_ref):
        cv = c_ref[...]
        cs = (cv * _sigmoid(cv)).astype(BF16)
        o_ref[...] = _dot(cs, w_ref[...].astype(BF16)) + b_ref[...]

    return pl.pallas_call(
        body, name="ada_fwd", grid=(ADA_SH // tn,),
        out_shape=jax.ShapeDtypeStruct((nb, ADA_SH), F32),
        in_specs=[pl.BlockSpec((nb, D), lambda j: (0, 0)), pl.BlockSpec((D, tn), lambda j: (0, j)),
                  pl.BlockSpec((1, tn), lambda j: (0, j))],
        out_specs=pl.BlockSpec((nb, tn), lambda j: (0, j)),
        compiler_params=_cparams(),
    )(c_all, w_sh, b_sh)


def _ada_bwd_adam(c_all, dada_sh, w, m, v):
    nb = c_all.shape[0]
    tn = 128

    def body(c_ref, d_ref, w_ref, m_ref, v_ref, g_out, d_out, m_out, v_out):
        cv = c_ref[...]
        cs = (cv * _sigmoid(cv)).astype(BF16)
        g = _dot_tn(cs, d_ref[...].astype(BF16))
        delta, m2, v2 = _adam(w_ref[...], g, m_ref[...], v_ref[...])
        g_out[...] = g
        d_out[...] = delta
        m_out[...] = m2
        v_out[...] = v2

    big = pl.BlockSpec((D, tn), lambda j: (0, j))
    shape = jax.ShapeDtypeStruct((D, ADA_SH), F32)
    return pl.pallas_call(
        body, name="ada_bwd_adam", grid=(ADA_SH // tn,),
        out_shape=[shape] * 4,
        in_specs=[pl.BlockSpec((nb, D), lambda j: (0, 0)), pl.BlockSpec((nb, tn), lambda j: (0, j)), big, big, big],
        out_specs=[big] * 4,
        compiler_params=_cparams(),
    )(c_all, dada_sh, w, m, v)


def _tok_specs(tm, width):
    return pl.BlockSpec((1, tm, width), lambda b, i: (b, i, 0))


def _mod_spec():
    return pl.BlockSpec((1, 1, D), lambda b, i: (b, 0, 0))


def _row_spec(width=D):
    return pl.BlockSpec((1, width), lambda b, i: (0, 0))


def _ffn_fwd(x, sh, sc, gt, g_pre, g_post, w_in4, w_out, target=None):
    nb, s, _ = x.shape
    tm = min(512, s)
    with_loss = target is not None

    def body(*refs):
        if with_loss:
            (x_ref, sh_ref, sc_ref, gt_ref, gpre_ref, gpost_ref, win_ref, wout_ref, tgt_ref,
             xo_ref, df_ref, p_ref, ls_ref, dgpost_ref, dgt_ref) = refs
        else:
            (x_ref, sh_ref, sc_ref, gt_ref, gpre_ref, gpost_ref, win_ref, wout_ref,
             xo_ref, f_ref, p_ref) = refs
        xv = x_ref[0]
        n, _ = _rms(xv)
        h = (n * gpre_ref[...]) * (1.0 + sc_ref[0]) + sh_ref[0]
        hb = h.astype(BF16)
        acc = jnp.zeros((tm, D), F32)
        for j in range(2):
            gate = _dot(hb, win_ref[j])
            up = _dot(hb, win_ref[2 + j])
            p_ref[0, :, j * FBLK:(j + 1) * FBLK] = gate.astype(BF16)
            p_ref[0, :, DFF + j * FBLK:DFF + (j + 1) * FBLK] = up.astype(BF16)
            a = (gate * _sigmoid(gate)) * up
            acc = acc + _dot(a.astype(BF16), wout_ref[j * FBLK:(j + 1) * FBLK, :])
        nf, q = _rms(acc)
        gpost = gpost_ref[...]
        half_gate = 0.5 * gt_ref[0]
        out = xv + half_gate * (nf * gpost)
        if with_loss:
            first = _first(pl.program_id(0), pl.program_id(1))
            err = out - tgt_ref[0]
            dout = err * (1.0 / D)
            xo_ref[0] = dout
            row = jnp.sum(err * err, axis=0, keepdims=True)
            part = row[:, 0:128]
            for k in range(1, D // 128):
                part = part + row[:, k * 128:(k + 1) * 128]
            _acc(ls_ref, part, first)
            df, dgpost = _rms_bwd(dout * half_gate, nf, q, gpost)
            df_ref[0] = df.astype(BF16)
            _acc(dgpost_ref, dgpost, first)
            _acc(dgt_ref, jnp.sum(dout * (0.5 * (nf * gpost)), axis=0, keepdims=True)[None], pl.program_id(1) == 0)
        else:
            f_ref[0] = acc
            xo_ref[0] = out

    in_specs = [_tok_specs(tm, D), _mod_spec(), _mod_spec(), _mod_spec(), _row_spec(), _row_spec(), VMEM_FULL, VMEM_FULL]
    args = [x, sh, sc, gt, g_pre, g_post, w_in4, w_out]
    out_shape = [jax.ShapeDtypeStruct((nb, s, D), F32), jax.ShapeDtypeStruct((nb, s, D), BF16 if with_loss else F32),
                 jax.ShapeDtypeStruct((nb, s, 2 * DFF), BF16)]
    out_specs = [_tok_specs(tm, D), _tok_specs(tm, D), _tok_specs(tm, 2 * DFF)]
    if with_loss:
        in_specs.append(_tok_specs(tm, D))
        args.append(target)
        out_shape += [jax.ShapeDtypeStruct((1, 128), F32), jax.ShapeDtypeStruct((1, D), F32),
                      jax.ShapeDtypeStruct((nb, 1, D), F32)]
        out_specs += [pl.BlockSpec((1, 128), lambda b, i: (0, 0)), _row_spec(), _mod_spec()]
    return pl.pallas_call(
        body, name="ffn_loss_fwd" if with_loss else "ffn_fwd", grid=(nb, s // tm),
        out_shape=out_shape, in_specs=in_specs, out_specs=out_specs,
        compiler_params=_cparams(),
    )(*args)


def _ffn_bwd(dxo, x, f, p, sh, sc, gt, g_pre, g_post, w_in4, w_out, df=None):
    nb, s, _ = x.shape
    tm = min(256, s)
    given = df is not None

    def body(*refs):
        if given:
            (dxo_ref, x_ref, dfin_ref, p_ref, sh_ref, sc_ref, gpre_ref, win_ref, wout_ref,
             dx_ref, dp_ref, h_ref, a_ref, dgpre_ref, dsh_ref, dsc_ref) = refs
        else:
            (dxo_ref, x_ref, f_ref, p_ref, sh_ref, sc_ref, gt_ref, gpre_ref, gpost_ref, win_ref, wout_ref,
             dx_ref, dp_ref, h_ref, a_ref, df_ref, dgpre_ref, dgpost_ref, dsh_ref, dsc_ref, dgt_ref) = refs
        b, i = pl.program_id(0), pl.program_id(1)
        dxo_v = dxo_ref[0]
        if given:
            dfb = dfin_ref[0]
        else:
            nf, q = _rms(f_ref[0])
            gpost = gpost_ref[...]
            dgt = jnp.sum(dxo_v * (0.5 * (nf * gpost)), axis=0, keepdims=True)
            do = dxo_v * (0.5 * gt_ref[0])
            dfv, dgpost = _rms_bwd(do, nf, q, gpost)
            dfb = dfv.astype(BF16)
            df_ref[0] = dfb
        xv = x_ref[0]
        n, r = _rms(xv)
        gpre = gpre_ref[...]
        ng = n * gpre
        scale1 = 1.0 + sc_ref[0]
        h = ng * scale1 + sh_ref[0]
        h_ref[0] = h.astype(BF16)
        dh = jnp.zeros((tm, D), F32)
        for j in range(2):
            gate = p_ref[0, :, j * FBLK:(j + 1) * FBLK].astype(F32)
            up = p_ref[0, :, DFF + j * FBLK:DFF + (j + 1) * FBLK].astype(F32)
            sg = _sigmoid(gate)
            act = gate * sg
            a_ref[0, :, j * FBLK:(j + 1) * FBLK] = (act * up).astype(BF16)
            da = _dot_nt(dfb, wout_ref[j * FBLK:(j + 1) * FBLK, :])
            dgate = (da * up * _dsilu(gate, sg)).astype(BF16)
            dup = (da * act).astype(BF16)
            dp_ref[0, :, j * FBLK:(j + 1) * FBLK] = dgate
            dp_ref[0, :, DFF + j * FBLK:DFF + (j + 1) * FBLK] = dup
            dh = dh + _dot_nt(dgate, win_ref[j]) + _dot_nt(dup, win_ref[2 + j])
        dsh = jnp.sum(dh, axis=0, keepdims=True)
        dsc = jnp.sum(dh * ng, axis=0, keepdims=True)
        dxn, dgpre = _rms_bwd(dh * scale1, n, r, gpre)
        dx_ref[0] = dxo_v + dxn
        _acc(dgpre_ref, dgpre, _first(b, i))
        _acc(dsh_ref, dsh[None], i == 0)
        _acc(dsc_ref, dsc[None], i == 0)
        if not given:
            _acc(dgpost_ref, dgpost, _first(b, i))
            _acc(dgt_ref, dgt[None], i == 0)

    tok = _tok_specs(tm, D)
    mod_shape = jax.ShapeDtypeStruct((nb, 1, D), F32)
    row_shape = jax.ShapeDtypeStruct((1, D), F32)
    big = [jax.ShapeDtypeStruct((nb, s, D), F32), jax.ShapeDtypeStruct((nb, s, 2 * DFF), BF16),
           jax.ShapeDtypeStruct((nb, s, D), BF16), jax.ShapeDtypeStruct((nb, s, DFF), BF16)]
    big_specs = [tok, _tok_specs(tm, 2 * DFF), tok, _tok_specs(tm, DFF)]
    if given:
        return pl.pallas_call(
            body, name="ffn_bwd_after_loss", grid=(nb, s // tm),
            out_shape=big + [row_shape, mod_shape, mod_shape],
            in_specs=[tok, tok, tok, _tok_specs(tm, 2 * DFF), _mod_spec(), _mod_spec(), _row_spec(), VMEM_FULL, VMEM_FULL],
            out_specs=big_specs + [_row_spec(), _mod_spec(), _mod_spec()],
            compiler_params=_cparams(),
        )(dxo, x, df, p, sh, sc, g_pre, w_in4, w_out)
    return pl.pallas_call(
        body, name="ffn_bwd", grid=(nb, s // tm),
        out_shape=big + [jax.ShapeDtypeStruct((nb, s, D), BF16), row_shape, row_shape, mod_shape, mod_shape, mod_shape],
        in_specs=[tok, tok, tok, _tok_specs(tm, 2 * DFF), _mod_spec(), _mod_spec(), _mod_spec(), _row_spec(), _row_spec(),
                  VMEM_FULL, VMEM_FULL],
        out_specs=big_specs + [tok, _row_spec(), _row_spec(), _mod_spec(), _mod_spec(), _mod_spec()],
        compiler_params=_cparams(),
    )(dxo, x, f, p, sh, sc, gt, g_pre, g_post, w_in4, w_out)


def _wgrad(name, a, b, col_block, chip_major):
    t, ka = a.shape
    n = b.shape[1]
    tk = min(t, 512)
    while tk * 2 <= t and t % (tk * 2) == 0 and 2 * (tk * 2) * max(ka, col_block) <= 6 * 1024 * 1024:
        tk *= 2
    nk = t // tk
    nblk = n // col_block

    def body(a_ref, b_ref, o_ref, obf_ref, acc_ref):
        k = pl.program_id(1)

        @pl.when(k == 0)
        def _():
            acc_ref[...] = jnp.zeros_like(acc_ref)

        acc_ref[...] += _dot_tn(a_ref[...], b_ref[...])

        @pl.when(k == nk - 1)
        def _():
            val = acc_ref[...]
            if chip_major:
                o_ref[0] = val
                obf_ref[0] = val.astype(BF16)
            else:
                o_ref[...] = val
                obf_ref[...] = val.astype(BF16)

    if chip_major:
        shape = (nblk, ka, col_block)
        ospec = pl.BlockSpec((1, ka, col_block), lambda j, k: (j, 0, 0))
    else:
        shape = (ka, n)
        ospec = pl.BlockSpec((ka, col_block), lambda j, k: (0, j))
    return pl.pallas_call(
        body, name=name, grid=(nblk, nk),
        out_shape=[jax.ShapeDtypeStruct(shape, F32), jax.ShapeDtypeStruct(shape, BF16)],
        in_specs=[pl.BlockSpec((tk, ka), lambda j, k: (k, 0)), pl.BlockSpec((tk, col_block), lambda j, k: (k, j))],
        out_specs=[ospec, ospec],
        scratch_shapes=[pltpu.VMEM((ka, col_block), F32)],
        compiler_params=_cparams(),
    )(a, b)


def _mix_in_fwd(x, sh, sc, g_pre, w_mi4):
    nb, s, _ = x.shape
    tm = min(512, s)

    def body(x_ref, sh_ref, sc_ref, gpre_ref, w_ref, u_ref, v_ref, a_ref, g_ref):
        n, _ = _rms(x_ref[0])
        hb = ((n * gpre_ref[...]) * (1.0 + sc_ref[0]) + sh_ref[0]).astype(BF16)
        for k, o_ref in enumerate((u_ref, v_ref, a_ref, g_ref)):
            o_ref[0] = _dot(hb, w_ref[k])

    shape = jax.ShapeDtypeStruct((nb, s, WA), F32)
    return pl.pallas_call(
        body, name="mix_in_fwd", grid=(nb, s // tm),
        out_shape=[shape] * 4,
        in_specs=[_tok_specs(tm, D), _mod_spec(), _mod_spec(), _row_spec(), VMEM_FULL],
        out_specs=[_tok_specs(tm, WA)] * 4,
        compiler_params=_cparams(),
    )(x, sh, sc, g_pre, w_mi4)


def _spatial_weights(wcat_ref, transposed):
    w = wcat_ref[...]
    row = lax.broadcasted_iota(jnp.int32, w.shape, 0)
    col = lax.broadcasted_iota(jnp.int32, w.shape, 1)
    keep = ((row & (CH - 1)) <= col) if transposed else ((col & (CH - 1)) <= row)
    return jnp.where(keep, w, 0.0).astype(BF16)


def _expand_heads(vc, masks):
    return jnp.concatenate([jnp.where(mk, vc, jnp.zeros_like(vc)) for mk in masks], axis=0)


def _spatial_bias(bspt_ref):
    return bspt_ref[...]


SHIFTS = 8
TAP_ROWS = 32


def _ext_rows(tm):
    return tm + HALO + SHIFTS


def _make_shifts(ext_ref, sh_ref, tm):
    ext_ref[tm + HALO:tm + HALO + SHIFTS, :] = jnp.zeros((SHIFTS, WB), F32)
    for r in range(SHIFTS):
        sh_ref[r] = ext_ref[r:r + tm + HALO, :]


def _conv_taps(sh_ref, w_ref, tm, taps, emit):
    def block(i, carry):
        r0 = pl.multiple_of(i * TAP_ROWS, TAP_ROWS)
        acc = jnp.zeros((TAP_ROWS, WB), F32)
        for o, k in taps:
            acc = acc + w_ref[k:k + 1, :] * sh_ref[o % SHIFTS, pl.ds(r0 + SHIFTS * (o // SHIFTS), TAP_ROWS), :]
        emit(r0, acc)
        return carry

    lax.fori_loop(0, tm // TAP_ROWS, block, 0)


def _halo_prev_spec(tm):
    return pl.BlockSpec((1, HALO, WB), lambda b, i: (b, jnp.maximum(i * (tm // HALO) - 1, 0), 0))


def _halo_next_spec(tm, s):
    return pl.BlockSpec((1, HALO, WB), lambda b, i: (b, jnp.minimum((i + 1) * (tm // HALO), s // HALO - 1), 0))


def _mix_mid_fwd(x, u, v, a, g, gt, gn_g, gn_b, wcat, bspt, conv_w, conv_b, cn_g, cn_b, go_a, go_b, w_mo, g_post):
    nb, s, _ = x.shape
    tm = min(512, s)

    def body(x_ref, u_ref, v_ref, a_ref, g_ref, ah_ref, gh_ref, gt_ref, gng_ref, gnb_ref, wcat_ref, bspt_ref,
             cw_ref, cb_ref, cng_ref, cnb_ref, goa_ref, gob_ref, wmo_ref, gpost_ref,
             xo_ref, conv_ref, y_ref, m_ref, ext_ref, sh_ref):
        i = pl.program_id(1)
        xhat, _ = _ln(v_ref[0])
        vb = (xhat * gng_ref[...] + gnb_ref[...]).astype(BF16)
        wsb = _spatial_weights(wcat_ref, False)
        bias = _spatial_bias(bspt_ref)
        masks = _head_mask((CH, WA))
        zs = []
        for cidx in range(tm // CH):
            vexp = _expand_heads(vb[cidx * CH:(cidx + 1) * CH, :], masks)
            zs.append(_dot(wsb, vexp) + bias)
        z = jnp.concatenate(zs, axis=0)
        na, _ = _rms(u_ref[0] * z)
        keep = jnp.where(i == 0, 0.0, 1.0).astype(F32)
        ext_ref[0:HALO, :] = (ah_ref[0] * _sigmoid(gh_ref[0])) * keep
        ext_ref[HALO:HALO + tm, :] = a_ref[0] * _sigmoid(g_ref[0])
        _make_shifts(ext_ref, sh_ref, tm)
        cb = cb_ref[...]

        def put_conv(r0, acc):
            conv_ref[0, pl.ds(r0, TAP_ROWS), :] = acc + cb

        _conv_taps(sh_ref, cw_ref, tm, [(k + HALO - (CK - 1), k) for k in range(CK)], put_conv)
        conv = conv_ref[0]
        chat, _ = _ln(conv)
        cln = chat * cng_ref[...] + cnb_ref[...]
        nbb, _ = _rms(cln * _sigmoid(cln))
        yb = jnp.concatenate([na * goa_ref[...], nbb * gob_ref[...]], axis=1).astype(BF16)
        y_ref[0] = yb
        m = _dot(yb, wmo_ref[...])
        m_ref[0] = m
        nm, _ = _rms(m)
        xo_ref[0] = x_ref[0] + gt_ref[0] * (nm * gpost_ref[...])

    t5 = _tok_specs(tm, WA)
    tok = _tok_specs(tm, D)
    r5 = _row_spec(WA)
    full = lambda shape: pl.BlockSpec(shape, lambda b, i: (0,) * len(shape))
    return pl.pallas_call(
        body, name="mix_mid_fwd", grid=(nb, s // tm),
        out_shape=[jax.ShapeDtypeStruct((nb, s, D), F32), jax.ShapeDtypeStruct((nb, s, WB), F32),
                   jax.ShapeDtypeStruct((nb, s, D), BF16), jax.ShapeDtypeStruct((nb, s, D), F32)],
        in_specs=[tok, t5, t5, t5, t5, _halo_prev_spec(tm), _halo_prev_spec(tm), _mod_spec(), r5, r5,
                  full((CH, NH * CH)), full((CH, WA)), full((HALO, WB)), r5, r5, r5, r5, r5, VMEM_FULL, _row_spec()],
        out_specs=[tok, t5, tok, tok],
        scratch_shapes=[pltpu.VMEM((_ext_rows(tm), WB), F32), pltpu.VMEM((SHIFTS, tm + HALO, WB), F32)],
        compiler_params=_cparams(),
    )(x, u, v, a, g, a, g, gt, gn_g, gn_b, wcat, bspt, conv_w, conv_b, cn_g, cn_b, go_a, go_b, w_mo, g_post)


def _mix_out_bwd(dxo, m, gt, g_post, w_mo):
    nb, s, _ = m.shape
    tm = min(512, s)

    def body(dxo_ref, m_ref, gt_ref, gpost_ref, wmo_ref, dy_ref, dm_ref, dgpost_ref, dgt_ref):
        b, i = pl.program_id(0), pl.program_id(1)
        dxo_v = dxo_ref[0]
        nm, q = _rms(m_ref[0])
        gpost = gpost_ref[...]
        dgt = jnp.sum(dxo_v * (nm * gpost), axis=0, keepdims=True)
        dm, dgpost = _rms_bwd(dxo_v * gt_ref[0], nm, q, gpost)
        dmb = dm.astype(BF16)
        dm_ref[0] = dmb
        dy_ref[0] = _dot_nt(dmb, wmo_ref[...])
        _acc(dgpost_ref, dgpost, _first(b, i))
        _acc(dgt_ref, dgt[None], i == 0)

    tok = _tok_specs(tm, D)
    return pl.pallas_call(
        body, name="mix_out_bwd", grid=(nb, s // tm),
        out_shape=[jax.ShapeDtypeStruct((nb, s, D), F32), jax.ShapeDtypeStruct((nb, s, D), BF16),
                   jax.ShapeDtypeStruct((1, D), F32), jax.ShapeDtypeStruct((nb, 1, D), F32)],
        in_specs=[tok, tok, _mod_spec(), _row_spec(), VMEM_FULL],
        out_specs=[tok, tok, _row_spec(), _mod_spec()],
        compiler_params=_cparams(),
    )(dxo, m, gt, g_post, w_mo)


def _mix_mid_bwd(dy, u, v, conv, gn_g, gn_b, wcat, wcat_t, bspt, cn_g, cn_b, go_a, go_b):
    nb, s, _ = dy.shape
    tm = min(512, s)
    nchunk = tm // CH

    def body(dy_ref, u_ref, v_ref, conv_ref, gng_ref, gnb_ref, wcat_ref, wcatt_ref, bspt_ref, cng_ref, cnb_ref,
             goa_ref, gob_ref,
             du_ref, dv_ref, dconv_ref, dwcat_ref, dbsp_ref, dgng_ref, dgnb_ref, dgoa_ref, dgob_ref,
             dcng_ref, dcnb_ref, dcb_ref):
        first = _first(pl.program_id(0), pl.program_id(1))
        dyv = dy_ref[0]
        xhat, rstd = _ln(v_ref[0])
        gng = gng_ref[...]
        vb = (xhat * gng + gnb_ref[...]).astype(BF16)
        wsb = _spatial_weights(wcat_ref, False)
        wsb_t = _spatial_weights(wcatt_ref, True)
        bias = _spatial_bias(bspt_ref)
        masks = _head_mask((CH, WA))
        vexps, zs = [], []
        for cidx in range(nchunk):
            vexp = _expand_heads(vb[cidx * CH:(cidx + 1) * CH, :], masks)
            vexps.append(vexp)
            zs.append(_dot(wsb, vexp) + bias)
        z = jnp.concatenate(zs, axis=0)
        uv = u_ref[0]
        na, ra = _rms(uv * z)
        dya, dgoa = _rms_bwd(dyv[:, 0:WA], na, ra, goa_ref[...])
        du_ref[0] = dya * z
        dz = dya * uv
        dwcat = jnp.zeros((CH, NH * CH), F32)
        dzsum = jnp.zeros((CH, WA), F32)
        dvlns = []
        for cidx in range(nchunk):
            dzc = dz[cidx * CH:(cidx + 1) * CH, :]
            dzsum = dzsum + dzc
            dzb = dzc.astype(BF16)
            dwcat = dwcat + _dot_nt(dzb, vexps[cidx])
            dvexp = _dot(wsb_t, dzb)
            dvl = jnp.zeros((CH, WA), F32)
            for h in range(NH):
                dvl = dvl + jnp.where(masks[h], dvexp[h * CH:(h + 1) * CH, :], 0.0)
            dvlns.append(dvl)
        dvln = jnp.concatenate(dvlns, axis=0)
        dv, dgng, dgnb = _ln_bwd(dvln, xhat, rstd, gng)
        dv_ref[0] = dv
        lane = lax.broadcasted_iota(jnp.int32, (NH, WA), 1)
        head = lax.broadcasted_iota(jnp.int32, (NH, WA), 0)
        sel = jnp.where((lane >= head * HD) & (lane < (head + 1) * HD), 1.0, 0.0).astype(F32)
        dbsp = lax.dot_general(sel, dzsum, NT, preferred_element_type=F32, precision=lax.Precision.HIGHEST)
        chat, crstd = _ln(conv_ref[0])
        cng = cng_ref[...]
        cln = chat * cng + cnb_ref[...]
        sg = _sigmoid(cln)
        nbb, rb = _rms(cln * sg)
        dyb, dgob = _rms_bwd(dyv[:, WA:D], nbb, rb, gob_ref[...])
        dconv, dcng, dcnb = _ln_bwd(dyb * _dsilu(cln, sg), chat, crstd, cng)
        dconv_ref[0] = dconv
        dcb = jnp.sum(dconv, axis=0, keepdims=True)
        for ref, val in ((dwcat_ref, dwcat), (dbsp_ref, dbsp), (dgng_ref, dgng), (dgnb_ref, dgnb), (dgoa_ref, dgoa),
                         (dgob_ref, dgob), (dcng_ref, dcng), (dcnb_ref, dcnb), (dcb_ref, dcb)):
            _acc(ref, val, first)

    t5 = _tok_specs(tm, WA)
    r5 = _row_spec(WA)
    full = lambda shape: pl.BlockSpec(shape, lambda b, i: (0,) * len(shape))
    big = jax.ShapeDtypeStruct((nb, s, WA), F32)
    row = jax.ShapeDtypeStruct((1, WA), F32)
    return pl.pallas_call(
        body, name="mix_mid_bwd", grid=(nb, s // tm),
        out_shape=[big, big, big, jax.ShapeDtypeStruct((CH, NH * CH), F32), jax.ShapeDtypeStruct((NH, CH), F32),
                   row, row, row, row, row, row, row],
        in_specs=[_tok_specs(tm, D), t5, t5, t5, r5, r5, full((CH, NH * CH)), full((NH * CH, CH)), full((CH, WA)),
                  r5, r5, r5, r5],
        out_specs=[t5, t5, t5, full((CH, NH * CH)), full((NH, CH)), r5, r5, r5, r5, r5, r5, r5],
        compiler_params=_cparams(),
    )(dy, u, v, conv, gn_g, gn_b, wcat, wcat_t, bspt, cn_g, cn_b, go_a, go_b)


def _mix_in_bwd(dxo, x, du, dv, dconv, a, g, sh, sc, g_pre, w_mi4, conv_w):
    nb, s, _ = x.shape
    tm = min(512, s)
    n_i = s // tm

    def body(dxo_ref, x_ref, du_ref, dv_ref, dc_ref, dch_ref, a_ref, g_ref, ah_ref, gh_ref, sh_ref, sc_ref,
             gpre_ref, w_ref, cw_ref,
             dx_ref, dproj_ref, h_ref, dgpre_ref, dsh_ref, dsc_ref, dcw_ref, ext_ref, shf_ref, dglu_ref):
        b, i = pl.program_id(0), pl.program_id(1)
        first = _first(b, i)
        av, gv = a_ref[0], g_ref[0]
        sg = _sigmoid(gv)
        dconv = dc_ref[0]
        ext_ref[0:tm, :] = dconv
        ext_ref[tm:tm + HALO, :] = dch_ref[0] * jnp.where(i == n_i - 1, 0.0, 1.0).astype(F32)
        _make_shifts(ext_ref, shf_ref, tm)

        def put_dglu(r0, acc):
            dglu_ref[pl.ds(r0, TAP_ROWS), :] = acc

        _conv_taps(shf_ref, cw_ref, tm, [(CK - 1 - k, k) for k in range(CK)], put_dglu)
        dglu = dglu_ref[...]
        ext_ref[0:HALO, :] = (ah_ref[0] * _sigmoid(gh_ref[0])) * jnp.where(i == 0, 0.0, 1.0).astype(F32)
        ext_ref[HALO:HALO + tm, :] = av * sg
        _make_shifts(ext_ref, shf_ref, tm)

        @pl.when(first)
        def _():
            dcw_ref[...] = jnp.zeros((HALO, WB), F32)

        for k in range(CK):
            o = k + HALO - (CK - 1)
            lo = SHIFTS * (o // SHIFTS)
            dcw_ref[k:k + 1, :] += jnp.sum(dconv * shf_ref[o % SHIFTS, lo:lo + tm, :], axis=0, keepdims=True)
        da = dglu * sg
        dg = dglu * av * (sg * (1.0 - sg))
        parts = [du_ref[0].astype(BF16), dv_ref[0].astype(BF16), da.astype(BF16), dg.astype(BF16)]
        dh = jnp.zeros((tm, D), F32)
        for k in range(4):
            dproj_ref[0, :, k * WA:(k + 1) * WA] = parts[k]
            dh = dh + _dot_nt(parts[k], w_ref[k])
        n, r = _rms(x_ref[0])
        gpre = gpre_ref[...]
        ng = n * gpre
        scale1 = 1.0 + sc_ref[0]
        h_ref[0] = (ng * scale1 + sh_ref[0]).astype(BF16)
        dsh = jnp.sum(dh, axis=0, keepdims=True)
        dsc = jnp.sum(dh * ng, axis=0, keepdims=True)
        dxn, dgpre = _rms_bwd(dh * scale1, n, r, gpre)
        dx_ref[0] = dxo_ref[0] + dxn
        _acc(dgpre_ref, dgpre, first)
        _acc(dsh_ref, dsh[None], i == 0)
        _acc(dsc_ref, dsc[None], i == 0)

    tok = _tok_specs(tm, D)
    t5 = _tok_specs(tm, WA)
    full = lambda shape: pl.BlockSpec(shape, lambda b, i: (0,) * len(shape))
    mod_shape = jax.ShapeDtypeStruct((nb, 1, D), F32)
    return pl.pallas_call(
        body, name="mix_in_bwd", grid=(nb, n_i),
        out_shape=[jax.ShapeDtypeStruct((nb, s, D), F32), jax.ShapeDtypeStruct((nb, s, 4 * WA), BF16),
                   jax.ShapeDtypeStruct((nb, s, D), BF16), jax.ShapeDtypeStruct((1, D), F32), mod_shape, mod_shape,
                   jax.ShapeDtypeStruct((HALO, WB), F32)],
        in_specs=[tok, tok, t5, t5, t5, _halo_next_spec(tm, s), t5, t5, _halo_prev_spec(tm), _halo_prev_spec(tm),
                  _mod_spec(), _mod_spec(), _row_spec(), VMEM_FULL, full((HALO, WB))],
        out_specs=[tok, _tok_specs(tm, 4 * WA), tok, _row_spec(), _mod_spec(), _mod_spec(), full((HALO, WB))],
        scratch_shapes=[pltpu.VMEM((_ext_rows(tm), WB), F32), pltpu.VMEM((SHIFTS, tm + HALO, WB), F32),
                        pltpu.VMEM((tm, WB), F32)],
        compiler_params=_cparams(),
    )(dxo, x, du, dv, dconv, dconv, a, g, a, g, sh, sc, g_pre, w_mi4, conv_w)


def _row_tile(rows, cols):
    best = 16
    for t in range(16, rows + 1, 16):
        if rows % t == 0 and t * cols * 4 <= 512 * 1024:
            best = t
    return best


def _sum4(name, own4, recv, j_arr):
    _, rows, cols = own4.shape
    tr = _row_tile(rows, cols)

    def body(j_ref, own_ref, recv_ref, o_ref):
        del j_ref
        acc = own_ref[0]
        for k in range(3):
            acc = acc + recv_ref[k].astype(F32)
        o_ref[...] = acc

    return pl.pallas_call(
        body, name=name,
        grid_spec=pltpu.PrefetchScalarGridSpec(
            num_scalar_prefetch=1, grid=(rows // tr,),
            in_specs=[pl.BlockSpec((1, tr, cols), lambda i, j: (j[0], i, 0)),
                      pl.BlockSpec((3, tr, cols), lambda i, j: (0, i, 0))],
            out_specs=pl.BlockSpec((tr, cols), lambda i, j: (i, 0))),
        out_shape=jax.ShapeDtypeStruct((rows, cols), F32),
        compiler_params=_cparams(),
    )(j_arr, own4, recv)


def _pair_exchange(name, arrs):
    n = len(arrs)

    def plan(x, y, c, ins, outs):
        sends = []
        for a in range(n):
            rows = arrs[a].shape[1] // 2
            theirs = pl.ds(pl.multiple_of((1 - c) * rows, 16), rows)
            sends.append((ins[a].at[:, theirs], outs[a], (x, y, 1 - c), outs[a]))
        return [], sends

    shapes = [jax.ShapeDtypeStruct((a.shape[0], a.shape[1] // 2, a.shape[2]), a.dtype) for a in arrs]
    return _run_exchange(name, arrs, shapes, plan, 0, n)


def _pair_sum(name, g32, recv, c_arr):
    nblk, rows, cols = recv.shape
    tr = _row_tile(rows, cols)
    nh = rows // tr

    def body(c_ref, g_ref, r_ref, o32_ref, obf_ref):
        del c_ref
        val = g_ref[0] + r_ref[0].astype(F32)
        o32_ref[0] = val
        obf_ref[0] = val.astype(BF16)

    spec = pl.BlockSpec((1, tr, cols), lambda k, i, c: (k, i, 0))
    return pl.pallas_call(
        body, name=name,
        grid_spec=pltpu.PrefetchScalarGridSpec(
            num_scalar_prefetch=1, grid=(nblk, nh),
            in_specs=[pl.BlockSpec((1, tr, cols), lambda k, i, c: (k, c[0] * nh + i, 0)), spec],
            out_specs=[spec, spec]),
        out_shape=[jax.ShapeDtypeStruct(recv.shape, F32), jax.ShapeDtypeStruct(recv.shape, BF16)],
        compiler_params=_cparams(),
    )(c_arr, g32, recv)


def _adam_halves(name, w, m, v, mine, theirs, c_arr):
    rows, cols = w.shape
    tr = _row_tile(rows // 2, cols)
    nh = (rows // 2) // tr

    def body(c_ref, w_ref, m_ref, v_ref, mine_ref, theirs_ref, g_out, d_out, m_out, v_out):
        here = (pl.program_id(0) // nh) == c_ref[0]
        g = jnp.where(here, mine_ref[...], theirs_ref[...])
        delta, m2, v2 = _adam(w_ref[...], g, m_ref[...], v_ref[...])
        g_out[...] = g
        d_out[...] = delta
        m_out[...] = m2
        v_out[...] = v2

    spec = pl.BlockSpec((tr, cols), lambda i, c: (i, 0))
    shape = jax.ShapeDtypeStruct((rows, cols), F32)
    return pl.pallas_call(
        body, name=name,
        grid_spec=pltpu.PrefetchScalarGridSpec(
            num_scalar_prefetch=1, grid=(2 * nh,),
            in_specs=[spec, spec, spec,
                      pl.BlockSpec((tr, cols), lambda i, c: (jnp.clip(i - c[0] * nh, 0, nh - 1), 0)),
                      pl.BlockSpec((tr, cols), lambda i, c: (jnp.clip(i - (1 - c[0]) * nh, 0, nh - 1), 0))],
            out_specs=[spec] * 4),
        out_shape=[shape] * 4,
        compiler_params=_cparams(),
    )(c_arr, w, m, v, mine, theirs)


def _adam_big(name, w, m, v, ga, gb):
    rows, cols = w.shape
    tr = _row_tile(rows, cols)

    def body(w_ref, m_ref, v_ref, ga_ref, gb_ref, g_out, d_out, m_out, v_out):
        gsum = ga_ref[...] + gb_ref[...]
        delta, m2, v2 = _adam(w_ref[...], gsum, m_ref[...], v_ref[...])
        g_out[...] = gsum
        d_out[...] = delta
        m_out[...] = m2
        v_out[...] = v2

    spec = pl.BlockSpec((tr, cols), lambda i: (i, 0))
    shape = jax.ShapeDtypeStruct((rows, cols), F32)
    return pl.pallas_call(
        body, name=name, grid=(rows // tr,), out_shape=[shape] * 4,
        in_specs=[spec] * 5, out_specs=[spec] * 4, compiler_params=_cparams(),
    )(w, m, v, ga, gb)


PK_VEC = 0
PK_LOSS = 6
PK_PAIR = 8
PK_BSP = 16
PK_WCAT = 24
PK_ROWS = PK_WCAT + CH
PAIR_ORDER = ("gmlp_norm_g", "gmlp_norm_b", "conv_b", "conv_norm_g", "conv_norm_b", "g_out_a", "g_out_b")
VEC_ORDER = ("g_pre_f1", "g_post_f1", "g_pre_m", "g_post_m", "g_pre_f2", "g_post_f2")


def _pack_late(rows):
    counts = [r.shape[0] for r in rows]
    assert sum(counts) == 8

    def body(*refs):
        o_ref = refs[-1]
        at = 0
        for r, cnt in zip(refs[:-1], counts):
            o_ref[at:at + cnt, :] = r[...]
            at += cnt

    return pl.pallas_call(
        body, name="pack_late", out_shape=jax.ShapeDtypeStruct((8, D), F32),
        in_specs=[VMEM_FULL] * len(rows), out_specs=VMEM_FULL, compiler_params=_cparams(),
    )(*rows)


def _pack_small(vecs, pairs, dbsp, dwcat, lsum):
    def body(*refs):
        vec_refs = refs[:4]
        pair_refs = refs[4:11]
        dbsp_ref, dwcat_ref, lsum_ref, o_ref = refs[11:]
        o_ref[0:PK_WCAT, :] = jnp.zeros((PK_WCAT, D), F32)
        o_ref[PK_LOSS:PK_LOSS + 1, 0:128] = lsum_ref[...]
        for k, r in enumerate(vec_refs):
            o_ref[PK_VEC + 2 + k:PK_VEC + 3 + k, :] = r[...]
        for k, r in enumerate(pair_refs):
            row, half = PK_PAIR + k // 2, k % 2
            o_ref[row:row + 1, half * WA:(half + 1) * WA] = r[...]
        o_ref[PK_BSP:PK_BSP + NH, 0:CH] = dbsp_ref[...]
        o_ref[PK_WCAT:PK_ROWS, :] = dwcat_ref[...]

    args = list(vecs) + list(pairs) + [dbsp, dwcat, lsum]
    return pl.pallas_call(
        body, name="pack_small", out_shape=jax.ShapeDtypeStruct((PK_ROWS, D), F32),
        in_specs=[VMEM_FULL] * len(args), out_specs=VMEM_FULL, compiler_params=_cparams(),
    )(*args)


def _small_adam(pack_all, late_all, dcw_all, dada_all, params):
    names = list(VEC_ORDER) + list(PAIR_ORDER) + ["b_spatial", "w_spatial", "conv_w", "b_ada"]
    flat = []
    for nm in names:
        flat += list(params[nm])
    n_in = 4 + len(flat)

    def body(*refs):
        pack_ref, late_ref, dcw_ref, dada_ref = refs[:4]
        prm = refs[4:n_in]
        outs = refs[n_in:]

        def total(r0, nr, c0, nc):
            acc = pack_ref[0, r0:r0 + nr, c0:c0 + nc]
            for d in range(1, NDEV):
                acc = acc + pack_ref[d, r0:r0 + nr, c0:c0 + nc]
            return acc

        def emit(idx, g, getw, put):
            w_ref, m_ref, v_ref = prm[3 * idx:3 * idx + 3]
            delta, m2, v2 = _adam(getw(w_ref), g, getw(m_ref), getw(v_ref))
            for o_ref, val in zip(outs[4 * idx:4 * idx + 4], (g, delta, m2, v2)):
                put(o_ref, val)

        def whole(ref):
            return ref[...]

        def put_whole(ref, val):
            ref[...] = val

        idx = 0
        for k in range(6):
            if k < 2:
                g = late_ref[0, k:k + 1, :]
                for d in range(1, NDEV):
                    g = g + late_ref[d, k:k + 1, :]
            else:
                g = total(PK_VEC + k, 1, 0, D)
            emit(idx, g, whole, put_whole)
            idx += 1
        for k in range(7):
            emit(idx, total(PK_PAIR + k // 2, 1, (k % 2) * WA, WA), whole, put_whole)
            idx += 1
        emit(idx, total(PK_BSP, NH, 0, CH), lambda r: r[0], lambda r, val: r.__setitem__(0, val))
        idx += 1
        row = lax.broadcasted_iota(jnp.int32, (CH, CH), 0)
        col = lax.broadcasted_iota(jnp.int32, (CH, CH), 1)
        for h in range(NH):
            gh = jnp.where(col <= row, total(PK_WCAT, CH, h * CH, CH), 0.0)
            w_ref, m_ref, v_ref = prm[3 * idx:3 * idx + 3]
            delta, m2, v2 = _adam(w_ref[0, h], gh, m_ref[0, h], v_ref[0, h])
            for o_ref, val in zip(outs[4 * idx:4 * idx + 4], (gh, delta, m2, v2)):
                o_ref[0, h] = val
        idx += 1
        gcw = dcw_ref[0, 0:CK, :]
        for d in range(1, NDEV):
            gcw = gcw + dcw_ref[d, 0:CK, :]
        emit(idx, gcw, lambda r: r[0], lambda r, val: r.__setitem__(0, val))
        idx += 1
        emit(idx, jnp.sum(dada_ref[...], axis=0, keepdims=True), whole, put_whole)
        outs[-1][...] = jnp.sum(total(PK_LOSS, 1, 0, 128), axis=1, keepdims=True) * (0.5 / D)

    out_shape = []
    for nm in names:
        w = params[nm][0]
        out_shape += [jax.ShapeDtypeStruct(w.shape, F32)] * 4
    out_shape.append(jax.ShapeDtypeStruct((1, 1), F32))
    res = pl.pallas_call(
        body, name="small_adam", out_shape=out_shape,
        in_specs=[VMEM_FULL] * n_in, out_specs=[VMEM_FULL] * len(out_shape), compiler_params=_cparams(),
    )(pack_all, late_all, dcw_all, dada_all, *flat)
    return {nm: tuple(res[4 * k:4 * k + 4]) for k, nm in enumerate(names)}, res[-1].reshape(())


WEIGHTS = ['w_ada', 'b_ada', 'g_pre_f1', 'g_post_f1', 'w_f1_in', 'w_f1_out', 'g_pre_m', 'g_post_m', 'w_mix_in',
           'gmlp_norm_g', 'gmlp_norm_b', 'w_spatial', 'b_spatial', 'conv_w', 'conv_b', 'conv_norm_g', 'conv_norm_b',
           'g_out_a', 'g_out_b', 'w_mix_out', 'g_pre_f2', 'g_post_f2', 'w_f2_in', 'w_f2_out']
BIG = ('w_f1_in', 'w_f1_out', 'w_mix_in', 'w_mix_out', 'w_f2_in', 'w_f2_out')


def kernel(x, c, w_ada, b_ada, g_pre_f1, g_post_f1, w_f1_in, w_f1_out, g_pre_m, g_post_m, w_mix_in, gmlp_norm_g, gmlp_norm_b, w_spatial, b_spatial, conv_w, conv_b, conv_norm_g, conv_norm_b, g_out_a, g_out_b, w_mix_out, g_pre_f2, g_post_f2, w_f2_in, w_f2_out, loss_target, m_w_ada, m_b_ada, m_g_pre_f1, m_g_post_f1, m_w_f1_in, m_w_f1_out, m_g_pre_m, m_g_post_m, m_w_mix_in, m_gmlp_norm_g, m_gmlp_norm_b, m_w_spatial, m_b_spatial, m_conv_w, m_conv_b, m_conv_norm_g, m_conv_norm_b, m_g_out_a, m_g_out_b, m_w_mix_out, m_g_pre_f2, m_g_post_f2, m_w_f2_in, m_w_f2_out, v_w_ada, v_b_ada, v_g_pre_f1, v_g_post_f1, v_w_f1_in, v_w_f1_out, v_g_pre_m, v_g_post_m, v_w_mix_in, v_gmlp_norm_g, v_gmlp_norm_b, v_w_spatial, v_b_spatial, v_conv_w, v_conv_b, v_conv_norm_g, v_conv_norm_b, v_g_out_a, v_g_out_b, v_w_mix_out, v_g_pre_f2, v_g_post_f2, v_w_f2_in, v_w_f2_out):
    env = dict(locals())
    wts = {n: env[n] for n in WEIGHTS}
    mom = {n: env["m_" + n] for n in WEIGHTS}
    var = {n: env["v_" + n] for n in WEIGHTS}
    nb, s, _ = x.shape
    t = nb * s
    ax, ay, ac = lax.axis_index("x"), lax.axis_index("y"), lax.axis_index("c")
    j_chip = 2 * ax + ay
    dev = 4 * ax + 2 * ay + ac
    j_arr = j_chip.reshape(1).astype(jnp.int32)

    groups = (("w_f1_in", "w_f1_out"), ("w_mix_in", "w_mix_out"), ("w_f2_in", "w_f2_out"))
    def gather_operands(gi):
        srcs = [wts[n][0].astype(BF16) for n in groups[gi]] + ([conv_w[0]] if gi == 1 else [])
        lands = [lax.dynamic_update_index_in_dim(lax.empty((NCHIP,) + a.shape, a.dtype), a, j_chip, 0) for a in srcs]
        return srcs, lands

    def gather_start(gi, behind, operands=None):
        srcs, lands = operands or gather_operands(gi)
        plan_a, plan_b, n_b = _gather_plans([a.shape for a in srcs])
        ssem, rsem, srcs, lands, token = _split_start("gw_start%d" % gi, srcs, lands, plan_a, 3 * len(srcs), behind)
        gather[gi] = (srcs, lands, ssem, rsem, plan_a, plan_b, n_b)
        return token

    def gather_forward(gi, behind):
        srcs, lands, ssem, rsem, plan_a, plan_b, n_b = gather[gi]
        ssem, rsem, lands, token = _split_forward("gw_fwd%d" % gi, srcs, lands, ssem, rsem, plan_a, plan_b, n_b, behind)
        gather[gi] = (lands, ssem, rsem, plan_b)
        return token

    def gathered(gi, behind):
        lands, ssem, rsem, plan_b = gather[gi]
        return _split_wait("gw_wait%d" % gi, [], lands, ssem, rsem, plan_b, behind)

    gather = {}
    (c_all8,) = _allgather8("gather_c", [c.reshape(8, (nb * D) // 8)])
    token = gather_start(0, c_all8)
    c_all = c_all8.reshape(NDEV * nb, D) + token[0, 0]
    b_sh = lax.dynamic_slice(b_ada, (0, j_chip * ADA_SH), (1, ADA_SH))
    ada_sh = _ada_fwd(c_all, w_ada[0], b_sh)
    later = [gather_operands(1), gather_operands(2)]
    (ada4,) = _chip_allgather("gather_ada", [ada_sh], behind=[a for pair in later for arrs in pair for a in arrs])
    token = gather_forward(0, ada4)
    token = gather_start(1, token, later[0])
    token = gather_start(2, token, later[1])
    ada4 = ada4 + token[0:1, 0:1]
    ada_me = lax.dynamic_slice(ada4, (0, dev * nb, 0), (NCHIP, nb, ADA_SH))
    ada_me = jnp.transpose(ada_me, (1, 0, 2)).reshape(nb, NMOD * D)
    sh1, sc1, gt1, sh2, sc2, gt2, sh3, sc3, gt3 = [ada_me[:, k * D:(k + 1) * D].reshape(nb, 1, D) for k in range(NMOD)]

    wcat = jnp.transpose(w_spatial[0], (1, 0, 2)).reshape(CH, NH * CH)
    wcat_t = jnp.transpose(w_spatial[0], (0, 2, 1)).reshape(NH * CH, CH)
    bspt = jnp.repeat(b_spatial[0].T, HD, axis=1)

    w1i, w1o = gathered(0, sh1)
    w1o = w1o.reshape(DFF, D)
    x1, f1, p1 = _ffn_fwd(x, sh1, sc1, gt1, g_pre_f1, g_post_f1, w1i, w1o)
    wmi, wmo, cw4 = gathered(1, gather_forward(1, x1))
    wmo = wmo.reshape(D, D)
    cw_full = jnp.transpose(cw4, (1, 0, 2)).reshape(CK, WB)
    cw_pad = jnp.pad(cw_full, ((0, HALO - CK), (0, 0)))
    u, v, a, g = _mix_in_fwd(x1, sh2, sc2, g_pre_m, wmi)
    x2, conv, yb, m = _mix_mid_fwd(x1, u, v, a, g, gt2, gmlp_norm_g, gmlp_norm_b, wcat, bspt, cw_pad, conv_b,
                                   conv_norm_g, conv_norm_b, g_out_a, g_out_b, wmo, g_post_m)
    w2i, w2o = gathered(2, gather_forward(2, x2))
    w2o = w2o.reshape(DFF, D)
    dx3, df2, p2, lsum, dg_post_f2, dgt3 = _ffn_fwd(x2, sh3, sc3, gt3, g_pre_f2, g_post_f2, w2i, w2o, target=loss_target)

    def chip4(pair, rows):
        return [arr.reshape(NCHIP, rows, arr.shape[-1]) for arr in pair]

    def scatter_start(tag, pairs, behind):
        srcs = [p[1] for p in pairs]
        lands = [lax.empty((3,) + a.shape[1:], a.dtype) for a in srcs]
        ssem, rsem, srcs, lands, token = _split_start("gs_start_" + tag, srcs, lands, _scatter_plan(len(srcs)),
                                                      3 * len(srcs), behind)
        return (srcs, lands, ssem, rsem), token

    def scatter_wait(tag, state, behind):
        srcs, lands, ssem, rsem = state
        return _split_wait("gs_wait_" + tag, srcs, lands, ssem, rsem, _scatter_plan(len(srcs)), behind)

    def allgather_start(tag, arrs, behind):
        lands = [lax.dynamic_update_index_in_dim(lax.empty((NDEV,) + a.shape, a.dtype), a, dev, 0) for a in arrs]
        ssem, rsem, srcs, lands, token = _split_start("small_start_" + tag, arrs, lands, _allgather_plan(len(arrs)),
                                                      7 * len(arrs), behind)
        return (srcs, lands, ssem, rsem), token

    def allgather_wait(tag, state, behind):
        srcs, lands, ssem, rsem = state
        return _split_wait("small_wait_" + tag, srcs, lands, ssem, rsem, _allgather_plan(len(srcs)), behind)

    out = {}
    dx2, dp2, h3, a2, dg_pre_f2, dsh3, dsc3 = _ffn_bwd(
        dx3, x2, None, p2, sh3, sc3, gt3, g_pre_f2, g_post_f2, w2i, w2o, df=df2)
    gw2i = _wgrad("wgrad_f2_in", h3.reshape(t, D), dp2.reshape(t, 2 * DFF), 2 * DFF // NCHIP, True)
    gw2o = chip4(_wgrad("wgrad_f2_out", a2.reshape(t, DFF), df2.reshape(t, D), D // 2, False), DFF // NCHIP)
    scat_f2, tok = scatter_start("f2", [gw2i, gw2o], dg_post_f2)
    dy, dm, dg_post_m, dgt2 = _mix_out_bwd(dx2, m, gt2 + tok[0, 0], g_post_m, wmo)
    gwmo = chip4(_wgrad("wgrad_mix_out", yb.reshape(t, D), dm.reshape(t, D), D // 2, False), D // NCHIP)
    (du, dv, dconv, dwcat, dbsp, dgn_g, dgn_b, dgo_a, dgo_b, dcn_g, dcn_b, dcb) = _mix_mid_bwd(
        dy, u, v, conv, gmlp_norm_g, gmlp_norm_b, wcat, wcat_t, bspt, conv_norm_g, conv_norm_b, g_out_a, g_out_b)
    dx1, dproj, h2, dg_pre_m, dsh2, dsc2, dcw = _mix_in_bwd(dx2, x1, du, dv, dconv, a, g, sh2, sc2, g_pre_m, wmi, cw_pad)
    gwmi = _wgrad("wgrad_mix_in", h2.reshape(t, D), dproj.reshape(t, 4 * WA), WA, True)
    scat_mix, tok = scatter_start("mix", [gwmi, gwmo], dg_pre_m)

    vec_grads = dict(g_pre_m=dg_pre_m, g_post_m=dg_post_m, g_pre_f2=dg_pre_f2, g_post_f2=dg_post_f2)
    pair_grads = dict(gmlp_norm_g=dgn_g, gmlp_norm_b=dgn_b, conv_b=dcb, conv_norm_g=dcn_g, conv_norm_b=dcn_b,
                      g_out_a=dgo_a, g_out_b=dgo_b)
    pack = _pack_small([vec_grads[n] for n in VEC_ORDER[2:]], [pair_grads[n] for n in PAIR_ORDER], dbsp, dwcat, lsum)
    dada_early = jnp.concatenate([q.reshape(nb, D) for q in (dsh2, dsc2, dgt2, dsh3, dsc3, dgt3)], axis=1)
    early, tok2 = allgather_start("early", [pack, dcw, dada_early.reshape(8, (nb * 6 * D) // 8)], tok)
    grad_x, dp1, h1, a1, df1, dg_pre_f1, dg_post_f1, dsh1, dsc1, dgt1 = _ffn_bwd(
        dx1, x, f1, p1, sh1 + tok2[0, 0], sc1, gt1, g_pre_f1, g_post_f1, w1i, w1o)
    late_pack = _pack_late([dg_pre_f1, dg_post_f1] + [q.reshape(nb, D) for q in (dsh1, dsc1, dgt1)])
    late, tok2 = allgather_start("late", [late_pack], dg_post_f1)
    gw1i = _wgrad("wgrad_f1_in", h1.reshape(t, D), dp1.reshape(t, 2 * DFF), 2 * DFF // NCHIP, True)
    gw1o = chip4(_wgrad("wgrad_f1_out", a1.reshape(t, DFF), df1.reshape(t, D), D // 2, False), DFF // NCHIP)
    c_arr = ac.reshape(1).astype(jnp.int32)
    sib = _pair_exchange("pair_f1", [gw1i[1], gw1o[1]])
    pair_i = _pair_sum("pairsum_f1_in", gw1i[0], sib[0], c_arr)
    pair_o = _pair_sum("pairsum_f1_out", gw1o[0], sib[1], c_arr)
    scat_f1, tok = scatter_start("f1", [pair_i, pair_o], tok2)

    def reduce_and_update(tag, state, names, pairs, behind):
        recv = scatter_wait(tag, state, behind)
        part = [_sum4("sum4_" + n, pairs[k][0], recv[k], j_arr) for k, n in enumerate(names)]
        other = _sibling_swap("swap_" + tag, part)
        for k, n in enumerate(names):
            out[n] = tuple(r[None] for r in _adam_big("adam_" + n, wts[n][0], mom[n][0], var[n][0], part[k], other[k]))

    reduce_and_update("f2", scat_f2, ("w_f2_in", "w_f2_out"), [gw2i, gw2o], tok)
    reduce_and_update("mix", scat_mix, ("w_mix_in", "w_mix_out"), [gwmi, gwmo], out["w_f2_out"][3])

    pack_all, dcw_all, dada_early8 = allgather_wait("early", early, out["w_mix_out"][3])
    (late_all,) = allgather_wait("late", late, pack_all)
    dada_late = jnp.transpose(late_all[:, 2:8, :].reshape(NDEV, 3, nb, D), (0, 2, 1, 3)).reshape(NDEV * nb, 3 * D)
    dada_all = jnp.concatenate([dada_late, dada_early8.reshape(NDEV * nb, 6 * D)], axis=1)
    dcw_mine = lax.dynamic_slice(dcw_all, (0, 0, j_chip * (WB // NCHIP)), (NDEV, HALO, WB // NCHIP))
    small = {n: (wts[n], mom[n], var[n]) for n in list(VEC_ORDER) + list(PAIR_ORDER) + ["b_spatial", "w_spatial", "conv_w", "b_ada"]}
    small_out, loss = _small_adam(pack_all, late_all, dcw_mine, dada_all, small)
    out.update(small_out)
    dada_sh = lax.dynamic_slice(dada_all, (0, j_chip * ADA_SH), (NDEV * nb, ADA_SH))
    out["w_ada"] = tuple(r[None] for r in _ada_bwd_adam(c_all, dada_sh, w_ada[0], m_w_ada[0], v_w_ada[0]))
    recv = scatter_wait("f1", scat_f1, out["w_ada"][3])
    names = ("w_f1_in", "w_f1_out")
    mine = [_sum4("sum4_" + n, p[0], recv[k], j_arr)
            for k, (n, p) in enumerate(zip(names, (pair_i, pair_o)))]
    theirs = _sibling_swap("swap_f1", mine)
    for k, n in enumerate(names):
        out[n] = tuple(r[None] for r in _adam_halves("adam_" + n, wts[n][0], mom[n][0], var[n][0], mine[k], theirs[k],
                                                     c_arr))

    res = [loss, grad_x]
    for k in range(4):
        res += [out[n][k] for n in WEIGHTS]
    return tuple(res)
```

```python
import functools

import jax
import jax.numpy as jnp
from jax import lax
from jax.experimental import pallas as pl
from jax.experimental.pallas import tpu as pltpu

D = 1024
DFF = 2816
WA = 512
WB = 512
NH = 8
HD = 64
CH = 128
CK = 31
HALO = 32
NMOD = 9
EPS = 1e-6
NCHIP = 4
NDEV = 8
FBLK = DFF // 2
ADA_SH = NMOD * D // NCHIP

LR, B1, B2, EPS_A, WD, STEP = 0.001, 0.9, 0.999, 1e-08, 0.01, 10

F32 = jnp.float32
BF16 = jnp.bfloat16
MESH = pl.DeviceIdType.MESH
ANY = pl.BlockSpec(memory_space=pl.ANY)
VMEM_FULL = pl.BlockSpec(memory_space=pltpu.VMEM)
VMEM_LIMIT = 56 * 1024 * 1024

NT = (((1,), (1,)), ((), ()))
TN = (((0,), (0,)), ((), ()))


def _dot(a, b):
    return jnp.dot(a, b, preferred_element_type=F32)


def _dot_nt(a, b):
    return lax.dot_general(a, b, NT, preferred_element_type=F32)


def _dot_tn(a, b):
    return lax.dot_general(a, b, TN, preferred_element_type=F32)


def _cparams():
    return pltpu.CompilerParams(vmem_limit_bytes=VMEM_LIMIT)


def _allgather8(name, arrs):
    n = len(arrs)

    def body(*refs):
        ins, outs = refs[:n], refs[n:2 * n]
        send_sems, recv_sems, local_sems = refs[2 * n:]
        x, y, c = lax.axis_index("x"), lax.axis_index("y"), lax.axis_index("c")
        me, sibling = (x, y, c), (x, y, 1 - c)
        chips = [(1 - x, y), (x, 1 - y), (1 - x, 1 - y)]

        def copy(a, k, block, to, src=None):
            rows = outs[a].at[4 * block[0] + 2 * block[1] + block[2]]
            return pltpu.make_async_remote_copy(
                src_ref=rows if src is None else src, dst_ref=rows,
                send_sem=send_sems.at[a, k], recv_sem=recv_sems.at[a, k],
                device_id=to, device_id_type=MESH)

        started, mine = [], []
        for a in range(n):
            loc = pltpu.make_async_copy(ins[a], outs[a].at[4 * x + 2 * y + c], local_sems.at[a])
            loc.start()
            mine.append(loc)
            first = [copy(a, 0, me, sibling, src=ins[a])]
            first += [copy(a, 1 + j, me, (*chip, c), src=ins[a]) for j, chip in enumerate(chips)]
            for cp in first:
                cp.start()
            started += first
        for a in range(n):
            for j, chip in enumerate(chips):
                copy(a, 1 + j, (*chip, c), me).wait_recv()
                fwd = copy(a, 4 + j, (*chip, c), sibling)
                fwd.start()
                started.append(fwd)
        for a in range(n):
            copy(a, 0, sibling, me).wait_recv()
            for j, chip in enumerate(chips):
                copy(a, 4 + j, (*chip, 1 - c), me).wait_recv()
        for cp in started:
            cp.wait_send()
        for loc in mine:
            loc.wait()

    return pl.pallas_call(
        body, name=name,
        out_shape=[jax.ShapeDtypeStruct((NDEV,) + a.shape, a.dtype) for a in arrs],
        in_specs=[ANY] * n, out_specs=[ANY] * n,
        scratch_shapes=[pltpu.SemaphoreType.DMA((n, 7)), pltpu.SemaphoreType.DMA((n, 7)),
                        pltpu.SemaphoreType.DMA((n,))],
    )(*arrs)


def _chip_relations(x, y):
    return [(1 - x, y), (x, 1 - y), (1 - x, 1 - y)]


def _exchange(name, arrs, out_shapes, plan):
    n = len(arrs)
    n_out = len(out_shapes)

    def body(*refs):
        ins, outs = refs[:n], refs[n:n + n_out]
        send_sems, recv_sems, local_sems = refs[n + n_out:]
        x, y, c = lax.axis_index("x"), lax.axis_index("y"), lax.axis_index("c")
        local, sends = plan(x, y, c, ins, outs)
        locs = [pltpu.make_async_copy(s, d, local_sems.at[i]) for i, (s, d) in enumerate(local)]
        for loc in locs:
            loc.start()
        cps = [pltpu.make_async_remote_copy(src_ref=s, dst_ref=d, send_sem=send_sems.at[i], recv_sem=recv_sems.at[i],
                                            device_id=peer, device_id_type=MESH)
               for i, (s, d, peer, _) in enumerate(sends)]
        for cp in cps:
            cp.start()
        for i, (s, _, peer, landing) in enumerate(sends):
            pltpu.make_async_remote_copy(src_ref=s, dst_ref=landing, send_sem=send_sems.at[i], recv_sem=recv_sems.at[i],
                                         device_id=peer, device_id_type=MESH).wait_recv()
        for cp in cps:
            cp.wait_send()
        for loc in locs:
            loc.wait()

    return n, n_out, body


def _run_exchange(name, arrs, out_shapes, plan, n_local, n_send):
    n, n_out, body = _exchange(name, arrs, out_shapes, plan)
    return pl.pallas_call(
        body, name=name, out_shape=out_shapes,
        in_specs=[ANY] * n, out_specs=[ANY] * n_out,
        scratch_shapes=[pltpu.SemaphoreType.DMA((n_send,)), pltpu.SemaphoreType.DMA((n_send,)),
                        pltpu.SemaphoreType.DMA((max(n_local, 1),))],
    )(*arrs)


def _chip_allgather(name, arrs, behind=()):
    n = len(arrs)

    def plan(x, y, c, ins, outs):
        j_me = 2 * x + y
        local = [(ins[a], outs[a].at[j_me]) for a in range(n)]
        sends = []
        for a in range(n):
            for (px, py) in _chip_relations(x, y):
                sends.append((ins[a], outs[a].at[j_me], (px, py, c), outs[a].at[2 * px + py]))
        return local, sends

    shapes = [jax.ShapeDtypeStruct((NCHIP,) + a.shape, a.dtype) for a in arrs]
    return _run_exchange(name, list(arrs) + list(behind), shapes, plan, n, 3 * n)


def _chip_scatter(name, arrs):
    n = len(arrs)

    def plan(x, y, c, ins, outs):
        sends = []
        for a in range(n):
            for k, (px, py) in enumerate(_chip_relations(x, y)):
                sends.append((ins[a].at[2 * px + py], outs[a].at[k], (px, py, c), outs[a].at[k]))
        return [], sends

    shapes = [jax.ShapeDtypeStruct((3,) + a.shape[1:], a.dtype) for a in arrs]
    return _run_exchange(name, arrs, shapes, plan, 0, 3 * n)


def _sibling_swap(name, arrs, behind=None):
    n = len(arrs)

    def plan(x, y, c, ins, outs):
        return [], [(ins[a], outs[a], (x, y, 1 - c), outs[a]) for a in range(n)]

    shapes = [jax.ShapeDtypeStruct(a.shape, a.dtype) for a in arrs]
    return _run_exchange(name, list(arrs) + ([] if behind is None else [behind]), shapes, plan, 0, n)


HBM = pl.BlockSpec(memory_space=pltpu.HBM)
SEM = pl.BlockSpec(memory_space=pltpu.SEMAPHORE)
EFFECT = pltpu.SideEffectType.DATAFLOW_SIDE_EFFECTING


def _split_start(name, srcs, lands, plan, n_send, after):
    n, nl = len(srcs), len(lands)

    def body(*refs):
        src, land = refs[:n], refs[n:n + nl]
        send_sems, recv_sems = refs[n + nl + 1], refs[n + nl + 2]
        token = refs[-2]
        local_sems = refs[-1]
        x, y, c = lax.axis_index("x"), lax.axis_index("y"), lax.axis_index("c")
        local, sends = plan(x, y, c, src, land)
        locs = [pltpu.make_async_copy(s, d, local_sems.at[i]) for i, (s, d) in enumerate(local)]
        for loc in locs:
            loc.start()
        for loc in locs:
            loc.wait()
        for i, (s, d, peer, _) in enumerate(sends):
            pltpu.make_async_remote_copy(src_ref=s, dst_ref=d, send_sem=send_sems.at[i], recv_sem=recv_sems.at[i],
                                         device_id=peer, device_id_type=MESH).start()
        token[...] = jnp.zeros_like(token)

    thru = [pltpu.HBM(a.shape, a.dtype) for a in list(srcs) + list(lands)]
    res = pl.pallas_call(
        body, name=name,
        out_shape=(pltpu.SemaphoreType.DMA((n_send,)), pltpu.SemaphoreType.DMA((n_send,)), *thru,
                   jax.ShapeDtypeStruct((8, 128), F32)),
        in_specs=[HBM] * (n + nl) + [ANY],
        out_specs=(SEM, SEM, *([HBM] * (n + nl)), pl.BlockSpec(memory_space=pltpu.VMEM)),
        input_output_aliases={i: 2 + i for i in range(n + nl)},
        scratch_shapes=[pltpu.SemaphoreType.DMA((max(len(srcs), 1),))],
        compiler_params=pltpu.CompilerParams(has_side_effects=EFFECT),
    )(*[pltpu.with_memory_space_constraint(a, pltpu.HBM) for a in list(srcs) + list(lands)], after)
    return res[0], res[1], list(res[2:2 + n]), list(res[2 + n:2 + n + nl]), res[-1]


def _split_wait(name, srcs, lands, send_sems, recv_sems, plan, after):
    n, nl = len(srcs), len(lands)

    def body(*refs):
        src, land = refs[:n], refs[n:n + nl]
        send_sems, recv_sems = refs[n + nl], refs[n + nl + 1]
        x, y, c = lax.axis_index("x"), lax.axis_index("y"), lax.axis_index("c")
        _, sends = plan(x, y, c, src, land)
        for i, (s, _, peer, landing) in enumerate(sends):
            cp = pltpu.make_async_remote_copy(src_ref=s, dst_ref=landing, send_sem=send_sems.at[i],
                                              recv_sem=recv_sems.at[i], device_id=peer, device_id_type=MESH)
            cp.wait_send()
            cp.wait_recv()

    thru = [pltpu.HBM(a.shape, a.dtype) for a in list(srcs) + list(lands)]
    res = pl.pallas_call(
        body, name=name, out_shape=tuple(thru),
        in_specs=[HBM] * (n + nl) + [SEM, SEM, ANY], out_specs=tuple([HBM] * (n + nl)),
        input_output_aliases={i: i for i in range(n + nl)},
        compiler_params=pltpu.CompilerParams(has_side_effects=EFFECT),
    )(*srcs, *lands, send_sems, recv_sems, after)
    return list(res[n:])


def _split_forward(name, srcs, lands, send_a, recv_a, plan_a, plan_b, n_b, after):
    n, nl = len(srcs), len(lands)

    def body(*refs):
        src, land = refs[:n], refs[n:n + nl]
        send_a, recv_a = refs[n + nl], refs[n + nl + 1]
        send_b, recv_b = refs[n + nl + 3], refs[n + nl + 4]
        token = refs[-1]
        x, y, c = lax.axis_index("x"), lax.axis_index("y"), lax.axis_index("c")
        _, first = plan_a(x, y, c, src, land)
        for i, (s, _, peer, landing) in enumerate(first):
            cp = pltpu.make_async_remote_copy(src_ref=s, dst_ref=landing, send_sem=send_a.at[i],
                                              recv_sem=recv_a.at[i], device_id=peer, device_id_type=MESH)
            cp.wait_send()
            cp.wait_recv()
        _, second = plan_b(x, y, c, src, land)
        for i, (s, d, peer, _) in enumerate(second):
            pltpu.make_async_remote_copy(src_ref=s, dst_ref=d, send_sem=send_b.at[i], recv_sem=recv_b.at[i],
                                         device_id=peer, device_id_type=MESH).start()
        token[...] = jnp.zeros_like(token)

    thru = [pltpu.HBM(a.shape, a.dtype) for a in lands]
    res = pl.pallas_call(
        body, name=name,
        out_shape=(pltpu.SemaphoreType.DMA((n_b,)), pltpu.SemaphoreType.DMA((n_b,)), *thru,
                   jax.ShapeDtypeStruct((8, 128), F32)),
        in_specs=[HBM] * (n + nl) + [SEM, SEM, ANY],
        out_specs=(SEM, SEM, *([HBM] * nl), pl.BlockSpec(memory_space=pltpu.VMEM)),
        input_output_aliases={n + i: 2 + i for i in range(nl)},
        compiler_params=pltpu.CompilerParams(has_side_effects=EFFECT),
    )(*srcs, *lands, send_a, recv_a, after)
    return res[0], res[1], list(res[2:2 + nl]), res[-1]


def _gather_plans(shapes):
    n = len(shapes)

    def halves(a, c):
        rows = shapes[a][0] // 2
        return pl.ds(pl.multiple_of(c * rows, 16), rows), pl.ds(pl.multiple_of((1 - c) * rows, 16), rows)

    def split(a):
        return shapes[a][0] % 32 == 0

    def plan_a(x, y, c, src, land):
        j_me = 2 * x + y
        sends = []
        for a in range(n):
            for (px, py) in _chip_relations(x, y):
                if split(a):
                    mine, _ = halves(a, c)
                    sends.append((src[a].at[mine], land[a].at[j_me, mine], (px, py, c), land[a].at[2 * px + py, mine]))
                else:
                    sends.append((src[a], land[a].at[j_me], (px, py, c), land[a].at[2 * px + py]))
        return [], sends

    def plan_b(x, y, c, src, land):
        sends = []
        for a in range(n):
            if split(a):
                mine, other = halves(a, c)
                for (px, py) in _chip_relations(x, y):
                    j = 2 * px + py
                    sends.append((land[a].at[j, mine], land[a].at[j, mine], (x, y, 1 - c), land[a].at[j, other]))
        return [], sends

    n_b = 3 * sum(1 for a in range(n) if split(a))
    return plan_a, plan_b, n_b


def _allgather_plan(n):
    flips = [(dx, dy, dc) for dx in (0, 1) for dy in (0, 1) for dc in (0, 1) if dx or dy or dc]

    def plan(x, y, c, src, land):
        sends = []
        for a in range(n):
            for dx, dy, dc in flips:
                px, py, pc = x ^ dx, y ^ dy, c ^ dc
                sends.append((src[a], land[a].at[4 * x + 2 * y + c], (px, py, pc), land[a].at[4 * px + 2 * py + pc]))
        return [], sends

    return plan


def _scatter_plan(n):
    def plan(x, y, c, src, land):
        sends = []
        for a in range(n):
            for k, (px, py) in enumerate(_chip_relations(x, y)):
                sends.append((src[a].at[2 * px + py], land[a].at[k], (px, py, c), land[a].at[k]))
        return [], sends

    return plan


def _rms(x):
    r = lax.rsqrt(jnp.mean(x * x, axis=-1, keepdims=True) + EPS)
    return x * r, r


def _rms_bwd(dy, n, r, g):
    dg = jnp.sum(dy * n, axis=0, keepdims=True)
    dn = dy * g
    dx = r * (dn - n * jnp.mean(dn * n, axis=-1, keepdims=True))
    return dx, dg


def _ln(x):
    mu = jnp.mean(x, axis=-1, keepdims=True)
    xc = x - mu
    rstd = lax.rsqrt(jnp.mean(xc * xc, axis=-1, keepdims=True) + EPS)
    return xc * rstd, rstd


def _ln_bwd(dy, xhat, rstd, g):
    dg = jnp.sum(dy * xhat, axis=0, keepdims=True)
    db = jnp.sum(dy, axis=0, keepdims=True)
    dxh = dy * g
    dx = rstd * (dxh - jnp.mean(dxh, axis=-1, keepdims=True) - xhat * jnp.mean(dxh * xhat, axis=-1, keepdims=True))
    return dx, dg, db


def _sigmoid(x):
    return jax.nn.sigmoid(x)


def _dsilu(x, s):
    return s * (1.0 + x * (1.0 - s))


def _adam(w, g, m, v):
    m = B1 * m + (1.0 - B1) * g
    v = B2 * v + (1.0 - B2) * (g * g)
    m_hat = m / (1.0 - B1 ** STEP)
    v_hat = v / (1.0 - B2 ** STEP)
    delta = -LR * (m_hat / (jnp.sqrt(v_hat) + EPS_A) + WD * w)
    return delta, m, v


def _head_mask(shape):
    lane = lax.broadcasted_iota(jnp.int32, shape, len(shape) - 1)
    return [(lane >= h * HD) & (lane < (h + 1) * HD) for h in range(NH)]


def _first(b, i):
    return jnp.logical_and(b == 0, i == 0)


def _acc(ref, val, first):
    @pl.when(first)
    def _():
        ref[...] = val

    @pl.when(jnp.logical_not(first))
    def _():
        ref[...] += val


def _ada_fwd(c_all, w_sh, b_sh):
    nb = c_all.shape[0]
    tn = 768

    def body(c_ref, w_ref, b_ref, o_ref):
        cv = c_ref[...]
        cs = (cv * _sigmoid(cv)).astype(BF16)
        o_ref[...] = _dot(cs, w_ref[...].astype(BF16)) + b_ref[...]

    return pl.pallas_call(
        body, name="ada_fwd", grid=(ADA_SH // tn,),
        out_shape=jax.ShapeDtypeStruct((nb, ADA_SH), F32),
        in_specs=[pl.BlockSpec((nb, D), lambda j: (0, 0)), pl.BlockSpec((D, tn), lambda j: (0, j)),
                  pl.BlockSpec((1, tn), lambda j: (0, j))],
        out_specs=pl.BlockSpec((nb, tn), lambda j: (0, j)),
        compiler_params=_cparams(),
    )(c_all, w_sh, b_sh)


def _ada_bwd_adam(c_all, dada_sh, w, m, v):
    nb = c_all.shape[0]
    tn = 768

    def body(c_ref, d_ref, w_ref, m_ref, v_ref, g_out, d_out, m_out, v_out):
        cv = c_ref[...]
        cs = (cv * _sigmoid(cv)).astype(BF16)
        g = _dot_tn(cs, d_ref[...].astype(BF16))
        delta, m2, v2 = _adam(w_ref[...], g, m_ref[...], v_ref[...])
        g_out[...] = g
        d_out[...] = delta
        m_out[...] = m2
        v_out[...] = v2

    big = pl.BlockSpec((D, tn), lambda j: (0, j))
    shape = jax.ShapeDtypeStruct((D, ADA_SH), F32)
    return pl.pallas_call(
        body, name="ada_bwd_adam", grid=(ADA_SH // tn,),
        out_shape=[shape] * 4,
        in_specs=[pl.BlockSpec((nb, D), lambda j: (0, 0)), pl.BlockSpec((nb, tn), lambda j: (0, j)), big, big, big],
        out_specs=[big] * 4,
        compiler_params=_cparams(),
    )(c_all, dada_sh, w, m, v)


def _tok_specs(tm, width):
    return pl.BlockSpec((1, tm, width), lambda b, i: (b, i, 0))


def _mod_spec():
    return pl.BlockSpec((1, 1, D), lambda b, i: (b, 0, 0))


def _row_spec(width=D):
    return pl.BlockSpec((1, width), lambda b, i: (0, 0))


def _ffn_fwd(x, sh, sc, gt, g_pre, g_post, w_in4, w_out, target=None):
    nb, s, _ = x.shape
    tm = min(512, s)
    with_loss = target is not None

    def body(*refs):
        if with_loss:
            (x_ref, sh_ref, sc_ref, gt_ref, gpre_ref, gpost_ref, win_ref, wout_ref, tgt_ref,
             xo_ref, df_ref, p_ref, ls_ref, dgpost_ref, dgt_ref) = refs
        else:
            (x_ref, sh_ref, sc_ref, gt_ref, gpre_ref, gpost_ref, win_ref, wout_ref,
             xo_ref, f_ref, p_ref) = refs
        xv = x_ref[0]
        n, _ = _rms(xv)
        h = (n * gpre_ref[...]) * (1.0 + sc_ref[0]) + sh_ref[0]
        hb = h.astype(BF16)
        acc = jnp.zeros((tm, D), F32)
        for j in range(2):
            gate = _dot(hb, win_ref[j])
            up = _dot(hb, win_ref[2 + j])
            p_ref[0, :, j * FBLK:(j + 1) * FBLK] = gate.astype(BF16)
            p_ref[0, :, DFF + j * FBLK:DFF + (j + 1) * FBLK] = up.astype(BF16)
            a = (gate * _sigmoid(gate)) * up
            acc = acc + _dot(a.astype(BF16), wout_ref[j * FBLK:(j + 1) * FBLK, :])
        nf, q = _rms(acc)
        gpost = gpost_ref[...]
        half_gate = 0.5 * gt_ref[0]
        out = xv + half_gate * (nf * gpost)
        if with_loss:
            first = _first(pl.program_id(0), pl.program_id(1))
            err = out - tgt_ref[0]
            dout = err * (1.0 / D)
            xo_ref[0] = dout
            row = jnp.sum(err * err, axis=0, keepdims=True)
            part = row[:, 0:128]
            for k in range(1, D // 128):
                part = part + row[:, k * 128:(k + 1) * 128]
            _acc(ls_ref, part, first)
            df, dgpost = _rms_bwd(dout * half_gate, nf, q, gpost)
            df_ref[0] = df.astype(BF16)
            _acc(dgpost_ref, dgpost, first)
            _acc(dgt_ref, jnp.sum(dout * (0.5 * (nf * gpost)), axis=0, keepdims=True)[None], pl.program_id(1) == 0)
        else:
            f_ref[0] = acc
            xo_ref[0] = out

    in_specs = [_tok_specs(tm, D), _mod_spec(), _mod_spec(), _mod_spec(), _row_spec(), _row_spec(), VMEM_FULL, VMEM_FULL]
    args = [x, sh, sc, gt, g_pre, g_post, w_in4, w_out]
    out_shape = [jax.ShapeDtypeStruct((nb, s, D), F32), jax.ShapeDtypeStruct((nb, s, D), BF16 if with_loss else F32),
                 jax.ShapeDtypeStruct((nb, s, 2 * DFF), BF16)]
    out_specs = [_tok_specs(tm, D), _tok_specs(tm, D), _tok_specs(tm, 2 * DFF)]
    if with_loss:
        in_specs.append(_tok_specs(tm, D))
        args.append(target)
        out_shape += [jax.ShapeDtypeStruct((1, 128), F32), jax.ShapeDtypeStruct((1, D), F32),
                      jax.ShapeDtypeStruct((nb, 1, D), F32)]
        out_specs += [pl.BlockSpec((1, 128), lambda b, i: (0, 0)), _row_spec(), _mod_spec()]
    return pl.pallas_call(
        body, name="ffn_loss_fwd" if with_loss else "ffn_fwd", grid=(nb, s // tm),
        out_shape=out_shape, in_specs=in_specs, out_specs=out_specs,
        compiler_params=_cparams(),
    )(*args)


def _ffn_bwd(dxo, x, f, p, sh, sc, gt, g_pre, g_post, w_in4, w_out, df=None):
    nb, s, _ = x.shape
    tm = min(256, s)
    given = df is not None

    def body(*refs):
        if given:
            (dxo_ref, x_ref, dfin_ref, p_ref, sh_ref, sc_ref, gpre_ref, win_ref, wout_ref,
             dx_ref, dp_ref, h_ref, a_ref, dgpre_ref, dsh_ref, dsc_ref) = refs
        else:
            (dxo_ref, x_ref, f_ref, p_ref, sh_ref, sc_ref, gt_ref, gpre_ref, gpost_ref, win_ref, wout_ref,
             dx_ref, dp_ref, h_ref, a_ref, df_ref, dgpre_ref, dgpost_ref, dsh_ref, dsc_ref, dgt_ref) = refs
        b, i = pl.program_id(0), pl.program_id(1)
        dxo_v = dxo_ref[0]
        if given:
            dfb = dfin_ref[0]
        else:
            nf, q = _rms(f_ref[0])
            gpost = gpost_ref[...]
            dgt = jnp.sum(dxo_v * (0.5 * (nf * gpost)), axis=0, keepdims=True)
            do = dxo_v * (0.5 * gt_ref[0])
            dfv, dgpost = _rms_bwd(do, nf, q, gpost)
            dfb = dfv.astype(BF16)
            df_ref[0] = dfb
        xv = x_ref[0]
        n, r = _rms(xv)
        gpre = gpre_ref[...]
        ng = n * gpre
        scale1 = 1.0 + sc_ref[0]
        h = ng * scale1 + sh_ref[0]
        h_ref[0] = h.astype(BF16)
        dh = jnp.zeros((tm, D), F32)
        for j in range(2):
            gate = p_ref[0, :, j * FBLK:(j + 1) * FBLK].astype(F32)
            up = p_ref[0, :, DFF + j * FBLK:DFF + (j + 1) * FBLK].astype(F32)
            sg = _sigmoid(gate)
            act = gate * sg
            a_ref[0, :, j * FBLK:(j + 1) * FBLK] = (act * up).astype(BF16)
            da = _dot_nt(dfb, wout_ref[j * FBLK:(j + 1) * FBLK, :])
            dgate = (da * up * _dsilu(gate, sg)).astype(BF16)
            dup = (da * act).astype(BF16)
            dp_ref[0, :, j * FBLK:(j + 1) * FBLK] = dgate
            dp_ref[0, :, DFF + j * FBLK:DFF + (j + 1) * FBLK] = dup
            dh = dh + _dot_nt(dgate, win_ref[j]) + _dot_nt(dup, win_ref[2 + j])
        dsh = jnp.sum(dh, axis=0, keepdims=True)
        dsc = jnp.sum(dh * ng, axis=0, keepdims=True)
        dxn, dgpre = _rms_bwd(dh * scale1, n, r, gpre)
        dx_ref[0] = dxo_v + dxn
        _acc(dgpre_ref, dgpre, _first(b, i))
        _acc(dsh_ref, dsh[None], i == 0)
        _acc(dsc_ref, dsc[None], i == 0)
        if not given:
            _acc(dgpost_ref, dgpost, _first(b, i))
            _acc(dgt_ref, dgt[None], i == 0)

    tok = _tok_specs(tm, D)
    mod_shape = jax.ShapeDtypeStruct((nb, 1, D), F32)
    row_shape = jax.ShapeDtypeStruct((1, D), F32)
    big = [jax.ShapeDtypeStruct((nb, s, D), F32), jax.ShapeDtypeStruct((nb, s, 2 * DFF), BF16),
           jax.ShapeDtypeStruct((nb, s, D), BF16), jax.ShapeDtypeStruct((nb, s, DFF), BF16)]
    big_specs = [tok, _tok_specs(tm, 2 * DFF), tok, _tok_specs(tm, DFF)]
    if given:
        return pl.pallas_call(
            body, name="ffn_bwd_after_loss", grid=(nb, s // tm),
            out_shape=big + [row_shape, mod_shape, mod_shape],
            in_specs=[tok, tok, tok, _tok_specs(tm, 2 * DFF), _mod_spec(), _mod_spec(), _row_spec(), VMEM_FULL, VMEM_FULL],
            out_specs=big_specs + [_row_spec(), _mod_spec(), _mod_spec()],
            compiler_params=_cparams(),
        )(dxo, x, df, p, sh, sc, g_pre, w_in4, w_out)
    return pl.pallas_call(
        body, name="ffn_bwd", grid=(nb, s // tm),
        out_shape=big + [jax.ShapeDtypeStruct((nb, s, D), BF16), row_shape, row_shape, mod_shape, mod_shape, mod_shape],
        in_specs=[tok, tok, tok, _tok_specs(tm, 2 * DFF), _mod_spec(), _mod_spec(), _mod_spec(), _row_spec(), _row_spec(),
                  VMEM_FULL, VMEM_FULL],
        out_specs=big_specs + [tok, _row_spec(), _row_spec(), _mod_spec(), _mod_spec(), _mod_spec()],
        compiler_params=_cparams(),
    )(dxo, x, f, p, sh, sc, gt, g_pre, g_post, w_in4, w_out)


def _wgrad(name, a, b, col_block, chip_major):
    t, ka = a.shape
    n = b.shape[1]
    tk = min(t, 512)
    while tk * 2 <= t and t % (tk * 2) == 0 and 2 * (tk * 2) * max(ka, col_block) <= 6 * 1024 * 1024:
        tk *= 2
    nk = t // tk
    nblk = n // col_block

    def body(a_ref, b_ref, o_ref, obf_ref, acc_ref):
        k = pl.program_id(1)

        @pl.when(k == 0)
        def _():
            acc_ref[...] = jnp.zeros_like(acc_ref)

        acc_ref[...] += _dot_tn(a_ref[...], b_ref[...])

        @pl.when(k == nk - 1)
        def _():
            val = acc_ref[...]
            if chip_major:
                o_ref[0] = val
                obf_ref[0] = val.astype(BF16)
            else:
                o_ref[...] = val
                obf_ref[...] = val.astype(BF16)

    if chip_major:
        shape = (nblk, ka, col_block)
        ospec = pl.BlockSpec((1, ka, col_block), lambda j, k: (j, 0, 0))
    else:
        shape = (ka, n)
        ospec = pl.BlockSpec((ka, col_block), lambda j, k: (0, j))
    return pl.pallas_call(
        body, name=name, grid=(nblk, nk),
        out_shape=[jax.ShapeDtypeStruct(shape, F32), jax.ShapeDtypeStruct(shape, BF16)],
        in_specs=[pl.BlockSpec((tk, ka), lambda j, k: (k, 0)), pl.BlockSpec((tk, col_block), lambda j, k: (k, j))],
        out_specs=[ospec, ospec],
        scratch_shapes=[pltpu.VMEM((ka, col_block), F32)],
        compiler_params=_cparams(),
    )(a, b)


def _mix_in_fwd(x, sh, sc, g_pre, w_mi4):
    nb, s, _ = x.shape
    tm = min(512, s)

    def body(x_ref, sh_ref, sc_ref, gpre_ref, w_ref, u_ref, v_ref, a_ref, g_ref):
        n, _ = _rms(x_ref[0])
        hb = ((n * gpre_ref[...]) * (1.0 + sc_ref[0]) + sh_ref[0]).astype(BF16)
        for k, o_ref in enumerate((u_ref, v_ref, a_ref, g_ref)):
            o_ref[0] = _dot(hb, w_ref[k])

    shape = jax.ShapeDtypeStruct((nb, s, WA), F32)
    return pl.pallas_call(
        body, name="mix_in_fwd", grid=(nb, s // tm),
        out_shape=[shape] * 4,
        in_specs=[_tok_specs(tm, D), _mod_spec(), _mod_spec(), _row_spec(), VMEM_FULL],
        out_specs=[_tok_specs(tm, WA)] * 4,
        compiler_params=_cparams(),
    )(x, sh, sc, g_pre, w_mi4)


def _spatial_weights(wcat_ref, transposed):
    w = wcat_ref[...]
    row = lax.broadcasted_iota(jnp.int32, w.shape, 0)
    col = lax.broadcasted_iota(jnp.int32, w.shape, 1)
    keep = ((row & (CH - 1)) <= col) if transposed else ((col & (CH - 1)) <= row)
    return jnp.where(keep, w, 0.0).astype(BF16)


def _expand_heads(vc, masks):
    return jnp.concatenate([jnp.where(mk, vc, jnp.zeros_like(vc)) for mk in masks], axis=0)


def _spatial_bias(bspt_ref):
    return bspt_ref[...]


SHIFTS = 8
TAP_ROWS = 32


def _ext_rows(tm):
    return tm + HALO + SHIFTS


def _make_shifts(ext_ref, sh_ref, tm):
    ext_ref[tm + HALO:tm + HALO + SHIFTS, :] = jnp.zeros((SHIFTS, WB), F32)
    for r in range(SHIFTS):
        sh_ref[r] = ext_ref[r:r + tm + HALO, :]


def _conv_taps(sh_ref, w_ref, tm, taps, emit):
    def block(i, carry):
        r0 = pl.multiple_of(i * TAP_ROWS, TAP_ROWS)
        acc = jnp.zeros((TAP_ROWS, WB), F32)
        for o, k in taps:
            acc = acc + w_ref[k:k + 1, :] * sh_ref[o % SHIFTS, pl.ds(r0 + SHIFTS * (o // SHIFTS), TAP_ROWS), :]
        emit(r0, acc)
        return carry

    lax.fori_loop(0, tm // TAP_ROWS, block, 0)


def _halo_prev_spec(tm):
    return pl.BlockSpec((1, HALO, WB), lambda b, i: (b, jnp.maximum(i * (tm // HALO) - 1, 0), 0))


def _halo_next_spec(tm, s):
    return pl.BlockSpec((1, HALO, WB), lambda b, i: (b, jnp.minimum((i + 1) * (tm // HALO), s // HALO - 1), 0))


def _mix_mid_fwd(x, u, v, a, g, gt, gn_g, gn_b, wcat, bspt, conv_w, conv_b, cn_g, cn_b, go_a, go_b, w_mo, g_post):
    nb, s, _ = x.shape
    tm = min(512, s)

    def body(x_ref, u_ref, v_ref, a_ref, g_ref, ah_ref, gh_ref, gt_ref, gng_ref, gnb_ref, wcat_ref, bspt_ref,
             cw_ref, cb_ref, cng_ref, cnb_ref, goa_ref, gob_ref, wmo_ref, gpost_ref,
             xo_ref, conv_ref, y_ref, m_ref, ext_ref, sh_ref):
        i = pl.program_id(1)
        xhat, _ = _ln(v_ref[0])
        vb = (xhat * gng_ref[...] + gnb_ref[...]).astype(BF16)
        wsb = _spatial_weights(wcat_ref, False)
        bias = _spatial_bias(bspt_ref)
        masks = _head_mask((CH, WA))
        zs = []
        for cidx in range(tm // CH):
            vexp = _expand_heads(vb[cidx * CH:(cidx + 1) * CH, :], masks)
            zs.append(_dot(wsb, vexp) + bias)
        z = jnp.concatenate(zs, axis=0)
        na, _ = _rms(u_ref[0] * z)
        keep = jnp.where(i == 0, 0.0, 1.0).astype(F32)
        ext_ref[0:HALO, :] = (ah_ref[0] * _sigmoid(gh_ref[0])) * keep
        ext_ref[HALO:HALO + tm, :] = a_ref[0] * _sigmoid(g_ref[0])
        _make_shifts(ext_ref, sh_ref, tm)
        cb = cb_ref[...]

        def put_conv(r0, acc):
            conv_ref[0, pl.ds(r0, TAP_ROWS), :] = acc + cb

        _conv_taps(sh_ref, cw_ref, tm, [(k + HALO - (CK - 1), k) for k in range(CK)], put_conv)
        conv = conv_ref[0]
        chat, _ = _ln(conv)
        cln = chat * cng_ref[...] + cnb_ref[...]
        nbb, _ = _rms(cln * _sigmoid(cln))
        yb = jnp.concatenate([na * goa_ref[...], nbb * gob_ref[...]], axis=1).astype(BF16)
        y_ref[0] = yb
        m = _dot(yb, wmo_ref[...])
        m_ref[0] = m
        nm, _ = _rms(m)
        xo_ref[0] = x_ref[0] + gt_ref[0] * (nm * gpost_ref[...])

    t5 = _tok_specs(tm, WA)
    tok = _tok_specs(tm, D)
    r5 = _row_spec(WA)
    full = lambda shape: pl.BlockSpec(shape, lambda b, i: (0,) * len(shape))
    return pl.pallas_call(
        body, name="mix_mid_fwd", grid=(nb, s // tm),
        out_shape=[jax.ShapeDtypeStruct((nb, s, D), F32), jax.ShapeDtypeStruct((nb, s, WB), F32),
                   jax.ShapeDtypeStruct((nb, s, D), BF16), jax.ShapeDtypeStruct((nb, s, D), F32)],
        in_specs=[tok, t5, t5, t5, t5, _halo_prev_spec(tm), _halo_prev_spec(tm), _mod_spec(), r5, r5,
                  full((CH, NH * CH)), full((CH, WA)), full((HALO, WB)), r5, r5, r5, r5, r5, VMEM_FULL, _row_spec()],
        out_specs=[tok, t5, tok, tok],
        scratch_shapes=[pltpu.VMEM((_ext_rows(tm), WB), F32), pltpu.VMEM((SHIFTS, tm + HALO, WB), F32)],
        compiler_params=_cparams(),
    )(x, u, v, a, g, a, g, gt, gn_g, gn_b, wcat, bspt, conv_w, conv_b, cn_g, cn_b, go_a, go_b, w_mo, g_post)


def _mix_out_bwd(dxo, m, gt, g_post, w_mo):
    nb, s, _ = m.shape
    tm = min(512, s)

    def body(dxo_ref, m_ref, gt_ref, gpost_ref, wmo_ref, dy_ref, dm_ref, dgpost_ref, dgt_ref):
        b, i = pl.program_id(0), pl.program_id(1)
        dxo_v = dxo_ref[0]
        nm, q = _rms(m_ref[0])
        gpost = gpost_ref[...]
        dgt = jnp.sum(dxo_v * (nm * gpost), axis=0, keepdims=True)
        dm, dgpost = _rms_bwd(dxo_v * gt_ref[0], nm, q, gpost)
        dmb = dm.astype(BF16)
        dm_ref[0] = dmb
        dy_ref[0] = _dot_nt(dmb, wmo_ref[...])
        _acc(dgpost_ref, dgpost, _first(b, i))
        _acc(dgt_ref, dgt[None], i == 0)

    tok = _tok_specs(tm, D)
    return pl.pallas_call(
        body, name="mix_out_bwd", grid=(nb, s // tm),
        out_shape=[jax.ShapeDtypeStruct((nb, s, D), F32), jax.ShapeDtypeStruct((nb, s, D), BF16),
                   jax.ShapeDtypeStruct((1, D), F32), jax.ShapeDtypeStruct((nb, 1, D), F32)],
        in_specs=[tok, tok, _mod_spec(), _row_spec(), VMEM_FULL],
        out_specs=[tok, tok, _row_spec(), _mod_spec()],
        compiler_params=_cparams(),
    )(dxo, m, gt, g_post, w_mo)


def _mix_mid_bwd(dy, u, v, conv, gn_g, gn_b, wcat, wcat_t, bspt, cn_g, cn_b, go_a, go_b):
    nb, s, _ = dy.shape
    tm = min(512, s)
    nchunk = tm // CH

    def body(dy_ref, u_ref, v_ref, conv_ref, gng_ref, gnb_ref, wcat_ref, wcatt_ref, bspt_ref, cng_ref, cnb_ref,
             goa_ref, gob_ref,
             du_ref, dv_ref, dconv_ref, dwcat_ref, dbsp_ref, dgng_ref, dgnb_ref, dgoa_ref, dgob_ref,
             dcng_ref, dcnb_ref, dcb_ref):
        first = _first(pl.program_id(0), pl.program_id(1))
        dyv = dy_ref[0]
        xhat, rstd = _ln(v_ref[0])
        gng = gng_ref[...]
        vb = (xhat * gng + gnb_ref[...]).astype(BF16)
        wsb = _spatial_weights(wcat_ref, False)
        wsb_t = _spatial_weights(wcatt_ref, True)
        bias = _spatial_bias(bspt_ref)
        masks = _head_mask((CH, WA))
        vexps, zs = [], []
        for cidx in range(nchunk):
            vexp = _expand_heads(vb[cidx * CH:(cidx + 1) * CH, :], masks)
            vexps.append(vexp)
            zs.append(_dot(wsb, vexp) + bias)
        z = jnp.concatenate(zs, axis=0)
        uv = u_ref[0]
        na, ra = _rms(uv * z)
        dya, dgoa = _rms_bwd(dyv[:, 0:WA], na, ra, goa_ref[...])
        du_ref[0] = dya * z
        dz = dya * uv
        dwcat = jnp.zeros((CH, NH * CH), F32)
        dzsum = jnp.zeros((CH, WA), F32)
        dvlns = []
        for cidx in range(nchunk):
            dzc = dz[cidx * CH:(cidx + 1) * CH, :]
            dzsum = dzsum + dzc
            dzb = dzc.astype(BF16)
            dwcat = dwcat + _dot_nt(dzb, vexps[cidx])
            dvexp = _dot(wsb_t, dzb)
            dvl = jnp.zeros((CH, WA), F32)
            for h in range(NH):
                dvl = dvl + jnp.where(masks[h], dvexp[h * CH:(h + 1) * CH, :], 0.0)
            dvlns.append(dvl)
        dvln = jnp.concatenate(dvlns, axis=0)
        dv, dgng, dgnb = _ln_bwd(dvln, xhat, rstd, gng)
        dv_ref[0] = dv
        lane = lax.broadcasted_iota(jnp.int32, (NH, WA), 1)
        head = lax.broadcasted_iota(jnp.int32, (NH, WA), 0)
        sel = jnp.where((lane >= head * HD) & (lane < (head + 1) * HD), 1.0, 0.0).astype(F32)
        dbsp = lax.dot_general(sel, dzsum, NT, preferred_element_type=F32, precision=lax.Precision.HIGHEST)
        chat, crstd = _ln(conv_ref[0])
        cng = cng_ref[...]
        cln = chat * cng + cnb_ref[...]
        sg = _sigmoid(cln)
        nbb, rb = _rms(cln * sg)
        dyb, dgob = _rms_bwd(dyv[:, WA:D], nbb, rb, gob_ref[...])
        dconv, dcng, dcnb = _ln_bwd(dyb * _dsilu(cln, sg), chat, crstd, cng)
        dconv_ref[0] = dconv
        dcb = jnp.sum(dconv, axis=0, keepdims=True)
        for ref, val in ((dwcat_ref, dwcat), (dbsp_ref, dbsp), (dgng_ref, dgng), (dgnb_ref, dgnb), (dgoa_ref, dgoa),
                         (dgob_ref, dgob), (dcng_ref, dcng), (dcnb_ref, dcnb), (dcb_ref, dcb)):
            _acc(ref, val, first)

    t5 = _tok_specs(tm, WA)
    r5 = _row_spec(WA)
    full = lambda shape: pl.BlockSpec(shape, lambda b, i: (0,) * len(shape))
    big = jax.ShapeDtypeStruct((nb, s, WA), F32)
    row = jax.ShapeDtypeStruct((1, WA), F32)
    return pl.pallas_call(
        body, name="mix_mid_bwd", grid=(nb, s // tm),
        out_shape=[big, big, big, jax.ShapeDtypeStruct((CH, NH * CH), F32), jax.ShapeDtypeStruct((NH, CH), F32),
                   row, row, row, row, row, row, row],
        in_specs=[_tok_specs(tm, D), t5, t5, t5, r5, r5, full((CH, NH * CH)), full((NH * CH, CH)), full((CH, WA)),
                  r5, r5, r5, r5],
        out_specs=[t5, t5, t5, full((CH, NH * CH)), full((NH, CH)), r5, r5, r5, r5, r5, r5, r5],
        compiler_params=_cparams(),
    )(dy, u, v, conv, gn_g, gn_b, wcat, wcat_t, bspt, cn_g, cn_b, go_a, go_b)


def _mix_in_bwd(dxo, x, du, dv, dconv, a, g, sh, sc, g_pre, w_mi4, conv_w):
    nb, s, _ = x.shape
    tm = min(512, s)
    n_i = s // tm

    def body(dxo_ref, x_ref, du_ref, dv_ref, dc_ref, dch_ref, a_ref, g_ref, ah_ref, gh_ref, sh_ref, sc_ref,
             gpre_ref, w_ref, cw_ref,
             dx_ref, dproj_ref, h_ref, dgpre_ref, dsh_ref, dsc_ref, dcw_ref, ext_ref, shf_ref, dglu_ref):
        b, i = pl.program_id(0), pl.program_id(1)
        first = _first(b, i)
        av, gv = a_ref[0], g_ref[0]
        sg = _sigmoid(gv)
        dconv = dc_ref[0]
        ext_ref[0:tm, :] = dconv
        ext_ref[tm:tm + HALO, :] = dch_ref[0] * jnp.where(i == n_i - 1, 0.0, 1.0).astype(F32)
        _make_shifts(ext_ref, shf_ref, tm)

        def put_dglu(r0, acc):
            dglu_ref[pl.ds(r0, TAP_ROWS), :] = acc

        _conv_taps(shf_ref, cw_ref, tm, [(CK - 1 - k, k) for k in range(CK)], put_dglu)
        dglu = dglu_ref[...]
        ext_ref[0:HALO, :] = (ah_ref[0] * _sigmoid(gh_ref[0])) * jnp.where(i == 0, 0.0, 1.0).astype(F32)
        ext_ref[HALO:HALO + tm, :] = av * sg
        _make_shifts(ext_ref, shf_ref, tm)

        @pl.when(first)
        def _():
            dcw_ref[...] = jnp.zeros((HALO, WB), F32)

        for k in range(CK):
            o = k + HALO - (CK - 1)
            lo = SHIFTS * (o // SHIFTS)
            dcw_ref[k:k + 1, :] += jnp.sum(dconv * shf_ref[o % SHIFTS, lo:lo + tm, :], axis=0, keepdims=True)
        da = dglu * sg
        dg = dglu * av * (sg * (1.0 - sg))
        parts = [du_ref[0].astype(BF16), dv_ref[0].astype(BF16), da.astype(BF16), dg.astype(BF16)]
        dh = jnp.zeros((tm, D), F32)
        for k in range(4):
            dproj_ref[0, :, k * WA:(k + 1) * WA] = parts[k]
            dh = dh + _dot_nt(parts[k], w_ref[k])
        n, r = _rms(x_ref[0])
        gpre = gpre_ref[...]
        ng = n * gpre
        scale1 = 1.0 + sc_ref[0]
        h_ref[0] = (ng * scale1 + sh_ref[0]).astype(BF16)
        dsh = jnp.sum(dh, axis=0, keepdims=True)
        dsc = jnp.sum(dh * ng, axis=0, keepdims=True)
        dxn, dgpre = _rms_bwd(dh * scale1, n, r, gpre)
        dx_ref[0] = dxo_ref[0] + dxn
        _acc(dgpre_ref, dgpre, first)
        _acc(dsh_ref, dsh[None], i == 0)
        _acc(dsc_ref, dsc[None], i == 0)

    tok = _tok_specs(tm, D)
    t5 = _tok_specs(tm, WA)
    full = lambda shape: pl.BlockSpec(shape, lambda b, i: (0,) * len(shape))
    mod_shape = jax.ShapeDtypeStruct((nb, 1, D), F32)
    return pl.pallas_call(
        body, name="mix_in_bwd", grid=(nb, n_i),
        out_shape=[jax.ShapeDtypeStruct((nb, s, D), F32), jax.ShapeDtypeStruct((nb, s, 4 * WA), BF16),
                   jax.ShapeDtypeStruct((nb, s, D), BF16), jax.ShapeDtypeStruct((1, D), F32), mod_shape, mod_shape,
                   jax.ShapeDtypeStruct((HALO, WB), F32)],
        in_specs=[tok, tok, t5, t5, t5, _halo_next_spec(tm, s), t5, t5, _halo_prev_spec(tm), _halo_prev_spec(tm),
                  _mod_spec(), _mod_spec(), _row_spec(), VMEM_FULL, full((HALO, WB))],
        out_specs=[tok, _tok_specs(tm, 4 * WA), tok, _row_spec(), _mod_spec(), _mod_spec(), full((HALO, WB))],
        scratch_shapes=[pltpu.VMEM((_ext_rows(tm), WB), F32), pltpu.VMEM((SHIFTS, tm + HALO, WB), F32),
                        pltpu.VMEM((tm, WB), F32)],
        compiler_params=_cparams(),
    )(dxo, x, du, dv, dconv, dconv, a, g, a, g, sh, sc, g_pre, w_mi4, conv_w)


def _row_tile(rows, cols):
    best = 16
    for t in range(16, rows + 1, 16):
        if rows % t == 0 and t * cols * 4 <= 1536 * 1024:
            best = t
    return best


def _sum4(name, own4, recv, j_arr):
    _, rows, cols = own4.shape
    tr = _row_tile(rows, cols)

    def body(j_ref, own_ref, recv_ref, o_ref):
        del j_ref
        acc = own_ref[0]
        for k in range(3):
            acc = acc + recv_ref[k].astype(F32)
        o_ref[...] = acc

    return pl.pallas_call(
        body, name=name,
        grid_spec=pltpu.PrefetchScalarGridSpec(
            num_scalar_prefetch=1, grid=(rows // tr,),
            in_specs=[pl.BlockSpec((1, tr, cols), lambda i, j: (j[0], i, 0)),
                      pl.BlockSpec((3, tr, cols), lambda i, j: (0, i, 0))],
            out_specs=pl.BlockSpec((tr, cols), lambda i, j: (i, 0))),
        out_shape=jax.ShapeDtypeStruct((rows, cols), F32),
        compiler_params=_cparams(),
    )(j_arr, own4, recv)


def _pair_exchange(name, arrs):
    n = len(arrs)

    def plan(x, y, c, ins, outs):
        sends = []
        for a in range(n):
            rows = arrs[a].shape[1] // 2
            theirs = pl.ds(pl.multiple_of((1 - c) * rows, 16), rows)
            sends.append((ins[a].at[:, theirs], outs[a], (x, y, 1 - c), outs[a]))
        return [], sends

    shapes = [jax.ShapeDtypeStruct((a.shape[0], a.shape[1] // 2, a.shape[2]), a.dtype) for a in arrs]
    return _run_exchange(name, arrs, shapes, plan, 0, n)


def _pair_sum(name, g32, recv, c_arr):
    nblk, rows, cols = recv.shape
    tr = _row_tile(rows, cols)
    nh = rows // tr

    def body(c_ref, g_ref, r_ref, o32_ref, obf_ref):
        del c_ref
        val = g_ref[0] + r_ref[0].astype(F32)
        o32_ref[0] = val
        obf_ref[0] = val.astype(BF16)

    spec = pl.BlockSpec((1, tr, cols), lambda k, i, c: (k, i, 0))
    return pl.pallas_call(
        body, name=name,
        grid_spec=pltpu.PrefetchScalarGridSpec(
            num_scalar_prefetch=1, grid=(nblk, nh),
            in_specs=[pl.BlockSpec((1, tr, cols), lambda k, i, c: (k, c[0] * nh + i, 0)), spec],
            out_specs=[spec, spec]),
        out_shape=[jax.ShapeDtypeStruct(recv.shape, F32), jax.ShapeDtypeStruct(recv.shape, BF16)],
        compiler_params=_cparams(),
    )(c_arr, g32, recv)


def _adam_halves(name, w, m, v, mine, theirs, c_arr):
    rows, cols = w.shape
    tr = _row_tile(rows // 2, cols)
    nh = (rows // 2) // tr

    def body(c_ref, w_ref, m_ref, v_ref, mine_ref, theirs_ref, g_out, d_out, m_out, v_out):
        here = (pl.program_id(0) // nh) == c_ref[0]
        g = jnp.where(here, mine_ref[...], theirs_ref[...])
        delta, m2, v2 = _adam(w_ref[...], g, m_ref[...], v_ref[...])
        g_out[...] = g
        d_out[...] = delta
        m_out[...] = m2
        v_out[...] = v2

    spec = pl.BlockSpec((tr, cols), lambda i, c: (i, 0))
    shape = jax.ShapeDtypeStruct((rows, cols), F32)
    return pl.pallas_call(
        body, name=name,
        grid_spec=pltpu.PrefetchScalarGridSpec(
            num_scalar_prefetch=1, grid=(2 * nh,),
            in_specs=[spec, spec, spec,
                      pl.BlockSpec((tr, cols), lambda i, c: (jnp.clip(i - c[0] * nh, 0, nh - 1), 0)),
                      pl.BlockSpec((tr, cols), lambda i, c: (jnp.clip(i - (1 - c[0]) * nh, 0, nh - 1), 0))],
            out_specs=[spec] * 4),
        out_shape=[shape] * 4,
        compiler_params=_cparams(),
    )(c_arr, w, m, v, mine, theirs)


def _adam_big(name, w, m, v, ga, gb):
    rows, cols = w.shape
    tr = _row_tile(rows, cols)

    def body(w_ref, m_ref, v_ref, ga_ref, gb_ref, g_out, d_out, m_out, v_out):
        gsum = ga_ref[...] + gb_ref[...]
        delta, m2, v2 = _adam(w_ref[...], gsum, m_ref[...], v_ref[...])
        g_out[...] = gsum
        d_out[...] = delta
        m_out[...] = m2
        v_out[...] = v2

    spec = pl.BlockSpec((tr, cols), lambda i: (i, 0))
    shape = jax.ShapeDtypeStruct((rows, cols), F32)
    return pl.pallas_call(
        body, name=name, grid=(rows // tr,), out_shape=[shape] * 4,
        in_specs=[spec] * 5, out_specs=[spec] * 4, compiler_params=_cparams(),
    )(w, m, v, ga, gb)


PK_VEC = 0
PK_LOSS = 6
PK_PAIR = 8
PK_BSP = 16
PK_WCAT = 24
PK_ROWS = PK_WCAT + CH
PAIR_ORDER = ("gmlp_norm_g", "gmlp_norm_b", "conv_b", "conv_norm_g", "conv_norm_b", "g_out_a", "g_out_b")
VEC_ORDER = ("g_pre_f1", "g_post_f1", "g_pre_m", "g_post_m", "g_pre_f2", "g_post_f2")


def _pack_late(rows):
    counts = [r.shape[0] for r in rows]
    assert sum(counts) == 8

    def body(*refs):
        o_ref = refs[-1]
        at = 0
        for r, cnt in zip(refs[:-1], counts):
            o_ref[at:at + cnt, :] = r[...]
            at += cnt

    return pl.pallas_call(
        body, name="pack_late", out_shape=jax.ShapeDtypeStruct((8, D), F32),
        in_specs=[VMEM_FULL] * len(rows), out_specs=VMEM_FULL, compiler_params=_cparams(),
    )(*rows)


def _pack_small(vecs, pairs, dbsp, dwcat, lsum):
    def body(*refs):
        vec_refs = refs[:4]
        pair_refs = refs[4:11]
        dbsp_ref, dwcat_ref, lsum_ref, o_ref = refs[11:]
        o_ref[0:PK_WCAT, :] = jnp.zeros((PK_WCAT, D), F32)
        o_ref[PK_LOSS:PK_LOSS + 1, 0:128] = lsum_ref[...]
        for k, r in enumerate(vec_refs):
            o_ref[PK_VEC + 2 + k:PK_VEC + 3 + k, :] = r[...]
        for k, r in enumerate(pair_refs):
            row, half = PK_PAIR + k // 2, k % 2
            o_ref[row:row + 1, half * WA:(half + 1) * WA] = r[...]
        o_ref[PK_BSP:PK_BSP + NH, 0:CH] = dbsp_ref[...]
        o_ref[PK_WCAT:PK_ROWS, :] = dwcat_ref[...]

    args = list(vecs) + list(pairs) + [dbsp, dwcat, lsum]
    return pl.pallas_call(
        body, name="pack_small", out_shape=jax.ShapeDtypeStruct((PK_ROWS, D), F32),
        in_specs=[VMEM_FULL] * len(args), out_specs=VMEM_FULL, compiler_params=_cparams(),
    )(*args)


def _small_adam(pack_all, late_all, dcw_all, dada_all, params):
    names = list(VEC_ORDER) + list(PAIR_ORDER) + ["b_spatial", "w_spatial", "conv_w", "b_ada"]
    flat = []
    for nm in names:
        flat += list(params[nm])
    n_in = 4 + len(flat)

    def body(*refs):
        pack_ref, late_ref, dcw_ref, dada_ref = refs[:4]
        prm = refs[4:n_in]
        outs = refs[n_in:]

        def total(r0, nr, c0, nc):
            acc = pack_ref[0, r0:r0 + nr, c0:c0 + nc]
            for d in range(1, NDEV):
                acc = acc + pack_ref[d, r0:r0 + nr, c0:c0 + nc]
            return acc

        def emit(idx, g, getw, put):
            w_ref, m_ref, v_ref = prm[3 * idx:3 * idx + 3]
            delta, m2, v2 = _adam(getw(w_ref), g, getw(m_ref), getw(v_ref))
            for o_ref, val in zip(outs[4 * idx:4 * idx + 4], (g, delta, m2, v2)):
                put(o_ref, val)

        def whole(ref):
            return ref[...]

        def put_whole(ref, val):
            ref[...] = val

        idx = 0
        for k in range(6):
            if k < 2:
                g = late_ref[0, k:k + 1, :]
                for d in range(1, NDEV):
                    g = g + late_ref[d, k:k + 1, :]
            else:
                g = total(PK_VEC + k, 1, 0, D)
            emit(idx, g, whole, put_whole)
            idx += 1
        for k in range(7):
            emit(idx, total(PK_PAIR + k // 2, 1, (k % 2) * WA, WA), whole, put_whole)
            idx += 1
        emit(idx, total(PK_BSP, NH, 0, CH), lambda r: r[0], lambda r, val: r.__setitem__(0, val))
        idx += 1
        row = lax.broadcasted_iota(jnp.int32, (CH, CH), 0)
        col = lax.broadcasted_iota(jnp.int32, (CH, CH), 1)
        for h in range(NH):
            gh = jnp.where(col <= row, total(PK_WCAT, CH, h * CH, CH), 0.0)
            w_ref, m_ref, v_ref = prm[3 * idx:3 * idx + 3]
            delta, m2, v2 = _adam(w_ref[0, h], gh, m_ref[0, h], v_ref[0, h])
            for o_ref, val in zip(outs[4 * idx:4 * idx + 4], (gh, delta, m2, v2)):
                o_ref[0, h] = val
        idx += 1
        gcw = dcw_ref[0, 0:CK, :]
        for d in range(1, NDEV):
            gcw = gcw + dcw_ref[d, 0:CK, :]
        emit(idx, gcw, lambda r: r[0], lambda r, val: r.__setitem__(0, val))
        idx += 1
        emit(idx, jnp.sum(dada_ref[...], axis=0, keepdims=True), whole, put_whole)
        outs[-1][...] = jnp.sum(total(PK_LOSS, 1, 0, 128), axis=1, keepdims=True) * (0.5 / D)

    out_shape = []
    for nm in names:
        w = params[nm][0]
        out_shape += [jax.ShapeDtypeStruct(w.shape, F32)] * 4
    out_shape.append(jax.ShapeDtypeStruct((1, 1), F32))
    res = pl.pallas_call(
        body, name="small_adam", out_shape=out_shape,
        in_specs=[VMEM_FULL] * n_in, out_specs=[VMEM_FULL] * len(out_shape), compiler_params=_cparams(),
    )(pack_all, late_all, dcw_all, dada_all, *flat)
    return {nm: tuple(res[4 * k:4 * k + 4]) for k, nm in enumerate(names)}, res[-1].reshape(())


WEIGHTS = ['w_ada', 'b_ada', 'g_pre_f1', 'g_post_f1', 'w_f1_in', 'w_f1_out', 'g_pre_m', 'g_post_m', 'w_mix_in',
           'gmlp_norm_g', 'gmlp_norm_b', 'w_spatial', 'b_spatial', 'conv_w', 'conv_b', 'conv_norm_g', 'conv_norm_b',
           'g_out_a', 'g_out_b', 'w_mix_out', 'g_pre_f2', 'g_post_f2', 'w_f2_in', 'w_f2_out']
BIG = ('w_f1_in', 'w_f1_out', 'w_mix_in', 'w_mix_out', 'w_f2_in', 'w_f2_out')


def kernel(x, c, w_ada, b_ada, g_pre_f1, g_post_f1, w_f1_in, w_f1_out, g_pre_m, g_post_m, w_mix_in, gmlp_norm_g, gmlp_norm_b, w_spatial, b_spatial, conv_w, conv_b, conv_norm_g, conv_norm_b, g_out_a, g_out_b, w_mix_out, g_pre_f2, g_post_f2, w_f2_in, w_f2_out, loss_target, m_w_ada, m_b_ada, m_g_pre_f1, m_g_post_f1, m_w_f1_in, m_w_f1_out, m_g_pre_m, m_g_post_m, m_w_mix_in, m_gmlp_norm_g, m_gmlp_norm_b, m_w_spatial, m_b_spatial, m_conv_w, m_conv_b, m_conv_norm_g, m_conv_norm_b, m_g_out_a, m_g_out_b, m_w_mix_out, m_g_pre_f2, m_g_post_f2, m_w_f2_in, m_w_f2_out, v_w_ada, v_b_ada, v_g_pre_f1, v_g_post_f1, v_w_f1_in, v_w_f1_out, v_g_pre_m, v_g_post_m, v_w_mix_in, v_gmlp_norm_g, v_gmlp_norm_b, v_w_spatial, v_b_spatial, v_conv_w, v_conv_b, v_conv_norm_g, v_conv_norm_b, v_g_out_a, v_g_out_b, v_w_mix_out, v_g_pre_f2, v_g_post_f2, v_w_f2_in, v_w_f2_out):
    env = dict(locals())
    wts = {n: env[n] for n in WEIGHTS}
    mom = {n: env["m_" + n] for n in WEIGHTS}
    var = {n: env["v_" + n] for n in WEIGHTS}
    nb, s, _ = x.shape
    t = nb * s
    ax, ay, ac = lax.axis_index("x"), lax.axis_index("y"), lax.axis_index("c")
    j_chip = 2 * ax + ay
    dev = 4 * ax + 2 * ay + ac
    j_arr = j_chip.reshape(1).astype(jnp.int32)

    groups = (("w_f1_in", "w_f1_out"), ("w_mix_in", "w_mix_out"), ("w_f2_in", "w_f2_out"))
    def gather_operands(gi):
        srcs = [wts[n][0].astype(BF16) for n in groups[gi]] + ([conv_w[0]] if gi == 1 else [])
        lands = [lax.dynamic_update_index_in_dim(lax.empty((NCHIP,) + a.shape, a.dtype), a, j_chip, 0) for a in srcs]
        return srcs, lands

    def gather_start(gi, behind, operands=None):
        srcs, lands = operands or gather_operands(gi)
        plan_a, plan_b, n_b = _gather_plans([a.shape for a in srcs])
        ssem, rsem, srcs, lands, token = _split_start("gw_start%d" % gi, srcs, lands, plan_a, 3 * len(srcs), behind)
        gather[gi] = (srcs, lands, ssem, rsem, plan_a, plan_b, n_b)
        return token

    def gather_forward(gi, behind):
        srcs, lands, ssem, rsem, plan_a, plan_b, n_b = gather[gi]
        ssem, rsem, lands, token = _split_forward("gw_fwd%d" % gi, srcs, lands, ssem, rsem, plan_a, plan_b, n_b, behind)
        gather[gi] = (lands, ssem, rsem, plan_b)
        return token

    def gathered(gi, behind):
        lands, ssem, rsem, plan_b = gather[gi]
        return _split_wait("gw_wait%d" % gi, [], lands, ssem, rsem, plan_b, behind)

    gather = {}
    (c_all8,) = _allgather8("gather_c", [c.reshape(8, (nb * D) // 8)])
    token = gather_start(0, c_all8)
    c_all = c_all8.reshape(NDEV * nb, D) + token[0, 0]
    b_sh = lax.dynamic_slice(b_ada, (0, j_chip * ADA_SH), (1, ADA_SH))
    ada_sh = _ada_fwd(c_all, w_ada[0], b_sh)
    later = [gather_operands(1), gather_operands(2)]
    (ada4,) = _chip_allgather("gather_ada", [ada_sh], behind=[a for pair in later for arrs in pair for a in arrs])
    token = gather_forward(0, ada4)
    token = gather_start(1, token, later[0])
    token = gather_start(2, token, later[1])
    ada4 = ada4 + token[0:1, 0:1]
    ada_me = lax.dynamic_slice(ada4, (0, dev * nb, 0), (NCHIP, nb, ADA_SH))
    ada_me = jnp.transpose(ada_me, (1, 0, 2)).reshape(nb, NMOD * D)
    sh1, sc1, gt1, sh2, sc2, gt2, sh3, sc3, gt3 = [ada_me[:, k * D:(k + 1) * D].reshape(nb, 1, D) for k in range(NMOD)]

    wcat = jnp.transpose(w_spatial[0], (1, 0, 2)).reshape(CH, NH * CH)
    wcat_t = jnp.transpose(w_spatial[0], (0, 2, 1)).reshape(NH * CH, CH)
    bspt = jnp.repeat(b_spatial[0].T, HD, axis=1)

    w1i, w1o = gathered(0, sh1)
    w1o = w1o.reshape(DFF, D)
    x1, f1, p1 = _ffn_fwd(x, sh1, sc1, gt1, g_pre_f1, g_post_f1, w1i, w1o)
    wmi, wmo, cw4 = gathered(1, gather_forward(1, x1))
    wmo = wmo.reshape(D, D)
    cw_full = jnp.transpose(cw4, (1, 0, 2)).reshape(CK, WB)
    cw_pad = jnp.pad(cw_full, ((0, HALO - CK), (0, 0)))
    u, v, a, g = _mix_in_fwd(x1, sh2, sc2, g_pre_m, wmi)
    x2, conv, yb, m = _mix_mid_fwd(x1, u, v, a, g, gt2, gmlp_norm_g, gmlp_norm_b, wcat, bspt, cw_pad, conv_b,
                                   conv_norm_g, conv_norm_b, g_out_a, g_out_b, wmo, g_post_m)
    w2i, w2o = gathered(2, gather_forward(2, x2))
    w2o = w2o.reshape(DFF, D)
    dx3, df2, p2, lsum, dg_post_f2, dgt3 = _ffn_fwd(x2, sh3, sc3, gt3, g_pre_f2, g_post_f2, w2i, w2o, target=loss_target)

    def chip4(pair, rows):
        return [arr.reshape(NCHIP, rows, arr.shape[-1]) for arr in pair]

    def scatter_start(tag, pairs, behind):
        srcs = [p[1] for p in pairs]
        lands = [lax.empty((3,) + a.shape[1:], a.dtype) for a in srcs]
        ssem, rsem, srcs, lands, token = _split_start("gs_start_" + tag, srcs, lands, _scatter_plan(len(srcs)),
                                                      3 * len(srcs), behind)
        return (srcs, lands, ssem, rsem), token

    def scatter_wait(tag, state, behind):
        srcs, lands, ssem, rsem = state
        return _split_wait("gs_wait_" + tag, srcs, lands, ssem, rsem, _scatter_plan(len(srcs)), behind)

    def allgather_start(tag, arrs, behind):
        lands = [lax.dynamic_update_index_in_dim(lax.empty((NDEV,) + a.shape, a.dtype), a, dev, 0) for a in arrs]
        ssem, rsem, srcs, lands, token = _split_start("small_start_" + tag, arrs, lands, _allgather_plan(len(arrs)),
                                                      7 * len(arrs), behind)
        return (srcs, lands, ssem, rsem), token

    def allgather_wait(tag, state, behind):
        srcs, lands, ssem, rsem = state
        return _split_wait("small_wait_" + tag, srcs, lands, ssem, rsem, _allgather_plan(len(srcs)), behind)

    out = {}
    dx2, dp2, h3, a2, dg_pre_f2, dsh3, dsc3 = _ffn_bwd(
        dx3, x2, None, p2, sh3, sc3, gt3, g_pre_f2, g_post_f2, w2i, w2o, df=df2)
    gw2i = _wgrad("wgrad_f2_in", h3.reshape(t, D), dp2.reshape(t, 2 * DFF), 2 * DFF // NCHIP, True)
    gw2o = chip4(_wgrad("wgrad_f2_out", a2.reshape(t, DFF), df2.reshape(t, D), D // 2, False), DFF // NCHIP)
    scat_f2, tok = scatter_start("f2", [gw2i, gw2o], dg_post_f2)
    dy, dm, dg_post_m, dgt2 = _mix_out_bwd(dx2, m, gt2 + tok[0, 0], g_post_m, wmo)
    gwmo = chip4(_wgrad("wgrad_mix_out", yb.reshape(t, D), dm.reshape(t, D), D // 2, False), D // NCHIP)
    (du, dv, dconv, dwcat, dbsp, dgn_g, dgn_b, dgo_a, dgo_b, dcn_g, dcn_b, dcb) = _mix_mid_bwd(
        dy, u, v, conv, gmlp_norm_g, gmlp_norm_b, wcat, wcat_t, bspt, conv_norm_g, conv_norm_b, g_out_a, g_out_b)
    dx1, dproj, h2, dg_pre_m, dsh2, dsc2, dcw = _mix_in_bwd(dx2, x1, du, dv, dconv, a, g, sh2, sc2, g_pre_m, wmi, cw_pad)
    gwmi = _wgrad("wgrad_mix_in", h2.reshape(t, D), dproj.reshape(t, 4 * WA), WA, True)
    scat_mix, tok = scatter_start("mix", [gwmi, gwmo], dg_pre_m)

    vec_grads = dict(g_pre_m=dg_pre_m, g_post_m=dg_post_m, g_pre_f2=dg_pre_f2, g_post_f2=dg_post_f2)
    pair_grads = dict(gmlp_norm_g=dgn_g, gmlp_norm_b=dgn_b, conv_b=dcb, conv_norm_g=dcn_g, conv_norm_b=dcn_b,
                      g_out_a=dgo_a, g_out_b=dgo_b)
    pack = _pack_small([vec_grads[n] for n in VEC_ORDER[2:]], [pair_grads[n] for n in PAIR_ORDER], dbsp, dwcat, lsum)
    dada_early = jnp.concatenate([q.reshape(nb, D) for q in (dsh2, dsc2, dgt2, dsh3, dsc3, dgt3)], axis=1)
    early, tok2 = allgather_start("early", [pack, dcw, dada_early.reshape(8, (nb * 6 * D) // 8)], tok)
    grad_x, dp1, h1, a1, df1, dg_pre_f1, dg_post_f1, dsh1, dsc1, dgt1 = _ffn_bwd(
        dx1, x, f1, p1, sh1 + tok2[0, 0], sc1, gt1, g_pre_f1, g_post_f1, w1i, w1o)
    late_pack = _pack_late([dg_pre_f1, dg_post_f1] + [q.reshape(nb, D) for q in (dsh1, dsc1, dgt1)])
    late, tok2 = allgather_start("late", [late_pack], dg_post_f1)
    gw1i = _wgrad("wgrad_f1_in", h1.reshape(t, D), dp1.reshape(t, 2 * DFF), 2 * DFF // NCHIP, True)
    gw1o = chip4(_wgrad("wgrad_f1_out", a1.reshape(t, DFF), df1.reshape(t, D), D // 2, False), DFF // NCHIP)
    c_arr = ac.reshape(1).astype(jnp.int32)
    sib = _pair_exchange("pair_f1", [gw1i[1], gw1o[1]])
    pair_i = _pair_sum("pairsum_f1_in", gw1i[0], sib[0], c_arr)
    pair_o = _pair_sum("pairsum_f1_out", gw1o[0], sib[1], c_arr)
    scat_f1, tok = scatter_start("f1", [pair_i, pair_o], tok2)

    def reduce_and_update(tag, state, names, pairs, behind):
        recv = scatter_wait(tag, state, behind)
        part = [_sum4("sum4_" + n, pairs[k][0], recv[k], j_arr) for k, n in enumerate(names)]
        other = _sibling_swap("swap_" + tag, part)
        for k, n in enumerate(names):
            out[n] = tuple(r[None] for r in _adam_big("adam_" + n, wts[n][0], mom[n][0], var[n][0], part[k], other[k]))

    reduce_and_update("f2", scat_f2, ("w_f2_in", "w_f2_out"), [gw2i, gw2o], tok)
    reduce_and_update("mix", scat_mix, ("w_mix_in", "w_mix_out"), [gwmi, gwmo], out["w_f2_out"][3])

    pack_all, dcw_all, dada_early8 = allgather_wait("early", early, out["w_mix_out"][3])
    (late_all,) = allgather_wait("late", late, pack_all)
    dada_late = jnp.transpose(late_all[:, 2:8, :].reshape(NDEV, 3, nb, D), (0, 2, 1, 3)).reshape(NDEV * nb, 3 * D)
    dada_all = jnp.concatenate([dada_late, dada_early8.reshape(NDEV * nb, 6 * D)], axis=1)
    dcw_mine = lax.dynamic_slice(dcw_all, (0, 0, j_chip * (WB // NCHIP)), (NDEV, HALO, WB // NCHIP))
    small = {n: (wts[n], mom[n], var[n]) for n in list(VEC_ORDER) + list(PAIR_ORDER) + ["b_spatial", "w_spatial", "conv_w", "b_ada"]}
    small_out, loss = _small_adam(pack_all, late_all, dcw_mine, dada_all, small)
    out.update(small_out)
    dada_sh = lax.dynamic_slice(dada_all, (0, j_chip * ADA_SH), (NDEV * nb, ADA_SH))
    out["w_ada"] = tuple(r[None] for r in _ada_bwd_adam(c_all, dada_sh, w_ada[0], m_w_ada[0], v_w_ada[0]))
    recv = scatter_wait("f1", scat_f1, out["w_ada"][3])
    names = ("w_f1_in", "w_f1_out")
    mine = [_sum4("sum4_" + n, p[0], recv[k], j_arr)
            for k, (n, p) in enumerate(zip(names, (pair_i, pair_o)))]
    theirs = _sibling_swap("swap_f1", mine)
    for k, n in enumerate(names):
        out[n] = tuple(r[None] for r in _adam_halves("adam_" + n, wts[n][0], mom[n][0], var[n][0], mine[k], theirs[k],
                                                     c_arr))

    res = [loss, grad_x]
    for k in range(4):
        res += [out[n][k] for n in WEIGHTS]
    return tuple(res)
```

```python
import functools

import jax
import jax.numpy as jnp
from jax import lax
from jax.experimental import pallas as pl
from jax.experimental.pallas import tpu as pltpu

D = 1024
DFF = 2816
WA = 512
WB = 512
NH = 8
HD = 64
CH = 128
CK = 31
HALO = 32
NMOD = 9
EPS = 1e-6
NCHIP = 4
NDEV = 8
FBLK = DFF // 2
ADA_SH = NMOD * D // NCHIP

LR, B1, B2, EPS_A, WD, STEP = 0.001, 0.9, 0.999, 1e-08, 0.01, 10

F32 = jnp.float32
BF16 = jnp.bfloat16
MESH = pl.DeviceIdType.MESH
ANY = pl.BlockSpec(memory_space=pl.ANY)
VMEM_FULL = pl.BlockSpec(memory_space=pltpu.VMEM)
VMEM_LIMIT = 56 * 1024 * 1024

NT = (((1,), (1,)), ((), ()))
TN = (((0,), (0,)), ((), ()))


def _dot(a, b):
    return jnp.dot(a, b, preferred_element_type=F32)


def _dot_nt(a, b):
    return lax.dot_general(a, b, NT, preferred_element_type=F32)


def _dot_tn(a, b):
    return lax.dot_general(a, b, TN, preferred_element_type=F32)


def _cparams():
    return pltpu.CompilerParams(vmem_limit_bytes=VMEM_LIMIT)


def _allgather8(name, arrs):
    n = len(arrs)

    def body(*refs):
        ins, outs = refs[:n], refs[n:2 * n]
        send_sems, recv_sems, local_sems = refs[2 * n:]
        x, y, c = lax.axis_index("x"), lax.axis_index("y"), lax.axis_index("c")
        me, sibling = (x, y, c), (x, y, 1 - c)
        chips = [(1 - x, y), (x, 1 - y), (1 - x, 1 - y)]

        def copy(a, k, block, to, src=None):
            rows = outs[a].at[4 * block[0] + 2 * block[1] + block[2]]
            return pltpu.make_async_remote_copy(
                src_ref=rows if src is None else src, dst_ref=rows,
                send_sem=send_sems.at[a, k], recv_sem=recv_sems.at[a, k],
                device_id=to, device_id_type=MESH)

        started, mine = [], []
        for a in range(n):
            loc = pltpu.make_async_copy(ins[a], outs[a].at[4 * x + 2 * y + c], local_sems.at[a])
            loc.start()
            mine.append(loc)
            first = [copy(a, 0, me, sibling, src=ins[a])]
            first += [copy(a, 1 + j, me, (*chip, c), src=ins[a]) for j, chip in enumerate(chips)]
            for cp in first:
                cp.start()
            started += first
        for a in range(n):
            for j, chip in enumerate(chips):
                copy(a, 1 + j, (*chip, c), me).wait_recv()
                fwd = copy(a, 4 + j, (*chip, c), sibling)
                fwd.start()
                started.append(fwd)
        for a in range(n):
            copy(a, 0, sibling, me).wait_recv()
            for j, chip in enumerate(chips):
                copy(a, 4 + j, (*chip, 1 - c), me).wait_recv()
        for cp in started:
            cp.wait_send()
        for loc in mine:
            loc.wait()

    return pl.pallas_call(
        body, name=name,
        out_shape=[jax.ShapeDtypeStruct((NDEV,) + a.shape, a.dtype) for a in arrs],
        in_specs=[ANY] * n, out_specs=[ANY] * n,
        scratch_shapes=[pltpu.SemaphoreType.DMA((n, 7)), pltpu.SemaphoreType.DMA((n, 7)),
                        pltpu.SemaphoreType.DMA((n,))],
    )(*arrs)


def _chip_relations(x, y):
    return [(1 - x, y), (x, 1 - y), (1 - x, 1 - y)]


def _exchange(name, arrs, out_shapes, plan):
    n = len(arrs)
    n_out = len(out_shapes)

    def body(*refs):
        ins, outs = refs[:n], refs[n:n + n_out]
        send_sems, recv_sems, local_sems = refs[n + n_out:]
        x, y, c = lax.axis_index("x"), lax.axis_index("y"), lax.axis_index("c")
        local, sends = plan(x, y, c, ins, outs)
        locs = [pltpu.make_async_copy(s, d, local_sems.at[i]) for i, (s, d) in enumerate(local)]
        for loc in locs:
            loc.start()
        cps = [pltpu.make_async_remote_copy(src_ref=s, dst_ref=d, send_sem=send_sems.at[i], recv_sem=recv_sems.at[i],
                                            device_id=peer, device_id_type=MESH)
               for i, (s, d, peer, _) in enumerate(sends)]
        for cp in cps:
            cp.start()
        for i, (s, _, peer, landing) in enumerate(sends):
            pltpu.make_async_remote_copy(src_ref=s, dst_ref=landing, send_sem=send_sems.at[i], recv_sem=recv_sems.at[i],
                                         device_id=peer, device_id_type=MESH).wait_recv()
        for cp in cps:
            cp.wait_send()
        for loc in locs:
            loc.wait()

    return n, n_out, body


def _run_exchange(name, arrs, out_shapes, plan, n_local, n_send):
    n, n_out, body = _exchange(name, arrs, out_shapes, plan)
    return pl.pallas_call(
        body, name=name, out_shape=out_shapes,
        in_specs=[ANY] * n, out_specs=[ANY] * n_out,
        scratch_shapes=[pltpu.SemaphoreType.DMA((n_send,)), pltpu.SemaphoreType.DMA((n_send,)),
                        pltpu.SemaphoreType.DMA((max(n_local, 1),))],
    )(*arrs)


def _chip_allgather(name, arrs, behind=()):
    n = len(arrs)

    def plan(x, y, c, ins, outs):
        j_me = 2 * x + y
        local = [(ins[a], outs[a].at[j_me]) for a in range(n)]
        sends = []
        for a in range(n):
            for (px, py) in _chip_relations(x, y):
                sends.append((ins[a], outs[a].at[j_me], (px, py, c), outs[a].at[2 * px + py]))
        return local, sends

    shapes = [jax.ShapeDtypeStruct((NCHIP,) + a.shape, a.dtype) for a in arrs]
    return _run_exchange(name, list(arrs) + list(behind), shapes, plan, n, 3 * n)


HBM = pl.BlockSpec(memory_space=pltpu.HBM)
SEM = pl.BlockSpec(memory_space=pltpu.SEMAPHORE)
EFFECT = pltpu.SideEffectType.DATAFLOW_SIDE_EFFECTING


def _split_start(name, srcs, lands, plan, n_send, after):
    n, nl = len(srcs), len(lands)

    def body(*refs):
        src, land = refs[:n], refs[n:n + nl]
        send_sems, recv_sems = refs[n + nl + 1], refs[n + nl + 2]
        token = refs[-2]
        local_sems = refs[-1]
        x, y, c = lax.axis_index("x"), lax.axis_index("y"), lax.axis_index("c")
        local, sends = plan(x, y, c, src, land)
        locs = [pltpu.make_async_copy(s, d, local_sems.at[i]) for i, (s, d) in enumerate(local)]
        for loc in locs:
            loc.start()
        for loc in locs:
            loc.wait()
        for i, (s, d, peer, _) in enumerate(sends):
            pltpu.make_async_remote_copy(src_ref=s, dst_ref=d, send_sem=send_sems.at[i], recv_sem=recv_sems.at[i],
                                         device_id=peer, device_id_type=MESH).start()
        token[...] = jnp.zeros_like(token)

    thru = [pltpu.HBM(a.shape, a.dtype) for a in list(srcs) + list(lands)]
    res = pl.pallas_call(
        body, name=name,
        out_shape=(pltpu.SemaphoreType.DMA((n_send,)), pltpu.SemaphoreType.DMA((n_send,)), *thru,
                   jax.ShapeDtypeStruct((8, 128), F32)),
        in_specs=[HBM] * (n + nl) + [ANY],
        out_specs=(SEM, SEM, *([HBM] * (n + nl)), pl.BlockSpec(memory_space=pltpu.VMEM)),
        input_output_aliases={i: 2 + i for i in range(n + nl)},
        scratch_shapes=[pltpu.SemaphoreType.DMA((max(len(srcs), 1),))],
        compiler_params=pltpu.CompilerParams(has_side_effects=EFFECT),
    )(*[pltpu.with_memory_space_constraint(a, pltpu.HBM) for a in list(srcs) + list(lands)], after)
    return res[0], res[1], list(res[2:2 + n]), list(res[2 + n:2 + n + nl]), res[-1]


def _split_wait(name, srcs, lands, send_sems, recv_sems, plan, after):
    n, nl = len(srcs), len(lands)

    def body(*refs):
        src, land = refs[:n], refs[n:n + nl]
        send_sems, recv_sems = refs[n + nl], refs[n + nl + 1]
        x, y, c = lax.axis_index("x"), lax.axis_index("y"), lax.axis_index("c")
        _, sends = plan(x, y, c, src, land)
        for i, (s, _, peer, landing) in enumerate(sends):
            cp = pltpu.make_async_remote_copy(src_ref=s, dst_ref=landing, send_sem=send_sems.at[i],
                                              recv_sem=recv_sems.at[i], device_id=peer, device_id_type=MESH)
            cp.wait_send()
            cp.wait_recv()

    thru = [pltpu.HBM(a.shape, a.dtype) for a in list(srcs) + list(lands)]
    res = pl.pallas_call(
        body, name=name, out_shape=tuple(thru),
        in_specs=[HBM] * (n + nl) + [SEM, SEM, ANY], out_specs=tuple([HBM] * (n + nl)),
        input_output_aliases={i: i for i in range(n + nl)},
        compiler_params=pltpu.CompilerParams(has_side_effects=EFFECT),
    )(*srcs, *lands, send_sems, recv_sems, after)
    return list(res[n:])


def _split_forward(name, srcs, lands, send_a, recv_a, plan_a, plan_b, n_b, after):
    n, nl = len(srcs), len(lands)

    def body(*refs):
        src, land = refs[:n], refs[n:n + nl]
        send_a, recv_a = refs[n + nl], refs[n + nl + 1]
        send_b, recv_b = refs[n + nl + 3], refs[n + nl + 4]
        token = refs[-1]
        x, y, c = lax.axis_index("x"), lax.axis_index("y"), lax.axis_index("c")
        _, first = plan_a(x, y, c, src, land)
        for i, (s, _, peer, landing) in enumerate(first):
            cp = pltpu.make_async_remote_copy(src_ref=s, dst_ref=landing, send_sem=send_a.at[i],
                                              recv_sem=recv_a.at[i], device_id=peer, device_id_type=MESH)
            cp.wait_send()
            cp.wait_recv()
        _, second = plan_b(x, y, c, src, land)
        for i, (s, d, peer, _) in enumerate(second):
            pltpu.make_async_remote_copy(src_ref=s, dst_ref=d, send_sem=send_b.at[i], recv_sem=recv_b.at[i],
                                         device_id=peer, device_id_type=MESH).start()
        token[...] = jnp.zeros_like(token)

    thru = [pltpu.HBM(a.shape, a.dtype) for a in lands]
    res = pl.pallas_call(
        body, name=name,
        out_shape=(pltpu.SemaphoreType.DMA((n_b,)), pltpu.SemaphoreType.DMA((n_b,)), *thru,
                   jax.ShapeDtypeStruct((8, 128), F32)),
        in_specs=[HBM] * (n + nl) + [SEM, SEM, ANY],
        out_specs=(SEM, SEM, *([HBM] * nl), pl.BlockSpec(memory_space=pltpu.VMEM)),
        input_output_aliases={n + i: 2 + i for i in range(nl)},
        compiler_params=pltpu.CompilerParams(has_side_effects=EFFECT),
    )(*srcs, *lands, send_a, recv_a, after)
    return res[0], res[1], list(res[2:2 + nl]), res[-1]


def _gather_plans(shapes):
    n = len(shapes)

    def halves(a, c):
        rows = shapes[a][0] // 2
        return pl.ds(pl.multiple_of(c * rows, 16), rows), pl.ds(pl.multiple_of((1 - c) * rows, 16), rows)

    def split(a):
        return shapes[a][0] % 32 == 0

    def plan_a(x, y, c, src, land):
        j_me = 2 * x + y
        sends = []
        for a in range(n):
            for (px, py) in _chip_relations(x, y):
                if split(a):
                    mine, _ = halves(a, c)
                    sends.append((src[a].at[mine], land[a].at[j_me, mine], (px, py, c), land[a].at[2 * px + py, mine]))
                else:
                    sends.append((src[a], land[a].at[j_me], (px, py, c), land[a].at[2 * px + py]))
        return [], sends

    def plan_b(x, y, c, src, land):
        sends = []
        for a in range(n):
            if split(a):
                mine, other = halves(a, c)
                for (px, py) in _chip_relations(x, y):
                    j = 2 * px + py
                    sends.append((land[a].at[j, mine], land[a].at[j, mine], (x, y, 1 - c), land[a].at[j, other]))
        return [], sends

    n_b = 3 * sum(1 for a in range(n) if split(a))
    return plan_a, plan_b, n_b


def _allgather_plan(n):
    flips = [(dx, dy, dc) for dx in (0, 1) for dy in (0, 1) for dc in (0, 1) if dx or dy or dc]

    def plan(x, y, c, src, land):
        sends = []
        for a in range(n):
            for dx, dy, dc in flips:
                px, py, pc = x ^ dx, y ^ dy, c ^ dc
                sends.append((src[a], land[a].at[4 * x + 2 * y + c], (px, py, pc), land[a].at[4 * px + 2 * py + pc]))
        return [], sends

    return plan


def _scatter_plan(n):
    def plan(x, y, c, src, land):
        sends = []
        for a in range(n):
            for k, (px, py) in enumerate(_chip_relations(x, y)):
                sends.append((src[a].at[2 * px + py], land[a].at[k], (px, py, c), land[a].at[k]))
        return [], sends

    return plan


def _rms(x):
    r = lax.rsqrt(jnp.mean(x * x, axis=-1, keepdims=True) + EPS)
    return x * r, r


def _rms_bwd(dy, n, r, g):
    dg = jnp.sum(dy * n, axis=0, keepdims=True)
    dn = dy * g
    dx = r * (dn - n * jnp.mean(dn * n, axis=-1, keepdims=True))
    return dx, dg


def _ln(x):
    mu = jnp.mean(x, axis=-1, keepdims=True)
    xc = x - mu
    rstd = lax.rsqrt(jnp.mean(xc * xc, axis=-1, keepdims=True) + EPS)
    return xc * rstd, rstd


def _ln_bwd(dy, xhat, rstd, g):
    dg = jnp.sum(dy * xhat, axis=0, keepdims=True)
    db = jnp.sum(dy, axis=0, keepdims=True)
    dxh = dy * g
    dx = rstd * (dxh - jnp.mean(dxh, axis=-1, keepdims=True) - xhat * jnp.mean(dxh * xhat, axis=-1, keepdims=True))
    return dx, dg, db


def _sigmoid(x):
    return jax.nn.sigmoid(x)


def _dsilu(x, s):
    return s * (1.0 + x * (1.0 - s))


def _adam(w, g, m, v):
    m = B1 * m + (1.0 - B1) * g
    v = B2 * v + (1.0 - B2) * (g * g)
    m_hat = m / (1.0 - B1 ** STEP)
    v_hat = v / (1.0 - B2 ** STEP)
    delta = -LR * (m_hat / (jnp.sqrt(v_hat) + EPS_A) + WD * w)
    return delta, m, v


def _head_mask(shape):
    lane = lax.broadcasted_iota(jnp.int32, shape, len(shape) - 1)
    return [(lane >= h * HD) & (lane < (h + 1) * HD) for h in range(NH)]


def _first(b, i):
    return jnp.logical_and(b == 0, i == 0)


def _acc(ref, val, first):
    @pl.when(first)
    def _():
        ref[...] = val

    @pl.when(jnp.logical_not(first))
    def _():
        ref[...] += val


def _ada_fwd(c_all, w_sh, b_sh):
    nb = c_all.shape[0]
    tn = 768

    def body(c_ref, w_ref, b_ref, o_ref):
        cv = c_ref[...]
        cs = (cv * _sigmoid(cv)).astype(BF16)
        o_ref[...] = _dot(cs, w_ref[...].astype(BF16)) + b_ref[...]

    return pl.pallas_call(
        body, name="ada_fwd", grid=(ADA_SH // tn,),
        out_shape=jax.ShapeDtypeStruct((nb, ADA_SH), F32),
        in_specs=[pl.BlockSpec((nb, D), lambda j: (0, 0)), pl.BlockSpec((D, tn), lambda j: (0, j)),
                  pl.BlockSpec((1, tn), lambda j: (0, j))],
        out_specs=pl.BlockSpec((nb, tn), lambda j: (0, j)),
        compiler_params=_cparams(),
    )(c_all, w_sh, b_sh)


def _ada_bwd_adam(c_all, dada_sh, w, m, v):
    nb = c_all.shape[0]
    tn = 768

    def body(c_ref, d_ref, w_ref, m_ref, v_ref, g_out, d_out, m_out, v_out):
        cv = c_ref[...]
        cs = (cv * _sigmoid(cv)).astype(BF16)
        g = _dot_tn(cs, d_ref[...].astype(BF16))
        delta, m2, v2 = _adam(w_ref[...], g, m_ref[...], v_ref[...])
        g_out[...] = g
        d_out[...] = delta
        m_out[...] = m2
        v_out[...] = v2

    big = pl.BlockSpec((D, tn), lambda j: (0, j))
    shape = jax.ShapeDtypeStruct((D, ADA_SH), F32)
    return pl.pallas_call(
        body, name="ada_bwd_adam", grid=(ADA_SH // tn,),
        out_shape=[shape] * 4,
        in_specs=[pl.BlockSpec((nb, D), lambda j: (0, 0)), pl.BlockSpec((nb, tn), lambda j: (0, j)), big, big, big],
        out_specs=[big] * 4,
        compiler_params=_cparams(),
    )(c_all, dada_sh, w, m, v)


def _tok_specs(tm, width):
    return pl.BlockSpec((1, tm, width), lambda b, i: (b, i, 0))


def _mod_spec():
    return pl.BlockSpec((1, 1, D), lambda b, i: (b, 0, 0))


def _row_spec(width=D):
    return pl.BlockSpec((1, width), lambda b, i: (0, 0))


def _ffn_fwd(x, sh, sc, gt, g_pre, g_post, w_in4, w_out, target=None):
    nb, s, _ = x.shape
    tm = min(512, s)
    with_loss = target is not None

    def body(*refs):
        if with_loss:
            (x_ref, sh_ref, sc_ref, gt_ref, gpre_ref, gpost_ref, win_ref, wout_ref, tgt_ref,
             xo_ref, df_ref, p_ref, ls_ref, dgpost_ref, dgt_ref) = refs
        else:
            (x_ref, sh_ref, sc_ref, gt_ref, gpre_ref, gpost_ref, win_ref, wout_ref,
             xo_ref, f_ref, p_ref) = refs
        xv = x_ref[0]
        n, _ = _rms(xv)
        h = (n * gpre_ref[...]) * (1.0 + sc_ref[0]) + sh_ref[0]
        hb = h.astype(BF16)
        acc = jnp.zeros((tm, D), F32)
        for j in range(2):
            gate = _dot(hb, win_ref[j])
            up = _dot(hb, win_ref[2 + j])
            p_ref[0, :, j * FBLK:(j + 1) * FBLK] = gate.astype(BF16)
            p_ref[0, :, DFF + j * FBLK:DFF + (j + 1) * FBLK] = up.astype(BF16)
            a = (gate * _sigmoid(gate)) * up
            acc = acc + _dot(a.astype(BF16), wout_ref[j * FBLK:(j + 1) * FBLK, :])
        nf, q = _rms(acc)
        gpost = gpost_ref[...]
        half_gate = 0.5 * gt_ref[0]
        out = xv + half_gate * (nf * gpost)
        if with_loss:
            first = _first(pl.program_id(0), pl.program_id(1))
            err = out - tgt_ref[0]
            dout = err * (1.0 / D)
            xo_ref[0] = dout
            row = jnp.sum(err * err, axis=0, keepdims=True)
            part = row[:, 0:128]
            for k in range(1, D // 128):
                part = part + row[:, k * 128:(k + 1) * 128]
            _acc(ls_ref, part, first)
            df, dgpost = _rms_bwd(dout * half_gate, nf, q, gpost)
            df_ref[0] = df.astype(BF16)
            _acc(dgpost_ref, dgpost, first)
            _acc(dgt_ref, jnp.sum(dout * (0.5 * (nf * gpost)), axis=0, keepdims=True)[None], pl.program_id(1) == 0)
        else:
            f_ref[0] = acc
            xo_ref[0] = out

    in_specs = [_tok_specs(tm, D), _mod_spec(), _mod_spec(), _mod_spec(), _row_spec(), _row_spec(), VMEM_FULL, VMEM_FULL]
    args = [x, sh, sc, gt, g_pre, g_post, w_in4, w_out]
    out_shape = [jax.ShapeDtypeStruct((nb, s, D), F32), jax.ShapeDtypeStruct((nb, s, D), BF16 if with_loss else F32),
                 jax.ShapeDtypeStruct((nb, s, 2 * DFF), BF16)]
    out_specs = [_tok_specs(tm, D), _tok_specs(tm, D), _tok_specs(tm, 2 * DFF)]
    if with_loss:
        in_specs.append(_tok_specs(tm, D))
        args.append(target)
        out_shape += [jax.ShapeDtypeStruct((1, 128), F32), jax.ShapeDtypeStruct((1, D), F32),
                      jax.ShapeDtypeStruct((nb, 1, D), F32)]
        out_specs += [pl.BlockSpec((1, 128), lambda b, i: (0, 0)), _row_spec(), _mod_spec()]
    return pl.pallas_call(
        body, name="ffn_loss_fwd" if with_loss else "ffn_fwd", grid=(nb, s // tm),
        out_shape=out_shape, in_specs=in_specs, out_specs=out_specs,
        compiler_params=_cparams(),
    )(*args)


def _ffn_bwd(dxo, x, f, p, sh, sc, gt, g_pre, g_post, w_in4, w_out, df=None):
    nb, s, _ = x.shape
    tm = min(256, s)
    given = df is not None

    def body(*refs):
        if given:
            (dxo_ref, x_ref, dfin_ref, p_ref, sh_ref, sc_ref, gpre_ref, win_ref, wout_ref,
             dx_ref, dp_ref, h_ref, a_ref, dgpre_ref, dsh_ref, dsc_ref) = refs
        else:
            (dxo_ref, x_ref, f_ref, p_ref, sh_ref, sc_ref, gt_ref, gpre_ref, gpost_ref, win_ref, wout_ref,
             dx_ref, dp_ref, h_ref, a_ref, df_ref, dgpre_ref, dgpost_ref, dsh_ref, dsc_ref, dgt_ref) = refs
        b, i = pl.program_id(0), pl.program_id(1)
        dxo_v = dxo_ref[0]
        if given:
            dfb = dfin_ref[0]
        else:
            nf, q = _rms(f_ref[0])
            gpost = gpost_ref[...]
            dgt = jnp.sum(dxo_v * (0.5 * (nf * gpost)), axis=0, keepdims=True)
            do = dxo_v * (0.5 * gt_ref[0])
            dfv, dgpost = _rms_bwd(do, nf, q, gpost)
            dfb = dfv.astype(BF16)
            df_ref[0] = dfb
        xv = x_ref[0]
        n, r = _rms(xv)
        gpre = gpre_ref[...]
        ng = n * gpre
        scale1 = 1.0 + sc_ref[0]
        h = ng * scale1 + sh_ref[0]
        h_ref[0] = h.astype(BF16)
        dh = jnp.zeros((tm, D), F32)
        for j in range(2):
            gate = p_ref[0, :, j * FBLK:(j + 1) * FBLK].astype(F32)
            up = p_ref[0, :, DFF + j * FBLK:DFF + (j + 1) * FBLK].astype(F32)
            sg = _sigmoid(gate)
            act = gate * sg
            a_ref[0, :, j * FBLK:(j + 1) * FBLK] = (act * up).astype(BF16)
            da = _dot_nt(dfb, wout_ref[j * FBLK:(j + 1) * FBLK, :])
            dgate = (da * up * _dsilu(gate, sg)).astype(BF16)
            dup = (da * act).astype(BF16)
            dp_ref[0, :, j * FBLK:(j + 1) * FBLK] = dgate
            dp_ref[0, :, DFF + j * FBLK:DFF + (j + 1) * FBLK] = dup
            dh = dh + _dot_nt(dgate, win_ref[j]) + _dot_nt(dup, win_ref[2 + j])
        dsh = jnp.sum(dh, axis=0, keepdims=True)
        dsc = jnp.sum(dh * ng, axis=0, keepdims=True)
        dxn, dgpre = _rms_bwd(dh * scale1, n, r, gpre)
        dx_ref[0] = dxo_v + dxn
        _acc(dgpre_ref, dgpre, _first(b, i))
        _acc(dsh_ref, dsh[None], i == 0)
        _acc(dsc_ref, dsc[None], i == 0)
        if not given:
            _acc(dgpost_ref, dgpost, _first(b, i))
            _acc(dgt_ref, dgt[None], i == 0)

    tok = _tok_specs(tm, D)
    mod_shape = jax.ShapeDtypeStruct((nb, 1, D), F32)
    row_shape = jax.ShapeDtypeStruct((1, D), F32)
    big = [jax.ShapeDtypeStruct((nb, s, D), F32), jax.ShapeDtypeStruct((nb, s, 2 * DFF), BF16),
           jax.ShapeDtypeStruct((nb, s, D), BF16), jax.ShapeDtypeStruct((nb, s, DFF), BF16)]
    big_specs = [tok, _tok_specs(tm, 2 * DFF), tok, _tok_specs(tm, DFF)]
    if given:
        return pl.pallas_call(
            body, name="ffn_bwd_after_loss", grid=(nb, s // tm),
            out_shape=big + [row_shape, mod_shape, mod_shape],
            in_specs=[tok, tok, tok, _tok_specs(tm, 2 * DFF), _mod_spec(), _mod_spec(), _row_spec(), VMEM_FULL, VMEM_FULL],
            out_specs=big_specs + [_row_spec(), _mod_spec(), _mod_spec()],
            compiler_params=_cparams(),
        )(dxo, x, df, p, sh, sc, g_pre, w_in4, w_out)
    return pl.pallas_call(
        body, name="ffn_bwd", grid=(nb, s // tm),
        out_shape=big + [jax.ShapeDtypeStruct((nb, s, D), BF16), row_shape, row_shape, mod_shape, mod_shape, mod_shape],
        in_specs=[tok, tok, tok, _tok_specs(tm, 2 * DFF), _mod_spec(), _mod_spec(), _mod_spec(), _row_spec(), _row_spec(),
                  VMEM_FULL, VMEM_FULL],
        out_specs=big_specs + [tok, _row_spec(), _row_spec(), _mod_spec(), _mod_spec(), _mod_spec()],
        compiler_params=_cparams(),
    )(dxo, x, f, p, sh, sc, gt, g_pre, g_post, w_in4, w_out)


def _wgrad(name, a, b, col_block, chip_major):
    t, ka = a.shape
    n = b.shape[1]
    tk = min(t, 512)
    while tk * 2 <= t and t % (tk * 2) == 0 and 2 * (tk * 2) * max(ka, col_block) <= 6 * 1024 * 1024:
        tk *= 2
    nk = t // tk
    nblk = n // col_block

    def body(a_ref, b_ref, o_ref, obf_ref, acc_ref):
        k = pl.program_id(1)

        @pl.when(k == 0)
        def _():
            acc_ref[...] = jnp.zeros_like(acc_ref)

        acc_ref[...] += _dot_tn(a_ref[...], b_ref[...])

        @pl.when(k == nk - 1)
        def _():
            val = acc_ref[...]
            if chip_major:
                o_ref[0] = val
                obf_ref[0] = val.astype(BF16)
            else:
                o_ref[...] = val
                obf_ref[...] = val.astype(BF16)

    if chip_major:
        shape = (nblk, ka, col_block)
        ospec = pl.BlockSpec((1, ka, col_block), lambda j, k: (j, 0, 0))
    else:
        shape = (ka, n)
        ospec = pl.BlockSpec((ka, col_block), lambda j, k: (0, j))
    return pl.pallas_call(
        body, name=name, grid=(nblk, nk),
        out_shape=[jax.ShapeDtypeStruct(shape, F32), jax.ShapeDtypeStruct(shape, BF16)],
        in_specs=[pl.BlockSpec((tk, ka), lambda j, k: (k, 0)), pl.BlockSpec((tk, col_block), lambda j, k: (k, j))],
        out_specs=[ospec, ospec],
        scratch_shapes=[pltpu.VMEM((ka, col_block), F32)],
        compiler_params=_cparams(),
    )(a, b)


def _mix_in_fwd(x, sh, sc, g_pre, w_mi4):
    nb, s, _ = x.shape
    tm = min(512, s)

    def body(x_ref, sh_ref, sc_ref, gpre_ref, w_ref, u_ref, v_ref, a_ref, g_ref):
        n, _ = _rms(x_ref[0])
        hb = ((n * gpre_ref[...]) * (1.0 + sc_ref[0]) + sh_ref[0]).astype(BF16)
        for k, o_ref in enumerate((u_ref, v_ref, a_ref, g_ref)):
            o_ref[0] = _dot(hb, w_ref[k])

    shape = jax.ShapeDtypeStruct((nb, s, WA), F32)
    return pl.pallas_call(
        body, name="mix_in_fwd", grid=(nb, s // tm),
        out_shape=[shape] * 4,
        in_specs=[_tok_specs(tm, D), _mod_spec(), _mod_spec(), _row_spec(), VMEM_FULL],
        out_specs=[_tok_specs(tm, WA)] * 4,
        compiler_params=_cparams(),
    )(x, sh, sc, g_pre, w_mi4)


def _spatial_weights(wcat_ref, transposed):
    w = wcat_ref[...]
    row = lax.broadcasted_iota(jnp.int32, w.shape, 0)
    col = lax.broadcasted_iota(jnp.int32, w.shape, 1)
    keep = ((row & (CH - 1)) <= col) if transposed else ((col & (CH - 1)) <= row)
    return jnp.where(keep, w, 0.0).astype(BF16)


def _expand_heads(vc, masks):
    return jnp.concatenate([jnp.where(mk, vc, jnp.zeros_like(vc)) for mk in masks], axis=0)


def _spatial_bias(bspt_ref):
    return bspt_ref[...]


SHIFTS = 8
TAP_ROWS = 32


def _ext_rows(tm):
    return tm + HALO + SHIFTS


def _make_shifts(ext_ref, sh_ref, tm):
    ext_ref[tm + HALO:tm + HALO + SHIFTS, :] = jnp.zeros((SHIFTS, WB), F32)
    for r in range(SHIFTS):
        sh_ref[r] = ext_ref[r:r + tm + HALO, :]


def _conv_taps(sh_ref, w_ref, tm, taps, emit):
    def block(i, carry):
        r0 = pl.multiple_of(i * TAP_ROWS, TAP_ROWS)
        acc = jnp.zeros((TAP_ROWS, WB), F32)
        for o, k in taps:
            acc = acc + w_ref[k:k + 1, :] * sh_ref[o % SHIFTS, pl.ds(r0 + SHIFTS * (o // SHIFTS), TAP_ROWS), :]
        emit(r0, acc)
        return carry

    lax.fori_loop(0, tm // TAP_ROWS, block, 0)


def _halo_prev_spec(tm):
    return pl.BlockSpec((1, HALO, WB), lambda b, i: (b, jnp.maximum(i * (tm // HALO) - 1, 0), 0))


def _halo_next_spec(tm, s):
    return pl.BlockSpec((1, HALO, WB), lambda b, i: (b, jnp.minimum((i + 1) * (tm // HALO), s // HALO - 1), 0))


def _mix_mid_fwd(x, u, v, a, g, gt, gn_g, gn_b, wcat, bspt, conv_w, conv_b, cn_g, cn_b, go_a, go_b, w_mo, g_post):
    nb, s, _ = x.shape
    tm = min(512, s)

    def body(x_ref, u_ref, v_ref, a_ref, g_ref, ah_ref, gh_ref, gt_ref, gng_ref, gnb_ref, wcat_ref, bspt_ref,
             cw_ref, cb_ref, cng_ref, cnb_ref, goa_ref, gob_ref, wmo_ref, gpost_ref,
             xo_ref, conv_ref, y_ref, m_ref, ext_ref, sh_ref):
        i = pl.program_id(1)
        xhat, _ = _ln(v_ref[0])
        vb = (xhat * gng_ref[...] + gnb_ref[...]).astype(BF16)
        wsb = _spatial_weights(wcat_ref, False)
        bias = _spatial_bias(bspt_ref)
        masks = _head_mask((CH, WA))
        zs = []
        for cidx in range(tm // CH):
            vexp = _expand_heads(vb[cidx * CH:(cidx + 1) * CH, :], masks)
            zs.append(_dot(wsb, vexp) + bias)
        z = jnp.concatenate(zs, axis=0)
        na, _ = _rms(u_ref[0] * z)
        keep = jnp.where(i == 0, 0.0, 1.0).astype(F32)
        ext_ref[0:HALO, :] = (ah_ref[0] * _sigmoid(gh_ref[0])) * keep
        ext_ref[HALO:HALO + tm, :] = a_ref[0] * _sigmoid(g_ref[0])
        _make_shifts(ext_ref, sh_ref, tm)
        cb = cb_ref[...]

        def put_conv(r0, acc):
            conv_ref[0, pl.ds(r0, TAP_ROWS), :] = acc + cb

        _conv_taps(sh_ref, cw_ref, tm, [(k + HALO - (CK - 1), k) for k in range(CK)], put_conv)
        conv = conv_ref[0]
        chat, _ = _ln(conv)
        cln = chat * cng_ref[...] + cnb_ref[...]
        nbb, _ = _rms(cln * _sigmoid(cln))
        yb = jnp.concatenate([na * goa_ref[...], nbb * gob_ref[...]], axis=1).astype(BF16)
        y_ref[0] = yb
        m = _dot(yb, wmo_ref[...])
        m_ref[0] = m
        nm, _ = _rms(m)
        xo_ref[0] = x_ref[0] + gt_ref[0] * (nm * gpost_ref[...])

    t5 = _tok_specs(tm, WA)
    tok = _tok_specs(tm, D)
    r5 = _row_spec(WA)
    full = lambda shape: pl.BlockSpec(shape, lambda b, i: (0,) * len(shape))
    return pl.pallas_call(
        body, name="mix_mid_fwd", grid=(nb, s // tm),
        out_shape=[jax.ShapeDtypeStruct((nb, s, D), F32), jax.ShapeDtypeStruct((nb, s, WB), F32),
                   jax.ShapeDtypeStruct((nb, s, D), BF16), jax.ShapeDtypeStruct((nb, s, D), F32)],
        in_specs=[tok, t5, t5, t5, t5, _halo_prev_spec(tm), _halo_prev_spec(tm), _mod_spec(), r5, r5,
                  full((CH, NH * CH)), full((CH, WA)), full((HALO, WB)), r5, r5, r5, r5, r5, VMEM_FULL, _row_spec()],
        out_specs=[tok, t5, tok, tok],
        scratch_shapes=[pltpu.VMEM((_ext_rows(tm), WB), F32), pltpu.VMEM((SHIFTS, tm + HALO, WB), F32)],
        compiler_params=_cparams(),
    )(x, u, v, a, g, a, g, gt, gn_g, gn_b, wcat, bspt, conv_w, conv_b, cn_g, cn_b, go_a, go_b, w_mo, g_post)


def _mix_out_bwd(dxo, m, gt, g_post, w_mo):
    nb, s, _ = m.shape
    tm = min(512, s)

    def body(dxo_ref, m_ref, gt_ref, gpost_ref, wmo_ref, dy_ref, dm_ref, dgpost_ref, dgt_ref):
        b, i = pl.program_id(0), pl.program_id(1)
        dxo_v = dxo_ref[0]
        nm, q = _rms(m_ref[0])
        gpost = gpost_ref[...]
        dgt = jnp.sum(dxo_v * (nm * gpost), axis=0, keepdims=True)
        dm, dgpost = _rms_bwd(dxo_v * gt_ref[0], nm, q, gpost)
        dmb = dm.astype(BF16)
        dm_ref[0] = dmb
        dy_ref[0] = _dot_nt(dmb, wmo_ref[...])
        _acc(dgpost_ref, dgpost, _first(b, i))
        _acc(dgt_ref, dgt[None], i == 0)

    tok = _tok_specs(tm, D)
    return pl.pallas_call(
        body, name="mix_out_bwd", grid=(nb, s // tm),
        out_shape=[jax.ShapeDtypeStruct((nb, s, D), F32), jax.ShapeDtypeStruct((nb, s, D), BF16),
                   jax.ShapeDtypeStruct((1, D), F32), jax.ShapeDtypeStruct((nb, 1, D), F32)],
        in_specs=[tok, tok, _mod_spec(), _row_spec(), VMEM_FULL],
        out_specs=[tok, tok, _row_spec(), _mod_spec()],
        compiler_params=_cparams(),
    )(dxo, m, gt, g_post, w_mo)


def _mix_mid_bwd(dy, u, v, conv, gn_g, gn_b, wcat, wcat_t, bspt, cn_g, cn_b, go_a, go_b):
    nb, s, _ = dy.shape
    tm = min(512, s)
    nchunk = tm // CH

    def body(dy_ref, u_ref, v_ref, conv_ref, gng_ref, gnb_ref, wcat_ref, wcatt_ref, bspt_ref, cng_ref, cnb_ref,
             goa_ref, gob_ref,
             du_ref, dv_ref, dconv_ref, dwcat_ref, dbsp_ref, dgng_ref, dgnb_ref, dgoa_ref, dgob_ref,
             dcng_ref, dcnb_ref, dcb_ref):
        first = _first(pl.program_id(0), pl.program_id(1))
        dyv = dy_ref[0]
        xhat, rstd = _ln(v_ref[0])
        gng = gng_ref[...]
        vb = (xhat * gng + gnb_ref[...]).astype(BF16)
        wsb = _spatial_weights(wcat_ref, False)
        wsb_t = _spatial_weights(wcatt_ref, True)
        bias = _spatial_bias(bspt_ref)
        masks = _head_mask((CH, WA))
        vexps, zs = [], []
        for cidx in range(nchunk):
            vexp = _expand_heads(vb[cidx * CH:(cidx + 1) * CH, :], masks)
            vexps.append(vexp)
            zs.append(_dot(wsb, vexp) + bias)
        z = jnp.concatenate(zs, axis=0)
        uv = u_ref[0]
        na, ra = _rms(uv * z)
        dya, dgoa = _rms_bwd(dyv[:, 0:WA], na, ra, goa_ref[...])
        du_ref[0] = dya * z
        dz = dya * uv
        dwcat = jnp.zeros((CH, NH * CH), F32)
        dzsum = jnp.zeros((CH, WA), F32)
        dvlns = []
        for cidx in range(nchunk):
            dzc = dz[cidx * CH:(cidx + 1) * CH, :]
            dzsum = dzsum + dzc
            dzb = dzc.astype(BF16)
            dwcat = dwcat + _dot_nt(dzb, vexps[cidx])
            dvexp = _dot(wsb_t, dzb)
            dvl = jnp.zeros((CH, WA), F32)
            for h in range(NH):
                dvl = dvl + jnp.where(masks[h], dvexp[h * CH:(h + 1) * CH, :], 0.0)
            dvlns.append(dvl)
        dvln = jnp.concatenate(dvlns, axis=0)
        dv, dgng, dgnb = _ln_bwd(dvln, xhat, rstd, gng)
        dv_ref[0] = dv
        lane = lax.broadcasted_iota(jnp.int32, (NH, WA), 1)
        head = lax.broadcasted_iota(jnp.int32, (NH, WA), 0)
        sel = jnp.where((lane >= head * HD) & (lane < (head + 1) * HD), 1.0, 0.0).astype(F32)
        dbsp = lax.dot_general(sel, dzsum, NT, preferred_element_type=F32, precision=lax.Precision.HIGHEST)
        chat, crstd = _ln(conv_ref[0])
        cng = cng_ref[...]
        cln = chat * cng + cnb_ref[...]
        sg = _sigmoid(cln)
        nbb, rb = _rms(cln * sg)
        dyb, dgob = _rms_bwd(dyv[:, WA:D], nbb, rb, gob_ref[...])
        dconv, dcng, dcnb = _ln_bwd(dyb * _dsilu(cln, sg), chat, crstd, cng)
        dconv_ref[0] = dconv
        dcb = jnp.sum(dconv, axis=0, keepdims=True)
        for ref, val in ((dwcat_ref, dwcat), (dbsp_ref, dbsp), (dgng_ref, dgng), (dgnb_ref, dgnb), (dgoa_ref, dgoa),
                         (dgob_ref, dgob), (dcng_ref, dcng), (dcnb_ref, dcnb), (dcb_ref, dcb)):
            _acc(ref, val, first)

    t5 = _tok_specs(tm, WA)
    r5 = _row_spec(WA)
    full = lambda shape: pl.BlockSpec(shape, lambda b, i: (0,) * len(shape))
    big = jax.ShapeDtypeStruct((nb, s, WA), F32)
    row = jax.ShapeDtypeStruct((1, WA), F32)
    return pl.pallas_call(
        body, name="mix_mid_bwd", grid=(nb, s // tm),
        out_shape=[big, big, big, jax.ShapeDtypeStruct((CH, NH * CH), F32), jax.ShapeDtypeStruct((NH, CH), F32),
                   row, row, row, row, row, row, row],
        in_specs=[_tok_specs(tm, D), t5, t5, t5, r5, r5, full((CH, NH * CH)), full((NH * CH, CH)), full((CH, WA)),
                  r5, r5, r5, r5],
        out_specs=[t5, t5, t5, full((CH, NH * CH)), full((NH, CH)), r5, r5, r5, r5, r5, r5, r5],
        compiler_params=_cparams(),
    )(dy, u, v, conv, gn_g, gn_b, wcat, wcat_t, bspt, cn_g, cn_b, go_a, go_b)


def _mix_in_bwd(dxo, x, du, dv, dconv, a, g, sh, sc, g_pre, w_mi4, conv_w):
    nb, s, _ = x.shape
    tm = min(512, s)
    n_i = s // tm

    def body(dxo_ref, x_ref, du_ref, dv_ref, dc_ref, dch_ref, a_ref, g_ref, ah_ref, gh_ref, sh_ref, sc_ref,
             gpre_ref, w_ref, cw_ref,
             dx_ref, dproj_ref, h_ref, dgpre_ref, dsh_ref, dsc_ref, dcw_ref, ext_ref, shf_ref, dglu_ref):
        b, i = pl.program_id(0), pl.program_id(1)
        first = _first(b, i)
        av, gv = a_ref[0], g_ref[0]
        sg = _sigmoid(gv)
        dconv = dc_ref[0]
        ext_ref[0:tm, :] = dconv
        ext_ref[tm:tm + HALO, :] = dch_ref[0] * jnp.where(i == n_i - 1, 0.0, 1.0).astype(F32)
        _make_shifts(ext_ref, shf_ref, tm)

        def put_dglu(r0, acc):
            dglu_ref[pl.ds(r0, TAP_ROWS), :] = acc

        _conv_taps(shf_ref, cw_ref, tm, [(CK - 1 - k, k) for k in range(CK)], put_dglu)
        dglu = dglu_ref[...]
        ext_ref[0:HALO, :] = (ah_ref[0] * _sigmoid(gh_ref[0])) * jnp.where(i == 0, 0.0, 1.0).astype(F32)
        ext_ref[HALO:HALO + tm, :] = av * sg
        _make_shifts(ext_ref, shf_ref, tm)

        @pl.when(first)
        def _():
            dcw_ref[...] = jnp.zeros((HALO, WB), F32)

        for k in range(CK):
            o = k + HALO - (CK - 1)
            lo = SHIFTS * (o // SHIFTS)
            dcw_ref[k:k + 1, :] += jnp.sum(dconv * shf_ref[o % SHIFTS, lo:lo + tm, :], axis=0, keepdims=True)
        da = dglu * sg
        dg = dglu * av * (sg * (1.0 - sg))
        parts = [du_ref[0].astype(BF16), dv_ref[0].astype(BF16), da.astype(BF16), dg.astype(BF16)]
        dh = jnp.zeros((tm, D), F32)
        for k in range(4):
            dproj_ref[0, :, k * WA:(k + 1) * WA] = parts[k]
            dh = dh + _dot_nt(parts[k], w_ref[k])
        n, r = _rms(x_ref[0])
        gpre = gpre_ref[...]
        ng = n * gpre
        scale1 = 1.0 + sc_ref[0]
        h_ref[0] = (ng * scale1 + sh_ref[0]).astype(BF16)
        dsh = jnp.sum(dh, axis=0, keepdims=True)
        dsc = jnp.sum(dh * ng, axis=0, keepdims=True)
        dxn, dgpre = _rms_bwd(dh * scale1, n, r, gpre)
        dx_ref[0] = dxo_ref[0] + dxn
        _acc(dgpre_ref, dgpre, first)
        _acc(dsh_ref, dsh[None], i == 0)
        _acc(dsc_ref, dsc[None], i == 0)

    tok = _tok_specs(tm, D)
    t5 = _tok_specs(tm, WA)
    full = lambda shape: pl.BlockSpec(shape, lambda b, i: (0,) * len(shape))
    mod_shape = jax.ShapeDtypeStruct((nb, 1, D), F32)
    return pl.pallas_call(
        body, name="mix_in_bwd", grid=(nb, n_i),
        out_shape=[jax.ShapeDtypeStruct((nb, s, D), F32), jax.ShapeDtypeStruct((nb, s, 4 * WA), BF16),
                   jax.ShapeDtypeStruct((nb, s, D), BF16), jax.ShapeDtypeStruct((1, D), F32), mod_shape, mod_shape,
                   jax.ShapeDtypeStruct((HALO, WB), F32)],
        in_specs=[tok, tok, t5, t5, t5, _halo_next_spec(tm, s), t5, t5, _halo_prev_spec(tm), _halo_prev_spec(tm),
                  _mod_spec(), _mod_spec(), _row_spec(), VMEM_FULL, full((HALO, WB))],
        out_specs=[tok, _tok_specs(tm, 4 * WA), tok, _row_spec(), _mod_spec(), _mod_spec(), full((HALO, WB))],
        scratch_shapes=[pltpu.VMEM((_ext_rows(tm), WB), F32), pltpu.VMEM((SHIFTS, tm + HALO, WB), F32),
                        pltpu.VMEM((tm, WB), F32)],
        compiler_params=_cparams(),
    )(dxo, x, du, dv, dconv, dconv, a, g, a, g, sh, sc, g_pre, w_mi4, conv_w)


def _row_tile(rows, cols):
    best = 16
    for t in range(16, rows + 1, 16):
        if rows % t == 0 and t * cols * 4 <= 1536 * 1024:
            best = t
    return best


def _sum4(name, own4, recv, j_arr):
    _, rows, cols = own4.shape
    tr = _row_tile(rows, cols)

    def body(j_ref, own_ref, recv_ref, o_ref):
        del j_ref
        acc = own_ref[0]
        for k in range(3):
            acc = acc + recv_ref[k].astype(F32)
        o_ref[...] = acc

    return pl.pallas_call(
        body, name=name,
        grid_spec=pltpu.PrefetchScalarGridSpec(
            num_scalar_prefetch=1, grid=(rows // tr,),
            in_specs=[pl.BlockSpec((1, tr, cols), lambda i, j: (j[0], i, 0)),
                      pl.BlockSpec((3, tr, cols), lambda i, j: (0, i, 0))],
            out_specs=pl.BlockSpec((tr, cols), lambda i, j: (i, 0))),
        out_shape=jax.ShapeDtypeStruct((rows, cols), F32),
        compiler_params=_cparams(),
    )(j_arr, own4, recv)


def _pair_plan(shapes):
    def plan(x, y, c, src, land):
        sends = []
        for a, shape in enumerate(shapes):
            rows = shape[1] // 2
            theirs = pl.ds(pl.multiple_of((1 - c) * rows, 16), rows)
            sends.append((src[a].at[:, theirs], land[a], (x, y, 1 - c), land[a]))
        return [], sends

    return plan


def _swap_plan(n):
    def plan(x, y, c, src, land):
        return [], [(src[a], land[a], (x, y, 1 - c), land[a]) for a in range(n)]

    return plan


def _pair_sum(name, g32, recv, c_arr):
    nblk, rows, cols = recv.shape
    tr = _row_tile(rows, cols)
    nh = rows // tr

    def body(c_ref, g_ref, r_ref, o32_ref, obf_ref):
        del c_ref
        val = g_ref[0] + r_ref[0].astype(F32)
        o32_ref[0] = val
        obf_ref[0] = val.astype(BF16)

    spec = pl.BlockSpec((1, tr, cols), lambda k, i, c: (k, i, 0))
    return pl.pallas_call(
        body, name=name,
        grid_spec=pltpu.PrefetchScalarGridSpec(
            num_scalar_prefetch=1, grid=(nblk, nh),
            in_specs=[pl.BlockSpec((1, tr, cols), lambda k, i, c: (k, c[0] * nh + i, 0)), spec],
            out_specs=[spec, spec]),
        out_shape=[jax.ShapeDtypeStruct(recv.shape, F32), jax.ShapeDtypeStruct(recv.shape, BF16)],
        compiler_params=_cparams(),
    )(c_arr, g32, recv)


def _adam_halves(name, w, m, v, mine, theirs, c_arr):
    rows, cols = w.shape
    tr = _row_tile(rows // 2, cols)
    nh = (rows // 2) // tr

    def body(c_ref, w_ref, m_ref, v_ref, mine_ref, theirs_ref, g_out, d_out, m_out, v_out):
        here = (pl.program_id(0) // nh) == c_ref[0]
        g = jnp.where(here, mine_ref[...], theirs_ref[...])
        delta, m2, v2 = _adam(w_ref[...], g, m_ref[...], v_ref[...])
        g_out[...] = g
        d_out[...] = delta
        m_out[...] = m2
        v_out[...] = v2

    spec = pl.BlockSpec((tr, cols), lambda i, c: (i, 0))
    shape = jax.ShapeDtypeStruct((rows, cols), F32)
    return pl.pallas_call(
        body, name=name,
        grid_spec=pltpu.PrefetchScalarGridSpec(
            num_scalar_prefetch=1, grid=(2 * nh,),
            in_specs=[spec, spec, spec,
                      pl.BlockSpec((tr, cols), lambda i, c: (jnp.clip(i - c[0] * nh, 0, nh - 1), 0)),
                      pl.BlockSpec((tr, cols), lambda i, c: (jnp.clip(i - (1 - c[0]) * nh, 0, nh - 1), 0))],
            out_specs=[spec] * 4),
        out_shape=[shape] * 4,
        compiler_params=_cparams(),
    )(c_arr, w, m, v, mine, theirs)


def _adam_big(name, w, m, v, ga, gb):
    rows, cols = w.shape
    tr = _row_tile(rows, cols)

    def body(w_ref, m_ref, v_ref, ga_ref, gb_ref, g_out, d_out, m_out, v_out):
        gsum = ga_ref[...] + gb_ref[...]
        delta, m2, v2 = _adam(w_ref[...], gsum, m_ref[...], v_ref[...])
        g_out[...] = gsum
        d_out[...] = delta
        m_out[...] = m2
        v_out[...] = v2

    spec = pl.BlockSpec((tr, cols), lambda i: (i, 0))
    shape = jax.ShapeDtypeStruct((rows, cols), F32)
    return pl.pallas_call(
        body, name=name, grid=(rows // tr,), out_shape=[shape] * 4,
        in_specs=[spec] * 5, out_specs=[spec] * 4, compiler_params=_cparams(),
    )(w, m, v, ga, gb)


PK_VEC = 0
PK_LOSS = 6
PK_PAIR = 8
PK_BSP = 16
PK_WCAT = 24
PK_ROWS = PK_WCAT + CH
PAIR_ORDER = ("gmlp_norm_g", "gmlp_norm_b", "conv_b", "conv_norm_g", "conv_norm_b", "g_out_a", "g_out_b")
VEC_ORDER = ("g_pre_f1", "g_post_f1", "g_pre_m", "g_post_m", "g_pre_f2", "g_post_f2")


def _pack_late(rows):
    counts = [r.shape[0] for r in rows]
    assert sum(counts) == 8

    def body(*refs):
        o_ref = refs[-1]
        at = 0
        for r, cnt in zip(refs[:-1], counts):
            o_ref[at:at + cnt, :] = r[...]
            at += cnt

    return pl.pallas_call(
        body, name="pack_late", out_shape=jax.ShapeDtypeStruct((8, D), F32),
        in_specs=[VMEM_FULL] * len(rows), out_specs=VMEM_FULL, compiler_params=_cparams(),
    )(*rows)


def _pack_small(vecs, pairs, dbsp, dwcat, lsum):
    def body(*refs):
        vec_refs = refs[:4]
        pair_refs = refs[4:11]
        dbsp_ref, dwcat_ref, lsum_ref, o_ref = refs[11:]
        o_ref[0:PK_WCAT, :] = jnp.zeros((PK_WCAT, D), F32)
        o_ref[PK_LOSS:PK_LOSS + 1, 0:128] = lsum_ref[...]
        for k, r in enumerate(vec_refs):
            o_ref[PK_VEC + 2 + k:PK_VEC + 3 + k, :] = r[...]
        for k, r in enumerate(pair_refs):
            row, half = PK_PAIR + k // 2, k % 2
            o_ref[row:row + 1, half * WA:(half + 1) * WA] = r[...]
        o_ref[PK_BSP:PK_BSP + NH, 0:CH] = dbsp_ref[...]
        o_ref[PK_WCAT:PK_ROWS, :] = dwcat_ref[...]

    args = list(vecs) + list(pairs) + [dbsp, dwcat, lsum]
    return pl.pallas_call(
        body, name="pack_small", out_shape=jax.ShapeDtypeStruct((PK_ROWS, D), F32),
        in_specs=[VMEM_FULL] * len(args), out_specs=VMEM_FULL, compiler_params=_cparams(),
    )(*args)


def _small_adam(pack_all, late_all, dcw_all, dada_all, params, behind):
    names = list(VEC_ORDER) + list(PAIR_ORDER) + ["b_spatial", "w_spatial", "conv_w", "b_ada"]
    flat = []
    for nm in names:
        flat += list(params[nm])
    n_in = 4 + len(flat)

    def body(*refs):
        pack_ref, late_ref, dcw_ref, dada_ref = refs[:4]
        prm = refs[4:n_in]
        outs = refs[n_in + 1:]

        def total(r0, nr, c0, nc):
            acc = pack_ref[0, r0:r0 + nr, c0:c0 + nc]
            for d in range(1, NDEV):
                acc = acc + pack_ref[d, r0:r0 + nr, c0:c0 + nc]
            return acc

        def emit(idx, g, getw, put):
            w_ref, m_ref, v_ref = prm[3 * idx:3 * idx + 3]
            delta, m2, v2 = _adam(getw(w_ref), g, getw(m_ref), getw(v_ref))
            for o_ref, val in zip(outs[4 * idx:4 * idx + 4], (g, delta, m2, v2)):
                put(o_ref, val)

        def whole(ref):
            return ref[...]

        def put_whole(ref, val):
            ref[...] = val

        idx = 0
        for k in range(6):
            if k < 2:
                g = late_ref[0, k:k + 1, :]
                for d in range(1, NDEV):
                    g = g + late_ref[d, k:k + 1, :]
            else:
                g = total(PK_VEC + k, 1, 0, D)
            emit(idx, g, whole, put_whole)
            idx += 1
        for k in range(7):
            emit(idx, total(PK_PAIR + k // 2, 1, (k % 2) * WA, WA), whole, put_whole)
            idx += 1
        emit(idx, total(PK_BSP, NH, 0, CH), lambda r: r[0], lambda r, val: r.__setitem__(0, val))
        idx += 1
        row = lax.broadcasted_iota(jnp.int32, (CH, CH), 0)
        col = lax.broadcasted_iota(jnp.int32, (CH, CH), 1)
        for h in range(NH):
            gh = jnp.where(col <= row, total(PK_WCAT, CH, h * CH, CH), 0.0)
            w_ref, m_ref, v_ref = prm[3 * idx:3 * idx + 3]
            delta, m2, v2 = _adam(w_ref[0, h], gh, m_ref[0, h], v_ref[0, h])
            for o_ref, val in zip(outs[4 * idx:4 * idx + 4], (gh, delta, m2, v2)):
                o_ref[0, h] = val
        idx += 1
        gcw = dcw_ref[0, 0:CK, :]
        for d in range(1, NDEV):
            gcw = gcw + dcw_ref[d, 0:CK, :]
        emit(idx, gcw, lambda r: r[0], lambda r, val: r.__setitem__(0, val))
        idx += 1
        emit(idx, jnp.sum(dada_ref[...], axis=0, keepdims=True), whole, put_whole)
        outs[-1][...] = jnp.sum(total(PK_LOSS, 1, 0, 128), axis=1, keepdims=True) * (0.5 / D)

    out_shape = []
    for nm in names:
        w = params[nm][0]
        out_shape += [jax.ShapeDtypeStruct(w.shape, F32)] * 4
    out_shape.append(jax.ShapeDtypeStruct((1, 1), F32))
    res = pl.pallas_call(
        body, name="small_adam", out_shape=out_shape,
        in_specs=[VMEM_FULL] * n_in + [ANY], out_specs=[VMEM_FULL] * len(out_shape), compiler_params=_cparams(),
    )(pack_all, late_all, dcw_all, dada_all, *flat, behind)
    return {nm: tuple(res[4 * k:4 * k + 4]) for k, nm in enumerate(names)}, res[-1].reshape(())


WEIGHTS = ['w_ada', 'b_ada', 'g_pre_f1', 'g_post_f1', 'w_f1_in', 'w_f1_out', 'g_pre_m', 'g_post_m', 'w_mix_in',
           'gmlp_norm_g', 'gmlp_norm_b', 'w_spatial', 'b_spatial', 'conv_w', 'conv_b', 'conv_norm_g', 'conv_norm_b',
           'g_out_a', 'g_out_b', 'w_mix_out', 'g_pre_f2', 'g_post_f2', 'w_f2_in', 'w_f2_out']
BIG = ('w_f1_in', 'w_f1_out', 'w_mix_in', 'w_mix_out', 'w_f2_in', 'w_f2_out')


def kernel(x, c, w_ada, b_ada, g_pre_f1, g_post_f1, w_f1_in, w_f1_out, g_pre_m, g_post_m, w_mix_in, gmlp_norm_g, gmlp_norm_b, w_spatial, b_spatial, conv_w, conv_b, conv_norm_g, conv_norm_b, g_out_a, g_out_b, w_mix_out, g_pre_f2, g_post_f2, w_f2_in, w_f2_out, loss_target, m_w_ada, m_b_ada, m_g_pre_f1, m_g_post_f1, m_w_f1_in, m_w_f1_out, m_g_pre_m, m_g_post_m, m_w_mix_in, m_gmlp_norm_g, m_gmlp_norm_b, m_w_spatial, m_b_spatial, m_conv_w, m_conv_b, m_conv_norm_g, m_conv_norm_b, m_g_out_a, m_g_out_b, m_w_mix_out, m_g_pre_f2, m_g_post_f2, m_w_f2_in, m_w_f2_out, v_w_ada, v_b_ada, v_g_pre_f1, v_g_post_f1, v_w_f1_in, v_w_f1_out, v_g_pre_m, v_g_post_m, v_w_mix_in, v_gmlp_norm_g, v_gmlp_norm_b, v_w_spatial, v_b_spatial, v_conv_w, v_conv_b, v_conv_norm_g, v_conv_norm_b, v_g_out_a, v_g_out_b, v_w_mix_out, v_g_pre_f2, v_g_post_f2, v_w_f2_in, v_w_f2_out):
    env = dict(locals())
    wts = {n: env[n] for n in WEIGHTS}
    mom = {n: env["m_" + n] for n in WEIGHTS}
    var = {n: env["v_" + n] for n in WEIGHTS}
    nb, s, _ = x.shape
    t = nb * s
    ax, ay, ac = lax.axis_index("x"), lax.axis_index("y"), lax.axis_index("c")
    j_chip = 2 * ax + ay
    dev = 4 * ax + 2 * ay + ac
    j_arr = j_chip.reshape(1).astype(jnp.int32)

    groups = (("w_f1_in", "w_f1_out"), ("w_mix_in", "w_mix_out"), ("w_f2_in", "w_f2_out"))
    def gather_operands(gi):
        srcs = [wts[n][0].astype(BF16) for n in groups[gi]] + ([conv_w[0]] if gi == 1 else [])
        lands = [lax.dynamic_update_index_in_dim(lax.empty((NCHIP,) + a.shape, a.dtype), a, j_chip, 0) for a in srcs]
        return srcs, lands

    def gather_start(gi, behind, operands=None):
        srcs, lands = operands or gather_operands(gi)
        plan_a, plan_b, n_b = _gather_plans([a.shape for a in srcs])
        ssem, rsem, srcs, lands, token = _split_start("gw_start%d" % gi, srcs, lands, plan_a, 3 * len(srcs), behind)
        gather[gi] = (srcs, lands, ssem, rsem, plan_a, plan_b, n_b)
        return token

    def gather_forward(gi, behind):
        srcs, lands, ssem, rsem, plan_a, plan_b, n_b = gather[gi]
        ssem, rsem, lands, token = _split_forward("gw_fwd%d" % gi, srcs, lands, ssem, rsem, plan_a, plan_b, n_b, behind)
        gather[gi] = (lands, ssem, rsem, plan_b)
        return token

    def gathered(gi, behind):
        lands, ssem, rsem, plan_b = gather[gi]
        return _split_wait("gw_wait%d" % gi, [], lands, ssem, rsem, plan_b, behind)

    gather = {}
    (c_all8,) = _allgather8("gather_c", [c.reshape(8, (nb * D) // 8)])
    token = gather_start(0, c_all8)
    c_all = c_all8.reshape(NDEV * nb, D) + token[0, 0]
    b_sh = lax.dynamic_slice(b_ada, (0, j_chip * ADA_SH), (1, ADA_SH))
    ada_sh = _ada_fwd(c_all, w_ada[0], b_sh)
    later = [gather_operands(1), gather_operands(2)]
    (ada4,) = _chip_allgather("gather_ada", [ada_sh], behind=[a for pair in later for arrs in pair for a in arrs])
    token = gather_forward(0, ada4)
    token = gather_start(1, token, later[0])
    token = gather_start(2, token, later[1])
    ada4 = ada4 + token[0:1, 0:1]
    ada_me = lax.dynamic_slice(ada4, (0, dev * nb, 0), (NCHIP, nb, ADA_SH))
    ada_me = jnp.transpose(ada_me, (1, 0, 2)).reshape(nb, NMOD * D)
    sh1, sc1, gt1, sh2, sc2, gt2, sh3, sc3, gt3 = [ada_me[:, k * D:(k + 1) * D].reshape(nb, 1, D) for k in range(NMOD)]

    wcat = jnp.transpose(w_spatial[0], (1, 0, 2)).reshape(CH, NH * CH)
    wcat_t = jnp.transpose(w_spatial[0], (0, 2, 1)).reshape(NH * CH, CH)
    bspt = jnp.repeat(b_spatial[0].T, HD, axis=1)

    w1i, w1o = gathered(0, sh1)
    w1o = w1o.reshape(DFF, D)
    x1, f1, p1 = _ffn_fwd(x, sh1, sc1, gt1, g_pre_f1, g_post_f1, w1i, w1o)
    wmi, wmo, cw4 = gathered(1, gather_forward(1, x1))
    wmo = wmo.reshape(D, D)
    cw_full = jnp.transpose(cw4, (1, 0, 2)).reshape(CK, WB)
    cw_pad = jnp.pad(cw_full, ((0, HALO - CK), (0, 0)))
    u, v, a, g = _mix_in_fwd(x1, sh2, sc2, g_pre_m, wmi)
    x2, conv, yb, m = _mix_mid_fwd(x1, u, v, a, g, gt2, gmlp_norm_g, gmlp_norm_b, wcat, bspt, cw_pad, conv_b,
                                   conv_norm_g, conv_norm_b, g_out_a, g_out_b, wmo, g_post_m)
    w2i, w2o = gathered(2, gather_forward(2, x2))
    w2o = w2o.reshape(DFF, D)
    dx3, df2, p2, lsum, dg_post_f2, dgt3 = _ffn_fwd(x2, sh3, sc3, gt3, g_pre_f2, g_post_f2, w2i, w2o, target=loss_target)

    def chip4(pair, rows):
        return [arr.reshape(NCHIP, rows, arr.shape[-1]) for arr in pair]

    def scatter_start(tag, pairs, behind):
        srcs = [p[1] for p in pairs]
        lands = [lax.empty((3,) + a.shape[1:], a.dtype) for a in srcs]
        ssem, rsem, srcs, lands, token = _split_start("gs_start_" + tag, srcs, lands, _scatter_plan(len(srcs)),
                                                      3 * len(srcs), behind)
        return (srcs, lands, ssem, rsem), token

    def scatter_wait(tag, state, behind):
        srcs, lands, ssem, rsem = state
        return _split_wait("gs_wait_" + tag, srcs, lands, ssem, rsem, _scatter_plan(len(srcs)), behind)

    def allgather_start(tag, arrs, behind):
        lands = [lax.dynamic_update_index_in_dim(lax.empty((NDEV,) + a.shape, a.dtype), a, dev, 0) for a in arrs]
        ssem, rsem, srcs, lands, token = _split_start("small_start_" + tag, arrs, lands, _allgather_plan(len(arrs)),
                                                      7 * len(arrs), behind)
        return (srcs, lands, ssem, rsem), token

    def allgather_wait(tag, state, behind):
        srcs, lands, ssem, rsem = state
        return _split_wait("small_wait_" + tag, srcs, lands, ssem, rsem, _allgather_plan(len(srcs)), behind)

    out = {}
    dx2, dp2, h3, a2, dg_pre_f2, dsh3, dsc3 = _ffn_bwd(
        dx3, x2, None, p2, sh3, sc3, gt3, g_pre_f2, g_post_f2, w2i, w2o, df=df2)
    gw2i = _wgrad("wgrad_f2_in", h3.reshape(t, D), dp2.reshape(t, 2 * DFF), 2 * DFF // NCHIP, True)
    gw2o = chip4(_wgrad("wgrad_f2_out", a2.reshape(t, DFF), df2.reshape(t, D), D // 2, False), DFF // NCHIP)
    scat_f2, tok = scatter_start("f2", [gw2i, gw2o], dg_post_f2)
    dy, dm, dg_post_m, dgt2 = _mix_out_bwd(dx2, m, gt2 + tok[0, 0], g_post_m, wmo)
    gwmo = chip4(_wgrad("wgrad_mix_out", yb.reshape(t, D), dm.reshape(t, D), D // 2, False), D // NCHIP)
    (du, dv, dconv, dwcat, dbsp, dgn_g, dgn_b, dgo_a, dgo_b, dcn_g, dcn_b, dcb) = _mix_mid_bwd(
        dy, u, v, conv, gmlp_norm_g, gmlp_norm_b, wcat, wcat_t, bspt, conv_norm_g, conv_norm_b, g_out_a, g_out_b)
    dx1, dproj, h2, dg_pre_m, dsh2, dsc2, dcw = _mix_in_bwd(dx2, x1, du, dv, dconv, a, g, sh2, sc2, g_pre_m, wmi, cw_pad)
    gwmi = _wgrad("wgrad_mix_in", h2.reshape(t, D), dproj.reshape(t, 4 * WA), WA, True)
    scat_mix, tok = scatter_start("mix", [gwmi, gwmo], dg_pre_m)

    vec_grads = dict(g_pre_m=dg_pre_m, g_post_m=dg_post_m, g_pre_f2=dg_pre_f2, g_post_f2=dg_post_f2)
    pair_grads = dict(gmlp_norm_g=dgn_g, gmlp_norm_b=dgn_b, conv_b=dcb, conv_norm_g=dcn_g, conv_norm_b=dcn_b,
                      g_out_a=dgo_a, g_out_b=dgo_b)
    pack = _pack_small([vec_grads[n] for n in VEC_ORDER[2:]], [pair_grads[n] for n in PAIR_ORDER], dbsp, dwcat, lsum)
    dada_early = jnp.concatenate([q.reshape(nb, D) for q in (dsh2, dsc2, dgt2, dsh3, dsc3, dgt3)], axis=1)
    early, tok2 = allgather_start("early", [pack, dcw, dada_early.reshape(8, (nb * 6 * D) // 8)], tok)
    grad_x, dp1, h1, a1, df1, dg_pre_f1, dg_post_f1, dsh1, dsc1, dgt1 = _ffn_bwd(
        dx1, x, f1, p1, sh1 + tok2[0, 0], sc1, gt1, g_pre_f1, g_post_f1, w1i, w1o)
    late_pack = _pack_late([dg_pre_f1, dg_post_f1] + [q.reshape(nb, D) for q in (dsh1, dsc1, dgt1)])
    late, tok2 = allgather_start("late", [late_pack], dg_post_f1)
    gw1i = _wgrad("wgrad_f1_in", h1.reshape(t, D), dp1.reshape(t, 2 * DFF), 2 * DFF // NCHIP, True)
    gw1o = chip4(_wgrad("wgrad_f1_out", a1.reshape(t, DFF), df1.reshape(t, D), D // 2, False), DFF // NCHIP)
    def d2d_start(tag, srcs, lands, plan, behind):
        ssem, rsem, srcs, lands, token = _split_start("d2d_start_" + tag, srcs, lands, plan, len(srcs), behind)
        return (srcs, lands, ssem, rsem, plan), token

    def d2d_wait(tag, state, behind):
        srcs, lands, ssem, rsem, plan = state
        return _split_wait("d2d_wait_" + tag, srcs, lands, ssem, rsem, plan, behind)

    def swap_start(tag, parts, behind):
        return d2d_start(tag, parts, [lax.empty(a.shape, a.dtype) for a in parts], _swap_plan(len(parts)), behind)

    def sums(names, pairs, recv):
        return [_sum4("sum4_" + n, pairs[k][0], recv[k], j_arr) for k, n in enumerate(names)]

    def update(names, part, other):
        for k, n in enumerate(names):
            out[n] = tuple(r[None] for r in _adam_big("adam_" + n, wts[n][0], mom[n][0], var[n][0], part[k], other[k]))

    c_arr = ac.reshape(1).astype(jnp.int32)
    halves = [gw1i[1], gw1o[1]]
    pair_st, tok = d2d_start("pair", halves, [lax.empty((a.shape[0], a.shape[1] // 2, a.shape[2]), a.dtype) for a in halves],
                             _pair_plan([a.shape for a in halves]), tok2)
    names_f2, names_mix, names_f1 = ("w_f2_in", "w_f2_out"), ("w_mix_in", "w_mix_out"), ("w_f1_in", "w_f1_out")
    part_f2 = sums(names_f2, [gw2i, gw2o], scatter_wait("f2", scat_f2, tok))
    sib = d2d_wait("pair", pair_st, part_f2[1])
    pair_i = _pair_sum("pairsum_f1_in", gw1i[0], sib[0], c_arr)
    pair_o = _pair_sum("pairsum_f1_out", gw1o[0], sib[1], c_arr)
    scat_f1, tok = scatter_start("f1", [pair_i, pair_o], tok2)
    swap_f2, tok = swap_start("swap_f2", part_f2, tok)
    part_mix = sums(names_mix, [gwmi, gwmo], scatter_wait("mix", scat_mix, tok))
    swap_mix, tok = swap_start("swap_mix", part_mix, part_mix[1])

    pack_all, dcw_all, dada_early8 = allgather_wait("early", early, tok)
    (late_all,) = allgather_wait("late", late, pack_all)
    dada_late = jnp.transpose(late_all[:, 2:8, :].reshape(NDEV, 3, nb, D), (0, 2, 1, 3)).reshape(NDEV * nb, 3 * D)
    dada_all = jnp.concatenate([dada_late, dada_early8.reshape(NDEV * nb, 6 * D)], axis=1)
    dada_sh = lax.dynamic_slice(dada_all, (0, j_chip * ADA_SH), (NDEV * nb, ADA_SH))
    out["w_ada"] = tuple(r[None] for r in _ada_bwd_adam(c_all, dada_sh, w_ada[0], m_w_ada[0], v_w_ada[0]))
    update(names_f2, part_f2, d2d_wait("swap_f2", swap_f2, out["w_ada"][3]))
    update(names_mix, part_mix, d2d_wait("swap_mix", swap_mix, out["w_f2_out"][3]))

    mine = sums(names_f1, [pair_i, pair_o], scatter_wait("f1", scat_f1, out["w_mix_out"][3]))
    swap_f1, tok = swap_start("swap_f1", mine, mine[1])
    dcw_mine = lax.dynamic_slice(dcw_all, (0, 0, j_chip * (WB // NCHIP)), (NDEV, HALO, WB // NCHIP))
    small = {n: (wts[n], mom[n], var[n]) for n in list(VEC_ORDER) + list(PAIR_ORDER) + ["b_spatial", "w_spatial", "conv_w", "b_ada"]}
    small_out, loss = _small_adam(pack_all, late_all, dcw_mine, dada_all, small, tok)
    out.update(small_out)
    theirs = d2d_wait("swap_f1", swap_f1, out["b_ada"][3])
    for k, n in enumerate(names_f1):
        out[n] = tuple(r[None] for r in _adam_halves("adam_" + n, wts[n][0], mom[n][0], var[n][0], mine[k], theirs[k],
                                                     c_arr))

    res = [loss, grad_x]
    for k in range(4):
        res += [out[n][k] for n in WEIGHTS]
    return tuple(res)
```

```python
import functools

import jax
import jax.numpy as jnp
from jax import lax
from jax.experimental import pallas as pl
from jax.experimental.pallas import tpu as pltpu

D = 1024
DFF = 2816
WA = 512
WB = 512
NH = 8
HD = 64
CH = 128
CK = 31
HALO = 32
NMOD = 9
EPS = 1e-6
NCHIP = 4
NDEV = 8
FBLK = DFF // 2
ADA_SH = NMOD * D // NCHIP

LR, B1, B2, EPS_A, WD, STEP = 0.001, 0.9, 0.999, 1e-08, 0.01, 10

F32 = jnp.float32
BF16 = jnp.bfloat16
MESH = pl.DeviceIdType.MESH
ANY = pl.BlockSpec(memory_space=pl.ANY)
VMEM_FULL = pl.BlockSpec(memory_space=pltpu.VMEM)
VMEM_LIMIT = 56 * 1024 * 1024

NT = (((1,), (1,)), ((), ()))
TN = (((0,), (0,)), ((), ()))


def _dot(a, b):
    return jnp.dot(a, b, preferred_element_type=F32)


def _dot_nt(a, b):
    return lax.dot_general(a, b, NT, preferred_element_type=F32)


def _dot_tn(a, b):
    return lax.dot_general(a, b, TN, preferred_element_type=F32)


def _cparams():
    return pltpu.CompilerParams(vmem_limit_bytes=VMEM_LIMIT)


def _allgather8(name, arrs):
    n = len(arrs)

    def body(*refs):
        ins, outs = refs[:n], refs[n:2 * n]
        send_sems, recv_sems, local_sems = refs[2 * n:]
        x, y, c = lax.axis_index("x"), lax.axis_index("y"), lax.axis_index("c")
        me, sibling = (x, y, c), (x, y, 1 - c)
        chips = [(1 - x, y), (x, 1 - y), (1 - x, 1 - y)]

        def copy(a, k, block, to, src=None):
            rows = outs[a].at[4 * block[0] + 2 * block[1] + block[2]]
            return pltpu.make_async_remote_copy(
                src_ref=rows if src is None else src, dst_ref=rows,
                send_sem=send_sems.at[a, k], recv_sem=recv_sems.at[a, k],
                device_id=to, device_id_type=MESH)

        started, mine = [], []
        for a in range(n):
            loc = pltpu.make_async_copy(ins[a], outs[a].at[4 * x + 2 * y + c], local_sems.at[a])
            loc.start()
            mine.append(loc)
            first = [copy(a, 0, me, sibling, src=ins[a])]
            first += [copy(a, 1 + j, me, (*chip, c), src=ins[a]) for j, chip in enumerate(chips)]
            for cp in first:
                cp.start()
            started += first
        for a in range(n):
            for j, chip in enumerate(chips):
                copy(a, 1 + j, (*chip, c), me).wait_recv()
                fwd = copy(a, 4 + j, (*chip, c), sibling)
                fwd.start()
                started.append(fwd)
        for a in range(n):
            copy(a, 0, sibling, me).wait_recv()
            for j, chip in enumerate(chips):
                copy(a, 4 + j, (*chip, 1 - c), me).wait_recv()
        for cp in started:
            cp.wait_send()
        for loc in mine:
            loc.wait()

    return pl.pallas_call(
        body, name=name,
        out_shape=[jax.ShapeDtypeStruct((NDEV,) + a.shape, a.dtype) for a in arrs],
        in_specs=[ANY] * n, out_specs=[ANY] * n,
        scratch_shapes=[pltpu.SemaphoreType.DMA((n, 7)), pltpu.SemaphoreType.DMA((n, 7)),
                        pltpu.SemaphoreType.DMA((n,))],
    )(*arrs)


def _chip_relations(x, y):
    return [(1 - x, y), (x, 1 - y), (1 - x, 1 - y)]


def _exchange(name, arrs, out_shapes, plan):
    n = len(arrs)
    n_out = len(out_shapes)

    def body(*refs):
        ins, outs = refs[:n], refs[n:n + n_out]
        send_sems, recv_sems, local_sems = refs[n + n_out:]
        x, y, c = lax.axis_index("x"), lax.axis_index("y"), lax.axis_index("c")
        local, sends = plan(x, y, c, ins, outs)
        locs = [pltpu.make_async_copy(s, d, local_sems.at[i]) for i, (s, d) in enumerate(local)]
        for loc in locs:
            loc.start()
        cps = [pltpu.make_async_remote_copy(src_ref=s, dst_ref=d, send_sem=send_sems.at[i], recv_sem=recv_sems.at[i],
                                            device_id=peer, device_id_type=MESH)
               for i, (s, d, peer, _) in enumerate(sends)]
        for cp in cps:
            cp.start()
        for i, (s, _, peer, landing) in enumerate(sends):
            pltpu.make_async_remote_copy(src_ref=s, dst_ref=landing, send_sem=send_sems.at[i], recv_sem=recv_sems.at[i],
                                         device_id=peer, device_id_type=MESH).wait_recv()
        for cp in cps:
            cp.wait_send()
        for loc in locs:
            loc.wait()

    return n, n_out, body


def _run_exchange(name, arrs, out_shapes, plan, n_local, n_send):
    n, n_out, body = _exchange(name, arrs, out_shapes, plan)
    return pl.pallas_call(
        body, name=name, out_shape=out_shapes,
        in_specs=[ANY] * n, out_specs=[ANY] * n_out,
        scratch_shapes=[pltpu.SemaphoreType.DMA((n_send,)), pltpu.SemaphoreType.DMA((n_send,)),
                        pltpu.SemaphoreType.DMA((max(n_local, 1),))],
    )(*arrs)


def _chip_allgather(name, arrs, behind=()):
    n = len(arrs)

    def plan(x, y, c, ins, outs):
        j_me = 2 * x + y
        local = [(ins[a], outs[a].at[j_me]) for a in range(n)]
        sends = []
        for a in range(n):
            for (px, py) in _chip_relations(x, y):
                sends.append((ins[a], outs[a].at[j_me], (px, py, c), outs[a].at[2 * px + py]))
        return local, sends

    shapes = [jax.ShapeDtypeStruct((NCHIP,) + a.shape, a.dtype) for a in arrs]
    return _run_exchange(name, list(arrs) + list(behind), shapes, plan, n, 3 * n)


HBM = pl.BlockSpec(memory_space=pltpu.HBM)
SEM = pl.BlockSpec(memory_space=pltpu.SEMAPHORE)
EFFECT = pltpu.SideEffectType.DATAFLOW_SIDE_EFFECTING


def _split_start(name, srcs, lands, plan, n_send, after):
    n, nl = len(srcs), len(lands)

    def body(*refs):
        src, land = refs[:n], refs[n:n + nl]
        send_sems, recv_sems = refs[n + nl + 1], refs[n + nl + 2]
        token = refs[-2]
        local_sems = refs[-1]
        x, y, c = lax.axis_index("x"), lax.axis_index("y"), lax.axis_index("c")
        local, sends = plan(x, y, c, src, land)
        locs = [pltpu.make_async_copy(s, d, local_sems.at[i]) for i, (s, d) in enumerate(local)]
        for loc in locs:
            loc.start()
        for loc in locs:
            loc.wait()
        for i, (s, d, peer, _) in enumerate(sends):
            pltpu.make_async_remote_copy(src_ref=s, dst_ref=d, send_sem=send_sems.at[i], recv_sem=recv_sems.at[i],
                                         device_id=peer, device_id_type=MESH).start()
        token[...] = jnp.zeros_like(token)

    thru = [pltpu.HBM(a.shape, a.dtype) for a in lands]
    srcs = [pltpu.with_memory_space_constraint(a, pltpu.HBM) for a in srcs]
    res = pl.pallas_call(
        body, name=name,
        out_shape=(pltpu.SemaphoreType.DMA((n_send,)), pltpu.SemaphoreType.DMA((n_send,)), *thru,
                   jax.ShapeDtypeStruct((8, 128), F32)),
        in_specs=[HBM] * (n + nl) + [ANY],
        out_specs=(SEM, SEM, *([HBM] * nl), pl.BlockSpec(memory_space=pltpu.VMEM)),
        input_output_aliases={n + i: 2 + i for i in range(nl)},
        scratch_shapes=[pltpu.SemaphoreType.DMA((max(n, 1),))],
        compiler_params=pltpu.CompilerParams(has_side_effects=EFFECT),
    )(*srcs, *[pltpu.with_memory_space_constraint(a, pltpu.HBM) for a in lands], after)
    return res[0], res[1], srcs, list(res[2:2 + nl]), res[-1]


def _split_wait(name, srcs, lands, send_sems, recv_sems, plan, after):
    n, nl = len(srcs), len(lands)

    def body(*refs):
        src, land = refs[:n], refs[n:n + nl]
        send_sems, recv_sems = refs[n + nl], refs[n + nl + 1]
        x, y, c = lax.axis_index("x"), lax.axis_index("y"), lax.axis_index("c")
        _, sends = plan(x, y, c, src, land)
        for i, (s, _, peer, landing) in enumerate(sends):
            cp = pltpu.make_async_remote_copy(src_ref=s, dst_ref=landing, send_sem=send_sems.at[i],
                                              recv_sem=recv_sems.at[i], device_id=peer, device_id_type=MESH)
            cp.wait_send()
            cp.wait_recv()

    thru = [pltpu.HBM(a.shape, a.dtype) for a in lands]
    res = pl.pallas_call(
        body, name=name, out_shape=tuple(thru),
        in_specs=[HBM] * (n + nl) + [SEM, SEM, ANY], out_specs=tuple([HBM] * nl),
        input_output_aliases={n + i: i for i in range(nl)},
        compiler_params=pltpu.CompilerParams(has_side_effects=EFFECT),
    )(*srcs, *lands, send_sems, recv_sems, after)
    return list(res)


def _split_forward(name, srcs, lands, send_a, recv_a, plan_a, plan_b, n_b, after):
    n, nl = len(srcs), len(lands)

    def body(*refs):
        src, land = refs[:n], refs[n:n + nl]
        send_a, recv_a = refs[n + nl], refs[n + nl + 1]
        send_b, recv_b = refs[n + nl + 3], refs[n + nl + 4]
        token = refs[-1]
        x, y, c = lax.axis_index("x"), lax.axis_index("y"), lax.axis_index("c")
        _, first = plan_a(x, y, c, src, land)
        for i, (s, _, peer, landing) in enumerate(first):
            cp = pltpu.make_async_remote_copy(src_ref=s, dst_ref=landing, send_sem=send_a.at[i],
                                              recv_sem=recv_a.at[i], device_id=peer, device_id_type=MESH)
            cp.wait_send()
            cp.wait_recv()
        _, second = plan_b(x, y, c, src, land)
        for i, (s, d, peer, _) in enumerate(second):
            pltpu.make_async_remote_copy(src_ref=s, dst_ref=d, send_sem=send_b.at[i], recv_sem=recv_b.at[i],
                                         device_id=peer, device_id_type=MESH).start()
        token[...] = jnp.zeros_like(token)

    thru = [pltpu.HBM(a.shape, a.dtype) for a in lands]
    res = pl.pallas_call(
        body, name=name,
        out_shape=(pltpu.SemaphoreType.DMA((n_b,)), pltpu.SemaphoreType.DMA((n_b,)), *thru,
                   jax.ShapeDtypeStruct((8, 128), F32)),
        in_specs=[HBM] * (n + nl) + [SEM, SEM, ANY],
        out_specs=(SEM, SEM, *([HBM] * nl), pl.BlockSpec(memory_space=pltpu.VMEM)),
        input_output_aliases={n + i: 2 + i for i in range(nl)},
        compiler_params=pltpu.CompilerParams(has_side_effects=EFFECT),
    )(*srcs, *lands, send_a, recv_a, after)
    return res[0], res[1], list(res[2:2 + nl]), res[-1]


def _gather_plans(shapes):
    n = len(shapes)

    def halves(a, c):
        rows = shapes[a][0] // 2
        return pl.ds(pl.multiple_of(c * rows, 16), rows), pl.ds(pl.multiple_of((1 - c) * rows, 16), rows)

    def split(a):
        return shapes[a][0] % 32 == 0

    def plan_a(x, y, c, src, land):
        j_me = 2 * x + y
        sends = []
        for a in range(n):
            for (px, py) in _chip_relations(x, y):
                if split(a):
                    mine, _ = halves(a, c)
                    sends.append((src[a].at[mine], land[a].at[j_me, mine], (px, py, c), land[a].at[2 * px + py, mine]))
                else:
                    sends.append((src[a], land[a].at[j_me], (px, py, c), land[a].at[2 * px + py]))
        return [], sends

    def plan_b(x, y, c, src, land):
        sends = []
        for a in range(n):
            if split(a):
                mine, other = halves(a, c)
                for (px, py) in _chip_relations(x, y):
                    j = 2 * px + py
                    sends.append((land[a].at[j, mine], land[a].at[j, mine], (x, y, 1 - c), land[a].at[j, other]))
        return [], sends

    n_b = 3 * sum(1 for a in range(n) if split(a))
    return plan_a, plan_b, n_b


def _allgather_plan(n):
    flips = [(dx, dy, dc) for dx in (0, 1) for dy in (0, 1) for dc in (0, 1) if dx or dy or dc]

    def plan(x, y, c, src, land):
        sends = []
        for a in range(n):
            for dx, dy, dc in flips:
                px, py, pc = x ^ dx, y ^ dy, c ^ dc
                sends.append((src[a], land[a].at[4 * x + 2 * y + c], (px, py, pc), land[a].at[4 * px + 2 * py + pc]))
        return [], sends

    return plan


def _scatter_plan(n):
    def plan(x, y, c, src, land):
        sends = []
        for a in range(n):
            for k, (px, py) in enumerate(_chip_relations(x, y)):
                sends.append((src[a].at[2 * px + py], land[a].at[k], (px, py, c), land[a].at[k]))
        return [], sends

    return plan


def _rms(x):
    r = lax.rsqrt(jnp.mean(x * x, axis=-1, keepdims=True) + EPS)
    return x * r, r


def _rms_bwd(dy, n, r, g):
    dg = jnp.sum(dy * n, axis=0, keepdims=True)
    dn = dy * g
    dx = r * (dn - n * jnp.mean(dn * n, axis=-1, keepdims=True))
    return dx, dg


def _ln(x):
    mu = jnp.mean(x, axis=-1, keepdims=True)
    xc = x - mu
    rstd = lax.rsqrt(jnp.mean(xc * xc, axis=-1, keepdims=True) + EPS)
    return xc * rstd, rstd


def _ln_bwd(dy, xhat, rstd, g):
    dg = jnp.sum(dy * xhat, axis=0, keepdims=True)
    db = jnp.sum(dy, axis=0, keepdims=True)
    dxh = dy * g
    dx = rstd * (dxh - jnp.mean(dxh, axis=-1, keepdims=True) - xhat * jnp.mean(dxh * xhat, axis=-1, keepdims=True))
    return dx, dg, db


def _sigmoid(x):
    return jax.nn.sigmoid(x)


def _dsilu(x, s):
    return s * (1.0 + x * (1.0 - s))


def _adam(w, g, m, v):
    m = B1 * m + (1.0 - B1) * g
    v = B2 * v + (1.0 - B2) * (g * g)
    m_hat = m / (1.0 - B1 ** STEP)
    v_hat = v / (1.0 - B2 ** STEP)
    delta = -LR * (m_hat / (jnp.sqrt(v_hat) + EPS_A) + WD * w)
    return delta, m, v


def _head_mask(shape):
    lane = lax.broadcasted_iota(jnp.int32, shape, len(shape) - 1)
    return [(lane >= h * HD) & (lane < (h + 1) * HD) for h in range(NH)]


def _first(b, i):
    return jnp.logical_and(b == 0, i == 0)


def _acc(ref, val, first):
    @pl.when(first)
    def _():
        ref[...] = val

    @pl.when(jnp.logical_not(first))
    def _():
        ref[...] += val


def _ada_fwd(c_all, w_sh, b_sh):
    nb = c_all.shape[0]
    tn = 768

    def body(c_ref, w_ref, b_ref, o_ref):
        cv = c_ref[...]
        cs = (cv * _sigmoid(cv)).astype(BF16)
        o_ref[...] = _dot(cs, w_ref[...].astype(BF16)) + b_ref[...]

    return pl.pallas_call(
        body, name="ada_fwd", grid=(ADA_SH // tn,),
        out_shape=jax.ShapeDtypeStruct((nb, ADA_SH), F32),
        in_specs=[pl.BlockSpec((nb, D), lambda j: (0, 0)), pl.BlockSpec((D, tn), lambda j: (0, j)),
                  pl.BlockSpec((1, tn), lambda j: (0, j))],
        out_specs=pl.BlockSpec((nb, tn), lambda j: (0, j)),
        compiler_params=_cparams(),
    )(c_all, w_sh, b_sh)


def _ada_bwd_adam(c_all, dada_sh, w, m, v):
    nb = c_all.shape[0]
    tn = 768

    def body(c_ref, d_ref, w_ref, m_ref, v_ref, g_out, d_out, m_out, v_out):
        cv = c_ref[...]
        cs = (cv * _sigmoid(cv)).astype(BF16)
        g = _dot_tn(cs, d_ref[...].astype(BF16))
        delta, m2, v2 = _adam(w_ref[...], g, m_ref[...], v_ref[...])
        g_out[...] = g
        d_out[...] = delta
        m_out[...] = m2
        v_out[...] = v2

    big = pl.BlockSpec((D, tn), lambda j: (0, j))
    shape = jax.ShapeDtypeStruct((D, ADA_SH), F32)
    return pl.pallas_call(
        body, name="ada_bwd_adam", grid=(ADA_SH // tn,),
        out_shape=[shape] * 4,
        in_specs=[pl.BlockSpec((nb, D), lambda j: (0, 0)), pl.BlockSpec((nb, tn), lambda j: (0, j)), big, big, big],
        out_specs=[big] * 4,
        compiler_params=_cparams(),
    )(c_all, dada_sh, w, m, v)


def _tok_specs(tm, width):
    return pl.BlockSpec((1, tm, width), lambda b, i: (b, i, 0))


def _mod_spec():
    return pl.BlockSpec((1, 1, D), lambda b, i: (b, 0, 0))


def _row_spec(width=D):
    return pl.BlockSpec((1, width), lambda b, i: (0, 0))


def _ffn_fwd(x, sh, sc, gt, g_pre, g_post, w_in4, w_out, target=None):
    nb, s, _ = x.shape
    tm = min(512, s)
    with_loss = target is not None

    def body(*refs):
        if with_loss:
            (x_ref, sh_ref, sc_ref, gt_ref, gpre_ref, gpost_ref, win_ref, wout_ref, tgt_ref,
             xo_ref, df_ref, p_ref, ls_ref, dgpost_ref, dgt_ref) = refs
        else:
            (x_ref, sh_ref, sc_ref, gt_ref, gpre_ref, gpost_ref, win_ref, wout_ref,
             xo_ref, f_ref, p_ref) = refs
        xv = x_ref[0]
        n, _ = _rms(xv)
        h = (n * gpre_ref[...]) * (1.0 + sc_ref[0]) + sh_ref[0]
        hb = h.astype(BF16)
        acc = jnp.zeros((tm, D), F32)
        for j in range(2):
            gate = _dot(hb, win_ref[j])
            up = _dot(hb, win_ref[2 + j])
            p_ref[0, :, j * FBLK:(j + 1) * FBLK] = gate.astype(BF16)
            p_ref[0, :, DFF + j * FBLK:DFF + (j + 1) * FBLK] = up.astype(BF16)
            a = (gate * _sigmoid(gate)) * up
            acc = acc + _dot(a.astype(BF16), wout_ref[j * FBLK:(j + 1) * FBLK, :])
        nf, q = _rms(acc)
        gpost = gpost_ref[...]
        half_gate = 0.5 * gt_ref[0]
        out = xv + half_gate * (nf * gpost)
        if with_loss:
            first = _first(pl.program_id(0), pl.program_id(1))
            err = out - tgt_ref[0]
            dout = err * (1.0 / D)
            xo_ref[0] = dout
            row = jnp.sum(err * err, axis=0, keepdims=True)
            part = row[:, 0:128]
            for k in range(1, D // 128):
                part = part + row[:, k * 128:(k + 1) * 128]
            _acc(ls_ref, part, first)
            df, dgpost = _rms_bwd(dout * half_gate, nf, q, gpost)
            df_ref[0] = df.astype(BF16)
            _acc(dgpost_ref, dgpost, first)
            _acc(dgt_ref, jnp.sum(dout * (0.5 * (nf * gpost)), axis=0, keepdims=True)[None], pl.program_id(1) == 0)
        else:
            f_ref[0] = acc
            xo_ref[0] = out

    in_specs = [_tok_specs(tm, D), _mod_spec(), _mod_spec(), _mod_spec(), _row_spec(), _row_spec(), VMEM_FULL, VMEM_FULL]
    args = [x, sh, sc, gt, g_pre, g_post, w_in4, w_out]
    out_shape = [jax.ShapeDtypeStruct((nb, s, D), F32), jax.ShapeDtypeStruct((nb, s, D), BF16 if with_loss else F32),
                 jax.ShapeDtypeStruct((nb, s, 2 * DFF), BF16)]
    out_specs = [_tok_specs(tm, D), _tok_specs(tm, D), _tok_specs(tm, 2 * DFF)]
    if with_loss:
        in_specs.append(_tok_specs(tm, D))
        args.append(target)
        out_shape += [jax.ShapeDtypeStruct((1, 128), F32), jax.ShapeDtypeStruct((1, D), F32),
                      jax.ShapeDtypeStruct((nb, 1, D), F32)]
        out_specs += [pl.BlockSpec((1, 128), lambda b, i: (0, 0)), _row_spec(), _mod_spec()]
    return pl.pallas_call(
        body, name="ffn_loss_fwd" if with_loss else "ffn_fwd", grid=(nb, s // tm),
        out_shape=out_shape, in_specs=in_specs, out_specs=out_specs,
        compiler_params=_cparams(),
    )(*args)


def _ffn_bwd(dxo, x, f, p, sh, sc, gt, g_pre, g_post, w_in4, w_out, df=None):
    nb, s, _ = x.shape
    tm = min(256, s)
    given = df is not None

    def body(*refs):
        if given:
            (dxo_ref, x_ref, dfin_ref, p_ref, sh_ref, sc_ref, gpre_ref, win_ref, wout_ref,
             dx_ref, dp_ref, h_ref, a_ref, dgpre_ref, dsh_ref, dsc_ref) = refs
        else:
            (dxo_ref, x_ref, f_ref, p_ref, sh_ref, sc_ref, gt_ref, gpre_ref, gpost_ref, win_ref, wout_ref,
             dx_ref, dp_ref, h_ref, a_ref, df_ref, dgpre_ref, dgpost_ref, dsh_ref, dsc_ref, dgt_ref) = refs
        b, i = pl.program_id(0), pl.program_id(1)
        dxo_v = dxo_ref[0]
        if given:
            dfb = dfin_ref[0]
        else:
            nf, q = _rms(f_ref[0])
            gpost = gpost_ref[...]
            dgt = jnp.sum(dxo_v * (0.5 * (nf * gpost)), axis=0, keepdims=True)
            do = dxo_v * (0.5 * gt_ref[0])
            dfv, dgpost = _rms_bwd(do, nf, q, gpost)
            dfb = dfv.astype(BF16)
            df_ref[0] = dfb
        xv = x_ref[0]
        n, r = _rms(xv)
        gpre = gpre_ref[...]
        ng = n * gpre
        scale1 = 1.0 + sc_ref[0]
        h = ng * scale1 + sh_ref[0]
        h_ref[0] = h.astype(BF16)
        dh = jnp.zeros((tm, D), F32)
        for j in range(2):
            gate = p_ref[0, :, j * FBLK:(j + 1) * FBLK].astype(F32)
            up = p_ref[0, :, DFF + j * FBLK:DFF + (j + 1) * FBLK].astype(F32)
            sg = _sigmoid(gate)
            act = gate * sg
            a_ref[0, :, j * FBLK:(j + 1) * FBLK] = (act * up).astype(BF16)
            da = _dot_nt(dfb, wout_ref[j * FBLK:(j + 1) * FBLK, :])
            dgate = (da * up * _dsilu(gate, sg)).astype(BF16)
            dup = (da * act).astype(BF16)
            dp_ref[0, :, j * FBLK:(j + 1) * FBLK] = dgate
            dp_ref[0, :, DFF + j * FBLK:DFF + (j + 1) * FBLK] = dup
            dh = dh + _dot_nt(dgate, win_ref[j]) + _dot_nt(dup, win_ref[2 + j])
        dsh = jnp.sum(dh, axis=0, keepdims=True)
        dsc = jnp.sum(dh * ng, axis=0, keepdims=True)
        dxn, dgpre = _rms_bwd(dh * scale1, n, r, gpre)
        dx_ref[0] = dxo_v + dxn
        _acc(dgpre_ref, dgpre, _first(b, i))
        _acc(dsh_ref, dsh[None], i == 0)
        _acc(dsc_ref, dsc[None], i == 0)
        if not given:
            _acc(dgpost_ref, dgpost, _first(b, i))
            _acc(dgt_ref, dgt[None], i == 0)

    tok = _tok_specs(tm, D)
    mod_shape = jax.ShapeDtypeStruct((nb, 1, D), F32)
    row_shape = jax.ShapeDtypeStruct((1, D), F32)
    big = [jax.ShapeDtypeStruct((nb, s, D), F32), jax.ShapeDtypeStruct((nb, s, 2 * DFF), BF16),
           jax.ShapeDtypeStruct((nb, s, D), BF16), jax.ShapeDtypeStruct((nb, s, DFF), BF16)]
    big_specs = [tok, _tok_specs(tm, 2 * DFF), tok, _tok_specs(tm, DFF)]
    if given:
        return pl.pallas_call(
            body, name="ffn_bwd_after_loss", grid=(nb, s // tm),
            out_shape=big + [row_shape, mod_shape, mod_shape],
            in_specs=[tok, tok, tok, _tok_specs(tm, 2 * DFF), _mod_spec(), _mod_spec(), _row_spec(), VMEM_FULL, VMEM_FULL],
            out_specs=big_specs + [_row_spec(), _mod_spec(), _mod_spec()],
            compiler_params=_cparams(),
        )(dxo, x, df, p, sh, sc, g_pre, w_in4, w_out)
    return pl.pallas_call(
        body, name="ffn_bwd", grid=(nb, s // tm),
        out_shape=big + [jax.ShapeDtypeStruct((nb, s, D), BF16), row_shape, row_shape, mod_shape, mod_shape, mod_shape],
        in_specs=[tok, tok, tok, _tok_specs(tm, 2 * DFF), _mod_spec(), _mod_spec(), _mod_spec(), _row_spec(), _row_spec(),
                  VMEM_FULL, VMEM_FULL],
        out_specs=big_specs + [tok, _row_spec(), _row_spec(), _mod_spec(), _mod_spec(), _mod_spec()],
        compiler_params=_cparams(),
    )(dxo, x, f, p, sh, sc, gt, g_pre, g_post, w_in4, w_out)


def _wgrad(name, a, b, col_block, chip_major):
    t, ka = a.shape
    n = b.shape[1]
    tk = min(t, 512)
    while tk * 2 <= t and t % (tk * 2) == 0 and 2 * (tk * 2) * max(ka, col_block) <= 6 * 1024 * 1024:
        tk *= 2
    nk = t // tk
    nblk = n // col_block

    def body(a_ref, b_ref, o_ref, obf_ref, acc_ref):
        k = pl.program_id(1)

        @pl.when(k == 0)
        def _():
            acc_ref[...] = jnp.zeros_like(acc_ref)

        acc_ref[...] += _dot_tn(a_ref[...], b_ref[...])

        @pl.when(k == nk - 1)
        def _():
            val = acc_ref[...]
            if chip_major:
                o_ref[0] = val
                obf_ref[0] = val.astype(BF16)
            else:
                o_ref[...] = val
                obf_ref[...] = val.astype(BF16)

    if chip_major:
        shape = (nblk, ka, col_block)
        ospec = pl.BlockSpec((1, ka, col_block), lambda j, k: (j, 0, 0))
    else:
        shape = (ka, n)
        ospec = pl.BlockSpec((ka, col_block), lambda j, k: (0, j))
    return pl.pallas_call(
        body, name=name, grid=(nblk, nk),
        out_shape=[jax.ShapeDtypeStruct(shape, F32), jax.ShapeDtypeStruct(shape, BF16)],
        in_specs=[pl.BlockSpec((tk, ka), lambda j, k: (k, 0)), pl.BlockSpec((tk, col_block), lambda j, k: (k, j))],
        out_specs=[ospec, ospec],
        scratch_shapes=[pltpu.VMEM((ka, col_block), F32)],
        compiler_params=_cparams(),
    )(a, b)


def _mix_in_fwd(x, sh, sc, g_pre, w_mi4):
    nb, s, _ = x.shape
    tm = min(512, s)

    def body(x_ref, sh_ref, sc_ref, gpre_ref, w_ref, u_ref, v_ref, a_ref, g_ref):
        n, _ = _rms(x_ref[0])
        hb = ((n * gpre_ref[...]) * (1.0 + sc_ref[0]) + sh_ref[0]).astype(BF16)
        for k, o_ref in enumerate((u_ref, v_ref, a_ref, g_ref)):
            o_ref[0] = _dot(hb, w_ref[k])

    shape = jax.ShapeDtypeStruct((nb, s, WA), F32)
    return pl.pallas_call(
        body, name="mix_in_fwd", grid=(nb, s // tm),
        out_shape=[shape] * 4,
        in_specs=[_tok_specs(tm, D), _mod_spec(), _mod_spec(), _row_spec(), VMEM_FULL],
        out_specs=[_tok_specs(tm, WA)] * 4,
        compiler_params=_cparams(),
    )(x, sh, sc, g_pre, w_mi4)


def _spatial_weights(wcat_ref, transposed):
    w = wcat_ref[...]
    row = lax.broadcasted_iota(jnp.int32, w.shape, 0)
    col = lax.broadcasted_iota(jnp.int32, w.shape, 1)
    keep = ((row & (CH - 1)) <= col) if transposed else ((col & (CH - 1)) <= row)
    return jnp.where(keep, w, 0.0).astype(BF16)


def _expand_heads(vc, masks):
    return jnp.concatenate([jnp.where(mk, vc, jnp.zeros_like(vc)) for mk in masks], axis=0)


def _spatial_bias(bspt_ref):
    return bspt_ref[...]


SHIFTS = 8
TAP_ROWS = 32


def _ext_rows(tm):
    return tm + HALO + SHIFTS


def _make_shifts(ext_ref, sh_ref, tm):
    ext_ref[tm + HALO:tm + HALO + SHIFTS, :] = jnp.zeros((SHIFTS, WB), F32)
    for r in range(SHIFTS):
        sh_ref[r] = ext_ref[r:r + tm + HALO, :]


def _conv_taps(sh_ref, w_ref, tm, taps, emit):
    def block(i, carry):
        r0 = pl.multiple_of(i * TAP_ROWS, TAP_ROWS)
        acc = jnp.zeros((TAP_ROWS, WB), F32)
        for o, k in taps:
            acc = acc + w_ref[k:k + 1, :] * sh_ref[o % SHIFTS, pl.ds(r0 + SHIFTS * (o // SHIFTS), TAP_ROWS), :]
        emit(r0, acc)
        return carry

    lax.fori_loop(0, tm // TAP_ROWS, block, 0)


def _halo_prev_spec(tm):
    return pl.BlockSpec((1, HALO, WB), lambda b, i: (b, jnp.maximum(i * (tm // HALO) - 1, 0), 0))


def _halo_next_spec(tm, s):
    return pl.BlockSpec((1, HALO, WB), lambda b, i: (b, jnp.minimum((i + 1) * (tm // HALO), s // HALO - 1), 0))


def _mix_mid_fwd(x, u, v, a, g, gt, gn_g, gn_b, wcat, bspt, conv_w, conv_b, cn_g, cn_b, go_a, go_b, w_mo, g_post):
    nb, s, _ = x.shape
    tm = min(512, s)

    def body(x_ref, u_ref, v_ref, a_ref, g_ref, ah_ref, gh_ref, gt_ref, gng_ref, gnb_ref, wcat_ref, bspt_ref,
             cw_ref, cb_ref, cng_ref, cnb_ref, goa_ref, gob_ref, wmo_ref, gpost_ref,
             xo_ref, conv_ref, y_ref, m_ref, ext_ref, sh_ref):
        i = pl.program_id(1)
        xhat, _ = _ln(v_ref[0])
        vb = (xhat * gng_ref[...] + gnb_ref[...]).astype(BF16)
        wsb = _spatial_weights(wcat_ref, False)
        bias = _spatial_bias(bspt_ref)
        masks = _head_mask((CH, WA))
        zs = []
        for cidx in range(tm // CH):
            vexp = _expand_heads(vb[cidx * CH:(cidx + 1) * CH, :], masks)
            zs.append(_dot(wsb, vexp) + bias)
        z = jnp.concatenate(zs, axis=0)
        na, _ = _rms(u_ref[0] * z)
        keep = jnp.where(i == 0, 0.0, 1.0).astype(F32)
        ext_ref[0:HALO, :] = (ah_ref[0] * _sigmoid(gh_ref[0])) * keep
        ext_ref[HALO:HALO + tm, :] = a_ref[0] * _sigmoid(g_ref[0])
        _make_shifts(ext_ref, sh_ref, tm)
        cb = cb_ref[...]

        def put_conv(r0, acc):
            conv_ref[0, pl.ds(r0, TAP_ROWS), :] = acc + cb

        _conv_taps(sh_ref, cw_ref, tm, [(k + HALO - (CK - 1), k) for k in range(CK)], put_conv)
        conv = conv_ref[0]
        chat, _ = _ln(conv)
        cln = chat * cng_ref[...] + cnb_ref[...]
        nbb, _ = _rms(cln * _sigmoid(cln))
        yb = jnp.concatenate([na * goa_ref[...], nbb * gob_ref[...]], axis=1).astype(BF16)
        y_ref[0] = yb
        m = _dot(yb, wmo_ref[...])
        m_ref[0] = m
        nm, _ = _rms(m)
        xo_ref[0] = x_ref[0] + gt_ref[0] * (nm * gpost_ref[...])

    t5 = _tok_specs(tm, WA)
    tok = _tok_specs(tm, D)
    r5 = _row_spec(WA)
    full = lambda shape: pl.BlockSpec(shape, lambda b, i: (0,) * len(shape))
    return pl.pallas_call(
        body, name="mix_mid_fwd", grid=(nb, s // tm),
        out_shape=[jax.ShapeDtypeStruct((nb, s, D), F32), jax.ShapeDtypeStruct((nb, s, WB), F32),
                   jax.ShapeDtypeStruct((nb, s, D), BF16), jax.ShapeDtypeStruct((nb, s, D), F32)],
        in_specs=[tok, t5, t5, t5, t5, _halo_prev_spec(tm), _halo_prev_spec(tm), _mod_spec(), r5, r5,
                  full((CH, NH * CH)), full((CH, WA)), full((HALO, WB)), r5, r5, r5, r5, r5, VMEM_FULL, _row_spec()],
        out_specs=[tok, t5, tok, tok],
        scratch_shapes=[pltpu.VMEM((_ext_rows(tm), WB), F32), pltpu.VMEM((SHIFTS, tm + HALO, WB), F32)],
        compiler_params=_cparams(),
    )(x, u, v, a, g, a, g, gt, gn_g, gn_b, wcat, bspt, conv_w, conv_b, cn_g, cn_b, go_a, go_b, w_mo, g_post)


def _mix_out_bwd(dxo, m, gt, g_post, w_mo):
    nb, s, _ = m.shape
    tm = min(512, s)

    def body(dxo_ref, m_ref, gt_ref, gpost_ref, wmo_ref, dy_ref, dm_ref, dgpost_ref, dgt_ref):
        b, i = pl.program_id(0), pl.program_id(1)
        dxo_v = dxo_ref[0]
        nm, q = _rms(m_ref[0])
        gpost = gpost_ref[...]
        dgt = jnp.sum(dxo_v * (nm * gpost), axis=0, keepdims=True)
        dm, dgpost = _rms_bwd(dxo_v * gt_ref[0], nm, q, gpost)
        dmb = dm.astype(BF16)
        dm_ref[0] = dmb
        dy_ref[0] = _dot_nt(dmb, wmo_ref[...])
        _acc(dgpost_ref, dgpost, _first(b, i))
        _acc(dgt_ref, dgt[None], i == 0)

    tok = _tok_specs(tm, D)
    return pl.pallas_call(
        body, name="mix_out_bwd", grid=(nb, s // tm),
        out_shape=[jax.ShapeDtypeStruct((nb, s, D), F32), jax.ShapeDtypeStruct((nb, s, D), BF16),
                   jax.ShapeDtypeStruct((1, D), F32), jax.ShapeDtypeStruct((nb, 1, D), F32)],
        in_specs=[tok, tok, _mod_spec(), _row_spec(), VMEM_FULL],
        out_specs=[tok, tok, _row_spec(), _mod_spec()],
        compiler_params=_cparams(),
    )(dxo, m, gt, g_post, w_mo)


def _mix_mid_bwd(dy, u, v, conv, gn_g, gn_b, wcat, wcat_t, bspt, cn_g, cn_b, go_a, go_b):
    nb, s, _ = dy.shape
    tm = min(512, s)
    nchunk = tm // CH

    def body(dy_ref, u_ref, v_ref, conv_ref, gng_ref, gnb_ref, wcat_ref, wcatt_ref, bspt_ref, cng_ref, cnb_ref,
             goa_ref, gob_ref,
             du_ref, dv_ref, dconv_ref, dwcat_ref, dbsp_ref, dgng_ref, dgnb_ref, dgoa_ref, dgob_ref,
             dcng_ref, dcnb_ref, dcb_ref):
        first = _first(pl.program_id(0), pl.program_id(1))
        dyv = dy_ref[0]
        xhat, rstd = _ln(v_ref[0])
        gng = gng_ref[...]
        vb = (xhat * gng + gnb_ref[...]).astype(BF16)
        wsb = _spatial_weights(wcat_ref, False)
        wsb_t = _spatial_weights(wcatt_ref, True)
        bias = _spatial_bias(bspt_ref)
        masks = _head_mask((CH, WA))
        vexps, zs = [], []
        for cidx in range(nchunk):
            vexp = _expand_heads(vb[cidx * CH:(cidx + 1) * CH, :], masks)
            vexps.append(vexp)
            zs.append(_dot(wsb, vexp) + bias)
        z = jnp.concatenate(zs, axis=0)
        uv = u_ref[0]
        na, ra = _rms(uv * z)
        dya, dgoa = _rms_bwd(dyv[:, 0:WA], na, ra, goa_ref[...])
        du_ref[0] = dya * z
        dz = dya * uv
        dwcat = jnp.zeros((CH, NH * CH), F32)
        dzsum = jnp.zeros((CH, WA), F32)
        dvlns = []
        for cidx in range(nchunk):
            dzc = dz[cidx * CH:(cidx + 1) * CH, :]
            dzsum = dzsum + dzc
            dzb = dzc.astype(BF16)
            dwcat = dwcat + _dot_nt(dzb, vexps[cidx])
            dvexp = _dot(wsb_t, dzb)
            dvl = jnp.zeros((CH, WA), F32)
            for h in range(NH):
                dvl = dvl + jnp.where(masks[h], dvexp[h * CH:(h + 1) * CH, :], 0.0)
            dvlns.append(dvl)
        dvln = jnp.concatenate(dvlns, axis=0)
        dv, dgng, dgnb = _ln_bwd(dvln, xhat, rstd, gng)
        dv_ref[0] = dv
        lane = lax.broadcasted_iota(jnp.int32, (NH, WA), 1)
        head = lax.broadcasted_iota(jnp.int32, (NH, WA), 0)
        sel = jnp.where((lane >= head * HD) & (lane < (head + 1) * HD), 1.0, 0.0).astype(F32)
        dbsp = lax.dot_general(sel, dzsum, NT, preferred_element_type=F32, precision=lax.Precision.HIGHEST)
        chat, crstd = _ln(conv_ref[0])
        cng = cng_ref[...]
        cln = chat * cng + cnb_ref[...]
        sg = _sigmoid(cln)
        nbb, rb = _rms(cln * sg)
        dyb, dgob = _rms_bwd(dyv[:, WA:D], nbb, rb, gob_ref[...])
        dconv, dcng, dcnb = _ln_bwd(dyb * _dsilu(cln, sg), chat, crstd, cng)
        dconv_ref[0] = dconv
        dcb = jnp.sum(dconv, axis=0, keepdims=True)
        for ref, val in ((dwcat_ref, dwcat), (dbsp_ref, dbsp), (dgng_ref, dgng), (dgnb_ref, dgnb), (dgoa_ref, dgoa),
                         (dgob_ref, dgob), (dcng_ref, dcng), (dcnb_ref, dcnb), (dcb_ref, dcb)):
            _acc(ref, val, first)

    t5 = _tok_specs(tm, WA)
    r5 = _row_spec(WA)
    full = lambda shape: pl.BlockSpec(shape, lambda b, i: (0,) * len(shape))
    big = jax.ShapeDtypeStruct((nb, s, WA), F32)
    row = jax.ShapeDtypeStruct((1, WA), F32)
    return pl.pallas_call(
        body, name="mix_mid_bwd", grid=(nb, s // tm),
        out_shape=[big, big, big, jax.ShapeDtypeStruct((CH, NH * CH), F32), jax.ShapeDtypeStruct((NH, CH), F32),
                   row, row, row, row, row, row, row],
        in_specs=[_tok_specs(tm, D), t5, t5, t5, r5, r5, full((CH, NH * CH)), full((NH * CH, CH)), full((CH, WA)),
                  r5, r5, r5, r5],
        out_specs=[t5, t5, t5, full((CH, NH * CH)), full((NH, CH)), r5, r5, r5, r5, r5, r5, r5],
        compiler_params=_cparams(),
    )(dy, u, v, conv, gn_g, gn_b, wcat, wcat_t, bspt, cn_g, cn_b, go_a, go_b)


def _mix_in_bwd(dxo, x, du, dv, dconv, a, g, sh, sc, g_pre, w_mi4, conv_w):
    nb, s, _ = x.shape
    tm = min(512, s)
    n_i = s // tm

    def body(dxo_ref, x_ref, du_ref, dv_ref, dc_ref, dch_ref, a_ref, g_ref, ah_ref, gh_ref, sh_ref, sc_ref,
             gpre_ref, w_ref, cw_ref,
             dx_ref, dproj_ref, h_ref, dgpre_ref, dsh_ref, dsc_ref, dcw_ref, ext_ref, shf_ref, dglu_ref):
        b, i = pl.program_id(0), pl.program_id(1)
        first = _first(b, i)
        av, gv = a_ref[0], g_ref[0]
        sg = _sigmoid(gv)
        dconv = dc_ref[0]
        ext_ref[0:tm, :] = dconv
        ext_ref[tm:tm + HALO, :] = dch_ref[0] * jnp.where(i == n_i - 1, 0.0, 1.0).astype(F32)
        _make_shifts(ext_ref, shf_ref, tm)

        def put_dglu(r0, acc):
            dglu_ref[pl.ds(r0, TAP_ROWS), :] = acc

        _conv_taps(shf_ref, cw_ref, tm, [(CK - 1 - k, k) for k in range(CK)], put_dglu)
        dglu = dglu_ref[...]
        ext_ref[0:HALO, :] = (ah_ref[0] * _sigmoid(gh_ref[0])) * jnp.where(i == 0, 0.0, 1.0).astype(F32)
        ext_ref[HALO:HALO + tm, :] = av * sg
        _make_shifts(ext_ref, shf_ref, tm)

        @pl.when(first)
        def _():
            dcw_ref[...] = jnp.zeros((HALO, WB), F32)

        for k in range(CK):
            o = k + HALO - (CK - 1)
            lo = SHIFTS * (o // SHIFTS)
            dcw_ref[k:k + 1, :] += jnp.sum(dconv * shf_ref[o % SHIFTS, lo:lo + tm, :], axis=0, keepdims=True)
        da = dglu * sg
        dg = dglu * av * (sg * (1.0 - sg))
        parts = [du_ref[0].astype(BF16), dv_ref[0].astype(BF16), da.astype(BF16), dg.astype(BF16)]
        dh = jnp.zeros((tm, D), F32)
        for k in range(4):
            dproj_ref[0, :, k * WA:(k + 1) * WA] = parts[k]
            dh = dh + _dot_nt(parts[k], w_ref[k])
        n, r = _rms(x_ref[0])
        gpre = gpre_ref[...]
        ng = n * gpre
        scale1 = 1.0 + sc_ref[0]
        h_ref[0] = (ng * scale1 + sh_ref[0]).astype(BF16)
        dsh = jnp.sum(dh, axis=0, keepdims=True)
        dsc = jnp.sum(dh * ng, axis=0, keepdims=True)
        dxn, dgpre = _rms_bwd(dh * scale1, n, r, gpre)
        dx_ref[0] = dxo_ref[0] + dxn
        _acc(dgpre_ref, dgpre, first)
        _acc(dsh_ref, dsh[None], i == 0)
        _acc(dsc_ref, dsc[None], i == 0)

    tok = _tok_specs(tm, D)
    t5 = _tok_specs(tm, WA)
    full = lambda shape: pl.BlockSpec(shape, lambda b, i: (0,) * len(shape))
    mod_shape = jax.ShapeDtypeStruct((nb, 1, D), F32)
    return pl.pallas_call(
        body, name="mix_in_bwd", grid=(nb, n_i),
        out_shape=[jax.ShapeDtypeStruct((nb, s, D), F32), jax.ShapeDtypeStruct((nb, s, 4 * WA), BF16),
                   jax.ShapeDtypeStruct((nb, s, D), BF16), jax.ShapeDtypeStruct((1, D), F32), mod_shape, mod_shape,
                   jax.ShapeDtypeStruct((HALO, WB), F32)],
        in_specs=[tok, tok, t5, t5, t5, _halo_next_spec(tm, s), t5, t5, _halo_prev_spec(tm), _halo_prev_spec(tm),
                  _mod_spec(), _mod_spec(), _row_spec(), VMEM_FULL, full((HALO, WB))],
        out_specs=[tok, _tok_specs(tm, 4 * WA), tok, _row_spec(), _mod_spec(), _mod_spec(), full((HALO, WB))],
        scratch_shapes=[pltpu.VMEM((_ext_rows(tm), WB), F32), pltpu.VMEM((SHIFTS, tm + HALO, WB), F32),
                        pltpu.VMEM((tm, WB), F32)],
        compiler_params=_cparams(),
    )(dxo, x, du, dv, dconv, dconv, a, g, a, g, sh, sc, g_pre, w_mi4, conv_w)


def _row_tile(rows, cols):
    best = 16
    for t in range(16, rows + 1, 16):
        if rows % t == 0 and t * cols * 4 <= 1536 * 1024:
            best = t
    return best


def _sum4(name, own4, recv, j_arr):
    _, rows, cols = own4.shape
    tr = _row_tile(rows, cols)

    def body(j_ref, own_ref, recv_ref, o_ref):
        del j_ref
        acc = own_ref[0]
        for k in range(3):
            acc = acc + recv_ref[k].astype(F32)
        o_ref[...] = acc

    return pl.pallas_call(
        body, name=name,
        grid_spec=pltpu.PrefetchScalarGridSpec(
            num_scalar_prefetch=1, grid=(rows // tr,),
            in_specs=[pl.BlockSpec((1, tr, cols), lambda i, j: (j[0], i, 0)),
                      pl.BlockSpec((3, tr, cols), lambda i, j: (0, i, 0))],
            out_specs=pl.BlockSpec((tr, cols), lambda i, j: (i, 0))),
        out_shape=jax.ShapeDtypeStruct((rows, cols), F32),
        compiler_params=_cparams(),
    )(j_arr, own4, recv)


def _pair_plan(shapes):
    def plan(x, y, c, src, land):
        sends = []
        for a, shape in enumerate(shapes):
            rows = shape[1] // 2
            theirs = pl.ds(pl.multiple_of((1 - c) * rows, 16), rows)
            sends.append((src[a].at[:, theirs], land[a], (x, y, 1 - c), land[a]))
        return [], sends

    return plan


def _swap_plan(n):
    def plan(x, y, c, src, land):
        return [], [(src[a], land[a], (x, y, 1 - c), land[a]) for a in range(n)]

    return plan


def _pair_sum(name, g32, recv, c_arr):
    nblk, rows, cols = recv.shape
    tr = _row_tile(rows, cols)
    nh = rows // tr

    def body(c_ref, g_ref, r_ref, o32_ref, obf_ref):
        del c_ref
        val = g_ref[0] + r_ref[0].astype(F32)
        o32_ref[0] = val
        obf_ref[0] = val.astype(BF16)

    spec = pl.BlockSpec((1, tr, cols), lambda k, i, c: (k, i, 0))
    return pl.pallas_call(
        body, name=name,
        grid_spec=pltpu.PrefetchScalarGridSpec(
            num_scalar_prefetch=1, grid=(nblk, nh),
            in_specs=[pl.BlockSpec((1, tr, cols), lambda k, i, c: (k, c[0] * nh + i, 0)), spec],
            out_specs=[spec, spec]),
        out_shape=[jax.ShapeDtypeStruct(recv.shape, F32), jax.ShapeDtypeStruct(recv.shape, BF16)],
        compiler_params=_cparams(),
    )(c_arr, g32, recv)


def _adam_halves(name, w, m, v, mine, theirs, c_arr):
    rows, cols = w.shape
    tr = _row_tile(rows // 2, cols)
    nh = (rows // 2) // tr

    def body(c_ref, w_ref, m_ref, v_ref, mine_ref, theirs_ref, g_out, d_out, m_out, v_out):
        here = (pl.program_id(0) // nh) == c_ref[0]
        g = jnp.where(here, mine_ref[...], theirs_ref[...])
        delta, m2, v2 = _adam(w_ref[...], g, m_ref[...], v_ref[...])
        g_out[...] = g
        d_out[...] = delta
        m_out[...] = m2
        v_out[...] = v2

    spec = pl.BlockSpec((tr, cols), lambda i, c: (i, 0))
    shape = jax.ShapeDtypeStruct((rows, cols), F32)
    return pl.pallas_call(
        body, name=name,
        grid_spec=pltpu.PrefetchScalarGridSpec(
            num_scalar_prefetch=1, grid=(2 * nh,),
            in_specs=[spec, spec, spec,
                      pl.BlockSpec((tr, cols), lambda i, c: (jnp.clip(i - c[0] * nh, 0, nh - 1), 0)),
                      pl.BlockSpec((tr, cols), lambda i, c: (jnp.clip(i - (1 - c[0]) * nh, 0, nh - 1), 0))],
            out_specs=[spec] * 4),
        out_shape=[shape] * 4,
        compiler_params=_cparams(),
    )(c_arr, w, m, v, mine, theirs)


def _adam_big(name, w, m, v, ga, gb):
    rows, cols = w.shape
    tr = _row_tile(rows, cols)

    def body(w_ref, m_ref, v_ref, ga_ref, gb_ref, g_out, d_out, m_out, v_out):
        gsum = ga_ref[...] + gb_ref[...]
        delta, m2, v2 = _adam(w_ref[...], gsum, m_ref[...], v_ref[...])
        g_out[...] = gsum
        d_out[...] = delta
        m_out[...] = m2
        v_out[...] = v2

    spec = pl.BlockSpec((tr, cols), lambda i: (i, 0))
    shape = jax.ShapeDtypeStruct((rows, cols), F32)
    return pl.pallas_call(
        body, name=name, grid=(rows // tr,), out_shape=[shape] * 4,
        in_specs=[spec] * 5, out_specs=[spec] * 4, compiler_params=_cparams(),
    )(w, m, v, ga, gb)


PK_VEC = 0
PK_LOSS = 6
PK_PAIR = 8
PK_BSP = 16
PK_WCAT = 24
PK_ROWS = PK_WCAT + CH
PAIR_ORDER = ("gmlp_norm_g", "gmlp_norm_b", "conv_b", "conv_norm_g", "conv_norm_b", "g_out_a", "g_out_b")
VEC_ORDER = ("g_pre_f1", "g_post_f1", "g_pre_m", "g_post_m", "g_pre_f2", "g_post_f2")


def _pack_late(rows):
    counts = [r.shape[0] for r in rows]
    assert sum(counts) == 8

    def body(*refs):
        o_ref = refs[-1]
        at = 0
        for r, cnt in zip(refs[:-1], counts):
            o_ref[at:at + cnt, :] = r[...]
            at += cnt

    return pl.pallas_call(
        body, name="pack_late", out_shape=jax.ShapeDtypeStruct((8, D), F32),
        in_specs=[VMEM_FULL] * len(rows), out_specs=VMEM_FULL, compiler_params=_cparams(),
    )(*rows)


def _pack_small(vecs, pairs, dbsp, dwcat, lsum):
    def body(*refs):
        vec_refs = refs[:4]
        pair_refs = refs[4:11]
        dbsp_ref, dwcat_ref, lsum_ref, o_ref = refs[11:]
        o_ref[0:PK_WCAT, :] = jnp.zeros((PK_WCAT, D), F32)
        o_ref[PK_LOSS:PK_LOSS + 1, 0:128] = lsum_ref[...]
        for k, r in enumerate(vec_refs):
            o_ref[PK_VEC + 2 + k:PK_VEC + 3 + k, :] = r[...]
        for k, r in enumerate(pair_refs):
            row, half = PK_PAIR + k // 2, k % 2
            o_ref[row:row + 1, half * WA:(half + 1) * WA] = r[...]
        o_ref[PK_BSP:PK_BSP + NH, 0:CH] = dbsp_ref[...]
        o_ref[PK_WCAT:PK_ROWS, :] = dwcat_ref[...]

    args = list(vecs) + list(pairs) + [dbsp, dwcat, lsum]
    return pl.pallas_call(
        body, name="pack_small", out_shape=jax.ShapeDtypeStruct((PK_ROWS, D), F32),
        in_specs=[VMEM_FULL] * len(args), out_specs=VMEM_FULL, compiler_params=_cparams(),
    )(*args)


def _small_adam(pack_all, late_all, dcw_all, dada_all, params, behind):
    names = list(VEC_ORDER) + list(PAIR_ORDER) + ["b_spatial", "w_spatial", "conv_w", "b_ada"]
    flat = []
    for nm in names:
        flat += list(params[nm])
    n_in = 4 + len(flat)

    def body(*refs):
        pack_ref, late_ref, dcw_ref, dada_ref = refs[:4]
        prm = refs[4:n_in]
        outs = refs[n_in + 1:]

        def total(r0, nr, c0, nc):
            acc = pack_ref[0, r0:r0 + nr, c0:c0 + nc]
            for d in range(1, NDEV):
                acc = acc + pack_ref[d, r0:r0 + nr, c0:c0 + nc]
            return acc

        def emit(idx, g, getw, put):
            w_ref, m_ref, v_ref = prm[3 * idx:3 * idx + 3]
            delta, m2, v2 = _adam(getw(w_ref), g, getw(m_ref), getw(v_ref))
            for o_ref, val in zip(outs[4 * idx:4 * idx + 4], (g, delta, m2, v2)):
                put(o_ref, val)

        def whole(ref):
            return ref[...]

        def put_whole(ref, val):
            ref[...] = val

        idx = 0
        for k in range(6):
            if k < 2:
                g = late_ref[0, k:k + 1, :]
                for d in range(1, NDEV):
                    g = g + late_ref[d, k:k + 1, :]
            else:
                g = total(PK_VEC + k, 1, 0, D)
            emit(idx, g, whole, put_whole)
            idx += 1
        for k in range(7):
            emit(idx, total(PK_PAIR + k // 2, 1, (k % 2) * WA, WA), whole, put_whole)
            idx += 1
        emit(idx, total(PK_BSP, NH, 0, CH), lambda r: r[0], lambda r, val: r.__setitem__(0, val))
        idx += 1
        row = lax.broadcasted_iota(jnp.int32, (CH, CH), 0)
        col = lax.broadcasted_iota(jnp.int32, (CH, CH), 1)
        for h in range(NH):
            gh = jnp.where(col <= row, total(PK_WCAT, CH, h * CH, CH), 0.0)
            w_ref, m_ref, v_ref = prm[3 * idx:3 * idx + 3]
            delta, m2, v2 = _adam(w_ref[0, h], gh, m_ref[0, h], v_ref[0, h])
            for o_ref, val in zip(outs[4 * idx:4 * idx + 4], (gh, delta, m2, v2)):
                o_ref[0, h] = val
        idx += 1
        gcw = dcw_ref[0, 0:CK, :]
        for d in range(1, NDEV):
            gcw = gcw + dcw_ref[d, 0:CK, :]
        emit(idx, gcw, lambda r: r[0], lambda r, val: r.__setitem__(0, val))
        idx += 1
        emit(idx, jnp.sum(dada_ref[...], axis=0, keepdims=True), whole, put_whole)
        outs[-1][...] = jnp.sum(total(PK_LOSS, 1, 0, 128), axis=1, keepdims=True) * (0.5 / D)

    out_shape = []
    for nm in names:
        w = params[nm][0]
        out_shape += [jax.ShapeDtypeStruct(w.shape, F32)] * 4
    out_shape.append(jax.ShapeDtypeStruct((1, 1), F32))
    res = pl.pallas_call(
        body, name="small_adam", out_shape=out_shape,
        in_specs=[VMEM_FULL] * n_in + [ANY], out_specs=[VMEM_FULL] * len(out_shape), compiler_params=_cparams(),
    )(pack_all, late_all, dcw_all, dada_all, *flat, behind)
    return {nm: tuple(res[4 * k:4 * k + 4]) for k, nm in enumerate(names)}, res[-1].reshape(())


WEIGHTS = ['w_ada', 'b_ada', 'g_pre_f1', 'g_post_f1', 'w_f1_in', 'w_f1_out', 'g_pre_m', 'g_post_m', 'w_mix_in',
           'gmlp_norm_g', 'gmlp_norm_b', 'w_spatial', 'b_spatial', 'conv_w', 'conv_b', 'conv_norm_g', 'conv_norm_b',
           'g_out_a', 'g_out_b', 'w_mix_out', 'g_pre_f2', 'g_post_f2', 'w_f2_in', 'w_f2_out']
BIG = ('w_f1_in', 'w_f1_out', 'w_mix_in', 'w_mix_out', 'w_f2_in', 'w_f2_out')


def kernel(x, c, w_ada, b_ada, g_pre_f1, g_post_f1, w_f1_in, w_f1_out, g_pre_m, g_post_m, w_mix_in, gmlp_norm_g, gmlp_norm_b, w_spatial, b_spatial, conv_w, conv_b, conv_norm_g, conv_norm_b, g_out_a, g_out_b, w_mix_out, g_pre_f2, g_post_f2, w_f2_in, w_f2_out, loss_target, m_w_ada, m_b_ada, m_g_pre_f1, m_g_post_f1, m_w_f1_in, m_w_f1_out, m_g_pre_m, m_g_post_m, m_w_mix_in, m_gmlp_norm_g, m_gmlp_norm_b, m_w_spatial, m_b_spatial, m_conv_w, m_conv_b, m_conv_norm_g, m_conv_norm_b, m_g_out_a, m_g_out_b, m_w_mix_out, m_g_pre_f2, m_g_post_f2, m_w_f2_in, m_w_f2_out, v_w_ada, v_b_ada, v_g_pre_f1, v_g_post_f1, v_w_f1_in, v_w_f1_out, v_g_pre_m, v_g_post_m, v_w_mix_in, v_gmlp_norm_g, v_gmlp_norm_b, v_w_spatial, v_b_spatial, v_conv_w, v_conv_b, v_conv_norm_g, v_conv_norm_b, v_g_out_a, v_g_out_b, v_w_mix_out, v_g_pre_f2, v_g_post_f2, v_w_f2_in, v_w_f2_out):
    env = dict(locals())
    wts = {n: env[n] for n in WEIGHTS}
    mom = {n: env["m_" + n] for n in WEIGHTS}
    var = {n: env["v_" + n] for n in WEIGHTS}
    nb, s, _ = x.shape
    t = nb * s
    ax, ay, ac = lax.axis_index("x"), lax.axis_index("y"), lax.axis_index("c")
    j_chip = 2 * ax + ay
    dev = 4 * ax + 2 * ay + ac
    j_arr = j_chip.reshape(1).astype(jnp.int32)

    groups = (("w_f1_in", "w_f1_out"), ("w_mix_in", "w_mix_out"), ("w_f2_in", "w_f2_out"))
    def gather_operands(gi):
        srcs = [wts[n][0].astype(BF16) for n in groups[gi]] + ([conv_w[0]] if gi == 1 else [])
        lands = [lax.dynamic_update_index_in_dim(lax.empty((NCHIP,) + a.shape, a.dtype), a, j_chip, 0) for a in srcs]
        return srcs, lands

    def gather_start(gi, behind, operands=None):
        srcs, lands = operands or gather_operands(gi)
        plan_a, plan_b, n_b = _gather_plans([a.shape for a in srcs])
        ssem, rsem, srcs, lands, token = _split_start("gw_start%d" % gi, srcs, lands, plan_a, 3 * len(srcs), behind)
        gather[gi] = (srcs, lands, ssem, rsem, plan_a, plan_b, n_b)
        return token

    def gather_forward(gi, behind):
        srcs, lands, ssem, rsem, plan_a, plan_b, n_b = gather[gi]
        ssem, rsem, lands, token = _split_forward("gw_fwd%d" % gi, srcs, lands, ssem, rsem, plan_a, plan_b, n_b, behind)
        gather[gi] = (lands, ssem, rsem, plan_b)
        return token

    def gathered(gi, behind):
        lands, ssem, rsem, plan_b = gather[gi]
        return _split_wait("gw_wait%d" % gi, [], lands, ssem, rsem, plan_b, behind)

    gather = {}
    (c_all8,) = _allgather8("gather_c", [c.reshape(8, (nb * D) // 8)])
    token = gather_start(0, c_all8)
    c_all = c_all8.reshape(NDEV * nb, D) + token[0, 0]
    b_sh = lax.dynamic_slice(b_ada, (0, j_chip * ADA_SH), (1, ADA_SH))
    ada_sh = _ada_fwd(c_all, w_ada[0], b_sh)
    later = [gather_operands(1), gather_operands(2)]
    (ada4,) = _chip_allgather("gather_ada", [ada_sh], behind=[a for pair in later for arrs in pair for a in arrs])
    token = gather_forward(0, ada4)
    token = gather_start(1, token, later[0])
    token = gather_start(2, token, later[1])
    ada4 = ada4 + token[0:1, 0:1]
    ada_me = lax.dynamic_slice(ada4, (0, dev * nb, 0), (NCHIP, nb, ADA_SH))
    ada_me = jnp.transpose(ada_me, (1, 0, 2)).reshape(nb, NMOD * D)
    sh1, sc1, gt1, sh2, sc2, gt2, sh3, sc3, gt3 = [ada_me[:, k * D:(k + 1) * D].reshape(nb, 1, D) for k in range(NMOD)]

    wcat = jnp.transpose(w_spatial[0], (1, 0, 2)).reshape(CH, NH * CH)
    wcat_t = jnp.transpose(w_spatial[0], (0, 2, 1)).reshape(NH * CH, CH)
    bspt = jnp.repeat(b_spatial[0].T, HD, axis=1)

    w1i, w1o = gathered(0, sh1)
    w1o = w1o.reshape(DFF, D)
    x1, f1, p1 = _ffn_fwd(x, sh1, sc1, gt1, g_pre_f1, g_post_f1, w1i, w1o)
    wmi, wmo, cw4 = gathered(1, gather_forward(1, x1))
    wmo = wmo.reshape(D, D)
    cw_full = jnp.transpose(cw4, (1, 0, 2)).reshape(CK, WB)
    cw_pad = jnp.pad(cw_full, ((0, HALO - CK), (0, 0)))
    u, v, a, g = _mix_in_fwd(x1, sh2, sc2, g_pre_m, wmi)
    x2, conv, yb, m = _mix_mid_fwd(x1, u, v, a, g, gt2, gmlp_norm_g, gmlp_norm_b, wcat, bspt, cw_pad, conv_b,
                                   conv_norm_g, conv_norm_b, g_out_a, g_out_b, wmo, g_post_m)
    w2i, w2o = gathered(2, gather_forward(2, x2))
    w2o = w2o.reshape(DFF, D)
    dx3, df2, p2, lsum, dg_post_f2, dgt3 = _ffn_fwd(x2, sh3, sc3, gt3, g_pre_f2, g_post_f2, w2i, w2o, target=loss_target)

    def chip4(pair, rows):
        return [arr.reshape(NCHIP, rows, arr.shape[-1]) for arr in pair]

    def scatter_start(tag, pairs, behind):
        srcs = [p[1] for p in pairs]
        lands = [lax.empty((3,) + a.shape[1:], a.dtype) for a in srcs]
        ssem, rsem, srcs, lands, token = _split_start("gs_start_" + tag, srcs, lands, _scatter_plan(len(srcs)),
                                                      3 * len(srcs), behind)
        return (srcs, lands, ssem, rsem), token

    def scatter_wait(tag, state, behind):
        srcs, lands, ssem, rsem = state
        return _split_wait("gs_wait_" + tag, srcs, lands, ssem, rsem, _scatter_plan(len(srcs)), behind)

    def allgather_start(tag, arrs, behind):
        lands = [lax.dynamic_update_index_in_dim(lax.empty((NDEV,) + a.shape, a.dtype), a, dev, 0) for a in arrs]
        ssem, rsem, srcs, lands, token = _split_start("small_start_" + tag, arrs, lands, _allgather_plan(len(arrs)),
                                                      7 * len(arrs), behind)
        return (srcs, lands, ssem, rsem), token

    def allgather_wait(tag, state, behind):
        srcs, lands, ssem, rsem = state
        return _split_wait("small_wait_" + tag, srcs, lands, ssem, rsem, _allgather_plan(len(srcs)), behind)

    out = {}
    dx2, dp2, h3, a2, dg_pre_f2, dsh3, dsc3 = _ffn_bwd(
        dx3, x2, None, p2, sh3, sc3, gt3, g_pre_f2, g_post_f2, w2i, w2o, df=df2)
    gw2i = _wgrad("wgrad_f2_in", h3.reshape(t, D), dp2.reshape(t, 2 * DFF), 2 * DFF // NCHIP, True)
    gw2o = chip4(_wgrad("wgrad_f2_out", a2.reshape(t, DFF), df2.reshape(t, D), D // 2, False), DFF // NCHIP)
    scat_f2, tok = scatter_start("f2", [gw2i, gw2o], dg_post_f2)
    dy, dm, dg_post_m, dgt2 = _mix_out_bwd(dx2, m, gt2 + tok[0, 0], g_post_m, wmo)
    gwmo = chip4(_wgrad("wgrad_mix_out", yb.reshape(t, D), dm.reshape(t, D), D // 2, False), D // NCHIP)
    (du, dv, dconv, dwcat, dbsp, dgn_g, dgn_b, dgo_a, dgo_b, dcn_g, dcn_b, dcb) = _mix_mid_bwd(
        dy, u, v, conv, gmlp_norm_g, gmlp_norm_b, wcat, wcat_t, bspt, conv_norm_g, conv_norm_b, g_out_a, g_out_b)
    dx1, dproj, h2, dg_pre_m, dsh2, dsc2, dcw = _mix_in_bwd(dx2, x1, du, dv, dconv, a, g, sh2, sc2, g_pre_m, wmi, cw_pad)
    gwmi = _wgrad("wgrad_mix_in", h2.reshape(t, D), dproj.reshape(t, 4 * WA), WA, True)
    scat_mix, tok = scatter_start("mix", [gwmi, gwmo], dg_pre_m)

    vec_grads = dict(g_pre_m=dg_pre_m, g_post_m=dg_post_m, g_pre_f2=dg_pre_f2, g_post_f2=dg_post_f2)
    pair_grads = dict(gmlp_norm_g=dgn_g, gmlp_norm_b=dgn_b, conv_b=dcb, conv_norm_g=dcn_g, conv_norm_b=dcn_b,
                      g_out_a=dgo_a, g_out_b=dgo_b)
    pack = _pack_small([vec_grads[n] for n in VEC_ORDER[2:]], [pair_grads[n] for n in PAIR_ORDER], dbsp, dwcat, lsum)
    dada_early = jnp.concatenate([q.reshape(nb, D) for q in (dsh2, dsc2, dgt2, dsh3, dsc3, dgt3)], axis=1)
    early, tok2 = allgather_start("early", [pack, dcw, dada_early.reshape(8, (nb * 6 * D) // 8)], tok)
    grad_x, dp1, h1, a1, df1, dg_pre_f1, dg_post_f1, dsh1, dsc1, dgt1 = _ffn_bwd(
        dx1, x, f1, p1, sh1 + tok2[0, 0], sc1, gt1, g_pre_f1, g_post_f1, w1i, w1o)
    late_pack = _pack_late([dg_pre_f1, dg_post_f1] + [q.reshape(nb, D) for q in (dsh1, dsc1, dgt1)])
    late, tok2 = allgather_start("late", [late_pack], dg_post_f1)
    gw1i = _wgrad("wgrad_f1_in", h1.reshape(t, D), dp1.reshape(t, 2 * DFF), 2 * DFF // NCHIP, True)
    gw1o = chip4(_wgrad("wgrad_f1_out", a1.reshape(t, DFF), df1.reshape(t, D), D // 2, False), DFF // NCHIP)
    def d2d_start(tag, srcs, lands, plan, behind):
        ssem, rsem, srcs, lands, token = _split_start("d2d_start_" + tag, srcs, lands, plan, len(srcs), behind)
        return (srcs, lands, ssem, rsem, plan), token

    def d2d_wait(tag, state, behind):
        srcs, lands, ssem, rsem, plan = state
        return _split_wait("d2d_wait_" + tag, srcs, lands, ssem, rsem, plan, behind)

    def swap_start(tag, parts, behind):
        return d2d_start(tag, parts, [lax.empty(a.shape, a.dtype) for a in parts], _swap_plan(len(parts)), behind)

    def sums(names, pairs, recv):
        return [_sum4("sum4_" + n, pairs[k][0], recv[k], j_arr) for k, n in enumerate(names)]

    def update(names, part, other):
        for k, n in enumerate(names):
            out[n] = tuple(r[None] for r in _adam_big("adam_" + n, wts[n][0], mom[n][0], var[n][0], part[k], other[k]))

    c_arr = ac.reshape(1).astype(jnp.int32)
    halves = [gw1i[1], gw1o[1]]
    pair_st, tok = d2d_start("pair", halves, [lax.empty((a.shape[0], a.shape[1] // 2, a.shape[2]), a.dtype) for a in halves],
                             _pair_plan([a.shape for a in halves]), tok2)
    names_f2, names_mix, names_f1 = ("w_f2_in", "w_f2_out"), ("w_mix_in", "w_mix_out"), ("w_f1_in", "w_f1_out")
    part_f2 = sums(names_f2, [gw2i, gw2o], scatter_wait("f2", scat_f2, tok))
    sib = d2d_wait("pair", pair_st, part_f2[1])
    pair_i = _pair_sum("pairsum_f1_in", gw1i[0], sib[0], c_arr)
    pair_o = _pair_sum("pairsum_f1_out", gw1o[0], sib[1], c_arr)
    scat_f1, tok = scatter_start("f1", [pair_i, pair_o], tok2)
    swap_f2, tok = swap_start("swap_f2", part_f2, tok)
    part_mix = sums(names_mix, [gwmi, gwmo], scatter_wait("mix", scat_mix, tok))
    swap_mix, tok = swap_start("swap_mix", part_mix, part_mix[1])

    pack_all, dcw_all, dada_early8 = allgather_wait("early", early, tok)
    (late_all,) = allgather_wait("late", late, pack_all)
    dada_late = jnp.transpose(late_all[:, 2:8, :].reshape(NDEV, 3, nb, D), (0, 2, 1, 3)).reshape(NDEV * nb, 3 * D)
    dada_all = jnp.concatenate([dada_late, dada_early8.reshape(NDEV * nb, 6 * D)], axis=1)
    dada_sh = lax.dynamic_slice(dada_all, (0, j_chip * ADA_SH), (NDEV * nb, ADA_SH))
    out["w_ada"] = tuple(r[None] for r in _ada_bwd_adam(c_all, dada_sh, w_ada[0], m_w_ada[0], v_w_ada[0]))
    update(names_f2, part_f2, d2d_wait("swap_f2", swap_f2, out["w_ada"][3]))
    update(names_mix, part_mix, d2d_wait("swap_mix", swap_mix, out["w_f2_out"][3]))

    mine = sums(names_f1, [pair_i, pair_o], scatter_wait("f1", scat_f1, out["w_mix_out"][3]))
    swap_f1, tok = swap_start("swap_f1", mine, mine[1])
    dcw_mine = lax.dynamic_slice(dcw_all, (0, 0, j_chip * (WB // NCHIP)), (NDEV, HALO, WB // NCHIP))
    small = {n: (wts[n], mom[n], var[n]) for n in list(VEC_ORDER) + list(PAIR_ORDER) + ["b_spatial", "w_spatial", "conv_w", "b_ada"]}
    small_out, loss = _small_adam(pack_all, late_all, dcw_mine, dada_all, small, tok)
    out.update(small_out)
    theirs = d2d_wait("swap_f1", swap_f1, out["b_ada"][3])
    for k, n in enumerate(names_f1):
        out[n] = tuple(r[None] for r in _adam_halves("adam_" + n, wts[n][0], mom[n][0], var[n][0], mine[k], theirs[k],
                                                     c_arr))

    res = [loss, grad_x]
    for k in range(4):
        res += [out[n][k] for n in WEIGHTS]
    return tuple(res)
```

```python
import functools

import jax
import jax.numpy as jnp
from jax import lax
from jax.experimental import pallas as pl
from jax.experimental.pallas import tpu as pltpu

D = 1024
DFF = 2816
WA = 512
WB = 512
NH = 8
HD = 64
CH = 128
CK = 31
HALO = 32
NMOD = 9
EPS = 1e-6
NCHIP = 4
NDEV = 8
FBLK = DFF // 2
ADA_SH = NMOD * D // NCHIP

LR, B1, B2, EPS_A, WD, STEP = 0.001, 0.9, 0.999, 1e-08, 0.01, 10

F32 = jnp.float32
BF16 = jnp.bfloat16
MESH = pl.DeviceIdType.MESH
ANY = pl.BlockSpec(memory_space=pl.ANY)
VMEM_FULL = pl.BlockSpec(memory_space=pltpu.VMEM)
VMEM_LIMIT = 56 * 1024 * 1024

NT = (((1,), (1,)), ((), ()))
TN = (((0,), (0,)), ((), ()))


def _dot(a, b):
    return jnp.dot(a, b, preferred_element_type=F32)


def _dot_nt(a, b):
    return lax.dot_general(a, b, NT, preferred_element_type=F32)


def _dot_tn(a, b):
    return lax.dot_general(a, b, TN, preferred_element_type=F32)


def _cparams():
    return pltpu.CompilerParams(vmem_limit_bytes=VMEM_LIMIT)


def _allgather8(name, arrs):
    n = len(arrs)
    remote = _allgather_plan(n)

    def plan(x, y, c, ins, outs):
        _, sends = remote(x, y, c, ins, outs)
        return [(ins[a], outs[a].at[4 * x + 2 * y + c]) for a in range(n)], sends

    shapes = [jax.ShapeDtypeStruct((NDEV,) + a.shape, a.dtype) for a in arrs]
    return _run_exchange(name, arrs, shapes, plan, n, 7 * n)


def _chip_relations(x, y):
    return [(1 - x, y), (x, 1 - y), (1 - x, 1 - y)]


def _exchange(name, arrs, out_shapes, plan):
    n = len(arrs)
    n_out = len(out_shapes)

    def body(*refs):
        ins, outs = refs[:n], refs[n:n + n_out]
        send_sems, recv_sems, local_sems = refs[n + n_out:]
        x, y, c = lax.axis_index("x"), lax.axis_index("y"), lax.axis_index("c")
        local, sends = plan(x, y, c, ins, outs)
        locs = [pltpu.make_async_copy(s, d, local_sems.at[i]) for i, (s, d) in enumerate(local)]
        for loc in locs:
            loc.start()
        cps = [pltpu.make_async_remote_copy(src_ref=s, dst_ref=d, send_sem=send_sems.at[i], recv_sem=recv_sems.at[i],
                                            device_id=peer, device_id_type=MESH)
               for i, (s, d, peer, _) in enumerate(sends)]
        for cp in cps:
            cp.start()
        for i, (s, _, peer, landing) in enumerate(sends):
            pltpu.make_async_remote_copy(src_ref=s, dst_ref=landing, send_sem=send_sems.at[i], recv_sem=recv_sems.at[i],
                                         device_id=peer, device_id_type=MESH).wait_recv()
        for cp in cps:
            cp.wait_send()
        for loc in locs:
            loc.wait()

    return n, n_out, body


def _run_exchange(name, arrs, out_shapes, plan, n_local, n_send):
    n, n_out, body = _exchange(name, arrs, out_shapes, plan)
    return pl.pallas_call(
        body, name=name, out_shape=out_shapes,
        in_specs=[ANY] * n, out_specs=[ANY] * n_out,
        scratch_shapes=[pltpu.SemaphoreType.DMA((n_send,)), pltpu.SemaphoreType.DMA((n_send,)),
                        pltpu.SemaphoreType.DMA((max(n_local, 1),))],
    )(*arrs)


def _chip_allgather(name, arrs, behind=()):
    n = len(arrs)

    def plan(x, y, c, ins, outs):
        j_me = 2 * x + y
        local = [(ins[a], outs[a].at[j_me]) for a in range(n)]
        sends = []
        for a in range(n):
            for (px, py) in _chip_relations(x, y):
                sends.append((ins[a], outs[a].at[j_me], (px, py, c), outs[a].at[2 * px + py]))
        return local, sends

    shapes = [jax.ShapeDtypeStruct((NCHIP,) + a.shape, a.dtype) for a in arrs]
    return _run_exchange(name, list(arrs) + list(behind), shapes, plan, n, 3 * n)


HBM = pl.BlockSpec(memory_space=pltpu.HBM)
SEM = pl.BlockSpec(memory_space=pltpu.SEMAPHORE)
EFFECT = pltpu.SideEffectType.DATAFLOW_SIDE_EFFECTING


def _split_start(name, srcs, lands, plan, n_send, after):
    n, nl = len(srcs), len(lands)

    def body(*refs):
        src, land = refs[:n], refs[n:n + nl]
        send_sems, recv_sems = refs[n + nl + 1], refs[n + nl + 2]
        token = refs[-2]
        local_sems = refs[-1]
        x, y, c = lax.axis_index("x"), lax.axis_index("y"), lax.axis_index("c")
        local, sends = plan(x, y, c, src, land)
        locs = [pltpu.make_async_copy(s, d, local_sems.at[i]) for i, (s, d) in enumerate(local)]
        for loc in locs:
            loc.start()
        for loc in locs:
            loc.wait()
        for i, (s, d, peer, _) in enumerate(sends):
            pltpu.make_async_remote_copy(src_ref=s, dst_ref=d, send_sem=send_sems.at[i], recv_sem=recv_sems.at[i],
                                         device_id=peer, device_id_type=MESH).start()
        token[...] = jnp.zeros_like(token)

    thru = [pltpu.HBM(a.shape, a.dtype) for a in lands]
    srcs = [pltpu.with_memory_space_constraint(a, pltpu.HBM) for a in srcs]
    res = pl.pallas_call(
        body, name=name,
        out_shape=(pltpu.SemaphoreType.DMA((n_send,)), pltpu.SemaphoreType.DMA((n_send,)), *thru,
                   jax.ShapeDtypeStruct((8, 128), F32)),
        in_specs=[HBM] * (n + nl) + [ANY],
        out_specs=(SEM, SEM, *([HBM] * nl), pl.BlockSpec(memory_space=pltpu.VMEM)),
        input_output_aliases={n + i: 2 + i for i in range(nl)},
        scratch_shapes=[pltpu.SemaphoreType.DMA((max(n, 1),))],
        compiler_params=pltpu.CompilerParams(has_side_effects=EFFECT),
    )(*srcs, *[pltpu.with_memory_space_constraint(a, pltpu.HBM) for a in lands], after)
    return res[0], res[1], srcs, list(res[2:2 + nl]), res[-1]


def _split_wait(name, srcs, lands, send_sems, recv_sems, plan, after):
    n, nl = len(srcs), len(lands)
    afters = list(after) if isinstance(after, (list, tuple)) else [after]

    def body(*refs):
        src, land = refs[:n], refs[n:n + nl]
        send_sems, recv_sems = refs[n + nl], refs[n + nl + 1]
        x, y, c = lax.axis_index("x"), lax.axis_index("y"), lax.axis_index("c")
        _, sends = plan(x, y, c, src, land)
        for i, (s, _, peer, landing) in enumerate(sends):
            cp = pltpu.make_async_remote_copy(src_ref=s, dst_ref=landing, send_sem=send_sems.at[i],
                                              recv_sem=recv_sems.at[i], device_id=peer, device_id_type=MESH)
            cp.wait_send()
            cp.wait_recv()

    thru = [pltpu.HBM(a.shape, a.dtype) for a in lands]
    res = pl.pallas_call(
        body, name=name, out_shape=tuple(thru),
        in_specs=[HBM] * (n + nl) + [SEM, SEM] + [ANY] * len(afters), out_specs=tuple([HBM] * nl),
        input_output_aliases={n + i: i for i in range(nl)},
        compiler_params=pltpu.CompilerParams(has_side_effects=EFFECT),
    )(*srcs, *lands, send_sems, recv_sems, *afters)
    return list(res)


def _split_forward(name, srcs, lands, send_a, recv_a, plan_a, plan_b, n_b, after):
    n, nl = len(srcs), len(lands)

    def body(*refs):
        src, land = refs[:n], refs[n:n + nl]
        send_a, recv_a = refs[n + nl], refs[n + nl + 1]
        send_b, recv_b = refs[n + nl + 3], refs[n + nl + 4]
        token = refs[-1]
        x, y, c = lax.axis_index("x"), lax.axis_index("y"), lax.axis_index("c")
        _, first = plan_a(x, y, c, src, land)
        for i, (s, _, peer, landing) in enumerate(first):
            cp = pltpu.make_async_remote_copy(src_ref=s, dst_ref=landing, send_sem=send_a.at[i],
                                              recv_sem=recv_a.at[i], device_id=peer, device_id_type=MESH)
            cp.wait_send()
            cp.wait_recv()
        _, second = plan_b(x, y, c, src, land)
        for i, (s, d, peer, _) in enumerate(second):
            pltpu.make_async_remote_copy(src_ref=s, dst_ref=d, send_sem=send_b.at[i], recv_sem=recv_b.at[i],
                                         device_id=peer, device_id_type=MESH).start()
        token[...] = jnp.zeros_like(token)

    thru = [pltpu.HBM(a.shape, a.dtype) for a in lands]
    res = pl.pallas_call(
        body, name=name,
        out_shape=(pltpu.SemaphoreType.DMA((n_b,)), pltpu.SemaphoreType.DMA((n_b,)), *thru,
                   jax.ShapeDtypeStruct((8, 128), F32)),
        in_specs=[HBM] * (n + nl) + [SEM, SEM, ANY],
        out_specs=(SEM, SEM, *([HBM] * nl), pl.BlockSpec(memory_space=pltpu.VMEM)),
        input_output_aliases={n + i: 2 + i for i in range(nl)},
        compiler_params=pltpu.CompilerParams(has_side_effects=EFFECT),
    )(*srcs, *lands, send_a, recv_a, after)
    return res[0], res[1], list(res[2:2 + nl]), res[-1]


def _gather_plans(shapes):
    n = len(shapes)

    def halves(a, c):
        rows = shapes[a][0] // 2
        return pl.ds(pl.multiple_of(c * rows, 16), rows), pl.ds(pl.multiple_of((1 - c) * rows, 16), rows)

    def split(a):
        return shapes[a][0] % 32 == 0

    def plan_a(x, y, c, src, land):
        j_me = 2 * x + y
        sends = []
        for a in range(n):
            for (px, py) in _chip_relations(x, y):
                if split(a):
                    mine, _ = halves(a, c)
                    sends.append((src[a].at[mine], land[a].at[j_me, mine], (px, py, c), land[a].at[2 * px + py, mine]))
                else:
                    sends.append((src[a], land[a].at[j_me], (px, py, c), land[a].at[2 * px + py]))
        return [], sends

    def plan_b(x, y, c, src, land):
        sends = []
        for a in range(n):
            if split(a):
                mine, other = halves(a, c)
                for (px, py) in _chip_relations(x, y):
                    j = 2 * px + py
                    sends.append((land[a].at[j, mine], land[a].at[j, mine], (x, y, 1 - c), land[a].at[j, other]))
        return [], sends

    n_b = 3 * sum(1 for a in range(n) if split(a))
    return plan_a, plan_b, n_b


def _allgather_plan(n):
    flips = [(dx, dy, dc) for dx in (0, 1) for dy in (0, 1) for dc in (0, 1) if dx or dy or dc]

    def plan(x, y, c, src, land):
        sends = []
        for a in range(n):
            for dx, dy, dc in flips:
                px, py, pc = x ^ dx, y ^ dy, c ^ dc
                sends.append((src[a], land[a].at[4 * x + 2 * y + c], (px, py, pc), land[a].at[4 * px + 2 * py + pc]))
        return [], sends

    return plan


def _scatter_plan(n):
    def plan(x, y, c, src, land):
        sends = []
        for a in range(n):
            for k, (px, py) in enumerate(_chip_relations(x, y)):
                sends.append((src[a].at[2 * px + py], land[a].at[k], (px, py, c), land[a].at[k]))
        return [], sends

    return plan


def _rms(x):
    r = lax.rsqrt(jnp.mean(x * x, axis=-1, keepdims=True) + EPS)
    return x * r, r


def _rms_bwd(dy, n, r, g):
    dg = jnp.sum(dy * n, axis=0, keepdims=True)
    dn = dy * g
    dx = r * (dn - n * jnp.mean(dn * n, axis=-1, keepdims=True))
    return dx, dg


def _ln(x):
    mu = jnp.mean(x, axis=-1, keepdims=True)
    xc = x - mu
    rstd = lax.rsqrt(jnp.mean(xc * xc, axis=-1, keepdims=True) + EPS)
    return xc * rstd, rstd


def _ln_bwd(dy, xhat, rstd, g):
    dg = jnp.sum(dy * xhat, axis=0, keepdims=True)
    db = jnp.sum(dy, axis=0, keepdims=True)
    dxh = dy * g
    dx = rstd * (dxh - jnp.mean(dxh, axis=-1, keepdims=True) - xhat * jnp.mean(dxh * xhat, axis=-1, keepdims=True))
    return dx, dg, db


def _sigmoid(x):
    return jax.nn.sigmoid(x)


def _dsilu(x, s):
    return s * (1.0 + x * (1.0 - s))


def _adam(w, g, m, v):
    m = B1 * m + (1.0 - B1) * g
    v = B2 * v + (1.0 - B2) * (g * g)
    m_hat = m / (1.0 - B1 ** STEP)
    v_hat = v / (1.0 - B2 ** STEP)
    delta = -LR * (m_hat / (jnp.sqrt(v_hat) + EPS_A) + WD * w)
    return delta, m, v


def _head_mask(shape):
    lane = lax.broadcasted_iota(jnp.int32, shape, len(shape) - 1)
    return [(lane >= h * HD) & (lane < (h + 1) * HD) for h in range(NH)]


def _first(b, i):
    return jnp.logical_and(b == 0, i == 0)


def _acc(ref, val, first):
    @pl.when(first)
    def _():
        ref[...] = val

    @pl.when(jnp.logical_not(first))
    def _():
        ref[...] += val


def _ada_fwd(c_all, w_sh, b_sh):
    nb = c_all.shape[0]
    tn = 768

    def body(c_ref, w_ref, b_ref, o_ref):
        cv = c_ref[...]
        cs = (cv * _sigmoid(cv)).astype(BF16)
        o_ref[...] = _dot(cs, w_ref[...].astype(BF16)) + b_ref[...]

    return pl.pallas_call(
        body, name="ada_fwd", grid=(ADA_SH // tn,),
        out_shape=jax.ShapeDtypeStruct((nb, ADA_SH), F32),
        in_specs=[pl.BlockSpec((nb, D), lambda j: (0, 0)), pl.BlockSpec((D, tn), lambda j: (0, j)),
                  pl.BlockSpec((1, tn), lambda j: (0, j))],
        out_specs=pl.BlockSpec((nb, tn), lambda j: (0, j)),
        compiler_params=_cparams(),
    )(c_all, w_sh, b_sh)


def _ada_bwd_adam(c_all, dada_sh, w, m, v):
    nb = c_all.shape[0]
    tn = 768

    def body(c_ref, d_ref, w_ref, m_ref, v_ref, g_out, d_out, m_out, v_out):
        cv = c_ref[...]
        cs = (cv * _sigmoid(cv)).astype(BF16)
        g = _dot_tn(cs, d_ref[...].astype(BF16))
        delta, m2, v2 = _adam(w_ref[...], g, m_ref[...], v_ref[...])
        g_out[...] = g
        d_out[...] = delta
        m_out[...] = m2
        v_out[...] = v2

    big = pl.BlockSpec((D, tn), lambda j: (0, j))
    shape = jax.ShapeDtypeStruct((D, ADA_SH), F32)
    return pl.pallas_call(
        body, name="ada_bwd_adam", grid=(ADA_SH // tn,),
        out_shape=[shape] * 4,
        in_specs=[pl.BlockSpec((nb, D), lambda j: (0, 0)), pl.BlockSpec((nb, tn), lambda j: (0, j)), big, big, big],
        out_specs=[big] * 4,
        compiler_params=_cparams(),
    )(c_all, dada_sh, w, m, v)


def _tok_specs(tm, width):
    return pl.BlockSpec((1, tm, width), lambda b, i: (b, i, 0))


def _mod_spec():
    return pl.BlockSpec((1, 1, D), lambda b, i: (b, 0, 0))


def _row_spec(width=D):
    return pl.BlockSpec((1, width), lambda b, i: (0, 0))


def _ffn_fwd(x, sh, sc, gt, g_pre, g_post, w_in4, w_out, target=None):
    nb, s, _ = x.shape
    tm = min(512, s)
    with_loss = target is not None

    def body(*refs):
        if with_loss:
            (x_ref, sh_ref, sc_ref, gt_ref, gpre_ref, gpost_ref, win_ref, wout_ref, tgt_ref,
             xo_ref, df_ref, p_ref, ls_ref, dgpost_ref, dgt_ref) = refs
        else:
            (x_ref, sh_ref, sc_ref, gt_ref, gpre_ref, gpost_ref, win_ref, wout_ref,
             xo_ref, f_ref, p_ref) = refs
        xv = x_ref[0]
        n, _ = _rms(xv)
        h = (n * gpre_ref[...]) * (1.0 + sc_ref[0]) + sh_ref[0]
        hb = h.astype(BF16)
        acc = jnp.zeros((tm, D), F32)
        for j in range(2):
            gate = _dot(hb, win_ref[j])
            up = _dot(hb, win_ref[2 + j])
            p_ref[0, :, j * FBLK:(j + 1) * FBLK] = gate.astype(BF16)
            p_ref[0, :, DFF + j * FBLK:DFF + (j + 1) * FBLK] = up.astype(BF16)
            a = (gate * _sigmoid(gate)) * up
            acc = acc + _dot(a.astype(BF16), wout_ref[j * FBLK:(j + 1) * FBLK, :])
        nf, q = _rms(acc)
        gpost = gpost_ref[...]
        half_gate = 0.5 * gt_ref[0]
        out = xv + half_gate * (nf * gpost)
        if with_loss:
            first = _first(pl.program_id(0), pl.program_id(1))
            err = out - tgt_ref[0]
            dout = err * (1.0 / D)
            xo_ref[0] = dout
            row = jnp.sum(err * err, axis=0, keepdims=True)
            part = row[:, 0:128]
            for k in range(1, D // 128):
                part = part + row[:, k * 128:(k + 1) * 128]
            _acc(ls_ref, part, first)
            df, dgpost = _rms_bwd(dout * half_gate, nf, q, gpost)
            df_ref[0] = df.astype(BF16)
            _acc(dgpost_ref, dgpost, first)
            _acc(dgt_ref, jnp.sum(dout * (0.5 * (nf * gpost)), axis=0, keepdims=True)[None], pl.program_id(1) == 0)
        else:
            f_ref[0] = acc
            xo_ref[0] = out

    in_specs = [_tok_specs(tm, D), _mod_spec(), _mod_spec(), _mod_spec(), _row_spec(), _row_spec(), VMEM_FULL, VMEM_FULL]
    args = [x, sh, sc, gt, g_pre, g_post, w_in4, w_out]
    out_shape = [jax.ShapeDtypeStruct((nb, s, D), F32), jax.ShapeDtypeStruct((nb, s, D), BF16 if with_loss else F32),
                 jax.ShapeDtypeStruct((nb, s, 2 * DFF), BF16)]
    out_specs = [_tok_specs(tm, D), _tok_specs(tm, D), _tok_specs(tm, 2 * DFF)]
    if with_loss:
        in_specs.append(_tok_specs(tm, D))
        args.append(target)
        out_shape += [jax.ShapeDtypeStruct((1, 128), F32), jax.ShapeDtypeStruct((1, D), F32),
                      jax.ShapeDtypeStruct((nb, 1, D), F32)]
        out_specs += [pl.BlockSpec((1, 128), lambda b, i: (0, 0)), _row_spec(), _mod_spec()]
    return pl.pallas_call(
        body, name="ffn_loss_fwd" if with_loss else "ffn_fwd", grid=(nb, s // tm),
        out_shape=out_shape, in_specs=in_specs, out_specs=out_specs,
        compiler_params=_cparams(),
    )(*args)


def _ffn_bwd(dxo, x, f, p, sh, sc, gt, g_pre, g_post, w_in4, w_out, df=None):
    nb, s, _ = x.shape
    tm = min(256, s)
    given = df is not None

    def body(*refs):
        if given:
            (dxo_ref, x_ref, dfin_ref, p_ref, sh_ref, sc_ref, gpre_ref, win_ref, wout_ref,
             dx_ref, dp_ref, h_ref, a_ref, dgpre_ref, dsh_ref, dsc_ref) = refs
        else:
            (dxo_ref, x_ref, f_ref, p_ref, sh_ref, sc_ref, gt_ref, gpre_ref, gpost_ref, win_ref, wout_ref,
             dx_ref, dp_ref, h_ref, a_ref, df_ref, dgpre_ref, dgpost_ref, dsh_ref, dsc_ref, dgt_ref) = refs
        b, i = pl.program_id(0), pl.program_id(1)
        dxo_v = dxo_ref[0]
        if given:
            dfb = dfin_ref[0]
        else:
            nf, q = _rms(f_ref[0])
            gpost = gpost_ref[...]
            dgt = jnp.sum(dxo_v * (0.5 * (nf * gpost)), axis=0, keepdims=True)
            do = dxo_v * (0.5 * gt_ref[0])
            dfv, dgpost = _rms_bwd(do, nf, q, gpost)
            dfb = dfv.astype(BF16)
            df_ref[0] = dfb
        xv = x_ref[0]
        n, r = _rms(xv)
        gpre = gpre_ref[...]
        ng = n * gpre
        scale1 = 1.0 + sc_ref[0]
        h = ng * scale1 + sh_ref[0]
        h_ref[0] = h.astype(BF16)
        dh = jnp.zeros((tm, D), F32)
        for j in range(2):
            gate = p_ref[0, :, j * FBLK:(j + 1) * FBLK].astype(F32)
            up = p_ref[0, :, DFF + j * FBLK:DFF + (j + 1) * FBLK].astype(F32)
            sg = _sigmoid(gate)
            act = gate * sg
            a_ref[0, :, j * FBLK:(j + 1) * FBLK] = (act * up).astype(BF16)
            da = _dot_nt(dfb, wout_ref[j * FBLK:(j + 1) * FBLK, :])
            dgate = (da * up * _dsilu(gate, sg)).astype(BF16)
            dup = (da * act).astype(BF16)
            dp_ref[0, :, j * FBLK:(j + 1) * FBLK] = dgate
            dp_ref[0, :, DFF + j * FBLK:DFF + (j + 1) * FBLK] = dup
            dh = dh + _dot_nt(dgate, win_ref[j]) + _dot_nt(dup, win_ref[2 + j])
        dsh = jnp.sum(dh, axis=0, keepdims=True)
        dsc = jnp.sum(dh * ng, axis=0, keepdims=True)
        dxn, dgpre = _rms_bwd(dh * scale1, n, r, gpre)
        dx_ref[0] = dxo_v + dxn
        _acc(dgpre_ref, dgpre, _first(b, i))
        _acc(dsh_ref, dsh[None], i == 0)
        _acc(dsc_ref, dsc[None], i == 0)
        if not given:
            _acc(dgpost_ref, dgpost, _first(b, i))
            _acc(dgt_ref, dgt[None], i == 0)

    tok = _tok_specs(tm, D)
    mod_shape = jax.ShapeDtypeStruct((nb, 1, D), F32)
    row_shape = jax.ShapeDtypeStruct((1, D), F32)
    big = [jax.ShapeDtypeStruct((nb, s, D), F32), jax.ShapeDtypeStruct((nb, s, 2 * DFF), BF16),
           jax.ShapeDtypeStruct((nb, s, D), BF16), jax.ShapeDtypeStruct((nb, s, DFF), BF16)]
    big_specs = [tok, _tok_specs(tm, 2 * DFF), tok, _tok_specs(tm, DFF)]
    if given:
        return pl.pallas_call(
            body, name="ffn_bwd_after_loss", grid=(nb, s // tm),
            out_shape=big + [row_shape, mod_shape, mod_shape],
            in_specs=[tok, tok, tok, _tok_specs(tm, 2 * DFF), _mod_spec(), _mod_spec(), _row_spec(), VMEM_FULL, VMEM_FULL],
            out_specs=big_specs + [_row_spec(), _mod_spec(), _mod_spec()],
            compiler_params=_cparams(),
        )(dxo, x, df, p, sh, sc, g_pre, w_in4, w_out)
    return pl.pallas_call(
        body, name="ffn_bwd", grid=(nb, s // tm),
        out_shape=big + [jax.ShapeDtypeStruct((nb, s, D), BF16), row_shape, row_shape, mod_shape, mod_shape, mod_shape],
        in_specs=[tok, tok, tok, _tok_specs(tm, 2 * DFF), _mod_spec(), _mod_spec(), _mod_spec(), _row_spec(), _row_spec(),
                  VMEM_FULL, VMEM_FULL],
        out_specs=big_specs + [tok, _row_spec(), _row_spec(), _mod_spec(), _mod_spec(), _mod_spec()],
        compiler_params=_cparams(),
    )(dxo, x, f, p, sh, sc, gt, g_pre, g_post, w_in4, w_out)


def _wgrad(name, a, b, col_block, chip_major):
    t, ka = a.shape
    n = b.shape[1]
    tk = min(t, 512)
    while tk * 2 <= t and t % (tk * 2) == 0 and 2 * (tk * 2) * max(ka, col_block) <= 6 * 1024 * 1024:
        tk *= 2
    nk = t // tk
    nblk = n // col_block

    def body(a_ref, b_ref, o_ref, obf_ref, acc_ref):
        k = pl.program_id(1)

        @pl.when(k == 0)
        def _():
            acc_ref[...] = jnp.zeros_like(acc_ref)

        acc_ref[...] += _dot_tn(a_ref[...], b_ref[...])

        @pl.when(k == nk - 1)
        def _():
            val = acc_ref[...]
            if chip_major:
                o_ref[0] = val
                obf_ref[0] = val.astype(BF16)
            else:
                o_ref[...] = val
                obf_ref[...] = val.astype(BF16)

    if chip_major:
        shape = (nblk, ka, col_block)
        ospec = pl.BlockSpec((1, ka, col_block), lambda j, k: (j, 0, 0))
    else:
        shape = (ka, n)
        ospec = pl.BlockSpec((ka, col_block), lambda j, k: (0, j))
    return pl.pallas_call(
        body, name=name, grid=(nblk, nk),
        out_shape=[jax.ShapeDtypeStruct(shape, F32), jax.ShapeDtypeStruct(shape, BF16)],
        in_specs=[pl.BlockSpec((tk, ka), lambda j, k: (k, 0)), pl.BlockSpec((tk, col_block), lambda j, k: (k, j))],
        out_specs=[ospec, ospec],
        scratch_shapes=[pltpu.VMEM((ka, col_block), F32)],
        compiler_params=_cparams(),
    )(a, b)


def _mix_in_fwd(x, sh, sc, g_pre, w_mi4):
    nb, s, _ = x.shape
    tm = min(512, s)

    def body(x_ref, sh_ref, sc_ref, gpre_ref, w_ref, u_ref, v_ref, a_ref, g_ref):
        n, _ = _rms(x_ref[0])
        hb = ((n * gpre_ref[...]) * (1.0 + sc_ref[0]) + sh_ref[0]).astype(BF16)
        for k, o_ref in enumerate((u_ref, v_ref, a_ref, g_ref)):
            o_ref[0] = _dot(hb, w_ref[k])

    shape = jax.ShapeDtypeStruct((nb, s, WA), F32)
    return pl.pallas_call(
        body, name="mix_in_fwd", grid=(nb, s // tm),
        out_shape=[shape] * 4,
        in_specs=[_tok_specs(tm, D), _mod_spec(), _mod_spec(), _row_spec(), VMEM_FULL],
        out_specs=[_tok_specs(tm, WA)] * 4,
        compiler_params=_cparams(),
    )(x, sh, sc, g_pre, w_mi4)


def _spatial_weights(wcat_ref, transposed):
    w = wcat_ref[...]
    row = lax.broadcasted_iota(jnp.int32, w.shape, 0)
    col = lax.broadcasted_iota(jnp.int32, w.shape, 1)
    keep = ((row & (CH - 1)) <= col) if transposed else ((col & (CH - 1)) <= row)
    return jnp.where(keep, w, 0.0).astype(BF16)


def _expand_heads(vc, masks):
    return jnp.concatenate([jnp.where(mk, vc, jnp.zeros_like(vc)) for mk in masks], axis=0)


def _spatial_bias(bspt_ref):
    return bspt_ref[...]


SHIFTS = 8
TAP_ROWS = 32


def _ext_rows(tm):
    return tm + HALO + SHIFTS


def _make_shifts(ext_ref, sh_ref, tm):
    ext_ref[tm + HALO:tm + HALO + SHIFTS, :] = jnp.zeros((SHIFTS, WB), F32)
    for r in range(SHIFTS):
        sh_ref[r] = ext_ref[r:r + tm + HALO, :]


def _conv_taps(sh_ref, w_ref, tm, taps, emit):
    def block(i, carry):
        r0 = pl.multiple_of(i * TAP_ROWS, TAP_ROWS)
        acc = jnp.zeros((TAP_ROWS, WB), F32)
        for o, k in taps:
            acc = acc + w_ref[k:k + 1, :] * sh_ref[o % SHIFTS, pl.ds(r0 + SHIFTS * (o // SHIFTS), TAP_ROWS), :]
        emit(r0, acc)
        return carry

    lax.fori_loop(0, tm // TAP_ROWS, block, 0)


def _halo_prev_spec(tm):
    return pl.BlockSpec((1, HALO, WB), lambda b, i: (b, jnp.maximum(i * (tm // HALO) - 1, 0), 0))


def _halo_next_spec(tm, s):
    return pl.BlockSpec((1, HALO, WB), lambda b, i: (b, jnp.minimum((i + 1) * (tm // HALO), s // HALO - 1), 0))


def _mix_mid_fwd(x, u, v, a, g, gt, gn_g, gn_b, wcat, bspt, conv_w, conv_b, cn_g, cn_b, go_a, go_b, w_mo, g_post):
    nb, s, _ = x.shape
    tm = min(512, s)

    def body(x_ref, u_ref, v_ref, a_ref, g_ref, ah_ref, gh_ref, gt_ref, gng_ref, gnb_ref, wcat_ref, bspt_ref,
             cw_ref, cb_ref, cng_ref, cnb_ref, goa_ref, gob_ref, wmo_ref, gpost_ref,
             xo_ref, conv_ref, y_ref, m_ref, ext_ref, sh_ref):
        i = pl.program_id(1)
        xhat, _ = _ln(v_ref[0])
        vb = (xhat * gng_ref[...] + gnb_ref[...]).astype(BF16)
        wsb = _spatial_weights(wcat_ref, False)
        bias = _spatial_bias(bspt_ref)
        masks = _head_mask((CH, WA))
        zs = []
        for cidx in range(tm // CH):
            vexp = _expand_heads(vb[cidx * CH:(cidx + 1) * CH, :], masks)
            zs.append(_dot(wsb, vexp) + bias)
        z = jnp.concatenate(zs, axis=0)
        na, _ = _rms(u_ref[0] * z)
        keep = jnp.where(i == 0, 0.0, 1.0).astype(F32)
        ext_ref[0:HALO, :] = (ah_ref[0] * _sigmoid(gh_ref[0])) * keep
        ext_ref[HALO:HALO + tm, :] = a_ref[0] * _sigmoid(g_ref[0])
        _make_shifts(ext_ref, sh_ref, tm)
        cb = cb_ref[...]

        def put_conv(r0, acc):
            conv_ref[0, pl.ds(r0, TAP_ROWS), :] = acc + cb

        _conv_taps(sh_ref, cw_ref, tm, [(k + HALO - (CK - 1), k) for k in range(CK)], put_conv)
        conv = conv_ref[0]
        chat, _ = _ln(conv)
        cln = chat * cng_ref[...] + cnb_ref[...]
        nbb, _ = _rms(cln * _sigmoid(cln))
        yb = jnp.concatenate([na * goa_ref[...], nbb * gob_ref[...]], axis=1).astype(BF16)
        y_ref[0] = yb
        m = _dot(yb, wmo_ref[...])
        m_ref[0] = m
        nm, _ = _rms(m)
        xo_ref[0] = x_ref[0] + gt_ref[0] * (nm * gpost_ref[...])

    t5 = _tok_specs(tm, WA)
    tok = _tok_specs(tm, D)
    r5 = _row_spec(WA)
    full = lambda shape: pl.BlockSpec(shape, lambda b, i: (0,) * len(shape))
    return pl.pallas_call(
        body, name="mix_mid_fwd", grid=(nb, s // tm),
        out_shape=[jax.ShapeDtypeStruct((nb, s, D), F32), jax.ShapeDtypeStruct((nb, s, WB), F32),
                   jax.ShapeDtypeStruct((nb, s, D), BF16), jax.ShapeDtypeStruct((nb, s, D), F32)],
        in_specs=[tok, t5, t5, t5, t5, _halo_prev_spec(tm), _halo_prev_spec(tm), _mod_spec(), r5, r5,
                  full((CH, NH * CH)), full((CH, WA)), full((HALO, WB)), r5, r5, r5, r5, r5, VMEM_FULL, _row_spec()],
        out_specs=[tok, t5, tok, tok],
        scratch_shapes=[pltpu.VMEM((_ext_rows(tm), WB), F32), pltpu.VMEM((SHIFTS, tm + HALO, WB), F32)],
        compiler_params=_cparams(),
    )(x, u, v, a, g, a, g, gt, gn_g, gn_b, wcat, bspt, conv_w, conv_b, cn_g, cn_b, go_a, go_b, w_mo, g_post)


def _mix_out_bwd(dxo, m, gt, g_post, w_mo):
    nb, s, _ = m.shape
    tm = min(512, s)

    def body(dxo_ref, m_ref, gt_ref, gpost_ref, wmo_ref, dy_ref, dm_ref, dgpost_ref, dgt_ref):
        b, i = pl.program_id(0), pl.program_id(1)
        dxo_v = dxo_ref[0]
        nm, q = _rms(m_ref[0])
        gpost = gpost_ref[...]
        dgt = jnp.sum(dxo_v * (nm * gpost), axis=0, keepdims=True)
        dm, dgpost = _rms_bwd(dxo_v * gt_ref[0], nm, q, gpost)
        dmb = dm.astype(BF16)
        dm_ref[0] = dmb
        dy_ref[0] = _dot_nt(dmb, wmo_ref[...])
        _acc(dgpost_ref, dgpost, _first(b, i))
        _acc(dgt_ref, dgt[None], i == 0)

    tok = _tok_specs(tm, D)
    return pl.pallas_call(
        body, name="mix_out_bwd", grid=(nb, s // tm),
        out_shape=[jax.ShapeDtypeStruct((nb, s, D), F32), jax.ShapeDtypeStruct((nb, s, D), BF16),
                   jax.ShapeDtypeStruct((1, D), F32), jax.ShapeDtypeStruct((nb, 1, D), F32)],
        in_specs=[tok, tok, _mod_spec(), _row_spec(), VMEM_FULL],
        out_specs=[tok, tok, _row_spec(), _mod_spec()],
        compiler_params=_cparams(),
    )(dxo, m, gt, g_post, w_mo)


def _mix_mid_bwd(dy, u, v, conv, gn_g, gn_b, wcat, wcat_t, bspt, cn_g, cn_b, go_a, go_b):
    nb, s, _ = dy.shape
    tm = min(512, s)
    nchunk = tm // CH

    def body(dy_ref, u_ref, v_ref, conv_ref, gng_ref, gnb_ref, wcat_ref, wcatt_ref, bspt_ref, cng_ref, cnb_ref,
             goa_ref, gob_ref,
             du_ref, dv_ref, dconv_ref, dwcat_ref, dbsp_ref, dgng_ref, dgnb_ref, dgoa_ref, dgob_ref,
             dcng_ref, dcnb_ref, dcb_ref):
        first = _first(pl.program_id(0), pl.program_id(1))
        dyv = dy_ref[0]
        xhat, rstd = _ln(v_ref[0])
        gng = gng_ref[...]
        vb = (xhat * gng + gnb_ref[...]).astype(BF16)
        wsb = _spatial_weights(wcat_ref, False)
        wsb_t = _spatial_weights(wcatt_ref, True)
        bias = _spatial_bias(bspt_ref)
        masks = _head_mask((CH, WA))
        vexps, zs = [], []
        for cidx in range(nchunk):
            vexp = _expand_heads(vb[cidx * CH:(cidx + 1) * CH, :], masks)
            vexps.append(vexp)
            zs.append(_dot(wsb, vexp) + bias)
        z = jnp.concatenate(zs, axis=0)
        uv = u_ref[0]
        na, ra = _rms(uv * z)
        dya, dgoa = _rms_bwd(dyv[:, 0:WA], na, ra, goa_ref[...])
        du_ref[0] = dya * z
        dz = dya * uv
        dwcat = jnp.zeros((CH, NH * CH), F32)
        dzsum = jnp.zeros((CH, WA), F32)
        dvlns = []
        for cidx in range(nchunk):
            dzc = dz[cidx * CH:(cidx + 1) * CH, :]
            dzsum = dzsum + dzc
            dzb = dzc.astype(BF16)
            dwcat = dwcat + _dot_nt(dzb, vexps[cidx])
            dvexp = _dot(wsb_t, dzb)
            dvl = jnp.zeros((CH, WA), F32)
            for h in range(NH):
                dvl = dvl + jnp.where(masks[h], dvexp[h * CH:(h + 1) * CH, :], 0.0)
            dvlns.append(dvl)
        dvln = jnp.concatenate(dvlns, axis=0)
        dv, dgng, dgnb = _ln_bwd(dvln, xhat, rstd, gng)
        dv_ref[0] = dv
        lane = lax.broadcasted_iota(jnp.int32, (NH, WA), 1)
        head = lax.broadcasted_iota(jnp.int32, (NH, WA), 0)
        sel = jnp.where((lane >= head * HD) & (lane < (head + 1) * HD), 1.0, 0.0).astype(F32)
        dbsp = lax.dot_general(sel, dzsum, NT, preferred_element_type=F32, precision=lax.Precision.HIGHEST)
        chat, crstd = _ln(conv_ref[0])
        cng = cng_ref[...]
        cln = chat * cng + cnb_ref[...]
        sg = _sigmoid(cln)
        nbb, rb = _rms(cln * sg)
        dyb, dgob = _rms_bwd(dyv[:, WA:D], nbb, rb, gob_ref[...])
        dconv, dcng, dcnb = _ln_bwd(dyb * _dsilu(cln, sg), chat, crstd, cng)
        dconv_ref[0] = dconv
        dcb = jnp.sum(dconv, axis=0, keepdims=True)
        for ref, val in ((dwcat_ref, dwcat), (dbsp_ref, dbsp), (dgng_ref, dgng), (dgnb_ref, dgnb), (dgoa_ref, dgoa),
                         (dgob_ref, dgob), (dcng_ref, dcng), (dcnb_ref, dcnb), (dcb_ref, dcb)):
            _acc(ref, val, first)

    t5 = _tok_specs(tm, WA)
    r5 = _row_spec(WA)
    full = lambda shape: pl.BlockSpec(shape, lambda b, i: (0,) * len(shape))
    big = jax.ShapeDtypeStruct((nb, s, WA), F32)
    row = jax.ShapeDtypeStruct((1, WA), F32)
    return pl.pallas_call(
        body, name="mix_mid_bwd", grid=(nb, s // tm),
        out_shape=[big, big, big, jax.ShapeDtypeStruct((CH, NH * CH), F32), jax.ShapeDtypeStruct((NH, CH), F32),
                   row, row, row, row, row, row, row],
        in_specs=[_tok_specs(tm, D), t5, t5, t5, r5, r5, full((CH, NH * CH)), full((NH * CH, CH)), full((CH, WA)),
                  r5, r5, r5, r5],
        out_specs=[t5, t5, t5, full((CH, NH * CH)), full((NH, CH)), r5, r5, r5, r5, r5, r5, r5],
        compiler_params=_cparams(),
    )(dy, u, v, conv, gn_g, gn_b, wcat, wcat_t, bspt, cn_g, cn_b, go_a, go_b)


def _mix_in_bwd(dxo, x, du, dv, dconv, a, g, sh, sc, g_pre, w_mi4, conv_w):
    nb, s, _ = x.shape
    tm = min(512, s)
    n_i = s // tm

    def body(dxo_ref, x_ref, du_ref, dv_ref, dc_ref, dch_ref, a_ref, g_ref, ah_ref, gh_ref, sh_ref, sc_ref,
             gpre_ref, w_ref, cw_ref,
             dx_ref, dproj_ref, h_ref, dgpre_ref, dsh_ref, dsc_ref, dcw_ref, ext_ref, shf_ref, dglu_ref):
        b, i = pl.program_id(0), pl.program_id(1)
        first = _first(b, i)
        av, gv = a_ref[0], g_ref[0]
        sg = _sigmoid(gv)
        dconv = dc_ref[0]
        ext_ref[0:tm, :] = dconv
        ext_ref[tm:tm + HALO, :] = dch_ref[0] * jnp.where(i == n_i - 1, 0.0, 1.0).astype(F32)
        _make_shifts(ext_ref, shf_ref, tm)

        def put_dglu(r0, acc):
            dglu_ref[pl.ds(r0, TAP_ROWS), :] = acc

        _conv_taps(shf_ref, cw_ref, tm, [(CK - 1 - k, k) for k in range(CK)], put_dglu)
        dglu = dglu_ref[...]
        ext_ref[0:HALO, :] = (ah_ref[0] * _sigmoid(gh_ref[0])) * jnp.where(i == 0, 0.0, 1.0).astype(F32)
        ext_ref[HALO:HALO + tm, :] = av * sg
        _make_shifts(ext_ref, shf_ref, tm)

        @pl.when(first)
        def _():
            dcw_ref[...] = jnp.zeros((HALO, WB), F32)

        for k in range(CK):
            o = k + HALO - (CK - 1)
            lo = SHIFTS * (o // SHIFTS)
            dcw_ref[k:k + 1, :] += jnp.sum(dconv * shf_ref[o % SHIFTS, lo:lo + tm, :], axis=0, keepdims=True)
        da = dglu * sg
        dg = dglu * av * (sg * (1.0 - sg))
        parts = [du_ref[0].astype(BF16), dv_ref[0].astype(BF16), da.astype(BF16), dg.astype(BF16)]
        dh = jnp.zeros((tm, D), F32)
        for k in range(4):
            dproj_ref[0, :, k * WA:(k + 1) * WA] = parts[k]
            dh = dh + _dot_nt(parts[k], w_ref[k])
        n, r = _rms(x_ref[0])
        gpre = gpre_ref[...]
        ng = n * gpre
        scale1 = 1.0 + sc_ref[0]
        h_ref[0] = (ng * scale1 + sh_ref[0]).astype(BF16)
        dsh = jnp.sum(dh, axis=0, keepdims=True)
        dsc = jnp.sum(dh * ng, axis=0, keepdims=True)
        dxn, dgpre = _rms_bwd(dh * scale1, n, r, gpre)
        dx_ref[0] = dxo_ref[0] + dxn
        _acc(dgpre_ref, dgpre, first)
        _acc(dsh_ref, dsh[None], i == 0)
        _acc(dsc_ref, dsc[None], i == 0)

    tok = _tok_specs(tm, D)
    t5 = _tok_specs(tm, WA)
    full = lambda shape: pl.BlockSpec(shape, lambda b, i: (0,) * len(shape))
    mod_shape = jax.ShapeDtypeStruct((nb, 1, D), F32)
    return pl.pallas_call(
        body, name="mix_in_bwd", grid=(nb, n_i),
        out_shape=[jax.ShapeDtypeStruct((nb, s, D), F32), jax.ShapeDtypeStruct((nb, s, 4 * WA), BF16),
                   jax.ShapeDtypeStruct((nb, s, D), BF16), jax.ShapeDtypeStruct((1, D), F32), mod_shape, mod_shape,
                   jax.ShapeDtypeStruct((HALO, WB), F32)],
        in_specs=[tok, tok, t5, t5, t5, _halo_next_spec(tm, s), t5, t5, _halo_prev_spec(tm), _halo_prev_spec(tm),
                  _mod_spec(), _mod_spec(), _row_spec(), VMEM_FULL, full((HALO, WB))],
        out_specs=[tok, _tok_specs(tm, 4 * WA), tok, _row_spec(), _mod_spec(), _mod_spec(), full((HALO, WB))],
        scratch_shapes=[pltpu.VMEM((_ext_rows(tm), WB), F32), pltpu.VMEM((SHIFTS, tm + HALO, WB), F32),
                        pltpu.VMEM((tm, WB), F32)],
        compiler_params=_cparams(),
    )(dxo, x, du, dv, dconv, dconv, a, g, a, g, sh, sc, g_pre, w_mi4, conv_w)


def _row_tile(rows, cols):
    best = 16
    for t in range(16, rows + 1, 16):
        if rows % t == 0 and t * cols * 4 <= 1536 * 1024:
            best = t
    return best


def _sum4(name, own4, recv, j_arr):
    _, rows, cols = own4.shape
    tr = _row_tile(rows, cols)

    def body(j_ref, own_ref, recv_ref, o_ref):
        del j_ref
        acc = own_ref[0]
        for k in range(3):
            acc = acc + recv_ref[k].astype(F32)
        o_ref[...] = acc

    return pl.pallas_call(
        body, name=name,
        grid_spec=pltpu.PrefetchScalarGridSpec(
            num_scalar_prefetch=1, grid=(rows // tr,),
            in_specs=[pl.BlockSpec((1, tr, cols), lambda i, j: (j[0], i, 0)),
                      pl.BlockSpec((3, tr, cols), lambda i, j: (0, i, 0))],
            out_specs=pl.BlockSpec((tr, cols), lambda i, j: (i, 0))),
        out_shape=jax.ShapeDtypeStruct((rows, cols), F32),
        compiler_params=_cparams(),
    )(j_arr, own4, recv)


def _pair_plan(shapes):
    def plan(x, y, c, src, land):
        sends = []
        for a, shape in enumerate(shapes):
            rows = shape[1] // 2
            theirs = pl.ds(pl.multiple_of((1 - c) * rows, 16), rows)
            sends.append((src[a].at[:, theirs], land[a], (x, y, 1 - c), land[a]))
        return [], sends

    return plan


def _swap_plan(n):
    def plan(x, y, c, src, land):
        return [], [(src[a], land[a], (x, y, 1 - c), land[a]) for a in range(n)]

    return plan


def _pair_sum(name, g32, recv, c_arr):
    nblk, rows, cols = recv.shape
    tr = _row_tile(rows, cols)
    nh = rows // tr

    def body(c_ref, g_ref, r_ref, o32_ref, obf_ref):
        del c_ref
        val = g_ref[0] + r_ref[0].astype(F32)
        o32_ref[0] = val
        obf_ref[0] = val.astype(BF16)

    spec = pl.BlockSpec((1, tr, cols), lambda k, i, c: (k, i, 0))
    return pl.pallas_call(
        body, name=name,
        grid_spec=pltpu.PrefetchScalarGridSpec(
            num_scalar_prefetch=1, grid=(nblk, nh),
            in_specs=[pl.BlockSpec((1, tr, cols), lambda k, i, c: (k, c[0] * nh + i, 0)), spec],
            out_specs=[spec, spec]),
        out_shape=[jax.ShapeDtypeStruct(recv.shape, F32), jax.ShapeDtypeStruct(recv.shape, BF16)],
        compiler_params=_cparams(),
    )(c_arr, g32, recv)


def _adam_halves(name, w, m, v, mine, theirs, c_arr):
    rows, cols = w.shape
    tr = _row_tile(rows // 2, cols)
    nh = (rows // 2) // tr

    def body(c_ref, w_ref, m_ref, v_ref, mine_ref, theirs_ref, g_out, d_out, m_out, v_out):
        here = (pl.program_id(0) // nh) == c_ref[0]
        g = jnp.where(here, mine_ref[...], theirs_ref[...])
        delta, m2, v2 = _adam(w_ref[...], g, m_ref[...], v_ref[...])
        g_out[...] = g
        d_out[...] = delta
        m_out[...] = m2
        v_out[...] = v2

    spec = pl.BlockSpec((tr, cols), lambda i, c: (i, 0))
    shape = jax.ShapeDtypeStruct((rows, cols), F32)
    return pl.pallas_call(
        body, name=name,
        grid_spec=pltpu.PrefetchScalarGridSpec(
            num_scalar_prefetch=1, grid=(2 * nh,),
            in_specs=[spec, spec, spec,
                      pl.BlockSpec((tr, cols), lambda i, c: (jnp.clip(i - c[0] * nh, 0, nh - 1), 0)),
                      pl.BlockSpec((tr, cols), lambda i, c: (jnp.clip(i - (1 - c[0]) * nh, 0, nh - 1), 0))],
            out_specs=[spec] * 4),
        out_shape=[shape] * 4,
        compiler_params=_cparams(),
    )(c_arr, w, m, v, mine, theirs)


def _adam_big(name, w, m, v, ga, gb):
    rows, cols = w.shape
    tr = _row_tile(rows, cols)

    def body(w_ref, m_ref, v_ref, ga_ref, gb_ref, g_out, d_out, m_out, v_out):
        gsum = ga_ref[...] + gb_ref[...]
        delta, m2, v2 = _adam(w_ref[...], gsum, m_ref[...], v_ref[...])
        g_out[...] = gsum
        d_out[...] = delta
        m_out[...] = m2
        v_out[...] = v2

    spec = pl.BlockSpec((tr, cols), lambda i: (i, 0))
    shape = jax.ShapeDtypeStruct((rows, cols), F32)
    return pl.pallas_call(
        body, name=name, grid=(rows // tr,), out_shape=[shape] * 4,
        in_specs=[spec] * 5, out_specs=[spec] * 4, compiler_params=_cparams(),
    )(w, m, v, ga, gb)


PK_VEC = 0
PK_LOSS = 6
PK_PAIR = 8
PK_BSP = 16
PK_WCAT = 24
PK_ROWS = PK_WCAT + CH
PAIR_ORDER = ("gmlp_norm_g", "gmlp_norm_b", "conv_b", "conv_norm_g", "conv_norm_b", "g_out_a", "g_out_b")
VEC_ORDER = ("g_pre_f1", "g_post_f1", "g_pre_m", "g_post_m", "g_pre_f2", "g_post_f2")


def _pack_late(rows):
    counts = [r.shape[0] for r in rows]
    assert sum(counts) == 8

    def body(*refs):
        o_ref = refs[-1]
        at = 0
        for r, cnt in zip(refs[:-1], counts):
            o_ref[at:at + cnt, :] = r[...]
            at += cnt

    return pl.pallas_call(
        body, name="pack_late", out_shape=jax.ShapeDtypeStruct((8, D), F32),
        in_specs=[VMEM_FULL] * len(rows), out_specs=VMEM_FULL, compiler_params=_cparams(),
    )(*rows)


def _pack_small(vecs, pairs, dbsp, dwcat, lsum):
    def body(*refs):
        vec_refs = refs[:4]
        pair_refs = refs[4:11]
        dbsp_ref, dwcat_ref, lsum_ref, o_ref = refs[11:]
        o_ref[0:PK_WCAT, :] = jnp.zeros((PK_WCAT, D), F32)
        o_ref[PK_LOSS:PK_LOSS + 1, 0:128] = lsum_ref[...]
        for k, r in enumerate(vec_refs):
            o_ref[PK_VEC + 2 + k:PK_VEC + 3 + k, :] = r[...]
        for k, r in enumerate(pair_refs):
            row, half = PK_PAIR + k // 2, k % 2
            o_ref[row:row + 1, half * WA:(half + 1) * WA] = r[...]
        o_ref[PK_BSP:PK_BSP + NH, 0:CH] = dbsp_ref[...]
        o_ref[PK_WCAT:PK_ROWS, :] = dwcat_ref[...]

    args = list(vecs) + list(pairs) + [dbsp, dwcat, lsum]
    return pl.pallas_call(
        body, name="pack_small", out_shape=jax.ShapeDtypeStruct((PK_ROWS, D), F32),
        in_specs=[VMEM_FULL] * len(args), out_specs=VMEM_FULL, compiler_params=_cparams(),
    )(*args)


def _small_adam(pack_all, late_all, dcw_all, dada_all, params, behind):
    names = list(VEC_ORDER) + list(PAIR_ORDER) + ["b_spatial", "w_spatial", "conv_w", "b_ada"]
    flat = []
    for nm in names:
        flat += list(params[nm])
    n_in = 4 + len(flat)

    def body(*refs):
        pack_ref, late_ref, dcw_ref, dada_ref = refs[:4]
        prm = refs[4:n_in]
        outs = refs[n_in + 1:]

        def total(r0, nr, c0, nc):
            acc = pack_ref[0, r0:r0 + nr, c0:c0 + nc]
            for d in range(1, NDEV):
                acc = acc + pack_ref[d, r0:r0 + nr, c0:c0 + nc]
            return acc

        def emit(idx, g, getw, put):
            w_ref, m_ref, v_ref = prm[3 * idx:3 * idx + 3]
            delta, m2, v2 = _adam(getw(w_ref), g, getw(m_ref), getw(v_ref))
            for o_ref, val in zip(outs[4 * idx:4 * idx + 4], (g, delta, m2, v2)):
                put(o_ref, val)

        def whole(ref):
            return ref[...]

        def put_whole(ref, val):
            ref[...] = val

        idx = 0
        for k in range(6):
            if k < 2:
                g = late_ref[0, k:k + 1, :]
                for d in range(1, NDEV):
                    g = g + late_ref[d, k:k + 1, :]
            else:
                g = total(PK_VEC + k, 1, 0, D)
            emit(idx, g, whole, put_whole)
            idx += 1
        for k in range(7):
            emit(idx, total(PK_PAIR + k // 2, 1, (k % 2) * WA, WA), whole, put_whole)
            idx += 1
        emit(idx, total(PK_BSP, NH, 0, CH), lambda r: r[0], lambda r, val: r.__setitem__(0, val))
        idx += 1
        row = lax.broadcasted_iota(jnp.int32, (CH, CH), 0)
        col = lax.broadcasted_iota(jnp.int32, (CH, CH), 1)
        for h in range(NH):
            gh = jnp.where(col <= row, total(PK_WCAT, CH, h * CH, CH), 0.0)
            w_ref, m_ref, v_ref = prm[3 * idx:3 * idx + 3]
            delta, m2, v2 = _adam(w_ref[0, h], gh, m_ref[0, h], v_ref[0, h])
            for o_ref, val in zip(outs[4 * idx:4 * idx + 4], (gh, delta, m2, v2)):
                o_ref[0, h] = val
        idx += 1
        gcw = dcw_ref[0, 0:CK, :]
        for d in range(1, NDEV):
            gcw = gcw + dcw_ref[d, 0:CK, :]
        emit(idx, gcw, lambda r: r[0], lambda r, val: r.__setitem__(0, val))
        idx += 1
        emit(idx, jnp.sum(dada_ref[...], axis=0, keepdims=True), whole, put_whole)
        outs[-1][...] = jnp.sum(total(PK_LOSS, 1, 0, 128), axis=1, keepdims=True) * (0.5 / D)

    out_shape = []
    for nm in names:
        w = params[nm][0]
        out_shape += [jax.ShapeDtypeStruct(w.shape, F32)] * 4
    out_shape.append(jax.ShapeDtypeStruct((1, 1), F32))
    res = pl.pallas_call(
        body, name="small_adam", out_shape=out_shape,
        in_specs=[VMEM_FULL] * n_in + [ANY], out_specs=[VMEM_FULL] * len(out_shape), compiler_params=_cparams(),
    )(pack_all, late_all, dcw_all, dada_all, *flat, behind)
    return {nm: tuple(res[4 * k:4 * k + 4]) for k, nm in enumerate(names)}, res[-1].reshape(())


WEIGHTS = ['w_ada', 'b_ada', 'g_pre_f1', 'g_post_f1', 'w_f1_in', 'w_f1_out', 'g_pre_m', 'g_post_m', 'w_mix_in',
           'gmlp_norm_g', 'gmlp_norm_b', 'w_spatial', 'b_spatial', 'conv_w', 'conv_b', 'conv_norm_g', 'conv_norm_b',
           'g_out_a', 'g_out_b', 'w_mix_out', 'g_pre_f2', 'g_post_f2', 'w_f2_in', 'w_f2_out']
BIG = ('w_f1_in', 'w_f1_out', 'w_mix_in', 'w_mix_out', 'w_f2_in', 'w_f2_out')


def kernel(x, c, w_ada, b_ada, g_pre_f1, g_post_f1, w_f1_in, w_f1_out, g_pre_m, g_post_m, w_mix_in, gmlp_norm_g, gmlp_norm_b, w_spatial, b_spatial, conv_w, conv_b, conv_norm_g, conv_norm_b, g_out_a, g_out_b, w_mix_out, g_pre_f2, g_post_f2, w_f2_in, w_f2_out, loss_target, m_w_ada, m_b_ada, m_g_pre_f1, m_g_post_f1, m_w_f1_in, m_w_f1_out, m_g_pre_m, m_g_post_m, m_w_mix_in, m_gmlp_norm_g, m_gmlp_norm_b, m_w_spatial, m_b_spatial, m_conv_w, m_conv_b, m_conv_norm_g, m_conv_norm_b, m_g_out_a, m_g_out_b, m_w_mix_out, m_g_pre_f2, m_g_post_f2, m_w_f2_in, m_w_f2_out, v_w_ada, v_b_ada, v_g_pre_f1, v_g_post_f1, v_w_f1_in, v_w_f1_out, v_g_pre_m, v_g_post_m, v_w_mix_in, v_gmlp_norm_g, v_gmlp_norm_b, v_w_spatial, v_b_spatial, v_conv_w, v_conv_b, v_conv_norm_g, v_conv_norm_b, v_g_out_a, v_g_out_b, v_w_mix_out, v_g_pre_f2, v_g_post_f2, v_w_f2_in, v_w_f2_out):
    env = dict(locals())
    wts = {n: env[n] for n in WEIGHTS}
    mom = {n: env["m_" + n] for n in WEIGHTS}
    var = {n: env["v_" + n] for n in WEIGHTS}
    nb, s, _ = x.shape
    t = nb * s
    ax, ay, ac = lax.axis_index("x"), lax.axis_index("y"), lax.axis_index("c")
    j_chip = 2 * ax + ay
    dev = 4 * ax + 2 * ay + ac
    j_arr = j_chip.reshape(1).astype(jnp.int32)

    groups = (("w_f1_in", "w_f1_out"), ("w_mix_in", "w_mix_out"), ("w_f2_in", "w_f2_out"))
    def gather_operands(gi):
        srcs = [wts[n][0].astype(BF16) for n in groups[gi]] + ([conv_w[0]] if gi == 1 else [])
        lands = [lax.dynamic_update_index_in_dim(lax.empty((NCHIP,) + a.shape, a.dtype), a, j_chip, 0) for a in srcs]
        return srcs, lands

    def gather_start(gi, behind, operands=None):
        srcs, lands = operands or gather_operands(gi)
        plan_a, plan_b, n_b = _gather_plans([a.shape for a in srcs])
        ssem, rsem, srcs, lands, token = _split_start("gw_start%d" % gi, srcs, lands, plan_a, 3 * len(srcs), behind)
        gather[gi] = (srcs, lands, ssem, rsem, plan_a, plan_b, n_b)
        return token

    def gather_forward(gi, behind):
        srcs, lands, ssem, rsem, plan_a, plan_b, n_b = gather[gi]
        ssem, rsem, lands, token = _split_forward("gw_fwd%d" % gi, srcs, lands, ssem, rsem, plan_a, plan_b, n_b, behind)
        gather[gi] = (lands, ssem, rsem, plan_b)
        return token

    def gathered(gi, behind):
        lands, ssem, rsem, plan_b = gather[gi]
        return _split_wait("gw_wait%d" % gi, [], lands, ssem, rsem, plan_b, behind)

    gather = {}
    (c_all8,) = _allgather8("gather_c", [c.reshape(8, (nb * D) // 8)])
    token = gather_start(0, c_all8)
    c_all = c_all8.reshape(NDEV * nb, D) + token[0, 0]
    b_sh = lax.dynamic_slice(b_ada, (0, j_chip * ADA_SH), (1, ADA_SH))
    ada_sh = _ada_fwd(c_all, w_ada[0], b_sh)
    later = [gather_operands(1), gather_operands(2)]
    (ada4,) = _chip_allgather("gather_ada", [ada_sh], behind=[a for pair in later for arrs in pair for a in arrs])
    token = gather_forward(0, ada4)
    token = gather_start(1, token, later[0])
    token = gather_start(2, token, later[1])
    ada_me = lax.dynamic_slice(ada4, (0, dev * nb, 0), (NCHIP, nb, ADA_SH))
    ada_me = jnp.transpose(ada_me, (1, 0, 2)).reshape(nb, NMOD * D)
    sh1, sc1, gt1, sh2, sc2, gt2, sh3, sc3, gt3 = [ada_me[:, k * D:(k + 1) * D].reshape(nb, 1, D) for k in range(NMOD)]

    wcat = jnp.transpose(w_spatial[0], (1, 0, 2)).reshape(CH, NH * CH)
    wcat_t = jnp.transpose(w_spatial[0], (0, 2, 1)).reshape(NH * CH, CH)
    bspt = jnp.repeat(b_spatial[0].T, HD, axis=1)

    w1i, w1o = gathered(0, token)
    w1o = w1o.reshape(DFF, D)
    x1, f1, p1 = _ffn_fwd(x, sh1, sc1, gt1, g_pre_f1, g_post_f1, w1i, w1o)
    wmi, wmo, cw4 = gathered(1, gather_forward(1, x1))
    wmo = wmo.reshape(D, D)
    cw_full = jnp.transpose(cw4, (1, 0, 2)).reshape(CK, WB)
    cw_pad = jnp.pad(cw_full, ((0, HALO - CK), (0, 0)))
    u, v, a, g = _mix_in_fwd(x1, sh2, sc2, g_pre_m, wmi)
    x2, conv, yb, m = _mix_mid_fwd(x1, u, v, a, g, gt2, gmlp_norm_g, gmlp_norm_b, wcat, bspt, cw_pad, conv_b,
                                   conv_norm_g, conv_norm_b, g_out_a, g_out_b, wmo, g_post_m)
    w2i, w2o = gathered(2, gather_forward(2, x2))
    w2o = w2o.reshape(DFF, D)
    dx3, df2, p2, lsum, dg_post_f2, dgt3 = _ffn_fwd(x2, sh3, sc3, gt3, g_pre_f2, g_post_f2, w2i, w2o, target=loss_target)

    def chip4(pair, rows):
        return [arr.reshape(NCHIP, rows, arr.shape[-1]) for arr in pair]

    def scatter_start(tag, pairs, behind):
        srcs = [p[1] for p in pairs]
        lands = [lax.empty((3,) + a.shape[1:], a.dtype) for a in srcs]
        ssem, rsem, srcs, lands, token = _split_start("gs_start_" + tag, srcs, lands, _scatter_plan(len(srcs)),
                                                      3 * len(srcs), behind)
        return (srcs, lands, ssem, rsem), token

    def scatter_wait(tag, state, behind):
        srcs, lands, ssem, rsem = state
        return _split_wait("gs_wait_" + tag, srcs, lands, ssem, rsem, _scatter_plan(len(srcs)), behind)

    def allgather_start(tag, arrs, behind):
        lands = [lax.dynamic_update_index_in_dim(lax.empty((NDEV,) + a.shape, a.dtype), a, dev, 0) for a in arrs]
        ssem, rsem, srcs, lands, token = _split_start("small_start_" + tag, arrs, lands, _allgather_plan(len(arrs)),
                                                      7 * len(arrs), behind)
        return (srcs, lands, ssem, rsem), token

    def allgather_wait(tag, state, behind):
        srcs, lands, ssem, rsem = state
        return _split_wait("small_wait_" + tag, srcs, lands, ssem, rsem, _allgather_plan(len(srcs)), behind)

    out = {}
    dx2, dp2, h3, a2, dg_pre_f2, dsh3, dsc3 = _ffn_bwd(
        dx3, x2, None, p2, sh3, sc3, gt3, g_pre_f2, g_post_f2, w2i, w2o, df=df2)
    gw2i = _wgrad("wgrad_f2_in", h3.reshape(t, D), dp2.reshape(t, 2 * DFF), 2 * DFF // NCHIP, True)
    gw2o = chip4(_wgrad("wgrad_f2_out", a2.reshape(t, DFF), df2.reshape(t, D), D // 2, False), DFF // NCHIP)
    scat_f2, tok = scatter_start("f2", [gw2i, gw2o], dg_post_f2)
    dy, dm, dg_post_m, dgt2 = _mix_out_bwd(dx2, m, gt2 + tok[0, 0], g_post_m, wmo)
    gwmo = chip4(_wgrad("wgrad_mix_out", yb.reshape(t, D), dm.reshape(t, D), D // 2, False), D // NCHIP)
    (du, dv, dconv, dwcat, dbsp, dgn_g, dgn_b, dgo_a, dgo_b, dcn_g, dcn_b, dcb) = _mix_mid_bwd(
        dy, u, v, conv, gmlp_norm_g, gmlp_norm_b, wcat, wcat_t, bspt, conv_norm_g, conv_norm_b, g_out_a, g_out_b)
    dx1, dproj, h2, dg_pre_m, dsh2, dsc2, dcw = _mix_in_bwd(dx2, x1, du, dv, dconv, a, g, sh2, sc2, g_pre_m, wmi, cw_pad)
    gwmi = _wgrad("wgrad_mix_in", h2.reshape(t, D), dproj.reshape(t, 4 * WA), WA, True)
    scat_mix, tok = scatter_start("mix", [gwmi, gwmo], dg_pre_m)

    vec_grads = dict(g_pre_m=dg_pre_m, g_post_m=dg_post_m, g_pre_f2=dg_pre_f2, g_post_f2=dg_post_f2)
    pair_grads = dict(gmlp_norm_g=dgn_g, gmlp_norm_b=dgn_b, conv_b=dcb, conv_norm_g=dcn_g, conv_norm_b=dcn_b,
                      g_out_a=dgo_a, g_out_b=dgo_b)
    pack = _pack_small([vec_grads[n] for n in VEC_ORDER[2:]], [pair_grads[n] for n in PAIR_ORDER], dbsp, dwcat, lsum)
    dada_early = jnp.concatenate([q.reshape(nb, D) for q in (dsh2, dsc2, dgt2, dsh3, dsc3, dgt3)], axis=1)
    early, tok2 = allgather_start("early", [pack, dcw, dada_early.reshape(8, (nb * 6 * D) // 8)], tok)
    grad_x, dp1, h1, a1, df1, dg_pre_f1, dg_post_f1, dsh1, dsc1, dgt1 = _ffn_bwd(
        dx1, x, f1, p1, sh1 + tok2[0, 0], sc1, gt1, g_pre_f1, g_post_f1, w1i, w1o)
    late_pack = _pack_late([dg_pre_f1, dg_post_f1] + [q.reshape(nb, D) for q in (dsh1, dsc1, dgt1)])
    late, tok2 = allgather_start("late", [late_pack], dg_post_f1)
    gw1i = _wgrad("wgrad_f1_in", h1.reshape(t, D), dp1.reshape(t, 2 * DFF), 2 * DFF // NCHIP, True)
    gw1o = chip4(_wgrad("wgrad_f1_out", a1.reshape(t, DFF), df1.reshape(t, D), D // 2, False), DFF // NCHIP)
    def d2d_start(tag, srcs, lands, plan, behind):
        ssem, rsem, srcs, lands, token = _split_start("d2d_start_" + tag, srcs, lands, plan, len(srcs), behind)
        return (srcs, lands, ssem, rsem, plan), token

    def d2d_wait(tag, state, behind):
        srcs, lands, ssem, rsem, plan = state
        return _split_wait("d2d_wait_" + tag, srcs, lands, ssem, rsem, plan, behind)

    def swap_start(tag, parts, behind):
        return d2d_start(tag, parts, [lax.empty(a.shape, a.dtype) for a in parts], _swap_plan(len(parts)), behind)

    def sums(names, pairs, recv):
        return [_sum4("sum4_" + n, pairs[k][0], recv[k], j_arr) for k, n in enumerate(names)]

    def update(names, part, other):
        for k, n in enumerate(names):
            out[n] = tuple(r[None] for r in _adam_big("adam_" + n, wts[n][0], mom[n][0], var[n][0], part[k], other[k]))

    c_arr = ac.reshape(1).astype(jnp.int32)
    halves = [gw1i[1], gw1o[1]]
    pair_st, tok = d2d_start("pair", halves, [lax.empty((a.shape[0], a.shape[1] // 2, a.shape[2]), a.dtype) for a in halves],
                             _pair_plan([a.shape for a in halves]), tok2)
    names_f2, names_mix, names_f1 = ("w_f2_in", "w_f2_out"), ("w_mix_in", "w_mix_out"), ("w_f1_in", "w_f1_out")
    part_f2 = sums(names_f2, [gw2i, gw2o], scatter_wait("f2", scat_f2, tok))
    sib = d2d_wait("pair", pair_st, part_f2)
    pair_i = _pair_sum("pairsum_f1_in", gw1i[0], sib[0], c_arr)
    pair_o = _pair_sum("pairsum_f1_out", gw1o[0], sib[1], c_arr)
    scat_f1, tok = scatter_start("f1", [pair_i, pair_o], tok2)
    swap_f2, tok = swap_start("swap_f2", part_f2, tok)
    part_mix = sums(names_mix, [gwmi, gwmo], scatter_wait("mix", scat_mix, tok))
    swap_mix, tok = swap_start("swap_mix", part_mix, part_mix[1])

    pack_all, dcw_all, dada_early8 = allgather_wait("early", early, tok)
    (late_all,) = allgather_wait("late", late, pack_all)
    dada_late = jnp.transpose(late_all[:, 2:8, :].reshape(NDEV, 3, nb, D), (0, 2, 1, 3)).reshape(NDEV * nb, 3 * D)
    dada_all = jnp.concatenate([dada_late, dada_early8.reshape(NDEV * nb, 6 * D)], axis=1)
    dada_sh = lax.dynamic_slice(dada_all, (0, j_chip * ADA_SH), (NDEV * nb, ADA_SH))
    out["w_ada"] = tuple(r[None] for r in _ada_bwd_adam(c_all, dada_sh, w_ada[0], m_w_ada[0], v_w_ada[0]))
    update(names_f2, part_f2, d2d_wait("swap_f2", swap_f2, out["w_ada"][3]))
    update(names_mix, part_mix, d2d_wait("swap_mix", swap_mix, out["w_f2_out"][3]))

    mine = sums(names_f1, [pair_i, pair_o], scatter_wait("f1", scat_f1, out["w_mix_out"][3]))
    swap_f1, tok = swap_start("swap_f1", mine, mine[1])
    dcw_mine = lax.dynamic_slice(dcw_all, (0, 0, j_chip * (WB // NCHIP)), (NDEV, HALO, WB // NCHIP))
    small = {n: (wts[n], mom[n], var[n]) for n in list(VEC_ORDER) + list(PAIR_ORDER) + ["b_spatial", "w_spatial", "conv_w", "b_ada"]}
    small_out, loss = _small_adam(pack_all, late_all, dcw_mine, dada_all, small, tok)
    out.update(small_out)
    theirs = d2d_wait("swap_f1", swap_f1, out["b_ada"][3])
    for k, n in enumerate(names_f1):
        out[n] = tuple(r[None] for r in _adam_halves("adam_" + n, wts[n][0], mom[n][0], var[n][0], mine[k], theirs[k],
                                                     c_arr))

    res = [loss, grad_x]
    for k in range(4):
        res += [out[n][k] for n in WEIGHTS]
    return tuple(res)
```

```python
import functools

import jax
import jax.numpy as jnp
from jax import lax
from jax.experimental import pallas as pl
from jax.experimental.pallas import tpu as pltpu

D = 1024
DFF = 2816
WA = 512
WB = 512
NH = 8
HD = 64
CH = 128
CK = 31
HALO = 32
NMOD = 9
EPS = 1e-6
NCHIP = 4
NDEV = 8
FBLK = DFF // 2
ADA_SH = NMOD * D // NCHIP

LR, B1, B2, EPS_A, WD, STEP = 0.001, 0.9, 0.999, 1e-08, 0.01, 10

F32 = jnp.float32
BF16 = jnp.bfloat16
MESH = pl.DeviceIdType.MESH
ANY = pl.BlockSpec(memory_space=pl.ANY)
VMEM_FULL = pl.BlockSpec(memory_space=pltpu.VMEM)
VMEM_LIMIT = 56 * 1024 * 1024

NT = (((1,), (1,)), ((), ()))
TN = (((0,), (0,)), ((), ()))


def _dot(a, b):
    return jnp.dot(a, b, preferred_element_type=F32)


def _dot_nt(a, b):
    return lax.dot_general(a, b, NT, preferred_element_type=F32)


def _dot_tn(a, b):
    return lax.dot_general(a, b, TN, preferred_element_type=F32)


def _cparams():
    return pltpu.CompilerParams(vmem_limit_bytes=VMEM_LIMIT)


def _allgather8(name, arrs):
    n = len(arrs)
    remote = _allgather_plan(n)

    def plan(x, y, c, ins, outs):
        _, sends = remote(x, y, c, ins, outs)
        return [(ins[a], outs[a].at[4 * x + 2 * y + c]) for a in range(n)], sends

    shapes = [jax.ShapeDtypeStruct((NDEV,) + a.shape, a.dtype) for a in arrs]
    return _run_exchange(name, arrs, shapes, plan, n, 7 * n)


def _chip_relations(x, y):
    return [(1 - x, y), (x, 1 - y), (1 - x, 1 - y)]


def _exchange(name, arrs, out_shapes, plan):
    n = len(arrs)
    n_out = len(out_shapes)

    def body(*refs):
        ins, outs = refs[:n], refs[n:n + n_out]
        send_sems, recv_sems, local_sems = refs[n + n_out:]
        x, y, c = lax.axis_index("x"), lax.axis_index("y"), lax.axis_index("c")
        local, sends = plan(x, y, c, ins, outs)
        locs = [pltpu.make_async_copy(s, d, local_sems.at[i]) for i, (s, d) in enumerate(local)]
        for loc in locs:
            loc.start()
        cps = [pltpu.make_async_remote_copy(src_ref=s, dst_ref=d, send_sem=send_sems.at[i], recv_sem=recv_sems.at[i],
                                            device_id=peer, device_id_type=MESH)
               for i, (s, d, peer, _) in enumerate(sends)]
        for cp in cps:
            cp.start()
        for i, (s, _, peer, landing) in enumerate(sends):
            pltpu.make_async_remote_copy(src_ref=s, dst_ref=landing, send_sem=send_sems.at[i], recv_sem=recv_sems.at[i],
                                         device_id=peer, device_id_type=MESH).wait_recv()
        for cp in cps:
            cp.wait_send()
        for loc in locs:
            loc.wait()

    return n, n_out, body


def _run_exchange(name, arrs, out_shapes, plan, n_local, n_send):
    n, n_out, body = _exchange(name, arrs, out_shapes, plan)
    return pl.pallas_call(
        body, name=name, out_shape=out_shapes,
        in_specs=[ANY] * n, out_specs=[ANY] * n_out,
        scratch_shapes=[pltpu.SemaphoreType.DMA((n_send,)), pltpu.SemaphoreType.DMA((n_send,)),
                        pltpu.SemaphoreType.DMA((max(n_local, 1),))],
    )(*arrs)


def _chip_allgather(name, arrs, behind=()):
    n = len(arrs)

    def plan(x, y, c, ins, outs):
        j_me = 2 * x + y
        local = [(ins[a], outs[a].at[j_me]) for a in range(n)]
        sends = []
        for a in range(n):
            for (px, py) in _chip_relations(x, y):
                sends.append((ins[a], outs[a].at[j_me], (px, py, c), outs[a].at[2 * px + py]))
        return local, sends

    shapes = [jax.ShapeDtypeStruct((NCHIP,) + a.shape, a.dtype) for a in arrs]
    return _run_exchange(name, list(arrs) + list(behind), shapes, plan, n, 3 * n)


HBM = pl.BlockSpec(memory_space=pltpu.HBM)
SEM = pl.BlockSpec(memory_space=pltpu.SEMAPHORE)
EFFECT = pltpu.SideEffectType.DATAFLOW_SIDE_EFFECTING


def _split_start(name, srcs, lands, plan, n_send, after):
    n, nl = len(srcs), len(lands)

    def body(*refs):
        src, land = refs[:n], refs[n:n + nl]
        send_sems, recv_sems = refs[n + nl + 1], refs[n + nl + 2]
        token = refs[-2]
        local_sems = refs[-1]
        x, y, c = lax.axis_index("x"), lax.axis_index("y"), lax.axis_index("c")
        local, sends = plan(x, y, c, src, land)
        locs = [pltpu.make_async_copy(s, d, local_sems.at[i]) for i, (s, d) in enumerate(local)]
        for loc in locs:
            loc.start()
        for loc in locs:
            loc.wait()
        for i, (s, d, peer, _) in enumerate(sends):
            pltpu.make_async_remote_copy(src_ref=s, dst_ref=d, send_sem=send_sems.at[i], recv_sem=recv_sems.at[i],
                                         device_id=peer, device_id_type=MESH).start()
        token[...] = jnp.zeros_like(token)

    thru = [pltpu.HBM(a.shape, a.dtype) for a in lands]
    srcs = [pltpu.with_memory_space_constraint(a, pltpu.HBM) for a in srcs]
    res = pl.pallas_call(
        body, name=name,
        out_shape=(pltpu.SemaphoreType.DMA((n_send,)), pltpu.SemaphoreType.DMA((n_send,)), *thru,
                   jax.ShapeDtypeStruct((8, 128), F32)),
        in_specs=[HBM] * (n + nl) + [ANY],
        out_specs=(SEM, SEM, *([HBM] * nl), pl.BlockSpec(memory_space=pltpu.VMEM)),
        input_output_aliases={n + i: 2 + i for i in range(nl)},
        scratch_shapes=[pltpu.SemaphoreType.DMA((max(n, 1),))],
        compiler_params=pltpu.CompilerParams(has_side_effects=EFFECT),
    )(*srcs, *[pltpu.with_memory_space_constraint(a, pltpu.HBM) for a in lands], after)
    return res[0], res[1], srcs, list(res[2:2 + nl]), res[-1]


def _split_wait(name, srcs, lands, send_sems, recv_sems, plan, after):
    n, nl = len(srcs), len(lands)
    afters = list(after) if isinstance(after, (list, tuple)) else [after]

    def body(*refs):
        src, land = refs[:n], refs[n:n + nl]
        send_sems, recv_sems = refs[n + nl], refs[n + nl + 1]
        x, y, c = lax.axis_index("x"), lax.axis_index("y"), lax.axis_index("c")
        _, sends = plan(x, y, c, src, land)
        for i, (s, _, peer, landing) in enumerate(sends):
            cp = pltpu.make_async_remote_copy(src_ref=s, dst_ref=landing, send_sem=send_sems.at[i],
                                              recv_sem=recv_sems.at[i], device_id=peer, device_id_type=MESH)
            cp.wait_send()
            cp.wait_recv()

    thru = [pltpu.HBM(a.shape, a.dtype) for a in lands]
    res = pl.pallas_call(
        body, name=name, out_shape=tuple(thru),
        in_specs=[HBM] * (n + nl) + [SEM, SEM] + [ANY] * len(afters), out_specs=tuple([HBM] * nl),
        input_output_aliases={n + i: i for i in range(nl)},
        compiler_params=pltpu.CompilerParams(has_side_effects=EFFECT),
    )(*srcs, *lands, send_sems, recv_sems, *afters)
    return list(res)


def _split_forward(name, srcs, lands, send_a, recv_a, plan_a, plan_b, n_b, after):
    n, nl = len(srcs), len(lands)

    def body(*refs):
        src, land = refs[:n], refs[n:n + nl]
        send_a, recv_a = refs[n + nl], refs[n + nl + 1]
        send_b, recv_b = refs[n + nl + 3], refs[n + nl + 4]
        token = refs[-1]
        x, y, c = lax.axis_index("x"), lax.axis_index("y"), lax.axis_index("c")
        _, first = plan_a(x, y, c, src, land)
        for i, (s, _, peer, landing) in enumerate(first):
            cp = pltpu.make_async_remote_copy(src_ref=s, dst_ref=landing, send_sem=send_a.at[i],
                                              recv_sem=recv_a.at[i], device_id=peer, device_id_type=MESH)
            cp.wait_send()
            cp.wait_recv()
        _, second = plan_b(x, y, c, src, land)
        for i, (s, d, peer, _) in enumerate(second):
            pltpu.make_async_remote_copy(src_ref=s, dst_ref=d, send_sem=send_b.at[i], recv_sem=recv_b.at[i],
                                         device_id=peer, device_id_type=MESH).start()
        token[...] = jnp.zeros_like(token)

    thru = [pltpu.HBM(a.shape, a.dtype) for a in lands]
    res = pl.pallas_call(
        body, name=name,
        out_shape=(pltpu.SemaphoreType.DMA((n_b,)), pltpu.SemaphoreType.DMA((n_b,)), *thru,
                   jax.ShapeDtypeStruct((8, 128), F32)),
        in_specs=[HBM] * (n + nl) + [SEM, SEM, ANY],
        out_specs=(SEM, SEM, *([HBM] * nl), pl.BlockSpec(memory_space=pltpu.VMEM)),
        input_output_aliases={n + i: 2 + i for i in range(nl)},
        compiler_params=pltpu.CompilerParams(has_side_effects=EFFECT),
    )(*srcs, *lands, send_a, recv_a, after)
    return res[0], res[1], list(res[2:2 + nl]), res[-1]


def _gather_plans(shapes):
    n = len(shapes)

    def halves(a, c):
        rows = shapes[a][0] // 2
        return pl.ds(pl.multiple_of(c * rows, 16), rows), pl.ds(pl.multiple_of((1 - c) * rows, 16), rows)

    def split(a):
        return shapes[a][0] % 32 == 0

    def plan_a(x, y, c, src, land):
        j_me = 2 * x + y
        sends = []
        for a in range(n):
            for (px, py) in _chip_relations(x, y):
                if split(a):
                    mine, _ = halves(a, c)
                    sends.append((src[a].at[mine], land[a].at[j_me, mine], (px, py, c), land[a].at[2 * px + py, mine]))
                else:
                    sends.append((src[a], land[a].at[j_me], (px, py, c), land[a].at[2 * px + py]))
        return [], sends

    def plan_b(x, y, c, src, land):
        sends = []
        for a in range(n):
            if split(a):
                mine, other = halves(a, c)
                for (px, py) in _chip_relations(x, y):
                    j = 2 * px + py
                    sends.append((land[a].at[j, mine], land[a].at[j, mine], (x, y, 1 - c), land[a].at[j, other]))
        return [], sends

    n_b = 3 * sum(1 for a in range(n) if split(a))
    return plan_a, plan_b, n_b


def _allgather_plan(n):
    flips = [(dx, dy, dc) for dx in (0, 1) for dy in (0, 1) for dc in (0, 1) if dx or dy or dc]

    def plan(x, y, c, src, land):
        sends = []
        for a in range(n):
            for dx, dy, dc in flips:
                px, py, pc = x ^ dx, y ^ dy, c ^ dc
                sends.append((src[a], land[a].at[4 * x + 2 * y + c], (px, py, pc), land[a].at[4 * px + 2 * py + pc]))
        return [], sends

    return plan


def _scatter_plan(n):
    def plan(x, y, c, src, land):
        sends = []
        for a in range(n):
            for k, (px, py) in enumerate(_chip_relations(x, y)):
                sends.append((src[a].at[2 * px + py], land[a].at[k], (px, py, c), land[a].at[k]))
        return [], sends

    return plan


def _rms(x):
    r = lax.rsqrt(jnp.mean(x * x, axis=-1, keepdims=True) + EPS)
    return x * r, r


def _rms_bwd(dy, n, r, g):
    dg = jnp.sum(dy * n, axis=0, keepdims=True)
    dn = dy * g
    dx = r * (dn - n * jnp.mean(dn * n, axis=-1, keepdims=True))
    return dx, dg


def _ln(x):
    mu = jnp.mean(x, axis=-1, keepdims=True)
    xc = x - mu
    rstd = lax.rsqrt(jnp.mean(xc * xc, axis=-1, keepdims=True) + EPS)
    return xc * rstd, rstd


def _ln_bwd(dy, xhat, rstd, g):
    dg = jnp.sum(dy * xhat, axis=0, keepdims=True)
    db = jnp.sum(dy, axis=0, keepdims=True)
    dxh = dy * g
    dx = rstd * (dxh - jnp.mean(dxh, axis=-1, keepdims=True) - xhat * jnp.mean(dxh * xhat, axis=-1, keepdims=True))
    return dx, dg, db


def _sigmoid(x):
    return jax.nn.sigmoid(x)


def _dsilu(x, s):
    return s * (1.0 + x * (1.0 - s))


def _adam(w, g, m, v):
    m = B1 * m + (1.0 - B1) * g
    v = B2 * v + (1.0 - B2) * (g * g)
    m_hat = m / (1.0 - B1 ** STEP)
    v_hat = v / (1.0 - B2 ** STEP)
    delta = -LR * (m_hat / (jnp.sqrt(v_hat) + EPS_A) + WD * w)
    return delta, m, v


def _head_mask(shape):
    lane = lax.broadcasted_iota(jnp.int32, shape, len(shape) - 1)
    return [(lane >= h * HD) & (lane < (h + 1) * HD) for h in range(NH)]


def _first(b, i):
    return jnp.logical_and(b == 0, i == 0)


def _acc(ref, val, first):
    @pl.when(first)
    def _():
        ref[...] = val

    @pl.when(jnp.logical_not(first))
    def _():
        ref[...] += val


def _ada_fwd(c_all, w_sh, b_sh):
    nb = c_all.shape[0]
    tn = 768

    def body(c_ref, w_ref, b_ref, o_ref):
        cv = c_ref[...]
        cs = (cv * _sigmoid(cv)).astype(BF16)
        o_ref[...] = _dot(cs, w_ref[...].astype(BF16)) + b_ref[...]

    return pl.pallas_call(
        body, name="ada_fwd", grid=(ADA_SH // tn,),
        out_shape=jax.ShapeDtypeStruct((nb, ADA_SH), F32),
        in_specs=[pl.BlockSpec((nb, D), lambda j: (0, 0)), pl.BlockSpec((D, tn), lambda j: (0, j)),
                  pl.BlockSpec((1, tn), lambda j: (0, j))],
        out_specs=pl.BlockSpec((nb, tn), lambda j: (0, j)),
        compiler_params=_cparams(),
    )(c_all, w_sh, b_sh)


def _ada_bwd_adam(c_all, dada_sh, w, m, v):
    nb = c_all.shape[0]
    tn = 768

    def body(c_ref, d_ref, w_ref, m_ref, v_ref, g_out, d_out, m_out, v_out):
        cv = c_ref[...]
        cs = (cv * _sigmoid(cv)).astype(BF16)
        g = _dot_tn(cs, d_ref[...].astype(BF16))
        delta, m2, v2 = _adam(w_ref[...], g, m_ref[...], v_ref[...])
        g_out[...] = g
        d_out[...] = delta
        m_out[...] = m2
        v_out[...] = v2

    big = pl.BlockSpec((D, tn), lambda j: (0, j))
    shape = jax.ShapeDtypeStruct((D, ADA_SH), F32)
    return pl.pallas_call(
        body, name="ada_bwd_adam", grid=(ADA_SH // tn,),
        out_shape=[shape] * 4,
        in_specs=[pl.BlockSpec((nb, D), lambda j: (0, 0)), pl.BlockSpec((nb, tn), lambda j: (0, j)), big, big, big],
        out_specs=[big] * 4,
        compiler_params=_cparams(),
    )(c_all, dada_sh, w, m, v)


def _tok_specs(tm, width):
    return pl.BlockSpec((1, tm, width), lambda b, i: (b, i, 0))


def _mod_spec():
    return pl.BlockSpec((1, 1, D), lambda b, i: (b, 0, 0))


def _row_spec(width=D):
    return pl.BlockSpec((1, width), lambda b, i: (0, 0))


def _ffn_fwd(x, sh, sc, gt, g_pre, g_post, w_in4, w_out, target=None):
    nb, s, _ = x.shape
    tm = min(512, s)
    with_loss = target is not None

    def body(*refs):
        if with_loss:
            (x_ref, sh_ref, sc_ref, gt_ref, gpre_ref, gpost_ref, win_ref, wout_ref, tgt_ref,
             xo_ref, df_ref, p_ref, ls_ref, dgpost_ref, dgt_ref) = refs
        else:
            (x_ref, sh_ref, sc_ref, gt_ref, gpre_ref, gpost_ref, win_ref, wout_ref,
             xo_ref, f_ref, p_ref) = refs
        xv = x_ref[0]
        n, _ = _rms(xv)
        h = (n * gpre_ref[...]) * (1.0 + sc_ref[0]) + sh_ref[0]
        hb = h.astype(BF16)
        acc = jnp.zeros((tm, D), F32)
        for j in range(2):
            gate = _dot(hb, win_ref[j])
            up = _dot(hb, win_ref[2 + j])
            p_ref[0, :, j * FBLK:(j + 1) * FBLK] = gate.astype(BF16)
            p_ref[0, :, DFF + j * FBLK:DFF + (j + 1) * FBLK] = up.astype(BF16)
            a = (gate * _sigmoid(gate)) * up
            acc = acc + _dot(a.astype(BF16), wout_ref[j * FBLK:(j + 1) * FBLK, :])
        nf, q = _rms(acc)
        gpost = gpost_ref[...]
        half_gate = 0.5 * gt_ref[0]
        out = xv + half_gate * (nf * gpost)
        if with_loss:
            first = _first(pl.program_id(0), pl.program_id(1))
            err = out - tgt_ref[0]
            dout = err * (1.0 / D)
            xo_ref[0] = dout
            row = jnp.sum(err * err, axis=0, keepdims=True)
            part = row[:, 0:128]
            for k in range(1, D // 128):
                part = part + row[:, k * 128:(k + 1) * 128]
            _acc(ls_ref, part, first)
            df, dgpost = _rms_bwd(dout * half_gate, nf, q, gpost)
            df_ref[0] = df.astype(BF16)
            _acc(dgpost_ref, dgpost, first)
            _acc(dgt_ref, jnp.sum(dout * (0.5 * (nf * gpost)), axis=0, keepdims=True)[None], pl.program_id(1) == 0)
        else:
            f_ref[0] = acc
            xo_ref[0] = out

    in_specs = [_tok_specs(tm, D), _mod_spec(), _mod_spec(), _mod_spec(), _row_spec(), _row_spec(), VMEM_FULL, VMEM_FULL]
    args = [x, sh, sc, gt, g_pre, g_post, w_in4, w_out]
    out_shape = [jax.ShapeDtypeStruct((nb, s, D), F32), jax.ShapeDtypeStruct((nb, s, D), BF16 if with_loss else F32),
                 jax.ShapeDtypeStruct((nb, s, 2 * DFF), BF16)]
    out_specs = [_tok_specs(tm, D), _tok_specs(tm, D), _tok_specs(tm, 2 * DFF)]
    if with_loss:
        in_specs.append(_tok_specs(tm, D))
        args.append(target)
        out_shape += [jax.ShapeDtypeStruct((1, 128), F32), jax.ShapeDtypeStruct((1, D), F32),
                      jax.ShapeDtypeStruct((nb, 1, D), F32)]
        out_specs += [pl.BlockSpec((1, 128), lambda b, i: (0, 0)), _row_spec(), _mod_spec()]
    return pl.pallas_call(
        body, name="ffn_loss_fwd" if with_loss else "ffn_fwd", grid=(nb, s // tm),
        out_shape=out_shape, in_specs=in_specs, out_specs=out_specs,
        compiler_params=_cparams(),
    )(*args)


def _ffn_bwd(dxo, x, f, p, sh, sc, gt, g_pre, g_post, w_in4, w_out, df=None):
    nb, s, _ = x.shape
    tm = min(256, s)
    given = df is not None

    def body(*refs):
        if given:
            (dxo_ref, x_ref, dfin_ref, p_ref, sh_ref, sc_ref, gpre_ref, win_ref, wout_ref,
             dx_ref, dp_ref, h_ref, a_ref, dgpre_ref, dsh_ref, dsc_ref) = refs
        else:
            (dxo_ref, x_ref, f_ref, p_ref, sh_ref, sc_ref, gt_ref, gpre_ref, gpost_ref, win_ref, wout_ref,
             dx_ref, dp_ref, h_ref, a_ref, df_ref, dgpre_ref, dgpost_ref, dsh_ref, dsc_ref, dgt_ref) = refs
        b, i = pl.program_id(0), pl.program_id(1)
        dxo_v = dxo_ref[0]
        if given:
            dfb = dfin_ref[0]
        else:
            nf, q = _rms(f_ref[0])
            gpost = gpost_ref[...]
            dgt = jnp.sum(dxo_v * (0.5 * (nf * gpost)), axis=0, keepdims=True)
            do = dxo_v * (0.5 * gt_ref[0])
            dfv, dgpost = _rms_bwd(do, nf, q, gpost)
            dfb = dfv.astype(BF16)
            df_ref[0] = dfb
        xv = x_ref[0]
        n, r = _rms(xv)
        gpre = gpre_ref[...]
        ng = n * gpre
        scale1 = 1.0 + sc_ref[0]
        h = ng * scale1 + sh_ref[0]
        h_ref[0] = h.astype(BF16)
        dh = jnp.zeros((tm, D), F32)
        for j in range(2):
            gate = p_ref[0, :, j * FBLK:(j + 1) * FBLK].astype(F32)
            up = p_ref[0, :, DFF + j * FBLK:DFF + (j + 1) * FBLK].astype(F32)
            sg = _sigmoid(gate)
            act = gate * sg
            a_ref[0, :, j * FBLK:(j + 1) * FBLK] = (act * up).astype(BF16)
            da = _dot_nt(dfb, wout_ref[j * FBLK:(j + 1) * FBLK, :])
            dgate = (da * up * _dsilu(gate, sg)).astype(BF16)
            dup = (da * act).astype(BF16)
            dp_ref[0, :, j * FBLK:(j + 1) * FBLK] = dgate
            dp_ref[0, :, DFF + j * FBLK:DFF + (j + 1) * FBLK] = dup
            dh = dh + _dot_nt(dgate, win_ref[j]) + _dot_nt(dup, win_ref[2 + j])
        dsh = jnp.sum(dh, axis=0, keepdims=True)
        dsc = jnp.sum(dh * ng, axis=0, keepdims=True)
        dxn, dgpre = _rms_bwd(dh * scale1, n, r, gpre)
        dx_ref[0] = dxo_v + dxn
        _acc(dgpre_ref, dgpre, _first(b, i))
        _acc(dsh_ref, dsh[None], i == 0)
        _acc(dsc_ref, dsc[None], i == 0)
        if not given:
            _acc(dgpost_ref, dgpost, _first(b, i))
            _acc(dgt_ref, dgt[None], i == 0)

    tok = _tok_specs(tm, D)
    mod_shape = jax.ShapeDtypeStruct((nb, 1, D), F32)
    row_shape = jax.ShapeDtypeStruct((1, D), F32)
    big = [jax.ShapeDtypeStruct((nb, s, D), F32), jax.ShapeDtypeStruct((nb, s, 2 * DFF), BF16),
           jax.ShapeDtypeStruct((nb, s, D), BF16), jax.ShapeDtypeStruct((nb, s, DFF), BF16)]
    big_specs = [tok, _tok_specs(tm, 2 * DFF), tok, _tok_specs(tm, DFF)]
    if given:
        return pl.pallas_call(
            body, name="ffn_bwd_after_loss", grid=(nb, s // tm),
            out_shape=big + [row_shape, mod_shape, mod_shape],
            in_specs=[tok, tok, tok, _tok_specs(tm, 2 * DFF), _mod_spec(), _mod_spec(), _row_spec(), VMEM_FULL, VMEM_FULL],
            out_specs=big_specs + [_row_spec(), _mod_spec(), _mod_spec()],
            compiler_params=_cparams(),
        )(dxo, x, df, p, sh, sc, g_pre, w_in4, w_out)
    return pl.pallas_call(
        body, name="ffn_bwd", grid=(nb, s // tm),
        out_shape=big + [jax.ShapeDtypeStruct((nb, s, D), BF16), row_shape, row_shape, mod_shape, mod_shape, mod_shape],
        in_specs=[tok, tok, tok, _tok_specs(tm, 2 * DFF), _mod_spec(), _mod_spec(), _mod_spec(), _row_spec(), _row_spec(),
                  VMEM_FULL, VMEM_FULL],
        out_specs=big_specs + [tok, _row_spec(), _row_spec(), _mod_spec(), _mod_spec(), _mod_spec()],
        compiler_params=_cparams(),
    )(dxo, x, f, p, sh, sc, gt, g_pre, g_post, w_in4, w_out)


def _wgrad(name, a, b, col_block, chip_major):
    t, ka = a.shape
    n = b.shape[1]
    tk = min(t, 512)
    while tk * 2 <= t and t % (tk * 2) == 0 and 2 * (tk * 2) * max(ka, col_block) <= 6 * 1024 * 1024:
        tk *= 2
    nk = t // tk
    nblk = n // col_block

    def body(a_ref, b_ref, o_ref, obf_ref, acc_ref):
        k = pl.program_id(1)

        @pl.when(k == 0)
        def _():
            acc_ref[...] = jnp.zeros_like(acc_ref)

        acc_ref[...] += _dot_tn(a_ref[...], b_ref[...])

        @pl.when(k == nk - 1)
        def _():
            val = acc_ref[...]
            if chip_major:
                o_ref[0] = val
                obf_ref[0] = val.astype(BF16)
            else:
                o_ref[...] = val
                obf_ref[...] = val.astype(BF16)

    if chip_major:
        shape = (nblk, ka, col_block)
        ospec = pl.BlockSpec((1, ka, col_block), lambda j, k: (j, 0, 0))
    else:
        shape = (ka, n)
        ospec = pl.BlockSpec((ka, col_block), lambda j, k: (0, j))
    return pl.pallas_call(
        body, name=name, grid=(nblk, nk),
        out_shape=[jax.ShapeDtypeStruct(shape, F32), jax.ShapeDtypeStruct(shape, BF16)],
        in_specs=[pl.BlockSpec((tk, ka), lambda j, k: (k, 0)), pl.BlockSpec((tk, col_block), lambda j, k: (k, j))],
        out_specs=[ospec, ospec],
        scratch_shapes=[pltpu.VMEM((ka, col_block), F32)],
        compiler_params=_cparams(),
    )(a, b)


def _mix_in_fwd(x, sh, sc, g_pre, w_mi4):
    nb, s, _ = x.shape
    tm = min(512, s)

    def body(x_ref, sh_ref, sc_ref, gpre_ref, w_ref, u_ref, v_ref, a_ref, g_ref):
        n, _ = _rms(x_ref[0])
        hb = ((n * gpre_ref[...]) * (1.0 + sc_ref[0]) + sh_ref[0]).astype(BF16)
        for k, o_ref in enumerate((u_ref, v_ref, a_ref, g_ref)):
            o_ref[0] = _dot(hb, w_ref[k])

    shape = jax.ShapeDtypeStruct((nb, s, WA), F32)
    return pl.pallas_call(
        body, name="mix_in_fwd", grid=(nb, s // tm),
        out_shape=[shape] * 4,
        in_specs=[_tok_specs(tm, D), _mod_spec(), _mod_spec(), _row_spec(), VMEM_FULL],
        out_specs=[_tok_specs(tm, WA)] * 4,
        compiler_params=_cparams(),
    )(x, sh, sc, g_pre, w_mi4)


def _spatial_weights(wcat_ref, transposed):
    w = wcat_ref[...]
    row = lax.broadcasted_iota(jnp.int32, w.shape, 0)
    col = lax.broadcasted_iota(jnp.int32, w.shape, 1)
    keep = ((row & (CH - 1)) <= col) if transposed else ((col & (CH - 1)) <= row)
    return jnp.where(keep, w, 0.0).astype(BF16)


def _expand_heads(vc, masks):
    return jnp.concatenate([jnp.where(mk, vc, jnp.zeros_like(vc)) for mk in masks], axis=0)


def _spatial_bias(bspt_ref):
    return bspt_ref[...]


SHIFTS = 8
TAP_ROWS = 32


def _ext_rows(tm):
    return tm + HALO + SHIFTS


def _make_shifts(ext_ref, sh_ref, tm):
    ext_ref[tm + HALO:tm + HALO + SHIFTS, :] = jnp.zeros((SHIFTS, WB), F32)
    for r in range(SHIFTS):
        sh_ref[r] = ext_ref[r:r + tm + HALO, :]


def _conv_taps(sh_ref, w_ref, tm, taps, emit):
    def block(i, carry):
        r0 = pl.multiple_of(i * TAP_ROWS, TAP_ROWS)
        acc = jnp.zeros((TAP_ROWS, WB), F32)
        for o, k in taps:
            acc = acc + w_ref[k:k + 1, :] * sh_ref[o % SHIFTS, pl.ds(r0 + SHIFTS * (o // SHIFTS), TAP_ROWS), :]
        emit(r0, acc)
        return carry

    lax.fori_loop(0, tm // TAP_ROWS, block, 0)


def _halo_prev_spec(tm):
    return pl.BlockSpec((1, HALO, WB), lambda b, i: (b, jnp.maximum(i * (tm // HALO) - 1, 0), 0))


def _halo_next_spec(tm, s):
    return pl.BlockSpec((1, HALO, WB), lambda b, i: (b, jnp.minimum((i + 1) * (tm // HALO), s // HALO - 1), 0))


def _mix_mid_fwd(x, u, v, a, g, gt, gn_g, gn_b, wcat, bspt, conv_w, conv_b, cn_g, cn_b, go_a, go_b, w_mo, g_post):
    nb, s, _ = x.shape
    tm = min(512, s)

    def body(x_ref, u_ref, v_ref, a_ref, g_ref, ah_ref, gh_ref, gt_ref, gng_ref, gnb_ref, wcat_ref, bspt_ref,
             cw_ref, cb_ref, cng_ref, cnb_ref, goa_ref, gob_ref, wmo_ref, gpost_ref,
             xo_ref, conv_ref, y_ref, m_ref, ext_ref, sh_ref):
        i = pl.program_id(1)
        xhat, _ = _ln(v_ref[0])
        vb = (xhat * gng_ref[...] + gnb_ref[...]).astype(BF16)
        wsb = _spatial_weights(wcat_ref, False)
        bias = _spatial_bias(bspt_ref)
        masks = _head_mask((CH, WA))
        zs = []
        for cidx in range(tm // CH):
            vexp = _expand_heads(vb[cidx * CH:(cidx + 1) * CH, :], masks)
            zs.append(_dot(wsb, vexp) + bias)
        z = jnp.concatenate(zs, axis=0)
        na, _ = _rms(u_ref[0] * z)
        keep = jnp.where(i == 0, 0.0, 1.0).astype(F32)
        ext_ref[0:HALO, :] = (ah_ref[0] * _sigmoid(gh_ref[0])) * keep
        ext_ref[HALO:HALO + tm, :] = a_ref[0] * _sigmoid(g_ref[0])
        _make_shifts(ext_ref, sh_ref, tm)
        cb = cb_ref[...]

        def put_conv(r0, acc):
            conv_ref[0, pl.ds(r0, TAP_ROWS), :] = acc + cb

        _conv_taps(sh_ref, cw_ref, tm, [(k + HALO - (CK - 1), k) for k in range(CK)], put_conv)
        conv = conv_ref[0]
        chat, _ = _ln(conv)
        cln = chat * cng_ref[...] + cnb_ref[...]
        nbb, _ = _rms(cln * _sigmoid(cln))
        yb = jnp.concatenate([na * goa_ref[...], nbb * gob_ref[...]], axis=1).astype(BF16)
        y_ref[0] = yb
        m = _dot(yb, wmo_ref[...])
        m_ref[0] = m
        nm, _ = _rms(m)
        xo_ref[0] = x_ref[0] + gt_ref[0] * (nm * gpost_ref[...])

    t5 = _tok_specs(tm, WA)
    tok = _tok_specs(tm, D)
    r5 = _row_spec(WA)
    full = lambda shape: pl.BlockSpec(shape, lambda b, i: (0,) * len(shape))
    return pl.pallas_call(
        body, name="mix_mid_fwd", grid=(nb, s // tm),
        out_shape=[jax.ShapeDtypeStruct((nb, s, D), F32), jax.ShapeDtypeStruct((nb, s, WB), F32),
                   jax.ShapeDtypeStruct((nb, s, D), BF16), jax.ShapeDtypeStruct((nb, s, D), F32)],
        in_specs=[tok, t5, t5, t5, t5, _halo_prev_spec(tm), _halo_prev_spec(tm), _mod_spec(), r5, r5,
                  full((CH, NH * CH)), full((CH, WA)), full((HALO, WB)), r5, r5, r5, r5, r5, VMEM_FULL, _row_spec()],
        out_specs=[tok, t5, tok, tok],
        scratch_shapes=[pltpu.VMEM((_ext_rows(tm), WB), F32), pltpu.VMEM((SHIFTS, tm + HALO, WB), F32)],
        compiler_params=_cparams(),
    )(x, u, v, a, g, a, g, gt, gn_g, gn_b, wcat, bspt, conv_w, conv_b, cn_g, cn_b, go_a, go_b, w_mo, g_post)


def _mix_out_bwd(dxo, m, gt, g_post, w_mo):
    nb, s, _ = m.shape
    tm = min(512, s)

    def body(dxo_ref, m_ref, gt_ref, gpost_ref, wmo_ref, dy_ref, dm_ref, dgpost_ref, dgt_ref):
        b, i = pl.program_id(0), pl.program_id(1)
        dxo_v = dxo_ref[0]
        nm, q = _rms(m_ref[0])
        gpost = gpost_ref[...]
        dgt = jnp.sum(dxo_v * (nm * gpost), axis=0, keepdims=True)
        dm, dgpost = _rms_bwd(dxo_v * gt_ref[0], nm, q, gpost)
        dmb = dm.astype(BF16)
        dm_ref[0] = dmb
        dy_ref[0] = _dot_nt(dmb, wmo_ref[...])
        _acc(dgpost_ref, dgpost, _first(b, i))
        _acc(dgt_ref, dgt[None], i == 0)

    tok = _tok_specs(tm, D)
    return pl.pallas_call(
        body, name="mix_out_bwd", grid=(nb, s // tm),
        out_shape=[jax.ShapeDtypeStruct((nb, s, D), F32), jax.ShapeDtypeStruct((nb, s, D), BF16),
                   jax.ShapeDtypeStruct((1, D), F32), jax.ShapeDtypeStruct((nb, 1, D), F32)],
        in_specs=[tok, tok, _mod_spec(), _row_spec(), VMEM_FULL],
        out_specs=[tok, tok, _row_spec(), _mod_spec()],
        compiler_params=_cparams(),
    )(dxo, m, gt, g_post, w_mo)


def _mix_mid_bwd(dy, u, v, conv, gn_g, gn_b, wcat, wcat_t, bspt, cn_g, cn_b, go_a, go_b):
    nb, s, _ = dy.shape
    tm = min(512, s)
    nchunk = tm // CH

    def body(dy_ref, u_ref, v_ref, conv_ref, gng_ref, gnb_ref, wcat_ref, wcatt_ref, bspt_ref, cng_ref, cnb_ref,
             goa_ref, gob_ref,
             du_ref, dv_ref, dconv_ref, dwcat_ref, dbsp_ref, dgng_ref, dgnb_ref, dgoa_ref, dgob_ref,
             dcng_ref, dcnb_ref, dcb_ref):
        first = _first(pl.program_id(0), pl.program_id(1))
        dyv = dy_ref[0]
        xhat, rstd = _ln(v_ref[0])
        gng = gng_ref[...]
        vb = (xhat * gng + gnb_ref[...]).astype(BF16)
        wsb = _spatial_weights(wcat_ref, False)
        wsb_t = _spatial_weights(wcatt_ref, True)
        bias = _spatial_bias(bspt_ref)
        masks = _head_mask((CH, WA))
        vexps, zs = [], []
        for cidx in range(nchunk):
            vexp = _expand_heads(vb[cidx * CH:(cidx + 1) * CH, :], masks)
            vexps.append(vexp)
            zs.append(_dot(wsb, vexp) + bias)
        z = jnp.concatenate(zs, axis=0)
        uv = u_ref[0]
        na, ra = _rms(uv * z)
        dya, dgoa = _rms_bwd(dyv[:, 0:WA], na, ra, goa_ref[...])
        du_ref[0] = dya * z
        dz = dya * uv
        dwcat = jnp.zeros((CH, NH * CH), F32)
        dzsum = jnp.zeros((CH, WA), F32)
        dvlns = []
        for cidx in range(nchunk):
            dzc = dz[cidx * CH:(cidx + 1) * CH, :]
            dzsum = dzsum + dzc
            dzb = dzc.astype(BF16)
            dwcat = dwcat + _dot_nt(dzb, vexps[cidx])
            dvexp = _dot(wsb_t, dzb)
            dvl = jnp.zeros((CH, WA), F32)
            for h in range(NH):
                dvl = dvl + jnp.where(masks[h], dvexp[h * CH:(h + 1) * CH, :], 0.0)
            dvlns.append(dvl)
        dvln = jnp.concatenate(dvlns, axis=0)
        dv, dgng, dgnb = _ln_bwd(dvln, xhat, rstd, gng)
        dv_ref[0] = dv
        lane = lax.broadcasted_iota(jnp.int32, (NH, WA), 1)
        head = lax.broadcasted_iota(jnp.int32, (NH, WA), 0)
        sel = jnp.where((lane >= head * HD) & (lane < (head + 1) * HD), 1.0, 0.0).astype(F32)
        dbsp = lax.dot_general(sel, dzsum, NT, preferred_element_type=F32, precision=lax.Precision.HIGHEST)
        chat, crstd = _ln(conv_ref[0])
        cng = cng_ref[...]
        cln = chat * cng + cnb_ref[...]
        sg = _sigmoid(cln)
        nbb, rb = _rms(cln * sg)
        dyb, dgob = _rms_bwd(dyv[:, WA:D], nbb, rb, gob_ref[...])
        dconv, dcng, dcnb = _ln_bwd(dyb * _dsilu(cln, sg), chat, crstd, cng)
        dconv_ref[0] = dconv
        dcb = jnp.sum(dconv, axis=0, keepdims=True)
        for ref, val in ((dwcat_ref, dwcat), (dbsp_ref, dbsp), (dgng_ref, dgng), (dgnb_ref, dgnb), (dgoa_ref, dgoa),
                         (dgob_ref, dgob), (dcng_ref, dcng), (dcnb_ref, dcnb), (dcb_ref, dcb)):
            _acc(ref, val, first)

    t5 = _tok_specs(tm, WA)
    r5 = _row_spec(WA)
    full = lambda shape: pl.BlockSpec(shape, lambda b, i: (0,) * len(shape))
    big = jax.ShapeDtypeStruct((nb, s, WA), F32)
    row = jax.ShapeDtypeStruct((1, WA), F32)
    return pl.pallas_call(
        body, name="mix_mid_bwd", grid=(nb, s // tm),
        out_shape=[big, big, big, jax.ShapeDtypeStruct((CH, NH * CH), F32), jax.ShapeDtypeStruct((NH, CH), F32),
                   row, row, row, row, row, row, row],
        in_specs=[_tok_specs(tm, D), t5, t5, t5, r5, r5, full((CH, NH * CH)), full((NH * CH, CH)), full((CH, WA)),
                  r5, r5, r5, r5],
        out_specs=[t5, t5, t5, full((CH, NH * CH)), full((NH, CH)), r5, r5, r5, r5, r5, r5, r5],
        compiler_params=_cparams(),
    )(dy, u, v, conv, gn_g, gn_b, wcat, wcat_t, bspt, cn_g, cn_b, go_a, go_b)


def _mix_in_bwd(dxo, x, du, dv, dconv, a, g, sh, sc, g_pre, w_mi4, conv_w):
    nb, s, _ = x.shape
    tm = min(512, s)
    n_i = s // tm

    def body(dxo_ref, x_ref, du_ref, dv_ref, dc_ref, dch_ref, a_ref, g_ref, ah_ref, gh_ref, sh_ref, sc_ref,
             gpre_ref, w_ref, cw_ref,
             dx_ref, dproj_ref, h_ref, dgpre_ref, dsh_ref, dsc_ref, dcw_ref, ext_ref, shf_ref, dglu_ref):
        b, i = pl.program_id(0), pl.program_id(1)
        first = _first(b, i)
        av, gv = a_ref[0], g_ref[0]
        sg = _sigmoid(gv)
        dconv = dc_ref[0]
        ext_ref[0:tm, :] = dconv
        ext_ref[tm:tm + HALO, :] = dch_ref[0] * jnp.where(i == n_i - 1, 0.0, 1.0).astype(F32)
        _make_shifts(ext_ref, shf_ref, tm)

        def put_dglu(r0, acc):
            dglu_ref[pl.ds(r0, TAP_ROWS), :] = acc

        _conv_taps(shf_ref, cw_ref, tm, [(CK - 1 - k, k) for k in range(CK)], put_dglu)
        dglu = dglu_ref[...]
        ext_ref[0:HALO, :] = (ah_ref[0] * _sigmoid(gh_ref[0])) * jnp.where(i == 0, 0.0, 1.0).astype(F32)
        ext_ref[HALO:HALO + tm, :] = av * sg
        _make_shifts(ext_ref, shf_ref, tm)

        @pl.when(first)
        def _():
            dcw_ref[...] = jnp.zeros((HALO, WB), F32)

        for k in range(CK):
            o = k + HALO - (CK - 1)
            lo = SHIFTS * (o // SHIFTS)
            dcw_ref[k:k + 1, :] += jnp.sum(dconv * shf_ref[o % SHIFTS, lo:lo + tm, :], axis=0, keepdims=True)
        da = dglu * sg
        dg = dglu * av * (sg * (1.0 - sg))
        parts = [du_ref[0].astype(BF16), dv_ref[0].astype(BF16), da.astype(BF16), dg.astype(BF16)]
        dh = jnp.zeros((tm, D), F32)
        for k in range(4):
            dproj_ref[0, :, k * WA:(k + 1) * WA] = parts[k]
            dh = dh + _dot_nt(parts[k], w_ref[k])
        n, r = _rms(x_ref[0])
        gpre = gpre_ref[...]
        ng = n * gpre
        scale1 = 1.0 + sc_ref[0]
        h_ref[0] = (ng * scale1 + sh_ref[0]).astype(BF16)
        dsh = jnp.sum(dh, axis=0, keepdims=True)
        dsc = jnp.sum(dh * ng, axis=0, keepdims=True)
        dxn, dgpre = _rms_bwd(dh * scale1, n, r, gpre)
        dx_ref[0] = dxo_ref[0] + dxn
        _acc(dgpre_ref, dgpre, first)
        _acc(dsh_ref, dsh[None], i == 0)
        _acc(dsc_ref, dsc[None], i == 0)

    tok = _tok_specs(tm, D)
    t5 = _tok_specs(tm, WA)
    full = lambda shape: pl.BlockSpec(shape, lambda b, i: (0,) * len(shape))
    mod_shape = jax.ShapeDtypeStruct((nb, 1, D), F32)
    return pl.pallas_call(
        body, name="mix_in_bwd", grid=(nb, n_i),
        out_shape=[jax.ShapeDtypeStruct((nb, s, D), F32), jax.ShapeDtypeStruct((nb, s, 4 * WA), BF16),
                   jax.ShapeDtypeStruct((nb, s, D), BF16), jax.ShapeDtypeStruct((1, D), F32), mod_shape, mod_shape,
                   jax.ShapeDtypeStruct((HALO, WB), F32)],
        in_specs=[tok, tok, t5, t5, t5, _halo_next_spec(tm, s), t5, t5, _halo_prev_spec(tm), _halo_prev_spec(tm),
                  _mod_spec(), _mod_spec(), _row_spec(), VMEM_FULL, full((HALO, WB))],
        out_specs=[tok, _tok_specs(tm, 4 * WA), tok, _row_spec(), _mod_spec(), _mod_spec(), full((HALO, WB))],
        scratch_shapes=[pltpu.VMEM((_ext_rows(tm), WB), F32), pltpu.VMEM((SHIFTS, tm + HALO, WB), F32),
                        pltpu.VMEM((tm, WB), F32)],
        compiler_params=_cparams(),
    )(dxo, x, du, dv, dconv, dconv, a, g, a, g, sh, sc, g_pre, w_mi4, conv_w)


def _row_tile(rows, cols):
    best = 16
    for t in range(16, rows + 1, 16):
        if rows % t == 0 and t * cols * 4 <= 1536 * 1024:
            best = t
    return best


def _sum4(name, own4, recv, j_arr):
    _, rows, cols = own4.shape
    tr = _row_tile(rows, cols)

    def body(j_ref, own_ref, recv_ref, o_ref):
        del j_ref
        acc = own_ref[0]
        for k in range(3):
            acc = acc + recv_ref[k].astype(F32)
        o_ref[...] = acc

    return pl.pallas_call(
        body, name=name,
        grid_spec=pltpu.PrefetchScalarGridSpec(
            num_scalar_prefetch=1, grid=(rows // tr,),
            in_specs=[pl.BlockSpec((1, tr, cols), lambda i, j: (j[0], i, 0)),
                      pl.BlockSpec((3, tr, cols), lambda i, j: (0, i, 0))],
            out_specs=pl.BlockSpec((tr, cols), lambda i, j: (i, 0))),
        out_shape=jax.ShapeDtypeStruct((rows, cols), F32),
        compiler_params=_cparams(),
    )(j_arr, own4, recv)


def _pair_plan(shapes):
    def plan(x, y, c, src, land):
        sends = []
        for a, shape in enumerate(shapes):
            rows = shape[1] // 2
            theirs = pl.ds(pl.multiple_of((1 - c) * rows, 16), rows)
            sends.append((src[a].at[:, theirs], land[a], (x, y, 1 - c), land[a]))
        return [], sends

    return plan


def _swap_plan(n):
    def plan(x, y, c, src, land):
        return [], [(src[a], land[a], (x, y, 1 - c), land[a]) for a in range(n)]

    return plan


def _pair_sum(name, g32, recv, c_arr):
    nblk, rows, cols = recv.shape
    tr = _row_tile(rows, cols)
    nh = rows // tr

    def body(c_ref, g_ref, r_ref, o32_ref, obf_ref):
        del c_ref
        val = g_ref[0] + r_ref[0].astype(F32)
        o32_ref[0] = val
        obf_ref[0] = val.astype(BF16)

    spec = pl.BlockSpec((1, tr, cols), lambda k, i, c: (k, i, 0))
    return pl.pallas_call(
        body, name=name,
        grid_spec=pltpu.PrefetchScalarGridSpec(
            num_scalar_prefetch=1, grid=(nblk, nh),
            in_specs=[pl.BlockSpec((1, tr, cols), lambda k, i, c: (k, c[0] * nh + i, 0)), spec],
            out_specs=[spec, spec]),
        out_shape=[jax.ShapeDtypeStruct(recv.shape, F32), jax.ShapeDtypeStruct(recv.shape, BF16)],
        compiler_params=_cparams(),
    )(c_arr, g32, recv)


def _adam_halves(name, w, m, v, mine, theirs, c_arr):
    rows, cols = w.shape
    tr = _row_tile(rows // 2, cols)
    nh = (rows // 2) // tr

    def body(c_ref, w_ref, m_ref, v_ref, mine_ref, theirs_ref, g_out, d_out, m_out, v_out):
        here = (pl.program_id(0) // nh) == c_ref[0]
        g = jnp.where(here, mine_ref[...], theirs_ref[...])
        delta, m2, v2 = _adam(w_ref[...], g, m_ref[...], v_ref[...])
        g_out[...] = g
        d_out[...] = delta
        m_out[...] = m2
        v_out[...] = v2

    spec = pl.BlockSpec((tr, cols), lambda i, c: (i, 0))
    shape = jax.ShapeDtypeStruct((rows, cols), F32)
    return pl.pallas_call(
        body, name=name,
        grid_spec=pltpu.PrefetchScalarGridSpec(
            num_scalar_prefetch=1, grid=(2 * nh,),
            in_specs=[spec, spec, spec,
                      pl.BlockSpec((tr, cols), lambda i, c: (jnp.clip(i - c[0] * nh, 0, nh - 1), 0)),
                      pl.BlockSpec((tr, cols), lambda i, c: (jnp.clip(i - (1 - c[0]) * nh, 0, nh - 1), 0))],
            out_specs=[spec] * 4),
        out_shape=[shape] * 4,
        compiler_params=_cparams(),
    )(c_arr, w, m, v, mine, theirs)


def _adam_big(name, w, m, v, ga, gb):
    rows, cols = w.shape
    tr = _row_tile(rows, cols)

    def body(w_ref, m_ref, v_ref, ga_ref, gb_ref, g_out, d_out, m_out, v_out):
        gsum = ga_ref[...] + gb_ref[...]
        delta, m2, v2 = _adam(w_ref[...], gsum, m_ref[...], v_ref[...])
        g_out[...] = gsum
        d_out[...] = delta
        m_out[...] = m2
        v_out[...] = v2

    spec = pl.BlockSpec((tr, cols), lambda i: (i, 0))
    shape = jax.ShapeDtypeStruct((rows, cols), F32)
    return pl.pallas_call(
        body, name=name, grid=(rows // tr,), out_shape=[shape] * 4,
        in_specs=[spec] * 5, out_specs=[spec] * 4, compiler_params=_cparams(),
    )(w, m, v, ga, gb)


PK_VEC = 0
PK_LOSS = 6
PK_PAIR = 8
PK_BSP = 16
PK_WCAT = 24
PK_ROWS = PK_WCAT + CH
PAIR_ORDER = ("gmlp_norm_g", "gmlp_norm_b", "conv_b", "conv_norm_g", "conv_norm_b", "g_out_a", "g_out_b")
VEC_ORDER = ("g_pre_f1", "g_post_f1", "g_pre_m", "g_post_m", "g_pre_f2", "g_post_f2")


def _pack_late(rows):
    counts = [r.shape[0] for r in rows]
    assert sum(counts) == 8

    def body(*refs):
        o_ref = refs[-1]
        at = 0
        for r, cnt in zip(refs[:-1], counts):
            o_ref[at:at + cnt, :] = r[...]
            at += cnt

    return pl.pallas_call(
        body, name="pack_late", out_shape=jax.ShapeDtypeStruct((8, D), F32),
        in_specs=[VMEM_FULL] * len(rows), out_specs=VMEM_FULL, compiler_params=_cparams(),
    )(*rows)


def _pack_small(vecs, pairs, dbsp, dwcat, lsum):
    def body(*refs):
        vec_refs = refs[:4]
        pair_refs = refs[4:11]
        dbsp_ref, dwcat_ref, lsum_ref, o_ref = refs[11:]
        o_ref[0:PK_WCAT, :] = jnp.zeros((PK_WCAT, D), F32)
        o_ref[PK_LOSS:PK_LOSS + 1, 0:128] = lsum_ref[...]
        for k, r in enumerate(vec_refs):
            o_ref[PK_VEC + 2 + k:PK_VEC + 3 + k, :] = r[...]
        for k, r in enumerate(pair_refs):
            row, half = PK_PAIR + k // 2, k % 2
            o_ref[row:row + 1, half * WA:(half + 1) * WA] = r[...]
        o_ref[PK_BSP:PK_BSP + NH, 0:CH] = dbsp_ref[...]
        o_ref[PK_WCAT:PK_ROWS, :] = dwcat_ref[...]

    args = list(vecs) + list(pairs) + [dbsp, dwcat, lsum]
    return pl.pallas_call(
        body, name="pack_small", out_shape=jax.ShapeDtypeStruct((PK_ROWS, D), F32),
        in_specs=[VMEM_FULL] * len(args), out_specs=VMEM_FULL, compiler_params=_cparams(),
    )(*args)


def _small_adam(pack_all, late_all, dcw_all, dada_all, params, behind):
    names = list(VEC_ORDER) + list(PAIR_ORDER) + ["b_spatial", "w_spatial", "conv_w", "b_ada"]
    flat = []
    for nm in names:
        flat += list(params[nm])
    n_in = 4 + len(flat)

    def body(*refs):
        pack_ref, late_ref, dcw_ref, dada_ref = refs[:4]
        prm = refs[4:n_in]
        outs = refs[n_in + 1:]

        def total(r0, nr, c0, nc):
            acc = pack_ref[0, r0:r0 + nr, c0:c0 + nc]
            for d in range(1, NDEV):
                acc = acc + pack_ref[d, r0:r0 + nr, c0:c0 + nc]
            return acc

        def emit(idx, g, getw, put):
            w_ref, m_ref, v_ref = prm[3 * idx:3 * idx + 3]
            delta, m2, v2 = _adam(getw(w_ref), g, getw(m_ref), getw(v_ref))
            for o_ref, val in zip(outs[4 * idx:4 * idx + 4], (g, delta, m2, v2)):
                put(o_ref, val)

        def whole(ref):
            return ref[...]

        def put_whole(ref, val):
            ref[...] = val

        idx = 0
        for k in range(6):
            if k < 2:
                g = late_ref[0, k:k + 1, :]
                for d in range(1, NDEV):
                    g = g + late_ref[d, k:k + 1, :]
            else:
                g = total(PK_VEC + k, 1, 0, D)
            emit(idx, g, whole, put_whole)
            idx += 1
        for k in range(7):
            emit(idx, total(PK_PAIR + k // 2, 1, (k % 2) * WA, WA), whole, put_whole)
            idx += 1
        emit(idx, total(PK_BSP, NH, 0, CH), lambda r: r[0], lambda r, val: r.__setitem__(0, val))
        idx += 1
        row = lax.broadcasted_iota(jnp.int32, (CH, CH), 0)
        col = lax.broadcasted_iota(jnp.int32, (CH, CH), 1)
        for h in range(NH):
            gh = jnp.where(col <= row, total(PK_WCAT, CH, h * CH, CH), 0.0)
            w_ref, m_ref, v_ref = prm[3 * idx:3 * idx + 3]
            delta, m2, v2 = _adam(w_ref[0, h], gh, m_ref[0, h], v_ref[0, h])
            for o_ref, val in zip(outs[4 * idx:4 * idx + 4], (gh, delta, m2, v2)):
                o_ref[0, h] = val
        idx += 1
        gcw = dcw_ref[0, 0:CK, :]
        for d in range(1, NDEV):
            gcw = gcw + dcw_ref[d, 0:CK, :]
        emit(idx, gcw, lambda r: r[0], lambda r, val: r.__setitem__(0, val))
        idx += 1
        emit(idx, jnp.sum(dada_ref[...], axis=0, keepdims=True), whole, put_whole)
        outs[-1][...] = jnp.sum(total(PK_LOSS, 1, 0, 128), axis=1, keepdims=True) * (0.5 / D)

    out_shape = []
    for nm in names:
        w = params[nm][0]
        out_shape += [jax.ShapeDtypeStruct(w.shape, F32)] * 4
    out_shape.append(jax.ShapeDtypeStruct((1, 1), F32))
    res = pl.pallas_call(
        body, name="small_adam", out_shape=out_shape,
        in_specs=[VMEM_FULL] * n_in + [ANY], out_specs=[VMEM_FULL] * len(out_shape), compiler_params=_cparams(),
    )(pack_all, late_all, dcw_all, dada_all, *flat, behind)
    return {nm: tuple(res[4 * k:4 * k + 4]) for k, nm in enumerate(names)}, res[-1].reshape(())


WEIGHTS = ['w_ada', 'b_ada', 'g_pre_f1', 'g_post_f1', 'w_f1_in', 'w_f1_out', 'g_pre_m', 'g_post_m', 'w_mix_in',
           'gmlp_norm_g', 'gmlp_norm_b', 'w_spatial', 'b_spatial', 'conv_w', 'conv_b', 'conv_norm_g', 'conv_norm_b',
           'g_out_a', 'g_out_b', 'w_mix_out', 'g_pre_f2', 'g_post_f2', 'w_f2_in', 'w_f2_out']
BIG = ('w_f1_in', 'w_f1_out', 'w_mix_in', 'w_mix_out', 'w_f2_in', 'w_f2_out')


def kernel(x, c, w_ada, b_ada, g_pre_f1, g_post_f1, w_f1_in, w_f1_out, g_pre_m, g_post_m, w_mix_in, gmlp_norm_g, gmlp_norm_b, w_spatial, b_spatial, conv_w, conv_b, conv_norm_g, conv_norm_b, g_out_a, g_out_b, w_mix_out, g_pre_f2, g_post_f2, w_f2_in, w_f2_out, loss_target, m_w_ada, m_b_ada, m_g_pre_f1, m_g_post_f1, m_w_f1_in, m_w_f1_out, m_g_pre_m, m_g_post_m, m_w_mix_in, m_gmlp_norm_g, m_gmlp_norm_b, m_w_spatial, m_b_spatial, m_conv_w, m_conv_b, m_conv_norm_g, m_conv_norm_b, m_g_out_a, m_g_out_b, m_w_mix_out, m_g_pre_f2, m_g_post_f2, m_w_f2_in, m_w_f2_out, v_w_ada, v_b_ada, v_g_pre_f1, v_g_post_f1, v_w_f1_in, v_w_f1_out, v_g_pre_m, v_g_post_m, v_w_mix_in, v_gmlp_norm_g, v_gmlp_norm_b, v_w_spatial, v_b_spatial, v_conv_w, v_conv_b, v_conv_norm_g, v_conv_norm_b, v_g_out_a, v_g_out_b, v_w_mix_out, v_g_pre_f2, v_g_post_f2, v_w_f2_in, v_w_f2_out):
    env = dict(locals())
    wts = {n: env[n] for n in WEIGHTS}
    mom = {n: env["m_" + n] for n in WEIGHTS}
    var = {n: env["v_" + n] for n in WEIGHTS}
    nb, s, _ = x.shape
    t = nb * s
    ax, ay, ac = lax.axis_index("x"), lax.axis_index("y"), lax.axis_index("c")
    j_chip = 2 * ax + ay
    dev = 4 * ax + 2 * ay + ac
    j_arr = j_chip.reshape(1).astype(jnp.int32)

    groups = (("w_f1_in", "w_f1_out"), ("w_mix_in", "w_mix_out"), ("w_f2_in", "w_f2_out"))
    def gather_operands(gi):
        srcs = [wts[n][0].astype(BF16) for n in groups[gi]] + ([conv_w[0]] if gi == 1 else [])
        lands = [lax.dynamic_update_index_in_dim(lax.empty((NCHIP,) + a.shape, a.dtype), a, j_chip, 0) for a in srcs]
        return srcs, lands

    def gather_start(gi, behind, operands=None):
        srcs, lands = operands or gather_operands(gi)
        plan_a, plan_b, n_b = _gather_plans([a.shape for a in srcs])
        ssem, rsem, srcs, lands, token = _split_start("gw_start%d" % gi, srcs, lands, plan_a, 3 * len(srcs), behind)
        gather[gi] = (srcs, lands, ssem, rsem, plan_a, plan_b, n_b)
        return token

    def gather_forward(gi, behind):
        srcs, lands, ssem, rsem, plan_a, plan_b, n_b = gather[gi]
        ssem, rsem, lands, token = _split_forward("gw_fwd%d" % gi, srcs, lands, ssem, rsem, plan_a, plan_b, n_b, behind)
        gather[gi] = (lands, ssem, rsem, plan_b)
        return token

    def gathered(gi, behind):
        lands, ssem, rsem, plan_b = gather[gi]
        return _split_wait("gw_wait%d" % gi, [], lands, ssem, rsem, plan_b, behind)

    gather = {}
    (c_all8,) = _allgather8("gather_c", [c.reshape(8, (nb * D) // 8)])
    token = gather_start(0, c_all8)
    c_all = c_all8.reshape(NDEV * nb, D) + token[0, 0]
    b_sh = lax.dynamic_slice(b_ada, (0, j_chip * ADA_SH), (1, ADA_SH))
    ada_sh = _ada_fwd(c_all, w_ada[0], b_sh)
    later = [gather_operands(1), gather_operands(2)]
    (ada4,) = _chip_allgather("gather_ada", [ada_sh], behind=[a for pair in later for arrs in pair for a in arrs])
    token = gather_forward(0, ada4)
    token = gather_start(1, token, later[0])
    token = gather_start(2, token, later[1])
    ada_me = lax.dynamic_slice(ada4, (0, dev * nb, 0), (NCHIP, nb, ADA_SH))
    ada_me = jnp.transpose(ada_me, (1, 0, 2)).reshape(nb, NMOD * D)
    sh1, sc1, gt1, sh2, sc2, gt2, sh3, sc3, gt3 = [ada_me[:, k * D:(k + 1) * D].reshape(nb, 1, D) for k in range(NMOD)]

    wcat = jnp.transpose(w_spatial[0], (1, 0, 2)).reshape(CH, NH * CH)
    wcat_t = jnp.transpose(w_spatial[0], (0, 2, 1)).reshape(NH * CH, CH)
    bspt = jnp.repeat(b_spatial[0].T, HD, axis=1)

    w1i, w1o = gathered(0, token)
    w1o = w1o.reshape(DFF, D)
    x1, f1, p1 = _ffn_fwd(x, sh1, sc1, gt1, g_pre_f1, g_post_f1, w1i, w1o)
    wmi, wmo, cw4 = gathered(1, gather_forward(1, x1))
    wmo = wmo.reshape(D, D)
    cw_full = jnp.transpose(cw4, (1, 0, 2)).reshape(CK, WB)
    cw_pad = jnp.pad(cw_full, ((0, HALO - CK), (0, 0)))
    u, v, a, g = _mix_in_fwd(x1, sh2, sc2, g_pre_m, wmi)
    token = gather_forward(2, u)
    x2, conv, yb, m = _mix_mid_fwd(x1, u, v, a, g, gt2 + token[0, 0], gmlp_norm_g, gmlp_norm_b, wcat, bspt, cw_pad, conv_b,
                                   conv_norm_g, conv_norm_b, g_out_a, g_out_b, wmo, g_post_m)
    w2i, w2o = gathered(2, [x2, token])
    w2o = w2o.reshape(DFF, D)
    dx3, df2, p2, lsum, dg_post_f2, dgt3 = _ffn_fwd(x2, sh3, sc3, gt3, g_pre_f2, g_post_f2, w2i, w2o, target=loss_target)

    def chip4(pair, rows):
        return [arr.reshape(NCHIP, rows, arr.shape[-1]) for arr in pair]

    def scatter_start(tag, pairs, behind):
        srcs = [p[1] for p in pairs]
        lands = [lax.empty((3,) + a.shape[1:], a.dtype) for a in srcs]
        ssem, rsem, srcs, lands, token = _split_start("gs_start_" + tag, srcs, lands, _scatter_plan(len(srcs)),
                                                      3 * len(srcs), behind)
        return (srcs, lands, ssem, rsem), token

    def scatter_wait(tag, state, behind):
        srcs, lands, ssem, rsem = state
        return _split_wait("gs_wait_" + tag, srcs, lands, ssem, rsem, _scatter_plan(len(srcs)), behind)

    def allgather_start(tag, arrs, behind):
        lands = [lax.dynamic_update_index_in_dim(lax.empty((NDEV,) + a.shape, a.dtype), a, dev, 0) for a in arrs]
        ssem, rsem, srcs, lands, token = _split_start("small_start_" + tag, arrs, lands, _allgather_plan(len(arrs)),
                                                      7 * len(arrs), behind)
        return (srcs, lands, ssem, rsem), token

    def allgather_wait(tag, state, behind):
        srcs, lands, ssem, rsem = state
        return _split_wait("small_wait_" + tag, srcs, lands, ssem, rsem, _allgather_plan(len(srcs)), behind)

    out = {}
    dx2, dp2, h3, a2, dg_pre_f2, dsh3, dsc3 = _ffn_bwd(
        dx3, x2, None, p2, sh3, sc3, gt3, g_pre_f2, g_post_f2, w2i, w2o, df=df2)
    gw2i = _wgrad("wgrad_f2_in", h3.reshape(t, D), dp2.reshape(t, 2 * DFF), 2 * DFF // NCHIP, True)
    gw2o = chip4(_wgrad("wgrad_f2_out", a2.reshape(t, DFF), df2.reshape(t, D), D // 2, False), DFF // NCHIP)
    scat_f2, tok = scatter_start("f2", [gw2i, gw2o], dg_post_f2)
    dy, dm, dg_post_m, dgt2 = _mix_out_bwd(dx2, m, gt2 + tok[0, 0], g_post_m, wmo)
    gwmo = chip4(_wgrad("wgrad_mix_out", yb.reshape(t, D), dm.reshape(t, D), D // 2, False), D // NCHIP)
    (du, dv, dconv, dwcat, dbsp, dgn_g, dgn_b, dgo_a, dgo_b, dcn_g, dcn_b, dcb) = _mix_mid_bwd(
        dy, u, v, conv, gmlp_norm_g, gmlp_norm_b, wcat, wcat_t, bspt, conv_norm_g, conv_norm_b, g_out_a, g_out_b)
    dx1, dproj, h2, dg_pre_m, dsh2, dsc2, dcw = _mix_in_bwd(dx2, x1, du, dv, dconv, a, g, sh2, sc2, g_pre_m, wmi, cw_pad)
    gwmi = _wgrad("wgrad_mix_in", h2.reshape(t, D), dproj.reshape(t, 4 * WA), WA, True)
    scat_mix, tok = scatter_start("mix", [gwmi, gwmo], dg_pre_m)

    vec_grads = dict(g_pre_m=dg_pre_m, g_post_m=dg_post_m, g_pre_f2=dg_pre_f2, g_post_f2=dg_post_f2)
    pair_grads = dict(gmlp_norm_g=dgn_g, gmlp_norm_b=dgn_b, conv_b=dcb, conv_norm_g=dcn_g, conv_norm_b=dcn_b,
                      g_out_a=dgo_a, g_out_b=dgo_b)
    pack = _pack_small([vec_grads[n] for n in VEC_ORDER[2:]], [pair_grads[n] for n in PAIR_ORDER], dbsp, dwcat, lsum)
    dada_early = jnp.concatenate([q.reshape(nb, D) for q in (dsh2, dsc2, dgt2, dsh3, dsc3, dgt3)], axis=1)
    early, tok2 = allgather_start("early", [pack, dcw, dada_early.reshape(8, (nb * 6 * D) // 8)], tok)
    grad_x, dp1, h1, a1, df1, dg_pre_f1, dg_post_f1, dsh1, dsc1, dgt1 = _ffn_bwd(
        dx1, x, f1, p1, sh1 + tok2[0, 0], sc1, gt1, g_pre_f1, g_post_f1, w1i, w1o)
    late_pack = _pack_late([dg_pre_f1, dg_post_f1] + [q.reshape(nb, D) for q in (dsh1, dsc1, dgt1)])
    late, tok2 = allgather_start("late", [late_pack], dg_post_f1)
    gw1i = _wgrad("wgrad_f1_in", h1.reshape(t, D), dp1.reshape(t, 2 * DFF), 2 * DFF // NCHIP, True)
    gw1o = chip4(_wgrad("wgrad_f1_out", a1.reshape(t, DFF), df1.reshape(t, D), D // 2, False), DFF // NCHIP)
    def d2d_start(tag, srcs, lands, plan, behind):
        ssem, rsem, srcs, lands, token = _split_start("d2d_start_" + tag, srcs, lands, plan, len(srcs), behind)
        return (srcs, lands, ssem, rsem, plan), token

    def d2d_wait(tag, state, behind):
        srcs, lands, ssem, rsem, plan = state
        return _split_wait("d2d_wait_" + tag, srcs, lands, ssem, rsem, plan, behind)

    def swap_start(tag, parts, behind):
        return d2d_start(tag, parts, [lax.empty(a.shape, a.dtype) for a in parts], _swap_plan(len(parts)), behind)

    def sums(names, pairs, recv):
        return [_sum4("sum4_" + n, pairs[k][0], recv[k], j_arr) for k, n in enumerate(names)]

    def update(names, part, other):
        for k, n in enumerate(names):
            out[n] = tuple(r[None] for r in _adam_big("adam_" + n, wts[n][0], mom[n][0], var[n][0], part[k], other[k]))

    c_arr = ac.reshape(1).astype(jnp.int32)
    halves = [gw1i[1], gw1o[1]]
    pair_st, tok = d2d_start("pair", halves, [lax.empty((a.shape[0], a.shape[1] // 2, a.shape[2]), a.dtype) for a in halves],
                             _pair_plan([a.shape for a in halves]), tok2)
    names_f2, names_mix, names_f1 = ("w_f2_in", "w_f2_out"), ("w_mix_in", "w_mix_out"), ("w_f1_in", "w_f1_out")
    part_f2 = sums(names_f2, [gw2i, gw2o], scatter_wait("f2", scat_f2, tok))
    sib = d2d_wait("pair", pair_st, part_f2)
    pair_i = _pair_sum("pairsum_f1_in", gw1i[0], sib[0], c_arr)
    pair_o = _pair_sum("pairsum_f1_out", gw1o[0], sib[1], c_arr)
    scat_f1, tok = scatter_start("f1", [pair_i, pair_o], tok2)
    swap_f2, tok = swap_start("swap_f2", part_f2, tok)
    part_mix = sums(names_mix, [gwmi, gwmo], scatter_wait("mix", scat_mix, tok))
    swap_mix, tok = swap_start("swap_mix", part_mix, part_mix[1])

    pack_all, dcw_all, dada_early8 = allgather_wait("early", early, tok)
    (late_all,) = allgather_wait("late", late, pack_all)
    dada_late = jnp.transpose(late_all[:, 2:8, :].reshape(NDEV, 3, nb, D), (0, 2, 1, 3)).reshape(NDEV * nb, 3 * D)
    dada_all = jnp.concatenate([dada_late, dada_early8.reshape(NDEV * nb, 6 * D)], axis=1)
    dada_sh = lax.dynamic_slice(dada_all, (0, j_chip * ADA_SH), (NDEV * nb, ADA_SH))
    out["w_ada"] = tuple(r[None] for r in _ada_bwd_adam(c_all, dada_sh, w_ada[0], m_w_ada[0], v_w_ada[0]))
    update(names_f2, part_f2, d2d_wait("swap_f2", swap_f2, out["w_ada"][3]))
    update(names_mix, part_mix, d2d_wait("swap_mix", swap_mix, out["w_f2_out"][3]))

    mine = sums(names_f1, [pair_i, pair_o], scatter_wait("f1", scat_f1, out["w_mix_out"][3]))
    swap_f1, tok = swap_start("swap_f1", mine, mine[1])
    dcw_mine = lax.dynamic_slice(dcw_all, (0, 0, j_chip * (WB // NCHIP)), (NDEV, HALO, WB // NCHIP))
    small = {n: (wts[n], mom[n], var[n]) for n in list(VEC_ORDER) + list(PAIR_ORDER) + ["b_spatial", "w_spatial", "conv_w", "b_ada"]}
    small_out, loss = _small_adam(pack_all, late_all, dcw_mine, dada_all, small, tok)
    out.update(small_out)
    theirs = d2d_wait("swap_f1", swap_f1, out["b_ada"][3])
    for k, n in enumerate(names_f1):
        out[n] = tuple(r[None] for r in _adam_halves("adam_" + n, wts[n][0], mom[n][0], var[n][0], mine[k], theirs[k],
                                                     c_arr))

    res = [loss, grad_x]
    for k in range(4):
        res += [out[n][k] for n in WEIGHTS]
    return tuple(res)
```

```python
import jax
import jax.numpy as jnp
from jax import lax
from jax.experimental import pallas as pl
from jax.experimental.pallas import tpu as pltpu

D = 1024
DFF = 2816
WA = 512
WB = 512
NH = 8
HD = 64
CH = 128
CK = 31
HALO = 32
NMOD = 9
EPS = 1e-6
NCHIP = 4
NDEV = 8
FBLK = DFF // 2
ADA_SH = NMOD * D // NCHIP

LR, B1, B2, EPS_A, WD, STEP = 0.001, 0.9, 0.999, 1e-08, 0.01, 10

F32 = jnp.float32
BF16 = jnp.bfloat16
MESH = pl.DeviceIdType.MESH
ANY = pl.BlockSpec(memory_space=pl.ANY)
VMEM_FULL = pl.BlockSpec(memory_space=pltpu.VMEM)
VMEM_LIMIT = 56 * 1024 * 1024

NT = (((1,), (1,)), ((), ()))
TN = (((0,), (0,)), ((), ()))


def _dot(a, b):
    return jnp.dot(a, b, preferred_element_type=F32)


def _dot_nt(a, b):
    return lax.dot_general(a, b, NT, preferred_element_type=F32)


def _dot_tn(a, b):
    return lax.dot_general(a, b, TN, preferred_element_type=F32)


def _cparams():
    return pltpu.CompilerParams(vmem_limit_bytes=VMEM_LIMIT)


def _allgather8(name, arrs):
    n = len(arrs)
    remote = _allgather_plan(n)

    def plan(x, y, c, ins, outs):
        _, sends = remote(x, y, c, ins, outs)
        return [(ins[a], outs[a].at[4 * x + 2 * y + c]) for a in range(n)], sends

    shapes = [jax.ShapeDtypeStruct((NDEV,) + a.shape, a.dtype) for a in arrs]
    return _run_exchange(name, arrs, shapes, plan, n, 7 * n)


def _chip_relations(x, y):
    return [(1 - x, y), (x, 1 - y), (1 - x, 1 - y)]


def _exchange(name, arrs, out_shapes, plan):
    n = len(arrs)
    n_out = len(out_shapes)

    def body(*refs):
        ins, outs = refs[:n], refs[n:n + n_out]
        send_sems, recv_sems, local_sems = refs[n + n_out:]
        x, y, c = lax.axis_index("x"), lax.axis_index("y"), lax.axis_index("c")
        local, sends = plan(x, y, c, ins, outs)
        locs = [pltpu.make_async_copy(s, d, local_sems.at[i]) for i, (s, d) in enumerate(local)]
        for loc in locs:
            loc.start()
        cps = [pltpu.make_async_remote_copy(src_ref=s, dst_ref=d, send_sem=send_sems.at[i], recv_sem=recv_sems.at[i],
                                            device_id=peer, device_id_type=MESH)
               for i, (s, d, peer, _) in enumerate(sends)]
        for cp in cps:
            cp.start()
        for i, (s, _, peer, landing) in enumerate(sends):
            pltpu.make_async_remote_copy(src_ref=s, dst_ref=landing, send_sem=send_sems.at[i], recv_sem=recv_sems.at[i],
                                         device_id=peer, device_id_type=MESH).wait_recv()
        for cp in cps:
            cp.wait_send()
        for loc in locs:
            loc.wait()

    return n, n_out, body


def _run_exchange(name, arrs, out_shapes, plan, n_local, n_send):
    n, n_out, body = _exchange(name, arrs, out_shapes, plan)
    return pl.pallas_call(
        body, name=name, out_shape=out_shapes,
        in_specs=[ANY] * n, out_specs=[ANY] * n_out,
        scratch_shapes=[pltpu.SemaphoreType.DMA((n_send,)), pltpu.SemaphoreType.DMA((n_send,)),
                        pltpu.SemaphoreType.DMA((max(n_local, 1),))],
    )(*arrs)


def _chip_allgather(name, arrs, behind=()):
    n = len(arrs)

    def plan(x, y, c, ins, outs):
        j_me = 2 * x + y
        local = [(ins[a], outs[a].at[j_me]) for a in range(n)]
        sends = []
        for a in range(n):
            for (px, py) in _chip_relations(x, y):
                sends.append((ins[a], outs[a].at[j_me], (px, py, c), outs[a].at[2 * px + py]))
        return local, sends

    shapes = [jax.ShapeDtypeStruct((NCHIP,) + a.shape, a.dtype) for a in arrs]
    return _run_exchange(name, list(arrs) + list(behind), shapes, plan, n, 3 * n)


HBM = pl.BlockSpec(memory_space=pltpu.HBM)
SEM = pl.BlockSpec(memory_space=pltpu.SEMAPHORE)
EFFECT = pltpu.SideEffectType.DATAFLOW_SIDE_EFFECTING


def _split_start(name, srcs, lands, plan, n_send, after):
    n, nl = len(srcs), len(lands)

    def body(*refs):
        src, land = refs[:n], refs[n:n + nl]
        send_sems, recv_sems = refs[n + nl + 1], refs[n + nl + 2]
        token = refs[-2]
        local_sems = refs[-1]
        x, y, c = lax.axis_index("x"), lax.axis_index("y"), lax.axis_index("c")
        local, sends = plan(x, y, c, src, land)
        locs = [pltpu.make_async_copy(s, d, local_sems.at[i]) for i, (s, d) in enumerate(local)]
        for loc in locs:
            loc.start()
        for loc in locs:
            loc.wait()
        for i, (s, d, peer, _) in enumerate(sends):
            pltpu.make_async_remote_copy(src_ref=s, dst_ref=d, send_sem=send_sems.at[i], recv_sem=recv_sems.at[i],
                                         device_id=peer, device_id_type=MESH).start()
        token[...] = jnp.zeros_like(token)

    thru = [pltpu.HBM(a.shape, a.dtype) for a in lands]
    srcs = [pltpu.with_memory_space_constraint(a, pltpu.HBM) for a in srcs]
    res = pl.pallas_call(
        body, name=name,
        out_shape=(pltpu.SemaphoreType.DMA((n_send,)), pltpu.SemaphoreType.DMA((n_send,)), *thru,
                   jax.ShapeDtypeStruct((8, 128), F32)),
        in_specs=[HBM] * (n + nl) + [ANY],
        out_specs=(SEM, SEM, *([HBM] * nl), pl.BlockSpec(memory_space=pltpu.VMEM)),
        input_output_aliases={n + i: 2 + i for i in range(nl)},
        scratch_shapes=[pltpu.SemaphoreType.DMA((max(n, 1),))],
        compiler_params=pltpu.CompilerParams(has_side_effects=EFFECT),
    )(*srcs, *[pltpu.with_memory_space_constraint(a, pltpu.HBM) for a in lands], after)
    return res[0], res[1], srcs, list(res[2:2 + nl]), res[-1]


def _split_wait(name, srcs, lands, send_sems, recv_sems, plan, after):
    n, nl = len(srcs), len(lands)
    afters = list(after) if isinstance(after, (list, tuple)) else [after]

    def body(*refs):
        src, land = refs[:n], refs[n:n + nl]
        send_sems, recv_sems = refs[n + nl], refs[n + nl + 1]
        x, y, c = lax.axis_index("x"), lax.axis_index("y"), lax.axis_index("c")
        _, sends = plan(x, y, c, src, land)
        for i, (s, _, peer, landing) in enumerate(sends):
            cp = pltpu.make_async_remote_copy(src_ref=s, dst_ref=landing, send_sem=send_sems.at[i],
                                              recv_sem=recv_sems.at[i], device_id=peer, device_id_type=MESH)
            cp.wait_send()
            cp.wait_recv()

    thru = [pltpu.HBM(a.shape, a.dtype) for a in lands]
    res = pl.pallas_call(
        body, name=name, out_shape=tuple(thru),
        in_specs=[HBM] * (n + nl) + [SEM, SEM] + [ANY] * len(afters), out_specs=tuple([HBM] * nl),
        input_output_aliases={n + i: i for i in range(nl)},
        compiler_params=pltpu.CompilerParams(has_side_effects=EFFECT),
    )(*srcs, *lands, send_sems, recv_sems, *afters)
    return list(res)


def _split_forward(name, srcs, lands, send_a, recv_a, plan_a, plan_b, n_b, after):
    n, nl = len(srcs), len(lands)

    def body(*refs):
        src, land = refs[:n], refs[n:n + nl]
        send_a, recv_a = refs[n + nl], refs[n + nl + 1]
        send_b, recv_b = refs[n + nl + 3], refs[n + nl + 4]
        token = refs[-1]
        x, y, c = lax.axis_index("x"), lax.axis_index("y"), lax.axis_index("c")
        _, first = plan_a(x, y, c, src, land)
        for i, (s, _, peer, landing) in enumerate(first):
            cp = pltpu.make_async_remote_copy(src_ref=s, dst_ref=landing, send_sem=send_a.at[i],
                                              recv_sem=recv_a.at[i], device_id=peer, device_id_type=MESH)
            cp.wait_send()
            cp.wait_recv()
        _, second = plan_b(x, y, c, src, land)
        for i, (s, d, peer, _) in enumerate(second):
            pltpu.make_async_remote_copy(src_ref=s, dst_ref=d, send_sem=send_b.at[i], recv_sem=recv_b.at[i],
                                         device_id=peer, device_id_type=MESH).start()
        token[...] = jnp.zeros_like(token)

    thru = [pltpu.HBM(a.shape, a.dtype) for a in lands]
    res = pl.pallas_call(
        body, name=name,
        out_shape=(pltpu.SemaphoreType.DMA((n_b,)), pltpu.SemaphoreType.DMA((n_b,)), *thru,
                   jax.ShapeDtypeStruct((8, 128), F32)),
        in_specs=[HBM] * (n + nl) + [SEM, SEM, ANY],
        out_specs=(SEM, SEM, *([HBM] * nl), pl.BlockSpec(memory_space=pltpu.VMEM)),
        input_output_aliases={n + i: 2 + i for i in range(nl)},
        compiler_params=pltpu.CompilerParams(has_side_effects=EFFECT),
    )(*srcs, *lands, send_a, recv_a, after)
    return res[0], res[1], list(res[2:2 + nl]), res[-1]


def _gather_plans(shapes):
    n = len(shapes)

    def halves(a, c):
        rows = shapes[a][0] // 2
        return pl.ds(pl.multiple_of(c * rows, 16), rows), pl.ds(pl.multiple_of((1 - c) * rows, 16), rows)

    def split(a):
        return shapes[a][0] % 32 == 0

    def plan_a(x, y, c, src, land):
        j_me = 2 * x + y
        sends = []
        for a in range(n):
            for (px, py) in _chip_relations(x, y):
                if split(a):
                    mine, _ = halves(a, c)
                    sends.append((src[a].at[mine], land[a].at[j_me, mine], (px, py, c), land[a].at[2 * px + py, mine]))
                else:
                    sends.append((src[a], land[a].at[j_me], (px, py, c), land[a].at[2 * px + py]))
        return [], sends

    def plan_b(x, y, c, src, land):
        sends = []
        for a in range(n):
            if split(a):
                mine, other = halves(a, c)
                for (px, py) in _chip_relations(x, y):
                    j = 2 * px + py
                    sends.append((land[a].at[j, mine], land[a].at[j, mine], (x, y, 1 - c), land[a].at[j, other]))
        return [], sends

    n_b = 3 * sum(1 for a in range(n) if split(a))
    return plan_a, plan_b, n_b


def _allgather_plan(n):
    flips = [(dx, dy, dc) for dx in (0, 1) for dy in (0, 1) for dc in (0, 1) if dx or dy or dc]

    def plan(x, y, c, src, land):
        sends = []
        for a in range(n):
            for dx, dy, dc in flips:
                px, py, pc = x ^ dx, y ^ dy, c ^ dc
                sends.append((src[a], land[a].at[4 * x + 2 * y + c], (px, py, pc), land[a].at[4 * px + 2 * py + pc]))
        return [], sends

    return plan


def _scatter_plan(n):
    def plan(x, y, c, src, land):
        sends = []
        for a in range(n):
            for k, (px, py) in enumerate(_chip_relations(x, y)):
                sends.append((src[a].at[2 * px + py], land[a].at[k], (px, py, c), land[a].at[k]))
        return [], sends

    return plan


def _rms(x):
    r = lax.rsqrt(jnp.mean(x * x, axis=-1, keepdims=True) + EPS)
    return x * r, r


def _rms_bwd(dy, n, r, g):
    dg = jnp.sum(dy * n, axis=0, keepdims=True)
    dn = dy * g
    dx = r * (dn - n * jnp.mean(dn * n, axis=-1, keepdims=True))
    return dx, dg


def _ln(x):
    mu = jnp.mean(x, axis=-1, keepdims=True)
    xc = x - mu
    rstd = lax.rsqrt(jnp.mean(xc * xc, axis=-1, keepdims=True) + EPS)
    return xc * rstd, rstd


def _ln_bwd(dy, xhat, rstd, g):
    dg = jnp.sum(dy * xhat, axis=0, keepdims=True)
    db = jnp.sum(dy, axis=0, keepdims=True)
    dxh = dy * g
    dx = rstd * (dxh - jnp.mean(dxh, axis=-1, keepdims=True) - xhat * jnp.mean(dxh * xhat, axis=-1, keepdims=True))
    return dx, dg, db


def _sigmoid(x):
    return jax.nn.sigmoid(x)


def _dsilu(x, s):
    return s * (1.0 + x * (1.0 - s))


def _adam(w, g, m, v):
    m = B1 * m + (1.0 - B1) * g
    v = B2 * v + (1.0 - B2) * (g * g)
    m_hat = m / (1.0 - B1 ** STEP)
    v_hat = v / (1.0 - B2 ** STEP)
    delta = -LR * (m_hat / (jnp.sqrt(v_hat) + EPS_A) + WD * w)
    return delta, m, v


def _head_mask(shape):
    lane = lax.broadcasted_iota(jnp.int32, shape, len(shape) - 1)
    return [(lane >= h * HD) & (lane < (h + 1) * HD) for h in range(NH)]


def _first(b, i):
    return jnp.logical_and(b == 0, i == 0)


def _acc(ref, val, first):
    @pl.when(first)
    def _():
        ref[...] = val

    @pl.when(jnp.logical_not(first))
    def _():
        ref[...] += val


def _ada_fwd(c_all, w_sh, b_sh):
    nb = c_all.shape[0]
    tn = 768

    def body(c_ref, w_ref, b_ref, o_ref):
        cv = c_ref[...]
        cs = (cv * _sigmoid(cv)).astype(BF16)
        o_ref[...] = _dot(cs, w_ref[...].astype(BF16)) + b_ref[...]

    return pl.pallas_call(
        body, name="ada_fwd", grid=(ADA_SH // tn,),
        out_shape=jax.ShapeDtypeStruct((nb, ADA_SH), F32),
        in_specs=[pl.BlockSpec((nb, D), lambda j: (0, 0)), pl.BlockSpec((D, tn), lambda j: (0, j)),
                  pl.BlockSpec((1, tn), lambda j: (0, j))],
        out_specs=pl.BlockSpec((nb, tn), lambda j: (0, j)),
        compiler_params=_cparams(),
    )(c_all, w_sh, b_sh)


def _ada_bwd_adam(c_all, dada_sh, w, m, v):
    nb = c_all.shape[0]
    tn = 768

    def body(c_ref, d_ref, w_ref, m_ref, v_ref, g_out, d_out, m_out, v_out):
        cv = c_ref[...]
        cs = (cv * _sigmoid(cv)).astype(BF16)
        g = _dot_tn(cs, d_ref[...].astype(BF16))
        delta, m2, v2 = _adam(w_ref[...], g, m_ref[...], v_ref[...])
        g_out[...] = g
        d_out[...] = delta
        m_out[...] = m2
        v_out[...] = v2

    big = pl.BlockSpec((D, tn), lambda j: (0, j))
    shape = jax.ShapeDtypeStruct((D, ADA_SH), F32)
    return pl.pallas_call(
        body, name="ada_bwd_adam", grid=(ADA_SH // tn,),
        out_shape=[shape] * 4,
        in_specs=[pl.BlockSpec((nb, D), lambda j: (0, 0)), pl.BlockSpec((nb, tn), lambda j: (0, j)), big, big, big],
        out_specs=[big] * 4,
        compiler_params=_cparams(),
    )(c_all, dada_sh, w, m, v)


def _tok_specs(tm, width):
    return pl.BlockSpec((1, tm, width), lambda b, i: (b, i, 0))


def _mod_spec():
    return pl.BlockSpec((1, 1, D), lambda b, i: (b, 0, 0))


def _row_spec(width=D):
    return pl.BlockSpec((1, width), lambda b, i: (0, 0))


def _ffn_fwd(x, sh, sc, gt, g_pre, g_post, w_in4, w_out, target=None):
    nb, s, _ = x.shape
    tm = min(512, s)
    with_loss = target is not None

    def body(*refs):
        if with_loss:
            (x_ref, sh_ref, sc_ref, gt_ref, gpre_ref, gpost_ref, win_ref, wout_ref, tgt_ref,
             xo_ref, df_ref, p_ref, ls_ref, dgpost_ref, dgt_ref) = refs
        else:
            (x_ref, sh_ref, sc_ref, gt_ref, gpre_ref, gpost_ref, win_ref, wout_ref,
             xo_ref, f_ref, p_ref) = refs
        xv = x_ref[0]
        n, _ = _rms(xv)
        h = (n * gpre_ref[...]) * (1.0 + sc_ref[0]) + sh_ref[0]
        hb = h.astype(BF16)
        acc = jnp.zeros((tm, D), F32)
        for j in range(2):
            gate = _dot(hb, win_ref[j])
            up = _dot(hb, win_ref[2 + j])
            p_ref[0, :, j * FBLK:(j + 1) * FBLK] = gate.astype(BF16)
            p_ref[0, :, DFF + j * FBLK:DFF + (j + 1) * FBLK] = up.astype(BF16)
            a = (gate * _sigmoid(gate)) * up
            acc = acc + _dot(a.astype(BF16), wout_ref[j * FBLK:(j + 1) * FBLK, :])
        nf, q = _rms(acc)
        gpost = gpost_ref[...]
        half_gate = 0.5 * gt_ref[0]
        out = xv + half_gate * (nf * gpost)
        if with_loss:
            first = _first(pl.program_id(0), pl.program_id(1))
            err = out - tgt_ref[0]
            dout = err * (1.0 / D)
            xo_ref[0] = dout
            row = jnp.sum(err * err, axis=0, keepdims=True)
            part = row[:, 0:128]
            for k in range(1, D // 128):
                part = part + row[:, k * 128:(k + 1) * 128]
            _acc(ls_ref, part, first)
            df, dgpost = _rms_bwd(dout * half_gate, nf, q, gpost)
            df_ref[0] = df.astype(BF16)
            _acc(dgpost_ref, dgpost, first)
            _acc(dgt_ref, jnp.sum(dout * (0.5 * (nf * gpost)), axis=0, keepdims=True)[None], pl.program_id(1) == 0)
        else:
            f_ref[0] = acc
            xo_ref[0] = out

    in_specs = [_tok_specs(tm, D), _mod_spec(), _mod_spec(), _mod_spec(), _row_spec(), _row_spec(), VMEM_FULL, VMEM_FULL]
    args = [x, sh, sc, gt, g_pre, g_post, w_in4, w_out]
    out_shape = [jax.ShapeDtypeStruct((nb, s, D), F32), jax.ShapeDtypeStruct((nb, s, D), BF16 if with_loss else F32),
                 jax.ShapeDtypeStruct((nb, s, 2 * DFF), BF16)]
    out_specs = [_tok_specs(tm, D), _tok_specs(tm, D), _tok_specs(tm, 2 * DFF)]
    if with_loss:
        in_specs.append(_tok_specs(tm, D))
        args.append(target)
        out_shape += [jax.ShapeDtypeStruct((1, 128), F32), jax.ShapeDtypeStruct((1, D), F32),
                      jax.ShapeDtypeStruct((nb, 1, D), F32)]
        out_specs += [pl.BlockSpec((1, 128), lambda b, i: (0, 0)), _row_spec(), _mod_spec()]
    return pl.pallas_call(
        body, name="ffn_loss_fwd" if with_loss else "ffn_fwd", grid=(nb, s // tm),
        out_shape=out_shape, in_specs=in_specs, out_specs=out_specs,
        compiler_params=_cparams(),
    )(*args)


def _ffn_up(x, sh, sc, g_pre, w_in4):
    nb, s, _ = x.shape
    tm = min(512, s)

    def body(x_ref, sh_ref, sc_ref, gpre_ref, win_ref, p_ref):
        n, _ = _rms(x_ref[0])
        hb = ((n * gpre_ref[...]) * (1.0 + sc_ref[0]) + sh_ref[0]).astype(BF16)
        for j in range(2):
            p_ref[0, :, j * FBLK:(j + 1) * FBLK] = _dot(hb, win_ref[j]).astype(BF16)
            p_ref[0, :, DFF + j * FBLK:DFF + (j + 1) * FBLK] = _dot(hb, win_ref[2 + j]).astype(BF16)

    return pl.pallas_call(
        body, name="ffn_up", grid=(nb, s // tm),
        out_shape=jax.ShapeDtypeStruct((nb, s, 2 * DFF), BF16),
        in_specs=[_tok_specs(tm, D), _mod_spec(), _mod_spec(), _row_spec(), VMEM_FULL],
        out_specs=_tok_specs(tm, 2 * DFF),
        compiler_params=_cparams(),
    )(x, sh, sc, g_pre, w_in4)


def _ffn_down(x, p, gt, g_post, w_out):
    nb, s, _ = x.shape
    tm = min(512, s)

    def body(x_ref, p_ref, gt_ref, gpost_ref, wout_ref, xo_ref, f_ref):
        acc = jnp.zeros((tm, D), F32)
        for j in range(2):
            gate = p_ref[0, :, j * FBLK:(j + 1) * FBLK].astype(F32)
            up = p_ref[0, :, DFF + j * FBLK:DFF + (j + 1) * FBLK].astype(F32)
            a = (gate * _sigmoid(gate)) * up
            acc = acc + _dot(a.astype(BF16), wout_ref[j * FBLK:(j + 1) * FBLK, :])
        f_ref[0] = acc
        nf, _ = _rms(acc)
        xo_ref[0] = x_ref[0] + (0.5 * gt_ref[0]) * (nf * gpost_ref[...])

    tok = _tok_specs(tm, D)
    shape = jax.ShapeDtypeStruct((nb, s, D), F32)
    return pl.pallas_call(
        body, name="ffn_down", grid=(nb, s // tm), out_shape=[shape, shape],
        in_specs=[tok, _tok_specs(tm, 2 * DFF), _mod_spec(), _row_spec(), VMEM_FULL],
        out_specs=[tok, tok],
        compiler_params=_cparams(),
    )(x, p, gt, g_post, w_out)


def _ffn_bwd(dxo, x, f, p, sh, sc, gt, g_pre, g_post, w_in4, w_out, df=None):
    nb, s, _ = x.shape
    tm = min(256, s)
    given = df is not None

    def body(*refs):
        if given:
            (dxo_ref, x_ref, dfin_ref, p_ref, sh_ref, sc_ref, gpre_ref, win_ref, wout_ref,
             dx_ref, dp_ref, h_ref, a_ref, dgpre_ref, dsh_ref, dsc_ref) = refs
        else:
            (dxo_ref, x_ref, f_ref, p_ref, sh_ref, sc_ref, gt_ref, gpre_ref, gpost_ref, win_ref, wout_ref,
             dx_ref, dp_ref, h_ref, a_ref, df_ref, dgpre_ref, dgpost_ref, dsh_ref, dsc_ref, dgt_ref) = refs
        b, i = pl.program_id(0), pl.program_id(1)
        dxo_v = dxo_ref[0]
        if given:
            dfb = dfin_ref[0]
        else:
            nf, q = _rms(f_ref[0])
            gpost = gpost_ref[...]
            dgt = jnp.sum(dxo_v * (0.5 * (nf * gpost)), axis=0, keepdims=True)
            do = dxo_v * (0.5 * gt_ref[0])
            dfv, dgpost = _rms_bwd(do, nf, q, gpost)
            dfb = dfv.astype(BF16)
            df_ref[0] = dfb
        xv = x_ref[0]
        n, r = _rms(xv)
        gpre = gpre_ref[...]
        ng = n * gpre
        scale1 = 1.0 + sc_ref[0]
        h = ng * scale1 + sh_ref[0]
        h_ref[0] = h.astype(BF16)
        dh = jnp.zeros((tm, D), F32)
        for j in range(2):
            gate = p_ref[0, :, j * FBLK:(j + 1) * FBLK].astype(F32)
            up = p_ref[0, :, DFF + j * FBLK:DFF + (j + 1) * FBLK].astype(F32)
            sg = _sigmoid(gate)
            act = gate * sg
            a_ref[0, :, j * FBLK:(j + 1) * FBLK] = (act * up).astype(BF16)
            da = _dot_nt(dfb, wout_ref[j * FBLK:(j + 1) * FBLK, :])
            dgate = (da * up * _dsilu(gate, sg)).astype(BF16)
            dup = (da * act).astype(BF16)
            dp_ref[0, :, j * FBLK:(j + 1) * FBLK] = dgate
            dp_ref[0, :, DFF + j * FBLK:DFF + (j + 1) * FBLK] = dup
            dh = dh + _dot_nt(dgate, win_ref[j]) + _dot_nt(dup, win_ref[2 + j])
        dsh = jnp.sum(dh, axis=0, keepdims=True)
        dsc = jnp.sum(dh * ng, axis=0, keepdims=True)
        dxn, dgpre = _rms_bwd(dh * scale1, n, r, gpre)
        dx_ref[0] = dxo_v + dxn
        _acc(dgpre_ref, dgpre, _first(b, i))
        _acc(dsh_ref, dsh[None], i == 0)
        _acc(dsc_ref, dsc[None], i == 0)
        if not given:
            _acc(dgpost_ref, dgpost, _first(b, i))
            _acc(dgt_ref, dgt[None], i == 0)

    tok = _tok_specs(tm, D)
    mod_shape = jax.ShapeDtypeStruct((nb, 1, D), F32)
    row_shape = jax.ShapeDtypeStruct((1, D), F32)
    big = [jax.ShapeDtypeStruct((nb, s, D), F32), jax.ShapeDtypeStruct((nb, s, 2 * DFF), BF16),
           jax.ShapeDtypeStruct((nb, s, D), BF16), jax.ShapeDtypeStruct((nb, s, DFF), BF16)]
    big_specs = [tok, _tok_specs(tm, 2 * DFF), tok, _tok_specs(tm, DFF)]
    if given:
        return pl.pallas_call(
            body, name="ffn_bwd_after_loss", grid=(nb, s // tm),
            out_shape=big + [row_shape, mod_shape, mod_shape],
            in_specs=[tok, tok, tok, _tok_specs(tm, 2 * DFF), _mod_spec(), _mod_spec(), _row_spec(), VMEM_FULL, VMEM_FULL],
            out_specs=big_specs + [_row_spec(), _mod_spec(), _mod_spec()],
            compiler_params=_cparams(),
        )(dxo, x, df, p, sh, sc, g_pre, w_in4, w_out)
    return pl.pallas_call(
        body, name="ffn_bwd", grid=(nb, s // tm),
        out_shape=big + [jax.ShapeDtypeStruct((nb, s, D), BF16), row_shape, row_shape, mod_shape, mod_shape, mod_shape],
        in_specs=[tok, tok, tok, _tok_specs(tm, 2 * DFF), _mod_spec(), _mod_spec(), _mod_spec(), _row_spec(), _row_spec(),
                  VMEM_FULL, VMEM_FULL],
        out_specs=big_specs + [tok, _row_spec(), _row_spec(), _mod_spec(), _mod_spec(), _mod_spec()],
        compiler_params=_cparams(),
    )(dxo, x, f, p, sh, sc, gt, g_pre, g_post, w_in4, w_out)


def _wgrad(name, a, b, col_block, chip_major):
    t, ka = a.shape
    n = b.shape[1]
    tk = min(t, 512)
    while tk * 2 <= t and t % (tk * 2) == 0 and 2 * (tk * 2) * max(ka, col_block) <= 6 * 1024 * 1024:
        tk *= 2
    nk = t // tk
    nblk = n // col_block

    def body(a_ref, b_ref, o_ref, obf_ref, acc_ref):
        k = pl.program_id(1)

        @pl.when(k == 0)
        def _():
            acc_ref[...] = jnp.zeros_like(acc_ref)

        acc_ref[...] += _dot_tn(a_ref[...], b_ref[...])

        @pl.when(k == nk - 1)
        def _():
            val = acc_ref[...]
            if chip_major:
                o_ref[0] = val
                obf_ref[0] = val.astype(BF16)
            else:
                o_ref[...] = val
                obf_ref[...] = val.astype(BF16)

    if chip_major:
        shape = (nblk, ka, col_block)
        ospec = pl.BlockSpec((1, ka, col_block), lambda j, k: (j, 0, 0))
    else:
        shape = (ka, n)
        ospec = pl.BlockSpec((ka, col_block), lambda j, k: (0, j))
    return pl.pallas_call(
        body, name=name, grid=(nblk, nk),
        out_shape=[jax.ShapeDtypeStruct(shape, F32), jax.ShapeDtypeStruct(shape, BF16)],
        in_specs=[pl.BlockSpec((tk, ka), lambda j, k: (k, 0)), pl.BlockSpec((tk, col_block), lambda j, k: (k, j))],
        out_specs=[ospec, ospec],
        scratch_shapes=[pltpu.VMEM((ka, col_block), F32)],
        compiler_params=_cparams(),
    )(a, b)


def _mix_in_fwd(x, sh, sc, g_pre, w_mi4):
    nb, s, _ = x.shape
    tm = min(512, s)

    def body(x_ref, sh_ref, sc_ref, gpre_ref, w_ref, u_ref, v_ref, a_ref, g_ref):
        n, _ = _rms(x_ref[0])
        hb = ((n * gpre_ref[...]) * (1.0 + sc_ref[0]) + sh_ref[0]).astype(BF16)
        for k, o_ref in enumerate((u_ref, v_ref, a_ref, g_ref)):
            o_ref[0] = _dot(hb, w_ref[k])

    shape = jax.ShapeDtypeStruct((nb, s, WA), F32)
    return pl.pallas_call(
        body, name="mix_in_fwd", grid=(nb, s // tm),
        out_shape=[shape] * 4,
        in_specs=[_tok_specs(tm, D), _mod_spec(), _mod_spec(), _row_spec(), VMEM_FULL],
        out_specs=[_tok_specs(tm, WA)] * 4,
        compiler_params=_cparams(),
    )(x, sh, sc, g_pre, w_mi4)


def _spatial_weights(wcat_ref, transposed):
    w = wcat_ref[...]
    row = lax.broadcasted_iota(jnp.int32, w.shape, 0)
    col = lax.broadcasted_iota(jnp.int32, w.shape, 1)
    keep = ((row & (CH - 1)) <= col) if transposed else ((col & (CH - 1)) <= row)
    return jnp.where(keep, w, 0.0).astype(BF16)


def _expand_heads(vc, masks):
    return jnp.concatenate([jnp.where(mk, vc, jnp.zeros_like(vc)) for mk in masks], axis=0)


def _spatial_bias(bspt_ref):
    return bspt_ref[...]


SHIFTS = 8
TAP_ROWS = 32


def _ext_rows(tm):
    return tm + HALO + SHIFTS


def _make_shifts(ext_ref, sh_ref, tm):
    ext_ref[tm + HALO:tm + HALO + SHIFTS, :] = jnp.zeros((SHIFTS, WB), F32)
    for r in range(SHIFTS):
        sh_ref[r] = ext_ref[r:r + tm + HALO, :]


def _conv_taps(sh_ref, w_ref, tm, taps, emit):
    def block(i, carry):
        r0 = pl.multiple_of(i * TAP_ROWS, TAP_ROWS)
        acc = jnp.zeros((TAP_ROWS, WB), F32)
        for o, k in taps:
            acc = acc + w_ref[k:k + 1, :] * sh_ref[o % SHIFTS, pl.ds(r0 + SHIFTS * (o // SHIFTS), TAP_ROWS), :]
        emit(r0, acc)
        return carry

    lax.fori_loop(0, tm // TAP_ROWS, block, 0)


def _halo_prev_spec(tm):
    return pl.BlockSpec((1, HALO, WB), lambda b, i: (b, jnp.maximum(i * (tm // HALO) - 1, 0), 0))


def _halo_next_spec(tm, s):
    return pl.BlockSpec((1, HALO, WB), lambda b, i: (b, jnp.minimum((i + 1) * (tm // HALO), s // HALO - 1), 0))


def _mix_mid_fwd(x, u, v, a, g, gt, gn_g, gn_b, wcat, bspt, conv_w, conv_b, cn_g, cn_b, go_a, go_b, w_mo, g_post):
    nb, s, _ = x.shape
    tm = min(512, s)

    def body(x_ref, u_ref, v_ref, a_ref, g_ref, ah_ref, gh_ref, gt_ref, gng_ref, gnb_ref, wcat_ref, bspt_ref,
             cw_ref, cb_ref, cng_ref, cnb_ref, goa_ref, gob_ref, wmo_ref, gpost_ref,
             xo_ref, conv_ref, y_ref, m_ref, ext_ref, sh_ref):
        i = pl.program_id(1)
        xhat, _ = _ln(v_ref[0])
        vb = (xhat * gng_ref[...] + gnb_ref[...]).astype(BF16)
        wsb = _spatial_weights(wcat_ref, False)
        bias = _spatial_bias(bspt_ref)
        masks = _head_mask((CH, WA))
        zs = []
        for cidx in range(tm // CH):
            vexp = _expand_heads(vb[cidx * CH:(cidx + 1) * CH, :], masks)
            zs.append(_dot(wsb, vexp) + bias)
        z = jnp.concatenate(zs, axis=0)
        na, _ = _rms(u_ref[0] * z)
        keep = jnp.where(i == 0, 0.0, 1.0).astype(F32)
        ext_ref[0:HALO, :] = (ah_ref[0] * _sigmoid(gh_ref[0])) * keep
        ext_ref[HALO:HALO + tm, :] = a_ref[0] * _sigmoid(g_ref[0])
        _make_shifts(ext_ref, sh_ref, tm)
        cb = cb_ref[...]

        def put_conv(r0, acc):
            conv_ref[0, pl.ds(r0, TAP_ROWS), :] = acc + cb

        _conv_taps(sh_ref, cw_ref, tm, [(k + HALO - (CK - 1), k) for k in range(CK)], put_conv)
        conv = conv_ref[0]
        chat, _ = _ln(conv)
        cln = chat * cng_ref[...] + cnb_ref[...]
        nbb, _ = _rms(cln * _sigmoid(cln))
        yb = jnp.concatenate([na * goa_ref[...], nbb * gob_ref[...]], axis=1).astype(BF16)
        y_ref[0] = yb
        m = _dot(yb, wmo_ref[...])
        m_ref[0] = m
        nm, _ = _rms(m)
        xo_ref[0] = x_ref[0] + gt_ref[0] * (nm * gpost_ref[...])

    t5 = _tok_specs(tm, WA)
    tok = _tok_specs(tm, D)
    r5 = _row_spec(WA)
    full = lambda shape: pl.BlockSpec(shape, lambda b, i: (0,) * len(shape))
    return pl.pallas_call(
        body, name="mix_mid_fwd", grid=(nb, s // tm),
        out_shape=[jax.ShapeDtypeStruct((nb, s, D), F32), jax.ShapeDtypeStruct((nb, s, WB), F32),
                   jax.ShapeDtypeStruct((nb, s, D), BF16), jax.ShapeDtypeStruct((nb, s, D), F32)],
        in_specs=[tok, t5, t5, t5, t5, _halo_prev_spec(tm), _halo_prev_spec(tm), _mod_spec(), r5, r5,
                  full((CH, NH * CH)), full((CH, WA)), full((HALO, WB)), r5, r5, r5, r5, r5, VMEM_FULL, _row_spec()],
        out_specs=[tok, t5, tok, tok],
        scratch_shapes=[pltpu.VMEM((_ext_rows(tm), WB), F32), pltpu.VMEM((SHIFTS, tm + HALO, WB), F32)],
        compiler_params=_cparams(),
    )(x, u, v, a, g, a, g, gt, gn_g, gn_b, wcat, bspt, conv_w, conv_b, cn_g, cn_b, go_a, go_b, w_mo, g_post)


def _mix_out_bwd(dxo, m, gt, g_post, w_mo):
    nb, s, _ = m.shape
    tm = min(512, s)

    def body(dxo_ref, m_ref, gt_ref, gpost_ref, wmo_ref, dy_ref, dm_ref, dgpost_ref, dgt_ref):
        b, i = pl.program_id(0), pl.program_id(1)
        dxo_v = dxo_ref[0]
        nm, q = _rms(m_ref[0])
        gpost = gpost_ref[...]
        dgt = jnp.sum(dxo_v * (nm * gpost), axis=0, keepdims=True)
        dm, dgpost = _rms_bwd(dxo_v * gt_ref[0], nm, q, gpost)
        dmb = dm.astype(BF16)
        dm_ref[0] = dmb
        dy_ref[0] = _dot_nt(dmb, wmo_ref[...])
        _acc(dgpost_ref, dgpost, _first(b, i))
        _acc(dgt_ref, dgt[None], i == 0)

    tok = _tok_specs(tm, D)
    return pl.pallas_call(
        body, name="mix_out_bwd", grid=(nb, s // tm),
        out_shape=[jax.ShapeDtypeStruct((nb, s, D), F32), jax.ShapeDtypeStruct((nb, s, D), BF16),
                   jax.ShapeDtypeStruct((1, D), F32), jax.ShapeDtypeStruct((nb, 1, D), F32)],
        in_specs=[tok, tok, _mod_spec(), _row_spec(), VMEM_FULL],
        out_specs=[tok, tok, _row_spec(), _mod_spec()],
        compiler_params=_cparams(),
    )(dxo, m, gt, g_post, w_mo)


def _mix_mid_bwd(dy, u, v, conv, gn_g, gn_b, wcat, wcat_t, bspt, cn_g, cn_b, go_a, go_b):
    nb, s, _ = dy.shape
    tm = min(512, s)
    nchunk = tm // CH

    def body(dy_ref, u_ref, v_ref, conv_ref, gng_ref, gnb_ref, wcat_ref, wcatt_ref, bspt_ref, cng_ref, cnb_ref,
             goa_ref, gob_ref,
             du_ref, dv_ref, dconv_ref, dwcat_ref, dbsp_ref, dgng_ref, dgnb_ref, dgoa_ref, dgob_ref,
             dcng_ref, dcnb_ref, dcb_ref):
        first = _first(pl.program_id(0), pl.program_id(1))
        dyv = dy_ref[0]
        xhat, rstd = _ln(v_ref[0])
        gng = gng_ref[...]
        vb = (xhat * gng + gnb_ref[...]).astype(BF16)
        wsb = _spatial_weights(wcat_ref, False)
        wsb_t = _spatial_weights(wcatt_ref, True)
        bias = _spatial_bias(bspt_ref)
        masks = _head_mask((CH, WA))
        vexps, zs = [], []
        for cidx in range(nchunk):
            vexp = _expand_heads(vb[cidx * CH:(cidx + 1) * CH, :], masks)
            vexps.append(vexp)
            zs.append(_dot(wsb, vexp) + bias)
        z = jnp.concatenate(zs, axis=0)
        uv = u_ref[0]
        na, ra = _rms(uv * z)
        dya, dgoa = _rms_bwd(dyv[:, 0:WA], na, ra, goa_ref[...])
        du_ref[0] = dya * z
        dz = dya * uv
        dwcat = jnp.zeros((CH, NH * CH), F32)
        dzsum = jnp.zeros((CH, WA), F32)
        dvlns = []
        for cidx in range(nchunk):
            dzc = dz[cidx * CH:(cidx + 1) * CH, :]
            dzsum = dzsum + dzc
            dzb = dzc.astype(BF16)
            dwcat = dwcat + _dot_nt(dzb, vexps[cidx])
            dvexp = _dot(wsb_t, dzb)
            dvl = jnp.zeros((CH, WA), F32)
            for h in range(NH):
                dvl = dvl + jnp.where(masks[h], dvexp[h * CH:(h + 1) * CH, :], 0.0)
            dvlns.append(dvl)
        dvln = jnp.concatenate(dvlns, axis=0)
        dv, dgng, dgnb = _ln_bwd(dvln, xhat, rstd, gng)
        dv_ref[0] = dv
        lane = lax.broadcasted_iota(jnp.int32, (NH, WA), 1)
        head = lax.broadcasted_iota(jnp.int32, (NH, WA), 0)
        sel = jnp.where((lane >= head * HD) & (lane < (head + 1) * HD), 1.0, 0.0).astype(F32)
        dbsp = lax.dot_general(sel, dzsum, NT, preferred_element_type=F32, precision=lax.Precision.HIGHEST)
        chat, crstd = _ln(conv_ref[0])
        cng = cng_ref[...]
        cln = chat * cng + cnb_ref[...]
        sg = _sigmoid(cln)
        nbb, rb = _rms(cln * sg)
        dyb, dgob = _rms_bwd(dyv[:, WA:D], nbb, rb, gob_ref[...])
        dconv, dcng, dcnb = _ln_bwd(dyb * _dsilu(cln, sg), chat, crstd, cng)
        dconv_ref[0] = dconv
        dcb = jnp.sum(dconv, axis=0, keepdims=True)
        for ref, val in ((dwcat_ref, dwcat), (dbsp_ref, dbsp), (dgng_ref, dgng), (dgnb_ref, dgnb), (dgoa_ref, dgoa),
                         (dgob_ref, dgob), (dcng_ref, dcng), (dcnb_ref, dcnb), (dcb_ref, dcb)):
            _acc(ref, val, first)

    t5 = _tok_specs(tm, WA)
    r5 = _row_spec(WA)
    full = lambda shape: pl.BlockSpec(shape, lambda b, i: (0,) * len(shape))
    big = jax.ShapeDtypeStruct((nb, s, WA), F32)
    row = jax.ShapeDtypeStruct((1, WA), F32)
    return pl.pallas_call(
        body, name="mix_mid_bwd", grid=(nb, s // tm),
        out_shape=[big, big, big, jax.ShapeDtypeStruct((CH, NH * CH), F32), jax.ShapeDtypeStruct((NH, CH), F32),
                   row, row, row, row, row, row, row],
        in_specs=[_tok_specs(tm, D), t5, t5, t5, r5, r5, full((CH, NH * CH)), full((NH * CH, CH)), full((CH, WA)),
                  r5, r5, r5, r5],
        out_specs=[t5, t5, t5, full((CH, NH * CH)), full((NH, CH)), r5, r5, r5, r5, r5, r5, r5],
        compiler_params=_cparams(),
    )(dy, u, v, conv, gn_g, gn_b, wcat, wcat_t, bspt, cn_g, cn_b, go_a, go_b)


def _mix_in_bwd(dxo, x, du, dv, dconv, a, g, sh, sc, g_pre, w_mi4, conv_w):
    nb, s, _ = x.shape
    tm = min(512, s)
    n_i = s // tm

    def body(dxo_ref, x_ref, du_ref, dv_ref, dc_ref, dch_ref, a_ref, g_ref, ah_ref, gh_ref, sh_ref, sc_ref,
             gpre_ref, w_ref, cw_ref,
             dx_ref, dproj_ref, h_ref, dgpre_ref, dsh_ref, dsc_ref, dcw_ref, ext_ref, shf_ref, dglu_ref):
        b, i = pl.program_id(0), pl.program_id(1)
        first = _first(b, i)
        av, gv = a_ref[0], g_ref[0]
        sg = _sigmoid(gv)
        dconv = dc_ref[0]
        ext_ref[0:tm, :] = dconv
        ext_ref[tm:tm + HALO, :] = dch_ref[0] * jnp.where(i == n_i - 1, 0.0, 1.0).astype(F32)
        _make_shifts(ext_ref, shf_ref, tm)

        def put_dglu(r0, acc):
            dglu_ref[pl.ds(r0, TAP_ROWS), :] = acc

        _conv_taps(shf_ref, cw_ref, tm, [(CK - 1 - k, k) for k in range(CK)], put_dglu)
        dglu = dglu_ref[...]
        ext_ref[0:HALO, :] = (ah_ref[0] * _sigmoid(gh_ref[0])) * jnp.where(i == 0, 0.0, 1.0).astype(F32)
        ext_ref[HALO:HALO + tm, :] = av * sg
        _make_shifts(ext_ref, shf_ref, tm)

        @pl.when(first)
        def _():
            dcw_ref[...] = jnp.zeros((HALO, WB), F32)

        for k in range(CK):
            o = k + HALO - (CK - 1)
            lo = SHIFTS * (o // SHIFTS)
            dcw_ref[k:k + 1, :] += jnp.sum(dconv * shf_ref[o % SHIFTS, lo:lo + tm, :], axis=0, keepdims=True)
        da = dglu * sg
        dg = dglu * av * (sg * (1.0 - sg))
        parts = [du_ref[0].astype(BF16), dv_ref[0].astype(BF16), da.astype(BF16), dg.astype(BF16)]
        dh = jnp.zeros((tm, D), F32)
        for k in range(4):
            dproj_ref[0, :, k * WA:(k + 1) * WA] = parts[k]
            dh = dh + _dot_nt(parts[k], w_ref[k])
        n, r = _rms(x_ref[0])
        gpre = gpre_ref[...]
        ng = n * gpre
        scale1 = 1.0 + sc_ref[0]
        h_ref[0] = (ng * scale1 + sh_ref[0]).astype(BF16)
        dsh = jnp.sum(dh, axis=0, keepdims=True)
        dsc = jnp.sum(dh * ng, axis=0, keepdims=True)
        dxn, dgpre = _rms_bwd(dh * scale1, n, r, gpre)
        dx_ref[0] = dxo_ref[0] + dxn
        _acc(dgpre_ref, dgpre, first)
        _acc(dsh_ref, dsh[None], i == 0)
        _acc(dsc_ref, dsc[None], i == 0)

    tok = _tok_specs(tm, D)
    t5 = _tok_specs(tm, WA)
    full = lambda shape: pl.BlockSpec(shape, lambda b, i: (0,) * len(shape))
    mod_shape = jax.ShapeDtypeStruct((nb, 1, D), F32)
    return pl.pallas_call(
        body, name="mix_in_bwd", grid=(nb, n_i),
        out_shape=[jax.ShapeDtypeStruct((nb, s, D), F32), jax.ShapeDtypeStruct((nb, s, 4 * WA), BF16),
                   jax.ShapeDtypeStruct((nb, s, D), BF16), jax.ShapeDtypeStruct((1, D), F32), mod_shape, mod_shape,
                   jax.ShapeDtypeStruct((HALO, WB), F32)],
        in_specs=[tok, tok, t5, t5, t5, _halo_next_spec(tm, s), t5, t5, _halo_prev_spec(tm), _halo_prev_spec(tm),
                  _mod_spec(), _mod_spec(), _row_spec(), VMEM_FULL, full((HALO, WB))],
        out_specs=[tok, _tok_specs(tm, 4 * WA), tok, _row_spec(), _mod_spec(), _mod_spec(), full((HALO, WB))],
        scratch_shapes=[pltpu.VMEM((_ext_rows(tm), WB), F32), pltpu.VMEM((SHIFTS, tm + HALO, WB), F32),
                        pltpu.VMEM((tm, WB), F32)],
        compiler_params=_cparams(),
    )(dxo, x, du, dv, dconv, dconv, a, g, a, g, sh, sc, g_pre, w_mi4, conv_w)


def _row_tile(rows, cols):
    best = 16
    for t in range(16, rows + 1, 16):
        if rows % t == 0 and t * cols * 4 <= 1536 * 1024:
            best = t
    return best


def _sum4(name, own4, recv, j_arr):
    _, rows, cols = own4.shape
    tr = _row_tile(rows, cols)

    def body(j_ref, own_ref, recv_ref, o_ref):
        del j_ref
        acc = own_ref[0]
        for k in range(3):
            acc = acc + recv_ref[k].astype(F32)
        o_ref[...] = acc

    return pl.pallas_call(
        body, name=name,
        grid_spec=pltpu.PrefetchScalarGridSpec(
            num_scalar_prefetch=1, grid=(rows // tr,),
            in_specs=[pl.BlockSpec((1, tr, cols), lambda i, j: (j[0], i, 0)),
                      pl.BlockSpec((3, tr, cols), lambda i, j: (0, i, 0))],
            out_specs=pl.BlockSpec((tr, cols), lambda i, j: (i, 0))),
        out_shape=jax.ShapeDtypeStruct((rows, cols), F32),
        compiler_params=_cparams(),
    )(j_arr, own4, recv)


def _pair_plan(shapes):
    def plan(x, y, c, src, land):
        sends = []
        for a, shape in enumerate(shapes):
            rows = shape[1] // 2
            theirs = pl.ds(pl.multiple_of((1 - c) * rows, 16), rows)
            sends.append((src[a].at[:, theirs], land[a], (x, y, 1 - c), land[a]))
        return [], sends

    return plan


def _swap_plan(n):
    def plan(x, y, c, src, land):
        return [], [(src[a], land[a], (x, y, 1 - c), land[a]) for a in range(n)]

    return plan


def _pair_sum(name, g32, recv, c_arr):
    nblk, rows, cols = recv.shape
    tr = _row_tile(rows, cols)
    nh = rows // tr

    def body(c_ref, g_ref, r_ref, o32_ref, obf_ref):
        del c_ref
        val = g_ref[0] + r_ref[0].astype(F32)
        o32_ref[0] = val
        obf_ref[0] = val.astype(BF16)

    spec = pl.BlockSpec((1, tr, cols), lambda k, i, c: (k, i, 0))
    return pl.pallas_call(
        body, name=name,
        grid_spec=pltpu.PrefetchScalarGridSpec(
            num_scalar_prefetch=1, grid=(nblk, nh),
            in_specs=[pl.BlockSpec((1, tr, cols), lambda k, i, c: (k, c[0] * nh + i, 0)), spec],
            out_specs=[spec, spec]),
        out_shape=[jax.ShapeDtypeStruct(recv.shape, F32), jax.ShapeDtypeStruct(recv.shape, BF16)],
        compiler_params=_cparams(),
    )(c_arr, g32, recv)


def _adam_halves(name, w, m, v, mine, theirs, c_arr):
    rows, cols = w.shape
    tr = _row_tile(rows // 2, cols)
    nh = (rows // 2) // tr

    def body(c_ref, w_ref, m_ref, v_ref, mine_ref, theirs_ref, g_out, d_out, m_out, v_out):
        here = (pl.program_id(0) // nh) == c_ref[0]
        g = jnp.where(here, mine_ref[...], theirs_ref[...])
        delta, m2, v2 = _adam(w_ref[...], g, m_ref[...], v_ref[...])
        g_out[...] = g
        d_out[...] = delta
        m_out[...] = m2
        v_out[...] = v2

    spec = pl.BlockSpec((tr, cols), lambda i, c: (i, 0))
    shape = jax.ShapeDtypeStruct((rows, cols), F32)
    return pl.pallas_call(
        body, name=name,
        grid_spec=pltpu.PrefetchScalarGridSpec(
            num_scalar_prefetch=1, grid=(2 * nh,),
            in_specs=[spec, spec, spec,
                      pl.BlockSpec((tr, cols), lambda i, c: (jnp.clip(i - c[0] * nh, 0, nh - 1), 0)),
                      pl.BlockSpec((tr, cols), lambda i, c: (jnp.clip(i - (1 - c[0]) * nh, 0, nh - 1), 0))],
            out_specs=[spec] * 4),
        out_shape=[shape] * 4,
        compiler_params=_cparams(),
    )(c_arr, w, m, v, mine, theirs)


def _adam_big(name, w, m, v, ga, gb):
    rows, cols = w.shape
    tr = _row_tile(rows, cols)

    def body(w_ref, m_ref, v_ref, ga_ref, gb_ref, g_out, d_out, m_out, v_out):
        gsum = ga_ref[...] + gb_ref[...]
        delta, m2, v2 = _adam(w_ref[...], gsum, m_ref[...], v_ref[...])
        g_out[...] = gsum
        d_out[...] = delta
        m_out[...] = m2
        v_out[...] = v2

    spec = pl.BlockSpec((tr, cols), lambda i: (i, 0))
    shape = jax.ShapeDtypeStruct((rows, cols), F32)
    return pl.pallas_call(
        body, name=name, grid=(rows // tr,), out_shape=[shape] * 4,
        in_specs=[spec] * 5, out_specs=[spec] * 4, compiler_params=_cparams(),
    )(w, m, v, ga, gb)


PK_VEC = 0
PK_LOSS = 6
PK_PAIR = 8
PK_BSP = 16
PK_WCAT = 24
PK_ROWS = PK_WCAT + CH
PAIR_ORDER = ("gmlp_norm_g", "gmlp_norm_b", "conv_b", "conv_norm_g", "conv_norm_b", "g_out_a", "g_out_b")
VEC_ORDER = ("g_pre_f1", "g_post_f1", "g_pre_m", "g_post_m", "g_pre_f2", "g_post_f2")


def _pack_late(rows):
    counts = [r.shape[0] for r in rows]
    assert sum(counts) == 8

    def body(*refs):
        o_ref = refs[-1]
        at = 0
        for r, cnt in zip(refs[:-1], counts):
            o_ref[at:at + cnt, :] = r[...]
            at += cnt

    return pl.pallas_call(
        body, name="pack_late", out_shape=jax.ShapeDtypeStruct((8, D), F32),
        in_specs=[VMEM_FULL] * len(rows), out_specs=VMEM_FULL, compiler_params=_cparams(),
    )(*rows)


def _pack_small(vecs, pairs, dbsp, dwcat, lsum):
    def body(*refs):
        vec_refs = refs[:4]
        pair_refs = refs[4:11]
        dbsp_ref, dwcat_ref, lsum_ref, o_ref = refs[11:]
        o_ref[0:PK_WCAT, :] = jnp.zeros((PK_WCAT, D), F32)
        o_ref[PK_LOSS:PK_LOSS + 1, 0:128] = lsum_ref[...]
        for k, r in enumerate(vec_refs):
            o_ref[PK_VEC + 2 + k:PK_VEC + 3 + k, :] = r[...]
        for k, r in enumerate(pair_refs):
            row, half = PK_PAIR + k // 2, k % 2
            o_ref[row:row + 1, half * WA:(half + 1) * WA] = r[...]
        o_ref[PK_BSP:PK_BSP + NH, 0:CH] = dbsp_ref[...]
        o_ref[PK_WCAT:PK_ROWS, :] = dwcat_ref[...]

    args = list(vecs) + list(pairs) + [dbsp, dwcat, lsum]
    return pl.pallas_call(
        body, name="pack_small", out_shape=jax.ShapeDtypeStruct((PK_ROWS, D), F32),
        in_specs=[VMEM_FULL] * len(args), out_specs=VMEM_FULL, compiler_params=_cparams(),
    )(*args)


def _small_adam(pack_all, late_all, dcw_all, dada_all, params, behind):
    names = list(VEC_ORDER) + list(PAIR_ORDER) + ["b_spatial", "w_spatial", "conv_w", "b_ada"]
    flat = []
    for nm in names:
        flat += list(params[nm])
    n_in = 4 + len(flat)

    def body(*refs):
        pack_ref, late_ref, dcw_ref, dada_ref = refs[:4]
        prm = refs[4:n_in]
        outs = refs[n_in + 1:]

        def total(r0, nr, c0, nc):
            acc = pack_ref[0, r0:r0 + nr, c0:c0 + nc]
            for d in range(1, NDEV):
                acc = acc + pack_ref[d, r0:r0 + nr, c0:c0 + nc]
            return acc

        def emit(idx, g, getw, put):
            w_ref, m_ref, v_ref = prm[3 * idx:3 * idx + 3]
            delta, m2, v2 = _adam(getw(w_ref), g, getw(m_ref), getw(v_ref))
            for o_ref, val in zip(outs[4 * idx:4 * idx + 4], (g, delta, m2, v2)):
                put(o_ref, val)

        def whole(ref):
            return ref[...]

        def put_whole(ref, val):
            ref[...] = val

        idx = 0
        for k in range(6):
            if k < 2:
                g = late_ref[0, k:k + 1, :]
                for d in range(1, NDEV):
                    g = g + late_ref[d, k:k + 1, :]
            else:
                g = total(PK_VEC + k, 1, 0, D)
            emit(idx, g, whole, put_whole)
            idx += 1
        for k in range(7):
            emit(idx, total(PK_PAIR + k // 2, 1, (k % 2) * WA, WA), whole, put_whole)
            idx += 1
        emit(idx, total(PK_BSP, NH, 0, CH), lambda r: r[0], lambda r, val: r.__setitem__(0, val))
        idx += 1
        row = lax.broadcasted_iota(jnp.int32, (CH, CH), 0)
        col = lax.broadcasted_iota(jnp.int32, (CH, CH), 1)
        for h in range(NH):
            gh = jnp.where(col <= row, total(PK_WCAT, CH, h * CH, CH), 0.0)
            w_ref, m_ref, v_ref = prm[3 * idx:3 * idx + 3]
            delta, m2, v2 = _adam(w_ref[0, h], gh, m_ref[0, h], v_ref[0, h])
            for o_ref, val in zip(outs[4 * idx:4 * idx + 4], (gh, delta, m2, v2)):
                o_ref[0, h] = val
        idx += 1
        gcw = dcw_ref[0, 0:CK, :]
        for d in range(1, NDEV):
            gcw = gcw + dcw_ref[d, 0:CK, :]
        emit(idx, gcw, lambda r: r[0], lambda r, val: r.__setitem__(0, val))
        idx += 1
        emit(idx, jnp.sum(dada_ref[...], axis=0, keepdims=True), whole, put_whole)
        outs[-1][...] = jnp.sum(total(PK_LOSS, 1, 0, 128), axis=1, keepdims=True) * (0.5 / D)

    out_shape = []
    for nm in names:
        w = params[nm][0]
        out_shape += [jax.ShapeDtypeStruct(w.shape, F32)] * 4
    out_shape.append(jax.ShapeDtypeStruct((1, 1), F32))
    res = pl.pallas_call(
        body, name="small_adam", out_shape=out_shape,
        in_specs=[VMEM_FULL] * n_in + [ANY], out_specs=[VMEM_FULL] * len(out_shape), compiler_params=_cparams(),
    )(pack_all, late_all, dcw_all, dada_all, *flat, behind)
    return {nm: tuple(res[4 * k:4 * k + 4]) for k, nm in enumerate(names)}, res[-1].reshape(())


WEIGHTS = ['w_ada', 'b_ada', 'g_pre_f1', 'g_post_f1', 'w_f1_in', 'w_f1_out', 'g_pre_m', 'g_post_m', 'w_mix_in',
           'gmlp_norm_g', 'gmlp_norm_b', 'w_spatial', 'b_spatial', 'conv_w', 'conv_b', 'conv_norm_g', 'conv_norm_b',
           'g_out_a', 'g_out_b', 'w_mix_out', 'g_pre_f2', 'g_post_f2', 'w_f2_in', 'w_f2_out']
BIG = ('w_f1_in', 'w_f1_out', 'w_mix_in', 'w_mix_out', 'w_f2_in', 'w_f2_out')


def kernel(x, c, w_ada, b_ada, g_pre_f1, g_post_f1, w_f1_in, w_f1_out, g_pre_m, g_post_m, w_mix_in, gmlp_norm_g, gmlp_norm_b, w_spatial, b_spatial, conv_w, conv_b, conv_norm_g, conv_norm_b, g_out_a, g_out_b, w_mix_out, g_pre_f2, g_post_f2, w_f2_in, w_f2_out, loss_target, m_w_ada, m_b_ada, m_g_pre_f1, m_g_post_f1, m_w_f1_in, m_w_f1_out, m_g_pre_m, m_g_post_m, m_w_mix_in, m_gmlp_norm_g, m_gmlp_norm_b, m_w_spatial, m_b_spatial, m_conv_w, m_conv_b, m_conv_norm_g, m_conv_norm_b, m_g_out_a, m_g_out_b, m_w_mix_out, m_g_pre_f2, m_g_post_f2, m_w_f2_in, m_w_f2_out, v_w_ada, v_b_ada, v_g_pre_f1, v_g_post_f1, v_w_f1_in, v_w_f1_out, v_g_pre_m, v_g_post_m, v_w_mix_in, v_gmlp_norm_g, v_gmlp_norm_b, v_w_spatial, v_b_spatial, v_conv_w, v_conv_b, v_conv_norm_g, v_conv_norm_b, v_g_out_a, v_g_out_b, v_w_mix_out, v_g_pre_f2, v_g_post_f2, v_w_f2_in, v_w_f2_out):
    env = dict(locals())
    wts = {n: env[n] for n in WEIGHTS}
    mom = {n: env["m_" + n] for n in WEIGHTS}
    var = {n: env["v_" + n] for n in WEIGHTS}
    nb, s, _ = x.shape
    t = nb * s
    ax, ay, ac = lax.axis_index("x"), lax.axis_index("y"), lax.axis_index("c")
    j_chip = 2 * ax + ay
    dev = 4 * ax + 2 * ay + ac
    j_arr = j_chip.reshape(1).astype(jnp.int32)

    groups = (("w_f1_in",), ("w_mix_in", "w_mix_out"), ("w_f2_in", "w_f2_out"), ("w_f1_out",))
    def gather_operands(gi):
        srcs = [wts[n][0].astype(BF16) for n in groups[gi]] + ([conv_w[0]] if gi == 1 else [])
        lands = [lax.dynamic_update_index_in_dim(lax.empty((NCHIP,) + a.shape, a.dtype), a, j_chip, 0) for a in srcs]
        return srcs, lands

    def gather_start(gi, behind, operands=None):
        srcs, lands = operands or gather_operands(gi)
        plan_a, plan_b, n_b = _gather_plans([a.shape for a in srcs])
        ssem, rsem, srcs, lands, token = _split_start("gw_start%d" % gi, srcs, lands, plan_a, 3 * len(srcs), behind)
        gather[gi] = (srcs, lands, ssem, rsem, plan_a, plan_b, n_b)
        return token

    def gather_forward(gi, behind):
        srcs, lands, ssem, rsem, plan_a, plan_b, n_b = gather[gi]
        ssem, rsem, lands, token = _split_forward("gw_fwd%d" % gi, srcs, lands, ssem, rsem, plan_a, plan_b, n_b, behind)
        gather[gi] = (lands, ssem, rsem, plan_b)
        return token

    def gathered(gi, behind):
        lands, ssem, rsem, plan_b = gather[gi]
        return _split_wait("gw_wait%d" % gi, [], lands, ssem, rsem, plan_b, behind)

    gather = {}
    (c_all8,) = _allgather8("gather_c", [c.reshape(8, (nb * D) // 8)])
    token = gather_start(0, c_all8)
    c_all = c_all8.reshape(NDEV * nb, D) + token[0, 0]
    b_sh = lax.dynamic_slice(b_ada, (0, j_chip * ADA_SH), (1, ADA_SH))
    ada_sh = _ada_fwd(c_all, w_ada[0], b_sh)
    later = [gather_operands(3), gather_operands(1), gather_operands(2)]
    (ada4,) = _chip_allgather("gather_ada", [ada_sh], behind=[a for pair in later for arrs in pair for a in arrs])
    token = gather_forward(0, ada4)
    token = gather_start(3, token, later[0])
    token = gather_start(1, token, later[1])
    token = gather_start(2, token, later[2])
    ada_me = lax.dynamic_slice(ada4, (0, dev * nb, 0), (NCHIP, nb, ADA_SH))
    ada_me = jnp.transpose(ada_me, (1, 0, 2)).reshape(nb, NMOD * D)
    sh1, sc1, gt1, sh2, sc2, gt2, sh3, sc3, gt3 = [ada_me[:, k * D:(k + 1) * D].reshape(nb, 1, D) for k in range(NMOD)]

    wcat = jnp.transpose(w_spatial[0], (1, 0, 2)).reshape(CH, NH * CH)
    wcat_t = jnp.transpose(w_spatial[0], (0, 2, 1)).reshape(NH * CH, CH)
    bspt = jnp.repeat(b_spatial[0].T, HD, axis=1)

    (w1i,) = gathered(0, token)
    p1 = _ffn_up(x, sh1, sc1, g_pre_f1, w1i)
    (w1o,) = gathered(3, gather_forward(3, p1))
    w1o = w1o.reshape(DFF, D)
    x1, f1 = _ffn_down(x, p1, gt1, g_post_f1, w1o)
    wmi, wmo, cw4 = gathered(1, gather_forward(1, x1))
    wmo = wmo.reshape(D, D)
    cw_full = jnp.transpose(cw4, (1, 0, 2)).reshape(CK, WB)
    cw_pad = jnp.pad(cw_full, ((0, HALO - CK), (0, 0)))
    u, v, a, g = _mix_in_fwd(x1, sh2, sc2, g_pre_m, wmi)
    token = gather_forward(2, u)
    x2, conv, yb, m = _mix_mid_fwd(x1, u, v, a, g, gt2 + token[0, 0], gmlp_norm_g, gmlp_norm_b, wcat, bspt, cw_pad, conv_b,
                                   conv_norm_g, conv_norm_b, g_out_a, g_out_b, wmo, g_post_m)
    w2i, w2o = gathered(2, [x2, token])
    w2o = w2o.reshape(DFF, D)
    dx3, df2, p2, lsum, dg_post_f2, dgt3 = _ffn_fwd(x2, sh3, sc3, gt3, g_pre_f2, g_post_f2, w2i, w2o, target=loss_target)

    def chip4(pair, rows):
        return [arr.reshape(NCHIP, rows, arr.shape[-1]) for arr in pair]

    def scatter_start(tag, pairs, behind):
        srcs = [p[1] for p in pairs]
        lands = [lax.empty((3,) + a.shape[1:], a.dtype) for a in srcs]
        ssem, rsem, srcs, lands, token = _split_start("gs_start_" + tag, srcs, lands, _scatter_plan(len(srcs)),
                                                      3 * len(srcs), behind)
        return (srcs, lands, ssem, rsem), token

    def scatter_wait(tag, state, behind):
        srcs, lands, ssem, rsem = state
        return _split_wait("gs_wait_" + tag, srcs, lands, ssem, rsem, _scatter_plan(len(srcs)), behind)

    def allgather_start(tag, arrs, behind):
        lands = [lax.dynamic_update_index_in_dim(lax.empty((NDEV,) + a.shape, a.dtype), a, dev, 0) for a in arrs]
        ssem, rsem, srcs, lands, token = _split_start("small_start_" + tag, arrs, lands, _allgather_plan(len(arrs)),
                                                      7 * len(arrs), behind)
        return (srcs, lands, ssem, rsem), token

    def allgather_wait(tag, state, behind):
        srcs, lands, ssem, rsem = state
        return _split_wait("small_wait_" + tag, srcs, lands, ssem, rsem, _allgather_plan(len(srcs)), behind)

    out = {}
    dx2, dp2, h3, a2, dg_pre_f2, dsh3, dsc3 = _ffn_bwd(
        dx3, x2, None, p2, sh3, sc3, gt3, g_pre_f2, g_post_f2, w2i, w2o, df=df2)
    gw2i = _wgrad("wgrad_f2_in", h3.reshape(t, D), dp2.reshape(t, 2 * DFF), 2 * DFF // NCHIP, True)
    gw2o = chip4(_wgrad("wgrad_f2_out", a2.reshape(t, DFF), df2.reshape(t, D), D // 2, False), DFF // NCHIP)
    scat_f2, tok = scatter_start("f2", [gw2i, gw2o], dg_post_f2)
    dy, dm, dg_post_m, dgt2 = _mix_out_bwd(dx2, m, gt2 + tok[0, 0], g_post_m, wmo)
    gwmo = chip4(_wgrad("wgrad_mix_out", yb.reshape(t, D), dm.reshape(t, D), D // 2, False), D // NCHIP)
    (du, dv, dconv, dwcat, dbsp, dgn_g, dgn_b, dgo_a, dgo_b, dcn_g, dcn_b, dcb) = _mix_mid_bwd(
        dy, u, v, conv, gmlp_norm_g, gmlp_norm_b, wcat, wcat_t, bspt, conv_norm_g, conv_norm_b, g_out_a, g_out_b)
    dx1, dproj, h2, dg_pre_m, dsh2, dsc2, dcw = _mix_in_bwd(dx2, x1, du, dv, dconv, a, g, sh2, sc2, g_pre_m, wmi, cw_pad)
    gwmi = _wgrad("wgrad_mix_in", h2.reshape(t, D), dproj.reshape(t, 4 * WA), WA, True)
    scat_mix, tok = scatter_start("mix", [gwmi, gwmo], dg_pre_m)

    vec_grads = dict(g_pre_m=dg_pre_m, g_post_m=dg_post_m, g_pre_f2=dg_pre_f2, g_post_f2=dg_post_f2)
    pair_grads = dict(gmlp_norm_g=dgn_g, gmlp_norm_b=dgn_b, conv_b=dcb, conv_norm_g=dcn_g, conv_norm_b=dcn_b,
                      g_out_a=dgo_a, g_out_b=dgo_b)
    pack = _pack_small([vec_grads[n] for n in VEC_ORDER[2:]], [pair_grads[n] for n in PAIR_ORDER], dbsp, dwcat, lsum)
    dada_early = jnp.concatenate([q.reshape(nb, D) for q in (dsh2, dsc2, dgt2, dsh3, dsc3, dgt3)], axis=1)
    early, tok2 = allgather_start("early", [pack, dcw, dada_early.reshape(8, (nb * 6 * D) // 8)], tok)
    grad_x, dp1, h1, a1, df1, dg_pre_f1, dg_post_f1, dsh1, dsc1, dgt1 = _ffn_bwd(
        dx1, x, f1, p1, sh1 + tok2[0, 0], sc1, gt1, g_pre_f1, g_post_f1, w1i, w1o)
    late_pack = _pack_late([dg_pre_f1, dg_post_f1] + [q.reshape(nb, D) for q in (dsh1, dsc1, dgt1)])
    late, tok2 = allgather_start("late", [late_pack], dg_post_f1)
    gw1i = _wgrad("wgrad_f1_in", h1.reshape(t, D), dp1.reshape(t, 2 * DFF), 2 * DFF // NCHIP, True)
    gw1o = chip4(_wgrad("wgrad_f1_out", a1.reshape(t, DFF), df1.reshape(t, D), D // 2, False), DFF // NCHIP)
    def d2d_start(tag, srcs, lands, plan, behind):
        ssem, rsem, srcs, lands, token = _split_start("d2d_start_" + tag, srcs, lands, plan, len(srcs), behind)
        return (srcs, lands, ssem, rsem, plan), token

    def d2d_wait(tag, state, behind):
        srcs, lands, ssem, rsem, plan = state
        return _split_wait("d2d_wait_" + tag, srcs, lands, ssem, rsem, plan, behind)

    def swap_start(tag, parts, behind):
        return d2d_start(tag, parts, [lax.empty(a.shape, a.dtype) for a in parts], _swap_plan(len(parts)), behind)

    def sums(names, pairs, recv):
        return [_sum4("sum4_" + n, pairs[k][0], recv[k], j_arr) for k, n in enumerate(names)]

    def update(names, part, other):
        for k, n in enumerate(names):
            out[n] = tuple(r[None] for r in _adam_big("adam_" + n, wts[n][0], mom[n][0], var[n][0], part[k], other[k]))

    c_arr = ac.reshape(1).astype(jnp.int32)
    halves = [gw1i[1], gw1o[1]]
    pair_st, tok = d2d_start("pair", halves, [lax.empty((a.shape[0], a.shape[1] // 2, a.shape[2]), a.dtype) for a in halves],
                             _pair_plan([a.shape for a in halves]), tok2)
    names_f2, names_mix, names_f1 = ("w_f2_in", "w_f2_out"), ("w_mix_in", "w_mix_out"), ("w_f1_in", "w_f1_out")
    part_f2 = sums(names_f2, [gw2i, gw2o], scatter_wait("f2", scat_f2, tok))
    sib = d2d_wait("pair", pair_st, part_f2)
    pair_i = _pair_sum("pairsum_f1_in", gw1i[0], sib[0], c_arr)
    pair_o = _pair_sum("pairsum_f1_out", gw1o[0], sib[1], c_arr)
    scat_f1, tok = scatter_start("f1", [pair_i, pair_o], tok2)
    swap_f2, tok = swap_start("swap_f2", part_f2, tok)
    part_mix = sums(names_mix, [gwmi, gwmo], scatter_wait("mix", scat_mix, tok))
    swap_mix, tok = swap_start("swap_mix", part_mix, part_mix[1])

    pack_all, dcw_all, dada_early8 = allgather_wait("early", early, tok)
    (late_all,) = allgather_wait("late", late, pack_all)
    dada_late = jnp.transpose(late_all[:, 2:8, :].reshape(NDEV, 3, nb, D), (0, 2, 1, 3)).reshape(NDEV * nb, 3 * D)
    dada_all = jnp.concatenate([dada_late, dada_early8.reshape(NDEV * nb, 6 * D)], axis=1)
    dada_sh = lax.dynamic_slice(dada_all, (0, j_chip * ADA_SH), (NDEV * nb, ADA_SH))
    out["w_ada"] = tuple(r[None] for r in _ada_bwd_adam(c_all, dada_sh, w_ada[0], m_w_ada[0], v_w_ada[0]))
    update(names_f2, part_f2, d2d_wait("swap_f2", swap_f2, out["w_ada"][3]))
    update(names_mix, part_mix, d2d_wait("swap_mix", swap_mix, out["w_f2_out"][3]))

    mine = sums(names_f1, [pair_i, pair_o], scatter_wait("f1", scat_f1, out["w_mix_out"][3]))
    swap_f1, tok = swap_start("swap_f1", mine, mine[1])
    dcw_mine = lax.dynamic_slice(dcw_all, (0, 0, j_chip * (WB // NCHIP)), (NDEV, HALO, WB // NCHIP))
    small = {n: (wts[n], mom[n], var[n]) for n in list(VEC_ORDER) + list(PAIR_ORDER) + ["b_spatial", "w_spatial", "conv_w", "b_ada"]}
    small_out, loss = _small_adam(pack_all, late_all, dcw_mine, dada_all, small, tok)
    out.update(small_out)
    theirs = d2d_wait("swap_f1", swap_f1, out["b_ada"][3])
    for k, n in enumerate(names_f1):
        out[n] = tuple(r[None] for r in _adam_halves("adam_" + n, wts[n][0], mom[n][0], var[n][0], mine[k], theirs[k],
                                                     c_arr))

    res = [loss, grad_x]
    for k in range(4):
        res += [out[n][k] for n in WEIGHTS]
    return tuple(res)
```

```python
import jax
import jax.numpy as jnp
from jax import lax
from jax.experimental import pallas as pl
from jax.experimental.pallas import tpu as pltpu

D = 1024
DFF = 2816
WA = 512
WB = 512
NH = 8
HD = 64
CH = 128
CK = 31
HALO = 32
NMOD = 9
EPS = 1e-6
NCHIP = 4
NDEV = 8
FBLK = DFF // 2
ADA_SH = NMOD * D // NCHIP

LR, B1, B2, EPS_A, WD, STEP = 0.001, 0.9, 0.999, 1e-08, 0.01, 10

F32 = jnp.float32
BF16 = jnp.bfloat16
MESH = pl.DeviceIdType.MESH
ANY = pl.BlockSpec(memory_space=pl.ANY)
VMEM_FULL = pl.BlockSpec(memory_space=pltpu.VMEM)
VMEM_LIMIT = 56 * 1024 * 1024

NT = (((1,), (1,)), ((), ()))
TN = (((0,), (0,)), ((), ()))


def _dot(a, b):
    return jnp.dot(a, b, preferred_element_type=F32)


def _dot_nt(a, b):
    return lax.dot_general(a, b, NT, preferred_element_type=F32)


def _dot_tn(a, b):
    return lax.dot_general(a, b, TN, preferred_element_type=F32)


def _cparams():
    return pltpu.CompilerParams(vmem_limit_bytes=VMEM_LIMIT)


def _allgather8(name, arrs):
    n = len(arrs)
    remote = _allgather_plan(n)

    def plan(x, y, c, ins, outs):
        _, sends = remote(x, y, c, ins, outs)
        return [(ins[a], outs[a].at[4 * x + 2 * y + c]) for a in range(n)], sends

    shapes = [jax.ShapeDtypeStruct((NDEV,) + a.shape, a.dtype) for a in arrs]
    return _run_exchange(name, arrs, shapes, plan, n, 7 * n)


def _chip_relations(x, y):
    return [(1 - x, y), (x, 1 - y), (1 - x, 1 - y)]


def _exchange(name, arrs, out_shapes, plan):
    n = len(arrs)
    n_out = len(out_shapes)

    def body(*refs):
        ins, outs = refs[:n], refs[n:n + n_out]
        send_sems, recv_sems, local_sems = refs[n + n_out:]
        x, y, c = lax.axis_index("x"), lax.axis_index("y"), lax.axis_index("c")
        local, sends = plan(x, y, c, ins, outs)
        locs = [pltpu.make_async_copy(s, d, local_sems.at[i]) for i, (s, d) in enumerate(local)]
        for loc in locs:
            loc.start()
        cps = [pltpu.make_async_remote_copy(src_ref=s, dst_ref=d, send_sem=send_sems.at[i], recv_sem=recv_sems.at[i],
                                            device_id=peer, device_id_type=MESH)
               for i, (s, d, peer, _) in enumerate(sends)]
        for cp in cps:
            cp.start()
        for i, (s, _, peer, landing) in enumerate(sends):
            pltpu.make_async_remote_copy(src_ref=s, dst_ref=landing, send_sem=send_sems.at[i], recv_sem=recv_sems.at[i],
                                         device_id=peer, device_id_type=MESH).wait_recv()
        for cp in cps:
            cp.wait_send()
        for loc in locs:
            loc.wait()

    return n, n_out, body


def _run_exchange(name, arrs, out_shapes, plan, n_local, n_send):
    n, n_out, body = _exchange(name, arrs, out_shapes, plan)
    return pl.pallas_call(
        body, name=name, out_shape=out_shapes,
        in_specs=[ANY] * n, out_specs=[ANY] * n_out,
        scratch_shapes=[pltpu.SemaphoreType.DMA((n_send,)), pltpu.SemaphoreType.DMA((n_send,)),
                        pltpu.SemaphoreType.DMA((max(n_local, 1),))],
    )(*arrs)


def _chip_allgather(name, arrs, behind=()):
    n = len(arrs)

    def plan(x, y, c, ins, outs):
        j_me = 2 * x + y
        local = [(ins[a], outs[a].at[j_me]) for a in range(n)]
        sends = []
        for a in range(n):
            for (px, py) in _chip_relations(x, y):
                sends.append((ins[a], outs[a].at[j_me], (px, py, c), outs[a].at[2 * px + py]))
        return local, sends

    shapes = [jax.ShapeDtypeStruct((NCHIP,) + a.shape, a.dtype) for a in arrs]
    return _run_exchange(name, list(arrs) + list(behind), shapes, plan, n, 3 * n)


HBM = pl.BlockSpec(memory_space=pltpu.HBM)
SEM = pl.BlockSpec(memory_space=pltpu.SEMAPHORE)
EFFECT = pltpu.SideEffectType.DATAFLOW_SIDE_EFFECTING


def _split_start(name, srcs, lands, plan, n_send, after):
    n, nl = len(srcs), len(lands)

    def body(*refs):
        src, land = refs[:n], refs[n:n + nl]
        send_sems, recv_sems = refs[n + nl + 1], refs[n + nl + 2]
        token = refs[-2]
        local_sems = refs[-1]
        x, y, c = lax.axis_index("x"), lax.axis_index("y"), lax.axis_index("c")
        local, sends = plan(x, y, c, src, land)
        locs = [pltpu.make_async_copy(s, d, local_sems.at[i]) for i, (s, d) in enumerate(local)]
        for loc in locs:
            loc.start()
        for loc in locs:
            loc.wait()
        for i, (s, d, peer, _) in enumerate(sends):
            pltpu.make_async_remote_copy(src_ref=s, dst_ref=d, send_sem=send_sems.at[i], recv_sem=recv_sems.at[i],
                                         device_id=peer, device_id_type=MESH).start()
        token[...] = jnp.zeros_like(token)

    thru = [pltpu.HBM(a.shape, a.dtype) for a in lands]
    srcs = [pltpu.with_memory_space_constraint(a, pltpu.HBM) for a in srcs]
    res = pl.pallas_call(
        body, name=name,
        out_shape=(pltpu.SemaphoreType.DMA((n_send,)), pltpu.SemaphoreType.DMA((n_send,)), *thru,
                   jax.ShapeDtypeStruct((8, 128), F32)),
        in_specs=[HBM] * (n + nl) + [ANY],
        out_specs=(SEM, SEM, *([HBM] * nl), pl.BlockSpec(memory_space=pltpu.VMEM)),
        input_output_aliases={n + i: 2 + i for i in range(nl)},
        scratch_shapes=[pltpu.SemaphoreType.DMA((max(n, 1),))],
        compiler_params=pltpu.CompilerParams(has_side_effects=EFFECT),
    )(*srcs, *[pltpu.with_memory_space_constraint(a, pltpu.HBM) for a in lands], after)
    return res[0], res[1], srcs, list(res[2:2 + nl]), res[-1]


def _split_wait(name, srcs, lands, send_sems, recv_sems, plan, after):
    n, nl = len(srcs), len(lands)
    afters = list(after) if isinstance(after, (list, tuple)) else [after]

    def body(*refs):
        src, land = refs[:n], refs[n:n + nl]
        send_sems, recv_sems = refs[n + nl], refs[n + nl + 1]
        x, y, c = lax.axis_index("x"), lax.axis_index("y"), lax.axis_index("c")
        _, sends = plan(x, y, c, src, land)
        for i, (s, _, peer, landing) in enumerate(sends):
            cp = pltpu.make_async_remote_copy(src_ref=s, dst_ref=landing, send_sem=send_sems.at[i],
                                              recv_sem=recv_sems.at[i], device_id=peer, device_id_type=MESH)
            cp.wait_send()
            cp.wait_recv()

    thru = [pltpu.HBM(a.shape, a.dtype) for a in lands]
    res = pl.pallas_call(
        body, name=name, out_shape=tuple(thru),
        in_specs=[HBM] * (n + nl) + [SEM, SEM] + [ANY] * len(afters), out_specs=tuple([HBM] * nl),
        input_output_aliases={n + i: i for i in range(nl)},
        compiler_params=pltpu.CompilerParams(has_side_effects=EFFECT),
    )(*srcs, *lands, send_sems, recv_sems, *afters)
    return list(res)


def _split_forward(name, srcs, lands, send_a, recv_a, plan_a, plan_b, n_b, after):
    n, nl = len(srcs), len(lands)

    def body(*refs):
        src, land = refs[:n], refs[n:n + nl]
        send_a, recv_a = refs[n + nl], refs[n + nl + 1]
        send_b, recv_b = refs[n + nl + 3], refs[n + nl + 4]
        token = refs[-1]
        x, y, c = lax.axis_index("x"), lax.axis_index("y"), lax.axis_index("c")
        _, first = plan_a(x, y, c, src, land)
        for i, (s, _, peer, landing) in enumerate(first):
            cp = pltpu.make_async_remote_copy(src_ref=s, dst_ref=landing, send_sem=send_a.at[i],
                                              recv_sem=recv_a.at[i], device_id=peer, device_id_type=MESH)
            cp.wait_send()
            cp.wait_recv()
        _, second = plan_b(x, y, c, src, land)
        for i, (s, d, peer, _) in enumerate(second):
            pltpu.make_async_remote_copy(src_ref=s, dst_ref=d, send_sem=send_b.at[i], recv_sem=recv_b.at[i],
                                         device_id=peer, device_id_type=MESH).start()
        token[...] = jnp.zeros_like(token)

    thru = [pltpu.HBM(a.shape, a.dtype) for a in lands]
    res = pl.pallas_call(
        body, name=name,
        out_shape=(pltpu.SemaphoreType.DMA((n_b,)), pltpu.SemaphoreType.DMA((n_b,)), *thru,
                   jax.ShapeDtypeStruct((8, 128), F32)),
        in_specs=[HBM] * (n + nl) + [SEM, SEM, ANY],
        out_specs=(SEM, SEM, *([HBM] * nl), pl.BlockSpec(memory_space=pltpu.VMEM)),
        input_output_aliases={n + i: 2 + i for i in range(nl)},
        compiler_params=pltpu.CompilerParams(has_side_effects=EFFECT),
    )(*srcs, *lands, send_a, recv_a, after)
    return res[0], res[1], list(res[2:2 + nl]), res[-1]


def _gather_plans(shapes):
    n = len(shapes)

    def halves(a, c):
        rows = shapes[a][0] // 2
        return pl.ds(pl.multiple_of(c * rows, 16), rows), pl.ds(pl.multiple_of((1 - c) * rows, 16), rows)

    def split(a):
        return shapes[a][0] % 32 == 0

    def plan_a(x, y, c, src, land):
        j_me = 2 * x + y
        sends = []
        for a in range(n):
            for (px, py) in _chip_relations(x, y):
                if split(a):
                    mine, _ = halves(a, c)
                    sends.append((src[a].at[mine], land[a].at[j_me, mine], (px, py, c), land[a].at[2 * px + py, mine]))
                else:
                    sends.append((src[a], land[a].at[j_me], (px, py, c), land[a].at[2 * px + py]))
        return [], sends

    def plan_b(x, y, c, src, land):
        sends = []
        for a in range(n):
            if split(a):
                mine, other = halves(a, c)
                for (px, py) in _chip_relations(x, y):
                    j = 2 * px + py
                    sends.append((land[a].at[j, mine], land[a].at[j, mine], (x, y, 1 - c), land[a].at[j, other]))
        return [], sends

    n_b = 3 * sum(1 for a in range(n) if split(a))
    return plan_a, plan_b, n_b


def _allgather_plan(n):
    flips = [(dx, dy, dc) for dx in (0, 1) for dy in (0, 1) for dc in (0, 1) if dx or dy or dc]

    def plan(x, y, c, src, land):
        sends = []
        for a in range(n):
            for dx, dy, dc in flips:
                px, py, pc = x ^ dx, y ^ dy, c ^ dc
                sends.append((src[a], land[a].at[4 * x + 2 * y + c], (px, py, pc), land[a].at[4 * px + 2 * py + pc]))
        return [], sends

    return plan


def _scatter_plan(n):
    def plan(x, y, c, src, land):
        sends = []
        for a in range(n):
            for k, (px, py) in enumerate(_chip_relations(x, y)):
                sends.append((src[a].at[2 * px + py], land[a].at[k], (px, py, c), land[a].at[k]))
        return [], sends

    return plan


def _rms(x):
    r = lax.rsqrt(jnp.mean(x * x, axis=-1, keepdims=True) + EPS)
    return x * r, r


def _rms_bwd(dy, n, r, g):
    dg = jnp.sum(dy * n, axis=0, keepdims=True)
    dn = dy * g
    dx = r * (dn - n * jnp.mean(dn * n, axis=-1, keepdims=True))
    return dx, dg


def _ln(x):
    mu = jnp.mean(x, axis=-1, keepdims=True)
    xc = x - mu
    rstd = lax.rsqrt(jnp.mean(xc * xc, axis=-1, keepdims=True) + EPS)
    return xc * rstd, rstd


def _ln_bwd(dy, xhat, rstd, g):
    dg = jnp.sum(dy * xhat, axis=0, keepdims=True)
    db = jnp.sum(dy, axis=0, keepdims=True)
    dxh = dy * g
    dx = rstd * (dxh - jnp.mean(dxh, axis=-1, keepdims=True) - xhat * jnp.mean(dxh * xhat, axis=-1, keepdims=True))
    return dx, dg, db


def _sigmoid(x):
    return jax.nn.sigmoid(x)


def _dsilu(x, s):
    return s * (1.0 + x * (1.0 - s))


def _adam(w, g, m, v):
    m = B1 * m + (1.0 - B1) * g
    v = B2 * v + (1.0 - B2) * (g * g)
    m_hat = m / (1.0 - B1 ** STEP)
    v_hat = v / (1.0 - B2 ** STEP)
    delta = -LR * (m_hat / (jnp.sqrt(v_hat) + EPS_A) + WD * w)
    return delta, m, v


def _head_mask(shape):
    lane = lax.broadcasted_iota(jnp.int32, shape, len(shape) - 1)
    return [(lane >= h * HD) & (lane < (h + 1) * HD) for h in range(NH)]


def _first(b, i):
    return jnp.logical_and(b == 0, i == 0)


def _acc(ref, val, first):
    @pl.when(first)
    def _():
        ref[...] = val

    @pl.when(jnp.logical_not(first))
    def _():
        ref[...] += val


def _ada_fwd(c_all, w_sh, b_sh):
    nb = c_all.shape[0]
    tn = 768

    def body(c_ref, w_ref, b_ref, o_ref):
        cv = c_ref[...]
        cs = (cv * _sigmoid(cv)).astype(BF16)
        o_ref[...] = _dot(cs, w_ref[...].astype(BF16)) + b_ref[...]

    return pl.pallas_call(
        body, name="ada_fwd", grid=(ADA_SH // tn,),
        out_shape=jax.ShapeDtypeStruct((nb, ADA_SH), F32),
        in_specs=[pl.BlockSpec((nb, D), lambda j: (0, 0)), pl.BlockSpec((D, tn), lambda j: (0, j)),
                  pl.BlockSpec((1, tn), lambda j: (0, j))],
        out_specs=pl.BlockSpec((nb, tn), lambda j: (0, j)),
        compiler_params=_cparams(),
    )(c_all, w_sh, b_sh)


def _ada_bwd_adam(c_all, dada_sh, w, m, v):
    nb = c_all.shape[0]
    tn = 768

    def body(c_ref, d_ref, w_ref, m_ref, v_ref, g_out, d_out, m_out, v_out):
        cv = c_ref[...]
        cs = (cv * _sigmoid(cv)).astype(BF16)
        g = _dot_tn(cs, d_ref[...].astype(BF16))
        delta, m2, v2 = _adam(w_ref[...], g, m_ref[...], v_ref[...])
        g_out[...] = g
        d_out[...] = delta
        m_out[...] = m2
        v_out[...] = v2

    big = pl.BlockSpec((D, tn), lambda j: (0, j))
    shape = jax.ShapeDtypeStruct((D, ADA_SH), F32)
    return pl.pallas_call(
        body, name="ada_bwd_adam", grid=(ADA_SH // tn,),
        out_shape=[shape] * 4,
        in_specs=[pl.BlockSpec((nb, D), lambda j: (0, 0)), pl.BlockSpec((nb, tn), lambda j: (0, j)), big, big, big],
        out_specs=[big] * 4,
        compiler_params=_cparams(),
    )(c_all, dada_sh, w, m, v)


def _tok_specs(tm, width):
    return pl.BlockSpec((1, tm, width), lambda b, i: (b, i, 0))


def _mod_spec():
    return pl.BlockSpec((1, 1, D), lambda b, i: (b, 0, 0))


def _row_spec(width=D):
    return pl.BlockSpec((1, width), lambda b, i: (0, 0))


def _ffn_fwd(x, sh, sc, gt, g_pre, g_post, w_in4, w_out, target=None):
    nb, s, _ = x.shape
    tm = min(512, s)
    with_loss = target is not None

    def body(*refs):
        if with_loss:
            (x_ref, sh_ref, sc_ref, gt_ref, gpre_ref, gpost_ref, win_ref, wout_ref, tgt_ref,
             xo_ref, df_ref, p_ref, ls_ref, dgpost_ref, dgt_ref) = refs
        else:
            (x_ref, sh_ref, sc_ref, gt_ref, gpre_ref, gpost_ref, win_ref, wout_ref,
             xo_ref, f_ref, p_ref) = refs
        xv = x_ref[0]
        n, _ = _rms(xv)
        h = (n * gpre_ref[...]) * (1.0 + sc_ref[0]) + sh_ref[0]
        hb = h.astype(BF16)
        acc = jnp.zeros((tm, D), F32)
        for j in range(2):
            gate = _dot(hb, win_ref[j])
            up = _dot(hb, win_ref[2 + j])
            p_ref[0, :, j * FBLK:(j + 1) * FBLK] = gate.astype(BF16)
            p_ref[0, :, DFF + j * FBLK:DFF + (j + 1) * FBLK] = up.astype(BF16)
            a = (gate * _sigmoid(gate)) * up
            acc = acc + _dot(a.astype(BF16), wout_ref[j * FBLK:(j + 1) * FBLK, :])
        nf, q = _rms(acc)
        gpost = gpost_ref[...]
        half_gate = 0.5 * gt_ref[0]
        out = xv + half_gate * (nf * gpost)
        if with_loss:
            first = _first(pl.program_id(0), pl.program_id(1))
            err = out - tgt_ref[0]
            dout = err * (1.0 / D)
            xo_ref[0] = dout
            row = jnp.sum(err * err, axis=0, keepdims=True)
            part = row[:, 0:128]
            for k in range(1, D // 128):
                part = part + row[:, k * 128:(k + 1) * 128]
            _acc(ls_ref, part, first)
            df, dgpost = _rms_bwd(dout * half_gate, nf, q, gpost)
            df_ref[0] = df.astype(BF16)
            _acc(dgpost_ref, dgpost, first)
            _acc(dgt_ref, jnp.sum(dout * (0.5 * (nf * gpost)), axis=0, keepdims=True)[None], pl.program_id(1) == 0)
        else:
            f_ref[0] = acc
            xo_ref[0] = out

    in_specs = [_tok_specs(tm, D), _mod_spec(), _mod_spec(), _mod_spec(), _row_spec(), _row_spec(), VMEM_FULL, VMEM_FULL]
    args = [x, sh, sc, gt, g_pre, g_post, w_in4, w_out]
    out_shape = [jax.ShapeDtypeStruct((nb, s, D), F32), jax.ShapeDtypeStruct((nb, s, D), BF16 if with_loss else F32),
                 jax.ShapeDtypeStruct((nb, s, 2 * DFF), BF16)]
    out_specs = [_tok_specs(tm, D), _tok_specs(tm, D), _tok_specs(tm, 2 * DFF)]
    if with_loss:
        in_specs.append(_tok_specs(tm, D))
        args.append(target)
        out_shape += [jax.ShapeDtypeStruct((1, 128), F32), jax.ShapeDtypeStruct((1, D), F32),
                      jax.ShapeDtypeStruct((nb, 1, D), F32)]
        out_specs += [pl.BlockSpec((1, 128), lambda b, i: (0, 0)), _row_spec(), _mod_spec()]
    return pl.pallas_call(
        body, name="ffn_loss_fwd" if with_loss else "ffn_fwd", grid=(nb, s // tm),
        out_shape=out_shape, in_specs=in_specs, out_specs=out_specs,
        compiler_params=_cparams(),
    )(*args)


def _ffn_up(x, sh, sc, g_pre, w_in4):
    nb, s, _ = x.shape
    tm = min(512, s)

    def body(x_ref, sh_ref, sc_ref, gpre_ref, win_ref, p_ref, a_ref):
        n, _ = _rms(x_ref[0])
        hb = ((n * gpre_ref[...]) * (1.0 + sc_ref[0]) + sh_ref[0]).astype(BF16)
        for j in range(2):
            gate = _dot(hb, win_ref[j])
            up = _dot(hb, win_ref[2 + j])
            p_ref[0, :, j * FBLK:(j + 1) * FBLK] = gate.astype(BF16)
            p_ref[0, :, DFF + j * FBLK:DFF + (j + 1) * FBLK] = up.astype(BF16)
            a_ref[0, :, j * FBLK:(j + 1) * FBLK] = ((gate * _sigmoid(gate)) * up).astype(BF16)

    return pl.pallas_call(
        body, name="ffn_up", grid=(nb, s // tm),
        out_shape=[jax.ShapeDtypeStruct((nb, s, 2 * DFF), BF16), jax.ShapeDtypeStruct((nb, s, DFF), BF16)],
        in_specs=[_tok_specs(tm, D), _mod_spec(), _mod_spec(), _row_spec(), VMEM_FULL],
        out_specs=[_tok_specs(tm, 2 * DFF), _tok_specs(tm, DFF)],
        compiler_params=_cparams(),
    )(x, sh, sc, g_pre, w_in4)


def _ffn_down(x, a, gt, g_post, w_out):
    nb, s, _ = x.shape
    tm = min(512, s)

    def body(x_ref, a_ref, gt_ref, gpost_ref, wout_ref, xo_ref, f_ref):
        acc = _dot(a_ref[0], wout_ref[...])
        f_ref[0] = acc
        nf, _ = _rms(acc)
        xo_ref[0] = x_ref[0] + (0.5 * gt_ref[0]) * (nf * gpost_ref[...])

    tok = _tok_specs(tm, D)
    shape = jax.ShapeDtypeStruct((nb, s, D), F32)
    return pl.pallas_call(
        body, name="ffn_down", grid=(nb, s // tm), out_shape=[shape, shape],
        in_specs=[tok, _tok_specs(tm, DFF), _mod_spec(), _row_spec(), VMEM_FULL],
        out_specs=[tok, tok],
        compiler_params=_cparams(),
    )(x, a, gt, g_post, w_out)


def _ffn_bwd(dxo, x, f, p, sh, sc, gt, g_pre, g_post, w_in4, w_out, df=None):
    nb, s, _ = x.shape
    tm = min(256, s)
    given = df is not None

    def body(*refs):
        if given:
            (dxo_ref, x_ref, dfin_ref, p_ref, sh_ref, sc_ref, gpre_ref, win_ref, wout_ref,
             dx_ref, dp_ref, h_ref, a_ref, dgpre_ref, dsh_ref, dsc_ref) = refs
        else:
            (dxo_ref, x_ref, f_ref, p_ref, sh_ref, sc_ref, gt_ref, gpre_ref, gpost_ref, win_ref, wout_ref,
             dx_ref, dp_ref, h_ref, a_ref, df_ref, dgpre_ref, dgpost_ref, dsh_ref, dsc_ref, dgt_ref) = refs
        b, i = pl.program_id(0), pl.program_id(1)
        dxo_v = dxo_ref[0]
        if given:
            dfb = dfin_ref[0]
        else:
            nf, q = _rms(f_ref[0])
            gpost = gpost_ref[...]
            dgt = jnp.sum(dxo_v * (0.5 * (nf * gpost)), axis=0, keepdims=True)
            do = dxo_v * (0.5 * gt_ref[0])
            dfv, dgpost = _rms_bwd(do, nf, q, gpost)
            dfb = dfv.astype(BF16)
            df_ref[0] = dfb
        xv = x_ref[0]
        n, r = _rms(xv)
        gpre = gpre_ref[...]
        ng = n * gpre
        scale1 = 1.0 + sc_ref[0]
        h = ng * scale1 + sh_ref[0]
        h_ref[0] = h.astype(BF16)
        dh = jnp.zeros((tm, D), F32)
        for j in range(2):
            gate = p_ref[0, :, j * FBLK:(j + 1) * FBLK].astype(F32)
            up = p_ref[0, :, DFF + j * FBLK:DFF + (j + 1) * FBLK].astype(F32)
            sg = _sigmoid(gate)
            act = gate * sg
            a_ref[0, :, j * FBLK:(j + 1) * FBLK] = (act * up).astype(BF16)
            da = _dot_nt(dfb, wout_ref[j * FBLK:(j + 1) * FBLK, :])
            dgate = (da * up * _dsilu(gate, sg)).astype(BF16)
            dup = (da * act).astype(BF16)
            dp_ref[0, :, j * FBLK:(j + 1) * FBLK] = dgate
            dp_ref[0, :, DFF + j * FBLK:DFF + (j + 1) * FBLK] = dup
            dh = dh + _dot_nt(dgate, win_ref[j]) + _dot_nt(dup, win_ref[2 + j])
        dsh = jnp.sum(dh, axis=0, keepdims=True)
        dsc = jnp.sum(dh * ng, axis=0, keepdims=True)
        dxn, dgpre = _rms_bwd(dh * scale1, n, r, gpre)
        dx_ref[0] = dxo_v + dxn
        _acc(dgpre_ref, dgpre, _first(b, i))
        _acc(dsh_ref, dsh[None], i == 0)
        _acc(dsc_ref, dsc[None], i == 0)
        if not given:
            _acc(dgpost_ref, dgpost, _first(b, i))
            _acc(dgt_ref, dgt[None], i == 0)

    tok = _tok_specs(tm, D)
    mod_shape = jax.ShapeDtypeStruct((nb, 1, D), F32)
    row_shape = jax.ShapeDtypeStruct((1, D), F32)
    big = [jax.ShapeDtypeStruct((nb, s, D), F32), jax.ShapeDtypeStruct((nb, s, 2 * DFF), BF16),
           jax.ShapeDtypeStruct((nb, s, D), BF16), jax.ShapeDtypeStruct((nb, s, DFF), BF16)]
    big_specs = [tok, _tok_specs(tm, 2 * DFF), tok, _tok_specs(tm, DFF)]
    if given:
        return pl.pallas_call(
            body, name="ffn_bwd_after_loss", grid=(nb, s // tm),
            out_shape=big + [row_shape, mod_shape, mod_shape],
            in_specs=[tok, tok, tok, _tok_specs(tm, 2 * DFF), _mod_spec(), _mod_spec(), _row_spec(), VMEM_FULL, VMEM_FULL],
            out_specs=big_specs + [_row_spec(), _mod_spec(), _mod_spec()],
            compiler_params=_cparams(),
        )(dxo, x, df, p, sh, sc, g_pre, w_in4, w_out)
    return pl.pallas_call(
        body, name="ffn_bwd", grid=(nb, s // tm),
        out_shape=big + [jax.ShapeDtypeStruct((nb, s, D), BF16), row_shape, row_shape, mod_shape, mod_shape, mod_shape],
        in_specs=[tok, tok, tok, _tok_specs(tm, 2 * DFF), _mod_spec(), _mod_spec(), _mod_spec(), _row_spec(), _row_spec(),
                  VMEM_FULL, VMEM_FULL],
        out_specs=big_specs + [tok, _row_spec(), _row_spec(), _mod_spec(), _mod_spec(), _mod_spec()],
        compiler_params=_cparams(),
    )(dxo, x, f, p, sh, sc, gt, g_pre, g_post, w_in4, w_out)


def _wgrad(name, a, b, col_block, chip_major):
    t, ka = a.shape
    n = b.shape[1]
    tk = min(t, 512)
    while tk * 2 <= t and t % (tk * 2) == 0 and 2 * (tk * 2) * max(ka, col_block) <= 6 * 1024 * 1024:
        tk *= 2
    nk = t // tk
    nblk = n // col_block

    def body(a_ref, b_ref, o_ref, obf_ref, acc_ref):
        k = pl.program_id(1)

        @pl.when(k == 0)
        def _():
            acc_ref[...] = jnp.zeros_like(acc_ref)

        acc_ref[...] += _dot_tn(a_ref[...], b_ref[...])

        @pl.when(k == nk - 1)
        def _():
            val = acc_ref[...]
            if chip_major:
                o_ref[0] = val
                obf_ref[0] = val.astype(BF16)
            else:
                o_ref[...] = val
                obf_ref[...] = val.astype(BF16)

    if chip_major:
        shape = (nblk, ka, col_block)
        ospec = pl.BlockSpec((1, ka, col_block), lambda j, k: (j, 0, 0))
    else:
        shape = (ka, n)
        ospec = pl.BlockSpec((ka, col_block), lambda j, k: (0, j))
    return pl.pallas_call(
        body, name=name, grid=(nblk, nk),
        out_shape=[jax.ShapeDtypeStruct(shape, F32), jax.ShapeDtypeStruct(shape, BF16)],
        in_specs=[pl.BlockSpec((tk, ka), lambda j, k: (k, 0)), pl.BlockSpec((tk, col_block), lambda j, k: (k, j))],
        out_specs=[ospec, ospec],
        scratch_shapes=[pltpu.VMEM((ka, col_block), F32)],
        compiler_params=_cparams(),
    )(a, b)


def _mix_in_fwd(x, sh, sc, g_pre, w_mi4):
    nb, s, _ = x.shape
    tm = min(512, s)

    def body(x_ref, sh_ref, sc_ref, gpre_ref, w_ref, u_ref, v_ref, a_ref, g_ref):
        n, _ = _rms(x_ref[0])
        hb = ((n * gpre_ref[...]) * (1.0 + sc_ref[0]) + sh_ref[0]).astype(BF16)
        for k, o_ref in enumerate((u_ref, v_ref, a_ref, g_ref)):
            o_ref[0] = _dot(hb, w_ref[k])

    shape = jax.ShapeDtypeStruct((nb, s, WA), F32)
    return pl.pallas_call(
        body, name="mix_in_fwd", grid=(nb, s // tm),
        out_shape=[shape] * 4,
        in_specs=[_tok_specs(tm, D), _mod_spec(), _mod_spec(), _row_spec(), VMEM_FULL],
        out_specs=[_tok_specs(tm, WA)] * 4,
        compiler_params=_cparams(),
    )(x, sh, sc, g_pre, w_mi4)


def _spatial_weights(wcat_ref, transposed):
    w = wcat_ref[...]
    row = lax.broadcasted_iota(jnp.int32, w.shape, 0)
    col = lax.broadcasted_iota(jnp.int32, w.shape, 1)
    keep = ((row & (CH - 1)) <= col) if transposed else ((col & (CH - 1)) <= row)
    return jnp.where(keep, w, 0.0).astype(BF16)


def _expand_heads(vc, masks):
    return jnp.concatenate([jnp.where(mk, vc, jnp.zeros_like(vc)) for mk in masks], axis=0)


def _spatial_bias(bspt_ref):
    return bspt_ref[...]


SHIFTS = 8
TAP_ROWS = 32


def _ext_rows(tm):
    return tm + HALO + SHIFTS


def _make_shifts(ext_ref, sh_ref, tm):
    ext_ref[tm + HALO:tm + HALO + SHIFTS, :] = jnp.zeros((SHIFTS, WB), F32)
    for r in range(SHIFTS):
        sh_ref[r] = ext_ref[r:r + tm + HALO, :]


def _conv_taps(sh_ref, w_ref, tm, taps, emit):
    def block(i, carry):
        r0 = pl.multiple_of(i * TAP_ROWS, TAP_ROWS)
        acc = jnp.zeros((TAP_ROWS, WB), F32)
        for o, k in taps:
            acc = acc + w_ref[k:k + 1, :] * sh_ref[o % SHIFTS, pl.ds(r0 + SHIFTS * (o // SHIFTS), TAP_ROWS), :]
        emit(r0, acc)
        return carry

    lax.fori_loop(0, tm // TAP_ROWS, block, 0)


def _halo_prev_spec(tm):
    return pl.BlockSpec((1, HALO, WB), lambda b, i: (b, jnp.maximum(i * (tm // HALO) - 1, 0), 0))


def _halo_next_spec(tm, s):
    return pl.BlockSpec((1, HALO, WB), lambda b, i: (b, jnp.minimum((i + 1) * (tm // HALO), s // HALO - 1), 0))


def _mix_mid_fwd(x, u, v, a, g, gt, gn_g, gn_b, wcat, bspt, conv_w, conv_b, cn_g, cn_b, go_a, go_b, w_mo, g_post):
    nb, s, _ = x.shape
    tm = min(512, s)

    def body(x_ref, u_ref, v_ref, a_ref, g_ref, ah_ref, gh_ref, gt_ref, gng_ref, gnb_ref, wcat_ref, bspt_ref,
             cw_ref, cb_ref, cng_ref, cnb_ref, goa_ref, gob_ref, wmo_ref, gpost_ref,
             xo_ref, conv_ref, y_ref, m_ref, ext_ref, sh_ref):
        i = pl.program_id(1)
        xhat, _ = _ln(v_ref[0])
        vb = (xhat * gng_ref[...] + gnb_ref[...]).astype(BF16)
        wsb = _spatial_weights(wcat_ref, False)
        bias = _spatial_bias(bspt_ref)
        masks = _head_mask((CH, WA))
        zs = []
        for cidx in range(tm // CH):
            vexp = _expand_heads(vb[cidx * CH:(cidx + 1) * CH, :], masks)
            zs.append(_dot(wsb, vexp) + bias)
        z = jnp.concatenate(zs, axis=0)
        na, _ = _rms(u_ref[0] * z)
        keep = jnp.where(i == 0, 0.0, 1.0).astype(F32)
        ext_ref[0:HALO, :] = (ah_ref[0] * _sigmoid(gh_ref[0])) * keep
        ext_ref[HALO:HALO + tm, :] = a_ref[0] * _sigmoid(g_ref[0])
        _make_shifts(ext_ref, sh_ref, tm)
        cb = cb_ref[...]

        def put_conv(r0, acc):
            conv_ref[0, pl.ds(r0, TAP_ROWS), :] = acc + cb

        _conv_taps(sh_ref, cw_ref, tm, [(k + HALO - (CK - 1), k) for k in range(CK)], put_conv)
        conv = conv_ref[0]
        chat, _ = _ln(conv)
        cln = chat * cng_ref[...] + cnb_ref[...]
        nbb, _ = _rms(cln * _sigmoid(cln))
        yb = jnp.concatenate([na * goa_ref[...], nbb * gob_ref[...]], axis=1).astype(BF16)
        y_ref[0] = yb
        m = _dot(yb, wmo_ref[...])
        m_ref[0] = m
        nm, _ = _rms(m)
        xo_ref[0] = x_ref[0] + gt_ref[0] * (nm * gpost_ref[...])

    t5 = _tok_specs(tm, WA)
    tok = _tok_specs(tm, D)
    r5 = _row_spec(WA)
    full = lambda shape: pl.BlockSpec(shape, lambda b, i: (0,) * len(shape))
    return pl.pallas_call(
        body, name="mix_mid_fwd", grid=(nb, s // tm),
        out_shape=[jax.ShapeDtypeStruct((nb, s, D), F32), jax.ShapeDtypeStruct((nb, s, WB), F32),
                   jax.ShapeDtypeStruct((nb, s, D), BF16), jax.ShapeDtypeStruct((nb, s, D), F32)],
        in_specs=[tok, t5, t5, t5, t5, _halo_prev_spec(tm), _halo_prev_spec(tm), _mod_spec(), r5, r5,
                  full((CH, NH * CH)), full((CH, WA)), full((HALO, WB)), r5, r5, r5, r5, r5, VMEM_FULL, _row_spec()],
        out_specs=[tok, t5, tok, tok],
        scratch_shapes=[pltpu.VMEM((_ext_rows(tm), WB), F32), pltpu.VMEM((SHIFTS, tm + HALO, WB), F32)],
        compiler_params=_cparams(),
    )(x, u, v, a, g, a, g, gt, gn_g, gn_b, wcat, bspt, conv_w, conv_b, cn_g, cn_b, go_a, go_b, w_mo, g_post)


def _mix_out_bwd(dxo, m, gt, g_post, w_mo):
    nb, s, _ = m.shape
    tm = min(512, s)

    def body(dxo_ref, m_ref, gt_ref, gpost_ref, wmo_ref, dy_ref, dm_ref, dgpost_ref, dgt_ref):
        b, i = pl.program_id(0), pl.program_id(1)
        dxo_v = dxo_ref[0]
        nm, q = _rms(m_ref[0])
        gpost = gpost_ref[...]
        dgt = jnp.sum(dxo_v * (nm * gpost), axis=0, keepdims=True)
        dm, dgpost = _rms_bwd(dxo_v * gt_ref[0], nm, q, gpost)
        dmb = dm.astype(BF16)
        dm_ref[0] = dmb
        dy_ref[0] = _dot_nt(dmb, wmo_ref[...])
        _acc(dgpost_ref, dgpost, _first(b, i))
        _acc(dgt_ref, dgt[None], i == 0)

    tok = _tok_specs(tm, D)
    return pl.pallas_call(
        body, name="mix_out_bwd", grid=(nb, s // tm),
        out_shape=[jax.ShapeDtypeStruct((nb, s, D), F32), jax.ShapeDtypeStruct((nb, s, D), BF16),
                   jax.ShapeDtypeStruct((1, D), F32), jax.ShapeDtypeStruct((nb, 1, D), F32)],
        in_specs=[tok, tok, _mod_spec(), _row_spec(), VMEM_FULL],
        out_specs=[tok, tok, _row_spec(), _mod_spec()],
        compiler_params=_cparams(),
    )(dxo, m, gt, g_post, w_mo)


def _mix_mid_bwd(dy, u, v, conv, gn_g, gn_b, wcat, wcat_t, bspt, cn_g, cn_b, go_a, go_b):
    nb, s, _ = dy.shape
    tm = min(512, s)
    nchunk = tm // CH

    def body(dy_ref, u_ref, v_ref, conv_ref, gng_ref, gnb_ref, wcat_ref, wcatt_ref, bspt_ref, cng_ref, cnb_ref,
             goa_ref, gob_ref,
             du_ref, dv_ref, dconv_ref, dwcat_ref, dbsp_ref, dgng_ref, dgnb_ref, dgoa_ref, dgob_ref,
             dcng_ref, dcnb_ref, dcb_ref):
        first = _first(pl.program_id(0), pl.program_id(1))
        dyv = dy_ref[0]
        xhat, rstd = _ln(v_ref[0])
        gng = gng_ref[...]
        vb = (xhat * gng + gnb_ref[...]).astype(BF16)
        wsb = _spatial_weights(wcat_ref, False)
        wsb_t = _spatial_weights(wcatt_ref, True)
        bias = _spatial_bias(bspt_ref)
        masks = _head_mask((CH, WA))
        vexps, zs = [], []
        for cidx in range(nchunk):
            vexp = _expand_heads(vb[cidx * CH:(cidx + 1) * CH, :], masks)
            vexps.append(vexp)
            zs.append(_dot(wsb, vexp) + bias)
        z = jnp.concatenate(zs, axis=0)
        uv = u_ref[0]
        na, ra = _rms(uv * z)
        dya, dgoa = _rms_bwd(dyv[:, 0:WA], na, ra, goa_ref[...])
        du_ref[0] = dya * z
        dz = dya * uv
        dwcat = jnp.zeros((CH, NH * CH), F32)
        dzsum = jnp.zeros((CH, WA), F32)
        dvlns = []
        for cidx in range(nchunk):
            dzc = dz[cidx * CH:(cidx + 1) * CH, :]
            dzsum = dzsum + dzc
            dzb = dzc.astype(BF16)
            dwcat = dwcat + _dot_nt(dzb, vexps[cidx])
            dvexp = _dot(wsb_t, dzb)
            dvl = jnp.zeros((CH, WA), F32)
            for h in range(NH):
                dvl = dvl + jnp.where(masks[h], dvexp[h * CH:(h + 1) * CH, :], 0.0)
            dvlns.append(dvl)
        dvln = jnp.concatenate(dvlns, axis=0)
        dv, dgng, dgnb = _ln_bwd(dvln, xhat, rstd, gng)
        dv_ref[0] = dv
        lane = lax.broadcasted_iota(jnp.int32, (NH, WA), 1)
        head = lax.broadcasted_iota(jnp.int32, (NH, WA), 0)
        sel = jnp.where((lane >= head * HD) & (lane < (head + 1) * HD), 1.0, 0.0).astype(F32)
        dbsp = lax.dot_general(sel, dzsum, NT, preferred_element_type=F32, precision=lax.Precision.HIGHEST)
        chat, crstd = _ln(conv_ref[0])
        cng = cng_ref[...]
        cln = chat * cng + cnb_ref[...]
        sg = _sigmoid(cln)
        nbb, rb = _rms(cln * sg)
        dyb, dgob = _rms_bwd(dyv[:, WA:D], nbb, rb, gob_ref[...])
        dconv, dcng, dcnb = _ln_bwd(dyb * _dsilu(cln, sg), chat, crstd, cng)
        dconv_ref[0] = dconv
        dcb = jnp.sum(dconv, axis=0, keepdims=True)
        for ref, val in ((dwcat_ref, dwcat), (dbsp_ref, dbsp), (dgng_ref, dgng), (dgnb_ref, dgnb), (dgoa_ref, dgoa),
                         (dgob_ref, dgob), (dcng_ref, dcng), (dcnb_ref, dcnb), (dcb_ref, dcb)):
            _acc(ref, val, first)

    t5 = _tok_specs(tm, WA)
    r5 = _row_spec(WA)
    full = lambda shape: pl.BlockSpec(shape, lambda b, i: (0,) * len(shape))
    big = jax.ShapeDtypeStruct((nb, s, WA), F32)
    row = jax.ShapeDtypeStruct((1, WA), F32)
    return pl.pallas_call(
        body, name="mix_mid_bwd", grid=(nb, s // tm),
        out_shape=[big, big, big, jax.ShapeDtypeStruct((CH, NH * CH), F32), jax.ShapeDtypeStruct((NH, CH), F32),
                   row, row, row, row, row, row, row],
        in_specs=[_tok_specs(tm, D), t5, t5, t5, r5, r5, full((CH, NH * CH)), full((NH * CH, CH)), full((CH, WA)),
                  r5, r5, r5, r5],
        out_specs=[t5, t5, t5, full((CH, NH * CH)), full((NH, CH)), r5, r5, r5, r5, r5, r5, r5],
        compiler_params=_cparams(),
    )(dy, u, v, conv, gn_g, gn_b, wcat, wcat_t, bspt, cn_g, cn_b, go_a, go_b)


def _mix_in_bwd(dxo, x, du, dv, dconv, a, g, sh, sc, g_pre, w_mi4, conv_w):
    nb, s, _ = x.shape
    tm = min(512, s)
    n_i = s // tm

    def body(dxo_ref, x_ref, du_ref, dv_ref, dc_ref, dch_ref, a_ref, g_ref, ah_ref, gh_ref, sh_ref, sc_ref,
             gpre_ref, w_ref, cw_ref,
             dx_ref, dproj_ref, h_ref, dgpre_ref, dsh_ref, dsc_ref, dcw_ref, ext_ref, shf_ref, dglu_ref):
        b, i = pl.program_id(0), pl.program_id(1)
        first = _first(b, i)
        av, gv = a_ref[0], g_ref[0]
        sg = _sigmoid(gv)
        dconv = dc_ref[0]
        ext_ref[0:tm, :] = dconv
        ext_ref[tm:tm + HALO, :] = dch_ref[0] * jnp.where(i == n_i - 1, 0.0, 1.0).astype(F32)
        _make_shifts(ext_ref, shf_ref, tm)

        def put_dglu(r0, acc):
            dglu_ref[pl.ds(r0, TAP_ROWS), :] = acc

        _conv_taps(shf_ref, cw_ref, tm, [(CK - 1 - k, k) for k in range(CK)], put_dglu)
        dglu = dglu_ref[...]
        ext_ref[0:HALO, :] = (ah_ref[0] * _sigmoid(gh_ref[0])) * jnp.where(i == 0, 0.0, 1.0).astype(F32)
        ext_ref[HALO:HALO + tm, :] = av * sg
        _make_shifts(ext_ref, shf_ref, tm)

        @pl.when(first)
        def _():
            dcw_ref[...] = jnp.zeros((HALO, WB), F32)

        for k in range(CK):
            o = k + HALO - (CK - 1)
            lo = SHIFTS * (o // SHIFTS)
            dcw_ref[k:k + 1, :] += jnp.sum(dconv * shf_ref[o % SHIFTS, lo:lo + tm, :], axis=0, keepdims=True)
        da = dglu * sg
        dg = dglu * av * (sg * (1.0 - sg))
        parts = [du_ref[0].astype(BF16), dv_ref[0].astype(BF16), da.astype(BF16), dg.astype(BF16)]
        dh = jnp.zeros((tm, D), F32)
        for k in range(4):
            dproj_ref[0, :, k * WA:(k + 1) * WA] = parts[k]
            dh = dh + _dot_nt(parts[k], w_ref[k])
        n, r = _rms(x_ref[0])
        gpre = gpre_ref[...]
        ng = n * gpre
        scale1 = 1.0 + sc_ref[0]
        h_ref[0] = (ng * scale1 + sh_ref[0]).astype(BF16)
        dsh = jnp.sum(dh, axis=0, keepdims=True)
        dsc = jnp.sum(dh * ng, axis=0, keepdims=True)
        dxn, dgpre = _rms_bwd(dh * scale1, n, r, gpre)
        dx_ref[0] = dxo_ref[0] + dxn
        _acc(dgpre_ref, dgpre, first)
        _acc(dsh_ref, dsh[None], i == 0)
        _acc(dsc_ref, dsc[None], i == 0)

    tok = _tok_specs(tm, D)
    t5 = _tok_specs(tm, WA)
    full = lambda shape: pl.BlockSpec(shape, lambda b, i: (0,) * len(shape))
    mod_shape = jax.ShapeDtypeStruct((nb, 1, D), F32)
    return pl.pallas_call(
        body, name="mix_in_bwd", grid=(nb, n_i),
        out_shape=[jax.ShapeDtypeStruct((nb, s, D), F32), jax.ShapeDtypeStruct((nb, s, 4 * WA), BF16),
                   jax.ShapeDtypeStruct((nb, s, D), BF16), jax.ShapeDtypeStruct((1, D), F32), mod_shape, mod_shape,
                   jax.ShapeDtypeStruct((HALO, WB), F32)],
        in_specs=[tok, tok, t5, t5, t5, _halo_next_spec(tm, s), t5, t5, _halo_prev_spec(tm), _halo_prev_spec(tm),
                  _mod_spec(), _mod_spec(), _row_spec(), VMEM_FULL, full((HALO, WB))],
        out_specs=[tok, _tok_specs(tm, 4 * WA), tok, _row_spec(), _mod_spec(), _mod_spec(), full((HALO, WB))],
        scratch_shapes=[pltpu.VMEM((_ext_rows(tm), WB), F32), pltpu.VMEM((SHIFTS, tm + HALO, WB), F32),
                        pltpu.VMEM((tm, WB), F32)],
        compiler_params=_cparams(),
    )(dxo, x, du, dv, dconv, dconv, a, g, a, g, sh, sc, g_pre, w_mi4, conv_w)


def _row_tile(rows, cols):
    best = 16
    for t in range(16, rows + 1, 16):
        if rows % t == 0 and t * cols * 4 <= 1536 * 1024:
            best = t
    return best


def _sum4(name, own4, recv, j_arr):
    _, rows, cols = own4.shape
    tr = _row_tile(rows, cols)

    def body(j_ref, own_ref, recv_ref, o_ref):
        del j_ref
        acc = own_ref[0]
        for k in range(3):
            acc = acc + recv_ref[k].astype(F32)
        o_ref[...] = acc

    return pl.pallas_call(
        body, name=name,
        grid_spec=pltpu.PrefetchScalarGridSpec(
            num_scalar_prefetch=1, grid=(rows // tr,),
            in_specs=[pl.BlockSpec((1, tr, cols), lambda i, j: (j[0], i, 0)),
                      pl.BlockSpec((3, tr, cols), lambda i, j: (0, i, 0))],
            out_specs=pl.BlockSpec((tr, cols), lambda i, j: (i, 0))),
        out_shape=jax.ShapeDtypeStruct((rows, cols), F32),
        compiler_params=_cparams(),
    )(j_arr, own4, recv)


def _pair_plan(shapes):
    def plan(x, y, c, src, land):
        sends = []
        for a, shape in enumerate(shapes):
            rows = shape[1] // 2
            theirs = pl.ds(pl.multiple_of((1 - c) * rows, 16), rows)
            sends.append((src[a].at[:, theirs], land[a], (x, y, 1 - c), land[a]))
        return [], sends

    return plan


def _swap_plan(n):
    def plan(x, y, c, src, land):
        return [], [(src[a], land[a], (x, y, 1 - c), land[a]) for a in range(n)]

    return plan


def _pair_sum(name, g32, recv, c_arr):
    nblk, rows, cols = recv.shape
    tr = _row_tile(rows, cols)
    nh = rows // tr

    def body(c_ref, g_ref, r_ref, o32_ref, obf_ref):
        del c_ref
        val = g_ref[0] + r_ref[0].astype(F32)
        o32_ref[0] = val
        obf_ref[0] = val.astype(BF16)

    spec = pl.BlockSpec((1, tr, cols), lambda k, i, c: (k, i, 0))
    return pl.pallas_call(
        body, name=name,
        grid_spec=pltpu.PrefetchScalarGridSpec(
            num_scalar_prefetch=1, grid=(nblk, nh),
            in_specs=[pl.BlockSpec((1, tr, cols), lambda k, i, c: (k, c[0] * nh + i, 0)), spec],
            out_specs=[spec, spec]),
        out_shape=[jax.ShapeDtypeStruct(recv.shape, F32), jax.ShapeDtypeStruct(recv.shape, BF16)],
        compiler_params=_cparams(),
    )(c_arr, g32, recv)


def _adam_halves(name, w, m, v, mine, theirs, c_arr):
    rows, cols = w.shape
    tr = _row_tile(rows // 2, cols)
    nh = (rows // 2) // tr

    def body(c_ref, w_ref, m_ref, v_ref, mine_ref, theirs_ref, g_out, d_out, m_out, v_out):
        here = (pl.program_id(0) // nh) == c_ref[0]
        g = jnp.where(here, mine_ref[...], theirs_ref[...])
        delta, m2, v2 = _adam(w_ref[...], g, m_ref[...], v_ref[...])
        g_out[...] = g
        d_out[...] = delta
        m_out[...] = m2
        v_out[...] = v2

    spec = pl.BlockSpec((tr, cols), lambda i, c: (i, 0))
    shape = jax.ShapeDtypeStruct((rows, cols), F32)
    return pl.pallas_call(
        body, name=name,
        grid_spec=pltpu.PrefetchScalarGridSpec(
            num_scalar_prefetch=1, grid=(2 * nh,),
            in_specs=[spec, spec, spec,
                      pl.BlockSpec((tr, cols), lambda i, c: (jnp.clip(i - c[0] * nh, 0, nh - 1), 0)),
                      pl.BlockSpec((tr, cols), lambda i, c: (jnp.clip(i - (1 - c[0]) * nh, 0, nh - 1), 0))],
            out_specs=[spec] * 4),
        out_shape=[shape] * 4,
        compiler_params=_cparams(),
    )(c_arr, w, m, v, mine, theirs)


def _adam_big(name, w, m, v, ga, gb):
    rows, cols = w.shape
    tr = _row_tile(rows, cols)

    def body(w_ref, m_ref, v_ref, ga_ref, gb_ref, g_out, d_out, m_out, v_out):
        gsum = ga_ref[...] + gb_ref[...]
        delta, m2, v2 = _adam(w_ref[...], gsum, m_ref[...], v_ref[...])
        g_out[...] = gsum
        d_out[...] = delta
        m_out[...] = m2
        v_out[...] = v2

    spec = pl.BlockSpec((tr, cols), lambda i: (i, 0))
    shape = jax.ShapeDtypeStruct((rows, cols), F32)
    return pl.pallas_call(
        body, name=name, grid=(rows // tr,), out_shape=[shape] * 4,
        in_specs=[spec] * 5, out_specs=[spec] * 4, compiler_params=_cparams(),
    )(w, m, v, ga, gb)


PK_VEC = 0
PK_LOSS = 6
PK_PAIR = 8
PK_BSP = 16
PK_WCAT = 24
PK_ROWS = PK_WCAT + CH
PAIR_ORDER = ("gmlp_norm_g", "gmlp_norm_b", "conv_b", "conv_norm_g", "conv_norm_b", "g_out_a", "g_out_b")
VEC_ORDER = ("g_pre_f1", "g_post_f1", "g_pre_m", "g_post_m", "g_pre_f2", "g_post_f2")


def _pack_late(rows):
    counts = [r.shape[0] for r in rows]
    assert sum(counts) == 8

    def body(*refs):
        o_ref = refs[-1]
        at = 0
        for r, cnt in zip(refs[:-1], counts):
            o_ref[at:at + cnt, :] = r[...]
            at += cnt

    return pl.pallas_call(
        body, name="pack_late", out_shape=jax.ShapeDtypeStruct((8, D), F32),
        in_specs=[VMEM_FULL] * len(rows), out_specs=VMEM_FULL, compiler_params=_cparams(),
    )(*rows)


def _pack_small(vecs, pairs, dbsp, dwcat, lsum):
    def body(*refs):
        vec_refs = refs[:4]
        pair_refs = refs[4:11]
        dbsp_ref, dwcat_ref, lsum_ref, o_ref = refs[11:]
        o_ref[0:PK_WCAT, :] = jnp.zeros((PK_WCAT, D), F32)
        o_ref[PK_LOSS:PK_LOSS + 1, 0:128] = lsum_ref[...]
        for k, r in enumerate(vec_refs):
            o_ref[PK_VEC + 2 + k:PK_VEC + 3 + k, :] = r[...]
        for k, r in enumerate(pair_refs):
            row, half = PK_PAIR + k // 2, k % 2
            o_ref[row:row + 1, half * WA:(half + 1) * WA] = r[...]
        o_ref[PK_BSP:PK_BSP + NH, 0:CH] = dbsp_ref[...]
        o_ref[PK_WCAT:PK_ROWS, :] = dwcat_ref[...]

    args = list(vecs) + list(pairs) + [dbsp, dwcat, lsum]
    return pl.pallas_call(
        body, name="pack_small", out_shape=jax.ShapeDtypeStruct((PK_ROWS, D), F32),
        in_specs=[VMEM_FULL] * len(args), out_specs=VMEM_FULL, compiler_params=_cparams(),
    )(*args)


def _small_adam(pack_all, late_all, dcw_all, dada_all, params, behind):
    names = list(VEC_ORDER) + list(PAIR_ORDER) + ["b_spatial", "w_spatial", "conv_w", "b_ada"]
    flat = []
    for nm in names:
        flat += list(params[nm])
    n_in = 4 + len(flat)

    def body(*refs):
        pack_ref, late_ref, dcw_ref, dada_ref = refs[:4]
        prm = refs[4:n_in]
        outs = refs[n_in + 1:]

        def total(r0, nr, c0, nc):
            acc = pack_ref[0, r0:r0 + nr, c0:c0 + nc]
            for d in range(1, NDEV):
                acc = acc + pack_ref[d, r0:r0 + nr, c0:c0 + nc]
            return acc

        def emit(idx, g, getw, put):
            w_ref, m_ref, v_ref = prm[3 * idx:3 * idx + 3]
            delta, m2, v2 = _adam(getw(w_ref), g, getw(m_ref), getw(v_ref))
            for o_ref, val in zip(outs[4 * idx:4 * idx + 4], (g, delta, m2, v2)):
                put(o_ref, val)

        def whole(ref):
            return ref[...]

        def put_whole(ref, val):
            ref[...] = val

        idx = 0
        for k in range(6):
            if k < 2:
                g = late_ref[0, k:k + 1, :]
                for d in range(1, NDEV):
                    g = g + late_ref[d, k:k + 1, :]
            else:
                g = total(PK_VEC + k, 1, 0, D)
            emit(idx, g, whole, put_whole)
            idx += 1
        for k in range(7):
            emit(idx, total(PK_PAIR + k // 2, 1, (k % 2) * WA, WA), whole, put_whole)
            idx += 1
        emit(idx, total(PK_BSP, NH, 0, CH), lambda r: r[0], lambda r, val: r.__setitem__(0, val))
        idx += 1
        row = lax.broadcasted_iota(jnp.int32, (CH, CH), 0)
        col = lax.broadcasted_iota(jnp.int32, (CH, CH), 1)
        for h in range(NH):
            gh = jnp.where(col <= row, total(PK_WCAT, CH, h * CH, CH), 0.0)
            w_ref, m_ref, v_ref = prm[3 * idx:3 * idx + 3]
            delta, m2, v2 = _adam(w_ref[0, h], gh, m_ref[0, h], v_ref[0, h])
            for o_ref, val in zip(outs[4 * idx:4 * idx + 4], (gh, delta, m2, v2)):
                o_ref[0, h] = val
        idx += 1
        gcw = dcw_ref[0, 0:CK, :]
        for d in range(1, NDEV):
            gcw = gcw + dcw_ref[d, 0:CK, :]
        emit(idx, gcw, lambda r: r[0], lambda r, val: r.__setitem__(0, val))
        idx += 1
        emit(idx, jnp.sum(dada_ref[...], axis=0, keepdims=True), whole, put_whole)
        outs[-1][...] = jnp.sum(total(PK_LOSS, 1, 0, 128), axis=1, keepdims=True) * (0.5 / D)

    out_shape = []
    for nm in names:
        w = params[nm][0]
        out_shape += [jax.ShapeDtypeStruct(w.shape, F32)] * 4
    out_shape.append(jax.ShapeDtypeStruct((1, 1), F32))
    res = pl.pallas_call(
        body, name="small_adam", out_shape=out_shape,
        in_specs=[VMEM_FULL] * n_in + [ANY], out_specs=[VMEM_FULL] * len(out_shape), compiler_params=_cparams(),
    )(pack_all, late_all, dcw_all, dada_all, *flat, behind)
    return {nm: tuple(res[4 * k:4 * k + 4]) for k, nm in enumerate(names)}, res[-1].reshape(())


WEIGHTS = ['w_ada', 'b_ada', 'g_pre_f1', 'g_post_f1', 'w_f1_in', 'w_f1_out', 'g_pre_m', 'g_post_m', 'w_mix_in',
           'gmlp_norm_g', 'gmlp_norm_b', 'w_spatial', 'b_spatial', 'conv_w', 'conv_b', 'conv_norm_g', 'conv_norm_b',
           'g_out_a', 'g_out_b', 'w_mix_out', 'g_pre_f2', 'g_post_f2', 'w_f2_in', 'w_f2_out']
BIG = ('w_f1_in', 'w_f1_out', 'w_mix_in', 'w_mix_out', 'w_f2_in', 'w_f2_out')


def kernel(x, c, w_ada, b_ada, g_pre_f1, g_post_f1, w_f1_in, w_f1_out, g_pre_m, g_post_m, w_mix_in, gmlp_norm_g, gmlp_norm_b, w_spatial, b_spatial, conv_w, conv_b, conv_norm_g, conv_norm_b, g_out_a, g_out_b, w_mix_out, g_pre_f2, g_post_f2, w_f2_in, w_f2_out, loss_target, m_w_ada, m_b_ada, m_g_pre_f1, m_g_post_f1, m_w_f1_in, m_w_f1_out, m_g_pre_m, m_g_post_m, m_w_mix_in, m_gmlp_norm_g, m_gmlp_norm_b, m_w_spatial, m_b_spatial, m_conv_w, m_conv_b, m_conv_norm_g, m_conv_norm_b, m_g_out_a, m_g_out_b, m_w_mix_out, m_g_pre_f2, m_g_post_f2, m_w_f2_in, m_w_f2_out, v_w_ada, v_b_ada, v_g_pre_f1, v_g_post_f1, v_w_f1_in, v_w_f1_out, v_g_pre_m, v_g_post_m, v_w_mix_in, v_gmlp_norm_g, v_gmlp_norm_b, v_w_spatial, v_b_spatial, v_conv_w, v_conv_b, v_conv_norm_g, v_conv_norm_b, v_g_out_a, v_g_out_b, v_w_mix_out, v_g_pre_f2, v_g_post_f2, v_w_f2_in, v_w_f2_out):
    env = dict(locals())
    wts = {n: env[n] for n in WEIGHTS}
    mom = {n: env["m_" + n] for n in WEIGHTS}
    var = {n: env["v_" + n] for n in WEIGHTS}
    nb, s, _ = x.shape
    t = nb * s
    ax, ay, ac = lax.axis_index("x"), lax.axis_index("y"), lax.axis_index("c")
    j_chip = 2 * ax + ay
    dev = 4 * ax + 2 * ay + ac
    j_arr = j_chip.reshape(1).astype(jnp.int32)

    groups = (("w_f1_in",), ("w_mix_in", "w_mix_out"), ("w_f2_in", "w_f2_out"), ("w_f1_out",))
    def gather_operands(gi):
        srcs = [wts[n][0].astype(BF16) for n in groups[gi]] + ([conv_w[0]] if gi == 1 else [])
        lands = [lax.dynamic_update_index_in_dim(lax.empty((NCHIP,) + a.shape, a.dtype), a, j_chip, 0) for a in srcs]
        return srcs, lands

    def gather_start(gi, behind, operands=None):
        srcs, lands = operands or gather_operands(gi)
        plan_a, plan_b, n_b = _gather_plans([a.shape for a in srcs])
        ssem, rsem, srcs, lands, token = _split_start("gw_start%d" % gi, srcs, lands, plan_a, 3 * len(srcs), behind)
        gather[gi] = (srcs, lands, ssem, rsem, plan_a, plan_b, n_b)
        return token

    def gather_forward(gi, behind):
        srcs, lands, ssem, rsem, plan_a, plan_b, n_b = gather[gi]
        ssem, rsem, lands, token = _split_forward("gw_fwd%d" % gi, srcs, lands, ssem, rsem, plan_a, plan_b, n_b, behind)
        gather[gi] = (lands, ssem, rsem, plan_b)
        return token

    def gathered(gi, behind):
        lands, ssem, rsem, plan_b = gather[gi]
        return _split_wait("gw_wait%d" % gi, [], lands, ssem, rsem, plan_b, behind)

    gather = {}
    (c_all8,) = _allgather8("gather_c", [c.reshape(8, (nb * D) // 8)])
    token = gather_start(0, c_all8)
    c_all = c_all8.reshape(NDEV * nb, D) + token[0, 0]
    b_sh = lax.dynamic_slice(b_ada, (0, j_chip * ADA_SH), (1, ADA_SH))
    ada_sh = _ada_fwd(c_all, w_ada[0], b_sh)
    later = [gather_operands(3), gather_operands(1), gather_operands(2)]
    (ada4,) = _chip_allgather("gather_ada", [ada_sh], behind=[a for pair in later for arrs in pair for a in arrs])
    token = gather_forward(0, ada4)
    token = gather_start(3, token, later[0])
    token = gather_start(1, token, later[1])
    token = gather_start(2, token, later[2])
    ada_me = lax.dynamic_slice(ada4, (0, dev * nb, 0), (NCHIP, nb, ADA_SH))
    ada_me = jnp.transpose(ada_me, (1, 0, 2)).reshape(nb, NMOD * D)
    sh1, sc1, gt1, sh2, sc2, gt2, sh3, sc3, gt3 = [ada_me[:, k * D:(k + 1) * D].reshape(nb, 1, D) for k in range(NMOD)]

    wcat = jnp.transpose(w_spatial[0], (1, 0, 2)).reshape(CH, NH * CH)
    wcat_t = jnp.transpose(w_spatial[0], (0, 2, 1)).reshape(NH * CH, CH)
    bspt = jnp.repeat(b_spatial[0].T, HD, axis=1)

    (w1i,) = gathered(0, token)
    p1, act1 = _ffn_up(x, sh1, sc1, g_pre_f1, w1i)
    (w1o,) = gathered(3, gather_forward(3, act1))
    w1o = w1o.reshape(DFF, D)
    x1, f1 = _ffn_down(x, act1, gt1, g_post_f1, w1o)
    wmi, wmo, cw4 = gathered(1, gather_forward(1, x1))
    wmo = wmo.reshape(D, D)
    cw_full = jnp.transpose(cw4, (1, 0, 2)).reshape(CK, WB)
    cw_pad = jnp.pad(cw_full, ((0, HALO - CK), (0, 0)))
    u, v, a, g = _mix_in_fwd(x1, sh2, sc2, g_pre_m, wmi)
    token = gather_forward(2, u)
    x2, conv, yb, m = _mix_mid_fwd(x1, u, v, a, g, gt2 + token[0, 0], gmlp_norm_g, gmlp_norm_b, wcat, bspt, cw_pad, conv_b,
                                   conv_norm_g, conv_norm_b, g_out_a, g_out_b, wmo, g_post_m)
    w2i, w2o = gathered(2, [x2, token])
    w2o = w2o.reshape(DFF, D)
    dx3, df2, p2, lsum, dg_post_f2, dgt3 = _ffn_fwd(x2, sh3, sc3, gt3, g_pre_f2, g_post_f2, w2i, w2o, target=loss_target)

    def chip4(pair, rows):
        return [arr.reshape(NCHIP, rows, arr.shape[-1]) for arr in pair]

    def scatter_start(tag, pairs, behind):
        srcs = [p[1] for p in pairs]
        lands = [lax.empty((3,) + a.shape[1:], a.dtype) for a in srcs]
        ssem, rsem, srcs, lands, token = _split_start("gs_start_" + tag, srcs, lands, _scatter_plan(len(srcs)),
                                                      3 * len(srcs), behind)
        return (srcs, lands, ssem, rsem), token

    def scatter_wait(tag, state, behind):
        srcs, lands, ssem, rsem = state
        return _split_wait("gs_wait_" + tag, srcs, lands, ssem, rsem, _scatter_plan(len(srcs)), behind)

    def allgather_start(tag, arrs, behind):
        lands = [lax.dynamic_update_index_in_dim(lax.empty((NDEV,) + a.shape, a.dtype), a, dev, 0) for a in arrs]
        ssem, rsem, srcs, lands, token = _split_start("small_start_" + tag, arrs, lands, _allgather_plan(len(arrs)),
                                                      7 * len(arrs), behind)
        return (srcs, lands, ssem, rsem), token

    def allgather_wait(tag, state, behind):
        srcs, lands, ssem, rsem = state
        return _split_wait("small_wait_" + tag, srcs, lands, ssem, rsem, _allgather_plan(len(srcs)), behind)

    out = {}
    dx2, dp2, h3, a2, dg_pre_f2, dsh3, dsc3 = _ffn_bwd(
        dx3, x2, None, p2, sh3, sc3, gt3, g_pre_f2, g_post_f2, w2i, w2o, df=df2)
    gw2i = _wgrad("wgrad_f2_in", h3.reshape(t, D), dp2.reshape(t, 2 * DFF), 2 * DFF // NCHIP, True)
    gw2o = chip4(_wgrad("wgrad_f2_out", a2.reshape(t, DFF), df2.reshape(t, D), D // 2, False), DFF // NCHIP)
    scat_f2, tok = scatter_start("f2", [gw2i, gw2o], dg_post_f2)
    dy, dm, dg_post_m, dgt2 = _mix_out_bwd(dx2, m, gt2 + tok[0, 0], g_post_m, wmo)
    gwmo = chip4(_wgrad("wgrad_mix_out", yb.reshape(t, D), dm.reshape(t, D), D // 2, False), D // NCHIP)
    (du, dv, dconv, dwcat, dbsp, dgn_g, dgn_b, dgo_a, dgo_b, dcn_g, dcn_b, dcb) = _mix_mid_bwd(
        dy, u, v, conv, gmlp_norm_g, gmlp_norm_b, wcat, wcat_t, bspt, conv_norm_g, conv_norm_b, g_out_a, g_out_b)
    dx1, dproj, h2, dg_pre_m, dsh2, dsc2, dcw = _mix_in_bwd(dx2, x1, du, dv, dconv, a, g, sh2, sc2, g_pre_m, wmi, cw_pad)
    gwmi = _wgrad("wgrad_mix_in", h2.reshape(t, D), dproj.reshape(t, 4 * WA), WA, True)
    scat_mix, tok = scatter_start("mix", [gwmi, gwmo], dg_pre_m)

    vec_grads = dict(g_pre_m=dg_pre_m, g_post_m=dg_post_m, g_pre_f2=dg_pre_f2, g_post_f2=dg_post_f2)
    pair_grads = dict(gmlp_norm_g=dgn_g, gmlp_norm_b=dgn_b, conv_b=dcb, conv_norm_g=dcn_g, conv_norm_b=dcn_b,
                      g_out_a=dgo_a, g_out_b=dgo_b)
    pack = _pack_small([vec_grads[n] for n in VEC_ORDER[2:]], [pair_grads[n] for n in PAIR_ORDER], dbsp, dwcat, lsum)
    dada_early = jnp.concatenate([q.reshape(nb, D) for q in (dsh2, dsc2, dgt2, dsh3, dsc3, dgt3)], axis=1)
    early, tok2 = allgather_start("early", [pack, dcw, dada_early.reshape(8, (nb * 6 * D) // 8)], tok)
    grad_x, dp1, h1, a1, df1, dg_pre_f1, dg_post_f1, dsh1, dsc1, dgt1 = _ffn_bwd(
        dx1, x, f1, p1, sh1 + tok2[0, 0], sc1, gt1, g_pre_f1, g_post_f1, w1i, w1o)
    late_pack = _pack_late([dg_pre_f1, dg_post_f1] + [q.reshape(nb, D) for q in (dsh1, dsc1, dgt1)])
    late, tok2 = allgather_start("late", [late_pack], dg_post_f1)
    gw1i = _wgrad("wgrad_f1_in", h1.reshape(t, D), dp1.reshape(t, 2 * DFF), 2 * DFF // NCHIP, True)
    gw1o = chip4(_wgrad("wgrad_f1_out", a1.reshape(t, DFF), df1.reshape(t, D), D // 2, False), DFF // NCHIP)
    def d2d_start(tag, srcs, lands, plan, behind):
        ssem, rsem, srcs, lands, token = _split_start("d2d_start_" + tag, srcs, lands, plan, len(srcs), behind)
        return (srcs, lands, ssem, rsem, plan), token

    def d2d_wait(tag, state, behind):
        srcs, lands, ssem, rsem, plan = state
        return _split_wait("d2d_wait_" + tag, srcs, lands, ssem, rsem, plan, behind)

    def swap_start(tag, parts, behind):
        return d2d_start(tag, parts, [lax.empty(a.shape, a.dtype) for a in parts], _swap_plan(len(parts)), behind)

    def sums(names, pairs, recv):
        return [_sum4("sum4_" + n, pairs[k][0], recv[k], j_arr) for k, n in enumerate(names)]

    def update(names, part, other):
        for k, n in enumerate(names):
            out[n] = tuple(r[None] for r in _adam_big("adam_" + n, wts[n][0], mom[n][0], var[n][0], part[k], other[k]))

    c_arr = ac.reshape(1).astype(jnp.int32)
    halves = [gw1i[1], gw1o[1]]
    pair_st, tok = d2d_start("pair", halves, [lax.empty((a.shape[0], a.shape[1] // 2, a.shape[2]), a.dtype) for a in halves],
                             _pair_plan([a.shape for a in halves]), tok2)
    names_f2, names_mix, names_f1 = ("w_f2_in", "w_f2_out"), ("w_mix_in", "w_mix_out"), ("w_f1_in", "w_f1_out")
    part_f2 = sums(names_f2, [gw2i, gw2o], scatter_wait("f2", scat_f2, tok))
    sib = d2d_wait("pair", pair_st, part_f2)
    pair_i = _pair_sum("pairsum_f1_in", gw1i[0], sib[0], c_arr)
    pair_o = _pair_sum("pairsum_f1_out", gw1o[0], sib[1], c_arr)
    scat_f1, tok = scatter_start("f1", [pair_i, pair_o], tok2)
    swap_f2, tok = swap_start("swap_f2", part_f2, tok)
    part_mix = sums(names_mix, [gwmi, gwmo], scatter_wait("mix", scat_mix, tok))
    swap_mix, tok = swap_start("swap_mix", part_mix, part_mix[1])

    pack_all, dcw_all, dada_early8 = allgather_wait("early", early, tok)
    (late_all,) = allgather_wait("late", late, pack_all)
    dada_late = jnp.transpose(late_all[:, 2:8, :].reshape(NDEV, 3, nb, D), (0, 2, 1, 3)).reshape(NDEV * nb, 3 * D)
    dada_all = jnp.concatenate([dada_late, dada_early8.reshape(NDEV * nb, 6 * D)], axis=1)
    dada_sh = lax.dynamic_slice(dada_all, (0, j_chip * ADA_SH), (NDEV * nb, ADA_SH))
    out["w_ada"] = tuple(r[None] for r in _ada_bwd_adam(c_all, dada_sh, w_ada[0], m_w_ada[0], v_w_ada[0]))
    update(names_f2, part_f2, d2d_wait("swap_f2", swap_f2, out["w_ada"][3]))
    update(names_mix, part_mix, d2d_wait("swap_mix", swap_mix, out["w_f2_out"][3]))

    mine = sums(names_f1, [pair_i, pair_o], scatter_wait("f1", scat_f1, out["w_mix_out"][3]))
    swap_f1, tok = swap_start("swap_f1", mine, mine[1])
    dcw_mine = lax.dynamic_slice(dcw_all, (0, 0, j_chip * (WB // NCHIP)), (NDEV, HALO, WB // NCHIP))
    small = {n: (wts[n], mom[n], var[n]) for n in list(VEC_ORDER) + list(PAIR_ORDER) + ["b_spatial", "w_spatial", "conv_w", "b_ada"]}
    small_out, loss = _small_adam(pack_all, late_all, dcw_mine, dada_all, small, tok)
    out.update(small_out)
    theirs = d2d_wait("swap_f1", swap_f1, out["b_ada"][3])
    for k, n in enumerate(names_f1):
        out[n] = tuple(r[None] for r in _adam_halves("adam_" + n, wts[n][0], mom[n][0], var[n][0], mine[k], theirs[k],
                                                     c_arr))

    res = [loss, grad_x]
    for k in range(4):
        res += [out[n][k] for n in WEIGHTS]
    return tuple(res)
```

```python
import jax
import jax.numpy as jnp
from jax import lax
from jax.experimental import pallas as pl
from jax.experimental.pallas import tpu as pltpu

D = 1024
DFF = 2816
WA = 512
WB = 512
NH = 8
HD = 64
CH = 128
CK = 31
HALO = 32
NMOD = 9
EPS = 1e-6
NCHIP = 4
NDEV = 8
FBLK = DFF // 2
ADA_SH = NMOD * D // NCHIP

LR, B1, B2, EPS_A, WD, STEP = 0.001, 0.9, 0.999, 1e-08, 0.01, 10

F32 = jnp.float32
BF16 = jnp.bfloat16
MESH = pl.DeviceIdType.MESH
ANY = pl.BlockSpec(memory_space=pl.ANY)
VMEM_FULL = pl.BlockSpec(memory_space=pltpu.VMEM)
VMEM_LIMIT = 56 * 1024 * 1024

NT = (((1,), (1,)), ((), ()))
TN = (((0,), (0,)), ((), ()))


def _dot(a, b):
    return jnp.dot(a, b, preferred_element_type=F32)


def _dot_nt(a, b):
    return lax.dot_general(a, b, NT, preferred_element_type=F32)


def _dot_tn(a, b):
    return lax.dot_general(a, b, TN, preferred_element_type=F32)


def _cparams():
    return pltpu.CompilerParams(vmem_limit_bytes=VMEM_LIMIT)


def _allgather8(name, arrs):
    n = len(arrs)
    remote = _allgather_plan(n)

    def plan(x, y, c, ins, outs):
        _, sends = remote(x, y, c, ins, outs)
        return [(ins[a], outs[a].at[4 * x + 2 * y + c]) for a in range(n)], sends

    shapes = [jax.ShapeDtypeStruct((NDEV,) + a.shape, a.dtype) for a in arrs]
    return _run_exchange(name, arrs, shapes, plan, n, 7 * n)


def _chip_relations(x, y):
    return [(1 - x, y), (x, 1 - y), (1 - x, 1 - y)]


def _exchange(name, arrs, out_shapes, plan):
    n = len(arrs)
    n_out = len(out_shapes)

    def body(*refs):
        ins, outs = refs[:n], refs[n:n + n_out]
        send_sems, recv_sems, local_sems = refs[n + n_out:]
        x, y, c = lax.axis_index("x"), lax.axis_index("y"), lax.axis_index("c")
        local, sends = plan(x, y, c, ins, outs)
        locs = [pltpu.make_async_copy(s, d, local_sems.at[i]) for i, (s, d) in enumerate(local)]
        for loc in locs:
            loc.start()
        cps = [pltpu.make_async_remote_copy(src_ref=s, dst_ref=d, send_sem=send_sems.at[i], recv_sem=recv_sems.at[i],
                                            device_id=peer, device_id_type=MESH)
               for i, (s, d, peer, _) in enumerate(sends)]
        for cp in cps:
            cp.start()
        for i, (s, _, peer, landing) in enumerate(sends):
            pltpu.make_async_remote_copy(src_ref=s, dst_ref=landing, send_sem=send_sems.at[i], recv_sem=recv_sems.at[i],
                                         device_id=peer, device_id_type=MESH).wait_recv()
        for cp in cps:
            cp.wait_send()
        for loc in locs:
            loc.wait()

    return n, n_out, body


def _run_exchange(name, arrs, out_shapes, plan, n_local, n_send):
    n, n_out, body = _exchange(name, arrs, out_shapes, plan)
    return pl.pallas_call(
        body, name=name, out_shape=out_shapes,
        in_specs=[ANY] * n, out_specs=[ANY] * n_out,
        scratch_shapes=[pltpu.SemaphoreType.DMA((n_send,)), pltpu.SemaphoreType.DMA((n_send,)),
                        pltpu.SemaphoreType.DMA((max(n_local, 1),))],
    )(*arrs)


def _chip_allgather(name, arrs, behind=()):
    n = len(arrs)

    def plan(x, y, c, ins, outs):
        j_me = 2 * x + y
        local = [(ins[a], outs[a].at[j_me]) for a in range(n)]
        sends = []
        for a in range(n):
            for (px, py) in _chip_relations(x, y):
                sends.append((ins[a], outs[a].at[j_me], (px, py, c), outs[a].at[2 * px + py]))
        return local, sends

    shapes = [jax.ShapeDtypeStruct((NCHIP,) + a.shape, a.dtype) for a in arrs]
    return _run_exchange(name, list(arrs) + list(behind), shapes, plan, n, 3 * n)


HBM = pl.BlockSpec(memory_space=pltpu.HBM)
SEM = pl.BlockSpec(memory_space=pltpu.SEMAPHORE)
EFFECT = pltpu.SideEffectType.DATAFLOW_SIDE_EFFECTING


def _split_start(name, srcs, lands, plan, n_send, after):
    n, nl = len(srcs), len(lands)

    def body(*refs):
        src, land = refs[:n], refs[n:n + nl]
        send_sems, recv_sems = refs[n + nl + 1], refs[n + nl + 2]
        token = refs[-2]
        local_sems = refs[-1]
        x, y, c = lax.axis_index("x"), lax.axis_index("y"), lax.axis_index("c")
        local, sends = plan(x, y, c, src, land)
        locs = [pltpu.make_async_copy(s, d, local_sems.at[i]) for i, (s, d) in enumerate(local)]
        for loc in locs:
            loc.start()
        for loc in locs:
            loc.wait()
        for i, (s, d, peer, _) in enumerate(sends):
            pltpu.make_async_remote_copy(src_ref=s, dst_ref=d, send_sem=send_sems.at[i], recv_sem=recv_sems.at[i],
                                         device_id=peer, device_id_type=MESH).start()
        token[...] = jnp.zeros_like(token)

    thru = [pltpu.HBM(a.shape, a.dtype) for a in lands]
    srcs = [pltpu.with_memory_space_constraint(a, pltpu.HBM) for a in srcs]
    res = pl.pallas_call(
        body, name=name,
        out_shape=(pltpu.SemaphoreType.DMA((n_send,)), pltpu.SemaphoreType.DMA((n_send,)), *thru,
                   jax.ShapeDtypeStruct((8, 128), F32)),
        in_specs=[HBM] * (n + nl) + [ANY],
        out_specs=(SEM, SEM, *([HBM] * nl), pl.BlockSpec(memory_space=pltpu.VMEM)),
        input_output_aliases={n + i: 2 + i for i in range(nl)},
        scratch_shapes=[pltpu.SemaphoreType.DMA((max(n, 1),))],
        compiler_params=pltpu.CompilerParams(has_side_effects=EFFECT),
    )(*srcs, *[pltpu.with_memory_space_constraint(a, pltpu.HBM) for a in lands], after)
    return res[0], res[1], srcs, list(res[2:2 + nl]), res[-1]


def _split_wait(name, srcs, lands, send_sems, recv_sems, plan, after):
    n, nl = len(srcs), len(lands)
    afters = list(after) if isinstance(after, (list, tuple)) else [after]

    def body(*refs):
        src, land = refs[:n], refs[n:n + nl]
        send_sems, recv_sems = refs[n + nl], refs[n + nl + 1]
        x, y, c = lax.axis_index("x"), lax.axis_index("y"), lax.axis_index("c")
        _, sends = plan(x, y, c, src, land)
        for i, (s, _, peer, landing) in enumerate(sends):
            cp = pltpu.make_async_remote_copy(src_ref=s, dst_ref=landing, send_sem=send_sems.at[i],
                                              recv_sem=recv_sems.at[i], device_id=peer, device_id_type=MESH)
            cp.wait_send()
            cp.wait_recv()

    thru = [pltpu.HBM(a.shape, a.dtype) for a in lands]
    res = pl.pallas_call(
        body, name=name, out_shape=tuple(thru),
        in_specs=[HBM] * (n + nl) + [SEM, SEM] + [ANY] * len(afters), out_specs=tuple([HBM] * nl),
        input_output_aliases={n + i: i for i in range(nl)},
        compiler_params=pltpu.CompilerParams(has_side_effects=EFFECT),
    )(*srcs, *lands, send_sems, recv_sems, *afters)
    return list(res)


def _split_forward(name, srcs, lands, send_a, recv_a, plan_a, plan_b, n_b, after):
    n, nl = len(srcs), len(lands)

    def body(*refs):
        src, land = refs[:n], refs[n:n + nl]
        send_a, recv_a = refs[n + nl], refs[n + nl + 1]
        send_b, recv_b = refs[n + nl + 3], refs[n + nl + 4]
        token = refs[-1]
        x, y, c = lax.axis_index("x"), lax.axis_index("y"), lax.axis_index("c")
        _, first = plan_a(x, y, c, src, land)
        for i, (s, _, peer, landing) in enumerate(first):
            cp = pltpu.make_async_remote_copy(src_ref=s, dst_ref=landing, send_sem=send_a.at[i],
                                              recv_sem=recv_a.at[i], device_id=peer, device_id_type=MESH)
            cp.wait_send()
            cp.wait_recv()
        _, second = plan_b(x, y, c, src, land)
        for i, (s, d, peer, _) in enumerate(second):
            pltpu.make_async_remote_copy(src_ref=s, dst_ref=d, send_sem=send_b.at[i], recv_sem=recv_b.at[i],
                                         device_id=peer, device_id_type=MESH).start()
        token[...] = jnp.zeros_like(token)

    thru = [pltpu.HBM(a.shape, a.dtype) for a in lands]
    res = pl.pallas_call(
        body, name=name,
        out_shape=(pltpu.SemaphoreType.DMA((n_b,)), pltpu.SemaphoreType.DMA((n_b,)), *thru,
                   jax.ShapeDtypeStruct((8, 128), F32)),
        in_specs=[HBM] * (n + nl) + [SEM, SEM, ANY],
        out_specs=(SEM, SEM, *([HBM] * nl), pl.BlockSpec(memory_space=pltpu.VMEM)),
        input_output_aliases={n + i: 2 + i for i in range(nl)},
        compiler_params=pltpu.CompilerParams(has_side_effects=EFFECT),
    )(*srcs, *lands, send_a, recv_a, after)
    return res[0], res[1], list(res[2:2 + nl]), res[-1]


def _gather_plans(shapes):
    n = len(shapes)

    def halves(a, c):
        rows = shapes[a][0] // 2
        return pl.ds(pl.multiple_of(c * rows, 16), rows), pl.ds(pl.multiple_of((1 - c) * rows, 16), rows)

    def split(a):
        return shapes[a][0] % 32 == 0

    def plan_a(x, y, c, src, land):
        j_me = 2 * x + y
        sends = []
        for a in range(n):
            for (px, py) in _chip_relations(x, y):
                if split(a):
                    mine, _ = halves(a, c)
                    sends.append((src[a].at[mine], land[a].at[j_me, mine], (px, py, c), land[a].at[2 * px + py, mine]))
                else:
                    sends.append((src[a], land[a].at[j_me], (px, py, c), land[a].at[2 * px + py]))
        return [], sends

    def plan_b(x, y, c, src, land):
        sends = []
        for a in range(n):
            if split(a):
                mine, other = halves(a, c)
                for (px, py) in _chip_relations(x, y):
                    j = 2 * px + py
                    sends.append((land[a].at[j, mine], land[a].at[j, mine], (x, y, 1 - c), land[a].at[j, other]))
        return [], sends

    n_b = 3 * sum(1 for a in range(n) if split(a))
    return plan_a, plan_b, n_b


def _allgather_plan(n):
    flips = [(dx, dy, dc) for dx in (0, 1) for dy in (0, 1) for dc in (0, 1) if dx or dy or dc]

    def plan(x, y, c, src, land):
        sends = []
        for a in range(n):
            for dx, dy, dc in flips:
                px, py, pc = x ^ dx, y ^ dy, c ^ dc
                sends.append((src[a], land[a].at[4 * x + 2 * y + c], (px, py, pc), land[a].at[4 * px + 2 * py + pc]))
        return [], sends

    return plan


def _scatter_plan(n):
    def plan(x, y, c, src, land):
        sends = []
        for a in range(n):
            for k, (px, py) in enumerate(_chip_relations(x, y)):
                sends.append((src[a].at[2 * px + py], land[a].at[k], (px, py, c), land[a].at[k]))
        return [], sends

    return plan


def _rms(x):
    r = lax.rsqrt(jnp.mean(x * x, axis=-1, keepdims=True) + EPS)
    return x * r, r


def _rms_bwd(dy, n, r, g):
    dg = jnp.sum(dy * n, axis=0, keepdims=True)
    dn = dy * g
    dx = r * (dn - n * jnp.mean(dn * n, axis=-1, keepdims=True))
    return dx, dg


def _ln(x):
    mu = jnp.mean(x, axis=-1, keepdims=True)
    xc = x - mu
    rstd = lax.rsqrt(jnp.mean(xc * xc, axis=-1, keepdims=True) + EPS)
    return xc * rstd, rstd


def _ln_bwd(dy, xhat, rstd, g):
    dg = jnp.sum(dy * xhat, axis=0, keepdims=True)
    db = jnp.sum(dy, axis=0, keepdims=True)
    dxh = dy * g
    dx = rstd * (dxh - jnp.mean(dxh, axis=-1, keepdims=True) - xhat * jnp.mean(dxh * xhat, axis=-1, keepdims=True))
    return dx, dg, db


def _sigmoid(x):
    return jax.nn.sigmoid(x)


def _dsilu(x, s):
    return s * (1.0 + x * (1.0 - s))


def _adam(w, g, m, v):
    m = B1 * m + (1.0 - B1) * g
    v = B2 * v + (1.0 - B2) * (g * g)
    m_hat = m / (1.0 - B1 ** STEP)
    v_hat = v / (1.0 - B2 ** STEP)
    delta = -LR * (m_hat / (jnp.sqrt(v_hat) + EPS_A) + WD * w)
    return delta, m, v


def _head_mask(shape):
    lane = lax.broadcasted_iota(jnp.int32, shape, len(shape) - 1)
    return [(lane >= h * HD) & (lane < (h + 1) * HD) for h in range(NH)]


def _first(b, i):
    return jnp.logical_and(b == 0, i == 0)


def _acc(ref, val, first):
    @pl.when(first)
    def _():
        ref[...] = val

    @pl.when(jnp.logical_not(first))
    def _():
        ref[...] += val


def _ada_fwd(c_all, w_sh, b_sh):
    nb = c_all.shape[0]
    tn = 768

    def body(c_ref, w_ref, b_ref, o_ref):
        cv = c_ref[...]
        cs = (cv * _sigmoid(cv)).astype(BF16)
        o_ref[...] = _dot(cs, w_ref[...].astype(BF16)) + b_ref[...]

    return pl.pallas_call(
        body, name="ada_fwd", grid=(ADA_SH // tn,),
        out_shape=jax.ShapeDtypeStruct((nb, ADA_SH), F32),
        in_specs=[pl.BlockSpec((nb, D), lambda j: (0, 0)), pl.BlockSpec((D, tn), lambda j: (0, j)),
                  pl.BlockSpec((1, tn), lambda j: (0, j))],
        out_specs=pl.BlockSpec((nb, tn), lambda j: (0, j)),
        compiler_params=_cparams(),
    )(c_all, w_sh, b_sh)


def _ada_bwd_adam(c_all, dada_sh, w, m, v):
    nb = c_all.shape[0]
    tn = 768

    def body(c_ref, d_ref, w_ref, m_ref, v_ref, g_out, d_out, m_out, v_out):
        cv = c_ref[...]
        cs = (cv * _sigmoid(cv)).astype(BF16)
        g = _dot_tn(cs, d_ref[...].astype(BF16))
        delta, m2, v2 = _adam(w_ref[...], g, m_ref[...], v_ref[...])
        g_out[...] = g
        d_out[...] = delta
        m_out[...] = m2
        v_out[...] = v2

    big = pl.BlockSpec((D, tn), lambda j: (0, j))
    shape = jax.ShapeDtypeStruct((D, ADA_SH), F32)
    return pl.pallas_call(
        body, name="ada_bwd_adam", grid=(ADA_SH // tn,),
        out_shape=[shape] * 4,
        in_specs=[pl.BlockSpec((nb, D), lambda j: (0, 0)), pl.BlockSpec((nb, tn), lambda j: (0, j)), big, big, big],
        out_specs=[big] * 4,
        compiler_params=_cparams(),
    )(c_all, dada_sh, w, m, v)


def _tok_specs(tm, width):
    return pl.BlockSpec((1, tm, width), lambda b, i: (b, i, 0))


def _mod_spec():
    return pl.BlockSpec((1, 1, D), lambda b, i: (b, 0, 0))


def _row_spec(width=D):
    return pl.BlockSpec((1, width), lambda b, i: (0, 0))


def _ffn_fwd(x, sh, sc, gt, g_pre, g_post, w_in4, w_out, target=None):
    nb, s, _ = x.shape
    tm = min(512, s)
    with_loss = target is not None

    def body(*refs):
        if with_loss:
            (x_ref, sh_ref, sc_ref, gt_ref, gpre_ref, gpost_ref, win_ref, wout_ref, tgt_ref,
             xo_ref, df_ref, p_ref, ls_ref, dgpost_ref, dgt_ref) = refs
        else:
            (x_ref, sh_ref, sc_ref, gt_ref, gpre_ref, gpost_ref, win_ref, wout_ref,
             xo_ref, f_ref, p_ref) = refs
        xv = x_ref[0]
        n, _ = _rms(xv)
        h = (n * gpre_ref[...]) * (1.0 + sc_ref[0]) + sh_ref[0]
        hb = h.astype(BF16)
        acc = jnp.zeros((tm, D), F32)
        for j in range(2):
            gate = _dot(hb, win_ref[j])
            up = _dot(hb, win_ref[2 + j])
            p_ref[0, :, j * FBLK:(j + 1) * FBLK] = gate.astype(BF16)
            p_ref[0, :, DFF + j * FBLK:DFF + (j + 1) * FBLK] = up.astype(BF16)
            a = (gate * _sigmoid(gate)) * up
            acc = acc + _dot(a.astype(BF16), wout_ref[j * FBLK:(j + 1) * FBLK, :])
        nf, q = _rms(acc)
        gpost = gpost_ref[...]
        half_gate = 0.5 * gt_ref[0]
        out = xv + half_gate * (nf * gpost)
        if with_loss:
            first = _first(pl.program_id(0), pl.program_id(1))
            err = out - tgt_ref[0]
            dout = err * (1.0 / D)
            xo_ref[0] = dout
            row = jnp.sum(err * err, axis=0, keepdims=True)
            part = row[:, 0:128]
            for k in range(1, D // 128):
                part = part + row[:, k * 128:(k + 1) * 128]
            _acc(ls_ref, part, first)
            df, dgpost = _rms_bwd(dout * half_gate, nf, q, gpost)
            df_ref[0] = df.astype(BF16)
            _acc(dgpost_ref, dgpost, first)
            _acc(dgt_ref, jnp.sum(dout * (0.5 * (nf * gpost)), axis=0, keepdims=True)[None], pl.program_id(1) == 0)
        else:
            f_ref[0] = acc
            xo_ref[0] = out

    in_specs = [_tok_specs(tm, D), _mod_spec(), _mod_spec(), _mod_spec(), _row_spec(), _row_spec(), VMEM_FULL, VMEM_FULL]
    args = [x, sh, sc, gt, g_pre, g_post, w_in4, w_out]
    out_shape = [jax.ShapeDtypeStruct((nb, s, D), F32), jax.ShapeDtypeStruct((nb, s, D), BF16 if with_loss else F32),
                 jax.ShapeDtypeStruct((nb, s, 2 * DFF), BF16)]
    out_specs = [_tok_specs(tm, D), _tok_specs(tm, D), _tok_specs(tm, 2 * DFF)]
    if with_loss:
        in_specs.append(_tok_specs(tm, D))
        args.append(target)
        out_shape += [jax.ShapeDtypeStruct((1, 128), F32), jax.ShapeDtypeStruct((1, D), F32),
                      jax.ShapeDtypeStruct((nb, 1, D), F32)]
        out_specs += [pl.BlockSpec((1, 128), lambda b, i: (0, 0)), _row_spec(), _mod_spec()]
    return pl.pallas_call(
        body, name="ffn_loss_fwd" if with_loss else "ffn_fwd", grid=(nb, s // tm),
        out_shape=out_shape, in_specs=in_specs, out_specs=out_specs,
        compiler_params=_cparams(),
    )(*args)


def _ffn_up(x, sh, sc, g_pre, w_in4):
    nb, s, _ = x.shape
    tm = min(512, s)

    def body(x_ref, sh_ref, sc_ref, gpre_ref, win_ref, p_ref, a_ref):
        n, _ = _rms(x_ref[0])
        hb = ((n * gpre_ref[...]) * (1.0 + sc_ref[0]) + sh_ref[0]).astype(BF16)
        for j in range(2):
            gate = _dot(hb, win_ref[j])
            up = _dot(hb, win_ref[2 + j])
            p_ref[0, :, j * FBLK:(j + 1) * FBLK] = gate.astype(BF16)
            p_ref[0, :, DFF + j * FBLK:DFF + (j + 1) * FBLK] = up.astype(BF16)
            a_ref[0, :, j * FBLK:(j + 1) * FBLK] = ((gate * _sigmoid(gate)) * up).astype(BF16)

    return pl.pallas_call(
        body, name="ffn_up", grid=(nb, s // tm),
        out_shape=[jax.ShapeDtypeStruct((nb, s, 2 * DFF), BF16), jax.ShapeDtypeStruct((nb, s, DFF), BF16)],
        in_specs=[_tok_specs(tm, D), _mod_spec(), _mod_spec(), _row_spec(), VMEM_FULL],
        out_specs=[_tok_specs(tm, 2 * DFF), _tok_specs(tm, DFF)],
        compiler_params=_cparams(),
    )(x, sh, sc, g_pre, w_in4)


def _ffn_down(x, a, gt, g_post, w_out):
    nb, s, _ = x.shape
    tm = min(512, s)

    def body(x_ref, a_ref, gt_ref, gpost_ref, wout_ref, xo_ref, f_ref):
        acc = _dot(a_ref[0], wout_ref[...])
        f_ref[0] = acc
        nf, _ = _rms(acc)
        xo_ref[0] = x_ref[0] + (0.5 * gt_ref[0]) * (nf * gpost_ref[...])

    tok = _tok_specs(tm, D)
    shape = jax.ShapeDtypeStruct((nb, s, D), F32)
    return pl.pallas_call(
        body, name="ffn_down", grid=(nb, s // tm), out_shape=[shape, shape],
        in_specs=[tok, _tok_specs(tm, DFF), _mod_spec(), _row_spec(), VMEM_FULL],
        out_specs=[tok, tok],
        compiler_params=_cparams(),
    )(x, a, gt, g_post, w_out)


def _ffn_bwd(dxo, x, f, p, sh, sc, gt, g_pre, g_post, w_in4, w_out, df=None):
    nb, s, _ = x.shape
    tm = min(256, s)
    given = df is not None

    def body(*refs):
        if given:
            (dxo_ref, x_ref, dfin_ref, p_ref, sh_ref, sc_ref, gpre_ref, win_ref, wout_ref,
             dx_ref, dp_ref, h_ref, a_ref, dgpre_ref, dsh_ref, dsc_ref) = refs
        else:
            (dxo_ref, x_ref, f_ref, p_ref, sh_ref, sc_ref, gt_ref, gpre_ref, gpost_ref, win_ref, wout_ref,
             dx_ref, dp_ref, h_ref, a_ref, df_ref, dgpre_ref, dgpost_ref, dsh_ref, dsc_ref, dgt_ref) = refs
        b, i = pl.program_id(0), pl.program_id(1)
        dxo_v = dxo_ref[0]
        if given:
            dfb = dfin_ref[0]
        else:
            nf, q = _rms(f_ref[0])
            gpost = gpost_ref[...]
            dgt = jnp.sum(dxo_v * (0.5 * (nf * gpost)), axis=0, keepdims=True)
            do = dxo_v * (0.5 * gt_ref[0])
            dfv, dgpost = _rms_bwd(do, nf, q, gpost)
            dfb = dfv.astype(BF16)
            df_ref[0] = dfb
        xv = x_ref[0]
        n, r = _rms(xv)
        gpre = gpre_ref[...]
        ng = n * gpre
        scale1 = 1.0 + sc_ref[0]
        h = ng * scale1 + sh_ref[0]
        h_ref[0] = h.astype(BF16)
        dh = jnp.zeros((tm, D), F32)
        for j in range(2):
            gate = p_ref[0, :, j * FBLK:(j + 1) * FBLK].astype(F32)
            up = p_ref[0, :, DFF + j * FBLK:DFF + (j + 1) * FBLK].astype(F32)
            sg = _sigmoid(gate)
            act = gate * sg
            a_ref[0, :, j * FBLK:(j + 1) * FBLK] = (act * up).astype(BF16)
            da = _dot_nt(dfb, wout_ref[j * FBLK:(j + 1) * FBLK, :])
            dgate = (da * up * _dsilu(gate, sg)).astype(BF16)
            dup = (da * act).astype(BF16)
            dp_ref[0, :, j * FBLK:(j + 1) * FBLK] = dgate
            dp_ref[0, :, DFF + j * FBLK:DFF + (j + 1) * FBLK] = dup
            dh = dh + _dot_nt(dgate, win_ref[j]) + _dot_nt(dup, win_ref[2 + j])
        dsh = jnp.sum(dh, axis=0, keepdims=True)
        dsc = jnp.sum(dh * ng, axis=0, keepdims=True)
        dxn, dgpre = _rms_bwd(dh * scale1, n, r, gpre)
        dx_ref[0] = dxo_v + dxn
        _acc(dgpre_ref, dgpre, _first(b, i))
        _acc(dsh_ref, dsh[None], i == 0)
        _acc(dsc_ref, dsc[None], i == 0)
        if not given:
            _acc(dgpost_ref, dgpost, _first(b, i))
            _acc(dgt_ref, dgt[None], i == 0)

    tok = _tok_specs(tm, D)
    mod_shape = jax.ShapeDtypeStruct((nb, 1, D), F32)
    row_shape = jax.ShapeDtypeStruct((1, D), F32)
    big = [jax.ShapeDtypeStruct((nb, s, D), F32), jax.ShapeDtypeStruct((nb, s, 2 * DFF), BF16),
           jax.ShapeDtypeStruct((nb, s, D), BF16), jax.ShapeDtypeStruct((nb, s, DFF), BF16)]
    big_specs = [tok, _tok_specs(tm, 2 * DFF), tok, _tok_specs(tm, DFF)]
    if given:
        return pl.pallas_call(
            body, name="ffn_bwd_after_loss", grid=(nb, s // tm),
            out_shape=big + [row_shape, mod_shape, mod_shape],
            in_specs=[tok, tok, tok, _tok_specs(tm, 2 * DFF), _mod_spec(), _mod_spec(), _row_spec(), VMEM_FULL, VMEM_FULL],
            out_specs=big_specs + [_row_spec(), _mod_spec(), _mod_spec()],
            compiler_params=_cparams(),
        )(dxo, x, df, p, sh, sc, g_pre, w_in4, w_out)
    return pl.pallas_call(
        body, name="ffn_bwd", grid=(nb, s // tm),
        out_shape=big + [jax.ShapeDtypeStruct((nb, s, D), BF16), row_shape, row_shape, mod_shape, mod_shape, mod_shape],
        in_specs=[tok, tok, tok, _tok_specs(tm, 2 * DFF), _mod_spec(), _mod_spec(), _mod_spec(), _row_spec(), _row_spec(),
                  VMEM_FULL, VMEM_FULL],
        out_specs=big_specs + [tok, _row_spec(), _row_spec(), _mod_spec(), _mod_spec(), _mod_spec()],
        compiler_params=_cparams(),
    )(dxo, x, f, p, sh, sc, gt, g_pre, g_post, w_in4, w_out)


def _wgrad(name, a, b, col_block, chip_major):
    t, ka = a.shape
    n = b.shape[1]
    tk = min(t, 512)
    while tk * 2 <= t and t % (tk * 2) == 0 and 2 * (tk * 2) * max(ka, col_block) <= 6 * 1024 * 1024:
        tk *= 2
    nk = t // tk
    nblk = n // col_block

    def body(a_ref, b_ref, o_ref, obf_ref, acc_ref):
        k = pl.program_id(1)

        @pl.when(k == 0)
        def _():
            acc_ref[...] = jnp.zeros_like(acc_ref)

        acc_ref[...] += _dot_tn(a_ref[...], b_ref[...])

        @pl.when(k == nk - 1)
        def _():
            val = acc_ref[...]
            if chip_major:
                o_ref[0] = val
                obf_ref[0] = val.astype(BF16)
            else:
                o_ref[...] = val
                obf_ref[...] = val.astype(BF16)

    if chip_major:
        shape = (nblk, ka, col_block)
        ospec = pl.BlockSpec((1, ka, col_block), lambda j, k: (j, 0, 0))
    else:
        shape = (ka, n)
        ospec = pl.BlockSpec((ka, col_block), lambda j, k: (0, j))
    return pl.pallas_call(
        body, name=name, grid=(nblk, nk),
        out_shape=[jax.ShapeDtypeStruct(shape, F32), jax.ShapeDtypeStruct(shape, BF16)],
        in_specs=[pl.BlockSpec((tk, ka), lambda j, k: (k, 0)), pl.BlockSpec((tk, col_block), lambda j, k: (k, j))],
        out_specs=[ospec, ospec],
        scratch_shapes=[pltpu.VMEM((ka, col_block), F32)],
        compiler_params=_cparams(),
    )(a, b)


def _mix_in_fwd(x, sh, sc, g_pre, w_mi4):
    nb, s, _ = x.shape
    tm = min(512, s)

    def body(x_ref, sh_ref, sc_ref, gpre_ref, w_ref, u_ref, v_ref, a_ref, g_ref):
        n, _ = _rms(x_ref[0])
        hb = ((n * gpre_ref[...]) * (1.0 + sc_ref[0]) + sh_ref[0]).astype(BF16)
        for k, o_ref in enumerate((u_ref, v_ref, a_ref, g_ref)):
            o_ref[0] = _dot(hb, w_ref[k])

    shape = jax.ShapeDtypeStruct((nb, s, WA), F32)
    return pl.pallas_call(
        body, name="mix_in_fwd", grid=(nb, s // tm),
        out_shape=[shape] * 4,
        in_specs=[_tok_specs(tm, D), _mod_spec(), _mod_spec(), _row_spec(), VMEM_FULL],
        out_specs=[_tok_specs(tm, WA)] * 4,
        compiler_params=_cparams(),
    )(x, sh, sc, g_pre, w_mi4)


def _spatial_weights(wcat_ref, transposed):
    w = wcat_ref[...]
    row = lax.broadcasted_iota(jnp.int32, w.shape, 0)
    col = lax.broadcasted_iota(jnp.int32, w.shape, 1)
    keep = ((row & (CH - 1)) <= col) if transposed else ((col & (CH - 1)) <= row)
    return jnp.where(keep, w, 0.0).astype(BF16)


def _expand_heads(vc, masks):
    return jnp.concatenate([jnp.where(mk, vc, jnp.zeros_like(vc)) for mk in masks], axis=0)


def _spatial_bias(bspt_ref):
    return bspt_ref[...]


SHIFTS = 8
TAP_ROWS = 32


def _ext_rows(tm):
    return tm + HALO + SHIFTS


def _make_shifts(ext_ref, sh_ref, tm):
    ext_ref[tm + HALO:tm + HALO + SHIFTS, :] = jnp.zeros((SHIFTS, WB), F32)
    for r in range(SHIFTS):
        sh_ref[r] = ext_ref[r:r + tm + HALO, :]


def _conv_taps(sh_ref, w_ref, tm, taps, emit):
    def block(i, carry):
        r0 = pl.multiple_of(i * TAP_ROWS, TAP_ROWS)
        acc = jnp.zeros((TAP_ROWS, WB), F32)
        for o, k in taps:
            acc = acc + w_ref[k:k + 1, :] * sh_ref[o % SHIFTS, pl.ds(r0 + SHIFTS * (o // SHIFTS), TAP_ROWS), :]
        emit(r0, acc)
        return carry

    lax.fori_loop(0, tm // TAP_ROWS, block, 0)


def _halo_prev_spec(tm):
    return pl.BlockSpec((1, HALO, WB), lambda b, i: (b, jnp.maximum(i * (tm // HALO) - 1, 0), 0))


def _halo_next_spec(tm, s):
    return pl.BlockSpec((1, HALO, WB), lambda b, i: (b, jnp.minimum((i + 1) * (tm // HALO), s // HALO - 1), 0))


def _mix_mid_fwd(x, u, v, a, g, gt, gn_g, gn_b, wcat, bspt, conv_w, conv_b, cn_g, cn_b, go_a, go_b, w_mo, g_post):
    nb, s, _ = x.shape
    tm = min(512, s)

    def body(x_ref, u_ref, v_ref, a_ref, g_ref, ah_ref, gh_ref, gt_ref, gng_ref, gnb_ref, wcat_ref, bspt_ref,
             cw_ref, cb_ref, cng_ref, cnb_ref, goa_ref, gob_ref, wmo_ref, gpost_ref,
             xo_ref, conv_ref, y_ref, m_ref, ext_ref, sh_ref):
        i = pl.program_id(1)
        xhat, _ = _ln(v_ref[0])
        vb = (xhat * gng_ref[...] + gnb_ref[...]).astype(BF16)
        wsb = _spatial_weights(wcat_ref, False)
        bias = _spatial_bias(bspt_ref)
        masks = _head_mask((CH, WA))
        zs = []
        for cidx in range(tm // CH):
            vexp = _expand_heads(vb[cidx * CH:(cidx + 1) * CH, :], masks)
            zs.append(_dot(wsb, vexp) + bias)
        z = jnp.concatenate(zs, axis=0)
        na, _ = _rms(u_ref[0] * z)
        keep = jnp.where(i == 0, 0.0, 1.0).astype(F32)
        ext_ref[0:HALO, :] = (ah_ref[0] * _sigmoid(gh_ref[0])) * keep
        ext_ref[HALO:HALO + tm, :] = a_ref[0] * _sigmoid(g_ref[0])
        _make_shifts(ext_ref, sh_ref, tm)
        cb = cb_ref[...]

        def put_conv(r0, acc):
            conv_ref[0, pl.ds(r0, TAP_ROWS), :] = acc + cb

        _conv_taps(sh_ref, cw_ref, tm, [(k + HALO - (CK - 1), k) for k in range(CK)], put_conv)
        conv = conv_ref[0]
        chat, _ = _ln(conv)
        cln = chat * cng_ref[...] + cnb_ref[...]
        nbb, _ = _rms(cln * _sigmoid(cln))
        yb = jnp.concatenate([na * goa_ref[...], nbb * gob_ref[...]], axis=1).astype(BF16)
        y_ref[0] = yb
        m = _dot(yb, wmo_ref[...])
        m_ref[0] = m
        nm, _ = _rms(m)
        xo_ref[0] = x_ref[0] + gt_ref[0] * (nm * gpost_ref[...])

    t5 = _tok_specs(tm, WA)
    tok = _tok_specs(tm, D)
    r5 = _row_spec(WA)
    full = lambda shape: pl.BlockSpec(shape, lambda b, i: (0,) * len(shape))
    return pl.pallas_call(
        body, name="mix_mid_fwd", grid=(nb, s // tm),
        out_shape=[jax.ShapeDtypeStruct((nb, s, D), F32), jax.ShapeDtypeStruct((nb, s, WB), F32),
                   jax.ShapeDtypeStruct((nb, s, D), BF16), jax.ShapeDtypeStruct((nb, s, D), F32)],
        in_specs=[tok, t5, t5, t5, t5, _halo_prev_spec(tm), _halo_prev_spec(tm), _mod_spec(), r5, r5,
                  full((CH, NH * CH)), full((CH, WA)), full((HALO, WB)), r5, r5, r5, r5, r5, VMEM_FULL, _row_spec()],
        out_specs=[tok, t5, tok, tok],
        scratch_shapes=[pltpu.VMEM((_ext_rows(tm), WB), F32), pltpu.VMEM((SHIFTS, tm + HALO, WB), F32)],
        compiler_params=_cparams(),
    )(x, u, v, a, g, a, g, gt, gn_g, gn_b, wcat, bspt, conv_w, conv_b, cn_g, cn_b, go_a, go_b, w_mo, g_post)


def _mix_out_bwd(dxo, m, gt, g_post, w_mo):
    nb, s, _ = m.shape
    tm = min(512, s)

    def body(dxo_ref, m_ref, gt_ref, gpost_ref, wmo_ref, dy_ref, dm_ref, dgpost_ref, dgt_ref):
        b, i = pl.program_id(0), pl.program_id(1)
        dxo_v = dxo_ref[0]
        nm, q = _rms(m_ref[0])
        gpost = gpost_ref[...]
        dgt = jnp.sum(dxo_v * (nm * gpost), axis=0, keepdims=True)
        dm, dgpost = _rms_bwd(dxo_v * gt_ref[0], nm, q, gpost)
        dmb = dm.astype(BF16)
        dm_ref[0] = dmb
        dy_ref[0] = _dot_nt(dmb, wmo_ref[...])
        _acc(dgpost_ref, dgpost, _first(b, i))
        _acc(dgt_ref, dgt[None], i == 0)

    tok = _tok_specs(tm, D)
    return pl.pallas_call(
        body, name="mix_out_bwd", grid=(nb, s // tm),
        out_shape=[jax.ShapeDtypeStruct((nb, s, D), F32), jax.ShapeDtypeStruct((nb, s, D), BF16),
                   jax.ShapeDtypeStruct((1, D), F32), jax.ShapeDtypeStruct((nb, 1, D), F32)],
        in_specs=[tok, tok, _mod_spec(), _row_spec(), VMEM_FULL],
        out_specs=[tok, tok, _row_spec(), _mod_spec()],
        compiler_params=_cparams(),
    )(dxo, m, gt, g_post, w_mo)


def _mix_mid_bwd(dy, u, v, conv, gn_g, gn_b, wcat, wcat_t, bspt, cn_g, cn_b, go_a, go_b):
    nb, s, _ = dy.shape
    tm = min(512, s)
    nchunk = tm // CH

    def body(dy_ref, u_ref, v_ref, conv_ref, gng_ref, gnb_ref, wcat_ref, wcatt_ref, bspt_ref, cng_ref, cnb_ref,
             goa_ref, gob_ref,
             du_ref, dv_ref, dconv_ref, dwcat_ref, dbsp_ref, dgng_ref, dgnb_ref, dgoa_ref, dgob_ref,
             dcng_ref, dcnb_ref, dcb_ref):
        first = _first(pl.program_id(0), pl.program_id(1))
        dyv = dy_ref[0]
        xhat, rstd = _ln(v_ref[0])
        gng = gng_ref[...]
        vb = (xhat * gng + gnb_ref[...]).astype(BF16)
        wsb = _spatial_weights(wcat_ref, False)
        wsb_t = _spatial_weights(wcatt_ref, True)
        bias = _spatial_bias(bspt_ref)
        masks = _head_mask((CH, WA))
        vexps, zs = [], []
        for cidx in range(nchunk):
            vexp = _expand_heads(vb[cidx * CH:(cidx + 1) * CH, :], masks)
            vexps.append(vexp)
            zs.append(_dot(wsb, vexp) + bias)
        z = jnp.concatenate(zs, axis=0)
        uv = u_ref[0]
        na, ra = _rms(uv * z)
        dya, dgoa = _rms_bwd(dyv[:, 0:WA], na, ra, goa_ref[...])
        du_ref[0] = dya * z
        dz = dya * uv
        dwcat = jnp.zeros((CH, NH * CH), F32)
        dzsum = jnp.zeros((CH, WA), F32)
        dvlns = []
        for cidx in range(nchunk):
            dzc = dz[cidx * CH:(cidx + 1) * CH, :]
            dzsum = dzsum + dzc
            dzb = dzc.astype(BF16)
            dwcat = dwcat + _dot_nt(dzb, vexps[cidx])
            dvexp = _dot(wsb_t, dzb)
            dvl = jnp.zeros((CH, WA), F32)
            for h in range(NH):
                dvl = dvl + jnp.where(masks[h], dvexp[h * CH:(h + 1) * CH, :], 0.0)
            dvlns.append(dvl)
        dvln = jnp.concatenate(dvlns, axis=0)
        dv, dgng, dgnb = _ln_bwd(dvln, xhat, rstd, gng)
        dv_ref[0] = dv
        lane = lax.broadcasted_iota(jnp.int32, (NH, WA), 1)
        head = lax.broadcasted_iota(jnp.int32, (NH, WA), 0)
        sel = jnp.where((lane >= head * HD) & (lane < (head + 1) * HD), 1.0, 0.0).astype(F32)
        dbsp = lax.dot_general(sel, dzsum, NT, preferred_element_type=F32, precision=lax.Precision.HIGHEST)
        chat, crstd = _ln(conv_ref[0])
        cng = cng_ref[...]
        cln = chat * cng + cnb_ref[...]
        sg = _sigmoid(cln)
        nbb, rb = _rms(cln * sg)
        dyb, dgob = _rms_bwd(dyv[:, WA:D], nbb, rb, gob_ref[...])
        dconv, dcng, dcnb = _ln_bwd(dyb * _dsilu(cln, sg), chat, crstd, cng)
        dconv_ref[0] = dconv
        dcb = jnp.sum(dconv, axis=0, keepdims=True)
        for ref, val in ((dwcat_ref, dwcat), (dbsp_ref, dbsp), (dgng_ref, dgng), (dgnb_ref, dgnb), (dgoa_ref, dgoa),
                         (dgob_ref, dgob), (dcng_ref, dcng), (dcnb_ref, dcnb), (dcb_ref, dcb)):
            _acc(ref, val, first)

    t5 = _tok_specs(tm, WA)
    r5 = _row_spec(WA)
    full = lambda shape: pl.BlockSpec(shape, lambda b, i: (0,) * len(shape))
    big = jax.ShapeDtypeStruct((nb, s, WA), F32)
    row = jax.ShapeDtypeStruct((1, WA), F32)
    return pl.pallas_call(
        body, name="mix_mid_bwd", grid=(nb, s // tm),
        out_shape=[big, big, big, jax.ShapeDtypeStruct((CH, NH * CH), F32), jax.ShapeDtypeStruct((NH, CH), F32),
                   row, row, row, row, row, row, row],
        in_specs=[_tok_specs(tm, D), t5, t5, t5, r5, r5, full((CH, NH * CH)), full((NH * CH, CH)), full((CH, WA)),
                  r5, r5, r5, r5],
        out_specs=[t5, t5, t5, full((CH, NH * CH)), full((NH, CH)), r5, r5, r5, r5, r5, r5, r5],
        compiler_params=_cparams(),
    )(dy, u, v, conv, gn_g, gn_b, wcat, wcat_t, bspt, cn_g, cn_b, go_a, go_b)


def _mix_in_bwd(dxo, x, du, dv, dconv, a, g, sh, sc, g_pre, w_mi4, conv_w):
    nb, s, _ = x.shape
    tm = min(512, s)
    n_i = s // tm

    def body(dxo_ref, x_ref, du_ref, dv_ref, dc_ref, dch_ref, a_ref, g_ref, ah_ref, gh_ref, sh_ref, sc_ref,
             gpre_ref, w_ref, cw_ref,
             dx_ref, dproj_ref, h_ref, dgpre_ref, dsh_ref, dsc_ref, dcw_ref, ext_ref, shf_ref, dglu_ref):
        b, i = pl.program_id(0), pl.program_id(1)
        first = _first(b, i)
        av, gv = a_ref[0], g_ref[0]
        sg = _sigmoid(gv)
        dconv = dc_ref[0]
        ext_ref[0:tm, :] = dconv
        ext_ref[tm:tm + HALO, :] = dch_ref[0] * jnp.where(i == n_i - 1, 0.0, 1.0).astype(F32)
        _make_shifts(ext_ref, shf_ref, tm)

        def put_dglu(r0, acc):
            dglu_ref[pl.ds(r0, TAP_ROWS), :] = acc

        _conv_taps(shf_ref, cw_ref, tm, [(CK - 1 - k, k) for k in range(CK)], put_dglu)
        dglu = dglu_ref[...]
        ext_ref[0:HALO, :] = (ah_ref[0] * _sigmoid(gh_ref[0])) * jnp.where(i == 0, 0.0, 1.0).astype(F32)
        ext_ref[HALO:HALO + tm, :] = av * sg
        _make_shifts(ext_ref, shf_ref, tm)

        @pl.when(first)
        def _():
            dcw_ref[...] = jnp.zeros((HALO, WB), F32)

        for k in range(CK):
            o = k + HALO - (CK - 1)
            lo = SHIFTS * (o // SHIFTS)
            dcw_ref[k:k + 1, :] += jnp.sum(dconv * shf_ref[o % SHIFTS, lo:lo + tm, :], axis=0, keepdims=True)
        da = dglu * sg
        dg = dglu * av * (sg * (1.0 - sg))
        parts = [du_ref[0].astype(BF16), dv_ref[0].astype(BF16), da.astype(BF16), dg.astype(BF16)]
        dh = jnp.zeros((tm, D), F32)
        for k in range(4):
            dproj_ref[0, :, k * WA:(k + 1) * WA] = parts[k]
            dh = dh + _dot_nt(parts[k], w_ref[k])
        n, r = _rms(x_ref[0])
        gpre = gpre_ref[...]
        ng = n * gpre
        scale1 = 1.0 + sc_ref[0]
        h_ref[0] = (ng * scale1 + sh_ref[0]).astype(BF16)
        dsh = jnp.sum(dh, axis=0, keepdims=True)
        dsc = jnp.sum(dh * ng, axis=0, keepdims=True)
        dxn, dgpre = _rms_bwd(dh * scale1, n, r, gpre)
        dx_ref[0] = dxo_ref[0] + dxn
        _acc(dgpre_ref, dgpre, first)
        _acc(dsh_ref, dsh[None], i == 0)
        _acc(dsc_ref, dsc[None], i == 0)

    tok = _tok_specs(tm, D)
    t5 = _tok_specs(tm, WA)
    full = lambda shape: pl.BlockSpec(shape, lambda b, i: (0,) * len(shape))
    mod_shape = jax.ShapeDtypeStruct((nb, 1, D), F32)
    return pl.pallas_call(
        body, name="mix_in_bwd", grid=(nb, n_i),
        out_shape=[jax.ShapeDtypeStruct((nb, s, D), F32), jax.ShapeDtypeStruct((nb, s, 4 * WA), BF16),
                   jax.ShapeDtypeStruct((nb, s, D), BF16), jax.ShapeDtypeStruct((1, D), F32), mod_shape, mod_shape,
                   jax.ShapeDtypeStruct((HALO, WB), F32)],
        in_specs=[tok, tok, t5, t5, t5, _halo_next_spec(tm, s), t5, t5, _halo_prev_spec(tm), _halo_prev_spec(tm),
                  _mod_spec(), _mod_spec(), _row_spec(), VMEM_FULL, full((HALO, WB))],
        out_specs=[tok, _tok_specs(tm, 4 * WA), tok, _row_spec(), _mod_spec(), _mod_spec(), full((HALO, WB))],
        scratch_shapes=[pltpu.VMEM((_ext_rows(tm), WB), F32), pltpu.VMEM((SHIFTS, tm + HALO, WB), F32),
                        pltpu.VMEM((tm, WB), F32)],
        compiler_params=_cparams(),
    )(dxo, x, du, dv, dconv, dconv, a, g, a, g, sh, sc, g_pre, w_mi4, conv_w)


def _row_tile(rows, cols):
    best = 16
    for t in range(16, rows + 1, 16):
        if rows % t == 0 and t * cols * 4 <= 1536 * 1024:
            best = t
    return best


def _sum4(name, own4, recv, j_arr):
    _, rows, cols = own4.shape
    tr = _row_tile(rows, cols)

    def body(j_ref, own_ref, recv_ref, o_ref):
        del j_ref
        acc = own_ref[0]
        for k in range(3):
            acc = acc + recv_ref[k].astype(F32)
        o_ref[...] = acc

    return pl.pallas_call(
        body, name=name,
        grid_spec=pltpu.PrefetchScalarGridSpec(
            num_scalar_prefetch=1, grid=(rows // tr,),
            in_specs=[pl.BlockSpec((1, tr, cols), lambda i, j: (j[0], i, 0)),
                      pl.BlockSpec((3, tr, cols), lambda i, j: (0, i, 0))],
            out_specs=pl.BlockSpec((tr, cols), lambda i, j: (i, 0))),
        out_shape=jax.ShapeDtypeStruct((rows, cols), F32),
        compiler_params=_cparams(),
    )(j_arr, own4, recv)


def _pair_plan(shapes):
    def plan(x, y, c, src, land):
        sends = []
        for a, shape in enumerate(shapes):
            rows = shape[1] // 2
            theirs = pl.ds(pl.multiple_of((1 - c) * rows, 16), rows)
            sends.append((src[a].at[:, theirs], land[a], (x, y, 1 - c), land[a]))
        return [], sends

    return plan


def _swap_plan(n):
    def plan(x, y, c, src, land):
        return [], [(src[a], land[a], (x, y, 1 - c), land[a]) for a in range(n)]

    return plan


def _pair_sum(name, g32, recv, c_arr):
    nblk, rows, cols = recv.shape
    tr = _row_tile(rows, cols)
    nh = rows // tr

    def body(c_ref, g_ref, r_ref, o32_ref, obf_ref):
        del c_ref
        val = g_ref[0] + r_ref[0].astype(F32)
        o32_ref[0] = val
        obf_ref[0] = val.astype(BF16)

    spec = pl.BlockSpec((1, tr, cols), lambda k, i, c: (k, i, 0))
    return pl.pallas_call(
        body, name=name,
        grid_spec=pltpu.PrefetchScalarGridSpec(
            num_scalar_prefetch=1, grid=(nblk, nh),
            in_specs=[pl.BlockSpec((1, tr, cols), lambda k, i, c: (k, c[0] * nh + i, 0)), spec],
            out_specs=[spec, spec]),
        out_shape=[jax.ShapeDtypeStruct(recv.shape, F32), jax.ShapeDtypeStruct(recv.shape, BF16)],
        compiler_params=_cparams(),
    )(c_arr, g32, recv)


def _adam_halves(name, w, m, v, mine, theirs, c_arr):
    rows, cols = w.shape
    tr = _row_tile(rows // 2, cols)
    nh = (rows // 2) // tr

    def body(c_ref, w_ref, m_ref, v_ref, mine_ref, theirs_ref, g_out, d_out, m_out, v_out):
        here = (pl.program_id(0) // nh) == c_ref[0]
        g = jnp.where(here, mine_ref[...], theirs_ref[...])
        delta, m2, v2 = _adam(w_ref[...], g, m_ref[...], v_ref[...])
        g_out[...] = g
        d_out[...] = delta
        m_out[...] = m2
        v_out[...] = v2

    spec = pl.BlockSpec((tr, cols), lambda i, c: (i, 0))
    shape = jax.ShapeDtypeStruct((rows, cols), F32)
    return pl.pallas_call(
        body, name=name,
        grid_spec=pltpu.PrefetchScalarGridSpec(
            num_scalar_prefetch=1, grid=(2 * nh,),
            in_specs=[spec, spec, spec,
                      pl.BlockSpec((tr, cols), lambda i, c: (jnp.clip(i - c[0] * nh, 0, nh - 1), 0)),
                      pl.BlockSpec((tr, cols), lambda i, c: (jnp.clip(i - (1 - c[0]) * nh, 0, nh - 1), 0))],
            out_specs=[spec] * 4),
        out_shape=[shape] * 4,
        compiler_params=_cparams(),
    )(c_arr, w, m, v, mine, theirs)


def _adam_big(name, w, m, v, ga, gb):
    rows, cols = w.shape
    tr = _row_tile(rows, cols)

    def body(w_ref, m_ref, v_ref, ga_ref, gb_ref, g_out, d_out, m_out, v_out):
        gsum = ga_ref[...] + gb_ref[...]
        delta, m2, v2 = _adam(w_ref[...], gsum, m_ref[...], v_ref[...])
        g_out[...] = gsum
        d_out[...] = delta
        m_out[...] = m2
        v_out[...] = v2

    spec = pl.BlockSpec((tr, cols), lambda i: (i, 0))
    shape = jax.ShapeDtypeStruct((rows, cols), F32)
    return pl.pallas_call(
        body, name=name, grid=(rows // tr,), out_shape=[shape] * 4,
        in_specs=[spec] * 5, out_specs=[spec] * 4, compiler_params=_cparams(),
    )(w, m, v, ga, gb)


PK_VEC = 0
PK_LOSS = 6
PK_PAIR = 8
PK_BSP = 16
PK_WCAT = 24
PK_ROWS = PK_WCAT + CH
PAIR_ORDER = ("gmlp_norm_g", "gmlp_norm_b", "conv_b", "conv_norm_g", "conv_norm_b", "g_out_a", "g_out_b")
VEC_ORDER = ("g_pre_f1", "g_post_f1", "g_pre_m", "g_post_m", "g_pre_f2", "g_post_f2")


def _pack_late(rows):
    counts = [r.shape[0] for r in rows]
    assert sum(counts) == 8

    def body(*refs):
        o_ref = refs[-1]
        at = 0
        for r, cnt in zip(refs[:-1], counts):
            o_ref[at:at + cnt, :] = r[...]
            at += cnt

    return pl.pallas_call(
        body, name="pack_late", out_shape=jax.ShapeDtypeStruct((8, D), F32),
        in_specs=[VMEM_FULL] * len(rows), out_specs=VMEM_FULL, compiler_params=_cparams(),
    )(*rows)


def _pack_small(vecs, pairs, dbsp, dwcat, lsum):
    def body(*refs):
        vec_refs = refs[:4]
        pair_refs = refs[4:11]
        dbsp_ref, dwcat_ref, lsum_ref, o_ref = refs[11:]
        o_ref[0:PK_WCAT, :] = jnp.zeros((PK_WCAT, D), F32)
        o_ref[PK_LOSS:PK_LOSS + 1, 0:128] = lsum_ref[...]
        for k, r in enumerate(vec_refs):
            o_ref[PK_VEC + 2 + k:PK_VEC + 3 + k, :] = r[...]
        for k, r in enumerate(pair_refs):
            row, half = PK_PAIR + k // 2, k % 2
            o_ref[row:row + 1, half * WA:(half + 1) * WA] = r[...]
        o_ref[PK_BSP:PK_BSP + NH, 0:CH] = dbsp_ref[...]
        o_ref[PK_WCAT:PK_ROWS, :] = dwcat_ref[...]

    args = list(vecs) + list(pairs) + [dbsp, dwcat, lsum]
    return pl.pallas_call(
        body, name="pack_small", out_shape=jax.ShapeDtypeStruct((PK_ROWS, D), F32),
        in_specs=[VMEM_FULL] * len(args), out_specs=VMEM_FULL, compiler_params=_cparams(),
    )(*args)


def _small_adam(pack_all, late_all, dcw_all, dada_all, params, behind):
    names = list(VEC_ORDER) + list(PAIR_ORDER) + ["b_spatial", "w_spatial", "conv_w", "b_ada"]
    flat = []
    for nm in names:
        flat += list(params[nm])
    n_in = 4 + len(flat)

    def body(*refs):
        pack_ref, late_ref, dcw_ref, dada_ref = refs[:4]
        prm = refs[4:n_in]
        outs = refs[n_in + 1:]

        def total(r0, nr, c0, nc):
            acc = pack_ref[0, r0:r0 + nr, c0:c0 + nc]
            for d in range(1, NDEV):
                acc = acc + pack_ref[d, r0:r0 + nr, c0:c0 + nc]
            return acc

        def emit(idx, g, getw, put):
            w_ref, m_ref, v_ref = prm[3 * idx:3 * idx + 3]
            delta, m2, v2 = _adam(getw(w_ref), g, getw(m_ref), getw(v_ref))
            for o_ref, val in zip(outs[4 * idx:4 * idx + 4], (g, delta, m2, v2)):
                put(o_ref, val)

        def whole(ref):
            return ref[...]

        def put_whole(ref, val):
            ref[...] = val

        idx = 0
        for k in range(6):
            if k < 2:
                g = late_ref[0, k:k + 1, :]
                for d in range(1, NDEV):
                    g = g + late_ref[d, k:k + 1, :]
            else:
                g = total(PK_VEC + k, 1, 0, D)
            emit(idx, g, whole, put_whole)
            idx += 1
        for k in range(7):
            emit(idx, total(PK_PAIR + k // 2, 1, (k % 2) * WA, WA), whole, put_whole)
            idx += 1
        emit(idx, total(PK_BSP, NH, 0, CH), lambda r: r[0], lambda r, val: r.__setitem__(0, val))
        idx += 1
        row = lax.broadcasted_iota(jnp.int32, (CH, CH), 0)
        col = lax.broadcasted_iota(jnp.int32, (CH, CH), 1)
        for h in range(NH):
            gh = jnp.where(col <= row, total(PK_WCAT, CH, h * CH, CH), 0.0)
            w_ref, m_ref, v_ref = prm[3 * idx:3 * idx + 3]
            delta, m2, v2 = _adam(w_ref[0, h], gh, m_ref[0, h], v_ref[0, h])
            for o_ref, val in zip(outs[4 * idx:4 * idx + 4], (gh, delta, m2, v2)):
                o_ref[0, h] = val
        idx += 1
        gcw = dcw_ref[0, 0:CK, :]
        for d in range(1, NDEV):
            gcw = gcw + dcw_ref[d, 0:CK, :]
        emit(idx, gcw, lambda r: r[0], lambda r, val: r.__setitem__(0, val))
        idx += 1
        emit(idx, jnp.sum(dada_ref[...], axis=0, keepdims=True), whole, put_whole)
        outs[-1][...] = jnp.sum(total(PK_LOSS, 1, 0, 128), axis=1, keepdims=True) * (0.5 / D)

    out_shape = []
    for nm in names:
        w = params[nm][0]
        out_shape += [jax.ShapeDtypeStruct(w.shape, F32)] * 4
    out_shape.append(jax.ShapeDtypeStruct((1, 1), F32))
    res = pl.pallas_call(
        body, name="small_adam", out_shape=out_shape,
        in_specs=[VMEM_FULL] * n_in + [ANY], out_specs=[VMEM_FULL] * len(out_shape), compiler_params=_cparams(),
    )(pack_all, late_all, dcw_all, dada_all, *flat, behind)
    return {nm: tuple(res[4 * k:4 * k + 4]) for k, nm in enumerate(names)}, res[-1].reshape(())


WEIGHTS = ['w_ada', 'b_ada', 'g_pre_f1', 'g_post_f1', 'w_f1_in', 'w_f1_out', 'g_pre_m', 'g_post_m', 'w_mix_in',
           'gmlp_norm_g', 'gmlp_norm_b', 'w_spatial', 'b_spatial', 'conv_w', 'conv_b', 'conv_norm_g', 'conv_norm_b',
           'g_out_a', 'g_out_b', 'w_mix_out', 'g_pre_f2', 'g_post_f2', 'w_f2_in', 'w_f2_out']
BIG = ('w_f1_in', 'w_f1_out', 'w_mix_in', 'w_mix_out', 'w_f2_in', 'w_f2_out')


def kernel(x, c, w_ada, b_ada, g_pre_f1, g_post_f1, w_f1_in, w_f1_out, g_pre_m, g_post_m, w_mix_in, gmlp_norm_g, gmlp_norm_b, w_spatial, b_spatial, conv_w, conv_b, conv_norm_g, conv_norm_b, g_out_a, g_out_b, w_mix_out, g_pre_f2, g_post_f2, w_f2_in, w_f2_out, loss_target, m_w_ada, m_b_ada, m_g_pre_f1, m_g_post_f1, m_w_f1_in, m_w_f1_out, m_g_pre_m, m_g_post_m, m_w_mix_in, m_gmlp_norm_g, m_gmlp_norm_b, m_w_spatial, m_b_spatial, m_conv_w, m_conv_b, m_conv_norm_g, m_conv_norm_b, m_g_out_a, m_g_out_b, m_w_mix_out, m_g_pre_f2, m_g_post_f2, m_w_f2_in, m_w_f2_out, v_w_ada, v_b_ada, v_g_pre_f1, v_g_post_f1, v_w_f1_in, v_w_f1_out, v_g_pre_m, v_g_post_m, v_w_mix_in, v_gmlp_norm_g, v_gmlp_norm_b, v_w_spatial, v_b_spatial, v_conv_w, v_conv_b, v_conv_norm_g, v_conv_norm_b, v_g_out_a, v_g_out_b, v_w_mix_out, v_g_pre_f2, v_g_post_f2, v_w_f2_in, v_w_f2_out):
    env = dict(locals())
    wts = {n: env[n] for n in WEIGHTS}
    mom = {n: env["m_" + n] for n in WEIGHTS}
    var = {n: env["v_" + n] for n in WEIGHTS}
    nb, s, _ = x.shape
    t = nb * s
    ax, ay, ac = lax.axis_index("x"), lax.axis_index("y"), lax.axis_index("c")
    j_chip = 2 * ax + ay
    dev = 4 * ax + 2 * ay + ac
    j_arr = j_chip.reshape(1).astype(jnp.int32)

    groups = (("w_f1_in",), ("w_mix_in", "w_mix_out"), ("w_f2_in", "w_f2_out"), ("w_f1_out",))
    def gather_operands(gi):
        srcs = [wts[n][0].astype(BF16) for n in groups[gi]] + ([conv_w[0]] if gi == 1 else [])
        lands = [lax.dynamic_update_index_in_dim(lax.empty((NCHIP,) + a.shape, a.dtype), a, j_chip, 0) for a in srcs]
        return srcs, lands

    def gather_start(gi, behind, operands=None):
        srcs, lands = operands or gather_operands(gi)
        plan_a, plan_b, n_b = _gather_plans([a.shape for a in srcs])
        ssem, rsem, srcs, lands, token = _split_start("gw_start%d" % gi, srcs, lands, plan_a, 3 * len(srcs), behind)
        gather[gi] = (srcs, lands, ssem, rsem, plan_a, plan_b, n_b)
        return token

    def gather_forward(gi, behind):
        srcs, lands, ssem, rsem, plan_a, plan_b, n_b = gather[gi]
        ssem, rsem, lands, token = _split_forward("gw_fwd%d" % gi, srcs, lands, ssem, rsem, plan_a, plan_b, n_b, behind)
        gather[gi] = (lands, ssem, rsem, plan_b)
        return token

    def gathered(gi, behind):
        lands, ssem, rsem, plan_b = gather[gi]
        return _split_wait("gw_wait%d" % gi, [], lands, ssem, rsem, plan_b, behind)

    gather = {}
    (c_all8,) = _allgather8("gather_c", [c.reshape(8, (nb * D) // 8)])
    token = gather_start(0, c_all8)
    c_all = c_all8.reshape(NDEV * nb, D) + token[0, 0]
    b_sh = lax.dynamic_slice(b_ada, (0, j_chip * ADA_SH), (1, ADA_SH))
    ada_sh = _ada_fwd(c_all, w_ada[0], b_sh)
    later = [gather_operands(3), gather_operands(1), gather_operands(2)]
    (ada4,) = _chip_allgather("gather_ada", [ada_sh], behind=[a for pair in later for arrs in pair for a in arrs])
    token = gather_forward(0, ada4)
    token = gather_start(3, token, later[0])
    token = gather_start(1, token, later[1])
    token = gather_start(2, token, later[2])
    ada_me = lax.dynamic_slice(ada4, (0, dev * nb, 0), (NCHIP, nb, ADA_SH))
    ada_me = jnp.transpose(ada_me, (1, 0, 2)).reshape(nb, NMOD * D)
    sh1, sc1, gt1, sh2, sc2, gt2, sh3, sc3, gt3 = [ada_me[:, k * D:(k + 1) * D].reshape(nb, 1, D) for k in range(NMOD)]

    wcat = jnp.transpose(w_spatial[0], (1, 0, 2)).reshape(CH, NH * CH)
    wcat_t = jnp.transpose(w_spatial[0], (0, 2, 1)).reshape(NH * CH, CH)
    bspt = jnp.repeat(b_spatial[0].T, HD, axis=1)

    (w1i,) = gathered(0, token)
    p1, act1 = _ffn_up(x, sh1, sc1, g_pre_f1, w1i)
    token = gather_forward(3, act1)
    token = gather_forward(1, token)
    (w1o,) = gathered(3, token)
    w1o = w1o.reshape(DFF, D)
    x1, f1 = _ffn_down(x, act1, gt1, g_post_f1, w1o)
    wmi, wmo, cw4 = gathered(1, x1)
    wmo = wmo.reshape(D, D)
    cw_full = jnp.transpose(cw4, (1, 0, 2)).reshape(CK, WB)
    cw_pad = jnp.pad(cw_full, ((0, HALO - CK), (0, 0)))
    u, v, a, g = _mix_in_fwd(x1, sh2, sc2, g_pre_m, wmi)
    token = gather_forward(2, u)
    x2, conv, yb, m = _mix_mid_fwd(x1, u, v, a, g, gt2 + token[0, 0], gmlp_norm_g, gmlp_norm_b, wcat, bspt, cw_pad, conv_b,
                                   conv_norm_g, conv_norm_b, g_out_a, g_out_b, wmo, g_post_m)
    w2i, w2o = gathered(2, [x2, token])
    w2o = w2o.reshape(DFF, D)
    dx3, df2, p2, lsum, dg_post_f2, dgt3 = _ffn_fwd(x2, sh3, sc3, gt3, g_pre_f2, g_post_f2, w2i, w2o, target=loss_target)

    def chip4(pair, rows):
        return [arr.reshape(NCHIP, rows, arr.shape[-1]) for arr in pair]

    def scatter_start(tag, pairs, behind):
        srcs = [p[1] for p in pairs]
        lands = [lax.empty((3,) + a.shape[1:], a.dtype) for a in srcs]
        ssem, rsem, srcs, lands, token = _split_start("gs_start_" + tag, srcs, lands, _scatter_plan(len(srcs)),
                                                      3 * len(srcs), behind)
        return (srcs, lands, ssem, rsem), token

    def scatter_wait(tag, state, behind):
        srcs, lands, ssem, rsem = state
        return _split_wait("gs_wait_" + tag, srcs, lands, ssem, rsem, _scatter_plan(len(srcs)), behind)

    def allgather_start(tag, arrs, behind):
        lands = [lax.dynamic_update_index_in_dim(lax.empty((NDEV,) + a.shape, a.dtype), a, dev, 0) for a in arrs]
        ssem, rsem, srcs, lands, token = _split_start("small_start_" + tag, arrs, lands, _allgather_plan(len(arrs)),
                                                      7 * len(arrs), behind)
        return (srcs, lands, ssem, rsem), token

    def allgather_wait(tag, state, behind):
        srcs, lands, ssem, rsem = state
        return _split_wait("small_wait_" + tag, srcs, lands, ssem, rsem, _allgather_plan(len(srcs)), behind)

    out = {}
    dx2, dp2, h3, a2, dg_pre_f2, dsh3, dsc3 = _ffn_bwd(
        dx3, x2, None, p2, sh3, sc3, gt3, g_pre_f2, g_post_f2, w2i, w2o, df=df2)
    gw2i = _wgrad("wgrad_f2_in", h3.reshape(t, D), dp2.reshape(t, 2 * DFF), 2 * DFF // NCHIP, True)
    gw2o = chip4(_wgrad("wgrad_f2_out", a2.reshape(t, DFF), df2.reshape(t, D), D // 2, False), DFF // NCHIP)
    scat_f2, tok = scatter_start("f2", [gw2i, gw2o], dg_post_f2)
    dy, dm, dg_post_m, dgt2 = _mix_out_bwd(dx2, m, gt2 + tok[0, 0], g_post_m, wmo)
    gwmo = chip4(_wgrad("wgrad_mix_out", yb.reshape(t, D), dm.reshape(t, D), D // 2, False), D // NCHIP)
    (du, dv, dconv, dwcat, dbsp, dgn_g, dgn_b, dgo_a, dgo_b, dcn_g, dcn_b, dcb) = _mix_mid_bwd(
        dy, u, v, conv, gmlp_norm_g, gmlp_norm_b, wcat, wcat_t, bspt, conv_norm_g, conv_norm_b, g_out_a, g_out_b)
    dx1, dproj, h2, dg_pre_m, dsh2, dsc2, dcw = _mix_in_bwd(dx2, x1, du, dv, dconv, a, g, sh2, sc2, g_pre_m, wmi, cw_pad)
    gwmi = _wgrad("wgrad_mix_in", h2.reshape(t, D), dproj.reshape(t, 4 * WA), WA, True)
    scat_mix, tok = scatter_start("mix", [gwmi, gwmo], dg_pre_m)

    vec_grads = dict(g_pre_m=dg_pre_m, g_post_m=dg_post_m, g_pre_f2=dg_pre_f2, g_post_f2=dg_post_f2)
    pair_grads = dict(gmlp_norm_g=dgn_g, gmlp_norm_b=dgn_b, conv_b=dcb, conv_norm_g=dcn_g, conv_norm_b=dcn_b,
                      g_out_a=dgo_a, g_out_b=dgo_b)
    pack = _pack_small([vec_grads[n] for n in VEC_ORDER[2:]], [pair_grads[n] for n in PAIR_ORDER], dbsp, dwcat, lsum)
    dada_early = jnp.concatenate([q.reshape(nb, D) for q in (dsh2, dsc2, dgt2, dsh3, dsc3, dgt3)], axis=1)
    early, tok2 = allgather_start("early", [pack, dcw, dada_early.reshape(8, (nb * 6 * D) // 8)], tok)
    grad_x, dp1, h1, a1, df1, dg_pre_f1, dg_post_f1, dsh1, dsc1, dgt1 = _ffn_bwd(
        dx1, x, f1, p1, sh1 + tok2[0, 0], sc1, gt1, g_pre_f1, g_post_f1, w1i, w1o)
    late_pack = _pack_late([dg_pre_f1, dg_post_f1] + [q.reshape(nb, D) for q in (dsh1, dsc1, dgt1)])
    late, tok2 = allgather_start("late", [late_pack], dg_post_f1)
    gw1i = _wgrad("wgrad_f1_in", h1.reshape(t, D), dp1.reshape(t, 2 * DFF), 2 * DFF // NCHIP, True)
    gw1o = chip4(_wgrad("wgrad_f1_out", a1.reshape(t, DFF), df1.reshape(t, D), D // 2, False), DFF // NCHIP)
    def d2d_start(tag, srcs, lands, plan, behind):
        ssem, rsem, srcs, lands, token = _split_start("d2d_start_" + tag, srcs, lands, plan, len(srcs), behind)
        return (srcs, lands, ssem, rsem, plan), token

    def d2d_wait(tag, state, behind):
        srcs, lands, ssem, rsem, plan = state
        return _split_wait("d2d_wait_" + tag, srcs, lands, ssem, rsem, plan, behind)

    def swap_start(tag, parts, behind):
        return d2d_start(tag, parts, [lax.empty(a.shape, a.dtype) for a in parts], _swap_plan(len(parts)), behind)

    def sums(names, pairs, recv):
        return [_sum4("sum4_" + n, pairs[k][0], recv[k], j_arr) for k, n in enumerate(names)]

    def update(names, part, other):
        for k, n in enumerate(names):
            out[n] = tuple(r[None] for r in _adam_big("adam_" + n, wts[n][0], mom[n][0], var[n][0], part[k], other[k]))

    c_arr = ac.reshape(1).astype(jnp.int32)
    halves = [gw1i[1], gw1o[1]]
    pair_st, tok = d2d_start("pair", halves, [lax.empty((a.shape[0], a.shape[1] // 2, a.shape[2]), a.dtype) for a in halves],
                             _pair_plan([a.shape for a in halves]), tok2)
    names_f2, names_mix, names_f1 = ("w_f2_in", "w_f2_out"), ("w_mix_in", "w_mix_out"), ("w_f1_in", "w_f1_out")
    part_f2 = sums(names_f2, [gw2i, gw2o], scatter_wait("f2", scat_f2, tok))
    sib = d2d_wait("pair", pair_st, part_f2)
    pair_i = _pair_sum("pairsum_f1_in", gw1i[0], sib[0], c_arr)
    pair_o = _pair_sum("pairsum_f1_out", gw1o[0], sib[1], c_arr)
    scat_f1, tok = scatter_start("f1", [pair_i, pair_o], tok2)
    swap_f2, tok = swap_start("swap_f2", part_f2, tok)
    part_mix = sums(names_mix, [gwmi, gwmo], scatter_wait("mix", scat_mix, tok))
    swap_mix, tok = swap_start("swap_mix", part_mix, part_mix[1])

    pack_all, dcw_all, dada_early8 = allgather_wait("early", early, tok)
    (late_all,) = allgather_wait("late", late, pack_all)
    dada_late = jnp.transpose(late_all[:, 2:8, :].reshape(NDEV, 3, nb, D), (0, 2, 1, 3)).reshape(NDEV * nb, 3 * D)
    dada_all = jnp.concatenate([dada_late, dada_early8.reshape(NDEV * nb, 6 * D)], axis=1)
    dada_sh = lax.dynamic_slice(dada_all, (0, j_chip * ADA_SH), (NDEV * nb, ADA_SH))
    out["w_ada"] = tuple(r[None] for r in _ada_bwd_adam(c_all, dada_sh, w_ada[0], m_w_ada[0], v_w_ada[0]))
    update(names_f2, part_f2, d2d_wait("swap_f2", swap_f2, out["w_ada"][3]))
    update(names_mix, part_mix, d2d_wait("swap_mix", swap_mix, out["w_f2_out"][3]))

    mine = sums(names_f1, [pair_i, pair_o], scatter_wait("f1", scat_f1, out["w_mix_out"][3]))
    swap_f1, tok = swap_start("swap_f1", mine, mine[1])
    dcw_mine = lax.dynamic_slice(dcw_all, (0, 0, j_chip * (WB // NCHIP)), (NDEV, HALO, WB // NCHIP))
    small = {n: (wts[n], mom[n], var[n]) for n in list(VEC_ORDER) + list(PAIR_ORDER) + ["b_spatial", "w_spatial", "conv_w", "b_ada"]}
    small_out, loss = _small_adam(pack_all, late_all, dcw_mine, dada_all, small, tok)
    out.update(small_out)
    theirs = d2d_wait("swap_f1", swap_f1, out["b_ada"][3])
    for k, n in enumerate(names_f1):
        out[n] = tuple(r[None] for r in _adam_halves("adam_" + n, wts[n][0], mom[n][0], var[n][0], mine[k], theirs[k],
                                                     c_arr))

    res = [loss, grad_x]
    for k in range(4):
        res += [out[n][k] for n in WEIGHTS]
    return tuple(res)
```

```python
import jax
import jax.numpy as jnp
from jax import lax
from jax.experimental import pallas as pl
from jax.experimental.pallas import tpu as pltpu

D = 1024
DFF = 2816
WA = 512
WB = 512
NH = 8
HD = 64
CH = 128
CK = 31
HALO = 32
NMOD = 9
EPS = 1e-6
NCHIP = 4
NDEV = 8
FBLK = DFF // 2
ADA_SH = NMOD * D // NCHIP

LR, B1, B2, EPS_A, WD, STEP = 0.001, 0.9, 0.999, 1e-08, 0.01, 10

F32 = jnp.float32
BF16 = jnp.bfloat16
MESH = pl.DeviceIdType.MESH
ANY = pl.BlockSpec(memory_space=pl.ANY)
VMEM_FULL = pl.BlockSpec(memory_space=pltpu.VMEM)
VMEM_LIMIT = 56 * 1024 * 1024

NT = (((1,), (1,)), ((), ()))
TN = (((0,), (0,)), ((), ()))


def _dot(a, b):
    return jnp.dot(a, b, preferred_element_type=F32)


def _dot_nt(a, b):
    return lax.dot_general(a, b, NT, preferred_element_type=F32)


def _dot_tn(a, b):
    return lax.dot_general(a, b, TN, preferred_element_type=F32)


def _cparams():
    return pltpu.CompilerParams(vmem_limit_bytes=VMEM_LIMIT)


def _allgather8(name, arrs):
    n = len(arrs)
    remote = _allgather_plan(n)

    def plan(x, y, c, ins, outs):
        _, sends = remote(x, y, c, ins, outs)
        return [(ins[a], outs[a].at[4 * x + 2 * y + c]) for a in range(n)], sends

    shapes = [jax.ShapeDtypeStruct((NDEV,) + a.shape, a.dtype) for a in arrs]
    return _run_exchange(name, arrs, shapes, plan, n, 7 * n)


def _chip_relations(x, y):
    return [(1 - x, y), (x, 1 - y), (1 - x, 1 - y)]


def _exchange(name, arrs, out_shapes, plan):
    n = len(arrs)
    n_out = len(out_shapes)

    def body(*refs):
        ins, outs = refs[:n], refs[n:n + n_out]
        send_sems, recv_sems, local_sems = refs[n + n_out:]
        x, y, c = lax.axis_index("x"), lax.axis_index("y"), lax.axis_index("c")
        local, sends = plan(x, y, c, ins, outs)
        locs = [pltpu.make_async_copy(s, d, local_sems.at[i]) for i, (s, d) in enumerate(local)]
        for loc in locs:
            loc.start()
        cps = [pltpu.make_async_remote_copy(src_ref=s, dst_ref=d, send_sem=send_sems.at[i], recv_sem=recv_sems.at[i],
                                            device_id=peer, device_id_type=MESH)
               for i, (s, d, peer, _) in enumerate(sends)]
        for cp in cps:
            cp.start()
        for i, (s, _, peer, landing) in enumerate(sends):
            pltpu.make_async_remote_copy(src_ref=s, dst_ref=landing, send_sem=send_sems.at[i], recv_sem=recv_sems.at[i],
                                         device_id=peer, device_id_type=MESH).wait_recv()
        for cp in cps:
            cp.wait_send()
        for loc in locs:
            loc.wait()

    return n, n_out, body


def _run_exchange(name, arrs, out_shapes, plan, n_local, n_send):
    n, n_out, body = _exchange(name, arrs, out_shapes, plan)
    return pl.pallas_call(
        body, name=name, out_shape=out_shapes,
        in_specs=[ANY] * n, out_specs=[ANY] * n_out,
        scratch_shapes=[pltpu.SemaphoreType.DMA((n_send,)), pltpu.SemaphoreType.DMA((n_send,)),
                        pltpu.SemaphoreType.DMA((max(n_local, 1),))],
    )(*arrs)


def _chip_allgather(name, arrs, behind=()):
    n = len(arrs)

    def plan(x, y, c, ins, outs):
        j_me = 2 * x + y
        local = [(ins[a], outs[a].at[j_me]) for a in range(n)]
        sends = []
        for a in range(n):
            for (px, py) in _chip_relations(x, y):
                sends.append((ins[a], outs[a].at[j_me], (px, py, c), outs[a].at[2 * px + py]))
        return local, sends

    shapes = [jax.ShapeDtypeStruct((NCHIP,) + a.shape, a.dtype) for a in arrs]
    return _run_exchange(name, list(arrs) + list(behind), shapes, plan, n, 3 * n)


HBM = pl.BlockSpec(memory_space=pltpu.HBM)
SEM = pl.BlockSpec(memory_space=pltpu.SEMAPHORE)
EFFECT = pltpu.SideEffectType.DATAFLOW_SIDE_EFFECTING


def _split_start_groups(name, groups, after):
    n_src = [len(g[0]) for g in groups]
    n_land = [len(g[1]) for g in groups]
    all_srcs = [pltpu.with_memory_space_constraint(a, pltpu.HBM) for g in groups for a in g[0]]
    all_lands = [pltpu.with_memory_space_constraint(a, pltpu.HBM) for g in groups for a in g[1]]
    ns, nl, ng = len(all_srcs), len(all_lands), len(groups)

    def body(*refs):
        src_refs, land_refs = refs[:ns], refs[ns:ns + nl]
        sem_refs = refs[ns + nl + 1:ns + nl + 1 + 2 * ng]
        token = refs[-1]
        x, y, c = lax.axis_index("x"), lax.axis_index("y"), lax.axis_index("c")
        at_src = at_land = 0
        for gi, (_, _, plan, _) in enumerate(groups):
            _, sends = plan(x, y, c, src_refs[at_src:at_src + n_src[gi]], land_refs[at_land:at_land + n_land[gi]])
            for i, (s, d, peer, _) in enumerate(sends):
                pltpu.make_async_remote_copy(src_ref=s, dst_ref=d, send_sem=sem_refs[2 * gi].at[i],
                                             recv_sem=sem_refs[2 * gi + 1].at[i], device_id=peer, device_id_type=MESH).start()
            at_src += n_src[gi]
            at_land += n_land[gi]
        token[...] = jnp.zeros_like(token)

    sems = [pltpu.SemaphoreType.DMA((g[3],)) for g in groups for _ in range(2)]
    res = pl.pallas_call(
        body, name=name,
        out_shape=(*sems, *[pltpu.HBM(a.shape, a.dtype) for a in all_lands], jax.ShapeDtypeStruct((8, 128), F32)),
        in_specs=[HBM] * (ns + nl) + [ANY],
        out_specs=(*([SEM] * (2 * ng)), *([HBM] * nl), pl.BlockSpec(memory_space=pltpu.VMEM)),
        input_output_aliases={ns + i: 2 * ng + i for i in range(nl)},
        compiler_params=pltpu.CompilerParams(has_side_effects=EFFECT),
    )(*all_srcs, *all_lands, after)
    out, at_src, at_land = [], 0, 2 * ng
    for gi in range(ng):
        out.append((res[2 * gi], res[2 * gi + 1], all_srcs[at_src:at_src + n_src[gi]],
                    list(res[at_land:at_land + n_land[gi]])))
        at_src += n_src[gi]
        at_land += n_land[gi]
    return out, res[-1]


def _split_start(name, srcs, lands, plan, n_send, after):
    (group,), token = _split_start_groups(name, [(srcs, lands, plan, n_send)], after)
    return (*group, token)


def _split_wait(name, srcs, lands, send_sems, recv_sems, plan, after):
    n, nl = len(srcs), len(lands)
    afters = list(after) if isinstance(after, (list, tuple)) else [after]

    def body(*refs):
        src, land = refs[:n], refs[n:n + nl]
        send_sems, recv_sems = refs[n + nl], refs[n + nl + 1]
        x, y, c = lax.axis_index("x"), lax.axis_index("y"), lax.axis_index("c")
        _, sends = plan(x, y, c, src, land)
        for i, (s, _, peer, landing) in enumerate(sends):
            cp = pltpu.make_async_remote_copy(src_ref=s, dst_ref=landing, send_sem=send_sems.at[i],
                                              recv_sem=recv_sems.at[i], device_id=peer, device_id_type=MESH)
            cp.wait_send()
            cp.wait_recv()

    thru = [pltpu.HBM(a.shape, a.dtype) for a in lands]
    res = pl.pallas_call(
        body, name=name, out_shape=tuple(thru),
        in_specs=[HBM] * (n + nl) + [SEM, SEM] + [ANY] * len(afters), out_specs=tuple([HBM] * nl),
        input_output_aliases={n + i: i for i in range(nl)},
        compiler_params=pltpu.CompilerParams(has_side_effects=EFFECT),
    )(*srcs, *lands, send_sems, recv_sems, *afters)
    return list(res)


def _split_forward(name, srcs, lands, send_a, recv_a, plan_a, plan_b, n_b, after):
    n, nl = len(srcs), len(lands)

    def body(*refs):
        src, land = refs[:n], refs[n:n + nl]
        send_a, recv_a = refs[n + nl], refs[n + nl + 1]
        send_b, recv_b = refs[n + nl + 3], refs[n + nl + 4]
        token = refs[-1]
        x, y, c = lax.axis_index("x"), lax.axis_index("y"), lax.axis_index("c")
        _, first = plan_a(x, y, c, src, land)
        for i, (s, _, peer, landing) in enumerate(first):
            cp = pltpu.make_async_remote_copy(src_ref=s, dst_ref=landing, send_sem=send_a.at[i],
                                              recv_sem=recv_a.at[i], device_id=peer, device_id_type=MESH)
            cp.wait_send()
            cp.wait_recv()
        _, second = plan_b(x, y, c, src, land)
        for i, (s, d, peer, _) in enumerate(second):
            pltpu.make_async_remote_copy(src_ref=s, dst_ref=d, send_sem=send_b.at[i], recv_sem=recv_b.at[i],
                                         device_id=peer, device_id_type=MESH).start()
        token[...] = jnp.zeros_like(token)

    thru = [pltpu.HBM(a.shape, a.dtype) for a in lands]
    res = pl.pallas_call(
        body, name=name,
        out_shape=(pltpu.SemaphoreType.DMA((n_b,)), pltpu.SemaphoreType.DMA((n_b,)), *thru,
                   jax.ShapeDtypeStruct((8, 128), F32)),
        in_specs=[HBM] * (n + nl) + [SEM, SEM, ANY],
        out_specs=(SEM, SEM, *([HBM] * nl), pl.BlockSpec(memory_space=pltpu.VMEM)),
        input_output_aliases={n + i: 2 + i for i in range(nl)},
        compiler_params=pltpu.CompilerParams(has_side_effects=EFFECT),
    )(*srcs, *lands, send_a, recv_a, after)
    return res[0], res[1], list(res[2:2 + nl]), res[-1]


def _gather_plans(shapes):
    n = len(shapes)

    def halves(a, c):
        rows = shapes[a][0] // 2
        return pl.ds(pl.multiple_of(c * rows, 16), rows), pl.ds(pl.multiple_of((1 - c) * rows, 16), rows)

    def split(a):
        return shapes[a][0] % 32 == 0

    def plan_a(x, y, c, src, land):
        j_me = 2 * x + y
        sends = []
        for a in range(n):
            for (px, py) in _chip_relations(x, y):
                if split(a):
                    mine, _ = halves(a, c)
                    sends.append((src[a].at[mine], land[a].at[j_me, mine], (px, py, c), land[a].at[2 * px + py, mine]))
                else:
                    sends.append((src[a], land[a].at[j_me], (px, py, c), land[a].at[2 * px + py]))
        return [], sends

    def plan_b(x, y, c, src, land):
        sends = []
        for a in range(n):
            if split(a):
                mine, other = halves(a, c)
                for (px, py) in _chip_relations(x, y):
                    j = 2 * px + py
                    sends.append((land[a].at[j, mine], land[a].at[j, mine], (x, y, 1 - c), land[a].at[j, other]))
        return [], sends

    n_b = 3 * sum(1 for a in range(n) if split(a))
    return plan_a, plan_b, n_b


def _allgather_plan(n):
    flips = [(dx, dy, dc) for dx in (0, 1) for dy in (0, 1) for dc in (0, 1) if dx or dy or dc]

    def plan(x, y, c, src, land):
        sends = []
        for a in range(n):
            for dx, dy, dc in flips:
                px, py, pc = x ^ dx, y ^ dy, c ^ dc
                sends.append((src[a], land[a].at[4 * x + 2 * y + c], (px, py, pc), land[a].at[4 * px + 2 * py + pc]))
        return [], sends

    return plan


def _scatter_plan(n):
    def plan(x, y, c, src, land):
        sends = []
        for a in range(n):
            for k, (px, py) in enumerate(_chip_relations(x, y)):
                sends.append((src[a].at[2 * px + py], land[a].at[k], (px, py, c), land[a].at[k]))
        return [], sends

    return plan


def _rms(x):
    r = lax.rsqrt(jnp.mean(x * x, axis=-1, keepdims=True) + EPS)
    return x * r, r


def _rms_bwd(dy, n, r, g):
    dg = jnp.sum(dy * n, axis=0, keepdims=True)
    dn = dy * g
    dx = r * (dn - n * jnp.mean(dn * n, axis=-1, keepdims=True))
    return dx, dg


def _ln(x):
    mu = jnp.mean(x, axis=-1, keepdims=True)
    xc = x - mu
    rstd = lax.rsqrt(jnp.mean(xc * xc, axis=-1, keepdims=True) + EPS)
    return xc * rstd, rstd


def _ln_bwd(dy, xhat, rstd, g):
    dg = jnp.sum(dy * xhat, axis=0, keepdims=True)
    db = jnp.sum(dy, axis=0, keepdims=True)
    dxh = dy * g
    dx = rstd * (dxh - jnp.mean(dxh, axis=-1, keepdims=True) - xhat * jnp.mean(dxh * xhat, axis=-1, keepdims=True))
    return dx, dg, db


def _sigmoid(x):
    return jax.nn.sigmoid(x)


def _dsilu(x, s):
    return s * (1.0 + x * (1.0 - s))


def _adam(w, g, m, v):
    m = B1 * m + (1.0 - B1) * g
    v = B2 * v + (1.0 - B2) * (g * g)
    m_hat = m / (1.0 - B1 ** STEP)
    v_hat = v / (1.0 - B2 ** STEP)
    delta = -LR * (m_hat / (jnp.sqrt(v_hat) + EPS_A) + WD * w)
    return delta, m, v


def _head_mask(shape):
    lane = lax.broadcasted_iota(jnp.int32, shape, len(shape) - 1)
    return [(lane >= h * HD) & (lane < (h + 1) * HD) for h in range(NH)]


def _first(b, i):
    return jnp.logical_and(b == 0, i == 0)


def _acc(ref, val, first):
    @pl.when(first)
    def _():
        ref[...] = val

    @pl.when(jnp.logical_not(first))
    def _():
        ref[...] += val


def _ada_fwd(c_all, w_sh, b_sh):
    nb = c_all.shape[0]
    tn = 768

    def body(c_ref, w_ref, b_ref, o_ref):
        cv = c_ref[...]
        cs = (cv * _sigmoid(cv)).astype(BF16)
        o_ref[...] = _dot(cs, w_ref[...].astype(BF16)) + b_ref[...]

    return pl.pallas_call(
        body, name="ada_fwd", grid=(ADA_SH // tn,),
        out_shape=jax.ShapeDtypeStruct((nb, ADA_SH), F32),
        in_specs=[pl.BlockSpec((nb, D), lambda j: (0, 0)), pl.BlockSpec((D, tn), lambda j: (0, j)),
                  pl.BlockSpec((1, tn), lambda j: (0, j))],
        out_specs=pl.BlockSpec((nb, tn), lambda j: (0, j)),
        compiler_params=_cparams(),
    )(c_all, w_sh, b_sh)


def _ada_bwd_adam(c_all, dada_sh, w, m, v):
    nb = c_all.shape[0]
    tn = 768

    def body(c_ref, d_ref, w_ref, m_ref, v_ref, g_out, d_out, m_out, v_out):
        cv = c_ref[...]
        cs = (cv * _sigmoid(cv)).astype(BF16)
        g = _dot_tn(cs, d_ref[...].astype(BF16))
        delta, m2, v2 = _adam(w_ref[...], g, m_ref[...], v_ref[...])
        g_out[...] = g
        d_out[...] = delta
        m_out[...] = m2
        v_out[...] = v2

    big = pl.BlockSpec((D, tn), lambda j: (0, j))
    shape = jax.ShapeDtypeStruct((D, ADA_SH), F32)
    return pl.pallas_call(
        body, name="ada_bwd_adam", grid=(ADA_SH // tn,),
        out_shape=[shape] * 4,
        in_specs=[pl.BlockSpec((nb, D), lambda j: (0, 0)), pl.BlockSpec((nb, tn), lambda j: (0, j)), big, big, big],
        out_specs=[big] * 4,
        compiler_params=_cparams(),
    )(c_all, dada_sh, w, m, v)


def _tok_specs(tm, width):
    return pl.BlockSpec((1, tm, width), lambda b, i: (b, i, 0))


def _mod_spec():
    return pl.BlockSpec((1, 1, D), lambda b, i: (b, 0, 0))


def _row_spec(width=D):
    return pl.BlockSpec((1, width), lambda b, i: (0, 0))


def _ffn_fwd(x, sh, sc, gt, g_pre, g_post, w_in4, w_out, target=None):
    nb, s, _ = x.shape
    tm = min(512, s)
    with_loss = target is not None

    def body(*refs):
        if with_loss:
            (x_ref, sh_ref, sc_ref, gt_ref, gpre_ref, gpost_ref, win_ref, wout_ref, tgt_ref,
             xo_ref, df_ref, p_ref, ls_ref, dgpost_ref, dgt_ref) = refs
        else:
            (x_ref, sh_ref, sc_ref, gt_ref, gpre_ref, gpost_ref, win_ref, wout_ref,
             xo_ref, f_ref, p_ref) = refs
        xv = x_ref[0]
        n, _ = _rms(xv)
        h = (n * gpre_ref[...]) * (1.0 + sc_ref[0]) + sh_ref[0]
        hb = h.astype(BF16)
        acc = jnp.zeros((tm, D), F32)
        for j in range(2):
            gate = _dot(hb, win_ref[j])
            up = _dot(hb, win_ref[2 + j])
            p_ref[0, :, j * FBLK:(j + 1) * FBLK] = gate.astype(BF16)
            p_ref[0, :, DFF + j * FBLK:DFF + (j + 1) * FBLK] = up.astype(BF16)
            a = (gate * _sigmoid(gate)) * up
            acc = acc + _dot(a.astype(BF16), wout_ref[j * FBLK:(j + 1) * FBLK, :])
        nf, q = _rms(acc)
        gpost = gpost_ref[...]
        half_gate = 0.5 * gt_ref[0]
        out = xv + half_gate * (nf * gpost)
        if with_loss:
            first = _first(pl.program_id(0), pl.program_id(1))
            err = out - tgt_ref[0]
            dout = err * (1.0 / D)
            xo_ref[0] = dout
            row = jnp.sum(err * err, axis=0, keepdims=True)
            part = row[:, 0:128]
            for k in range(1, D // 128):
                part = part + row[:, k * 128:(k + 1) * 128]
            _acc(ls_ref, part, first)
            df, dgpost = _rms_bwd(dout * half_gate, nf, q, gpost)
            df_ref[0] = df.astype(BF16)
            _acc(dgpost_ref, dgpost, first)
            _acc(dgt_ref, jnp.sum(dout * (0.5 * (nf * gpost)), axis=0, keepdims=True)[None], pl.program_id(1) == 0)
        else:
            f_ref[0] = acc
            xo_ref[0] = out

    in_specs = [_tok_specs(tm, D), _mod_spec(), _mod_spec(), _mod_spec(), _row_spec(), _row_spec(), VMEM_FULL, VMEM_FULL]
    args = [x, sh, sc, gt, g_pre, g_post, w_in4, w_out]
    out_shape = [jax.ShapeDtypeStruct((nb, s, D), F32), jax.ShapeDtypeStruct((nb, s, D), BF16 if with_loss else F32),
                 jax.ShapeDtypeStruct((nb, s, 2 * DFF), BF16)]
    out_specs = [_tok_specs(tm, D), _tok_specs(tm, D), _tok_specs(tm, 2 * DFF)]
    if with_loss:
        in_specs.append(_tok_specs(tm, D))
        args.append(target)
        out_shape += [jax.ShapeDtypeStruct((1, 128), F32), jax.ShapeDtypeStruct((1, D), F32),
                      jax.ShapeDtypeStruct((nb, 1, D), F32)]
        out_specs += [pl.BlockSpec((1, 128), lambda b, i: (0, 0)), _row_spec(), _mod_spec()]
    return pl.pallas_call(
        body, name="ffn_loss_fwd" if with_loss else "ffn_fwd", grid=(nb, s // tm),
        out_shape=out_shape, in_specs=in_specs, out_specs=out_specs,
        compiler_params=_cparams(),
    )(*args)


def _ffn_up(x, sh, sc, g_pre, w_in4):
    nb, s, _ = x.shape
    tm = min(512, s)

    def body(x_ref, sh_ref, sc_ref, gpre_ref, win_ref, p_ref, a_ref):
        n, _ = _rms(x_ref[0])
        hb = ((n * gpre_ref[...]) * (1.0 + sc_ref[0]) + sh_ref[0]).astype(BF16)
        for j in range(2):
            gate = _dot(hb, win_ref[j])
            up = _dot(hb, win_ref[2 + j])
            p_ref[0, :, j * FBLK:(j + 1) * FBLK] = gate.astype(BF16)
            p_ref[0, :, DFF + j * FBLK:DFF + (j + 1) * FBLK] = up.astype(BF16)
            a_ref[0, :, j * FBLK:(j + 1) * FBLK] = ((gate * _sigmoid(gate)) * up).astype(BF16)

    return pl.pallas_call(
        body, name="ffn_up", grid=(nb, s // tm),
        out_shape=[jax.ShapeDtypeStruct((nb, s, 2 * DFF), BF16), jax.ShapeDtypeStruct((nb, s, DFF), BF16)],
        in_specs=[_tok_specs(tm, D), _mod_spec(), _mod_spec(), _row_spec(), VMEM_FULL],
        out_specs=[_tok_specs(tm, 2 * DFF), _tok_specs(tm, DFF)],
        compiler_params=_cparams(),
    )(x, sh, sc, g_pre, w_in4)


def _ffn_down(x, a, gt, g_post, w_out):
    nb, s, _ = x.shape
    tm = min(512, s)

    def body(x_ref, a_ref, gt_ref, gpost_ref, wout_ref, xo_ref, f_ref):
        acc = _dot(a_ref[0], wout_ref[...])
        f_ref[0] = acc
        nf, _ = _rms(acc)
        xo_ref[0] = x_ref[0] + (0.5 * gt_ref[0]) * (nf * gpost_ref[...])

    tok = _tok_specs(tm, D)
    shape = jax.ShapeDtypeStruct((nb, s, D), F32)
    return pl.pallas_call(
        body, name="ffn_down", grid=(nb, s // tm), out_shape=[shape, shape],
        in_specs=[tok, _tok_specs(tm, DFF), _mod_spec(), _row_spec(), VMEM_FULL],
        out_specs=[tok, tok],
        compiler_params=_cparams(),
    )(x, a, gt, g_post, w_out)


def _ffn_bwd(dxo, x, f, p, sh, sc, gt, g_pre, g_post, w_in4, w_out, df=None):
    nb, s, _ = x.shape
    tm = min(256, s)
    given = df is not None

    def body(*refs):
        if given:
            (dxo_ref, x_ref, dfin_ref, p_ref, sh_ref, sc_ref, gpre_ref, win_ref, wout_ref,
             dx_ref, dp_ref, h_ref, a_ref, dgpre_ref, dsh_ref, dsc_ref) = refs
        else:
            (dxo_ref, x_ref, f_ref, p_ref, sh_ref, sc_ref, gt_ref, gpre_ref, gpost_ref, win_ref, wout_ref,
             dx_ref, dp_ref, h_ref, a_ref, df_ref, dgpre_ref, dgpost_ref, dsh_ref, dsc_ref, dgt_ref) = refs
        b, i = pl.program_id(0), pl.program_id(1)
        dxo_v = dxo_ref[0]
        if given:
            dfb = dfin_ref[0]
        else:
            nf, q = _rms(f_ref[0])
            gpost = gpost_ref[...]
            dgt = jnp.sum(dxo_v * (0.5 * (nf * gpost)), axis=0, keepdims=True)
            do = dxo_v * (0.5 * gt_ref[0])
            dfv, dgpost = _rms_bwd(do, nf, q, gpost)
            dfb = dfv.astype(BF16)
            df_ref[0] = dfb
        xv = x_ref[0]
        n, r = _rms(xv)
        gpre = gpre_ref[...]
        ng = n * gpre
        scale1 = 1.0 + sc_ref[0]
        h = ng * scale1 + sh_ref[0]
        h_ref[0] = h.astype(BF16)
        dh = jnp.zeros((tm, D), F32)
        for j in range(2):
            gate = p_ref[0, :, j * FBLK:(j + 1) * FBLK].astype(F32)
            up = p_ref[0, :, DFF + j * FBLK:DFF + (j + 1) * FBLK].astype(F32)
            sg = _sigmoid(gate)
            act = gate * sg
            a_ref[0, :, j * FBLK:(j + 1) * FBLK] = (act * up).astype(BF16)
            da = _dot_nt(dfb, wout_ref[j * FBLK:(j + 1) * FBLK, :])
            dgate = (da * up * _dsilu(gate, sg)).astype(BF16)
            dup = (da * act).astype(BF16)
            dp_ref[0, :, j * FBLK:(j + 1) * FBLK] = dgate
            dp_ref[0, :, DFF + j * FBLK:DFF + (j + 1) * FBLK] = dup
            dh = dh + _dot_nt(dgate, win_ref[j]) + _dot_nt(dup, win_ref[2 + j])
        dsh = jnp.sum(dh, axis=0, keepdims=True)
        dsc = jnp.sum(dh * ng, axis=0, keepdims=True)
        dxn, dgpre = _rms_bwd(dh * scale1, n, r, gpre)
        dx_ref[0] = dxo_v + dxn
        _acc(dgpre_ref, dgpre, _first(b, i))
        _acc(dsh_ref, dsh[None], i == 0)
        _acc(dsc_ref, dsc[None], i == 0)
        if not given:
            _acc(dgpost_ref, dgpost, _first(b, i))
            _acc(dgt_ref, dgt[None], i == 0)

    tok = _tok_specs(tm, D)
    mod_shape = jax.ShapeDtypeStruct((nb, 1, D), F32)
    row_shape = jax.ShapeDtypeStruct((1, D), F32)
    big = [jax.ShapeDtypeStruct((nb, s, D), F32), jax.ShapeDtypeStruct((nb, s, 2 * DFF), BF16),
           jax.ShapeDtypeStruct((nb, s, D), BF16), jax.ShapeDtypeStruct((nb, s, DFF), BF16)]
    big_specs = [tok, _tok_specs(tm, 2 * DFF), tok, _tok_specs(tm, DFF)]
    if given:
        return pl.pallas_call(
            body, name="ffn_bwd_after_loss", grid=(nb, s // tm),
            out_shape=big + [row_shape, mod_shape, mod_shape],
            in_specs=[tok, tok, tok, _tok_specs(tm, 2 * DFF), _mod_spec(), _mod_spec(), _row_spec(), VMEM_FULL, VMEM_FULL],
            out_specs=big_specs + [_row_spec(), _mod_spec(), _mod_spec()],
            compiler_params=_cparams(),
        )(dxo, x, df, p, sh, sc, g_pre, w_in4, w_out)
    return pl.pallas_call(
        body, name="ffn_bwd", grid=(nb, s // tm),
        out_shape=big + [jax.ShapeDtypeStruct((nb, s, D), BF16), row_shape, row_shape, mod_shape, mod_shape, mod_shape],
        in_specs=[tok, tok, tok, _tok_specs(tm, 2 * DFF), _mod_spec(), _mod_spec(), _mod_spec(), _row_spec(), _row_spec(),
                  VMEM_FULL, VMEM_FULL],
        out_specs=big_specs + [tok, _row_spec(), _row_spec(), _mod_spec(), _mod_spec(), _mod_spec()],
        compiler_params=_cparams(),
    )(dxo, x, f, p, sh, sc, gt, g_pre, g_post, w_in4, w_out)


def _wgrad(name, a, b, col_block, chip_major):
    t, ka = a.shape
    n = b.shape[1]
    tk = min(t, 512)
    while tk * 2 <= t and t % (tk * 2) == 0 and 2 * (tk * 2) * max(ka, col_block) <= 6 * 1024 * 1024:
        tk *= 2
    nk = t // tk
    nblk = n // col_block

    def body(a_ref, b_ref, o_ref, obf_ref, acc_ref):
        k = pl.program_id(1)

        @pl.when(k == 0)
        def _():
            acc_ref[...] = jnp.zeros_like(acc_ref)

        acc_ref[...] += _dot_tn(a_ref[...], b_ref[...])

        @pl.when(k == nk - 1)
        def _():
            val = acc_ref[...]
            if chip_major:
                o_ref[0] = val
                obf_ref[0] = val.astype(BF16)
            else:
                o_ref[...] = val
                obf_ref[...] = val.astype(BF16)

    if chip_major:
        shape = (nblk, ka, col_block)
        ospec = pl.BlockSpec((1, ka, col_block), lambda j, k: (j, 0, 0))
    else:
        shape = (ka, n)
        ospec = pl.BlockSpec((ka, col_block), lambda j, k: (0, j))
    return pl.pallas_call(
        body, name=name, grid=(nblk, nk),
        out_shape=[jax.ShapeDtypeStruct(shape, F32), jax.ShapeDtypeStruct(shape, BF16)],
        in_specs=[pl.BlockSpec((tk, ka), lambda j, k: (k, 0)), pl.BlockSpec((tk, col_block), lambda j, k: (k, j))],
        out_specs=[ospec, ospec],
        scratch_shapes=[pltpu.VMEM((ka, col_block), F32)],
        compiler_params=_cparams(),
    )(a, b)


def _mix_in_fwd(x, sh, sc, g_pre, w_mi4):
    nb, s, _ = x.shape
    tm = min(512, s)

    def body(x_ref, sh_ref, sc_ref, gpre_ref, w_ref, u_ref, v_ref, a_ref, g_ref):
        n, _ = _rms(x_ref[0])
        hb = ((n * gpre_ref[...]) * (1.0 + sc_ref[0]) + sh_ref[0]).astype(BF16)
        for k, o_ref in enumerate((u_ref, v_ref, a_ref, g_ref)):
            o_ref[0] = _dot(hb, w_ref[k])

    shape = jax.ShapeDtypeStruct((nb, s, WA), F32)
    return pl.pallas_call(
        body, name="mix_in_fwd", grid=(nb, s // tm),
        out_shape=[shape] * 4,
        in_specs=[_tok_specs(tm, D), _mod_spec(), _mod_spec(), _row_spec(), VMEM_FULL],
        out_specs=[_tok_specs(tm, WA)] * 4,
        compiler_params=_cparams(),
    )(x, sh, sc, g_pre, w_mi4)


def _spatial_weights(wcat_ref, transposed):
    w = wcat_ref[...]
    row = lax.broadcasted_iota(jnp.int32, w.shape, 0)
    col = lax.broadcasted_iota(jnp.int32, w.shape, 1)
    keep = ((row & (CH - 1)) <= col) if transposed else ((col & (CH - 1)) <= row)
    return jnp.where(keep, w, 0.0).astype(BF16)


def _expand_heads(vc, masks):
    return jnp.concatenate([jnp.where(mk, vc, jnp.zeros_like(vc)) for mk in masks], axis=0)


def _spatial_bias(bspt_ref):
    return bspt_ref[...]


SHIFTS = 8
TAP_ROWS = 32


def _ext_rows(tm):
    return tm + HALO + SHIFTS


def _make_shifts(ext_ref, sh_ref, tm):
    ext_ref[tm + HALO:tm + HALO + SHIFTS, :] = jnp.zeros((SHIFTS, WB), F32)
    for r in range(SHIFTS):
        sh_ref[r] = ext_ref[r:r + tm + HALO, :]


def _conv_taps(sh_ref, w_ref, tm, taps, emit):
    def block(i, carry):
        r0 = pl.multiple_of(i * TAP_ROWS, TAP_ROWS)
        acc = jnp.zeros((TAP_ROWS, WB), F32)
        for o, k in taps:
            acc = acc + w_ref[k:k + 1, :] * sh_ref[o % SHIFTS, pl.ds(r0 + SHIFTS * (o // SHIFTS), TAP_ROWS), :]
        emit(r0, acc)
        return carry

    lax.fori_loop(0, tm // TAP_ROWS, block, 0)


def _halo_prev_spec(tm):
    return pl.BlockSpec((1, HALO, WB), lambda b, i: (b, jnp.maximum(i * (tm // HALO) - 1, 0), 0))


def _halo_next_spec(tm, s):
    return pl.BlockSpec((1, HALO, WB), lambda b, i: (b, jnp.minimum((i + 1) * (tm // HALO), s // HALO - 1), 0))


def _mix_mid_fwd(x, u, v, a, g, gt, gn_g, gn_b, wcat, bspt, conv_w, conv_b, cn_g, cn_b, go_a, go_b, w_mo, g_post):
    nb, s, _ = x.shape
    tm = min(512, s)

    def body(x_ref, u_ref, v_ref, a_ref, g_ref, ah_ref, gh_ref, gt_ref, gng_ref, gnb_ref, wcat_ref, bspt_ref,
             cw_ref, cb_ref, cng_ref, cnb_ref, goa_ref, gob_ref, wmo_ref, gpost_ref,
             xo_ref, conv_ref, y_ref, m_ref, ext_ref, sh_ref):
        i = pl.program_id(1)
        xhat, _ = _ln(v_ref[0])
        vb = (xhat * gng_ref[...] + gnb_ref[...]).astype(BF16)
        wsb = _spatial_weights(wcat_ref, False)
        bias = _spatial_bias(bspt_ref)
        masks = _head_mask((CH, WA))
        zs = []
        for cidx in range(tm // CH):
            vexp = _expand_heads(vb[cidx * CH:(cidx + 1) * CH, :], masks)
            zs.append(_dot(wsb, vexp) + bias)
        z = jnp.concatenate(zs, axis=0)
        na, _ = _rms(u_ref[0] * z)
        keep = jnp.where(i == 0, 0.0, 1.0).astype(F32)
        ext_ref[0:HALO, :] = (ah_ref[0] * _sigmoid(gh_ref[0])) * keep
        ext_ref[HALO:HALO + tm, :] = a_ref[0] * _sigmoid(g_ref[0])
        _make_shifts(ext_ref, sh_ref, tm)
        cb = cb_ref[...]

        def put_conv(r0, acc):
            conv_ref[0, pl.ds(r0, TAP_ROWS), :] = acc + cb

        _conv_taps(sh_ref, cw_ref, tm, [(k + HALO - (CK - 1), k) for k in range(CK)], put_conv)
        conv = conv_ref[0]
        chat, _ = _ln(conv)
        cln = chat * cng_ref[...] + cnb_ref[...]
        nbb, _ = _rms(cln * _sigmoid(cln))
        yb = jnp.concatenate([na * goa_ref[...], nbb * gob_ref[...]], axis=1).astype(BF16)
        y_ref[0] = yb
        m = _dot(yb, wmo_ref[...])
        m_ref[0] = m
        nm, _ = _rms(m)
        xo_ref[0] = x_ref[0] + gt_ref[0] * (nm * gpost_ref[...])

    t5 = _tok_specs(tm, WA)
    tok = _tok_specs(tm, D)
    r5 = _row_spec(WA)
    full = lambda shape: pl.BlockSpec(shape, lambda b, i: (0,) * len(shape))
    return pl.pallas_call(
        body, name="mix_mid_fwd", grid=(nb, s // tm),
        out_shape=[jax.ShapeDtypeStruct((nb, s, D), F32), jax.ShapeDtypeStruct((nb, s, WB), F32),
                   jax.ShapeDtypeStruct((nb, s, D), BF16), jax.ShapeDtypeStruct((nb, s, D), F32)],
        in_specs=[tok, t5, t5, t5, t5, _halo_prev_spec(tm), _halo_prev_spec(tm), _mod_spec(), r5, r5,
                  full((CH, NH * CH)), full((CH, WA)), full((HALO, WB)), r5, r5, r5, r5, r5, VMEM_FULL, _row_spec()],
        out_specs=[tok, t5, tok, tok],
        scratch_shapes=[pltpu.VMEM((_ext_rows(tm), WB), F32), pltpu.VMEM((SHIFTS, tm + HALO, WB), F32)],
        compiler_params=_cparams(),
    )(x, u, v, a, g, a, g, gt, gn_g, gn_b, wcat, bspt, conv_w, conv_b, cn_g, cn_b, go_a, go_b, w_mo, g_post)


def _mix_out_bwd(dxo, m, gt, g_post, w_mo):
    nb, s, _ = m.shape
    tm = min(512, s)

    def body(dxo_ref, m_ref, gt_ref, gpost_ref, wmo_ref, dy_ref, dm_ref, dgpost_ref, dgt_ref):
        b, i = pl.program_id(0), pl.program_id(1)
        dxo_v = dxo_ref[0]
        nm, q = _rms(m_ref[0])
        gpost = gpost_ref[...]
        dgt = jnp.sum(dxo_v * (nm * gpost), axis=0, keepdims=True)
        dm, dgpost = _rms_bwd(dxo_v * gt_ref[0], nm, q, gpost)
        dmb = dm.astype(BF16)
        dm_ref[0] = dmb
        dy_ref[0] = _dot_nt(dmb, wmo_ref[...])
        _acc(dgpost_ref, dgpost, _first(b, i))
        _acc(dgt_ref, dgt[None], i == 0)

    tok = _tok_specs(tm, D)
    return pl.pallas_call(
        body, name="mix_out_bwd", grid=(nb, s // tm),
        out_shape=[jax.ShapeDtypeStruct((nb, s, D), F32), jax.ShapeDtypeStruct((nb, s, D), BF16),
                   jax.ShapeDtypeStruct((1, D), F32), jax.ShapeDtypeStruct((nb, 1, D), F32)],
        in_specs=[tok, tok, _mod_spec(), _row_spec(), VMEM_FULL],
        out_specs=[tok, tok, _row_spec(), _mod_spec()],
        compiler_params=_cparams(),
    )(dxo, m, gt, g_post, w_mo)


def _mix_mid_bwd(dy, u, v, conv, gn_g, gn_b, wcat, wcat_t, bspt, cn_g, cn_b, go_a, go_b):
    nb, s, _ = dy.shape
    tm = min(512, s)
    nchunk = tm // CH

    def body(dy_ref, u_ref, v_ref, conv_ref, gng_ref, gnb_ref, wcat_ref, wcatt_ref, bspt_ref, cng_ref, cnb_ref,
             goa_ref, gob_ref,
             du_ref, dv_ref, dconv_ref, dwcat_ref, dbsp_ref, dgng_ref, dgnb_ref, dgoa_ref, dgob_ref,
             dcng_ref, dcnb_ref, dcb_ref):
        first = _first(pl.program_id(0), pl.program_id(1))
        dyv = dy_ref[0]
        xhat, rstd = _ln(v_ref[0])
        gng = gng_ref[...]
        vb = (xhat * gng + gnb_ref[...]).astype(BF16)
        wsb = _spatial_weights(wcat_ref, False)
        wsb_t = _spatial_weights(wcatt_ref, True)
        bias = _spatial_bias(bspt_ref)
        masks = _head_mask((CH, WA))
        vexps, zs = [], []
        for cidx in range(nchunk):
            vexp = _expand_heads(vb[cidx * CH:(cidx + 1) * CH, :], masks)
            vexps.append(vexp)
            zs.append(_dot(wsb, vexp) + bias)
        z = jnp.concatenate(zs, axis=0)
        uv = u_ref[0]
        na, ra = _rms(uv * z)
        dya, dgoa = _rms_bwd(dyv[:, 0:WA], na, ra, goa_ref[...])
        du_ref[0] = dya * z
        dz = dya * uv
        dwcat = jnp.zeros((CH, NH * CH), F32)
        dzsum = jnp.zeros((CH, WA), F32)
        dvlns = []
        for cidx in range(nchunk):
            dzc = dz[cidx * CH:(cidx + 1) * CH, :]
            dzsum = dzsum + dzc
            dzb = dzc.astype(BF16)
            dwcat = dwcat + _dot_nt(dzb, vexps[cidx])
            dvexp = _dot(wsb_t, dzb)
            dvl = jnp.zeros((CH, WA), F32)
            for h in range(NH):
                dvl = dvl + jnp.where(masks[h], dvexp[h * CH:(h + 1) * CH, :], 0.0)
            dvlns.append(dvl)
        dvln = jnp.concatenate(dvlns, axis=0)
        dv, dgng, dgnb = _ln_bwd(dvln, xhat, rstd, gng)
        dv_ref[0] = dv
        lane = lax.broadcasted_iota(jnp.int32, (NH, WA), 1)
        head = lax.broadcasted_iota(jnp.int32, (NH, WA), 0)
        sel = jnp.where((lane >= head * HD) & (lane < (head + 1) * HD), 1.0, 0.0).astype(F32)
        dbsp = lax.dot_general(sel, dzsum, NT, preferred_element_type=F32, precision=lax.Precision.HIGHEST)
        chat, crstd = _ln(conv_ref[0])
        cng = cng_ref[...]
        cln = chat * cng + cnb_ref[...]
        sg = _sigmoid(cln)
        nbb, rb = _rms(cln * sg)
        dyb, dgob = _rms_bwd(dyv[:, WA:D], nbb, rb, gob_ref[...])
        dconv, dcng, dcnb = _ln_bwd(dyb * _dsilu(cln, sg), chat, crstd, cng)
        dconv_ref[0] = dconv
        dcb = jnp.sum(dconv, axis=0, keepdims=True)
        for ref, val in ((dwcat_ref, dwcat), (dbsp_ref, dbsp), (dgng_ref, dgng), (dgnb_ref, dgnb), (dgoa_ref, dgoa),
                         (dgob_ref, dgob), (dcng_ref, dcng), (dcnb_ref, dcnb), (dcb_ref, dcb)):
            _acc(ref, val, first)

    t5 = _tok_specs(tm, WA)
    r5 = _row_spec(WA)
    full = lambda shape: pl.BlockSpec(shape, lambda b, i: (0,) * len(shape))
    big = jax.ShapeDtypeStruct((nb, s, WA), F32)
    row = jax.ShapeDtypeStruct((1, WA), F32)
    return pl.pallas_call(
        body, name="mix_mid_bwd", grid=(nb, s // tm),
        out_shape=[big, big, big, jax.ShapeDtypeStruct((CH, NH * CH), F32), jax.ShapeDtypeStruct((NH, CH), F32),
                   row, row, row, row, row, row, row],
        in_specs=[_tok_specs(tm, D), t5, t5, t5, r5, r5, full((CH, NH * CH)), full((NH * CH, CH)), full((CH, WA)),
                  r5, r5, r5, r5],
        out_specs=[t5, t5, t5, full((CH, NH * CH)), full((NH, CH)), r5, r5, r5, r5, r5, r5, r5],
        compiler_params=_cparams(),
    )(dy, u, v, conv, gn_g, gn_b, wcat, wcat_t, bspt, cn_g, cn_b, go_a, go_b)


def _mix_in_bwd(dxo, x, du, dv, dconv, a, g, sh, sc, g_pre, w_mi4, conv_w):
    nb, s, _ = x.shape
    tm = min(512, s)
    n_i = s // tm

    def body(dxo_ref, x_ref, du_ref, dv_ref, dc_ref, dch_ref, a_ref, g_ref, ah_ref, gh_ref, sh_ref, sc_ref,
             gpre_ref, w_ref, cw_ref,
             dx_ref, dproj_ref, h_ref, dgpre_ref, dsh_ref, dsc_ref, dcw_ref, ext_ref, shf_ref, dglu_ref):
        b, i = pl.program_id(0), pl.program_id(1)
        first = _first(b, i)
        av, gv = a_ref[0], g_ref[0]
        sg = _sigmoid(gv)
        dconv = dc_ref[0]
        ext_ref[0:tm, :] = dconv
        ext_ref[tm:tm + HALO, :] = dch_ref[0] * jnp.where(i == n_i - 1, 0.0, 1.0).astype(F32)
        _make_shifts(ext_ref, shf_ref, tm)

        def put_dglu(r0, acc):
            dglu_ref[pl.ds(r0, TAP_ROWS), :] = acc

        _conv_taps(shf_ref, cw_ref, tm, [(CK - 1 - k, k) for k in range(CK)], put_dglu)
        dglu = dglu_ref[...]
        ext_ref[0:HALO, :] = (ah_ref[0] * _sigmoid(gh_ref[0])) * jnp.where(i == 0, 0.0, 1.0).astype(F32)
        ext_ref[HALO:HALO + tm, :] = av * sg
        _make_shifts(ext_ref, shf_ref, tm)

        @pl.when(first)
        def _():
            dcw_ref[...] = jnp.zeros((HALO, WB), F32)

        for k in range(CK):
            o = k + HALO - (CK - 1)
            lo = SHIFTS * (o // SHIFTS)
            dcw_ref[k:k + 1, :] += jnp.sum(dconv * shf_ref[o % SHIFTS, lo:lo + tm, :], axis=0, keepdims=True)
        da = dglu * sg
        dg = dglu * av * (sg * (1.0 - sg))
        parts = [du_ref[0].astype(BF16), dv_ref[0].astype(BF16), da.astype(BF16), dg.astype(BF16)]
        dh = jnp.zeros((tm, D), F32)
        for k in range(4):
            dproj_ref[0, :, k * WA:(k + 1) * WA] = parts[k]
            dh = dh + _dot_nt(parts[k], w_ref[k])
        n, r = _rms(x_ref[0])
        gpre = gpre_ref[...]
        ng = n * gpre
        scale1 = 1.0 + sc_ref[0]
        h_ref[0] = (ng * scale1 + sh_ref[0]).astype(BF16)
        dsh = jnp.sum(dh, axis=0, keepdims=True)
        dsc = jnp.sum(dh * ng, axis=0, keepdims=True)
        dxn, dgpre = _rms_bwd(dh * scale1, n, r, gpre)
        dx_ref[0] = dxo_ref[0] + dxn
        _acc(dgpre_ref, dgpre, first)
        _acc(dsh_ref, dsh[None], i == 0)
        _acc(dsc_ref, dsc[None], i == 0)

    tok = _tok_specs(tm, D)
    t5 = _tok_specs(tm, WA)
    full = lambda shape: pl.BlockSpec(shape, lambda b, i: (0,) * len(shape))
    mod_shape = jax.ShapeDtypeStruct((nb, 1, D), F32)
    return pl.pallas_call(
        body, name="mix_in_bwd", grid=(nb, n_i),
        out_shape=[jax.ShapeDtypeStruct((nb, s, D), F32), jax.ShapeDtypeStruct((nb, s, 4 * WA), BF16),
                   jax.ShapeDtypeStruct((nb, s, D), BF16), jax.ShapeDtypeStruct((1, D), F32), mod_shape, mod_shape,
                   jax.ShapeDtypeStruct((HALO, WB), F32)],
        in_specs=[tok, tok, t5, t5, t5, _halo_next_spec(tm, s), t5, t5, _halo_prev_spec(tm), _halo_prev_spec(tm),
                  _mod_spec(), _mod_spec(), _row_spec(), VMEM_FULL, full((HALO, WB))],
        out_specs=[tok, _tok_specs(tm, 4 * WA), tok, _row_spec(), _mod_spec(), _mod_spec(), full((HALO, WB))],
        scratch_shapes=[pltpu.VMEM((_ext_rows(tm), WB), F32), pltpu.VMEM((SHIFTS, tm + HALO, WB), F32),
                        pltpu.VMEM((tm, WB), F32)],
        compiler_params=_cparams(),
    )(dxo, x, du, dv, dconv, dconv, a, g, a, g, sh, sc, g_pre, w_mi4, conv_w)


def _row_tile(rows, cols):
    best = 16
    for t in range(16, rows + 1, 16):
        if rows % t == 0 and t * cols * 4 <= 1536 * 1024:
            best = t
    return best


def _sum4(name, own4, recv, j_arr):
    _, rows, cols = own4.shape
    tr = _row_tile(rows, cols)

    def body(j_ref, own_ref, recv_ref, o_ref):
        del j_ref
        acc = own_ref[0]
        for k in range(3):
            acc = acc + recv_ref[k].astype(F32)
        o_ref[...] = acc

    return pl.pallas_call(
        body, name=name,
        grid_spec=pltpu.PrefetchScalarGridSpec(
            num_scalar_prefetch=1, grid=(rows // tr,),
            in_specs=[pl.BlockSpec((1, tr, cols), lambda i, j: (j[0], i, 0)),
                      pl.BlockSpec((3, tr, cols), lambda i, j: (0, i, 0))],
            out_specs=pl.BlockSpec((tr, cols), lambda i, j: (i, 0))),
        out_shape=jax.ShapeDtypeStruct((rows, cols), F32),
        compiler_params=_cparams(),
    )(j_arr, own4, recv)


def _pair_plan(shapes):
    def plan(x, y, c, src, land):
        sends = []
        for a, shape in enumerate(shapes):
            rows = shape[1] // 2
            theirs = pl.ds(pl.multiple_of((1 - c) * rows, 16), rows)
            sends.append((src[a].at[:, theirs], land[a], (x, y, 1 - c), land[a]))
        return [], sends

    return plan


def _swap_plan(n):
    def plan(x, y, c, src, land):
        return [], [(src[a], land[a], (x, y, 1 - c), land[a]) for a in range(n)]

    return plan


def _pair_sum(name, g32, recv, c_arr):
    nblk, rows, cols = recv.shape
    tr = _row_tile(rows, cols)
    nh = rows // tr

    def body(c_ref, g_ref, r_ref, o32_ref, obf_ref):
        del c_ref
        val = g_ref[0] + r_ref[0].astype(F32)
        o32_ref[0] = val
        obf_ref[0] = val.astype(BF16)

    spec = pl.BlockSpec((1, tr, cols), lambda k, i, c: (k, i, 0))
    return pl.pallas_call(
        body, name=name,
        grid_spec=pltpu.PrefetchScalarGridSpec(
            num_scalar_prefetch=1, grid=(nblk, nh),
            in_specs=[pl.BlockSpec((1, tr, cols), lambda k, i, c: (k, c[0] * nh + i, 0)), spec],
            out_specs=[spec, spec]),
        out_shape=[jax.ShapeDtypeStruct(recv.shape, F32), jax.ShapeDtypeStruct(recv.shape, BF16)],
        compiler_params=_cparams(),
    )(c_arr, g32, recv)


def _adam_halves(name, w, m, v, mine, theirs, c_arr):
    rows, cols = w.shape
    tr = _row_tile(rows // 2, cols)
    nh = (rows // 2) // tr

    def body(c_ref, w_ref, m_ref, v_ref, mine_ref, theirs_ref, g_out, d_out, m_out, v_out):
        here = (pl.program_id(0) // nh) == c_ref[0]
        g = jnp.where(here, mine_ref[...], theirs_ref[...])
        delta, m2, v2 = _adam(w_ref[...], g, m_ref[...], v_ref[...])
        g_out[...] = g
        d_out[...] = delta
        m_out[...] = m2
        v_out[...] = v2

    spec = pl.BlockSpec((tr, cols), lambda i, c: (i, 0))
    shape = jax.ShapeDtypeStruct((rows, cols), F32)
    return pl.pallas_call(
        body, name=name,
        grid_spec=pltpu.PrefetchScalarGridSpec(
            num_scalar_prefetch=1, grid=(2 * nh,),
            in_specs=[spec, spec, spec,
                      pl.BlockSpec((tr, cols), lambda i, c: (jnp.clip(i - c[0] * nh, 0, nh - 1), 0)),
                      pl.BlockSpec((tr, cols), lambda i, c: (jnp.clip(i - (1 - c[0]) * nh, 0, nh - 1), 0))],
            out_specs=[spec] * 4),
        out_shape=[shape] * 4,
        compiler_params=_cparams(),
    )(c_arr, w, m, v, mine, theirs)


def _adam_big(name, w, m, v, ga, gb):
    rows, cols = w.shape
    tr = _row_tile(rows, cols)

    def body(w_ref, m_ref, v_ref, ga_ref, gb_ref, g_out, d_out, m_out, v_out):
        gsum = ga_ref[...] + gb_ref[...]
        delta, m2, v2 = _adam(w_ref[...], gsum, m_ref[...], v_ref[...])
        g_out[...] = gsum
        d_out[...] = delta
        m_out[...] = m2
        v_out[...] = v2

    spec = pl.BlockSpec((tr, cols), lambda i: (i, 0))
    shape = jax.ShapeDtypeStruct((rows, cols), F32)
    return pl.pallas_call(
        body, name=name, grid=(rows // tr,), out_shape=[shape] * 4,
        in_specs=[spec] * 5, out_specs=[spec] * 4, compiler_params=_cparams(),
    )(w, m, v, ga, gb)


PK_VEC = 0
PK_LOSS = 6
PK_PAIR = 8
PK_BSP = 16
PK_WCAT = 24
PK_ROWS = PK_WCAT + CH
PAIR_ORDER = ("gmlp_norm_g", "gmlp_norm_b", "conv_b", "conv_norm_g", "conv_norm_b", "g_out_a", "g_out_b")
VEC_ORDER = ("g_pre_f1", "g_post_f1", "g_pre_m", "g_post_m", "g_pre_f2", "g_post_f2")


def _pack_late(rows):
    counts = [r.shape[0] for r in rows]
    assert sum(counts) == 8

    def body(*refs):
        o_ref = refs[-1]
        at = 0
        for r, cnt in zip(refs[:-1], counts):
            o_ref[at:at + cnt, :] = r[...]
            at += cnt

    return pl.pallas_call(
        body, name="pack_late", out_shape=jax.ShapeDtypeStruct((8, D), F32),
        in_specs=[VMEM_FULL] * len(rows), out_specs=VMEM_FULL, compiler_params=_cparams(),
    )(*rows)


def _pack_small(vecs, pairs, dbsp, dwcat, lsum):
    def body(*refs):
        vec_refs = refs[:4]
        pair_refs = refs[4:11]
        dbsp_ref, dwcat_ref, lsum_ref, o_ref = refs[11:]
        o_ref[0:PK_WCAT, :] = jnp.zeros((PK_WCAT, D), F32)
        o_ref[PK_LOSS:PK_LOSS + 1, 0:128] = lsum_ref[...]
        for k, r in enumerate(vec_refs):
            o_ref[PK_VEC + 2 + k:PK_VEC + 3 + k, :] = r[...]
        for k, r in enumerate(pair_refs):
            row, half = PK_PAIR + k // 2, k % 2
            o_ref[row:row + 1, half * WA:(half + 1) * WA] = r[...]
        o_ref[PK_BSP:PK_BSP + NH, 0:CH] = dbsp_ref[...]
        o_ref[PK_WCAT:PK_ROWS, :] = dwcat_ref[...]

    args = list(vecs) + list(pairs) + [dbsp, dwcat, lsum]
    return pl.pallas_call(
        body, name="pack_small", out_shape=jax.ShapeDtypeStruct((PK_ROWS, D), F32),
        in_specs=[VMEM_FULL] * len(args), out_specs=VMEM_FULL, compiler_params=_cparams(),
    )(*args)


def _small_adam(pack_all, late_all, dcw_all, dada_all, params, behind):
    names = list(VEC_ORDER) + list(PAIR_ORDER) + ["b_spatial", "w_spatial", "conv_w", "b_ada"]
    flat = []
    for nm in names:
        flat += list(params[nm])
    n_in = 4 + len(flat)

    def body(*refs):
        pack_ref, late_ref, dcw_ref, dada_ref = refs[:4]
        prm = refs[4:n_in]
        outs = refs[n_in + 1:]

        def total(r0, nr, c0, nc):
            acc = pack_ref[0, r0:r0 + nr, c0:c0 + nc]
            for d in range(1, NDEV):
                acc = acc + pack_ref[d, r0:r0 + nr, c0:c0 + nc]
            return acc

        def emit(idx, g, getw, put):
            w_ref, m_ref, v_ref = prm[3 * idx:3 * idx + 3]
            delta, m2, v2 = _adam(getw(w_ref), g, getw(m_ref), getw(v_ref))
            for o_ref, val in zip(outs[4 * idx:4 * idx + 4], (g, delta, m2, v2)):
                put(o_ref, val)

        def whole(ref):
            return ref[...]

        def put_whole(ref, val):
            ref[...] = val

        idx = 0
        for k in range(6):
            if k < 2:
                g = late_ref[0, k:k + 1, :]
                for d in range(1, NDEV):
                    g = g + late_ref[d, k:k + 1, :]
            else:
                g = total(PK_VEC + k, 1, 0, D)
            emit(idx, g, whole, put_whole)
            idx += 1
        for k in range(7):
            emit(idx, total(PK_PAIR + k // 2, 1, (k % 2) * WA, WA), whole, put_whole)
            idx += 1
        emit(idx, total(PK_BSP, NH, 0, CH), lambda r: r[0], lambda r, val: r.__setitem__(0, val))
        idx += 1
        row = lax.broadcasted_iota(jnp.int32, (CH, CH), 0)
        col = lax.broadcasted_iota(jnp.int32, (CH, CH), 1)
        for h in range(NH):
            gh = jnp.where(col <= row, total(PK_WCAT, CH, h * CH, CH), 0.0)
            w_ref, m_ref, v_ref = prm[3 * idx:3 * idx + 3]
            delta, m2, v2 = _adam(w_ref[0, h], gh, m_ref[0, h], v_ref[0, h])
            for o_ref, val in zip(outs[4 * idx:4 * idx + 4], (gh, delta, m2, v2)):
                o_ref[0, h] = val
        idx += 1
        gcw = dcw_ref[0, 0:CK, :]
        for d in range(1, NDEV):
            gcw = gcw + dcw_ref[d, 0:CK, :]
        emit(idx, gcw, lambda r: r[0], lambda r, val: r.__setitem__(0, val))
        idx += 1
        emit(idx, jnp.sum(dada_ref[...], axis=0, keepdims=True), whole, put_whole)
        outs[-1][...] = jnp.sum(total(PK_LOSS, 1, 0, 128), axis=1, keepdims=True) * (0.5 / D)

    out_shape = []
    for nm in names:
        w = params[nm][0]
        out_shape += [jax.ShapeDtypeStruct(w.shape, F32)] * 4
    out_shape.append(jax.ShapeDtypeStruct((1, 1), F32))
    res = pl.pallas_call(
        body, name="small_adam", out_shape=out_shape,
        in_specs=[VMEM_FULL] * n_in + [ANY], out_specs=[VMEM_FULL] * len(out_shape), compiler_params=_cparams(),
    )(pack_all, late_all, dcw_all, dada_all, *flat, behind)
    return {nm: tuple(res[4 * k:4 * k + 4]) for k, nm in enumerate(names)}, res[-1].reshape(())


WEIGHTS = ['w_ada', 'b_ada', 'g_pre_f1', 'g_post_f1', 'w_f1_in', 'w_f1_out', 'g_pre_m', 'g_post_m', 'w_mix_in',
           'gmlp_norm_g', 'gmlp_norm_b', 'w_spatial', 'b_spatial', 'conv_w', 'conv_b', 'conv_norm_g', 'conv_norm_b',
           'g_out_a', 'g_out_b', 'w_mix_out', 'g_pre_f2', 'g_post_f2', 'w_f2_in', 'w_f2_out']
BIG = ('w_f1_in', 'w_f1_out', 'w_mix_in', 'w_mix_out', 'w_f2_in', 'w_f2_out')


def kernel(x, c, w_ada, b_ada, g_pre_f1, g_post_f1, w_f1_in, w_f1_out, g_pre_m, g_post_m, w_mix_in, gmlp_norm_g, gmlp_norm_b, w_spatial, b_spatial, conv_w, conv_b, conv_norm_g, conv_norm_b, g_out_a, g_out_b, w_mix_out, g_pre_f2, g_post_f2, w_f2_in, w_f2_out, loss_target, m_w_ada, m_b_ada, m_g_pre_f1, m_g_post_f1, m_w_f1_in, m_w_f1_out, m_g_pre_m, m_g_post_m, m_w_mix_in, m_gmlp_norm_g, m_gmlp_norm_b, m_w_spatial, m_b_spatial, m_conv_w, m_conv_b, m_conv_norm_g, m_conv_norm_b, m_g_out_a, m_g_out_b, m_w_mix_out, m_g_pre_f2, m_g_post_f2, m_w_f2_in, m_w_f2_out, v_w_ada, v_b_ada, v_g_pre_f1, v_g_post_f1, v_w_f1_in, v_w_f1_out, v_g_pre_m, v_g_post_m, v_w_mix_in, v_gmlp_norm_g, v_gmlp_norm_b, v_w_spatial, v_b_spatial, v_conv_w, v_conv_b, v_conv_norm_g, v_conv_norm_b, v_g_out_a, v_g_out_b, v_w_mix_out, v_g_pre_f2, v_g_post_f2, v_w_f2_in, v_w_f2_out):
    env = dict(locals())
    wts = {n: env[n] for n in WEIGHTS}
    mom = {n: env["m_" + n] for n in WEIGHTS}
    var = {n: env["v_" + n] for n in WEIGHTS}
    nb, s, _ = x.shape
    t = nb * s
    ax, ay, ac = lax.axis_index("x"), lax.axis_index("y"), lax.axis_index("c")
    j_chip = 2 * ax + ay
    dev = 4 * ax + 2 * ay + ac
    j_arr = j_chip.reshape(1).astype(jnp.int32)

    groups = (("w_f1_in",), ("w_mix_in", "w_mix_out"), ("w_f2_in", "w_f2_out"), ("w_f1_out",))
    def gather_operands(gi):
        srcs = [wts[n][0].astype(BF16) for n in groups[gi]] + ([conv_w[0]] if gi == 1 else [])
        lands = [lax.dynamic_update_index_in_dim(lax.empty((NCHIP,) + a.shape, a.dtype), a, j_chip, 0) for a in srcs]
        return srcs, lands

    def gather_start(gi, behind, operands=None):
        srcs, lands = operands or gather_operands(gi)
        plan_a, plan_b, n_b = _gather_plans([a.shape for a in srcs])
        ssem, rsem, srcs, lands, token = _split_start("gw_start%d" % gi, srcs, lands, plan_a, 3 * len(srcs), behind)
        gather[gi] = (srcs, lands, ssem, rsem, plan_a, plan_b, n_b)
        return token

    def gather_forward(gi, behind):
        srcs, lands, ssem, rsem, plan_a, plan_b, n_b = gather[gi]
        ssem, rsem, lands, token = _split_forward("gw_fwd%d" % gi, srcs, lands, ssem, rsem, plan_a, plan_b, n_b, behind)
        gather[gi] = (lands, ssem, rsem, plan_b)
        return token

    def gathered(gi, behind):
        lands, ssem, rsem, plan_b = gather[gi]
        return _split_wait("gw_wait%d" % gi, [], lands, ssem, rsem, plan_b, behind)

    gather = {}
    (c_all8,) = _allgather8("gather_c", [c.reshape(8, (nb * D) // 8)])
    token = gather_start(0, c_all8)
    c_all = c_all8.reshape(NDEV * nb, D) + token[0, 0]
    b_sh = lax.dynamic_slice(b_ada, (0, j_chip * ADA_SH), (1, ADA_SH))
    ada_sh = _ada_fwd(c_all, w_ada[0], b_sh)
    later = [gather_operands(3), gather_operands(1), gather_operands(2)]
    (ada4,) = _chip_allgather("gather_ada", [ada_sh], behind=[a for pair in later for arrs in pair for a in arrs])
    token = gather_forward(0, ada4)
    plans = [_gather_plans([a.shape for a in srcs]) for srcs, _ in later]
    started, token = _split_start_groups(
        "gw_start_later", [(srcs, lands, pa, 3 * len(srcs)) for (srcs, lands), (pa, _, _) in zip(later, plans)], token)
    for gi, (ssem, rsem, srcs, lands), (pa, pb, n_b) in zip((3, 1, 2), started, plans):
        gather[gi] = (srcs, lands, ssem, rsem, pa, pb, n_b)
    ada_me = lax.dynamic_slice(ada4, (0, dev * nb, 0), (NCHIP, nb, ADA_SH))
    ada_me = jnp.transpose(ada_me, (1, 0, 2)).reshape(nb, NMOD * D)
    sh1, sc1, gt1, sh2, sc2, gt2, sh3, sc3, gt3 = [ada_me[:, k * D:(k + 1) * D].reshape(nb, 1, D) for k in range(NMOD)]

    wcat = jnp.transpose(w_spatial[0], (1, 0, 2)).reshape(CH, NH * CH)
    wcat_t = jnp.transpose(w_spatial[0], (0, 2, 1)).reshape(NH * CH, CH)
    bspt = jnp.repeat(b_spatial[0].T, HD, axis=1)

    (w1i,) = gathered(0, token)
    p1, act1 = _ffn_up(x, sh1, sc1, g_pre_f1, w1i)
    token = gather_forward(3, act1)
    token = gather_forward(1, token)
    (w1o,) = gathered(3, token)
    w1o = w1o.reshape(DFF, D)
    x1, f1 = _ffn_down(x, act1, gt1, g_post_f1, w1o)
    wmi, wmo, cw4 = gathered(1, x1)
    wmo = wmo.reshape(D, D)
    cw_full = jnp.transpose(cw4, (1, 0, 2)).reshape(CK, WB)
    cw_pad = jnp.pad(cw_full, ((0, HALO - CK), (0, 0)))
    u, v, a, g = _mix_in_fwd(x1, sh2, sc2, g_pre_m, wmi)
    token = gather_forward(2, u)
    x2, conv, yb, m = _mix_mid_fwd(x1, u, v, a, g, gt2 + token[0, 0], gmlp_norm_g, gmlp_norm_b, wcat, bspt, cw_pad, conv_b,
                                   conv_norm_g, conv_norm_b, g_out_a, g_out_b, wmo, g_post_m)
    w2i, w2o = gathered(2, [x2, token])
    w2o = w2o.reshape(DFF, D)
    dx3, df2, p2, lsum, dg_post_f2, dgt3 = _ffn_fwd(x2, sh3, sc3, gt3, g_pre_f2, g_post_f2, w2i, w2o, target=loss_target)

    def chip4(pair, rows):
        return [arr.reshape(NCHIP, rows, arr.shape[-1]) for arr in pair]

    def scatter_start(tag, pairs, behind):
        srcs = [p[1] for p in pairs]
        lands = [lax.empty((3,) + a.shape[1:], a.dtype) for a in srcs]
        ssem, rsem, srcs, lands, token = _split_start("gs_start_" + tag, srcs, lands, _scatter_plan(len(srcs)),
                                                      3 * len(srcs), behind)
        return (srcs, lands, ssem, rsem), token

    def scatter_wait(tag, state, behind):
        srcs, lands, ssem, rsem = state
        return _split_wait("gs_wait_" + tag, srcs, lands, ssem, rsem, _scatter_plan(len(srcs)), behind)

    def allgather_start(tag, arrs, behind):
        lands = [lax.dynamic_update_index_in_dim(lax.empty((NDEV,) + a.shape, a.dtype), a, dev, 0) for a in arrs]
        ssem, rsem, srcs, lands, token = _split_start("small_start_" + tag, arrs, lands, _allgather_plan(len(arrs)),
                                                      7 * len(arrs), behind)
        return (srcs, lands, ssem, rsem), token

    def allgather_wait(tag, state, behind):
        srcs, lands, ssem, rsem = state
        return _split_wait("small_wait_" + tag, srcs, lands, ssem, rsem, _allgather_plan(len(srcs)), behind)

    out = {}
    dx2, dp2, h3, a2, dg_pre_f2, dsh3, dsc3 = _ffn_bwd(
        dx3, x2, None, p2, sh3, sc3, gt3, g_pre_f2, g_post_f2, w2i, w2o, df=df2)
    gw2i = _wgrad("wgrad_f2_in", h3.reshape(t, D), dp2.reshape(t, 2 * DFF), 2 * DFF // NCHIP, True)
    gw2o = chip4(_wgrad("wgrad_f2_out", a2.reshape(t, DFF), df2.reshape(t, D), D // 2, False), DFF // NCHIP)
    scat_f2, tok = scatter_start("f2", [gw2i, gw2o], dg_post_f2)
    dy, dm, dg_post_m, dgt2 = _mix_out_bwd(dx2, m, gt2 + tok[0, 0], g_post_m, wmo)
    gwmo = chip4(_wgrad("wgrad_mix_out", yb.reshape(t, D), dm.reshape(t, D), D // 2, False), D // NCHIP)
    (du, dv, dconv, dwcat, dbsp, dgn_g, dgn_b, dgo_a, dgo_b, dcn_g, dcn_b, dcb) = _mix_mid_bwd(
        dy, u, v, conv, gmlp_norm_g, gmlp_norm_b, wcat, wcat_t, bspt, conv_norm_g, conv_norm_b, g_out_a, g_out_b)
    dx1, dproj, h2, dg_pre_m, dsh2, dsc2, dcw = _mix_in_bwd(dx2, x1, du, dv, dconv, a, g, sh2, sc2, g_pre_m, wmi, cw_pad)
    gwmi = _wgrad("wgrad_mix_in", h2.reshape(t, D), dproj.reshape(t, 4 * WA), WA, True)

    vec_grads = dict(g_pre_m=dg_pre_m, g_post_m=dg_post_m, g_pre_f2=dg_pre_f2, g_post_f2=dg_post_f2)
    pair_grads = dict(gmlp_norm_g=dgn_g, gmlp_norm_b=dgn_b, conv_b=dcb, conv_norm_g=dcn_g, conv_norm_b=dcn_b,
                      g_out_a=dgo_a, g_out_b=dgo_b)
    pack = _pack_small([vec_grads[n] for n in VEC_ORDER[2:]], [pair_grads[n] for n in PAIR_ORDER], dbsp, dwcat, lsum)
    dada_early = jnp.concatenate([q.reshape(nb, D) for q in (dsh2, dsc2, dgt2, dsh3, dsc3, dgt3)], axis=1)
    small_early = [pack, dcw, dada_early.reshape(8, (nb * 6 * D) // 8)]
    mix_bf16 = [gwmi[1], gwmo[1]]
    (s_mix, s_early), tok2 = _split_start_groups("gs_start_mix_small", [
        (mix_bf16, [lax.empty((3,) + a.shape[1:], a.dtype) for a in mix_bf16], _scatter_plan(2), 6),
        (small_early, [lax.dynamic_update_index_in_dim(lax.empty((NDEV,) + a.shape, a.dtype), a, dev, 0) for a in small_early],
         _allgather_plan(3), 21)], dg_pre_m)
    scat_mix = (s_mix[2], s_mix[3], s_mix[0], s_mix[1])
    early = (s_early[2], s_early[3], s_early[0], s_early[1])
    grad_x, dp1, h1, a1, df1, dg_pre_f1, dg_post_f1, dsh1, dsc1, dgt1 = _ffn_bwd(
        dx1, x, f1, p1, sh1 + tok2[0, 0], sc1, gt1, g_pre_f1, g_post_f1, w1i, w1o)
    late_pack = _pack_late([dg_pre_f1, dg_post_f1] + [q.reshape(nb, D) for q in (dsh1, dsc1, dgt1)])
    late, tok2 = allgather_start("late", [late_pack], dg_post_f1)
    gw1i = _wgrad("wgrad_f1_in", h1.reshape(t, D), dp1.reshape(t, 2 * DFF), 2 * DFF // NCHIP, True)
    gw1o = chip4(_wgrad("wgrad_f1_out", a1.reshape(t, DFF), df1.reshape(t, D), D // 2, False), DFF // NCHIP)
    def d2d_start(tag, srcs, lands, plan, behind):
        ssem, rsem, srcs, lands, token = _split_start("d2d_start_" + tag, srcs, lands, plan, len(srcs), behind)
        return (srcs, lands, ssem, rsem, plan), token

    def d2d_wait(tag, state, behind):
        srcs, lands, ssem, rsem, plan = state
        return _split_wait("d2d_wait_" + tag, srcs, lands, ssem, rsem, plan, behind)

    def swap_start(tag, parts, behind):
        return d2d_start(tag, parts, [lax.empty(a.shape, a.dtype) for a in parts], _swap_plan(len(parts)), behind)

    def sums(names, pairs, recv):
        return [_sum4("sum4_" + n, pairs[k][0], recv[k], j_arr) for k, n in enumerate(names)]

    def update(names, part, other):
        for k, n in enumerate(names):
            out[n] = tuple(r[None] for r in _adam_big("adam_" + n, wts[n][0], mom[n][0], var[n][0], part[k], other[k]))

    c_arr = ac.reshape(1).astype(jnp.int32)
    halves = [gw1i[1], gw1o[1]]
    pair_st, tok = d2d_start("pair", halves, [lax.empty((a.shape[0], a.shape[1] // 2, a.shape[2]), a.dtype) for a in halves],
                             _pair_plan([a.shape for a in halves]), tok2)
    names_f2, names_mix, names_f1 = ("w_f2_in", "w_f2_out"), ("w_mix_in", "w_mix_out"), ("w_f1_in", "w_f1_out")
    part_f2 = sums(names_f2, [gw2i, gw2o], scatter_wait("f2", scat_f2, tok))
    sib = d2d_wait("pair", pair_st, part_f2)
    pair_i = _pair_sum("pairsum_f1_in", gw1i[0], sib[0], c_arr)
    pair_o = _pair_sum("pairsum_f1_out", gw1o[0], sib[1], c_arr)
    scat_f1, tok = scatter_start("f1", [pair_i, pair_o], tok2)
    swap_f2, tok = swap_start("swap_f2", part_f2, tok)
    part_mix = sums(names_mix, [gwmi, gwmo], scatter_wait("mix", scat_mix, tok))
    swap_mix, tok = swap_start("swap_mix", part_mix, part_mix[1])

    pack_all, dcw_all, dada_early8 = allgather_wait("early", early, tok)
    (late_all,) = allgather_wait("late", late, pack_all)
    dada_late = jnp.transpose(late_all[:, 2:8, :].reshape(NDEV, 3, nb, D), (0, 2, 1, 3)).reshape(NDEV * nb, 3 * D)
    dada_all = jnp.concatenate([dada_late, dada_early8.reshape(NDEV * nb, 6 * D)], axis=1)
    dada_sh = lax.dynamic_slice(dada_all, (0, j_chip * ADA_SH), (NDEV * nb, ADA_SH))
    out["w_ada"] = tuple(r[None] for r in _ada_bwd_adam(c_all, dada_sh, w_ada[0], m_w_ada[0], v_w_ada[0]))
    update(names_f2, part_f2, d2d_wait("swap_f2", swap_f2, out["w_ada"][3]))
    update(names_mix, part_mix, d2d_wait("swap_mix", swap_mix, out["w_f2_out"][3]))

    mine = sums(names_f1, [pair_i, pair_o], scatter_wait("f1", scat_f1, out["w_mix_out"][3]))
    swap_f1, tok = swap_start("swap_f1", mine, mine[1])
    dcw_mine = lax.dynamic_slice(dcw_all, (0, 0, j_chip * (WB // NCHIP)), (NDEV, HALO, WB // NCHIP))
    small = {n: (wts[n], mom[n], var[n]) for n in list(VEC_ORDER) + list(PAIR_ORDER) + ["b_spatial", "w_spatial", "conv_w", "b_ada"]}
    small_out, loss = _small_adam(pack_all, late_all, dcw_mine, dada_all, small, tok)
    out.update(small_out)
    theirs = d2d_wait("swap_f1", swap_f1, out["b_ada"][3])
    for k, n in enumerate(names_f1):
        out[n] = tuple(r[None] for r in _adam_halves("adam_" + n, wts[n][0], mom[n][0], var[n][0], mine[k], theirs[k],
                                                     c_arr))

    res = [loss, grad_x]
    for k in range(4):
        res += [out[n][k] for n in WEIGHTS]
    return tuple(res)
```

```python
import jax
import jax.numpy as jnp
from jax import lax
from jax.experimental import pallas as pl
from jax.experimental.pallas import tpu as pltpu

D = 1024
DFF = 2816
WA = 512
WB = 512
NH = 8
HD = 64
CH = 128
CK = 31
HALO = 32
NMOD = 9
EPS = 1e-6
NCHIP = 4
NDEV = 8
FBLK = DFF // 2
ADA_SH = NMOD * D // NCHIP

LR, B1, B2, EPS_A, WD, STEP = 0.001, 0.9, 0.999, 1e-08, 0.01, 10

F32 = jnp.float32
BF16 = jnp.bfloat16
MESH = pl.DeviceIdType.MESH
ANY = pl.BlockSpec(memory_space=pl.ANY)
VMEM_FULL = pl.BlockSpec(memory_space=pltpu.VMEM)
VMEM_LIMIT = 56 * 1024 * 1024
WGRAD_VMEM_BUDGET = 52 * 1024 * 1024
TM = 512
TM_FFN_BWD = 256

NT = (((1,), (1,)), ((), ()))
TN = (((0,), (0,)), ((), ()))


def _dot(a, b):
    return jnp.dot(a, b, preferred_element_type=F32)


def _dot_nt(a, b):
    return lax.dot_general(a, b, NT, preferred_element_type=F32)


def _dot_tn(a, b):
    return lax.dot_general(a, b, TN, preferred_element_type=F32)


def _cparams():
    return pltpu.CompilerParams(vmem_limit_bytes=VMEM_LIMIT)


def _allgather8(name, arrs):
    n = len(arrs)
    remote = _allgather_plan(n)

    def plan(x, y, c, ins, outs):
        _, sends = remote(x, y, c, ins, outs)
        return [(ins[a], outs[a].at[4 * x + 2 * y + c]) for a in range(n)], sends

    shapes = [jax.ShapeDtypeStruct((NDEV,) + a.shape, a.dtype) for a in arrs]
    return _run_exchange(name, arrs, shapes, plan, n, 7 * n)


def _chip_relations(x, y):
    return [(1 - x, y), (x, 1 - y), (1 - x, 1 - y)]


def _exchange(name, arrs, out_shapes, plan):
    n = len(arrs)
    n_out = len(out_shapes)

    def body(*refs):
        ins, outs = refs[:n], refs[n:n + n_out]
        send_sems, recv_sems, local_sems = refs[n + n_out:]
        x, y, c = lax.axis_index("x"), lax.axis_index("y"), lax.axis_index("c")
        local, sends = plan(x, y, c, ins, outs)
        locs = [pltpu.make_async_copy(s, d, local_sems.at[i]) for i, (s, d) in enumerate(local)]
        for loc in locs:
            loc.start()
        cps = [pltpu.make_async_remote_copy(src_ref=s, dst_ref=d, send_sem=send_sems.at[i], recv_sem=recv_sems.at[i],
                                            device_id=peer, device_id_type=MESH)
               for i, (s, d, peer, _) in enumerate(sends)]
        for cp in cps:
            cp.start()
        for i, (s, _, peer, landing) in enumerate(sends):
            pltpu.make_async_remote_copy(src_ref=s, dst_ref=landing, send_sem=send_sems.at[i], recv_sem=recv_sems.at[i],
                                         device_id=peer, device_id_type=MESH).wait_recv()
        for cp in cps:
            cp.wait_send()
        for loc in locs:
            loc.wait()

    return n, n_out, body


def _run_exchange(name, arrs, out_shapes, plan, n_local, n_send):
    n, n_out, body = _exchange(name, arrs, out_shapes, plan)
    return pl.pallas_call(
        body, name=name, out_shape=out_shapes,
        in_specs=[ANY] * n, out_specs=[ANY] * n_out,
        scratch_shapes=[pltpu.SemaphoreType.DMA((n_send,)), pltpu.SemaphoreType.DMA((n_send,)),
                        pltpu.SemaphoreType.DMA((max(n_local, 1),))],
    )(*arrs)


def _chip_allgather(name, arrs, behind=()):
    n = len(arrs)

    def plan(x, y, c, ins, outs):
        j_me = 2 * x + y
        local = [(ins[a], outs[a].at[j_me]) for a in range(n)]
        sends = []
        for a in range(n):
            for (px, py) in _chip_relations(x, y):
                sends.append((ins[a], outs[a].at[j_me], (px, py, c), outs[a].at[2 * px + py]))
        return local, sends

    shapes = [jax.ShapeDtypeStruct((NCHIP,) + a.shape, a.dtype) for a in arrs]
    return _run_exchange(name, list(arrs) + list(behind), shapes, plan, n, 3 * n)


HBM = pl.BlockSpec(memory_space=pltpu.HBM)
SEM = pl.BlockSpec(memory_space=pltpu.SEMAPHORE)
EFFECT = pltpu.SideEffectType.DATAFLOW_SIDE_EFFECTING


def _split_start_groups(name, groups, after):
    n_src = [len(g[0]) for g in groups]
    n_land = [len(g[1]) for g in groups]
    all_srcs = [pltpu.with_memory_space_constraint(a, pltpu.HBM) for g in groups for a in g[0]]
    all_lands = [pltpu.with_memory_space_constraint(a, pltpu.HBM) for g in groups for a in g[1]]
    ns, nl, ng = len(all_srcs), len(all_lands), len(groups)

    def body(*refs):
        src_refs, land_refs = refs[:ns], refs[ns:ns + nl]
        sem_refs = refs[ns + nl + 1:ns + nl + 1 + 2 * ng]
        token = refs[-1]
        x, y, c = lax.axis_index("x"), lax.axis_index("y"), lax.axis_index("c")
        at_src = at_land = 0
        for gi, (_, _, plan, _) in enumerate(groups):
            _, sends = plan(x, y, c, src_refs[at_src:at_src + n_src[gi]], land_refs[at_land:at_land + n_land[gi]])
            for i, (s, d, peer, _) in enumerate(sends):
                pltpu.make_async_remote_copy(src_ref=s, dst_ref=d, send_sem=sem_refs[2 * gi].at[i],
                                             recv_sem=sem_refs[2 * gi + 1].at[i], device_id=peer, device_id_type=MESH).start()
            at_src += n_src[gi]
            at_land += n_land[gi]
        token[...] = jnp.zeros_like(token)

    sems = [pltpu.SemaphoreType.DMA((g[3],)) for g in groups for _ in range(2)]
    res = pl.pallas_call(
        body, name=name,
        out_shape=(*sems, *[pltpu.HBM(a.shape, a.dtype) for a in all_lands], jax.ShapeDtypeStruct((8, 128), F32)),
        in_specs=[HBM] * (ns + nl) + [ANY],
        out_specs=(*([SEM] * (2 * ng)), *([HBM] * nl), pl.BlockSpec(memory_space=pltpu.VMEM)),
        input_output_aliases={ns + i: 2 * ng + i for i in range(nl)},
        compiler_params=pltpu.CompilerParams(has_side_effects=EFFECT),
    )(*all_srcs, *all_lands, after)
    out, at_src, at_land = [], 0, 2 * ng
    for gi in range(ng):
        out.append((res[2 * gi], res[2 * gi + 1], all_srcs[at_src:at_src + n_src[gi]],
                    list(res[at_land:at_land + n_land[gi]])))
        at_src += n_src[gi]
        at_land += n_land[gi]
    return out, res[-1]


def _split_start(name, srcs, lands, plan, n_send, after):
    (group,), token = _split_start_groups(name, [(srcs, lands, plan, n_send)], after)
    return (*group, token)


def _split_wait(name, srcs, lands, send_sems, recv_sems, plan, after):
    n, nl = len(srcs), len(lands)
    afters = list(after) if isinstance(after, (list, tuple)) else [after]

    def body(*refs):
        src, land = refs[:n], refs[n:n + nl]
        send_sems, recv_sems = refs[n + nl], refs[n + nl + 1]
        x, y, c = lax.axis_index("x"), lax.axis_index("y"), lax.axis_index("c")
        _, sends = plan(x, y, c, src, land)
        for i, (s, _, peer, landing) in enumerate(sends):
            cp = pltpu.make_async_remote_copy(src_ref=s, dst_ref=landing, send_sem=send_sems.at[i],
                                              recv_sem=recv_sems.at[i], device_id=peer, device_id_type=MESH)
            cp.wait_send()
            cp.wait_recv()

    thru = [pltpu.HBM(a.shape, a.dtype) for a in lands]
    res = pl.pallas_call(
        body, name=name, out_shape=tuple(thru),
        in_specs=[HBM] * (n + nl) + [SEM, SEM] + [ANY] * len(afters), out_specs=tuple([HBM] * nl),
        input_output_aliases={n + i: i for i in range(nl)},
        compiler_params=pltpu.CompilerParams(has_side_effects=EFFECT),
    )(*srcs, *lands, send_sems, recv_sems, *afters)
    return list(res)


def _split_forward(name, srcs, lands, send_a, recv_a, plan_a, plan_b, n_b, after):
    n, nl = len(srcs), len(lands)

    def body(*refs):
        src, land = refs[:n], refs[n:n + nl]
        send_a, recv_a = refs[n + nl], refs[n + nl + 1]
        send_b, recv_b = refs[n + nl + 3], refs[n + nl + 4]
        token = refs[-1]
        x, y, c = lax.axis_index("x"), lax.axis_index("y"), lax.axis_index("c")
        _, first = plan_a(x, y, c, src, land)
        for i, (s, _, peer, landing) in enumerate(first):
            cp = pltpu.make_async_remote_copy(src_ref=s, dst_ref=landing, send_sem=send_a.at[i],
                                              recv_sem=recv_a.at[i], device_id=peer, device_id_type=MESH)
            cp.wait_send()
            cp.wait_recv()
        _, second = plan_b(x, y, c, src, land)
        for i, (s, d, peer, _) in enumerate(second):
            pltpu.make_async_remote_copy(src_ref=s, dst_ref=d, send_sem=send_b.at[i], recv_sem=recv_b.at[i],
                                         device_id=peer, device_id_type=MESH).start()
        token[...] = jnp.zeros_like(token)

    thru = [pltpu.HBM(a.shape, a.dtype) for a in lands]
    res = pl.pallas_call(
        body, name=name,
        out_shape=(pltpu.SemaphoreType.DMA((n_b,)), pltpu.SemaphoreType.DMA((n_b,)), *thru,
                   jax.ShapeDtypeStruct((8, 128), F32)),
        in_specs=[HBM] * (n + nl) + [SEM, SEM, ANY],
        out_specs=(SEM, SEM, *([HBM] * nl), pl.BlockSpec(memory_space=pltpu.VMEM)),
        input_output_aliases={n + i: 2 + i for i in range(nl)},
        compiler_params=pltpu.CompilerParams(has_side_effects=EFFECT),
    )(*srcs, *lands, send_a, recv_a, after)
    return res[0], res[1], list(res[2:2 + nl]), res[-1]


def _gather_plans(shapes):
    n = len(shapes)

    def halves(a, c):
        rows = shapes[a][0] // 2
        return pl.ds(pl.multiple_of(c * rows, 16), rows), pl.ds(pl.multiple_of((1 - c) * rows, 16), rows)

    def split(a):
        return shapes[a][0] % 32 == 0

    def plan_a(x, y, c, src, land):
        j_me = 2 * x + y
        sends = []
        for a in range(n):
            for (px, py) in _chip_relations(x, y):
                if split(a):
                    mine, _ = halves(a, c)
                    sends.append((src[a].at[mine], land[a].at[j_me, mine], (px, py, c), land[a].at[2 * px + py, mine]))
                else:
                    sends.append((src[a], land[a].at[j_me], (px, py, c), land[a].at[2 * px + py]))
        return [], sends

    def plan_b(x, y, c, src, land):
        sends = []
        for a in range(n):
            if split(a):
                mine, other = halves(a, c)
                for (px, py) in _chip_relations(x, y):
                    j = 2 * px + py
                    sends.append((land[a].at[j, mine], land[a].at[j, mine], (x, y, 1 - c), land[a].at[j, other]))
        return [], sends

    n_b = 3 * sum(1 for a in range(n) if split(a))
    return plan_a, plan_b, n_b


def _allgather_plan(n):
    flips = [(dx, dy, dc) for dx in (0, 1) for dy in (0, 1) for dc in (0, 1) if dx or dy or dc]

    def plan(x, y, c, src, land):
        sends = []
        for a in range(n):
            for dx, dy, dc in flips:
                px, py, pc = x ^ dx, y ^ dy, c ^ dc
                sends.append((src[a], land[a].at[4 * x + 2 * y + c], (px, py, pc), land[a].at[4 * px + 2 * py + pc]))
        return [], sends

    return plan


def _scatter_plan(n):
    def plan(x, y, c, src, land):
        sends = []
        for a in range(n):
            for k, (px, py) in enumerate(_chip_relations(x, y)):
                sends.append((src[a].at[2 * px + py], land[a].at[k], (px, py, c), land[a].at[k]))
        return [], sends

    return plan


def _rms(x):
    r = lax.rsqrt(jnp.mean(x * x, axis=-1, keepdims=True) + EPS)
    return x * r, r


def _rms_bwd(dy, n, r, g):
    dg = jnp.sum(dy * n, axis=0, keepdims=True)
    dn = dy * g
    dx = r * (dn - n * jnp.mean(dn * n, axis=-1, keepdims=True))
    return dx, dg


def _ln(x):
    mu = jnp.mean(x, axis=-1, keepdims=True)
    xc = x - mu
    rstd = lax.rsqrt(jnp.mean(xc * xc, axis=-1, keepdims=True) + EPS)
    return xc * rstd, rstd


def _ln_bwd(dy, xhat, rstd, g):
    dg = jnp.sum(dy * xhat, axis=0, keepdims=True)
    db = jnp.sum(dy, axis=0, keepdims=True)
    dxh = dy * g
    dx = rstd * (dxh - jnp.mean(dxh, axis=-1, keepdims=True) - xhat * jnp.mean(dxh * xhat, axis=-1, keepdims=True))
    return dx, dg, db


def _sigmoid(x):
    return jax.nn.sigmoid(x)


def _dsilu(x, s):
    return s * (1.0 + x * (1.0 - s))


def _adam(w, g, m, v):
    m = B1 * m + (1.0 - B1) * g
    v = B2 * v + (1.0 - B2) * (g * g)
    m_hat = m / (1.0 - B1 ** STEP)
    v_hat = v / (1.0 - B2 ** STEP)
    delta = -LR * (m_hat / (jnp.sqrt(v_hat) + EPS_A) + WD * w)
    return delta, m, v


def _head_mask(shape):
    lane = lax.broadcasted_iota(jnp.int32, shape, len(shape) - 1)
    return [(lane >= h * HD) & (lane < (h + 1) * HD) for h in range(NH)]


def _first(b, i):
    return jnp.logical_and(b == 0, i == 0)


def _acc(ref, val, first):
    @pl.when(first)
    def _():
        ref[...] = val

    @pl.when(jnp.logical_not(first))
    def _():
        ref[...] += val


def _ada_fwd(c_all, w_sh, b_sh):
    nb = c_all.shape[0]
    tn = 768

    def body(c_ref, w_ref, b_ref, o_ref):
        cv = c_ref[...]
        cs = (cv * _sigmoid(cv)).astype(BF16)
        o_ref[...] = _dot(cs, w_ref[...].astype(BF16)) + b_ref[...]

    return pl.pallas_call(
        body, name="ada_fwd", grid=(ADA_SH // tn,),
        out_shape=jax.ShapeDtypeStruct((nb, ADA_SH), F32),
        in_specs=[pl.BlockSpec((nb, D), lambda j: (0, 0)), pl.BlockSpec((D, tn), lambda j: (0, j)),
                  pl.BlockSpec((1, tn), lambda j: (0, j))],
        out_specs=pl.BlockSpec((nb, tn), lambda j: (0, j)),
        compiler_params=_cparams(),
    )(c_all, w_sh, b_sh)


def _ada_bwd_adam(c_all, dada_sh, w, m, v):
    nb = c_all.shape[0]
    tn = 768

    def body(c_ref, d_ref, w_ref, m_ref, v_ref, g_out, d_out, m_out, v_out):
        cv = c_ref[...]
        cs = (cv * _sigmoid(cv)).astype(BF16)
        g = _dot_tn(cs, d_ref[...].astype(BF16))
        delta, m2, v2 = _adam(w_ref[...], g, m_ref[...], v_ref[...])
        g_out[...] = g
        d_out[...] = delta
        m_out[...] = m2
        v_out[...] = v2

    big = pl.BlockSpec((D, tn), lambda j: (0, j))
    shape = jax.ShapeDtypeStruct((D, ADA_SH), F32)
    return pl.pallas_call(
        body, name="ada_bwd_adam", grid=(ADA_SH // tn,),
        out_shape=[shape] * 4,
        in_specs=[pl.BlockSpec((nb, D), lambda j: (0, 0)), pl.BlockSpec((nb, tn), lambda j: (0, j)), big, big, big],
        out_specs=[big] * 4,
        compiler_params=_cparams(),
    )(c_all, dada_sh, w, m, v)


def _tok_specs(tm, width):
    return pl.BlockSpec((1, tm, width), lambda b, i: (b, i, 0))


def _mod_spec():
    return pl.BlockSpec((1, 1, D), lambda b, i: (b, 0, 0))


def _row_spec(width=D):
    return pl.BlockSpec((1, width), lambda b, i: (0, 0))


def _ffn_loss_fwd(x, sh, sc, gt, g_pre, g_post, w_in4, w_out, target):
    nb, s, _ = x.shape
    tm = min(TM, s)

    def body(x_ref, sh_ref, sc_ref, gt_ref, gpre_ref, gpost_ref, win_ref, wout_ref, tgt_ref,
             xo_ref, df_ref, p_ref, ls_ref, dgpost_ref, dgt_ref):
        xv = x_ref[0]
        n, _ = _rms(xv)
        h = (n * gpre_ref[...]) * (1.0 + sc_ref[0]) + sh_ref[0]
        hb = h.astype(BF16)
        acc = jnp.zeros((tm, D), F32)
        for j in range(2):
            gate = _dot(hb, win_ref[j])
            up = _dot(hb, win_ref[2 + j])
            p_ref[0, :, j * FBLK:(j + 1) * FBLK] = gate.astype(BF16)
            p_ref[0, :, DFF + j * FBLK:DFF + (j + 1) * FBLK] = up.astype(BF16)
            a = (gate * _sigmoid(gate)) * up
            acc = acc + _dot(a.astype(BF16), wout_ref[j * FBLK:(j + 1) * FBLK, :])
        nf, q = _rms(acc)
        gpost = gpost_ref[...]
        half_gate = 0.5 * gt_ref[0]
        out = xv + half_gate * (nf * gpost)
        first = _first(pl.program_id(0), pl.program_id(1))
        err = out - tgt_ref[0]
        dout = err * (1.0 / D)
        xo_ref[0] = dout
        row = jnp.sum(err * err, axis=0, keepdims=True)
        part = row[:, 0:128]
        for k in range(1, D // 128):
            part = part + row[:, k * 128:(k + 1) * 128]
        _acc(ls_ref, part, first)
        df, dgpost = _rms_bwd(dout * half_gate, nf, q, gpost)
        df_ref[0] = df.astype(BF16)
        _acc(dgpost_ref, dgpost, first)
        _acc(dgt_ref, jnp.sum(dout * (0.5 * (nf * gpost)), axis=0, keepdims=True)[None], pl.program_id(1) == 0)

    tok = _tok_specs(tm, D)
    return pl.pallas_call(
        body, name="ffn_loss_fwd", grid=(nb, s // tm),
        out_shape=[jax.ShapeDtypeStruct((nb, s, D), F32), jax.ShapeDtypeStruct((nb, s, D), BF16),
                   jax.ShapeDtypeStruct((nb, s, 2 * DFF), BF16), jax.ShapeDtypeStruct((1, 128), F32),
                   jax.ShapeDtypeStruct((1, D), F32), jax.ShapeDtypeStruct((nb, 1, D), F32)],
        in_specs=[tok, _mod_spec(), _mod_spec(), _mod_spec(), _row_spec(), _row_spec(), VMEM_FULL, VMEM_FULL, tok],
        out_specs=[tok, tok, _tok_specs(tm, 2 * DFF), pl.BlockSpec((1, 128), lambda b, i: (0, 0)), _row_spec(), _mod_spec()],
        compiler_params=_cparams(),
    )(x, sh, sc, gt, g_pre, g_post, w_in4, w_out, target)


def _ffn_up(x, sh, sc, g_pre, w_in4):
    nb, s, _ = x.shape
    tm = min(TM, s)

    def body(x_ref, sh_ref, sc_ref, gpre_ref, win_ref, p_ref, a_ref):
        n, _ = _rms(x_ref[0])
        hb = ((n * gpre_ref[...]) * (1.0 + sc_ref[0]) + sh_ref[0]).astype(BF16)
        for j in range(2):
            gate = _dot(hb, win_ref[j])
            up = _dot(hb, win_ref[2 + j])
            p_ref[0, :, j * FBLK:(j + 1) * FBLK] = gate.astype(BF16)
            p_ref[0, :, DFF + j * FBLK:DFF + (j + 1) * FBLK] = up.astype(BF16)
            a_ref[0, :, j * FBLK:(j + 1) * FBLK] = ((gate * _sigmoid(gate)) * up).astype(BF16)

    return pl.pallas_call(
        body, name="ffn_up", grid=(nb, s // tm),
        out_shape=[jax.ShapeDtypeStruct((nb, s, 2 * DFF), BF16), jax.ShapeDtypeStruct((nb, s, DFF), BF16)],
        in_specs=[_tok_specs(tm, D), _mod_spec(), _mod_spec(), _row_spec(), VMEM_FULL],
        out_specs=[_tok_specs(tm, 2 * DFF), _tok_specs(tm, DFF)],
        compiler_params=_cparams(),
    )(x, sh, sc, g_pre, w_in4)


def _ffn_down(x, a, gt, g_post, w_out):
    nb, s, _ = x.shape
    tm = min(TM, s)

    def body(x_ref, a_ref, gt_ref, gpost_ref, wout_ref, xo_ref, f_ref):
        acc = _dot(a_ref[0], wout_ref[...])
        f_ref[0] = acc
        nf, _ = _rms(acc)
        xo_ref[0] = x_ref[0] + (0.5 * gt_ref[0]) * (nf * gpost_ref[...])

    tok = _tok_specs(tm, D)
    shape = jax.ShapeDtypeStruct((nb, s, D), F32)
    return pl.pallas_call(
        body, name="ffn_down", grid=(nb, s // tm), out_shape=[shape, shape],
        in_specs=[tok, _tok_specs(tm, DFF), _mod_spec(), _row_spec(), VMEM_FULL],
        out_specs=[tok, tok],
        compiler_params=_cparams(),
    )(x, a, gt, g_post, w_out)


def _ffn_bwd(dxo, x, f, p, sh, sc, gt, g_pre, g_post, w_in4, w_out, df=None):
    nb, s, _ = x.shape
    tm = min(TM_FFN_BWD, s)
    given = df is not None

    def body(*refs):
        if given:
            (dxo_ref, x_ref, dfin_ref, p_ref, sh_ref, sc_ref, gpre_ref, win_ref, wout_ref,
             dx_ref, dp_ref, h_ref, a_ref, dgpre_ref, dsh_ref, dsc_ref) = refs
        else:
            (dxo_ref, x_ref, f_ref, p_ref, sh_ref, sc_ref, gt_ref, gpre_ref, gpost_ref, win_ref, wout_ref,
             dx_ref, dp_ref, h_ref, a_ref, df_ref, dgpre_ref, dgpost_ref, dsh_ref, dsc_ref, dgt_ref) = refs
        b, i = pl.program_id(0), pl.program_id(1)
        dxo_v = dxo_ref[0]
        if given:
            dfb = dfin_ref[0]
        else:
            nf, q = _rms(f_ref[0])
            gpost = gpost_ref[...]
            dgt = jnp.sum(dxo_v * (0.5 * (nf * gpost)), axis=0, keepdims=True)
            do = dxo_v * (0.5 * gt_ref[0])
            dfv, dgpost = _rms_bwd(do, nf, q, gpost)
            dfb = dfv.astype(BF16)
            df_ref[0] = dfb
        xv = x_ref[0]
        n, r = _rms(xv)
        gpre = gpre_ref[...]
        ng = n * gpre
        scale1 = 1.0 + sc_ref[0]
        h = ng * scale1 + sh_ref[0]
        h_ref[0] = h.astype(BF16)
        dh = jnp.zeros((tm, D), F32)
        for j in range(2):
            gate = p_ref[0, :, j * FBLK:(j + 1) * FBLK].astype(F32)
            up = p_ref[0, :, DFF + j * FBLK:DFF + (j + 1) * FBLK].astype(F32)
            sg = _sigmoid(gate)
            act = gate * sg
            a_ref[0, :, j * FBLK:(j + 1) * FBLK] = (act * up).astype(BF16)
            da = _dot_nt(dfb, wout_ref[j * FBLK:(j + 1) * FBLK, :])
            dgate = (da * up * _dsilu(gate, sg)).astype(BF16)
            dup = (da * act).astype(BF16)
            dp_ref[0, :, j * FBLK:(j + 1) * FBLK] = dgate
            dp_ref[0, :, DFF + j * FBLK:DFF + (j + 1) * FBLK] = dup
            dh = dh + _dot_nt(dgate, win_ref[j]) + _dot_nt(dup, win_ref[2 + j])
        dsh = jnp.sum(dh, axis=0, keepdims=True)
        dsc = jnp.sum(dh * ng, axis=0, keepdims=True)
        dxn, dgpre = _rms_bwd(dh * scale1, n, r, gpre)
        dx_ref[0] = dxo_v + dxn
        _acc(dgpre_ref, dgpre, _first(b, i))
        _acc(dsh_ref, dsh[None], i == 0)
        _acc(dsc_ref, dsc[None], i == 0)
        if not given:
            _acc(dgpost_ref, dgpost, _first(b, i))
            _acc(dgt_ref, dgt[None], i == 0)

    tok = _tok_specs(tm, D)
    mod_shape = jax.ShapeDtypeStruct((nb, 1, D), F32)
    row_shape = jax.ShapeDtypeStruct((1, D), F32)
    big = [jax.ShapeDtypeStruct((nb, s, D), F32), jax.ShapeDtypeStruct((nb, s, 2 * DFF), BF16),
           jax.ShapeDtypeStruct((nb, s, D), BF16), jax.ShapeDtypeStruct((nb, s, DFF), BF16)]
    big_specs = [tok, _tok_specs(tm, 2 * DFF), tok, _tok_specs(tm, DFF)]
    if given:
        return pl.pallas_call(
            body, name="ffn_bwd_after_loss", grid=(nb, s // tm),
            out_shape=big + [row_shape, mod_shape, mod_shape],
            in_specs=[tok, tok, tok, _tok_specs(tm, 2 * DFF), _mod_spec(), _mod_spec(), _row_spec(), VMEM_FULL, VMEM_FULL],
            out_specs=big_specs + [_row_spec(), _mod_spec(), _mod_spec()],
            compiler_params=_cparams(),
        )(dxo, x, df, p, sh, sc, g_pre, w_in4, w_out)
    return pl.pallas_call(
        body, name="ffn_bwd", grid=(nb, s // tm),
        out_shape=big + [jax.ShapeDtypeStruct((nb, s, D), BF16), row_shape, row_shape, mod_shape, mod_shape, mod_shape],
        in_specs=[tok, tok, tok, _tok_specs(tm, 2 * DFF), _mod_spec(), _mod_spec(), _mod_spec(), _row_spec(), _row_spec(),
                  VMEM_FULL, VMEM_FULL],
        out_specs=big_specs + [tok, _row_spec(), _row_spec(), _mod_spec(), _mod_spec(), _mod_spec()],
        compiler_params=_cparams(),
    )(dxo, x, f, p, sh, sc, gt, g_pre, g_post, w_in4, w_out)


def _wgrad(name, a, b, col_block, chip_major):
    t, ka = a.shape
    n = b.shape[1]
    def vmem_bytes(rows):
        return 2 * 2 * rows * (ka + col_block) + 4 * ka * col_block + 2 * (4 + 2) * ka * col_block

    tk = min(t, 512)
    while tk * 2 <= t and t % (tk * 2) == 0 and vmem_bytes(tk * 2) <= WGRAD_VMEM_BUDGET:
        tk *= 2
    nk = t // tk
    nblk = n // col_block

    def body(a_ref, b_ref, o_ref, obf_ref, acc_ref):
        k = pl.program_id(1)

        @pl.when(k == 0)
        def _():
            acc_ref[...] = jnp.zeros_like(acc_ref)

        acc_ref[...] += _dot_tn(a_ref[...], b_ref[...])

        @pl.when(k == nk - 1)
        def _():
            val = acc_ref[...]
            if chip_major:
                o_ref[0] = val
                obf_ref[0] = val.astype(BF16)
            else:
                o_ref[...] = val
                obf_ref[...] = val.astype(BF16)

    if chip_major:
        shape = (nblk, ka, col_block)
        ospec = pl.BlockSpec((1, ka, col_block), lambda j, k: (j, 0, 0))
    else:
        shape = (ka, n)
        ospec = pl.BlockSpec((ka, col_block), lambda j, k: (0, j))
    return pl.pallas_call(
        body, name=name, grid=(nblk, nk),
        out_shape=[jax.ShapeDtypeStruct(shape, F32), jax.ShapeDtypeStruct(shape, BF16)],
        in_specs=[pl.BlockSpec((tk, ka), lambda j, k: (k, 0)), pl.BlockSpec((tk, col_block), lambda j, k: (k, j))],
        out_specs=[ospec, ospec],
        scratch_shapes=[pltpu.VMEM((ka, col_block), F32)],
        compiler_params=_cparams(),
    )(a, b)


def _mix_in_fwd(x, sh, sc, g_pre, w_mi4):
    nb, s, _ = x.shape
    tm = min(TM, s)

    def body(x_ref, sh_ref, sc_ref, gpre_ref, w_ref, u_ref, v_ref, a_ref, g_ref):
        n, _ = _rms(x_ref[0])
        hb = ((n * gpre_ref[...]) * (1.0 + sc_ref[0]) + sh_ref[0]).astype(BF16)
        for k, o_ref in enumerate((u_ref, v_ref, a_ref, g_ref)):
            o_ref[0] = _dot(hb, w_ref[k])

    shape = jax.ShapeDtypeStruct((nb, s, WA), F32)
    return pl.pallas_call(
        body, name="mix_in_fwd", grid=(nb, s // tm),
        out_shape=[shape] * 4,
        in_specs=[_tok_specs(tm, D), _mod_spec(), _mod_spec(), _row_spec(), VMEM_FULL],
        out_specs=[_tok_specs(tm, WA)] * 4,
        compiler_params=_cparams(),
    )(x, sh, sc, g_pre, w_mi4)


def _spatial_weights(wcat_ref, transposed):
    w = wcat_ref[...]
    row = lax.broadcasted_iota(jnp.int32, w.shape, 0)
    col = lax.broadcasted_iota(jnp.int32, w.shape, 1)
    keep = ((row & (CH - 1)) <= col) if transposed else ((col & (CH - 1)) <= row)
    return jnp.where(keep, w, 0.0).astype(BF16)


def _expand_heads(vc, masks):
    return jnp.concatenate([jnp.where(mk, vc, jnp.zeros_like(vc)) for mk in masks], axis=0)


def _spatial_bias(bspt_ref):
    return bspt_ref[...]


SHIFTS = 8
TAP_ROWS = 32


def _ext_rows(tm):
    return tm + HALO + SHIFTS


def _make_shifts(ext_ref, sh_ref, tm):
    ext_ref[tm + HALO:tm + HALO + SHIFTS, :] = jnp.zeros((SHIFTS, WB), F32)
    for r in range(SHIFTS):
        sh_ref[r] = ext_ref[r:r + tm + HALO, :]


def _conv_taps(sh_ref, w_ref, tm, taps, emit):
    def block(i, carry):
        r0 = pl.multiple_of(i * TAP_ROWS, TAP_ROWS)
        acc = jnp.zeros((TAP_ROWS, WB), F32)
        for o, k in taps:
            acc = acc + w_ref[k:k + 1, :] * sh_ref[o % SHIFTS, pl.ds(r0 + SHIFTS * (o // SHIFTS), TAP_ROWS), :]
        emit(r0, acc)
        return carry

    lax.fori_loop(0, tm // TAP_ROWS, block, 0)


def _halo_prev_spec(tm):
    return pl.BlockSpec((1, HALO, WB), lambda b, i: (b, jnp.maximum(i * (tm // HALO) - 1, 0), 0))


def _halo_next_spec(tm, s):
    return pl.BlockSpec((1, HALO, WB), lambda b, i: (b, jnp.minimum((i + 1) * (tm // HALO), s // HALO - 1), 0))


def _mix_mid_fwd(x, u, v, a, g, gt, gn_g, gn_b, wcat, bspt, conv_w, conv_b, cn_g, cn_b, go_a, go_b, w_mo, g_post):
    nb, s, _ = x.shape
    tm = min(TM, s)

    def body(x_ref, u_ref, v_ref, a_ref, g_ref, ah_ref, gh_ref, gt_ref, gng_ref, gnb_ref, wcat_ref, bspt_ref,
             cw_ref, cb_ref, cng_ref, cnb_ref, goa_ref, gob_ref, wmo_ref, gpost_ref,
             xo_ref, conv_ref, y_ref, m_ref, ext_ref, sh_ref):
        i = pl.program_id(1)
        xhat, _ = _ln(v_ref[0])
        vb = (xhat * gng_ref[...] + gnb_ref[...]).astype(BF16)
        wsb = _spatial_weights(wcat_ref, False)
        bias = _spatial_bias(bspt_ref)
        masks = _head_mask((CH, WA))
        zs = []
        for cidx in range(tm // CH):
            vexp = _expand_heads(vb[cidx * CH:(cidx + 1) * CH, :], masks)
            zs.append(_dot(wsb, vexp) + bias)
        z = jnp.concatenate(zs, axis=0)
        na, _ = _rms(u_ref[0] * z)
        keep = jnp.where(i == 0, 0.0, 1.0).astype(F32)
        ext_ref[0:HALO, :] = (ah_ref[0] * _sigmoid(gh_ref[0])) * keep
        ext_ref[HALO:HALO + tm, :] = a_ref[0] * _sigmoid(g_ref[0])
        _make_shifts(ext_ref, sh_ref, tm)
        cb = cb_ref[...]

        def put_conv(r0, acc):
            conv_ref[0, pl.ds(r0, TAP_ROWS), :] = acc + cb

        _conv_taps(sh_ref, cw_ref, tm, [(k + HALO - (CK - 1), k) for k in range(CK)], put_conv)
        conv = conv_ref[0]
        chat, _ = _ln(conv)
        cln = chat * cng_ref[...] + cnb_ref[...]
        nbb, _ = _rms(cln * _sigmoid(cln))
        yb = jnp.concatenate([na * goa_ref[...], nbb * gob_ref[...]], axis=1).astype(BF16)
        y_ref[0] = yb
        m = _dot(yb, wmo_ref[...])
        m_ref[0] = m
        nm, _ = _rms(m)
        xo_ref[0] = x_ref[0] + gt_ref[0] * (nm * gpost_ref[...])

    t5 = _tok_specs(tm, WA)
    tok = _tok_specs(tm, D)
    r5 = _row_spec(WA)
    full = lambda shape: pl.BlockSpec(shape, lambda b, i: (0,) * len(shape))
    return pl.pallas_call(
        body, name="mix_mid_fwd", grid=(nb, s // tm),
        out_shape=[jax.ShapeDtypeStruct((nb, s, D), F32), jax.ShapeDtypeStruct((nb, s, WB), F32),
                   jax.ShapeDtypeStruct((nb, s, D), BF16), jax.ShapeDtypeStruct((nb, s, D), F32)],
        in_specs=[tok, t5, t5, t5, t5, _halo_prev_spec(tm), _halo_prev_spec(tm), _mod_spec(), r5, r5,
                  full((CH, NH * CH)), full((CH, WA)), full((HALO, WB)), r5, r5, r5, r5, r5, VMEM_FULL, _row_spec()],
        out_specs=[tok, t5, tok, tok],
        scratch_shapes=[pltpu.VMEM((_ext_rows(tm), WB), F32), pltpu.VMEM((SHIFTS, tm + HALO, WB), F32)],
        compiler_params=_cparams(),
    )(x, u, v, a, g, a, g, gt, gn_g, gn_b, wcat, bspt, conv_w, conv_b, cn_g, cn_b, go_a, go_b, w_mo, g_post)


def _mix_out_bwd(dxo, m, gt, g_post, w_mo):
    nb, s, _ = m.shape
    tm = min(TM, s)

    def body(dxo_ref, m_ref, gt_ref, gpost_ref, wmo_ref, dy_ref, dm_ref, dgpost_ref, dgt_ref):
        b, i = pl.program_id(0), pl.program_id(1)
        dxo_v = dxo_ref[0]
        nm, q = _rms(m_ref[0])
        gpost = gpost_ref[...]
        dgt = jnp.sum(dxo_v * (nm * gpost), axis=0, keepdims=True)
        dm, dgpost = _rms_bwd(dxo_v * gt_ref[0], nm, q, gpost)
        dmb = dm.astype(BF16)
        dm_ref[0] = dmb
        dy_ref[0] = _dot_nt(dmb, wmo_ref[...])
        _acc(dgpost_ref, dgpost, _first(b, i))
        _acc(dgt_ref, dgt[None], i == 0)

    tok = _tok_specs(tm, D)
    return pl.pallas_call(
        body, name="mix_out_bwd", grid=(nb, s // tm),
        out_shape=[jax.ShapeDtypeStruct((nb, s, D), F32), jax.ShapeDtypeStruct((nb, s, D), BF16),
                   jax.ShapeDtypeStruct((1, D), F32), jax.ShapeDtypeStruct((nb, 1, D), F32)],
        in_specs=[tok, tok, _mod_spec(), _row_spec(), VMEM_FULL],
        out_specs=[tok, tok, _row_spec(), _mod_spec()],
        compiler_params=_cparams(),
    )(dxo, m, gt, g_post, w_mo)


def _mix_mid_bwd(dy, u, v, conv, gn_g, gn_b, wcat, wcat_t, bspt, cn_g, cn_b, go_a, go_b):
    nb, s, _ = dy.shape
    tm = min(TM, s)
    nchunk = tm // CH

    def body(dy_ref, u_ref, v_ref, conv_ref, gng_ref, gnb_ref, wcat_ref, wcatt_ref, bspt_ref, cng_ref, cnb_ref,
             goa_ref, gob_ref,
             du_ref, dv_ref, dconv_ref, dwcat_ref, dbsp_ref, dgng_ref, dgnb_ref, dgoa_ref, dgob_ref,
             dcng_ref, dcnb_ref, dcb_ref):
        first = _first(pl.program_id(0), pl.program_id(1))
        dyv = dy_ref[0]
        xhat, rstd = _ln(v_ref[0])
        gng = gng_ref[...]
        vb = (xhat * gng + gnb_ref[...]).astype(BF16)
        wsb = _spatial_weights(wcat_ref, False)
        wsb_t = _spatial_weights(wcatt_ref, True)
        bias = _spatial_bias(bspt_ref)
        masks = _head_mask((CH, WA))
        vexps, zs = [], []
        for cidx in range(nchunk):
            vexp = _expand_heads(vb[cidx * CH:(cidx + 1) * CH, :], masks)
            vexps.append(vexp)
            zs.append(_dot(wsb, vexp) + bias)
        z = jnp.concatenate(zs, axis=0)
        uv = u_ref[0]
        na, ra = _rms(uv * z)
        dya, dgoa = _rms_bwd(dyv[:, 0:WA], na, ra, goa_ref[...])
        du_ref[0] = dya * z
        dz = dya * uv
        dwcat = jnp.zeros((CH, NH * CH), F32)
        dzsum = jnp.zeros((CH, WA), F32)
        dvlns = []
        for cidx in range(nchunk):
            dzc = dz[cidx * CH:(cidx + 1) * CH, :]
            dzsum = dzsum + dzc
            dzb = dzc.astype(BF16)
            dwcat = dwcat + _dot_nt(dzb, vexps[cidx])
            dvexp = _dot(wsb_t, dzb)
            dvl = jnp.zeros((CH, WA), F32)
            for h in range(NH):
                dvl = dvl + jnp.where(masks[h], dvexp[h * CH:(h + 1) * CH, :], 0.0)
            dvlns.append(dvl)
        dvln = jnp.concatenate(dvlns, axis=0)
        dv, dgng, dgnb = _ln_bwd(dvln, xhat, rstd, gng)
        dv_ref[0] = dv
        lane = lax.broadcasted_iota(jnp.int32, (NH, WA), 1)
        head = lax.broadcasted_iota(jnp.int32, (NH, WA), 0)
        sel = jnp.where((lane >= head * HD) & (lane < (head + 1) * HD), 1.0, 0.0).astype(F32)
        dbsp = lax.dot_general(sel, dzsum, NT, preferred_element_type=F32, precision=lax.Precision.HIGHEST)
        chat, crstd = _ln(conv_ref[0])
        cng = cng_ref[...]
        cln = chat * cng + cnb_ref[...]
        sg = _sigmoid(cln)
        nbb, rb = _rms(cln * sg)
        dyb, dgob = _rms_bwd(dyv[:, WA:D], nbb, rb, gob_ref[...])
        dconv, dcng, dcnb = _ln_bwd(dyb * _dsilu(cln, sg), chat, crstd, cng)
        dconv_ref[0] = dconv
        dcb = jnp.sum(dconv, axis=0, keepdims=True)
        for ref, val in ((dwcat_ref, dwcat), (dbsp_ref, dbsp), (dgng_ref, dgng), (dgnb_ref, dgnb), (dgoa_ref, dgoa),
                         (dgob_ref, dgob), (dcng_ref, dcng), (dcnb_ref, dcnb), (dcb_ref, dcb)):
            _acc(ref, val, first)

    t5 = _tok_specs(tm, WA)
    r5 = _row_spec(WA)
    full = lambda shape: pl.BlockSpec(shape, lambda b, i: (0,) * len(shape))
    big = jax.ShapeDtypeStruct((nb, s, WA), F32)
    row = jax.ShapeDtypeStruct((1, WA), F32)
    return pl.pallas_call(
        body, name="mix_mid_bwd", grid=(nb, s // tm),
        out_shape=[big, big, big, jax.ShapeDtypeStruct((CH, NH * CH), F32), jax.ShapeDtypeStruct((NH, CH), F32),
                   row, row, row, row, row, row, row],
        in_specs=[_tok_specs(tm, D), t5, t5, t5, r5, r5, full((CH, NH * CH)), full((NH * CH, CH)), full((CH, WA)),
                  r5, r5, r5, r5],
        out_specs=[t5, t5, t5, full((CH, NH * CH)), full((NH, CH)), r5, r5, r5, r5, r5, r5, r5],
        compiler_params=_cparams(),
    )(dy, u, v, conv, gn_g, gn_b, wcat, wcat_t, bspt, cn_g, cn_b, go_a, go_b)


def _mix_in_bwd(dxo, x, du, dv, dconv, a, g, sh, sc, g_pre, w_mi4, conv_w):
    nb, s, _ = x.shape
    tm = min(TM, s)
    n_i = s // tm

    def body(dxo_ref, x_ref, du_ref, dv_ref, dc_ref, dch_ref, a_ref, g_ref, ah_ref, gh_ref, sh_ref, sc_ref,
             gpre_ref, w_ref, cw_ref,
             dx_ref, dproj_ref, h_ref, dgpre_ref, dsh_ref, dsc_ref, dcw_ref, ext_ref, shf_ref, dglu_ref):
        b, i = pl.program_id(0), pl.program_id(1)
        first = _first(b, i)
        av, gv = a_ref[0], g_ref[0]
        sg = _sigmoid(gv)
        dconv = dc_ref[0]
        ext_ref[0:tm, :] = dconv
        ext_ref[tm:tm + HALO, :] = dch_ref[0] * jnp.where(i == n_i - 1, 0.0, 1.0).astype(F32)
        _make_shifts(ext_ref, shf_ref, tm)

        def put_dglu(r0, acc):
            dglu_ref[pl.ds(r0, TAP_ROWS), :] = acc

        _conv_taps(shf_ref, cw_ref, tm, [(CK - 1 - k, k) for k in range(CK)], put_dglu)
        dglu = dglu_ref[...]
        ext_ref[0:HALO, :] = (ah_ref[0] * _sigmoid(gh_ref[0])) * jnp.where(i == 0, 0.0, 1.0).astype(F32)
        ext_ref[HALO:HALO + tm, :] = av * sg
        _make_shifts(ext_ref, shf_ref, tm)

        @pl.when(first)
        def _():
            dcw_ref[...] = jnp.zeros((HALO, WB), F32)

        for k in range(CK):
            o = k + HALO - (CK - 1)
            lo = SHIFTS * (o // SHIFTS)
            dcw_ref[k:k + 1, :] += jnp.sum(dconv * shf_ref[o % SHIFTS, lo:lo + tm, :], axis=0, keepdims=True)
        da = dglu * sg
        dg = dglu * av * (sg * (1.0 - sg))
        parts = [du_ref[0].astype(BF16), dv_ref[0].astype(BF16), da.astype(BF16), dg.astype(BF16)]
        dh = jnp.zeros((tm, D), F32)
        for k in range(4):
            dproj_ref[0, :, k * WA:(k + 1) * WA] = parts[k]
            dh = dh + _dot_nt(parts[k], w_ref[k])
        n, r = _rms(x_ref[0])
        gpre = gpre_ref[...]
        ng = n * gpre
        scale1 = 1.0 + sc_ref[0]
        h_ref[0] = (ng * scale1 + sh_ref[0]).astype(BF16)
        dsh = jnp.sum(dh, axis=0, keepdims=True)
        dsc = jnp.sum(dh * ng, axis=0, keepdims=True)
        dxn, dgpre = _rms_bwd(dh * scale1, n, r, gpre)
        dx_ref[0] = dxo_ref[0] + dxn
        _acc(dgpre_ref, dgpre, first)
        _acc(dsh_ref, dsh[None], i == 0)
        _acc(dsc_ref, dsc[None], i == 0)

    tok = _tok_specs(tm, D)
    t5 = _tok_specs(tm, WA)
    full = lambda shape: pl.BlockSpec(shape, lambda b, i: (0,) * len(shape))
    mod_shape = jax.ShapeDtypeStruct((nb, 1, D), F32)
    return pl.pallas_call(
        body, name="mix_in_bwd", grid=(nb, n_i),
        out_shape=[jax.ShapeDtypeStruct((nb, s, D), F32), jax.ShapeDtypeStruct((nb, s, 4 * WA), BF16),
                   jax.ShapeDtypeStruct((nb, s, D), BF16), jax.ShapeDtypeStruct((1, D), F32), mod_shape, mod_shape,
                   jax.ShapeDtypeStruct((HALO, WB), F32)],
        in_specs=[tok, tok, t5, t5, t5, _halo_next_spec(tm, s), t5, t5, _halo_prev_spec(tm), _halo_prev_spec(tm),
                  _mod_spec(), _mod_spec(), _row_spec(), VMEM_FULL, full((HALO, WB))],
        out_specs=[tok, _tok_specs(tm, 4 * WA), tok, _row_spec(), _mod_spec(), _mod_spec(), full((HALO, WB))],
        scratch_shapes=[pltpu.VMEM((_ext_rows(tm), WB), F32), pltpu.VMEM((SHIFTS, tm + HALO, WB), F32),
                        pltpu.VMEM((tm, WB), F32)],
        compiler_params=_cparams(),
    )(dxo, x, du, dv, dconv, dconv, a, g, a, g, sh, sc, g_pre, w_mi4, conv_w)


def _row_tile(rows, cols):
    best = 16
    for t in range(16, rows + 1, 16):
        if rows % t == 0 and t * cols * 4 <= 1536 * 1024:
            best = t
    return best


def _sum4(name, own4, recv, j_arr):
    _, rows, cols = own4.shape
    tr = _row_tile(rows, cols)

    def body(j_ref, own_ref, recv_ref, o_ref):
        del j_ref
        acc = own_ref[0]
        for k in range(3):
            acc = acc + recv_ref[k].astype(F32)
        o_ref[...] = acc

    return pl.pallas_call(
        body, name=name,
        grid_spec=pltpu.PrefetchScalarGridSpec(
            num_scalar_prefetch=1, grid=(rows // tr,),
            in_specs=[pl.BlockSpec((1, tr, cols), lambda i, j: (j[0], i, 0)),
                      pl.BlockSpec((3, tr, cols), lambda i, j: (0, i, 0))],
            out_specs=pl.BlockSpec((tr, cols), lambda i, j: (i, 0))),
        out_shape=jax.ShapeDtypeStruct((rows, cols), F32),
        compiler_params=_cparams(),
    )(j_arr, own4, recv)


def _pair_plan(shapes):
    def plan(x, y, c, src, land):
        sends = []
        for a, shape in enumerate(shapes):
            rows = shape[1] // 2
            theirs = pl.ds(pl.multiple_of((1 - c) * rows, 16), rows)
            sends.append((src[a].at[:, theirs], land[a], (x, y, 1 - c), land[a]))
        return [], sends

    return plan


def _swap_plan(n):
    def plan(x, y, c, src, land):
        return [], [(src[a], land[a], (x, y, 1 - c), land[a]) for a in range(n)]

    return plan


def _pair_sum(name, g32, recv, c_arr):
    nblk, rows, cols = recv.shape
    tr = _row_tile(rows, cols)
    nh = rows // tr

    def body(c_ref, g_ref, r_ref, o32_ref, obf_ref):
        del c_ref
        val = g_ref[0] + r_ref[0].astype(F32)
        o32_ref[0] = val
        obf_ref[0] = val.astype(BF16)

    spec = pl.BlockSpec((1, tr, cols), lambda k, i, c: (k, i, 0))
    return pl.pallas_call(
        body, name=name,
        grid_spec=pltpu.PrefetchScalarGridSpec(
            num_scalar_prefetch=1, grid=(nblk, nh),
            in_specs=[pl.BlockSpec((1, tr, cols), lambda k, i, c: (k, c[0] * nh + i, 0)), spec],
            out_specs=[spec, spec]),
        out_shape=[jax.ShapeDtypeStruct(recv.shape, F32), jax.ShapeDtypeStruct(recv.shape, BF16)],
        compiler_params=_cparams(),
    )(c_arr, g32, recv)


def _adam_halves(name, w, m, v, mine, theirs, c_arr):
    rows, cols = w.shape
    tr = _row_tile(rows // 2, cols)
    nh = (rows // 2) // tr

    def body(c_ref, w_ref, m_ref, v_ref, mine_ref, theirs_ref, g_out, d_out, m_out, v_out):
        here = (pl.program_id(0) // nh) == c_ref[0]
        g = jnp.where(here, mine_ref[...], theirs_ref[...])
        delta, m2, v2 = _adam(w_ref[...], g, m_ref[...], v_ref[...])
        g_out[...] = g
        d_out[...] = delta
        m_out[...] = m2
        v_out[...] = v2

    spec = pl.BlockSpec((tr, cols), lambda i, c: (i, 0))
    shape = jax.ShapeDtypeStruct((rows, cols), F32)
    return pl.pallas_call(
        body, name=name,
        grid_spec=pltpu.PrefetchScalarGridSpec(
            num_scalar_prefetch=1, grid=(2 * nh,),
            in_specs=[spec, spec, spec,
                      pl.BlockSpec((tr, cols), lambda i, c: (jnp.clip(i - c[0] * nh, 0, nh - 1), 0)),
                      pl.BlockSpec((tr, cols), lambda i, c: (jnp.clip(i - (1 - c[0]) * nh, 0, nh - 1), 0))],
            out_specs=[spec] * 4),
        out_shape=[shape] * 4,
        compiler_params=_cparams(),
    )(c_arr, w, m, v, mine, theirs)


def _adam_big(name, w, m, v, ga, gb):
    rows, cols = w.shape
    tr = _row_tile(rows, cols)

    def body(w_ref, m_ref, v_ref, ga_ref, gb_ref, g_out, d_out, m_out, v_out):
        gsum = ga_ref[...] + gb_ref[...]
        delta, m2, v2 = _adam(w_ref[...], gsum, m_ref[...], v_ref[...])
        g_out[...] = gsum
        d_out[...] = delta
        m_out[...] = m2
        v_out[...] = v2

    spec = pl.BlockSpec((tr, cols), lambda i: (i, 0))
    shape = jax.ShapeDtypeStruct((rows, cols), F32)
    return pl.pallas_call(
        body, name=name, grid=(rows // tr,), out_shape=[shape] * 4,
        in_specs=[spec] * 5, out_specs=[spec] * 4, compiler_params=_cparams(),
    )(w, m, v, ga, gb)


PK_VEC = 0
PK_LOSS = 6
PK_PAIR = 8
PK_BSP = 16
PK_WCAT = 24
PK_ROWS = PK_WCAT + CH
PAIR_ORDER = ("gmlp_norm_g", "gmlp_norm_b", "conv_b", "conv_norm_g", "conv_norm_b", "g_out_a", "g_out_b")
VEC_ORDER = ("g_pre_f1", "g_post_f1", "g_pre_m", "g_post_m", "g_pre_f2", "g_post_f2")


def _pack_late(rows):
    counts = [r.shape[0] for r in rows]
    assert sum(counts) == 8

    def body(*refs):
        o_ref = refs[-1]
        at = 0
        for r, cnt in zip(refs[:-1], counts):
            o_ref[at:at + cnt, :] = r[...]
            at += cnt

    return pl.pallas_call(
        body, name="pack_late", out_shape=jax.ShapeDtypeStruct((8, D), F32),
        in_specs=[VMEM_FULL] * len(rows), out_specs=VMEM_FULL, compiler_params=_cparams(),
    )(*rows)


def _pack_small(vecs, pairs, dbsp, dwcat, lsum):
    def body(*refs):
        vec_refs = refs[:4]
        pair_refs = refs[4:11]
        dbsp_ref, dwcat_ref, lsum_ref, o_ref = refs[11:]
        o_ref[0:PK_WCAT, :] = jnp.zeros((PK_WCAT, D), F32)
        o_ref[PK_LOSS:PK_LOSS + 1, 0:128] = lsum_ref[...]
        for k, r in enumerate(vec_refs):
            o_ref[PK_VEC + 2 + k:PK_VEC + 3 + k, :] = r[...]
        for k, r in enumerate(pair_refs):
            row, half = PK_PAIR + k // 2, k % 2
            o_ref[row:row + 1, half * WA:(half + 1) * WA] = r[...]
        o_ref[PK_BSP:PK_BSP + NH, 0:CH] = dbsp_ref[...]
        o_ref[PK_WCAT:PK_ROWS, :] = dwcat_ref[...]

    args = list(vecs) + list(pairs) + [dbsp, dwcat, lsum]
    return pl.pallas_call(
        body, name="pack_small", out_shape=jax.ShapeDtypeStruct((PK_ROWS, D), F32),
        in_specs=[VMEM_FULL] * len(args), out_specs=VMEM_FULL, compiler_params=_cparams(),
    )(*args)


def _small_adam(pack_all, late_all, dcw_all, dada_all, params, behind):
    names = list(VEC_ORDER) + list(PAIR_ORDER) + ["b_spatial", "w_spatial", "conv_w", "b_ada"]
    flat = []
    for nm in names:
        flat += list(params[nm])
    n_in = 4 + len(flat)

    def body(*refs):
        pack_ref, late_ref, dcw_ref, dada_ref = refs[:4]
        prm = refs[4:n_in]
        outs = refs[n_in + 1:]

        def total(r0, nr, c0, nc):
            acc = pack_ref[0, r0:r0 + nr, c0:c0 + nc]
            for d in range(1, NDEV):
                acc = acc + pack_ref[d, r0:r0 + nr, c0:c0 + nc]
            return acc

        def emit(idx, g, getw, put):
            w_ref, m_ref, v_ref = prm[3 * idx:3 * idx + 3]
            delta, m2, v2 = _adam(getw(w_ref), g, getw(m_ref), getw(v_ref))
            for o_ref, val in zip(outs[4 * idx:4 * idx + 4], (g, delta, m2, v2)):
                put(o_ref, val)

        def whole(ref):
            return ref[...]

        def put_whole(ref, val):
            ref[...] = val

        idx = 0
        for k in range(6):
            if k < 2:
                g = late_ref[0, k:k + 1, :]
                for d in range(1, NDEV):
                    g = g + late_ref[d, k:k + 1, :]
            else:
                g = total(PK_VEC + k, 1, 0, D)
            emit(idx, g, whole, put_whole)
            idx += 1
        for k in range(7):
            emit(idx, total(PK_PAIR + k // 2, 1, (k % 2) * WA, WA), whole, put_whole)
            idx += 1
        emit(idx, total(PK_BSP, NH, 0, CH), lambda r: r[0], lambda r, val: r.__setitem__(0, val))
        idx += 1
        row = lax.broadcasted_iota(jnp.int32, (CH, CH), 0)
        col = lax.broadcasted_iota(jnp.int32, (CH, CH), 1)
        for h in range(NH):
            gh = jnp.where(col <= row, total(PK_WCAT, CH, h * CH, CH), 0.0)
            w_ref, m_ref, v_ref = prm[3 * idx:3 * idx + 3]
            delta, m2, v2 = _adam(w_ref[0, h], gh, m_ref[0, h], v_ref[0, h])
            for o_ref, val in zip(outs[4 * idx:4 * idx + 4], (gh, delta, m2, v2)):
                o_ref[0, h] = val
        idx += 1
        gcw = dcw_ref[0, 0:CK, :]
        for d in range(1, NDEV):
            gcw = gcw + dcw_ref[d, 0:CK, :]
        emit(idx, gcw, lambda r: r[0], lambda r, val: r.__setitem__(0, val))
        idx += 1
        emit(idx, jnp.sum(dada_ref[...], axis=0, keepdims=True), whole, put_whole)
        outs[-1][...] = jnp.sum(total(PK_LOSS, 1, 0, 128), axis=1, keepdims=True) * (0.5 / D)

    out_shape = []
    for nm in names:
        w = params[nm][0]
        out_shape += [jax.ShapeDtypeStruct(w.shape, F32)] * 4
    out_shape.append(jax.ShapeDtypeStruct((1, 1), F32))
    res = pl.pallas_call(
        body, name="small_adam", out_shape=out_shape,
        in_specs=[VMEM_FULL] * n_in + [ANY], out_specs=[VMEM_FULL] * len(out_shape), compiler_params=_cparams(),
    )(pack_all, late_all, dcw_all, dada_all, *flat, behind)
    return {nm: tuple(res[4 * k:4 * k + 4]) for k, nm in enumerate(names)}, res[-1].reshape(())


WEIGHTS = ['w_ada', 'b_ada', 'g_pre_f1', 'g_post_f1', 'w_f1_in', 'w_f1_out', 'g_pre_m', 'g_post_m', 'w_mix_in',
           'gmlp_norm_g', 'gmlp_norm_b', 'w_spatial', 'b_spatial', 'conv_w', 'conv_b', 'conv_norm_g', 'conv_norm_b',
           'g_out_a', 'g_out_b', 'w_mix_out', 'g_pre_f2', 'g_post_f2', 'w_f2_in', 'w_f2_out']


def kernel(x, c, w_ada, b_ada, g_pre_f1, g_post_f1, w_f1_in, w_f1_out, g_pre_m, g_post_m, w_mix_in, gmlp_norm_g, gmlp_norm_b, w_spatial, b_spatial, conv_w, conv_b, conv_norm_g, conv_norm_b, g_out_a, g_out_b, w_mix_out, g_pre_f2, g_post_f2, w_f2_in, w_f2_out, loss_target, m_w_ada, m_b_ada, m_g_pre_f1, m_g_post_f1, m_w_f1_in, m_w_f1_out, m_g_pre_m, m_g_post_m, m_w_mix_in, m_gmlp_norm_g, m_gmlp_norm_b, m_w_spatial, m_b_spatial, m_conv_w, m_conv_b, m_conv_norm_g, m_conv_norm_b, m_g_out_a, m_g_out_b, m_w_mix_out, m_g_pre_f2, m_g_post_f2, m_w_f2_in, m_w_f2_out, v_w_ada, v_b_ada, v_g_pre_f1, v_g_post_f1, v_w_f1_in, v_w_f1_out, v_g_pre_m, v_g_post_m, v_w_mix_in, v_gmlp_norm_g, v_gmlp_norm_b, v_w_spatial, v_b_spatial, v_conv_w, v_conv_b, v_conv_norm_g, v_conv_norm_b, v_g_out_a, v_g_out_b, v_w_mix_out, v_g_pre_f2, v_g_post_f2, v_w_f2_in, v_w_f2_out):
    env = dict(locals())
    wts = {n: env[n] for n in WEIGHTS}
    mom = {n: env["m_" + n] for n in WEIGHTS}
    var = {n: env["v_" + n] for n in WEIGHTS}
    nb, s, _ = x.shape
    t = nb * s
    ax, ay, ac = lax.axis_index("x"), lax.axis_index("y"), lax.axis_index("c")
    j_chip = 2 * ax + ay
    dev = 4 * ax + 2 * ay + ac
    j_arr = j_chip.reshape(1).astype(jnp.int32)

    groups = (("w_f1_in",), ("w_mix_in", "w_mix_out"), ("w_f2_in", "w_f2_out"), ("w_f1_out",))
    def gather_operands(gi):
        srcs = [wts[n][0].astype(BF16) for n in groups[gi]] + ([conv_w[0]] if gi == 1 else [])
        lands = [lax.dynamic_update_index_in_dim(lax.empty((NCHIP,) + a.shape, a.dtype), a, j_chip, 0) for a in srcs]
        return srcs, lands

    def gather_start(gi, behind, operands=None):
        srcs, lands = operands or gather_operands(gi)
        plan_a, plan_b, n_b = _gather_plans([a.shape for a in srcs])
        ssem, rsem, srcs, lands, token = _split_start("gw_start%d" % gi, srcs, lands, plan_a, 3 * len(srcs), behind)
        gather[gi] = (srcs, lands, ssem, rsem, plan_a, plan_b, n_b)
        return token

    def gather_forward(gi, behind):
        srcs, lands, ssem, rsem, plan_a, plan_b, n_b = gather[gi]
        ssem, rsem, lands, token = _split_forward("gw_fwd%d" % gi, srcs, lands, ssem, rsem, plan_a, plan_b, n_b, behind)
        gather[gi] = (lands, ssem, rsem, plan_b)
        return token

    def gathered(gi, behind):
        lands, ssem, rsem, plan_b = gather[gi]
        return _split_wait("gw_wait%d" % gi, [], lands, ssem, rsem, plan_b, behind)

    gather = {}
    (c_all8,) = _allgather8("gather_c", [c.reshape(8, (nb * D) // 8)])
    token = gather_start(0, c_all8)
    c_all = c_all8.reshape(NDEV * nb, D) + token[0, 0]
    b_sh = lax.dynamic_slice(b_ada, (0, j_chip * ADA_SH), (1, ADA_SH))
    ada_sh = _ada_fwd(c_all, w_ada[0], b_sh)
    later = [gather_operands(3), gather_operands(1), gather_operands(2)]
    (ada4,) = _chip_allgather("gather_ada", [ada_sh], behind=[a for pair in later for arrs in pair for a in arrs])
    token = gather_forward(0, ada4)
    plans = [_gather_plans([a.shape for a in srcs]) for srcs, _ in later]
    started, token = _split_start_groups(
        "gw_start_later", [(srcs, lands, pa, 3 * len(srcs)) for (srcs, lands), (pa, _, _) in zip(later, plans)], token)
    for gi, (ssem, rsem, srcs, lands), (pa, pb, n_b) in zip((3, 1, 2), started, plans):
        gather[gi] = (srcs, lands, ssem, rsem, pa, pb, n_b)
    ada_me = lax.dynamic_slice(ada4, (0, dev * nb, 0), (NCHIP, nb, ADA_SH))
    ada_me = jnp.transpose(ada_me, (1, 0, 2)).reshape(nb, NMOD * D)
    sh1, sc1, gt1, sh2, sc2, gt2, sh3, sc3, gt3 = [ada_me[:, k * D:(k + 1) * D].reshape(nb, 1, D) for k in range(NMOD)]

    wcat = jnp.transpose(w_spatial[0], (1, 0, 2)).reshape(CH, NH * CH)
    wcat_t = jnp.transpose(w_spatial[0], (0, 2, 1)).reshape(NH * CH, CH)
    bspt = jnp.repeat(b_spatial[0].T, HD, axis=1)

    (w1i,) = gathered(0, token)
    p1, act1 = _ffn_up(x, sh1, sc1, g_pre_f1, w1i)
    token = gather_forward(3, act1)
    token = gather_forward(1, token)
    (w1o,) = gathered(3, token)
    w1o = w1o.reshape(DFF, D)
    x1, f1 = _ffn_down(x, act1, gt1, g_post_f1, w1o)
    wmi, wmo, cw4 = gathered(1, x1)
    wmo = wmo.reshape(D, D)
    cw_full = jnp.transpose(cw4, (1, 0, 2)).reshape(CK, WB)
    cw_pad = jnp.pad(cw_full, ((0, HALO - CK), (0, 0)))
    u, v, a, g = _mix_in_fwd(x1, sh2, sc2, g_pre_m, wmi)
    token = gather_forward(2, u)
    x2, conv, yb, m = _mix_mid_fwd(x1, u, v, a, g, gt2 + token[0, 0], gmlp_norm_g, gmlp_norm_b, wcat, bspt, cw_pad, conv_b,
                                   conv_norm_g, conv_norm_b, g_out_a, g_out_b, wmo, g_post_m)
    w2i, w2o = gathered(2, [x2, token])
    w2o = w2o.reshape(DFF, D)
    dx3, df2, p2, lsum, dg_post_f2, dgt3 = _ffn_loss_fwd(x2, sh3, sc3, gt3, g_pre_f2, g_post_f2, w2i, w2o, loss_target)

    def chip4(pair, rows):
        return [arr.reshape(NCHIP, rows, arr.shape[-1]) for arr in pair]

    def scatter_start(tag, pairs, behind):
        srcs = [p[1] for p in pairs]
        lands = [lax.empty((3,) + a.shape[1:], a.dtype) for a in srcs]
        ssem, rsem, srcs, lands, token = _split_start("gs_start_" + tag, srcs, lands, _scatter_plan(len(srcs)),
                                                      3 * len(srcs), behind)
        return (srcs, lands, ssem, rsem), token

    def scatter_wait(tag, state, behind):
        srcs, lands, ssem, rsem = state
        return _split_wait("gs_wait_" + tag, srcs, lands, ssem, rsem, _scatter_plan(len(srcs)), behind)

    def allgather_start(tag, arrs, behind):
        lands = [lax.dynamic_update_index_in_dim(lax.empty((NDEV,) + a.shape, a.dtype), a, dev, 0) for a in arrs]
        ssem, rsem, srcs, lands, token = _split_start("small_start_" + tag, arrs, lands, _allgather_plan(len(arrs)),
                                                      7 * len(arrs), behind)
        return (srcs, lands, ssem, rsem), token

    def allgather_wait(tag, state, behind):
        srcs, lands, ssem, rsem = state
        return _split_wait("small_wait_" + tag, srcs, lands, ssem, rsem, _allgather_plan(len(srcs)), behind)

    out = {}
    dx2, dp2, h3, a2, dg_pre_f2, dsh3, dsc3 = _ffn_bwd(
        dx3, x2, None, p2, sh3, sc3, gt3, g_pre_f2, g_post_f2, w2i, w2o, df=df2)
    gw2i = _wgrad("wgrad_f2_in", h3.reshape(t, D), dp2.reshape(t, 2 * DFF), 2 * DFF // NCHIP, True)
    gw2o = chip4(_wgrad("wgrad_f2_out", a2.reshape(t, DFF), df2.reshape(t, D), D // 2, False), DFF // NCHIP)
    scat_f2, tok = scatter_start("f2", [gw2i, gw2o], dg_post_f2)
    dy, dm, dg_post_m, dgt2 = _mix_out_bwd(dx2, m, gt2 + tok[0, 0], g_post_m, wmo)
    gwmo = chip4(_wgrad("wgrad_mix_out", yb.reshape(t, D), dm.reshape(t, D), D // 2, False), D // NCHIP)
    (du, dv, dconv, dwcat, dbsp, dgn_g, dgn_b, dgo_a, dgo_b, dcn_g, dcn_b, dcb) = _mix_mid_bwd(
        dy, u, v, conv, gmlp_norm_g, gmlp_norm_b, wcat, wcat_t, bspt, conv_norm_g, conv_norm_b, g_out_a, g_out_b)
    dx1, dproj, h2, dg_pre_m, dsh2, dsc2, dcw = _mix_in_bwd(dx2, x1, du, dv, dconv, a, g, sh2, sc2, g_pre_m, wmi, cw_pad)
    gwmi = _wgrad("wgrad_mix_in", h2.reshape(t, D), dproj.reshape(t, 4 * WA), WA, True)

    vec_grads = dict(g_pre_m=dg_pre_m, g_post_m=dg_post_m, g_pre_f2=dg_pre_f2, g_post_f2=dg_post_f2)
    pair_grads = dict(gmlp_norm_g=dgn_g, gmlp_norm_b=dgn_b, conv_b=dcb, conv_norm_g=dcn_g, conv_norm_b=dcn_b,
                      g_out_a=dgo_a, g_out_b=dgo_b)
    pack = _pack_small([vec_grads[n] for n in VEC_ORDER[2:]], [pair_grads[n] for n in PAIR_ORDER], dbsp, dwcat, lsum)
    dada_early = jnp.concatenate([q.reshape(nb, D) for q in (dsh2, dsc2, dgt2, dsh3, dsc3, dgt3)], axis=1)
    small_early = [pack, dcw, dada_early.reshape(8, (nb * 6 * D) // 8)]
    mix_bf16 = [gwmi[1], gwmo[1]]
    (s_mix, s_early), tok2 = _split_start_groups("gs_start_mix_small", [
        (mix_bf16, [lax.empty((3,) + a.shape[1:], a.dtype) for a in mix_bf16], _scatter_plan(2), 6),
        (small_early, [lax.dynamic_update_index_in_dim(lax.empty((NDEV,) + a.shape, a.dtype), a, dev, 0) for a in small_early],
         _allgather_plan(3), 21)], dg_pre_m)
    scat_mix = (s_mix[2], s_mix[3], s_mix[0], s_mix[1])
    early = (s_early[2], s_early[3], s_early[0], s_early[1])
    grad_x, dp1, h1, a1, df1, dg_pre_f1, dg_post_f1, dsh1, dsc1, dgt1 = _ffn_bwd(
        dx1, x, f1, p1, sh1 + tok2[0, 0], sc1, gt1, g_pre_f1, g_post_f1, w1i, w1o)
    late_pack = _pack_late([dg_pre_f1, dg_post_f1] + [q.reshape(nb, D) for q in (dsh1, dsc1, dgt1)])
    late, tok2 = allgather_start("late", [late_pack], dg_post_f1)
    gw1i = _wgrad("wgrad_f1_in", h1.reshape(t, D), dp1.reshape(t, 2 * DFF), 2 * DFF // NCHIP, True)
    gw1o = chip4(_wgrad("wgrad_f1_out", a1.reshape(t, DFF), df1.reshape(t, D), D // 2, False), DFF // NCHIP)
    def d2d_start(tag, srcs, lands, plan, behind):
        ssem, rsem, srcs, lands, token = _split_start("d2d_start_" + tag, srcs, lands, plan, len(srcs), behind)
        return (srcs, lands, ssem, rsem, plan), token

    def d2d_wait(tag, state, behind):
        srcs, lands, ssem, rsem, plan = state
        return _split_wait("d2d_wait_" + tag, srcs, lands, ssem, rsem, plan, behind)

    def swap_start(tag, parts, behind):
        return d2d_start(tag, parts, [lax.empty(a.shape, a.dtype) for a in parts], _swap_plan(len(parts)), behind)

    def sums(names, pairs, recv):
        return [_sum4("sum4_" + n, pairs[k][0], recv[k], j_arr) for k, n in enumerate(names)]

    def update(names, part, other):
        for k, n in enumerate(names):
            out[n] = tuple(r[None] for r in _adam_big("adam_" + n, wts[n][0], mom[n][0], var[n][0], part[k], other[k]))

    c_arr = ac.reshape(1).astype(jnp.int32)
    halves = [gw1i[1], gw1o[1]]
    pair_st, tok = d2d_start("pair", halves, [lax.empty((a.shape[0], a.shape[1] // 2, a.shape[2]), a.dtype) for a in halves],
                             _pair_plan([a.shape for a in halves]), tok2)
    names_f2, names_mix, names_f1 = ("w_f2_in", "w_f2_out"), ("w_mix_in", "w_mix_out"), ("w_f1_in", "w_f1_out")
    part_f2 = sums(names_f2, [gw2i, gw2o], scatter_wait("f2", scat_f2, tok))
    sib = d2d_wait("pair", pair_st, part_f2)
    pair_i = _pair_sum("pairsum_f1_in", gw1i[0], sib[0], c_arr)
    pair_o = _pair_sum("pairsum_f1_out", gw1o[0], sib[1], c_arr)
    scat_f1, tok = scatter_start("f1", [pair_i, pair_o], tok2)
    swap_f2, tok = swap_start("swap_f2", part_f2, tok)
    part_mix = sums(names_mix, [gwmi, gwmo], scatter_wait("mix", scat_mix, tok))
    swap_mix, tok = swap_start("swap_mix", part_mix, part_mix[1])

    pack_all, dcw_all, dada_early8 = allgather_wait("early", early, tok)
    (late_all,) = allgather_wait("late", late, pack_all)
    dada_late = jnp.transpose(late_all[:, 2:8, :].reshape(NDEV, 3, nb, D), (0, 2, 1, 3)).reshape(NDEV * nb, 3 * D)
    dada_all = jnp.concatenate([dada_late, dada_early8.reshape(NDEV * nb, 6 * D)], axis=1)
    dada_sh = lax.dynamic_slice(dada_all, (0, j_chip * ADA_SH), (NDEV * nb, ADA_SH))
    out["w_ada"] = tuple(r[None] for r in _ada_bwd_adam(c_all, dada_sh, w_ada[0], m_w_ada[0], v_w_ada[0]))
    update(names_f2, part_f2, d2d_wait("swap_f2", swap_f2, out["w_ada"][3]))
    update(names_mix, part_mix, d2d_wait("swap_mix", swap_mix, out["w_f2_out"][3]))

    mine = sums(names_f1, [pair_i, pair_o], scatter_wait("f1", scat_f1, out["w_mix_out"][3]))
    swap_f1, tok = swap_start("swap_f1", mine, mine[1])
    dcw_mine = lax.dynamic_slice(dcw_all, (0, 0, j_chip * (WB // NCHIP)), (NDEV, HALO, WB // NCHIP))
    small = {n: (wts[n], mom[n], var[n]) for n in list(VEC_ORDER) + list(PAIR_ORDER) + ["b_spatial", "w_spatial", "conv_w", "b_ada"]}
    small_out, loss = _small_adam(pack_all, late_all, dcw_mine, dada_all, small, tok)
    out.update(small_out)
    theirs = d2d_wait("swap_f1", swap_f1, out["b_ada"][3])
    for k, n in enumerate(names_f1):
        out[n] = tuple(r[None] for r in _adam_halves("adam_" + n, wts[n][0], mom[n][0], var[n][0], mine[k], theirs[k],
                                                     c_arr))

    res = [loss, grad_x]
    for k in range(4):
        res += [out[n][k] for n in WEIGHTS]
    return tuple(res)
```

```python
import jax
import jax.numpy as jnp
from jax import lax
from jax.experimental import pallas as pl
from jax.experimental.pallas import tpu as pltpu

D = 1024
DFF = 2816
WA = 512
WB = 512
NH = 8
HD = 64
CH = 128
CK = 31
HALO = 32
NMOD = 9
EPS = 1e-6
NCHIP = 4
NDEV = 8
FBLK = DFF // 2
ADA_SH = NMOD * D // NCHIP

LR, B1, B2, EPS_A, WD, STEP = 0.001, 0.9, 0.999, 1e-08, 0.01, 10

F32 = jnp.float32
BF16 = jnp.bfloat16
MESH = pl.DeviceIdType.MESH
ANY = pl.BlockSpec(memory_space=pl.ANY)
VMEM_FULL = pl.BlockSpec(memory_space=pltpu.VMEM)
VMEM_LIMIT = 56 * 1024 * 1024
WGRAD_VMEM_BUDGET = 52 * 1024 * 1024
TM = 512
TM_FFN_BWD = 256

NT = (((1,), (1,)), ((), ()))
TN = (((0,), (0,)), ((), ()))


def _dot(a, b):
    return jnp.dot(a, b, preferred_element_type=F32)


def _dot_nt(a, b):
    return lax.dot_general(a, b, NT, preferred_element_type=F32)


def _dot_tn(a, b):
    return lax.dot_general(a, b, TN, preferred_element_type=F32)


def _cparams():
    return pltpu.CompilerParams(vmem_limit_bytes=VMEM_LIMIT)


def _allgather8(name, arrs):
    n = len(arrs)
    remote = _allgather_plan(n)

    def plan(x, y, c, ins, outs):
        _, sends = remote(x, y, c, ins, outs)
        return [(ins[a], outs[a].at[4 * x + 2 * y + c]) for a in range(n)], sends

    shapes = [jax.ShapeDtypeStruct((NDEV,) + a.shape, a.dtype) for a in arrs]
    return _run_exchange(name, arrs, shapes, plan, n, 7 * n)


def _chip_relations(x, y):
    return [(1 - x, y), (x, 1 - y), (1 - x, 1 - y)]


def _exchange(name, arrs, out_shapes, plan):
    n = len(arrs)
    n_out = len(out_shapes)

    def body(*refs):
        ins, outs = refs[:n], refs[n:n + n_out]
        send_sems, recv_sems, local_sems = refs[n + n_out:]
        x, y, c = lax.axis_index("x"), lax.axis_index("y"), lax.axis_index("c")
        local, sends = plan(x, y, c, ins, outs)
        locs = [pltpu.make_async_copy(s, d, local_sems.at[i]) for i, (s, d) in enumerate(local)]
        for loc in locs:
            loc.start()
        cps = [pltpu.make_async_remote_copy(src_ref=s, dst_ref=d, send_sem=send_sems.at[i], recv_sem=recv_sems.at[i],
                                            device_id=peer, device_id_type=MESH)
               for i, (s, d, peer, _) in enumerate(sends)]
        for cp in cps:
            cp.start()
        for i, (s, _, peer, landing) in enumerate(sends):
            pltpu.make_async_remote_copy(src_ref=s, dst_ref=landing, send_sem=send_sems.at[i], recv_sem=recv_sems.at[i],
                                         device_id=peer, device_id_type=MESH).wait_recv()
        for cp in cps:
            cp.wait_send()
        for loc in locs:
            loc.wait()

    return n, n_out, body


def _run_exchange(name, arrs, out_shapes, plan, n_local, n_send):
    n, n_out, body = _exchange(name, arrs, out_shapes, plan)
    return pl.pallas_call(
        body, name=name, out_shape=out_shapes,
        in_specs=[ANY] * n, out_specs=[ANY] * n_out,
        scratch_shapes=[pltpu.SemaphoreType.DMA((n_send,)), pltpu.SemaphoreType.DMA((n_send,)),
                        pltpu.SemaphoreType.DMA((max(n_local, 1),))],
    )(*arrs)


def _chip_allgather(name, arrs, behind=()):
    n = len(arrs)

    def plan(x, y, c, ins, outs):
        j_me = 2 * x + y
        local = [(ins[a], outs[a].at[j_me]) for a in range(n)]
        sends = []
        for a in range(n):
            for (px, py) in _chip_relations(x, y):
                sends.append((ins[a], outs[a].at[j_me], (px, py, c), outs[a].at[2 * px + py]))
        return local, sends

    shapes = [jax.ShapeDtypeStruct((NCHIP,) + a.shape, a.dtype) for a in arrs]
    return _run_exchange(name, list(arrs) + list(behind), shapes, plan, n, 3 * n)


HBM = pl.BlockSpec(memory_space=pltpu.HBM)
SEM = pl.BlockSpec(memory_space=pltpu.SEMAPHORE)
EFFECT = pltpu.SideEffectType.DATAFLOW_SIDE_EFFECTING


def _split_start_groups(name, groups, after):
    n_src = [len(g[0]) for g in groups]
    n_land = [len(g[1]) for g in groups]
    all_srcs = [pltpu.with_memory_space_constraint(a, pltpu.HBM) for g in groups for a in g[0]]
    all_lands = [pltpu.with_memory_space_constraint(a, pltpu.HBM) for g in groups for a in g[1]]
    ns, nl, ng = len(all_srcs), len(all_lands), len(groups)

    def body(*refs):
        src_refs, land_refs = refs[:ns], refs[ns:ns + nl]
        sem_refs = refs[ns + nl + 1:ns + nl + 1 + 2 * ng]
        token = refs[-1]
        x, y, c = lax.axis_index("x"), lax.axis_index("y"), lax.axis_index("c")
        at_src = at_land = 0
        for gi, (_, _, plan, _) in enumerate(groups):
            _, sends = plan(x, y, c, src_refs[at_src:at_src + n_src[gi]], land_refs[at_land:at_land + n_land[gi]])
            for i, (s, d, peer, _) in enumerate(sends):
                pltpu.make_async_remote_copy(src_ref=s, dst_ref=d, send_sem=sem_refs[2 * gi].at[i],
                                             recv_sem=sem_refs[2 * gi + 1].at[i], device_id=peer, device_id_type=MESH).start()
            at_src += n_src[gi]
            at_land += n_land[gi]
        token[...] = jnp.zeros_like(token)

    sems = [pltpu.SemaphoreType.DMA((g[3],)) for g in groups for _ in range(2)]
    res = pl.pallas_call(
        body, name=name,
        out_shape=(*sems, *[pltpu.HBM(a.shape, a.dtype) for a in all_lands], jax.ShapeDtypeStruct((8, 128), F32)),
        in_specs=[HBM] * (ns + nl) + [ANY],
        out_specs=(*([SEM] * (2 * ng)), *([HBM] * nl), pl.BlockSpec(memory_space=pltpu.VMEM)),
        input_output_aliases={ns + i: 2 * ng + i for i in range(nl)},
        compiler_params=pltpu.CompilerParams(has_side_effects=EFFECT),
    )(*all_srcs, *all_lands, after)
    out, at_src, at_land = [], 0, 2 * ng
    for gi in range(ng):
        out.append((res[2 * gi], res[2 * gi + 1], all_srcs[at_src:at_src + n_src[gi]],
                    list(res[at_land:at_land + n_land[gi]])))
        at_src += n_src[gi]
        at_land += n_land[gi]
    return out, res[-1]


def _split_start(name, srcs, lands, plan, n_send, after):
    (group,), token = _split_start_groups(name, [(srcs, lands, plan, n_send)], after)
    return (*group, token)


def _split_wait(name, srcs, lands, send_sems, recv_sems, plan, after):
    n, nl = len(srcs), len(lands)
    afters = list(after) if isinstance(after, (list, tuple)) else [after]

    def body(*refs):
        src, land = refs[:n], refs[n:n + nl]
        send_sems, recv_sems = refs[n + nl], refs[n + nl + 1]
        x, y, c = lax.axis_index("x"), lax.axis_index("y"), lax.axis_index("c")
        _, sends = plan(x, y, c, src, land)
        for i, (s, _, peer, landing) in enumerate(sends):
            cp = pltpu.make_async_remote_copy(src_ref=s, dst_ref=landing, send_sem=send_sems.at[i],
                                              recv_sem=recv_sems.at[i], device_id=peer, device_id_type=MESH)
            cp.wait_send()
            cp.wait_recv()

    thru = [pltpu.HBM(a.shape, a.dtype) for a in lands]
    res = pl.pallas_call(
        body, name=name, out_shape=tuple(thru),
        in_specs=[HBM] * (n + nl) + [SEM, SEM] + [ANY] * len(afters), out_specs=tuple([HBM] * nl),
        input_output_aliases={n + i: i for i in range(nl)},
        compiler_params=pltpu.CompilerParams(has_side_effects=EFFECT),
    )(*srcs, *lands, send_sems, recv_sems, *afters)
    return list(res)


def _split_forward(name, srcs, lands, send_a, recv_a, plan_a, plan_b, n_b, after):
    n, nl = len(srcs), len(lands)

    def body(*refs):
        src, land = refs[:n], refs[n:n + nl]
        send_a, recv_a = refs[n + nl], refs[n + nl + 1]
        send_b, recv_b = refs[n + nl + 3], refs[n + nl + 4]
        token = refs[-1]
        x, y, c = lax.axis_index("x"), lax.axis_index("y"), lax.axis_index("c")
        _, first = plan_a(x, y, c, src, land)
        for i, (s, _, peer, landing) in enumerate(first):
            cp = pltpu.make_async_remote_copy(src_ref=s, dst_ref=landing, send_sem=send_a.at[i],
                                              recv_sem=recv_a.at[i], device_id=peer, device_id_type=MESH)
            cp.wait_send()
            cp.wait_recv()
        _, second = plan_b(x, y, c, src, land)
        for i, (s, d, peer, _) in enumerate(second):
            pltpu.make_async_remote_copy(src_ref=s, dst_ref=d, send_sem=send_b.at[i], recv_sem=recv_b.at[i],
                                         device_id=peer, device_id_type=MESH).start()
        token[...] = jnp.zeros_like(token)

    thru = [pltpu.HBM(a.shape, a.dtype) for a in lands]
    res = pl.pallas_call(
        body, name=name,
        out_shape=(pltpu.SemaphoreType.DMA((n_b,)), pltpu.SemaphoreType.DMA((n_b,)), *thru,
                   jax.ShapeDtypeStruct((8, 128), F32)),
        in_specs=[HBM] * (n + nl) + [SEM, SEM, ANY],
        out_specs=(SEM, SEM, *([HBM] * nl), pl.BlockSpec(memory_space=pltpu.VMEM)),
        input_output_aliases={n + i: 2 + i for i in range(nl)},
        compiler_params=pltpu.CompilerParams(has_side_effects=EFFECT),
    )(*srcs, *lands, send_a, recv_a, after)
    return res[0], res[1], list(res[2:2 + nl]), res[-1]


def _gather_plans(shapes):
    n = len(shapes)

    def halves(a, c):
        rows = shapes[a][0] // 2
        return pl.ds(pl.multiple_of(c * rows, 16), rows), pl.ds(pl.multiple_of((1 - c) * rows, 16), rows)

    def split(a):
        return shapes[a][0] % 32 == 0

    def plan_a(x, y, c, src, land):
        j_me = 2 * x + y
        sends = []
        for a in range(n):
            for (px, py) in _chip_relations(x, y):
                if split(a):
                    mine, _ = halves(a, c)
                    sends.append((src[a].at[mine], land[a].at[j_me, mine], (px, py, c), land[a].at[2 * px + py, mine]))
                else:
                    sends.append((src[a], land[a].at[j_me], (px, py, c), land[a].at[2 * px + py]))
        return [], sends

    def plan_b(x, y, c, src, land):
        sends = []
        for a in range(n):
            if split(a):
                mine, other = halves(a, c)
                for (px, py) in _chip_relations(x, y):
                    j = 2 * px + py
                    sends.append((land[a].at[j, mine], land[a].at[j, mine], (x, y, 1 - c), land[a].at[j, other]))
        return [], sends

    n_b = 3 * sum(1 for a in range(n) if split(a))
    return plan_a, plan_b, n_b


def _allgather_plan(n):
    flips = [(dx, dy, dc) for dx in (0, 1) for dy in (0, 1) for dc in (0, 1) if dx or dy or dc]

    def plan(x, y, c, src, land):
        sends = []
        for a in range(n):
            for dx, dy, dc in flips:
                px, py, pc = x ^ dx, y ^ dy, c ^ dc
                sends.append((src[a], land[a].at[4 * x + 2 * y + c], (px, py, pc), land[a].at[4 * px + 2 * py + pc]))
        return [], sends

    return plan


def _scatter_plan(n):
    def plan(x, y, c, src, land):
        sends = []
        for a in range(n):
            for k, (px, py) in enumerate(_chip_relations(x, y)):
                sends.append((src[a].at[2 * px + py], land[a].at[k], (px, py, c), land[a].at[k]))
        return [], sends

    return plan


def _rms(x):
    r = lax.rsqrt(jnp.mean(x * x, axis=-1, keepdims=True) + EPS)
    return x * r, r


def _rms_bwd(dy, n, r, g):
    dg = jnp.sum(dy * n, axis=0, keepdims=True)
    dn = dy * g
    dx = r * (dn - n * jnp.mean(dn * n, axis=-1, keepdims=True))
    return dx, dg


def _ln(x):
    mu = jnp.mean(x, axis=-1, keepdims=True)
    xc = x - mu
    rstd = lax.rsqrt(jnp.mean(xc * xc, axis=-1, keepdims=True) + EPS)
    return xc * rstd, rstd


def _ln_bwd(dy, xhat, rstd, g):
    dg = jnp.sum(dy * xhat, axis=0, keepdims=True)
    db = jnp.sum(dy, axis=0, keepdims=True)
    dxh = dy * g
    dx = rstd * (dxh - jnp.mean(dxh, axis=-1, keepdims=True) - xhat * jnp.mean(dxh * xhat, axis=-1, keepdims=True))
    return dx, dg, db


def _sigmoid(x):
    return jax.nn.sigmoid(x)


def _dsilu(x, s):
    return s * (1.0 + x * (1.0 - s))


def _adam(w, g, m, v):
    m = B1 * m + (1.0 - B1) * g
    v = B2 * v + (1.0 - B2) * (g * g)
    m_hat = m / (1.0 - B1 ** STEP)
    v_hat = v / (1.0 - B2 ** STEP)
    delta = -LR * (m_hat / (jnp.sqrt(v_hat) + EPS_A) + WD * w)
    return delta, m, v


def _head_mask(shape):
    lane = lax.broadcasted_iota(jnp.int32, shape, len(shape) - 1)
    return [(lane >= h * HD) & (lane < (h + 1) * HD) for h in range(NH)]


def _first(b, i):
    return jnp.logical_and(b == 0, i == 0)


def _acc(ref, val, first):
    @pl.when(first)
    def _():
        ref[...] = val

    @pl.when(jnp.logical_not(first))
    def _():
        ref[...] += val


def _ada_fwd(c_all, w_sh, b_sh):
    nb = c_all.shape[0]
    tn = 768

    def body(c_ref, w_ref, b_ref, o_ref):
        cv = c_ref[...]
        cs = (cv * _sigmoid(cv)).astype(BF16)
        o_ref[...] = _dot(cs, w_ref[...].astype(BF16)) + b_ref[...]

    return pl.pallas_call(
        body, name="ada_fwd", grid=(ADA_SH // tn,),
        out_shape=jax.ShapeDtypeStruct((nb, ADA_SH), F32),
        in_specs=[pl.BlockSpec((nb, D), lambda j: (0, 0)), pl.BlockSpec((D, tn), lambda j: (0, j)),
                  pl.BlockSpec((1, tn), lambda j: (0, j))],
        out_specs=pl.BlockSpec((nb, tn), lambda j: (0, j)),
        compiler_params=_cparams(),
    )(c_all, w_sh, b_sh)


def _ada_bwd_adam(c_all, dada_sh, w, m, v):
    nb = c_all.shape[0]
    tn = 768

    def body(c_ref, d_ref, w_ref, m_ref, v_ref, g_out, d_out, m_out, v_out):
        cv = c_ref[...]
        cs = (cv * _sigmoid(cv)).astype(BF16)
        g = _dot_tn(cs, d_ref[...].astype(BF16))
        delta, m2, v2 = _adam(w_ref[...], g, m_ref[...], v_ref[...])
        g_out[...] = g
        d_out[...] = delta
        m_out[...] = m2
        v_out[...] = v2

    big = pl.BlockSpec((D, tn), lambda j: (0, j))
    shape = jax.ShapeDtypeStruct((D, ADA_SH), F32)
    return pl.pallas_call(
        body, name="ada_bwd_adam", grid=(ADA_SH // tn,),
        out_shape=[shape] * 4,
        in_specs=[pl.BlockSpec((nb, D), lambda j: (0, 0)), pl.BlockSpec((nb, tn), lambda j: (0, j)), big, big, big],
        out_specs=[big] * 4,
        compiler_params=_cparams(),
    )(c_all, dada_sh, w, m, v)


def _tok_specs(tm, width):
    return pl.BlockSpec((1, tm, width), lambda b, i: (b, i, 0))


def _mod_spec():
    return pl.BlockSpec((1, 1, D), lambda b, i: (b, 0, 0))


def _row_spec(width=D):
    return pl.BlockSpec((1, width), lambda b, i: (0, 0))


def _ffn_loss_fwd(x, sh, sc, gt, g_pre, g_post, w_in4, w_out, target):
    nb, s, _ = x.shape
    tm = min(TM, s)

    def body(x_ref, sh_ref, sc_ref, gt_ref, gpre_ref, gpost_ref, win_ref, wout_ref, tgt_ref,
             xo_ref, df_ref, p_ref, ls_ref, dgpost_ref, dgt_ref):
        xv = x_ref[0]
        n, _ = _rms(xv)
        h = (n * gpre_ref[...]) * (1.0 + sc_ref[0]) + sh_ref[0]
        hb = h.astype(BF16)
        acc = jnp.zeros((tm, D), F32)
        for j in range(2):
            gate = _dot(hb, win_ref[j])
            up = _dot(hb, win_ref[2 + j])
            p_ref[0, :, j * FBLK:(j + 1) * FBLK] = gate.astype(BF16)
            p_ref[0, :, DFF + j * FBLK:DFF + (j + 1) * FBLK] = up.astype(BF16)
            a = (gate * _sigmoid(gate)) * up
            acc = acc + _dot(a.astype(BF16), wout_ref[j * FBLK:(j + 1) * FBLK, :])
        nf, q = _rms(acc)
        gpost = gpost_ref[...]
        half_gate = 0.5 * gt_ref[0]
        out = xv + half_gate * (nf * gpost)
        first = _first(pl.program_id(0), pl.program_id(1))
        err = out - tgt_ref[0]
        dout = err * (1.0 / D)
        xo_ref[0] = dout
        row = jnp.sum(err * err, axis=0, keepdims=True)
        part = row[:, 0:128]
        for k in range(1, D // 128):
            part = part + row[:, k * 128:(k + 1) * 128]
        _acc(ls_ref, part, first)
        df, dgpost = _rms_bwd(dout * half_gate, nf, q, gpost)
        df_ref[0] = df.astype(BF16)
        _acc(dgpost_ref, dgpost, first)
        _acc(dgt_ref, jnp.sum(dout * (0.5 * (nf * gpost)), axis=0, keepdims=True)[None], pl.program_id(1) == 0)

    tok = _tok_specs(tm, D)
    return pl.pallas_call(
        body, name="ffn_loss_fwd", grid=(nb, s // tm),
        out_shape=[jax.ShapeDtypeStruct((nb, s, D), F32), jax.ShapeDtypeStruct((nb, s, D), BF16),
                   jax.ShapeDtypeStruct((nb, s, 2 * DFF), BF16), jax.ShapeDtypeStruct((1, 128), F32),
                   jax.ShapeDtypeStruct((1, D), F32), jax.ShapeDtypeStruct((nb, 1, D), F32)],
        in_specs=[tok, _mod_spec(), _mod_spec(), _mod_spec(), _row_spec(), _row_spec(), VMEM_FULL, VMEM_FULL, tok],
        out_specs=[tok, tok, _tok_specs(tm, 2 * DFF), pl.BlockSpec((1, 128), lambda b, i: (0, 0)), _row_spec(), _mod_spec()],
        compiler_params=_cparams(),
    )(x, sh, sc, gt, g_pre, g_post, w_in4, w_out, target)


def _ffn_up(x, sh, sc, g_pre, w_in4):
    nb, s, _ = x.shape
    tm = min(TM, s)

    def body(x_ref, sh_ref, sc_ref, gpre_ref, win_ref, p_ref, a_ref):
        n, _ = _rms(x_ref[0])
        hb = ((n * gpre_ref[...]) * (1.0 + sc_ref[0]) + sh_ref[0]).astype(BF16)
        for j in range(2):
            gate = _dot(hb, win_ref[j])
            up = _dot(hb, win_ref[2 + j])
            p_ref[0, :, j * FBLK:(j + 1) * FBLK] = gate.astype(BF16)
            p_ref[0, :, DFF + j * FBLK:DFF + (j + 1) * FBLK] = up.astype(BF16)
            a_ref[0, :, j * FBLK:(j + 1) * FBLK] = ((gate * _sigmoid(gate)) * up).astype(BF16)

    return pl.pallas_call(
        body, name="ffn_up", grid=(nb, s // tm),
        out_shape=[jax.ShapeDtypeStruct((nb, s, 2 * DFF), BF16), jax.ShapeDtypeStruct((nb, s, DFF), BF16)],
        in_specs=[_tok_specs(tm, D), _mod_spec(), _mod_spec(), _row_spec(), VMEM_FULL],
        out_specs=[_tok_specs(tm, 2 * DFF), _tok_specs(tm, DFF)],
        compiler_params=_cparams(),
    )(x, sh, sc, g_pre, w_in4)


def _ffn_down(x, a, gt, g_post, w_out):
    nb, s, _ = x.shape
    tm = min(TM, s)

    def body(x_ref, a_ref, gt_ref, gpost_ref, wout_ref, xo_ref, f_ref):
        acc = _dot(a_ref[0], wout_ref[...])
        f_ref[0] = acc
        nf, _ = _rms(acc)
        xo_ref[0] = x_ref[0] + (0.5 * gt_ref[0]) * (nf * gpost_ref[...])

    tok = _tok_specs(tm, D)
    shape = jax.ShapeDtypeStruct((nb, s, D), F32)
    return pl.pallas_call(
        body, name="ffn_down", grid=(nb, s // tm), out_shape=[shape, shape],
        in_specs=[tok, _tok_specs(tm, DFF), _mod_spec(), _row_spec(), VMEM_FULL],
        out_specs=[tok, tok],
        compiler_params=_cparams(),
    )(x, a, gt, g_post, w_out)


def _ffn_bwd(dxo, x, f, p, sh, sc, gt, g_pre, g_post, w_in4, w_out, df=None):
    nb, s, _ = x.shape
    tm = min(TM_FFN_BWD, s)
    given = df is not None

    def body(*refs):
        if given:
            (dxo_ref, x_ref, dfin_ref, p_ref, sh_ref, sc_ref, gpre_ref, win_ref, wout_ref,
             dx_ref, dp_ref, h_ref, a_ref, dgpre_ref, dsh_ref, dsc_ref) = refs
        else:
            (dxo_ref, x_ref, f_ref, p_ref, sh_ref, sc_ref, gt_ref, gpre_ref, gpost_ref, win_ref, wout_ref,
             dx_ref, dp_ref, h_ref, a_ref, df_ref, dgpre_ref, dgpost_ref, dsh_ref, dsc_ref, dgt_ref) = refs
        b, i = pl.program_id(0), pl.program_id(1)
        dxo_v = dxo_ref[0]
        if given:
            dfb = dfin_ref[0]
        else:
            nf, q = _rms(f_ref[0])
            gpost = gpost_ref[...]
            dgt = jnp.sum(dxo_v * (0.5 * (nf * gpost)), axis=0, keepdims=True)
            do = dxo_v * (0.5 * gt_ref[0])
            dfv, dgpost = _rms_bwd(do, nf, q, gpost)
            dfb = dfv.astype(BF16)
            df_ref[0] = dfb
        xv = x_ref[0]
        n, r = _rms(xv)
        gpre = gpre_ref[...]
        ng = n * gpre
        scale1 = 1.0 + sc_ref[0]
        h = ng * scale1 + sh_ref[0]
        h_ref[0] = h.astype(BF16)
        dh = jnp.zeros((tm, D), F32)
        for j in range(2):
            gate = p_ref[0, :, j * FBLK:(j + 1) * FBLK].astype(F32)
            up = p_ref[0, :, DFF + j * FBLK:DFF + (j + 1) * FBLK].astype(F32)
            sg = _sigmoid(gate)
            act = gate * sg
            a_ref[0, :, j * FBLK:(j + 1) * FBLK] = (act * up).astype(BF16)
            da = _dot_nt(dfb, wout_ref[j * FBLK:(j + 1) * FBLK, :])
            dgate = (da * up * _dsilu(gate, sg)).astype(BF16)
            dup = (da * act).astype(BF16)
            dp_ref[0, :, j * FBLK:(j + 1) * FBLK] = dgate
            dp_ref[0, :, DFF + j * FBLK:DFF + (j + 1) * FBLK] = dup
            dh = dh + _dot_nt(dgate, win_ref[j]) + _dot_nt(dup, win_ref[2 + j])
        dsh = jnp.sum(dh, axis=0, keepdims=True)
        dsc = jnp.sum(dh * ng, axis=0, keepdims=True)
        dxn, dgpre = _rms_bwd(dh * scale1, n, r, gpre)
        dx_ref[0] = dxo_v + dxn
        _acc(dgpre_ref, dgpre, _first(b, i))
        _acc(dsh_ref, dsh[None], i == 0)
        _acc(dsc_ref, dsc[None], i == 0)
        if not given:
            _acc(dgpost_ref, dgpost, _first(b, i))
            _acc(dgt_ref, dgt[None], i == 0)

    tok = _tok_specs(tm, D)
    mod_shape = jax.ShapeDtypeStruct((nb, 1, D), F32)
    row_shape = jax.ShapeDtypeStruct((1, D), F32)
    big = [jax.ShapeDtypeStruct((nb, s, D), F32), jax.ShapeDtypeStruct((nb, s, 2 * DFF), BF16),
           jax.ShapeDtypeStruct((nb, s, D), BF16), jax.ShapeDtypeStruct((nb, s, DFF), BF16)]
    big_specs = [tok, _tok_specs(tm, 2 * DFF), tok, _tok_specs(tm, DFF)]
    if given:
        return pl.pallas_call(
            body, name="ffn_bwd_after_loss", grid=(nb, s // tm),
            out_shape=big + [row_shape, mod_shape, mod_shape],
            in_specs=[tok, tok, tok, _tok_specs(tm, 2 * DFF), _mod_spec(), _mod_spec(), _row_spec(), VMEM_FULL, VMEM_FULL],
            out_specs=big_specs + [_row_spec(), _mod_spec(), _mod_spec()],
            compiler_params=_cparams(),
        )(dxo, x, df, p, sh, sc, g_pre, w_in4, w_out)
    return pl.pallas_call(
        body, name="ffn_bwd", grid=(nb, s // tm),
        out_shape=big + [jax.ShapeDtypeStruct((nb, s, D), BF16), row_shape, row_shape, mod_shape, mod_shape, mod_shape],
        in_specs=[tok, tok, tok, _tok_specs(tm, 2 * DFF), _mod_spec(), _mod_spec(), _mod_spec(), _row_spec(), _row_spec(),
                  VMEM_FULL, VMEM_FULL],
        out_specs=big_specs + [tok, _row_spec(), _row_spec(), _mod_spec(), _mod_spec(), _mod_spec()],
        compiler_params=_cparams(),
    )(dxo, x, f, p, sh, sc, gt, g_pre, g_post, w_in4, w_out)


def _wgrad(name, a, b, col_block, chip_major):
    t, ka = a.shape
    n = b.shape[1]
    def vmem_bytes(rows):
        return 2 * 2 * rows * (ka + col_block) + 4 * ka * col_block + 2 * (4 + 2) * ka * col_block

    tk = min(t, 512)
    while tk * 2 <= t and t % (tk * 2) == 0 and vmem_bytes(tk * 2) <= WGRAD_VMEM_BUDGET:
        tk *= 2
    nk = t // tk
    nblk = n // col_block

    def body(a_ref, b_ref, o_ref, obf_ref, acc_ref):
        k = pl.program_id(1)

        @pl.when(k == 0)
        def _():
            acc_ref[...] = jnp.zeros_like(acc_ref)

        acc_ref[...] += _dot_tn(a_ref[...], b_ref[...])

        @pl.when(k == nk - 1)
        def _():
            val = acc_ref[...]
            if chip_major:
                o_ref[0] = val
                obf_ref[0] = val.astype(BF16)
            else:
                o_ref[...] = val
                obf_ref[...] = val.astype(BF16)

    if chip_major:
        shape = (nblk, ka, col_block)
        ospec = pl.BlockSpec((1, ka, col_block), lambda j, k: (j, 0, 0))
    else:
        shape = (ka, n)
        ospec = pl.BlockSpec((ka, col_block), lambda j, k: (0, j))
    return pl.pallas_call(
        body, name=name, grid=(nblk, nk),
        out_shape=[jax.ShapeDtypeStruct(shape, F32), jax.ShapeDtypeStruct(shape, BF16)],
        in_specs=[pl.BlockSpec((tk, ka), lambda j, k: (k, 0)), pl.BlockSpec((tk, col_block), lambda j, k: (k, j))],
        out_specs=[ospec, ospec],
        scratch_shapes=[pltpu.VMEM((ka, col_block), F32)],
        compiler_params=_cparams(),
    )(a, b)


def _mix_in_fwd(x, sh, sc, g_pre, w_mi4):
    nb, s, _ = x.shape
    tm = min(TM, s)

    def body(x_ref, sh_ref, sc_ref, gpre_ref, w_ref, u_ref, v_ref, a_ref, g_ref):
        n, _ = _rms(x_ref[0])
        hb = ((n * gpre_ref[...]) * (1.0 + sc_ref[0]) + sh_ref[0]).astype(BF16)
        for k, o_ref in enumerate((u_ref, v_ref, a_ref, g_ref)):
            o_ref[0] = _dot(hb, w_ref[k])

    shape = jax.ShapeDtypeStruct((nb, s, WA), F32)
    return pl.pallas_call(
        body, name="mix_in_fwd", grid=(nb, s // tm),
        out_shape=[shape] * 4,
        in_specs=[_tok_specs(tm, D), _mod_spec(), _mod_spec(), _row_spec(), VMEM_FULL],
        out_specs=[_tok_specs(tm, WA)] * 4,
        compiler_params=_cparams(),
    )(x, sh, sc, g_pre, w_mi4)


def _spatial_weights(wcat_ref, transposed):
    w = wcat_ref[...]
    row = lax.broadcasted_iota(jnp.int32, w.shape, 0)
    col = lax.broadcasted_iota(jnp.int32, w.shape, 1)
    keep = ((row & (CH - 1)) <= col) if transposed else ((col & (CH - 1)) <= row)
    return jnp.where(keep, w, 0.0).astype(BF16)


def _expand_heads(vc, masks):
    return jnp.concatenate([jnp.where(mk, vc, jnp.zeros_like(vc)) for mk in masks], axis=0)


def _spatial_bias(bspt_ref):
    return bspt_ref[...]


SHIFTS = 8
TAP_ROWS = 32


def _ext_rows(tm):
    return tm + HALO + SHIFTS


def _make_shifts(ext_ref, sh_ref, tm):
    ext_ref[tm + HALO:tm + HALO + SHIFTS, :] = jnp.zeros((SHIFTS, WB), F32)
    for r in range(SHIFTS):
        sh_ref[r] = ext_ref[r:r + tm + HALO, :]


def _conv_taps(sh_ref, w_ref, tm, taps, emit):
    def block(i, carry):
        r0 = pl.multiple_of(i * TAP_ROWS, TAP_ROWS)
        acc = jnp.zeros((TAP_ROWS, WB), F32)
        for o, k in taps:
            acc = acc + w_ref[k:k + 1, :] * sh_ref[o % SHIFTS, pl.ds(r0 + SHIFTS * (o // SHIFTS), TAP_ROWS), :]
        emit(r0, acc)
        return carry

    lax.fori_loop(0, tm // TAP_ROWS, block, 0)


def _halo_prev_spec(tm):
    return pl.BlockSpec((1, HALO, WB), lambda b, i: (b, jnp.maximum(i * (tm // HALO) - 1, 0), 0))


def _halo_next_spec(tm, s):
    return pl.BlockSpec((1, HALO, WB), lambda b, i: (b, jnp.minimum((i + 1) * (tm // HALO), s // HALO - 1), 0))


def _mix_mid_fwd(x, u, v, a, g, gt, gn_g, gn_b, wcat, bspt, conv_w, conv_b, cn_g, cn_b, go_a, go_b, w_mo, g_post):
    nb, s, _ = x.shape
    tm = min(TM, s)

    def body(x_ref, u_ref, v_ref, a_ref, g_ref, ah_ref, gh_ref, gt_ref, gng_ref, gnb_ref, wcat_ref, bspt_ref,
             cw_ref, cb_ref, cng_ref, cnb_ref, goa_ref, gob_ref, wmo_ref, gpost_ref,
             xo_ref, conv_ref, y_ref, m_ref, ext_ref, sh_ref):
        i = pl.program_id(1)
        xhat, _ = _ln(v_ref[0])
        vb = (xhat * gng_ref[...] + gnb_ref[...]).astype(BF16)
        wsb = _spatial_weights(wcat_ref, False)
        bias = _spatial_bias(bspt_ref)
        masks = _head_mask((CH, WA))
        zs = []
        for cidx in range(tm // CH):
            vexp = _expand_heads(vb[cidx * CH:(cidx + 1) * CH, :], masks)
            zs.append(_dot(wsb, vexp) + bias)
        z = jnp.concatenate(zs, axis=0)
        na, _ = _rms(u_ref[0] * z)
        keep = jnp.where(i == 0, 0.0, 1.0).astype(F32)
        ext_ref[0:HALO, :] = (ah_ref[0] * _sigmoid(gh_ref[0])) * keep
        ext_ref[HALO:HALO + tm, :] = a_ref[0] * _sigmoid(g_ref[0])
        _make_shifts(ext_ref, sh_ref, tm)
        cb = cb_ref[...]

        def put_conv(r0, acc):
            conv_ref[0, pl.ds(r0, TAP_ROWS), :] = acc + cb

        _conv_taps(sh_ref, cw_ref, tm, [(k + HALO - (CK - 1), k) for k in range(CK)], put_conv)
        conv = conv_ref[0]
        chat, _ = _ln(conv)
        cln = chat * cng_ref[...] + cnb_ref[...]
        nbb, _ = _rms(cln * _sigmoid(cln))
        yb = jnp.concatenate([na * goa_ref[...], nbb * gob_ref[...]], axis=1).astype(BF16)
        y_ref[0] = yb
        m = _dot(yb, wmo_ref[...])
        m_ref[0] = m
        nm, _ = _rms(m)
        xo_ref[0] = x_ref[0] + gt_ref[0] * (nm * gpost_ref[...])

    t5 = _tok_specs(tm, WA)
    tok = _tok_specs(tm, D)
    r5 = _row_spec(WA)
    full = lambda shape: pl.BlockSpec(shape, lambda b, i: (0,) * len(shape))
    return pl.pallas_call(
        body, name="mix_mid_fwd", grid=(nb, s // tm),
        out_shape=[jax.ShapeDtypeStruct((nb, s, D), F32), jax.ShapeDtypeStruct((nb, s, WB), F32),
                   jax.ShapeDtypeStruct((nb, s, D), BF16), jax.ShapeDtypeStruct((nb, s, D), F32)],
        in_specs=[tok, t5, t5, t5, t5, _halo_prev_spec(tm), _halo_prev_spec(tm), _mod_spec(), r5, r5,
                  full((CH, NH * CH)), full((CH, WA)), full((HALO, WB)), r5, r5, r5, r5, r5, VMEM_FULL, _row_spec()],
        out_specs=[tok, t5, tok, tok],
        scratch_shapes=[pltpu.VMEM((_ext_rows(tm), WB), F32), pltpu.VMEM((SHIFTS, tm + HALO, WB), F32)],
        compiler_params=_cparams(),
    )(x, u, v, a, g, a, g, gt, gn_g, gn_b, wcat, bspt, conv_w, conv_b, cn_g, cn_b, go_a, go_b, w_mo, g_post)


def _mix_out_bwd(dxo, m, gt, g_post, w_mo):
    nb, s, _ = m.shape
    tm = min(TM, s)

    def body(dxo_ref, m_ref, gt_ref, gpost_ref, wmo_ref, dy_ref, dm_ref, dgpost_ref, dgt_ref):
        b, i = pl.program_id(0), pl.program_id(1)
        dxo_v = dxo_ref[0]
        nm, q = _rms(m_ref[0])
        gpost = gpost_ref[...]
        dgt = jnp.sum(dxo_v * (nm * gpost), axis=0, keepdims=True)
        dm, dgpost = _rms_bwd(dxo_v * gt_ref[0], nm, q, gpost)
        dmb = dm.astype(BF16)
        dm_ref[0] = dmb
        dy_ref[0] = _dot_nt(dmb, wmo_ref[...])
        _acc(dgpost_ref, dgpost, _first(b, i))
        _acc(dgt_ref, dgt[None], i == 0)

    tok = _tok_specs(tm, D)
    return pl.pallas_call(
        body, name="mix_out_bwd", grid=(nb, s // tm),
        out_shape=[jax.ShapeDtypeStruct((nb, s, D), F32), jax.ShapeDtypeStruct((nb, s, D), BF16),
                   jax.ShapeDtypeStruct((1, D), F32), jax.ShapeDtypeStruct((nb, 1, D), F32)],
        in_specs=[tok, tok, _mod_spec(), _row_spec(), VMEM_FULL],
        out_specs=[tok, tok, _row_spec(), _mod_spec()],
        compiler_params=_cparams(),
    )(dxo, m, gt, g_post, w_mo)


def _mix_mid_bwd(dy, u, v, conv, gn_g, gn_b, wcat, wcat_t, bspt, cn_g, cn_b, go_a, go_b):
    nb, s, _ = dy.shape
    tm = min(TM, s)
    nchunk = tm // CH

    def body(dy_ref, u_ref, v_ref, conv_ref, gng_ref, gnb_ref, wcat_ref, wcatt_ref, bspt_ref, cng_ref, cnb_ref,
             goa_ref, gob_ref,
             du_ref, dv_ref, dconv_ref, dwcat_ref, dbsp_ref, dgng_ref, dgnb_ref, dgoa_ref, dgob_ref,
             dcng_ref, dcnb_ref, dcb_ref):
        first = _first(pl.program_id(0), pl.program_id(1))
        dyv = dy_ref[0]
        xhat, rstd = _ln(v_ref[0])
        gng = gng_ref[...]
        vb = (xhat * gng + gnb_ref[...]).astype(BF16)
        wsb = _spatial_weights(wcat_ref, False)
        wsb_t = _spatial_weights(wcatt_ref, True)
        bias = _spatial_bias(bspt_ref)
        masks = _head_mask((CH, WA))
        vexps, zs = [], []
        for cidx in range(nchunk):
            vexp = _expand_heads(vb[cidx * CH:(cidx + 1) * CH, :], masks)
            vexps.append(vexp)
            zs.append(_dot(wsb, vexp) + bias)
        z = jnp.concatenate(zs, axis=0)
        uv = u_ref[0]
        na, ra = _rms(uv * z)
        dya, dgoa = _rms_bwd(dyv[:, 0:WA], na, ra, goa_ref[...])
        du_ref[0] = dya * z
        dz = dya * uv
        dwcat = jnp.zeros((CH, NH * CH), F32)
        dzsum = jnp.zeros((CH, WA), F32)
        dvlns = []
        for cidx in range(nchunk):
            dzc = dz[cidx * CH:(cidx + 1) * CH, :]
            dzsum = dzsum + dzc
            dzb = dzc.astype(BF16)
            dwcat = dwcat + _dot_nt(dzb, vexps[cidx])
            dvexp = _dot(wsb_t, dzb)
            dvl = jnp.zeros((CH, WA), F32)
            for h in range(NH):
                dvl = dvl + jnp.where(masks[h], dvexp[h * CH:(h + 1) * CH, :], 0.0)
            dvlns.append(dvl)
        dvln = jnp.concatenate(dvlns, axis=0)
        dv, dgng, dgnb = _ln_bwd(dvln, xhat, rstd, gng)
        dv_ref[0] = dv
        lane = lax.broadcasted_iota(jnp.int32, (NH, WA), 1)
        head = lax.broadcasted_iota(jnp.int32, (NH, WA), 0)
        sel = jnp.where((lane >= head * HD) & (lane < (head + 1) * HD), 1.0, 0.0).astype(F32)
        dbsp = lax.dot_general(sel, dzsum, NT, preferred_element_type=F32, precision=lax.Precision.HIGHEST)
        chat, crstd = _ln(conv_ref[0])
        cng = cng_ref[...]
        cln = chat * cng + cnb_ref[...]
        sg = _sigmoid(cln)
        nbb, rb = _rms(cln * sg)
        dyb, dgob = _rms_bwd(dyv[:, WA:D], nbb, rb, gob_ref[...])
        dconv, dcng, dcnb = _ln_bwd(dyb * _dsilu(cln, sg), chat, crstd, cng)
        dconv_ref[0] = dconv
        dcb = jnp.sum(dconv, axis=0, keepdims=True)
        for ref, val in ((dwcat_ref, dwcat), (dbsp_ref, dbsp), (dgng_ref, dgng), (dgnb_ref, dgnb), (dgoa_ref, dgoa),
                         (dgob_ref, dgob), (dcng_ref, dcng), (dcnb_ref, dcnb), (dcb_ref, dcb)):
            _acc(ref, val, first)

    t5 = _tok_specs(tm, WA)
    r5 = _row_spec(WA)
    full = lambda shape: pl.BlockSpec(shape, lambda b, i: (0,) * len(shape))
    big = jax.ShapeDtypeStruct((nb, s, WA), F32)
    row = jax.ShapeDtypeStruct((1, WA), F32)
    return pl.pallas_call(
        body, name="mix_mid_bwd", grid=(nb, s // tm),
        out_shape=[big, big, big, jax.ShapeDtypeStruct((CH, NH * CH), F32), jax.ShapeDtypeStruct((NH, CH), F32),
                   row, row, row, row, row, row, row],
        in_specs=[_tok_specs(tm, D), t5, t5, t5, r5, r5, full((CH, NH * CH)), full((NH * CH, CH)), full((CH, WA)),
                  r5, r5, r5, r5],
        out_specs=[t5, t5, t5, full((CH, NH * CH)), full((NH, CH)), r5, r5, r5, r5, r5, r5, r5],
        compiler_params=_cparams(),
    )(dy, u, v, conv, gn_g, gn_b, wcat, wcat_t, bspt, cn_g, cn_b, go_a, go_b)


def _mix_in_bwd(dxo, x, du, dv, dconv, a, g, sh, sc, g_pre, w_mi4, conv_w):
    nb, s, _ = x.shape
    tm = min(TM, s)
    n_i = s // tm

    def body(dxo_ref, x_ref, du_ref, dv_ref, dc_ref, dch_ref, a_ref, g_ref, ah_ref, gh_ref, sh_ref, sc_ref,
             gpre_ref, w_ref, cw_ref,
             dx_ref, dproj_ref, h_ref, dgpre_ref, dsh_ref, dsc_ref, dcw_ref, ext_ref, shf_ref, dglu_ref):
        b, i = pl.program_id(0), pl.program_id(1)
        first = _first(b, i)
        av, gv = a_ref[0], g_ref[0]
        sg = _sigmoid(gv)
        dconv = dc_ref[0]
        ext_ref[0:tm, :] = dconv
        ext_ref[tm:tm + HALO, :] = dch_ref[0] * jnp.where(i == n_i - 1, 0.0, 1.0).astype(F32)
        _make_shifts(ext_ref, shf_ref, tm)

        def put_dglu(r0, acc):
            dglu_ref[pl.ds(r0, TAP_ROWS), :] = acc

        _conv_taps(shf_ref, cw_ref, tm, [(CK - 1 - k, k) for k in range(CK)], put_dglu)
        dglu = dglu_ref[...]
        ext_ref[0:HALO, :] = (ah_ref[0] * _sigmoid(gh_ref[0])) * jnp.where(i == 0, 0.0, 1.0).astype(F32)
        ext_ref[HALO:HALO + tm, :] = av * sg
        _make_shifts(ext_ref, shf_ref, tm)

        @pl.when(first)
        def _():
            dcw_ref[...] = jnp.zeros((HALO, WB), F32)

        for k in range(CK):
            o = k + HALO - (CK - 1)
            lo = SHIFTS * (o // SHIFTS)
            dcw_ref[k:k + 1, :] += jnp.sum(dconv * shf_ref[o % SHIFTS, lo:lo + tm, :], axis=0, keepdims=True)
        da = dglu * sg
        dg = dglu * av * (sg * (1.0 - sg))
        parts = [du_ref[0].astype(BF16), dv_ref[0].astype(BF16), da.astype(BF16), dg.astype(BF16)]
        dh = jnp.zeros((tm, D), F32)
        for k in range(4):
            dproj_ref[0, :, k * WA:(k + 1) * WA] = parts[k]
            dh = dh + _dot_nt(parts[k], w_ref[k])
        n, r = _rms(x_ref[0])
        gpre = gpre_ref[...]
        ng = n * gpre
        scale1 = 1.0 + sc_ref[0]
        h_ref[0] = (ng * scale1 + sh_ref[0]).astype(BF16)
        dsh = jnp.sum(dh, axis=0, keepdims=True)
        dsc = jnp.sum(dh * ng, axis=0, keepdims=True)
        dxn, dgpre = _rms_bwd(dh * scale1, n, r, gpre)
        dx_ref[0] = dxo_ref[0] + dxn
        _acc(dgpre_ref, dgpre, first)
        _acc(dsh_ref, dsh[None], i == 0)
        _acc(dsc_ref, dsc[None], i == 0)

    tok = _tok_specs(tm, D)
    t5 = _tok_specs(tm, WA)
    full = lambda shape: pl.BlockSpec(shape, lambda b, i: (0,) * len(shape))
    mod_shape = jax.ShapeDtypeStruct((nb, 1, D), F32)
    return pl.pallas_call(
        body, name="mix_in_bwd", grid=(nb, n_i),
        out_shape=[jax.ShapeDtypeStruct((nb, s, D), F32), jax.ShapeDtypeStruct((nb, s, 4 * WA), BF16),
                   jax.ShapeDtypeStruct((nb, s, D), BF16), jax.ShapeDtypeStruct((1, D), F32), mod_shape, mod_shape,
                   jax.ShapeDtypeStruct((HALO, WB), F32)],
        in_specs=[tok, tok, t5, t5, t5, _halo_next_spec(tm, s), t5, t5, _halo_prev_spec(tm), _halo_prev_spec(tm),
                  _mod_spec(), _mod_spec(), _row_spec(), VMEM_FULL, full((HALO, WB))],
        out_specs=[tok, _tok_specs(tm, 4 * WA), tok, _row_spec(), _mod_spec(), _mod_spec(), full((HALO, WB))],
        scratch_shapes=[pltpu.VMEM((_ext_rows(tm), WB), F32), pltpu.VMEM((SHIFTS, tm + HALO, WB), F32),
                        pltpu.VMEM((tm, WB), F32)],
        compiler_params=_cparams(),
    )(dxo, x, du, dv, dconv, dconv, a, g, a, g, sh, sc, g_pre, w_mi4, conv_w)


def _row_tile(rows, cols):
    best = 16
    for t in range(16, rows + 1, 16):
        if rows % t == 0 and t * cols * 4 <= 1536 * 1024:
            best = t
    return best


def _walk(steps):
    offs = [sum(steps[:k]) for k in range(len(steps))]

    def tile(k):
        return lambda i: jnp.clip(i - offs[k], 0, steps[k] - 1)

    def mine(k, i):
        return jnp.logical_and(i >= offs[k], i < offs[k] + steps[k])

    return sum(steps), tile, mine


def _sum4(name, own4s, recvs, j_arr):
    n = len(own4s)
    shapes = [o.shape[1:] for o in own4s]
    trs = [_row_tile(r, c) for r, c in shapes]
    total, tile, mine = _walk([r // tr for (r, _), tr in zip(shapes, trs)])

    def body(j_ref, *refs):
        del j_ref
        i = pl.program_id(0)
        for k in range(n):
            own_ref, recv_ref, o_ref = refs[2 * k], refs[2 * k + 1], refs[2 * n + k]

            def add(own_ref=own_ref, recv_ref=recv_ref, o_ref=o_ref):
                acc = own_ref[0]
                for q in range(3):
                    acc = acc + recv_ref[q].astype(F32)
                o_ref[...] = acc

            pl.when(mine(k, i))(add)

    in_specs, out_specs = [], []
    for k, ((_, cols), tr) in enumerate(zip(shapes, trs)):
        in_specs += [pl.BlockSpec((1, tr, cols), lambda i, j, t=tile(k): (j[0], t(i), 0)),
                     pl.BlockSpec((3, tr, cols), lambda i, j, t=tile(k): (0, t(i), 0))]
        out_specs.append(pl.BlockSpec((tr, cols), lambda i, j, t=tile(k): (t(i), 0)))
    return pl.pallas_call(
        body, name=name,
        grid_spec=pltpu.PrefetchScalarGridSpec(num_scalar_prefetch=1, grid=(total,), in_specs=in_specs, out_specs=out_specs),
        out_shape=[jax.ShapeDtypeStruct(sh, F32) for sh in shapes],
        compiler_params=_cparams(),
    )(j_arr, *[a for pair in zip(own4s, recvs) for a in pair])


def _pair_plan(shapes):
    def plan(x, y, c, src, land):
        sends = []
        for a, shape in enumerate(shapes):
            rows = shape[1] // 2
            theirs = pl.ds(pl.multiple_of((1 - c) * rows, 16), rows)
            sends.append((src[a].at[:, theirs], land[a], (x, y, 1 - c), land[a]))
        return [], sends

    return plan


def _swap_plan(n):
    def plan(x, y, c, src, land):
        return [], [(src[a], land[a], (x, y, 1 - c), land[a]) for a in range(n)]

    return plan


def _pair_sum(name, g32s, recvs, c_arr):
    n = len(g32s)
    shapes = [r.shape for r in recvs]
    trs = [_row_tile(rows, cols) for _, rows, cols in shapes]
    nhs = [rows // tr for (_, rows, _), tr in zip(shapes, trs)]
    total, tile, mine = _walk([nblk * nh for (nblk, _, _), nh in zip(shapes, nhs)])

    def body(c_ref, *refs):
        del c_ref
        i = pl.program_id(0)
        for k in range(n):
            g_ref, r_ref, o32_ref, obf_ref = refs[2 * k], refs[2 * k + 1], refs[2 * n + 2 * k], refs[2 * n + 2 * k + 1]

            def add(g_ref=g_ref, r_ref=r_ref, o32_ref=o32_ref, obf_ref=obf_ref):
                val = g_ref[0] + r_ref[0].astype(F32)
                o32_ref[0] = val
                obf_ref[0] = val.astype(BF16)

            pl.when(mine(k, i))(add)

    in_specs, out_specs, out_shape = [], [], []
    for k, ((_, _, cols), tr, nh) in enumerate(zip(shapes, trs, nhs)):
        def half(tr=tr, cols=cols, t=tile(k), nh=nh):
            return pl.BlockSpec((1, tr, cols), lambda i, c: (t(i) // nh, t(i) % nh, 0))

        in_specs += [pl.BlockSpec((1, tr, cols), lambda i, c, t=tile(k), nh=nh: (t(i) // nh, c[0] * nh + t(i) % nh, 0)), half()]
        out_specs += [half(), half()]
        out_shape += [jax.ShapeDtypeStruct(shapes[k], F32), jax.ShapeDtypeStruct(shapes[k], BF16)]
    res = pl.pallas_call(
        body, name=name,
        grid_spec=pltpu.PrefetchScalarGridSpec(num_scalar_prefetch=1, grid=(total,), in_specs=in_specs, out_specs=out_specs),
        out_shape=out_shape,
        compiler_params=_cparams(),
    )(c_arr, *[a for pair in zip(g32s, recvs) for a in pair])
    return [(res[2 * k], res[2 * k + 1]) for k in range(n)]


def _adam_halves(name, w, m, v, mine, theirs, c_arr):
    rows, cols = w.shape
    tr = _row_tile(rows // 2, cols)
    nh = (rows // 2) // tr

    def body(c_ref, w_ref, m_ref, v_ref, mine_ref, theirs_ref, g_out, d_out, m_out, v_out):
        here = (pl.program_id(0) // nh) == c_ref[0]
        g = jnp.where(here, mine_ref[...], theirs_ref[...])
        delta, m2, v2 = _adam(w_ref[...], g, m_ref[...], v_ref[...])
        g_out[...] = g
        d_out[...] = delta
        m_out[...] = m2
        v_out[...] = v2

    spec = pl.BlockSpec((tr, cols), lambda i, c: (i, 0))
    shape = jax.ShapeDtypeStruct((rows, cols), F32)
    return pl.pallas_call(
        body, name=name,
        grid_spec=pltpu.PrefetchScalarGridSpec(
            num_scalar_prefetch=1, grid=(2 * nh,),
            in_specs=[spec, spec, spec,
                      pl.BlockSpec((tr, cols), lambda i, c: (jnp.clip(i - c[0] * nh, 0, nh - 1), 0)),
                      pl.BlockSpec((tr, cols), lambda i, c: (jnp.clip(i - (1 - c[0]) * nh, 0, nh - 1), 0))],
            out_specs=[spec] * 4),
        out_shape=[shape] * 4,
        compiler_params=_cparams(),
    )(c_arr, w, m, v, mine, theirs)


def _adam_big(name, w, m, v, ga, gb):
    rows, cols = w.shape
    tr = _row_tile(rows, cols)

    def body(w_ref, m_ref, v_ref, ga_ref, gb_ref, g_out, d_out, m_out, v_out):
        gsum = ga_ref[...] + gb_ref[...]
        delta, m2, v2 = _adam(w_ref[...], gsum, m_ref[...], v_ref[...])
        g_out[...] = gsum
        d_out[...] = delta
        m_out[...] = m2
        v_out[...] = v2

    spec = pl.BlockSpec((tr, cols), lambda i: (i, 0))
    shape = jax.ShapeDtypeStruct((rows, cols), F32)
    return pl.pallas_call(
        body, name=name, grid=(rows // tr,), out_shape=[shape] * 4,
        in_specs=[spec] * 5, out_specs=[spec] * 4, compiler_params=_cparams(),
    )(w, m, v, ga, gb)


PK_VEC = 0
PK_LOSS = 6
PK_PAIR = 8
PK_BSP = 16
PK_WCAT = 24
PK_ROWS = PK_WCAT + CH
PAIR_ORDER = ("gmlp_norm_g", "gmlp_norm_b", "conv_b", "conv_norm_g", "conv_norm_b", "g_out_a", "g_out_b")
VEC_ORDER = ("g_pre_f1", "g_post_f1", "g_pre_m", "g_post_m", "g_pre_f2", "g_post_f2")


def _pack_late(rows):
    counts = [r.shape[0] for r in rows]
    assert sum(counts) == 8

    def body(*refs):
        o_ref = refs[-1]
        at = 0
        for r, cnt in zip(refs[:-1], counts):
            o_ref[at:at + cnt, :] = r[...]
            at += cnt

    return pl.pallas_call(
        body, name="pack_late", out_shape=jax.ShapeDtypeStruct((8, D), F32),
        in_specs=[VMEM_FULL] * len(rows), out_specs=VMEM_FULL, compiler_params=_cparams(),
    )(*rows)


def _pack_small(vecs, pairs, dbsp, dwcat, lsum):
    def body(*refs):
        vec_refs = refs[:4]
        pair_refs = refs[4:11]
        dbsp_ref, dwcat_ref, lsum_ref, o_ref = refs[11:]
        o_ref[0:PK_WCAT, :] = jnp.zeros((PK_WCAT, D), F32)
        o_ref[PK_LOSS:PK_LOSS + 1, 0:128] = lsum_ref[...]
        for k, r in enumerate(vec_refs):
            o_ref[PK_VEC + 2 + k:PK_VEC + 3 + k, :] = r[...]
        for k, r in enumerate(pair_refs):
            row, half = PK_PAIR + k // 2, k % 2
            o_ref[row:row + 1, half * WA:(half + 1) * WA] = r[...]
        o_ref[PK_BSP:PK_BSP + NH, 0:CH] = dbsp_ref[...]
        o_ref[PK_WCAT:PK_ROWS, :] = dwcat_ref[...]

    args = list(vecs) + list(pairs) + [dbsp, dwcat, lsum]
    return pl.pallas_call(
        body, name="pack_small", out_shape=jax.ShapeDtypeStruct((PK_ROWS, D), F32),
        in_specs=[VMEM_FULL] * len(args), out_specs=VMEM_FULL, compiler_params=_cparams(),
    )(*args)


def _small_adam(pack_all, late_all, dcw_all, dada_all, params, behind):
    names = list(VEC_ORDER) + list(PAIR_ORDER) + ["b_spatial", "w_spatial", "conv_w", "b_ada"]
    flat = []
    for nm in names:
        flat += list(params[nm])
    n_in = 4 + len(flat)

    def body(*refs):
        pack_ref, late_ref, dcw_ref, dada_ref = refs[:4]
        prm = refs[4:n_in]
        outs = refs[n_in + 1:]

        def total(r0, nr, c0, nc):
            acc = pack_ref[0, r0:r0 + nr, c0:c0 + nc]
            for d in range(1, NDEV):
                acc = acc + pack_ref[d, r0:r0 + nr, c0:c0 + nc]
            return acc

        def emit(idx, g, getw, put):
            w_ref, m_ref, v_ref = prm[3 * idx:3 * idx + 3]
            delta, m2, v2 = _adam(getw(w_ref), g, getw(m_ref), getw(v_ref))
            for o_ref, val in zip(outs[4 * idx:4 * idx + 4], (g, delta, m2, v2)):
                put(o_ref, val)

        def whole(ref):
            return ref[...]

        def put_whole(ref, val):
            ref[...] = val

        idx = 0
        for k in range(6):
            if k < 2:
                g = late_ref[0, k:k + 1, :]
                for d in range(1, NDEV):
                    g = g + late_ref[d, k:k + 1, :]
            else:
                g = total(PK_VEC + k, 1, 0, D)
            emit(idx, g, whole, put_whole)
            idx += 1
        for k in range(7):
            emit(idx, total(PK_PAIR + k // 2, 1, (k % 2) * WA, WA), whole, put_whole)
            idx += 1
        emit(idx, total(PK_BSP, NH, 0, CH), lambda r: r[0], lambda r, val: r.__setitem__(0, val))
        idx += 1
        row = lax.broadcasted_iota(jnp.int32, (CH, CH), 0)
        col = lax.broadcasted_iota(jnp.int32, (CH, CH), 1)
        for h in range(NH):
            gh = jnp.where(col <= row, total(PK_WCAT, CH, h * CH, CH), 0.0)
            w_ref, m_ref, v_ref = prm[3 * idx:3 * idx + 3]
            delta, m2, v2 = _adam(w_ref[0, h], gh, m_ref[0, h], v_ref[0, h])
            for o_ref, val in zip(outs[4 * idx:4 * idx + 4], (gh, delta, m2, v2)):
                o_ref[0, h] = val
        idx += 1
        gcw = dcw_ref[0, 0:CK, :]
        for d in range(1, NDEV):
            gcw = gcw + dcw_ref[d, 0:CK, :]
        emit(idx, gcw, lambda r: r[0], lambda r, val: r.__setitem__(0, val))
        idx += 1
        emit(idx, jnp.sum(dada_ref[...], axis=0, keepdims=True), whole, put_whole)
        outs[-1][...] = jnp.sum(total(PK_LOSS, 1, 0, 128), axis=1, keepdims=True) * (0.5 / D)

    out_shape = []
    for nm in names:
        w = params[nm][0]
        out_shape += [jax.ShapeDtypeStruct(w.shape, F32)] * 4
    out_shape.append(jax.ShapeDtypeStruct((1, 1), F32))
    res = pl.pallas_call(
        body, name="small_adam", out_shape=out_shape,
        in_specs=[VMEM_FULL] * n_in + [ANY], out_specs=[VMEM_FULL] * len(out_shape), compiler_params=_cparams(),
    )(pack_all, late_all, dcw_all, dada_all, *flat, behind)
    return {nm: tuple(res[4 * k:4 * k + 4]) for k, nm in enumerate(names)}, res[-1].reshape(())


WEIGHTS = ['w_ada', 'b_ada', 'g_pre_f1', 'g_post_f1', 'w_f1_in', 'w_f1_out', 'g_pre_m', 'g_post_m', 'w_mix_in',
           'gmlp_norm_g', 'gmlp_norm_b', 'w_spatial', 'b_spatial', 'conv_w', 'conv_b', 'conv_norm_g', 'conv_norm_b',
           'g_out_a', 'g_out_b', 'w_mix_out', 'g_pre_f2', 'g_post_f2', 'w_f2_in', 'w_f2_out']


def kernel(x, c, w_ada, b_ada, g_pre_f1, g_post_f1, w_f1_in, w_f1_out, g_pre_m, g_post_m, w_mix_in, gmlp_norm_g, gmlp_norm_b, w_spatial, b_spatial, conv_w, conv_b, conv_norm_g, conv_norm_b, g_out_a, g_out_b, w_mix_out, g_pre_f2, g_post_f2, w_f2_in, w_f2_out, loss_target, m_w_ada, m_b_ada, m_g_pre_f1, m_g_post_f1, m_w_f1_in, m_w_f1_out, m_g_pre_m, m_g_post_m, m_w_mix_in, m_gmlp_norm_g, m_gmlp_norm_b, m_w_spatial, m_b_spatial, m_conv_w, m_conv_b, m_conv_norm_g, m_conv_norm_b, m_g_out_a, m_g_out_b, m_w_mix_out, m_g_pre_f2, m_g_post_f2, m_w_f2_in, m_w_f2_out, v_w_ada, v_b_ada, v_g_pre_f1, v_g_post_f1, v_w_f1_in, v_w_f1_out, v_g_pre_m, v_g_post_m, v_w_mix_in, v_gmlp_norm_g, v_gmlp_norm_b, v_w_spatial, v_b_spatial, v_conv_w, v_conv_b, v_conv_norm_g, v_conv_norm_b, v_g_out_a, v_g_out_b, v_w_mix_out, v_g_pre_f2, v_g_post_f2, v_w_f2_in, v_w_f2_out):
    env = dict(locals())
    wts = {n: env[n] for n in WEIGHTS}
    mom = {n: env["m_" + n] for n in WEIGHTS}
    var = {n: env["v_" + n] for n in WEIGHTS}
    nb, s, _ = x.shape
    t = nb * s
    ax, ay, ac = lax.axis_index("x"), lax.axis_index("y"), lax.axis_index("c")
    j_chip = 2 * ax + ay
    dev = 4 * ax + 2 * ay + ac
    j_arr = j_chip.reshape(1).astype(jnp.int32)

    groups = (("w_f1_in",), ("w_mix_in", "w_mix_out"), ("w_f2_in", "w_f2_out"), ("w_f1_out",))
    def gather_operands(gi):
        srcs = [wts[n][0].astype(BF16) for n in groups[gi]] + ([conv_w[0]] if gi == 1 else [])
        lands = [lax.dynamic_update_index_in_dim(lax.empty((NCHIP,) + a.shape, a.dtype), a, j_chip, 0) for a in srcs]
        return srcs, lands

    def gather_start(gi, behind, operands=None):
        srcs, lands = operands or gather_operands(gi)
        plan_a, plan_b, n_b = _gather_plans([a.shape for a in srcs])
        ssem, rsem, srcs, lands, token = _split_start("gw_start%d" % gi, srcs, lands, plan_a, 3 * len(srcs), behind)
        gather[gi] = (srcs, lands, ssem, rsem, plan_a, plan_b, n_b)
        return token

    def gather_forward(gi, behind):
        srcs, lands, ssem, rsem, plan_a, plan_b, n_b = gather[gi]
        ssem, rsem, lands, token = _split_forward("gw_fwd%d" % gi, srcs, lands, ssem, rsem, plan_a, plan_b, n_b, behind)
        gather[gi] = (lands, ssem, rsem, plan_b)
        return token

    def gathered(gi, behind):
        lands, ssem, rsem, plan_b = gather[gi]
        return _split_wait("gw_wait%d" % gi, [], lands, ssem, rsem, plan_b, behind)

    gather = {}
    (c_all8,) = _allgather8("gather_c", [c.reshape(8, (nb * D) // 8)])
    token = gather_start(0, c_all8)
    c_all = c_all8.reshape(NDEV * nb, D) + token[0, 0]
    b_sh = lax.dynamic_slice(b_ada, (0, j_chip * ADA_SH), (1, ADA_SH))
    ada_sh = _ada_fwd(c_all, w_ada[0], b_sh)
    later = [gather_operands(3), gather_operands(1), gather_operands(2)]
    (ada4,) = _chip_allgather("gather_ada", [ada_sh], behind=[a for pair in later for arrs in pair for a in arrs])
    token = gather_forward(0, ada4)
    plans = [_gather_plans([a.shape for a in srcs]) for srcs, _ in later]
    started, token = _split_start_groups(
        "gw_start_later", [(srcs, lands, pa, 3 * len(srcs)) for (srcs, lands), (pa, _, _) in zip(later, plans)], token)
    for gi, (ssem, rsem, srcs, lands), (pa, pb, n_b) in zip((3, 1, 2), started, plans):
        gather[gi] = (srcs, lands, ssem, rsem, pa, pb, n_b)
    ada_me = lax.dynamic_slice(ada4, (0, dev * nb, 0), (NCHIP, nb, ADA_SH))
    ada_me = jnp.transpose(ada_me, (1, 0, 2)).reshape(nb, NMOD * D)
    sh1, sc1, gt1, sh2, sc2, gt2, sh3, sc3, gt3 = [ada_me[:, k * D:(k + 1) * D].reshape(nb, 1, D) for k in range(NMOD)]

    wcat = jnp.transpose(w_spatial[0], (1, 0, 2)).reshape(CH, NH * CH)
    wcat_t = jnp.transpose(w_spatial[0], (0, 2, 1)).reshape(NH * CH, CH)
    bspt = jnp.repeat(b_spatial[0].T, HD, axis=1)

    (w1i,) = gathered(0, token)
    p1, act1 = _ffn_up(x, sh1, sc1, g_pre_f1, w1i)
    token = gather_forward(3, act1)
    token = gather_forward(1, token)
    (w1o,) = gathered(3, token)
    w1o = w1o.reshape(DFF, D)
    x1, f1 = _ffn_down(x, act1, gt1, g_post_f1, w1o)
    wmi, wmo, cw4 = gathered(1, x1)
    wmo = wmo.reshape(D, D)
    cw_full = jnp.transpose(cw4, (1, 0, 2)).reshape(CK, WB)
    cw_pad = jnp.pad(cw_full, ((0, HALO - CK), (0, 0)))
    u, v, a, g = _mix_in_fwd(x1, sh2, sc2, g_pre_m, wmi)
    token = gather_forward(2, u)
    x2, conv, yb, m = _mix_mid_fwd(x1, u, v, a, g, gt2 + token[0, 0], gmlp_norm_g, gmlp_norm_b, wcat, bspt, cw_pad, conv_b,
                                   conv_norm_g, conv_norm_b, g_out_a, g_out_b, wmo, g_post_m)
    w2i, w2o = gathered(2, [x2, token])
    w2o = w2o.reshape(DFF, D)
    dx3, df2, p2, lsum, dg_post_f2, dgt3 = _ffn_loss_fwd(x2, sh3, sc3, gt3, g_pre_f2, g_post_f2, w2i, w2o, loss_target)

    def chip4(pair, rows):
        return [arr.reshape(NCHIP, rows, arr.shape[-1]) for arr in pair]

    def scatter_start(tag, pairs, behind):
        srcs = [p[1] for p in pairs]
        lands = [lax.empty((3,) + a.shape[1:], a.dtype) for a in srcs]
        ssem, rsem, srcs, lands, token = _split_start("gs_start_" + tag, srcs, lands, _scatter_plan(len(srcs)),
                                                      3 * len(srcs), behind)
        return (srcs, lands, ssem, rsem), token

    def scatter_wait(tag, state, behind):
        srcs, lands, ssem, rsem = state
        return _split_wait("gs_wait_" + tag, srcs, lands, ssem, rsem, _scatter_plan(len(srcs)), behind)

    def allgather_start(tag, arrs, behind):
        lands = [lax.dynamic_update_index_in_dim(lax.empty((NDEV,) + a.shape, a.dtype), a, dev, 0) for a in arrs]
        ssem, rsem, srcs, lands, token = _split_start("small_start_" + tag, arrs, lands, _allgather_plan(len(arrs)),
                                                      7 * len(arrs), behind)
        return (srcs, lands, ssem, rsem), token

    def allgather_wait(tag, state, behind):
        srcs, lands, ssem, rsem = state
        return _split_wait("small_wait_" + tag, srcs, lands, ssem, rsem, _allgather_plan(len(srcs)), behind)

    out = {}
    dx2, dp2, h3, a2, dg_pre_f2, dsh3, dsc3 = _ffn_bwd(
        dx3, x2, None, p2, sh3, sc3, gt3, g_pre_f2, g_post_f2, w2i, w2o, df=df2)
    gw2i = _wgrad("wgrad_f2_in", h3.reshape(t, D), dp2.reshape(t, 2 * DFF), 2 * DFF // NCHIP, True)
    gw2o = chip4(_wgrad("wgrad_f2_out", a2.reshape(t, DFF), df2.reshape(t, D), D // 2, False), DFF // NCHIP)
    scat_f2, tok = scatter_start("f2", [gw2i, gw2o], dg_post_f2)
    dy, dm, dg_post_m, dgt2 = _mix_out_bwd(dx2, m, gt2 + tok[0, 0], g_post_m, wmo)
    gwmo = chip4(_wgrad("wgrad_mix_out", yb.reshape(t, D), dm.reshape(t, D), D // 2, False), D // NCHIP)
    (du, dv, dconv, dwcat, dbsp, dgn_g, dgn_b, dgo_a, dgo_b, dcn_g, dcn_b, dcb) = _mix_mid_bwd(
        dy, u, v, conv, gmlp_norm_g, gmlp_norm_b, wcat, wcat_t, bspt, conv_norm_g, conv_norm_b, g_out_a, g_out_b)
    dx1, dproj, h2, dg_pre_m, dsh2, dsc2, dcw = _mix_in_bwd(dx2, x1, du, dv, dconv, a, g, sh2, sc2, g_pre_m, wmi, cw_pad)
    gwmi = _wgrad("wgrad_mix_in", h2.reshape(t, D), dproj.reshape(t, 4 * WA), WA, True)

    vec_grads = dict(g_pre_m=dg_pre_m, g_post_m=dg_post_m, g_pre_f2=dg_pre_f2, g_post_f2=dg_post_f2)
    pair_grads = dict(gmlp_norm_g=dgn_g, gmlp_norm_b=dgn_b, conv_b=dcb, conv_norm_g=dcn_g, conv_norm_b=dcn_b,
                      g_out_a=dgo_a, g_out_b=dgo_b)
    pack = _pack_small([vec_grads[n] for n in VEC_ORDER[2:]], [pair_grads[n] for n in PAIR_ORDER], dbsp, dwcat, lsum)
    dada_early = jnp.concatenate([q.reshape(nb, D) for q in (dsh2, dsc2, dgt2, dsh3, dsc3, dgt3)], axis=1)
    small_early = [pack, dcw, dada_early.reshape(8, (nb * 6 * D) // 8)]
    mix_bf16 = [gwmi[1], gwmo[1]]
    (s_mix, s_early), tok2 = _split_start_groups("gs_start_mix_small", [
        (mix_bf16, [lax.empty((3,) + a.shape[1:], a.dtype) for a in mix_bf16], _scatter_plan(2), 6),
        (small_early, [lax.dynamic_update_index_in_dim(lax.empty((NDEV,) + a.shape, a.dtype), a, dev, 0) for a in small_early],
         _allgather_plan(3), 21)], dg_pre_m)
    scat_mix = (s_mix[2], s_mix[3], s_mix[0], s_mix[1])
    early = (s_early[2], s_early[3], s_early[0], s_early[1])
    grad_x, dp1, h1, a1, df1, dg_pre_f1, dg_post_f1, dsh1, dsc1, dgt1 = _ffn_bwd(
        dx1, x, f1, p1, sh1 + tok2[0, 0], sc1, gt1, g_pre_f1, g_post_f1, w1i, w1o)
    late_pack = _pack_late([dg_pre_f1, dg_post_f1] + [q.reshape(nb, D) for q in (dsh1, dsc1, dgt1)])
    late, tok2 = allgather_start("late", [late_pack], dg_post_f1)
    gw1i = _wgrad("wgrad_f1_in", h1.reshape(t, D), dp1.reshape(t, 2 * DFF), 2 * DFF // NCHIP, True)
    gw1o = chip4(_wgrad("wgrad_f1_out", a1.reshape(t, DFF), df1.reshape(t, D), D // 2, False), DFF // NCHIP)
    def d2d_start(tag, srcs, lands, plan, behind):
        ssem, rsem, srcs, lands, token = _split_start("d2d_start_" + tag, srcs, lands, plan, len(srcs), behind)
        return (srcs, lands, ssem, rsem, plan), token

    def d2d_wait(tag, state, behind):
        srcs, lands, ssem, rsem, plan = state
        return _split_wait("d2d_wait_" + tag, srcs, lands, ssem, rsem, plan, behind)

    def swap_start(tag, parts, behind):
        return d2d_start(tag, parts, [lax.empty(a.shape, a.dtype) for a in parts], _swap_plan(len(parts)), behind)

    def sums(names, pairs, recv):
        return _sum4("sum4_" + names[0][2:4], [p[0] for p in pairs], recv, j_arr)

    def update(names, part, other):
        for k, n in enumerate(names):
            out[n] = tuple(r[None] for r in _adam_big("adam_" + n, wts[n][0], mom[n][0], var[n][0], part[k], other[k]))

    c_arr = ac.reshape(1).astype(jnp.int32)
    halves = [gw1i[1], gw1o[1]]
    pair_st, tok = d2d_start("pair", halves, [lax.empty((a.shape[0], a.shape[1] // 2, a.shape[2]), a.dtype) for a in halves],
                             _pair_plan([a.shape for a in halves]), tok2)
    names_f2, names_mix, names_f1 = ("w_f2_in", "w_f2_out"), ("w_mix_in", "w_mix_out"), ("w_f1_in", "w_f1_out")
    part_f2 = sums(names_f2, [gw2i, gw2o], scatter_wait("f2", scat_f2, tok))
    sib = d2d_wait("pair", pair_st, part_f2)
    pair_i, pair_o = _pair_sum("pairsum_f1", [gw1i[0], gw1o[0]], sib, c_arr)
    scat_f1, tok = scatter_start("f1", [pair_i, pair_o], tok2)
    swap_f2, tok = swap_start("swap_f2", part_f2, tok)
    part_mix = sums(names_mix, [gwmi, gwmo], scatter_wait("mix", scat_mix, tok))
    swap_mix, tok = swap_start("swap_mix", part_mix, part_mix[1])

    pack_all, dcw_all, dada_early8 = allgather_wait("early", early, tok)
    (late_all,) = allgather_wait("late", late, pack_all)
    dada_late = jnp.transpose(late_all[:, 2:8, :].reshape(NDEV, 3, nb, D), (0, 2, 1, 3)).reshape(NDEV * nb, 3 * D)
    dada_all = jnp.concatenate([dada_late, dada_early8.reshape(NDEV * nb, 6 * D)], axis=1)
    dada_sh = lax.dynamic_slice(dada_all, (0, j_chip * ADA_SH), (NDEV * nb, ADA_SH))
    out["w_ada"] = tuple(r[None] for r in _ada_bwd_adam(c_all, dada_sh, w_ada[0], m_w_ada[0], v_w_ada[0]))
    update(names_f2, part_f2, d2d_wait("swap_f2", swap_f2, out["w_ada"][3]))
    update(names_mix, part_mix, d2d_wait("swap_mix", swap_mix, out["w_f2_out"][3]))

    mine = sums(names_f1, [pair_i, pair_o], scatter_wait("f1", scat_f1, out["w_mix_out"][3]))
    swap_f1, tok = swap_start("swap_f1", mine, mine[1])
    dcw_mine = lax.dynamic_slice(dcw_all, (0, 0, j_chip * (WB // NCHIP)), (NDEV, HALO, WB // NCHIP))
    small = {n: (wts[n], mom[n], var[n]) for n in list(VEC_ORDER) + list(PAIR_ORDER) + ["b_spatial", "w_spatial", "conv_w", "b_ada"]}
    small_out, loss = _small_adam(pack_all, late_all, dcw_mine, dada_all, small, tok)
    out.update(small_out)
    theirs = d2d_wait("swap_f1", swap_f1, out["b_ada"][3])
    for k, n in enumerate(names_f1):
        out[n] = tuple(r[None] for r in _adam_halves("adam_" + n, wts[n][0], mom[n][0], var[n][0], mine[k], theirs[k],
                                                     c_arr))

    res = [loss, grad_x]
    for k in range(4):
        res += [out[n][k] for n in WEIGHTS]
    return tuple(res)
```

```python
import jax
import jax.numpy as jnp
from jax import lax
from jax.experimental import pallas as pl
from jax.experimental.pallas import tpu as pltpu

D = 1024
DFF = 2816
WA = 512
WB = 512
NH = 8
HD = 64
CH = 128
CK = 31
HALO = 32
NMOD = 9
EPS = 1e-6
NCHIP = 4
NDEV = 8
FBLK = DFF // 2
ADA_SH = NMOD * D // NCHIP

LR, B1, B2, EPS_A, WD, STEP = 0.001, 0.9, 0.999, 1e-08, 0.01, 10

F32 = jnp.float32
BF16 = jnp.bfloat16
MESH = pl.DeviceIdType.MESH
ANY = pl.BlockSpec(memory_space=pl.ANY)
VMEM_FULL = pl.BlockSpec(memory_space=pltpu.VMEM)
VMEM_LIMIT = 56 * 1024 * 1024
WGRAD_VMEM_BUDGET = 52 * 1024 * 1024
TM = 512
TM_FFN_BWD = 256

NT = (((1,), (1,)), ((), ()))
TN = (((0,), (0,)), ((), ()))


def _dot(a, b):
    return jnp.dot(a, b, preferred_element_type=F32)


def _dot_nt(a, b):
    return lax.dot_general(a, b, NT, preferred_element_type=F32)


def _dot_tn(a, b):
    return lax.dot_general(a, b, TN, preferred_element_type=F32)


def _cparams():
    return pltpu.CompilerParams(vmem_limit_bytes=VMEM_LIMIT)


def _chip_relations(x, y):
    return [(1 - x, y), (x, 1 - y), (1 - x, 1 - y)]


def _exchange(name, arrs, out_shapes, plan):
    n = len(arrs)
    n_out = len(out_shapes)

    def body(*refs):
        ins, outs = refs[:n], refs[n:n + n_out]
        send_sems, recv_sems, local_sems = refs[n + n_out:]
        x, y, c = lax.axis_index("x"), lax.axis_index("y"), lax.axis_index("c")
        local, sends = plan(x, y, c, ins, outs)
        locs = [pltpu.make_async_copy(s, d, local_sems.at[i]) for i, (s, d) in enumerate(local)]
        for loc in locs:
            loc.start()
        cps = [pltpu.make_async_remote_copy(src_ref=s, dst_ref=d, send_sem=send_sems.at[i], recv_sem=recv_sems.at[i],
                                            device_id=peer, device_id_type=MESH)
               for i, (s, d, peer, _) in enumerate(sends)]
        for cp in cps:
            cp.start()
        for i, (s, _, peer, landing) in enumerate(sends):
            pltpu.make_async_remote_copy(src_ref=s, dst_ref=landing, send_sem=send_sems.at[i], recv_sem=recv_sems.at[i],
                                         device_id=peer, device_id_type=MESH).wait_recv()
        for cp in cps:
            cp.wait_send()
        for loc in locs:
            loc.wait()

    return n, n_out, body


def _run_exchange(name, arrs, out_shapes, plan, n_local, n_send):
    n, n_out, body = _exchange(name, arrs, out_shapes, plan)
    return pl.pallas_call(
        body, name=name, out_shape=out_shapes,
        in_specs=[ANY] * n, out_specs=[ANY] * n_out,
        scratch_shapes=[pltpu.SemaphoreType.DMA((n_send,)), pltpu.SemaphoreType.DMA((n_send,)),
                        pltpu.SemaphoreType.DMA((max(n_local, 1),))],
    )(*arrs)


def _chip_allgather(name, arrs, behind=()):
    n = len(arrs)

    def plan(x, y, c, ins, outs):
        j_me = 2 * x + y
        local = [(ins[a], outs[a].at[j_me]) for a in range(n)]
        sends = []
        for a in range(n):
            for (px, py) in _chip_relations(x, y):
                sends.append((ins[a], outs[a].at[j_me], (px, py, c), outs[a].at[2 * px + py]))
        return local, sends

    shapes = [jax.ShapeDtypeStruct((NCHIP,) + a.shape, a.dtype) for a in arrs]
    return _run_exchange(name, list(arrs) + list(behind), shapes, plan, n, 3 * n)


HBM = pl.BlockSpec(memory_space=pltpu.HBM)
SEM = pl.BlockSpec(memory_space=pltpu.SEMAPHORE)
EFFECT = pltpu.SideEffectType.DATAFLOW_SIDE_EFFECTING


def _split_start_groups(name, groups, after):
    n_src = [len(g[0]) for g in groups]
    n_land = [len(g[1]) for g in groups]
    all_srcs = [pltpu.with_memory_space_constraint(a, pltpu.HBM) for g in groups for a in g[0]]
    all_lands = [pltpu.with_memory_space_constraint(a, pltpu.HBM) for g in groups for a in g[1]]
    ns, nl, ng = len(all_srcs), len(all_lands), len(groups)

    def body(*refs):
        src_refs, land_refs = refs[:ns], refs[ns:ns + nl]
        sem_refs = refs[ns + nl + 1:ns + nl + 1 + 2 * ng]
        token = refs[-1]
        x, y, c = lax.axis_index("x"), lax.axis_index("y"), lax.axis_index("c")
        at_src = at_land = 0
        for gi, (_, _, plan, _) in enumerate(groups):
            _, sends = plan(x, y, c, src_refs[at_src:at_src + n_src[gi]], land_refs[at_land:at_land + n_land[gi]])
            for i, (s, d, peer, _) in enumerate(sends):
                pltpu.make_async_remote_copy(src_ref=s, dst_ref=d, send_sem=sem_refs[2 * gi].at[i],
                                             recv_sem=sem_refs[2 * gi + 1].at[i], device_id=peer, device_id_type=MESH).start()
            at_src += n_src[gi]
            at_land += n_land[gi]
        token[...] = jnp.zeros_like(token)

    sems = [pltpu.SemaphoreType.DMA((g[3],)) for g in groups for _ in range(2)]
    res = pl.pallas_call(
        body, name=name,
        out_shape=(*sems, *[pltpu.HBM(a.shape, a.dtype) for a in all_lands], jax.ShapeDtypeStruct((8, 128), F32)),
        in_specs=[HBM] * (ns + nl) + [ANY],
        out_specs=(*([SEM] * (2 * ng)), *([HBM] * nl), pl.BlockSpec(memory_space=pltpu.VMEM)),
        input_output_aliases={ns + i: 2 * ng + i for i in range(nl)},
        compiler_params=pltpu.CompilerParams(has_side_effects=EFFECT),
    )(*all_srcs, *all_lands, after)
    out, at_src, at_land = [], 0, 2 * ng
    for gi in range(ng):
        out.append((res[2 * gi], res[2 * gi + 1], all_srcs[at_src:at_src + n_src[gi]],
                    list(res[at_land:at_land + n_land[gi]])))
        at_src += n_src[gi]
        at_land += n_land[gi]
    return out, res[-1]


def _split_start(name, srcs, lands, plan, n_send, after):
    (group,), token = _split_start_groups(name, [(srcs, lands, plan, n_send)], after)
    return (*group, token)


def _split_wait(name, srcs, lands, send_sems, recv_sems, plan, after):
    n, nl = len(srcs), len(lands)
    afters = list(after) if isinstance(after, (list, tuple)) else [after]

    def body(*refs):
        src, land = refs[:n], refs[n:n + nl]
        send_sems, recv_sems = refs[n + nl], refs[n + nl + 1]
        x, y, c = lax.axis_index("x"), lax.axis_index("y"), lax.axis_index("c")
        _, sends = plan(x, y, c, src, land)
        for i, (s, _, peer, landing) in enumerate(sends):
            cp = pltpu.make_async_remote_copy(src_ref=s, dst_ref=landing, send_sem=send_sems.at[i],
                                              recv_sem=recv_sems.at[i], device_id=peer, device_id_type=MESH)
            cp.wait_send()
            cp.wait_recv()

    thru = [pltpu.HBM(a.shape, a.dtype) for a in lands]
    res = pl.pallas_call(
        body, name=name, out_shape=tuple(thru),
        in_specs=[HBM] * (n + nl) + [SEM, SEM] + [ANY] * len(afters), out_specs=tuple([HBM] * nl),
        input_output_aliases={n + i: i for i in range(nl)},
        compiler_params=pltpu.CompilerParams(has_side_effects=EFFECT),
    )(*srcs, *lands, send_sems, recv_sems, *afters)
    return list(res)


def _split_forward(name, srcs, lands, send_a, recv_a, plan_a, plan_b, n_b, after):
    n, nl = len(srcs), len(lands)

    def body(*refs):
        src, land = refs[:n], refs[n:n + nl]
        send_a, recv_a = refs[n + nl], refs[n + nl + 1]
        send_b, recv_b = refs[n + nl + 3], refs[n + nl + 4]
        token = refs[-1]
        x, y, c = lax.axis_index("x"), lax.axis_index("y"), lax.axis_index("c")
        _, first = plan_a(x, y, c, src, land)
        for i, (s, _, peer, landing) in enumerate(first):
            cp = pltpu.make_async_remote_copy(src_ref=s, dst_ref=landing, send_sem=send_a.at[i],
                                              recv_sem=recv_a.at[i], device_id=peer, device_id_type=MESH)
            cp.wait_send()
            cp.wait_recv()
        _, second = plan_b(x, y, c, src, land)
        for i, (s, d, peer, _) in enumerate(second):
            pltpu.make_async_remote_copy(src_ref=s, dst_ref=d, send_sem=send_b.at[i], recv_sem=recv_b.at[i],
                                         device_id=peer, device_id_type=MESH).start()
        token[...] = jnp.zeros_like(token)

    thru = [pltpu.HBM(a.shape, a.dtype) for a in lands]
    res = pl.pallas_call(
        body, name=name,
        out_shape=(pltpu.SemaphoreType.DMA((n_b,)), pltpu.SemaphoreType.DMA((n_b,)), *thru,
                   jax.ShapeDtypeStruct((8, 128), F32)),
        in_specs=[HBM] * (n + nl) + [SEM, SEM, ANY],
        out_specs=(SEM, SEM, *([HBM] * nl), pl.BlockSpec(memory_space=pltpu.VMEM)),
        input_output_aliases={n + i: 2 + i for i in range(nl)},
        compiler_params=pltpu.CompilerParams(has_side_effects=EFFECT),
    )(*srcs, *lands, send_a, recv_a, after)
    return res[0], res[1], list(res[2:2 + nl]), res[-1]


def _gather_plans(shapes):
    n = len(shapes)

    def halves(a, c):
        rows = shapes[a][0] // 2
        return pl.ds(pl.multiple_of(c * rows, 16), rows), pl.ds(pl.multiple_of((1 - c) * rows, 16), rows)

    def split(a):
        return shapes[a][0] % 32 == 0

    def plan_a(x, y, c, src, land):
        j_me = 2 * x + y
        sends = []
        for a in range(n):
            for (px, py) in _chip_relations(x, y):
                if split(a):
                    mine, _ = halves(a, c)
                    sends.append((src[a].at[mine], land[a].at[j_me, mine], (px, py, c), land[a].at[2 * px + py, mine]))
                else:
                    sends.append((src[a], land[a].at[j_me], (px, py, c), land[a].at[2 * px + py]))
        return [], sends

    def plan_b(x, y, c, src, land):
        sends = []
        for a in range(n):
            if split(a):
                mine, other = halves(a, c)
                for (px, py) in _chip_relations(x, y):
                    j = 2 * px + py
                    sends.append((land[a].at[j, mine], land[a].at[j, mine], (x, y, 1 - c), land[a].at[j, other]))
        return [], sends

    n_b = 3 * sum(1 for a in range(n) if split(a))
    return plan_a, plan_b, n_b


def _allgather_plan(n):
    flips = [(dx, dy, dc) for dx in (0, 1) for dy in (0, 1) for dc in (0, 1) if dx or dy or dc]

    def plan(x, y, c, src, land):
        sends = []
        for a in range(n):
            for dx, dy, dc in flips:
                px, py, pc = x ^ dx, y ^ dy, c ^ dc
                sends.append((src[a], land[a].at[4 * x + 2 * y + c], (px, py, pc), land[a].at[4 * px + 2 * py + pc]))
        return [], sends

    return plan


def _scatter_plan(n):
    def plan(x, y, c, src, land):
        sends = []
        for a in range(n):
            for k, (px, py) in enumerate(_chip_relations(x, y)):
                sends.append((src[a].at[2 * px + py], land[a].at[k], (px, py, c), land[a].at[k]))
        return [], sends

    return plan


def _rms(x):
    r = lax.rsqrt(jnp.mean(x * x, axis=-1, keepdims=True) + EPS)
    return x * r, r


def _rms_bwd(dy, n, r, g):
    dg = jnp.sum(dy * n, axis=0, keepdims=True)
    dn = dy * g
    dx = r * (dn - n * jnp.mean(dn * n, axis=-1, keepdims=True))
    return dx, dg


def _ln(x):
    mu = jnp.mean(x, axis=-1, keepdims=True)
    xc = x - mu
    rstd = lax.rsqrt(jnp.mean(xc * xc, axis=-1, keepdims=True) + EPS)
    return xc * rstd, rstd


def _ln_bwd(dy, xhat, rstd, g):
    dg = jnp.sum(dy * xhat, axis=0, keepdims=True)
    db = jnp.sum(dy, axis=0, keepdims=True)
    dxh = dy * g
    dx = rstd * (dxh - jnp.mean(dxh, axis=-1, keepdims=True) - xhat * jnp.mean(dxh * xhat, axis=-1, keepdims=True))
    return dx, dg, db


def _sigmoid(x):
    return jax.nn.sigmoid(x)


def _dsilu(x, s):
    return s * (1.0 + x * (1.0 - s))


def _adam(w, g, m, v):
    m = B1 * m + (1.0 - B1) * g
    v = B2 * v + (1.0 - B2) * (g * g)
    m_hat = m / (1.0 - B1 ** STEP)
    v_hat = v / (1.0 - B2 ** STEP)
    delta = -LR * (m_hat / (jnp.sqrt(v_hat) + EPS_A) + WD * w)
    return delta, m, v


def _head_mask(shape):
    lane = lax.broadcasted_iota(jnp.int32, shape, len(shape) - 1)
    return [(lane >= h * HD) & (lane < (h + 1) * HD) for h in range(NH)]


def _first(b, i):
    return jnp.logical_and(b == 0, i == 0)


def _acc(ref, val, first):
    @pl.when(first)
    def _():
        ref[...] = val

    @pl.when(jnp.logical_not(first))
    def _():
        ref[...] += val


def _ada_fwd(c_all, w_sh, b_sh):
    nb = c_all.shape[0]
    tn = 768

    def body(c_ref, w_ref, b_ref, o_ref):
        cv = c_ref[...]
        cs = (cv * _sigmoid(cv)).astype(BF16)
        o_ref[...] = _dot(cs, w_ref[...].astype(BF16)) + b_ref[...]

    return pl.pallas_call(
        body, name="ada_fwd", grid=(ADA_SH // tn,),
        out_shape=jax.ShapeDtypeStruct((nb, ADA_SH), F32),
        in_specs=[pl.BlockSpec((nb, D), lambda j: (0, 0)), pl.BlockSpec((D, tn), lambda j: (0, j)),
                  pl.BlockSpec((1, tn), lambda j: (0, j))],
        out_specs=pl.BlockSpec((nb, tn), lambda j: (0, j)),
        compiler_params=_cparams(),
    )(c_all, w_sh, b_sh)


def _ada_bwd_adam(c_all, dada_sh, w, m, v):
    nb = c_all.shape[0]
    tn = 768

    def body(c_ref, d_ref, w_ref, m_ref, v_ref, g_out, d_out, m_out, v_out):
        cv = c_ref[...]
        cs = (cv * _sigmoid(cv)).astype(BF16)
        g = _dot_tn(cs, d_ref[...].astype(BF16))
        delta, m2, v2 = _adam(w_ref[...], g, m_ref[...], v_ref[...])
        g_out[...] = g
        d_out[...] = delta
        m_out[...] = m2
        v_out[...] = v2

    big = pl.BlockSpec((D, tn), lambda j: (0, j))
    shape = jax.ShapeDtypeStruct((D, ADA_SH), F32)
    return pl.pallas_call(
        body, name="ada_bwd_adam", grid=(ADA_SH // tn,),
        out_shape=[shape] * 4,
        in_specs=[pl.BlockSpec((nb, D), lambda j: (0, 0)), pl.BlockSpec((nb, tn), lambda j: (0, j)), big, big, big],
        out_specs=[big] * 4,
        compiler_params=_cparams(),
    )(c_all, dada_sh, w, m, v)


def _tok_specs(tm, width):
    return pl.BlockSpec((1, tm, width), lambda b, i: (b, i, 0))


def _mod_spec():
    return pl.BlockSpec((1, 1, D), lambda b, i: (b, 0, 0))


def _row_spec(width=D):
    return pl.BlockSpec((1, width), lambda b, i: (0, 0))


def _ffn_loss_fwd(x, sh, sc, gt, g_pre, g_post, w_in4, w_out, target):
    nb, s, _ = x.shape
    tm = min(TM, s)

    def body(x_ref, sh_ref, sc_ref, gt_ref, gpre_ref, gpost_ref, win_ref, wout_ref, tgt_ref,
             xo_ref, df_ref, p_ref, ls_ref, dgpost_ref, dgt_ref):
        xv = x_ref[0]
        n, _ = _rms(xv)
        h = (n * gpre_ref[...]) * (1.0 + sc_ref[0]) + sh_ref[0]
        hb = h.astype(BF16)
        acc = jnp.zeros((tm, D), F32)
        for j in range(2):
            gate = _dot(hb, win_ref[j])
            up = _dot(hb, win_ref[2 + j])
            p_ref[0, :, j * FBLK:(j + 1) * FBLK] = gate.astype(BF16)
            p_ref[0, :, DFF + j * FBLK:DFF + (j + 1) * FBLK] = up.astype(BF16)
            a = (gate * _sigmoid(gate)) * up
            acc = acc + _dot(a.astype(BF16), wout_ref[j * FBLK:(j + 1) * FBLK, :])
        nf, q = _rms(acc)
        gpost = gpost_ref[...]
        half_gate = 0.5 * gt_ref[0]
        out = xv + half_gate * (nf * gpost)
        first = _first(pl.program_id(0), pl.program_id(1))
        err = out - tgt_ref[0]
        dout = err * (1.0 / D)
        xo_ref[0] = dout
        row = jnp.sum(err * err, axis=0, keepdims=True)
        part = row[:, 0:128]
        for k in range(1, D // 128):
            part = part + row[:, k * 128:(k + 1) * 128]
        _acc(ls_ref, part, first)
        df, dgpost = _rms_bwd(dout * half_gate, nf, q, gpost)
        df_ref[0] = df.astype(BF16)
        _acc(dgpost_ref, dgpost, first)
        _acc(dgt_ref, jnp.sum(dout * (0.5 * (nf * gpost)), axis=0, keepdims=True)[None], pl.program_id(1) == 0)

    tok = _tok_specs(tm, D)
    return pl.pallas_call(
        body, name="ffn_loss_fwd", grid=(nb, s // tm),
        out_shape=[jax.ShapeDtypeStruct((nb, s, D), F32), jax.ShapeDtypeStruct((nb, s, D), BF16),
                   jax.ShapeDtypeStruct((nb, s, 2 * DFF), BF16), jax.ShapeDtypeStruct((1, 128), F32),
                   jax.ShapeDtypeStruct((1, D), F32), jax.ShapeDtypeStruct((nb, 1, D), F32)],
        in_specs=[tok, _mod_spec(), _mod_spec(), _mod_spec(), _row_spec(), _row_spec(), VMEM_FULL, VMEM_FULL, tok],
        out_specs=[tok, tok, _tok_specs(tm, 2 * DFF), pl.BlockSpec((1, 128), lambda b, i: (0, 0)), _row_spec(), _mod_spec()],
        compiler_params=_cparams(),
    )(x, sh, sc, gt, g_pre, g_post, w_in4, w_out, target)


def _ffn_up(x, sh, sc, g_pre, w_in4):
    nb, s, _ = x.shape
    tm = min(TM, s)

    def body(x_ref, sh_ref, sc_ref, gpre_ref, win_ref, p_ref, a_ref):
        n, _ = _rms(x_ref[0])
        hb = ((n * gpre_ref[...]) * (1.0 + sc_ref[0]) + sh_ref[0]).astype(BF16)
        for j in range(2):
            gate = _dot(hb, win_ref[j])
            up = _dot(hb, win_ref[2 + j])
            p_ref[0, :, j * FBLK:(j + 1) * FBLK] = gate.astype(BF16)
            p_ref[0, :, DFF + j * FBLK:DFF + (j + 1) * FBLK] = up.astype(BF16)
            a_ref[0, :, j * FBLK:(j + 1) * FBLK] = ((gate * _sigmoid(gate)) * up).astype(BF16)

    return pl.pallas_call(
        body, name="ffn_up", grid=(nb, s // tm),
        out_shape=[jax.ShapeDtypeStruct((nb, s, 2 * DFF), BF16), jax.ShapeDtypeStruct((nb, s, DFF), BF16)],
        in_specs=[_tok_specs(tm, D), _mod_spec(), _mod_spec(), _row_spec(), VMEM_FULL],
        out_specs=[_tok_specs(tm, 2 * DFF), _tok_specs(tm, DFF)],
        compiler_params=_cparams(),
    )(x, sh, sc, g_pre, w_in4)


def _ffn_down(x, a, gt, g_post, w_out):
    nb, s, _ = x.shape
    tm = min(TM, s)

    def body(x_ref, a_ref, gt_ref, gpost_ref, wout_ref, xo_ref, f_ref):
        acc = _dot(a_ref[0], wout_ref[...])
        f_ref[0] = acc
        nf, _ = _rms(acc)
        xo_ref[0] = x_ref[0] + (0.5 * gt_ref[0]) * (nf * gpost_ref[...])

    tok = _tok_specs(tm, D)
    shape = jax.ShapeDtypeStruct((nb, s, D), F32)
    return pl.pallas_call(
        body, name="ffn_down", grid=(nb, s // tm), out_shape=[shape, shape],
        in_specs=[tok, _tok_specs(tm, DFF), _mod_spec(), _row_spec(), VMEM_FULL],
        out_specs=[tok, tok],
        compiler_params=_cparams(),
    )(x, a, gt, g_post, w_out)


def _ffn_bwd(dxo, x, f, p, sh, sc, gt, g_pre, g_post, w_in4, w_out, df=None):
    nb, s, _ = x.shape
    tm = min(TM_FFN_BWD, s)
    given = df is not None

    def body(*refs):
        if given:
            (dxo_ref, x_ref, dfin_ref, p_ref, sh_ref, sc_ref, gpre_ref, win_ref, wout_ref,
             dx_ref, dp_ref, h_ref, a_ref, dgpre_ref, dsh_ref, dsc_ref) = refs
        else:
            (dxo_ref, x_ref, f_ref, p_ref, sh_ref, sc_ref, gt_ref, gpre_ref, gpost_ref, win_ref, wout_ref,
             dx_ref, dp_ref, h_ref, a_ref, df_ref, dgpre_ref, dgpost_ref, dsh_ref, dsc_ref, dgt_ref) = refs
        b, i = pl.program_id(0), pl.program_id(1)
        dxo_v = dxo_ref[0]
        if given:
            dfb = dfin_ref[0]
        else:
            nf, q = _rms(f_ref[0])
            gpost = gpost_ref[...]
            dgt = jnp.sum(dxo_v * (0.5 * (nf * gpost)), axis=0, keepdims=True)
            do = dxo_v * (0.5 * gt_ref[0])
            dfv, dgpost = _rms_bwd(do, nf, q, gpost)
            dfb = dfv.astype(BF16)
            df_ref[0] = dfb
        xv = x_ref[0]
        n, r = _rms(xv)
        gpre = gpre_ref[...]
        ng = n * gpre
        scale1 = 1.0 + sc_ref[0]
        h = ng * scale1 + sh_ref[0]
        h_ref[0] = h.astype(BF16)
        dh = jnp.zeros((tm, D), F32)
        for j in range(2):
            gate = p_ref[0, :, j * FBLK:(j + 1) * FBLK].astype(F32)
            up = p_ref[0, :, DFF + j * FBLK:DFF + (j + 1) * FBLK].astype(F32)
            sg = _sigmoid(gate)
            act = gate * sg
            a_ref[0, :, j * FBLK:(j + 1) * FBLK] = (act * up).astype(BF16)
            da = _dot_nt(dfb, wout_ref[j * FBLK:(j + 1) * FBLK, :])
            dgate = (da * up * _dsilu(gate, sg)).astype(BF16)
            dup = (da * act).astype(BF16)
            dp_ref[0, :, j * FBLK:(j + 1) * FBLK] = dgate
            dp_ref[0, :, DFF + j * FBLK:DFF + (j + 1) * FBLK] = dup
            dh = dh + _dot_nt(dgate, win_ref[j]) + _dot_nt(dup, win_ref[2 + j])
        dsh = jnp.sum(dh, axis=0, keepdims=True)
        dsc = jnp.sum(dh * ng, axis=0, keepdims=True)
        dxn, dgpre = _rms_bwd(dh * scale1, n, r, gpre)
        dx_ref[0] = dxo_v + dxn
        _acc(dgpre_ref, dgpre, _first(b, i))
        _acc(dsh_ref, dsh[None], i == 0)
        _acc(dsc_ref, dsc[None], i == 0)
        if not given:
            _acc(dgpost_ref, dgpost, _first(b, i))
            _acc(dgt_ref, dgt[None], i == 0)

    tok = _tok_specs(tm, D)
    mod_shape = jax.ShapeDtypeStruct((nb, 1, D), F32)
    row_shape = jax.ShapeDtypeStruct((1, D), F32)
    big = [jax.ShapeDtypeStruct((nb, s, D), F32), jax.ShapeDtypeStruct((nb, s, 2 * DFF), BF16),
           jax.ShapeDtypeStruct((nb, s, D), BF16), jax.ShapeDtypeStruct((nb, s, DFF), BF16)]
    big_specs = [tok, _tok_specs(tm, 2 * DFF), tok, _tok_specs(tm, DFF)]
    if given:
        return pl.pallas_call(
            body, name="ffn_bwd_after_loss", grid=(nb, s // tm),
            out_shape=big + [row_shape, mod_shape, mod_shape],
            in_specs=[tok, tok, tok, _tok_specs(tm, 2 * DFF), _mod_spec(), _mod_spec(), _row_spec(), VMEM_FULL, VMEM_FULL],
            out_specs=big_specs + [_row_spec(), _mod_spec(), _mod_spec()],
            compiler_params=_cparams(),
        )(dxo, x, df, p, sh, sc, g_pre, w_in4, w_out)
    return pl.pallas_call(
        body, name="ffn_bwd", grid=(nb, s // tm),
        out_shape=big + [jax.ShapeDtypeStruct((nb, s, D), BF16), row_shape, row_shape, mod_shape, mod_shape, mod_shape],
        in_specs=[tok, tok, tok, _tok_specs(tm, 2 * DFF), _mod_spec(), _mod_spec(), _mod_spec(), _row_spec(), _row_spec(),
                  VMEM_FULL, VMEM_FULL],
        out_specs=big_specs + [tok, _row_spec(), _row_spec(), _mod_spec(), _mod_spec(), _mod_spec()],
        compiler_params=_cparams(),
    )(dxo, x, f, p, sh, sc, gt, g_pre, g_post, w_in4, w_out)


def _wgrad(name, a, b, col_block, chip_major):
    t, ka = a.shape
    n = b.shape[1]
    def vmem_bytes(rows):
        return 2 * 2 * rows * (ka + col_block) + 4 * ka * col_block + 2 * (4 + 2) * ka * col_block

    tk = min(t, 512)
    while tk * 2 <= t and t % (tk * 2) == 0 and vmem_bytes(tk * 2) <= WGRAD_VMEM_BUDGET:
        tk *= 2
    nk = t // tk
    nblk = n // col_block

    def body(a_ref, b_ref, o_ref, obf_ref, acc_ref):
        k = pl.program_id(1)

        @pl.when(k == 0)
        def _():
            acc_ref[...] = jnp.zeros_like(acc_ref)

        acc_ref[...] += _dot_tn(a_ref[...], b_ref[...])

        @pl.when(k == nk - 1)
        def _():
            val = acc_ref[...]
            if chip_major:
                o_ref[0] = val
                obf_ref[0] = val.astype(BF16)
            else:
                o_ref[...] = val
                obf_ref[...] = val.astype(BF16)

    if chip_major:
        shape = (nblk, ka, col_block)
        ospec = pl.BlockSpec((1, ka, col_block), lambda j, k: (j, 0, 0))
    else:
        shape = (ka, n)
        ospec = pl.BlockSpec((ka, col_block), lambda j, k: (0, j))
    return pl.pallas_call(
        body, name=name, grid=(nblk, nk),
        out_shape=[jax.ShapeDtypeStruct(shape, F32), jax.ShapeDtypeStruct(shape, BF16)],
        in_specs=[pl.BlockSpec((tk, ka), lambda j, k: (k, 0)), pl.BlockSpec((tk, col_block), lambda j, k: (k, j))],
        out_specs=[ospec, ospec],
        scratch_shapes=[pltpu.VMEM((ka, col_block), F32)],
        compiler_params=_cparams(),
    )(a, b)


def _mix_in_fwd(x, sh, sc, g_pre, w_mi4):
    nb, s, _ = x.shape
    tm = min(TM, s)

    def body(x_ref, sh_ref, sc_ref, gpre_ref, w_ref, u_ref, v_ref, a_ref, g_ref):
        n, _ = _rms(x_ref[0])
        hb = ((n * gpre_ref[...]) * (1.0 + sc_ref[0]) + sh_ref[0]).astype(BF16)
        for k, o_ref in enumerate((u_ref, v_ref, a_ref, g_ref)):
            o_ref[0] = _dot(hb, w_ref[k])

    shape = jax.ShapeDtypeStruct((nb, s, WA), F32)
    return pl.pallas_call(
        body, name="mix_in_fwd", grid=(nb, s // tm),
        out_shape=[shape] * 4,
        in_specs=[_tok_specs(tm, D), _mod_spec(), _mod_spec(), _row_spec(), VMEM_FULL],
        out_specs=[_tok_specs(tm, WA)] * 4,
        compiler_params=_cparams(),
    )(x, sh, sc, g_pre, w_mi4)


def _spatial_weights(wcat_ref, transposed):
    w = wcat_ref[...]
    row = lax.broadcasted_iota(jnp.int32, w.shape, 0)
    col = lax.broadcasted_iota(jnp.int32, w.shape, 1)
    keep = ((row & (CH - 1)) <= col) if transposed else ((col & (CH - 1)) <= row)
    return jnp.where(keep, w, 0.0).astype(BF16)


def _expand_heads(vc, masks):
    return jnp.concatenate([jnp.where(mk, vc, jnp.zeros_like(vc)) for mk in masks], axis=0)


def _spatial_bias(bspt_ref):
    return bspt_ref[...]


SHIFTS = 8
TAP_ROWS = 32


def _ext_rows(tm):
    return tm + HALO + SHIFTS


def _make_shifts(ext_ref, sh_ref, tm):
    ext_ref[tm + HALO:tm + HALO + SHIFTS, :] = jnp.zeros((SHIFTS, WB), F32)
    for r in range(SHIFTS):
        sh_ref[r] = ext_ref[r:r + tm + HALO, :]


def _conv_taps(sh_ref, w_ref, tm, taps, emit):
    def block(i, carry):
        r0 = pl.multiple_of(i * TAP_ROWS, TAP_ROWS)
        acc = jnp.zeros((TAP_ROWS, WB), F32)
        for o, k in taps:
            acc = acc + w_ref[k:k + 1, :] * sh_ref[o % SHIFTS, pl.ds(r0 + SHIFTS * (o // SHIFTS), TAP_ROWS), :]
        emit(r0, acc)
        return carry

    lax.fori_loop(0, tm // TAP_ROWS, block, 0)


def _halo_prev_spec(tm):
    return pl.BlockSpec((1, HALO, WB), lambda b, i: (b, jnp.maximum(i * (tm // HALO) - 1, 0), 0))


def _halo_next_spec(tm, s):
    return pl.BlockSpec((1, HALO, WB), lambda b, i: (b, jnp.minimum((i + 1) * (tm // HALO), s // HALO - 1), 0))


def _mix_mid_fwd(x, u, v, a, g, gt, gn_g, gn_b, wcat, bspt, conv_w, conv_b, cn_g, cn_b, go_a, go_b, w_mo, g_post):
    nb, s, _ = x.shape
    tm = min(TM, s)

    def body(x_ref, u_ref, v_ref, a_ref, g_ref, ah_ref, gh_ref, gt_ref, gng_ref, gnb_ref, wcat_ref, bspt_ref,
             cw_ref, cb_ref, cng_ref, cnb_ref, goa_ref, gob_ref, wmo_ref, gpost_ref,
             xo_ref, conv_ref, y_ref, m_ref, ext_ref, sh_ref):
        i = pl.program_id(1)
        xhat, _ = _ln(v_ref[0])
        vb = (xhat * gng_ref[...] + gnb_ref[...]).astype(BF16)
        wsb = _spatial_weights(wcat_ref, False)
        bias = _spatial_bias(bspt_ref)
        masks = _head_mask((CH, WA))
        zs = []
        for cidx in range(tm // CH):
            vexp = _expand_heads(vb[cidx * CH:(cidx + 1) * CH, :], masks)
            zs.append(_dot(wsb, vexp) + bias)
        z = jnp.concatenate(zs, axis=0)
        na, _ = _rms(u_ref[0] * z)
        keep = jnp.where(i == 0, 0.0, 1.0).astype(F32)
        ext_ref[0:HALO, :] = (ah_ref[0] * _sigmoid(gh_ref[0])) * keep
        ext_ref[HALO:HALO + tm, :] = a_ref[0] * _sigmoid(g_ref[0])
        _make_shifts(ext_ref, sh_ref, tm)
        cb = cb_ref[...]

        def put_conv(r0, acc):
            conv_ref[0, pl.ds(r0, TAP_ROWS), :] = acc + cb

        _conv_taps(sh_ref, cw_ref, tm, [(k + HALO - (CK - 1), k) for k in range(CK)], put_conv)
        conv = conv_ref[0]
        chat, _ = _ln(conv)
        cln = chat * cng_ref[...] + cnb_ref[...]
        nbb, _ = _rms(cln * _sigmoid(cln))
        yb = jnp.concatenate([na * goa_ref[...], nbb * gob_ref[...]], axis=1).astype(BF16)
        y_ref[0] = yb
        m = _dot(yb, wmo_ref[...])
        m_ref[0] = m
        nm, _ = _rms(m)
        xo_ref[0] = x_ref[0] + gt_ref[0] * (nm * gpost_ref[...])

    t5 = _tok_specs(tm, WA)
    tok = _tok_specs(tm, D)
    r5 = _row_spec(WA)
    full = lambda shape: pl.BlockSpec(shape, lambda b, i: (0,) * len(shape))
    return pl.pallas_call(
        body, name="mix_mid_fwd", grid=(nb, s // tm),
        out_shape=[jax.ShapeDtypeStruct((nb, s, D), F32), jax.ShapeDtypeStruct((nb, s, WB), F32),
                   jax.ShapeDtypeStruct((nb, s, D), BF16), jax.ShapeDtypeStruct((nb, s, D), F32)],
        in_specs=[tok, t5, t5, t5, t5, _halo_prev_spec(tm), _halo_prev_spec(tm), _mod_spec(), r5, r5,
                  full((CH, NH * CH)), full((CH, WA)), full((HALO, WB)), r5, r5, r5, r5, r5, VMEM_FULL, _row_spec()],
        out_specs=[tok, t5, tok, tok],
        scratch_shapes=[pltpu.VMEM((_ext_rows(tm), WB), F32), pltpu.VMEM((SHIFTS, tm + HALO, WB), F32)],
        compiler_params=_cparams(),
    )(x, u, v, a, g, a, g, gt, gn_g, gn_b, wcat, bspt, conv_w, conv_b, cn_g, cn_b, go_a, go_b, w_mo, g_post)


def _mix_out_bwd(dxo, m, gt, g_post, w_mo):
    nb, s, _ = m.shape
    tm = min(TM, s)

    def body(dxo_ref, m_ref, gt_ref, gpost_ref, wmo_ref, dy_ref, dm_ref, dgpost_ref, dgt_ref):
        b, i = pl.program_id(0), pl.program_id(1)
        dxo_v = dxo_ref[0]
        nm, q = _rms(m_ref[0])
        gpost = gpost_ref[...]
        dgt = jnp.sum(dxo_v * (nm * gpost), axis=0, keepdims=True)
        dm, dgpost = _rms_bwd(dxo_v * gt_ref[0], nm, q, gpost)
        dmb = dm.astype(BF16)
        dm_ref[0] = dmb
        dy_ref[0] = _dot_nt(dmb, wmo_ref[...])
        _acc(dgpost_ref, dgpost, _first(b, i))
        _acc(dgt_ref, dgt[None], i == 0)

    tok = _tok_specs(tm, D)
    return pl.pallas_call(
        body, name="mix_out_bwd", grid=(nb, s // tm),
        out_shape=[jax.ShapeDtypeStruct((nb, s, D), F32), jax.ShapeDtypeStruct((nb, s, D), BF16),
                   jax.ShapeDtypeStruct((1, D), F32), jax.ShapeDtypeStruct((nb, 1, D), F32)],
        in_specs=[tok, tok, _mod_spec(), _row_spec(), VMEM_FULL],
        out_specs=[tok, tok, _row_spec(), _mod_spec()],
        compiler_params=_cparams(),
    )(dxo, m, gt, g_post, w_mo)


def _mix_mid_bwd(dy, u, v, conv, gn_g, gn_b, wcat, wcat_t, bspt, cn_g, cn_b, go_a, go_b):
    nb, s, _ = dy.shape
    tm = min(TM, s)
    nchunk = tm // CH

    def body(dy_ref, u_ref, v_ref, conv_ref, gng_ref, gnb_ref, wcat_ref, wcatt_ref, bspt_ref, cng_ref, cnb_ref,
             goa_ref, gob_ref,
             du_ref, dv_ref, dconv_ref, dwcat_ref, dbsp_ref, dgng_ref, dgnb_ref, dgoa_ref, dgob_ref,
             dcng_ref, dcnb_ref, dcb_ref):
        first = _first(pl.program_id(0), pl.program_id(1))
        dyv = dy_ref[0]
        xhat, rstd = _ln(v_ref[0])
        gng = gng_ref[...]
        vb = (xhat * gng + gnb_ref[...]).astype(BF16)
        wsb = _spatial_weights(wcat_ref, False)
        wsb_t = _spatial_weights(wcatt_ref, True)
        bias = _spatial_bias(bspt_ref)
        masks = _head_mask((CH, WA))
        vexps, zs = [], []
        for cidx in range(nchunk):
            vexp = _expand_heads(vb[cidx * CH:(cidx + 1) * CH, :], masks)
            vexps.append(vexp)
            zs.append(_dot(wsb, vexp) + bias)
        z = jnp.concatenate(zs, axis=0)
        uv = u_ref[0]
        na, ra = _rms(uv * z)
        dya, dgoa = _rms_bwd(dyv[:, 0:WA], na, ra, goa_ref[...])
        du_ref[0] = dya * z
        dz = dya * uv
        dwcat = jnp.zeros((CH, NH * CH), F32)
        dzsum = jnp.zeros((CH, WA), F32)
        dvlns = []
        for cidx in range(nchunk):
            dzc = dz[cidx * CH:(cidx + 1) * CH, :]
            dzsum = dzsum + dzc
            dzb = dzc.astype(BF16)
            dwcat = dwcat + _dot_nt(dzb, vexps[cidx])
            dvexp = _dot(wsb_t, dzb)
            dvl = jnp.zeros((CH, WA), F32)
            for h in range(NH):
                dvl = dvl + jnp.where(masks[h], dvexp[h * CH:(h + 1) * CH, :], 0.0)
            dvlns.append(dvl)
        dvln = jnp.concatenate(dvlns, axis=0)
        dv, dgng, dgnb = _ln_bwd(dvln, xhat, rstd, gng)
        dv_ref[0] = dv
        lane = lax.broadcasted_iota(jnp.int32, (NH, WA), 1)
        head = lax.broadcasted_iota(jnp.int32, (NH, WA), 0)
        sel = jnp.where((lane >= head * HD) & (lane < (head + 1) * HD), 1.0, 0.0).astype(F32)
        dbsp = lax.dot_general(sel, dzsum, NT, preferred_element_type=F32, precision=lax.Precision.HIGHEST)
        chat, crstd = _ln(conv_ref[0])
        cng = cng_ref[...]
        cln = chat * cng + cnb_ref[...]
        sg = _sigmoid(cln)
        nbb, rb = _rms(cln * sg)
        dyb, dgob = _rms_bwd(dyv[:, WA:D], nbb, rb, gob_ref[...])
        dconv, dcng, dcnb = _ln_bwd(dyb * _dsilu(cln, sg), chat, crstd, cng)
        dconv_ref[0] = dconv
        dcb = jnp.sum(dconv, axis=0, keepdims=True)
        for ref, val in ((dwcat_ref, dwcat), (dbsp_ref, dbsp), (dgng_ref, dgng), (dgnb_ref, dgnb), (dgoa_ref, dgoa),
                         (dgob_ref, dgob), (dcng_ref, dcng), (dcnb_ref, dcnb), (dcb_ref, dcb)):
            _acc(ref, val, first)

    t5 = _tok_specs(tm, WA)
    r5 = _row_spec(WA)
    full = lambda shape: pl.BlockSpec(shape, lambda b, i: (0,) * len(shape))
    big = jax.ShapeDtypeStruct((nb, s, WA), F32)
    row = jax.ShapeDtypeStruct((1, WA), F32)
    return pl.pallas_call(
        body, name="mix_mid_bwd", grid=(nb, s // tm),
        out_shape=[big, big, big, jax.ShapeDtypeStruct((CH, NH * CH), F32), jax.ShapeDtypeStruct((NH, CH), F32),
                   row, row, row, row, row, row, row],
        in_specs=[_tok_specs(tm, D), t5, t5, t5, r5, r5, full((CH, NH * CH)), full((NH * CH, CH)), full((CH, WA)),
                  r5, r5, r5, r5],
        out_specs=[t5, t5, t5, full((CH, NH * CH)), full((NH, CH)), r5, r5, r5, r5, r5, r5, r5],
        compiler_params=_cparams(),
    )(dy, u, v, conv, gn_g, gn_b, wcat, wcat_t, bspt, cn_g, cn_b, go_a, go_b)


def _mix_in_bwd(dxo, x, du, dv, dconv, a, g, sh, sc, g_pre, w_mi4, conv_w):
    nb, s, _ = x.shape
    tm = min(TM, s)
    n_i = s // tm

    def body(dxo_ref, x_ref, du_ref, dv_ref, dc_ref, dch_ref, a_ref, g_ref, ah_ref, gh_ref, sh_ref, sc_ref,
             gpre_ref, w_ref, cw_ref,
             dx_ref, dproj_ref, h_ref, dgpre_ref, dsh_ref, dsc_ref, dcw_ref, ext_ref, shf_ref, dglu_ref):
        b, i = pl.program_id(0), pl.program_id(1)
        first = _first(b, i)
        av, gv = a_ref[0], g_ref[0]
        sg = _sigmoid(gv)
        dconv = dc_ref[0]
        ext_ref[0:tm, :] = dconv
        ext_ref[tm:tm + HALO, :] = dch_ref[0] * jnp.where(i == n_i - 1, 0.0, 1.0).astype(F32)
        _make_shifts(ext_ref, shf_ref, tm)

        def put_dglu(r0, acc):
            dglu_ref[pl.ds(r0, TAP_ROWS), :] = acc

        _conv_taps(shf_ref, cw_ref, tm, [(CK - 1 - k, k) for k in range(CK)], put_dglu)
        dglu = dglu_ref[...]
        ext_ref[0:HALO, :] = (ah_ref[0] * _sigmoid(gh_ref[0])) * jnp.where(i == 0, 0.0, 1.0).astype(F32)
        ext_ref[HALO:HALO + tm, :] = av * sg
        _make_shifts(ext_ref, shf_ref, tm)

        @pl.when(first)
        def _():
            dcw_ref[...] = jnp.zeros((HALO, WB), F32)

        for k in range(CK):
            o = k + HALO - (CK - 1)
            lo = SHIFTS * (o // SHIFTS)
            dcw_ref[k:k + 1, :] += jnp.sum(dconv * shf_ref[o % SHIFTS, lo:lo + tm, :], axis=0, keepdims=True)
        da = dglu * sg
        dg = dglu * av * (sg * (1.0 - sg))
        parts = [du_ref[0].astype(BF16), dv_ref[0].astype(BF16), da.astype(BF16), dg.astype(BF16)]
        dh = jnp.zeros((tm, D), F32)
        for k in range(4):
            dproj_ref[0, :, k * WA:(k + 1) * WA] = parts[k]
            dh = dh + _dot_nt(parts[k], w_ref[k])
        n, r = _rms(x_ref[0])
        gpre = gpre_ref[...]
        ng = n * gpre
        scale1 = 1.0 + sc_ref[0]
        h_ref[0] = (ng * scale1 + sh_ref[0]).astype(BF16)
        dsh = jnp.sum(dh, axis=0, keepdims=True)
        dsc = jnp.sum(dh * ng, axis=0, keepdims=True)
        dxn, dgpre = _rms_bwd(dh * scale1, n, r, gpre)
        dx_ref[0] = dxo_ref[0] + dxn
        _acc(dgpre_ref, dgpre, first)
        _acc(dsh_ref, dsh[None], i == 0)
        _acc(dsc_ref, dsc[None], i == 0)

    tok = _tok_specs(tm, D)
    t5 = _tok_specs(tm, WA)
    full = lambda shape: pl.BlockSpec(shape, lambda b, i: (0,) * len(shape))
    mod_shape = jax.ShapeDtypeStruct((nb, 1, D), F32)
    return pl.pallas_call(
        body, name="mix_in_bwd", grid=(nb, n_i),
        out_shape=[jax.ShapeDtypeStruct((nb, s, D), F32), jax.ShapeDtypeStruct((nb, s, 4 * WA), BF16),
                   jax.ShapeDtypeStruct((nb, s, D), BF16), jax.ShapeDtypeStruct((1, D), F32), mod_shape, mod_shape,
                   jax.ShapeDtypeStruct((HALO, WB), F32)],
        in_specs=[tok, tok, t5, t5, t5, _halo_next_spec(tm, s), t5, t5, _halo_prev_spec(tm), _halo_prev_spec(tm),
                  _mod_spec(), _mod_spec(), _row_spec(), VMEM_FULL, full((HALO, WB))],
        out_specs=[tok, _tok_specs(tm, 4 * WA), tok, _row_spec(), _mod_spec(), _mod_spec(), full((HALO, WB))],
        scratch_shapes=[pltpu.VMEM((_ext_rows(tm), WB), F32), pltpu.VMEM((SHIFTS, tm + HALO, WB), F32),
                        pltpu.VMEM((tm, WB), F32)],
        compiler_params=_cparams(),
    )(dxo, x, du, dv, dconv, dconv, a, g, a, g, sh, sc, g_pre, w_mi4, conv_w)


def _row_tile(rows, cols):
    best = 16
    for t in range(16, rows + 1, 16):
        if rows % t == 0 and t * cols * 4 <= 1536 * 1024:
            best = t
    return best


def _walk(steps):
    offs = [sum(steps[:k]) for k in range(len(steps))]

    def tile(k):
        return lambda i: jnp.clip(i - offs[k], 0, steps[k] - 1)

    def mine(k, i):
        return jnp.logical_and(i >= offs[k], i < offs[k] + steps[k])

    return sum(steps), tile, mine


def _sum4(name, own4s, recvs, j_arr):
    n = len(own4s)
    shapes = [o.shape[1:] for o in own4s]
    trs = [_row_tile(r, c) for r, c in shapes]
    total, tile, mine = _walk([r // tr for (r, _), tr in zip(shapes, trs)])

    def body(j_ref, *refs):
        del j_ref
        i = pl.program_id(0)
        for k in range(n):
            own_ref, recv_ref, o_ref = refs[2 * k], refs[2 * k + 1], refs[2 * n + k]

            def add(own_ref=own_ref, recv_ref=recv_ref, o_ref=o_ref):
                acc = own_ref[0]
                for q in range(3):
                    acc = acc + recv_ref[q].astype(F32)
                o_ref[...] = acc

            pl.when(mine(k, i))(add)

    in_specs, out_specs = [], []
    for k, ((_, cols), tr) in enumerate(zip(shapes, trs)):
        in_specs += [pl.BlockSpec((1, tr, cols), lambda i, j, t=tile(k): (j[0], t(i), 0)),
                     pl.BlockSpec((3, tr, cols), lambda i, j, t=tile(k): (0, t(i), 0))]
        out_specs.append(pl.BlockSpec((tr, cols), lambda i, j, t=tile(k): (t(i), 0)))
    return pl.pallas_call(
        body, name=name,
        grid_spec=pltpu.PrefetchScalarGridSpec(num_scalar_prefetch=1, grid=(total,), in_specs=in_specs, out_specs=out_specs),
        out_shape=[jax.ShapeDtypeStruct(sh, F32) for sh in shapes],
        compiler_params=_cparams(),
    )(j_arr, *[a for pair in zip(own4s, recvs) for a in pair])


def _pair_plan(shapes):
    def plan(x, y, c, src, land):
        sends = []
        for a, shape in enumerate(shapes):
            rows = shape[1] // 2
            theirs = pl.ds(pl.multiple_of((1 - c) * rows, 16), rows)
            sends.append((src[a].at[:, theirs], land[a], (x, y, 1 - c), land[a]))
        return [], sends

    return plan


def _swap_plan(n):
    def plan(x, y, c, src, land):
        return [], [(src[a], land[a], (x, y, 1 - c), land[a]) for a in range(n)]

    return plan


def _pair_sum(name, g32s, recvs, c_arr):
    n = len(g32s)
    shapes = [r.shape for r in recvs]
    trs = [_row_tile(rows, cols) for _, rows, cols in shapes]
    nhs = [rows // tr for (_, rows, _), tr in zip(shapes, trs)]
    total, tile, mine = _walk([nblk * nh for (nblk, _, _), nh in zip(shapes, nhs)])

    def body(c_ref, *refs):
        del c_ref
        i = pl.program_id(0)
        for k in range(n):
            g_ref, r_ref, o32_ref, obf_ref = refs[2 * k], refs[2 * k + 1], refs[2 * n + 2 * k], refs[2 * n + 2 * k + 1]

            def add(g_ref=g_ref, r_ref=r_ref, o32_ref=o32_ref, obf_ref=obf_ref):
                val = g_ref[0] + r_ref[0].astype(F32)
                o32_ref[0] = val
                obf_ref[0] = val.astype(BF16)

            pl.when(mine(k, i))(add)

    in_specs, out_specs, out_shape = [], [], []
    for k, ((_, _, cols), tr, nh) in enumerate(zip(shapes, trs, nhs)):
        def half(tr=tr, cols=cols, t=tile(k), nh=nh):
            return pl.BlockSpec((1, tr, cols), lambda i, c: (t(i) // nh, t(i) % nh, 0))

        in_specs += [pl.BlockSpec((1, tr, cols), lambda i, c, t=tile(k), nh=nh: (t(i) // nh, c[0] * nh + t(i) % nh, 0)), half()]
        out_specs += [half(), half()]
        out_shape += [jax.ShapeDtypeStruct(shapes[k], F32), jax.ShapeDtypeStruct(shapes[k], BF16)]
    res = pl.pallas_call(
        body, name=name,
        grid_spec=pltpu.PrefetchScalarGridSpec(num_scalar_prefetch=1, grid=(total,), in_specs=in_specs, out_specs=out_specs),
        out_shape=out_shape,
        compiler_params=_cparams(),
    )(c_arr, *[a for pair in zip(g32s, recvs) for a in pair])
    return [(res[2 * k], res[2 * k + 1]) for k in range(n)]


def _adam_halves(name, w, m, v, mine, theirs, c_arr):
    rows, cols = w.shape
    tr = _row_tile(rows // 2, cols)
    nh = (rows // 2) // tr

    def body(c_ref, w_ref, m_ref, v_ref, mine_ref, theirs_ref, g_out, d_out, m_out, v_out):
        here = (pl.program_id(0) // nh) == c_ref[0]
        g = jnp.where(here, mine_ref[...], theirs_ref[...])
        delta, m2, v2 = _adam(w_ref[...], g, m_ref[...], v_ref[...])
        g_out[...] = g
        d_out[...] = delta
        m_out[...] = m2
        v_out[...] = v2

    spec = pl.BlockSpec((tr, cols), lambda i, c: (i, 0))
    shape = jax.ShapeDtypeStruct((rows, cols), F32)
    return pl.pallas_call(
        body, name=name,
        grid_spec=pltpu.PrefetchScalarGridSpec(
            num_scalar_prefetch=1, grid=(2 * nh,),
            in_specs=[spec, spec, spec,
                      pl.BlockSpec((tr, cols), lambda i, c: (jnp.clip(i - c[0] * nh, 0, nh - 1), 0)),
                      pl.BlockSpec((tr, cols), lambda i, c: (jnp.clip(i - (1 - c[0]) * nh, 0, nh - 1), 0))],
            out_specs=[spec] * 4),
        out_shape=[shape] * 4,
        compiler_params=_cparams(),
    )(c_arr, w, m, v, mine, theirs)


def _adam_big(name, w, m, v, ga, gb):
    rows, cols = w.shape
    tr = _row_tile(rows, cols)

    def body(w_ref, m_ref, v_ref, ga_ref, gb_ref, g_out, d_out, m_out, v_out):
        gsum = ga_ref[...] + gb_ref[...]
        delta, m2, v2 = _adam(w_ref[...], gsum, m_ref[...], v_ref[...])
        g_out[...] = gsum
        d_out[...] = delta
        m_out[...] = m2
        v_out[...] = v2

    spec = pl.BlockSpec((tr, cols), lambda i: (i, 0))
    shape = jax.ShapeDtypeStruct((rows, cols), F32)
    return pl.pallas_call(
        body, name=name, grid=(rows // tr,), out_shape=[shape] * 4,
        in_specs=[spec] * 5, out_specs=[spec] * 4, compiler_params=_cparams(),
    )(w, m, v, ga, gb)


PK_VEC = 0
PK_LOSS = 6
PK_PAIR = 8
PK_BSP = 16
PK_WCAT = 24
PK_ROWS = PK_WCAT + CH
PAIR_ORDER = ("gmlp_norm_g", "gmlp_norm_b", "conv_b", "conv_norm_g", "conv_norm_b", "g_out_a", "g_out_b")
VEC_ORDER = ("g_pre_f1", "g_post_f1", "g_pre_m", "g_post_m", "g_pre_f2", "g_post_f2")


def _pack_late(rows):
    counts = [r.shape[0] for r in rows]
    assert sum(counts) == 8

    def body(*refs):
        o_ref = refs[-1]
        at = 0
        for r, cnt in zip(refs[:-1], counts):
            o_ref[at:at + cnt, :] = r[...]
            at += cnt

    return pl.pallas_call(
        body, name="pack_late", out_shape=jax.ShapeDtypeStruct((8, D), F32),
        in_specs=[VMEM_FULL] * len(rows), out_specs=VMEM_FULL, compiler_params=_cparams(),
    )(*rows)


def _pack_small(vecs, pairs, dbsp, dwcat, lsum):
    def body(*refs):
        vec_refs = refs[:4]
        pair_refs = refs[4:11]
        dbsp_ref, dwcat_ref, lsum_ref, o_ref = refs[11:]
        o_ref[0:PK_WCAT, :] = jnp.zeros((PK_WCAT, D), F32)
        o_ref[PK_LOSS:PK_LOSS + 1, 0:128] = lsum_ref[...]
        for k, r in enumerate(vec_refs):
            o_ref[PK_VEC + 2 + k:PK_VEC + 3 + k, :] = r[...]
        for k, r in enumerate(pair_refs):
            row, half = PK_PAIR + k // 2, k % 2
            o_ref[row:row + 1, half * WA:(half + 1) * WA] = r[...]
        o_ref[PK_BSP:PK_BSP + NH, 0:CH] = dbsp_ref[...]
        o_ref[PK_WCAT:PK_ROWS, :] = dwcat_ref[...]

    args = list(vecs) + list(pairs) + [dbsp, dwcat, lsum]
    return pl.pallas_call(
        body, name="pack_small", out_shape=jax.ShapeDtypeStruct((PK_ROWS, D), F32),
        in_specs=[VMEM_FULL] * len(args), out_specs=VMEM_FULL, compiler_params=_cparams(),
    )(*args)


def _small_adam(pack_all, late_all, dcw_all, dada_all, params, behind):
    names = list(VEC_ORDER) + list(PAIR_ORDER) + ["b_spatial", "w_spatial", "conv_w", "b_ada"]
    flat = []
    for nm in names:
        flat += list(params[nm])
    n_in = 4 + len(flat)

    def body(*refs):
        pack_ref, late_ref, dcw_ref, dada_ref = refs[:4]
        prm = refs[4:n_in]
        outs = refs[n_in + 1:]

        def total(r0, nr, c0, nc):
            acc = pack_ref[0, r0:r0 + nr, c0:c0 + nc]
            for d in range(1, NDEV):
                acc = acc + pack_ref[d, r0:r0 + nr, c0:c0 + nc]
            return acc

        def emit(idx, g, getw, put):
            w_ref, m_ref, v_ref = prm[3 * idx:3 * idx + 3]
            delta, m2, v2 = _adam(getw(w_ref), g, getw(m_ref), getw(v_ref))
            for o_ref, val in zip(outs[4 * idx:4 * idx + 4], (g, delta, m2, v2)):
                put(o_ref, val)

        def whole(ref):
            return ref[...]

        def put_whole(ref, val):
            ref[...] = val

        idx = 0
        for k in range(6):
            if k < 2:
                g = late_ref[0, k:k + 1, :]
                for d in range(1, NDEV):
                    g = g + late_ref[d, k:k + 1, :]
            else:
                g = total(PK_VEC + k, 1, 0, D)
            emit(idx, g, whole, put_whole)
            idx += 1
        for k in range(7):
            emit(idx, total(PK_PAIR + k // 2, 1, (k % 2) * WA, WA), whole, put_whole)
            idx += 1
        emit(idx, total(PK_BSP, NH, 0, CH), lambda r: r[0], lambda r, val: r.__setitem__(0, val))
        idx += 1
        row = lax.broadcasted_iota(jnp.int32, (CH, CH), 0)
        col = lax.broadcasted_iota(jnp.int32, (CH, CH), 1)
        for h in range(NH):
            gh = jnp.where(col <= row, total(PK_WCAT, CH, h * CH, CH), 0.0)
            w_ref, m_ref, v_ref = prm[3 * idx:3 * idx + 3]
            delta, m2, v2 = _adam(w_ref[0, h], gh, m_ref[0, h], v_ref[0, h])
            for o_ref, val in zip(outs[4 * idx:4 * idx + 4], (gh, delta, m2, v2)):
                o_ref[0, h] = val
        idx += 1
        gcw = dcw_ref[0, 0:CK, :]
        for d in range(1, NDEV):
            gcw = gcw + dcw_ref[d, 0:CK, :]
        emit(idx, gcw, lambda r: r[0], lambda r, val: r.__setitem__(0, val))
        idx += 1
        emit(idx, jnp.sum(dada_ref[...], axis=0, keepdims=True), whole, put_whole)
        outs[-1][...] = jnp.sum(total(PK_LOSS, 1, 0, 128), axis=1, keepdims=True) * (0.5 / D)

    out_shape = []
    for nm in names:
        w = params[nm][0]
        out_shape += [jax.ShapeDtypeStruct(w.shape, F32)] * 4
    out_shape.append(jax.ShapeDtypeStruct((1, 1), F32))
    res = pl.pallas_call(
        body, name="small_adam", out_shape=out_shape,
        in_specs=[VMEM_FULL] * n_in + [ANY], out_specs=[VMEM_FULL] * len(out_shape), compiler_params=_cparams(),
    )(pack_all, late_all, dcw_all, dada_all, *flat, behind)
    return {nm: tuple(res[4 * k:4 * k + 4]) for k, nm in enumerate(names)}, res[-1].reshape(())


WEIGHTS = ['w_ada', 'b_ada', 'g_pre_f1', 'g_post_f1', 'w_f1_in', 'w_f1_out', 'g_pre_m', 'g_post_m', 'w_mix_in',
           'gmlp_norm_g', 'gmlp_norm_b', 'w_spatial', 'b_spatial', 'conv_w', 'conv_b', 'conv_norm_g', 'conv_norm_b',
           'g_out_a', 'g_out_b', 'w_mix_out', 'g_pre_f2', 'g_post_f2', 'w_f2_in', 'w_f2_out']


def kernel(x, c, w_ada, b_ada, g_pre_f1, g_post_f1, w_f1_in, w_f1_out, g_pre_m, g_post_m, w_mix_in, gmlp_norm_g, gmlp_norm_b, w_spatial, b_spatial, conv_w, conv_b, conv_norm_g, conv_norm_b, g_out_a, g_out_b, w_mix_out, g_pre_f2, g_post_f2, w_f2_in, w_f2_out, loss_target, m_w_ada, m_b_ada, m_g_pre_f1, m_g_post_f1, m_w_f1_in, m_w_f1_out, m_g_pre_m, m_g_post_m, m_w_mix_in, m_gmlp_norm_g, m_gmlp_norm_b, m_w_spatial, m_b_spatial, m_conv_w, m_conv_b, m_conv_norm_g, m_conv_norm_b, m_g_out_a, m_g_out_b, m_w_mix_out, m_g_pre_f2, m_g_post_f2, m_w_f2_in, m_w_f2_out, v_w_ada, v_b_ada, v_g_pre_f1, v_g_post_f1, v_w_f1_in, v_w_f1_out, v_g_pre_m, v_g_post_m, v_w_mix_in, v_gmlp_norm_g, v_gmlp_norm_b, v_w_spatial, v_b_spatial, v_conv_w, v_conv_b, v_conv_norm_g, v_conv_norm_b, v_g_out_a, v_g_out_b, v_w_mix_out, v_g_pre_f2, v_g_post_f2, v_w_f2_in, v_w_f2_out):
    env = dict(locals())
    wts = {n: env[n] for n in WEIGHTS}
    mom = {n: env["m_" + n] for n in WEIGHTS}
    var = {n: env["v_" + n] for n in WEIGHTS}
    nb, s, _ = x.shape
    t = nb * s
    ax, ay, ac = lax.axis_index("x"), lax.axis_index("y"), lax.axis_index("c")
    j_chip = 2 * ax + ay
    dev = 4 * ax + 2 * ay + ac
    j_arr = j_chip.reshape(1).astype(jnp.int32)

    groups = (("w_f1_in",), ("w_mix_in", "w_mix_out"), ("w_f2_in", "w_f2_out"), ("w_f1_out",))
    def gather_operands(gi):
        srcs = [wts[n][0].astype(BF16) for n in groups[gi]] + ([conv_w[0]] if gi == 1 else [])
        lands = [lax.dynamic_update_index_in_dim(lax.empty((NCHIP,) + a.shape, a.dtype), a, j_chip, 0) for a in srcs]
        return srcs, lands

    def gather_start(gi, behind, operands=None):
        srcs, lands = operands or gather_operands(gi)
        plan_a, plan_b, n_b = _gather_plans([a.shape for a in srcs])
        ssem, rsem, srcs, lands, token = _split_start("gw_start%d" % gi, srcs, lands, plan_a, 3 * len(srcs), behind)
        gather[gi] = (srcs, lands, ssem, rsem, plan_a, plan_b, n_b)
        return token

    def gather_forward(gi, behind):
        srcs, lands, ssem, rsem, plan_a, plan_b, n_b = gather[gi]
        ssem, rsem, lands, token = _split_forward("gw_fwd%d" % gi, srcs, lands, ssem, rsem, plan_a, plan_b, n_b, behind)
        gather[gi] = (lands, ssem, rsem, plan_b)
        return token

    def gathered(gi, behind):
        lands, ssem, rsem, plan_b = gather[gi]
        return _split_wait("gw_wait%d" % gi, [], lands, ssem, rsem, plan_b, behind)

    gather = {}
    def allgather_start(tag, arrs, behind):
        lands = [lax.dynamic_update_index_in_dim(lax.empty((NDEV,) + a.shape, a.dtype), a, dev, 0) for a in arrs]
        ssem, rsem, srcs, lands, token = _split_start("small_start_" + tag, arrs, lands, _allgather_plan(len(arrs)),
                                                      7 * len(arrs), behind)
        return (srcs, lands, ssem, rsem), token

    def allgather_wait(tag, state, behind):
        srcs, lands, ssem, rsem = state
        return _split_wait("small_wait_" + tag, srcs, lands, ssem, rsem, _allgather_plan(len(srcs)), behind)

    c_state, token = allgather_start("c", [c.reshape(8, (nb * D) // 8)], c)
    token = gather_start(0, token)
    (c_all8,) = allgather_wait("c", c_state, token)
    c_all = c_all8.reshape(NDEV * nb, D)
    b_sh = lax.dynamic_slice(b_ada, (0, j_chip * ADA_SH), (1, ADA_SH))
    ada_sh = _ada_fwd(c_all, w_ada[0], b_sh)
    later = [gather_operands(3), gather_operands(1), gather_operands(2)]
    (ada4,) = _chip_allgather("gather_ada", [ada_sh], behind=[a for pair in later for arrs in pair for a in arrs])
    token = gather_forward(0, ada4)
    plans = [_gather_plans([a.shape for a in srcs]) for srcs, _ in later]
    started, token = _split_start_groups(
        "gw_start_later", [(srcs, lands, pa, 3 * len(srcs)) for (srcs, lands), (pa, _, _) in zip(later, plans)], token)
    for gi, (ssem, rsem, srcs, lands), (pa, pb, n_b) in zip((3, 1, 2), started, plans):
        gather[gi] = (srcs, lands, ssem, rsem, pa, pb, n_b)
    ada_me = lax.dynamic_slice(ada4, (0, dev * nb, 0), (NCHIP, nb, ADA_SH))
    ada_me = jnp.transpose(ada_me, (1, 0, 2)).reshape(nb, NMOD * D)
    sh1, sc1, gt1, sh2, sc2, gt2, sh3, sc3, gt3 = [ada_me[:, k * D:(k + 1) * D].reshape(nb, 1, D) for k in range(NMOD)]

    wcat = jnp.transpose(w_spatial[0], (1, 0, 2)).reshape(CH, NH * CH)
    wcat_t = jnp.transpose(w_spatial[0], (0, 2, 1)).reshape(NH * CH, CH)
    bspt = jnp.repeat(b_spatial[0].T, HD, axis=1)

    (w1i,) = gathered(0, token)
    p1, act1 = _ffn_up(x, sh1, sc1, g_pre_f1, w1i)
    token = gather_forward(3, act1)
    token = gather_forward(1, token)
    (w1o,) = gathered(3, token)
    w1o = w1o.reshape(DFF, D)
    x1, f1 = _ffn_down(x, act1, gt1, g_post_f1, w1o)
    wmi, wmo, cw4 = gathered(1, x1)
    wmo = wmo.reshape(D, D)
    cw_full = jnp.transpose(cw4, (1, 0, 2)).reshape(CK, WB)
    cw_pad = jnp.pad(cw_full, ((0, HALO - CK), (0, 0)))
    u, v, a, g = _mix_in_fwd(x1, sh2, sc2, g_pre_m, wmi)
    token = gather_forward(2, u)
    x2, conv, yb, m = _mix_mid_fwd(x1, u, v, a, g, gt2 + token[0, 0], gmlp_norm_g, gmlp_norm_b, wcat, bspt, cw_pad, conv_b,
                                   conv_norm_g, conv_norm_b, g_out_a, g_out_b, wmo, g_post_m)
    w2i, w2o = gathered(2, [x2, token])
    w2o = w2o.reshape(DFF, D)
    dx3, df2, p2, lsum, dg_post_f2, dgt3 = _ffn_loss_fwd(x2, sh3, sc3, gt3, g_pre_f2, g_post_f2, w2i, w2o, loss_target)

    def chip4(pair, rows):
        return [arr.reshape(NCHIP, rows, arr.shape[-1]) for arr in pair]

    def scatter_start(tag, pairs, behind):
        srcs = [p[1] for p in pairs]
        lands = [lax.empty((3,) + a.shape[1:], a.dtype) for a in srcs]
        ssem, rsem, srcs, lands, token = _split_start("gs_start_" + tag, srcs, lands, _scatter_plan(len(srcs)),
                                                      3 * len(srcs), behind)
        return (srcs, lands, ssem, rsem), token

    def scatter_wait(tag, state, behind):
        srcs, lands, ssem, rsem = state
        return _split_wait("gs_wait_" + tag, srcs, lands, ssem, rsem, _scatter_plan(len(srcs)), behind)

    out = {}
    dx2, dp2, h3, a2, dg_pre_f2, dsh3, dsc3 = _ffn_bwd(
        dx3, x2, None, p2, sh3, sc3, gt3, g_pre_f2, g_post_f2, w2i, w2o, df=df2)
    gw2i = _wgrad("wgrad_f2_in", h3.reshape(t, D), dp2.reshape(t, 2 * DFF), 2 * DFF // NCHIP, True)
    gw2o = chip4(_wgrad("wgrad_f2_out", a2.reshape(t, DFF), df2.reshape(t, D), D // 2, False), DFF // NCHIP)
    scat_f2, tok = scatter_start("f2", [gw2i, gw2o], dg_post_f2)
    dy, dm, dg_post_m, dgt2 = _mix_out_bwd(dx2, m, gt2 + tok[0, 0], g_post_m, wmo)
    gwmo = chip4(_wgrad("wgrad_mix_out", yb.reshape(t, D), dm.reshape(t, D), D // 2, False), D // NCHIP)
    (du, dv, dconv, dwcat, dbsp, dgn_g, dgn_b, dgo_a, dgo_b, dcn_g, dcn_b, dcb) = _mix_mid_bwd(
        dy, u, v, conv, gmlp_norm_g, gmlp_norm_b, wcat, wcat_t, bspt, conv_norm_g, conv_norm_b, g_out_a, g_out_b)
    dx1, dproj, h2, dg_pre_m, dsh2, dsc2, dcw = _mix_in_bwd(dx2, x1, du, dv, dconv, a, g, sh2, sc2, g_pre_m, wmi, cw_pad)
    gwmi = _wgrad("wgrad_mix_in", h2.reshape(t, D), dproj.reshape(t, 4 * WA), WA, True)

    vec_grads = dict(g_pre_m=dg_pre_m, g_post_m=dg_post_m, g_pre_f2=dg_pre_f2, g_post_f2=dg_post_f2)
    pair_grads = dict(gmlp_norm_g=dgn_g, gmlp_norm_b=dgn_b, conv_b=dcb, conv_norm_g=dcn_g, conv_norm_b=dcn_b,
                      g_out_a=dgo_a, g_out_b=dgo_b)
    pack = _pack_small([vec_grads[n] for n in VEC_ORDER[2:]], [pair_grads[n] for n in PAIR_ORDER], dbsp, dwcat, lsum)
    dada_early = jnp.concatenate([q.reshape(nb, D) for q in (dsh2, dsc2, dgt2, dsh3, dsc3, dgt3)], axis=1)
    small_early = [pack, dcw, dada_early.reshape(8, (nb * 6 * D) // 8)]
    mix_bf16 = [gwmi[1], gwmo[1]]
    (s_mix, s_early), tok2 = _split_start_groups("gs_start_mix_small", [
        (mix_bf16, [lax.empty((3,) + a.shape[1:], a.dtype) for a in mix_bf16], _scatter_plan(2), 6),
        (small_early, [lax.dynamic_update_index_in_dim(lax.empty((NDEV,) + a.shape, a.dtype), a, dev, 0) for a in small_early],
         _allgather_plan(3), 21)], dg_pre_m)
    scat_mix = (s_mix[2], s_mix[3], s_mix[0], s_mix[1])
    early = (s_early[2], s_early[3], s_early[0], s_early[1])
    grad_x, dp1, h1, a1, df1, dg_pre_f1, dg_post_f1, dsh1, dsc1, dgt1 = _ffn_bwd(
        dx1, x, f1, p1, sh1 + tok2[0, 0], sc1, gt1, g_pre_f1, g_post_f1, w1i, w1o)
    late_pack = _pack_late([dg_pre_f1, dg_post_f1] + [q.reshape(nb, D) for q in (dsh1, dsc1, dgt1)])
    late, tok2 = allgather_start("late", [late_pack], dg_post_f1)
    gw1i = _wgrad("wgrad_f1_in", h1.reshape(t, D), dp1.reshape(t, 2 * DFF), 2 * DFF // NCHIP, True)
    gw1o = chip4(_wgrad("wgrad_f1_out", a1.reshape(t, DFF), df1.reshape(t, D), D // 2, False), DFF // NCHIP)
    def d2d_start(tag, srcs, lands, plan, behind):
        ssem, rsem, srcs, lands, token = _split_start("d2d_start_" + tag, srcs, lands, plan, len(srcs), behind)
        return (srcs, lands, ssem, rsem, plan), token

    def d2d_wait(tag, state, behind):
        srcs, lands, ssem, rsem, plan = state
        return _split_wait("d2d_wait_" + tag, srcs, lands, ssem, rsem, plan, behind)

    def swap_start(tag, parts, behind):
        return d2d_start(tag, parts, [lax.empty(a.shape, a.dtype) for a in parts], _swap_plan(len(parts)), behind)

    def sums(names, pairs, recv):
        return _sum4("sum4_" + names[0][2:4], [p[0] for p in pairs], recv, j_arr)

    def update(names, part, other):
        for k, n in enumerate(names):
            out[n] = tuple(r[None] for r in _adam_big("adam_" + n, wts[n][0], mom[n][0], var[n][0], part[k], other[k]))

    c_arr = ac.reshape(1).astype(jnp.int32)
    halves = [gw1i[1], gw1o[1]]
    pair_st, tok = d2d_start("pair", halves, [lax.empty((a.shape[0], a.shape[1] // 2, a.shape[2]), a.dtype) for a in halves],
                             _pair_plan([a.shape for a in halves]), tok2)
    names_f2, names_mix, names_f1 = ("w_f2_in", "w_f2_out"), ("w_mix_in", "w_mix_out"), ("w_f1_in", "w_f1_out")
    part_f2 = sums(names_f2, [gw2i, gw2o], scatter_wait("f2", scat_f2, tok))
    sib = d2d_wait("pair", pair_st, part_f2)
    pair_i, pair_o = _pair_sum("pairsum_f1", [gw1i[0], gw1o[0]], sib, c_arr)
    scat_f1, tok = scatter_start("f1", [pair_i, pair_o], tok2)
    swap_f2, tok = swap_start("swap_f2", part_f2, tok)
    part_mix = sums(names_mix, [gwmi, gwmo], scatter_wait("mix", scat_mix, tok))
    swap_mix, tok = swap_start("swap_mix", part_mix, part_mix[1])

    pack_all, dcw_all, dada_early8 = allgather_wait("early", early, tok)
    (late_all,) = allgather_wait("late", late, pack_all)
    dada_late = jnp.transpose(late_all[:, 2:8, :].reshape(NDEV, 3, nb, D), (0, 2, 1, 3)).reshape(NDEV * nb, 3 * D)
    dada_all = jnp.concatenate([dada_late, dada_early8.reshape(NDEV * nb, 6 * D)], axis=1)
    dada_sh = lax.dynamic_slice(dada_all, (0, j_chip * ADA_SH), (NDEV * nb, ADA_SH))
    out["w_ada"] = tuple(r[None] for r in _ada_bwd_adam(c_all, dada_sh, w_ada[0], m_w_ada[0], v_w_ada[0]))
    update(names_f2, part_f2, d2d_wait("swap_f2", swap_f2, out["w_ada"][3]))
    update(names_mix, part_mix, d2d_wait("swap_mix", swap_mix, out["w_f2_out"][3]))

    mine = sums(names_f1, [pair_i, pair_o], scatter_wait("f1", scat_f1, out["w_mix_out"][3]))
    swap_f1, tok = swap_start("swap_f1", mine, mine[1])
    dcw_mine = lax.dynamic_slice(dcw_all, (0, 0, j_chip * (WB // NCHIP)), (NDEV, HALO, WB // NCHIP))
    small = {n: (wts[n], mom[n], var[n]) for n in list(VEC_ORDER) + list(PAIR_ORDER) + ["b_spatial", "w_spatial", "conv_w", "b_ada"]}
    small_out, loss = _small_adam(pack_all, late_all, dcw_mine, dada_all, small, tok)
    out.update(small_out)
    theirs = d2d_wait("swap_f1", swap_f1, out["b_ada"][3])
    for k, n in enumerate(names_f1):
        out[n] = tuple(r[None] for r in _adam_halves("adam_" + n, wts[n][0], mom[n][0], var[n][0], mine[k], theirs[k],
                                                     c_arr))

    res = [loss, grad_x]
    for k in range(4):
        res += [out[n][k] for n in WEIGHTS]
    return tuple(res)
```

```python
import jax
import jax.numpy as jnp
from jax import lax
from jax.experimental import pallas as pl
from jax.experimental.pallas import tpu as pltpu

D = 1024
DFF = 2816
WA = 512
WB = 512
NH = 8
HD = 64
CH = 128
CK = 31
HALO = 32
NMOD = 9
EPS = 1e-6
NCHIP = 4
NDEV = 8
FBLK = DFF // 2
ADA_SH = NMOD * D // NCHIP

LR, B1, B2, EPS_A, WD, STEP = 0.001, 0.9, 0.999, 1e-08, 0.01, 10

F32 = jnp.float32
BF16 = jnp.bfloat16
MESH = pl.DeviceIdType.MESH
ANY = pl.BlockSpec(memory_space=pl.ANY)
VMEM_FULL = pl.BlockSpec(memory_space=pltpu.VMEM)
VMEM_LIMIT = 56 * 1024 * 1024
WGRAD_VMEM_BUDGET = 52 * 1024 * 1024
TM = 512
TM_FFN_BWD = 256

NT = (((1,), (1,)), ((), ()))
TN = (((0,), (0,)), ((), ()))


def _dot(a, b):
    return jnp.dot(a, b, preferred_element_type=F32)


def _dot_nt(a, b):
    return lax.dot_general(a, b, NT, preferred_element_type=F32)


def _dot_tn(a, b):
    return lax.dot_general(a, b, TN, preferred_element_type=F32)


def _cparams():
    return pltpu.CompilerParams(vmem_limit_bytes=VMEM_LIMIT)


def _chip_relations(x, y):
    return [(1 - x, y), (x, 1 - y), (1 - x, 1 - y)]


def _exchange(name, arrs, out_shapes, plan):
    n = len(arrs)
    n_out = len(out_shapes)

    def body(*refs):
        ins, outs = refs[:n], refs[n:n + n_out]
        send_sems, recv_sems, local_sems = refs[n + n_out:]
        x, y, c = lax.axis_index("x"), lax.axis_index("y"), lax.axis_index("c")
        local, sends = plan(x, y, c, ins, outs)
        locs = [pltpu.make_async_copy(s, d, local_sems.at[i]) for i, (s, d) in enumerate(local)]
        for loc in locs:
            loc.start()
        cps = [pltpu.make_async_remote_copy(src_ref=s, dst_ref=d, send_sem=send_sems.at[i], recv_sem=recv_sems.at[i],
                                            device_id=peer, device_id_type=MESH)
               for i, (s, d, peer, _) in enumerate(sends)]
        for cp in cps:
            cp.start()
        for i, (s, _, peer, landing) in enumerate(sends):
            pltpu.make_async_remote_copy(src_ref=s, dst_ref=landing, send_sem=send_sems.at[i], recv_sem=recv_sems.at[i],
                                         device_id=peer, device_id_type=MESH).wait_recv()
        for cp in cps:
            cp.wait_send()
        for loc in locs:
            loc.wait()

    return n, n_out, body


def _run_exchange(name, arrs, out_shapes, plan, n_local, n_send):
    n, n_out, body = _exchange(name, arrs, out_shapes, plan)
    return pl.pallas_call(
        body, name=name, out_shape=out_shapes,
        in_specs=[ANY] * n, out_specs=[ANY] * n_out,
        scratch_shapes=[pltpu.SemaphoreType.DMA((n_send,)), pltpu.SemaphoreType.DMA((n_send,)),
                        pltpu.SemaphoreType.DMA((max(n_local, 1),))],
    )(*arrs)


def _chip_allgather(name, arrs, behind=()):
    n = len(arrs)

    def plan(x, y, c, ins, outs):
        j_me = 2 * x + y
        local = [(ins[a], outs[a].at[j_me]) for a in range(n)]
        sends = []
        for a in range(n):
            for (px, py) in _chip_relations(x, y):
                sends.append((ins[a], outs[a].at[j_me], (px, py, c), outs[a].at[2 * px + py]))
        return local, sends

    shapes = [jax.ShapeDtypeStruct((NCHIP,) + a.shape, a.dtype) for a in arrs]
    return _run_exchange(name, list(arrs) + list(behind), shapes, plan, n, 3 * n)


HBM = pl.BlockSpec(memory_space=pltpu.HBM)
SEM = pl.BlockSpec(memory_space=pltpu.SEMAPHORE)
EFFECT = pltpu.SideEffectType.DATAFLOW_SIDE_EFFECTING


def _split_start_groups(name, groups, after):
    n_src = [len(g[0]) for g in groups]
    n_land = [len(g[1]) for g in groups]
    all_srcs = [pltpu.with_memory_space_constraint(a, pltpu.HBM) for g in groups for a in g[0]]
    all_lands = [pltpu.with_memory_space_constraint(a, pltpu.HBM) for g in groups for a in g[1]]
    ns, nl, ng = len(all_srcs), len(all_lands), len(groups)

    def body(*refs):
        src_refs, land_refs = refs[:ns], refs[ns:ns + nl]
        sem_refs = refs[ns + nl + 1:ns + nl + 1 + 2 * ng]
        token = refs[-1]
        x, y, c = lax.axis_index("x"), lax.axis_index("y"), lax.axis_index("c")
        at_src = at_land = 0
        for gi, (_, _, plan, _) in enumerate(groups):
            _, sends = plan(x, y, c, src_refs[at_src:at_src + n_src[gi]], land_refs[at_land:at_land + n_land[gi]])
            for i, (s, d, peer, _) in enumerate(sends):
                pltpu.make_async_remote_copy(src_ref=s, dst_ref=d, send_sem=sem_refs[2 * gi].at[i],
                                             recv_sem=sem_refs[2 * gi + 1].at[i], device_id=peer, device_id_type=MESH).start()
            at_src += n_src[gi]
            at_land += n_land[gi]
        token[...] = jnp.zeros_like(token)

    sems = [pltpu.SemaphoreType.DMA((g[3],)) for g in groups for _ in range(2)]
    res = pl.pallas_call(
        body, name=name,
        out_shape=(*sems, *[pltpu.HBM(a.shape, a.dtype) for a in all_lands], jax.ShapeDtypeStruct((8, 128), F32)),
        in_specs=[HBM] * (ns + nl) + [ANY],
        out_specs=(*([SEM] * (2 * ng)), *([HBM] * nl), pl.BlockSpec(memory_space=pltpu.VMEM)),
        input_output_aliases={ns + i: 2 * ng + i for i in range(nl)},
        compiler_params=pltpu.CompilerParams(has_side_effects=EFFECT),
    )(*all_srcs, *all_lands, after)
    out, at_src, at_land = [], 0, 2 * ng
    for gi in range(ng):
        out.append((res[2 * gi], res[2 * gi + 1], all_srcs[at_src:at_src + n_src[gi]],
                    list(res[at_land:at_land + n_land[gi]])))
        at_src += n_src[gi]
        at_land += n_land[gi]
    return out, res[-1]


def _split_start(name, srcs, lands, plan, n_send, after):
    (group,), token = _split_start_groups(name, [(srcs, lands, plan, n_send)], after)
    return (*group, token)


def _split_wait(name, srcs, lands, send_sems, recv_sems, plan, after):
    n, nl = len(srcs), len(lands)
    afters = list(after) if isinstance(after, (list, tuple)) else [after]

    def body(*refs):
        src, land = refs[:n], refs[n:n + nl]
        send_sems, recv_sems = refs[n + nl], refs[n + nl + 1]
        x, y, c = lax.axis_index("x"), lax.axis_index("y"), lax.axis_index("c")
        _, sends = plan(x, y, c, src, land)
        for i, (s, _, peer, landing) in enumerate(sends):
            cp = pltpu.make_async_remote_copy(src_ref=s, dst_ref=landing, send_sem=send_sems.at[i],
                                              recv_sem=recv_sems.at[i], device_id=peer, device_id_type=MESH)
            cp.wait_send()
            cp.wait_recv()

    thru = [pltpu.HBM(a.shape, a.dtype) for a in lands]
    res = pl.pallas_call(
        body, name=name, out_shape=tuple(thru),
        in_specs=[HBM] * (n + nl) + [SEM, SEM] + [ANY] * len(afters), out_specs=tuple([HBM] * nl),
        input_output_aliases={n + i: i for i in range(nl)},
        compiler_params=pltpu.CompilerParams(has_side_effects=EFFECT),
    )(*srcs, *lands, send_sems, recv_sems, *afters)
    return list(res)


def _split_forward_groups(name, groups, after):
    n_src = [len(g[0]) for g in groups]
    n_land = [len(g[1]) for g in groups]
    all_srcs = [a for g in groups for a in g[0]]
    all_lands = [a for g in groups for a in g[1]]
    ns, nl, ng = len(all_srcs), len(all_lands), len(groups)

    def body(*refs):
        src_refs, land_refs = refs[:ns], refs[ns:ns + nl]
        sems_a = refs[ns + nl:ns + nl + 2 * ng]
        sems_b = refs[ns + nl + 2 * ng + 1:ns + nl + 4 * ng + 1]
        token = refs[-1]
        x, y, c = lax.axis_index("x"), lax.axis_index("y"), lax.axis_index("c")
        at_src = at_land = 0
        for gi, g in enumerate(groups):
            src, land = src_refs[at_src:at_src + n_src[gi]], land_refs[at_land:at_land + n_land[gi]]
            plan_a, plan_b = g[4], g[5]
            _, first = plan_a(x, y, c, src, land)
            for i, (s, _, peer, landing) in enumerate(first):
                cp = pltpu.make_async_remote_copy(src_ref=s, dst_ref=landing, send_sem=sems_a[2 * gi].at[i],
                                                  recv_sem=sems_a[2 * gi + 1].at[i], device_id=peer, device_id_type=MESH)
                cp.wait_send()
                cp.wait_recv()
            _, second = plan_b(x, y, c, src, land)
            for i, (s, d, peer, _) in enumerate(second):
                pltpu.make_async_remote_copy(src_ref=s, dst_ref=d, send_sem=sems_b[2 * gi].at[i],
                                             recv_sem=sems_b[2 * gi + 1].at[i], device_id=peer, device_id_type=MESH).start()
            at_src += n_src[gi]
            at_land += n_land[gi]
        token[...] = jnp.zeros_like(token)

    sems = [pltpu.SemaphoreType.DMA((g[6],)) for g in groups for _ in range(2)]
    res = pl.pallas_call(
        body, name=name,
        out_shape=(*sems, *[pltpu.HBM(a.shape, a.dtype) for a in all_lands], jax.ShapeDtypeStruct((8, 128), F32)),
        in_specs=[HBM] * (ns + nl) + [SEM] * (2 * ng) + [ANY],
        out_specs=(*([SEM] * (2 * ng)), *([HBM] * nl), pl.BlockSpec(memory_space=pltpu.VMEM)),
        input_output_aliases={ns + i: 2 * ng + i for i in range(nl)},
        compiler_params=pltpu.CompilerParams(has_side_effects=EFFECT),
    )(*all_srcs, *all_lands, *[s for g in groups for s in (g[2], g[3])], after)
    out, at_land = [], 2 * ng
    for gi in range(ng):
        out.append((res[2 * gi], res[2 * gi + 1], list(res[at_land:at_land + n_land[gi]])))
        at_land += n_land[gi]
    return out, res[-1]


def _split_forward(name, srcs, lands, send_a, recv_a, plan_a, plan_b, n_b, after):
    (group,), token = _split_forward_groups(name, [(srcs, lands, send_a, recv_a, plan_a, plan_b, n_b)], after)
    return (*group, token)


def _gather_plans(shapes):
    n = len(shapes)

    def halves(a, c):
        rows = shapes[a][0] // 2
        return pl.ds(pl.multiple_of(c * rows, 16), rows), pl.ds(pl.multiple_of((1 - c) * rows, 16), rows)

    def split(a):
        return shapes[a][0] % 32 == 0

    def plan_a(x, y, c, src, land):
        j_me = 2 * x + y
        sends = []
        for a in range(n):
            for (px, py) in _chip_relations(x, y):
                if split(a):
                    mine, _ = halves(a, c)
                    sends.append((src[a].at[mine], land[a].at[j_me, mine], (px, py, c), land[a].at[2 * px + py, mine]))
                else:
                    sends.append((src[a], land[a].at[j_me], (px, py, c), land[a].at[2 * px + py]))
        return [], sends

    def plan_b(x, y, c, src, land):
        sends = []
        for a in range(n):
            if split(a):
                mine, other = halves(a, c)
                for (px, py) in _chip_relations(x, y):
                    j = 2 * px + py
                    sends.append((land[a].at[j, mine], land[a].at[j, mine], (x, y, 1 - c), land[a].at[j, other]))
        return [], sends

    n_b = 3 * sum(1 for a in range(n) if split(a))
    return plan_a, plan_b, n_b


def _allgather_plan(n):
    flips = [(dx, dy, dc) for dx in (0, 1) for dy in (0, 1) for dc in (0, 1) if dx or dy or dc]

    def plan(x, y, c, src, land):
        sends = []
        for a in range(n):
            for dx, dy, dc in flips:
                px, py, pc = x ^ dx, y ^ dy, c ^ dc
                sends.append((src[a], land[a].at[4 * x + 2 * y + c], (px, py, pc), land[a].at[4 * px + 2 * py + pc]))
        return [], sends

    return plan


def _scatter_plan(n):
    def plan(x, y, c, src, land):
        sends = []
        for a in range(n):
            for k, (px, py) in enumerate(_chip_relations(x, y)):
                sends.append((src[a].at[2 * px + py], land[a].at[k], (px, py, c), land[a].at[k]))
        return [], sends

    return plan


def _rms(x):
    r = lax.rsqrt(jnp.mean(x * x, axis=-1, keepdims=True) + EPS)
    return x * r, r


def _rms_bwd(dy, n, r, g):
    dg = jnp.sum(dy * n, axis=0, keepdims=True)
    dn = dy * g
    dx = r * (dn - n * jnp.mean(dn * n, axis=-1, keepdims=True))
    return dx, dg


def _ln(x):
    mu = jnp.mean(x, axis=-1, keepdims=True)
    xc = x - mu
    rstd = lax.rsqrt(jnp.mean(xc * xc, axis=-1, keepdims=True) + EPS)
    return xc * rstd, rstd


def _ln_bwd(dy, xhat, rstd, g):
    dg = jnp.sum(dy * xhat, axis=0, keepdims=True)
    db = jnp.sum(dy, axis=0, keepdims=True)
    dxh = dy * g
    dx = rstd * (dxh - jnp.mean(dxh, axis=-1, keepdims=True) - xhat * jnp.mean(dxh * xhat, axis=-1, keepdims=True))
    return dx, dg, db


def _sigmoid(x):
    return jax.nn.sigmoid(x)


def _dsilu(x, s):
    return s * (1.0 + x * (1.0 - s))


def _adam(w, g, m, v):
    m = B1 * m + (1.0 - B1) * g
    v = B2 * v + (1.0 - B2) * (g * g)
    m_hat = m / (1.0 - B1 ** STEP)
    v_hat = v / (1.0 - B2 ** STEP)
    delta = -LR * (m_hat / (jnp.sqrt(v_hat) + EPS_A) + WD * w)
    return delta, m, v


def _head_mask(shape):
    lane = lax.broadcasted_iota(jnp.int32, shape, len(shape) - 1)
    return [(lane >= h * HD) & (lane < (h + 1) * HD) for h in range(NH)]


def _first(b, i):
    return jnp.logical_and(b == 0, i == 0)


def _acc(ref, val, first):
    @pl.when(first)
    def _():
        ref[...] = val

    @pl.when(jnp.logical_not(first))
    def _():
        ref[...] += val


def _ada_fwd(c_all, w_sh, b_sh):
    nb = c_all.shape[0]
    tn = 768

    def body(c_ref, w_ref, b_ref, o_ref):
        cv = c_ref[...]
        cs = (cv * _sigmoid(cv)).astype(BF16)
        o_ref[...] = _dot(cs, w_ref[...].astype(BF16)) + b_ref[...]

    return pl.pallas_call(
        body, name="ada_fwd", grid=(ADA_SH // tn,),
        out_shape=jax.ShapeDtypeStruct((nb, ADA_SH), F32),
        in_specs=[pl.BlockSpec((nb, D), lambda j: (0, 0)), pl.BlockSpec((D, tn), lambda j: (0, j)),
                  pl.BlockSpec((1, tn), lambda j: (0, j))],
        out_specs=pl.BlockSpec((nb, tn), lambda j: (0, j)),
        compiler_params=_cparams(),
    )(c_all, w_sh, b_sh)


def _ada_bwd_adam(c_all, dada_sh, w, m, v):
    nb = c_all.shape[0]
    tn = 768

    def body(c_ref, d_ref, w_ref, m_ref, v_ref, g_out, d_out, m_out, v_out):
        cv = c_ref[...]
        cs = (cv * _sigmoid(cv)).astype(BF16)
        g = _dot_tn(cs, d_ref[...].astype(BF16))
        delta, m2, v2 = _adam(w_ref[...], g, m_ref[...], v_ref[...])
        g_out[...] = g
        d_out[...] = delta
        m_out[...] = m2
        v_out[...] = v2

    big = pl.BlockSpec((D, tn), lambda j: (0, j))
    shape = jax.ShapeDtypeStruct((D, ADA_SH), F32)
    return pl.pallas_call(
        body, name="ada_bwd_adam", grid=(ADA_SH // tn,),
        out_shape=[shape] * 4,
        in_specs=[pl.BlockSpec((nb, D), lambda j: (0, 0)), pl.BlockSpec((nb, tn), lambda j: (0, j)), big, big, big],
        out_specs=[big] * 4,
        compiler_params=_cparams(),
    )(c_all, dada_sh, w, m, v)


def _tok_specs(tm, width):
    return pl.BlockSpec((1, tm, width), lambda b, i: (b, i, 0))


def _mod_spec():
    return pl.BlockSpec((1, 1, D), lambda b, i: (b, 0, 0))


def _row_spec(width=D):
    return pl.BlockSpec((1, width), lambda b, i: (0, 0))


def _ffn_loss_fwd(x, sh, sc, gt, g_pre, g_post, w_in4, w_out, target):
    nb, s, _ = x.shape
    tm = min(TM, s)

    def body(x_ref, sh_ref, sc_ref, gt_ref, gpre_ref, gpost_ref, win_ref, wout_ref, tgt_ref,
             xo_ref, df_ref, p_ref, ls_ref, dgpost_ref, dgt_ref):
        xv = x_ref[0]
        n, _ = _rms(xv)
        h = (n * gpre_ref[...]) * (1.0 + sc_ref[0]) + sh_ref[0]
        hb = h.astype(BF16)
        acc = jnp.zeros((tm, D), F32)
        for j in range(2):
            gate = _dot(hb, win_ref[j])
            up = _dot(hb, win_ref[2 + j])
            p_ref[0, :, j * FBLK:(j + 1) * FBLK] = gate.astype(BF16)
            p_ref[0, :, DFF + j * FBLK:DFF + (j + 1) * FBLK] = up.astype(BF16)
            a = (gate * _sigmoid(gate)) * up
            acc = acc + _dot(a.astype(BF16), wout_ref[j * FBLK:(j + 1) * FBLK, :])
        nf, q = _rms(acc)
        gpost = gpost_ref[...]
        half_gate = 0.5 * gt_ref[0]
        out = xv + half_gate * (nf * gpost)
        first = _first(pl.program_id(0), pl.program_id(1))
        err = out - tgt_ref[0]
        dout = err * (1.0 / D)
        xo_ref[0] = dout
        row = jnp.sum(err * err, axis=0, keepdims=True)
        part = row[:, 0:128]
        for k in range(1, D // 128):
            part = part + row[:, k * 128:(k + 1) * 128]
        _acc(ls_ref, part, first)
        df, dgpost = _rms_bwd(dout * half_gate, nf, q, gpost)
        df_ref[0] = df.astype(BF16)
        _acc(dgpost_ref, dgpost, first)
        _acc(dgt_ref, jnp.sum(dout * (0.5 * (nf * gpost)), axis=0, keepdims=True)[None], pl.program_id(1) == 0)

    tok = _tok_specs(tm, D)
    return pl.pallas_call(
        body, name="ffn_loss_fwd", grid=(nb, s // tm),
        out_shape=[jax.ShapeDtypeStruct((nb, s, D), F32), jax.ShapeDtypeStruct((nb, s, D), BF16),
                   jax.ShapeDtypeStruct((nb, s, 2 * DFF), BF16), jax.ShapeDtypeStruct((1, 128), F32),
                   jax.ShapeDtypeStruct((1, D), F32), jax.ShapeDtypeStruct((nb, 1, D), F32)],
        in_specs=[tok, _mod_spec(), _mod_spec(), _mod_spec(), _row_spec(), _row_spec(), VMEM_FULL, VMEM_FULL, tok],
        out_specs=[tok, tok, _tok_specs(tm, 2 * DFF), pl.BlockSpec((1, 128), lambda b, i: (0, 0)), _row_spec(), _mod_spec()],
        compiler_params=_cparams(),
    )(x, sh, sc, gt, g_pre, g_post, w_in4, w_out, target)


def _ffn_up(x, sh, sc, g_pre, w_in4):
    nb, s, _ = x.shape
    tm = min(TM, s)

    def body(x_ref, sh_ref, sc_ref, gpre_ref, win_ref, p_ref, a_ref):
        n, _ = _rms(x_ref[0])
        hb = ((n * gpre_ref[...]) * (1.0 + sc_ref[0]) + sh_ref[0]).astype(BF16)
        for j in range(2):
            gate = _dot(hb, win_ref[j])
            up = _dot(hb, win_ref[2 + j])
            p_ref[0, :, j * FBLK:(j + 1) * FBLK] = gate.astype(BF16)
            p_ref[0, :, DFF + j * FBLK:DFF + (j + 1) * FBLK] = up.astype(BF16)
            a_ref[0, :, j * FBLK:(j + 1) * FBLK] = ((gate * _sigmoid(gate)) * up).astype(BF16)

    return pl.pallas_call(
        body, name="ffn_up", grid=(nb, s // tm),
        out_shape=[jax.ShapeDtypeStruct((nb, s, 2 * DFF), BF16), jax.ShapeDtypeStruct((nb, s, DFF), BF16)],
        in_specs=[_tok_specs(tm, D), _mod_spec(), _mod_spec(), _row_spec(), VMEM_FULL],
        out_specs=[_tok_specs(tm, 2 * DFF), _tok_specs(tm, DFF)],
        compiler_params=_cparams(),
    )(x, sh, sc, g_pre, w_in4)


def _ffn_down(x, a, gt, g_post, w_out):
    nb, s, _ = x.shape
    tm = min(TM, s)

    def body(x_ref, a_ref, gt_ref, gpost_ref, wout_ref, xo_ref, f_ref):
        acc = _dot(a_ref[0], wout_ref[...])
        f_ref[0] = acc
        nf, _ = _rms(acc)
        xo_ref[0] = x_ref[0] + (0.5 * gt_ref[0]) * (nf * gpost_ref[...])

    tok = _tok_specs(tm, D)
    shape = jax.ShapeDtypeStruct((nb, s, D), F32)
    return pl.pallas_call(
        body, name="ffn_down", grid=(nb, s // tm), out_shape=[shape, shape],
        in_specs=[tok, _tok_specs(tm, DFF), _mod_spec(), _row_spec(), VMEM_FULL],
        out_specs=[tok, tok],
        compiler_params=_cparams(),
    )(x, a, gt, g_post, w_out)


def _ffn_bwd(dxo, x, f, p, sh, sc, gt, g_pre, g_post, w_in4, w_out, df=None):
    nb, s, _ = x.shape
    tm = min(TM_FFN_BWD, s)
    given = df is not None

    def body(*refs):
        if given:
            (dxo_ref, x_ref, dfin_ref, p_ref, sh_ref, sc_ref, gpre_ref, win_ref, wout_ref,
             dx_ref, dp_ref, h_ref, a_ref, dgpre_ref, dsh_ref, dsc_ref) = refs
        else:
            (dxo_ref, x_ref, f_ref, p_ref, sh_ref, sc_ref, gt_ref, gpre_ref, gpost_ref, win_ref, wout_ref,
             dx_ref, dp_ref, h_ref, a_ref, df_ref, dgpre_ref, dgpost_ref, dsh_ref, dsc_ref, dgt_ref) = refs
        b, i = pl.program_id(0), pl.program_id(1)
        dxo_v = dxo_ref[0]
        if given:
            dfb = dfin_ref[0]
        else:
            nf, q = _rms(f_ref[0])
            gpost = gpost_ref[...]
            dgt = jnp.sum(dxo_v * (0.5 * (nf * gpost)), axis=0, keepdims=True)
            do = dxo_v * (0.5 * gt_ref[0])
            dfv, dgpost = _rms_bwd(do, nf, q, gpost)
            dfb = dfv.astype(BF16)
            df_ref[0] = dfb
        xv = x_ref[0]
        n, r = _rms(xv)
        gpre = gpre_ref[...]
        ng = n * gpre
        scale1 = 1.0 + sc_ref[0]
        h = ng * scale1 + sh_ref[0]
        h_ref[0] = h.astype(BF16)
        dh = jnp.zeros((tm, D), F32)
        for j in range(2):
            gate = p_ref[0, :, j * FBLK:(j + 1) * FBLK].astype(F32)
            up = p_ref[0, :, DFF + j * FBLK:DFF + (j + 1) * FBLK].astype(F32)
            sg = _sigmoid(gate)
            act = gate * sg
            a_ref[0, :, j * FBLK:(j + 1) * FBLK] = (act * up).astype(BF16)
            da = _dot_nt(dfb, wout_ref[j * FBLK:(j + 1) * FBLK, :])
            dgate = (da * up * _dsilu(gate, sg)).astype(BF16)
            dup = (da * act).astype(BF16)
            dp_ref[0, :, j * FBLK:(j + 1) * FBLK] = dgate
            dp_ref[0, :, DFF + j * FBLK:DFF + (j + 1) * FBLK] = dup
            dh = dh + _dot_nt(dgate, win_ref[j]) + _dot_nt(dup, win_ref[2 + j])
        dsh = jnp.sum(dh, axis=0, keepdims=True)
        dsc = jnp.sum(dh * ng, axis=0, keepdims=True)
        dxn, dgpre = _rms_bwd(dh * scale1, n, r, gpre)
        dx_ref[0] = dxo_v + dxn
        _acc(dgpre_ref, dgpre, _first(b, i))
        _acc(dsh_ref, dsh[None], i == 0)
        _acc(dsc_ref, dsc[None], i == 0)
        if not given:
            _acc(dgpost_ref, dgpost, _first(b, i))
            _acc(dgt_ref, dgt[None], i == 0)

    tok = _tok_specs(tm, D)
    mod_shape = jax.ShapeDtypeStruct((nb, 1, D), F32)
    row_shape = jax.ShapeDtypeStruct((1, D), F32)
    big = [jax.ShapeDtypeStruct((nb, s, D), F32), jax.ShapeDtypeStruct((nb, s, 2 * DFF), BF16),
           jax.ShapeDtypeStruct((nb, s, D), BF16), jax.ShapeDtypeStruct((nb, s, DFF), BF16)]
    big_specs = [tok, _tok_specs(tm, 2 * DFF), tok, _tok_specs(tm, DFF)]
    if given:
        return pl.pallas_call(
            body, name="ffn_bwd_after_loss", grid=(nb, s // tm),
            out_shape=big + [row_shape, mod_shape, mod_shape],
            in_specs=[tok, tok, tok, _tok_specs(tm, 2 * DFF), _mod_spec(), _mod_spec(), _row_spec(), VMEM_FULL, VMEM_FULL],
            out_specs=big_specs + [_row_spec(), _mod_spec(), _mod_spec()],
            compiler_params=_cparams(),
        )(dxo, x, df, p, sh, sc, g_pre, w_in4, w_out)
    return pl.pallas_call(
        body, name="ffn_bwd", grid=(nb, s // tm),
        out_shape=big + [jax.ShapeDtypeStruct((nb, s, D), BF16), row_shape, row_shape, mod_shape, mod_shape, mod_shape],
        in_specs=[tok, tok, tok, _tok_specs(tm, 2 * DFF), _mod_spec(), _mod_spec(), _mod_spec(), _row_spec(), _row_spec(),
                  VMEM_FULL, VMEM_FULL],
        out_specs=big_specs + [tok, _row_spec(), _row_spec(), _mod_spec(), _mod_spec(), _mod_spec()],
        compiler_params=_cparams(),
    )(dxo, x, f, p, sh, sc, gt, g_pre, g_post, w_in4, w_out)


def _wgrad(name, a, b, col_block, chip_major):
    t, ka = a.shape
    n = b.shape[1]
    def vmem_bytes(rows):
        return 2 * 2 * rows * (ka + col_block) + 4 * ka * col_block + 2 * (4 + 2) * ka * col_block

    tk = min(t, 512)
    while tk * 2 <= t and t % (tk * 2) == 0 and vmem_bytes(tk * 2) <= WGRAD_VMEM_BUDGET:
        tk *= 2
    nk = t // tk
    nblk = n // col_block

    def body(a_ref, b_ref, o_ref, obf_ref, acc_ref):
        k = pl.program_id(1)

        @pl.when(k == 0)
        def _():
            acc_ref[...] = jnp.zeros_like(acc_ref)

        acc_ref[...] += _dot_tn(a_ref[...], b_ref[...])

        @pl.when(k == nk - 1)
        def _():
            val = acc_ref[...]
            if chip_major:
                o_ref[0] = val
                obf_ref[0] = val.astype(BF16)
            else:
                o_ref[...] = val
                obf_ref[...] = val.astype(BF16)

    if chip_major:
        shape = (nblk, ka, col_block)
        ospec = pl.BlockSpec((1, ka, col_block), lambda j, k: (j, 0, 0))
    else:
        shape = (ka, n)
        ospec = pl.BlockSpec((ka, col_block), lambda j, k: (0, j))
    return pl.pallas_call(
        body, name=name, grid=(nblk, nk),
        out_shape=[jax.ShapeDtypeStruct(shape, F32), jax.ShapeDtypeStruct(shape, BF16)],
        in_specs=[pl.BlockSpec((tk, ka), lambda j, k: (k, 0)), pl.BlockSpec((tk, col_block), lambda j, k: (k, j))],
        out_specs=[ospec, ospec],
        scratch_shapes=[pltpu.VMEM((ka, col_block), F32)],
        compiler_params=_cparams(),
    )(a, b)


def _mix_in_fwd(x, sh, sc, g_pre, w_mi4):
    nb, s, _ = x.shape
    tm = min(TM, s)

    def body(x_ref, sh_ref, sc_ref, gpre_ref, w_ref, u_ref, v_ref, a_ref, g_ref):
        n, _ = _rms(x_ref[0])
        hb = ((n * gpre_ref[...]) * (1.0 + sc_ref[0]) + sh_ref[0]).astype(BF16)
        for k, o_ref in enumerate((u_ref, v_ref, a_ref, g_ref)):
            o_ref[0] = _dot(hb, w_ref[k])

    shape = jax.ShapeDtypeStruct((nb, s, WA), F32)
    return pl.pallas_call(
        body, name="mix_in_fwd", grid=(nb, s // tm),
        out_shape=[shape] * 4,
        in_specs=[_tok_specs(tm, D), _mod_spec(), _mod_spec(), _row_spec(), VMEM_FULL],
        out_specs=[_tok_specs(tm, WA)] * 4,
        compiler_params=_cparams(),
    )(x, sh, sc, g_pre, w_mi4)


def _spatial_weights(wcat_ref, transposed):
    w = wcat_ref[...]
    row = lax.broadcasted_iota(jnp.int32, w.shape, 0)
    col = lax.broadcasted_iota(jnp.int32, w.shape, 1)
    keep = ((row & (CH - 1)) <= col) if transposed else ((col & (CH - 1)) <= row)
    return jnp.where(keep, w, 0.0).astype(BF16)


def _expand_heads(vc, masks):
    return jnp.concatenate([jnp.where(mk, vc, jnp.zeros_like(vc)) for mk in masks], axis=0)


def _spatial_bias(bspt_ref):
    return bspt_ref[...]


SHIFTS = 8
TAP_ROWS = 32


def _ext_rows(tm):
    return tm + HALO + SHIFTS


def _make_shifts(ext_ref, sh_ref, tm):
    ext_ref[tm + HALO:tm + HALO + SHIFTS, :] = jnp.zeros((SHIFTS, WB), F32)
    for r in range(SHIFTS):
        sh_ref[r] = ext_ref[r:r + tm + HALO, :]


def _conv_taps(sh_ref, w_ref, tm, taps, emit):
    def block(i, carry):
        r0 = pl.multiple_of(i * TAP_ROWS, TAP_ROWS)
        acc = jnp.zeros((TAP_ROWS, WB), F32)
        for o, k in taps:
            acc = acc + w_ref[k:k + 1, :] * sh_ref[o % SHIFTS, pl.ds(r0 + SHIFTS * (o // SHIFTS), TAP_ROWS), :]
        emit(r0, acc)
        return carry

    lax.fori_loop(0, tm // TAP_ROWS, block, 0)


def _halo_prev_spec(tm):
    return pl.BlockSpec((1, HALO, WB), lambda b, i: (b, jnp.maximum(i * (tm // HALO) - 1, 0), 0))


def _halo_next_spec(tm, s):
    return pl.BlockSpec((1, HALO, WB), lambda b, i: (b, jnp.minimum((i + 1) * (tm // HALO), s // HALO - 1), 0))


def _mix_mid_fwd(x, u, v, a, g, gt, gn_g, gn_b, wcat, bspt, conv_w, conv_b, cn_g, cn_b, go_a, go_b, w_mo, g_post):
    nb, s, _ = x.shape
    tm = min(TM, s)

    def body(x_ref, u_ref, v_ref, a_ref, g_ref, ah_ref, gh_ref, gt_ref, gng_ref, gnb_ref, wcat_ref, bspt_ref,
             cw_ref, cb_ref, cng_ref, cnb_ref, goa_ref, gob_ref, wmo_ref, gpost_ref,
             xo_ref, conv_ref, y_ref, m_ref, ext_ref, sh_ref):
        i = pl.program_id(1)
        xhat, _ = _ln(v_ref[0])
        vb = (xhat * gng_ref[...] + gnb_ref[...]).astype(BF16)
        wsb = _spatial_weights(wcat_ref, False)
        bias = _spatial_bias(bspt_ref)
        masks = _head_mask((CH, WA))
        zs = []
        for cidx in range(tm // CH):
            vexp = _expand_heads(vb[cidx * CH:(cidx + 1) * CH, :], masks)
            zs.append(_dot(wsb, vexp) + bias)
        z = jnp.concatenate(zs, axis=0)
        na, _ = _rms(u_ref[0] * z)
        keep = jnp.where(i == 0, 0.0, 1.0).astype(F32)
        ext_ref[0:HALO, :] = (ah_ref[0] * _sigmoid(gh_ref[0])) * keep
        ext_ref[HALO:HALO + tm, :] = a_ref[0] * _sigmoid(g_ref[0])
        _make_shifts(ext_ref, sh_ref, tm)
        cb = cb_ref[...]

        def put_conv(r0, acc):
            conv_ref[0, pl.ds(r0, TAP_ROWS), :] = acc + cb

        _conv_taps(sh_ref, cw_ref, tm, [(k + HALO - (CK - 1), k) for k in range(CK)], put_conv)
        conv = conv_ref[0]
        chat, _ = _ln(conv)
        cln = chat * cng_ref[...] + cnb_ref[...]
        nbb, _ = _rms(cln * _sigmoid(cln))
        yb = jnp.concatenate([na * goa_ref[...], nbb * gob_ref[...]], axis=1).astype(BF16)
        y_ref[0] = yb
        m = _dot(yb, wmo_ref[...])
        m_ref[0] = m
        nm, _ = _rms(m)
        xo_ref[0] = x_ref[0] + gt_ref[0] * (nm * gpost_ref[...])

    t5 = _tok_specs(tm, WA)
    tok = _tok_specs(tm, D)
    r5 = _row_spec(WA)
    full = lambda shape: pl.BlockSpec(shape, lambda b, i: (0,) * len(shape))
    return pl.pallas_call(
        body, name="mix_mid_fwd", grid=(nb, s // tm),
        out_shape=[jax.ShapeDtypeStruct((nb, s, D), F32), jax.ShapeDtypeStruct((nb, s, WB), F32),
                   jax.ShapeDtypeStruct((nb, s, D), BF16), jax.ShapeDtypeStruct((nb, s, D), F32)],
        in_specs=[tok, t5, t5, t5, t5, _halo_prev_spec(tm), _halo_prev_spec(tm), _mod_spec(), r5, r5,
                  full((CH, NH * CH)), full((CH, WA)), full((HALO, WB)), r5, r5, r5, r5, r5, VMEM_FULL, _row_spec()],
        out_specs=[tok, t5, tok, tok],
        scratch_shapes=[pltpu.VMEM((_ext_rows(tm), WB), F32), pltpu.VMEM((SHIFTS, tm + HALO, WB), F32)],
        compiler_params=_cparams(),
    )(x, u, v, a, g, a, g, gt, gn_g, gn_b, wcat, bspt, conv_w, conv_b, cn_g, cn_b, go_a, go_b, w_mo, g_post)


def _mix_out_bwd(dxo, m, gt, g_post, w_mo):
    nb, s, _ = m.shape
    tm = min(TM, s)

    def body(dxo_ref, m_ref, gt_ref, gpost_ref, wmo_ref, dy_ref, dm_ref, dgpost_ref, dgt_ref):
        b, i = pl.program_id(0), pl.program_id(1)
        dxo_v = dxo_ref[0]
        nm, q = _rms(m_ref[0])
        gpost = gpost_ref[...]
        dgt = jnp.sum(dxo_v * (nm * gpost), axis=0, keepdims=True)
        dm, dgpost = _rms_bwd(dxo_v * gt_ref[0], nm, q, gpost)
        dmb = dm.astype(BF16)
        dm_ref[0] = dmb
        dy_ref[0] = _dot_nt(dmb, wmo_ref[...])
        _acc(dgpost_ref, dgpost, _first(b, i))
        _acc(dgt_ref, dgt[None], i == 0)

    tok = _tok_specs(tm, D)
    return pl.pallas_call(
        body, name="mix_out_bwd", grid=(nb, s // tm),
        out_shape=[jax.ShapeDtypeStruct((nb, s, D), F32), jax.ShapeDtypeStruct((nb, s, D), BF16),
                   jax.ShapeDtypeStruct((1, D), F32), jax.ShapeDtypeStruct((nb, 1, D), F32)],
        in_specs=[tok, tok, _mod_spec(), _row_spec(), VMEM_FULL],
        out_specs=[tok, tok, _row_spec(), _mod_spec()],
        compiler_params=_cparams(),
    )(dxo, m, gt, g_post, w_mo)


def _mix_mid_bwd(dy, u, v, conv, gn_g, gn_b, wcat, wcat_t, bspt, cn_g, cn_b, go_a, go_b):
    nb, s, _ = dy.shape
    tm = min(TM, s)
    nchunk = tm // CH

    def body(dy_ref, u_ref, v_ref, conv_ref, gng_ref, gnb_ref, wcat_ref, wcatt_ref, bspt_ref, cng_ref, cnb_ref,
             goa_ref, gob_ref,
             du_ref, dv_ref, dconv_ref, dwcat_ref, dbsp_ref, dgng_ref, dgnb_ref, dgoa_ref, dgob_ref,
             dcng_ref, dcnb_ref, dcb_ref):
        first = _first(pl.program_id(0), pl.program_id(1))
        dyv = dy_ref[0]
        xhat, rstd = _ln(v_ref[0])
        gng = gng_ref[...]
        vb = (xhat * gng + gnb_ref[...]).astype(BF16)
        wsb = _spatial_weights(wcat_ref, False)
        wsb_t = _spatial_weights(wcatt_ref, True)
        bias = _spatial_bias(bspt_ref)
        masks = _head_mask((CH, WA))
        vexps, zs = [], []
        for cidx in range(nchunk):
            vexp = _expand_heads(vb[cidx * CH:(cidx + 1) * CH, :], masks)
            vexps.append(vexp)
            zs.append(_dot(wsb, vexp) + bias)
        z = jnp.concatenate(zs, axis=0)
        uv = u_ref[0]
        na, ra = _rms(uv * z)
        dya, dgoa = _rms_bwd(dyv[:, 0:WA], na, ra, goa_ref[...])
        du_ref[0] = dya * z
        dz = dya * uv
        dwcat = jnp.zeros((CH, NH * CH), F32)
        dzsum = jnp.zeros((CH, WA), F32)
        dvlns = []
        for cidx in range(nchunk):
            dzc = dz[cidx * CH:(cidx + 1) * CH, :]
            dzsum = dzsum + dzc
            dzb = dzc.astype(BF16)
            dwcat = dwcat + _dot_nt(dzb, vexps[cidx])
            dvexp = _dot(wsb_t, dzb)
            dvl = jnp.zeros((CH, WA), F32)
            for h in range(NH):
                dvl = dvl + jnp.where(masks[h], dvexp[h * CH:(h + 1) * CH, :], 0.0)
            dvlns.append(dvl)
        dvln = jnp.concatenate(dvlns, axis=0)
        dv, dgng, dgnb = _ln_bwd(dvln, xhat, rstd, gng)
        dv_ref[0] = dv
        lane = lax.broadcasted_iota(jnp.int32, (NH, WA), 1)
        head = lax.broadcasted_iota(jnp.int32, (NH, WA), 0)
        sel = jnp.where((lane >= head * HD) & (lane < (head + 1) * HD), 1.0, 0.0).astype(F32)
        dbsp = lax.dot_general(sel, dzsum, NT, preferred_element_type=F32, precision=lax.Precision.HIGHEST)
        chat, crstd = _ln(conv_ref[0])
        cng = cng_ref[...]
        cln = chat * cng + cnb_ref[...]
        sg = _sigmoid(cln)
        nbb, rb = _rms(cln * sg)
        dyb, dgob = _rms_bwd(dyv[:, WA:D], nbb, rb, gob_ref[...])
        dconv, dcng, dcnb = _ln_bwd(dyb * _dsilu(cln, sg), chat, crstd, cng)
        dconv_ref[0] = dconv
        dcb = jnp.sum(dconv, axis=0, keepdims=True)
        for ref, val in ((dwcat_ref, dwcat), (dbsp_ref, dbsp), (dgng_ref, dgng), (dgnb_ref, dgnb), (dgoa_ref, dgoa),
                         (dgob_ref, dgob), (dcng_ref, dcng), (dcnb_ref, dcnb), (dcb_ref, dcb)):
            _acc(ref, val, first)

    t5 = _tok_specs(tm, WA)
    r5 = _row_spec(WA)
    full = lambda shape: pl.BlockSpec(shape, lambda b, i: (0,) * len(shape))
    big = jax.ShapeDtypeStruct((nb, s, WA), F32)
    row = jax.ShapeDtypeStruct((1, WA), F32)
    return pl.pallas_call(
        body, name="mix_mid_bwd", grid=(nb, s // tm),
        out_shape=[big, big, big, jax.ShapeDtypeStruct((CH, NH * CH), F32), jax.ShapeDtypeStruct((NH, CH), F32),
                   row, row, row, row, row, row, row],
        in_specs=[_tok_specs(tm, D), t5, t5, t5, r5, r5, full((CH, NH * CH)), full((NH * CH, CH)), full((CH, WA)),
                  r5, r5, r5, r5],
        out_specs=[t5, t5, t5, full((CH, NH * CH)), full((NH, CH)), r5, r5, r5, r5, r5, r5, r5],
        compiler_params=_cparams(),
    )(dy, u, v, conv, gn_g, gn_b, wcat, wcat_t, bspt, cn_g, cn_b, go_a, go_b)


def _mix_in_bwd(dxo, x, du, dv, dconv, a, g, sh, sc, g_pre, w_mi4, conv_w):
    nb, s, _ = x.shape
    tm = min(TM, s)
    n_i = s // tm

    def body(dxo_ref, x_ref, du_ref, dv_ref, dc_ref, dch_ref, a_ref, g_ref, ah_ref, gh_ref, sh_ref, sc_ref,
             gpre_ref, w_ref, cw_ref,
             dx_ref, dproj_ref, h_ref, dgpre_ref, dsh_ref, dsc_ref, dcw_ref, ext_ref, shf_ref, dglu_ref):
        b, i = pl.program_id(0), pl.program_id(1)
        first = _first(b, i)
        av, gv = a_ref[0], g_ref[0]
        sg = _sigmoid(gv)
        dconv = dc_ref[0]
        ext_ref[0:tm, :] = dconv
        ext_ref[tm:tm + HALO, :] = dch_ref[0] * jnp.where(i == n_i - 1, 0.0, 1.0).astype(F32)
        _make_shifts(ext_ref, shf_ref, tm)

        def put_dglu(r0, acc):
            dglu_ref[pl.ds(r0, TAP_ROWS), :] = acc

        _conv_taps(shf_ref, cw_ref, tm, [(CK - 1 - k, k) for k in range(CK)], put_dglu)
        dglu = dglu_ref[...]
        ext_ref[0:HALO, :] = (ah_ref[0] * _sigmoid(gh_ref[0])) * jnp.where(i == 0, 0.0, 1.0).astype(F32)
        ext_ref[HALO:HALO + tm, :] = av * sg
        _make_shifts(ext_ref, shf_ref, tm)

        @pl.when(first)
        def _():
            dcw_ref[...] = jnp.zeros((HALO, WB), F32)

        for k in range(CK):
            o = k + HALO - (CK - 1)
            lo = SHIFTS * (o // SHIFTS)
            dcw_ref[k:k + 1, :] += jnp.sum(dconv * shf_ref[o % SHIFTS, lo:lo + tm, :], axis=0, keepdims=True)
        da = dglu * sg
        dg = dglu * av * (sg * (1.0 - sg))
        parts = [du_ref[0].astype(BF16), dv_ref[0].astype(BF16), da.astype(BF16), dg.astype(BF16)]
        dh = jnp.zeros((tm, D), F32)
        for k in range(4):
            dproj_ref[0, :, k * WA:(k + 1) * WA] = parts[k]
            dh = dh + _dot_nt(parts[k], w_ref[k])
        n, r = _rms(x_ref[0])
        gpre = gpre_ref[...]
        ng = n * gpre
        scale1 = 1.0 + sc_ref[0]
        h_ref[0] = (ng * scale1 + sh_ref[0]).astype(BF16)
        dsh = jnp.sum(dh, axis=0, keepdims=True)
        dsc = jnp.sum(dh * ng, axis=0, keepdims=True)
        dxn, dgpre = _rms_bwd(dh * scale1, n, r, gpre)
        dx_ref[0] = dxo_ref[0] + dxn
        _acc(dgpre_ref, dgpre, first)
        _acc(dsh_ref, dsh[None], i == 0)
        _acc(dsc_ref, dsc[None], i == 0)

    tok = _tok_specs(tm, D)
    t5 = _tok_specs(tm, WA)
    full = lambda shape: pl.BlockSpec(shape, lambda b, i: (0,) * len(shape))
    mod_shape = jax.ShapeDtypeStruct((nb, 1, D), F32)
    return pl.pallas_call(
        body, name="mix_in_bwd", grid=(nb, n_i),
        out_shape=[jax.ShapeDtypeStruct((nb, s, D), F32), jax.ShapeDtypeStruct((nb, s, 4 * WA), BF16),
                   jax.ShapeDtypeStruct((nb, s, D), BF16), jax.ShapeDtypeStruct((1, D), F32), mod_shape, mod_shape,
                   jax.ShapeDtypeStruct((HALO, WB), F32)],
        in_specs=[tok, tok, t5, t5, t5, _halo_next_spec(tm, s), t5, t5, _halo_prev_spec(tm), _halo_prev_spec(tm),
                  _mod_spec(), _mod_spec(), _row_spec(), VMEM_FULL, full((HALO, WB))],
        out_specs=[tok, _tok_specs(tm, 4 * WA), tok, _row_spec(), _mod_spec(), _mod_spec(), full((HALO, WB))],
        scratch_shapes=[pltpu.VMEM((_ext_rows(tm), WB), F32), pltpu.VMEM((SHIFTS, tm + HALO, WB), F32),
                        pltpu.VMEM((tm, WB), F32)],
        compiler_params=_cparams(),
    )(dxo, x, du, dv, dconv, dconv, a, g, a, g, sh, sc, g_pre, w_mi4, conv_w)


def _row_tile(rows, cols):
    best = 16
    for t in range(16, rows + 1, 16):
        if rows % t == 0 and t * cols * 4 <= 1536 * 1024:
            best = t
    return best


def _walk(steps):
    offs = [sum(steps[:k]) for k in range(len(steps))]

    def tile(k):
        return lambda i: jnp.clip(i - offs[k], 0, steps[k] - 1)

    def mine(k, i):
        return jnp.logical_and(i >= offs[k], i < offs[k] + steps[k])

    return sum(steps), tile, mine


def _sum4(name, own4s, recvs, j_arr):
    n = len(own4s)
    shapes = [o.shape[1:] for o in own4s]
    trs = [_row_tile(r, c) for r, c in shapes]
    total, tile, mine = _walk([r // tr for (r, _), tr in zip(shapes, trs)])

    def body(j_ref, *refs):
        del j_ref
        i = pl.program_id(0)
        for k in range(n):
            own_ref, recv_ref, o_ref = refs[2 * k], refs[2 * k + 1], refs[2 * n + k]

            def add(own_ref=own_ref, recv_ref=recv_ref, o_ref=o_ref):
                acc = own_ref[0]
                for q in range(3):
                    acc = acc + recv_ref[q].astype(F32)
                o_ref[...] = acc

            pl.when(mine(k, i))(add)

    in_specs, out_specs = [], []
    for k, ((_, cols), tr) in enumerate(zip(shapes, trs)):
        in_specs += [pl.BlockSpec((1, tr, cols), lambda i, j, t=tile(k): (j[0], t(i), 0)),
                     pl.BlockSpec((3, tr, cols), lambda i, j, t=tile(k): (0, t(i), 0))]
        out_specs.append(pl.BlockSpec((tr, cols), lambda i, j, t=tile(k): (t(i), 0)))
    return pl.pallas_call(
        body, name=name,
        grid_spec=pltpu.PrefetchScalarGridSpec(num_scalar_prefetch=1, grid=(total,), in_specs=in_specs, out_specs=out_specs),
        out_shape=[jax.ShapeDtypeStruct(sh, F32) for sh in shapes],
        compiler_params=_cparams(),
    )(j_arr, *[a for pair in zip(own4s, recvs) for a in pair])


def _pair_plan(shapes):
    def plan(x, y, c, src, land):
        sends = []
        for a, shape in enumerate(shapes):
            rows = shape[1] // 2
            theirs = pl.ds(pl.multiple_of((1 - c) * rows, 16), rows)
            sends.append((src[a].at[:, theirs], land[a], (x, y, 1 - c), land[a]))
        return [], sends

    return plan


def _swap_plan(n):
    def plan(x, y, c, src, land):
        return [], [(src[a], land[a], (x, y, 1 - c), land[a]) for a in range(n)]

    return plan


def _pair_sum(name, g32s, recvs, c_arr):
    n = len(g32s)
    shapes = [r.shape for r in recvs]
    trs = [_row_tile(rows, cols) for _, rows, cols in shapes]
    nhs = [rows // tr for (_, rows, _), tr in zip(shapes, trs)]
    total, tile, mine = _walk([nblk * nh for (nblk, _, _), nh in zip(shapes, nhs)])

    def body(c_ref, *refs):
        del c_ref
        i = pl.program_id(0)
        for k in range(n):
            g_ref, r_ref, o32_ref, obf_ref = refs[2 * k], refs[2 * k + 1], refs[2 * n + 2 * k], refs[2 * n + 2 * k + 1]

            def add(g_ref=g_ref, r_ref=r_ref, o32_ref=o32_ref, obf_ref=obf_ref):
                val = g_ref[0] + r_ref[0].astype(F32)
                o32_ref[0] = val
                obf_ref[0] = val.astype(BF16)

            pl.when(mine(k, i))(add)

    in_specs, out_specs, out_shape = [], [], []
    for k, ((_, _, cols), tr, nh) in enumerate(zip(shapes, trs, nhs)):
        def half(tr=tr, cols=cols, t=tile(k), nh=nh):
            return pl.BlockSpec((1, tr, cols), lambda i, c: (t(i) // nh, t(i) % nh, 0))

        in_specs += [pl.BlockSpec((1, tr, cols), lambda i, c, t=tile(k), nh=nh: (t(i) // nh, c[0] * nh + t(i) % nh, 0)), half()]
        out_specs += [half(), half()]
        out_shape += [jax.ShapeDtypeStruct(shapes[k], F32), jax.ShapeDtypeStruct(shapes[k], BF16)]
    res = pl.pallas_call(
        body, name=name,
        grid_spec=pltpu.PrefetchScalarGridSpec(num_scalar_prefetch=1, grid=(total,), in_specs=in_specs, out_specs=out_specs),
        out_shape=out_shape,
        compiler_params=_cparams(),
    )(c_arr, *[a for pair in zip(g32s, recvs) for a in pair])
    return [(res[2 * k], res[2 * k + 1]) for k in range(n)]


def _adam_halves(name, w, m, v, mine, theirs, c_arr):
    rows, cols = w.shape
    tr = _row_tile(rows // 2, cols)
    nh = (rows // 2) // tr

    def body(c_ref, w_ref, m_ref, v_ref, mine_ref, theirs_ref, g_out, d_out, m_out, v_out):
        here = (pl.program_id(0) // nh) == c_ref[0]
        g = jnp.where(here, mine_ref[...], theirs_ref[...])
        delta, m2, v2 = _adam(w_ref[...], g, m_ref[...], v_ref[...])
        g_out[...] = g
        d_out[...] = delta
        m_out[...] = m2
        v_out[...] = v2

    spec = pl.BlockSpec((tr, cols), lambda i, c: (i, 0))
    shape = jax.ShapeDtypeStruct((rows, cols), F32)
    return pl.pallas_call(
        body, name=name,
        grid_spec=pltpu.PrefetchScalarGridSpec(
            num_scalar_prefetch=1, grid=(2 * nh,),
            in_specs=[spec, spec, spec,
                      pl.BlockSpec((tr, cols), lambda i, c: (jnp.clip(i - c[0] * nh, 0, nh - 1), 0)),
                      pl.BlockSpec((tr, cols), lambda i, c: (jnp.clip(i - (1 - c[0]) * nh, 0, nh - 1), 0))],
            out_specs=[spec] * 4),
        out_shape=[shape] * 4,
        compiler_params=_cparams(),
    )(c_arr, w, m, v, mine, theirs)


def _adam_big(name, w, m, v, ga, gb):
    rows, cols = w.shape
    tr = _row_tile(rows, cols)

    def body(w_ref, m_ref, v_ref, ga_ref, gb_ref, g_out, d_out, m_out, v_out):
        gsum = ga_ref[...] + gb_ref[...]
        delta, m2, v2 = _adam(w_ref[...], gsum, m_ref[...], v_ref[...])
        g_out[...] = gsum
        d_out[...] = delta
        m_out[...] = m2
        v_out[...] = v2

    spec = pl.BlockSpec((tr, cols), lambda i: (i, 0))
    shape = jax.ShapeDtypeStruct((rows, cols), F32)
    return pl.pallas_call(
        body, name=name, grid=(rows // tr,), out_shape=[shape] * 4,
        in_specs=[spec] * 5, out_specs=[spec] * 4, compiler_params=_cparams(),
    )(w, m, v, ga, gb)


PK_VEC = 0
PK_LOSS = 6
PK_PAIR = 8
PK_BSP = 16
PK_WCAT = 24
PK_ROWS = PK_WCAT + CH
PAIR_ORDER = ("gmlp_norm_g", "gmlp_norm_b", "conv_b", "conv_norm_g", "conv_norm_b", "g_out_a", "g_out_b")
VEC_ORDER = ("g_pre_f1", "g_post_f1", "g_pre_m", "g_post_m", "g_pre_f2", "g_post_f2")


def _pack_late(rows):
    counts = [r.shape[0] for r in rows]
    assert sum(counts) == 8

    def body(*refs):
        o_ref = refs[-1]
        at = 0
        for r, cnt in zip(refs[:-1], counts):
            o_ref[at:at + cnt, :] = r[...]
            at += cnt

    return pl.pallas_call(
        body, name="pack_late", out_shape=jax.ShapeDtypeStruct((8, D), F32),
        in_specs=[VMEM_FULL] * len(rows), out_specs=VMEM_FULL, compiler_params=_cparams(),
    )(*rows)


def _pack_small(vecs, pairs, dbsp, dwcat, lsum):
    def body(*refs):
        vec_refs = refs[:4]
        pair_refs = refs[4:11]
        dbsp_ref, dwcat_ref, lsum_ref, o_ref = refs[11:]
        o_ref[0:PK_WCAT, :] = jnp.zeros((PK_WCAT, D), F32)
        o_ref[PK_LOSS:PK_LOSS + 1, 0:128] = lsum_ref[...]
        for k, r in enumerate(vec_refs):
            o_ref[PK_VEC + 2 + k:PK_VEC + 3 + k, :] = r[...]
        for k, r in enumerate(pair_refs):
            row, half = PK_PAIR + k // 2, k % 2
            o_ref[row:row + 1, half * WA:(half + 1) * WA] = r[...]
        o_ref[PK_BSP:PK_BSP + NH, 0:CH] = dbsp_ref[...]
        o_ref[PK_WCAT:PK_ROWS, :] = dwcat_ref[...]

    args = list(vecs) + list(pairs) + [dbsp, dwcat, lsum]
    return pl.pallas_call(
        body, name="pack_small", out_shape=jax.ShapeDtypeStruct((PK_ROWS, D), F32),
        in_specs=[VMEM_FULL] * len(args), out_specs=VMEM_FULL, compiler_params=_cparams(),
    )(*args)


def _small_adam(pack_all, late_all, dcw_all, dada_all, params, behind):
    names = list(VEC_ORDER) + list(PAIR_ORDER) + ["b_spatial", "w_spatial", "conv_w", "b_ada"]
    flat = []
    for nm in names:
        flat += list(params[nm])
    n_in = 4 + len(flat)

    def body(*refs):
        pack_ref, late_ref, dcw_ref, dada_ref = refs[:4]
        prm = refs[4:n_in]
        outs = refs[n_in + 1:]

        def total(r0, nr, c0, nc):
            acc = pack_ref[0, r0:r0 + nr, c0:c0 + nc]
            for d in range(1, NDEV):
                acc = acc + pack_ref[d, r0:r0 + nr, c0:c0 + nc]
            return acc

        def emit(idx, g, getw, put):
            w_ref, m_ref, v_ref = prm[3 * idx:3 * idx + 3]
            delta, m2, v2 = _adam(getw(w_ref), g, getw(m_ref), getw(v_ref))
            for o_ref, val in zip(outs[4 * idx:4 * idx + 4], (g, delta, m2, v2)):
                put(o_ref, val)

        def whole(ref):
            return ref[...]

        def put_whole(ref, val):
            ref[...] = val

        idx = 0
        for k in range(6):
            if k < 2:
                g = late_ref[0, k:k + 1, :]
                for d in range(1, NDEV):
                    g = g + late_ref[d, k:k + 1, :]
            else:
                g = total(PK_VEC + k, 1, 0, D)
            emit(idx, g, whole, put_whole)
            idx += 1
        for k in range(7):
            emit(idx, total(PK_PAIR + k // 2, 1, (k % 2) * WA, WA), whole, put_whole)
            idx += 1
        emit(idx, total(PK_BSP, NH, 0, CH), lambda r: r[0], lambda r, val: r.__setitem__(0, val))
        idx += 1
        row = lax.broadcasted_iota(jnp.int32, (CH, CH), 0)
        col = lax.broadcasted_iota(jnp.int32, (CH, CH), 1)
        for h in range(NH):
            gh = jnp.where(col <= row, total(PK_WCAT, CH, h * CH, CH), 0.0)
            w_ref, m_ref, v_ref = prm[3 * idx:3 * idx + 3]
            delta, m2, v2 = _adam(w_ref[0, h], gh, m_ref[0, h], v_ref[0, h])
            for o_ref, val in zip(outs[4 * idx:4 * idx + 4], (gh, delta, m2, v2)):
                o_ref[0, h] = val
        idx += 1
        gcw = dcw_ref[0, 0:CK, :]
        for d in range(1, NDEV):
            gcw = gcw + dcw_ref[d, 0:CK, :]
        emit(idx, gcw, lambda r: r[0], lambda r, val: r.__setitem__(0, val))
        idx += 1
        emit(idx, jnp.sum(dada_ref[...], axis=0, keepdims=True), whole, put_whole)
        outs[-1][...] = jnp.sum(total(PK_LOSS, 1, 0, 128), axis=1, keepdims=True) * (0.5 / D)

    out_shape = []
    for nm in names:
        w = params[nm][0]
        out_shape += [jax.ShapeDtypeStruct(w.shape, F32)] * 4
    out_shape.append(jax.ShapeDtypeStruct((1, 1), F32))
    res = pl.pallas_call(
        body, name="small_adam", out_shape=out_shape,
        in_specs=[VMEM_FULL] * n_in + [ANY], out_specs=[VMEM_FULL] * len(out_shape), compiler_params=_cparams(),
    )(pack_all, late_all, dcw_all, dada_all, *flat, behind)
    return {nm: tuple(res[4 * k:4 * k + 4]) for k, nm in enumerate(names)}, res[-1].reshape(())


WEIGHTS = ['w_ada', 'b_ada', 'g_pre_f1', 'g_post_f1', 'w_f1_in', 'w_f1_out', 'g_pre_m', 'g_post_m', 'w_mix_in',
           'gmlp_norm_g', 'gmlp_norm_b', 'w_spatial', 'b_spatial', 'conv_w', 'conv_b', 'conv_norm_g', 'conv_norm_b',
           'g_out_a', 'g_out_b', 'w_mix_out', 'g_pre_f2', 'g_post_f2', 'w_f2_in', 'w_f2_out']


def kernel(x, c, w_ada, b_ada, g_pre_f1, g_post_f1, w_f1_in, w_f1_out, g_pre_m, g_post_m, w_mix_in, gmlp_norm_g, gmlp_norm_b, w_spatial, b_spatial, conv_w, conv_b, conv_norm_g, conv_norm_b, g_out_a, g_out_b, w_mix_out, g_pre_f2, g_post_f2, w_f2_in, w_f2_out, loss_target, m_w_ada, m_b_ada, m_g_pre_f1, m_g_post_f1, m_w_f1_in, m_w_f1_out, m_g_pre_m, m_g_post_m, m_w_mix_in, m_gmlp_norm_g, m_gmlp_norm_b, m_w_spatial, m_b_spatial, m_conv_w, m_conv_b, m_conv_norm_g, m_conv_norm_b, m_g_out_a, m_g_out_b, m_w_mix_out, m_g_pre_f2, m_g_post_f2, m_w_f2_in, m_w_f2_out, v_w_ada, v_b_ada, v_g_pre_f1, v_g_post_f1, v_w_f1_in, v_w_f1_out, v_g_pre_m, v_g_post_m, v_w_mix_in, v_gmlp_norm_g, v_gmlp_norm_b, v_w_spatial, v_b_spatial, v_conv_w, v_conv_b, v_conv_norm_g, v_conv_norm_b, v_g_out_a, v_g_out_b, v_w_mix_out, v_g_pre_f2, v_g_post_f2, v_w_f2_in, v_w_f2_out):
    env = dict(locals())
    wts = {n: env[n] for n in WEIGHTS}
    mom = {n: env["m_" + n] for n in WEIGHTS}
    var = {n: env["v_" + n] for n in WEIGHTS}
    nb, s, _ = x.shape
    t = nb * s
    ax, ay, ac = lax.axis_index("x"), lax.axis_index("y"), lax.axis_index("c")
    j_chip = 2 * ax + ay
    dev = 4 * ax + 2 * ay + ac
    j_arr = j_chip.reshape(1).astype(jnp.int32)

    groups = (("w_f1_in",), ("w_mix_in", "w_mix_out"), ("w_f2_in", "w_f2_out"), ("w_f1_out",))
    def gather_operands(gi):
        srcs = [wts[n][0].astype(BF16) for n in groups[gi]] + ([conv_w[0]] if gi == 1 else [])
        lands = [lax.dynamic_update_index_in_dim(lax.empty((NCHIP,) + a.shape, a.dtype), a, j_chip, 0) for a in srcs]
        return srcs, lands

    def gather_start(gi, behind, operands=None):
        srcs, lands = operands or gather_operands(gi)
        plan_a, plan_b, n_b = _gather_plans([a.shape for a in srcs])
        ssem, rsem, srcs, lands, token = _split_start("gw_start%d" % gi, srcs, lands, plan_a, 3 * len(srcs), behind)
        gather[gi] = (srcs, lands, ssem, rsem, plan_a, plan_b, n_b)
        return token

    def gather_forward(gi, behind):
        srcs, lands, ssem, rsem, plan_a, plan_b, n_b = gather[gi]
        ssem, rsem, lands, token = _split_forward("gw_fwd%d" % gi, srcs, lands, ssem, rsem, plan_a, plan_b, n_b, behind)
        gather[gi] = (lands, ssem, rsem, plan_b)
        return token

    def gathered(gi, behind):
        lands, ssem, rsem, plan_b = gather[gi]
        return _split_wait("gw_wait%d" % gi, [], lands, ssem, rsem, plan_b, behind)

    gather = {}
    def allgather_start(tag, arrs, behind):
        lands = [lax.dynamic_update_index_in_dim(lax.empty((NDEV,) + a.shape, a.dtype), a, dev, 0) for a in arrs]
        ssem, rsem, srcs, lands, token = _split_start("small_start_" + tag, arrs, lands, _allgather_plan(len(arrs)),
                                                      7 * len(arrs), behind)
        return (srcs, lands, ssem, rsem), token

    def allgather_wait(tag, state, behind):
        srcs, lands, ssem, rsem = state
        return _split_wait("small_wait_" + tag, srcs, lands, ssem, rsem, _allgather_plan(len(srcs)), behind)

    c_state, token = allgather_start("c", [c.reshape(8, (nb * D) // 8)], c)
    token = gather_start(0, token)
    (c_all8,) = allgather_wait("c", c_state, token)
    c_all = c_all8.reshape(NDEV * nb, D)
    b_sh = lax.dynamic_slice(b_ada, (0, j_chip * ADA_SH), (1, ADA_SH))
    ada_sh = _ada_fwd(c_all, w_ada[0], b_sh)
    later = [gather_operands(3), gather_operands(1), gather_operands(2)]
    (ada4,) = _chip_allgather("gather_ada", [ada_sh], behind=[a for pair in later for arrs in pair for a in arrs])
    token = gather_forward(0, ada4)
    plans = [_gather_plans([a.shape for a in srcs]) for srcs, _ in later]
    started, token = _split_start_groups(
        "gw_start_later", [(srcs, lands, pa, 3 * len(srcs)) for (srcs, lands), (pa, _, _) in zip(later, plans)], token)
    for gi, (ssem, rsem, srcs, lands), (pa, pb, n_b) in zip((3, 1, 2), started, plans):
        gather[gi] = (srcs, lands, ssem, rsem, pa, pb, n_b)
    ada_me = lax.dynamic_slice(ada4, (0, dev * nb, 0), (NCHIP, nb, ADA_SH))
    ada_me = jnp.transpose(ada_me, (1, 0, 2)).reshape(nb, NMOD * D)
    sh1, sc1, gt1, sh2, sc2, gt2, sh3, sc3, gt3 = [ada_me[:, k * D:(k + 1) * D].reshape(nb, 1, D) for k in range(NMOD)]

    wcat = jnp.transpose(w_spatial[0], (1, 0, 2)).reshape(CH, NH * CH)
    wcat_t = jnp.transpose(w_spatial[0], (0, 2, 1)).reshape(NH * CH, CH)
    bspt = jnp.repeat(b_spatial[0].T, HD, axis=1)

    (w1i,) = gathered(0, token)
    p1, act1 = _ffn_up(x, sh1, sc1, g_pre_f1, w1i)
    forwarded, token = _split_forward_groups("gw_fwd_f1_out_mix", [gather[3], gather[1]], act1)
    for gi, (ssem, rsem, lands) in zip((3, 1), forwarded):
        gather[gi] = (lands, ssem, rsem, gather[gi][5])
    (w1o,) = gathered(3, token)
    w1o = w1o.reshape(DFF, D)
    x1, f1 = _ffn_down(x, act1, gt1, g_post_f1, w1o)
    wmi, wmo, cw4 = gathered(1, x1)
    wmo = wmo.reshape(D, D)
    cw_full = jnp.transpose(cw4, (1, 0, 2)).reshape(CK, WB)
    cw_pad = jnp.pad(cw_full, ((0, HALO - CK), (0, 0)))
    u, v, a, g = _mix_in_fwd(x1, sh2, sc2, g_pre_m, wmi)
    token = gather_forward(2, u)
    x2, conv, yb, m = _mix_mid_fwd(x1, u, v, a, g, gt2 + token[0, 0], gmlp_norm_g, gmlp_norm_b, wcat, bspt, cw_pad, conv_b,
                                   conv_norm_g, conv_norm_b, g_out_a, g_out_b, wmo, g_post_m)
    w2i, w2o = gathered(2, [x2, token])
    w2o = w2o.reshape(DFF, D)
    dx3, df2, p2, lsum, dg_post_f2, dgt3 = _ffn_loss_fwd(x2, sh3, sc3, gt3, g_pre_f2, g_post_f2, w2i, w2o, loss_target)

    def chip4(pair, rows):
        return [arr.reshape(NCHIP, rows, arr.shape[-1]) for arr in pair]

    def scatter_start(tag, pairs, behind):
        srcs = [p[1] for p in pairs]
        lands = [lax.empty((3,) + a.shape[1:], a.dtype) for a in srcs]
        ssem, rsem, srcs, lands, token = _split_start("gs_start_" + tag, srcs, lands, _scatter_plan(len(srcs)),
                                                      3 * len(srcs), behind)
        return (srcs, lands, ssem, rsem), token

    def scatter_wait(tag, state, behind):
        srcs, lands, ssem, rsem = state
        return _split_wait("gs_wait_" + tag, srcs, lands, ssem, rsem, _scatter_plan(len(srcs)), behind)

    out = {}
    dx2, dp2, h3, a2, dg_pre_f2, dsh3, dsc3 = _ffn_bwd(
        dx3, x2, None, p2, sh3, sc3, gt3, g_pre_f2, g_post_f2, w2i, w2o, df=df2)
    gw2i = _wgrad("wgrad_f2_in", h3.reshape(t, D), dp2.reshape(t, 2 * DFF), 2 * DFF // NCHIP, True)
    gw2o = chip4(_wgrad("wgrad_f2_out", a2.reshape(t, DFF), df2.reshape(t, D), D // 2, False), DFF // NCHIP)
    scat_f2, tok = scatter_start("f2", [gw2i, gw2o], dg_post_f2)
    dy, dm, dg_post_m, dgt2 = _mix_out_bwd(dx2, m, gt2 + tok[0, 0], g_post_m, wmo)
    gwmo = chip4(_wgrad("wgrad_mix_out", yb.reshape(t, D), dm.reshape(t, D), D // 2, False), D // NCHIP)
    (du, dv, dconv, dwcat, dbsp, dgn_g, dgn_b, dgo_a, dgo_b, dcn_g, dcn_b, dcb) = _mix_mid_bwd(
        dy, u, v, conv, gmlp_norm_g, gmlp_norm_b, wcat, wcat_t, bspt, conv_norm_g, conv_norm_b, g_out_a, g_out_b)
    dx1, dproj, h2, dg_pre_m, dsh2, dsc2, dcw = _mix_in_bwd(dx2, x1, du, dv, dconv, a, g, sh2, sc2, g_pre_m, wmi, cw_pad)
    gwmi = _wgrad("wgrad_mix_in", h2.reshape(t, D), dproj.reshape(t, 4 * WA), WA, True)

    vec_grads = dict(g_pre_m=dg_pre_m, g_post_m=dg_post_m, g_pre_f2=dg_pre_f2, g_post_f2=dg_post_f2)
    pair_grads = dict(gmlp_norm_g=dgn_g, gmlp_norm_b=dgn_b, conv_b=dcb, conv_norm_g=dcn_g, conv_norm_b=dcn_b,
                      g_out_a=dgo_a, g_out_b=dgo_b)
    pack = _pack_small([vec_grads[n] for n in VEC_ORDER[2:]], [pair_grads[n] for n in PAIR_ORDER], dbsp, dwcat, lsum)
    dada_early = jnp.concatenate([q.reshape(nb, D) for q in (dsh2, dsc2, dgt2, dsh3, dsc3, dgt3)], axis=1)
    small_early = [pack, dcw, dada_early.reshape(8, (nb * 6 * D) // 8)]
    mix_bf16 = [gwmi[1], gwmo[1]]
    (s_mix, s_early), tok2 = _split_start_groups("gs_start_mix_small", [
        (mix_bf16, [lax.empty((3,) + a.shape[1:], a.dtype) for a in mix_bf16], _scatter_plan(2), 6),
        (small_early, [lax.dynamic_update_index_in_dim(lax.empty((NDEV,) + a.shape, a.dtype), a, dev, 0) for a in small_early],
         _allgather_plan(3), 21)], dg_pre_m)
    scat_mix = (s_mix[2], s_mix[3], s_mix[0], s_mix[1])
    early = (s_early[2], s_early[3], s_early[0], s_early[1])
    grad_x, dp1, h1, a1, df1, dg_pre_f1, dg_post_f1, dsh1, dsc1, dgt1 = _ffn_bwd(
        dx1, x, f1, p1, sh1 + tok2[0, 0], sc1, gt1, g_pre_f1, g_post_f1, w1i, w1o)
    late_pack = _pack_late([dg_pre_f1, dg_post_f1] + [q.reshape(nb, D) for q in (dsh1, dsc1, dgt1)])
    late, tok2 = allgather_start("late", [late_pack], dg_post_f1)
    gw1i = _wgrad("wgrad_f1_in", h1.reshape(t, D), dp1.reshape(t, 2 * DFF), 2 * DFF // NCHIP, True)
    gw1o = chip4(_wgrad("wgrad_f1_out", a1.reshape(t, DFF), df1.reshape(t, D), D // 2, False), DFF // NCHIP)
    def d2d_start(tag, srcs, lands, plan, behind):
        ssem, rsem, srcs, lands, token = _split_start("d2d_start_" + tag, srcs, lands, plan, len(srcs), behind)
        return (srcs, lands, ssem, rsem, plan), token

    def d2d_wait(tag, state, behind):
        srcs, lands, ssem, rsem, plan = state
        return _split_wait("d2d_wait_" + tag, srcs, lands, ssem, rsem, plan, behind)

    def swap_start(tag, parts, behind):
        return d2d_start(tag, parts, [lax.empty(a.shape, a.dtype) for a in parts], _swap_plan(len(parts)), behind)

    def sums(names, pairs, recv):
        return _sum4("sum4_" + names[0][2:4], [p[0] for p in pairs], recv, j_arr)

    def update(names, part, other):
        for k, n in enumerate(names):
            out[n] = tuple(r[None] for r in _adam_big("adam_" + n, wts[n][0], mom[n][0], var[n][0], part[k], other[k]))

    c_arr = ac.reshape(1).astype(jnp.int32)
    halves = [gw1i[1], gw1o[1]]
    pair_st, tok = d2d_start("pair", halves, [lax.empty((a.shape[0], a.shape[1] // 2, a.shape[2]), a.dtype) for a in halves],
                             _pair_plan([a.shape for a in halves]), tok2)
    names_f2, names_mix, names_f1 = ("w_f2_in", "w_f2_out"), ("w_mix_in", "w_mix_out"), ("w_f1_in", "w_f1_out")
    part_f2 = sums(names_f2, [gw2i, gw2o], scatter_wait("f2", scat_f2, tok))
    sib = d2d_wait("pair", pair_st, part_f2)
    pair_i, pair_o = _pair_sum("pairsum_f1", [gw1i[0], gw1o[0]], sib, c_arr)
    scat_f1, tok = scatter_start("f1", [pair_i, pair_o], tok2)
    swap_f2, tok = swap_start("swap_f2", part_f2, tok)
    part_mix = sums(names_mix, [gwmi, gwmo], scatter_wait("mix", scat_mix, tok))
    swap_mix, tok = swap_start("swap_mix", part_mix, part_mix[1])

    pack_all, dcw_all, dada_early8 = allgather_wait("early", early, tok)
    (late_all,) = allgather_wait("late", late, pack_all)
    dada_late = jnp.transpose(late_all[:, 2:8, :].reshape(NDEV, 3, nb, D), (0, 2, 1, 3)).reshape(NDEV * nb, 3 * D)
    dada_all = jnp.concatenate([dada_late, dada_early8.reshape(NDEV * nb, 6 * D)], axis=1)
    dada_sh = lax.dynamic_slice(dada_all, (0, j_chip * ADA_SH), (NDEV * nb, ADA_SH))
    out["w_ada"] = tuple(r[None] for r in _ada_bwd_adam(c_all, dada_sh, w_ada[0], m_w_ada[0], v_w_ada[0]))
    update(names_f2, part_f2, d2d_wait("swap_f2", swap_f2, out["w_ada"][3]))
    update(names_mix, part_mix, d2d_wait("swap_mix", swap_mix, out["w_f2_out"][3]))

    mine = sums(names_f1, [pair_i, pair_o], scatter_wait("f1", scat_f1, out["w_mix_out"][3]))
    swap_f1, tok = swap_start("swap_f1", mine, mine[1])
    dcw_mine = lax.dynamic_slice(dcw_all, (0, 0, j_chip * (WB // NCHIP)), (NDEV, HALO, WB // NCHIP))
    small = {n: (wts[n], mom[n], var[n]) for n in list(VEC_ORDER) + list(PAIR_ORDER) + ["b_spatial", "w_spatial", "conv_w", "b_ada"]}
    small_out, loss = _small_adam(pack_all, late_all, dcw_mine, dada_all, small, tok)
    out.update(small_out)
    theirs = d2d_wait("swap_f1", swap_f1, out["b_ada"][3])
    for k, n in enumerate(names_f1):
        out[n] = tuple(r[None] for r in _adam_halves("adam_" + n, wts[n][0], mom[n][0], var[n][0], mine[k], theirs[k],
                                                     c_arr))

    res = [loss, grad_x]
    for k in range(4):
        res += [out[n][k] for n in WEIGHTS]
    return tuple(res)
```

```python
import jax
import jax.numpy as jnp
from jax import lax
from jax.experimental import pallas as pl
from jax.experimental.pallas import tpu as pltpu

D = 1024
DFF = 2816
WA = 512
WB = 512
NH = 8
HD = 64
CH = 128
CK = 31
HALO = 32
NMOD = 9
EPS = 1e-6
NCHIP = 4
NDEV = 8
FBLK = DFF // 2
ADA_SH = NMOD * D // NCHIP

LR, B1, B2, EPS_A, WD, STEP = 0.001, 0.9, 0.999, 1e-08, 0.01, 10

F32 = jnp.float32
BF16 = jnp.bfloat16
MESH = pl.DeviceIdType.MESH
ANY = pl.BlockSpec(memory_space=pl.ANY)
VMEM_FULL = pl.BlockSpec(memory_space=pltpu.VMEM)
VMEM_LIMIT = 56 * 1024 * 1024
WGRAD_VMEM_BUDGET = 52 * 1024 * 1024
TM = 512
TM_FFN_BWD = 256

NT = (((1,), (1,)), ((), ()))
TN = (((0,), (0,)), ((), ()))


def _dot(a, b):
    return jnp.dot(a, b, preferred_element_type=F32)


def _dot_nt(a, b):
    return lax.dot_general(a, b, NT, preferred_element_type=F32)


def _dot_tn(a, b):
    return lax.dot_general(a, b, TN, preferred_element_type=F32)


def _cparams():
    return pltpu.CompilerParams(vmem_limit_bytes=VMEM_LIMIT)


def _chip_relations(x, y):
    return [(1 - x, y), (x, 1 - y), (1 - x, 1 - y)]


def _exchange(name, arrs, out_shapes, plan):
    n = len(arrs)
    n_out = len(out_shapes)

    def body(*refs):
        ins, outs = refs[:n], refs[n:n + n_out]
        send_sems, recv_sems, local_sems = refs[n + n_out:]
        x, y, c = lax.axis_index("x"), lax.axis_index("y"), lax.axis_index("c")
        local, sends = plan(x, y, c, ins, outs)
        locs = [pltpu.make_async_copy(s, d, local_sems.at[i]) for i, (s, d) in enumerate(local)]
        for loc in locs:
            loc.start()
        cps = [pltpu.make_async_remote_copy(src_ref=s, dst_ref=d, send_sem=send_sems.at[i], recv_sem=recv_sems.at[i],
                                            device_id=peer, device_id_type=MESH)
               for i, (s, d, peer, _) in enumerate(sends)]
        for cp in cps:
            cp.start()
        for i, (s, _, peer, landing) in enumerate(sends):
            pltpu.make_async_remote_copy(src_ref=s, dst_ref=landing, send_sem=send_sems.at[i], recv_sem=recv_sems.at[i],
                                         device_id=peer, device_id_type=MESH).wait_recv()
        for cp in cps:
            cp.wait_send()
        for loc in locs:
            loc.wait()

    return n, n_out, body


def _run_exchange(name, arrs, out_shapes, plan, n_local, n_send):
    n, n_out, body = _exchange(name, arrs, out_shapes, plan)
    return pl.pallas_call(
        body, name=name, out_shape=out_shapes,
        in_specs=[ANY] * n, out_specs=[ANY] * n_out,
        scratch_shapes=[pltpu.SemaphoreType.DMA((n_send,)), pltpu.SemaphoreType.DMA((n_send,)),
                        pltpu.SemaphoreType.DMA((max(n_local, 1),))],
    )(*arrs)


def _chip_allgather(name, arrs, behind=()):
    n = len(arrs)

    def plan(x, y, c, ins, outs):
        j_me = 2 * x + y
        local = [(ins[a], outs[a].at[j_me]) for a in range(n)]
        sends = []
        for a in range(n):
            for (px, py) in _chip_relations(x, y):
                sends.append((ins[a], outs[a].at[j_me], (px, py, c), outs[a].at[2 * px + py]))
        return local, sends

    shapes = [jax.ShapeDtypeStruct((NCHIP,) + a.shape, a.dtype) for a in arrs]
    return _run_exchange(name, list(arrs) + list(behind), shapes, plan, n, 3 * n)


HBM = pl.BlockSpec(memory_space=pltpu.HBM)
SEM = pl.BlockSpec(memory_space=pltpu.SEMAPHORE)
EFFECT = pltpu.SideEffectType.DATAFLOW_SIDE_EFFECTING


def _split_start_groups(name, groups, after):
    n_src = [len(g[0]) for g in groups]
    n_land = [len(g[1]) for g in groups]
    all_srcs = [pltpu.with_memory_space_constraint(a, pltpu.HBM) for g in groups for a in g[0]]
    all_lands = [pltpu.with_memory_space_constraint(a, pltpu.HBM) for g in groups for a in g[1]]
    ns, nl, ng = len(all_srcs), len(all_lands), len(groups)

    def body(*refs):
        src_refs, land_refs = refs[:ns], refs[ns:ns + nl]
        sem_refs = refs[ns + nl + 1:ns + nl + 1 + 2 * ng]
        token = refs[-1]
        x, y, c = lax.axis_index("x"), lax.axis_index("y"), lax.axis_index("c")
        at_src = at_land = 0
        for gi, (_, _, plan, _) in enumerate(groups):
            _, sends = plan(x, y, c, src_refs[at_src:at_src + n_src[gi]], land_refs[at_land:at_land + n_land[gi]])
            for i, (s, d, peer, _) in enumerate(sends):
                pltpu.make_async_remote_copy(src_ref=s, dst_ref=d, send_sem=sem_refs[2 * gi].at[i],
                                             recv_sem=sem_refs[2 * gi + 1].at[i], device_id=peer, device_id_type=MESH).start()
            at_src += n_src[gi]
            at_land += n_land[gi]
        token[...] = jnp.zeros_like(token)

    sems = [pltpu.SemaphoreType.DMA((g[3],)) for g in groups for _ in range(2)]
    res = pl.pallas_call(
        body, name=name,
        out_shape=(*sems, *[pltpu.HBM(a.shape, a.dtype) for a in all_lands], jax.ShapeDtypeStruct((8, 128), F32)),
        in_specs=[HBM] * (ns + nl) + [ANY],
        out_specs=(*([SEM] * (2 * ng)), *([HBM] * nl), pl.BlockSpec(memory_space=pltpu.VMEM)),
        input_output_aliases={ns + i: 2 * ng + i for i in range(nl)},
        compiler_params=pltpu.CompilerParams(has_side_effects=EFFECT),
    )(*all_srcs, *all_lands, after)
    out, at_src, at_land = [], 0, 2 * ng
    for gi in range(ng):
        out.append((res[2 * gi], res[2 * gi + 1], all_srcs[at_src:at_src + n_src[gi]],
                    list(res[at_land:at_land + n_land[gi]])))
        at_src += n_src[gi]
        at_land += n_land[gi]
    return out, res[-1]


def _split_start(name, srcs, lands, plan, n_send, after):
    (group,), token = _split_start_groups(name, [(srcs, lands, plan, n_send)], after)
    return (*group, token)


def _split_wait(name, srcs, lands, send_sems, recv_sems, plan, after):
    n, nl = len(srcs), len(lands)
    afters = list(after) if isinstance(after, (list, tuple)) else [after]

    def body(*refs):
        src, land = refs[:n], refs[n:n + nl]
        send_sems, recv_sems = refs[n + nl], refs[n + nl + 1]
        x, y, c = lax.axis_index("x"), lax.axis_index("y"), lax.axis_index("c")
        _, sends = plan(x, y, c, src, land)
        for i, (s, _, peer, landing) in enumerate(sends):
            cp = pltpu.make_async_remote_copy(src_ref=s, dst_ref=landing, send_sem=send_sems.at[i],
                                              recv_sem=recv_sems.at[i], device_id=peer, device_id_type=MESH)
            cp.wait_send()
            cp.wait_recv()

    thru = [pltpu.HBM(a.shape, a.dtype) for a in lands]
    res = pl.pallas_call(
        body, name=name, out_shape=tuple(thru),
        in_specs=[HBM] * (n + nl) + [SEM, SEM] + [ANY] * len(afters), out_specs=tuple([HBM] * nl),
        input_output_aliases={n + i: i for i in range(nl)},
        compiler_params=pltpu.CompilerParams(has_side_effects=EFFECT),
    )(*srcs, *lands, send_sems, recv_sems, *afters)
    return list(res)


def _split_forward_groups(name, groups, after):
    n_src = [len(g[0]) for g in groups]
    n_land = [len(g[1]) for g in groups]
    all_srcs = [a for g in groups for a in g[0]]
    all_lands = [a for g in groups for a in g[1]]
    ns, nl, ng = len(all_srcs), len(all_lands), len(groups)

    def body(*refs):
        src_refs, land_refs = refs[:ns], refs[ns:ns + nl]
        sems_a = refs[ns + nl:ns + nl + 2 * ng]
        sems_b = refs[ns + nl + 2 * ng + 1:ns + nl + 4 * ng + 1]
        token = refs[-1]
        x, y, c = lax.axis_index("x"), lax.axis_index("y"), lax.axis_index("c")
        at_src = at_land = 0
        for gi, g in enumerate(groups):
            src, land = src_refs[at_src:at_src + n_src[gi]], land_refs[at_land:at_land + n_land[gi]]
            plan_a, plan_b = g[4], g[5]
            _, first = plan_a(x, y, c, src, land)
            for i, (s, _, peer, landing) in enumerate(first):
                cp = pltpu.make_async_remote_copy(src_ref=s, dst_ref=landing, send_sem=sems_a[2 * gi].at[i],
                                                  recv_sem=sems_a[2 * gi + 1].at[i], device_id=peer, device_id_type=MESH)
                cp.wait_send()
                cp.wait_recv()
            _, second = plan_b(x, y, c, src, land)
            for i, (s, d, peer, _) in enumerate(second):
                pltpu.make_async_remote_copy(src_ref=s, dst_ref=d, send_sem=sems_b[2 * gi].at[i],
                                             recv_sem=sems_b[2 * gi + 1].at[i], device_id=peer, device_id_type=MESH).start()
            at_src += n_src[gi]
            at_land += n_land[gi]
        token[...] = jnp.zeros_like(token)

    sems = [pltpu.SemaphoreType.DMA((g[6],)) for g in groups for _ in range(2)]
    res = pl.pallas_call(
        body, name=name,
        out_shape=(*sems, *[pltpu.HBM(a.shape, a.dtype) for a in all_lands], jax.ShapeDtypeStruct((8, 128), F32)),
        in_specs=[HBM] * (ns + nl) + [SEM] * (2 * ng) + [ANY],
        out_specs=(*([SEM] * (2 * ng)), *([HBM] * nl), pl.BlockSpec(memory_space=pltpu.VMEM)),
        input_output_aliases={ns + i: 2 * ng + i for i in range(nl)},
        compiler_params=pltpu.CompilerParams(has_side_effects=EFFECT),
    )(*all_srcs, *all_lands, *[s for g in groups for s in (g[2], g[3])], after)
    out, at_land = [], 2 * ng
    for gi in range(ng):
        out.append((res[2 * gi], res[2 * gi + 1], list(res[at_land:at_land + n_land[gi]])))
        at_land += n_land[gi]
    return out, res[-1]


def _split_forward(name, srcs, lands, send_a, recv_a, plan_a, plan_b, n_b, after):
    (group,), token = _split_forward_groups(name, [(srcs, lands, send_a, recv_a, plan_a, plan_b, n_b)], after)
    return (*group, token)


def _gather_plans(shapes):
    n = len(shapes)

    def halves(a, c):
        rows = shapes[a][0] // 2
        return pl.ds(pl.multiple_of(c * rows, 16), rows), pl.ds(pl.multiple_of((1 - c) * rows, 16), rows)

    def split(a):
        return shapes[a][0] % 32 == 0

    def plan_a(x, y, c, src, land):
        j_me = 2 * x + y
        sends = []
        for a in range(n):
            for (px, py) in _chip_relations(x, y):
                if split(a):
                    mine, _ = halves(a, c)
                    sends.append((src[a].at[mine], land[a].at[j_me, mine], (px, py, c), land[a].at[2 * px + py, mine]))
                else:
                    sends.append((src[a], land[a].at[j_me], (px, py, c), land[a].at[2 * px + py]))
        return [], sends

    def plan_b(x, y, c, src, land):
        sends = []
        for a in range(n):
            if split(a):
                mine, other = halves(a, c)
                for (px, py) in _chip_relations(x, y):
                    j = 2 * px + py
                    sends.append((land[a].at[j, mine], land[a].at[j, mine], (x, y, 1 - c), land[a].at[j, other]))
        return [], sends

    n_b = 3 * sum(1 for a in range(n) if split(a))
    return plan_a, plan_b, n_b


def _allgather_plan(n):
    flips = [(dx, dy, dc) for dx in (0, 1) for dy in (0, 1) for dc in (0, 1) if dx or dy or dc]

    def plan(x, y, c, src, land):
        sends = []
        for a in range(n):
            for dx, dy, dc in flips:
                px, py, pc = x ^ dx, y ^ dy, c ^ dc
                sends.append((src[a], land[a].at[4 * x + 2 * y + c], (px, py, pc), land[a].at[4 * px + 2 * py + pc]))
        return [], sends

    return plan


def _scatter_plan(n):
    def plan(x, y, c, src, land):
        sends = []
        for a in range(n):
            for k, (px, py) in enumerate(_chip_relations(x, y)):
                sends.append((src[a].at[2 * px + py], land[a].at[k], (px, py, c), land[a].at[k]))
        return [], sends

    return plan


def _rms(x):
    r = lax.rsqrt(jnp.mean(x * x, axis=-1, keepdims=True) + EPS)
    return x * r, r


def _rms_bwd(dy, n, r, g):
    dg = jnp.sum(dy * n, axis=0, keepdims=True)
    dn = dy * g
    dx = r * (dn - n * jnp.mean(dn * n, axis=-1, keepdims=True))
    return dx, dg


def _ln(x):
    mu = jnp.mean(x, axis=-1, keepdims=True)
    xc = x - mu
    rstd = lax.rsqrt(jnp.mean(xc * xc, axis=-1, keepdims=True) + EPS)
    return xc * rstd, rstd


def _ln_bwd(dy, xhat, rstd, g):
    dg = jnp.sum(dy * xhat, axis=0, keepdims=True)
    db = jnp.sum(dy, axis=0, keepdims=True)
    dxh = dy * g
    dx = rstd * (dxh - jnp.mean(dxh, axis=-1, keepdims=True) - xhat * jnp.mean(dxh * xhat, axis=-1, keepdims=True))
    return dx, dg, db


def _sigmoid(x):
    return jax.nn.sigmoid(x)


def _dsilu(x, s):
    return s * (1.0 + x * (1.0 - s))


def _adam(w, g, m, v):
    m = B1 * m + (1.0 - B1) * g
    v = B2 * v + (1.0 - B2) * (g * g)
    m_hat = m / (1.0 - B1 ** STEP)
    v_hat = v / (1.0 - B2 ** STEP)
    delta = -LR * (m_hat / (jnp.sqrt(v_hat) + EPS_A) + WD * w)
    return delta, m, v


def _head_mask(shape):
    lane = lax.broadcasted_iota(jnp.int32, shape, len(shape) - 1)
    return [(lane >= h * HD) & (lane < (h + 1) * HD) for h in range(NH)]


def _first(b, i):
    return jnp.logical_and(b == 0, i == 0)


def _acc(ref, val, first):
    @pl.when(first)
    def _():
        ref[...] = val

    @pl.when(jnp.logical_not(first))
    def _():
        ref[...] += val


def _ada_fwd(c_all, w_sh, b_sh):
    nb = c_all.shape[0]
    tn = 768

    def body(c_ref, w_ref, b_ref, o_ref):
        cv = c_ref[...]
        cs = (cv * _sigmoid(cv)).astype(BF16)
        o_ref[...] = _dot(cs, w_ref[...].astype(BF16)) + b_ref[...]

    return pl.pallas_call(
        body, name="ada_fwd", grid=(ADA_SH // tn,),
        out_shape=jax.ShapeDtypeStruct((nb, ADA_SH), F32),
        in_specs=[pl.BlockSpec((nb, D), lambda j: (0, 0)), pl.BlockSpec((D, tn), lambda j: (0, j)),
                  pl.BlockSpec((1, tn), lambda j: (0, j))],
        out_specs=pl.BlockSpec((nb, tn), lambda j: (0, j)),
        compiler_params=_cparams(),
    )(c_all, w_sh, b_sh)


def _ada_bwd_adam(c_all, dada_sh, w, m, v):
    nb = c_all.shape[0]
    tn = 768

    def body(c_ref, d_ref, w_ref, m_ref, v_ref, g_out, d_out, m_out, v_out):
        cv = c_ref[...]
        cs = (cv * _sigmoid(cv)).astype(BF16)
        g = _dot_tn(cs, d_ref[...].astype(BF16))
        delta, m2, v2 = _adam(w_ref[...], g, m_ref[...], v_ref[...])
        g_out[...] = g
        d_out[...] = delta
        m_out[...] = m2
        v_out[...] = v2

    big = pl.BlockSpec((D, tn), lambda j: (0, j))
    shape = jax.ShapeDtypeStruct((D, ADA_SH), F32)
    return pl.pallas_call(
        body, name="ada_bwd_adam", grid=(ADA_SH // tn,),
        out_shape=[shape] * 4,
        in_specs=[pl.BlockSpec((nb, D), lambda j: (0, 0)), pl.BlockSpec((nb, tn), lambda j: (0, j)), big, big, big],
        out_specs=[big] * 4,
        compiler_params=_cparams(),
    )(c_all, dada_sh, w, m, v)


def _tok_specs(tm, width):
    return pl.BlockSpec((1, tm, width), lambda b, i: (b, i, 0))


def _mod_spec():
    return pl.BlockSpec((1, 1, D), lambda b, i: (b, 0, 0))


def _row_spec(width=D):
    return pl.BlockSpec((1, width), lambda b, i: (0, 0))


def _ffn_loss_fwd(x, sh, sc, gt, g_pre, g_post, w_in4, w_out, target):
    nb, s, _ = x.shape
    tm = min(TM, s)

    def body(x_ref, sh_ref, sc_ref, gt_ref, gpre_ref, gpost_ref, win_ref, wout_ref, tgt_ref,
             xo_ref, df_ref, p_ref, ls_ref, dgpost_ref, dgt_ref):
        xv = x_ref[0]
        n, _ = _rms(xv)
        h = (n * gpre_ref[...]) * (1.0 + sc_ref[0]) + sh_ref[0]
        hb = h.astype(BF16)
        acc = jnp.zeros((tm, D), F32)
        for j in range(2):
            gate = _dot(hb, win_ref[j])
            up = _dot(hb, win_ref[2 + j])
            p_ref[0, :, j * FBLK:(j + 1) * FBLK] = gate.astype(BF16)
            p_ref[0, :, DFF + j * FBLK:DFF + (j + 1) * FBLK] = up.astype(BF16)
            a = (gate * _sigmoid(gate)) * up
            acc = acc + _dot(a.astype(BF16), wout_ref[j * FBLK:(j + 1) * FBLK, :])
        nf, q = _rms(acc)
        gpost = gpost_ref[...]
        half_gate = 0.5 * gt_ref[0]
        out = xv + half_gate * (nf * gpost)
        first = _first(pl.program_id(0), pl.program_id(1))
        err = out - tgt_ref[0]
        dout = err * (1.0 / D)
        xo_ref[0] = dout
        row = jnp.sum(err * err, axis=0, keepdims=True)
        part = row[:, 0:128]
        for k in range(1, D // 128):
            part = part + row[:, k * 128:(k + 1) * 128]
        _acc(ls_ref, part, first)
        df, dgpost = _rms_bwd(dout * half_gate, nf, q, gpost)
        df_ref[0] = df.astype(BF16)
        _acc(dgpost_ref, dgpost, first)
        _acc(dgt_ref, jnp.sum(dout * (0.5 * (nf * gpost)), axis=0, keepdims=True)[None], pl.program_id(1) == 0)

    tok = _tok_specs(tm, D)
    return pl.pallas_call(
        body, name="ffn_loss_fwd", grid=(nb, s // tm),
        out_shape=[jax.ShapeDtypeStruct((nb, s, D), F32), jax.ShapeDtypeStruct((nb, s, D), BF16),
                   jax.ShapeDtypeStruct((nb, s, 2 * DFF), BF16), jax.ShapeDtypeStruct((1, 128), F32),
                   jax.ShapeDtypeStruct((1, D), F32), jax.ShapeDtypeStruct((nb, 1, D), F32)],
        in_specs=[tok, _mod_spec(), _mod_spec(), _mod_spec(), _row_spec(), _row_spec(), VMEM_FULL, VMEM_FULL, tok],
        out_specs=[tok, tok, _tok_specs(tm, 2 * DFF), pl.BlockSpec((1, 128), lambda b, i: (0, 0)), _row_spec(), _mod_spec()],
        compiler_params=_cparams(),
    )(x, sh, sc, gt, g_pre, g_post, w_in4, w_out, target)


def _ffn_up(x, sh, sc, g_pre, w_in4):
    nb, s, _ = x.shape
    tm = min(TM, s)

    def body(x_ref, sh_ref, sc_ref, gpre_ref, win_ref, p_ref, a_ref):
        n, _ = _rms(x_ref[0])
        hb = ((n * gpre_ref[...]) * (1.0 + sc_ref[0]) + sh_ref[0]).astype(BF16)
        for j in range(2):
            gate = _dot(hb, win_ref[j])
            up = _dot(hb, win_ref[2 + j])
            p_ref[0, :, j * FBLK:(j + 1) * FBLK] = gate.astype(BF16)
            p_ref[0, :, DFF + j * FBLK:DFF + (j + 1) * FBLK] = up.astype(BF16)
            a_ref[0, :, j * FBLK:(j + 1) * FBLK] = ((gate * _sigmoid(gate)) * up).astype(BF16)

    return pl.pallas_call(
        body, name="ffn_up", grid=(nb, s // tm),
        out_shape=[jax.ShapeDtypeStruct((nb, s, 2 * DFF), BF16), jax.ShapeDtypeStruct((nb, s, DFF), BF16)],
        in_specs=[_tok_specs(tm, D), _mod_spec(), _mod_spec(), _row_spec(), VMEM_FULL],
        out_specs=[_tok_specs(tm, 2 * DFF), _tok_specs(tm, DFF)],
        compiler_params=_cparams(),
    )(x, sh, sc, g_pre, w_in4)


def _ffn_down(x, a, gt, g_post, w_out):
    nb, s, _ = x.shape
    tm = min(TM, s)

    def body(x_ref, a_ref, gt_ref, gpost_ref, wout_ref, xo_ref, f_ref):
        acc = _dot(a_ref[0], wout_ref[...])
        f_ref[0] = acc
        nf, _ = _rms(acc)
        xo_ref[0] = x_ref[0] + (0.5 * gt_ref[0]) * (nf * gpost_ref[...])

    tok = _tok_specs(tm, D)
    shape = jax.ShapeDtypeStruct((nb, s, D), F32)
    return pl.pallas_call(
        body, name="ffn_down", grid=(nb, s // tm), out_shape=[shape, shape],
        in_specs=[tok, _tok_specs(tm, DFF), _mod_spec(), _row_spec(), VMEM_FULL],
        out_specs=[tok, tok],
        compiler_params=_cparams(),
    )(x, a, gt, g_post, w_out)


def _ffn_bwd(dxo, x, f, p, sh, sc, gt, g_pre, g_post, w_in4, w_out, df=None):
    nb, s, _ = x.shape
    tm = min(TM_FFN_BWD, s)
    given = df is not None

    def body(*refs):
        if given:
            (dxo_ref, x_ref, dfin_ref, p_ref, sh_ref, sc_ref, gpre_ref, win_ref, wout_ref,
             dx_ref, dp_ref, h_ref, a_ref, dgpre_ref, dsh_ref, dsc_ref) = refs
        else:
            (dxo_ref, x_ref, f_ref, p_ref, sh_ref, sc_ref, gt_ref, gpre_ref, gpost_ref, win_ref, wout_ref,
             dx_ref, dp_ref, h_ref, a_ref, df_ref, dgpre_ref, dgpost_ref, dsh_ref, dsc_ref, dgt_ref) = refs
        b, i = pl.program_id(0), pl.program_id(1)
        dxo_v = dxo_ref[0]
        if given:
            dfb = dfin_ref[0]
        else:
            nf, q = _rms(f_ref[0])
            gpost = gpost_ref[...]
            dgt = jnp.sum(dxo_v * (0.5 * (nf * gpost)), axis=0, keepdims=True)
            do = dxo_v * (0.5 * gt_ref[0])
            dfv, dgpost = _rms_bwd(do, nf, q, gpost)
            dfb = dfv.astype(BF16)
            df_ref[0] = dfb
        xv = x_ref[0]
        n, r = _rms(xv)
        gpre = gpre_ref[...]
        ng = n * gpre
        scale1 = 1.0 + sc_ref[0]
        h = ng * scale1 + sh_ref[0]
        h_ref[0] = h.astype(BF16)
        dh = jnp.zeros((tm, D), F32)
        for j in range(2):
            gate = p_ref[0, :, j * FBLK:(j + 1) * FBLK].astype(F32)
            up = p_ref[0, :, DFF + j * FBLK:DFF + (j + 1) * FBLK].astype(F32)
            sg = _sigmoid(gate)
            act = gate * sg
            a_ref[0, :, j * FBLK:(j + 1) * FBLK] = (act * up).astype(BF16)
            da = _dot_nt(dfb, wout_ref[j * FBLK:(j + 1) * FBLK, :])
            dgate = (da * up * _dsilu(gate, sg)).astype(BF16)
            dup = (da * act).astype(BF16)
            dp_ref[0, :, j * FBLK:(j + 1) * FBLK] = dgate
            dp_ref[0, :, DFF + j * FBLK:DFF + (j + 1) * FBLK] = dup
            dh = dh + _dot_nt(dgate, win_ref[j]) + _dot_nt(dup, win_ref[2 + j])
        dsh = jnp.sum(dh, axis=0, keepdims=True)
        dsc = jnp.sum(dh * ng, axis=0, keepdims=True)
        dxn, dgpre = _rms_bwd(dh * scale1, n, r, gpre)
        dx_ref[0] = dxo_v + dxn
        _acc(dgpre_ref, dgpre, _first(b, i))
        _acc(dsh_ref, dsh[None], i == 0)
        _acc(dsc_ref, dsc[None], i == 0)
        if not given:
            _acc(dgpost_ref, dgpost, _first(b, i))
            _acc(dgt_ref, dgt[None], i == 0)

    tok = _tok_specs(tm, D)
    mod_shape = jax.ShapeDtypeStruct((nb, 1, D), F32)
    row_shape = jax.ShapeDtypeStruct((1, D), F32)
    big = [jax.ShapeDtypeStruct((nb, s, D), F32), jax.ShapeDtypeStruct((nb, s, 2 * DFF), BF16),
           jax.ShapeDtypeStruct((nb, s, D), BF16), jax.ShapeDtypeStruct((nb, s, DFF), BF16)]
    big_specs = [tok, _tok_specs(tm, 2 * DFF), tok, _tok_specs(tm, DFF)]
    if given:
        return pl.pallas_call(
            body, name="ffn_bwd_after_loss", grid=(nb, s // tm),
            out_shape=big + [row_shape, mod_shape, mod_shape],
            in_specs=[tok, tok, tok, _tok_specs(tm, 2 * DFF), _mod_spec(), _mod_spec(), _row_spec(), VMEM_FULL, VMEM_FULL],
            out_specs=big_specs + [_row_spec(), _mod_spec(), _mod_spec()],
            compiler_params=_cparams(),
        )(dxo, x, df, p, sh, sc, g_pre, w_in4, w_out)
    return pl.pallas_call(
        body, name="ffn_bwd", grid=(nb, s // tm),
        out_shape=big + [jax.ShapeDtypeStruct((nb, s, D), BF16), row_shape, row_shape, mod_shape, mod_shape, mod_shape],
        in_specs=[tok, tok, tok, _tok_specs(tm, 2 * DFF), _mod_spec(), _mod_spec(), _mod_spec(), _row_spec(), _row_spec(),
                  VMEM_FULL, VMEM_FULL],
        out_specs=big_specs + [tok, _row_spec(), _row_spec(), _mod_spec(), _mod_spec(), _mod_spec()],
        compiler_params=_cparams(),
    )(dxo, x, f, p, sh, sc, gt, g_pre, g_post, w_in4, w_out)


def _wgrad(name, a, b, col_block, chip_major):
    t, ka = a.shape
    n = b.shape[1]
    def vmem_bytes(rows):
        return 2 * 2 * rows * (ka + col_block) + 4 * ka * col_block + 2 * (4 + 2) * ka * col_block

    tk = min(t, 512)
    while tk * 2 <= t and t % (tk * 2) == 0 and vmem_bytes(tk * 2) <= WGRAD_VMEM_BUDGET:
        tk *= 2
    nk = t // tk
    nblk = n // col_block

    def body(a_ref, b_ref, o_ref, obf_ref, acc_ref):
        k = pl.program_id(1)

        @pl.when(k == 0)
        def _():
            acc_ref[...] = jnp.zeros_like(acc_ref)

        acc_ref[...] += _dot_tn(a_ref[...], b_ref[...])

        @pl.when(k == nk - 1)
        def _():
            val = acc_ref[...]
            if chip_major:
                o_ref[0] = val
                obf_ref[0] = val.astype(BF16)
            else:
                o_ref[...] = val
                obf_ref[...] = val.astype(BF16)

    if chip_major:
        shape = (nblk, ka, col_block)
        ospec = pl.BlockSpec((1, ka, col_block), lambda j, k: (j, 0, 0))
    else:
        shape = (ka, n)
        ospec = pl.BlockSpec((ka, col_block), lambda j, k: (0, j))
    return pl.pallas_call(
        body, name=name, grid=(nblk, nk),
        out_shape=[jax.ShapeDtypeStruct(shape, F32), jax.ShapeDtypeStruct(shape, BF16)],
        in_specs=[pl.BlockSpec((tk, ka), lambda j, k: (k, 0)), pl.BlockSpec((tk, col_block), lambda j, k: (k, j))],
        out_specs=[ospec, ospec],
        scratch_shapes=[pltpu.VMEM((ka, col_block), F32)],
        compiler_params=_cparams(),
    )(a, b)


def _mix_in_fwd(x, sh, sc, g_pre, w_mi4):
    nb, s, _ = x.shape
    tm = min(TM, s)

    def body(x_ref, sh_ref, sc_ref, gpre_ref, w_ref, u_ref, v_ref, a_ref, g_ref):
        n, _ = _rms(x_ref[0])
        hb = ((n * gpre_ref[...]) * (1.0 + sc_ref[0]) + sh_ref[0]).astype(BF16)
        for k, o_ref in enumerate((u_ref, v_ref, a_ref, g_ref)):
            o_ref[0] = _dot(hb, w_ref[k])

    shape = jax.ShapeDtypeStruct((nb, s, WA), F32)
    return pl.pallas_call(
        body, name="mix_in_fwd", grid=(nb, s // tm),
        out_shape=[shape] * 4,
        in_specs=[_tok_specs(tm, D), _mod_spec(), _mod_spec(), _row_spec(), VMEM_FULL],
        out_specs=[_tok_specs(tm, WA)] * 4,
        compiler_params=_cparams(),
    )(x, sh, sc, g_pre, w_mi4)


def _spatial_weights(wcat_ref, transposed):
    w = wcat_ref[...]
    row = lax.broadcasted_iota(jnp.int32, w.shape, 0)
    col = lax.broadcasted_iota(jnp.int32, w.shape, 1)
    keep = ((row & (CH - 1)) <= col) if transposed else ((col & (CH - 1)) <= row)
    return jnp.where(keep, w, 0.0).astype(BF16)


def _expand_heads(vc, masks):
    return jnp.concatenate([jnp.where(mk, vc, jnp.zeros_like(vc)) for mk in masks], axis=0)


def _spatial_bias(bspt_ref):
    return bspt_ref[...]


SHIFTS = 8
TAP_ROWS = 64


def _ext_rows(tm):
    return tm + HALO + SHIFTS


def _make_shifts(ext_ref, sh_ref, tm):
    ext_ref[tm + HALO:tm + HALO + SHIFTS, :] = jnp.zeros((SHIFTS, WB), F32)
    for r in range(SHIFTS):
        sh_ref[r] = ext_ref[r:r + tm + HALO, :]


def _conv_taps(sh_ref, w_ref, tm, taps, emit):
    def block(i, carry):
        r0 = pl.multiple_of(i * TAP_ROWS, TAP_ROWS)
        acc = jnp.zeros((TAP_ROWS, WB), F32)
        for o, k in taps:
            acc = acc + w_ref[k:k + 1, :] * sh_ref[o % SHIFTS, pl.ds(r0 + SHIFTS * (o // SHIFTS), TAP_ROWS), :]
        emit(r0, acc)
        return carry

    lax.fori_loop(0, tm // TAP_ROWS, block, 0)


def _halo_prev_spec(tm):
    return pl.BlockSpec((1, HALO, WB), lambda b, i: (b, jnp.maximum(i * (tm // HALO) - 1, 0), 0))


def _halo_next_spec(tm, s):
    return pl.BlockSpec((1, HALO, WB), lambda b, i: (b, jnp.minimum((i + 1) * (tm // HALO), s // HALO - 1), 0))


def _mix_mid_fwd(x, u, v, a, g, gt, gn_g, gn_b, wcat, bspt, conv_w, conv_b, cn_g, cn_b, go_a, go_b, w_mo, g_post):
    nb, s, _ = x.shape
    tm = min(TM, s)

    def body(x_ref, u_ref, v_ref, a_ref, g_ref, ah_ref, gh_ref, gt_ref, gng_ref, gnb_ref, wcat_ref, bspt_ref,
             cw_ref, cb_ref, cng_ref, cnb_ref, goa_ref, gob_ref, wmo_ref, gpost_ref,
             xo_ref, conv_ref, y_ref, m_ref, ext_ref, sh_ref):
        i = pl.program_id(1)
        xhat, _ = _ln(v_ref[0])
        vb = (xhat * gng_ref[...] + gnb_ref[...]).astype(BF16)
        wsb = _spatial_weights(wcat_ref, False)
        bias = _spatial_bias(bspt_ref)
        masks = _head_mask((CH, WA))
        zs = []
        for cidx in range(tm // CH):
            vexp = _expand_heads(vb[cidx * CH:(cidx + 1) * CH, :], masks)
            zs.append(_dot(wsb, vexp) + bias)
        z = jnp.concatenate(zs, axis=0)
        na, _ = _rms(u_ref[0] * z)
        keep = jnp.where(i == 0, 0.0, 1.0).astype(F32)
        ext_ref[0:HALO, :] = (ah_ref[0] * _sigmoid(gh_ref[0])) * keep
        ext_ref[HALO:HALO + tm, :] = a_ref[0] * _sigmoid(g_ref[0])
        _make_shifts(ext_ref, sh_ref, tm)
        cb = cb_ref[...]

        def put_conv(r0, acc):
            conv_ref[0, pl.ds(r0, TAP_ROWS), :] = acc + cb

        _conv_taps(sh_ref, cw_ref, tm, [(k + HALO - (CK - 1), k) for k in range(CK)], put_conv)
        conv = conv_ref[0]
        chat, _ = _ln(conv)
        cln = chat * cng_ref[...] + cnb_ref[...]
        nbb, _ = _rms(cln * _sigmoid(cln))
        yb = jnp.concatenate([na * goa_ref[...], nbb * gob_ref[...]], axis=1).astype(BF16)
        y_ref[0] = yb
        m = _dot(yb, wmo_ref[...])
        m_ref[0] = m
        nm, _ = _rms(m)
        xo_ref[0] = x_ref[0] + gt_ref[0] * (nm * gpost_ref[...])

    t5 = _tok_specs(tm, WA)
    tok = _tok_specs(tm, D)
    r5 = _row_spec(WA)
    full = lambda shape: pl.BlockSpec(shape, lambda b, i: (0,) * len(shape))
    return pl.pallas_call(
        body, name="mix_mid_fwd", grid=(nb, s // tm),
        out_shape=[jax.ShapeDtypeStruct((nb, s, D), F32), jax.ShapeDtypeStruct((nb, s, WB), F32),
                   jax.ShapeDtypeStruct((nb, s, D), BF16), jax.ShapeDtypeStruct((nb, s, D), F32)],
        in_specs=[tok, t5, t5, t5, t5, _halo_prev_spec(tm), _halo_prev_spec(tm), _mod_spec(), r5, r5,
                  full((CH, NH * CH)), full((CH, WA)), full((HALO, WB)), r5, r5, r5, r5, r5, VMEM_FULL, _row_spec()],
        out_specs=[tok, t5, tok, tok],
        scratch_shapes=[pltpu.VMEM((_ext_rows(tm), WB), F32), pltpu.VMEM((SHIFTS, tm + HALO, WB), F32)],
        compiler_params=_cparams(),
    )(x, u, v, a, g, a, g, gt, gn_g, gn_b, wcat, bspt, conv_w, conv_b, cn_g, cn_b, go_a, go_b, w_mo, g_post)


def _mix_out_bwd(dxo, m, gt, g_post, w_mo):
    nb, s, _ = m.shape
    tm = min(TM, s)

    def body(dxo_ref, m_ref, gt_ref, gpost_ref, wmo_ref, dy_ref, dm_ref, dgpost_ref, dgt_ref):
        b, i = pl.program_id(0), pl.program_id(1)
        dxo_v = dxo_ref[0]
        nm, q = _rms(m_ref[0])
        gpost = gpost_ref[...]
        dgt = jnp.sum(dxo_v * (nm * gpost), axis=0, keepdims=True)
        dm, dgpost = _rms_bwd(dxo_v * gt_ref[0], nm, q, gpost)
        dmb = dm.astype(BF16)
        dm_ref[0] = dmb
        dy_ref[0] = _dot_nt(dmb, wmo_ref[...])
        _acc(dgpost_ref, dgpost, _first(b, i))
        _acc(dgt_ref, dgt[None], i == 0)

    tok = _tok_specs(tm, D)
    return pl.pallas_call(
        body, name="mix_out_bwd", grid=(nb, s // tm),
        out_shape=[jax.ShapeDtypeStruct((nb, s, D), F32), jax.ShapeDtypeStruct((nb, s, D), BF16),
                   jax.ShapeDtypeStruct((1, D), F32), jax.ShapeDtypeStruct((nb, 1, D), F32)],
        in_specs=[tok, tok, _mod_spec(), _row_spec(), VMEM_FULL],
        out_specs=[tok, tok, _row_spec(), _mod_spec()],
        compiler_params=_cparams(),
    )(dxo, m, gt, g_post, w_mo)


def _mix_mid_bwd(dy, u, v, conv, gn_g, gn_b, wcat, wcat_t, bspt, cn_g, cn_b, go_a, go_b):
    nb, s, _ = dy.shape
    tm = min(TM, s)
    nchunk = tm // CH

    def body(dy_ref, u_ref, v_ref, conv_ref, gng_ref, gnb_ref, wcat_ref, wcatt_ref, bspt_ref, cng_ref, cnb_ref,
             goa_ref, gob_ref,
             du_ref, dv_ref, dconv_ref, dwcat_ref, dbsp_ref, dgng_ref, dgnb_ref, dgoa_ref, dgob_ref,
             dcng_ref, dcnb_ref, dcb_ref):
        first = _first(pl.program_id(0), pl.program_id(1))
        dyv = dy_ref[0]
        xhat, rstd = _ln(v_ref[0])
        gng = gng_ref[...]
        vb = (xhat * gng + gnb_ref[...]).astype(BF16)
        wsb = _spatial_weights(wcat_ref, False)
        wsb_t = _spatial_weights(wcatt_ref, True)
        bias = _spatial_bias(bspt_ref)
        masks = _head_mask((CH, WA))
        vexps, zs = [], []
        for cidx in range(nchunk):
            vexp = _expand_heads(vb[cidx * CH:(cidx + 1) * CH, :], masks)
            vexps.append(vexp)
            zs.append(_dot(wsb, vexp) + bias)
        z = jnp.concatenate(zs, axis=0)
        uv = u_ref[0]
        na, ra = _rms(uv * z)
        dya, dgoa = _rms_bwd(dyv[:, 0:WA], na, ra, goa_ref[...])
        du_ref[0] = dya * z
        dz = dya * uv
        dwcat = jnp.zeros((CH, NH * CH), F32)
        dzsum = jnp.zeros((CH, WA), F32)
        dvlns = []
        for cidx in range(nchunk):
            dzc = dz[cidx * CH:(cidx + 1) * CH, :]
            dzsum = dzsum + dzc
            dzb = dzc.astype(BF16)
            dwcat = dwcat + _dot_nt(dzb, vexps[cidx])
            dvexp = _dot(wsb_t, dzb)
            dvl = jnp.zeros((CH, WA), F32)
            for h in range(NH):
                dvl = dvl + jnp.where(masks[h], dvexp[h * CH:(h + 1) * CH, :], 0.0)
            dvlns.append(dvl)
        dvln = jnp.concatenate(dvlns, axis=0)
        dv, dgng, dgnb = _ln_bwd(dvln, xhat, rstd, gng)
        dv_ref[0] = dv
        lane = lax.broadcasted_iota(jnp.int32, (NH, WA), 1)
        head = lax.broadcasted_iota(jnp.int32, (NH, WA), 0)
        sel = jnp.where((lane >= head * HD) & (lane < (head + 1) * HD), 1.0, 0.0).astype(F32)
        dbsp = lax.dot_general(sel, dzsum, NT, preferred_element_type=F32, precision=lax.Precision.HIGHEST)
        chat, crstd = _ln(conv_ref[0])
        cng = cng_ref[...]
        cln = chat * cng + cnb_ref[...]
        sg = _sigmoid(cln)
        nbb, rb = _rms(cln * sg)
        dyb, dgob = _rms_bwd(dyv[:, WA:D], nbb, rb, gob_ref[...])
        dconv, dcng, dcnb = _ln_bwd(dyb * _dsilu(cln, sg), chat, crstd, cng)
        dconv_ref[0] = dconv
        dcb = jnp.sum(dconv, axis=0, keepdims=True)
        for ref, val in ((dwcat_ref, dwcat), (dbsp_ref, dbsp), (dgng_ref, dgng), (dgnb_ref, dgnb), (dgoa_ref, dgoa),
                         (dgob_ref, dgob), (dcng_ref, dcng), (dcnb_ref, dcnb), (dcb_ref, dcb)):
            _acc(ref, val, first)

    t5 = _tok_specs(tm, WA)
    r5 = _row_spec(WA)
    full = lambda shape: pl.BlockSpec(shape, lambda b, i: (0,) * len(shape))
    big = jax.ShapeDtypeStruct((nb, s, WA), F32)
    row = jax.ShapeDtypeStruct((1, WA), F32)
    return pl.pallas_call(
        body, name="mix_mid_bwd", grid=(nb, s // tm),
        out_shape=[big, big, big, jax.ShapeDtypeStruct((CH, NH * CH), F32), jax.ShapeDtypeStruct((NH, CH), F32),
                   row, row, row, row, row, row, row],
        in_specs=[_tok_specs(tm, D), t5, t5, t5, r5, r5, full((CH, NH * CH)), full((NH * CH, CH)), full((CH, WA)),
                  r5, r5, r5, r5],
        out_specs=[t5, t5, t5, full((CH, NH * CH)), full((NH, CH)), r5, r5, r5, r5, r5, r5, r5],
        compiler_params=_cparams(),
    )(dy, u, v, conv, gn_g, gn_b, wcat, wcat_t, bspt, cn_g, cn_b, go_a, go_b)


def _mix_in_bwd(dxo, x, du, dv, dconv, a, g, sh, sc, g_pre, w_mi4, conv_w):
    nb, s, _ = x.shape
    tm = min(TM, s)
    n_i = s // tm

    def body(dxo_ref, x_ref, du_ref, dv_ref, dc_ref, dch_ref, a_ref, g_ref, ah_ref, gh_ref, sh_ref, sc_ref,
             gpre_ref, w_ref, cw_ref,
             dx_ref, dproj_ref, h_ref, dgpre_ref, dsh_ref, dsc_ref, dcw_ref, ext_ref, shf_ref, dglu_ref):
        b, i = pl.program_id(0), pl.program_id(1)
        first = _first(b, i)
        av, gv = a_ref[0], g_ref[0]
        sg = _sigmoid(gv)
        dconv = dc_ref[0]
        ext_ref[0:tm, :] = dconv
        ext_ref[tm:tm + HALO, :] = dch_ref[0] * jnp.where(i == n_i - 1, 0.0, 1.0).astype(F32)
        _make_shifts(ext_ref, shf_ref, tm)

        def put_dglu(r0, acc):
            dglu_ref[pl.ds(r0, TAP_ROWS), :] = acc

        _conv_taps(shf_ref, cw_ref, tm, [(CK - 1 - k, k) for k in range(CK)], put_dglu)
        dglu = dglu_ref[...]
        ext_ref[0:HALO, :] = (ah_ref[0] * _sigmoid(gh_ref[0])) * jnp.where(i == 0, 0.0, 1.0).astype(F32)
        ext_ref[HALO:HALO + tm, :] = av * sg
        _make_shifts(ext_ref, shf_ref, tm)

        @pl.when(first)
        def _():
            dcw_ref[...] = jnp.zeros((HALO, WB), F32)

        for k in range(CK):
            o = k + HALO - (CK - 1)
            lo = SHIFTS * (o // SHIFTS)
            dcw_ref[k:k + 1, :] += jnp.sum(dconv * shf_ref[o % SHIFTS, lo:lo + tm, :], axis=0, keepdims=True)
        da = dglu * sg
        dg = dglu * av * (sg * (1.0 - sg))
        parts = [du_ref[0].astype(BF16), dv_ref[0].astype(BF16), da.astype(BF16), dg.astype(BF16)]
        dh = jnp.zeros((tm, D), F32)
        for k in range(4):
            dproj_ref[0, :, k * WA:(k + 1) * WA] = parts[k]
            dh = dh + _dot_nt(parts[k], w_ref[k])
        n, r = _rms(x_ref[0])
        gpre = gpre_ref[...]
        ng = n * gpre
        scale1 = 1.0 + sc_ref[0]
        h_ref[0] = (ng * scale1 + sh_ref[0]).astype(BF16)
        dsh = jnp.sum(dh, axis=0, keepdims=True)
        dsc = jnp.sum(dh * ng, axis=0, keepdims=True)
        dxn, dgpre = _rms_bwd(dh * scale1, n, r, gpre)
        dx_ref[0] = dxo_ref[0] + dxn
        _acc(dgpre_ref, dgpre, first)
        _acc(dsh_ref, dsh[None], i == 0)
        _acc(dsc_ref, dsc[None], i == 0)

    tok = _tok_specs(tm, D)
    t5 = _tok_specs(tm, WA)
    full = lambda shape: pl.BlockSpec(shape, lambda b, i: (0,) * len(shape))
    mod_shape = jax.ShapeDtypeStruct((nb, 1, D), F32)
    return pl.pallas_call(
        body, name="mix_in_bwd", grid=(nb, n_i),
        out_shape=[jax.ShapeDtypeStruct((nb, s, D), F32), jax.ShapeDtypeStruct((nb, s, 4 * WA), BF16),
                   jax.ShapeDtypeStruct((nb, s, D), BF16), jax.ShapeDtypeStruct((1, D), F32), mod_shape, mod_shape,
                   jax.ShapeDtypeStruct((HALO, WB), F32)],
        in_specs=[tok, tok, t5, t5, t5, _halo_next_spec(tm, s), t5, t5, _halo_prev_spec(tm), _halo_prev_spec(tm),
                  _mod_spec(), _mod_spec(), _row_spec(), VMEM_FULL, full((HALO, WB))],
        out_specs=[tok, _tok_specs(tm, 4 * WA), tok, _row_spec(), _mod_spec(), _mod_spec(), full((HALO, WB))],
        scratch_shapes=[pltpu.VMEM((_ext_rows(tm), WB), F32), pltpu.VMEM((SHIFTS, tm + HALO, WB), F32),
                        pltpu.VMEM((tm, WB), F32)],
        compiler_params=_cparams(),
    )(dxo, x, du, dv, dconv, dconv, a, g, a, g, sh, sc, g_pre, w_mi4, conv_w)


def _row_tile(rows, cols):
    best = 16
    for t in range(16, rows + 1, 16):
        if rows % t == 0 and t * cols * 4 <= 1536 * 1024:
            best = t
    return best


def _walk(steps):
    offs = [sum(steps[:k]) for k in range(len(steps))]

    def tile(k):
        return lambda i: jnp.clip(i - offs[k], 0, steps[k] - 1)

    def mine(k, i):
        return jnp.logical_and(i >= offs[k], i < offs[k] + steps[k])

    return sum(steps), tile, mine


def _sum4(name, own4s, recvs, j_arr):
    n = len(own4s)
    shapes = [o.shape[1:] for o in own4s]
    trs = [_row_tile(r, c) for r, c in shapes]
    total, tile, mine = _walk([r // tr for (r, _), tr in zip(shapes, trs)])

    def body(j_ref, *refs):
        del j_ref
        i = pl.program_id(0)
        for k in range(n):
            own_ref, recv_ref, o_ref = refs[2 * k], refs[2 * k + 1], refs[2 * n + k]

            def add(own_ref=own_ref, recv_ref=recv_ref, o_ref=o_ref):
                acc = own_ref[0]
                for q in range(3):
                    acc = acc + recv_ref[q].astype(F32)
                o_ref[...] = acc

            pl.when(mine(k, i))(add)

    in_specs, out_specs = [], []
    for k, ((_, cols), tr) in enumerate(zip(shapes, trs)):
        in_specs += [pl.BlockSpec((1, tr, cols), lambda i, j, t=tile(k): (j[0], t(i), 0)),
                     pl.BlockSpec((3, tr, cols), lambda i, j, t=tile(k): (0, t(i), 0))]
        out_specs.append(pl.BlockSpec((tr, cols), lambda i, j, t=tile(k): (t(i), 0)))
    return pl.pallas_call(
        body, name=name,
        grid_spec=pltpu.PrefetchScalarGridSpec(num_scalar_prefetch=1, grid=(total,), in_specs=in_specs, out_specs=out_specs),
        out_shape=[jax.ShapeDtypeStruct(sh, F32) for sh in shapes],
        compiler_params=_cparams(),
    )(j_arr, *[a for pair in zip(own4s, recvs) for a in pair])


def _pair_plan(shapes):
    def plan(x, y, c, src, land):
        sends = []
        for a, shape in enumerate(shapes):
            rows = shape[1] // 2
            theirs = pl.ds(pl.multiple_of((1 - c) * rows, 16), rows)
            sends.append((src[a].at[:, theirs], land[a], (x, y, 1 - c), land[a]))
        return [], sends

    return plan


def _swap_plan(n):
    def plan(x, y, c, src, land):
        return [], [(src[a], land[a], (x, y, 1 - c), land[a]) for a in range(n)]

    return plan


def _pair_sum(name, g32s, recvs, c_arr):
    n = len(g32s)
    shapes = [r.shape for r in recvs]
    trs = [_row_tile(rows, cols) for _, rows, cols in shapes]
    nhs = [rows // tr for (_, rows, _), tr in zip(shapes, trs)]
    total, tile, mine = _walk([nblk * nh for (nblk, _, _), nh in zip(shapes, nhs)])

    def body(c_ref, *refs):
        del c_ref
        i = pl.program_id(0)
        for k in range(n):
            g_ref, r_ref, o32_ref, obf_ref = refs[2 * k], refs[2 * k + 1], refs[2 * n + 2 * k], refs[2 * n + 2 * k + 1]

            def add(g_ref=g_ref, r_ref=r_ref, o32_ref=o32_ref, obf_ref=obf_ref):
                val = g_ref[0] + r_ref[0].astype(F32)
                o32_ref[0] = val
                obf_ref[0] = val.astype(BF16)

            pl.when(mine(k, i))(add)

    in_specs, out_specs, out_shape = [], [], []
    for k, ((_, _, cols), tr, nh) in enumerate(zip(shapes, trs, nhs)):
        def half(tr=tr, cols=cols, t=tile(k), nh=nh):
            return pl.BlockSpec((1, tr, cols), lambda i, c: (t(i) // nh, t(i) % nh, 0))

        in_specs += [pl.BlockSpec((1, tr, cols), lambda i, c, t=tile(k), nh=nh: (t(i) // nh, c[0] * nh + t(i) % nh, 0)), half()]
        out_specs += [half(), half()]
        out_shape += [jax.ShapeDtypeStruct(shapes[k], F32), jax.ShapeDtypeStruct(shapes[k], BF16)]
    res = pl.pallas_call(
        body, name=name,
        grid_spec=pltpu.PrefetchScalarGridSpec(num_scalar_prefetch=1, grid=(total,), in_specs=in_specs, out_specs=out_specs),
        out_shape=out_shape,
        compiler_params=_cparams(),
    )(c_arr, *[a for pair in zip(g32s, recvs) for a in pair])
    return [(res[2 * k], res[2 * k + 1]) for k in range(n)]


def _adam_halves(name, w, m, v, mine, theirs, c_arr):
    rows, cols = w.shape
    tr = _row_tile(rows // 2, cols)
    nh = (rows // 2) // tr

    def body(c_ref, w_ref, m_ref, v_ref, mine_ref, theirs_ref, g_out, d_out, m_out, v_out):
        here = (pl.program_id(0) // nh) == c_ref[0]
        g = jnp.where(here, mine_ref[...], theirs_ref[...])
        delta, m2, v2 = _adam(w_ref[...], g, m_ref[...], v_ref[...])
        g_out[...] = g
        d_out[...] = delta
        m_out[...] = m2
        v_out[...] = v2

    spec = pl.BlockSpec((tr, cols), lambda i, c: (i, 0))
    shape = jax.ShapeDtypeStruct((rows, cols), F32)
    return pl.pallas_call(
        body, name=name,
        grid_spec=pltpu.PrefetchScalarGridSpec(
            num_scalar_prefetch=1, grid=(2 * nh,),
            in_specs=[spec, spec, spec,
                      pl.BlockSpec((tr, cols), lambda i, c: (jnp.clip(i - c[0] * nh, 0, nh - 1), 0)),
                      pl.BlockSpec((tr, cols), lambda i, c: (jnp.clip(i - (1 - c[0]) * nh, 0, nh - 1), 0))],
            out_specs=[spec] * 4),
        out_shape=[shape] * 4,
        compiler_params=_cparams(),
    )(c_arr, w, m, v, mine, theirs)


def _adam_big(name, w, m, v, ga, gb):
    rows, cols = w.shape
    tr = _row_tile(rows, cols)

    def body(w_ref, m_ref, v_ref, ga_ref, gb_ref, g_out, d_out, m_out, v_out):
        gsum = ga_ref[...] + gb_ref[...]
        delta, m2, v2 = _adam(w_ref[...], gsum, m_ref[...], v_ref[...])
        g_out[...] = gsum
        d_out[...] = delta
        m_out[...] = m2
        v_out[...] = v2

    spec = pl.BlockSpec((tr, cols), lambda i: (i, 0))
    shape = jax.ShapeDtypeStruct((rows, cols), F32)
    return pl.pallas_call(
        body, name=name, grid=(rows // tr,), out_shape=[shape] * 4,
        in_specs=[spec] * 5, out_specs=[spec] * 4, compiler_params=_cparams(),
    )(w, m, v, ga, gb)


PK_VEC = 0
PK_LOSS = 6
PK_PAIR = 8
PK_BSP = 16
PK_WCAT = 24
PK_ROWS = PK_WCAT + CH
PAIR_ORDER = ("gmlp_norm_g", "gmlp_norm_b", "conv_b", "conv_norm_g", "conv_norm_b", "g_out_a", "g_out_b")
VEC_ORDER = ("g_pre_f1", "g_post_f1", "g_pre_m", "g_post_m", "g_pre_f2", "g_post_f2")


def _pack_late(rows):
    counts = [r.shape[0] for r in rows]
    assert sum(counts) == 8

    def body(*refs):
        o_ref = refs[-1]
        at = 0
        for r, cnt in zip(refs[:-1], counts):
            o_ref[at:at + cnt, :] = r[...]
            at += cnt

    return pl.pallas_call(
        body, name="pack_late", out_shape=jax.ShapeDtypeStruct((8, D), F32),
        in_specs=[VMEM_FULL] * len(rows), out_specs=VMEM_FULL, compiler_params=_cparams(),
    )(*rows)


def _pack_small(vecs, pairs, dbsp, dwcat, lsum):
    def body(*refs):
        vec_refs = refs[:4]
        pair_refs = refs[4:11]
        dbsp_ref, dwcat_ref, lsum_ref, o_ref = refs[11:]
        o_ref[0:PK_WCAT, :] = jnp.zeros((PK_WCAT, D), F32)
        o_ref[PK_LOSS:PK_LOSS + 1, 0:128] = lsum_ref[...]
        for k, r in enumerate(vec_refs):
            o_ref[PK_VEC + 2 + k:PK_VEC + 3 + k, :] = r[...]
        for k, r in enumerate(pair_refs):
            row, half = PK_PAIR + k // 2, k % 2
            o_ref[row:row + 1, half * WA:(half + 1) * WA] = r[...]
        o_ref[PK_BSP:PK_BSP + NH, 0:CH] = dbsp_ref[...]
        o_ref[PK_WCAT:PK_ROWS, :] = dwcat_ref[...]

    args = list(vecs) + list(pairs) + [dbsp, dwcat, lsum]
    return pl.pallas_call(
        body, name="pack_small", out_shape=jax.ShapeDtypeStruct((PK_ROWS, D), F32),
        in_specs=[VMEM_FULL] * len(args), out_specs=VMEM_FULL, compiler_params=_cparams(),
    )(*args)


def _small_adam(pack_all, late_all, dcw_all, dada_all, params, behind):
    names = list(VEC_ORDER) + list(PAIR_ORDER) + ["b_spatial", "w_spatial", "conv_w", "b_ada"]
    flat = []
    for nm in names:
        flat += list(params[nm])
    n_in = 4 + len(flat)

    def body(*refs):
        pack_ref, late_ref, dcw_ref, dada_ref = refs[:4]
        prm = refs[4:n_in]
        outs = refs[n_in + 1:]

        def total(r0, nr, c0, nc):
            acc = pack_ref[0, r0:r0 + nr, c0:c0 + nc]
            for d in range(1, NDEV):
                acc = acc + pack_ref[d, r0:r0 + nr, c0:c0 + nc]
            return acc

        def emit(idx, g, getw, put):
            w_ref, m_ref, v_ref = prm[3 * idx:3 * idx + 3]
            delta, m2, v2 = _adam(getw(w_ref), g, getw(m_ref), getw(v_ref))
            for o_ref, val in zip(outs[4 * idx:4 * idx + 4], (g, delta, m2, v2)):
                put(o_ref, val)

        def whole(ref):
            return ref[...]

        def put_whole(ref, val):
            ref[...] = val

        idx = 0
        for k in range(6):
            if k < 2:
                g = late_ref[0, k:k + 1, :]
                for d in range(1, NDEV):
                    g = g + late_ref[d, k:k + 1, :]
            else:
                g = total(PK_VEC + k, 1, 0, D)
            emit(idx, g, whole, put_whole)
            idx += 1
        for k in range(7):
            emit(idx, total(PK_PAIR + k // 2, 1, (k % 2) * WA, WA), whole, put_whole)
            idx += 1
        emit(idx, total(PK_BSP, NH, 0, CH), lambda r: r[0], lambda r, val: r.__setitem__(0, val))
        idx += 1
        row = lax.broadcasted_iota(jnp.int32, (CH, CH), 0)
        col = lax.broadcasted_iota(jnp.int32, (CH, CH), 1)
        for h in range(NH):
            gh = jnp.where(col <= row, total(PK_WCAT, CH, h * CH, CH), 0.0)
            w_ref, m_ref, v_ref = prm[3 * idx:3 * idx + 3]
            delta, m2, v2 = _adam(w_ref[0, h], gh, m_ref[0, h], v_ref[0, h])
            for o_ref, val in zip(outs[4 * idx:4 * idx + 4], (gh, delta, m2, v2)):
                o_ref[0, h] = val
        idx += 1
        gcw = dcw_ref[0, 0:CK, :]
        for d in range(1, NDEV):
            gcw = gcw + dcw_ref[d, 0:CK, :]
        emit(idx, gcw, lambda r: r[0], lambda r, val: r.__setitem__(0, val))
        idx += 1
        emit(idx, jnp.sum(dada_ref[...], axis=0, keepdims=True), whole, put_whole)
        outs[-1][...] = jnp.sum(total(PK_LOSS, 1, 0, 128), axis=1, keepdims=True) * (0.5 / D)

    out_shape = []
    for nm in names:
        w = params[nm][0]
        out_shape += [jax.ShapeDtypeStruct(w.shape, F32)] * 4
    out_shape.append(jax.ShapeDtypeStruct((1, 1), F32))
    res = pl.pallas_call(
        body, name="small_adam", out_shape=out_shape,
        in_specs=[VMEM_FULL] * n_in + [ANY], out_specs=[VMEM_FULL] * len(out_shape), compiler_params=_cparams(),
    )(pack_all, late_all, dcw_all, dada_all, *flat, behind)
    return {nm: tuple(res[4 * k:4 * k + 4]) for k, nm in enumerate(names)}, res[-1].reshape(())


WEIGHTS = ['w_ada', 'b_ada', 'g_pre_f1', 'g_post_f1', 'w_f1_in', 'w_f1_out', 'g_pre_m', 'g_post_m', 'w_mix_in',
           'gmlp_norm_g', 'gmlp_norm_b', 'w_spatial', 'b_spatial', 'conv_w', 'conv_b', 'conv_norm_g', 'conv_norm_b',
           'g_out_a', 'g_out_b', 'w_mix_out', 'g_pre_f2', 'g_post_f2', 'w_f2_in', 'w_f2_out']


def kernel(x, c, w_ada, b_ada, g_pre_f1, g_post_f1, w_f1_in, w_f1_out, g_pre_m, g_post_m, w_mix_in, gmlp_norm_g, gmlp_norm_b, w_spatial, b_spatial, conv_w, conv_b, conv_norm_g, conv_norm_b, g_out_a, g_out_b, w_mix_out, g_pre_f2, g_post_f2, w_f2_in, w_f2_out, loss_target, m_w_ada, m_b_ada, m_g_pre_f1, m_g_post_f1, m_w_f1_in, m_w_f1_out, m_g_pre_m, m_g_post_m, m_w_mix_in, m_gmlp_norm_g, m_gmlp_norm_b, m_w_spatial, m_b_spatial, m_conv_w, m_conv_b, m_conv_norm_g, m_conv_norm_b, m_g_out_a, m_g_out_b, m_w_mix_out, m_g_pre_f2, m_g_post_f2, m_w_f2_in, m_w_f2_out, v_w_ada, v_b_ada, v_g_pre_f1, v_g_post_f1, v_w_f1_in, v_w_f1_out, v_g_pre_m, v_g_post_m, v_w_mix_in, v_gmlp_norm_g, v_gmlp_norm_b, v_w_spatial, v_b_spatial, v_conv_w, v_conv_b, v_conv_norm_g, v_conv_norm_b, v_g_out_a, v_g_out_b, v_w_mix_out, v_g_pre_f2, v_g_post_f2, v_w_f2_in, v_w_f2_out):
    env = dict(locals())
    wts = {n: env[n] for n in WEIGHTS}
    mom = {n: env["m_" + n] for n in WEIGHTS}
    var = {n: env["v_" + n] for n in WEIGHTS}
    nb, s, _ = x.shape
    t = nb * s
    ax, ay, ac = lax.axis_index("x"), lax.axis_index("y"), lax.axis_index("c")
    j_chip = 2 * ax + ay
    dev = 4 * ax + 2 * ay + ac
    j_arr = j_chip.reshape(1).astype(jnp.int32)

    groups = (("w_f1_in",), ("w_mix_in", "w_mix_out"), ("w_f2_in", "w_f2_out"), ("w_f1_out",))
    def gather_operands(gi):
        srcs = [wts[n][0].astype(BF16) for n in groups[gi]] + ([conv_w[0]] if gi == 1 else [])
        lands = [lax.dynamic_update_index_in_dim(lax.empty((NCHIP,) + a.shape, a.dtype), a, j_chip, 0) for a in srcs]
        return srcs, lands

    def gather_start(gi, behind, operands=None):
        srcs, lands = operands or gather_operands(gi)
        plan_a, plan_b, n_b = _gather_plans([a.shape for a in srcs])
        ssem, rsem, srcs, lands, token = _split_start("gw_start%d" % gi, srcs, lands, plan_a, 3 * len(srcs), behind)
        gather[gi] = (srcs, lands, ssem, rsem, plan_a, plan_b, n_b)
        return token

    def gather_forward(gi, behind):
        srcs, lands, ssem, rsem, plan_a, plan_b, n_b = gather[gi]
        ssem, rsem, lands, token = _split_forward("gw_fwd%d" % gi, srcs, lands, ssem, rsem, plan_a, plan_b, n_b, behind)
        gather[gi] = (lands, ssem, rsem, plan_b)
        return token

    def gathered(gi, behind):
        lands, ssem, rsem, plan_b = gather[gi]
        return _split_wait("gw_wait%d" % gi, [], lands, ssem, rsem, plan_b, behind)

    gather = {}
    def allgather_start(tag, arrs, behind):
        lands = [lax.dynamic_update_index_in_dim(lax.empty((NDEV,) + a.shape, a.dtype), a, dev, 0) for a in arrs]
        ssem, rsem, srcs, lands, token = _split_start("small_start_" + tag, arrs, lands, _allgather_plan(len(arrs)),
                                                      7 * len(arrs), behind)
        return (srcs, lands, ssem, rsem), token

    def allgather_wait(tag, state, behind):
        srcs, lands, ssem, rsem = state
        return _split_wait("small_wait_" + tag, srcs, lands, ssem, rsem, _allgather_plan(len(srcs)), behind)

    c_state, token = allgather_start("c", [c.reshape(8, (nb * D) // 8)], c)
    token = gather_start(0, token)
    (c_all8,) = allgather_wait("c", c_state, token)
    c_all = c_all8.reshape(NDEV * nb, D)
    b_sh = lax.dynamic_slice(b_ada, (0, j_chip * ADA_SH), (1, ADA_SH))
    ada_sh = _ada_fwd(c_all, w_ada[0], b_sh)
    later = [gather_operands(3), gather_operands(1), gather_operands(2)]
    (ada4,) = _chip_allgather("gather_ada", [ada_sh], behind=[a for pair in later for arrs in pair for a in arrs])
    token = gather_forward(0, ada4)
    plans = [_gather_plans([a.shape for a in srcs]) for srcs, _ in later]
    started, token = _split_start_groups(
        "gw_start_later", [(srcs, lands, pa, 3 * len(srcs)) for (srcs, lands), (pa, _, _) in zip(later, plans)], token)
    for gi, (ssem, rsem, srcs, lands), (pa, pb, n_b) in zip((3, 1, 2), started, plans):
        gather[gi] = (srcs, lands, ssem, rsem, pa, pb, n_b)
    ada_me = lax.dynamic_slice(ada4, (0, dev * nb, 0), (NCHIP, nb, ADA_SH))
    ada_me = jnp.transpose(ada_me, (1, 0, 2)).reshape(nb, NMOD * D)
    sh1, sc1, gt1, sh2, sc2, gt2, sh3, sc3, gt3 = [ada_me[:, k * D:(k + 1) * D].reshape(nb, 1, D) for k in range(NMOD)]

    wcat = jnp.transpose(w_spatial[0], (1, 0, 2)).reshape(CH, NH * CH)
    wcat_t = jnp.transpose(w_spatial[0], (0, 2, 1)).reshape(NH * CH, CH)
    bspt = jnp.repeat(b_spatial[0].T, HD, axis=1)

    (w1i,) = gathered(0, token)
    p1, act1 = _ffn_up(x, sh1, sc1, g_pre_f1, w1i)
    forwarded, token = _split_forward_groups("gw_fwd_f1_out_mix", [gather[3], gather[1]], act1)
    for gi, (ssem, rsem, lands) in zip((3, 1), forwarded):
        gather[gi] = (lands, ssem, rsem, gather[gi][5])
    (w1o,) = gathered(3, token)
    w1o = w1o.reshape(DFF, D)
    x1, f1 = _ffn_down(x, act1, gt1, g_post_f1, w1o)
    wmi, wmo, cw4 = gathered(1, x1)
    wmo = wmo.reshape(D, D)
    cw_full = jnp.transpose(cw4, (1, 0, 2)).reshape(CK, WB)
    cw_pad = jnp.pad(cw_full, ((0, HALO - CK), (0, 0)))
    u, v, a, g = _mix_in_fwd(x1, sh2, sc2, g_pre_m, wmi)
    token = gather_forward(2, u)
    x2, conv, yb, m = _mix_mid_fwd(x1, u, v, a, g, gt2 + token[0, 0], gmlp_norm_g, gmlp_norm_b, wcat, bspt, cw_pad, conv_b,
                                   conv_norm_g, conv_norm_b, g_out_a, g_out_b, wmo, g_post_m)
    w2i, w2o = gathered(2, [x2, token])
    w2o = w2o.reshape(DFF, D)
    dx3, df2, p2, lsum, dg_post_f2, dgt3 = _ffn_loss_fwd(x2, sh3, sc3, gt3, g_pre_f2, g_post_f2, w2i, w2o, loss_target)

    def chip4(pair, rows):
        return [arr.reshape(NCHIP, rows, arr.shape[-1]) for arr in pair]

    def scatter_start(tag, pairs, behind):
        srcs = [p[1] for p in pairs]
        lands = [lax.empty((3,) + a.shape[1:], a.dtype) for a in srcs]
        ssem, rsem, srcs, lands, token = _split_start("gs_start_" + tag, srcs, lands, _scatter_plan(len(srcs)),
                                                      3 * len(srcs), behind)
        return (srcs, lands, ssem, rsem), token

    def scatter_wait(tag, state, behind):
        srcs, lands, ssem, rsem = state
        return _split_wait("gs_wait_" + tag, srcs, lands, ssem, rsem, _scatter_plan(len(srcs)), behind)

    out = {}
    dx2, dp2, h3, a2, dg_pre_f2, dsh3, dsc3 = _ffn_bwd(
        dx3, x2, None, p2, sh3, sc3, gt3, g_pre_f2, g_post_f2, w2i, w2o, df=df2)
    gw2i = _wgrad("wgrad_f2_in", h3.reshape(t, D), dp2.reshape(t, 2 * DFF), 2 * DFF // NCHIP, True)
    gw2o = chip4(_wgrad("wgrad_f2_out", a2.reshape(t, DFF), df2.reshape(t, D), D // 2, False), DFF // NCHIP)
    scat_f2, tok = scatter_start("f2", [gw2i, gw2o], dg_post_f2)
    dy, dm, dg_post_m, dgt2 = _mix_out_bwd(dx2, m, gt2 + tok[0, 0], g_post_m, wmo)
    gwmo = chip4(_wgrad("wgrad_mix_out", yb.reshape(t, D), dm.reshape(t, D), D // 2, False), D // NCHIP)
    (du, dv, dconv, dwcat, dbsp, dgn_g, dgn_b, dgo_a, dgo_b, dcn_g, dcn_b, dcb) = _mix_mid_bwd(
        dy, u, v, conv, gmlp_norm_g, gmlp_norm_b, wcat, wcat_t, bspt, conv_norm_g, conv_norm_b, g_out_a, g_out_b)
    dx1, dproj, h2, dg_pre_m, dsh2, dsc2, dcw = _mix_in_bwd(dx2, x1, du, dv, dconv, a, g, sh2, sc2, g_pre_m, wmi, cw_pad)
    gwmi = _wgrad("wgrad_mix_in", h2.reshape(t, D), dproj.reshape(t, 4 * WA), WA, True)

    vec_grads = dict(g_pre_m=dg_pre_m, g_post_m=dg_post_m, g_pre_f2=dg_pre_f2, g_post_f2=dg_post_f2)
    pair_grads = dict(gmlp_norm_g=dgn_g, gmlp_norm_b=dgn_b, conv_b=dcb, conv_norm_g=dcn_g, conv_norm_b=dcn_b,
                      g_out_a=dgo_a, g_out_b=dgo_b)
    pack = _pack_small([vec_grads[n] for n in VEC_ORDER[2:]], [pair_grads[n] for n in PAIR_ORDER], dbsp, dwcat, lsum)
    dada_early = jnp.concatenate([q.reshape(nb, D) for q in (dsh2, dsc2, dgt2, dsh3, dsc3, dgt3)], axis=1)
    small_early = [pack, dcw, dada_early.reshape(8, (nb * 6 * D) // 8)]
    mix_bf16 = [gwmi[1], gwmo[1]]
    (s_mix, s_early), tok2 = _split_start_groups("gs_start_mix_small", [
        (mix_bf16, [lax.empty((3,) + a.shape[1:], a.dtype) for a in mix_bf16], _scatter_plan(2), 6),
        (small_early, [lax.dynamic_update_index_in_dim(lax.empty((NDEV,) + a.shape, a.dtype), a, dev, 0) for a in small_early],
         _allgather_plan(3), 21)], dg_pre_m)
    scat_mix = (s_mix[2], s_mix[3], s_mix[0], s_mix[1])
    early = (s_early[2], s_early[3], s_early[0], s_early[1])
    grad_x, dp1, h1, a1, df1, dg_pre_f1, dg_post_f1, dsh1, dsc1, dgt1 = _ffn_bwd(
        dx1, x, f1, p1, sh1 + tok2[0, 0], sc1, gt1, g_pre_f1, g_post_f1, w1i, w1o)
    late_pack = _pack_late([dg_pre_f1, dg_post_f1] + [q.reshape(nb, D) for q in (dsh1, dsc1, dgt1)])
    late, tok2 = allgather_start("late", [late_pack], dg_post_f1)
    gw1i = _wgrad("wgrad_f1_in", h1.reshape(t, D), dp1.reshape(t, 2 * DFF), 2 * DFF // NCHIP, True)
    gw1o = chip4(_wgrad("wgrad_f1_out", a1.reshape(t, DFF), df1.reshape(t, D), D // 2, False), DFF // NCHIP)
    def d2d_start(tag, srcs, lands, plan, behind):
        ssem, rsem, srcs, lands, token = _split_start("d2d_start_" + tag, srcs, lands, plan, len(srcs), behind)
        return (srcs, lands, ssem, rsem, plan), token

    def d2d_wait(tag, state, behind):
        srcs, lands, ssem, rsem, plan = state
        return _split_wait("d2d_wait_" + tag, srcs, lands, ssem, rsem, plan, behind)

    def swap_start(tag, parts, behind):
        return d2d_start(tag, parts, [lax.empty(a.shape, a.dtype) for a in parts], _swap_plan(len(parts)), behind)

    def sums(names, pairs, recv):
        return _sum4("sum4_" + names[0][2:4], [p[0] for p in pairs], recv, j_arr)

    def update(names, part, other):
        for k, n in enumerate(names):
            out[n] = tuple(r[None] for r in _adam_big("adam_" + n, wts[n][0], mom[n][0], var[n][0], part[k], other[k]))

    c_arr = ac.reshape(1).astype(jnp.int32)
    halves = [gw1i[1], gw1o[1]]
    pair_st, tok = d2d_start("pair", halves, [lax.empty((a.shape[0], a.shape[1] // 2, a.shape[2]), a.dtype) for a in halves],
                             _pair_plan([a.shape for a in halves]), tok2)
    names_f2, names_mix, names_f1 = ("w_f2_in", "w_f2_out"), ("w_mix_in", "w_mix_out"), ("w_f1_in", "w_f1_out")
    part_f2 = sums(names_f2, [gw2i, gw2o], scatter_wait("f2", scat_f2, tok))
    sib = d2d_wait("pair", pair_st, part_f2)
    pair_i, pair_o = _pair_sum("pairsum_f1", [gw1i[0], gw1o[0]], sib, c_arr)
    scat_f1, tok = scatter_start("f1", [pair_i, pair_o], tok2)
    swap_f2, tok = swap_start("swap_f2", part_f2, tok)
    part_mix = sums(names_mix, [gwmi, gwmo], scatter_wait("mix", scat_mix, tok))
    swap_mix, tok = swap_start("swap_mix", part_mix, part_mix[1])

    pack_all, dcw_all, dada_early8 = allgather_wait("early", early, tok)
    (late_all,) = allgather_wait("late", late, pack_all)
    dada_late = jnp.transpose(late_all[:, 2:8, :].reshape(NDEV, 3, nb, D), (0, 2, 1, 3)).reshape(NDEV * nb, 3 * D)
    dada_all = jnp.concatenate([dada_late, dada_early8.reshape(NDEV * nb, 6 * D)], axis=1)
    dada_sh = lax.dynamic_slice(dada_all, (0, j_chip * ADA_SH), (NDEV * nb, ADA_SH))
    out["w_ada"] = tuple(r[None] for r in _ada_bwd_adam(c_all, dada_sh, w_ada[0], m_w_ada[0], v_w_ada[0]))
    update(names_f2, part_f2, d2d_wait("swap_f2", swap_f2, out["w_ada"][3]))
    update(names_mix, part_mix, d2d_wait("swap_mix", swap_mix, out["w_f2_out"][3]))

    mine = sums(names_f1, [pair_i, pair_o], scatter_wait("f1", scat_f1, out["w_mix_out"][3]))
    swap_f1, tok = swap_start("swap_f1", mine, mine[1])
    dcw_mine = lax.dynamic_slice(dcw_all, (0, 0, j_chip * (WB // NCHIP)), (NDEV, HALO, WB // NCHIP))
    small = {n: (wts[n], mom[n], var[n]) for n in list(VEC_ORDER) + list(PAIR_ORDER) + ["b_spatial", "w_spatial", "conv_w", "b_ada"]}
    small_out, loss = _small_adam(pack_all, late_all, dcw_mine, dada_all, small, tok)
    out.update(small_out)
    theirs = d2d_wait("swap_f1", swap_f1, out["b_ada"][3])
    for k, n in enumerate(names_f1):
        out[n] = tuple(r[None] for r in _adam_halves("adam_" + n, wts[n][0], mom[n][0], var[n][0], mine[k], theirs[k],
                                                     c_arr))

    res = [loss, grad_x]
    for k in range(4):
        res += [out[n][k] for n in WEIGHTS]
    return tuple(res)
```

```python
import jax
import jax.numpy as jnp
from jax import lax
from jax.experimental import pallas as pl
from jax.experimental.pallas import tpu as pltpu

D = 1024
DFF = 2816
WA = 512
WB = 512
NH = 8
HD = 64
CH = 128
CK = 31
HALO = 32
NMOD = 9
EPS = 1e-6
NCHIP = 4
NDEV = 8
FBLK = DFF // 2
ADA_SH = NMOD * D // NCHIP

LR, B1, B2, EPS_A, WD, STEP = 0.001, 0.9, 0.999, 1e-08, 0.01, 10

F32 = jnp.float32
BF16 = jnp.bfloat16
MESH = pl.DeviceIdType.MESH
ANY = pl.BlockSpec(memory_space=pl.ANY)
VMEM_FULL = pl.BlockSpec(memory_space=pltpu.VMEM)
VMEM_LIMIT = 56 * 1024 * 1024
WGRAD_VMEM_BUDGET = 52 * 1024 * 1024
STREAM_BUFFERS = 3
TM = 512
TM_FFN_BWD = 256

NT = (((1,), (1,)), ((), ()))
TN = (((0,), (0,)), ((), ()))


def _dot(a, b):
    return jnp.dot(a, b, preferred_element_type=F32)


def _dot_nt(a, b):
    return lax.dot_general(a, b, NT, preferred_element_type=F32)


def _dot_tn(a, b):
    return lax.dot_general(a, b, TN, preferred_element_type=F32)


def _cparams():
    return pltpu.CompilerParams(vmem_limit_bytes=VMEM_LIMIT)


def _chip_relations(x, y):
    return [(1 - x, y), (x, 1 - y), (1 - x, 1 - y)]


def _exchange(name, arrs, out_shapes, plan):
    n = len(arrs)
    n_out = len(out_shapes)

    def body(*refs):
        ins, outs = refs[:n], refs[n:n + n_out]
        send_sems, recv_sems, local_sems = refs[n + n_out:]
        x, y, c = lax.axis_index("x"), lax.axis_index("y"), lax.axis_index("c")
        local, sends = plan(x, y, c, ins, outs)
        locs = [pltpu.make_async_copy(s, d, local_sems.at[i]) for i, (s, d) in enumerate(local)]
        for loc in locs:
            loc.start()
        cps = [pltpu.make_async_remote_copy(src_ref=s, dst_ref=d, send_sem=send_sems.at[i], recv_sem=recv_sems.at[i],
                                            device_id=peer, device_id_type=MESH)
               for i, (s, d, peer, _) in enumerate(sends)]
        for cp in cps:
            cp.start()
        for i, (s, _, peer, landing) in enumerate(sends):
            pltpu.make_async_remote_copy(src_ref=s, dst_ref=landing, send_sem=send_sems.at[i], recv_sem=recv_sems.at[i],
                                         device_id=peer, device_id_type=MESH).wait_recv()
        for cp in cps:
            cp.wait_send()
        for loc in locs:
            loc.wait()

    return n, n_out, body


def _run_exchange(name, arrs, out_shapes, plan, n_local, n_send):
    n, n_out, body = _exchange(name, arrs, out_shapes, plan)
    return pl.pallas_call(
        body, name=name, out_shape=out_shapes,
        in_specs=[ANY] * n, out_specs=[ANY] * n_out,
        scratch_shapes=[pltpu.SemaphoreType.DMA((n_send,)), pltpu.SemaphoreType.DMA((n_send,)),
                        pltpu.SemaphoreType.DMA((max(n_local, 1),))],
    )(*arrs)


def _chip_allgather(name, arrs, behind=()):
    n = len(arrs)

    def plan(x, y, c, ins, outs):
        j_me = 2 * x + y
        local = [(ins[a], outs[a].at[j_me]) for a in range(n)]
        sends = []
        for a in range(n):
            for (px, py) in _chip_relations(x, y):
                sends.append((ins[a], outs[a].at[j_me], (px, py, c), outs[a].at[2 * px + py]))
        return local, sends

    shapes = [jax.ShapeDtypeStruct((NCHIP,) + a.shape, a.dtype) for a in arrs]
    return _run_exchange(name, list(arrs) + list(behind), shapes, plan, n, 3 * n)


HBM = pl.BlockSpec(memory_space=pltpu.HBM)
SEM = pl.BlockSpec(memory_space=pltpu.SEMAPHORE)
EFFECT = pltpu.SideEffectType.DATAFLOW_SIDE_EFFECTING


def _split_start_groups(name, groups, after):
    n_src = [len(g[0]) for g in groups]
    n_land = [len(g[1]) for g in groups]
    all_srcs = [pltpu.with_memory_space_constraint(a, pltpu.HBM) for g in groups for a in g[0]]
    all_lands = [pltpu.with_memory_space_constraint(a, pltpu.HBM) for g in groups for a in g[1]]
    ns, nl, ng = len(all_srcs), len(all_lands), len(groups)

    def body(*refs):
        src_refs, land_refs = refs[:ns], refs[ns:ns + nl]
        sem_refs = refs[ns + nl + 1:ns + nl + 1 + 2 * ng]
        token = refs[-1]
        x, y, c = lax.axis_index("x"), lax.axis_index("y"), lax.axis_index("c")
        at_src = at_land = 0
        for gi, (_, _, plan, _) in enumerate(groups):
            _, sends = plan(x, y, c, src_refs[at_src:at_src + n_src[gi]], land_refs[at_land:at_land + n_land[gi]])
            for i, (s, d, peer, _) in enumerate(sends):
                pltpu.make_async_remote_copy(src_ref=s, dst_ref=d, send_sem=sem_refs[2 * gi].at[i],
                                             recv_sem=sem_refs[2 * gi + 1].at[i], device_id=peer, device_id_type=MESH).start()
            at_src += n_src[gi]
            at_land += n_land[gi]
        token[...] = jnp.zeros_like(token)

    sems = [pltpu.SemaphoreType.DMA((g[3],)) for g in groups for _ in range(2)]
    res = pl.pallas_call(
        body, name=name,
        out_shape=(*sems, *[pltpu.HBM(a.shape, a.dtype) for a in all_lands], jax.ShapeDtypeStruct((8, 128), F32)),
        in_specs=[HBM] * (ns + nl) + [ANY],
        out_specs=(*([SEM] * (2 * ng)), *([HBM] * nl), pl.BlockSpec(memory_space=pltpu.VMEM)),
        input_output_aliases={ns + i: 2 * ng + i for i in range(nl)},
        compiler_params=pltpu.CompilerParams(has_side_effects=EFFECT),
    )(*all_srcs, *all_lands, after)
    out, at_src, at_land = [], 0, 2 * ng
    for gi in range(ng):
        out.append((res[2 * gi], res[2 * gi + 1], all_srcs[at_src:at_src + n_src[gi]],
                    list(res[at_land:at_land + n_land[gi]])))
        at_src += n_src[gi]
        at_land += n_land[gi]
    return out, res[-1]


def _split_start(name, srcs, lands, plan, n_send, after):
    (group,), token = _split_start_groups(name, [(srcs, lands, plan, n_send)], after)
    return (*group, token)


def _split_wait(name, srcs, lands, send_sems, recv_sems, plan, after):
    n, nl = len(srcs), len(lands)
    afters = list(after) if isinstance(after, (list, tuple)) else [after]

    def body(*refs):
        src, land = refs[:n], refs[n:n + nl]
        send_sems, recv_sems = refs[n + nl], refs[n + nl + 1]
        x, y, c = lax.axis_index("x"), lax.axis_index("y"), lax.axis_index("c")
        _, sends = plan(x, y, c, src, land)
        for i, (s, _, peer, landing) in enumerate(sends):
            cp = pltpu.make_async_remote_copy(src_ref=s, dst_ref=landing, send_sem=send_sems.at[i],
                                              recv_sem=recv_sems.at[i], device_id=peer, device_id_type=MESH)
            cp.wait_send()
            cp.wait_recv()

    thru = [pltpu.HBM(a.shape, a.dtype) for a in lands]
    res = pl.pallas_call(
        body, name=name, out_shape=tuple(thru),
        in_specs=[HBM] * (n + nl) + [SEM, SEM] + [ANY] * len(afters), out_specs=tuple([HBM] * nl),
        input_output_aliases={n + i: i for i in range(nl)},
        compiler_params=pltpu.CompilerParams(has_side_effects=EFFECT),
    )(*srcs, *lands, send_sems, recv_sems, *afters)
    return list(res)


def _split_forward_groups(name, groups, after):
    n_src = [len(g[0]) for g in groups]
    n_land = [len(g[1]) for g in groups]
    all_srcs = [a for g in groups for a in g[0]]
    all_lands = [a for g in groups for a in g[1]]
    ns, nl, ng = len(all_srcs), len(all_lands), len(groups)

    def body(*refs):
        src_refs, land_refs = refs[:ns], refs[ns:ns + nl]
        sems_a = refs[ns + nl:ns + nl + 2 * ng]
        sems_b = refs[ns + nl + 2 * ng + 1:ns + nl + 4 * ng + 1]
        token = refs[-1]
        x, y, c = lax.axis_index("x"), lax.axis_index("y"), lax.axis_index("c")
        at_src = at_land = 0
        for gi, g in enumerate(groups):
            src, land = src_refs[at_src:at_src + n_src[gi]], land_refs[at_land:at_land + n_land[gi]]
            plan_a, plan_b = g[4], g[5]
            _, first = plan_a(x, y, c, src, land)
            for i, (s, _, peer, landing) in enumerate(first):
                cp = pltpu.make_async_remote_copy(src_ref=s, dst_ref=landing, send_sem=sems_a[2 * gi].at[i],
                                                  recv_sem=sems_a[2 * gi + 1].at[i], device_id=peer, device_id_type=MESH)
                cp.wait_send()
                cp.wait_recv()
            _, second = plan_b(x, y, c, src, land)
            for i, (s, d, peer, _) in enumerate(second):
                pltpu.make_async_remote_copy(src_ref=s, dst_ref=d, send_sem=sems_b[2 * gi].at[i],
                                             recv_sem=sems_b[2 * gi + 1].at[i], device_id=peer, device_id_type=MESH).start()
            at_src += n_src[gi]
            at_land += n_land[gi]
        token[...] = jnp.zeros_like(token)

    sems = [pltpu.SemaphoreType.DMA((g[6],)) for g in groups for _ in range(2)]
    res = pl.pallas_call(
        body, name=name,
        out_shape=(*sems, *[pltpu.HBM(a.shape, a.dtype) for a in all_lands], jax.ShapeDtypeStruct((8, 128), F32)),
        in_specs=[HBM] * (ns + nl) + [SEM] * (2 * ng) + [ANY],
        out_specs=(*([SEM] * (2 * ng)), *([HBM] * nl), pl.BlockSpec(memory_space=pltpu.VMEM)),
        input_output_aliases={ns + i: 2 * ng + i for i in range(nl)},
        compiler_params=pltpu.CompilerParams(has_side_effects=EFFECT),
    )(*all_srcs, *all_lands, *[s for g in groups for s in (g[2], g[3])], after)
    out, at_land = [], 2 * ng
    for gi in range(ng):
        out.append((res[2 * gi], res[2 * gi + 1], list(res[at_land:at_land + n_land[gi]])))
        at_land += n_land[gi]
    return out, res[-1]


def _split_forward(name, srcs, lands, send_a, recv_a, plan_a, plan_b, n_b, after):
    (group,), token = _split_forward_groups(name, [(srcs, lands, send_a, recv_a, plan_a, plan_b, n_b)], after)
    return (*group, token)


def _gather_plans(shapes):
    n = len(shapes)

    def halves(a, c):
        rows = shapes[a][0] // 2
        return pl.ds(pl.multiple_of(c * rows, 16), rows), pl.ds(pl.multiple_of((1 - c) * rows, 16), rows)

    def split(a):
        return shapes[a][0] % 32 == 0

    def plan_a(x, y, c, src, land):
        j_me = 2 * x + y
        sends = []
        for a in range(n):
            for (px, py) in _chip_relations(x, y):
                if split(a):
                    mine, _ = halves(a, c)
                    sends.append((src[a].at[mine], land[a].at[j_me, mine], (px, py, c), land[a].at[2 * px + py, mine]))
                else:
                    sends.append((src[a], land[a].at[j_me], (px, py, c), land[a].at[2 * px + py]))
        return [], sends

    def plan_b(x, y, c, src, land):
        sends = []
        for a in range(n):
            if split(a):
                mine, other = halves(a, c)
                for (px, py) in _chip_relations(x, y):
                    j = 2 * px + py
                    sends.append((land[a].at[j, mine], land[a].at[j, mine], (x, y, 1 - c), land[a].at[j, other]))
        return [], sends

    n_b = 3 * sum(1 for a in range(n) if split(a))
    return plan_a, plan_b, n_b


def _allgather_plan(n):
    flips = [(dx, dy, dc) for dx in (0, 1) for dy in (0, 1) for dc in (0, 1) if dx or dy or dc]

    def plan(x, y, c, src, land):
        sends = []
        for a in range(n):
            for dx, dy, dc in flips:
                px, py, pc = x ^ dx, y ^ dy, c ^ dc
                sends.append((src[a], land[a].at[4 * x + 2 * y + c], (px, py, pc), land[a].at[4 * px + 2 * py + pc]))
        return [], sends

    return plan


def _scatter_plan(n):
    def plan(x, y, c, src, land):
        sends = []
        for a in range(n):
            for k, (px, py) in enumerate(_chip_relations(x, y)):
                sends.append((src[a].at[2 * px + py], land[a].at[k], (px, py, c), land[a].at[k]))
        return [], sends

    return plan


def _rms(x):
    r = lax.rsqrt(jnp.mean(x * x, axis=-1, keepdims=True) + EPS)
    return x * r, r


def _rms_bwd(dy, n, r, g):
    dg = jnp.sum(dy * n, axis=0, keepdims=True)
    dn = dy * g
    dx = r * (dn - n * jnp.mean(dn * n, axis=-1, keepdims=True))
    return dx, dg


def _ln(x):
    mu = jnp.mean(x, axis=-1, keepdims=True)
    xc = x - mu
    rstd = lax.rsqrt(jnp.mean(xc * xc, axis=-1, keepdims=True) + EPS)
    return xc * rstd, rstd


def _ln_bwd(dy, xhat, rstd, g):
    dg = jnp.sum(dy * xhat, axis=0, keepdims=True)
    db = jnp.sum(dy, axis=0, keepdims=True)
    dxh = dy * g
    dx = rstd * (dxh - jnp.mean(dxh, axis=-1, keepdims=True) - xhat * jnp.mean(dxh * xhat, axis=-1, keepdims=True))
    return dx, dg, db


def _sigmoid(x):
    return jax.nn.sigmoid(x)


def _dsilu(x, s):
    return s * (1.0 + x * (1.0 - s))


def _adam(w, g, m, v):
    m = B1 * m + (1.0 - B1) * g
    v = B2 * v + (1.0 - B2) * (g * g)
    m_hat = m / (1.0 - B1 ** STEP)
    v_hat = v / (1.0 - B2 ** STEP)
    delta = -LR * (m_hat / (jnp.sqrt(v_hat) + EPS_A) + WD * w)
    return delta, m, v


def _head_mask(shape):
    lane = lax.broadcasted_iota(jnp.int32, shape, len(shape) - 1)
    return [(lane >= h * HD) & (lane < (h + 1) * HD) for h in range(NH)]


def _first(b, i):
    return jnp.logical_and(b == 0, i == 0)


def _acc(ref, val, first):
    @pl.when(first)
    def _():
        ref[...] = val

    @pl.when(jnp.logical_not(first))
    def _():
        ref[...] += val


def _ada_fwd(c_all, w_sh, b_sh):
    nb = c_all.shape[0]
    tn = 768

    def body(c_ref, w_ref, b_ref, o_ref):
        cv = c_ref[...]
        cs = (cv * _sigmoid(cv)).astype(BF16)
        o_ref[...] = _dot(cs, w_ref[...].astype(BF16)) + b_ref[...]

    return pl.pallas_call(
        body, name="ada_fwd", grid=(ADA_SH // tn,),
        out_shape=jax.ShapeDtypeStruct((nb, ADA_SH), F32),
        in_specs=[pl.BlockSpec((nb, D), lambda j: (0, 0)), pl.BlockSpec((D, tn), lambda j: (0, j)),
                  pl.BlockSpec((1, tn), lambda j: (0, j))],
        out_specs=pl.BlockSpec((nb, tn), lambda j: (0, j)),
        compiler_params=_cparams(),
    )(c_all, w_sh, b_sh)


def _ada_bwd_adam(c_all, dada_sh, w, m, v):
    nb = c_all.shape[0]
    tn = 768

    def body(c_ref, d_ref, w_ref, m_ref, v_ref, g_out, d_out, m_out, v_out):
        cv = c_ref[...]
        cs = (cv * _sigmoid(cv)).astype(BF16)
        g = _dot_tn(cs, d_ref[...].astype(BF16))
        delta, m2, v2 = _adam(w_ref[...], g, m_ref[...], v_ref[...])
        g_out[...] = g
        d_out[...] = delta
        m_out[...] = m2
        v_out[...] = v2

    big = pl.BlockSpec((D, tn), lambda j: (0, j))
    shape = jax.ShapeDtypeStruct((D, ADA_SH), F32)
    return pl.pallas_call(
        body, name="ada_bwd_adam", grid=(ADA_SH // tn,),
        out_shape=[shape] * 4,
        in_specs=[pl.BlockSpec((nb, D), lambda j: (0, 0)), pl.BlockSpec((nb, tn), lambda j: (0, j)), big, big, big],
        out_specs=[big] * 4,
        compiler_params=_cparams(),
    )(c_all, dada_sh, w, m, v)


def _tok_specs(tm, width):
    return pl.BlockSpec((1, tm, width), lambda b, i: (b, i, 0))


def _mod_spec():
    return pl.BlockSpec((1, 1, D), lambda b, i: (b, 0, 0))


def _row_spec(width=D):
    return pl.BlockSpec((1, width), lambda b, i: (0, 0))


def _ffn_loss_fwd(x, sh, sc, gt, g_pre, g_post, w_in4, w_out, target):
    nb, s, _ = x.shape
    tm = min(TM, s)

    def body(x_ref, sh_ref, sc_ref, gt_ref, gpre_ref, gpost_ref, win_ref, wout_ref, tgt_ref,
             xo_ref, df_ref, p_ref, ls_ref, dgpost_ref, dgt_ref):
        xv = x_ref[0]
        n, _ = _rms(xv)
        h = (n * gpre_ref[...]) * (1.0 + sc_ref[0]) + sh_ref[0]
        hb = h.astype(BF16)
        acc = jnp.zeros((tm, D), F32)
        for j in range(2):
            gate = _dot(hb, win_ref[j])
            up = _dot(hb, win_ref[2 + j])
            p_ref[0, :, j * FBLK:(j + 1) * FBLK] = gate.astype(BF16)
            p_ref[0, :, DFF + j * FBLK:DFF + (j + 1) * FBLK] = up.astype(BF16)
            a = (gate * _sigmoid(gate)) * up
            acc = acc + _dot(a.astype(BF16), wout_ref[j * FBLK:(j + 1) * FBLK, :])
        nf, q = _rms(acc)
        gpost = gpost_ref[...]
        half_gate = 0.5 * gt_ref[0]
        out = xv + half_gate * (nf * gpost)
        first = _first(pl.program_id(0), pl.program_id(1))
        err = out - tgt_ref[0]
        dout = err * (1.0 / D)
        xo_ref[0] = dout
        row = jnp.sum(err * err, axis=0, keepdims=True)
        part = row[:, 0:128]
        for k in range(1, D // 128):
            part = part + row[:, k * 128:(k + 1) * 128]
        _acc(ls_ref, part, first)
        df, dgpost = _rms_bwd(dout * half_gate, nf, q, gpost)
        df_ref[0] = df.astype(BF16)
        _acc(dgpost_ref, dgpost, first)
        _acc(dgt_ref, jnp.sum(dout * (0.5 * (nf * gpost)), axis=0, keepdims=True)[None], pl.program_id(1) == 0)

    tok = _tok_specs(tm, D)
    return pl.pallas_call(
        body, name="ffn_loss_fwd", grid=(nb, s // tm),
        out_shape=[jax.ShapeDtypeStruct((nb, s, D), F32), jax.ShapeDtypeStruct((nb, s, D), BF16),
                   jax.ShapeDtypeStruct((nb, s, 2 * DFF), BF16), jax.ShapeDtypeStruct((1, 128), F32),
                   jax.ShapeDtypeStruct((1, D), F32), jax.ShapeDtypeStruct((nb, 1, D), F32)],
        in_specs=[tok, _mod_spec(), _mod_spec(), _mod_spec(), _row_spec(), _row_spec(), VMEM_FULL, VMEM_FULL, tok],
        out_specs=[tok, tok, _tok_specs(tm, 2 * DFF), pl.BlockSpec((1, 128), lambda b, i: (0, 0)), _row_spec(), _mod_spec()],
        compiler_params=_cparams(),
    )(x, sh, sc, gt, g_pre, g_post, w_in4, w_out, target)


def _ffn_up(x, sh, sc, g_pre, w_in4):
    nb, s, _ = x.shape
    tm = min(TM, s)

    def body(x_ref, sh_ref, sc_ref, gpre_ref, win_ref, p_ref, a_ref):
        n, _ = _rms(x_ref[0])
        hb = ((n * gpre_ref[...]) * (1.0 + sc_ref[0]) + sh_ref[0]).astype(BF16)
        for j in range(2):
            gate = _dot(hb, win_ref[j])
            up = _dot(hb, win_ref[2 + j])
            p_ref[0, :, j * FBLK:(j + 1) * FBLK] = gate.astype(BF16)
            p_ref[0, :, DFF + j * FBLK:DFF + (j + 1) * FBLK] = up.astype(BF16)
            a_ref[0, :, j * FBLK:(j + 1) * FBLK] = ((gate * _sigmoid(gate)) * up).astype(BF16)

    return pl.pallas_call(
        body, name="ffn_up", grid=(nb, s // tm),
        out_shape=[jax.ShapeDtypeStruct((nb, s, 2 * DFF), BF16), jax.ShapeDtypeStruct((nb, s, DFF), BF16)],
        in_specs=[_tok_specs(tm, D), _mod_spec(), _mod_spec(), _row_spec(), VMEM_FULL],
        out_specs=[_tok_specs(tm, 2 * DFF), _tok_specs(tm, DFF)],
        compiler_params=_cparams(),
    )(x, sh, sc, g_pre, w_in4)


def _ffn_down(x, a, gt, g_post, w_out):
    nb, s, _ = x.shape
    tm = min(TM, s)

    def body(x_ref, a_ref, gt_ref, gpost_ref, wout_ref, xo_ref, f_ref):
        acc = _dot(a_ref[0], wout_ref[...])
        f_ref[0] = acc
        nf, _ = _rms(acc)
        xo_ref[0] = x_ref[0] + (0.5 * gt_ref[0]) * (nf * gpost_ref[...])

    tok = _tok_specs(tm, D)
    shape = jax.ShapeDtypeStruct((nb, s, D), F32)
    return pl.pallas_call(
        body, name="ffn_down", grid=(nb, s // tm), out_shape=[shape, shape],
        in_specs=[tok, _tok_specs(tm, DFF), _mod_spec(), _row_spec(), VMEM_FULL],
        out_specs=[tok, tok],
        compiler_params=_cparams(),
    )(x, a, gt, g_post, w_out)


def _ffn_bwd(dxo, x, f, p, sh, sc, gt, g_pre, g_post, w_in4, w_out, df=None):
    nb, s, _ = x.shape
    tm = min(TM_FFN_BWD, s)
    given = df is not None

    def body(*refs):
        if given:
            (dxo_ref, x_ref, dfin_ref, p_ref, sh_ref, sc_ref, gpre_ref, win_ref, wout_ref,
             dx_ref, dp_ref, h_ref, a_ref, dgpre_ref, dsh_ref, dsc_ref) = refs
        else:
            (dxo_ref, x_ref, f_ref, p_ref, sh_ref, sc_ref, gt_ref, gpre_ref, gpost_ref, win_ref, wout_ref,
             dx_ref, dp_ref, h_ref, a_ref, df_ref, dgpre_ref, dgpost_ref, dsh_ref, dsc_ref, dgt_ref) = refs
        b, i = pl.program_id(0), pl.program_id(1)
        dxo_v = dxo_ref[0]
        if given:
            dfb = dfin_ref[0]
        else:
            nf, q = _rms(f_ref[0])
            gpost = gpost_ref[...]
            dgt = jnp.sum(dxo_v * (0.5 * (nf * gpost)), axis=0, keepdims=True)
            do = dxo_v * (0.5 * gt_ref[0])
            dfv, dgpost = _rms_bwd(do, nf, q, gpost)
            dfb = dfv.astype(BF16)
            df_ref[0] = dfb
        xv = x_ref[0]
        n, r = _rms(xv)
        gpre = gpre_ref[...]
        ng = n * gpre
        scale1 = 1.0 + sc_ref[0]
        h = ng * scale1 + sh_ref[0]
        h_ref[0] = h.astype(BF16)
        dh = jnp.zeros((tm, D), F32)
        for j in range(2):
            gate = p_ref[0, :, j * FBLK:(j + 1) * FBLK].astype(F32)
            up = p_ref[0, :, DFF + j * FBLK:DFF + (j + 1) * FBLK].astype(F32)
            sg = _sigmoid(gate)
            act = gate * sg
            a_ref[0, :, j * FBLK:(j + 1) * FBLK] = (act * up).astype(BF16)
            da = _dot_nt(dfb, wout_ref[j * FBLK:(j + 1) * FBLK, :])
            dgate = (da * up * _dsilu(gate, sg)).astype(BF16)
            dup = (da * act).astype(BF16)
            dp_ref[0, :, j * FBLK:(j + 1) * FBLK] = dgate
            dp_ref[0, :, DFF + j * FBLK:DFF + (j + 1) * FBLK] = dup
            dh = dh + _dot_nt(dgate, win_ref[j]) + _dot_nt(dup, win_ref[2 + j])
        dsh = jnp.sum(dh, axis=0, keepdims=True)
        dsc = jnp.sum(dh * ng, axis=0, keepdims=True)
        dxn, dgpre = _rms_bwd(dh * scale1, n, r, gpre)
        dx_ref[0] = dxo_v + dxn
        _acc(dgpre_ref, dgpre, _first(b, i))
        _acc(dsh_ref, dsh[None], i == 0)
        _acc(dsc_ref, dsc[None], i == 0)
        if not given:
            _acc(dgpost_ref, dgpost, _first(b, i))
            _acc(dgt_ref, dgt[None], i == 0)

    tok = _tok_specs(tm, D)
    mod_shape = jax.ShapeDtypeStruct((nb, 1, D), F32)
    row_shape = jax.ShapeDtypeStruct((1, D), F32)
    big = [jax.ShapeDtypeStruct((nb, s, D), F32), jax.ShapeDtypeStruct((nb, s, 2 * DFF), BF16),
           jax.ShapeDtypeStruct((nb, s, D), BF16), jax.ShapeDtypeStruct((nb, s, DFF), BF16)]
    big_specs = [tok, _tok_specs(tm, 2 * DFF), tok, _tok_specs(tm, DFF)]
    if given:
        return pl.pallas_call(
            body, name="ffn_bwd_after_loss", grid=(nb, s // tm),
            out_shape=big + [row_shape, mod_shape, mod_shape],
            in_specs=[tok, tok, tok, _tok_specs(tm, 2 * DFF), _mod_spec(), _mod_spec(), _row_spec(), VMEM_FULL, VMEM_FULL],
            out_specs=big_specs + [_row_spec(), _mod_spec(), _mod_spec()],
            compiler_params=_cparams(),
        )(dxo, x, df, p, sh, sc, g_pre, w_in4, w_out)
    return pl.pallas_call(
        body, name="ffn_bwd", grid=(nb, s // tm),
        out_shape=big + [jax.ShapeDtypeStruct((nb, s, D), BF16), row_shape, row_shape, mod_shape, mod_shape, mod_shape],
        in_specs=[tok, tok, tok, _tok_specs(tm, 2 * DFF), _mod_spec(), _mod_spec(), _mod_spec(), _row_spec(), _row_spec(),
                  VMEM_FULL, VMEM_FULL],
        out_specs=big_specs + [tok, _row_spec(), _row_spec(), _mod_spec(), _mod_spec(), _mod_spec()],
        compiler_params=_cparams(),
    )(dxo, x, f, p, sh, sc, gt, g_pre, g_post, w_in4, w_out)


def _wgrad(name, a, b, col_block, chip_major):
    t, ka = a.shape
    n = b.shape[1]
    def vmem_bytes(rows):
        return 2 * 2 * rows * (ka + col_block) + 4 * ka * col_block + 2 * (4 + 2) * ka * col_block

    tk = min(t, 512)
    while tk * 2 <= t and t % (tk * 2) == 0 and vmem_bytes(tk * 2) <= WGRAD_VMEM_BUDGET:
        tk *= 2
    nk = t // tk
    nblk = n // col_block

    def body(a_ref, b_ref, o_ref, obf_ref, acc_ref):
        k = pl.program_id(1)

        @pl.when(k == 0)
        def _():
            acc_ref[...] = jnp.zeros_like(acc_ref)

        acc_ref[...] += _dot_tn(a_ref[...], b_ref[...])

        @pl.when(k == nk - 1)
        def _():
            val = acc_ref[...]
            if chip_major:
                o_ref[0] = val
                obf_ref[0] = val.astype(BF16)
            else:
                o_ref[...] = val
                obf_ref[...] = val.astype(BF16)

    if chip_major:
        shape = (nblk, ka, col_block)
        ospec = pl.BlockSpec((1, ka, col_block), lambda j, k: (j, 0, 0))
    else:
        shape = (ka, n)
        ospec = pl.BlockSpec((ka, col_block), lambda j, k: (0, j))
    return pl.pallas_call(
        body, name=name, grid=(nblk, nk),
        out_shape=[jax.ShapeDtypeStruct(shape, F32), jax.ShapeDtypeStruct(shape, BF16)],
        in_specs=[pl.BlockSpec((tk, ka), lambda j, k: (k, 0)), pl.BlockSpec((tk, col_block), lambda j, k: (k, j))],
        out_specs=[ospec, ospec],
        scratch_shapes=[pltpu.VMEM((ka, col_block), F32)],
        compiler_params=_cparams(),
    )(a, b)


def _mix_in_fwd(x, sh, sc, g_pre, w_mi4):
    nb, s, _ = x.shape
    tm = min(TM, s)

    def body(x_ref, sh_ref, sc_ref, gpre_ref, w_ref, u_ref, v_ref, a_ref, g_ref):
        n, _ = _rms(x_ref[0])
        hb = ((n * gpre_ref[...]) * (1.0 + sc_ref[0]) + sh_ref[0]).astype(BF16)
        for k, o_ref in enumerate((u_ref, v_ref, a_ref, g_ref)):
            o_ref[0] = _dot(hb, w_ref[k])

    shape = jax.ShapeDtypeStruct((nb, s, WA), F32)
    return pl.pallas_call(
        body, name="mix_in_fwd", grid=(nb, s // tm),
        out_shape=[shape] * 4,
        in_specs=[_tok_specs(tm, D), _mod_spec(), _mod_spec(), _row_spec(), VMEM_FULL],
        out_specs=[_tok_specs(tm, WA)] * 4,
        compiler_params=_cparams(),
    )(x, sh, sc, g_pre, w_mi4)


def _spatial_weights(wcat_ref, transposed):
    w = wcat_ref[...]
    row = lax.broadcasted_iota(jnp.int32, w.shape, 0)
    col = lax.broadcasted_iota(jnp.int32, w.shape, 1)
    keep = ((row & (CH - 1)) <= col) if transposed else ((col & (CH - 1)) <= row)
    return jnp.where(keep, w, 0.0).astype(BF16)


def _expand_heads(vc, masks):
    return jnp.concatenate([jnp.where(mk, vc, jnp.zeros_like(vc)) for mk in masks], axis=0)


def _spatial_bias(bspt_ref):
    return bspt_ref[...]


SHIFTS = 8
TAP_ROWS = 64


def _ext_rows(tm):
    return tm + HALO + SHIFTS


def _make_shifts(ext_ref, sh_ref, tm):
    ext_ref[tm + HALO:tm + HALO + SHIFTS, :] = jnp.zeros((SHIFTS, WB), F32)
    for r in range(SHIFTS):
        sh_ref[r] = ext_ref[r:r + tm + HALO, :]


def _conv_taps(sh_ref, w_ref, tm, taps, emit):
    def block(i, carry):
        r0 = pl.multiple_of(i * TAP_ROWS, TAP_ROWS)
        acc = jnp.zeros((TAP_ROWS, WB), F32)
        for o, k in taps:
            acc = acc + w_ref[k:k + 1, :] * sh_ref[o % SHIFTS, pl.ds(r0 + SHIFTS * (o // SHIFTS), TAP_ROWS), :]
        emit(r0, acc)
        return carry

    lax.fori_loop(0, tm // TAP_ROWS, block, 0)


def _halo_prev_spec(tm):
    return pl.BlockSpec((1, HALO, WB), lambda b, i: (b, jnp.maximum(i * (tm // HALO) - 1, 0), 0))


def _halo_next_spec(tm, s):
    return pl.BlockSpec((1, HALO, WB), lambda b, i: (b, jnp.minimum((i + 1) * (tm // HALO), s // HALO - 1), 0))


def _mix_mid_fwd(x, u, v, a, g, gt, gn_g, gn_b, wcat, bspt, conv_w, conv_b, cn_g, cn_b, go_a, go_b, w_mo, g_post):
    nb, s, _ = x.shape
    tm = min(TM, s)

    def body(x_ref, u_ref, v_ref, a_ref, g_ref, ah_ref, gh_ref, gt_ref, gng_ref, gnb_ref, wcat_ref, bspt_ref,
             cw_ref, cb_ref, cng_ref, cnb_ref, goa_ref, gob_ref, wmo_ref, gpost_ref,
             xo_ref, conv_ref, y_ref, m_ref, ext_ref, sh_ref):
        i = pl.program_id(1)
        xhat, _ = _ln(v_ref[0])
        vb = (xhat * gng_ref[...] + gnb_ref[...]).astype(BF16)
        wsb = _spatial_weights(wcat_ref, False)
        bias = _spatial_bias(bspt_ref)
        masks = _head_mask((CH, WA))
        zs = []
        for cidx in range(tm // CH):
            vexp = _expand_heads(vb[cidx * CH:(cidx + 1) * CH, :], masks)
            zs.append(_dot(wsb, vexp) + bias)
        z = jnp.concatenate(zs, axis=0)
        na, _ = _rms(u_ref[0] * z)
        keep = jnp.where(i == 0, 0.0, 1.0).astype(F32)
        ext_ref[0:HALO, :] = (ah_ref[0] * _sigmoid(gh_ref[0])) * keep
        ext_ref[HALO:HALO + tm, :] = a_ref[0] * _sigmoid(g_ref[0])
        _make_shifts(ext_ref, sh_ref, tm)
        cb = cb_ref[...]

        def put_conv(r0, acc):
            conv_ref[0, pl.ds(r0, TAP_ROWS), :] = acc + cb

        _conv_taps(sh_ref, cw_ref, tm, [(k + HALO - (CK - 1), k) for k in range(CK)], put_conv)
        conv = conv_ref[0]
        chat, _ = _ln(conv)
        cln = chat * cng_ref[...] + cnb_ref[...]
        nbb, _ = _rms(cln * _sigmoid(cln))
        yb = jnp.concatenate([na * goa_ref[...], nbb * gob_ref[...]], axis=1).astype(BF16)
        y_ref[0] = yb
        m = _dot(yb, wmo_ref[...])
        m_ref[0] = m
        nm, _ = _rms(m)
        xo_ref[0] = x_ref[0] + gt_ref[0] * (nm * gpost_ref[...])

    t5 = _tok_specs(tm, WA)
    tok = _tok_specs(tm, D)
    r5 = _row_spec(WA)
    full = lambda shape: pl.BlockSpec(shape, lambda b, i: (0,) * len(shape))
    return pl.pallas_call(
        body, name="mix_mid_fwd", grid=(nb, s // tm),
        out_shape=[jax.ShapeDtypeStruct((nb, s, D), F32), jax.ShapeDtypeStruct((nb, s, WB), F32),
                   jax.ShapeDtypeStruct((nb, s, D), BF16), jax.ShapeDtypeStruct((nb, s, D), F32)],
        in_specs=[tok, t5, t5, t5, t5, _halo_prev_spec(tm), _halo_prev_spec(tm), _mod_spec(), r5, r5,
                  full((CH, NH * CH)), full((CH, WA)), full((HALO, WB)), r5, r5, r5, r5, r5, VMEM_FULL, _row_spec()],
        out_specs=[tok, t5, tok, tok],
        scratch_shapes=[pltpu.VMEM((_ext_rows(tm), WB), F32), pltpu.VMEM((SHIFTS, tm + HALO, WB), F32)],
        compiler_params=_cparams(),
    )(x, u, v, a, g, a, g, gt, gn_g, gn_b, wcat, bspt, conv_w, conv_b, cn_g, cn_b, go_a, go_b, w_mo, g_post)


def _mix_out_bwd(dxo, m, gt, g_post, w_mo):
    nb, s, _ = m.shape
    tm = min(TM, s)

    def body(dxo_ref, m_ref, gt_ref, gpost_ref, wmo_ref, dy_ref, dm_ref, dgpost_ref, dgt_ref):
        b, i = pl.program_id(0), pl.program_id(1)
        dxo_v = dxo_ref[0]
        nm, q = _rms(m_ref[0])
        gpost = gpost_ref[...]
        dgt = jnp.sum(dxo_v * (nm * gpost), axis=0, keepdims=True)
        dm, dgpost = _rms_bwd(dxo_v * gt_ref[0], nm, q, gpost)
        dmb = dm.astype(BF16)
        dm_ref[0] = dmb
        dy_ref[0] = _dot_nt(dmb, wmo_ref[...])
        _acc(dgpost_ref, dgpost, _first(b, i))
        _acc(dgt_ref, dgt[None], i == 0)

    tok = _tok_specs(tm, D)
    return pl.pallas_call(
        body, name="mix_out_bwd", grid=(nb, s // tm),
        out_shape=[jax.ShapeDtypeStruct((nb, s, D), F32), jax.ShapeDtypeStruct((nb, s, D), BF16),
                   jax.ShapeDtypeStruct((1, D), F32), jax.ShapeDtypeStruct((nb, 1, D), F32)],
        in_specs=[tok, tok, _mod_spec(), _row_spec(), VMEM_FULL],
        out_specs=[tok, tok, _row_spec(), _mod_spec()],
        compiler_params=_cparams(),
    )(dxo, m, gt, g_post, w_mo)


def _mix_mid_bwd(dy, u, v, conv, gn_g, gn_b, wcat, wcat_t, bspt, cn_g, cn_b, go_a, go_b):
    nb, s, _ = dy.shape
    tm = min(TM, s)
    nchunk = tm // CH

    def body(dy_ref, u_ref, v_ref, conv_ref, gng_ref, gnb_ref, wcat_ref, wcatt_ref, bspt_ref, cng_ref, cnb_ref,
             goa_ref, gob_ref,
             du_ref, dv_ref, dconv_ref, dwcat_ref, dbsp_ref, dgng_ref, dgnb_ref, dgoa_ref, dgob_ref,
             dcng_ref, dcnb_ref, dcb_ref):
        first = _first(pl.program_id(0), pl.program_id(1))
        dyv = dy_ref[0]
        xhat, rstd = _ln(v_ref[0])
        gng = gng_ref[...]
        vb = (xhat * gng + gnb_ref[...]).astype(BF16)
        wsb = _spatial_weights(wcat_ref, False)
        wsb_t = _spatial_weights(wcatt_ref, True)
        bias = _spatial_bias(bspt_ref)
        masks = _head_mask((CH, WA))
        vexps, zs = [], []
        for cidx in range(nchunk):
            vexp = _expand_heads(vb[cidx * CH:(cidx + 1) * CH, :], masks)
            vexps.append(vexp)
            zs.append(_dot(wsb, vexp) + bias)
        z = jnp.concatenate(zs, axis=0)
        uv = u_ref[0]
        na, ra = _rms(uv * z)
        dya, dgoa = _rms_bwd(dyv[:, 0:WA], na, ra, goa_ref[...])
        du_ref[0] = dya * z
        dz = dya * uv
        dwcat = jnp.zeros((CH, NH * CH), F32)
        dzsum = jnp.zeros((CH, WA), F32)
        dvlns = []
        for cidx in range(nchunk):
            dzc = dz[cidx * CH:(cidx + 1) * CH, :]
            dzsum = dzsum + dzc
            dzb = dzc.astype(BF16)
            dwcat = dwcat + _dot_nt(dzb, vexps[cidx])
            dvexp = _dot(wsb_t, dzb)
            dvl = jnp.zeros((CH, WA), F32)
            for h in range(NH):
                dvl = dvl + jnp.where(masks[h], dvexp[h * CH:(h + 1) * CH, :], 0.0)
            dvlns.append(dvl)
        dvln = jnp.concatenate(dvlns, axis=0)
        dv, dgng, dgnb = _ln_bwd(dvln, xhat, rstd, gng)
        dv_ref[0] = dv
        lane = lax.broadcasted_iota(jnp.int32, (NH, WA), 1)
        head = lax.broadcasted_iota(jnp.int32, (NH, WA), 0)
        sel = jnp.where((lane >= head * HD) & (lane < (head + 1) * HD), 1.0, 0.0).astype(F32)
        dbsp = lax.dot_general(sel, dzsum, NT, preferred_element_type=F32, precision=lax.Precision.HIGHEST)
        chat, crstd = _ln(conv_ref[0])
        cng = cng_ref[...]
        cln = chat * cng + cnb_ref[...]
        sg = _sigmoid(cln)
        nbb, rb = _rms(cln * sg)
        dyb, dgob = _rms_bwd(dyv[:, WA:D], nbb, rb, gob_ref[...])
        dconv, dcng, dcnb = _ln_bwd(dyb * _dsilu(cln, sg), chat, crstd, cng)
        dconv_ref[0] = dconv
        dcb = jnp.sum(dconv, axis=0, keepdims=True)
        for ref, val in ((dwcat_ref, dwcat), (dbsp_ref, dbsp), (dgng_ref, dgng), (dgnb_ref, dgnb), (dgoa_ref, dgoa),
                         (dgob_ref, dgob), (dcng_ref, dcng), (dcnb_ref, dcnb), (dcb_ref, dcb)):
            _acc(ref, val, first)

    t5 = _tok_specs(tm, WA)
    r5 = _row_spec(WA)
    full = lambda shape: pl.BlockSpec(shape, lambda b, i: (0,) * len(shape))
    big = jax.ShapeDtypeStruct((nb, s, WA), F32)
    row = jax.ShapeDtypeStruct((1, WA), F32)
    return pl.pallas_call(
        body, name="mix_mid_bwd", grid=(nb, s // tm),
        out_shape=[big, big, big, jax.ShapeDtypeStruct((CH, NH * CH), F32), jax.ShapeDtypeStruct((NH, CH), F32),
                   row, row, row, row, row, row, row],
        in_specs=[_tok_specs(tm, D), t5, t5, t5, r5, r5, full((CH, NH * CH)), full((NH * CH, CH)), full((CH, WA)),
                  r5, r5, r5, r5],
        out_specs=[t5, t5, t5, full((CH, NH * CH)), full((NH, CH)), r5, r5, r5, r5, r5, r5, r5],
        compiler_params=_cparams(),
    )(dy, u, v, conv, gn_g, gn_b, wcat, wcat_t, bspt, cn_g, cn_b, go_a, go_b)


def _mix_in_bwd(dxo, x, du, dv, dconv, a, g, sh, sc, g_pre, w_mi4, conv_w):
    nb, s, _ = x.shape
    tm = min(TM, s)
    n_i = s // tm

    def body(dxo_ref, x_ref, du_ref, dv_ref, dc_ref, dch_ref, a_ref, g_ref, ah_ref, gh_ref, sh_ref, sc_ref,
             gpre_ref, w_ref, cw_ref,
             dx_ref, dproj_ref, h_ref, dgpre_ref, dsh_ref, dsc_ref, dcw_ref, ext_ref, shf_ref, dglu_ref):
        b, i = pl.program_id(0), pl.program_id(1)
        first = _first(b, i)
        av, gv = a_ref[0], g_ref[0]
        sg = _sigmoid(gv)
        dconv = dc_ref[0]
        ext_ref[0:tm, :] = dconv
        ext_ref[tm:tm + HALO, :] = dch_ref[0] * jnp.where(i == n_i - 1, 0.0, 1.0).astype(F32)
        _make_shifts(ext_ref, shf_ref, tm)

        def put_dglu(r0, acc):
            dglu_ref[pl.ds(r0, TAP_ROWS), :] = acc

        _conv_taps(shf_ref, cw_ref, tm, [(CK - 1 - k, k) for k in range(CK)], put_dglu)
        dglu = dglu_ref[...]
        ext_ref[0:HALO, :] = (ah_ref[0] * _sigmoid(gh_ref[0])) * jnp.where(i == 0, 0.0, 1.0).astype(F32)
        ext_ref[HALO:HALO + tm, :] = av * sg
        _make_shifts(ext_ref, shf_ref, tm)

        @pl.when(first)
        def _():
            dcw_ref[...] = jnp.zeros((HALO, WB), F32)

        for k in range(CK):
            o = k + HALO - (CK - 1)
            lo = SHIFTS * (o // SHIFTS)
            dcw_ref[k:k + 1, :] += jnp.sum(dconv * shf_ref[o % SHIFTS, lo:lo + tm, :], axis=0, keepdims=True)
        da = dglu * sg
        dg = dglu * av * (sg * (1.0 - sg))
        parts = [du_ref[0].astype(BF16), dv_ref[0].astype(BF16), da.astype(BF16), dg.astype(BF16)]
        dh = jnp.zeros((tm, D), F32)
        for k in range(4):
            dproj_ref[0, :, k * WA:(k + 1) * WA] = parts[k]
            dh = dh + _dot_nt(parts[k], w_ref[k])
        n, r = _rms(x_ref[0])
        gpre = gpre_ref[...]
        ng = n * gpre
        scale1 = 1.0 + sc_ref[0]
        h_ref[0] = (ng * scale1 + sh_ref[0]).astype(BF16)
        dsh = jnp.sum(dh, axis=0, keepdims=True)
        dsc = jnp.sum(dh * ng, axis=0, keepdims=True)
        dxn, dgpre = _rms_bwd(dh * scale1, n, r, gpre)
        dx_ref[0] = dxo_ref[0] + dxn
        _acc(dgpre_ref, dgpre, first)
        _acc(dsh_ref, dsh[None], i == 0)
        _acc(dsc_ref, dsc[None], i == 0)

    tok = _tok_specs(tm, D)
    t5 = _tok_specs(tm, WA)
    full = lambda shape: pl.BlockSpec(shape, lambda b, i: (0,) * len(shape))
    mod_shape = jax.ShapeDtypeStruct((nb, 1, D), F32)
    return pl.pallas_call(
        body, name="mix_in_bwd", grid=(nb, n_i),
        out_shape=[jax.ShapeDtypeStruct((nb, s, D), F32), jax.ShapeDtypeStruct((nb, s, 4 * WA), BF16),
                   jax.ShapeDtypeStruct((nb, s, D), BF16), jax.ShapeDtypeStruct((1, D), F32), mod_shape, mod_shape,
                   jax.ShapeDtypeStruct((HALO, WB), F32)],
        in_specs=[tok, tok, t5, t5, t5, _halo_next_spec(tm, s), t5, t5, _halo_prev_spec(tm), _halo_prev_spec(tm),
                  _mod_spec(), _mod_spec(), _row_spec(), VMEM_FULL, full((HALO, WB))],
        out_specs=[tok, _tok_specs(tm, 4 * WA), tok, _row_spec(), _mod_spec(), _mod_spec(), full((HALO, WB))],
        scratch_shapes=[pltpu.VMEM((_ext_rows(tm), WB), F32), pltpu.VMEM((SHIFTS, tm + HALO, WB), F32),
                        pltpu.VMEM((tm, WB), F32)],
        compiler_params=_cparams(),
    )(dxo, x, du, dv, dconv, dconv, a, g, a, g, sh, sc, g_pre, w_mi4, conv_w)


def _row_tile(rows, cols):
    best = 16
    for t in range(16, rows + 1, 16):
        if rows % t == 0 and t * cols * 4 <= 1536 * 1024:
            best = t
    return best


def _walk(steps):
    offs = [sum(steps[:k]) for k in range(len(steps))]

    def tile(k):
        return lambda i: jnp.clip(i - offs[k], 0, steps[k] - 1)

    def mine(k, i):
        return jnp.logical_and(i >= offs[k], i < offs[k] + steps[k])

    return sum(steps), tile, mine


def _sum4(name, own4s, recvs, j_arr):
    n = len(own4s)
    shapes = [o.shape[1:] for o in own4s]
    trs = [_row_tile(r, c) for r, c in shapes]
    total, tile, mine = _walk([r // tr for (r, _), tr in zip(shapes, trs)])

    def body(j_ref, *refs):
        del j_ref
        i = pl.program_id(0)
        for k in range(n):
            own_ref, recv_ref, o_ref = refs[2 * k], refs[2 * k + 1], refs[2 * n + k]

            def add(own_ref=own_ref, recv_ref=recv_ref, o_ref=o_ref):
                acc = own_ref[0]
                for q in range(3):
                    acc = acc + recv_ref[q].astype(F32)
                o_ref[...] = acc

            pl.when(mine(k, i))(add)

    in_specs, out_specs = [], []
    for k, ((_, cols), tr) in enumerate(zip(shapes, trs)):
        in_specs += [pl.BlockSpec((1, tr, cols), lambda i, j, t=tile(k): (j[0], t(i), 0)),
                     pl.BlockSpec((3, tr, cols), lambda i, j, t=tile(k): (0, t(i), 0))]
        out_specs.append(pl.BlockSpec((tr, cols), lambda i, j, t=tile(k): (t(i), 0)))
    return pl.pallas_call(
        body, name=name,
        grid_spec=pltpu.PrefetchScalarGridSpec(num_scalar_prefetch=1, grid=(total,), in_specs=in_specs, out_specs=out_specs),
        out_shape=[jax.ShapeDtypeStruct(sh, F32) for sh in shapes],
        compiler_params=_cparams(),
    )(j_arr, *[a for pair in zip(own4s, recvs) for a in pair])


def _pair_plan(shapes):
    def plan(x, y, c, src, land):
        sends = []
        for a, shape in enumerate(shapes):
            rows = shape[1] // 2
            theirs = pl.ds(pl.multiple_of((1 - c) * rows, 16), rows)
            sends.append((src[a].at[:, theirs], land[a], (x, y, 1 - c), land[a]))
        return [], sends

    return plan


def _swap_plan(n):
    def plan(x, y, c, src, land):
        return [], [(src[a], land[a], (x, y, 1 - c), land[a]) for a in range(n)]

    return plan


def _pair_sum(name, g32s, recvs, c_arr):
    n = len(g32s)
    shapes = [r.shape for r in recvs]
    trs = [_row_tile(rows, cols) for _, rows, cols in shapes]
    nhs = [rows // tr for (_, rows, _), tr in zip(shapes, trs)]
    total, tile, mine = _walk([nblk * nh for (nblk, _, _), nh in zip(shapes, nhs)])

    def body(c_ref, *refs):
        del c_ref
        i = pl.program_id(0)
        for k in range(n):
            g_ref, r_ref, o32_ref, obf_ref = refs[2 * k], refs[2 * k + 1], refs[2 * n + 2 * k], refs[2 * n + 2 * k + 1]

            def add(g_ref=g_ref, r_ref=r_ref, o32_ref=o32_ref, obf_ref=obf_ref):
                val = g_ref[0] + r_ref[0].astype(F32)
                o32_ref[0] = val
                obf_ref[0] = val.astype(BF16)

            pl.when(mine(k, i))(add)

    in_specs, out_specs, out_shape = [], [], []
    for k, ((_, _, cols), tr, nh) in enumerate(zip(shapes, trs, nhs)):
        def half(tr=tr, cols=cols, t=tile(k), nh=nh):
            return pl.BlockSpec((1, tr, cols), lambda i, c: (t(i) // nh, t(i) % nh, 0))

        in_specs += [pl.BlockSpec((1, tr, cols), lambda i, c, t=tile(k), nh=nh: (t(i) // nh, c[0] * nh + t(i) % nh, 0)), half()]
        out_specs += [half(), half()]
        out_shape += [jax.ShapeDtypeStruct(shapes[k], F32), jax.ShapeDtypeStruct(shapes[k], BF16)]
    res = pl.pallas_call(
        body, name=name,
        grid_spec=pltpu.PrefetchScalarGridSpec(num_scalar_prefetch=1, grid=(total,), in_specs=in_specs, out_specs=out_specs),
        out_shape=out_shape,
        compiler_params=_cparams(),
    )(c_arr, *[a for pair in zip(g32s, recvs) for a in pair])
    return [(res[2 * k], res[2 * k + 1]) for k in range(n)]


def _adam_halves(name, w, m, v, mine, theirs, c_arr):
    rows, cols = w.shape
    tr = _row_tile(rows // 2, cols)
    nh = (rows // 2) // tr

    def body(c_ref, w_ref, m_ref, v_ref, mine_ref, theirs_ref, g_out, d_out, m_out, v_out):
        here = (pl.program_id(0) // nh) == c_ref[0]
        g = jnp.where(here, mine_ref[...], theirs_ref[...])
        delta, m2, v2 = _adam(w_ref[...], g, m_ref[...], v_ref[...])
        g_out[...] = g
        d_out[...] = delta
        m_out[...] = m2
        v_out[...] = v2

    spec = pl.BlockSpec((tr, cols), lambda i, c: (i, 0))
    shape = jax.ShapeDtypeStruct((rows, cols), F32)
    return pl.pallas_call(
        body, name=name,
        grid_spec=pltpu.PrefetchScalarGridSpec(
            num_scalar_prefetch=1, grid=(2 * nh,),
            in_specs=[spec, spec, spec,
                      pl.BlockSpec((tr, cols), lambda i, c: (jnp.clip(i - c[0] * nh, 0, nh - 1), 0)),
                      pl.BlockSpec((tr, cols), lambda i, c: (jnp.clip(i - (1 - c[0]) * nh, 0, nh - 1), 0))],
            out_specs=[spec] * 4),
        out_shape=[shape] * 4,
        compiler_params=_cparams(),
    )(c_arr, w, m, v, mine, theirs)


def _adam_big(name, w, m, v, ga, gb):
    rows, cols = w.shape
    tr = _row_tile(rows, cols)

    n = rows // tr
    nbuf = min(STREAM_BUFFERS, n)

    def body(w_ref, m_ref, v_ref, ga_ref, gb_ref, g_out, d_out, m_out, v_out, in_buf, out_buf, in_sems, out_sems):
        ins = (w_ref, m_ref, v_ref, ga_ref, gb_ref)
        outs = (g_out, d_out, m_out, v_out)

        def fetch(i):
            return [pltpu.make_async_copy(r.at[i * tr:(i + 1) * tr], in_buf.at[i % nbuf, k], in_sems.at[i % nbuf, k])
                    for k, r in enumerate(ins)]

        def flush(i):
            return [pltpu.make_async_copy(out_buf.at[i % 2, k], r.at[i * tr:(i + 1) * tr], out_sems.at[i % 2, k])
                    for k, r in enumerate(outs)]

        for i in range(nbuf):
            for cp in fetch(i):
                cp.start()
        for i in range(n):
            for cp in fetch(i):
                cp.wait()
            slot = i % nbuf
            gsum = in_buf[slot, 3] + in_buf[slot, 4]
            delta, m2, v2 = _adam(in_buf[slot, 0], gsum, in_buf[slot, 1], in_buf[slot, 2])
            if i >= 2:
                for cp in flush(i - 2):
                    cp.wait()
            for k, val in enumerate((gsum, delta, m2, v2)):
                out_buf[i % 2, k] = val
            for cp in flush(i):
                cp.start()
            if i + nbuf < n:
                for cp in fetch(i + nbuf):
                    cp.start()
        for i in range(max(n - 2, 0), n):
            for cp in flush(i):
                cp.wait()

    shape = jax.ShapeDtypeStruct((rows, cols), F32)
    return pl.pallas_call(
        body, name=name, out_shape=[shape] * 4,
        in_specs=[ANY] * 5, out_specs=[ANY] * 4,
        scratch_shapes=[pltpu.VMEM((nbuf, 5, tr, cols), F32), pltpu.VMEM((2, 4, tr, cols), F32),
                        pltpu.SemaphoreType.DMA((nbuf, 5)), pltpu.SemaphoreType.DMA((2, 4))],
        compiler_params=_cparams(),
    )(w, m, v, ga, gb)


PK_VEC = 0
PK_LOSS = 6
PK_PAIR = 8
PK_BSP = 16
PK_WCAT = 24
PK_ROWS = PK_WCAT + CH
PAIR_ORDER = ("gmlp_norm_g", "gmlp_norm_b", "conv_b", "conv_norm_g", "conv_norm_b", "g_out_a", "g_out_b")
VEC_ORDER = ("g_pre_f1", "g_post_f1", "g_pre_m", "g_post_m", "g_pre_f2", "g_post_f2")


def _pack_late(rows):
    counts = [r.shape[0] for r in rows]
    assert sum(counts) == 8

    def body(*refs):
        o_ref = refs[-1]
        at = 0
        for r, cnt in zip(refs[:-1], counts):
            o_ref[at:at + cnt, :] = r[...]
            at += cnt

    return pl.pallas_call(
        body, name="pack_late", out_shape=jax.ShapeDtypeStruct((8, D), F32),
        in_specs=[VMEM_FULL] * len(rows), out_specs=VMEM_FULL, compiler_params=_cparams(),
    )(*rows)


def _pack_small(vecs, pairs, dbsp, dwcat, lsum):
    def body(*refs):
        vec_refs = refs[:4]
        pair_refs = refs[4:11]
        dbsp_ref, dwcat_ref, lsum_ref, o_ref = refs[11:]
        o_ref[0:PK_WCAT, :] = jnp.zeros((PK_WCAT, D), F32)
        o_ref[PK_LOSS:PK_LOSS + 1, 0:128] = lsum_ref[...]
        for k, r in enumerate(vec_refs):
            o_ref[PK_VEC + 2 + k:PK_VEC + 3 + k, :] = r[...]
        for k, r in enumerate(pair_refs):
            row, half = PK_PAIR + k // 2, k % 2
            o_ref[row:row + 1, half * WA:(half + 1) * WA] = r[...]
        o_ref[PK_BSP:PK_BSP + NH, 0:CH] = dbsp_ref[...]
        o_ref[PK_WCAT:PK_ROWS, :] = dwcat_ref[...]

    args = list(vecs) + list(pairs) + [dbsp, dwcat, lsum]
    return pl.pallas_call(
        body, name="pack_small", out_shape=jax.ShapeDtypeStruct((PK_ROWS, D), F32),
        in_specs=[VMEM_FULL] * len(args), out_specs=VMEM_FULL, compiler_params=_cparams(),
    )(*args)


def _small_adam(pack_all, late_all, dcw_all, dada_all, params, behind):
    names = list(VEC_ORDER) + list(PAIR_ORDER) + ["b_spatial", "w_spatial", "conv_w", "b_ada"]
    flat = []
    for nm in names:
        flat += list(params[nm])
    n_in = 4 + len(flat)

    def body(*refs):
        pack_ref, late_ref, dcw_ref, dada_ref = refs[:4]
        prm = refs[4:n_in]
        outs = refs[n_in + 1:]

        def total(r0, nr, c0, nc):
            acc = pack_ref[0, r0:r0 + nr, c0:c0 + nc]
            for d in range(1, NDEV):
                acc = acc + pack_ref[d, r0:r0 + nr, c0:c0 + nc]
            return acc

        def emit(idx, g, getw, put):
            w_ref, m_ref, v_ref = prm[3 * idx:3 * idx + 3]
            delta, m2, v2 = _adam(getw(w_ref), g, getw(m_ref), getw(v_ref))
            for o_ref, val in zip(outs[4 * idx:4 * idx + 4], (g, delta, m2, v2)):
                put(o_ref, val)

        def whole(ref):
            return ref[...]

        def put_whole(ref, val):
            ref[...] = val

        idx = 0
        for k in range(6):
            if k < 2:
                g = late_ref[0, k:k + 1, :]
                for d in range(1, NDEV):
                    g = g + late_ref[d, k:k + 1, :]
            else:
                g = total(PK_VEC + k, 1, 0, D)
            emit(idx, g, whole, put_whole)
            idx += 1
        for k in range(7):
            emit(idx, total(PK_PAIR + k // 2, 1, (k % 2) * WA, WA), whole, put_whole)
            idx += 1
        emit(idx, total(PK_BSP, NH, 0, CH), lambda r: r[0], lambda r, val: r.__setitem__(0, val))
        idx += 1
        row = lax.broadcasted_iota(jnp.int32, (CH, CH), 0)
        col = lax.broadcasted_iota(jnp.int32, (CH, CH), 1)
        for h in range(NH):
            gh = jnp.where(col <= row, total(PK_WCAT, CH, h * CH, CH), 0.0)
            w_ref, m_ref, v_ref = prm[3 * idx:3 * idx + 3]
            delta, m2, v2 = _adam(w_ref[0, h], gh, m_ref[0, h], v_ref[0, h])
            for o_ref, val in zip(outs[4 * idx:4 * idx + 4], (gh, delta, m2, v2)):
                o_ref[0, h] = val
        idx += 1
        gcw = dcw_ref[0, 0:CK, :]
        for d in range(1, NDEV):
            gcw = gcw + dcw_ref[d, 0:CK, :]
        emit(idx, gcw, lambda r: r[0], lambda r, val: r.__setitem__(0, val))
        idx += 1
        emit(idx, jnp.sum(dada_ref[...], axis=0, keepdims=True), whole, put_whole)
        outs[-1][...] = jnp.sum(total(PK_LOSS, 1, 0, 128), axis=1, keepdims=True) * (0.5 / D)

    out_shape = []
    for nm in names:
        w = params[nm][0]
        out_shape += [jax.ShapeDtypeStruct(w.shape, F32)] * 4
    out_shape.append(jax.ShapeDtypeStruct((1, 1), F32))
    res = pl.pallas_call(
        body, name="small_adam", out_shape=out_shape,
        in_specs=[VMEM_FULL] * n_in + [ANY], out_specs=[VMEM_FULL] * len(out_shape), compiler_params=_cparams(),
    )(pack_all, late_all, dcw_all, dada_all, *flat, behind)
    return {nm: tuple(res[4 * k:4 * k + 4]) for k, nm in enumerate(names)}, res[-1].reshape(())


WEIGHTS = ['w_ada', 'b_ada', 'g_pre_f1', 'g_post_f1', 'w_f1_in', 'w_f1_out', 'g_pre_m', 'g_post_m', 'w_mix_in',
           'gmlp_norm_g', 'gmlp_norm_b', 'w_spatial', 'b_spatial', 'conv_w', 'conv_b', 'conv_norm_g', 'conv_norm_b',
           'g_out_a', 'g_out_b', 'w_mix_out', 'g_pre_f2', 'g_post_f2', 'w_f2_in', 'w_f2_out']


def kernel(x, c, w_ada, b_ada, g_pre_f1, g_post_f1, w_f1_in, w_f1_out, g_pre_m, g_post_m, w_mix_in, gmlp_norm_g, gmlp_norm_b, w_spatial, b_spatial, conv_w, conv_b, conv_norm_g, conv_norm_b, g_out_a, g_out_b, w_mix_out, g_pre_f2, g_post_f2, w_f2_in, w_f2_out, loss_target, m_w_ada, m_b_ada, m_g_pre_f1, m_g_post_f1, m_w_f1_in, m_w_f1_out, m_g_pre_m, m_g_post_m, m_w_mix_in, m_gmlp_norm_g, m_gmlp_norm_b, m_w_spatial, m_b_spatial, m_conv_w, m_conv_b, m_conv_norm_g, m_conv_norm_b, m_g_out_a, m_g_out_b, m_w_mix_out, m_g_pre_f2, m_g_post_f2, m_w_f2_in, m_w_f2_out, v_w_ada, v_b_ada, v_g_pre_f1, v_g_post_f1, v_w_f1_in, v_w_f1_out, v_g_pre_m, v_g_post_m, v_w_mix_in, v_gmlp_norm_g, v_gmlp_norm_b, v_w_spatial, v_b_spatial, v_conv_w, v_conv_b, v_conv_norm_g, v_conv_norm_b, v_g_out_a, v_g_out_b, v_w_mix_out, v_g_pre_f2, v_g_post_f2, v_w_f2_in, v_w_f2_out):
    env = dict(locals())
    wts = {n: env[n] for n in WEIGHTS}
    mom = {n: env["m_" + n] for n in WEIGHTS}
    var = {n: env["v_" + n] for n in WEIGHTS}
    nb, s, _ = x.shape
    t = nb * s
    ax, ay, ac = lax.axis_index("x"), lax.axis_index("y"), lax.axis_index("c")
    j_chip = 2 * ax + ay
    dev = 4 * ax + 2 * ay + ac
    j_arr = j_chip.reshape(1).astype(jnp.int32)

    groups = (("w_f1_in",), ("w_mix_in", "w_mix_out"), ("w_f2_in", "w_f2_out"), ("w_f1_out",))
    def gather_operands(gi):
        srcs = [wts[n][0].astype(BF16) for n in groups[gi]] + ([conv_w[0]] if gi == 1 else [])
        lands = [lax.dynamic_update_index_in_dim(lax.empty((NCHIP,) + a.shape, a.dtype), a, j_chip, 0) for a in srcs]
        return srcs, lands

    def gather_start(gi, behind, operands=None):
        srcs, lands = operands or gather_operands(gi)
        plan_a, plan_b, n_b = _gather_plans([a.shape for a in srcs])
        ssem, rsem, srcs, lands, token = _split_start("gw_start%d" % gi, srcs, lands, plan_a, 3 * len(srcs), behind)
        gather[gi] = (srcs, lands, ssem, rsem, plan_a, plan_b, n_b)
        return token

    def gather_forward(gi, behind):
        srcs, lands, ssem, rsem, plan_a, plan_b, n_b = gather[gi]
        ssem, rsem, lands, token = _split_forward("gw_fwd%d" % gi, srcs, lands, ssem, rsem, plan_a, plan_b, n_b, behind)
        gather[gi] = (lands, ssem, rsem, plan_b)
        return token

    def gathered(gi, behind):
        lands, ssem, rsem, plan_b = gather[gi]
        return _split_wait("gw_wait%d" % gi, [], lands, ssem, rsem, plan_b, behind)

    gather = {}
    def allgather_start(tag, arrs, behind):
        lands = [lax.dynamic_update_index_in_dim(lax.empty((NDEV,) + a.shape, a.dtype), a, dev, 0) for a in arrs]
        ssem, rsem, srcs, lands, token = _split_start("small_start_" + tag, arrs, lands, _allgather_plan(len(arrs)),
                                                      7 * len(arrs), behind)
        return (srcs, lands, ssem, rsem), token

    def allgather_wait(tag, state, behind):
        srcs, lands, ssem, rsem = state
        return _split_wait("small_wait_" + tag, srcs, lands, ssem, rsem, _allgather_plan(len(srcs)), behind)

    c_state, token = allgather_start("c", [c.reshape(8, (nb * D) // 8)], c)
    token = gather_start(0, token)
    (c_all8,) = allgather_wait("c", c_state, token)
    c_all = c_all8.reshape(NDEV * nb, D)
    b_sh = lax.dynamic_slice(b_ada, (0, j_chip * ADA_SH), (1, ADA_SH))
    ada_sh = _ada_fwd(c_all, w_ada[0], b_sh)
    later = [gather_operands(3), gather_operands(1), gather_operands(2)]
    (ada4,) = _chip_allgather("gather_ada", [ada_sh], behind=[a for pair in later for arrs in pair for a in arrs])
    token = gather_forward(0, ada4)
    plans = [_gather_plans([a.shape for a in srcs]) for srcs, _ in later]
    started, token = _split_start_groups(
        "gw_start_later", [(srcs, lands, pa, 3 * len(srcs)) for (srcs, lands), (pa, _, _) in zip(later, plans)], token)
    for gi, (ssem, rsem, srcs, lands), (pa, pb, n_b) in zip((3, 1, 2), started, plans):
        gather[gi] = (srcs, lands, ssem, rsem, pa, pb, n_b)
    ada_me = lax.dynamic_slice(ada4, (0, dev * nb, 0), (NCHIP, nb, ADA_SH))
    ada_me = jnp.transpose(ada_me, (1, 0, 2)).reshape(nb, NMOD * D)
    sh1, sc1, gt1, sh2, sc2, gt2, sh3, sc3, gt3 = [ada_me[:, k * D:(k + 1) * D].reshape(nb, 1, D) for k in range(NMOD)]

    wcat = jnp.transpose(w_spatial[0], (1, 0, 2)).reshape(CH, NH * CH)
    wcat_t = jnp.transpose(w_spatial[0], (0, 2, 1)).reshape(NH * CH, CH)
    bspt = jnp.repeat(b_spatial[0].T, HD, axis=1)

    (w1i,) = gathered(0, token)
    p1, act1 = _ffn_up(x, sh1, sc1, g_pre_f1, w1i)
    forwarded, token = _split_forward_groups("gw_fwd_f1_out_mix", [gather[3], gather[1]], act1)
    for gi, (ssem, rsem, lands) in zip((3, 1), forwarded):
        gather[gi] = (lands, ssem, rsem, gather[gi][5])
    (w1o,) = gathered(3, token)
    w1o = w1o.reshape(DFF, D)
    x1, f1 = _ffn_down(x, act1, gt1, g_post_f1, w1o)
    wmi, wmo, cw4 = gathered(1, x1)
    wmo = wmo.reshape(D, D)
    cw_full = jnp.transpose(cw4, (1, 0, 2)).reshape(CK, WB)
    cw_pad = jnp.pad(cw_full, ((0, HALO - CK), (0, 0)))
    u, v, a, g = _mix_in_fwd(x1, sh2, sc2, g_pre_m, wmi)
    token = gather_forward(2, u)
    x2, conv, yb, m = _mix_mid_fwd(x1, u, v, a, g, gt2 + token[0, 0], gmlp_norm_g, gmlp_norm_b, wcat, bspt, cw_pad, conv_b,
                                   conv_norm_g, conv_norm_b, g_out_a, g_out_b, wmo, g_post_m)
    w2i, w2o = gathered(2, [x2, token])
    w2o = w2o.reshape(DFF, D)
    dx3, df2, p2, lsum, dg_post_f2, dgt3 = _ffn_loss_fwd(x2, sh3, sc3, gt3, g_pre_f2, g_post_f2, w2i, w2o, loss_target)

    def chip4(pair, rows):
        return [arr.reshape(NCHIP, rows, arr.shape[-1]) for arr in pair]

    def scatter_start(tag, pairs, behind):
        srcs = [p[1] for p in pairs]
        lands = [lax.empty((3,) + a.shape[1:], a.dtype) for a in srcs]
        ssem, rsem, srcs, lands, token = _split_start("gs_start_" + tag, srcs, lands, _scatter_plan(len(srcs)),
                                                      3 * len(srcs), behind)
        return (srcs, lands, ssem, rsem), token

    def scatter_wait(tag, state, behind):
        srcs, lands, ssem, rsem = state
        return _split_wait("gs_wait_" + tag, srcs, lands, ssem, rsem, _scatter_plan(len(srcs)), behind)

    out = {}
    dx2, dp2, h3, a2, dg_pre_f2, dsh3, dsc3 = _ffn_bwd(
        dx3, x2, None, p2, sh3, sc3, gt3, g_pre_f2, g_post_f2, w2i, w2o, df=df2)
    gw2i = _wgrad("wgrad_f2_in", h3.reshape(t, D), dp2.reshape(t, 2 * DFF), 2 * DFF // NCHIP, True)
    gw2o = chip4(_wgrad("wgrad_f2_out", a2.reshape(t, DFF), df2.reshape(t, D), D // 2, False), DFF // NCHIP)
    scat_f2, tok = scatter_start("f2", [gw2i, gw2o], dg_post_f2)
    dy, dm, dg_post_m, dgt2 = _mix_out_bwd(dx2, m, gt2 + tok[0, 0], g_post_m, wmo)
    gwmo = chip4(_wgrad("wgrad_mix_out", yb.reshape(t, D), dm.reshape(t, D), D // 2, False), D // NCHIP)
    (du, dv, dconv, dwcat, dbsp, dgn_g, dgn_b, dgo_a, dgo_b, dcn_g, dcn_b, dcb) = _mix_mid_bwd(
        dy, u, v, conv, gmlp_norm_g, gmlp_norm_b, wcat, wcat_t, bspt, conv_norm_g, conv_norm_b, g_out_a, g_out_b)
    dx1, dproj, h2, dg_pre_m, dsh2, dsc2, dcw = _mix_in_bwd(dx2, x1, du, dv, dconv, a, g, sh2, sc2, g_pre_m, wmi, cw_pad)
    gwmi = _wgrad("wgrad_mix_in", h2.reshape(t, D), dproj.reshape(t, 4 * WA), WA, True)

    vec_grads = dict(g_pre_m=dg_pre_m, g_post_m=dg_post_m, g_pre_f2=dg_pre_f2, g_post_f2=dg_post_f2)
    pair_grads = dict(gmlp_norm_g=dgn_g, gmlp_norm_b=dgn_b, conv_b=dcb, conv_norm_g=dcn_g, conv_norm_b=dcn_b,
                      g_out_a=dgo_a, g_out_b=dgo_b)
    pack = _pack_small([vec_grads[n] for n in VEC_ORDER[2:]], [pair_grads[n] for n in PAIR_ORDER], dbsp, dwcat, lsum)
    dada_early = jnp.concatenate([q.reshape(nb, D) for q in (dsh2, dsc2, dgt2, dsh3, dsc3, dgt3)], axis=1)
    small_early = [pack, dcw, dada_early.reshape(8, (nb * 6 * D) // 8)]
    mix_bf16 = [gwmi[1], gwmo[1]]
    (s_mix, s_early), tok2 = _split_start_groups("gs_start_mix_small", [
        (mix_bf16, [lax.empty((3,) + a.shape[1:], a.dtype) for a in mix_bf16], _scatter_plan(2), 6),
        (small_early, [lax.dynamic_update_index_in_dim(lax.empty((NDEV,) + a.shape, a.dtype), a, dev, 0) for a in small_early],
         _allgather_plan(3), 21)], dg_pre_m)
    scat_mix = (s_mix[2], s_mix[3], s_mix[0], s_mix[1])
    early = (s_early[2], s_early[3], s_early[0], s_early[1])
    grad_x, dp1, h1, a1, df1, dg_pre_f1, dg_post_f1, dsh1, dsc1, dgt1 = _ffn_bwd(
        dx1, x, f1, p1, sh1 + tok2[0, 0], sc1, gt1, g_pre_f1, g_post_f1, w1i, w1o)
    late_pack = _pack_late([dg_pre_f1, dg_post_f1] + [q.reshape(nb, D) for q in (dsh1, dsc1, dgt1)])
    late, tok2 = allgather_start("late", [late_pack], dg_post_f1)
    gw1i = _wgrad("wgrad_f1_in", h1.reshape(t, D), dp1.reshape(t, 2 * DFF), 2 * DFF // NCHIP, True)
    gw1o = chip4(_wgrad("wgrad_f1_out", a1.reshape(t, DFF), df1.reshape(t, D), D // 2, False), DFF // NCHIP)
    def d2d_start(tag, srcs, lands, plan, behind):
        ssem, rsem, srcs, lands, token = _split_start("d2d_start_" + tag, srcs, lands, plan, len(srcs), behind)
        return (srcs, lands, ssem, rsem, plan), token

    def d2d_wait(tag, state, behind):
        srcs, lands, ssem, rsem, plan = state
        return _split_wait("d2d_wait_" + tag, srcs, lands, ssem, rsem, plan, behind)

    def swap_start(tag, parts, behind):
        return d2d_start(tag, parts, [lax.empty(a.shape, a.dtype) for a in parts], _swap_plan(len(parts)), behind)

    def sums(names, pairs, recv):
        return _sum4("sum4_" + names[0][2:4], [p[0] for p in pairs], recv, j_arr)

    def update(names, part, other):
        for k, n in enumerate(names):
            out[n] = tuple(r[None] for r in _adam_big("adam_" + n, wts[n][0], mom[n][0], var[n][0], part[k], other[k]))

    c_arr = ac.reshape(1).astype(jnp.int32)
    halves = [gw1i[1], gw1o[1]]
    pair_st, tok = d2d_start("pair", halves, [lax.empty((a.shape[0], a.shape[1] // 2, a.shape[2]), a.dtype) for a in halves],
                             _pair_plan([a.shape for a in halves]), tok2)
    names_f2, names_mix, names_f1 = ("w_f2_in", "w_f2_out"), ("w_mix_in", "w_mix_out"), ("w_f1_in", "w_f1_out")
    part_f2 = sums(names_f2, [gw2i, gw2o], scatter_wait("f2", scat_f2, tok))
    sib = d2d_wait("pair", pair_st, part_f2)
    pair_i, pair_o = _pair_sum("pairsum_f1", [gw1i[0], gw1o[0]], sib, c_arr)
    scat_f1, tok = scatter_start("f1", [pair_i, pair_o], tok2)
    swap_f2, tok = swap_start("swap_f2", part_f2, tok)
    part_mix = sums(names_mix, [gwmi, gwmo], scatter_wait("mix", scat_mix, tok))
    swap_mix, tok = swap_start("swap_mix", part_mix, part_mix[1])

    pack_all, dcw_all, dada_early8 = allgather_wait("early", early, tok)
    (late_all,) = allgather_wait("late", late, pack_all)
    dada_late = jnp.transpose(late_all[:, 2:8, :].reshape(NDEV, 3, nb, D), (0, 2, 1, 3)).reshape(NDEV * nb, 3 * D)
    dada_all = jnp.concatenate([dada_late, dada_early8.reshape(NDEV * nb, 6 * D)], axis=1)
    dada_sh = lax.dynamic_slice(dada_all, (0, j_chip * ADA_SH), (NDEV * nb, ADA_SH))
    out["w_ada"] = tuple(r[None] for r in _ada_bwd_adam(c_all, dada_sh, w_ada[0], m_w_ada[0], v_w_ada[0]))
    update(names_f2, part_f2, d2d_wait("swap_f2", swap_f2, out["w_ada"][3]))
    update(names_mix, part_mix, d2d_wait("swap_mix", swap_mix, out["w_f2_out"][3]))

    mine = sums(names_f1, [pair_i, pair_o], scatter_wait("f1", scat_f1, out["w_mix_out"][3]))
    swap_f1, tok = swap_start("swap_f1", mine, mine[1])
    dcw_mine = lax.dynamic_slice(dcw_all, (0, 0, j_chip * (WB // NCHIP)), (NDEV, HALO, WB // NCHIP))
    small = {n: (wts[n], mom[n], var[n]) for n in list(VEC_ORDER) + list(PAIR_ORDER) + ["b_spatial", "w_spatial", "conv_w", "b_ada"]}
    small_out, loss = _small_adam(pack_all, late_all, dcw_mine, dada_all, small, tok)
    out.update(small_out)
    theirs = d2d_wait("swap_f1", swap_f1, out["b_ada"][3])
    for k, n in enumerate(names_f1):
        out[n] = tuple(r[None] for r in _adam_halves("adam_" + n, wts[n][0], mom[n][0], var[n][0], mine[k], theirs[k],
                                                     c_arr))

    res = [loss, grad_x]
    for k in range(4):
        res += [out[n][k] for n in WEIGHTS]
    return tuple(res)
```

```python
import jax
import jax.numpy as jnp
from jax import lax
from jax.experimental import pallas as pl
from jax.experimental.pallas import tpu as pltpu

D = 1024
DFF = 2816
WA = 512
WB = 512
NH = 8
HD = 64
CH = 128
CK = 31
HALO = 32
NMOD = 9
EPS = 1e-6
NCHIP = 4
NDEV = 8
FBLK = DFF // 2
ADA_SH = NMOD * D // NCHIP

LR, B1, B2, EPS_A, WD, STEP = 0.001, 0.9, 0.999, 1e-08, 0.01, 10

F32 = jnp.float32
BF16 = jnp.bfloat16
MESH = pl.DeviceIdType.MESH
ANY = pl.BlockSpec(memory_space=pl.ANY)
VMEM_FULL = pl.BlockSpec(memory_space=pltpu.VMEM)
VMEM_LIMIT = 56 * 1024 * 1024
WGRAD_VMEM_BUDGET = 52 * 1024 * 1024
TM = 512
TM_FFN_BWD = 256

NT = (((1,), (1,)), ((), ()))
TN = (((0,), (0,)), ((), ()))


def _dot(a, b):
    return jnp.dot(a, b, preferred_element_type=F32)


def _dot_nt(a, b):
    return lax.dot_general(a, b, NT, preferred_element_type=F32)


def _dot_tn(a, b):
    return lax.dot_general(a, b, TN, preferred_element_type=F32)


def _cparams():
    return pltpu.CompilerParams(vmem_limit_bytes=VMEM_LIMIT)


def _chip_relations(x, y):
    return [(1 - x, y), (x, 1 - y), (1 - x, 1 - y)]


def _exchange(name, arrs, out_shapes, plan):
    n = len(arrs)
    n_out = len(out_shapes)

    def body(*refs):
        ins, outs = refs[:n], refs[n:n + n_out]
        send_sems, recv_sems, local_sems = refs[n + n_out:]
        x, y, c = lax.axis_index("x"), lax.axis_index("y"), lax.axis_index("c")
        local, sends = plan(x, y, c, ins, outs)
        locs = [pltpu.make_async_copy(s, d, local_sems.at[i]) for i, (s, d) in enumerate(local)]
        for loc in locs:
            loc.start()
        cps = [pltpu.make_async_remote_copy(src_ref=s, dst_ref=d, send_sem=send_sems.at[i], recv_sem=recv_sems.at[i],
                                            device_id=peer, device_id_type=MESH)
               for i, (s, d, peer, _) in enumerate(sends)]
        for cp in cps:
            cp.start()
        for i, (s, _, peer, landing) in enumerate(sends):
            pltpu.make_async_remote_copy(src_ref=s, dst_ref=landing, send_sem=send_sems.at[i], recv_sem=recv_sems.at[i],
                                         device_id=peer, device_id_type=MESH).wait_recv()
        for cp in cps:
            cp.wait_send()
        for loc in locs:
            loc.wait()

    return n, n_out, body


def _run_exchange(name, arrs, out_shapes, plan, n_local, n_send):
    n, n_out, body = _exchange(name, arrs, out_shapes, plan)
    return pl.pallas_call(
        body, name=name, out_shape=out_shapes,
        in_specs=[ANY] * n, out_specs=[ANY] * n_out,
        scratch_shapes=[pltpu.SemaphoreType.DMA((n_send,)), pltpu.SemaphoreType.DMA((n_send,)),
                        pltpu.SemaphoreType.DMA((max(n_local, 1),))],
    )(*arrs)


def _chip_allgather(name, arrs, behind=()):
    n = len(arrs)

    def plan(x, y, c, ins, outs):
        j_me = 2 * x + y
        local = [(ins[a], outs[a].at[j_me]) for a in range(n)]
        sends = []
        for a in range(n):
            for (px, py) in _chip_relations(x, y):
                sends.append((ins[a], outs[a].at[j_me], (px, py, c), outs[a].at[2 * px + py]))
        return local, sends

    shapes = [jax.ShapeDtypeStruct((NCHIP,) + a.shape, a.dtype) for a in arrs]
    return _run_exchange(name, list(arrs) + list(behind), shapes, plan, n, 3 * n)


HBM = pl.BlockSpec(memory_space=pltpu.HBM)
SEM = pl.BlockSpec(memory_space=pltpu.SEMAPHORE)
EFFECT = pltpu.SideEffectType.DATAFLOW_SIDE_EFFECTING


def _split_start_groups(name, groups, after):
    n_src = [len(g[0]) for g in groups]
    n_land = [len(g[1]) for g in groups]
    all_srcs = [pltpu.with_memory_space_constraint(a, pltpu.HBM) for g in groups for a in g[0]]
    all_lands = [pltpu.with_memory_space_constraint(a, pltpu.HBM) for g in groups for a in g[1]]
    ns, nl, ng = len(all_srcs), len(all_lands), len(groups)

    def body(*refs):
        src_refs, land_refs = refs[:ns], refs[ns:ns + nl]
        sem_refs = refs[ns + nl + 1:ns + nl + 1 + 2 * ng]
        token = refs[-1]
        x, y, c = lax.axis_index("x"), lax.axis_index("y"), lax.axis_index("c")
        at_src = at_land = 0
        for gi, (_, _, plan, _) in enumerate(groups):
            _, sends = plan(x, y, c, src_refs[at_src:at_src + n_src[gi]], land_refs[at_land:at_land + n_land[gi]])
            for i, (s, d, peer, _) in enumerate(sends):
                pltpu.make_async_remote_copy(src_ref=s, dst_ref=d, send_sem=sem_refs[2 * gi].at[i],
                                             recv_sem=sem_refs[2 * gi + 1].at[i], device_id=peer, device_id_type=MESH).start()
            at_src += n_src[gi]
            at_land += n_land[gi]
        token[...] = jnp.zeros_like(token)

    sems = [pltpu.SemaphoreType.DMA((g[3],)) for g in groups for _ in range(2)]
    res = pl.pallas_call(
        body, name=name,
        out_shape=(*sems, *[pltpu.HBM(a.shape, a.dtype) for a in all_lands], jax.ShapeDtypeStruct((8, 128), F32)),
        in_specs=[HBM] * (ns + nl) + [ANY],
        out_specs=(*([SEM] * (2 * ng)), *([HBM] * nl), pl.BlockSpec(memory_space=pltpu.VMEM)),
        input_output_aliases={ns + i: 2 * ng + i for i in range(nl)},
        compiler_params=pltpu.CompilerParams(has_side_effects=EFFECT),
    )(*all_srcs, *all_lands, after)
    out, at_src, at_land = [], 0, 2 * ng
    for gi in range(ng):
        out.append((res[2 * gi], res[2 * gi + 1], all_srcs[at_src:at_src + n_src[gi]],
                    list(res[at_land:at_land + n_land[gi]])))
        at_src += n_src[gi]
        at_land += n_land[gi]
    return out, res[-1]


def _split_start(name, srcs, lands, plan, n_send, after):
    (group,), token = _split_start_groups(name, [(srcs, lands, plan, n_send)], after)
    return (*group, token)


def _split_wait(name, srcs, lands, send_sems, recv_sems, plan, after):
    n, nl = len(srcs), len(lands)
    afters = list(after) if isinstance(after, (list, tuple)) else [after]

    def body(*refs):
        src, land = refs[:n], refs[n:n + nl]
        send_sems, recv_sems = refs[n + nl], refs[n + nl + 1]
        x, y, c = lax.axis_index("x"), lax.axis_index("y"), lax.axis_index("c")
        _, sends = plan(x, y, c, src, land)
        for i, (s, _, peer, landing) in enumerate(sends):
            cp = pltpu.make_async_remote_copy(src_ref=s, dst_ref=landing, send_sem=send_sems.at[i],
                                              recv_sem=recv_sems.at[i], device_id=peer, device_id_type=MESH)
            cp.wait_send()
            cp.wait_recv()

    thru = [pltpu.HBM(a.shape, a.dtype) for a in lands]
    res = pl.pallas_call(
        body, name=name, out_shape=tuple(thru),
        in_specs=[HBM] * (n + nl) + [SEM, SEM] + [ANY] * len(afters), out_specs=tuple([HBM] * nl),
        input_output_aliases={n + i: i for i in range(nl)},
        compiler_params=pltpu.CompilerParams(has_side_effects=EFFECT),
    )(*srcs, *lands, send_sems, recv_sems, *afters)
    return list(res)


def _split_forward_groups(name, groups, after):
    n_src = [len(g[0]) for g in groups]
    n_land = [len(g[1]) for g in groups]
    all_srcs = [a for g in groups for a in g[0]]
    all_lands = [a for g in groups for a in g[1]]
    ns, nl, ng = len(all_srcs), len(all_lands), len(groups)

    def body(*refs):
        src_refs, land_refs = refs[:ns], refs[ns:ns + nl]
        sems_a = refs[ns + nl:ns + nl + 2 * ng]
        sems_b = refs[ns + nl + 2 * ng + 1:ns + nl + 4 * ng + 1]
        token = refs[-1]
        x, y, c = lax.axis_index("x"), lax.axis_index("y"), lax.axis_index("c")
        at_src = at_land = 0
        for gi, g in enumerate(groups):
            src, land = src_refs[at_src:at_src + n_src[gi]], land_refs[at_land:at_land + n_land[gi]]
            plan_a, plan_b = g[4], g[5]
            _, first = plan_a(x, y, c, src, land)
            for i, (s, _, peer, landing) in enumerate(first):
                cp = pltpu.make_async_remote_copy(src_ref=s, dst_ref=landing, send_sem=sems_a[2 * gi].at[i],
                                                  recv_sem=sems_a[2 * gi + 1].at[i], device_id=peer, device_id_type=MESH)
                cp.wait_send()
                cp.wait_recv()
            _, second = plan_b(x, y, c, src, land)
            for i, (s, d, peer, _) in enumerate(second):
                pltpu.make_async_remote_copy(src_ref=s, dst_ref=d, send_sem=sems_b[2 * gi].at[i],
                                             recv_sem=sems_b[2 * gi + 1].at[i], device_id=peer, device_id_type=MESH).start()
            at_src += n_src[gi]
            at_land += n_land[gi]
        token[...] = jnp.zeros_like(token)

    sems = [pltpu.SemaphoreType.DMA((g[6],)) for g in groups for _ in range(2)]
    res = pl.pallas_call(
        body, name=name,
        out_shape=(*sems, *[pltpu.HBM(a.shape, a.dtype) for a in all_lands], jax.ShapeDtypeStruct((8, 128), F32)),
        in_specs=[HBM] * (ns + nl) + [SEM] * (2 * ng) + [ANY],
        out_specs=(*([SEM] * (2 * ng)), *([HBM] * nl), pl.BlockSpec(memory_space=pltpu.VMEM)),
        input_output_aliases={ns + i: 2 * ng + i for i in range(nl)},
        compiler_params=pltpu.CompilerParams(has_side_effects=EFFECT),
    )(*all_srcs, *all_lands, *[s for g in groups for s in (g[2], g[3])], after)
    out, at_land = [], 2 * ng
    for gi in range(ng):
        out.append((res[2 * gi], res[2 * gi + 1], list(res[at_land:at_land + n_land[gi]])))
        at_land += n_land[gi]
    return out, res[-1]


def _split_forward(name, srcs, lands, send_a, recv_a, plan_a, plan_b, n_b, after):
    (group,), token = _split_forward_groups(name, [(srcs, lands, send_a, recv_a, plan_a, plan_b, n_b)], after)
    return (*group, token)


def _gather_plans(shapes):
    n = len(shapes)

    def halves(a, c):
        rows = shapes[a][0] // 2
        return pl.ds(pl.multiple_of(c * rows, 16), rows), pl.ds(pl.multiple_of((1 - c) * rows, 16), rows)

    def split(a):
        return shapes[a][0] % 32 == 0

    def plan_a(x, y, c, src, land):
        j_me = 2 * x + y
        sends = []
        for a in range(n):
            for (px, py) in _chip_relations(x, y):
                if split(a):
                    mine, _ = halves(a, c)
                    sends.append((src[a].at[mine], land[a].at[j_me, mine], (px, py, c), land[a].at[2 * px + py, mine]))
                else:
                    sends.append((src[a], land[a].at[j_me], (px, py, c), land[a].at[2 * px + py]))
        return [], sends

    def plan_b(x, y, c, src, land):
        sends = []
        for a in range(n):
            if split(a):
                mine, other = halves(a, c)
                for (px, py) in _chip_relations(x, y):
                    j = 2 * px + py
                    sends.append((land[a].at[j, mine], land[a].at[j, mine], (x, y, 1 - c), land[a].at[j, other]))
        return [], sends

    n_b = 3 * sum(1 for a in range(n) if split(a))
    return plan_a, plan_b, n_b


def _allgather_plan(n):
    flips = [(dx, dy, dc) for dx in (0, 1) for dy in (0, 1) for dc in (0, 1) if dx or dy or dc]

    def plan(x, y, c, src, land):
        sends = []
        for a in range(n):
            for dx, dy, dc in flips:
                px, py, pc = x ^ dx, y ^ dy, c ^ dc
                sends.append((src[a], land[a].at[4 * x + 2 * y + c], (px, py, pc), land[a].at[4 * px + 2 * py + pc]))
        return [], sends

    return plan


def _scatter_plan(n):
    def plan(x, y, c, src, land):
        sends = []
        for a in range(n):
            for k, (px, py) in enumerate(_chip_relations(x, y)):
                sends.append((src[a].at[2 * px + py], land[a].at[k], (px, py, c), land[a].at[k]))
        return [], sends

    return plan


def _rms(x):
    r = lax.rsqrt(jnp.mean(x * x, axis=-1, keepdims=True) + EPS)
    return x * r, r


def _rms_bwd(dy, n, r, g):
    dg = jnp.sum(dy * n, axis=0, keepdims=True)
    dn = dy * g
    dx = r * (dn - n * jnp.mean(dn * n, axis=-1, keepdims=True))
    return dx, dg


def _ln(x):
    mu = jnp.mean(x, axis=-1, keepdims=True)
    xc = x - mu
    rstd = lax.rsqrt(jnp.mean(xc * xc, axis=-1, keepdims=True) + EPS)
    return xc * rstd, rstd


def _ln_bwd(dy, xhat, rstd, g):
    dg = jnp.sum(dy * xhat, axis=0, keepdims=True)
    db = jnp.sum(dy, axis=0, keepdims=True)
    dxh = dy * g
    dx = rstd * (dxh - jnp.mean(dxh, axis=-1, keepdims=True) - xhat * jnp.mean(dxh * xhat, axis=-1, keepdims=True))
    return dx, dg, db


def _sigmoid(x):
    return jax.nn.sigmoid(x)


def _dsilu(x, s):
    return s * (1.0 + x * (1.0 - s))


def _adam(w, g, m, v):
    m = B1 * m + (1.0 - B1) * g
    v = B2 * v + (1.0 - B2) * (g * g)
    m_hat = m / (1.0 - B1 ** STEP)
    v_hat = v / (1.0 - B2 ** STEP)
    delta = -LR * (m_hat / (jnp.sqrt(v_hat) + EPS_A) + WD * w)
    return delta, m, v


def _head_mask(shape):
    lane = lax.broadcasted_iota(jnp.int32, shape, len(shape) - 1)
    return [(lane >= h * HD) & (lane < (h + 1) * HD) for h in range(NH)]


def _first(b, i):
    return jnp.logical_and(b == 0, i == 0)


def _acc(ref, val, first):
    @pl.when(first)
    def _():
        ref[...] = val

    @pl.when(jnp.logical_not(first))
    def _():
        ref[...] += val


def _ada_fwd(c_all, w_sh, b_sh):
    nb = c_all.shape[0]
    tn = 768

    def body(c_ref, w_ref, b_ref, o_ref):
        cv = c_ref[...]
        cs = (cv * _sigmoid(cv)).astype(BF16)
        o_ref[...] = _dot(cs, w_ref[...].astype(BF16)) + b_ref[...]

    return pl.pallas_call(
        body, name="ada_fwd", grid=(ADA_SH // tn,),
        out_shape=jax.ShapeDtypeStruct((nb, ADA_SH), F32),
        in_specs=[pl.BlockSpec((nb, D), lambda j: (0, 0)), pl.BlockSpec((D, tn), lambda j: (0, j)),
                  pl.BlockSpec((1, tn), lambda j: (0, j))],
        out_specs=pl.BlockSpec((nb, tn), lambda j: (0, j)),
        compiler_params=_cparams(),
    )(c_all, w_sh, b_sh)


def _ada_bwd_adam(c_all, dada_sh, w, m, v):
    nb = c_all.shape[0]
    tn = 768

    def body(c_ref, d_ref, w_ref, m_ref, v_ref, g_out, d_out, m_out, v_out):
        cv = c_ref[...]
        cs = (cv * _sigmoid(cv)).astype(BF16)
        g = _dot_tn(cs, d_ref[...].astype(BF16))
        delta, m2, v2 = _adam(w_ref[...], g, m_ref[...], v_ref[...])
        g_out[...] = g
        d_out[...] = delta
        m_out[...] = m2
        v_out[...] = v2

    big = pl.BlockSpec((D, tn), lambda j: (0, j))
    shape = jax.ShapeDtypeStruct((D, ADA_SH), F32)
    return pl.pallas_call(
        body, name="ada_bwd_adam", grid=(ADA_SH // tn,),
        out_shape=[shape] * 4,
        in_specs=[pl.BlockSpec((nb, D), lambda j: (0, 0)), pl.BlockSpec((nb, tn), lambda j: (0, j)), big, big, big],
        out_specs=[big] * 4,
        compiler_params=_cparams(),
    )(c_all, dada_sh, w, m, v)


def _tok_specs(tm, width):
    return pl.BlockSpec((1, tm, width), lambda b, i: (b, i, 0))


def _mod_spec():
    return pl.BlockSpec((1, 1, D), lambda b, i: (b, 0, 0))


def _row_spec(width=D):
    return pl.BlockSpec((1, width), lambda b, i: (0, 0))


def _ffn_loss_fwd(x, sh, sc, gt, g_pre, g_post, w_in4, w_out, target):
    nb, s, _ = x.shape
    tm = min(TM, s)

    def body(x_ref, sh_ref, sc_ref, gt_ref, gpre_ref, gpost_ref, win_ref, wout_ref, tgt_ref,
             xo_ref, df_ref, p_ref, ls_ref, dgpost_ref, dgt_ref):
        xv = x_ref[0]
        n, _ = _rms(xv)
        h = (n * gpre_ref[...]) * (1.0 + sc_ref[0]) + sh_ref[0]
        hb = h.astype(BF16)
        acc = jnp.zeros((tm, D), F32)
        for j in range(2):
            gate = _dot(hb, win_ref[j])
            up = _dot(hb, win_ref[2 + j])
            p_ref[0, :, j * FBLK:(j + 1) * FBLK] = gate.astype(BF16)
            p_ref[0, :, DFF + j * FBLK:DFF + (j + 1) * FBLK] = up.astype(BF16)
            a = (gate * _sigmoid(gate)) * up
            acc = acc + _dot(a.astype(BF16), wout_ref[j * FBLK:(j + 1) * FBLK, :])
        nf, q = _rms(acc)
        gpost = gpost_ref[...]
        half_gate = 0.5 * gt_ref[0]
        out = xv + half_gate * (nf * gpost)
        first = _first(pl.program_id(0), pl.program_id(1))
        err = out - tgt_ref[0]
        dout = err * (1.0 / D)
        xo_ref[0] = dout
        row = jnp.sum(err * err, axis=0, keepdims=True)
        part = row[:, 0:128]
        for k in range(1, D // 128):
            part = part + row[:, k * 128:(k + 1) * 128]
        _acc(ls_ref, part, first)
        df, dgpost = _rms_bwd(dout * half_gate, nf, q, gpost)
        df_ref[0] = df.astype(BF16)
        _acc(dgpost_ref, dgpost, first)
        _acc(dgt_ref, jnp.sum(dout * (0.5 * (nf * gpost)), axis=0, keepdims=True)[None], pl.program_id(1) == 0)

    tok = _tok_specs(tm, D)
    return pl.pallas_call(
        body, name="ffn_loss_fwd", grid=(nb, s // tm),
        out_shape=[jax.ShapeDtypeStruct((nb, s, D), F32), jax.ShapeDtypeStruct((nb, s, D), BF16),
                   jax.ShapeDtypeStruct((nb, s, 2 * DFF), BF16), jax.ShapeDtypeStruct((1, 128), F32),
                   jax.ShapeDtypeStruct((1, D), F32), jax.ShapeDtypeStruct((nb, 1, D), F32)],
        in_specs=[tok, _mod_spec(), _mod_spec(), _mod_spec(), _row_spec(), _row_spec(), VMEM_FULL, VMEM_FULL, tok],
        out_specs=[tok, tok, _tok_specs(tm, 2 * DFF), pl.BlockSpec((1, 128), lambda b, i: (0, 0)), _row_spec(), _mod_spec()],
        compiler_params=_cparams(),
    )(x, sh, sc, gt, g_pre, g_post, w_in4, w_out, target)


def _ffn_up(x, sh, sc, g_pre, w_in4):
    nb, s, _ = x.shape
    tm = min(TM, s)

    def body(x_ref, sh_ref, sc_ref, gpre_ref, win_ref, p_ref, a_ref):
        n, _ = _rms(x_ref[0])
        hb = ((n * gpre_ref[...]) * (1.0 + sc_ref[0]) + sh_ref[0]).astype(BF16)
        for j in range(2):
            gate = _dot(hb, win_ref[j])
            up = _dot(hb, win_ref[2 + j])
            p_ref[0, :, j * FBLK:(j + 1) * FBLK] = gate.astype(BF16)
            p_ref[0, :, DFF + j * FBLK:DFF + (j + 1) * FBLK] = up.astype(BF16)
            a_ref[0, :, j * FBLK:(j + 1) * FBLK] = ((gate * _sigmoid(gate)) * up).astype(BF16)

    return pl.pallas_call(
        body, name="ffn_up", grid=(nb, s // tm),
        out_shape=[jax.ShapeDtypeStruct((nb, s, 2 * DFF), BF16), jax.ShapeDtypeStruct((nb, s, DFF), BF16)],
        in_specs=[_tok_specs(tm, D), _mod_spec(), _mod_spec(), _row_spec(), VMEM_FULL],
        out_specs=[_tok_specs(tm, 2 * DFF), _tok_specs(tm, DFF)],
        compiler_params=_cparams(),
    )(x, sh, sc, g_pre, w_in4)


def _ffn_down(x, a, gt, g_post, w_out):
    nb, s, _ = x.shape
    tm = min(TM, s)

    def body(x_ref, a_ref, gt_ref, gpost_ref, wout_ref, xo_ref, f_ref):
        acc = _dot(a_ref[0], wout_ref[...])
        f_ref[0] = acc
        nf, _ = _rms(acc)
        xo_ref[0] = x_ref[0] + (0.5 * gt_ref[0]) * (nf * gpost_ref[...])

    tok = _tok_specs(tm, D)
    shape = jax.ShapeDtypeStruct((nb, s, D), F32)
    return pl.pallas_call(
        body, name="ffn_down", grid=(nb, s // tm), out_shape=[shape, shape],
        in_specs=[tok, _tok_specs(tm, DFF), _mod_spec(), _row_spec(), VMEM_FULL],
        out_specs=[tok, tok],
        compiler_params=_cparams(),
    )(x, a, gt, g_post, w_out)


def _ffn_bwd(dxo, x, f, p, sh, sc, gt, g_pre, g_post, w_in4, w_out, df=None):
    nb, s, _ = x.shape
    tm = min(TM_FFN_BWD, s)
    given = df is not None

    def body(*refs):
        if given:
            (dxo_ref, x_ref, dfin_ref, p_ref, sh_ref, sc_ref, gpre_ref, win_ref, wout_ref,
             dx_ref, dp_ref, h_ref, a_ref, dgpre_ref, dsh_ref, dsc_ref) = refs
        else:
            (dxo_ref, x_ref, f_ref, p_ref, sh_ref, sc_ref, gt_ref, gpre_ref, gpost_ref, win_ref, wout_ref,
             dx_ref, dp_ref, h_ref, a_ref, df_ref, dgpre_ref, dgpost_ref, dsh_ref, dsc_ref, dgt_ref) = refs
        b, i = pl.program_id(0), pl.program_id(1)
        dxo_v = dxo_ref[0]
        if given:
            dfb = dfin_ref[0]
        else:
            nf, q = _rms(f_ref[0])
            gpost = gpost_ref[...]
            dgt = jnp.sum(dxo_v * (0.5 * (nf * gpost)), axis=0, keepdims=True)
            do = dxo_v * (0.5 * gt_ref[0])
            dfv, dgpost = _rms_bwd(do, nf, q, gpost)
            dfb = dfv.astype(BF16)
            df_ref[0] = dfb
        xv = x_ref[0]
        n, r = _rms(xv)
        gpre = gpre_ref[...]
        ng = n * gpre
        scale1 = 1.0 + sc_ref[0]
        h = ng * scale1 + sh_ref[0]
        h_ref[0] = h.astype(BF16)
        dh = jnp.zeros((tm, D), F32)
        for j in range(2):
            gate = p_ref[0, :, j * FBLK:(j + 1) * FBLK].astype(F32)
            up = p_ref[0, :, DFF + j * FBLK:DFF + (j + 1) * FBLK].astype(F32)
            sg = _sigmoid(gate)
            act = gate * sg
            a_ref[0, :, j * FBLK:(j + 1) * FBLK] = (act * up).astype(BF16)
            da = _dot_nt(dfb, wout_ref[j * FBLK:(j + 1) * FBLK, :])
            dgate = (da * up * _dsilu(gate, sg)).astype(BF16)
            dup = (da * act).astype(BF16)
            dp_ref[0, :, j * FBLK:(j + 1) * FBLK] = dgate
            dp_ref[0, :, DFF + j * FBLK:DFF + (j + 1) * FBLK] = dup
            dh = dh + _dot_nt(dgate, win_ref[j]) + _dot_nt(dup, win_ref[2 + j])
        dsh = jnp.sum(dh, axis=0, keepdims=True)
        dsc = jnp.sum(dh * ng, axis=0, keepdims=True)
        dxn, dgpre = _rms_bwd(dh * scale1, n, r, gpre)
        dx_ref[0] = dxo_v + dxn
        _acc(dgpre_ref, dgpre, _first(b, i))
        _acc(dsh_ref, dsh[None], i == 0)
        _acc(dsc_ref, dsc[None], i == 0)
        if not given:
            _acc(dgpost_ref, dgpost, _first(b, i))
            _acc(dgt_ref, dgt[None], i == 0)

    tok = _tok_specs(tm, D)
    mod_shape = jax.ShapeDtypeStruct((nb, 1, D), F32)
    row_shape = jax.ShapeDtypeStruct((1, D), F32)
    big = [jax.ShapeDtypeStruct((nb, s, D), F32), jax.ShapeDtypeStruct((nb, s, 2 * DFF), BF16),
           jax.ShapeDtypeStruct((nb, s, D), BF16), jax.ShapeDtypeStruct((nb, s, DFF), BF16)]
    big_specs = [tok, _tok_specs(tm, 2 * DFF), tok, _tok_specs(tm, DFF)]
    if given:
        return pl.pallas_call(
            body, name="ffn_bwd_after_loss", grid=(nb, s // tm),
            out_shape=big + [row_shape, mod_shape, mod_shape],
            in_specs=[tok, tok, tok, _tok_specs(tm, 2 * DFF), _mod_spec(), _mod_spec(), _row_spec(), VMEM_FULL, VMEM_FULL],
            out_specs=big_specs + [_row_spec(), _mod_spec(), _mod_spec()],
            compiler_params=_cparams(),
        )(dxo, x, df, p, sh, sc, g_pre, w_in4, w_out)
    return pl.pallas_call(
        body, name="ffn_bwd", grid=(nb, s // tm),
        out_shape=big + [jax.ShapeDtypeStruct((nb, s, D), BF16), row_shape, row_shape, mod_shape, mod_shape, mod_shape],
        in_specs=[tok, tok, tok, _tok_specs(tm, 2 * DFF), _mod_spec(), _mod_spec(), _mod_spec(), _row_spec(), _row_spec(),
                  VMEM_FULL, VMEM_FULL],
        out_specs=big_specs + [tok, _row_spec(), _row_spec(), _mod_spec(), _mod_spec(), _mod_spec()],
        compiler_params=_cparams(),
    )(dxo, x, f, p, sh, sc, gt, g_pre, g_post, w_in4, w_out)


def _wgrad(name, a, b, col_block, chip_major):
    t, ka = a.shape
    n = b.shape[1]
    def vmem_bytes(rows):
        return 2 * 2 * rows * (ka + col_block) + 4 * ka * col_block + 2 * (4 + 2) * ka * col_block

    tk = min(t, 512)
    while tk * 2 <= t and t % (tk * 2) == 0 and vmem_bytes(tk * 2) <= WGRAD_VMEM_BUDGET:
        tk *= 2
    nk = t // tk
    nblk = n // col_block

    def body(a_ref, b_ref, o_ref, obf_ref, acc_ref):
        k = pl.program_id(1)

        @pl.when(k == 0)
        def _():
            acc_ref[...] = jnp.zeros_like(acc_ref)

        acc_ref[...] += _dot_tn(a_ref[...], b_ref[...])

        @pl.when(k == nk - 1)
        def _():
            val = acc_ref[...]
            if chip_major:
                o_ref[0] = val
                obf_ref[0] = val.astype(BF16)
            else:
                o_ref[...] = val
                obf_ref[...] = val.astype(BF16)

    if chip_major:
        shape = (nblk, ka, col_block)
        ospec = pl.BlockSpec((1, ka, col_block), lambda j, k: (j, 0, 0))
    else:
        shape = (ka, n)
        ospec = pl.BlockSpec((ka, col_block), lambda j, k: (0, j))
    return pl.pallas_call(
        body, name=name, grid=(nblk, nk),
        out_shape=[jax.ShapeDtypeStruct(shape, F32), jax.ShapeDtypeStruct(shape, BF16)],
        in_specs=[pl.BlockSpec((tk, ka), lambda j, k: (k, 0)), pl.BlockSpec((tk, col_block), lambda j, k: (k, j))],
        out_specs=[ospec, ospec],
        scratch_shapes=[pltpu.VMEM((ka, col_block), F32)],
        compiler_params=_cparams(),
    )(a, b)


def _mix_in_fwd(x, sh, sc, g_pre, w_mi4):
    nb, s, _ = x.shape
    tm = min(TM, s)

    def body(x_ref, sh_ref, sc_ref, gpre_ref, w_ref, u_ref, v_ref, a_ref, g_ref):
        n, _ = _rms(x_ref[0])
        hb = ((n * gpre_ref[...]) * (1.0 + sc_ref[0]) + sh_ref[0]).astype(BF16)
        for k, o_ref in enumerate((u_ref, v_ref, a_ref, g_ref)):
            o_ref[0] = _dot(hb, w_ref[k])

    shape = jax.ShapeDtypeStruct((nb, s, WA), F32)
    return pl.pallas_call(
        body, name="mix_in_fwd", grid=(nb, s // tm),
        out_shape=[shape] * 4,
        in_specs=[_tok_specs(tm, D), _mod_spec(), _mod_spec(), _row_spec(), VMEM_FULL],
        out_specs=[_tok_specs(tm, WA)] * 4,
        compiler_params=_cparams(),
    )(x, sh, sc, g_pre, w_mi4)


def _spatial_weights(wcat_ref, transposed):
    w = wcat_ref[...]
    row = lax.broadcasted_iota(jnp.int32, w.shape, 0)
    col = lax.broadcasted_iota(jnp.int32, w.shape, 1)
    keep = ((row & (CH - 1)) <= col) if transposed else ((col & (CH - 1)) <= row)
    return jnp.where(keep, w, 0.0).astype(BF16)


def _expand_heads(vc, masks):
    return jnp.concatenate([jnp.where(mk, vc, jnp.zeros_like(vc)) for mk in masks], axis=0)


def _spatial_bias(bspt_ref):
    return bspt_ref[...]


SHIFTS = 8
TAP_ROWS = 64


def _ext_rows(tm):
    return tm + HALO + SHIFTS


def _make_shifts(ext_ref, sh_ref, tm):
    ext_ref[tm + HALO:tm + HALO + SHIFTS, :] = jnp.zeros((SHIFTS, WB), F32)
    for r in range(SHIFTS):
        sh_ref[r] = ext_ref[r:r + tm + HALO, :]


def _conv_taps(sh_ref, w_ref, tm, taps, emit):
    def block(i, carry):
        r0 = pl.multiple_of(i * TAP_ROWS, TAP_ROWS)
        acc = jnp.zeros((TAP_ROWS, WB), F32)
        for o, k in taps:
            acc = acc + w_ref[k:k + 1, :] * sh_ref[o % SHIFTS, pl.ds(r0 + SHIFTS * (o // SHIFTS), TAP_ROWS), :]
        emit(r0, acc)
        return carry

    lax.fori_loop(0, tm // TAP_ROWS, block, 0)


def _halo_prev_spec(tm):
    return pl.BlockSpec((1, HALO, WB), lambda b, i: (b, jnp.maximum(i * (tm // HALO) - 1, 0), 0))


def _halo_next_spec(tm, s):
    return pl.BlockSpec((1, HALO, WB), lambda b, i: (b, jnp.minimum((i + 1) * (tm // HALO), s // HALO - 1), 0))


def _mix_mid_fwd(x, u, v, a, g, gt, gn_g, gn_b, wcat, bspt, conv_w, conv_b, cn_g, cn_b, go_a, go_b, w_mo, g_post):
    nb, s, _ = x.shape
    tm = min(TM, s)

    def body(x_ref, u_ref, v_ref, a_ref, g_ref, ah_ref, gh_ref, gt_ref, gng_ref, gnb_ref, wcat_ref, bspt_ref,
             cw_ref, cb_ref, cng_ref, cnb_ref, goa_ref, gob_ref, wmo_ref, gpost_ref,
             xo_ref, conv_ref, y_ref, m_ref, ext_ref, sh_ref):
        i = pl.program_id(1)
        xhat, _ = _ln(v_ref[0])
        vb = (xhat * gng_ref[...] + gnb_ref[...]).astype(BF16)
        wsb = _spatial_weights(wcat_ref, False)
        bias = _spatial_bias(bspt_ref)
        masks = _head_mask((CH, WA))
        zs = []
        for cidx in range(tm // CH):
            vexp = _expand_heads(vb[cidx * CH:(cidx + 1) * CH, :], masks)
            zs.append(_dot(wsb, vexp) + bias)
        z = jnp.concatenate(zs, axis=0)
        na, _ = _rms(u_ref[0] * z)
        keep = jnp.where(i == 0, 0.0, 1.0).astype(F32)
        ext_ref[0:HALO, :] = (ah_ref[0] * _sigmoid(gh_ref[0])) * keep
        ext_ref[HALO:HALO + tm, :] = a_ref[0] * _sigmoid(g_ref[0])
        _make_shifts(ext_ref, sh_ref, tm)
        cb = cb_ref[...]

        def put_conv(r0, acc):
            conv_ref[0, pl.ds(r0, TAP_ROWS), :] = acc + cb

        _conv_taps(sh_ref, cw_ref, tm, [(k + HALO - (CK - 1), k) for k in range(CK)], put_conv)
        conv = conv_ref[0]
        chat, _ = _ln(conv)
        cln = chat * cng_ref[...] + cnb_ref[...]
        nbb, _ = _rms(cln * _sigmoid(cln))
        yb = jnp.concatenate([na * goa_ref[...], nbb * gob_ref[...]], axis=1).astype(BF16)
        y_ref[0] = yb
        m = _dot(yb, wmo_ref[...])
        m_ref[0] = m
        nm, _ = _rms(m)
        xo_ref[0] = x_ref[0] + gt_ref[0] * (nm * gpost_ref[...])

    t5 = _tok_specs(tm, WA)
    tok = _tok_specs(tm, D)
    r5 = _row_spec(WA)
    full = lambda shape: pl.BlockSpec(shape, lambda b, i: (0,) * len(shape))
    return pl.pallas_call(
        body, name="mix_mid_fwd", grid=(nb, s // tm),
        out_shape=[jax.ShapeDtypeStruct((nb, s, D), F32), jax.ShapeDtypeStruct((nb, s, WB), F32),
                   jax.ShapeDtypeStruct((nb, s, D), BF16), jax.ShapeDtypeStruct((nb, s, D), F32)],
        in_specs=[tok, t5, t5, t5, t5, _halo_prev_spec(tm), _halo_prev_spec(tm), _mod_spec(), r5, r5,
                  full((CH, NH * CH)), full((CH, WA)), full((HALO, WB)), r5, r5, r5, r5, r5, VMEM_FULL, _row_spec()],
        out_specs=[tok, t5, tok, tok],
        scratch_shapes=[pltpu.VMEM((_ext_rows(tm), WB), F32), pltpu.VMEM((SHIFTS, tm + HALO, WB), F32)],
        compiler_params=_cparams(),
    )(x, u, v, a, g, a, g, gt, gn_g, gn_b, wcat, bspt, conv_w, conv_b, cn_g, cn_b, go_a, go_b, w_mo, g_post)


def _mix_out_bwd(dxo, m, gt, g_post, w_mo):
    nb, s, _ = m.shape
    tm = min(TM, s)

    def body(dxo_ref, m_ref, gt_ref, gpost_ref, wmo_ref, dy_ref, dm_ref, dgpost_ref, dgt_ref):
        b, i = pl.program_id(0), pl.program_id(1)
        dxo_v = dxo_ref[0]
        nm, q = _rms(m_ref[0])
        gpost = gpost_ref[...]
        dgt = jnp.sum(dxo_v * (nm * gpost), axis=0, keepdims=True)
        dm, dgpost = _rms_bwd(dxo_v * gt_ref[0], nm, q, gpost)
        dmb = dm.astype(BF16)
        dm_ref[0] = dmb
        dy_ref[0] = _dot_nt(dmb, wmo_ref[...])
        _acc(dgpost_ref, dgpost, _first(b, i))
        _acc(dgt_ref, dgt[None], i == 0)

    tok = _tok_specs(tm, D)
    return pl.pallas_call(
        body, name="mix_out_bwd", grid=(nb, s // tm),
        out_shape=[jax.ShapeDtypeStruct((nb, s, D), F32), jax.ShapeDtypeStruct((nb, s, D), BF16),
                   jax.ShapeDtypeStruct((1, D), F32), jax.ShapeDtypeStruct((nb, 1, D), F32)],
        in_specs=[tok, tok, _mod_spec(), _row_spec(), VMEM_FULL],
        out_specs=[tok, tok, _row_spec(), _mod_spec()],
        compiler_params=_cparams(),
    )(dxo, m, gt, g_post, w_mo)


def _mix_mid_bwd(dy, u, v, conv, gn_g, gn_b, wcat, wcat_t, bspt, cn_g, cn_b, go_a, go_b):
    nb, s, _ = dy.shape
    tm = min(TM, s)
    nchunk = tm // CH

    def body(dy_ref, u_ref, v_ref, conv_ref, gng_ref, gnb_ref, wcat_ref, wcatt_ref, bspt_ref, cng_ref, cnb_ref,
             goa_ref, gob_ref,
             du_ref, dv_ref, dconv_ref, dwcat_ref, dbsp_ref, dgng_ref, dgnb_ref, dgoa_ref, dgob_ref,
             dcng_ref, dcnb_ref, dcb_ref):
        first = _first(pl.program_id(0), pl.program_id(1))
        dyv = dy_ref[0]
        xhat, rstd = _ln(v_ref[0])
        gng = gng_ref[...]
        vb = (xhat * gng + gnb_ref[...]).astype(BF16)
        wsb = _spatial_weights(wcat_ref, False)
        wsb_t = _spatial_weights(wcatt_ref, True)
        bias = _spatial_bias(bspt_ref)
        masks = _head_mask((CH, WA))
        vexps, zs = [], []
        for cidx in range(nchunk):
            vexp = _expand_heads(vb[cidx * CH:(cidx + 1) * CH, :], masks)
            vexps.append(vexp)
            zs.append(_dot(wsb, vexp) + bias)
        z = jnp.concatenate(zs, axis=0)
        uv = u_ref[0]
        na, ra = _rms(uv * z)
        dya, dgoa = _rms_bwd(dyv[:, 0:WA], na, ra, goa_ref[...])
        du_ref[0] = dya * z
        dz = dya * uv
        dwcat = jnp.zeros((CH, NH * CH), F32)
        dzsum = jnp.zeros((CH, WA), F32)
        dvlns = []
        for cidx in range(nchunk):
            dzc = dz[cidx * CH:(cidx + 1) * CH, :]
            dzsum = dzsum + dzc
            dzb = dzc.astype(BF16)
            dwcat = dwcat + _dot_nt(dzb, vexps[cidx])
            dvexp = _dot(wsb_t, dzb)
            dvl = jnp.zeros((CH, WA), F32)
            for h in range(NH):
                dvl = dvl + jnp.where(masks[h], dvexp[h * CH:(h + 1) * CH, :], 0.0)
            dvlns.append(dvl)
        dvln = jnp.concatenate(dvlns, axis=0)
        dv, dgng, dgnb = _ln_bwd(dvln, xhat, rstd, gng)
        dv_ref[0] = dv
        lane = lax.broadcasted_iota(jnp.int32, (NH, WA), 1)
        head = lax.broadcasted_iota(jnp.int32, (NH, WA), 0)
        sel = jnp.where((lane >= head * HD) & (lane < (head + 1) * HD), 1.0, 0.0).astype(F32)
        dbsp = lax.dot_general(sel, dzsum, NT, preferred_element_type=F32, precision=lax.Precision.HIGHEST)
        chat, crstd = _ln(conv_ref[0])
        cng = cng_ref[...]
        cln = chat * cng + cnb_ref[...]
        sg = _sigmoid(cln)
        nbb, rb = _rms(cln * sg)
        dyb, dgob = _rms_bwd(dyv[:, WA:D], nbb, rb, gob_ref[...])
        dconv, dcng, dcnb = _ln_bwd(dyb * _dsilu(cln, sg), chat, crstd, cng)
        dconv_ref[0] = dconv
        dcb = jnp.sum(dconv, axis=0, keepdims=True)
        for ref, val in ((dwcat_ref, dwcat), (dbsp_ref, dbsp), (dgng_ref, dgng), (dgnb_ref, dgnb), (dgoa_ref, dgoa),
                         (dgob_ref, dgob), (dcng_ref, dcng), (dcnb_ref, dcnb), (dcb_ref, dcb)):
            _acc(ref, val, first)

    t5 = _tok_specs(tm, WA)
    r5 = _row_spec(WA)
    full = lambda shape: pl.BlockSpec(shape, lambda b, i: (0,) * len(shape))
    big = jax.ShapeDtypeStruct((nb, s, WA), F32)
    row = jax.ShapeDtypeStruct((1, WA), F32)
    return pl.pallas_call(
        body, name="mix_mid_bwd", grid=(nb, s // tm),
        out_shape=[big, big, big, jax.ShapeDtypeStruct((CH, NH * CH), F32), jax.ShapeDtypeStruct((NH, CH), F32),
                   row, row, row, row, row, row, row],
        in_specs=[_tok_specs(tm, D), t5, t5, t5, r5, r5, full((CH, NH * CH)), full((NH * CH, CH)), full((CH, WA)),
                  r5, r5, r5, r5],
        out_specs=[t5, t5, t5, full((CH, NH * CH)), full((NH, CH)), r5, r5, r5, r5, r5, r5, r5],
        compiler_params=_cparams(),
    )(dy, u, v, conv, gn_g, gn_b, wcat, wcat_t, bspt, cn_g, cn_b, go_a, go_b)


def _mix_in_bwd(dxo, x, du, dv, dconv, a, g, sh, sc, g_pre, w_mi4, conv_w):
    nb, s, _ = x.shape
    tm = min(TM, s)
    n_i = s // tm

    def body(dxo_ref, x_ref, du_ref, dv_ref, dc_ref, dch_ref, a_ref, g_ref, ah_ref, gh_ref, sh_ref, sc_ref,
             gpre_ref, w_ref, cw_ref,
             dx_ref, dproj_ref, h_ref, dgpre_ref, dsh_ref, dsc_ref, dcw_ref, ext_ref, shf_ref, dglu_ref):
        b, i = pl.program_id(0), pl.program_id(1)
        first = _first(b, i)
        av, gv = a_ref[0], g_ref[0]
        sg = _sigmoid(gv)
        dconv = dc_ref[0]
        ext_ref[0:tm, :] = dconv
        ext_ref[tm:tm + HALO, :] = dch_ref[0] * jnp.where(i == n_i - 1, 0.0, 1.0).astype(F32)
        _make_shifts(ext_ref, shf_ref, tm)

        def put_dglu(r0, acc):
            dglu_ref[pl.ds(r0, TAP_ROWS), :] = acc

        _conv_taps(shf_ref, cw_ref, tm, [(CK - 1 - k, k) for k in range(CK)], put_dglu)
        dglu = dglu_ref[...]
        ext_ref[0:HALO, :] = (ah_ref[0] * _sigmoid(gh_ref[0])) * jnp.where(i == 0, 0.0, 1.0).astype(F32)
        ext_ref[HALO:HALO + tm, :] = av * sg
        _make_shifts(ext_ref, shf_ref, tm)

        @pl.when(first)
        def _():
            dcw_ref[...] = jnp.zeros((HALO, WB), F32)

        for k in range(CK):
            o = k + HALO - (CK - 1)
            lo = SHIFTS * (o // SHIFTS)
            dcw_ref[k:k + 1, :] += jnp.sum(dconv * shf_ref[o % SHIFTS, lo:lo + tm, :], axis=0, keepdims=True)
        da = dglu * sg
        dg = dglu * av * (sg * (1.0 - sg))
        parts = [du_ref[0].astype(BF16), dv_ref[0].astype(BF16), da.astype(BF16), dg.astype(BF16)]
        dh = jnp.zeros((tm, D), F32)
        for k in range(4):
            dproj_ref[0, :, k * WA:(k + 1) * WA] = parts[k]
            dh = dh + _dot_nt(parts[k], w_ref[k])
        n, r = _rms(x_ref[0])
        gpre = gpre_ref[...]
        ng = n * gpre
        scale1 = 1.0 + sc_ref[0]
        h_ref[0] = (ng * scale1 + sh_ref[0]).astype(BF16)
        dsh = jnp.sum(dh, axis=0, keepdims=True)
        dsc = jnp.sum(dh * ng, axis=0, keepdims=True)
        dxn, dgpre = _rms_bwd(dh * scale1, n, r, gpre)
        dx_ref[0] = dxo_ref[0] + dxn
        _acc(dgpre_ref, dgpre, first)
        _acc(dsh_ref, dsh[None], i == 0)
        _acc(dsc_ref, dsc[None], i == 0)

    tok = _tok_specs(tm, D)
    t5 = _tok_specs(tm, WA)
    full = lambda shape: pl.BlockSpec(shape, lambda b, i: (0,) * len(shape))
    mod_shape = jax.ShapeDtypeStruct((nb, 1, D), F32)
    return pl.pallas_call(
        body, name="mix_in_bwd", grid=(nb, n_i),
        out_shape=[jax.ShapeDtypeStruct((nb, s, D), F32), jax.ShapeDtypeStruct((nb, s, 4 * WA), BF16),
                   jax.ShapeDtypeStruct((nb, s, D), BF16), jax.ShapeDtypeStruct((1, D), F32), mod_shape, mod_shape,
                   jax.ShapeDtypeStruct((HALO, WB), F32)],
        in_specs=[tok, tok, t5, t5, t5, _halo_next_spec(tm, s), t5, t5, _halo_prev_spec(tm), _halo_prev_spec(tm),
                  _mod_spec(), _mod_spec(), _row_spec(), VMEM_FULL, full((HALO, WB))],
        out_specs=[tok, _tok_specs(tm, 4 * WA), tok, _row_spec(), _mod_spec(), _mod_spec(), full((HALO, WB))],
        scratch_shapes=[pltpu.VMEM((_ext_rows(tm), WB), F32), pltpu.VMEM((SHIFTS, tm + HALO, WB), F32),
                        pltpu.VMEM((tm, WB), F32)],
        compiler_params=_cparams(),
    )(dxo, x, du, dv, dconv, dconv, a, g, a, g, sh, sc, g_pre, w_mi4, conv_w)


def _row_tile(rows, cols):
    best = 16
    for t in range(16, rows + 1, 16):
        if rows % t == 0 and t * cols * 4 <= 1536 * 1024:
            best = t
    return best


def _walk(steps):
    offs = [sum(steps[:k]) for k in range(len(steps))]

    def tile(k):
        return lambda i: jnp.clip(i - offs[k], 0, steps[k] - 1)

    def mine(k, i):
        return jnp.logical_and(i >= offs[k], i < offs[k] + steps[k])

    return sum(steps), tile, mine


def _sum4(name, own4s, recvs, j_arr):
    n = len(own4s)
    shapes = [o.shape[1:] for o in own4s]
    trs = [_row_tile(r, c) for r, c in shapes]
    total, tile, mine = _walk([r // tr for (r, _), tr in zip(shapes, trs)])

    def body(j_ref, *refs):
        del j_ref
        i = pl.program_id(0)
        for k in range(n):
            own_ref, recv_ref, o_ref = refs[2 * k], refs[2 * k + 1], refs[2 * n + k]

            def add(own_ref=own_ref, recv_ref=recv_ref, o_ref=o_ref):
                acc = own_ref[0]
                for q in range(3):
                    acc = acc + recv_ref[q].astype(F32)
                o_ref[...] = acc

            pl.when(mine(k, i))(add)

    in_specs, out_specs = [], []
    for k, ((_, cols), tr) in enumerate(zip(shapes, trs)):
        in_specs += [pl.BlockSpec((1, tr, cols), lambda i, j, t=tile(k): (j[0], t(i), 0)),
                     pl.BlockSpec((3, tr, cols), lambda i, j, t=tile(k): (0, t(i), 0))]
        out_specs.append(pl.BlockSpec((tr, cols), lambda i, j, t=tile(k): (t(i), 0)))
    return pl.pallas_call(
        body, name=name,
        grid_spec=pltpu.PrefetchScalarGridSpec(num_scalar_prefetch=1, grid=(total,), in_specs=in_specs, out_specs=out_specs),
        out_shape=[jax.ShapeDtypeStruct(sh, F32) for sh in shapes],
        compiler_params=_cparams(),
    )(j_arr, *[a for pair in zip(own4s, recvs) for a in pair])


def _pair_plan(shapes):
    def plan(x, y, c, src, land):
        sends = []
        for a, shape in enumerate(shapes):
            rows = shape[1] // 2
            theirs = pl.ds(pl.multiple_of((1 - c) * rows, 16), rows)
            sends.append((src[a].at[:, theirs], land[a], (x, y, 1 - c), land[a]))
        return [], sends

    return plan


def _swap_plan(n):
    def plan(x, y, c, src, land):
        return [], [(src[a], land[a], (x, y, 1 - c), land[a]) for a in range(n)]

    return plan


def _pair_sum(name, g32s, recvs, c_arr):
    n = len(g32s)
    shapes = [r.shape for r in recvs]
    trs = [_row_tile(rows, cols) for _, rows, cols in shapes]
    nhs = [rows // tr for (_, rows, _), tr in zip(shapes, trs)]
    total, tile, mine = _walk([nblk * nh for (nblk, _, _), nh in zip(shapes, nhs)])

    def body(c_ref, *refs):
        del c_ref
        i = pl.program_id(0)
        for k in range(n):
            g_ref, r_ref, o32_ref, obf_ref = refs[2 * k], refs[2 * k + 1], refs[2 * n + 2 * k], refs[2 * n + 2 * k + 1]

            def add(g_ref=g_ref, r_ref=r_ref, o32_ref=o32_ref, obf_ref=obf_ref):
                val = g_ref[0] + r_ref[0].astype(F32)
                o32_ref[0] = val
                obf_ref[0] = val.astype(BF16)

            pl.when(mine(k, i))(add)

    in_specs, out_specs, out_shape = [], [], []
    for k, ((_, _, cols), tr, nh) in enumerate(zip(shapes, trs, nhs)):
        def half(tr=tr, cols=cols, t=tile(k), nh=nh):
            return pl.BlockSpec((1, tr, cols), lambda i, c: (t(i) // nh, t(i) % nh, 0))

        in_specs += [pl.BlockSpec((1, tr, cols), lambda i, c, t=tile(k), nh=nh: (t(i) // nh, c[0] * nh + t(i) % nh, 0)), half()]
        out_specs += [half(), half()]
        out_shape += [jax.ShapeDtypeStruct(shapes[k], F32), jax.ShapeDtypeStruct(shapes[k], BF16)]
    res = pl.pallas_call(
        body, name=name,
        grid_spec=pltpu.PrefetchScalarGridSpec(num_scalar_prefetch=1, grid=(total,), in_specs=in_specs, out_specs=out_specs),
        out_shape=out_shape,
        compiler_params=_cparams(),
    )(c_arr, *[a for pair in zip(g32s, recvs) for a in pair])
    return [(res[2 * k], res[2 * k + 1]) for k in range(n)]


def _adam_halves(name, w, m, v, mine, theirs, c_arr):
    rows, cols = w.shape
    tr = _row_tile(rows // 2, cols)
    nh = (rows // 2) // tr

    def body(c_ref, w_ref, m_ref, v_ref, mine_ref, theirs_ref, g_out, d_out, m_out, v_out):
        here = (pl.program_id(0) // nh) == c_ref[0]
        g = jnp.where(here, mine_ref[...], theirs_ref[...])
        delta, m2, v2 = _adam(w_ref[...], g, m_ref[...], v_ref[...])
        g_out[...] = g
        d_out[...] = delta
        m_out[...] = m2
        v_out[...] = v2

    spec = pl.BlockSpec((tr, cols), lambda i, c: (i, 0))
    shape = jax.ShapeDtypeStruct((rows, cols), F32)
    return pl.pallas_call(
        body, name=name,
        grid_spec=pltpu.PrefetchScalarGridSpec(
            num_scalar_prefetch=1, grid=(2 * nh,),
            in_specs=[spec, spec, spec,
                      pl.BlockSpec((tr, cols), lambda i, c: (jnp.clip(i - c[0] * nh, 0, nh - 1), 0)),
                      pl.BlockSpec((tr, cols), lambda i, c: (jnp.clip(i - (1 - c[0]) * nh, 0, nh - 1), 0))],
            out_specs=[spec] * 4),
        out_shape=[shape] * 4,
        compiler_params=_cparams(),
    )(c_arr, w, m, v, mine, theirs)


def _adam_big(name, w, m, v, ga, gb):
    rows, cols = w.shape
    tr = _row_tile(rows, cols)

    def body(w_ref, m_ref, v_ref, ga_ref, gb_ref, g_out, d_out, m_out, v_out):
        gsum = ga_ref[...] + gb_ref[...]
        delta, m2, v2 = _adam(w_ref[...], gsum, m_ref[...], v_ref[...])
        g_out[...] = gsum
        d_out[...] = delta
        m_out[...] = m2
        v_out[...] = v2

    spec = pl.BlockSpec((tr, cols), lambda i: (i, 0))
    shape = jax.ShapeDtypeStruct((rows, cols), F32)
    return pl.pallas_call(
        body, name=name, grid=(rows // tr,), out_shape=[shape] * 4,
        in_specs=[spec] * 5, out_specs=[spec] * 4, compiler_params=_cparams(),
    )(w, m, v, ga, gb)


PK_VEC = 0
PK_LOSS = 6
PK_PAIR = 8
PK_BSP = 16
PK_WCAT = 24
PK_ROWS = PK_WCAT + CH
PAIR_ORDER = ("gmlp_norm_g", "gmlp_norm_b", "conv_b", "conv_norm_g", "conv_norm_b", "g_out_a", "g_out_b")
VEC_ORDER = ("g_pre_f1", "g_post_f1", "g_pre_m", "g_post_m", "g_pre_f2", "g_post_f2")


def _pack_late(rows):
    counts = [r.shape[0] for r in rows]
    assert sum(counts) == 8

    def body(*refs):
        o_ref = refs[-1]
        at = 0
        for r, cnt in zip(refs[:-1], counts):
            o_ref[at:at + cnt, :] = r[...]
            at += cnt

    return pl.pallas_call(
        body, name="pack_late", out_shape=jax.ShapeDtypeStruct((8, D), F32),
        in_specs=[VMEM_FULL] * len(rows), out_specs=VMEM_FULL, compiler_params=_cparams(),
    )(*rows)


def _pack_small(vecs, pairs, dbsp, dwcat, lsum):
    def body(*refs):
        vec_refs = refs[:4]
        pair_refs = refs[4:11]
        dbsp_ref, dwcat_ref, lsum_ref, o_ref = refs[11:]
        o_ref[0:PK_WCAT, :] = jnp.zeros((PK_WCAT, D), F32)
        o_ref[PK_LOSS:PK_LOSS + 1, 0:128] = lsum_ref[...]
        for k, r in enumerate(vec_refs):
            o_ref[PK_VEC + 2 + k:PK_VEC + 3 + k, :] = r[...]
        for k, r in enumerate(pair_refs):
            row, half = PK_PAIR + k // 2, k % 2
            o_ref[row:row + 1, half * WA:(half + 1) * WA] = r[...]
        o_ref[PK_BSP:PK_BSP + NH, 0:CH] = dbsp_ref[...]
        o_ref[PK_WCAT:PK_ROWS, :] = dwcat_ref[...]

    args = list(vecs) + list(pairs) + [dbsp, dwcat, lsum]
    return pl.pallas_call(
        body, name="pack_small", out_shape=jax.ShapeDtypeStruct((PK_ROWS, D), F32),
        in_specs=[VMEM_FULL] * len(args), out_specs=VMEM_FULL, compiler_params=_cparams(),
    )(*args)


def _small_adam(pack_all, late_all, dcw_all, dada_all, params, behind):
    names = list(VEC_ORDER) + list(PAIR_ORDER) + ["b_spatial", "w_spatial", "conv_w", "b_ada"]
    flat = []
    for nm in names:
        flat += list(params[nm])
    n_in = 4 + len(flat)

    def body(*refs):
        pack_ref, late_ref, dcw_ref, dada_ref = refs[:4]
        prm = refs[4:n_in]
        outs = refs[n_in + 1:]

        def total(r0, nr, c0, nc):
            acc = pack_ref[0, r0:r0 + nr, c0:c0 + nc]
            for d in range(1, NDEV):
                acc = acc + pack_ref[d, r0:r0 + nr, c0:c0 + nc]
            return acc

        def emit(idx, g, getw, put):
            w_ref, m_ref, v_ref = prm[3 * idx:3 * idx + 3]
            delta, m2, v2 = _adam(getw(w_ref), g, getw(m_ref), getw(v_ref))
            for o_ref, val in zip(outs[4 * idx:4 * idx + 4], (g, delta, m2, v2)):
                put(o_ref, val)

        def whole(ref):
            return ref[...]

        def put_whole(ref, val):
            ref[...] = val

        idx = 0
        for k in range(6):
            if k < 2:
                g = late_ref[0, k:k + 1, :]
                for d in range(1, NDEV):
                    g = g + late_ref[d, k:k + 1, :]
            else:
                g = total(PK_VEC + k, 1, 0, D)
            emit(idx, g, whole, put_whole)
            idx += 1
        for k in range(7):
            emit(idx, total(PK_PAIR + k // 2, 1, (k % 2) * WA, WA), whole, put_whole)
            idx += 1
        emit(idx, total(PK_BSP, NH, 0, CH), lambda r: r[0], lambda r, val: r.__setitem__(0, val))
        idx += 1
        row = lax.broadcasted_iota(jnp.int32, (CH, CH), 0)
        col = lax.broadcasted_iota(jnp.int32, (CH, CH), 1)
        for h in range(NH):
            gh = jnp.where(col <= row, total(PK_WCAT, CH, h * CH, CH), 0.0)
            w_ref, m_ref, v_ref = prm[3 * idx:3 * idx + 3]
            delta, m2, v2 = _adam(w_ref[0, h], gh, m_ref[0, h], v_ref[0, h])
            for o_ref, val in zip(outs[4 * idx:4 * idx + 4], (gh, delta, m2, v2)):
                o_ref[0, h] = val
        idx += 1
        gcw = dcw_ref[0, 0:CK, :]
        for d in range(1, NDEV):
            gcw = gcw + dcw_ref[d, 0:CK, :]
        emit(idx, gcw, lambda r: r[0], lambda r, val: r.__setitem__(0, val))
        idx += 1
        emit(idx, jnp.sum(dada_ref[...], axis=0, keepdims=True), whole, put_whole)
        outs[-1][...] = jnp.sum(total(PK_LOSS, 1, 0, 128), axis=1, keepdims=True) * (0.5 / D)

    out_shape = []
    for nm in names:
        w = params[nm][0]
        out_shape += [jax.ShapeDtypeStruct(w.shape, F32)] * 4
    out_shape.append(jax.ShapeDtypeStruct((1, 1), F32))
    res = pl.pallas_call(
        body, name="small_adam", out_shape=out_shape,
        in_specs=[VMEM_FULL] * n_in + [ANY], out_specs=[VMEM_FULL] * len(out_shape), compiler_params=_cparams(),
    )(pack_all, late_all, dcw_all, dada_all, *flat, behind)
    return {nm: tuple(res[4 * k:4 * k + 4]) for k, nm in enumerate(names)}, res[-1].reshape(())


WEIGHTS = ['w_ada', 'b_ada', 'g_pre_f1', 'g_post_f1', 'w_f1_in', 'w_f1_out', 'g_pre_m', 'g_post_m', 'w_mix_in',
           'gmlp_norm_g', 'gmlp_norm_b', 'w_spatial', 'b_spatial', 'conv_w', 'conv_b', 'conv_norm_g', 'conv_norm_b',
           'g_out_a', 'g_out_b', 'w_mix_out', 'g_pre_f2', 'g_post_f2', 'w_f2_in', 'w_f2_out']


def kernel(x, c, w_ada, b_ada, g_pre_f1, g_post_f1, w_f1_in, w_f1_out, g_pre_m, g_post_m, w_mix_in, gmlp_norm_g, gmlp_norm_b, w_spatial, b_spatial, conv_w, conv_b, conv_norm_g, conv_norm_b, g_out_a, g_out_b, w_mix_out, g_pre_f2, g_post_f2, w_f2_in, w_f2_out, loss_target, m_w_ada, m_b_ada, m_g_pre_f1, m_g_post_f1, m_w_f1_in, m_w_f1_out, m_g_pre_m, m_g_post_m, m_w_mix_in, m_gmlp_norm_g, m_gmlp_norm_b, m_w_spatial, m_b_spatial, m_conv_w, m_conv_b, m_conv_norm_g, m_conv_norm_b, m_g_out_a, m_g_out_b, m_w_mix_out, m_g_pre_f2, m_g_post_f2, m_w_f2_in, m_w_f2_out, v_w_ada, v_b_ada, v_g_pre_f1, v_g_post_f1, v_w_f1_in, v_w_f1_out, v_g_pre_m, v_g_post_m, v_w_mix_in, v_gmlp_norm_g, v_gmlp_norm_b, v_w_spatial, v_b_spatial, v_conv_w, v_conv_b, v_conv_norm_g, v_conv_norm_b, v_g_out_a, v_g_out_b, v_w_mix_out, v_g_pre_f2, v_g_post_f2, v_w_f2_in, v_w_f2_out):
    env = dict(locals())
    wts = {n: env[n] for n in WEIGHTS}
    mom = {n: env["m_" + n] for n in WEIGHTS}
    var = {n: env["v_" + n] for n in WEIGHTS}
    nb, s, _ = x.shape
    t = nb * s
    ax, ay, ac = lax.axis_index("x"), lax.axis_index("y"), lax.axis_index("c")
    j_chip = 2 * ax + ay
    dev = 4 * ax + 2 * ay + ac
    j_arr = j_chip.reshape(1).astype(jnp.int32)

    groups = (("w_f1_in",), ("w_mix_in", "w_mix_out"), ("w_f2_in", "w_f2_out"), ("w_f1_out",))
    def gather_operands(gi):
        srcs = [wts[n][0].astype(BF16) for n in groups[gi]] + ([conv_w[0]] if gi == 1 else [])
        lands = [lax.dynamic_update_index_in_dim(lax.empty((NCHIP,) + a.shape, a.dtype), a, j_chip, 0) for a in srcs]
        return srcs, lands

    def gather_start(gi, behind, operands=None):
        srcs, lands = operands or gather_operands(gi)
        plan_a, plan_b, n_b = _gather_plans([a.shape for a in srcs])
        ssem, rsem, srcs, lands, token = _split_start("gw_start%d" % gi, srcs, lands, plan_a, 3 * len(srcs), behind)
        gather[gi] = (srcs, lands, ssem, rsem, plan_a, plan_b, n_b)
        return token

    def gather_forward(gi, behind):
        srcs, lands, ssem, rsem, plan_a, plan_b, n_b = gather[gi]
        ssem, rsem, lands, token = _split_forward("gw_fwd%d" % gi, srcs, lands, ssem, rsem, plan_a, plan_b, n_b, behind)
        gather[gi] = (lands, ssem, rsem, plan_b)
        return token

    def gathered(gi, behind):
        lands, ssem, rsem, plan_b = gather[gi]
        return _split_wait("gw_wait%d" % gi, [], lands, ssem, rsem, plan_b, behind)

    gather = {}
    def allgather_start(tag, arrs, behind):
        lands = [lax.dynamic_update_index_in_dim(lax.empty((NDEV,) + a.shape, a.dtype), a, dev, 0) for a in arrs]
        ssem, rsem, srcs, lands, token = _split_start("small_start_" + tag, arrs, lands, _allgather_plan(len(arrs)),
                                                      7 * len(arrs), behind)
        return (srcs, lands, ssem, rsem), token

    def allgather_wait(tag, state, behind):
        srcs, lands, ssem, rsem = state
        return _split_wait("small_wait_" + tag, srcs, lands, ssem, rsem, _allgather_plan(len(srcs)), behind)

    c_state, token = allgather_start("c", [c.reshape(8, (nb * D) // 8)], c)
    token = gather_start(0, token)
    (c_all8,) = allgather_wait("c", c_state, token)
    c_all = c_all8.reshape(NDEV * nb, D)
    b_sh = lax.dynamic_slice(b_ada, (0, j_chip * ADA_SH), (1, ADA_SH))
    ada_sh = _ada_fwd(c_all, w_ada[0], b_sh)
    later = [gather_operands(3), gather_operands(1), gather_operands(2)]
    (ada4,) = _chip_allgather("gather_ada", [ada_sh], behind=[a for pair in later for arrs in pair for a in arrs])
    token = gather_forward(0, ada4)
    plans = [_gather_plans([a.shape for a in srcs]) for srcs, _ in later]
    started, token = _split_start_groups(
        "gw_start_later", [(srcs, lands, pa, 3 * len(srcs)) for (srcs, lands), (pa, _, _) in zip(later, plans)], token)
    for gi, (ssem, rsem, srcs, lands), (pa, pb, n_b) in zip((3, 1, 2), started, plans):
        gather[gi] = (srcs, lands, ssem, rsem, pa, pb, n_b)
    ada_me = lax.dynamic_slice(ada4, (0, dev * nb, 0), (NCHIP, nb, ADA_SH))
    ada_me = jnp.transpose(ada_me, (1, 0, 2)).reshape(nb, NMOD * D)
    sh1, sc1, gt1, sh2, sc2, gt2, sh3, sc3, gt3 = [ada_me[:, k * D:(k + 1) * D].reshape(nb, 1, D) for k in range(NMOD)]

    wcat = jnp.transpose(w_spatial[0], (1, 0, 2)).reshape(CH, NH * CH)
    wcat_t = jnp.transpose(w_spatial[0], (0, 2, 1)).reshape(NH * CH, CH)
    bspt = jnp.repeat(b_spatial[0].T, HD, axis=1)

    (w1i,) = gathered(0, token)
    p1, act1 = _ffn_up(x, sh1, sc1, g_pre_f1, w1i)
    forwarded, token = _split_forward_groups("gw_fwd_f1_out_mix", [gather[3], gather[1]], act1)
    for gi, (ssem, rsem, lands) in zip((3, 1), forwarded):
        gather[gi] = (lands, ssem, rsem, gather[gi][5])
    (w1o,) = gathered(3, token)
    w1o = w1o.reshape(DFF, D)
    x1, f1 = _ffn_down(x, act1, gt1, g_post_f1, w1o)
    wmi, wmo, cw4 = gathered(1, x1)
    wmo = wmo.reshape(D, D)
    cw_full = jnp.transpose(cw4, (1, 0, 2)).reshape(CK, WB)
    cw_pad = jnp.pad(cw_full, ((0, HALO - CK), (0, 0)))
    u, v, a, g = _mix_in_fwd(x1, sh2, sc2, g_pre_m, wmi)
    token = gather_forward(2, u)
    x2, conv, yb, m = _mix_mid_fwd(x1, u, v, a, g, gt2 + token[0, 0], gmlp_norm_g, gmlp_norm_b, wcat, bspt, cw_pad, conv_b,
                                   conv_norm_g, conv_norm_b, g_out_a, g_out_b, wmo, g_post_m)
    w2i, w2o = gathered(2, [x2, token])
    w2o = w2o.reshape(DFF, D)
    dx3, df2, p2, lsum, dg_post_f2, dgt3 = _ffn_loss_fwd(x2, sh3, sc3, gt3, g_pre_f2, g_post_f2, w2i, w2o, loss_target)

    def chip4(pair, rows):
        return [arr.reshape(NCHIP, rows, arr.shape[-1]) for arr in pair]

    def scatter_start(tag, pairs, behind):
        srcs = [p[1] for p in pairs]
        lands = [lax.empty((3,) + a.shape[1:], a.dtype) for a in srcs]
        ssem, rsem, srcs, lands, token = _split_start("gs_start_" + tag, srcs, lands, _scatter_plan(len(srcs)),
                                                      3 * len(srcs), behind)
        return (srcs, lands, ssem, rsem), token

    def scatter_wait(tag, state, behind):
        srcs, lands, ssem, rsem = state
        return _split_wait("gs_wait_" + tag, srcs, lands, ssem, rsem, _scatter_plan(len(srcs)), behind)

    out = {}
    dx2, dp2, h3, a2, dg_pre_f2, dsh3, dsc3 = _ffn_bwd(
        dx3, x2, None, p2, sh3, sc3, gt3, g_pre_f2, g_post_f2, w2i, w2o, df=df2)
    gw2i = _wgrad("wgrad_f2_in", h3.reshape(t, D), dp2.reshape(t, 2 * DFF), 2 * DFF // NCHIP, True)
    gw2o = chip4(_wgrad("wgrad_f2_out", a2.reshape(t, DFF), df2.reshape(t, D), D // 2, False), DFF // NCHIP)
    scat_f2, tok = scatter_start("f2", [gw2i, gw2o], dg_post_f2)
    dy, dm, dg_post_m, dgt2 = _mix_out_bwd(dx2, m, gt2 + tok[0, 0], g_post_m, wmo)
    gwmo = chip4(_wgrad("wgrad_mix_out", yb.reshape(t, D), dm.reshape(t, D), D // 2, False), D // NCHIP)
    (du, dv, dconv, dwcat, dbsp, dgn_g, dgn_b, dgo_a, dgo_b, dcn_g, dcn_b, dcb) = _mix_mid_bwd(
        dy, u, v, conv, gmlp_norm_g, gmlp_norm_b, wcat, wcat_t, bspt, conv_norm_g, conv_norm_b, g_out_a, g_out_b)
    dx1, dproj, h2, dg_pre_m, dsh2, dsc2, dcw = _mix_in_bwd(dx2, x1, du, dv, dconv, a, g, sh2, sc2, g_pre_m, wmi, cw_pad)
    gwmi = _wgrad("wgrad_mix_in", h2.reshape(t, D), dproj.reshape(t, 4 * WA), WA, True)

    vec_grads = dict(g_pre_m=dg_pre_m, g_post_m=dg_post_m, g_pre_f2=dg_pre_f2, g_post_f2=dg_post_f2)
    pair_grads = dict(gmlp_norm_g=dgn_g, gmlp_norm_b=dgn_b, conv_b=dcb, conv_norm_g=dcn_g, conv_norm_b=dcn_b,
                      g_out_a=dgo_a, g_out_b=dgo_b)
    pack = _pack_small([vec_grads[n] for n in VEC_ORDER[2:]], [pair_grads[n] for n in PAIR_ORDER], dbsp, dwcat, lsum)
    dada_early = jnp.concatenate([q.reshape(nb, D) for q in (dsh2, dsc2, dgt2, dsh3, dsc3, dgt3)], axis=1)
    small_early = [pack, dcw, dada_early.reshape(8, (nb * 6 * D) // 8)]
    mix_bf16 = [gwmi[1], gwmo[1]]
    (s_mix, s_early), tok2 = _split_start_groups("gs_start_mix_small", [
        (mix_bf16, [lax.empty((3,) + a.shape[1:], a.dtype) for a in mix_bf16], _scatter_plan(2), 6),
        (small_early, [lax.dynamic_update_index_in_dim(lax.empty((NDEV,) + a.shape, a.dtype), a, dev, 0) for a in small_early],
         _allgather_plan(3), 21)], dg_pre_m)
    scat_mix = (s_mix[2], s_mix[3], s_mix[0], s_mix[1])
    early = (s_early[2], s_early[3], s_early[0], s_early[1])
    grad_x, dp1, h1, a1, df1, dg_pre_f1, dg_post_f1, dsh1, dsc1, dgt1 = _ffn_bwd(
        dx1, x, f1, p1, sh1 + tok2[0, 0], sc1, gt1, g_pre_f1, g_post_f1, w1i, w1o)
    late_pack = _pack_late([dg_pre_f1, dg_post_f1] + [q.reshape(nb, D) for q in (dsh1, dsc1, dgt1)])
    late, tok2 = allgather_start("late", [late_pack], dg_post_f1)
    gw1i = _wgrad("wgrad_f1_in", h1.reshape(t, D), dp1.reshape(t, 2 * DFF), 2 * DFF // NCHIP, True)
    gw1o = chip4(_wgrad("wgrad_f1_out", a1.reshape(t, DFF), df1.reshape(t, D), D // 2, False), DFF // NCHIP)
    def d2d_start(tag, srcs, lands, plan, behind):
        ssem, rsem, srcs, lands, token = _split_start("d2d_start_" + tag, srcs, lands, plan, len(srcs), behind)
        return (srcs, lands, ssem, rsem, plan), token

    def d2d_wait(tag, state, behind):
        srcs, lands, ssem, rsem, plan = state
        return _split_wait("d2d_wait_" + tag, srcs, lands, ssem, rsem, plan, behind)

    def swap_start(tag, parts, behind):
        return d2d_start(tag, parts, [lax.empty(a.shape, a.dtype) for a in parts], _swap_plan(len(parts)), behind)

    def sums(names, pairs, recv):
        return _sum4("sum4_" + names[0][2:4], [p[0] for p in pairs], recv, j_arr)

    def update(names, part, other):
        for k, n in enumerate(names):
            out[n] = tuple(r[None] for r in _adam_big("adam_" + n, wts[n][0], mom[n][0], var[n][0], part[k], other[k]))

    c_arr = ac.reshape(1).astype(jnp.int32)
    halves = [gw1i[1], gw1o[1]]
    pair_st, tok = d2d_start("pair", halves, [lax.empty((a.shape[0], a.shape[1] // 2, a.shape[2]), a.dtype) for a in halves],
                             _pair_plan([a.shape for a in halves]), tok2)
    names_f2, names_mix, names_f1 = ("w_f2_in", "w_f2_out"), ("w_mix_in", "w_mix_out"), ("w_f1_in", "w_f1_out")
    part_f2 = sums(names_f2, [gw2i, gw2o], scatter_wait("f2", scat_f2, tok))
    sib = d2d_wait("pair", pair_st, part_f2)
    pair_i, pair_o = _pair_sum("pairsum_f1", [gw1i[0], gw1o[0]], sib, c_arr)
    f1_bf16 = [pair_i[1], pair_o[1]]
    (s_f1, s_sw), tok = _split_start_groups("gs_start_f1_swap_f2", [
        (f1_bf16, [lax.empty((3,) + a.shape[1:], a.dtype) for a in f1_bf16], _scatter_plan(2), 6),
        (part_f2, [lax.empty(a.shape, a.dtype) for a in part_f2], _swap_plan(2), 2)], tok2)
    scat_f1 = (s_f1[2], s_f1[3], s_f1[0], s_f1[1])
    swap_f2 = (s_sw[2], s_sw[3], s_sw[0], s_sw[1], _swap_plan(2))
    part_mix = sums(names_mix, [gwmi, gwmo], scatter_wait("mix", scat_mix, tok))
    swap_mix, tok = swap_start("swap_mix", part_mix, part_mix[1])

    pack_all, dcw_all, dada_early8 = allgather_wait("early", early, tok)
    (late_all,) = allgather_wait("late", late, pack_all)
    dada_late = jnp.transpose(late_all[:, 2:8, :].reshape(NDEV, 3, nb, D), (0, 2, 1, 3)).reshape(NDEV * nb, 3 * D)
    dada_all = jnp.concatenate([dada_late, dada_early8.reshape(NDEV * nb, 6 * D)], axis=1)
    dada_sh = lax.dynamic_slice(dada_all, (0, j_chip * ADA_SH), (NDEV * nb, ADA_SH))
    out["w_ada"] = tuple(r[None] for r in _ada_bwd_adam(c_all, dada_sh, w_ada[0], m_w_ada[0], v_w_ada[0]))
    update(names_f2, part_f2, d2d_wait("swap_f2", swap_f2, out["w_ada"][3]))
    update(names_mix, part_mix, d2d_wait("swap_mix", swap_mix, out["w_f2_out"][3]))

    mine = sums(names_f1, [pair_i, pair_o], scatter_wait("f1", scat_f1, out["w_mix_out"][3]))
    swap_f1, tok = swap_start("swap_f1", mine, mine[1])
    dcw_mine = lax.dynamic_slice(dcw_all, (0, 0, j_chip * (WB // NCHIP)), (NDEV, HALO, WB // NCHIP))
    small = {n: (wts[n], mom[n], var[n]) for n in list(VEC_ORDER) + list(PAIR_ORDER) + ["b_spatial", "w_spatial", "conv_w", "b_ada"]}
    small_out, loss = _small_adam(pack_all, late_all, dcw_mine, dada_all, small, tok)
    out.update(small_out)
    theirs = d2d_wait("swap_f1", swap_f1, out["b_ada"][3])
    for k, n in enumerate(names_f1):
        out[n] = tuple(r[None] for r in _adam_halves("adam_" + n, wts[n][0], mom[n][0], var[n][0], mine[k], theirs[k],
                                                     c_arr))

    res = [loss, grad_x]
    for k in range(4):
        res += [out[n][k] for n in WEIGHTS]
    return tuple(res)
```

```python
import jax
import jax.numpy as jnp
from jax import lax
from jax.experimental import pallas as pl
from jax.experimental.pallas import tpu as pltpu

D = 1024
DFF = 2816
WA = 512
WB = 512
NH = 8
HD = 64
CH = 128
CK = 31
HALO = 32
NMOD = 9
EPS = 1e-6
NCHIP = 4
NDEV = 8
FBLK = DFF // 2
ADA_SH = NMOD * D // NCHIP

LR, B1, B2, EPS_A, WD, STEP = 0.001, 0.9, 0.999, 1e-08, 0.01, 10

F32 = jnp.float32
BF16 = jnp.bfloat16
MESH = pl.DeviceIdType.MESH
ANY = pl.BlockSpec(memory_space=pl.ANY)
VMEM_FULL = pl.BlockSpec(memory_space=pltpu.VMEM)
VMEM_LIMIT = 56 * 1024 * 1024
WGRAD_VMEM_BUDGET = 52 * 1024 * 1024
TM = 512
TM_FFN_BWD = 256

NT = (((1,), (1,)), ((), ()))
TN = (((0,), (0,)), ((), ()))


def _dot(a, b):
    return jnp.dot(a, b, preferred_element_type=F32)


def _dot_nt(a, b):
    return lax.dot_general(a, b, NT, preferred_element_type=F32)


def _dot_tn(a, b):
    return lax.dot_general(a, b, TN, preferred_element_type=F32)


def _cparams():
    return pltpu.CompilerParams(vmem_limit_bytes=VMEM_LIMIT)


def _chip_relations(x, y):
    return [(1 - x, y), (x, 1 - y), (1 - x, 1 - y)]


def _exchange(name, arrs, out_shapes, plan):
    n = len(arrs)
    n_out = len(out_shapes)

    def body(*refs):
        ins, outs = refs[:n], refs[n:n + n_out]
        send_sems, recv_sems, local_sems = refs[n + n_out:]
        x, y, c = lax.axis_index("x"), lax.axis_index("y"), lax.axis_index("c")
        local, sends = plan(x, y, c, ins, outs)
        locs = [pltpu.make_async_copy(s, d, local_sems.at[i]) for i, (s, d) in enumerate(local)]
        for loc in locs:
            loc.start()
        cps = [pltpu.make_async_remote_copy(src_ref=s, dst_ref=d, send_sem=send_sems.at[i], recv_sem=recv_sems.at[i],
                                            device_id=peer, device_id_type=MESH)
               for i, (s, d, peer, _) in enumerate(sends)]
        for cp in cps:
            cp.start()
        for i, (s, _, peer, landing) in enumerate(sends):
            pltpu.make_async_remote_copy(src_ref=s, dst_ref=landing, send_sem=send_sems.at[i], recv_sem=recv_sems.at[i],
                                         device_id=peer, device_id_type=MESH).wait_recv()
        for cp in cps:
            cp.wait_send()
        for loc in locs:
            loc.wait()

    return n, n_out, body


def _run_exchange(name, arrs, out_shapes, plan, n_local, n_send):
    n, n_out, body = _exchange(name, arrs, out_shapes, plan)
    return pl.pallas_call(
        body, name=name, out_shape=out_shapes,
        in_specs=[ANY] * n, out_specs=[ANY] * n_out,
        scratch_shapes=[pltpu.SemaphoreType.DMA((n_send,)), pltpu.SemaphoreType.DMA((n_send,)),
                        pltpu.SemaphoreType.DMA((max(n_local, 1),))],
    )(*arrs)


def _chip_allgather(name, arrs, behind=()):
    n = len(arrs)

    def plan(x, y, c, ins, outs):
        j_me = 2 * x + y
        local = [(ins[a], outs[a].at[j_me]) for a in range(n)]
        sends = []
        for a in range(n):
            for (px, py) in _chip_relations(x, y):
                sends.append((ins[a], outs[a].at[j_me], (px, py, c), outs[a].at[2 * px + py]))
        return local, sends

    shapes = [jax.ShapeDtypeStruct((NCHIP,) + a.shape, a.dtype) for a in arrs]
    return _run_exchange(name, list(arrs) + list(behind), shapes, plan, n, 3 * n)


HBM = pl.BlockSpec(memory_space=pltpu.HBM)
SEM = pl.BlockSpec(memory_space=pltpu.SEMAPHORE)
EFFECT = pltpu.SideEffectType.DATAFLOW_SIDE_EFFECTING


def _split_start_groups(name, groups, after):
    n_src = [len(g[0]) for g in groups]
    n_land = [len(g[1]) for g in groups]
    all_srcs = [pltpu.with_memory_space_constraint(a, pltpu.HBM) for g in groups for a in g[0]]
    all_lands = [pltpu.with_memory_space_constraint(a, pltpu.HBM) for g in groups for a in g[1]]
    ns, nl, ng = len(all_srcs), len(all_lands), len(groups)

    def body(*refs):
        src_refs, land_refs = refs[:ns], refs[ns:ns + nl]
        sem_refs = refs[ns + nl + 1:ns + nl + 1 + 2 * ng]
        token = refs[-1]
        x, y, c = lax.axis_index("x"), lax.axis_index("y"), lax.axis_index("c")
        at_src = at_land = 0
        for gi, (_, _, plan, _) in enumerate(groups):
            _, sends = plan(x, y, c, src_refs[at_src:at_src + n_src[gi]], land_refs[at_land:at_land + n_land[gi]])
            for i, (s, d, peer, _) in enumerate(sends):
                pltpu.make_async_remote_copy(src_ref=s, dst_ref=d, send_sem=sem_refs[2 * gi].at[i],
                                             recv_sem=sem_refs[2 * gi + 1].at[i], device_id=peer, device_id_type=MESH).start()
            at_src += n_src[gi]
            at_land += n_land[gi]
        token[...] = jnp.zeros_like(token)

    sems = [pltpu.SemaphoreType.DMA((g[3],)) for g in groups for _ in range(2)]
    res = pl.pallas_call(
        body, name=name,
        out_shape=(*sems, *[pltpu.HBM(a.shape, a.dtype) for a in all_lands], jax.ShapeDtypeStruct((8, 128), F32)),
        in_specs=[HBM] * (ns + nl) + [ANY],
        out_specs=(*([SEM] * (2 * ng)), *([HBM] * nl), pl.BlockSpec(memory_space=pltpu.VMEM)),
        input_output_aliases={ns + i: 2 * ng + i for i in range(nl)},
        compiler_params=pltpu.CompilerParams(has_side_effects=EFFECT),
    )(*all_srcs, *all_lands, after)
    out, at_src, at_land = [], 0, 2 * ng
    for gi in range(ng):
        out.append((res[2 * gi], res[2 * gi + 1], all_srcs[at_src:at_src + n_src[gi]],
                    list(res[at_land:at_land + n_land[gi]])))
        at_src += n_src[gi]
        at_land += n_land[gi]
    return out, res[-1]


def _split_start(name, srcs, lands, plan, n_send, after):
    (group,), token = _split_start_groups(name, [(srcs, lands, plan, n_send)], after)
    return (*group, token)


def _split_wait(name, srcs, lands, send_sems, recv_sems, plan, after):
    n, nl = len(srcs), len(lands)
    afters = list(after) if isinstance(after, (list, tuple)) else [after]

    def body(*refs):
        src, land = refs[:n], refs[n:n + nl]
        send_sems, recv_sems = refs[n + nl], refs[n + nl + 1]
        x, y, c = lax.axis_index("x"), lax.axis_index("y"), lax.axis_index("c")
        _, sends = plan(x, y, c, src, land)
        for i, (s, _, peer, landing) in enumerate(sends):
            cp = pltpu.make_async_remote_copy(src_ref=s, dst_ref=landing, send_sem=send_sems.at[i],
                                              recv_sem=recv_sems.at[i], device_id=peer, device_id_type=MESH)
            cp.wait_send()
            cp.wait_recv()

    thru = [pltpu.HBM(a.shape, a.dtype) for a in lands]
    res = pl.pallas_call(
        body, name=name, out_shape=tuple(thru),
        in_specs=[HBM] * (n + nl) + [SEM, SEM] + [ANY] * len(afters), out_specs=tuple([HBM] * nl),
        input_output_aliases={n + i: i for i in range(nl)},
        compiler_params=pltpu.CompilerParams(has_side_effects=EFFECT),
    )(*srcs, *lands, send_sems, recv_sems, *afters)
    return list(res)


def _split_forward_groups(name, groups, after):
    n_src = [len(g[0]) for g in groups]
    n_land = [len(g[1]) for g in groups]
    all_srcs = [a for g in groups for a in g[0]]
    all_lands = [a for g in groups for a in g[1]]
    ns, nl, ng = len(all_srcs), len(all_lands), len(groups)

    def body(*refs):
        src_refs, land_refs = refs[:ns], refs[ns:ns + nl]
        sems_a = refs[ns + nl:ns + nl + 2 * ng]
        sems_b = refs[ns + nl + 2 * ng + 1:ns + nl + 4 * ng + 1]
        token = refs[-1]
        x, y, c = lax.axis_index("x"), lax.axis_index("y"), lax.axis_index("c")
        at_src = at_land = 0
        for gi, g in enumerate(groups):
            src, land = src_refs[at_src:at_src + n_src[gi]], land_refs[at_land:at_land + n_land[gi]]
            plan_a, plan_b = g[4], g[5]
            _, first = plan_a(x, y, c, src, land)
            for i, (s, _, peer, landing) in enumerate(first):
                cp = pltpu.make_async_remote_copy(src_ref=s, dst_ref=landing, send_sem=sems_a[2 * gi].at[i],
                                                  recv_sem=sems_a[2 * gi + 1].at[i], device_id=peer, device_id_type=MESH)
                cp.wait_send()
                cp.wait_recv()
            _, second = plan_b(x, y, c, src, land)
            for i, (s, d, peer, _) in enumerate(second):
                pltpu.make_async_remote_copy(src_ref=s, dst_ref=d, send_sem=sems_b[2 * gi].at[i],
                                             recv_sem=sems_b[2 * gi + 1].at[i], device_id=peer, device_id_type=MESH).start()
            at_src += n_src[gi]
            at_land += n_land[gi]
        token[...] = jnp.zeros_like(token)

    sems = [pltpu.SemaphoreType.DMA((g[6],)) for g in groups for _ in range(2)]
    res = pl.pallas_call(
        body, name=name,
        out_shape=(*sems, *[pltpu.HBM(a.shape, a.dtype) for a in all_lands], jax.ShapeDtypeStruct((8, 128), F32)),
        in_specs=[HBM] * (ns + nl) + [SEM] * (2 * ng) + [ANY],
        out_specs=(*([SEM] * (2 * ng)), *([HBM] * nl), pl.BlockSpec(memory_space=pltpu.VMEM)),
        input_output_aliases={ns + i: 2 * ng + i for i in range(nl)},
        compiler_params=pltpu.CompilerParams(has_side_effects=EFFECT),
    )(*all_srcs, *all_lands, *[s for g in groups for s in (g[2], g[3])], after)
    out, at_land = [], 2 * ng
    for gi in range(ng):
        out.append((res[2 * gi], res[2 * gi + 1], list(res[at_land:at_land + n_land[gi]])))
        at_land += n_land[gi]
    return out, res[-1]


def _split_forward(name, srcs, lands, send_a, recv_a, plan_a, plan_b, n_b, after):
    (group,), token = _split_forward_groups(name, [(srcs, lands, send_a, recv_a, plan_a, plan_b, n_b)], after)
    return (*group, token)


def _gather_plans(shapes):
    n = len(shapes)

    def halves(a, c):
        rows = shapes[a][0] // 2
        return pl.ds(pl.multiple_of(c * rows, 16), rows), pl.ds(pl.multiple_of((1 - c) * rows, 16), rows)

    def split(a):
        return shapes[a][0] % 32 == 0

    def plan_a(x, y, c, src, land):
        j_me = 2 * x + y
        sends = []
        for a in range(n):
            for (px, py) in _chip_relations(x, y):
                if split(a):
                    mine, _ = halves(a, c)
                    sends.append((src[a].at[mine], land[a].at[j_me, mine], (px, py, c), land[a].at[2 * px + py, mine]))
                else:
                    sends.append((src[a], land[a].at[j_me], (px, py, c), land[a].at[2 * px + py]))
        return [], sends

    def plan_b(x, y, c, src, land):
        sends = []
        for a in range(n):
            if split(a):
                mine, other = halves(a, c)
                for (px, py) in _chip_relations(x, y):
                    j = 2 * px + py
                    sends.append((land[a].at[j, mine], land[a].at[j, mine], (x, y, 1 - c), land[a].at[j, other]))
        return [], sends

    n_b = 3 * sum(1 for a in range(n) if split(a))
    return plan_a, plan_b, n_b


def _allgather_plan(n):
    flips = [(dx, dy, dc) for dx in (0, 1) for dy in (0, 1) for dc in (0, 1) if dx or dy or dc]

    def plan(x, y, c, src, land):
        sends = []
        for a in range(n):
            for dx, dy, dc in flips:
                px, py, pc = x ^ dx, y ^ dy, c ^ dc
                sends.append((src[a], land[a].at[4 * x + 2 * y + c], (px, py, pc), land[a].at[4 * px + 2 * py + pc]))
        return [], sends

    return plan


def _scatter_plan(n):
    def plan(x, y, c, src, land):
        sends = []
        for a in range(n):
            for k, (px, py) in enumerate(_chip_relations(x, y)):
                sends.append((src[a].at[2 * px + py], land[a].at[k], (px, py, c), land[a].at[k]))
        return [], sends

    return plan


def _rms(x):
    r = lax.rsqrt(jnp.mean(x * x, axis=-1, keepdims=True) + EPS)
    return x * r, r


def _rms_bwd(dy, n, r, g):
    dg = jnp.sum(dy * n, axis=0, keepdims=True)
    dn = dy * g
    dx = r * (dn - n * jnp.mean(dn * n, axis=-1, keepdims=True))
    return dx, dg


def _ln(x):
    mu = jnp.mean(x, axis=-1, keepdims=True)
    xc = x - mu
    rstd = lax.rsqrt(jnp.mean(xc * xc, axis=-1, keepdims=True) + EPS)
    return xc * rstd, rstd


def _ln_bwd(dy, xhat, rstd, g):
    dg = jnp.sum(dy * xhat, axis=0, keepdims=True)
    db = jnp.sum(dy, axis=0, keepdims=True)
    dxh = dy * g
    dx = rstd * (dxh - jnp.mean(dxh, axis=-1, keepdims=True) - xhat * jnp.mean(dxh * xhat, axis=-1, keepdims=True))
    return dx, dg, db


def _sigmoid(x):
    return jax.nn.sigmoid(x)


def _dsilu(x, s):
    return s * (1.0 + x * (1.0 - s))


def _adam(w, g, m, v):
    m = B1 * m + (1.0 - B1) * g
    v = B2 * v + (1.0 - B2) * (g * g)
    m_hat = m / (1.0 - B1 ** STEP)
    v_hat = v / (1.0 - B2 ** STEP)
    delta = -LR * (m_hat / (jnp.sqrt(v_hat) + EPS_A) + WD * w)
    return delta, m, v


def _head_mask(shape):
    lane = lax.broadcasted_iota(jnp.int32, shape, len(shape) - 1)
    return [(lane >= h * HD) & (lane < (h + 1) * HD) for h in range(NH)]


def _first(b, i):
    return jnp.logical_and(b == 0, i == 0)


def _acc(ref, val, first):
    @pl.when(first)
    def _():
        ref[...] = val

    @pl.when(jnp.logical_not(first))
    def _():
        ref[...] += val


def _ada_fwd(c_all, w_sh, b_sh):
    nb = c_all.shape[0]
    tn = 768

    def body(c_ref, w_ref, b_ref, o_ref):
        cv = c_ref[...]
        cs = (cv * _sigmoid(cv)).astype(BF16)
        o_ref[...] = _dot(cs, w_ref[...].astype(BF16)) + b_ref[...]

    return pl.pallas_call(
        body, name="ada_fwd", grid=(ADA_SH // tn,),
        out_shape=jax.ShapeDtypeStruct((nb, ADA_SH), F32),
        in_specs=[pl.BlockSpec((nb, D), lambda j: (0, 0)), pl.BlockSpec((D, tn), lambda j: (0, j)),
                  pl.BlockSpec((1, tn), lambda j: (0, j))],
        out_specs=pl.BlockSpec((nb, tn), lambda j: (0, j)),
        compiler_params=_cparams(),
    )(c_all, w_sh, b_sh)


def _ada_bwd_adam(c_all, dada_sh, w, m, v):
    nb = c_all.shape[0]
    tn = 768

    def body(c_ref, d_ref, w_ref, m_ref, v_ref, g_out, d_out, m_out, v_out):
        cv = c_ref[...]
        cs = (cv * _sigmoid(cv)).astype(BF16)
        g = _dot_tn(cs, d_ref[...].astype(BF16))
        delta, m2, v2 = _adam(w_ref[...], g, m_ref[...], v_ref[...])
        g_out[...] = g
        d_out[...] = delta
        m_out[...] = m2
        v_out[...] = v2

    big = pl.BlockSpec((D, tn), lambda j: (0, j))
    shape = jax.ShapeDtypeStruct((D, ADA_SH), F32)
    return pl.pallas_call(
        body, name="ada_bwd_adam", grid=(ADA_SH // tn,),
        out_shape=[shape] * 4,
        in_specs=[pl.BlockSpec((nb, D), lambda j: (0, 0)), pl.BlockSpec((nb, tn), lambda j: (0, j)), big, big, big],
        out_specs=[big] * 4,
        compiler_params=_cparams(),
    )(c_all, dada_sh, w, m, v)


def _tok_specs(tm, width):
    return pl.BlockSpec((1, tm, width), lambda b, i: (b, i, 0))


def _mod_spec():
    return pl.BlockSpec((1, 1, D), lambda b, i: (b, 0, 0))


def _row_spec(width=D):
    return pl.BlockSpec((1, width), lambda b, i: (0, 0))


def _ffn_loss_fwd(x, sh, sc, gt, g_pre, g_post, w_in4, w_out, target):
    nb, s, _ = x.shape
    tm = min(TM, s)

    def body(x_ref, sh_ref, sc_ref, gt_ref, gpre_ref, gpost_ref, win_ref, wout_ref, tgt_ref,
             xo_ref, df_ref, p_ref, ls_ref, dgpost_ref, dgt_ref):
        xv = x_ref[0]
        n, _ = _rms(xv)
        h = (n * gpre_ref[...]) * (1.0 + sc_ref[0]) + sh_ref[0]
        hb = h.astype(BF16)
        acc = jnp.zeros((tm, D), F32)
        for j in range(2):
            gate = _dot(hb, win_ref[j])
            up = _dot(hb, win_ref[2 + j])
            p_ref[0, :, j * FBLK:(j + 1) * FBLK] = gate.astype(BF16)
            p_ref[0, :, DFF + j * FBLK:DFF + (j + 1) * FBLK] = up.astype(BF16)
            a = (gate * _sigmoid(gate)) * up
            acc = acc + _dot(a.astype(BF16), wout_ref[j * FBLK:(j + 1) * FBLK, :])
        nf, q = _rms(acc)
        gpost = gpost_ref[...]
        half_gate = 0.5 * gt_ref[0]
        out = xv + half_gate * (nf * gpost)
        first = _first(pl.program_id(0), pl.program_id(1))
        err = out - tgt_ref[0]
        dout = err * (1.0 / D)
        xo_ref[0] = dout
        row = jnp.sum(err * err, axis=0, keepdims=True)
        part = row[:, 0:128]
        for k in range(1, D // 128):
            part = part + row[:, k * 128:(k + 1) * 128]
        _acc(ls_ref, part, first)
        df, dgpost = _rms_bwd(dout * half_gate, nf, q, gpost)
        df_ref[0] = df.astype(BF16)
        _acc(dgpost_ref, dgpost, first)
        _acc(dgt_ref, jnp.sum(dout * (0.5 * (nf * gpost)), axis=0, keepdims=True)[None], pl.program_id(1) == 0)

    tok = _tok_specs(tm, D)
    return pl.pallas_call(
        body, name="ffn_loss_fwd", grid=(nb, s // tm),
        out_shape=[jax.ShapeDtypeStruct((nb, s, D), F32), jax.ShapeDtypeStruct((nb, s, D), BF16),
                   jax.ShapeDtypeStruct((nb, s, 2 * DFF), BF16), jax.ShapeDtypeStruct((1, 128), F32),
                   jax.ShapeDtypeStruct((1, D), F32), jax.ShapeDtypeStruct((nb, 1, D), F32)],
        in_specs=[tok, _mod_spec(), _mod_spec(), _mod_spec(), _row_spec(), _row_spec(), VMEM_FULL, VMEM_FULL, tok],
        out_specs=[tok, tok, _tok_specs(tm, 2 * DFF), pl.BlockSpec((1, 128), lambda b, i: (0, 0)), _row_spec(), _mod_spec()],
        compiler_params=_cparams(),
    )(x, sh, sc, gt, g_pre, g_post, w_in4, w_out, target)


def _ffn_up(x, sh, sc, g_pre, w_in4):
    nb, s, _ = x.shape
    tm = min(TM, s)

    def body(x_ref, sh_ref, sc_ref, gpre_ref, win_ref, p_ref, a_ref):
        n, _ = _rms(x_ref[0])
        hb = ((n * gpre_ref[...]) * (1.0 + sc_ref[0]) + sh_ref[0]).astype(BF16)
        for j in range(2):
            gate = _dot(hb, win_ref[j])
            up = _dot(hb, win_ref[2 + j])
            p_ref[0, :, j * FBLK:(j + 1) * FBLK] = gate.astype(BF16)
            p_ref[0, :, DFF + j * FBLK:DFF + (j + 1) * FBLK] = up.astype(BF16)
            a_ref[0, :, j * FBLK:(j + 1) * FBLK] = ((gate * _sigmoid(gate)) * up).astype(BF16)

    return pl.pallas_call(
        body, name="ffn_up", grid=(nb, s // tm),
        out_shape=[jax.ShapeDtypeStruct((nb, s, 2 * DFF), BF16), jax.ShapeDtypeStruct((nb, s, DFF), BF16)],
        in_specs=[_tok_specs(tm, D), _mod_spec(), _mod_spec(), _row_spec(), VMEM_FULL],
        out_specs=[_tok_specs(tm, 2 * DFF), _tok_specs(tm, DFF)],
        compiler_params=_cparams(),
    )(x, sh, sc, g_pre, w_in4)


def _ffn_down(x, a, gt, g_post, w_out):
    nb, s, _ = x.shape
    tm = min(TM, s)

    def body(x_ref, a_ref, gt_ref, gpost_ref, wout_ref, xo_ref, f_ref):
        acc = _dot(a_ref[0], wout_ref[...])
        f_ref[0] = acc
        nf, _ = _rms(acc)
        xo_ref[0] = x_ref[0] + (0.5 * gt_ref[0]) * (nf * gpost_ref[...])

    tok = _tok_specs(tm, D)
    shape = jax.ShapeDtypeStruct((nb, s, D), F32)
    return pl.pallas_call(
        body, name="ffn_down", grid=(nb, s // tm), out_shape=[shape, shape],
        in_specs=[tok, _tok_specs(tm, DFF), _mod_spec(), _row_spec(), VMEM_FULL],
        out_specs=[tok, tok],
        compiler_params=_cparams(),
    )(x, a, gt, g_post, w_out)


def _ffn_bwd(dxo, x, f, p, sh, sc, gt, g_pre, g_post, w_in4, w_out, df=None):
    nb, s, _ = x.shape
    tm = min(TM_FFN_BWD, s)
    given = df is not None

    def body(*refs):
        if given:
            (dxo_ref, x_ref, dfin_ref, p_ref, sh_ref, sc_ref, gpre_ref, win_ref, wout_ref,
             dx_ref, dp_ref, h_ref, a_ref, dgpre_ref, dsh_ref, dsc_ref) = refs
        else:
            (dxo_ref, x_ref, f_ref, p_ref, sh_ref, sc_ref, gt_ref, gpre_ref, gpost_ref, win_ref, wout_ref,
             dx_ref, dp_ref, h_ref, a_ref, df_ref, dgpre_ref, dgpost_ref, dsh_ref, dsc_ref, dgt_ref) = refs
        b, i = pl.program_id(0), pl.program_id(1)
        dxo_v = dxo_ref[0]
        if given:
            dfb = dfin_ref[0]
        else:
            nf, q = _rms(f_ref[0])
            gpost = gpost_ref[...]
            dgt = jnp.sum(dxo_v * (0.5 * (nf * gpost)), axis=0, keepdims=True)
            do = dxo_v * (0.5 * gt_ref[0])
            dfv, dgpost = _rms_bwd(do, nf, q, gpost)
            dfb = dfv.astype(BF16)
            df_ref[0] = dfb
        xv = x_ref[0]
        n, r = _rms(xv)
        gpre = gpre_ref[...]
        ng = n * gpre
        scale1 = 1.0 + sc_ref[0]
        h = ng * scale1 + sh_ref[0]
        h_ref[0] = h.astype(BF16)
        dh = jnp.zeros((tm, D), F32)
        for j in range(2):
            gate = p_ref[0, :, j * FBLK:(j + 1) * FBLK].astype(F32)
            up = p_ref[0, :, DFF + j * FBLK:DFF + (j + 1) * FBLK].astype(F32)
            sg = _sigmoid(gate)
            act = gate * sg
            a_ref[0, :, j * FBLK:(j + 1) * FBLK] = (act * up).astype(BF16)
            da = _dot_nt(dfb, wout_ref[j * FBLK:(j + 1) * FBLK, :])
            dgate = (da * up * _dsilu(gate, sg)).astype(BF16)
            dup = (da * act).astype(BF16)
            dp_ref[0, :, j * FBLK:(j + 1) * FBLK] = dgate
            dp_ref[0, :, DFF + j * FBLK:DFF + (j + 1) * FBLK] = dup
            dh = dh + _dot_nt(dgate, win_ref[j]) + _dot_nt(dup, win_ref[2 + j])
        dsh = jnp.sum(dh, axis=0, keepdims=True)
        dsc = jnp.sum(dh * ng, axis=0, keepdims=True)
        dxn, dgpre = _rms_bwd(dh * scale1, n, r, gpre)
        dx_ref[0] = dxo_v + dxn
        _acc(dgpre_ref, dgpre, _first(b, i))
        _acc(dsh_ref, dsh[None], i == 0)
        _acc(dsc_ref, dsc[None], i == 0)
        if not given:
            _acc(dgpost_ref, dgpost, _first(b, i))
            _acc(dgt_ref, dgt[None], i == 0)

    tok = _tok_specs(tm, D)
    mod_shape = jax.ShapeDtypeStruct((nb, 1, D), F32)
    row_shape = jax.ShapeDtypeStruct((1, D), F32)
    big = [jax.ShapeDtypeStruct((nb, s, D), F32), jax.ShapeDtypeStruct((nb, s, 2 * DFF), BF16),
           jax.ShapeDtypeStruct((nb, s, D), BF16), jax.ShapeDtypeStruct((nb, s, DFF), BF16)]
    big_specs = [tok, _tok_specs(tm, 2 * DFF), tok, _tok_specs(tm, DFF)]
    if given:
        return pl.pallas_call(
            body, name="ffn_bwd_after_loss", grid=(nb, s // tm),
            out_shape=big + [row_shape, mod_shape, mod_shape],
            in_specs=[tok, tok, tok, _tok_specs(tm, 2 * DFF), _mod_spec(), _mod_spec(), _row_spec(), VMEM_FULL, VMEM_FULL],
            out_specs=big_specs + [_row_spec(), _mod_spec(), _mod_spec()],
            compiler_params=_cparams(),
        )(dxo, x, df, p, sh, sc, g_pre, w_in4, w_out)
    return pl.pallas_call(
        body, name="ffn_bwd", grid=(nb, s // tm),
        out_shape=big + [jax.ShapeDtypeStruct((nb, s, D), BF16), row_shape, row_shape, mod_shape, mod_shape, mod_shape],
        in_specs=[tok, tok, tok, _tok_specs(tm, 2 * DFF), _mod_spec(), _mod_spec(), _mod_spec(), _row_spec(), _row_spec(),
                  VMEM_FULL, VMEM_FULL],
        out_specs=big_specs + [tok, _row_spec(), _row_spec(), _mod_spec(), _mod_spec(), _mod_spec()],
        compiler_params=_cparams(),
    )(dxo, x, f, p, sh, sc, gt, g_pre, g_post, w_in4, w_out)


def _wgrad(name, a, b, col_block, chip_major):
    t, ka = a.shape
    n = b.shape[1]
    def vmem_bytes(rows):
        return 2 * 2 * rows * (ka + col_block) + 4 * ka * col_block + 2 * (4 + 2) * ka * col_block

    tk = min(t, 512)
    while tk * 2 <= t and t % (tk * 2) == 0 and vmem_bytes(tk * 2) <= WGRAD_VMEM_BUDGET:
        tk *= 2
    nk = t // tk
    nblk = n // col_block

    def body(a_ref, b_ref, o_ref, obf_ref, acc_ref):
        k = pl.program_id(1)

        @pl.when(k == 0)
        def _():
            acc_ref[...] = jnp.zeros_like(acc_ref)

        acc_ref[...] += _dot_tn(a_ref[...], b_ref[...])

        @pl.when(k == nk - 1)
        def _():
            val = acc_ref[...]
            if chip_major:
                o_ref[0] = val
                obf_ref[0] = val.astype(BF16)
            else:
                o_ref[...] = val
                obf_ref[...] = val.astype(BF16)

    if chip_major:
        shape = (nblk, ka, col_block)
        ospec = pl.BlockSpec((1, ka, col_block), lambda j, k: (j, 0, 0))
    else:
        shape = (ka, n)
        ospec = pl.BlockSpec((ka, col_block), lambda j, k: (0, j))
    return pl.pallas_call(
        body, name=name, grid=(nblk, nk),
        out_shape=[jax.ShapeDtypeStruct(shape, F32), jax.ShapeDtypeStruct(shape, BF16)],
        in_specs=[pl.BlockSpec((tk, ka), lambda j, k: (k, 0)), pl.BlockSpec((tk, col_block), lambda j, k: (k, j))],
        out_specs=[ospec, ospec],
        scratch_shapes=[pltpu.VMEM((ka, col_block), F32)],
        compiler_params=_cparams(),
    )(a, b)


def _mix_in_fwd(x, sh, sc, g_pre, w_mi4):
    nb, s, _ = x.shape
    tm = min(TM, s)

    def body(x_ref, sh_ref, sc_ref, gpre_ref, w_ref, u_ref, v_ref, a_ref, g_ref):
        n, _ = _rms(x_ref[0])
        hb = ((n * gpre_ref[...]) * (1.0 + sc_ref[0]) + sh_ref[0]).astype(BF16)
        for k, o_ref in enumerate((u_ref, v_ref, a_ref, g_ref)):
            o_ref[0] = _dot(hb, w_ref[k])

    shape = jax.ShapeDtypeStruct((nb, s, WA), F32)
    return pl.pallas_call(
        body, name="mix_in_fwd", grid=(nb, s // tm),
        out_shape=[shape] * 4,
        in_specs=[_tok_specs(tm, D), _mod_spec(), _mod_spec(), _row_spec(), VMEM_FULL],
        out_specs=[_tok_specs(tm, WA)] * 4,
        compiler_params=_cparams(),
    )(x, sh, sc, g_pre, w_mi4)


def _spatial_weights(wcat_ref, transposed):
    w = wcat_ref[...]
    row = lax.broadcasted_iota(jnp.int32, w.shape, 0)
    col = lax.broadcasted_iota(jnp.int32, w.shape, 1)
    keep = ((row & (CH - 1)) <= col) if transposed else ((col & (CH - 1)) <= row)
    return jnp.where(keep, w, 0.0).astype(BF16)


def _expand_heads(vc, masks):
    return jnp.concatenate([jnp.where(mk, vc, jnp.zeros_like(vc)) for mk in masks], axis=0)


def _spatial_bias(bspt_ref):
    return bspt_ref[...]


SHIFTS = 8
TAP_ROWS = 64


def _ext_rows(tm):
    return tm + HALO + SHIFTS


def _make_shifts(ext_ref, sh_ref, tm):
    ext_ref[tm + HALO:tm + HALO + SHIFTS, :] = jnp.zeros((SHIFTS, WB), F32)
    for r in range(SHIFTS):
        sh_ref[r] = ext_ref[r:r + tm + HALO, :]


def _conv_taps(sh_ref, w_ref, tm, taps, emit):
    def block(i, carry):
        r0 = pl.multiple_of(i * TAP_ROWS, TAP_ROWS)
        acc = jnp.zeros((TAP_ROWS, WB), F32)
        for o, k in taps:
            acc = acc + w_ref[k:k + 1, :] * sh_ref[o % SHIFTS, pl.ds(r0 + SHIFTS * (o // SHIFTS), TAP_ROWS), :]
        emit(r0, acc)
        return carry

    lax.fori_loop(0, tm // TAP_ROWS, block, 0)


def _halo_prev_spec(tm):
    return pl.BlockSpec((1, HALO, WB), lambda b, i: (b, jnp.maximum(i * (tm // HALO) - 1, 0), 0))


def _halo_next_spec(tm, s):
    return pl.BlockSpec((1, HALO, WB), lambda b, i: (b, jnp.minimum((i + 1) * (tm // HALO), s // HALO - 1), 0))


def _mix_mid_fwd(x, u, v, a, g, gt, gn_g, gn_b, wcat, bspt, conv_w, conv_b, cn_g, cn_b, go_a, go_b, w_mo, g_post):
    nb, s, _ = x.shape
    tm = min(TM, s)

    def body(x_ref, u_ref, v_ref, a_ref, g_ref, ah_ref, gh_ref, gt_ref, gng_ref, gnb_ref, wcat_ref, bspt_ref,
             cw_ref, cb_ref, cng_ref, cnb_ref, goa_ref, gob_ref, wmo_ref, gpost_ref,
             xo_ref, conv_ref, y_ref, m_ref, ext_ref, sh_ref):
        i = pl.program_id(1)
        xhat, _ = _ln(v_ref[0])
        vb = (xhat * gng_ref[...] + gnb_ref[...]).astype(BF16)
        wsb = _spatial_weights(wcat_ref, False)
        bias = _spatial_bias(bspt_ref)
        masks = _head_mask((CH, WA))
        zs = []
        for cidx in range(tm // CH):
            vexp = _expand_heads(vb[cidx * CH:(cidx + 1) * CH, :], masks)
            zs.append(_dot(wsb, vexp) + bias)
        z = jnp.concatenate(zs, axis=0)
        na, _ = _rms(u_ref[0] * z)
        keep = jnp.where(i == 0, 0.0, 1.0).astype(F32)
        ext_ref[0:HALO, :] = (ah_ref[0] * _sigmoid(gh_ref[0])) * keep
        ext_ref[HALO:HALO + tm, :] = a_ref[0] * _sigmoid(g_ref[0])
        _make_shifts(ext_ref, sh_ref, tm)
        cb = cb_ref[...]

        def put_conv(r0, acc):
            conv_ref[0, pl.ds(r0, TAP_ROWS), :] = acc + cb

        _conv_taps(sh_ref, cw_ref, tm, [(k + HALO - (CK - 1), k) for k in range(CK)], put_conv)
        conv = conv_ref[0]
        chat, _ = _ln(conv)
        cln = chat * cng_ref[...] + cnb_ref[...]
        nbb, _ = _rms(cln * _sigmoid(cln))
        yb = jnp.concatenate([na * goa_ref[...], nbb * gob_ref[...]], axis=1).astype(BF16)
        y_ref[0] = yb
        m = _dot(yb, wmo_ref[...])
        m_ref[0] = m
        nm, _ = _rms(m)
        xo_ref[0] = x_ref[0] + gt_ref[0] * (nm * gpost_ref[...])

    t5 = _tok_specs(tm, WA)
    tok = _tok_specs(tm, D)
    r5 = _row_spec(WA)
    full = lambda shape: pl.BlockSpec(shape, lambda b, i: (0,) * len(shape))
    return pl.pallas_call(
        body, name="mix_mid_fwd", grid=(nb, s // tm),
        out_shape=[jax.ShapeDtypeStruct((nb, s, D), F32), jax.ShapeDtypeStruct((nb, s, WB), F32),
                   jax.ShapeDtypeStruct((nb, s, D), BF16), jax.ShapeDtypeStruct((nb, s, D), F32)],
        in_specs=[tok, t5, t5, t5, t5, _halo_prev_spec(tm), _halo_prev_spec(tm), _mod_spec(), r5, r5,
                  full((CH, NH * CH)), full((CH, WA)), full((HALO, WB)), r5, r5, r5, r5, r5, VMEM_FULL, _row_spec()],
        out_specs=[tok, t5, tok, tok],
        scratch_shapes=[pltpu.VMEM((_ext_rows(tm), WB), F32), pltpu.VMEM((SHIFTS, tm + HALO, WB), F32)],
        compiler_params=_cparams(),
    )(x, u, v, a, g, a, g, gt, gn_g, gn_b, wcat, bspt, conv_w, conv_b, cn_g, cn_b, go_a, go_b, w_mo, g_post)


def _mix_out_bwd(dxo, m, gt, g_post, w_mo):
    nb, s, _ = m.shape
    tm = min(TM, s)

    def body(dxo_ref, m_ref, gt_ref, gpost_ref, wmo_ref, dy_ref, dm_ref, dgpost_ref, dgt_ref):
        b, i = pl.program_id(0), pl.program_id(1)
        dxo_v = dxo_ref[0]
        nm, q = _rms(m_ref[0])
        gpost = gpost_ref[...]
        dgt = jnp.sum(dxo_v * (nm * gpost), axis=0, keepdims=True)
        dm, dgpost = _rms_bwd(dxo_v * gt_ref[0], nm, q, gpost)
        dmb = dm.astype(BF16)
        dm_ref[0] = dmb
        dy_ref[0] = _dot_nt(dmb, wmo_ref[...])
        _acc(dgpost_ref, dgpost, _first(b, i))
        _acc(dgt_ref, dgt[None], i == 0)

    tok = _tok_specs(tm, D)
    return pl.pallas_call(
        body, name="mix_out_bwd", grid=(nb, s // tm),
        out_shape=[jax.ShapeDtypeStruct((nb, s, D), F32), jax.ShapeDtypeStruct((nb, s, D), BF16),
                   jax.ShapeDtypeStruct((1, D), F32), jax.ShapeDtypeStruct((nb, 1, D), F32)],
        in_specs=[tok, tok, _mod_spec(), _row_spec(), VMEM_FULL],
        out_specs=[tok, tok, _row_spec(), _mod_spec()],
        compiler_params=_cparams(),
    )(dxo, m, gt, g_post, w_mo)


def _mix_mid_bwd(dy, u, v, conv, gn_g, gn_b, wcat, wcat_t, bspt, cn_g, cn_b, go_a, go_b):
    nb, s, _ = dy.shape
    tm = min(TM, s)
    nchunk = tm // CH

    def body(dy_ref, u_ref, v_ref, conv_ref, gng_ref, gnb_ref, wcat_ref, wcatt_ref, bspt_ref, cng_ref, cnb_ref,
             goa_ref, gob_ref,
             du_ref, dv_ref, dconv_ref, dwcat_ref, dbsp_ref, dgng_ref, dgnb_ref, dgoa_ref, dgob_ref,
             dcng_ref, dcnb_ref, dcb_ref):
        first = _first(pl.program_id(0), pl.program_id(1))
        dyv = dy_ref[0]
        xhat, rstd = _ln(v_ref[0])
        gng = gng_ref[...]
        vb = (xhat * gng + gnb_ref[...]).astype(BF16)
        wsb = _spatial_weights(wcat_ref, False)
        wsb_t = _spatial_weights(wcatt_ref, True)
        bias = _spatial_bias(bspt_ref)
        masks = _head_mask((CH, WA))
        vexps, zs = [], []
        for cidx in range(nchunk):
            vexp = _expand_heads(vb[cidx * CH:(cidx + 1) * CH, :], masks)
            vexps.append(vexp)
            zs.append(_dot(wsb, vexp) + bias)
        z = jnp.concatenate(zs, axis=0)
        uv = u_ref[0]
        na, ra = _rms(uv * z)
        dya, dgoa = _rms_bwd(dyv[:, 0:WA], na, ra, goa_ref[...])
        du_ref[0] = dya * z
        dz = dya * uv
        dwcat = jnp.zeros((CH, NH * CH), F32)
        dzsum = jnp.zeros((CH, WA), F32)
        dvlns = []
        for cidx in range(nchunk):
            dzc = dz[cidx * CH:(cidx + 1) * CH, :]
            dzsum = dzsum + dzc
            dzb = dzc.astype(BF16)
            dwcat = dwcat + _dot_nt(dzb, vexps[cidx])
            dvexp = _dot(wsb_t, dzb)
            dvl = jnp.zeros((CH, WA), F32)
            for h in range(NH):
                dvl = dvl + jnp.where(masks[h], dvexp[h * CH:(h + 1) * CH, :], 0.0)
            dvlns.append(dvl)
        dvln = jnp.concatenate(dvlns, axis=0)
        dv, dgng, dgnb = _ln_bwd(dvln, xhat, rstd, gng)
        dv_ref[0] = dv
        lane = lax.broadcasted_iota(jnp.int32, (NH, WA), 1)
        head = lax.broadcasted_iota(jnp.int32, (NH, WA), 0)
        sel = jnp.where((lane >= head * HD) & (lane < (head + 1) * HD), 1.0, 0.0).astype(F32)
        dbsp = lax.dot_general(sel, dzsum, NT, preferred_element_type=F32, precision=lax.Precision.HIGHEST)
        chat, crstd = _ln(conv_ref[0])
        cng = cng_ref[...]
        cln = chat * cng + cnb_ref[...]
        sg = _sigmoid(cln)
        nbb, rb = _rms(cln * sg)
        dyb, dgob = _rms_bwd(dyv[:, WA:D], nbb, rb, gob_ref[...])
        dconv, dcng, dcnb = _ln_bwd(dyb * _dsilu(cln, sg), chat, crstd, cng)
        dconv_ref[0] = dconv
        dcb = jnp.sum(dconv, axis=0, keepdims=True)
        for ref, val in ((dwcat_ref, dwcat), (dbsp_ref, dbsp), (dgng_ref, dgng), (dgnb_ref, dgnb), (dgoa_ref, dgoa),
                         (dgob_ref, dgob), (dcng_ref, dcng), (dcnb_ref, dcnb), (dcb_ref, dcb)):
            _acc(ref, val, first)

    t5 = _tok_specs(tm, WA)
    r5 = _row_spec(WA)
    full = lambda shape: pl.BlockSpec(shape, lambda b, i: (0,) * len(shape))
    big = jax.ShapeDtypeStruct((nb, s, WA), F32)
    row = jax.ShapeDtypeStruct((1, WA), F32)
    return pl.pallas_call(
        body, name="mix_mid_bwd", grid=(nb, s // tm),
        out_shape=[big, big, big, jax.ShapeDtypeStruct((CH, NH * CH), F32), jax.ShapeDtypeStruct((NH, CH), F32),
                   row, row, row, row, row, row, row],
        in_specs=[_tok_specs(tm, D), t5, t5, t5, r5, r5, full((CH, NH * CH)), full((NH * CH, CH)), full((CH, WA)),
                  r5, r5, r5, r5],
        out_specs=[t5, t5, t5, full((CH, NH * CH)), full((NH, CH)), r5, r5, r5, r5, r5, r5, r5],
        compiler_params=_cparams(),
    )(dy, u, v, conv, gn_g, gn_b, wcat, wcat_t, bspt, cn_g, cn_b, go_a, go_b)


def _mix_in_bwd(dxo, x, du, dv, dconv, a, g, sh, sc, g_pre, w_mi4, conv_w):
    nb, s, _ = x.shape
    tm = min(TM, s)
    n_i = s // tm

    def body(dxo_ref, x_ref, du_ref, dv_ref, dc_ref, dch_ref, a_ref, g_ref, ah_ref, gh_ref, sh_ref, sc_ref,
             gpre_ref, w_ref, cw_ref,
             dx_ref, dproj_ref, h_ref, dgpre_ref, dsh_ref, dsc_ref, dcw_ref, ext_ref, shf_ref, dglu_ref):
        b, i = pl.program_id(0), pl.program_id(1)
        first = _first(b, i)
        av, gv = a_ref[0], g_ref[0]
        sg = _sigmoid(gv)
        dconv = dc_ref[0]
        ext_ref[0:tm, :] = dconv
        ext_ref[tm:tm + HALO, :] = dch_ref[0] * jnp.where(i == n_i - 1, 0.0, 1.0).astype(F32)
        _make_shifts(ext_ref, shf_ref, tm)

        def put_dglu(r0, acc):
            dglu_ref[pl.ds(r0, TAP_ROWS), :] = acc

        _conv_taps(shf_ref, cw_ref, tm, [(CK - 1 - k, k) for k in range(CK)], put_dglu)
        dglu = dglu_ref[...]
        ext_ref[0:HALO, :] = (ah_ref[0] * _sigmoid(gh_ref[0])) * jnp.where(i == 0, 0.0, 1.0).astype(F32)
        ext_ref[HALO:HALO + tm, :] = av * sg
        _make_shifts(ext_ref, shf_ref, tm)

        @pl.when(first)
        def _():
            dcw_ref[...] = jnp.zeros((HALO, WB), F32)

        for k in range(CK):
            o = k + HALO - (CK - 1)
            lo = SHIFTS * (o // SHIFTS)
            dcw_ref[k:k + 1, :] += jnp.sum(dconv * shf_ref[o % SHIFTS, lo:lo + tm, :], axis=0, keepdims=True)
        da = dglu * sg
        dg = dglu * av * (sg * (1.0 - sg))
        parts = [du_ref[0].astype(BF16), dv_ref[0].astype(BF16), da.astype(BF16), dg.astype(BF16)]
        dh = jnp.zeros((tm, D), F32)
        for k in range(4):
            dproj_ref[0, :, k * WA:(k + 1) * WA] = parts[k]
            dh = dh + _dot_nt(parts[k], w_ref[k])
        n, r = _rms(x_ref[0])
        gpre = gpre_ref[...]
        ng = n * gpre
        scale1 = 1.0 + sc_ref[0]
        h_ref[0] = (ng * scale1 + sh_ref[0]).astype(BF16)
        dsh = jnp.sum(dh, axis=0, keepdims=True)
        dsc = jnp.sum(dh * ng, axis=0, keepdims=True)
        dxn, dgpre = _rms_bwd(dh * scale1, n, r, gpre)
        dx_ref[0] = dxo_ref[0] + dxn
        _acc(dgpre_ref, dgpre, first)
        _acc(dsh_ref, dsh[None], i == 0)
        _acc(dsc_ref, dsc[None], i == 0)

    tok = _tok_specs(tm, D)
    t5 = _tok_specs(tm, WA)
    full = lambda shape: pl.BlockSpec(shape, lambda b, i: (0,) * len(shape))
    mod_shape = jax.ShapeDtypeStruct((nb, 1, D), F32)
    return pl.pallas_call(
        body, name="mix_in_bwd", grid=(nb, n_i),
        out_shape=[jax.ShapeDtypeStruct((nb, s, D), F32), jax.ShapeDtypeStruct((nb, s, 4 * WA), BF16),
                   jax.ShapeDtypeStruct((nb, s, D), BF16), jax.ShapeDtypeStruct((1, D), F32), mod_shape, mod_shape,
                   jax.ShapeDtypeStruct((HALO, WB), F32)],
        in_specs=[tok, tok, t5, t5, t5, _halo_next_spec(tm, s), t5, t5, _halo_prev_spec(tm), _halo_prev_spec(tm),
                  _mod_spec(), _mod_spec(), _row_spec(), VMEM_FULL, full((HALO, WB))],
        out_specs=[tok, _tok_specs(tm, 4 * WA), tok, _row_spec(), _mod_spec(), _mod_spec(), full((HALO, WB))],
        scratch_shapes=[pltpu.VMEM((_ext_rows(tm), WB), F32), pltpu.VMEM((SHIFTS, tm + HALO, WB), F32),
                        pltpu.VMEM((tm, WB), F32)],
        compiler_params=_cparams(),
    )(dxo, x, du, dv, dconv, dconv, a, g, a, g, sh, sc, g_pre, w_mi4, conv_w)


def _row_tile(rows, cols):
    best = 16
    for t in range(16, rows + 1, 16):
        if rows % t == 0 and t * cols * 4 <= 1536 * 1024:
            best = t
    return best


def _walk(steps):
    offs = [sum(steps[:k]) for k in range(len(steps))]

    def tile(k):
        return lambda i: jnp.clip(i - offs[k], 0, steps[k] - 1)

    def mine(k, i):
        return jnp.logical_and(i >= offs[k], i < offs[k] + steps[k])

    return sum(steps), tile, mine


def _sum4(name, own4s, recvs, j_arr):
    n = len(own4s)
    shapes = [o.shape[1:] for o in own4s]
    trs = [_row_tile(r, c) for r, c in shapes]
    total, tile, mine = _walk([r // tr for (r, _), tr in zip(shapes, trs)])

    def body(j_ref, *refs):
        del j_ref
        i = pl.program_id(0)
        for k in range(n):
            own_ref, recv_ref, o_ref = refs[2 * k], refs[2 * k + 1], refs[2 * n + k]

            def add(own_ref=own_ref, recv_ref=recv_ref, o_ref=o_ref):
                acc = own_ref[0]
                for q in range(3):
                    acc = acc + recv_ref[q].astype(F32)
                o_ref[...] = acc

            pl.when(mine(k, i))(add)

    in_specs, out_specs = [], []
    for k, ((_, cols), tr) in enumerate(zip(shapes, trs)):
        in_specs += [pl.BlockSpec((1, tr, cols), lambda i, j, t=tile(k): (j[0], t(i), 0)),
                     pl.BlockSpec((3, tr, cols), lambda i, j, t=tile(k): (0, t(i), 0))]
        out_specs.append(pl.BlockSpec((tr, cols), lambda i, j, t=tile(k): (t(i), 0)))
    return pl.pallas_call(
        body, name=name,
        grid_spec=pltpu.PrefetchScalarGridSpec(num_scalar_prefetch=1, grid=(total,), in_specs=in_specs, out_specs=out_specs),
        out_shape=[jax.ShapeDtypeStruct(sh, F32) for sh in shapes],
        compiler_params=_cparams(),
    )(j_arr, *[a for pair in zip(own4s, recvs) for a in pair])


def _pair_plan(shapes):
    def plan(x, y, c, src, land):
        sends = []
        for a, shape in enumerate(shapes):
            rows = shape[1] // 2
            theirs = pl.ds(pl.multiple_of((1 - c) * rows, 16), rows)
            sends.append((src[a].at[:, theirs], land[a], (x, y, 1 - c), land[a]))
        return [], sends

    return plan


def _swap_plan(n):
    def plan(x, y, c, src, land):
        return [], [(src[a], land[a], (x, y, 1 - c), land[a]) for a in range(n)]

    return plan


def _pair_sum(name, g32s, recvs, c_arr):
    n = len(g32s)
    shapes = [r.shape for r in recvs]
    trs = [_row_tile(rows, cols) for _, rows, cols in shapes]
    nhs = [rows // tr for (_, rows, _), tr in zip(shapes, trs)]
    total, tile, mine = _walk([nblk * nh for (nblk, _, _), nh in zip(shapes, nhs)])

    def body(c_ref, *refs):
        del c_ref
        i = pl.program_id(0)
        for k in range(n):
            g_ref, r_ref, o32_ref, obf_ref = refs[2 * k], refs[2 * k + 1], refs[2 * n + 2 * k], refs[2 * n + 2 * k + 1]

            def add(g_ref=g_ref, r_ref=r_ref, o32_ref=o32_ref, obf_ref=obf_ref):
                val = g_ref[0] + r_ref[0].astype(F32)
                o32_ref[0] = val
                obf_ref[0] = val.astype(BF16)

            pl.when(mine(k, i))(add)

    in_specs, out_specs, out_shape = [], [], []
    for k, ((_, _, cols), tr, nh) in enumerate(zip(shapes, trs, nhs)):
        def half(tr=tr, cols=cols, t=tile(k), nh=nh):
            return pl.BlockSpec((1, tr, cols), lambda i, c: (t(i) // nh, t(i) % nh, 0))

        in_specs += [pl.BlockSpec((1, tr, cols), lambda i, c, t=tile(k), nh=nh: (t(i) // nh, c[0] * nh + t(i) % nh, 0)), half()]
        out_specs += [half(), half()]
        out_shape += [jax.ShapeDtypeStruct(shapes[k], F32), jax.ShapeDtypeStruct(shapes[k], BF16)]
    res = pl.pallas_call(
        body, name=name,
        grid_spec=pltpu.PrefetchScalarGridSpec(num_scalar_prefetch=1, grid=(total,), in_specs=in_specs, out_specs=out_specs),
        out_shape=out_shape,
        compiler_params=_cparams(),
    )(c_arr, *[a for pair in zip(g32s, recvs) for a in pair])
    return [(res[2 * k], res[2 * k + 1]) for k in range(n)]


def _adam_halves(name, w, m, v, mine, theirs, c_arr):
    rows, cols = w.shape
    tr = _row_tile(rows // 2, cols)
    nh = (rows // 2) // tr

    def body(c_ref, w_ref, m_ref, v_ref, mine_ref, theirs_ref, g_out, d_out, m_out, v_out):
        here = (pl.program_id(0) // nh) == c_ref[0]
        g = jnp.where(here, mine_ref[...], theirs_ref[...])
        delta, m2, v2 = _adam(w_ref[...], g, m_ref[...], v_ref[...])
        g_out[...] = g
        d_out[...] = delta
        m_out[...] = m2
        v_out[...] = v2

    spec = pl.BlockSpec((tr, cols), lambda i, c: (i, 0))
    shape = jax.ShapeDtypeStruct((rows, cols), F32)
    return pl.pallas_call(
        body, name=name,
        grid_spec=pltpu.PrefetchScalarGridSpec(
            num_scalar_prefetch=1, grid=(2 * nh,),
            in_specs=[spec, spec, spec,
                      pl.BlockSpec((tr, cols), lambda i, c: (jnp.clip(i - c[0] * nh, 0, nh - 1), 0)),
                      pl.BlockSpec((tr, cols), lambda i, c: (jnp.clip(i - (1 - c[0]) * nh, 0, nh - 1), 0))],
            out_specs=[spec] * 4),
        out_shape=[shape] * 4,
        compiler_params=_cparams(),
    )(c_arr, w, m, v, mine, theirs)


def _adam_big(name, w, m, v, ga, gb):
    rows, cols = w.shape
    tr = _row_tile(rows, cols)

    def body(w_ref, m_ref, v_ref, ga_ref, gb_ref, g_out, d_out, m_out, v_out):
        gsum = ga_ref[...] + gb_ref[...]
        delta, m2, v2 = _adam(w_ref[...], gsum, m_ref[...], v_ref[...])
        g_out[...] = gsum
        d_out[...] = delta
        m_out[...] = m2
        v_out[...] = v2

    spec = pl.BlockSpec((tr, cols), lambda i: (i, 0))
    shape = jax.ShapeDtypeStruct((rows, cols), F32)
    return pl.pallas_call(
        body, name=name, grid=(rows // tr,), out_shape=[shape] * 4,
        in_specs=[spec] * 5, out_specs=[spec] * 4, compiler_params=_cparams(),
    )(w, m, v, ga, gb)


PK_VEC = 0
PK_LOSS = 6
PK_PAIR = 8
PK_BSP = 16
PK_WCAT = 24
PK_ROWS = PK_WCAT + CH
PAIR_ORDER = ("gmlp_norm_g", "gmlp_norm_b", "conv_b", "conv_norm_g", "conv_norm_b", "g_out_a", "g_out_b")
VEC_ORDER = ("g_pre_f1", "g_post_f1", "g_pre_m", "g_post_m", "g_pre_f2", "g_post_f2")


def _pack_late(rows):
    counts = [r.shape[0] for r in rows]
    assert sum(counts) == 8

    def body(*refs):
        o_ref = refs[-1]
        at = 0
        for r, cnt in zip(refs[:-1], counts):
            o_ref[at:at + cnt, :] = r[...]
            at += cnt

    return pl.pallas_call(
        body, name="pack_late", out_shape=jax.ShapeDtypeStruct((8, D), F32),
        in_specs=[VMEM_FULL] * len(rows), out_specs=VMEM_FULL, compiler_params=_cparams(),
    )(*rows)


def _pack_small(vecs, pairs, dbsp, dwcat, lsum):
    def body(*refs):
        vec_refs = refs[:4]
        pair_refs = refs[4:11]
        dbsp_ref, dwcat_ref, lsum_ref, o_ref = refs[11:]
        o_ref[0:PK_WCAT, :] = jnp.zeros((PK_WCAT, D), F32)
        o_ref[PK_LOSS:PK_LOSS + 1, 0:128] = lsum_ref[...]
        for k, r in enumerate(vec_refs):
            o_ref[PK_VEC + 2 + k:PK_VEC + 3 + k, :] = r[...]
        for k, r in enumerate(pair_refs):
            row, half = PK_PAIR + k // 2, k % 2
            o_ref[row:row + 1, half * WA:(half + 1) * WA] = r[...]
        o_ref[PK_BSP:PK_BSP + NH, 0:CH] = dbsp_ref[...]
        o_ref[PK_WCAT:PK_ROWS, :] = dwcat_ref[...]

    args = list(vecs) + list(pairs) + [dbsp, dwcat, lsum]
    return pl.pallas_call(
        body, name="pack_small", out_shape=jax.ShapeDtypeStruct((PK_ROWS, D), F32),
        in_specs=[VMEM_FULL] * len(args), out_specs=VMEM_FULL, compiler_params=_cparams(),
    )(*args)


def _small_adam(pack_all, late_all, dcw_all, dada_all, params, behind):
    names = list(VEC_ORDER) + list(PAIR_ORDER) + ["b_spatial", "w_spatial", "conv_w", "b_ada"]
    flat = []
    for nm in names:
        flat += list(params[nm])
    n_in = 4 + len(flat)

    def body(*refs):
        pack_ref, late_ref, dcw_ref, dada_ref = refs[:4]
        prm = refs[4:n_in]
        outs = refs[n_in + 1:]

        def total(r0, nr, c0, nc):
            acc = pack_ref[0, r0:r0 + nr, c0:c0 + nc]
            for d in range(1, NDEV):
                acc = acc + pack_ref[d, r0:r0 + nr, c0:c0 + nc]
            return acc

        def emit(idx, g, getw, put):
            w_ref, m_ref, v_ref = prm[3 * idx:3 * idx + 3]
            delta, m2, v2 = _adam(getw(w_ref), g, getw(m_ref), getw(v_ref))
            for o_ref, val in zip(outs[4 * idx:4 * idx + 4], (g, delta, m2, v2)):
                put(o_ref, val)

        def whole(ref):
            return ref[...]

        def put_whole(ref, val):
            ref[...] = val

        idx = 0
        for k in range(6):
            if k < 2:
                g = late_ref[0, k:k + 1, :]
                for d in range(1, NDEV):
                    g = g + late_ref[d, k:k + 1, :]
            else:
                g = total(PK_VEC + k, 1, 0, D)
            emit(idx, g, whole, put_whole)
            idx += 1
        for k in range(7):
            emit(idx, total(PK_PAIR + k // 2, 1, (k % 2) * WA, WA), whole, put_whole)
            idx += 1
        emit(idx, total(PK_BSP, NH, 0, CH), lambda r: r[0], lambda r, val: r.__setitem__(0, val))
        idx += 1
        row = lax.broadcasted_iota(jnp.int32, (CH, CH), 0)
        col = lax.broadcasted_iota(jnp.int32, (CH, CH), 1)
        for h in range(NH):
            gh = jnp.where(col <= row, total(PK_WCAT, CH, h * CH, CH), 0.0)
            w_ref, m_ref, v_ref = prm[3 * idx:3 * idx + 3]
            delta, m2, v2 = _adam(w_ref[0, h], gh, m_ref[0, h], v_ref[0, h])
            for o_ref, val in zip(outs[4 * idx:4 * idx + 4], (gh, delta, m2, v2)):
                o_ref[0, h] = val
        idx += 1
        gcw = dcw_ref[0, 0:CK, :]
        for d in range(1, NDEV):
            gcw = gcw + dcw_ref[d, 0:CK, :]
        emit(idx, gcw, lambda r: r[0], lambda r, val: r.__setitem__(0, val))
        idx += 1
        emit(idx, jnp.sum(dada_ref[...], axis=0, keepdims=True), whole, put_whole)
        outs[-1][...] = jnp.sum(total(PK_LOSS, 1, 0, 128), axis=1, keepdims=True) * (0.5 / D)

    out_shape = []
    for nm in names:
        w = params[nm][0]
        out_shape += [jax.ShapeDtypeStruct(w.shape, F32)] * 4
    out_shape.append(jax.ShapeDtypeStruct((1, 1), F32))
    res = pl.pallas_call(
        body, name="small_adam", out_shape=out_shape,
        in_specs=[VMEM_FULL] * n_in + [ANY], out_specs=[VMEM_FULL] * len(out_shape), compiler_params=_cparams(),
    )(pack_all, late_all, dcw_all, dada_all, *flat, behind)
    return {nm: tuple(res[4 * k:4 * k + 4]) for k, nm in enumerate(names)}, res[-1].reshape(())


WEIGHTS = ['w_ada', 'b_ada', 'g_pre_f1', 'g_post_f1', 'w_f1_in', 'w_f1_out', 'g_pre_m', 'g_post_m', 'w_mix_in',
           'gmlp_norm_g', 'gmlp_norm_b', 'w_spatial', 'b_spatial', 'conv_w', 'conv_b', 'conv_norm_g', 'conv_norm_b',
           'g_out_a', 'g_out_b', 'w_mix_out', 'g_pre_f2', 'g_post_f2', 'w_f2_in', 'w_f2_out']


def kernel(x, c, w_ada, b_ada, g_pre_f1, g_post_f1, w_f1_in, w_f1_out, g_pre_m, g_post_m, w_mix_in, gmlp_norm_g, gmlp_norm_b, w_spatial, b_spatial, conv_w, conv_b, conv_norm_g, conv_norm_b, g_out_a, g_out_b, w_mix_out, g_pre_f2, g_post_f2, w_f2_in, w_f2_out, loss_target, m_w_ada, m_b_ada, m_g_pre_f1, m_g_post_f1, m_w_f1_in, m_w_f1_out, m_g_pre_m, m_g_post_m, m_w_mix_in, m_gmlp_norm_g, m_gmlp_norm_b, m_w_spatial, m_b_spatial, m_conv_w, m_conv_b, m_conv_norm_g, m_conv_norm_b, m_g_out_a, m_g_out_b, m_w_mix_out, m_g_pre_f2, m_g_post_f2, m_w_f2_in, m_w_f2_out, v_w_ada, v_b_ada, v_g_pre_f1, v_g_post_f1, v_w_f1_in, v_w_f1_out, v_g_pre_m, v_g_post_m, v_w_mix_in, v_gmlp_norm_g, v_gmlp_norm_b, v_w_spatial, v_b_spatial, v_conv_w, v_conv_b, v_conv_norm_g, v_conv_norm_b, v_g_out_a, v_g_out_b, v_w_mix_out, v_g_pre_f2, v_g_post_f2, v_w_f2_in, v_w_f2_out):
    env = dict(locals())
    wts = {n: env[n] for n in WEIGHTS}
    mom = {n: env["m_" + n] for n in WEIGHTS}
    var = {n: env["v_" + n] for n in WEIGHTS}
    nb, s, _ = x.shape
    t = nb * s
    ax, ay, ac = lax.axis_index("x"), lax.axis_index("y"), lax.axis_index("c")
    j_chip = 2 * ax + ay
    dev = 4 * ax + 2 * ay + ac
    j_arr = j_chip.reshape(1).astype(jnp.int32)

    groups = (("w_f1_in",), ("w_mix_in", "w_mix_out"), ("w_f2_in", "w_f2_out"), ("w_f1_out",))
    def gather_operands(gi):
        srcs = [wts[n][0].astype(BF16) for n in groups[gi]] + ([conv_w[0]] if gi == 1 else [])
        lands = [lax.dynamic_update_index_in_dim(lax.empty((NCHIP,) + a.shape, a.dtype), a, j_chip, 0) for a in srcs]
        return srcs, lands

    def gather_start(gi, behind, operands=None):
        srcs, lands = operands or gather_operands(gi)
        plan_a, plan_b, n_b = _gather_plans([a.shape for a in srcs])
        ssem, rsem, srcs, lands, token = _split_start("gw_start%d" % gi, srcs, lands, plan_a, 3 * len(srcs), behind)
        gather[gi] = (srcs, lands, ssem, rsem, plan_a, plan_b, n_b)
        return token

    def gather_forward(gi, behind):
        srcs, lands, ssem, rsem, plan_a, plan_b, n_b = gather[gi]
        ssem, rsem, lands, token = _split_forward("gw_fwd%d" % gi, srcs, lands, ssem, rsem, plan_a, plan_b, n_b, behind)
        gather[gi] = (lands, ssem, rsem, plan_b)
        return token

    def gathered(gi, behind):
        lands, ssem, rsem, plan_b = gather[gi]
        return _split_wait("gw_wait%d" % gi, [], lands, ssem, rsem, plan_b, behind)

    gather = {}
    def allgather_start(tag, arrs, behind):
        lands = [lax.dynamic_update_index_in_dim(lax.empty((NDEV,) + a.shape, a.dtype), a, dev, 0) for a in arrs]
        ssem, rsem, srcs, lands, token = _split_start("small_start_" + tag, arrs, lands, _allgather_plan(len(arrs)),
                                                      7 * len(arrs), behind)
        return (srcs, lands, ssem, rsem), token

    def allgather_wait(tag, state, behind):
        srcs, lands, ssem, rsem = state
        return _split_wait("small_wait_" + tag, srcs, lands, ssem, rsem, _allgather_plan(len(srcs)), behind)

    c_state, token = allgather_start("c", [c.reshape(8, (nb * D) // 8)], c)
    token = gather_start(0, token)
    (c_all8,) = allgather_wait("c", c_state, token)
    c_all = c_all8.reshape(NDEV * nb, D)
    b_sh = lax.dynamic_slice(b_ada, (0, j_chip * ADA_SH), (1, ADA_SH))
    ada_sh = _ada_fwd(c_all, w_ada[0], b_sh)
    later = [gather_operands(3), gather_operands(1), gather_operands(2)]
    (ada4,) = _chip_allgather("gather_ada", [ada_sh], behind=[a for pair in later for arrs in pair for a in arrs])
    token = gather_forward(0, ada4)
    plans = [_gather_plans([a.shape for a in srcs]) for srcs, _ in later]
    started, token = _split_start_groups(
        "gw_start_later", [(srcs, lands, pa, 3 * len(srcs)) for (srcs, lands), (pa, _, _) in zip(later, plans)], token)
    for gi, (ssem, rsem, srcs, lands), (pa, pb, n_b) in zip((3, 1, 2), started, plans):
        gather[gi] = (srcs, lands, ssem, rsem, pa, pb, n_b)
    ada_me = lax.dynamic_slice(ada4, (0, dev * nb, 0), (NCHIP, nb, ADA_SH))
    ada_me = jnp.transpose(ada_me, (1, 0, 2)).reshape(nb, NMOD * D)
    sh1, sc1, gt1, sh2, sc2, gt2, sh3, sc3, gt3 = [ada_me[:, k * D:(k + 1) * D].reshape(nb, 1, D) for k in range(NMOD)]

    wcat = jnp.transpose(w_spatial[0], (1, 0, 2)).reshape(CH, NH * CH)
    wcat_t = jnp.transpose(w_spatial[0], (0, 2, 1)).reshape(NH * CH, CH)
    bspt = jnp.repeat(b_spatial[0].T, HD, axis=1)

    (w1i,) = gathered(0, token)
    p1, act1 = _ffn_up(x, sh1, sc1, g_pre_f1, w1i)
    forwarded, token = _split_forward_groups("gw_fwd_f1_out_mix", [gather[3], gather[1]], act1)
    for gi, (ssem, rsem, lands) in zip((3, 1), forwarded):
        gather[gi] = (lands, ssem, rsem, gather[gi][5])
    (w1o,) = gathered(3, token)
    w1o = w1o.reshape(DFF, D)
    x1, f1 = _ffn_down(x, act1, gt1, g_post_f1, w1o)
    wmi, wmo, cw4 = gathered(1, x1)
    wmo = wmo.reshape(D, D)
    cw_full = jnp.transpose(cw4, (1, 0, 2)).reshape(CK, WB)
    cw_pad = jnp.pad(cw_full, ((0, HALO - CK), (0, 0)))
    u, v, a, g = _mix_in_fwd(x1, sh2, sc2, g_pre_m, wmi)
    token = gather_forward(2, u)
    x2, conv, yb, m = _mix_mid_fwd(x1, u, v, a, g, gt2 + token[0, 0], gmlp_norm_g, gmlp_norm_b, wcat, bspt, cw_pad, conv_b,
                                   conv_norm_g, conv_norm_b, g_out_a, g_out_b, wmo, g_post_m)
    w2i, w2o = gathered(2, [x2, token])
    w2o = w2o.reshape(DFF, D)
    dx3, df2, p2, lsum, dg_post_f2, dgt3 = _ffn_loss_fwd(x2, sh3, sc3, gt3, g_pre_f2, g_post_f2, w2i, w2o, loss_target)

    def chip4(pair, rows):
        return [arr.reshape(NCHIP, rows, arr.shape[-1]) for arr in pair]

    def scatter_start(tag, pairs, behind):
        srcs = [p[1] for p in pairs]
        lands = [lax.empty((3,) + a.shape[1:], a.dtype) for a in srcs]
        ssem, rsem, srcs, lands, token = _split_start("gs_start_" + tag, srcs, lands, _scatter_plan(len(srcs)),
                                                      3 * len(srcs), behind)
        return (srcs, lands, ssem, rsem), token

    def scatter_wait(tag, state, behind):
        srcs, lands, ssem, rsem = state
        return _split_wait("gs_wait_" + tag, srcs, lands, ssem, rsem, _scatter_plan(len(srcs)), behind)

    out = {}
    dx2, dp2, h3, a2, dg_pre_f2, dsh3, dsc3 = _ffn_bwd(
        dx3, x2, None, p2, sh3, sc3, gt3, g_pre_f2, g_post_f2, w2i, w2o, df=df2)
    gw2i = _wgrad("wgrad_f2_in", h3.reshape(t, D), dp2.reshape(t, 2 * DFF), 2 * DFF // NCHIP, True)
    gw2o = chip4(_wgrad("wgrad_f2_out", a2.reshape(t, DFF), df2.reshape(t, D), D // 2, False), DFF // NCHIP)
    scat_f2, tok = scatter_start("f2", [gw2i, gw2o], dg_post_f2)
    dy, dm, dg_post_m, dgt2 = _mix_out_bwd(dx2, m, gt2 + tok[0, 0], g_post_m, wmo)
    gwmo = chip4(_wgrad("wgrad_mix_out", yb.reshape(t, D), dm.reshape(t, D), D // 2, False), D // NCHIP)
    (du, dv, dconv, dwcat, dbsp, dgn_g, dgn_b, dgo_a, dgo_b, dcn_g, dcn_b, dcb) = _mix_mid_bwd(
        dy, u, v, conv, gmlp_norm_g, gmlp_norm_b, wcat, wcat_t, bspt, conv_norm_g, conv_norm_b, g_out_a, g_out_b)
    dx1, dproj, h2, dg_pre_m, dsh2, dsc2, dcw = _mix_in_bwd(dx2, x1, du, dv, dconv, a, g, sh2, sc2, g_pre_m, wmi, cw_pad)
    gwmi = _wgrad("wgrad_mix_in", h2.reshape(t, D), dproj.reshape(t, 4 * WA), WA, True)

    vec_grads = dict(g_pre_m=dg_pre_m, g_post_m=dg_post_m, g_pre_f2=dg_pre_f2, g_post_f2=dg_post_f2)
    pair_grads = dict(gmlp_norm_g=dgn_g, gmlp_norm_b=dgn_b, conv_b=dcb, conv_norm_g=dcn_g, conv_norm_b=dcn_b,
                      g_out_a=dgo_a, g_out_b=dgo_b)
    pack = _pack_small([vec_grads[n] for n in VEC_ORDER[2:]], [pair_grads[n] for n in PAIR_ORDER], dbsp, dwcat, lsum)
    dada_early = jnp.concatenate([q.reshape(nb, D) for q in (dsh2, dsc2, dgt2, dsh3, dsc3, dgt3)], axis=1)
    small_early = [pack, dcw, dada_early.reshape(8, (nb * 6 * D) // 8)]
    mix_bf16 = [gwmi[1], gwmo[1]]
    (s_mix, s_early), tok2 = _split_start_groups("gs_start_mix_small", [
        (mix_bf16, [lax.empty((3,) + a.shape[1:], a.dtype) for a in mix_bf16], _scatter_plan(2), 6),
        (small_early, [lax.dynamic_update_index_in_dim(lax.empty((NDEV,) + a.shape, a.dtype), a, dev, 0) for a in small_early],
         _allgather_plan(3), 21)], dg_pre_m)
    scat_mix = (s_mix[2], s_mix[3], s_mix[0], s_mix[1])
    early = (s_early[2], s_early[3], s_early[0], s_early[1])
    grad_x, dp1, h1, a1, df1, dg_pre_f1, dg_post_f1, dsh1, dsc1, dgt1 = _ffn_bwd(
        dx1, x, f1, p1, sh1 + tok2[0, 0], sc1, gt1, g_pre_f1, g_post_f1, w1i, w1o)
    late_pack = _pack_late([dg_pre_f1, dg_post_f1] + [q.reshape(nb, D) for q in (dsh1, dsc1, dgt1)])
    late, tok2 = allgather_start("late", [late_pack], dg_post_f1)
    gw1o = chip4(_wgrad("wgrad_f1_out", a1.reshape(t, DFF), df1.reshape(t, D), D // 2, False), DFF // NCHIP)
    scat_f1o, tok2 = scatter_start("f1o", [gw1o], tok2)
    gw1i = _wgrad("wgrad_f1_in", h1.reshape(t, D), dp1.reshape(t, 2 * DFF), 2 * DFF // NCHIP, True)
    def d2d_start(tag, srcs, lands, plan, behind):
        ssem, rsem, srcs, lands, token = _split_start("d2d_start_" + tag, srcs, lands, plan, len(srcs), behind)
        return (srcs, lands, ssem, rsem, plan), token

    def d2d_wait(tag, state, behind):
        srcs, lands, ssem, rsem, plan = state
        return _split_wait("d2d_wait_" + tag, srcs, lands, ssem, rsem, plan, behind)

    def swap_start(tag, parts, behind):
        return d2d_start(tag, parts, [lax.empty(a.shape, a.dtype) for a in parts], _swap_plan(len(parts)), behind)

    def sums(names, pairs, recv):
        return _sum4("sum4_" + names[0][2:4], [p[0] for p in pairs], recv, j_arr)

    def update(names, part, other):
        for k, n in enumerate(names):
            out[n] = tuple(r[None] for r in _adam_big("adam_" + n, wts[n][0], mom[n][0], var[n][0], part[k], other[k]))

    c_arr = ac.reshape(1).astype(jnp.int32)
    halves = [gw1i[1]]
    pair_st, tok = d2d_start("pair", halves, [lax.empty((a.shape[0], a.shape[1] // 2, a.shape[2]), a.dtype) for a in halves],
                             _pair_plan([a.shape for a in halves]), tok2)
    names_f2, names_mix, names_f1 = ("w_f2_in", "w_f2_out"), ("w_mix_in", "w_mix_out", "w_f1_out"), ("w_f1_in",)
    part_f2 = sums(names_f2, [gw2i, gw2o], scatter_wait("f2", scat_f2, tok))
    sib = d2d_wait("pair", pair_st, part_f2)
    (pair_i,) = _pair_sum("pairsum_f1", [gw1i[0]], sib, c_arr)
    f1_bf16 = [pair_i[1]]
    (s_f1, s_sw), tok = _split_start_groups("gs_start_f1_swap_f2", [
        (f1_bf16, [lax.empty((3,) + a.shape[1:], a.dtype) for a in f1_bf16], _scatter_plan(1), 3),
        (part_f2, [lax.empty(a.shape, a.dtype) for a in part_f2], _swap_plan(2), 2)], tok2)
    scat_f1 = (s_f1[2], s_f1[3], s_f1[0], s_f1[1])
    swap_f2 = (s_sw[2], s_sw[3], s_sw[0], s_sw[1], _swap_plan(2))
    part_mix = sums(names_mix, [gwmi, gwmo, gw1o],
                    list(scatter_wait("mix", scat_mix, tok)) + list(scatter_wait("f1o", scat_f1o, tok)))
    swap_mix, tok = swap_start("swap_mix", part_mix, part_mix[1])

    pack_all, dcw_all, dada_early8 = allgather_wait("early", early, tok)
    (late_all,) = allgather_wait("late", late, pack_all)
    dada_late = jnp.transpose(late_all[:, 2:8, :].reshape(NDEV, 3, nb, D), (0, 2, 1, 3)).reshape(NDEV * nb, 3 * D)
    dada_all = jnp.concatenate([dada_late, dada_early8.reshape(NDEV * nb, 6 * D)], axis=1)
    dada_sh = lax.dynamic_slice(dada_all, (0, j_chip * ADA_SH), (NDEV * nb, ADA_SH))
    out["w_ada"] = tuple(r[None] for r in _ada_bwd_adam(c_all, dada_sh, w_ada[0], m_w_ada[0], v_w_ada[0]))
    update(names_f2, part_f2, d2d_wait("swap_f2", swap_f2, out["w_ada"][3]))
    update(names_mix, part_mix, d2d_wait("swap_mix", swap_mix, out["w_f2_out"][3]))

    mine = sums(names_f1, [pair_i], scatter_wait("f1", scat_f1, out["w_f1_out"][3]))
    swap_f1, tok = swap_start("swap_f1", mine, mine[0])
    dcw_mine = lax.dynamic_slice(dcw_all, (0, 0, j_chip * (WB // NCHIP)), (NDEV, HALO, WB // NCHIP))
    small = {n: (wts[n], mom[n], var[n]) for n in list(VEC_ORDER) + list(PAIR_ORDER) + ["b_spatial", "w_spatial", "conv_w", "b_ada"]}
    small_out, loss = _small_adam(pack_all, late_all, dcw_mine, dada_all, small, tok)
    out.update(small_out)
    theirs = d2d_wait("swap_f1", swap_f1, out["b_ada"][3])
    for k, n in enumerate(names_f1):
        out[n] = tuple(r[None] for r in _adam_halves("adam_" + n, wts[n][0], mom[n][0], var[n][0], mine[k], theirs[k],
                                                     c_arr))

    res = [loss, grad_x]
    for k in range(4):
        res += [out[n][k] for n in WEIGHTS]
    return tuple(res)
```
